```python
import math
import jax, jax.numpy as jnp
from jax import lax
import numpy as np

D_MODEL = 1024
BATCH = 8
SEQ = 4096
DEPTH = 2

CHUNK = 64
D_FF = 2816
MACARON_WEIGHT = 0.5
RMS_EPS = 1e-6
A_WIDTH = D_MODEL // 2
A_GROUPS = 8
CONV_WIDTH = 3
SB_HEADS = 8
SB_HEAD_DIM = (D_MODEL // 2) // SB_HEADS
B_WIDTH = SB_HEADS * SB_HEAD_DIM
SB_BLOCK = 128
AB_IN_COLS = 3 * A_WIDTH + 3 * B_WIDTH
C_WIDTH = D_MODEL
C_HEAD_DIM = 128
C_HEADS = C_WIDTH // C_HEAD_DIM
C_IN_COLS = 4 * C_WIDTH
N_EVEN = (DEPTH + 1) // 2
N_ODD = DEPTH // 2

kernel_name = "hybrid_shortconv_stickbreak_hgrn2_macaron"


def rms_norm(x, gain):
    x32 = x.astype(jnp.float32)
    y = x32 * lax.rsqrt(jnp.mean(x32 * x32, axis=-1, keepdims=True) + RMS_EPS)
    return (y * gain.astype(jnp.float32)).astype(x.dtype)


def swiglu(h, w_gate, w_up, w_down):
    return (jax.nn.silu(h @ w_gate) * (h @ w_up)) @ w_down


def stick_breaking_attention(q, k, v):
    bsz, nh, s_len, dh = q.shape
    nb = s_len // SB_BLOCK
    scale = 1.0 / math.sqrt(dh)
    qb = jnp.moveaxis(q.reshape(bsz, nh, nb, SB_BLOCK, dh), 2, 0)
    starts = jnp.arange(nb, dtype=jnp.int32) * SB_BLOCK
    kpos = jnp.arange(s_len, dtype=jnp.int32)

    def one_block(args):
        qi, start = args
        qpos = start + jnp.arange(SB_BLOCK, dtype=jnp.int32)
        mask = kpos[None, :] < qpos[:, None]
        z = jnp.einsum('bhqd,bhkd->bhqk', qi, k) * scale
        log_beta = jax.nn.log_sigmoid(z)
        log_keep = jnp.where(mask, jax.nn.log_sigmoid(-z), 0.0)
        later = lax.cumsum(log_keep, axis=3, reverse=True) - log_keep
        w = jnp.where(mask, jnp.exp(log_beta + later), 0.0)
        return jnp.einsum('bhqk,bhkd->bhqd', w, v)

    out = lax.map(one_block, (qb, starts))
    return jnp.moveaxis(out, 0, 2).reshape(bsz, nh, s_len, dh)


def shortconv_stickbreak_mixer(h, w_in, conv_w, w_out):
    bsz, s_len, _ = h.shape
    proj = h @ w_in
    a_b, a_c, a_x, q, k, v = jnp.split(proj, 6, axis=-1)
    u = a_c * a_x
    conv = lax.conv_general_dilated(
        u, conv_w[:, None, :].astype(u.dtype), window_strides=(1,),
        padding=[(CONV_WIDTH - 1, 0)], dimension_numbers=('NWC', 'WIO', 'NWC'),
        feature_group_count=A_WIDTH)
    y_a = a_b * conv
    def heads(t):
        return t.reshape(bsz, s_len, SB_HEADS, SB_HEAD_DIM).transpose(0, 2, 1, 3).astype(jnp.float32)
    y_b = stick_breaking_attention(heads(q), heads(k), heads(v))
    y_b = y_b.transpose(0, 2, 1, 3).reshape(bsz, s_len, B_WIDTH).astype(h.dtype)
    return jnp.concatenate([y_a, y_b], axis=-1) @ w_out


def chunkwise_gated_recurrence(q, log_f, k, v):
    bsz, nh, s_len, dk = q.shape
    dv = v.shape[-1]
    n_chunks = s_len // CHUNK

    def to_chunks(t):
        return jnp.moveaxis(t.reshape(bsz, nh, n_chunks, CHUNK, t.shape[-1]), 2, 0)

    tri = jnp.tril(jnp.ones((CHUNK, CHUNK), dtype=bool))

    def step(state, inp):
        qc, gc, kc, vc = inp
        b = jnp.cumsum(gc, axis=2)
        o_inter = jnp.einsum('bhtk,bhkv->bhtv', qc * jnp.exp(b), state)
        diff = b[:, :, :, None, :] - b[:, :, None, :, :]
        decay = jnp.exp(jnp.where(tri[None, None, :, :, None], diff, -jnp.inf))
        scores = jnp.einsum('bhtk,bhsk,bhtsk->bhts', qc, kc, decay)
        o_intra = jnp.einsum('bhts,bhsv->bhtv', scores, vc)
        b_last = b[:, :, -1:, :]
        new_state = (jnp.exp(b_last[:, :, 0, :])[..., None] * state
                     + jnp.einsum('bhsk,bhsv->bhkv', kc * jnp.exp(b_last - b), vc))
        return new_state, o_inter + o_intra

    state0 = jnp.zeros((bsz, nh, dk, dv), jnp.float32)
    _, out = lax.scan(step, state0, (to_chunks(q), to_chunks(log_f), to_chunks(k), to_chunks(v)))
    return jnp.moveaxis(out, 0, 2).reshape(bsz, nh, s_len, dv)


def hgrn2_mixer(h, w_in, lower_bound, out_norm, w_out):
    bsz, s_len, _ = h.shape
    proj = h @ w_in
    q, f, i, g = jnp.split(proj, 4, axis=-1)
    lb = lower_bound.astype(jnp.float32)
    log_f = jnp.logaddexp(jnp.log(lb), jnp.log1p(-lb) + jax.nn.log_sigmoid(f.astype(jnp.float32)))
    k = -jnp.expm1(log_f)
    q = jax.nn.silu(q.astype(jnp.float32))

    def heads(t):
        return t.reshape(bsz, s_len, C_HEADS, C_HEAD_DIM).transpose(0, 2, 1, 3).astype(jnp.float32)

    o = chunkwise_gated_recurrence(heads(q), heads(log_f), heads(k), heads(i))
    o = o * lax.rsqrt(jnp.mean(o * o, axis=-1, keepdims=True) + RMS_EPS) * out_norm.astype(jnp.float32)
    o = o.transpose(0, 2, 1, 3).reshape(bsz, s_len, C_WIDTH)
    o = (o * jax.nn.silu(g.astype(jnp.float32))).astype(h.dtype)
    return o @ w_out


def _fwd_setup_inputs(seed: int = 0) -> dict:
    key = jax.random.key(seed)
    ks = jax.random.split(key, 20)
    f32 = jnp.float32

    def w(k, shape, fan_in):
        return jax.random.normal(k, shape, f32) * (fan_in ** -0.5)

    def gain(k, shape):
        return 1.0 + 0.02 * jax.random.normal(k, shape, f32)

    return {
        "x": jax.random.normal(ks[0], (BATCH, SEQ, D_MODEL), f32),
        "ffn_pre_norm": gain(ks[1], (DEPTH, D_MODEL)),
        "ffn_pre_w_gate": w(ks[2], (DEPTH, D_MODEL, D_FF), D_MODEL),
        "ffn_pre_w_up": w(ks[3], (DEPTH, D_MODEL, D_FF), D_MODEL),
        "ffn_pre_w_down": w(ks[4], (DEPTH, D_FF, D_MODEL), D_FF),
        "mix_norm": gain(ks[5], (DEPTH, D_MODEL)),
        "ffn_post_norm": gain(ks[6], (DEPTH, D_MODEL)),
        "ffn_post_w_gate": w(ks[7], (DEPTH, D_MODEL, D_FF), D_MODEL),
        "ffn_post_w_up": w(ks[8], (DEPTH, D_MODEL, D_FF), D_MODEL),
        "ffn_post_w_down": w(ks[9], (DEPTH, D_FF, D_MODEL), D_FF),
        "ab_w_in": w(ks[10], (N_EVEN, D_MODEL, AB_IN_COLS), D_MODEL),
        "ab_conv_w": w(ks[11], (N_EVEN, CONV_WIDTH, A_WIDTH), CONV_WIDTH),
        "ab_w_out": w(ks[12], (N_EVEN, A_WIDTH + B_WIDTH, D_MODEL), A_WIDTH + B_WIDTH),
        "c_w_in": w(ks[13], (N_ODD, D_MODEL, C_IN_COLS), D_MODEL),
        "c_lower_bounds": 0.1 * jax.random.normal(ks[14], (DEPTH, C_WIDTH), f32),
        "c_out_norm": gain(ks[15], (N_ODD, C_HEAD_DIM)),
        "c_w_out": w(ks[16], (N_ODD, C_WIDTH, D_MODEL), C_WIDTH),
        "final_norm": gain(ks[17], (D_MODEL,)),
    }


def _fwd_reference(x, ffn_pre_norm, ffn_pre_w_gate, ffn_pre_w_up, ffn_pre_w_down, mix_norm,
              ffn_post_norm, ffn_post_w_gate, ffn_post_w_up, ffn_post_w_down,
              ab_w_in, ab_conv_w, ab_w_out, c_w_in, c_lower_bounds, c_out_norm, c_w_out,
              final_norm):
    lb_soft = jax.nn.softmax(c_lower_bounds.astype(jnp.float32), axis=0)
    lb_cum = jnp.cumsum(lb_soft, axis=0)
    lower_bounds = lb_cum - lb_cum[0:1]

    h = x
    for layer in range(DEPTH):
        h = h + MACARON_WEIGHT * swiglu(rms_norm(h, ffn_pre_norm[layer]), ffn_pre_w_gate[layer],
                                        ffn_pre_w_up[layer], ffn_pre_w_down[layer])
        hn = rms_norm(h, mix_norm[layer])
        if layer % 2 == 0:
            e = layer // 2
            h = h + shortconv_stickbreak_mixer(hn, ab_w_in[e], ab_conv_w[e], ab_w_out[e])
        else:
            o = layer // 2
            h = h + hgrn2_mixer(hn, c_w_in[o], lower_bounds[layer], c_out_norm[o], c_w_out[o])
        h = h + MACARON_WEIGHT * swiglu(rms_norm(h, ffn_post_norm[layer]), ffn_post_w_gate[layer],
                                        ffn_post_w_up[layer], ffn_post_w_down[layer])
    return rms_norm(h, final_norm)


import jax as _jax
import jax.numpy as _jnp

TWIN_FORMAT = 'train_step'
FWD_PARAMS = ['x', 'ffn_pre_norm', 'ffn_pre_w_gate', 'ffn_pre_w_up', 'ffn_pre_w_down', 'mix_norm', 'ffn_post_norm', 'ffn_post_w_gate', 'ffn_post_w_up', 'ffn_post_w_down', 'ab_w_in', 'ab_conv_w', 'ab_w_out', 'c_w_in', 'c_lower_bounds', 'c_out_norm', 'c_w_out', 'final_norm']
TWIN_WEIGHTS = ['ffn_pre_norm', 'ffn_pre_w_gate', 'ffn_pre_w_up', 'ffn_pre_w_down', 'mix_norm', 'ffn_post_norm', 'ffn_post_w_gate', 'ffn_post_w_up', 'ffn_post_w_down', 'ab_w_in', 'ab_conv_w', 'ab_w_out', 'c_w_in', 'c_lower_bounds', 'c_out_norm', 'c_w_out', 'final_norm']
TWIN_DIFF_INPUT = 'x'
TWIN_INPUTS = ['x', 'ffn_pre_norm', 'ffn_pre_w_gate', 'ffn_pre_w_up', 'ffn_pre_w_down', 'mix_norm', 'ffn_post_norm', 'ffn_post_w_gate', 'ffn_post_w_up', 'ffn_post_w_down', 'ab_w_in', 'ab_conv_w', 'ab_w_out', 'c_w_in', 'c_lower_bounds', 'c_out_norm', 'c_w_out', 'final_norm', 'loss_target', 'm_ffn_pre_norm', 'm_ffn_pre_w_gate', 'm_ffn_pre_w_up', 'm_ffn_pre_w_down', 'm_mix_norm', 'm_ffn_post_norm', 'm_ffn_post_w_gate', 'm_ffn_post_w_up', 'm_ffn_post_w_down', 'm_ab_w_in', 'm_ab_conv_w', 'm_ab_w_out', 'm_c_w_in', 'm_c_lower_bounds', 'm_c_out_norm', 'm_c_w_out', 'm_final_norm', 'v_ffn_pre_norm', 'v_ffn_pre_w_gate', 'v_ffn_pre_w_up', 'v_ffn_pre_w_down', 'v_mix_norm', 'v_ffn_post_norm', 'v_ffn_post_w_gate', 'v_ffn_post_w_up', 'v_ffn_post_w_down', 'v_ab_w_in', 'v_ab_conv_w', 'v_ab_w_out', 'v_c_w_in', 'v_c_lower_bounds', 'v_c_out_norm', 'v_c_w_out', 'v_final_norm']
TWIN_OUTPUTS = ['loss', 'grad_x', 'grad_ffn_pre_norm', 'grad_ffn_pre_w_gate', 'grad_ffn_pre_w_up', 'grad_ffn_pre_w_down', 'grad_mix_norm', 'grad_ffn_post_norm', 'grad_ffn_post_w_gate', 'grad_ffn_post_w_up', 'grad_ffn_post_w_down', 'grad_ab_w_in', 'grad_ab_conv_w', 'grad_ab_w_out', 'grad_c_w_in', 'grad_c_lower_bounds', 'grad_c_out_norm', 'grad_c_w_out', 'grad_final_norm', 'delta_ffn_pre_norm', 'delta_ffn_pre_w_gate', 'delta_ffn_pre_w_up', 'delta_ffn_pre_w_down', 'delta_mix_norm', 'delta_ffn_post_norm', 'delta_ffn_post_w_gate', 'delta_ffn_post_w_up', 'delta_ffn_post_w_down', 'delta_ab_w_in', 'delta_ab_conv_w', 'delta_ab_w_out', 'delta_c_w_in', 'delta_c_lower_bounds', 'delta_c_out_norm', 'delta_c_w_out', 'delta_final_norm', 'new_m_ffn_pre_norm', 'new_m_ffn_pre_w_gate', 'new_m_ffn_pre_w_up', 'new_m_ffn_pre_w_down', 'new_m_mix_norm', 'new_m_ffn_post_norm', 'new_m_ffn_post_w_gate', 'new_m_ffn_post_w_up', 'new_m_ffn_post_w_down', 'new_m_ab_w_in', 'new_m_ab_conv_w', 'new_m_ab_w_out', 'new_m_c_w_in', 'new_m_c_lower_bounds', 'new_m_c_out_norm', 'new_m_c_w_out', 'new_m_final_norm', 'new_v_ffn_pre_norm', 'new_v_ffn_pre_w_gate', 'new_v_ffn_pre_w_up', 'new_v_ffn_pre_w_down', 'new_v_mix_norm', 'new_v_ffn_post_norm', 'new_v_ffn_post_w_gate', 'new_v_ffn_post_w_up', 'new_v_ffn_post_w_down', 'new_v_ab_w_in', 'new_v_ab_conv_w', 'new_v_ab_w_out', 'new_v_c_w_in', 'new_v_c_lower_bounds', 'new_v_c_out_norm', 'new_v_c_w_out', 'new_v_final_norm']
TWIN_LEAF_KINDS = {'loss': 'loss', 'grad_x': 'grad_x', 'grad_ffn_pre_norm': 'grad_w', 'grad_ffn_pre_w_gate': 'grad_w', 'grad_ffn_pre_w_up': 'grad_w', 'grad_ffn_pre_w_down': 'grad_w', 'grad_mix_norm': 'grad_w', 'grad_ffn_post_norm': 'grad_w', 'grad_ffn_post_w_gate': 'grad_w', 'grad_ffn_post_w_up': 'grad_w', 'grad_ffn_post_w_down': 'grad_w', 'grad_ab_w_in': 'grad_w', 'grad_ab_conv_w': 'grad_w', 'grad_ab_w_out': 'grad_w', 'grad_c_w_in': 'grad_w', 'grad_c_lower_bounds': 'grad_w', 'grad_c_out_norm': 'grad_w', 'grad_c_w_out': 'grad_w', 'grad_final_norm': 'grad_w', 'delta_ffn_pre_norm': 'delta_w', 'delta_ffn_pre_w_gate': 'delta_w', 'delta_ffn_pre_w_up': 'delta_w', 'delta_ffn_pre_w_down': 'delta_w', 'delta_mix_norm': 'delta_w', 'delta_ffn_post_norm': 'delta_w', 'delta_ffn_post_w_gate': 'delta_w', 'delta_ffn_post_w_up': 'delta_w', 'delta_ffn_post_w_down': 'delta_w', 'delta_ab_w_in': 'delta_w', 'delta_ab_conv_w': 'delta_w', 'delta_ab_w_out': 'delta_w', 'delta_c_w_in': 'delta_w', 'delta_c_lower_bounds': 'delta_w', 'delta_c_out_norm': 'delta_w', 'delta_c_w_out': 'delta_w', 'delta_final_norm': 'delta_w', 'new_m_ffn_pre_norm': 'new_m', 'new_m_ffn_pre_w_gate': 'new_m', 'new_m_ffn_pre_w_up': 'new_m', 'new_m_ffn_pre_w_down': 'new_m', 'new_m_mix_norm': 'new_m', 'new_m_ffn_post_norm': 'new_m', 'new_m_ffn_post_w_gate': 'new_m', 'new_m_ffn_post_w_up': 'new_m', 'new_m_ffn_post_w_down': 'new_m', 'new_m_ab_w_in': 'new_m', 'new_m_ab_conv_w': 'new_m', 'new_m_ab_w_out': 'new_m', 'new_m_c_w_in': 'new_m', 'new_m_c_lower_bounds': 'new_m', 'new_m_c_out_norm': 'new_m', 'new_m_c_w_out': 'new_m', 'new_m_final_norm': 'new_m', 'new_v_ffn_pre_norm': 'new_v', 'new_v_ffn_pre_w_gate': 'new_v', 'new_v_ffn_pre_w_up': 'new_v', 'new_v_ffn_pre_w_down': 'new_v', 'new_v_mix_norm': 'new_v', 'new_v_ffn_post_norm': 'new_v', 'new_v_ffn_post_w_gate': 'new_v', 'new_v_ffn_post_w_up': 'new_v', 'new_v_ffn_post_w_down': 'new_v', 'new_v_ab_w_in': 'new_v', 'new_v_ab_conv_w': 'new_v', 'new_v_ab_w_out': 'new_v', 'new_v_c_w_in': 'new_v', 'new_v_c_lower_bounds': 'new_v', 'new_v_c_out_norm': 'new_v', 'new_v_c_w_out': 'new_v', 'new_v_final_norm': 'new_v'}


def _forward(args):
    return _fwd_reference(*[args[k] for k in FWD_PARAMS])


def _output_shape():
    out = _jax.eval_shape(lambda: _forward(_fwd_setup_inputs(0)))
    return out.shape, out.dtype

N_MICROBATCH = 1
ADAM_LR = 0.001
ADAM_B1 = 0.9
ADAM_B2 = 0.999
ADAM_EPS = 1e-08
ADAM_WD = 0.01
ADAM_STEP = 10
PER_EXAMPLE_BATCH_AXIS = {'x': 0, 'loss_target': 0}
SHARED_INPUTS = []
_WEIGHT_DTYPES = {'ffn_pre_norm': _jnp.float32, 'ffn_pre_w_gate': _jnp.float32, 'ffn_pre_w_up': _jnp.float32, 'ffn_pre_w_down': _jnp.float32, 'mix_norm': _jnp.float32, 'ffn_post_norm': _jnp.float32, 'ffn_post_w_gate': _jnp.float32, 'ffn_post_w_up': _jnp.float32, 'ffn_post_w_down': _jnp.float32, 'ab_w_in': _jnp.float32, 'ab_conv_w': _jnp.float32, 'ab_w_out': _jnp.float32, 'c_w_in': _jnp.float32, 'c_lower_bounds': _jnp.float32, 'c_out_norm': _jnp.float32, 'c_w_out': _jnp.float32, 'final_norm': _jnp.float32}
MOMENT_SCALE = {'ffn_pre_norm': 8.631289e-02, 'ffn_pre_w_gate': 3.733472e-02, 'ffn_pre_w_up': 3.609568e-02, 'ffn_pre_w_down': 5.991300e-02, 'mix_norm': 1.662956e-01, 'ffn_post_norm': 5.592490e-02, 'ffn_post_w_gate': 2.439046e-02, 'ffn_post_w_up': 2.367259e-02, 'ffn_post_w_down': 3.915344e-02, 'ab_w_in': 1.174203e-01, 'ab_conv_w': 1.492441e-01, 'ab_w_out': 1.286597e-01, 'c_w_in': 5.143731e-02, 'c_lower_bounds': 6.915951e-03, 'c_out_norm': 2.112406e-01, 'c_w_out': 7.099115e-02, 'final_norm': 3.197935e+01}


def _to_microbatches(a, axis):
    t = _jnp.moveaxis(a, axis, 0)
    t = t.reshape((N_MICROBATCH, t.shape[0] // N_MICROBATCH) + t.shape[1:])
    return _jnp.moveaxis(t, 1, axis + 1)


def setup_inputs(seed: int = 0) -> dict:
    inp = _fwd_setup_inputs(seed)
    key = _jax.random.fold_in(_jax.random.key(seed), 7919)
    shape, _ = _output_shape()
    out = dict(inp)
    out["loss_target"] = _jax.random.normal(_jax.random.fold_in(key, 0), shape, _jnp.float32)
    for i, name in enumerate(TWIN_WEIGHTS):
        w = inp[name].astype(_jnp.float32)
        if MOMENT_SCALE is None:
            s = _jnp.sqrt(_jnp.mean(_jnp.square(w)) + 1e-30)
        else:
            s = MOMENT_SCALE[name]
        km, kv = _jax.random.split(_jax.random.fold_in(key, i + 1))
        out[name] = w
        out["m_" + name] = s * _jax.random.normal(km, w.shape, _jnp.float32)
        out["v_" + name] = (s * s) * _jax.random.uniform(kv, w.shape, _jnp.float32, 0.5, 1.5)
    if N_MICROBATCH > 1:
        for name, axis in PER_EXAMPLE_BATCH_AXIS.items():
            out[name] = _to_microbatches(out[name], axis)
    return {'x': out['x'], 'ffn_pre_norm': out['ffn_pre_norm'], 'ffn_pre_w_gate': out['ffn_pre_w_gate'], 'ffn_pre_w_up': out['ffn_pre_w_up'], 'ffn_pre_w_down': out['ffn_pre_w_down'], 'mix_norm': out['mix_norm'], 'ffn_post_norm': out['ffn_post_norm'], 'ffn_post_w_gate': out['ffn_post_w_gate'], 'ffn_post_w_up': out['ffn_post_w_up'], 'ffn_post_w_down': out['ffn_post_w_down'], 'ab_w_in': out['ab_w_in'], 'ab_conv_w': out['ab_conv_w'], 'ab_w_out': out['ab_w_out'], 'c_w_in': out['c_w_in'], 'c_lower_bounds': out['c_lower_bounds'], 'c_out_norm': out['c_out_norm'], 'c_w_out': out['c_w_out'], 'final_norm': out['final_norm'], 'loss_target': out['loss_target'], 'm_ffn_pre_norm': out['m_ffn_pre_norm'], 'm_ffn_pre_w_gate': out['m_ffn_pre_w_gate'], 'm_ffn_pre_w_up': out['m_ffn_pre_w_up'], 'm_ffn_pre_w_down': out['m_ffn_pre_w_down'], 'm_mix_norm': out['m_mix_norm'], 'm_ffn_post_norm': out['m_ffn_post_norm'], 'm_ffn_post_w_gate': out['m_ffn_post_w_gate'], 'm_ffn_post_w_up': out['m_ffn_post_w_up'], 'm_ffn_post_w_down': out['m_ffn_post_w_down'], 'm_ab_w_in': out['m_ab_w_in'], 'm_ab_conv_w': out['m_ab_conv_w'], 'm_ab_w_out': out['m_ab_w_out'], 'm_c_w_in': out['m_c_w_in'], 'm_c_lower_bounds': out['m_c_lower_bounds'], 'm_c_out_norm': out['m_c_out_norm'], 'm_c_w_out': out['m_c_w_out'], 'm_final_norm': out['m_final_norm'], 'v_ffn_pre_norm': out['v_ffn_pre_norm'], 'v_ffn_pre_w_gate': out['v_ffn_pre_w_gate'], 'v_ffn_pre_w_up': out['v_ffn_pre_w_up'], 'v_ffn_pre_w_down': out['v_ffn_pre_w_down'], 'v_mix_norm': out['v_mix_norm'], 'v_ffn_post_norm': out['v_ffn_post_norm'], 'v_ffn_post_w_gate': out['v_ffn_post_w_gate'], 'v_ffn_post_w_up': out['v_ffn_post_w_up'], 'v_ffn_post_w_down': out['v_ffn_post_w_down'], 'v_ab_w_in': out['v_ab_w_in'], 'v_ab_conv_w': out['v_ab_conv_w'], 'v_ab_w_out': out['v_ab_w_out'], 'v_c_w_in': out['v_c_w_in'], 'v_c_lower_bounds': out['v_c_lower_bounds'], 'v_c_out_norm': out['v_c_out_norm'], 'v_c_w_out': out['v_c_w_out'], 'v_final_norm': out['v_final_norm']}


def _loss(weights, diff, rest, loss_target):
    with _jax.named_scope("forward"):
        args = {**rest, TWIN_DIFF_INPUT: diff, **{k: w.astype(_WEIGHT_DTYPES[k]) for k, w in weights.items()}}
        y = _forward(args)
    with _jax.named_scope("loss_head"):
        err = _jnp.square(y.astype(_jnp.float32) - loss_target)
        return 0.5 * _jnp.sum(_jnp.mean(err, axis=-1)) if err.ndim else 0.5 * err


def _adamw(w, g, m, v):
    m = ADAM_B1 * m + (1.0 - ADAM_B1) * g
    v = ADAM_B2 * v + (1.0 - ADAM_B2) * _jnp.square(g)
    m_hat = m / (1.0 - ADAM_B1 ** ADAM_STEP)
    v_hat = v / (1.0 - ADAM_B2 ** ADAM_STEP)
    delta = -ADAM_LR * (m_hat / (_jnp.sqrt(v_hat) + ADAM_EPS) + ADAM_WD * w)
    return delta, m, v


def reference(x, ffn_pre_norm, ffn_pre_w_gate, ffn_pre_w_up, ffn_pre_w_down, mix_norm, ffn_post_norm, ffn_post_w_gate, ffn_post_w_up, ffn_post_w_down, ab_w_in, ab_conv_w, ab_w_out, c_w_in, c_lower_bounds, c_out_norm, c_w_out, final_norm, loss_target, m_ffn_pre_norm, m_ffn_pre_w_gate, m_ffn_pre_w_up, m_ffn_pre_w_down, m_mix_norm, m_ffn_post_norm, m_ffn_post_w_gate, m_ffn_post_w_up, m_ffn_post_w_down, m_ab_w_in, m_ab_conv_w, m_ab_w_out, m_c_w_in, m_c_lower_bounds, m_c_out_norm, m_c_w_out, m_final_norm, v_ffn_pre_norm, v_ffn_pre_w_gate, v_ffn_pre_w_up, v_ffn_pre_w_down, v_mix_norm, v_ffn_post_norm, v_ffn_post_w_gate, v_ffn_post_w_up, v_ffn_post_w_down, v_ab_w_in, v_ab_conv_w, v_ab_w_out, v_c_w_in, v_c_lower_bounds, v_c_out_norm, v_c_w_out, v_final_norm):
    given = dict(x=x, ffn_pre_norm=ffn_pre_norm, ffn_pre_w_gate=ffn_pre_w_gate, ffn_pre_w_up=ffn_pre_w_up, ffn_pre_w_down=ffn_pre_w_down, mix_norm=mix_norm, ffn_post_norm=ffn_post_norm, ffn_post_w_gate=ffn_post_w_gate, ffn_post_w_up=ffn_post_w_up, ffn_post_w_down=ffn_post_w_down, ab_w_in=ab_w_in, ab_conv_w=ab_conv_w, ab_w_out=ab_w_out, c_w_in=c_w_in, c_lower_bounds=c_lower_bounds, c_out_norm=c_out_norm, c_w_out=c_w_out, final_norm=final_norm, loss_target=loss_target, m_ffn_pre_norm=m_ffn_pre_norm, m_ffn_pre_w_gate=m_ffn_pre_w_gate, m_ffn_pre_w_up=m_ffn_pre_w_up, m_ffn_pre_w_down=m_ffn_pre_w_down, m_mix_norm=m_mix_norm, m_ffn_post_norm=m_ffn_post_norm, m_ffn_post_w_gate=m_ffn_post_w_gate, m_ffn_post_w_up=m_ffn_post_w_up, m_ffn_post_w_down=m_ffn_post_w_down, m_ab_w_in=m_ab_w_in, m_ab_conv_w=m_ab_conv_w, m_ab_w_out=m_ab_w_out, m_c_w_in=m_c_w_in, m_c_lower_bounds=m_c_lower_bounds, m_c_out_norm=m_c_out_norm, m_c_w_out=m_c_w_out, m_final_norm=m_final_norm, v_ffn_pre_norm=v_ffn_pre_norm, v_ffn_pre_w_gate=v_ffn_pre_w_gate, v_ffn_pre_w_up=v_ffn_pre_w_up, v_ffn_pre_w_down=v_ffn_pre_w_down, v_mix_norm=v_mix_norm, v_ffn_post_norm=v_ffn_post_norm, v_ffn_post_w_gate=v_ffn_post_w_gate, v_ffn_post_w_up=v_ffn_post_w_up, v_ffn_post_w_down=v_ffn_post_w_down, v_ab_w_in=v_ab_w_in, v_ab_conv_w=v_ab_conv_w, v_ab_w_out=v_ab_w_out, v_c_w_in=v_c_w_in, v_c_lower_bounds=v_c_lower_bounds, v_c_out_norm=v_c_out_norm, v_c_w_out=v_c_w_out, v_final_norm=v_final_norm)
    weights = {n: given[n] for n in TWIN_WEIGHTS}
    shared = {n: given[n] for n in SHARED_INPUTS}
    per_example = {n: given[n] for n in ['x']}
    grad_fn = _jax.value_and_grad(_loss, argnums=(0, 1))

    def one_microbatch(ex, loss_target):
        ex = dict(ex)
        diff = ex.pop(TWIN_DIFF_INPUT)
        return grad_fn(weights, diff, {**shared, **ex}, loss_target)

    if N_MICROBATCH == 1:
        loss, (grad_w, grad_x) = one_microbatch(per_example, given["loss_target"])
    else:
        def body(carry, xs):
            loss_sum, grad_sum = carry
            l_k, (gw_k, gx_k) = one_microbatch(xs[0], xs[1])
            with _jax.named_scope("update"):
                return (loss_sum + l_k, _jax.tree.map(_jnp.add, grad_sum, gw_k)), gx_k

        init = (_jnp.zeros((), _jnp.float32), _jax.tree.map(_jnp.zeros_like, weights))
        (loss, grad_w), grad_x = _jax.lax.scan(body, init, (per_example, given["loss_target"]))
    with _jax.named_scope("update"):
        delta_w, new_m, new_v = {}, {}, {}
        for n in TWIN_WEIGHTS:
            delta_w[n], new_m[n], new_v[n] = _adamw(weights[n], grad_w[n], given["m_" + n], given["v_" + n])
    return (loss, grad_x, *[grad_w[n] for n in TWIN_WEIGHTS], *[delta_w[n] for n in TWIN_WEIGHTS],
            *[new_m[n] for n in TWIN_WEIGHTS], *[new_v[n] for n in TWIN_WEIGHTS])
```

```python
import functools
import math

import jax
import jax.numpy as jnp
from jax import lax
from jax.experimental import pallas as pl
from jax.experimental.pallas import tpu as pltpu

F32 = jnp.float32
BF16 = jnp.bfloat16
MESH = pl.DeviceIdType.MESH
ANY = pl.BlockSpec(memory_space=pl.ANY)

D = 1024
FS = 704
NSH = 4
RMS_EPS = 1e-6
MACARON = 0.5
CHUNK = 64
HD = 128
SBQ = 128
CONV_HALO = 8
LANE = 128
ROW_TILE = 512
VMEM_LIMIT = 48 * 1024 * 1024

ADAM_LR, ADAM_B1, ADAM_B2, ADAM_EPS, ADAM_WD, ADAM_STEP = 0.001, 0.9, 0.999, 1e-08, 0.01, 10

NN = ((1,), (0,))
NT = ((1,), (1,))
TN = ((0,), (0,))

SMALL_ROWS = 16
R_PRE, R_MIX, R_POST, R_CLB, R_FIN, R_GAM, R_CONV, R_LOSS = 0, 2, 4, 6, 8, 9, 10, 13

B_DOWN = 0
B_ABOUT = 4 * FS
B_CIN = B_ABOUT + 256
B_COUT = B_CIN + 1024
B_ROWS = B_COUT + 256


def _dg(a, b, dims):
    return lax.dot_general(a, b, (dims, ((), ())), preferred_element_type=F32)


def _split(x):
    hi = x.astype(BF16)
    lo = (x - hi.astype(F32)).astype(BF16)
    return hi, lo


def _dot3(a, b, dims):
    ah, al = _split(a)
    bh, bl = _split(b)
    return _dg(ah, bh, dims) + _dg(ah, bl, dims) + _dg(al, bh, dims)


def _sigmoid(x):
    return 1.0 / (1.0 + jnp.exp(-x))


def _params(sem):
    return pltpu.CompilerParams(dimension_semantics=sem, vmem_limit_bytes=VMEM_LIMIT)


def _spec(shape, imap):
    return pl.BlockSpec(shape, imap)


def _mm(name, pairs, *, grid, o_shape, o_dtype, o_spec, dims, kaxis, nk, acc_shape=None, res=None, scale=None):
    npairs = len(pairs)
    operands, specs = [], []
    for a, a_spec, b, b_spec in pairs:
        operands += [a, b]
        specs += [a_spec, b_spec]
    if res is not None:
        operands.append(res[0])
        specs.append(res[1])

    def body(*refs):
        o_ref = refs[2 * npairs + (1 if res is not None else 0)]
        part = None
        for n in range(npairs):
            d = _dg(refs[2 * n][...], refs[2 * n + 1][...], dims)
            part = d if part is None else part + d

        def finish(val):
            if scale is not None:
                val = val * scale
            if res is not None:
                val = val + refs[2 * npairs][...]
            o_ref[...] = val.astype(o_dtype)

        if nk == 1:
            finish(part)
        else:
            acc_ref = refs[-1]
            k = pl.program_id(kaxis)

            @pl.when(k == 0)
            def _():
                acc_ref[...] = part

            @pl.when(k > 0)
            def _():
                acc_ref[...] += part

            @pl.when(k == nk - 1)
            def _():
                finish(acc_ref[...])

    sem = tuple("arbitrary" if (ax == kaxis and nk > 1) else "parallel" for ax in range(len(grid)))
    return pl.pallas_call(
        body, name=name, grid=grid, in_specs=specs, out_specs=o_spec,
        out_shape=jax.ShapeDtypeStruct(o_shape, o_dtype),
        scratch_shapes=[pltpu.VMEM(acc_shape, F32)] if nk > 1 else [],
        compiler_params=_params(sem),
    )(*operands)


def _norm_fwd(name, h, gain_slab, row):
    t = h.shape[0]
    tm = min(ROW_TILE, t)

    def body(h_ref, g_ref, o_ref):
        x = h_ref[...]
        r = lax.rsqrt(jnp.mean(x * x, axis=-1, keepdims=True) + RMS_EPS)
        o_ref[...] = (x * r * g_ref[...]).astype(BF16)

    return pl.pallas_call(
        body, name=name, grid=(t // tm,),
        in_specs=[_spec((tm, D), lambda i: (i, 0)), _spec((None, 1, D), lambda i: (row, 0, 0))],
        out_specs=_spec((tm, D), lambda i: (i, 0)),
        out_shape=jax.ShapeDtypeStruct((t, D), BF16),
        compiler_params=_params(("parallel",)),
    )(h, gain_slab)


def _norm_bwd(name, dhn, h, gain_slab, row, dres):
    t = h.shape[0]
    tm = min(ROW_TILE, t)
    nt = t // tm

    def body(dhn_ref, h_ref, g_ref, dres_ref, dh_ref, dhb_ref, dg_ref, acc_ref):
        i = pl.program_id(0)
        x = h_ref[...]
        r = lax.rsqrt(jnp.mean(x * x, axis=-1, keepdims=True) + RMS_EPS)
        xh = x * r
        dy = dhn_ref[...]
        gdy = dy * g_ref[...]
        dx = (gdy - xh * jnp.mean(gdy * xh, axis=-1, keepdims=True)) * r
        dh = dres_ref[...] + dx
        dh_ref[...] = dh
        dhb_ref[...] = dh.astype(BF16)
        part = jnp.sum((dy * xh).reshape(tm // 8, 8, D), axis=0)

        @pl.when(i == 0)
        def _():
            acc_ref[...] = part

        @pl.when(i > 0)
        def _():
            acc_ref[...] += part

        @pl.when(i == nt - 1)
        def _():
            dg_ref[...] = jnp.sum(acc_ref[...], axis=0, keepdims=True)

    row_spec = _spec((tm, D), lambda i: (i, 0))
    return pl.pallas_call(
        body, name=name, grid=(nt,),
        in_specs=[row_spec, row_spec, _spec((None, 1, D), lambda i: (row, 0, 0)), row_spec],
        out_specs=[row_spec, row_spec, _spec((1, D), lambda i: (0, 0))],
        out_shape=[jax.ShapeDtypeStruct((t, D), F32), jax.ShapeDtypeStruct((t, D), BF16),
                   jax.ShapeDtypeStruct((1, D), F32)],
        scratch_shapes=[pltpu.VMEM((8, D), F32)],
        compiler_params=_params(("arbitrary",)),
    )(dhn, h, gain_slab, dres)


def _final_loss(h, gain_slab, target):
    t = h.shape[0]
    tm = min(ROW_TILE, t)
    nt = t // tm

    def body(h_ref, g_ref, t_ref, dh_ref, dhb_ref, dg_ref, loss_ref, acc_ref, lacc_ref):
        i = pl.program_id(0)
        x = h_ref[...]
        g = g_ref[...]
        r = lax.rsqrt(jnp.mean(x * x, axis=-1, keepdims=True) + RMS_EPS)
        xh = x * r
        err = xh * g - t_ref[...]
        dy = err * (1.0 / D)
        gdy = dy * g
        dh = (gdy - xh * jnp.mean(gdy * xh, axis=-1, keepdims=True)) * r
        dh_ref[...] = dh
        dhb_ref[...] = dh.astype(BF16)
        part = jnp.sum((dy * xh).reshape(tm // 8, 8, D), axis=0)
        lpart = jnp.sum((err * err).reshape(tm // 8, 8, D), axis=0)

        @pl.when(i == 0)
        def _():
            acc_ref[...] = part
            lacc_ref[...] = lpart

        @pl.when(i > 0)
        def _():
            acc_ref[...] += part
            lacc_ref[...] += lpart

        @pl.when(i == nt - 1)
        def _():
            dg_ref[...] = jnp.sum(acc_ref[...], axis=0, keepdims=True)
            rows = jnp.sum(lacc_ref[...], axis=0, keepdims=True)
            loss_ref[...] = jnp.sum(rows, axis=1, keepdims=True) * (0.5 / D)

    row_spec = _spec((tm, D), lambda i: (i, 0))
    return pl.pallas_call(
        body, name="final_loss", grid=(nt,),
        in_specs=[row_spec, _spec((None, 1, D), lambda i: (R_FIN, 0, 0)), row_spec],
        out_specs=[row_spec, row_spec, _spec((1, D), lambda i: (0, 0)), _spec((1, 1), lambda i: (0, 0))],
        out_shape=[jax.ShapeDtypeStruct((t, D), F32), jax.ShapeDtypeStruct((t, D), BF16),
                   jax.ShapeDtypeStruct((1, D), F32), jax.ShapeDtypeStruct((1, 1), F32)],
        scratch_shapes=[pltpu.VMEM((8, D), F32), pltpu.VMEM((8, D), F32)],
        compiler_params=_params(("arbitrary",)),
    )(h, gain_slab, target)


def _ffn_up(name, hn, wa, gate_idx, up_idx):
    t = hn.shape[0]
    tm = min(ROW_TILE, t)

    def body(x_ref, wg_ref, wu_ref, g_ref, u_ref, a_ref):
        x = x_ref[...]
        g = _dg(x, wg_ref[...], NN)
        u = _dg(x, wu_ref[...], NN)
        g_ref[...] = g.astype(BF16)
        u_ref[...] = u.astype(BF16)
        a_ref[...] = (g * _sigmoid(g) * u).astype(BF16)

    act = _spec((None, tm, FS), lambda s, i: (s, i, 0))
    shape = jax.ShapeDtypeStruct((NSH, t, FS), BF16)
    return pl.pallas_call(
        body, name=name, grid=(NSH, t // tm),
        in_specs=[_spec((tm, D), lambda s, i: (i, 0)),
                  _spec((None, None, D, FS), lambda s, i: (s, gate_idx, 0, 0)),
                  _spec((None, None, D, FS), lambda s, i: (s, up_idx, 0, 0))],
        out_specs=[act, act, act], out_shape=[shape, shape, shape],
        compiler_params=_params(("parallel", "parallel")),
    )(hn, wa, wa)


def _ffn_down(name, a, wb, down_idx, h):
    t = h.shape[0]
    tm = min(ROW_TILE, t)
    return _mm(name, [(a, _spec((None, tm, FS), lambda i, k: (k, i, 0)),
                       wb, _spec((None, FS, D), lambda i, k: (k, down_idx, 0)))],
               grid=(t // tm, NSH), o_shape=(t, D), o_dtype=F32, o_spec=_spec((tm, D), lambda i, k: (i, 0)),
               dims=NN, kaxis=1, nk=NSH, acc_shape=(tm, D), res=(h, _spec((tm, D), lambda i, k: (i, 0))),
               scale=MACARON)


def _ffn_bwd_up(name, dhb, wb, down_idx, g, u):
    t = dhb.shape[0]
    tm = min(ROW_TILE, t)

    def body(dh_ref, wd_ref, g_ref, u_ref, dg_ref, du_ref):
        da = _dg(dh_ref[...], wd_ref[...], NT) * MACARON
        gv = g_ref[...].astype(F32)
        uv = u_ref[...].astype(F32)
        sg = _sigmoid(gv)
        du_ref[...] = (da * gv * sg).astype(BF16)
        dg_ref[...] = (da * uv * (sg * (1.0 + gv * (1.0 - sg)))).astype(BF16)

    act = _spec((None, tm, FS), lambda s, i: (s, i, 0))
    shape = jax.ShapeDtypeStruct((NSH, t, FS), BF16)
    return pl.pallas_call(
        body, name=name, grid=(NSH, t // tm),
        in_specs=[_spec((tm, D), lambda s, i: (i, 0)), _spec((None, FS, D), lambda s, i: (s, down_idx, 0)), act, act],
        out_specs=[act, act], out_shape=[shape, shape],
        compiler_params=_params(("parallel", "parallel")),
    )(dhb, wb, g, u)


def _wgrad(name, a, a_spec, b, b_spec, out_rows, out_cols, t, tt, scale=None):
    return _mm(name, [(a, a_spec, b, b_spec)], grid=(NSH, t // tt),
               o_shape=(NSH, out_rows, out_cols), o_dtype=F32,
               o_spec=_spec((None, out_rows, out_cols), lambda s, k: (s, 0, 0)),
               dims=TN, kaxis=1, nk=t // tt, acc_shape=(out_rows, out_cols), scale=scale)


def _ffn_backward(tag, dh, dhb, h_in, hn, g, u, a, wa, wb, gate_idx, up_idx, down_idx, small, norm_row):
    t = dh.shape[0]
    tm = min(ROW_TILE, t)
    tt = tm
    dg, du = _ffn_bwd_up(tag + "_bwd_up", dhb, wb, down_idx, g, u)
    tok = _spec((tt, D), lambda s, k: (k, 0))
    hid = _spec((None, tt, FS), lambda s, k: (s, k, 0))
    d_wd = _wgrad(tag + "_dwd", a, hid, dhb, tok, FS, D, t, tt, scale=MACARON)
    d_wg = _wgrad(tag + "_dwg", hn, tok, dg, hid, D, FS, t, tt)
    d_wu = _wgrad(tag + "_dwu", hn, tok, du, hid, D, FS, t, tt)
    act = _spec((None, tm, FS), lambda i, k: (k, i, 0))
    dhn = _mm(tag + "_dhn",
              [(dg, act, wa, _spec((None, None, D, FS), lambda i, k: (k, gate_idx, 0, 0))),
               (du, act, wa, _spec((None, None, D, FS), lambda i, k: (k, up_idx, 0, 0)))],
              grid=(t // tm, NSH), o_shape=(t, D), o_dtype=F32, o_spec=_spec((tm, D), lambda i, k: (i, 0)),
              dims=NT, kaxis=1, nk=NSH, acc_shape=(tm, D))
    dh_in, dhb_in, d_gain = _norm_bwd(tag + "_norm_bwd", dhn, h_in, small, norm_row, dh)
    return dh_in, dhb_in, d_wg, d_wu, d_wd, d_gain


def _conv_fwd(proj_a, conv_w):
    t = proj_a.shape[0]
    tm = min(ROW_TILE, t)
    hb = tm // CONV_HALO

    def body(ab_ref, ac_ref, ax_ref, acp_ref, axp_ref, w_ref, y_ref):
        i = pl.program_id(1)
        u = ac_ref[...] * ax_ref[...]
        up = jnp.where(i > 0, acp_ref[...] * axp_ref[...], 0.0)
        ext = jnp.concatenate([up, u], axis=0)
        u1 = pltpu.roll(ext, 1, 0)[CONV_HALO:]
        u2 = pltpu.roll(ext, 2, 0)[CONV_HALO:]
        w = w_ref[...]
        conv = w[0:1] * u2 + w[1:2] * u1 + w[2:3] * u
        y_ref[...] = (ab_ref[...] * conv).astype(BF16)

    def cur(off):
        return _spec((tm, LANE), lambda j, i: (i, off + j))

    def prev(off):
        return _spec((CONV_HALO, LANE), lambda j, i: (jnp.maximum(i * hb - 1, 0), off + j))

    return pl.pallas_call(
        body, name="conv_fwd", grid=(4, t // tm),
        in_specs=[cur(0), cur(4), cur(8), prev(4), prev(8),
                  _spec((None, None, 8, LANE), lambda j, i: (j, 0, 0, 0))],
        out_specs=_spec((tm, LANE), lambda j, i: (i, j)),
        out_shape=jax.ShapeDtypeStruct((t, 512), BF16),
        compiler_params=_params(("parallel", "parallel")),
    )(proj_a, proj_a, proj_a, proj_a, proj_a, conv_w)


def _conv_bwd(proj_a, conv_w, dy):
    t = proj_a.shape[0]
    tm = min(ROW_TILE, t)
    hb = tm // CONV_HALO
    nt = t // tm

    def body(ab_ref, ac_ref, ax_ref, dy_ref, acp_ref, axp_ref, abn_ref, dyn_ref, w_ref,
             dab_ref, dac_ref, dax_ref, dw_ref, acc_ref):
        i = pl.program_id(1)
        ab, ac, ax = ab_ref[...], ac_ref[...], ax_ref[...]
        u = ac * ax
        up = jnp.where(i > 0, acp_ref[...] * axp_ref[...], 0.0)
        ext = jnp.concatenate([up, u], axis=0)
        u1 = pltpu.roll(ext, 1, 0)[CONV_HALO:]
        u2 = pltpu.roll(ext, 2, 0)[CONV_HALO:]
        w = w_ref[...]
        conv = w[0:1] * u2 + w[1:2] * u1 + w[2:3] * u
        dy_v = dy_ref[...]
        dab_ref[...] = (dy_v * conv).astype(BF16)
        dc = dy_v * ab
        dcn = jnp.where(i < nt - 1, dyn_ref[...] * abn_ref[...], 0.0)
        extn = jnp.concatenate([dc, dcn], axis=0)
        n = tm + CONV_HALO
        dc1 = pltpu.roll(extn, n - 1, 0)[:tm]
        dc2 = pltpu.roll(extn, n - 2, 0)[:tm]
        du = w[2:3] * dc + w[1:2] * dc1 + w[0:1] * dc2
        dac_ref[...] = (du * ax).astype(BF16)
        dax_ref[...] = (du * ac).astype(BF16)
        rid = lax.broadcasted_iota(jnp.int32, (8, LANE), 0)
        part = jnp.where(rid == 0, jnp.sum(dc * u2, axis=0, keepdims=True),
                         jnp.where(rid == 1, jnp.sum(dc * u1, axis=0, keepdims=True),
                                   jnp.where(rid == 2, jnp.sum(dc * u, axis=0, keepdims=True), 0.0)))

        @pl.when(i == 0)
        def _():
            acc_ref[...] = part

        @pl.when(i > 0)
        def _():
            acc_ref[...] += part

        @pl.when(i == nt - 1)
        def _():
            dw_ref[...] = acc_ref[...]

    def cur(off):
        return _spec((tm, LANE), lambda j, i: (i, off + j))

    def prev(off):
        return _spec((CONV_HALO, LANE), lambda j, i: (jnp.maximum(i * hb - 1, 0), off + j))

    def nxt(off):
        return _spec((CONV_HALO, LANE), lambda j, i: (jnp.minimum((i + 1) * hb, nt * hb - 1), off + j))

    outs = pl.pallas_call(
        body, name="conv_bwd", grid=(4, nt),
        in_specs=[cur(0), cur(4), cur(8), cur(0), prev(4), prev(8), nxt(0), nxt(0),
                  _spec((None, None, 8, LANE), lambda j, i: (j, 0, 0, 0))],
        out_specs=[_spec((tm, LANE), lambda j, i: (i, j)), _spec((tm, LANE), lambda j, i: (i, j)),
                   _spec((tm, LANE), lambda j, i: (i, j)), _spec((None, 8, LANE), lambda j, i: (j, 0, 0))],
        out_shape=[jax.ShapeDtypeStruct((t, 512), BF16), jax.ShapeDtypeStruct((t, 512), BF16),
                   jax.ShapeDtypeStruct((t, 512), BF16), jax.ShapeDtypeStruct((4, 8, LANE), F32)],
        scratch_shapes=[pltpu.VMEM((8, LANE), F32)],
        compiler_params=_params(("parallel", "arbitrary")),
    )(proj_a, proj_a, proj_a, dy, proj_a, proj_a, proj_a, dy, conv_w)
    return outs


def _log_sigmoid(z):
    return jnp.minimum(z, 0.0) - jnp.log(1.0 + jnp.exp(-jnp.abs(z)))


def _sb_masks():
    row = lax.broadcasted_iota(jnp.int32, (SBQ, SBQ), 0)
    col = lax.broadcasted_iota(jnp.int32, (SBQ, SBQ), 1)
    return row, col


def _ones_where(mask):
    return jnp.where(mask, 1.0, 0.0).astype(BF16)


def _head_mask(head):
    lane = lax.broadcasted_iota(jnp.int32, (1, LANE), 1)
    return lane >= 64 if head else lane < 64


def _sb_fwd(proj_b):
    t = proj_b.shape[0]
    nq = t // SBQ
    scale = 1.0 / math.sqrt(64.0)

    def body(q_ref, k_ref, v_ref, y_ref, l_ref):
        qi = pl.program_id(1)
        row, col = _sb_masks()
        strict = col < row
        m_suffix = _ones_where(row > col)
        q_all = q_ref[...]
        out = jnp.zeros((SBQ, LANE), F32)
        tot = jnp.zeros((SBQ, LANE), F32)
        for head in range(2):
            hm = _head_mask(head)
            qh = jnp.where(hm, q_all, jnp.zeros_like(q_all))

            def block(kb, carry, diag):
                run, acc = carry
                start = pl.multiple_of(kb * SBQ, SBQ)
                kk = k_ref[pl.ds(start, SBQ), :]
                vv = v_ref[pl.ds(start, SBQ), :]
                z = _dg(qh, kk, NT) * scale
                lb = _log_sigmoid(z)
                lk = lb - z
                if diag:
                    lk = jnp.where(strict, lk, 0.0)
                hi, lo = _split(lk)
                later = run + _dg(hi, m_suffix, NN) + _dg(lo, m_suffix, NN)
                w = jnp.exp(lb + later)
                if diag:
                    w = jnp.where(strict, w, 0.0)
                acc = acc + _dg(w.astype(BF16), vv, NN)
                run = run + jnp.sum(hi.astype(F32) + lo.astype(F32), axis=1, keepdims=True)
                return run, acc

            carry = (jnp.zeros((SBQ, 1), F32), jnp.zeros((SBQ, LANE), F32))
            carry = block(qi, carry, True)
            run, acc = lax.fori_loop(0, qi, lambda n, c: block(qi - 1 - n, c, False), carry)
            out = jnp.where(hm, acc, out)
            tot = jnp.where(hm, run, tot)
        y_ref[...] = out.astype(BF16)
        l_ref[...] = tot

    return pl.pallas_call(
        body, name="sb_fwd", grid=(4, nq),
        in_specs=[_spec((SBQ, LANE), lambda hp, i: (i, hp)),
                  _spec((t, LANE), lambda hp, i: (0, 4 + hp)),
                  _spec((t, LANE), lambda hp, i: (0, 8 + hp))],
        out_specs=[_spec((SBQ, LANE), lambda hp, i: (i, hp)), _spec((None, SBQ, LANE), lambda hp, i: (hp, i, 0))],
        out_shape=[jax.ShapeDtypeStruct((t, 512), BF16), jax.ShapeDtypeStruct((4, t, LANE), F32)],
        compiler_params=_params(("parallel", "parallel")),
    )(proj_b, proj_b, proj_b)


def _sb_bwd(proj_b, dy, ltot):
    t = proj_b.shape[0]
    nq = t // SBQ
    scale = 1.0 / math.sqrt(64.0)

    def body(q_ref, k_ref, v_ref, dy_ref, l_ref, dq_ref, dk_ref, dv_ref, dk_acc, dv_acc):
        qi = pl.program_id(1)

        @pl.when(qi == 0)
        def _():
            dk_acc[...] = jnp.zeros_like(dk_acc)
            dv_acc[...] = jnp.zeros_like(dv_acc)

        row, col = _sb_masks()
        strict = col < row
        m_prefix = _ones_where(row <= col)
        m_before = _ones_where(row < col)
        q_all = q_ref[...]
        do_all = dy_ref[...].astype(BF16)
        l_all = l_ref[...]
        dq_out = jnp.zeros((SBQ, LANE), F32)
        for head in range(2):
            hm = _head_mask(head)
            qh = jnp.where(hm, q_all, jnp.zeros_like(q_all))
            doh = jnp.where(hm, do_all, jnp.zeros_like(do_all))
            ltot_h = l_all[:, head * 64:head * 64 + 1]

            def block(kb, carry, diag):
                seen, dseen, dq = carry
                start = pl.multiple_of(kb * SBQ, SBQ)
                kk = k_ref[pl.ds(start, SBQ), :]
                vv = v_ref[pl.ds(start, SBQ), :]
                z = _dg(qh, kk, NT) * scale
                lb = _log_sigmoid(z)
                lk = lb - z
                if diag:
                    lk = jnp.where(strict, lk, 0.0)
                hi, lo = _split(lk)
                later = (ltot_h - seen) - (_dg(hi, m_prefix, NN) + _dg(lo, m_prefix, NN))
                w = jnp.exp(lb + later)
                if diag:
                    w = jnp.where(strict, w, 0.0)
                dw = _dg(doh, vv, NT)
                da = w * dw
                dv_acc[pl.ds(start, SBQ), :] += _dg(w.astype(BF16), doh, TN)
                dah, dal = _split(da)
                dlk = dseen + _dg(dah, m_before, NN) + _dg(dal, m_before, NN)
                sig = jnp.exp(lb)
                dz = (da * (1.0 - sig) - dlk * sig) * scale
                if diag:
                    dz = jnp.where(strict, dz, 0.0)
                dzb = dz.astype(BF16)
                dq = dq + _dg(dzb, kk, NN)
                dk_acc[pl.ds(start, SBQ), :] += _dg(dzb, qh, TN)
                seen = seen + jnp.sum(hi.astype(F32) + lo.astype(F32), axis=1, keepdims=True)
                dseen = dseen + jnp.sum(da, axis=1, keepdims=True)
                return seen, dseen, dq

            carry = (jnp.zeros((SBQ, 1), F32), jnp.zeros((SBQ, 1), F32), jnp.zeros((SBQ, LANE), F32))
            carry = lax.fori_loop(0, qi, lambda n, c: block(n, c, False), carry)
            _, _, dq = block(qi, carry, True)
            dq_out = jnp.where(hm, dq, dq_out)
        dq_ref[...] = dq_out.astype(BF16)

        @pl.when(qi == nq - 1)
        def _():
            dk_ref[...] = dk_acc[...].astype(BF16)
            dv_ref[...] = dv_acc[...].astype(BF16)

    full = jax.ShapeDtypeStruct((t, 512), BF16)
    return pl.pallas_call(
        body, name="sb_bwd", grid=(4, nq),
        in_specs=[_spec((SBQ, LANE), lambda hp, i: (i, hp)),
                  _spec((t, LANE), lambda hp, i: (0, 4 + hp)),
                  _spec((t, LANE), lambda hp, i: (0, 8 + hp)),
                  _spec((SBQ, LANE), lambda hp, i: (i, 4 + hp)),
                  _spec((None, SBQ, LANE), lambda hp, i: (hp, i, 0))],
        out_specs=[_spec((SBQ, LANE), lambda hp, i: (i, hp)),
                   _spec((t, LANE), lambda hp, i: (0, hp)), _spec((t, LANE), lambda hp, i: (0, hp))],
        out_shape=[full, full, full],
        scratch_shapes=[pltpu.VMEM((t, LANE), F32), pltpu.VMEM((t, LANE), F32)],
        compiler_params=_params(("parallel", "arbitrary")),
    )(proj_b, proj_b, proj_b, dy, ltot)


def _hgrn_gates(qr, fr, c0, c1):
    mx = jnp.maximum(c0, c1)
    e0, e1 = jnp.exp(c0 - mx), jnp.exp(c1 - mx)
    lb = e1 / (e0 + e1)
    sx = _sigmoid(fr)
    f = lb + (1.0 - lb) * sx
    k = (1.0 - lb) * (1.0 - sx)
    sq = _sigmoid(qr)
    return lb, sx, f, k, sq, qr * sq


def _hgrn_decays(f):
    row = lax.broadcasted_iota(jnp.int32, (CHUNK, CHUNK), 0)
    col = lax.broadcasted_iota(jnp.int32, (CHUNK, CHUNK), 1)
    tril = col <= row
    hi, lo = _split(jnp.log(f))
    trib = _ones_where(tril)
    b = _dg(trib, hi, NN) + _dg(trib, lo, NN)
    bm = b[CHUNK // 2 - 1:CHUNK // 2]
    bl = b[CHUNK - 1:CHUNK]
    return tril, col >= row, b, bm, bl


def _hgrn_fwd(proj_c, small):
    t = proj_c.shape[1]
    nc = t // CHUNK
    nh = D // HD

    def body(p_ref, c0_ref, c1_ref, gam_ref, o_ref, y_ref, sst_ref, st_ref):
        c = pl.program_id(1)

        @pl.when(c == 0)
        def _():
            st_ref[...] = jnp.zeros_like(st_ref)

        st0 = st_ref[...]
        sst_ref[...] = st0
        qr, fr, v, g = p_ref[0], p_ref[1], p_ref[2], p_ref[3]
        lb, sx, f, k, sq, q = _hgrn_gates(qr, fr, c0_ref[...], c1_ref[...])
        tril, _, b, bm, bl = _hgrn_decays(f)
        qd = q * jnp.exp(b)
        qt = q * jnp.exp(b - bm)
        kt = k * jnp.exp(bm - b)
        kl = k * jnp.exp(bl - b)
        vb = v.astype(BF16)
        att = jnp.where(tril, _dot3(qt, kt, NT), 0.0)
        o = _dg(qd.astype(BF16), st0.astype(BF16), NT) + _dg(att.astype(BF16), vb, NN)
        st_ref[...] = st0 * jnp.exp(bl) + _dg(vb, kl.astype(BF16), TN)
        o_ref[...] = o
        r = lax.rsqrt(jnp.mean(o * o, axis=-1, keepdims=True) + RMS_EPS)
        y_ref[...] = (o * r * gam_ref[...] * (g * _sigmoid(g))).astype(BF16)

    return pl.pallas_call(
        body, name="hgrn_fwd", grid=(nh, nc),
        in_specs=[_spec((4, CHUNK, HD), lambda h, c: (0, c, h)),
                  _spec((None, 1, HD), lambda h, c: (R_CLB, 0, h)),
                  _spec((None, 1, HD), lambda h, c: (R_CLB + 1, 0, h)),
                  _spec((None, 1, HD), lambda h, c: (R_GAM, 0, 0))],
        out_specs=[_spec((CHUNK, HD), lambda h, c: (c, h)), _spec((CHUNK, HD), lambda h, c: (c, h)),
                   _spec((None, None, HD, HD), lambda h, c: (c, h, 0, 0))],
        out_shape=[jax.ShapeDtypeStruct((t, D), F32), jax.ShapeDtypeStruct((t, D), BF16),
                   jax.ShapeDtypeStruct((nc, nh, HD, HD), F32)],
        scratch_shapes=[pltpu.VMEM((HD, HD), F32)],
        compiler_params=_params(("parallel", "arbitrary")),
    )(proj_c, small, small, small)


def _hgrn_bwd(proj_c, small, o, sst, dyc):
    t = proj_c.shape[1]
    nc = t // CHUNK
    nh = D // HD

    def body(p_ref, c0_ref, c1_ref, gam_ref, o_ref, sst_ref, dy_ref, dp_ref, dclb_ref, dgam_ref, dst_ref, dlb_acc, dgam_acc):
        head = pl.program_id(0)
        step = pl.program_id(1)

        @pl.when(step == 0)
        def _():
            dst_ref[...] = jnp.zeros_like(dst_ref)
            dlb_acc[...] = jnp.zeros_like(dlb_acc)

        @pl.when((step == 0) & (head == 0))
        def _():
            dgam_acc[...] = jnp.zeros_like(dgam_acc)

        st0 = sst_ref[...]
        dst1 = dst_ref[...]
        qr, fr, v, g = p_ref[0], p_ref[1], p_ref[2], p_ref[3]
        gam = gam_ref[...]
        lb, sx, f, k, sq, q = _hgrn_gates(qr, fr, c0_ref[...], c1_ref[...])
        tril, triu, b, bm, bl = _hgrn_decays(f)
        eb = jnp.exp(b)
        e_qt = jnp.exp(b - bm)
        e_kt = jnp.exp(bm - b)
        e_kl = jnp.exp(bl - b)
        e_bl = jnp.exp(bl)
        qd, qt, kt, kl = q * eb, q * e_qt, k * e_kt, k * e_kl
        ov = o_ref[...]
        r = lax.rsqrt(jnp.mean(ov * ov, axis=-1, keepdims=True) + RMS_EPS)
        oh = ov * r
        sg = _sigmoid(g)
        dy = dy_ref[...]
        dp_ref[3] = (dy * oh * gam * (sg * (1.0 + g * (1.0 - sg)))).astype(BF16)
        dyv = dy * (g * sg)
        dgam_acc[...] += jnp.sum(dyv * oh, axis=0, keepdims=True)
        gdy = dyv * gam
        do = (gdy - oh * jnp.mean(gdy * oh, axis=-1, keepdims=True)) * r
        dob, vb = do.astype(BF16), v.astype(BF16)
        st0b, dst1b = st0.astype(BF16), dst1.astype(BF16)
        st1 = st0 * e_bl + _dg(vb, kl.astype(BF16), TN)
        att = jnp.where(tril, _dot3(qt, kt, NT), 0.0)
        datt = jnp.where(tril, _dg(dob, vb, NT), 0.0)
        dv = _dg(att.astype(BF16), dob, TN) + _dg(kl.astype(BF16), dst1b, NT)
        dq = _dot3(datt, kt, NN) * e_qt + _dg(dob, st0b, NN) * eb
        dk = _dot3(datt, qt, TN) * e_kt + _dg(vb, dst1b, NN) * e_kl
        db = q * dq - k * dk
        last = lax.broadcasted_iota(jnp.int32, (CHUNK, 1), 0) == CHUNK - 1
        db = db + jnp.where(last, jnp.sum(dst1 * st1, axis=0, keepdims=True), 0.0)
        dbh, dbl = _split(db)
        triub = _ones_where(triu)
        dlf = _dg(triub, dbh, NN) + _dg(triub, dbl, NN)
        dst_ref[...] = dst1 * e_bl + _dg(dob, qd.astype(BF16), TN)
        dp_ref[0] = (dq * (sq * (1.0 + qr * (1.0 - sq)))).astype(BF16)
        tmp = dlf / f - dk
        dp_ref[1] = (tmp * (1.0 - lb) * sx * (1.0 - sx)).astype(BF16)
        dp_ref[2] = dv.astype(BF16)
        dlb_acc[...] += jnp.sum((1.0 - sx) * tmp, axis=0, keepdims=True)

        @pl.when(step == nc - 1)
        def _():
            d1 = dlb_acc[...] * lb * (1.0 - lb)
            dclb_ref[...] = jnp.where(lax.broadcasted_iota(jnp.int32, (2, HD), 0) == 0, -d1, d1)

        @pl.when((step == nc - 1) & (head == nh - 1))
        def _():
            dgam_ref[...] = dgam_acc[...]

    rev = lambda h, s: (nc - 1 - s, h)
    return pl.pallas_call(
        body, name="hgrn_bwd", grid=(nh, nc),
        in_specs=[_spec((4, CHUNK, HD), lambda h, s: (0, nc - 1 - s, h)),
                  _spec((None, 1, HD), lambda h, s: (R_CLB, 0, h)),
                  _spec((None, 1, HD), lambda h, s: (R_CLB + 1, 0, h)),
                  _spec((None, 1, HD), lambda h, s: (R_GAM, 0, 0)),
                  _spec((CHUNK, HD), rev),
                  _spec((None, None, HD, HD), lambda h, s: (nc - 1 - s, h, 0, 0)),
                  _spec((CHUNK, HD), rev)],
        out_specs=[_spec((4, CHUNK, HD), lambda h, s: (0, nc - 1 - s, h)),
                   _spec((2, HD), lambda h, s: (0, h)),
                   _spec((1, HD), lambda h, s: (0, 0))],
        out_shape=[jax.ShapeDtypeStruct((4, t, D), BF16), jax.ShapeDtypeStruct((2, D), F32),
                   jax.ShapeDtypeStruct((1, HD), F32)],
        scratch_shapes=[pltpu.VMEM((HD, HD), F32), pltpu.VMEM((1, HD), F32), pltpu.VMEM((1, HD), F32)],
        compiler_params=_params(("arbitrary", "arbitrary")),
    )(proj_c, small, small, small, o, sst, dyc)


def _adamw(name, w, g, m, v):
    rows, cols = w.shape
    br = rows
    for cand in (512, 352, 256):
        if rows % cand == 0:
            br = cand
            break
    c1 = 1.0 - ADAM_B1 ** ADAM_STEP
    c2 = 1.0 - ADAM_B2 ** ADAM_STEP

    def body(w_ref, g_ref, m_ref, v_ref, d_ref, mo_ref, vo_ref):
        gv = g_ref[...]
        mn = ADAM_B1 * m_ref[...] + (1.0 - ADAM_B1) * gv
        vn = ADAM_B2 * v_ref[...] + (1.0 - ADAM_B2) * (gv * gv)
        mo_ref[...] = mn
        vo_ref[...] = vn
        d_ref[...] = -ADAM_LR * ((mn / c1) / (jnp.sqrt(vn / c2) + ADAM_EPS) + ADAM_WD * w_ref[...])

    blk = _spec((br, cols), lambda i: (i, 0))
    shape = jax.ShapeDtypeStruct((rows, cols), F32)
    return pl.pallas_call(
        body, name=name, grid=(rows // br,), in_specs=[blk] * 4, out_specs=[blk] * 3, out_shape=[shape] * 3,
        compiler_params=_params(("parallel",)),
    )(w, g, m, v)


def _place():
    x, y, c = lax.axis_index("x"), lax.axis_index("y"), lax.axis_index("c")
    chips = [(1 - x, y), (x, 1 - y), (1 - x, 1 - y)]
    return x, y, c, chips


def _gather_weights(groups):
    ng = len(groups)

    def body(*refs):
        ins, outs = refs[:ng], refs[ng:2 * ng]
        send, recv, local = refs[2 * ng:]
        x, y, c, chips = _place()
        me = 2 * x + y
        sibling = (x, y, 1 - c)

        def half(gi, chip, hc):
            return outs[gi].at[2 * chip[0] + chip[1], hc]

        def copy(gi, k, src, dst, to):
            return pltpu.make_async_remote_copy(src_ref=src, dst_ref=dst, send_sem=send.at[6 * gi + k],
                                                recv_sem=recv.at[6 * gi + k], device_id=to, device_id_type=MESH)

        mine = [pltpu.make_async_copy(ins[gi], outs[gi].at[me], local.at[gi]) for gi in range(ng)]
        for cp in mine:
            cp.start()
        first = [copy(gi, j, ins[gi].at[c], half(gi, (x, y), c), (*chip, c))
                 for gi in range(ng) for j, chip in enumerate(chips)]
        for cp in first:
            cp.start()
        passed = []
        for gi in range(ng):
            for j, chip in enumerate(chips):
                copy(gi, j, half(gi, chip, c), half(gi, chip, c), sibling).wait_recv()
                fwd = copy(gi, 3 + j, half(gi, chip, c), half(gi, chip, c), sibling)
                fwd.start()
                passed.append(fwd)
        for gi in range(ng):
            for j, chip in enumerate(chips):
                copy(gi, 3 + j, half(gi, chip, 1 - c), half(gi, chip, 1 - c), sibling).wait_recv()
        for cp in first + passed:
            cp.wait_send()
        for cp in mine:
            cp.wait()

    return pl.pallas_call(
        body, name="gather_weights",
        in_specs=[ANY] * ng, out_specs=[ANY] * ng,
        out_shape=[jax.ShapeDtypeStruct((NSH,) + g.shape, g.dtype) for g in groups],
        scratch_shapes=[pltpu.SemaphoreType.DMA((6 * ng,)), pltpu.SemaphoreType.DMA((6 * ng,)),
                        pltpu.SemaphoreType.DMA((ng,))],
        compiler_params=pltpu.CompilerParams(has_side_effects=True),
    )(*groups)


def _swap_halves(groups):
    ng = len(groups)

    def body(*refs):
        ins, outs = refs[:ng], refs[ng:2 * ng]
        send, recv = refs[2 * ng:]
        x, y, c, _ = _place()
        cps = [pltpu.make_async_remote_copy(src_ref=ins[gi].at[1 - c], dst_ref=outs[gi], send_sem=send.at[gi],
                                            recv_sem=recv.at[gi], device_id=(x, y, 1 - c), device_id_type=MESH)
               for gi in range(ng)]
        for cp in cps:
            cp.start()
        for cp in cps:
            cp.wait()

    return pl.pallas_call(
        body, name="grad_swap_halves", in_specs=[ANY] * ng, out_specs=[ANY] * ng,
        out_shape=[jax.ShapeDtypeStruct(g.shape[1:], g.dtype) for g in groups],
        scratch_shapes=[pltpu.SemaphoreType.DMA((ng,)), pltpu.SemaphoreType.DMA((ng,))],
        compiler_params=pltpu.CompilerParams(has_side_effects=True),
    )(*groups)


def _send_to_owners(groups):
    ng = len(groups)

    def body(*refs):
        ins, outs = refs[:ng], refs[ng:2 * ng]
        send, recv = refs[2 * ng:]
        x, y, c, chips = _place()
        cps = [pltpu.make_async_remote_copy(src_ref=ins[gi].at[2 * chip[0] + chip[1]], dst_ref=outs[gi].at[j],
                                            send_sem=send.at[3 * gi + j], recv_sem=recv.at[3 * gi + j],
                                            device_id=(*chip, c), device_id_type=MESH)
               for gi in range(ng) for j, chip in enumerate(chips)]
        for cp in cps:
            cp.start()
        for cp in cps:
            cp.wait()

    return pl.pallas_call(
        body, name="grad_send_to_owners", in_specs=[ANY] * ng, out_specs=[ANY] * ng,
        out_shape=[jax.ShapeDtypeStruct((3,) + g.shape[1:], g.dtype) for g in groups],
        scratch_shapes=[pltpu.SemaphoreType.DMA((3 * ng,)), pltpu.SemaphoreType.DMA((3 * ng,))],
        compiler_params=pltpu.CompilerParams(has_side_effects=True),
    )(*groups)


def _share_halves(groups):
    ng = len(groups)

    def body(*refs):
        ins, outs = refs[:ng], refs[ng:2 * ng]
        send, recv, local = refs[2 * ng:]
        x, y, c, _ = _place()
        mine = [pltpu.make_async_copy(ins[gi], outs[gi].at[c], local.at[gi]) for gi in range(ng)]
        for cp in mine:
            cp.start()
        cps = [pltpu.make_async_remote_copy(src_ref=ins[gi], dst_ref=outs[gi].at[c], send_sem=send.at[gi],
                                            recv_sem=recv.at[gi], device_id=(x, y, 1 - c), device_id_type=MESH)
               for gi in range(ng)]
        for cp in cps:
            cp.start()
        for gi in range(ng):
            pltpu.make_async_remote_copy(src_ref=ins[gi], dst_ref=outs[gi].at[1 - c], send_sem=send.at[gi],
                                         recv_sem=recv.at[gi], device_id=(x, y, 1 - c), device_id_type=MESH).wait_recv()
        for cp in cps:
            cp.wait_send()
        for cp in mine:
            cp.wait()

    return pl.pallas_call(
        body, name="grad_share_halves", in_specs=[ANY] * ng, out_specs=[ANY] * ng,
        out_shape=[jax.ShapeDtypeStruct((2,) + g.shape, g.dtype) for g in groups],
        scratch_shapes=[pltpu.SemaphoreType.DMA((ng,)), pltpu.SemaphoreType.DMA((ng,)), pltpu.SemaphoreType.DMA((ng,))],
        compiler_params=pltpu.CompilerParams(has_side_effects=True),
    )(*groups)


def _pair_sum(name, g, got, c_idx):
    _, nsh, rows, cols = g.shape
    br = rows // 4

    def body(c_ref, a_ref, b_ref, o_ref):
        o_ref[...] = (a_ref[...].astype(F32) + b_ref[...].astype(F32)).astype(BF16)

    return pl.pallas_call(
        body, name=name,
        grid_spec=pltpu.PrefetchScalarGridSpec(
            num_scalar_prefetch=1, grid=(nsh, 4),
            in_specs=[pl.BlockSpec((None, None, br, cols), lambda q, i, cr: (cr[0], q, i, 0)),
                      pl.BlockSpec((None, br, cols), lambda q, i, cr: (q, i, 0))],
            out_specs=pl.BlockSpec((None, br, cols), lambda q, i, cr: (q, i, 0))),
        out_shape=jax.ShapeDtypeStruct((nsh, rows, cols), BF16),
        compiler_params=_params(("parallel", "parallel")),
    )(c_idx, g, got)


def _owner_sum(name, own, got, p_idx):
    _, rows, cols = own.shape
    br = rows // 4

    def body(p_ref, a_ref, b_ref, o_ref):
        o_ref[...] = ((a_ref[...].astype(F32) + b_ref[0].astype(F32)) + b_ref[1].astype(F32)) + b_ref[2].astype(F32)

    return pl.pallas_call(
        body, name=name,
        grid_spec=pltpu.PrefetchScalarGridSpec(
            num_scalar_prefetch=1, grid=(4,),
            in_specs=[pl.BlockSpec((None, br, cols), lambda i, pr: (pr[0], i, 0)),
                      pl.BlockSpec((3, br, cols), lambda i, pr: (0, i, 0))],
            out_specs=pl.BlockSpec((br, cols), lambda i, pr: (i, 0))),
        out_shape=jax.ShapeDtypeStruct((rows, cols), F32),
        compiler_params=_params(("parallel",)),
    )(p_idx, own, got)


def _sum_small(slab):
    def body(in_ref, out_ref, all_ref, send, recv):
        x, y, c, _ = _place()
        me = 4 * x + 2 * y + c
        all_ref[me] = in_ref[...]
        cps = []
        for k in range(1, 8):
            peer = (x ^ (k >> 2), y ^ ((k >> 1) & 1), c ^ (k & 1))
            cps.append(pltpu.make_async_remote_copy(src_ref=in_ref, dst_ref=all_ref.at[me], send_sem=send.at[k - 1],
                                                    recv_sem=recv.at[k - 1], device_id=peer, device_id_type=MESH))
        for cp in cps:
            cp.start()
        for cp in cps:
            cp.wait()
        total = all_ref[0]
        for d in range(1, 8):
            total = total + all_ref[d]
        out_ref[...] = total

    return pl.pallas_call(
        body, name="sum_small",
        in_specs=[pl.BlockSpec(memory_space=pltpu.VMEM)], out_specs=pl.BlockSpec(memory_space=pltpu.VMEM),
        out_shape=jax.ShapeDtypeStruct(slab.shape, F32),
        scratch_shapes=[pltpu.VMEM((8,) + slab.shape, F32), pltpu.SemaphoreType.DMA((7,)), pltpu.SemaphoreType.DMA((7,))],
        compiler_params=pltpu.CompilerParams(has_side_effects=True),
    )(slab)


def _local_step(x, target, wa, wb, wc, conv_w, small):
    t = x.shape[0]
    tm = min(ROW_TILE, t)
    nt = t // tm
    tok_si = _spec((tm, D), lambda s, i: (i, 0))

    def ffn_forward(tag, h, norm_row, gate_idx, up_idx, down_idx):
        hn = _norm_fwd(tag + "_norm", h, small, norm_row)
        g, u, a = _ffn_up(tag + "_up", hn, wa, gate_idx, up_idx)
        return _ffn_down(tag + "_down", a, wb, down_idx, h), (h, hn, g, u, a)

    def out_proj(name, y, blk, h):
        return _mm(name, [(y, _spec((tm, 256), lambda i, k: (i, k)), wb, _spec((None, 256, D), lambda i, k: (k, blk, 0)))],
                   grid=(nt, NSH), o_shape=(t, D), o_dtype=F32, o_spec=_spec((tm, D), lambda i, k: (i, 0)),
                   dims=NN, kaxis=1, nk=NSH, acc_shape=(tm, D), res=(h, _spec((tm, D), lambda i, k: (i, 0))))

    def out_proj_bwd(tag, dhb, y, blk):
        dy = _mm(tag + "_dy", [(dhb, tok_si, wb, _spec((None, 256, D), lambda s, i: (s, blk, 0)))],
                 grid=(NSH, nt), o_shape=(t, D), o_dtype=F32, o_spec=_spec((tm, 256), lambda s, i: (i, s)),
                 dims=NT, kaxis=1, nk=1)
        dw = _wgrad(tag + "_dwout", y, _spec((tm, 256), lambda s, k: (k, s)), dhb, _spec((tm, D), lambda s, k: (k, 0)),
                    256, D, t, tm)
        return dy, dw

    h0 = x
    h1, pre0 = ffn_forward("pre0", h0, R_PRE, 0, 2, 0)
    hn_ab = _norm_fwd("mix0_norm", h1, small, R_MIX)
    proj_a = _mm("ab_proj_a", [(hn_ab, tok_si, wc, _spec((None, D, 768), lambda s, i: (s, 0, 0)))],
                 grid=(2, nt), o_shape=(t, 1536), o_dtype=F32, o_spec=_spec((tm, 768), lambda s, i: (i, s)),
                 dims=NN, kaxis=1, nk=1)
    proj_b = _mm("ab_proj_b", [(hn_ab, tok_si, wc, _spec((None, D, 768), lambda s, i: (s + 2, 0, 0)))],
                 grid=(2, nt), o_shape=(t, 1536), o_dtype=BF16, o_spec=_spec((tm, 768), lambda s, i: (i, s)),
                 dims=NN, kaxis=1, nk=1)
    y_a = _conv_fwd(proj_a, conv_w)
    y_b, ltot = _sb_fwd(proj_b)
    y_ab = jnp.concatenate([y_a, y_b], axis=1)
    h2 = out_proj("ab_out", y_ab, B_ABOUT // 256, h1)
    h3, post0 = ffn_forward("post0", h2, R_POST, 4, 6, 2)
    h4, pre1 = ffn_forward("pre1", h3, R_PRE + 1, 1, 3, 1)
    hn_c = _norm_fwd("mix1_norm", h4, small, R_MIX + 1)
    proj_c = _mm("c_proj", [(hn_c, tok_si, wb, _spec((None, D, D), lambda s, i: (s, B_CIN // D, 0)))],
                 grid=(NSH, nt), o_shape=(NSH, t, D), o_dtype=F32, o_spec=_spec((None, tm, D), lambda s, i: (s, i, 0)),
                 dims=NN, kaxis=1, nk=1)
    o_c, y_c, sst = _hgrn_fwd(proj_c, small)
    h5 = out_proj("c_out", y_c, B_COUT // 256, h4)
    h6, post1 = ffn_forward("post1", h5, R_POST + 1, 5, 7, 3)
    dh, dhb, d_fin, loss = _final_loss(h6, small, target)

    dh, dhb, dg_post1, du_post1, dd_post1, dn_post1 = _ffn_backward("post1", dh, dhb, *post1, wa, wb, 5, 7, 3, small, R_POST + 1)
    dy_c, d_cout = out_proj_bwd("c", dhb, y_c, B_COUT // 256)
    dproj_c, d_clb, d_gam = _hgrn_bwd(proj_c, small, o_c, sst, dy_c)
    d_cin = _wgrad("c_dwin", hn_c, _spec((tm, D), lambda s, k: (k, 0)), dproj_c, _spec((None, tm, D), lambda s, k: (s, k, 0)),
                   D, D, t, tm)
    dhn = _mm("c_dhn", [(dproj_c, _spec((None, tm, D), lambda i, k: (k, i, 0)),
                         wb, _spec((None, D, D), lambda i, k: (k, B_CIN // D, 0)))],
              grid=(nt, NSH), o_shape=(t, D), o_dtype=F32, o_spec=_spec((tm, D), lambda i, k: (i, 0)),
              dims=NT, kaxis=1, nk=NSH, acc_shape=(tm, D))
    dh, dhb, dn_mix1 = _norm_bwd("mix1_norm_bwd", dhn, h4, small, R_MIX + 1, dh)
    dh, dhb, dg_pre1, du_pre1, dd_pre1, dn_pre1 = _ffn_backward("pre1", dh, dhb, *pre1, wa, wb, 1, 3, 1, small, R_PRE + 1)
    dh, dhb, dg_post0, du_post0, dd_post0, dn_post0 = _ffn_backward("post0", dh, dhb, *post0, wa, wb, 4, 6, 2, small, R_POST)
    dy_ab, d_about = out_proj_bwd("ab", dhb, y_ab, B_ABOUT // 256)
    dab, dac, dax, d_conv = _conv_bwd(proj_a, conv_w, dy_ab)
    dq, dk, dv = _sb_bwd(proj_b, dy_ab, ltot)
    dproj_ab = jnp.concatenate([dab, dac, dax, dq, dk, dv], axis=1)
    d_abin = _wgrad("ab_dwin", hn_ab, _spec((tm, D), lambda s, k: (k, 0)), dproj_ab, _spec((tm, 768), lambda s, k: (k, s)),
                    D, 768, t, tm)
    dhn = _mm("ab_dhn", [(dproj_ab, _spec((tm, 768), lambda i, k: (i, k)), wc, _spec((None, D, 768), lambda i, k: (k, 0, 0)))],
              grid=(nt, NSH), o_shape=(t, D), o_dtype=F32, o_spec=_spec((tm, D), lambda i, k: (i, 0)),
              dims=NT, kaxis=1, nk=NSH, acc_shape=(tm, D))
    dh, dhb, dn_mix0 = _norm_bwd("mix0_norm_bwd", dhn, h1, small, R_MIX, dh)
    dh, dhb, dg_pre0, du_pre0, dd_pre0, dn_pre0 = _ffn_backward("pre0", dh, dhb, *pre0, wa, wb, 0, 2, 0, small, R_PRE)

    grads_a = [dg_pre0, dg_pre1, du_pre0, du_pre1, dg_post0, dg_post1, du_post0, du_post1]
    grads_b = [dd_pre0, dd_pre1, dd_post0, dd_post1, d_about, d_cin, d_cout]
    zero = jnp.zeros((1, D), F32)
    conv_rows = jnp.pad(jnp.transpose(d_conv[:, :3, :], (1, 0, 2)).reshape(3, 512), ((0, 0), (0, D - 512)))
    small_grad = jnp.concatenate([
        dn_pre0, dn_pre1, dn_mix0, dn_mix1, dn_post0, dn_post1, d_clb, d_fin,
        jnp.pad(d_gam, ((0, 0), (0, D - HD))), conv_rows,
        jnp.pad(loss, ((0, 0), (0, D - 1))), zero, zero], axis=0)
    return dh, grads_a, grads_b, d_abin, small_grad


def _small_slab(rows):
    parts = [jnp.pad(r.astype(F32), ((0, 0), (0, D - r.shape[1]))) for r in rows]
    slab = jnp.concatenate(parts, axis=0)
    return jnp.pad(slab, ((0, SMALL_ROWS - slab.shape[0]), (0, 0)))


def kernel(x, ffn_pre_norm, ffn_pre_w_gate, ffn_pre_w_up, ffn_pre_w_down, mix_norm, ffn_post_norm, ffn_post_w_gate, ffn_post_w_up, ffn_post_w_down, ab_w_in, ab_conv_w, ab_w_out, c_w_in, c_lower_bounds, c_out_norm, c_w_out, final_norm, loss_target, m_ffn_pre_norm, m_ffn_pre_w_gate, m_ffn_pre_w_up, m_ffn_pre_w_down, m_mix_norm, m_ffn_post_norm, m_ffn_post_w_gate, m_ffn_post_w_up, m_ffn_post_w_down, m_ab_w_in, m_ab_conv_w, m_ab_w_out, m_c_w_in, m_c_lower_bounds, m_c_out_norm, m_c_w_out, m_final_norm, v_ffn_pre_norm, v_ffn_pre_w_gate, v_ffn_pre_w_up, v_ffn_pre_w_down, v_mix_norm, v_ffn_post_norm, v_ffn_post_w_gate, v_ffn_post_w_up, v_ffn_post_w_down, v_ab_w_in, v_ab_conv_w, v_ab_w_out, v_c_w_in, v_c_lower_bounds, v_c_out_norm, v_c_w_out, v_final_norm):
    t = x.shape[1]
    xi, yi, ci = lax.axis_index("x"), lax.axis_index("y"), lax.axis_index("c")
    p_idx = (2 * xi + yi).astype(jnp.int32).reshape(1)
    c_idx = ci.astype(jnp.int32).reshape(1)

    ga = jnp.concatenate([ffn_pre_w_gate, ffn_pre_w_up, ffn_post_w_gate, ffn_post_w_up], axis=0).astype(BF16)
    gb = jnp.concatenate([ffn_pre_w_down.reshape(2 * FS, D), ffn_post_w_down.reshape(2 * FS, D),
                          ab_w_out[0], c_w_in[0], c_w_out[0]], axis=0).astype(BF16)
    gc = ab_w_in[0].astype(BF16)
    gd = jnp.pad(ab_conv_w[0], ((0, 5), (0, 0)))
    wa, wb, wc, conv_w = _gather_weights([ga.reshape(2, 4 * D, FS), gb.reshape(2, B_ROWS // 2, D),
                                          gc.reshape(2, D // 2, 768), jnp.stack([gd, jnp.zeros_like(gd)])])
    wa = wa.reshape(NSH, 8, D, FS)
    wb = wb.reshape(NSH, B_ROWS, D)
    wc = wc.reshape(NSH, D, 768)

    small = _small_slab([ffn_pre_norm, mix_norm, ffn_post_norm, c_lower_bounds, final_norm.reshape(1, D), c_out_norm])
    small = small.reshape(SMALL_ROWS, 1, D)

    grad_x, grads_a, grads_b, d_abin, small_grad = _local_step(x[0], loss_target[0], wa, wb, wc, conv_w, small)

    def halves(g):
        nsh, rows, cols = g.shape
        return jnp.transpose(g.astype(BF16).reshape(nsh, 2, rows // 2, cols), (1, 0, 2, 3))

    full = [halves(jnp.concatenate(grads_a, axis=1)), halves(jnp.concatenate(grads_b, axis=1)), halves(d_abin)]
    got = _swap_halves(full)
    pair = [_pair_sum("grad_pair_sum_%d" % n, g, r, c_idx) for n, (g, r) in enumerate(zip(full, got))]
    got = _send_to_owners(pair)
    mine = [_owner_sum("grad_owner_sum_%d" % n, g, r, p_idx) for n, (g, r) in enumerate(zip(pair, got))]
    ra, rb, rc = _share_halves(mine)
    ra = ra.reshape(8, D, FS)
    rb = rb.reshape(B_ROWS, D)
    rc = rc.reshape(D, 768)
    small_sum = _sum_small(small_grad)

    my_conv = lax.dynamic_slice(small_sum[R_CONV:R_CONV + 3], (0, (2 * xi + yi) * 128), (3, 128))
    grads = {
        "ffn_pre_norm": small_sum[R_PRE:R_PRE + 2], "ffn_pre_w_gate": ra[0:2], "ffn_pre_w_up": ra[2:4],
        "ffn_pre_w_down": rb[0:2 * FS].reshape(2, FS, D), "mix_norm": small_sum[R_MIX:R_MIX + 2],
        "ffn_post_norm": small_sum[R_POST:R_POST + 2], "ffn_post_w_gate": ra[4:6], "ffn_post_w_up": ra[6:8],
        "ffn_post_w_down": rb[2 * FS:4 * FS].reshape(2, FS, D), "ab_w_in": rc.reshape(1, D, 768),
        "ab_conv_w": my_conv.reshape(1, 3, 128), "ab_w_out": rb[B_ABOUT:B_ABOUT + 256].reshape(1, 256, D),
        "c_w_in": rb[B_CIN:B_CIN + D].reshape(1, D, D), "c_lower_bounds": small_sum[R_CLB:R_CLB + 2],
        "c_out_norm": small_sum[R_GAM:R_GAM + 1, :HD], "c_w_out": rb[B_COUT:B_COUT + 256].reshape(1, 256, D),
        "final_norm": small_sum[R_FIN],
    }
    weights = dict(ffn_pre_norm=ffn_pre_norm, ffn_pre_w_gate=ffn_pre_w_gate, ffn_pre_w_up=ffn_pre_w_up, ffn_pre_w_down=ffn_pre_w_down, mix_norm=mix_norm, ffn_post_norm=ffn_post_norm, ffn_post_w_gate=ffn_post_w_gate, ffn_post_w_up=ffn_post_w_up, ffn_post_w_down=ffn_post_w_down, ab_w_in=ab_w_in, ab_conv_w=ab_conv_w, ab_w_out=ab_w_out, c_w_in=c_w_in, c_lower_bounds=c_lower_bounds, c_out_norm=c_out_norm, c_w_out=c_w_out, final_norm=final_norm)
    m_in = dict(ffn_pre_norm=m_ffn_pre_norm, ffn_pre_w_gate=m_ffn_pre_w_gate, ffn_pre_w_up=m_ffn_pre_w_up, ffn_pre_w_down=m_ffn_pre_w_down, mix_norm=m_mix_norm, ffn_post_norm=m_ffn_post_norm, ffn_post_w_gate=m_ffn_post_w_gate, ffn_post_w_up=m_ffn_post_w_up, ffn_post_w_down=m_ffn_post_w_down, ab_w_in=m_ab_w_in, ab_conv_w=m_ab_conv_w, ab_w_out=m_ab_w_out, c_w_in=m_c_w_in, c_lower_bounds=m_c_lower_bounds, c_out_norm=m_c_out_norm, c_w_out=m_c_w_out, final_norm=m_final_norm)
    v_in = dict(ffn_pre_norm=v_ffn_pre_norm, ffn_pre_w_gate=v_ffn_pre_w_gate, ffn_pre_w_up=v_ffn_pre_w_up, ffn_pre_w_down=v_ffn_pre_w_down, mix_norm=v_mix_norm, ffn_post_norm=v_ffn_post_norm, ffn_post_w_gate=v_ffn_post_w_gate, ffn_post_w_up=v_ffn_post_w_up, ffn_post_w_down=v_ffn_post_w_down, ab_w_in=v_ab_w_in, ab_conv_w=v_ab_conv_w, ab_w_out=v_ab_w_out, c_w_in=v_c_w_in, c_lower_bounds=v_c_lower_bounds, c_out_norm=v_c_out_norm, c_w_out=v_c_w_out, final_norm=v_final_norm)
    names = list(weights)
    big = [n for n in names if weights[n].size >= 65536]
    tiny = [n for n in names if n not in big]

    delta, new_m, new_v = {}, {}, {}
    for n in big:
        shape = weights[n].shape
        two_d = (shape[0] * shape[1], shape[2])
        d, m2, v2 = _adamw("adamw_" + n, weights[n].reshape(two_d), grads[n].reshape(two_d),
                           m_in[n].reshape(two_d), v_in[n].reshape(two_d))
        delta[n], new_m[n], new_v[n] = d.reshape(shape), m2.reshape(shape), v2.reshape(shape)

    def tiny_slab(src):
        return _small_slab([src[n].reshape(-1, src[n].shape[-1]) for n in tiny])

    offs, row = {}, 0
    for n in tiny:
        nrows = weights[n].size // weights[n].shape[-1]
        offs[n] = (row, nrows)
        row += nrows
    d, m2, v2 = _adamw("adamw_small", tiny_slab(weights), tiny_slab(grads), tiny_slab(m_in), tiny_slab(v_in))
    for n in tiny:
        r0, nr = offs[n]
        shape = weights[n].shape
        for dst, src in ((delta, d), (new_m, m2), (new_v, v2)):
            dst[n] = src[r0:r0 + nr, :shape[-1]].reshape(shape)

    loss = small_sum[R_LOSS, 0]
    return (loss, grad_x.reshape(1, t, D), *[grads[n] for n in names], *[delta[n] for n in names],
            *[new_m[n] for n in names], *[new_v[n] for n in names])
```

```python
import functools
import math

import jax
import jax.numpy as jnp
from jax import lax
from jax.experimental import pallas as pl
from jax.experimental.pallas import tpu as pltpu

F32 = jnp.float32
BF16 = jnp.bfloat16
MESH = pl.DeviceIdType.MESH
ANY = pl.BlockSpec(memory_space=pl.ANY)

D = 1024
FS = 704
NSH = 4
RMS_EPS = 1e-6
MACARON = 0.5
CHUNK = 64
HD = 128
SBQ = 128
SB_DEAD = -105.0
CONV_HALO = 8
LANE = 128
ROW_TILE = 512
VMEM_LIMIT = 48 * 1024 * 1024

ADAM_LR, ADAM_B1, ADAM_B2, ADAM_EPS, ADAM_WD, ADAM_STEP = 0.001, 0.9, 0.999, 1e-08, 0.01, 10

NN = ((1,), (0,))
NT = ((1,), (1,))
TN = ((0,), (0,))

SMALL_ROWS = 16
R_PRE, R_MIX, R_POST, R_CLB, R_FIN, R_GAM, R_CONV, R_LOSS = 0, 2, 4, 6, 8, 9, 10, 13

B_DOWN = 0
B_ABOUT = 4 * FS
B_CIN = B_ABOUT + 256
B_COUT = B_CIN + 1024
B_ROWS = B_COUT + 256


def _dg(a, b, dims):
    return lax.dot_general(a, b, (dims, ((), ())), preferred_element_type=F32)


def _split(x):
    hi = x.astype(BF16)
    lo = (x - hi.astype(F32)).astype(BF16)
    return hi, lo


def _dot3(a, b, dims):
    ah, al = _split(a)
    bh, bl = _split(b)
    return _dg(ah, bh, dims) + _dg(ah, bl, dims) + _dg(al, bh, dims)


def _sigmoid(x):
    return 1.0 / (1.0 + jnp.exp(-x))


def _params(sem):
    return pltpu.CompilerParams(dimension_semantics=sem, vmem_limit_bytes=VMEM_LIMIT)


def _spec(shape, imap):
    return pl.BlockSpec(shape, imap)


def _mm(name, pairs, *, grid, o_shape, o_dtype, o_spec, dims, kaxis, nk, acc_shape=None, res=None, scale=None):
    npairs = len(pairs)
    operands, specs = [], []
    for a, a_spec, b, b_spec in pairs:
        operands += [a, b]
        specs += [a_spec, b_spec]
    if res is not None:
        operands.append(res[0])
        specs.append(res[1])

    def body(*refs):
        o_ref = refs[2 * npairs + (1 if res is not None else 0)]
        part = None
        for n in range(npairs):
            d = _dg(refs[2 * n][...], refs[2 * n + 1][...], dims)
            part = d if part is None else part + d

        def finish(val):
            if scale is not None:
                val = val * scale
            if res is not None:
                val = val + refs[2 * npairs][...]
            o_ref[...] = val.astype(o_dtype)

        if nk == 1:
            finish(part)
        else:
            acc_ref = refs[-1]
            k = pl.program_id(kaxis)

            @pl.when(k == 0)
            def _():
                acc_ref[...] = part

            @pl.when(k > 0)
            def _():
                acc_ref[...] += part

            @pl.when(k == nk - 1)
            def _():
                finish(acc_ref[...])

    sem = tuple("arbitrary" if (ax == kaxis and nk > 1) else "parallel" for ax in range(len(grid)))
    return pl.pallas_call(
        body, name=name, grid=grid, in_specs=specs, out_specs=o_spec,
        out_shape=jax.ShapeDtypeStruct(o_shape, o_dtype),
        scratch_shapes=[pltpu.VMEM(acc_shape, F32)] if nk > 1 else [],
        compiler_params=_params(sem),
    )(*operands)


def _norm_fwd(name, h, gain_slab, row):
    t = h.shape[0]
    tm = min(ROW_TILE, t)

    def body(h_ref, g_ref, o_ref):
        x = h_ref[...]
        r = lax.rsqrt(jnp.mean(x * x, axis=-1, keepdims=True) + RMS_EPS)
        o_ref[...] = (x * r * g_ref[...]).astype(BF16)

    return pl.pallas_call(
        body, name=name, grid=(t // tm,),
        in_specs=[_spec((tm, D), lambda i: (i, 0)), _spec((None, 1, D), lambda i: (row, 0, 0))],
        out_specs=_spec((tm, D), lambda i: (i, 0)),
        out_shape=jax.ShapeDtypeStruct((t, D), BF16),
        compiler_params=_params(("parallel",)),
    )(h, gain_slab)


def _norm_bwd(name, dhn, h, gain_slab, row, dres):
    t = h.shape[0]
    tm = min(ROW_TILE, t)
    nt = t // tm

    def body(dhn_ref, h_ref, g_ref, dres_ref, dh_ref, dhb_ref, dg_ref, acc_ref):
        i = pl.program_id(0)
        x = h_ref[...]
        r = lax.rsqrt(jnp.mean(x * x, axis=-1, keepdims=True) + RMS_EPS)
        xh = x * r
        dy = dhn_ref[...]
        gdy = dy * g_ref[...]
        dx = (gdy - xh * jnp.mean(gdy * xh, axis=-1, keepdims=True)) * r
        dh = dres_ref[...] + dx
        dh_ref[...] = dh
        dhb_ref[...] = dh.astype(BF16)
        part = jnp.sum((dy * xh).reshape(tm // 8, 8, D), axis=0)

        @pl.when(i == 0)
        def _():
            acc_ref[...] = part

        @pl.when(i > 0)
        def _():
            acc_ref[...] += part

        @pl.when(i == nt - 1)
        def _():
            dg_ref[...] = jnp.sum(acc_ref[...], axis=0, keepdims=True)

    row_spec = _spec((tm, D), lambda i: (i, 0))
    return pl.pallas_call(
        body, name=name, grid=(nt,),
        in_specs=[row_spec, row_spec, _spec((None, 1, D), lambda i: (row, 0, 0)), row_spec],
        out_specs=[row_spec, row_spec, _spec((1, D), lambda i: (0, 0))],
        out_shape=[jax.ShapeDtypeStruct((t, D), F32), jax.ShapeDtypeStruct((t, D), BF16),
                   jax.ShapeDtypeStruct((1, D), F32)],
        scratch_shapes=[pltpu.VMEM((8, D), F32)],
        compiler_params=_params(("arbitrary",)),
    )(dhn, h, gain_slab, dres)


def _final_loss(h, gain_slab, target):
    t = h.shape[0]
    tm = min(ROW_TILE, t)
    nt = t // tm

    def body(h_ref, g_ref, t_ref, dh_ref, dhb_ref, dg_ref, loss_ref, acc_ref, lacc_ref):
        i = pl.program_id(0)
        x = h_ref[...]
        g = g_ref[...]
        r = lax.rsqrt(jnp.mean(x * x, axis=-1, keepdims=True) + RMS_EPS)
        xh = x * r
        err = xh * g - t_ref[...]
        dy = err * (1.0 / D)
        gdy = dy * g
        dh = (gdy - xh * jnp.mean(gdy * xh, axis=-1, keepdims=True)) * r
        dh_ref[...] = dh
        dhb_ref[...] = dh.astype(BF16)
        part = jnp.sum((dy * xh).reshape(tm // 8, 8, D), axis=0)
        lpart = jnp.sum((err * err).reshape(tm // 8, 8, D), axis=0)

        @pl.when(i == 0)
        def _():
            acc_ref[...] = part
            lacc_ref[...] = lpart

        @pl.when(i > 0)
        def _():
            acc_ref[...] += part
            lacc_ref[...] += lpart

        @pl.when(i == nt - 1)
        def _():
            dg_ref[...] = jnp.sum(acc_ref[...], axis=0, keepdims=True)
            rows = jnp.sum(lacc_ref[...], axis=0, keepdims=True)
            loss_ref[...] = jnp.sum(rows, axis=1, keepdims=True) * (0.5 / D)

    row_spec = _spec((tm, D), lambda i: (i, 0))
    return pl.pallas_call(
        body, name="final_loss", grid=(nt,),
        in_specs=[row_spec, _spec((None, 1, D), lambda i: (R_FIN, 0, 0)), row_spec],
        out_specs=[row_spec, row_spec, _spec((1, D), lambda i: (0, 0)), _spec((1, 1), lambda i: (0, 0))],
        out_shape=[jax.ShapeDtypeStruct((t, D), F32), jax.ShapeDtypeStruct((t, D), BF16),
                   jax.ShapeDtypeStruct((1, D), F32), jax.ShapeDtypeStruct((1, 1), F32)],
        scratch_shapes=[pltpu.VMEM((8, D), F32), pltpu.VMEM((8, D), F32)],
        compiler_params=_params(("arbitrary",)),
    )(h, gain_slab, target)


def _ffn_up(name, hn, wa, gate_idx, up_idx):
    t = hn.shape[0]
    tm = min(ROW_TILE, t)

    def body(x_ref, wg_ref, wu_ref, g_ref, u_ref, a_ref):
        x = x_ref[...]
        g = _dg(x, wg_ref[...], NN)
        u = _dg(x, wu_ref[...], NN)
        g_ref[...] = g.astype(BF16)
        u_ref[...] = u.astype(BF16)
        a_ref[...] = (g * _sigmoid(g) * u).astype(BF16)

    act = _spec((None, tm, FS), lambda s, i: (s, i, 0))
    shape = jax.ShapeDtypeStruct((NSH, t, FS), BF16)
    return pl.pallas_call(
        body, name=name, grid=(NSH, t // tm),
        in_specs=[_spec((tm, D), lambda s, i: (i, 0)),
                  _spec((None, None, D, FS), lambda s, i: (s, gate_idx, 0, 0)),
                  _spec((None, None, D, FS), lambda s, i: (s, up_idx, 0, 0))],
        out_specs=[act, act, act], out_shape=[shape, shape, shape],
        compiler_params=_params(("parallel", "parallel")),
    )(hn, wa, wa)


def _ffn_down(name, a, wb, down_idx, h):
    t = h.shape[0]
    tm = min(ROW_TILE, t)
    return _mm(name, [(a, _spec((None, tm, FS), lambda i, k: (k, i, 0)),
                       wb, _spec((None, FS, D), lambda i, k: (k, down_idx, 0)))],
               grid=(t // tm, NSH), o_shape=(t, D), o_dtype=F32, o_spec=_spec((tm, D), lambda i, k: (i, 0)),
               dims=NN, kaxis=1, nk=NSH, acc_shape=(tm, D), res=(h, _spec((tm, D), lambda i, k: (i, 0))),
               scale=MACARON)


def _ffn_bwd_up(name, dhb, wb, down_idx, g, u):
    t = dhb.shape[0]
    tm = min(ROW_TILE, t)

    def body(dh_ref, wd_ref, g_ref, u_ref, dg_ref, du_ref):
        da = _dg(dh_ref[...], wd_ref[...], NT) * MACARON
        gv = g_ref[...].astype(F32)
        uv = u_ref[...].astype(F32)
        sg = _sigmoid(gv)
        du_ref[...] = (da * gv * sg).astype(BF16)
        dg_ref[...] = (da * uv * (sg * (1.0 + gv * (1.0 - sg)))).astype(BF16)

    act = _spec((None, tm, FS), lambda s, i: (s, i, 0))
    shape = jax.ShapeDtypeStruct((NSH, t, FS), BF16)
    return pl.pallas_call(
        body, name=name, grid=(NSH, t // tm),
        in_specs=[_spec((tm, D), lambda s, i: (i, 0)), _spec((None, FS, D), lambda s, i: (s, down_idx, 0)), act, act],
        out_specs=[act, act], out_shape=[shape, shape],
        compiler_params=_params(("parallel", "parallel")),
    )(dhb, wb, g, u)


def _wgrad(name, a, a_spec, b, b_spec, out_rows, out_cols, t, tt, scale=None):
    return _mm(name, [(a, a_spec, b, b_spec)], grid=(NSH, t // tt),
               o_shape=(NSH, out_rows, out_cols), o_dtype=F32,
               o_spec=_spec((None, out_rows, out_cols), lambda s, k: (s, 0, 0)),
               dims=TN, kaxis=1, nk=t // tt, acc_shape=(out_rows, out_cols), scale=scale)


def _ffn_backward(tag, dh, dhb, h_in, hn, g, u, a, wa, wb, gate_idx, up_idx, down_idx, small, norm_row):
    t = dh.shape[0]
    tm = min(ROW_TILE, t)
    tt = tm
    dg, du = _ffn_bwd_up(tag + "_bwd_up", dhb, wb, down_idx, g, u)
    tok = _spec((tt, D), lambda s, k: (k, 0))
    hid = _spec((None, tt, FS), lambda s, k: (s, k, 0))
    d_wd = _wgrad(tag + "_dwd", a, hid, dhb, tok, FS, D, t, tt, scale=MACARON)
    d_wg = _wgrad(tag + "_dwg", hn, tok, dg, hid, D, FS, t, tt)
    d_wu = _wgrad(tag + "_dwu", hn, tok, du, hid, D, FS, t, tt)
    act = _spec((None, tm, FS), lambda i, k: (k, i, 0))
    dhn = _mm(tag + "_dhn",
              [(dg, act, wa, _spec((None, None, D, FS), lambda i, k: (k, gate_idx, 0, 0))),
               (du, act, wa, _spec((None, None, D, FS), lambda i, k: (k, up_idx, 0, 0)))],
              grid=(t // tm, NSH), o_shape=(t, D), o_dtype=F32, o_spec=_spec((tm, D), lambda i, k: (i, 0)),
              dims=NT, kaxis=1, nk=NSH, acc_shape=(tm, D))
    dh_in, dhb_in, d_gain = _norm_bwd(tag + "_norm_bwd", dhn, h_in, small, norm_row, dh)
    return dh_in, dhb_in, d_wg, d_wu, d_wd, d_gain


def _conv_fwd(proj_a, conv_w):
    t = proj_a.shape[0]
    tm = min(ROW_TILE, t)
    hb = tm // CONV_HALO

    def body(ab_ref, ac_ref, ax_ref, acp_ref, axp_ref, w_ref, y_ref):
        i = pl.program_id(1)
        u = ac_ref[...] * ax_ref[...]
        up = jnp.where(i > 0, acp_ref[...] * axp_ref[...], 0.0)
        ext = jnp.concatenate([up, u], axis=0)
        u1 = pltpu.roll(ext, 1, 0)[CONV_HALO:]
        u2 = pltpu.roll(ext, 2, 0)[CONV_HALO:]
        w = w_ref[...]
        conv = w[0:1] * u2 + w[1:2] * u1 + w[2:3] * u
        y_ref[...] = (ab_ref[...] * conv).astype(BF16)

    def cur(off):
        return _spec((tm, LANE), lambda j, i: (i, off + j))

    def prev(off):
        return _spec((CONV_HALO, LANE), lambda j, i: (jnp.maximum(i * hb - 1, 0), off + j))

    return pl.pallas_call(
        body, name="conv_fwd", grid=(4, t // tm),
        in_specs=[cur(0), cur(4), cur(8), prev(4), prev(8),
                  _spec((None, None, 8, LANE), lambda j, i: (j, 0, 0, 0))],
        out_specs=_spec((tm, LANE), lambda j, i: (i, j)),
        out_shape=jax.ShapeDtypeStruct((t, 512), BF16),
        compiler_params=_params(("parallel", "parallel")),
    )(proj_a, proj_a, proj_a, proj_a, proj_a, conv_w)


def _conv_bwd(proj_a, conv_w, dy):
    t = proj_a.shape[0]
    tm = min(ROW_TILE, t)
    hb = tm // CONV_HALO
    nt = t // tm

    def body(ab_ref, ac_ref, ax_ref, dy_ref, acp_ref, axp_ref, abn_ref, dyn_ref, w_ref,
             dab_ref, dac_ref, dax_ref, dw_ref, acc_ref):
        i = pl.program_id(1)
        ab, ac, ax = ab_ref[...], ac_ref[...], ax_ref[...]
        u = ac * ax
        up = jnp.where(i > 0, acp_ref[...] * axp_ref[...], 0.0)
        ext = jnp.concatenate([up, u], axis=0)
        u1 = pltpu.roll(ext, 1, 0)[CONV_HALO:]
        u2 = pltpu.roll(ext, 2, 0)[CONV_HALO:]
        w = w_ref[...]
        conv = w[0:1] * u2 + w[1:2] * u1 + w[2:3] * u
        dy_v = dy_ref[...]
        dab_ref[...] = (dy_v * conv).astype(BF16)
        dc = dy_v * ab
        dcn = jnp.where(i < nt - 1, dyn_ref[...] * abn_ref[...], 0.0)
        extn = jnp.concatenate([dc, dcn], axis=0)
        n = tm + CONV_HALO
        dc1 = pltpu.roll(extn, n - 1, 0)[:tm]
        dc2 = pltpu.roll(extn, n - 2, 0)[:tm]
        du = w[2:3] * dc + w[1:2] * dc1 + w[0:1] * dc2
        dac_ref[...] = (du * ax).astype(BF16)
        dax_ref[...] = (du * ac).astype(BF16)
        rid = lax.broadcasted_iota(jnp.int32, (8, LANE), 0)
        part = jnp.where(rid == 0, jnp.sum(dc * u2, axis=0, keepdims=True),
                         jnp.where(rid == 1, jnp.sum(dc * u1, axis=0, keepdims=True),
                                   jnp.where(rid == 2, jnp.sum(dc * u, axis=0, keepdims=True), 0.0)))

        @pl.when(i == 0)
        def _():
            acc_ref[...] = part

        @pl.when(i > 0)
        def _():
            acc_ref[...] += part

        @pl.when(i == nt - 1)
        def _():
            dw_ref[...] = acc_ref[...]

    def cur(off):
        return _spec((tm, LANE), lambda j, i: (i, off + j))

    def prev(off):
        return _spec((CONV_HALO, LANE), lambda j, i: (jnp.maximum(i * hb - 1, 0), off + j))

    def nxt(off):
        return _spec((CONV_HALO, LANE), lambda j, i: (jnp.minimum((i + 1) * hb, nt * hb - 1), off + j))

    outs = pl.pallas_call(
        body, name="conv_bwd", grid=(4, nt),
        in_specs=[cur(0), cur(4), cur(8), cur(0), prev(4), prev(8), nxt(0), nxt(0),
                  _spec((None, None, 8, LANE), lambda j, i: (j, 0, 0, 0))],
        out_specs=[_spec((tm, LANE), lambda j, i: (i, j)), _spec((tm, LANE), lambda j, i: (i, j)),
                   _spec((tm, LANE), lambda j, i: (i, j)), _spec((None, 8, LANE), lambda j, i: (j, 0, 0))],
        out_shape=[jax.ShapeDtypeStruct((t, 512), BF16), jax.ShapeDtypeStruct((t, 512), BF16),
                   jax.ShapeDtypeStruct((t, 512), BF16), jax.ShapeDtypeStruct((4, 8, LANE), F32)],
        scratch_shapes=[pltpu.VMEM((8, LANE), F32)],
        compiler_params=_params(("parallel", "arbitrary")),
    )(proj_a, proj_a, proj_a, dy, proj_a, proj_a, proj_a, dy, conv_w)
    return outs


def _log_sigmoid(z):
    return jnp.minimum(z, 0.0) - jnp.log(1.0 + jnp.exp(-jnp.abs(z)))


def _sb_masks():
    row = lax.broadcasted_iota(jnp.int32, (SBQ, SBQ), 0)
    col = lax.broadcasted_iota(jnp.int32, (SBQ, SBQ), 1)
    return row, col


def _ones_where(mask):
    return jnp.where(mask, 1.0, 0.0).astype(BF16)


def _head_mask(head):
    lane = lax.broadcasted_iota(jnp.int32, (1, LANE), 1)
    return lane >= 64 if head else lane < 64


def _sb_fwd(proj_b):
    t = proj_b.shape[0]
    nq = t // SBQ
    scale = 1.0 / math.sqrt(64.0)

    def body(q_ref, k_ref, v_ref, y_ref, l_ref, n_ref):
        hp = pl.program_id(0)
        qi = pl.program_id(1)
        row, col = _sb_masks()
        strict = col < row
        m_suffix = _ones_where(row > col)
        q_all = q_ref[...]
        out = jnp.zeros((SBQ, LANE), F32)
        tot = jnp.zeros((SBQ, LANE), F32)
        for head in range(2):
            hm = _head_mask(head)
            qh = jnp.where(hm, q_all, jnp.zeros_like(q_all))

            def block(kb, carry, diag):
                run, acc = carry
                start = pl.multiple_of(kb * SBQ, SBQ)
                kk = k_ref[pl.ds(start, SBQ), :]
                vv = v_ref[pl.ds(start, SBQ), :]
                z = _dg(qh, kk, NT) * scale
                lb = _log_sigmoid(z)
                lk = lb - z
                if diag:
                    lk = jnp.where(strict, lk, 0.0)
                hi, lo = _split(lk)
                later = run + _dg(hi, m_suffix, NN) + _dg(lo, m_suffix, NN)
                w = jnp.exp(lb + later)
                if diag:
                    w = jnp.where(strict, w, 0.0)
                acc = acc + _dg(w.astype(BF16), vv, NN)
                run = run + jnp.sum(hi.astype(F32) + lo.astype(F32), axis=1, keepdims=True)
                return run, acc

            run, acc = block(qi, (jnp.zeros((SBQ, 1), F32), jnp.zeros((SBQ, LANE), F32)), True)

            def live(c):
                return jnp.logical_and(c[0] < qi, jnp.max(c[1]) > SB_DEAD)

            def step(c):
                run, acc = block(qi - 1 - c[0], (c[1], c[2]), False)
                return c[0] + 1, run, acc

            n, run, acc = lax.while_loop(live, step, (jnp.int32(0), run, acc))
            out = jnp.where(hm, acc, out)
            tot = jnp.where(hm, run, tot)
            n_ref[(hp * nq + qi) * 2 + head] = n.astype(F32)
        y_ref[...] = out.astype(BF16)
        l_ref[...] = tot

    return pl.pallas_call(
        body, name="sb_fwd", grid=(4, nq),
        in_specs=[_spec((SBQ, LANE), lambda hp, i: (i, hp)),
                  _spec((t, LANE), lambda hp, i: (0, 4 + hp)),
                  _spec((t, LANE), lambda hp, i: (0, 8 + hp))],
        out_specs=[_spec((SBQ, LANE), lambda hp, i: (i, hp)), _spec((None, SBQ, LANE), lambda hp, i: (hp, i, 0)),
                   pl.BlockSpec(memory_space=pltpu.SMEM)],
        out_shape=[jax.ShapeDtypeStruct((t, 512), BF16), jax.ShapeDtypeStruct((4, t, LANE), F32),
                   jax.ShapeDtypeStruct((8 * nq,), F32)],
        compiler_params=_params(("arbitrary", "arbitrary")),
    )(proj_b, proj_b, proj_b)


def _sb_bwd(proj_b, dy, ltot, nblk):
    t = proj_b.shape[0]
    nq = t // SBQ
    scale = 1.0 / math.sqrt(64.0)

    def body(q_ref, k_ref, v_ref, dy_ref, l_ref, n_ref, dq_ref, dk_ref, dv_ref, dk_acc, dv_acc):
        hp = pl.program_id(0)
        qi = pl.program_id(1)

        @pl.when(qi == 0)
        def _():
            dk_acc[...] = jnp.zeros_like(dk_acc)
            dv_acc[...] = jnp.zeros_like(dv_acc)

        row, col = _sb_masks()
        strict = col < row
        m_prefix = _ones_where(row <= col)
        m_before = _ones_where(row < col)
        q_all = q_ref[...]
        do_all = dy_ref[...].astype(BF16)
        l_all = l_ref[...]
        dq_out = jnp.zeros((SBQ, LANE), F32)
        for head in range(2):
            hm = _head_mask(head)
            qh = jnp.where(hm, q_all, jnp.zeros_like(q_all))
            doh = jnp.where(hm, do_all, jnp.zeros_like(do_all))
            ltot_h = l_all[:, head * 64:head * 64 + 1]

            def block(kb, carry, diag):
                seen, dseen, dq = carry
                start = pl.multiple_of(kb * SBQ, SBQ)
                kk = k_ref[pl.ds(start, SBQ), :]
                vv = v_ref[pl.ds(start, SBQ), :]
                z = _dg(qh, kk, NT) * scale
                lb = _log_sigmoid(z)
                lk = lb - z
                if diag:
                    lk = jnp.where(strict, lk, 0.0)
                hi, lo = _split(lk)
                later = (ltot_h - seen) - (_dg(hi, m_prefix, NN) + _dg(lo, m_prefix, NN))
                w = jnp.exp(lb + later)
                if diag:
                    w = jnp.where(strict, w, 0.0)
                dw = _dg(doh, vv, NT)
                da = w * dw
                dv_acc[pl.ds(start, SBQ), :] += _dg(w.astype(BF16), doh, TN)
                dah, dal = _split(da)
                dlk = dseen + _dg(dah, m_before, NN) + _dg(dal, m_before, NN)
                sig = jnp.exp(lb)
                dz = (da * (1.0 - sig) - dlk * sig) * scale
                if diag:
                    dz = jnp.where(strict, dz, 0.0)
                dzb = dz.astype(BF16)
                dq = dq + _dg(dzb, kk, NN)
                dk_acc[pl.ds(start, SBQ), :] += _dg(dzb, qh, TN)
                seen = seen + jnp.sum(hi.astype(F32) + lo.astype(F32), axis=1, keepdims=True)
                dseen = dseen + jnp.sum(da, axis=1, keepdims=True)
                return seen, dseen, dq

            carry = (jnp.zeros((SBQ, 1), F32), jnp.zeros((SBQ, 1), F32), jnp.zeros((SBQ, LANE), F32))
            first = qi - n_ref[(hp * nq + qi) * 2 + head].astype(jnp.int32)
            carry = lax.fori_loop(first, qi, lambda n, c: block(n, c, False), carry)
            _, _, dq = block(qi, carry, True)
            dq_out = jnp.where(hm, dq, dq_out)
        dq_ref[...] = dq_out.astype(BF16)

        @pl.when(qi == nq - 1)
        def _():
            dk_ref[...] = dk_acc[...].astype(BF16)
            dv_ref[...] = dv_acc[...].astype(BF16)

    full = jax.ShapeDtypeStruct((t, 512), BF16)
    return pl.pallas_call(
        body, name="sb_bwd", grid=(4, nq),
        in_specs=[_spec((SBQ, LANE), lambda hp, i: (i, hp)),
                  _spec((t, LANE), lambda hp, i: (0, 4 + hp)),
                  _spec((t, LANE), lambda hp, i: (0, 8 + hp)),
                  _spec((SBQ, LANE), lambda hp, i: (i, 4 + hp)),
                  _spec((None, SBQ, LANE), lambda hp, i: (hp, i, 0)),
                  pl.BlockSpec(memory_space=pltpu.SMEM)],
        out_specs=[_spec((SBQ, LANE), lambda hp, i: (i, hp)),
                   _spec((t, LANE), lambda hp, i: (0, hp)), _spec((t, LANE), lambda hp, i: (0, hp))],
        out_shape=[full, full, full],
        scratch_shapes=[pltpu.VMEM((t, LANE), F32), pltpu.VMEM((t, LANE), F32)],
        compiler_params=_params(("parallel", "arbitrary")),
    )(proj_b, proj_b, proj_b, dy, ltot, nblk)


def _hgrn_gates(qr, fr, c0, c1):
    mx = jnp.maximum(c0, c1)
    e0, e1 = jnp.exp(c0 - mx), jnp.exp(c1 - mx)
    lb = e1 / (e0 + e1)
    sx = _sigmoid(fr)
    f = lb + (1.0 - lb) * sx
    k = (1.0 - lb) * (1.0 - sx)
    sq = _sigmoid(qr)
    return lb, sx, f, k, sq, qr * sq


def _hgrn_decays(f):
    row = lax.broadcasted_iota(jnp.int32, (CHUNK, CHUNK), 0)
    col = lax.broadcasted_iota(jnp.int32, (CHUNK, CHUNK), 1)
    tril = col <= row
    hi, lo = _split(jnp.log(f))
    trib = _ones_where(tril)
    b = _dg(trib, hi, NN) + _dg(trib, lo, NN)
    bm = b[CHUNK // 2 - 1:CHUNK // 2]
    bl = b[CHUNK - 1:CHUNK]
    return tril, col >= row, b, bm, bl


def _hgrn_fwd(proj_c, small):
    t = proj_c.shape[1]
    nc = t // CHUNK
    nh = D // HD

    def body(p_ref, c0_ref, c1_ref, gam_ref, o_ref, y_ref, sst_ref, st_ref):
        c = pl.program_id(1)

        @pl.when(c == 0)
        def _():
            st_ref[...] = jnp.zeros_like(st_ref)

        st0 = st_ref[...]
        sst_ref[...] = st0
        qr, fr, v, g = p_ref[0], p_ref[1], p_ref[2], p_ref[3]
        lb, sx, f, k, sq, q = _hgrn_gates(qr, fr, c0_ref[...], c1_ref[...])
        tril, _, b, bm, bl = _hgrn_decays(f)
        qd = q * jnp.exp(b)
        qt = q * jnp.exp(b - bm)
        kt = k * jnp.exp(bm - b)
        kl = k * jnp.exp(bl - b)
        vb = v.astype(BF16)
        att = jnp.where(tril, _dot3(qt, kt, NT), 0.0)
        o = _dg(qd.astype(BF16), st0.astype(BF16), NT) + _dg(att.astype(BF16), vb, NN)
        st_ref[...] = st0 * jnp.exp(bl) + _dg(vb, kl.astype(BF16), TN)
        o_ref[...] = o
        r = lax.rsqrt(jnp.mean(o * o, axis=-1, keepdims=True) + RMS_EPS)
        y_ref[...] = (o * r * gam_ref[...] * (g * _sigmoid(g))).astype(BF16)

    return pl.pallas_call(
        body, name="hgrn_fwd", grid=(nh, nc),
        in_specs=[_spec((4, CHUNK, HD), lambda h, c: (0, c, h)),
                  _spec((None, 1, HD), lambda h, c: (R_CLB, 0, h)),
                  _spec((None, 1, HD), lambda h, c: (R_CLB + 1, 0, h)),
                  _spec((None, 1, HD), lambda h, c: (R_GAM, 0, 0))],
        out_specs=[_spec((CHUNK, HD), lambda h, c: (c, h)), _spec((CHUNK, HD), lambda h, c: (c, h)),
                   _spec((None, None, HD, HD), lambda h, c: (c, h, 0, 0))],
        out_shape=[jax.ShapeDtypeStruct((t, D), F32), jax.ShapeDtypeStruct((t, D), BF16),
                   jax.ShapeDtypeStruct((nc, nh, HD, HD), F32)],
        scratch_shapes=[pltpu.VMEM((HD, HD), F32)],
        compiler_params=_params(("parallel", "arbitrary")),
    )(proj_c, small, small, small)


def _hgrn_bwd(proj_c, small, o, sst, dyc):
    t = proj_c.shape[1]
    nc = t // CHUNK
    nh = D // HD

    def body(p_ref, c0_ref, c1_ref, gam_ref, o_ref, sst_ref, dy_ref, dp_ref, dclb_ref, dgam_ref, dst_ref, dlb_acc, dgam_acc):
        head = pl.program_id(0)
        step = pl.program_id(1)

        @pl.when(step == 0)
        def _():
            dst_ref[...] = jnp.zeros_like(dst_ref)
            dlb_acc[...] = jnp.zeros_like(dlb_acc)

        @pl.when((step == 0) & (head == 0))
        def _():
            dgam_acc[...] = jnp.zeros_like(dgam_acc)

        st0 = sst_ref[...]
        dst1 = dst_ref[...]
        qr, fr, v, g = p_ref[0], p_ref[1], p_ref[2], p_ref[3]
        gam = gam_ref[...]
        lb, sx, f, k, sq, q = _hgrn_gates(qr, fr, c0_ref[...], c1_ref[...])
        tril, triu, b, bm, bl = _hgrn_decays(f)
        eb = jnp.exp(b)
        e_qt = jnp.exp(b - bm)
        e_kt = jnp.exp(bm - b)
        e_kl = jnp.exp(bl - b)
        e_bl = jnp.exp(bl)
        qd, qt, kt, kl = q * eb, q * e_qt, k * e_kt, k * e_kl
        ov = o_ref[...]
        r = lax.rsqrt(jnp.mean(ov * ov, axis=-1, keepdims=True) + RMS_EPS)
        oh = ov * r
        sg = _sigmoid(g)
        dy = dy_ref[...]
        dp_ref[3] = (dy * oh * gam * (sg * (1.0 + g * (1.0 - sg)))).astype(BF16)
        dyv = dy * (g * sg)
        dgam_acc[...] += jnp.sum(dyv * oh, axis=0, keepdims=True)
        gdy = dyv * gam
        do = (gdy - oh * jnp.mean(gdy * oh, axis=-1, keepdims=True)) * r
        dob, vb = do.astype(BF16), v.astype(BF16)
        st0b, dst1b = st0.astype(BF16), dst1.astype(BF16)
        st1 = st0 * e_bl + _dg(vb, kl.astype(BF16), TN)
        att = jnp.where(tril, _dot3(qt, kt, NT), 0.0)
        datt = jnp.where(tril, _dg(dob, vb, NT), 0.0)
        dv = _dg(att.astype(BF16), dob, TN) + _dg(kl.astype(BF16), dst1b, NT)
        dq = _dot3(datt, kt, NN) * e_qt + _dg(dob, st0b, NN) * eb
        dk = _dot3(datt, qt, TN) * e_kt + _dg(vb, dst1b, NN) * e_kl
        db = q * dq - k * dk
        last = lax.broadcasted_iota(jnp.int32, (CHUNK, 1), 0) == CHUNK - 1
        db = db + jnp.where(last, jnp.sum(dst1 * st1, axis=0, keepdims=True), 0.0)
        dbh, dbl = _split(db)
        triub = _ones_where(triu)
        dlf = _dg(triub, dbh, NN) + _dg(triub, dbl, NN)
        dst_ref[...] = dst1 * e_bl + _dg(dob, qd.astype(BF16), TN)
        dp_ref[0] = (dq * (sq * (1.0 + qr * (1.0 - sq)))).astype(BF16)
        tmp = dlf / f - dk
        dp_ref[1] = (tmp * (1.0 - lb) * sx * (1.0 - sx)).astype(BF16)
        dp_ref[2] = dv.astype(BF16)
        dlb_acc[...] += jnp.sum((1.0 - sx) * tmp, axis=0, keepdims=True)

        @pl.when(step == nc - 1)
        def _():
            d1 = dlb_acc[...] * lb * (1.0 - lb)
            dclb_ref[...] = jnp.where(lax.broadcasted_iota(jnp.int32, (2, HD), 0) == 0, -d1, d1)

        @pl.when((step == nc - 1) & (head == nh - 1))
        def _():
            dgam_ref[...] = dgam_acc[...]

    rev = lambda h, s: (nc - 1 - s, h)
    return pl.pallas_call(
        body, name="hgrn_bwd", grid=(nh, nc),
        in_specs=[_spec((4, CHUNK, HD), lambda h, s: (0, nc - 1 - s, h)),
                  _spec((None, 1, HD), lambda h, s: (R_CLB, 0, h)),
                  _spec((None, 1, HD), lambda h, s: (R_CLB + 1, 0, h)),
                  _spec((None, 1, HD), lambda h, s: (R_GAM, 0, 0)),
                  _spec((CHUNK, HD), rev),
                  _spec((None, None, HD, HD), lambda h, s: (nc - 1 - s, h, 0, 0)),
                  _spec((CHUNK, HD), rev)],
        out_specs=[_spec((4, CHUNK, HD), lambda h, s: (0, nc - 1 - s, h)),
                   _spec((2, HD), lambda h, s: (0, h)),
                   _spec((1, HD), lambda h, s: (0, 0))],
        out_shape=[jax.ShapeDtypeStruct((4, t, D), BF16), jax.ShapeDtypeStruct((2, D), F32),
                   jax.ShapeDtypeStruct((1, HD), F32)],
        scratch_shapes=[pltpu.VMEM((HD, HD), F32), pltpu.VMEM((1, HD), F32), pltpu.VMEM((1, HD), F32)],
        compiler_params=_params(("arbitrary", "arbitrary")),
    )(proj_c, small, small, small, o, sst, dyc)


def _adamw(name, w, g, m, v):
    rows, cols = w.shape
    br = rows
    for cand in (512, 352, 256):
        if rows % cand == 0:
            br = cand
            break
    c1 = 1.0 - ADAM_B1 ** ADAM_STEP
    c2 = 1.0 - ADAM_B2 ** ADAM_STEP

    def body(w_ref, g_ref, m_ref, v_ref, d_ref, mo_ref, vo_ref):
        gv = g_ref[...]
        mn = ADAM_B1 * m_ref[...] + (1.0 - ADAM_B1) * gv
        vn = ADAM_B2 * v_ref[...] + (1.0 - ADAM_B2) * (gv * gv)
        mo_ref[...] = mn
        vo_ref[...] = vn
        d_ref[...] = -ADAM_LR * ((mn / c1) / (jnp.sqrt(vn / c2) + ADAM_EPS) + ADAM_WD * w_ref[...])

    blk = _spec((br, cols), lambda i: (i, 0))
    shape = jax.ShapeDtypeStruct((rows, cols), F32)
    return pl.pallas_call(
        body, name=name, grid=(rows // br,), in_specs=[blk] * 4, out_specs=[blk] * 3, out_shape=[shape] * 3,
        compiler_params=_params(("parallel",)),
    )(w, g, m, v)


def _place():
    x, y, c = lax.axis_index("x"), lax.axis_index("y"), lax.axis_index("c")
    chips = [(1 - x, y), (x, 1 - y), (1 - x, 1 - y)]
    return x, y, c, chips


def _gather_weights(groups):
    ng = len(groups)

    def body(*refs):
        ins, outs = refs[:ng], refs[ng:2 * ng]
        send, recv, local = refs[2 * ng:]
        x, y, c, chips = _place()
        me = 2 * x + y
        sibling = (x, y, 1 - c)

        def half(gi, chip, hc):
            return outs[gi].at[2 * chip[0] + chip[1], hc]

        def copy(gi, k, src, dst, to):
            return pltpu.make_async_remote_copy(src_ref=src, dst_ref=dst, send_sem=send.at[6 * gi + k],
                                                recv_sem=recv.at[6 * gi + k], device_id=to, device_id_type=MESH)

        mine = [pltpu.make_async_copy(ins[gi], outs[gi].at[me], local.at[gi]) for gi in range(ng)]
        for cp in mine:
            cp.start()
        first = [copy(gi, j, ins[gi].at[c], half(gi, (x, y), c), (*chip, c))
                 for gi in range(ng) for j, chip in enumerate(chips)]
        for cp in first:
            cp.start()
        passed = []
        for gi in range(ng):
            for j, chip in enumerate(chips):
                copy(gi, j, half(gi, chip, c), half(gi, chip, c), sibling).wait_recv()
                fwd = copy(gi, 3 + j, half(gi, chip, c), half(gi, chip, c), sibling)
                fwd.start()
                passed.append(fwd)
        for gi in range(ng):
            for j, chip in enumerate(chips):
                copy(gi, 3 + j, half(gi, chip, 1 - c), half(gi, chip, 1 - c), sibling).wait_recv()
        for cp in first + passed:
            cp.wait_send()
        for cp in mine:
            cp.wait()

    return pl.pallas_call(
        body, name="gather_weights",
        in_specs=[ANY] * ng, out_specs=[ANY] * ng,
        out_shape=[jax.ShapeDtypeStruct((NSH,) + g.shape, g.dtype) for g in groups],
        scratch_shapes=[pltpu.SemaphoreType.DMA((6 * ng,)), pltpu.SemaphoreType.DMA((6 * ng,)),
                        pltpu.SemaphoreType.DMA((ng,))],
        compiler_params=pltpu.CompilerParams(has_side_effects=True),
    )(*groups)


def _swap_halves(groups):
    ng = len(groups)

    def body(*refs):
        ins, outs = refs[:ng], refs[ng:2 * ng]
        send, recv = refs[2 * ng:]
        x, y, c, _ = _place()
        cps = [pltpu.make_async_remote_copy(src_ref=ins[gi].at[1 - c], dst_ref=outs[gi], send_sem=send.at[gi],
                                            recv_sem=recv.at[gi], device_id=(x, y, 1 - c), device_id_type=MESH)
               for gi in range(ng)]
        for cp in cps:
            cp.start()
        for cp in cps:
            cp.wait()

    return pl.pallas_call(
        body, name="grad_swap_halves", in_specs=[ANY] * ng, out_specs=[ANY] * ng,
        out_shape=[jax.ShapeDtypeStruct(g.shape[1:], g.dtype) for g in groups],
        scratch_shapes=[pltpu.SemaphoreType.DMA((ng,)), pltpu.SemaphoreType.DMA((ng,))],
        compiler_params=pltpu.CompilerParams(has_side_effects=True),
    )(*groups)


def _send_to_owners(groups):
    ng = len(groups)

    def body(*refs):
        ins, outs = refs[:ng], refs[ng:2 * ng]
        send, recv = refs[2 * ng:]
        x, y, c, chips = _place()
        cps = [pltpu.make_async_remote_copy(src_ref=ins[gi].at[2 * chip[0] + chip[1]], dst_ref=outs[gi].at[j],
                                            send_sem=send.at[3 * gi + j], recv_sem=recv.at[3 * gi + j],
                                            device_id=(*chip, c), device_id_type=MESH)
               for gi in range(ng) for j, chip in enumerate(chips)]
        for cp in cps:
            cp.start()
        for cp in cps:
            cp.wait()

    return pl.pallas_call(
        body, name="grad_send_to_owners", in_specs=[ANY] * ng, out_specs=[ANY] * ng,
        out_shape=[jax.ShapeDtypeStruct((3,) + g.shape[1:], g.dtype) for g in groups],
        scratch_shapes=[pltpu.SemaphoreType.DMA((3 * ng,)), pltpu.SemaphoreType.DMA((3 * ng,))],
        compiler_params=pltpu.CompilerParams(has_side_effects=True),
    )(*groups)


def _share_halves(groups):
    ng = len(groups)

    def body(*refs):
        ins, outs = refs[:ng], refs[ng:2 * ng]
        send, recv, local = refs[2 * ng:]
        x, y, c, _ = _place()
        mine = [pltpu.make_async_copy(ins[gi], outs[gi].at[c], local.at[gi]) for gi in range(ng)]
        for cp in mine:
            cp.start()
        cps = [pltpu.make_async_remote_copy(src_ref=ins[gi], dst_ref=outs[gi].at[c], send_sem=send.at[gi],
                                            recv_sem=recv.at[gi], device_id=(x, y, 1 - c), device_id_type=MESH)
               for gi in range(ng)]
        for cp in cps:
            cp.start()
        for gi in range(ng):
            pltpu.make_async_remote_copy(src_ref=ins[gi], dst_ref=outs[gi].at[1 - c], send_sem=send.at[gi],
                                         recv_sem=recv.at[gi], device_id=(x, y, 1 - c), device_id_type=MESH).wait_recv()
        for cp in cps:
            cp.wait_send()
        for cp in mine:
            cp.wait()

    return pl.pallas_call(
        body, name="grad_share_halves", in_specs=[ANY] * ng, out_specs=[ANY] * ng,
        out_shape=[jax.ShapeDtypeStruct((2,) + g.shape, g.dtype) for g in groups],
        scratch_shapes=[pltpu.SemaphoreType.DMA((ng,)), pltpu.SemaphoreType.DMA((ng,)), pltpu.SemaphoreType.DMA((ng,))],
        compiler_params=pltpu.CompilerParams(has_side_effects=True),
    )(*groups)


def _pair_sum(name, g, got, c_idx):
    _, nsh, rows, cols = g.shape
    br = rows // 4

    def body(c_ref, a_ref, b_ref, o_ref):
        o_ref[...] = (a_ref[...].astype(F32) + b_ref[...].astype(F32)).astype(BF16)

    return pl.pallas_call(
        body, name=name,
        grid_spec=pltpu.PrefetchScalarGridSpec(
            num_scalar_prefetch=1, grid=(nsh, 4),
            in_specs=[pl.BlockSpec((None, None, br, cols), lambda q, i, cr: (cr[0], q, i, 0)),
                      pl.BlockSpec((None, br, cols), lambda q, i, cr: (q, i, 0))],
            out_specs=pl.BlockSpec((None, br, cols), lambda q, i, cr: (q, i, 0))),
        out_shape=jax.ShapeDtypeStruct((nsh, rows, cols), BF16),
        compiler_params=_params(("parallel", "parallel")),
    )(c_idx, g, got)


def _owner_sum(name, own, got, p_idx):
    _, rows, cols = own.shape
    br = rows // 4

    def body(p_ref, a_ref, b_ref, o_ref):
        o_ref[...] = ((a_ref[...].astype(F32) + b_ref[0].astype(F32)) + b_ref[1].astype(F32)) + b_ref[2].astype(F32)

    return pl.pallas_call(
        body, name=name,
        grid_spec=pltpu.PrefetchScalarGridSpec(
            num_scalar_prefetch=1, grid=(4,),
            in_specs=[pl.BlockSpec((None, br, cols), lambda i, pr: (pr[0], i, 0)),
                      pl.BlockSpec((3, br, cols), lambda i, pr: (0, i, 0))],
            out_specs=pl.BlockSpec((br, cols), lambda i, pr: (i, 0))),
        out_shape=jax.ShapeDtypeStruct((rows, cols), F32),
        compiler_params=_params(("parallel",)),
    )(p_idx, own, got)


def _sum_small(slab):
    def body(in_ref, out_ref, all_ref, send, recv):
        x, y, c, _ = _place()
        me = 4 * x + 2 * y + c
        all_ref[me] = in_ref[...]
        cps = []
        for k in range(1, 8):
            peer = (x ^ (k >> 2), y ^ ((k >> 1) & 1), c ^ (k & 1))
            cps.append(pltpu.make_async_remote_copy(src_ref=in_ref, dst_ref=all_ref.at[me], send_sem=send.at[k - 1],
                                                    recv_sem=recv.at[k - 1], device_id=peer, device_id_type=MESH))
        for cp in cps:
            cp.start()
        for cp in cps:
            cp.wait()
        total = all_ref[0]
        for d in range(1, 8):
            total = total + all_ref[d]
        out_ref[...] = total

    return pl.pallas_call(
        body, name="sum_small",
        in_specs=[pl.BlockSpec(memory_space=pltpu.VMEM)], out_specs=pl.BlockSpec(memory_space=pltpu.VMEM),
        out_shape=jax.ShapeDtypeStruct(slab.shape, F32),
        scratch_shapes=[pltpu.VMEM((8,) + slab.shape, F32), pltpu.SemaphoreType.DMA((7,)), pltpu.SemaphoreType.DMA((7,))],
        compiler_params=pltpu.CompilerParams(has_side_effects=True),
    )(slab)


def _local_step(x, target, wa, wb, wc, conv_w, small):
    t = x.shape[0]
    tm = min(ROW_TILE, t)
    nt = t // tm
    tok_si = _spec((tm, D), lambda s, i: (i, 0))

    def ffn_forward(tag, h, norm_row, gate_idx, up_idx, down_idx):
        hn = _norm_fwd(tag + "_norm", h, small, norm_row)
        g, u, a = _ffn_up(tag + "_up", hn, wa, gate_idx, up_idx)
        return _ffn_down(tag + "_down", a, wb, down_idx, h), (h, hn, g, u, a)

    def out_proj(name, y, blk, h):
        return _mm(name, [(y, _spec((tm, 256), lambda i, k: (i, k)), wb, _spec((None, 256, D), lambda i, k: (k, blk, 0)))],
                   grid=(nt, NSH), o_shape=(t, D), o_dtype=F32, o_spec=_spec((tm, D), lambda i, k: (i, 0)),
                   dims=NN, kaxis=1, nk=NSH, acc_shape=(tm, D), res=(h, _spec((tm, D), lambda i, k: (i, 0))))

    def out_proj_bwd(tag, dhb, y, blk):
        dy = _mm(tag + "_dy", [(dhb, tok_si, wb, _spec((None, 256, D), lambda s, i: (s, blk, 0)))],
                 grid=(NSH, nt), o_shape=(t, D), o_dtype=F32, o_spec=_spec((tm, 256), lambda s, i: (i, s)),
                 dims=NT, kaxis=1, nk=1)
        dw = _wgrad(tag + "_dwout", y, _spec((tm, 256), lambda s, k: (k, s)), dhb, _spec((tm, D), lambda s, k: (k, 0)),
                    256, D, t, tm)
        return dy, dw

    h0 = x
    h1, pre0 = ffn_forward("pre0", h0, R_PRE, 0, 2, 0)
    hn_ab = _norm_fwd("mix0_norm", h1, small, R_MIX)
    proj_a = _mm("ab_proj_a", [(hn_ab, tok_si, wc, _spec((None, D, 768), lambda s, i: (s, 0, 0)))],
                 grid=(2, nt), o_shape=(t, 1536), o_dtype=F32, o_spec=_spec((tm, 768), lambda s, i: (i, s)),
                 dims=NN, kaxis=1, nk=1)
    proj_b = _mm("ab_proj_b", [(hn_ab, tok_si, wc, _spec((None, D, 768), lambda s, i: (s + 2, 0, 0)))],
                 grid=(2, nt), o_shape=(t, 1536), o_dtype=BF16, o_spec=_spec((tm, 768), lambda s, i: (i, s)),
                 dims=NN, kaxis=1, nk=1)
    y_a = _conv_fwd(proj_a, conv_w)
    y_b, ltot, nblk = _sb_fwd(proj_b)
    y_ab = jnp.concatenate([y_a, y_b], axis=1)
    h2 = out_proj("ab_out", y_ab, B_ABOUT // 256, h1)
    h3, post0 = ffn_forward("post0", h2, R_POST, 4, 6, 2)
    h4, pre1 = ffn_forward("pre1", h3, R_PRE + 1, 1, 3, 1)
    hn_c = _norm_fwd("mix1_norm", h4, small, R_MIX + 1)
    proj_c = _mm("c_proj", [(hn_c, tok_si, wb, _spec((None, D, D), lambda s, i: (s, B_CIN // D, 0)))],
                 grid=(NSH, nt), o_shape=(NSH, t, D), o_dtype=F32, o_spec=_spec((None, tm, D), lambda s, i: (s, i, 0)),
                 dims=NN, kaxis=1, nk=1)
    o_c, y_c, sst = _hgrn_fwd(proj_c, small)
    h5 = out_proj("c_out", y_c, B_COUT // 256, h4)
    h6, post1 = ffn_forward("post1", h5, R_POST + 1, 5, 7, 3)
    dh, dhb, d_fin, loss = _final_loss(h6, small, target)

    dh, dhb, dg_post1, du_post1, dd_post1, dn_post1 = _ffn_backward("post1", dh, dhb, *post1, wa, wb, 5, 7, 3, small, R_POST + 1)
    dy_c, d_cout = out_proj_bwd("c", dhb, y_c, B_COUT // 256)
    dproj_c, d_clb, d_gam = _hgrn_bwd(proj_c, small, o_c, sst, dy_c)
    d_cin = _wgrad("c_dwin", hn_c, _spec((tm, D), lambda s, k: (k, 0)), dproj_c, _spec((None, tm, D), lambda s, k: (s, k, 0)),
                   D, D, t, tm)
    dhn = _mm("c_dhn", [(dproj_c, _spec((None, tm, D), lambda i, k: (k, i, 0)),
                         wb, _spec((None, D, D), lambda i, k: (k, B_CIN // D, 0)))],
              grid=(nt, NSH), o_shape=(t, D), o_dtype=F32, o_spec=_spec((tm, D), lambda i, k: (i, 0)),
              dims=NT, kaxis=1, nk=NSH, acc_shape=(tm, D))
    dh, dhb, dn_mix1 = _norm_bwd("mix1_norm_bwd", dhn, h4, small, R_MIX + 1, dh)
    dh, dhb, dg_pre1, du_pre1, dd_pre1, dn_pre1 = _ffn_backward("pre1", dh, dhb, *pre1, wa, wb, 1, 3, 1, small, R_PRE + 1)
    dh, dhb, dg_post0, du_post0, dd_post0, dn_post0 = _ffn_backward("post0", dh, dhb, *post0, wa, wb, 4, 6, 2, small, R_POST)
    dy_ab, d_about = out_proj_bwd("ab", dhb, y_ab, B_ABOUT // 256)
    dab, dac, dax, d_conv = _conv_bwd(proj_a, conv_w, dy_ab)
    dq, dk, dv = _sb_bwd(proj_b, dy_ab, ltot, nblk)
    dproj_ab = jnp.concatenate([dab, dac, dax, dq, dk, dv], axis=1)
    d_abin = _wgrad("ab_dwin", hn_ab, _spec((tm, D), lambda s, k: (k, 0)), dproj_ab, _spec((tm, 768), lambda s, k: (k, s)),
                    D, 768, t, tm)
    dhn = _mm("ab_dhn", [(dproj_ab, _spec((tm, 768), lambda i, k: (i, k)), wc, _spec((None, D, 768), lambda i, k: (k, 0, 0)))],
              grid=(nt, NSH), o_shape=(t, D), o_dtype=F32, o_spec=_spec((tm, D), lambda i, k: (i, 0)),
              dims=NT, kaxis=1, nk=NSH, acc_shape=(tm, D))
    dh, dhb, dn_mix0 = _norm_bwd("mix0_norm_bwd", dhn, h1, small, R_MIX, dh)
    dh, dhb, dg_pre0, du_pre0, dd_pre0, dn_pre0 = _ffn_backward("pre0", dh, dhb, *pre0, wa, wb, 0, 2, 0, small, R_PRE)

    grads_a = [dg_pre0, dg_pre1, du_pre0, du_pre1, dg_post0, dg_post1, du_post0, du_post1]
    grads_b = [dd_pre0, dd_pre1, dd_post0, dd_post1, d_about, d_cin, d_cout]
    zero = jnp.zeros((1, D), F32)
    conv_rows = jnp.pad(jnp.transpose(d_conv[:, :3, :], (1, 0, 2)).reshape(3, 512), ((0, 0), (0, D - 512)))
    small_grad = jnp.concatenate([
        dn_pre0, dn_pre1, dn_mix0, dn_mix1, dn_post0, dn_post1, d_clb, d_fin,
        jnp.pad(d_gam, ((0, 0), (0, D - HD))), conv_rows,
        jnp.pad(loss, ((0, 0), (0, D - 1))), zero, zero], axis=0)
    return dh, grads_a, grads_b, d_abin, small_grad


def _small_slab(rows):
    parts = [jnp.pad(r.astype(F32), ((0, 0), (0, D - r.shape[1]))) for r in rows]
    slab = jnp.concatenate(parts, axis=0)
    return jnp.pad(slab, ((0, SMALL_ROWS - slab.shape[0]), (0, 0)))


def kernel(x, ffn_pre_norm, ffn_pre_w_gate, ffn_pre_w_up, ffn_pre_w_down, mix_norm, ffn_post_norm, ffn_post_w_gate, ffn_post_w_up, ffn_post_w_down, ab_w_in, ab_conv_w, ab_w_out, c_w_in, c_lower_bounds, c_out_norm, c_w_out, final_norm, loss_target, m_ffn_pre_norm, m_ffn_pre_w_gate, m_ffn_pre_w_up, m_ffn_pre_w_down, m_mix_norm, m_ffn_post_norm, m_ffn_post_w_gate, m_ffn_post_w_up, m_ffn_post_w_down, m_ab_w_in, m_ab_conv_w, m_ab_w_out, m_c_w_in, m_c_lower_bounds, m_c_out_norm, m_c_w_out, m_final_norm, v_ffn_pre_norm, v_ffn_pre_w_gate, v_ffn_pre_w_up, v_ffn_pre_w_down, v_mix_norm, v_ffn_post_norm, v_ffn_post_w_gate, v_ffn_post_w_up, v_ffn_post_w_down, v_ab_w_in, v_ab_conv_w, v_ab_w_out, v_c_w_in, v_c_lower_bounds, v_c_out_norm, v_c_w_out, v_final_norm):
    t = x.shape[1]
    xi, yi, ci = lax.axis_index("x"), lax.axis_index("y"), lax.axis_index("c")
    p_idx = (2 * xi + yi).astype(jnp.int32).reshape(1)
    c_idx = ci.astype(jnp.int32).reshape(1)

    ga = jnp.concatenate([ffn_pre_w_gate, ffn_pre_w_up, ffn_post_w_gate, ffn_post_w_up], axis=0).astype(BF16)
    gb = jnp.concatenate([ffn_pre_w_down.reshape(2 * FS, D), ffn_post_w_down.reshape(2 * FS, D),
                          ab_w_out[0], c_w_in[0], c_w_out[0]], axis=0).astype(BF16)
    gc = ab_w_in[0].astype(BF16)
    gd = jnp.pad(ab_conv_w[0], ((0, 5), (0, 0)))
    wa, wb, wc, conv_w = _gather_weights([ga.reshape(2, 4 * D, FS), gb.reshape(2, B_ROWS // 2, D),
                                          gc.reshape(2, D // 2, 768), jnp.stack([gd, jnp.zeros_like(gd)])])
    wa = wa.reshape(NSH, 8, D, FS)
    wb = wb.reshape(NSH, B_ROWS, D)
    wc = wc.reshape(NSH, D, 768)

    small = _small_slab([ffn_pre_norm, mix_norm, ffn_post_norm, c_lower_bounds, final_norm.reshape(1, D), c_out_norm])
    small = small.reshape(SMALL_ROWS, 1, D)

    grad_x, grads_a, grads_b, d_abin, small_grad = _local_step(x[0], loss_target[0], wa, wb, wc, conv_w, small)

    def halves(g):
        nsh, rows, cols = g.shape
        return jnp.transpose(g.astype(BF16).reshape(nsh, 2, rows // 2, cols), (1, 0, 2, 3))

    full = [halves(jnp.concatenate(grads_a, axis=1)), halves(jnp.concatenate(grads_b, axis=1)), halves(d_abin)]
    got = _swap_halves(full)
    pair = [_pair_sum("grad_pair_sum_%d" % n, g, r, c_idx) for n, (g, r) in enumerate(zip(full, got))]
    got = _send_to_owners(pair)
    mine = [_owner_sum("grad_owner_sum_%d" % n, g, r, p_idx) for n, (g, r) in enumerate(zip(pair, got))]
    ra, rb, rc = _share_halves(mine)
    ra = ra.reshape(8, D, FS)
    rb = rb.reshape(B_ROWS, D)
    rc = rc.reshape(D, 768)
    small_sum = _sum_small(small_grad)

    my_conv = lax.dynamic_slice(small_sum[R_CONV:R_CONV + 3], (0, (2 * xi + yi) * 128), (3, 128))
    grads = {
        "ffn_pre_norm": small_sum[R_PRE:R_PRE + 2], "ffn_pre_w_gate": ra[0:2], "ffn_pre_w_up": ra[2:4],
        "ffn_pre_w_down": rb[0:2 * FS].reshape(2, FS, D), "mix_norm": small_sum[R_MIX:R_MIX + 2],
        "ffn_post_norm": small_sum[R_POST:R_POST + 2], "ffn_post_w_gate": ra[4:6], "ffn_post_w_up": ra[6:8],
        "ffn_post_w_down": rb[2 * FS:4 * FS].reshape(2, FS, D), "ab_w_in": rc.reshape(1, D, 768),
        "ab_conv_w": my_conv.reshape(1, 3, 128), "ab_w_out": rb[B_ABOUT:B_ABOUT + 256].reshape(1, 256, D),
        "c_w_in": rb[B_CIN:B_CIN + D].reshape(1, D, D), "c_lower_bounds": small_sum[R_CLB:R_CLB + 2],
        "c_out_norm": small_sum[R_GAM:R_GAM + 1, :HD], "c_w_out": rb[B_COUT:B_COUT + 256].reshape(1, 256, D),
        "final_norm": small_sum[R_FIN],
    }
    weights = dict(ffn_pre_norm=ffn_pre_norm, ffn_pre_w_gate=ffn_pre_w_gate, ffn_pre_w_up=ffn_pre_w_up, ffn_pre_w_down=ffn_pre_w_down, mix_norm=mix_norm, ffn_post_norm=ffn_post_norm, ffn_post_w_gate=ffn_post_w_gate, ffn_post_w_up=ffn_post_w_up, ffn_post_w_down=ffn_post_w_down, ab_w_in=ab_w_in, ab_conv_w=ab_conv_w, ab_w_out=ab_w_out, c_w_in=c_w_in, c_lower_bounds=c_lower_bounds, c_out_norm=c_out_norm, c_w_out=c_w_out, final_norm=final_norm)
    m_in = dict(ffn_pre_norm=m_ffn_pre_norm, ffn_pre_w_gate=m_ffn_pre_w_gate, ffn_pre_w_up=m_ffn_pre_w_up, ffn_pre_w_down=m_ffn_pre_w_down, mix_norm=m_mix_norm, ffn_post_norm=m_ffn_post_norm, ffn_post_w_gate=m_ffn_post_w_gate, ffn_post_w_up=m_ffn_post_w_up, ffn_post_w_down=m_ffn_post_w_down, ab_w_in=m_ab_w_in, ab_conv_w=m_ab_conv_w, ab_w_out=m_ab_w_out, c_w_in=m_c_w_in, c_lower_bounds=m_c_lower_bounds, c_out_norm=m_c_out_norm, c_w_out=m_c_w_out, final_norm=m_final_norm)
    v_in = dict(ffn_pre_norm=v_ffn_pre_norm, ffn_pre_w_gate=v_ffn_pre_w_gate, ffn_pre_w_up=v_ffn_pre_w_up, ffn_pre_w_down=v_ffn_pre_w_down, mix_norm=v_mix_norm, ffn_post_norm=v_ffn_post_norm, ffn_post_w_gate=v_ffn_post_w_gate, ffn_post_w_up=v_ffn_post_w_up, ffn_post_w_down=v_ffn_post_w_down, ab_w_in=v_ab_w_in, ab_conv_w=v_ab_conv_w, ab_w_out=v_ab_w_out, c_w_in=v_c_w_in, c_lower_bounds=v_c_lower_bounds, c_out_norm=v_c_out_norm, c_w_out=v_c_w_out, final_norm=v_final_norm)
    names = list(weights)
    big = [n for n in names if weights[n].size >= 65536]
    tiny = [n for n in names if n not in big]

    delta, new_m, new_v = {}, {}, {}
    for n in big:
        shape = weights[n].shape
        two_d = (shape[0] * shape[1], shape[2])
        d, m2, v2 = _adamw("adamw_" + n, weights[n].reshape(two_d), grads[n].reshape(two_d),
                           m_in[n].reshape(two_d), v_in[n].reshape(two_d))
        delta[n], new_m[n], new_v[n] = d.reshape(shape), m2.reshape(shape), v2.reshape(shape)

    def tiny_slab(src):
        return _small_slab([src[n].reshape(-1, src[n].shape[-1]) for n in tiny])

    offs, row = {}, 0
    for n in tiny:
        nrows = weights[n].size // weights[n].shape[-1]
        offs[n] = (row, nrows)
        row += nrows
    d, m2, v2 = _adamw("adamw_small", tiny_slab(weights), tiny_slab(grads), tiny_slab(m_in), tiny_slab(v_in))
    for n in tiny:
        r0, nr = offs[n]
        shape = weights[n].shape
        for dst, src in ((delta, d), (new_m, m2), (new_v, v2)):
            dst[n] = src[r0:r0 + nr, :shape[-1]].reshape(shape)

    loss = small_sum[R_LOSS, 0]
    return (loss, grad_x.reshape(1, t, D), *[grads[n] for n in names], *[delta[n] for n in names],
            *[new_m[n] for n in names], *[new_v[n] for n in names])
```

```python
import functools
import math

import jax
import jax.numpy as jnp
from jax import lax
from jax.experimental import pallas as pl
from jax.experimental.pallas import tpu as pltpu

F32 = jnp.float32
BF16 = jnp.bfloat16
MESH = pl.DeviceIdType.MESH
ANY = pl.BlockSpec(memory_space=pl.ANY)

D = 1024
FS = 704
NSH = 4
RMS_EPS = 1e-6
MACARON = 0.5
CHUNK = 64
HD = 128
SBQ = 128
SB_DEAD = -105.0
CONV_HALO = 8
LANE = 128
ROW_TILE = 512
VMEM_LIMIT = 48 * 1024 * 1024

ADAM_LR, ADAM_B1, ADAM_B2, ADAM_EPS, ADAM_WD, ADAM_STEP = 0.001, 0.9, 0.999, 1e-08, 0.01, 10

NN = ((1,), (0,))
NT = ((1,), (1,))
TN = ((0,), (0,))

SMALL_ROWS = 16
R_PRE, R_MIX, R_POST, R_CLB, R_FIN, R_GAM, R_CONV, R_LOSS = 0, 2, 4, 6, 8, 9, 10, 13

B_DOWN = 0
B_ABOUT = 4 * FS
B_CIN = B_ABOUT + 256
B_COUT = B_CIN + 1024
B_ROWS = B_COUT + 256


def _dg(a, b, dims):
    return lax.dot_general(a, b, (dims, ((), ())), preferred_element_type=F32)


def _split(x):
    hi = x.astype(BF16)
    lo = (x - hi.astype(F32)).astype(BF16)
    return hi, lo


def _dot3(a, b, dims):
    ah, al = _split(a)
    bh, bl = _split(b)
    return _dg(ah, bh, dims) + _dg(ah, bl, dims) + _dg(al, bh, dims)


def _sigmoid(x):
    return 1.0 / (1.0 + jnp.exp(-x))


def _params(sem):
    return pltpu.CompilerParams(dimension_semantics=sem, vmem_limit_bytes=VMEM_LIMIT)


def _spec(shape, imap):
    return pl.BlockSpec(shape, imap)


def _mm(name, pairs, *, grid, o_shape, o_dtype, o_spec, dims, kaxis, nk, acc_shape=None, res=None, scale=None):
    npairs = len(pairs)
    operands, specs = [], []
    for a, a_spec, b, b_spec in pairs:
        operands += [a, b]
        specs += [a_spec, b_spec]
    if res is not None:
        operands.append(res[0])
        specs.append(res[1])

    def body(*refs):
        o_ref = refs[2 * npairs + (1 if res is not None else 0)]
        part = None
        for n in range(npairs):
            d = _dg(refs[2 * n][...], refs[2 * n + 1][...], dims)
            part = d if part is None else part + d

        def finish(val):
            if scale is not None:
                val = val * scale
            if res is not None:
                val = val + refs[2 * npairs][...]
            o_ref[...] = val.astype(o_dtype)

        if nk == 1:
            finish(part)
        else:
            acc_ref = refs[-1]
            k = pl.program_id(kaxis)

            @pl.when(k == 0)
            def _():
                acc_ref[...] = part

            @pl.when(k > 0)
            def _():
                acc_ref[...] += part

            @pl.when(k == nk - 1)
            def _():
                finish(acc_ref[...])

    sem = tuple("arbitrary" if (ax == kaxis and nk > 1) else "parallel" for ax in range(len(grid)))
    return pl.pallas_call(
        body, name=name, grid=grid, in_specs=specs, out_specs=o_spec,
        out_shape=jax.ShapeDtypeStruct(o_shape, o_dtype),
        scratch_shapes=[pltpu.VMEM(acc_shape, F32)] if nk > 1 else [],
        compiler_params=_params(sem),
    )(*operands)


def _norm_fwd(name, h, gain_slab, row):
    t = h.shape[0]
    tm = min(ROW_TILE, t)

    def body(h_ref, g_ref, o_ref):
        x = h_ref[...]
        r = lax.rsqrt(jnp.mean(x * x, axis=-1, keepdims=True) + RMS_EPS)
        o_ref[...] = (x * r * g_ref[...]).astype(BF16)

    return pl.pallas_call(
        body, name=name, grid=(t // tm,),
        in_specs=[_spec((tm, D), lambda i: (i, 0)), _spec((None, 1, D), lambda i: (row, 0, 0))],
        out_specs=_spec((tm, D), lambda i: (i, 0)),
        out_shape=jax.ShapeDtypeStruct((t, D), BF16),
        compiler_params=_params(("parallel",)),
    )(h, gain_slab)


def _norm_bwd(name, dhn, h, gain_slab, row, dres):
    t = h.shape[0]
    tm = min(ROW_TILE, t)
    nt = t // tm

    def body(dhn_ref, h_ref, g_ref, dres_ref, dh_ref, dhb_ref, dg_ref, acc_ref):
        i = pl.program_id(0)
        x = h_ref[...]
        r = lax.rsqrt(jnp.mean(x * x, axis=-1, keepdims=True) + RMS_EPS)
        xh = x * r
        dy = dhn_ref[...]
        gdy = dy * g_ref[...]
        dx = (gdy - xh * jnp.mean(gdy * xh, axis=-1, keepdims=True)) * r
        dh = dres_ref[...] + dx
        dh_ref[...] = dh
        dhb_ref[...] = dh.astype(BF16)
        part = jnp.sum((dy * xh).reshape(tm // 8, 8, D), axis=0)

        @pl.when(i == 0)
        def _():
            acc_ref[...] = part

        @pl.when(i > 0)
        def _():
            acc_ref[...] += part

        @pl.when(i == nt - 1)
        def _():
            dg_ref[...] = jnp.sum(acc_ref[...], axis=0, keepdims=True)

    row_spec = _spec((tm, D), lambda i: (i, 0))
    return pl.pallas_call(
        body, name=name, grid=(nt,),
        in_specs=[row_spec, row_spec, _spec((None, 1, D), lambda i: (row, 0, 0)), row_spec],
        out_specs=[row_spec, row_spec, _spec((1, D), lambda i: (0, 0))],
        out_shape=[jax.ShapeDtypeStruct((t, D), F32), jax.ShapeDtypeStruct((t, D), BF16),
                   jax.ShapeDtypeStruct((1, D), F32)],
        scratch_shapes=[pltpu.VMEM((8, D), F32)],
        compiler_params=_params(("arbitrary",)),
    )(dhn, h, gain_slab, dres)


def _final_loss(h, gain_slab, target):
    t = h.shape[0]
    tm = min(ROW_TILE, t)
    nt = t // tm

    def body(h_ref, g_ref, t_ref, dh_ref, dhb_ref, dg_ref, loss_ref, acc_ref, lacc_ref):
        i = pl.program_id(0)
        x = h_ref[...]
        g = g_ref[...]
        r = lax.rsqrt(jnp.mean(x * x, axis=-1, keepdims=True) + RMS_EPS)
        xh = x * r
        err = xh * g - t_ref[...]
        dy = err * (1.0 / D)
        gdy = dy * g
        dh = (gdy - xh * jnp.mean(gdy * xh, axis=-1, keepdims=True)) * r
        dh_ref[...] = dh
        dhb_ref[...] = dh.astype(BF16)
        part = jnp.sum((dy * xh).reshape(tm // 8, 8, D), axis=0)
        lpart = jnp.sum((err * err).reshape(tm // 8, 8, D), axis=0)

        @pl.when(i == 0)
        def _():
            acc_ref[...] = part
            lacc_ref[...] = lpart

        @pl.when(i > 0)
        def _():
            acc_ref[...] += part
            lacc_ref[...] += lpart

        @pl.when(i == nt - 1)
        def _():
            dg_ref[...] = jnp.sum(acc_ref[...], axis=0, keepdims=True)
            rows = jnp.sum(lacc_ref[...], axis=0, keepdims=True)
            loss_ref[...] = jnp.sum(rows, axis=1, keepdims=True) * (0.5 / D)

    row_spec = _spec((tm, D), lambda i: (i, 0))
    return pl.pallas_call(
        body, name="final_loss", grid=(nt,),
        in_specs=[row_spec, _spec((None, 1, D), lambda i: (R_FIN, 0, 0)), row_spec],
        out_specs=[row_spec, row_spec, _spec((1, D), lambda i: (0, 0)), _spec((1, 1), lambda i: (0, 0))],
        out_shape=[jax.ShapeDtypeStruct((t, D), F32), jax.ShapeDtypeStruct((t, D), BF16),
                   jax.ShapeDtypeStruct((1, D), F32), jax.ShapeDtypeStruct((1, 1), F32)],
        scratch_shapes=[pltpu.VMEM((8, D), F32), pltpu.VMEM((8, D), F32)],
        compiler_params=_params(("arbitrary",)),
    )(h, gain_slab, target)


def _ffn_up(name, hn, wa, gate_idx, up_idx):
    t = hn.shape[0]
    tm = min(ROW_TILE, t)

    def body(x_ref, wg_ref, wu_ref, g_ref, u_ref, a_ref):
        x = x_ref[...]
        g = _dg(x, wg_ref[...], NN)
        u = _dg(x, wu_ref[...], NN)
        g_ref[...] = g.astype(BF16)
        u_ref[...] = u.astype(BF16)
        a_ref[...] = (g * _sigmoid(g) * u).astype(BF16)

    act = _spec((None, tm, FS), lambda s, i: (s, i, 0))
    shape = jax.ShapeDtypeStruct((NSH, t, FS), BF16)
    return pl.pallas_call(
        body, name=name, grid=(NSH, t // tm),
        in_specs=[_spec((tm, D), lambda s, i: (i, 0)),
                  _spec((None, None, D, FS), lambda s, i: (s, gate_idx, 0, 0)),
                  _spec((None, None, D, FS), lambda s, i: (s, up_idx, 0, 0))],
        out_specs=[act, act, act], out_shape=[shape, shape, shape],
        compiler_params=_params(("parallel", "parallel")),
    )(hn, wa, wa)


def _ffn_down(name, a, wb, down_idx, h):
    t = h.shape[0]
    tm = min(ROW_TILE, t)
    return _mm(name, [(a, _spec((None, tm, FS), lambda i, k: (k, i, 0)),
                       wb, _spec((None, FS, D), lambda i, k: (k, down_idx, 0)))],
               grid=(t // tm, NSH), o_shape=(t, D), o_dtype=F32, o_spec=_spec((tm, D), lambda i, k: (i, 0)),
               dims=NN, kaxis=1, nk=NSH, acc_shape=(tm, D), res=(h, _spec((tm, D), lambda i, k: (i, 0))),
               scale=MACARON)


def _ffn_bwd_up(name, dhb, wb, down_idx, g, u):
    t = dhb.shape[0]
    tm = min(ROW_TILE, t)

    def body(dh_ref, wd_ref, g_ref, u_ref, dg_ref, du_ref):
        da = _dg(dh_ref[...], wd_ref[...], NT) * MACARON
        gv = g_ref[...].astype(F32)
        uv = u_ref[...].astype(F32)
        sg = _sigmoid(gv)
        du_ref[...] = (da * gv * sg).astype(BF16)
        dg_ref[...] = (da * uv * (sg * (1.0 + gv * (1.0 - sg)))).astype(BF16)

    act = _spec((None, tm, FS), lambda s, i: (s, i, 0))
    shape = jax.ShapeDtypeStruct((NSH, t, FS), BF16)
    return pl.pallas_call(
        body, name=name, grid=(NSH, t // tm),
        in_specs=[_spec((tm, D), lambda s, i: (i, 0)), _spec((None, FS, D), lambda s, i: (s, down_idx, 0)), act, act],
        out_specs=[act, act], out_shape=[shape, shape],
        compiler_params=_params(("parallel", "parallel")),
    )(dhb, wb, g, u)


def _wgrad(name, a, a_spec, b, b_spec, out_rows, out_cols, t, tt, scale=None):
    return _mm(name, [(a, a_spec, b, b_spec)], grid=(NSH, t // tt),
               o_shape=(NSH, out_rows, out_cols), o_dtype=F32,
               o_spec=_spec((None, out_rows, out_cols), lambda s, k: (s, 0, 0)),
               dims=TN, kaxis=1, nk=t // tt, acc_shape=(out_rows, out_cols), scale=scale)


def _ffn_backward(tag, dh, dhb, h_in, hn, g, u, a, wa, wb, gate_idx, up_idx, down_idx, small, norm_row):
    t = dh.shape[0]
    tm = min(ROW_TILE, t)
    tt = tm
    dg, du = _ffn_bwd_up(tag + "_bwd_up", dhb, wb, down_idx, g, u)
    tok = _spec((tt, D), lambda s, k: (k, 0))
    hid = _spec((None, tt, FS), lambda s, k: (s, k, 0))
    d_wd = _wgrad(tag + "_dwd", a, hid, dhb, tok, FS, D, t, tt, scale=MACARON)
    d_wg = _wgrad(tag + "_dwg", hn, tok, dg, hid, D, FS, t, tt)
    d_wu = _wgrad(tag + "_dwu", hn, tok, du, hid, D, FS, t, tt)
    act = _spec((None, tm, FS), lambda i, k: (k, i, 0))
    dhn = _mm(tag + "_dhn",
              [(dg, act, wa, _spec((None, None, D, FS), lambda i, k: (k, gate_idx, 0, 0))),
               (du, act, wa, _spec((None, None, D, FS), lambda i, k: (k, up_idx, 0, 0)))],
              grid=(t // tm, NSH), o_shape=(t, D), o_dtype=F32, o_spec=_spec((tm, D), lambda i, k: (i, 0)),
              dims=NT, kaxis=1, nk=NSH, acc_shape=(tm, D))
    dh_in, dhb_in, d_gain = _norm_bwd(tag + "_norm_bwd", dhn, h_in, small, norm_row, dh)
    return dh_in, dhb_in, d_wg, d_wu, d_wd, d_gain


def _conv_fwd(proj_a, conv_w):
    t = proj_a.shape[0]
    tm = min(ROW_TILE, t)
    hb = tm // CONV_HALO

    def body(ab_ref, ac_ref, ax_ref, acp_ref, axp_ref, w_ref, y_ref):
        i = pl.program_id(1)
        u = ac_ref[...] * ax_ref[...]
        up = jnp.where(i > 0, acp_ref[...] * axp_ref[...], 0.0)
        ext = jnp.concatenate([up, u], axis=0)
        u1 = pltpu.roll(ext, 1, 0)[CONV_HALO:]
        u2 = pltpu.roll(ext, 2, 0)[CONV_HALO:]
        w = w_ref[...]
        conv = w[0:1] * u2 + w[1:2] * u1 + w[2:3] * u
        y_ref[...] = (ab_ref[...] * conv).astype(BF16)

    def cur(off):
        return _spec((tm, LANE), lambda j, i: (i, off + j))

    def prev(off):
        return _spec((CONV_HALO, LANE), lambda j, i: (jnp.maximum(i * hb - 1, 0), off + j))

    return pl.pallas_call(
        body, name="conv_fwd", grid=(4, t // tm),
        in_specs=[cur(0), cur(4), cur(8), prev(4), prev(8),
                  _spec((None, None, 8, LANE), lambda j, i: (j, 0, 0, 0))],
        out_specs=_spec((tm, LANE), lambda j, i: (i, j)),
        out_shape=jax.ShapeDtypeStruct((t, 512), BF16),
        compiler_params=_params(("parallel", "parallel")),
    )(proj_a, proj_a, proj_a, proj_a, proj_a, conv_w)


def _conv_bwd(proj_a, conv_w, dy):
    t = proj_a.shape[0]
    tm = min(ROW_TILE, t)
    hb = tm // CONV_HALO
    nt = t // tm

    def body(ab_ref, ac_ref, ax_ref, dy_ref, acp_ref, axp_ref, abn_ref, dyn_ref, w_ref,
             dab_ref, dac_ref, dax_ref, dw_ref, acc_ref):
        i = pl.program_id(1)
        ab, ac, ax = ab_ref[...], ac_ref[...], ax_ref[...]
        u = ac * ax
        up = jnp.where(i > 0, acp_ref[...] * axp_ref[...], 0.0)
        ext = jnp.concatenate([up, u], axis=0)
        u1 = pltpu.roll(ext, 1, 0)[CONV_HALO:]
        u2 = pltpu.roll(ext, 2, 0)[CONV_HALO:]
        w = w_ref[...]
        conv = w[0:1] * u2 + w[1:2] * u1 + w[2:3] * u
        dy_v = dy_ref[...]
        dab_ref[...] = (dy_v * conv).astype(BF16)
        dc = dy_v * ab
        dcn = jnp.where(i < nt - 1, dyn_ref[...] * abn_ref[...], 0.0)
        extn = jnp.concatenate([dc, dcn], axis=0)
        n = tm + CONV_HALO
        dc1 = pltpu.roll(extn, n - 1, 0)[:tm]
        dc2 = pltpu.roll(extn, n - 2, 0)[:tm]
        du = w[2:3] * dc + w[1:2] * dc1 + w[0:1] * dc2
        dac_ref[...] = (du * ax).astype(BF16)
        dax_ref[...] = (du * ac).astype(BF16)
        rid = lax.broadcasted_iota(jnp.int32, (8, LANE), 0)
        part = jnp.where(rid == 0, jnp.sum(dc * u2, axis=0, keepdims=True),
                         jnp.where(rid == 1, jnp.sum(dc * u1, axis=0, keepdims=True),
                                   jnp.where(rid == 2, jnp.sum(dc * u, axis=0, keepdims=True), 0.0)))

        @pl.when(i == 0)
        def _():
            acc_ref[...] = part

        @pl.when(i > 0)
        def _():
            acc_ref[...] += part

        @pl.when(i == nt - 1)
        def _():
            dw_ref[...] = acc_ref[...]

    def cur(off):
        return _spec((tm, LANE), lambda j, i: (i, off + j))

    def prev(off):
        return _spec((CONV_HALO, LANE), lambda j, i: (jnp.maximum(i * hb - 1, 0), off + j))

    def nxt(off):
        return _spec((CONV_HALO, LANE), lambda j, i: (jnp.minimum((i + 1) * hb, nt * hb - 1), off + j))

    outs = pl.pallas_call(
        body, name="conv_bwd", grid=(4, nt),
        in_specs=[cur(0), cur(4), cur(8), cur(0), prev(4), prev(8), nxt(0), nxt(0),
                  _spec((None, None, 8, LANE), lambda j, i: (j, 0, 0, 0))],
        out_specs=[_spec((tm, LANE), lambda j, i: (i, j)), _spec((tm, LANE), lambda j, i: (i, j)),
                   _spec((tm, LANE), lambda j, i: (i, j)), _spec((None, 8, LANE), lambda j, i: (j, 0, 0))],
        out_shape=[jax.ShapeDtypeStruct((t, 512), BF16), jax.ShapeDtypeStruct((t, 512), BF16),
                   jax.ShapeDtypeStruct((t, 512), BF16), jax.ShapeDtypeStruct((4, 8, LANE), F32)],
        scratch_shapes=[pltpu.VMEM((8, LANE), F32)],
        compiler_params=_params(("parallel", "arbitrary")),
    )(proj_a, proj_a, proj_a, dy, proj_a, proj_a, proj_a, dy, conv_w)
    return outs


def _log_sigmoid(z):
    return jnp.minimum(z, 0.0) - jnp.log(1.0 + jnp.exp(-jnp.abs(z)))


def _sb_masks():
    row = lax.broadcasted_iota(jnp.int32, (SBQ, SBQ), 0)
    col = lax.broadcasted_iota(jnp.int32, (SBQ, SBQ), 1)
    return row, col


def _ones_where(mask):
    return jnp.where(mask, 1.0, 0.0).astype(BF16)


def _head_mask(head):
    lane = lax.broadcasted_iota(jnp.int32, (1, LANE), 1)
    return lane >= 64 if head else lane < 64


def _sb_fwd(proj_b):
    t = proj_b.shape[0]
    nq = t // SBQ
    scale = 1.0 / math.sqrt(64.0)

    def body(q_ref, k_ref, v_ref, y_ref, l_ref, n_ref):
        hp = pl.program_id(0)
        qi = pl.program_id(1)
        row, col = _sb_masks()
        strict = col < row
        m_suffix = _ones_where(row > col)
        q_all = q_ref[...]
        out = jnp.zeros((SBQ, LANE), F32)
        tot = jnp.zeros((SBQ, LANE), F32)
        for head in range(2):
            hm = _head_mask(head)
            qh = jnp.where(hm, q_all, jnp.zeros_like(q_all))

            def block(kb, carry, diag):
                run, acc = carry
                start = pl.multiple_of(kb * SBQ, SBQ)
                kk = k_ref[pl.ds(start, SBQ), :]
                vv = v_ref[pl.ds(start, SBQ), :]
                z = _dg(qh, kk, NT) * scale
                lb = _log_sigmoid(z)
                lk = lb - z
                if diag:
                    lk = jnp.where(strict, lk, 0.0)
                hi, lo = _split(lk)
                later = run + _dg(hi, m_suffix, NN) + _dg(lo, m_suffix, NN)
                w = jnp.exp(lb + later)
                if diag:
                    w = jnp.where(strict, w, 0.0)
                acc = acc + _dg(w.astype(BF16), vv, NN)
                run = run + jnp.sum(hi.astype(F32) + lo.astype(F32), axis=1, keepdims=True)
                return run, acc

            run, acc = block(qi, (jnp.zeros((SBQ, 1), F32), jnp.zeros((SBQ, LANE), F32)), True)

            def live(c):
                return jnp.logical_and(c[0] < qi, jnp.max(c[1]) > SB_DEAD)

            def step(c):
                run, acc = block(qi - 1 - c[0], (c[1], c[2]), False)
                return c[0] + 1, run, acc

            n, run, acc = lax.while_loop(live, step, (jnp.int32(0), run, acc))
            out = jnp.where(hm, acc, out)
            tot = jnp.where(hm, run, tot)
            n_ref[(hp * nq + qi) * 2 + head] = n.astype(F32)
        y_ref[...] = out.astype(BF16)
        l_ref[...] = tot

    return pl.pallas_call(
        body, name="sb_fwd", grid=(4, nq),
        in_specs=[_spec((SBQ, LANE), lambda hp, i: (i, hp)),
                  _spec((t, LANE), lambda hp, i: (0, 4 + hp)),
                  _spec((t, LANE), lambda hp, i: (0, 8 + hp))],
        out_specs=[_spec((SBQ, LANE), lambda hp, i: (i, hp)), _spec((None, SBQ, LANE), lambda hp, i: (hp, i, 0)),
                   pl.BlockSpec(memory_space=pltpu.SMEM)],
        out_shape=[jax.ShapeDtypeStruct((t, 512), BF16), jax.ShapeDtypeStruct((4, t, LANE), F32),
                   jax.ShapeDtypeStruct((8 * nq,), F32)],
        compiler_params=_params(("arbitrary", "arbitrary")),
    )(proj_b, proj_b, proj_b)


def _sb_bwd(proj_b, dy, ltot, nblk):
    t = proj_b.shape[0]
    nq = t // SBQ
    scale = 1.0 / math.sqrt(64.0)

    def body(q_ref, k_ref, v_ref, dy_ref, l_ref, n_ref, dq_ref, dk_ref, dv_ref, dk_acc, dv_acc):
        hp = pl.program_id(0)
        qi = pl.program_id(1)

        @pl.when(qi == 0)
        def _():
            dk_acc[...] = jnp.zeros_like(dk_acc)
            dv_acc[...] = jnp.zeros_like(dv_acc)

        row, col = _sb_masks()
        strict = col < row
        m_prefix = _ones_where(row <= col)
        m_before = _ones_where(row < col)
        q_all = q_ref[...]
        do_all = dy_ref[...].astype(BF16)
        l_all = l_ref[...]
        dq_out = jnp.zeros((SBQ, LANE), F32)
        for head in range(2):
            hm = _head_mask(head)
            qh = jnp.where(hm, q_all, jnp.zeros_like(q_all))
            doh = jnp.where(hm, do_all, jnp.zeros_like(do_all))
            ltot_h = l_all[:, head * 64:head * 64 + 1]

            def block(kb, carry, diag):
                seen, dseen, dq = carry
                start = pl.multiple_of(kb * SBQ, SBQ)
                kk = k_ref[pl.ds(start, SBQ), :]
                vv = v_ref[pl.ds(start, SBQ), :]
                z = _dg(qh, kk, NT) * scale
                lb = _log_sigmoid(z)
                lk = lb - z
                if diag:
                    lk = jnp.where(strict, lk, 0.0)
                hi, lo = _split(lk)
                later = (ltot_h - seen) - (_dg(hi, m_prefix, NN) + _dg(lo, m_prefix, NN))
                w = jnp.exp(lb + later)
                if diag:
                    w = jnp.where(strict, w, 0.0)
                dw = _dg(doh, vv, NT)
                da = w * dw
                dv_acc[pl.ds(start, SBQ), :] += _dg(w.astype(BF16), doh, TN)
                dah, dal = _split(da)
                dlk = dseen + _dg(dah, m_before, NN) + _dg(dal, m_before, NN)
                sig = jnp.exp(lb)
                dz = (da * (1.0 - sig) - dlk * sig) * scale
                if diag:
                    dz = jnp.where(strict, dz, 0.0)
                dzb = dz.astype(BF16)
                dq = dq + _dg(dzb, kk, NN)
                dk_acc[pl.ds(start, SBQ), :] += _dg(dzb, qh, TN)
                seen = seen + jnp.sum(hi.astype(F32) + lo.astype(F32), axis=1, keepdims=True)
                dseen = dseen + jnp.sum(da, axis=1, keepdims=True)
                return seen, dseen, dq

            carry = (jnp.zeros((SBQ, 1), F32), jnp.zeros((SBQ, 1), F32), jnp.zeros((SBQ, LANE), F32))
            first = qi - n_ref[(hp * nq + qi) * 2 + head].astype(jnp.int32)
            carry = lax.fori_loop(first, qi, lambda n, c: block(n, c, False), carry)
            _, _, dq = block(qi, carry, True)
            dq_out = jnp.where(hm, dq, dq_out)
        dq_ref[...] = dq_out.astype(BF16)

        @pl.when(qi == nq - 1)
        def _():
            dk_ref[...] = dk_acc[...].astype(BF16)
            dv_ref[...] = dv_acc[...].astype(BF16)

    full = jax.ShapeDtypeStruct((t, 512), BF16)
    return pl.pallas_call(
        body, name="sb_bwd", grid=(4, nq),
        in_specs=[_spec((SBQ, LANE), lambda hp, i: (i, hp)),
                  _spec((t, LANE), lambda hp, i: (0, 4 + hp)),
                  _spec((t, LANE), lambda hp, i: (0, 8 + hp)),
                  _spec((SBQ, LANE), lambda hp, i: (i, 4 + hp)),
                  _spec((None, SBQ, LANE), lambda hp, i: (hp, i, 0)),
                  pl.BlockSpec(memory_space=pltpu.SMEM)],
        out_specs=[_spec((SBQ, LANE), lambda hp, i: (i, hp)),
                   _spec((t, LANE), lambda hp, i: (0, hp)), _spec((t, LANE), lambda hp, i: (0, hp))],
        out_shape=[full, full, full],
        scratch_shapes=[pltpu.VMEM((t, LANE), F32), pltpu.VMEM((t, LANE), F32)],
        compiler_params=_params(("parallel", "arbitrary")),
    )(proj_b, proj_b, proj_b, dy, ltot, nblk)


def _hgrn_gates(qr, fr, c0, c1):
    mx = jnp.maximum(c0, c1)
    e0, e1 = jnp.exp(c0 - mx), jnp.exp(c1 - mx)
    lb = e1 / (e0 + e1)
    sx = _sigmoid(fr)
    f = lb + (1.0 - lb) * sx
    k = (1.0 - lb) * (1.0 - sx)
    sq = _sigmoid(qr)
    return lb, sx, f, k, sq, qr * sq


def _hgrn_decays(f):
    row = lax.broadcasted_iota(jnp.int32, (CHUNK, CHUNK), 0)
    col = lax.broadcasted_iota(jnp.int32, (CHUNK, CHUNK), 1)
    tril = col <= row
    hi, lo = _split(jnp.log(f))
    trib = _ones_where(tril)
    b = _dg(trib, hi, NN) + _dg(trib, lo, NN)
    bm = b[CHUNK // 2 - 1:CHUNK // 2]
    bl = b[CHUNK - 1:CHUNK]
    return tril, col >= row, b, bm, bl


def _hgrn_fwd(proj_c, small):
    t = proj_c.shape[1]
    nc = t // CHUNK
    nh = D // HD

    def body(p_ref, c0_ref, c1_ref, gam_ref, o_ref, y_ref, sst_ref, st_ref):
        c = pl.program_id(1)

        @pl.when(c == 0)
        def _():
            st_ref[...] = jnp.zeros_like(st_ref)

        st0 = st_ref[...]
        sst_ref[...] = st0
        qr, fr, v, g = p_ref[0], p_ref[1], p_ref[2], p_ref[3]
        lb, sx, f, k, sq, q = _hgrn_gates(qr, fr, c0_ref[...], c1_ref[...])
        tril, _, b, bm, bl = _hgrn_decays(f)
        qd = q * jnp.exp(b)
        qt = q * jnp.exp(b - bm)
        kt = k * jnp.exp(bm - b)
        kl = k * jnp.exp(bl - b)
        vb = v.astype(BF16)
        att = jnp.where(tril, _dot3(qt, kt, NT), 0.0)
        o = _dg(qd.astype(BF16), st0.astype(BF16), NT) + _dg(att.astype(BF16), vb, NN)
        st_ref[...] = st0 * jnp.exp(bl) + _dg(vb, kl.astype(BF16), TN)
        o_ref[...] = o
        r = lax.rsqrt(jnp.mean(o * o, axis=-1, keepdims=True) + RMS_EPS)
        y_ref[...] = (o * r * gam_ref[...] * (g * _sigmoid(g))).astype(BF16)

    return pl.pallas_call(
        body, name="hgrn_fwd", grid=(nh, nc),
        in_specs=[_spec((4, CHUNK, HD), lambda h, c: (0, c, h)),
                  _spec((None, 1, HD), lambda h, c: (R_CLB, 0, h)),
                  _spec((None, 1, HD), lambda h, c: (R_CLB + 1, 0, h)),
                  _spec((None, 1, HD), lambda h, c: (R_GAM, 0, 0))],
        out_specs=[_spec((CHUNK, HD), lambda h, c: (c, h)), _spec((CHUNK, HD), lambda h, c: (c, h)),
                   _spec((None, None, HD, HD), lambda h, c: (c, h, 0, 0))],
        out_shape=[jax.ShapeDtypeStruct((t, D), F32), jax.ShapeDtypeStruct((t, D), BF16),
                   jax.ShapeDtypeStruct((nc, nh, HD, HD), F32)],
        scratch_shapes=[pltpu.VMEM((HD, HD), F32)],
        compiler_params=_params(("parallel", "arbitrary")),
    )(proj_c, small, small, small)


def _hgrn_bwd(proj_c, small, o, sst, dyc):
    t = proj_c.shape[1]
    nc = t // CHUNK
    nh = D // HD

    def body(p_ref, c0_ref, c1_ref, gam_ref, o_ref, sst_ref, dy_ref, dp_ref, dclb_ref, dgam_ref, dst_ref, dlb_acc, dgam_acc):
        head = pl.program_id(0)
        step = pl.program_id(1)

        @pl.when(step == 0)
        def _():
            dst_ref[...] = jnp.zeros_like(dst_ref)
            dlb_acc[...] = jnp.zeros_like(dlb_acc)

        @pl.when((step == 0) & (head == 0))
        def _():
            dgam_acc[...] = jnp.zeros_like(dgam_acc)

        st0 = sst_ref[...]
        dst1 = dst_ref[...]
        qr, fr, v, g = p_ref[0], p_ref[1], p_ref[2], p_ref[3]
        gam = gam_ref[...]
        lb, sx, f, k, sq, q = _hgrn_gates(qr, fr, c0_ref[...], c1_ref[...])
        tril, triu, b, bm, bl = _hgrn_decays(f)
        eb = jnp.exp(b)
        e_qt = jnp.exp(b - bm)
        e_kt = jnp.exp(bm - b)
        e_kl = jnp.exp(bl - b)
        e_bl = jnp.exp(bl)
        qd, qt, kt, kl = q * eb, q * e_qt, k * e_kt, k * e_kl
        ov = o_ref[...]
        r = lax.rsqrt(jnp.mean(ov * ov, axis=-1, keepdims=True) + RMS_EPS)
        oh = ov * r
        sg = _sigmoid(g)
        dy = dy_ref[...]
        dp_ref[3] = (dy * oh * gam * (sg * (1.0 + g * (1.0 - sg)))).astype(BF16)
        dyv = dy * (g * sg)
        dgam_acc[...] += jnp.sum(dyv * oh, axis=0, keepdims=True)
        gdy = dyv * gam
        do = (gdy - oh * jnp.mean(gdy * oh, axis=-1, keepdims=True)) * r
        dob, vb = do.astype(BF16), v.astype(BF16)
        st0b, dst1b = st0.astype(BF16), dst1.astype(BF16)
        st1 = st0 * e_bl + _dg(vb, kl.astype(BF16), TN)
        att = jnp.where(tril, _dot3(qt, kt, NT), 0.0)
        datt = jnp.where(tril, _dg(dob, vb, NT), 0.0)
        dv = _dg(att.astype(BF16), dob, TN) + _dg(kl.astype(BF16), dst1b, NT)
        dq = _dot3(datt, kt, NN) * e_qt + _dg(dob, st0b, NN) * eb
        dk = _dot3(datt, qt, TN) * e_kt + _dg(vb, dst1b, NN) * e_kl
        db = q * dq - k * dk
        last = lax.broadcasted_iota(jnp.int32, (CHUNK, 1), 0) == CHUNK - 1
        db = db + jnp.where(last, jnp.sum(dst1 * st1, axis=0, keepdims=True), 0.0)
        dbh, dbl = _split(db)
        triub = _ones_where(triu)
        dlf = _dg(triub, dbh, NN) + _dg(triub, dbl, NN)
        dst_ref[...] = dst1 * e_bl + _dg(dob, qd.astype(BF16), TN)
        dp_ref[0] = (dq * (sq * (1.0 + qr * (1.0 - sq)))).astype(BF16)
        tmp = dlf / f - dk
        dp_ref[1] = (tmp * (1.0 - lb) * sx * (1.0 - sx)).astype(BF16)
        dp_ref[2] = dv.astype(BF16)
        dlb_acc[...] += jnp.sum((1.0 - sx) * tmp, axis=0, keepdims=True)

        @pl.when(step == nc - 1)
        def _():
            d1 = dlb_acc[...] * lb * (1.0 - lb)
            dclb_ref[...] = jnp.where(lax.broadcasted_iota(jnp.int32, (2, HD), 0) == 0, -d1, d1)

        @pl.when((step == nc - 1) & (head == nh - 1))
        def _():
            dgam_ref[...] = dgam_acc[...]

    rev = lambda h, s: (nc - 1 - s, h)
    return pl.pallas_call(
        body, name="hgrn_bwd", grid=(nh, nc),
        in_specs=[_spec((4, CHUNK, HD), lambda h, s: (0, nc - 1 - s, h)),
                  _spec((None, 1, HD), lambda h, s: (R_CLB, 0, h)),
                  _spec((None, 1, HD), lambda h, s: (R_CLB + 1, 0, h)),
                  _spec((None, 1, HD), lambda h, s: (R_GAM, 0, 0)),
                  _spec((CHUNK, HD), rev),
                  _spec((None, None, HD, HD), lambda h, s: (nc - 1 - s, h, 0, 0)),
                  _spec((CHUNK, HD), rev)],
        out_specs=[_spec((4, CHUNK, HD), lambda h, s: (0, nc - 1 - s, h)),
                   _spec((2, HD), lambda h, s: (0, h)),
                   _spec((1, HD), lambda h, s: (0, 0))],
        out_shape=[jax.ShapeDtypeStruct((4, t, D), BF16), jax.ShapeDtypeStruct((2, D), F32),
                   jax.ShapeDtypeStruct((1, HD), F32)],
        scratch_shapes=[pltpu.VMEM((HD, HD), F32), pltpu.VMEM((1, HD), F32), pltpu.VMEM((1, HD), F32)],
        compiler_params=_params(("arbitrary", "arbitrary")),
    )(proj_c, small, small, small, o, sst, dyc)


def _adamw(name, w, g, m, v):
    rows, cols = w.shape
    br = rows
    for cand in (512, 352, 256):
        if rows % cand == 0:
            br = cand
            break
    c1 = 1.0 - ADAM_B1 ** ADAM_STEP
    c2 = 1.0 - ADAM_B2 ** ADAM_STEP

    def body(w_ref, g_ref, m_ref, v_ref, d_ref, mo_ref, vo_ref):
        gv = g_ref[...]
        mn = ADAM_B1 * m_ref[...] + (1.0 - ADAM_B1) * gv
        vn = ADAM_B2 * v_ref[...] + (1.0 - ADAM_B2) * (gv * gv)
        mo_ref[...] = mn
        vo_ref[...] = vn
        d_ref[...] = -ADAM_LR * ((mn / c1) / (jnp.sqrt(vn / c2) + ADAM_EPS) + ADAM_WD * w_ref[...])

    blk = _spec((br, cols), lambda i: (i, 0))
    shape = jax.ShapeDtypeStruct((rows, cols), F32)
    return pl.pallas_call(
        body, name=name, grid=(rows // br,), in_specs=[blk] * 4, out_specs=[blk] * 3, out_shape=[shape] * 3,
        compiler_params=_params(("parallel",)),
    )(w, g, m, v)


def _place():
    x, y, c = lax.axis_index("x"), lax.axis_index("y"), lax.axis_index("c")
    chips = [(1 - x, y), (x, 1 - y), (1 - x, 1 - y)]
    return x, y, c, chips


def _gather_weights(groups):
    ng = len(groups)

    def body(*refs):
        ins, outs = refs[:ng], refs[ng:2 * ng]
        send, recv = refs[2 * ng:]
        x, y, c, chips = _place()
        sibling = (x, y, 1 - c)

        def half(gi, chip, hc):
            return outs[gi].at[2 * chip[0] + chip[1], hc]

        def copy(gi, k, src, dst, to):
            return pltpu.make_async_remote_copy(src_ref=src, dst_ref=dst, send_sem=send.at[6 * gi + k],
                                                recv_sem=recv.at[6 * gi + k], device_id=to, device_id_type=MESH)

        first = [copy(gi, j, ins[gi].at[c], half(gi, (x, y), c), (*chip, c))
                 for gi in range(ng) for j, chip in enumerate(chips)]
        for cp in first:
            cp.start()
        passed = []
        for gi in range(ng):
            for j, chip in enumerate(chips):
                copy(gi, j, half(gi, chip, c), half(gi, chip, c), sibling).wait_recv()
                fwd = copy(gi, 3 + j, half(gi, chip, c), half(gi, chip, c), sibling)
                fwd.start()
                passed.append(fwd)
        for gi in range(ng):
            for j, chip in enumerate(chips):
                copy(gi, 3 + j, half(gi, chip, 1 - c), half(gi, chip, 1 - c), sibling).wait_recv()
        for cp in first + passed:
            cp.wait_send()

    outs = pl.pallas_call(
        body, name="gather_weights",
        in_specs=[ANY] * ng, out_specs=[ANY] * ng,
        out_shape=[jax.ShapeDtypeStruct((NSH,) + g.shape, g.dtype) for g in groups],
        scratch_shapes=[pltpu.SemaphoreType.DMA((6 * ng,)), pltpu.SemaphoreType.DMA((6 * ng,))],
        compiler_params=pltpu.CompilerParams(has_side_effects=True),
    )(*groups)
    me = 2 * lax.axis_index("x") + lax.axis_index("y")
    return [lax.dynamic_update_index_in_dim(o, g, me, 0) for o, g in zip(outs, groups)]


def _swap_halves(groups):
    ng = len(groups)

    def body(*refs):
        ins, outs = refs[:ng], refs[ng:2 * ng]
        send, recv = refs[2 * ng:]
        x, y, c, _ = _place()
        cps = [pltpu.make_async_remote_copy(src_ref=ins[gi].at[1 - c], dst_ref=outs[gi], send_sem=send.at[gi],
                                            recv_sem=recv.at[gi], device_id=(x, y, 1 - c), device_id_type=MESH)
               for gi in range(ng)]
        for cp in cps:
            cp.start()
        for cp in cps:
            cp.wait()

    return pl.pallas_call(
        body, name="grad_swap_halves", in_specs=[ANY] * ng, out_specs=[ANY] * ng,
        out_shape=[jax.ShapeDtypeStruct(g.shape[1:], g.dtype) for g in groups],
        scratch_shapes=[pltpu.SemaphoreType.DMA((ng,)), pltpu.SemaphoreType.DMA((ng,))],
        compiler_params=pltpu.CompilerParams(has_side_effects=True),
    )(*groups)


def _send_to_owners(groups):
    ng = len(groups)

    def body(*refs):
        ins, outs = refs[:ng], refs[ng:2 * ng]
        send, recv = refs[2 * ng:]
        x, y, c, chips = _place()
        cps = [pltpu.make_async_remote_copy(src_ref=ins[gi].at[2 * chip[0] + chip[1]], dst_ref=outs[gi].at[j],
                                            send_sem=send.at[3 * gi + j], recv_sem=recv.at[3 * gi + j],
                                            device_id=(*chip, c), device_id_type=MESH)
               for gi in range(ng) for j, chip in enumerate(chips)]
        for cp in cps:
            cp.start()
        for cp in cps:
            cp.wait()

    return pl.pallas_call(
        body, name="grad_send_to_owners", in_specs=[ANY] * ng, out_specs=[ANY] * ng,
        out_shape=[jax.ShapeDtypeStruct((3,) + g.shape[1:], g.dtype) for g in groups],
        scratch_shapes=[pltpu.SemaphoreType.DMA((3 * ng,)), pltpu.SemaphoreType.DMA((3 * ng,))],
        compiler_params=pltpu.CompilerParams(has_side_effects=True),
    )(*groups)


def _share_halves(groups):
    ng = len(groups)

    def body(*refs):
        ins, outs = refs[:ng], refs[ng:2 * ng]
        send, recv = refs[2 * ng:]
        x, y, c, _ = _place()
        cps = [pltpu.make_async_remote_copy(src_ref=ins[gi], dst_ref=outs[gi].at[c], send_sem=send.at[gi],
                                            recv_sem=recv.at[gi], device_id=(x, y, 1 - c), device_id_type=MESH)
               for gi in range(ng)]
        for cp in cps:
            cp.start()
        for gi in range(ng):
            pltpu.make_async_remote_copy(src_ref=ins[gi], dst_ref=outs[gi].at[1 - c], send_sem=send.at[gi],
                                         recv_sem=recv.at[gi], device_id=(x, y, 1 - c), device_id_type=MESH).wait_recv()
        for cp in cps:
            cp.wait_send()

    outs = pl.pallas_call(
        body, name="grad_share_halves", in_specs=[ANY] * ng, out_specs=[ANY] * ng,
        out_shape=[jax.ShapeDtypeStruct((2,) + g.shape, g.dtype) for g in groups],
        scratch_shapes=[pltpu.SemaphoreType.DMA((ng,)), pltpu.SemaphoreType.DMA((ng,))],
        compiler_params=pltpu.CompilerParams(has_side_effects=True),
    )(*groups)
    c = lax.axis_index("c")
    return [lax.dynamic_update_index_in_dim(o, g, c, 0) for o, g in zip(outs, groups)]


def _pair_sum(name, g, got, c_idx):
    _, nsh, rows, cols = g.shape
    br = rows // 4

    def body(c_ref, a_ref, b_ref, o_ref):
        o_ref[...] = (a_ref[...].astype(F32) + b_ref[...].astype(F32)).astype(BF16)

    return pl.pallas_call(
        body, name=name,
        grid_spec=pltpu.PrefetchScalarGridSpec(
            num_scalar_prefetch=1, grid=(nsh, 4),
            in_specs=[pl.BlockSpec((None, None, br, cols), lambda q, i, cr: (cr[0], q, i, 0)),
                      pl.BlockSpec((None, br, cols), lambda q, i, cr: (q, i, 0))],
            out_specs=pl.BlockSpec((None, br, cols), lambda q, i, cr: (q, i, 0))),
        out_shape=jax.ShapeDtypeStruct((nsh, rows, cols), BF16),
        compiler_params=_params(("parallel", "parallel")),
    )(c_idx, g, got)


def _owner_sum(name, own, got, p_idx):
    _, rows, cols = own.shape
    br = rows // 4

    def body(p_ref, a_ref, b_ref, o_ref):
        o_ref[...] = ((a_ref[...].astype(F32) + b_ref[0].astype(F32)) + b_ref[1].astype(F32)) + b_ref[2].astype(F32)

    return pl.pallas_call(
        body, name=name,
        grid_spec=pltpu.PrefetchScalarGridSpec(
            num_scalar_prefetch=1, grid=(4,),
            in_specs=[pl.BlockSpec((None, br, cols), lambda i, pr: (pr[0], i, 0)),
                      pl.BlockSpec((3, br, cols), lambda i, pr: (0, i, 0))],
            out_specs=pl.BlockSpec((br, cols), lambda i, pr: (i, 0))),
        out_shape=jax.ShapeDtypeStruct((rows, cols), F32),
        compiler_params=_params(("parallel",)),
    )(p_idx, own, got)


def _sum_small(slab):
    def body(in_ref, out_ref, all_ref, send, recv):
        x, y, c, _ = _place()
        me = 4 * x + 2 * y + c
        all_ref[me] = in_ref[...]
        cps = []
        for k in range(1, 8):
            peer = (x ^ (k >> 2), y ^ ((k >> 1) & 1), c ^ (k & 1))
            cps.append(pltpu.make_async_remote_copy(src_ref=in_ref, dst_ref=all_ref.at[me], send_sem=send.at[k - 1],
                                                    recv_sem=recv.at[k - 1], device_id=peer, device_id_type=MESH))
        for cp in cps:
            cp.start()
        for cp in cps:
            cp.wait()
        total = all_ref[0]
        for d in range(1, 8):
            total = total + all_ref[d]
        out_ref[...] = total

    return pl.pallas_call(
        body, name="sum_small",
        in_specs=[pl.BlockSpec(memory_space=pltpu.VMEM)], out_specs=pl.BlockSpec(memory_space=pltpu.VMEM),
        out_shape=jax.ShapeDtypeStruct(slab.shape, F32),
        scratch_shapes=[pltpu.VMEM((8,) + slab.shape, F32), pltpu.SemaphoreType.DMA((7,)), pltpu.SemaphoreType.DMA((7,))],
        compiler_params=pltpu.CompilerParams(has_side_effects=True),
    )(slab)


def _local_step(x, target, wa, wb, wc, conv_w, small):
    t = x.shape[0]
    tm = min(ROW_TILE, t)
    nt = t // tm
    tok_si = _spec((tm, D), lambda s, i: (i, 0))

    def ffn_forward(tag, h, norm_row, gate_idx, up_idx, down_idx):
        hn = _norm_fwd(tag + "_norm", h, small, norm_row)
        g, u, a = _ffn_up(tag + "_up", hn, wa, gate_idx, up_idx)
        return _ffn_down(tag + "_down", a, wb, down_idx, h), (h, hn, g, u, a)

    def out_proj(name, y, blk, h):
        return _mm(name, [(y, _spec((tm, 256), lambda i, k: (i, k)), wb, _spec((None, 256, D), lambda i, k: (k, blk, 0)))],
                   grid=(nt, NSH), o_shape=(t, D), o_dtype=F32, o_spec=_spec((tm, D), lambda i, k: (i, 0)),
                   dims=NN, kaxis=1, nk=NSH, acc_shape=(tm, D), res=(h, _spec((tm, D), lambda i, k: (i, 0))))

    def out_proj_bwd(tag, dhb, y, blk):
        dy = _mm(tag + "_dy", [(dhb, tok_si, wb, _spec((None, 256, D), lambda s, i: (s, blk, 0)))],
                 grid=(NSH, nt), o_shape=(t, D), o_dtype=F32, o_spec=_spec((tm, 256), lambda s, i: (i, s)),
                 dims=NT, kaxis=1, nk=1)
        dw = _wgrad(tag + "_dwout", y, _spec((tm, 256), lambda s, k: (k, s)), dhb, _spec((tm, D), lambda s, k: (k, 0)),
                    256, D, t, tm)
        return dy, dw

    h0 = x
    h1, pre0 = ffn_forward("pre0", h0, R_PRE, 0, 2, 0)
    hn_ab = _norm_fwd("mix0_norm", h1, small, R_MIX)
    proj_a = _mm("ab_proj_a", [(hn_ab, tok_si, wc, _spec((None, D, 768), lambda s, i: (s, 0, 0)))],
                 grid=(2, nt), o_shape=(t, 1536), o_dtype=F32, o_spec=_spec((tm, 768), lambda s, i: (i, s)),
                 dims=NN, kaxis=1, nk=1)
    proj_b = _mm("ab_proj_b", [(hn_ab, tok_si, wc, _spec((None, D, 768), lambda s, i: (s + 2, 0, 0)))],
                 grid=(2, nt), o_shape=(t, 1536), o_dtype=BF16, o_spec=_spec((tm, 768), lambda s, i: (i, s)),
                 dims=NN, kaxis=1, nk=1)
    y_a = _conv_fwd(proj_a, conv_w)
    y_b, ltot, nblk = _sb_fwd(proj_b)
    y_ab = jnp.concatenate([y_a, y_b], axis=1)
    h2 = out_proj("ab_out", y_ab, B_ABOUT // 256, h1)
    h3, post0 = ffn_forward("post0", h2, R_POST, 4, 6, 2)
    h4, pre1 = ffn_forward("pre1", h3, R_PRE + 1, 1, 3, 1)
    hn_c = _norm_fwd("mix1_norm", h4, small, R_MIX + 1)
    proj_c = _mm("c_proj", [(hn_c, tok_si, wb, _spec((None, D, D), lambda s, i: (s, B_CIN // D, 0)))],
                 grid=(NSH, nt), o_shape=(NSH, t, D), o_dtype=F32, o_spec=_spec((None, tm, D), lambda s, i: (s, i, 0)),
                 dims=NN, kaxis=1, nk=1)
    o_c, y_c, sst = _hgrn_fwd(proj_c, small)
    h5 = out_proj("c_out", y_c, B_COUT // 256, h4)
    h6, post1 = ffn_forward("post1", h5, R_POST + 1, 5, 7, 3)
    dh, dhb, d_fin, loss = _final_loss(h6, small, target)

    dh, dhb, dg_post1, du_post1, dd_post1, dn_post1 = _ffn_backward("post1", dh, dhb, *post1, wa, wb, 5, 7, 3, small, R_POST + 1)
    dy_c, d_cout = out_proj_bwd("c", dhb, y_c, B_COUT // 256)
    dproj_c, d_clb, d_gam = _hgrn_bwd(proj_c, small, o_c, sst, dy_c)
    d_cin = _wgrad("c_dwin", hn_c, _spec((tm, D), lambda s, k: (k, 0)), dproj_c, _spec((None, tm, D), lambda s, k: (s, k, 0)),
                   D, D, t, tm)
    dhn = _mm("c_dhn", [(dproj_c, _spec((None, tm, D), lambda i, k: (k, i, 0)),
                         wb, _spec((None, D, D), lambda i, k: (k, B_CIN // D, 0)))],
              grid=(nt, NSH), o_shape=(t, D), o_dtype=F32, o_spec=_spec((tm, D), lambda i, k: (i, 0)),
              dims=NT, kaxis=1, nk=NSH, acc_shape=(tm, D))
    dh, dhb, dn_mix1 = _norm_bwd("mix1_norm_bwd", dhn, h4, small, R_MIX + 1, dh)
    dh, dhb, dg_pre1, du_pre1, dd_pre1, dn_pre1 = _ffn_backward("pre1", dh, dhb, *pre1, wa, wb, 1, 3, 1, small, R_PRE + 1)
    dh, dhb, dg_post0, du_post0, dd_post0, dn_post0 = _ffn_backward("post0", dh, dhb, *post0, wa, wb, 4, 6, 2, small, R_POST)
    dy_ab, d_about = out_proj_bwd("ab", dhb, y_ab, B_ABOUT // 256)
    dab, dac, dax, d_conv = _conv_bwd(proj_a, conv_w, dy_ab)
    dq, dk, dv = _sb_bwd(proj_b, dy_ab, ltot, nblk)
    dproj_ab = jnp.concatenate([dab, dac, dax, dq, dk, dv], axis=1)
    d_abin = _wgrad("ab_dwin", hn_ab, _spec((tm, D), lambda s, k: (k, 0)), dproj_ab, _spec((tm, 768), lambda s, k: (k, s)),
                    D, 768, t, tm)
    dhn = _mm("ab_dhn", [(dproj_ab, _spec((tm, 768), lambda i, k: (i, k)), wc, _spec((None, D, 768), lambda i, k: (k, 0, 0)))],
              grid=(nt, NSH), o_shape=(t, D), o_dtype=F32, o_spec=_spec((tm, D), lambda i, k: (i, 0)),
              dims=NT, kaxis=1, nk=NSH, acc_shape=(tm, D))
    dh, dhb, dn_mix0 = _norm_bwd("mix0_norm_bwd", dhn, h1, small, R_MIX, dh)
    dh, dhb, dg_pre0, du_pre0, dd_pre0, dn_pre0 = _ffn_backward("pre0", dh, dhb, *pre0, wa, wb, 0, 2, 0, small, R_PRE)

    grads_a = [dg_pre0, dg_pre1, du_pre0, du_pre1, dg_post0, dg_post1, du_post0, du_post1]
    grads_b = [dd_pre0, dd_pre1, dd_post0, dd_post1, d_about, d_cin, d_cout]
    zero = jnp.zeros((1, D), F32)
    conv_rows = jnp.pad(jnp.transpose(d_conv[:, :3, :], (1, 0, 2)).reshape(3, 512), ((0, 0), (0, D - 512)))
    small_grad = jnp.concatenate([
        dn_pre0, dn_pre1, dn_mix0, dn_mix1, dn_post0, dn_post1, d_clb, d_fin,
        jnp.pad(d_gam, ((0, 0), (0, D - HD))), conv_rows,
        jnp.pad(loss, ((0, 0), (0, D - 1))), zero, zero], axis=0)
    return dh, grads_a, grads_b, d_abin, small_grad


def _small_slab(rows):
    parts = [jnp.pad(r.astype(F32), ((0, 0), (0, D - r.shape[1]))) for r in rows]
    slab = jnp.concatenate(parts, axis=0)
    return jnp.pad(slab, ((0, SMALL_ROWS - slab.shape[0]), (0, 0)))


def kernel(x, ffn_pre_norm, ffn_pre_w_gate, ffn_pre_w_up, ffn_pre_w_down, mix_norm, ffn_post_norm, ffn_post_w_gate, ffn_post_w_up, ffn_post_w_down, ab_w_in, ab_conv_w, ab_w_out, c_w_in, c_lower_bounds, c_out_norm, c_w_out, final_norm, loss_target, m_ffn_pre_norm, m_ffn_pre_w_gate, m_ffn_pre_w_up, m_ffn_pre_w_down, m_mix_norm, m_ffn_post_norm, m_ffn_post_w_gate, m_ffn_post_w_up, m_ffn_post_w_down, m_ab_w_in, m_ab_conv_w, m_ab_w_out, m_c_w_in, m_c_lower_bounds, m_c_out_norm, m_c_w_out, m_final_norm, v_ffn_pre_norm, v_ffn_pre_w_gate, v_ffn_pre_w_up, v_ffn_pre_w_down, v_mix_norm, v_ffn_post_norm, v_ffn_post_w_gate, v_ffn_post_w_up, v_ffn_post_w_down, v_ab_w_in, v_ab_conv_w, v_ab_w_out, v_c_w_in, v_c_lower_bounds, v_c_out_norm, v_c_w_out, v_final_norm):
    t = x.shape[1]
    xi, yi, ci = lax.axis_index("x"), lax.axis_index("y"), lax.axis_index("c")
    p_idx = (2 * xi + yi).astype(jnp.int32).reshape(1)
    c_idx = ci.astype(jnp.int32).reshape(1)

    ga = jnp.concatenate([ffn_pre_w_gate, ffn_pre_w_up, ffn_post_w_gate, ffn_post_w_up], axis=0).astype(BF16)
    gb = jnp.concatenate([ffn_pre_w_down.reshape(2 * FS, D), ffn_post_w_down.reshape(2 * FS, D),
                          ab_w_out[0], c_w_in[0], c_w_out[0]], axis=0).astype(BF16)
    gc = ab_w_in[0].astype(BF16)
    gd = jnp.pad(ab_conv_w[0], ((0, 5), (0, 0)))
    wa, wb, wc, conv_w = _gather_weights([ga.reshape(2, 4 * D, FS), gb.reshape(2, B_ROWS // 2, D),
                                          gc.reshape(2, D // 2, 768), jnp.stack([gd, jnp.zeros_like(gd)])])
    wa = wa.reshape(NSH, 8, D, FS)
    wb = wb.reshape(NSH, B_ROWS, D)
    wc = wc.reshape(NSH, D, 768)

    small = _small_slab([ffn_pre_norm, mix_norm, ffn_post_norm, c_lower_bounds, final_norm.reshape(1, D), c_out_norm])
    small = small.reshape(SMALL_ROWS, 1, D)

    grad_x, grads_a, grads_b, d_abin, small_grad = _local_step(x[0], loss_target[0], wa, wb, wc, conv_w, small)

    def halves(g):
        nsh, rows, cols = g.shape
        return jnp.transpose(g.astype(BF16).reshape(nsh, 2, rows // 2, cols), (1, 0, 2, 3))

    full = [halves(jnp.concatenate(grads_a, axis=1)), halves(jnp.concatenate(grads_b, axis=1)), halves(d_abin)]
    got = _swap_halves(full)
    pair = [_pair_sum("grad_pair_sum_%d" % n, g, r, c_idx) for n, (g, r) in enumerate(zip(full, got))]
    got = _send_to_owners(pair)
    mine = [_owner_sum("grad_owner_sum_%d" % n, g, r, p_idx) for n, (g, r) in enumerate(zip(pair, got))]
    ra, rb, rc = _share_halves(mine)
    ra = ra.reshape(8, D, FS)
    rb = rb.reshape(B_ROWS, D)
    rc = rc.reshape(D, 768)
    small_sum = _sum_small(small_grad)

    my_conv = lax.dynamic_slice(small_sum[R_CONV:R_CONV + 3], (0, (2 * xi + yi) * 128), (3, 128))
    grads = {
        "ffn_pre_norm": small_sum[R_PRE:R_PRE + 2], "ffn_pre_w_gate": ra[0:2], "ffn_pre_w_up": ra[2:4],
        "ffn_pre_w_down": rb[0:2 * FS].reshape(2, FS, D), "mix_norm": small_sum[R_MIX:R_MIX + 2],
        "ffn_post_norm": small_sum[R_POST:R_POST + 2], "ffn_post_w_gate": ra[4:6], "ffn_post_w_up": ra[6:8],
        "ffn_post_w_down": rb[2 * FS:4 * FS].reshape(2, FS, D), "ab_w_in": rc.reshape(1, D, 768),
        "ab_conv_w": my_conv.reshape(1, 3, 128), "ab_w_out": rb[B_ABOUT:B_ABOUT + 256].reshape(1, 256, D),
        "c_w_in": rb[B_CIN:B_CIN + D].reshape(1, D, D), "c_lower_bounds": small_sum[R_CLB:R_CLB + 2],
        "c_out_norm": small_sum[R_GAM:R_GAM + 1, :HD], "c_w_out": rb[B_COUT:B_COUT + 256].reshape(1, 256, D),
        "final_norm": small_sum[R_FIN],
    }
    weights = dict(ffn_pre_norm=ffn_pre_norm, ffn_pre_w_gate=ffn_pre_w_gate, ffn_pre_w_up=ffn_pre_w_up, ffn_pre_w_down=ffn_pre_w_down, mix_norm=mix_norm, ffn_post_norm=ffn_post_norm, ffn_post_w_gate=ffn_post_w_gate, ffn_post_w_up=ffn_post_w_up, ffn_post_w_down=ffn_post_w_down, ab_w_in=ab_w_in, ab_conv_w=ab_conv_w, ab_w_out=ab_w_out, c_w_in=c_w_in, c_lower_bounds=c_lower_bounds, c_out_norm=c_out_norm, c_w_out=c_w_out, final_norm=final_norm)
    m_in = dict(ffn_pre_norm=m_ffn_pre_norm, ffn_pre_w_gate=m_ffn_pre_w_gate, ffn_pre_w_up=m_ffn_pre_w_up, ffn_pre_w_down=m_ffn_pre_w_down, mix_norm=m_mix_norm, ffn_post_norm=m_ffn_post_norm, ffn_post_w_gate=m_ffn_post_w_gate, ffn_post_w_up=m_ffn_post_w_up, ffn_post_w_down=m_ffn_post_w_down, ab_w_in=m_ab_w_in, ab_conv_w=m_ab_conv_w, ab_w_out=m_ab_w_out, c_w_in=m_c_w_in, c_lower_bounds=m_c_lower_bounds, c_out_norm=m_c_out_norm, c_w_out=m_c_w_out, final_norm=m_final_norm)
    v_in = dict(ffn_pre_norm=v_ffn_pre_norm, ffn_pre_w_gate=v_ffn_pre_w_gate, ffn_pre_w_up=v_ffn_pre_w_up, ffn_pre_w_down=v_ffn_pre_w_down, mix_norm=v_mix_norm, ffn_post_norm=v_ffn_post_norm, ffn_post_w_gate=v_ffn_post_w_gate, ffn_post_w_up=v_ffn_post_w_up, ffn_post_w_down=v_ffn_post_w_down, ab_w_in=v_ab_w_in, ab_conv_w=v_ab_conv_w, ab_w_out=v_ab_w_out, c_w_in=v_c_w_in, c_lower_bounds=v_c_lower_bounds, c_out_norm=v_c_out_norm, c_w_out=v_c_w_out, final_norm=v_final_norm)
    names = list(weights)
    big = [n for n in names if weights[n].size >= 65536]
    tiny = [n for n in names if n not in big]

    delta, new_m, new_v = {}, {}, {}
    for n in big:
        shape = weights[n].shape
        two_d = (shape[0] * shape[1], shape[2])
        d, m2, v2 = _adamw("adamw_" + n, weights[n].reshape(two_d), grads[n].reshape(two_d),
                           m_in[n].reshape(two_d), v_in[n].reshape(two_d))
        delta[n], new_m[n], new_v[n] = d.reshape(shape), m2.reshape(shape), v2.reshape(shape)

    def tiny_slab(src):
        return _small_slab([src[n].reshape(-1, src[n].shape[-1]) for n in tiny])

    offs, row = {}, 0
    for n in tiny:
        nrows = weights[n].size // weights[n].shape[-1]
        offs[n] = (row, nrows)
        row += nrows
    d, m2, v2 = _adamw("adamw_small", tiny_slab(weights), tiny_slab(grads), tiny_slab(m_in), tiny_slab(v_in))
    for n in tiny:
        r0, nr = offs[n]
        shape = weights[n].shape
        for dst, src in ((delta, d), (new_m, m2), (new_v, v2)):
            dst[n] = src[r0:r0 + nr, :shape[-1]].reshape(shape)

    loss = small_sum[R_LOSS, 0]
    return (loss, grad_x.reshape(1, t, D), *[grads[n] for n in names], *[delta[n] for n in names],
            *[new_m[n] for n in names], *[new_v[n] for n in names])
```

```python
import functools
import math

import jax
import jax.numpy as jnp
from jax import lax
from jax.experimental import pallas as pl
from jax.experimental.pallas import tpu as pltpu

F32 = jnp.float32
BF16 = jnp.bfloat16
MESH = pl.DeviceIdType.MESH
ANY = pl.BlockSpec(memory_space=pl.ANY)

D = 1024
FS = 704
NSH = 4
RMS_EPS = 1e-6
MACARON = 0.5
CHUNK = 64
HD = 128
HGRN_HPS = 4
SBQ = 128
SB_PAIRS = 2
SB_DEAD = -105.0
CONV_HALO = 8
LANE = 128
ROW_TILE = 512
VMEM_LIMIT = 48 * 1024 * 1024

ADAM_LR, ADAM_B1, ADAM_B2, ADAM_EPS, ADAM_WD, ADAM_STEP = 0.001, 0.9, 0.999, 1e-08, 0.01, 10

NN = ((1,), (0,))
NT = ((1,), (1,))
TN = ((0,), (0,))

SMALL_ROWS = 16
R_PRE, R_MIX, R_POST, R_CLB, R_FIN, R_GAM, R_CONV, R_LOSS = 0, 2, 4, 6, 8, 9, 10, 13

B_DOWN = 0
B_ABOUT = 4 * FS
B_CIN = B_ABOUT + 256
B_COUT = B_CIN + 1024
B_ROWS = B_COUT + 256


def _dg(a, b, dims):
    return lax.dot_general(a, b, (dims, ((), ())), preferred_element_type=F32)


def _split(x):
    hi = x.astype(BF16)
    lo = (x - hi.astype(F32)).astype(BF16)
    return hi, lo


def _dot3(a, b, dims):
    ah, al = _split(a)
    bh, bl = _split(b)
    return _dg(ah, bh, dims) + _dg(ah, bl, dims) + _dg(al, bh, dims)


def _sigmoid(x):
    return 1.0 / (1.0 + jnp.exp(-x))


def _params(sem):
    return pltpu.CompilerParams(dimension_semantics=sem, vmem_limit_bytes=VMEM_LIMIT)


def _spec(shape, imap):
    return pl.BlockSpec(shape, imap)


def _mm(name, pairs, *, grid, o_shape, o_dtype, o_spec, dims, kaxis, nk, acc_shape=None, res=None, scale=None,
        into=None):
    npairs = len(pairs)
    operands, specs = [], []
    for a, a_spec, b, b_spec in pairs:
        operands += [a, b]
        specs += [a_spec, b_spec]
    if res is not None:
        operands.append(res[0])
        specs.append(res[1])
    aliases = {}
    if into is not None:
        aliases = {len(operands): 0}
        operands.append(into)
        specs.append(ANY)

    def body(*refs):
        o_ref = refs[len(operands)]
        part = None
        for n in range(npairs):
            d = _dg(refs[2 * n][...], refs[2 * n + 1][...], dims)
            part = d if part is None else part + d

        def finish(val):
            if scale is not None:
                val = val * scale
            if res is not None:
                val = val + refs[2 * npairs][...]
            o_ref[...] = val.astype(o_dtype)

        if nk == 1:
            finish(part)
        else:
            acc_ref = refs[len(operands) + 1]
            k = pl.program_id(kaxis)

            @pl.when(k == 0)
            def _():
                acc_ref[...] = part

            @pl.when(k > 0)
            def _():
                acc_ref[...] += part

            @pl.when(k == nk - 1)
            def _():
                finish(acc_ref[...])

    sem = tuple("arbitrary" if (ax == kaxis and nk > 1) else "parallel" for ax in range(len(grid)))
    return pl.pallas_call(
        body, name=name, grid=grid, in_specs=specs, out_specs=o_spec,
        out_shape=jax.ShapeDtypeStruct(o_shape, o_dtype),
        scratch_shapes=[pltpu.VMEM(acc_shape, F32)] if nk > 1 else [],
        input_output_aliases=aliases,
        compiler_params=_params(sem),
    )(*operands)


def _norm_fwd(name, h, gain_slab, row):
    t = h.shape[0]
    tm = min(ROW_TILE, t)

    def body(h_ref, g_ref, o_ref):
        x = h_ref[...]
        r = lax.rsqrt(jnp.mean(x * x, axis=-1, keepdims=True) + RMS_EPS)
        o_ref[...] = (x * r * g_ref[...]).astype(BF16)

    return pl.pallas_call(
        body, name=name, grid=(t // tm,),
        in_specs=[_spec((tm, D), lambda i: (i, 0)), _spec((None, 1, D), lambda i: (row, 0, 0))],
        out_specs=_spec((tm, D), lambda i: (i, 0)),
        out_shape=jax.ShapeDtypeStruct((t, D), BF16),
        compiler_params=_params(("parallel",)),
    )(h, gain_slab)


def _norm_bwd(name, dhn, h, gain_slab, row, dres):
    t = h.shape[0]
    tm = min(ROW_TILE, t)
    nt = t // tm

    def body(dhn_ref, h_ref, g_ref, dres_ref, dh_ref, dhb_ref, dg_ref, acc_ref):
        i = pl.program_id(0)
        x = h_ref[...]
        r = lax.rsqrt(jnp.mean(x * x, axis=-1, keepdims=True) + RMS_EPS)
        xh = x * r
        dy = dhn_ref[...]
        gdy = dy * g_ref[...]
        dx = (gdy - xh * jnp.mean(gdy * xh, axis=-1, keepdims=True)) * r
        dh = dres_ref[...] + dx
        dh_ref[...] = dh
        dhb_ref[...] = dh.astype(BF16)
        part = jnp.sum((dy * xh).reshape(tm // 8, 8, D), axis=0)

        @pl.when(i == 0)
        def _():
            acc_ref[...] = part

        @pl.when(i > 0)
        def _():
            acc_ref[...] += part

        @pl.when(i == nt - 1)
        def _():
            dg_ref[...] = jnp.sum(acc_ref[...], axis=0, keepdims=True)

    row_spec = _spec((tm, D), lambda i: (i, 0))
    return pl.pallas_call(
        body, name=name, grid=(nt,),
        in_specs=[row_spec, row_spec, _spec((None, 1, D), lambda i: (row, 0, 0)), row_spec],
        out_specs=[row_spec, row_spec, _spec((1, D), lambda i: (0, 0))],
        out_shape=[jax.ShapeDtypeStruct((t, D), F32), jax.ShapeDtypeStruct((t, D), BF16),
                   jax.ShapeDtypeStruct((1, D), F32)],
        scratch_shapes=[pltpu.VMEM((8, D), F32)],
        compiler_params=_params(("arbitrary",)),
    )(dhn, h, gain_slab, dres)


def _final_loss(h, gain_slab, target):
    t = h.shape[0]
    tm = min(ROW_TILE, t)
    nt = t // tm

    def body(h_ref, g_ref, t_ref, dh_ref, dhb_ref, dg_ref, loss_ref, acc_ref, lacc_ref):
        i = pl.program_id(0)
        x = h_ref[...]
        g = g_ref[...]
        r = lax.rsqrt(jnp.mean(x * x, axis=-1, keepdims=True) + RMS_EPS)
        xh = x * r
        err = xh * g - t_ref[...]
        dy = err * (1.0 / D)
        gdy = dy * g
        dh = (gdy - xh * jnp.mean(gdy * xh, axis=-1, keepdims=True)) * r
        dh_ref[...] = dh
        dhb_ref[...] = dh.astype(BF16)
        part = jnp.sum((dy * xh).reshape(tm // 8, 8, D), axis=0)
        lpart = jnp.sum((err * err).reshape(tm // 8, 8, D), axis=0)

        @pl.when(i == 0)
        def _():
            acc_ref[...] = part
            lacc_ref[...] = lpart

        @pl.when(i > 0)
        def _():
            acc_ref[...] += part
            lacc_ref[...] += lpart

        @pl.when(i == nt - 1)
        def _():
            dg_ref[...] = jnp.sum(acc_ref[...], axis=0, keepdims=True)
            rows = jnp.sum(lacc_ref[...], axis=0, keepdims=True)
            loss_ref[...] = jnp.sum(rows, axis=1, keepdims=True) * (0.5 / D)

    row_spec = _spec((tm, D), lambda i: (i, 0))
    return pl.pallas_call(
        body, name="final_loss", grid=(nt,),
        in_specs=[row_spec, _spec((None, 1, D), lambda i: (R_FIN, 0, 0)), row_spec],
        out_specs=[row_spec, row_spec, _spec((1, D), lambda i: (0, 0)), _spec((1, 1), lambda i: (0, 0))],
        out_shape=[jax.ShapeDtypeStruct((t, D), F32), jax.ShapeDtypeStruct((t, D), BF16),
                   jax.ShapeDtypeStruct((1, D), F32), jax.ShapeDtypeStruct((1, 1), F32)],
        scratch_shapes=[pltpu.VMEM((8, D), F32), pltpu.VMEM((8, D), F32)],
        compiler_params=_params(("arbitrary",)),
    )(h, gain_slab, target)


def _ffn_up(name, hn, wa, gate_idx, up_idx):
    t = hn.shape[0]
    tm = min(ROW_TILE, t)

    def body(x_ref, wg_ref, wu_ref, g_ref, u_ref, a_ref):
        x = x_ref[...]
        g = _dg(x, wg_ref[...], NN)
        u = _dg(x, wu_ref[...], NN)
        g_ref[...] = g.astype(BF16)
        u_ref[...] = u.astype(BF16)
        a_ref[...] = (g * _sigmoid(g) * u).astype(BF16)

    act = _spec((None, tm, FS), lambda s, i: (s, i, 0))
    shape = jax.ShapeDtypeStruct((NSH, t, FS), BF16)
    return pl.pallas_call(
        body, name=name, grid=(NSH, t // tm),
        in_specs=[_spec((tm, D), lambda s, i: (i, 0)),
                  _spec((None, None, D, FS), lambda s, i: (s, gate_idx, 0, 0)),
                  _spec((None, None, D, FS), lambda s, i: (s, up_idx, 0, 0))],
        out_specs=[act, act, act], out_shape=[shape, shape, shape],
        compiler_params=_params(("parallel", "parallel")),
    )(hn, wa, wa)


def _ffn_down(name, a, wb, down_idx, h):
    t = h.shape[0]
    tm = min(ROW_TILE, t)
    return _mm(name, [(a, _spec((None, tm, FS), lambda i, k: (k, i, 0)),
                       wb, _spec((None, FS, D), lambda i, k: (k, down_idx, 0)))],
               grid=(t // tm, NSH), o_shape=(t, D), o_dtype=F32, o_spec=_spec((tm, D), lambda i, k: (i, 0)),
               dims=NN, kaxis=1, nk=NSH, acc_shape=(tm, D), res=(h, _spec((tm, D), lambda i, k: (i, 0))),
               scale=MACARON)


def _ffn_bwd_up(name, dhb, wb, down_idx, g, u):
    t = dhb.shape[0]
    tm = min(ROW_TILE, t)

    def body(dh_ref, wd_ref, g_ref, u_ref, dg_ref, du_ref):
        da = _dg(dh_ref[...], wd_ref[...], NT) * MACARON
        gv = g_ref[...].astype(F32)
        uv = u_ref[...].astype(F32)
        sg = _sigmoid(gv)
        du_ref[...] = (da * gv * sg).astype(BF16)
        dg_ref[...] = (da * uv * (sg * (1.0 + gv * (1.0 - sg)))).astype(BF16)

    act = _spec((None, tm, FS), lambda s, i: (s, i, 0))
    shape = jax.ShapeDtypeStruct((NSH, t, FS), BF16)
    return pl.pallas_call(
        body, name=name, grid=(NSH, t // tm),
        in_specs=[_spec((tm, D), lambda s, i: (i, 0)), _spec((None, FS, D), lambda s, i: (s, down_idx, 0)), act, act],
        out_specs=[act, act], out_shape=[shape, shape],
        compiler_params=_params(("parallel", "parallel")),
    )(dhb, wb, g, u)


def _wgrad(name, a, a_spec, b, b_spec, out_rows, out_cols, t, tt, group, slot, scale=None):
    first = isinstance(group, int)
    rows = group if first else group.shape[1]
    return _mm(name, [(a, a_spec, b, b_spec)], grid=(NSH, t // tt),
               o_shape=(NSH, rows, out_cols), o_dtype=BF16,
               o_spec=_spec((None, out_rows, out_cols), lambda s, k: (s, slot, 0)),
               dims=TN, kaxis=1, nk=t // tt, acc_shape=(out_rows, out_cols), scale=scale,
               into=None if first else group)


def _ffn_backward(tag, dh, dhb, h_in, hn, g, u, a, wa, wb, gate_idx, up_idx, down_idx, small, norm_row, grad_a, grad_b):
    t = dh.shape[0]
    tm = min(ROW_TILE, t)
    tt = tm
    dg, du = _ffn_bwd_up(tag + "_bwd_up", dhb, wb, down_idx, g, u)
    tok = _spec((tt, D), lambda s, k: (k, 0))
    hid = _spec((None, tt, FS), lambda s, k: (s, k, 0))
    grad_b = _wgrad(tag + "_dwd", a, hid, dhb, tok, FS, D, t, tt, grad_b, down_idx, scale=MACARON)
    grad_a = _wgrad(tag + "_dwg", hn, tok, dg, hid, D, FS, t, tt, grad_a, gate_idx)
    grad_a = _wgrad(tag + "_dwu", hn, tok, du, hid, D, FS, t, tt, grad_a, up_idx)
    act = _spec((None, tm, FS), lambda i, k: (k, i, 0))
    dhn = _mm(tag + "_dhn",
              [(dg, act, wa, _spec((None, None, D, FS), lambda i, k: (k, gate_idx, 0, 0))),
               (du, act, wa, _spec((None, None, D, FS), lambda i, k: (k, up_idx, 0, 0)))],
              grid=(t // tm, NSH), o_shape=(t, D), o_dtype=F32, o_spec=_spec((tm, D), lambda i, k: (i, 0)),
              dims=NT, kaxis=1, nk=NSH, acc_shape=(tm, D))
    dh_in, dhb_in, d_gain = _norm_bwd(tag + "_norm_bwd", dhn, h_in, small, norm_row, dh)
    return dh_in, dhb_in, grad_a, grad_b, d_gain


def _conv_fwd(proj_a, conv_w):
    t = proj_a.shape[0]
    tm = min(ROW_TILE, t)
    hb = tm // CONV_HALO

    def body(ab_ref, ac_ref, ax_ref, acp_ref, axp_ref, w_ref, y_ref):
        i = pl.program_id(1)
        u = ac_ref[...] * ax_ref[...]
        up = jnp.where(i > 0, acp_ref[...] * axp_ref[...], 0.0)
        ext = jnp.concatenate([up, u], axis=0)
        u1 = pltpu.roll(ext, 1, 0)[CONV_HALO:]
        u2 = pltpu.roll(ext, 2, 0)[CONV_HALO:]
        w = w_ref[...]
        conv = w[0:1] * u2 + w[1:2] * u1 + w[2:3] * u
        y_ref[...] = (ab_ref[...] * conv).astype(BF16)

    def cur(off):
        return _spec((tm, LANE), lambda j, i: (i, off + j))

    def prev(off):
        return _spec((CONV_HALO, LANE), lambda j, i: (jnp.maximum(i * hb - 1, 0), off + j))

    return pl.pallas_call(
        body, name="conv_fwd", grid=(4, t // tm),
        in_specs=[cur(0), cur(4), cur(8), prev(4), prev(8),
                  _spec((None, None, 8, LANE), lambda j, i: (j, 0, 0, 0))],
        out_specs=_spec((tm, LANE), lambda j, i: (i, j)),
        out_shape=jax.ShapeDtypeStruct((t, 512), BF16),
        compiler_params=_params(("parallel", "parallel")),
    )(proj_a, proj_a, proj_a, proj_a, proj_a, conv_w)


def _conv_bwd(proj_a, conv_w, dy):
    t = proj_a.shape[0]
    tm = min(ROW_TILE, t)
    hb = tm // CONV_HALO
    nt = t // tm

    def body(ab_ref, ac_ref, ax_ref, dy_ref, acp_ref, axp_ref, abn_ref, dyn_ref, w_ref,
             dab_ref, dac_ref, dax_ref, dw_ref, acc_ref):
        i = pl.program_id(1)
        ab, ac, ax = ab_ref[...], ac_ref[...], ax_ref[...]
        u = ac * ax
        up = jnp.where(i > 0, acp_ref[...] * axp_ref[...], 0.0)
        ext = jnp.concatenate([up, u], axis=0)
        u1 = pltpu.roll(ext, 1, 0)[CONV_HALO:]
        u2 = pltpu.roll(ext, 2, 0)[CONV_HALO:]
        w = w_ref[...]
        conv = w[0:1] * u2 + w[1:2] * u1 + w[2:3] * u
        dy_v = dy_ref[...]
        dab_ref[...] = (dy_v * conv).astype(BF16)
        dc = dy_v * ab
        dcn = jnp.where(i < nt - 1, dyn_ref[...] * abn_ref[...], 0.0)
        extn = jnp.concatenate([dc, dcn], axis=0)
        n = tm + CONV_HALO
        dc1 = pltpu.roll(extn, n - 1, 0)[:tm]
        dc2 = pltpu.roll(extn, n - 2, 0)[:tm]
        du = w[2:3] * dc + w[1:2] * dc1 + w[0:1] * dc2
        dac_ref[...] = (du * ax).astype(BF16)
        dax_ref[...] = (du * ac).astype(BF16)
        rid = lax.broadcasted_iota(jnp.int32, (8, LANE), 0)
        part = jnp.where(rid == 0, jnp.sum(dc * u2, axis=0, keepdims=True),
                         jnp.where(rid == 1, jnp.sum(dc * u1, axis=0, keepdims=True),
                                   jnp.where(rid == 2, jnp.sum(dc * u, axis=0, keepdims=True), 0.0)))

        @pl.when(i == 0)
        def _():
            acc_ref[...] = part

        @pl.when(i > 0)
        def _():
            acc_ref[...] += part

        @pl.when(i == nt - 1)
        def _():
            dw_ref[...] = acc_ref[...]

    def cur(off):
        return _spec((tm, LANE), lambda j, i: (i, off + j))

    def prev(off):
        return _spec((CONV_HALO, LANE), lambda j, i: (jnp.maximum(i * hb - 1, 0), off + j))

    def nxt(off):
        return _spec((CONV_HALO, LANE), lambda j, i: (jnp.minimum((i + 1) * hb, nt * hb - 1), off + j))

    outs = pl.pallas_call(
        body, name="conv_bwd", grid=(4, nt),
        in_specs=[cur(0), cur(4), cur(8), cur(0), prev(4), prev(8), nxt(0), nxt(0),
                  _spec((None, None, 8, LANE), lambda j, i: (j, 0, 0, 0))],
        out_specs=[_spec((tm, LANE), lambda j, i: (i, j)), _spec((tm, LANE), lambda j, i: (i, j)),
                   _spec((tm, LANE), lambda j, i: (i, j)), _spec((None, 8, LANE), lambda j, i: (j, 0, 0))],
        out_shape=[jax.ShapeDtypeStruct((t, 512), BF16), jax.ShapeDtypeStruct((t, 512), BF16),
                   jax.ShapeDtypeStruct((t, 512), BF16), jax.ShapeDtypeStruct((4, 8, LANE), F32)],
        scratch_shapes=[pltpu.VMEM((8, LANE), F32)],
        compiler_params=_params(("parallel", "arbitrary")),
    )(proj_a, proj_a, proj_a, dy, proj_a, proj_a, proj_a, dy, conv_w)
    return outs


def _log_sigmoid(z):
    return jnp.minimum(z, 0.0) - jnp.log(1.0 + jnp.exp(-jnp.abs(z)))


def _sb_masks():
    row = lax.broadcasted_iota(jnp.int32, (SBQ, SBQ), 0)
    col = lax.broadcasted_iota(jnp.int32, (SBQ, SBQ), 1)
    return row, col


def _ones_where(mask):
    return jnp.where(mask, 1.0, 0.0).astype(BF16)


def _head_mask(head):
    lane = lax.broadcasted_iota(jnp.int32, (1, LANE), 1)
    return lane >= 64 if head else lane < 64


def _sb_fwd(proj_b):
    t = proj_b.shape[0]
    nq = t // SBQ
    scale = 1.0 / math.sqrt(64.0)

    npair = SB_PAIRS
    wide = npair * LANE
    ngrp = 4 // npair
    chains = [(p, head) for p in range(npair) for head in range(2)]

    def body(q_ref, k_ref, v_ref, y_ref, l_ref, n_ref):
        grp = pl.program_id(0)
        qi = pl.program_id(1)
        row, col = _sb_masks()
        strict = col < row
        m_suffix = _ones_where(row > col)
        qh = []
        for p, head in chains:
            q_all = q_ref[:, p * LANE:(p + 1) * LANE]
            qh.append(jnp.where(_head_mask(head), q_all, jnp.zeros_like(q_all)))

        def block(kb, carry, diag):
            start = pl.multiple_of(kb * SBQ, SBQ)
            new = []
            for n, (p, head) in enumerate(chains):
                run, acc = carry[n]
                kk = k_ref[pl.ds(start, SBQ), p * LANE:(p + 1) * LANE]
                vv = v_ref[pl.ds(start, SBQ), p * LANE:(p + 1) * LANE]
                z = _dg(qh[n], kk, NT) * scale
                lb = _log_sigmoid(z)
                lk = lb - z
                if diag:
                    lk = jnp.where(strict, lk, 0.0)
                hi, lo = _split(lk)
                later = run + _dg(hi, m_suffix, NN) + _dg(lo, m_suffix, NN)
                w = jnp.exp(lb + later)
                if diag:
                    w = jnp.where(strict, w, 0.0)
                acc = acc + _dg(w.astype(BF16), vv, NN)
                run = run + jnp.sum(hi.astype(F32) + lo.astype(F32), axis=1, keepdims=True)
                new.append((run, acc))
            return tuple(new)

        zero = (jnp.zeros((SBQ, 1), F32), jnp.zeros((SBQ, LANE), F32))
        carry = block(qi, tuple(zero for _ in chains), True)

        def live(c):
            top = c[1][0][0]
            for n in range(1, len(chains)):
                top = jnp.maximum(top, c[1][n][0])
            return jnp.logical_and(c[0] < qi, jnp.max(top) > SB_DEAD)

        def step(c):
            return c[0] + 1, block(qi - 1 - c[0], c[1], False)

        count, carry = lax.while_loop(live, step, (jnp.int32(0), carry))
        n_ref[grp * nq + qi] = count.astype(F32)
        for p in range(npair):
            (run0, acc0), (run1, acc1) = carry[2 * p], carry[2 * p + 1]
            hm = _head_mask(0)
            y_ref[:, p * LANE:(p + 1) * LANE] = jnp.where(hm, acc0, acc1).astype(BF16)
            l_ref[p] = jnp.where(hm, run0, run1)

    return pl.pallas_call(
        body, name="sb_fwd", grid=(ngrp, nq),
        in_specs=[_spec((SBQ, wide), lambda g, i: (i, g)),
                  _spec((t, wide), lambda g, i: (0, ngrp + g)),
                  _spec((t, wide), lambda g, i: (0, 2 * ngrp + g))],
        out_specs=[_spec((SBQ, wide), lambda g, i: (i, g)), _spec((npair, SBQ, LANE), lambda g, i: (g, i, 0)),
                   pl.BlockSpec(memory_space=pltpu.SMEM)],
        out_shape=[jax.ShapeDtypeStruct((t, 512), BF16), jax.ShapeDtypeStruct((4, t, LANE), F32),
                   jax.ShapeDtypeStruct((ngrp * nq,), F32)],
        compiler_params=_params(("arbitrary", "arbitrary")),
    )(proj_b, proj_b, proj_b)


def _sb_bwd(proj_b, dy, ltot, nblk):
    t = proj_b.shape[0]
    nq = t // SBQ
    scale = 1.0 / math.sqrt(64.0)

    npair = SB_PAIRS
    wide = npair * LANE
    ngrp = 4 // npair
    chains = [(p, head) for p in range(npair) for head in range(2)]

    def body(q_ref, k_ref, v_ref, dy_ref, l_ref, n_ref, dq_ref, dk_ref, dv_ref, dk_acc, dv_acc):
        grp = pl.program_id(0)
        qi = pl.program_id(1)

        @pl.when(qi == 0)
        def _():
            dk_acc[...] = jnp.zeros_like(dk_acc)
            dv_acc[...] = jnp.zeros_like(dv_acc)

        row, col = _sb_masks()
        strict = col < row
        m_prefix = _ones_where(row <= col)
        m_before = _ones_where(row < col)
        qh, doh, ltot_h = [], [], []
        for p, head in chains:
            pl_ = slice(p * LANE, (p + 1) * LANE)
            q_all = q_ref[:, pl_]
            do_all = dy_ref[:, pl_].astype(BF16)
            hm = _head_mask(head)
            qh.append(jnp.where(hm, q_all, jnp.zeros_like(q_all)))
            doh.append(jnp.where(hm, do_all, jnp.zeros_like(do_all)))
            ltot_h.append(l_ref[p][:, head * 64:head * 64 + 1])

        def block(kb, carry, diag):
            start = pl.multiple_of(kb * SBQ, SBQ)
            new = []
            for p in range(npair):
                pl_ = slice(p * LANE, (p + 1) * LANE)
                kk = k_ref[pl.ds(start, SBQ), pl_]
                vv = v_ref[pl.ds(start, SBQ), pl_]
                dk_blk, dv_blk = None, None
                for head in range(2):
                    n = 2 * p + head
                    seen, dseen, dq = carry[n]
                    z = _dg(qh[n], kk, NT) * scale
                    lb = _log_sigmoid(z)
                    lk = lb - z
                    if diag:
                        lk = jnp.where(strict, lk, 0.0)
                    hi, lo = _split(lk)
                    later = (ltot_h[n] - seen) - (_dg(hi, m_prefix, NN) + _dg(lo, m_prefix, NN))
                    w = jnp.exp(lb + later)
                    if diag:
                        w = jnp.where(strict, w, 0.0)
                    da = w * _dg(doh[n], vv, NT)
                    dv_h = _dg(w.astype(BF16), doh[n], TN)
                    dah, dal = _split(da)
                    dlk = dseen + _dg(dah, m_before, NN) + _dg(dal, m_before, NN)
                    sig = jnp.exp(lb)
                    dz = (da * (1.0 - sig) - dlk * sig) * scale
                    if diag:
                        dz = jnp.where(strict, dz, 0.0)
                    dzb = dz.astype(BF16)
                    dk_h = _dg(dzb, qh[n], TN)
                    dk_blk = dk_h if dk_blk is None else dk_blk + dk_h
                    dv_blk = dv_h if dv_blk is None else dv_blk + dv_h
                    new.append((seen + jnp.sum(hi.astype(F32) + lo.astype(F32), axis=1, keepdims=True),
                                dseen + jnp.sum(da, axis=1, keepdims=True),
                                dq + _dg(dzb, kk, NN)))
                dk_acc[pl.ds(start, SBQ), pl_] += dk_blk
                dv_acc[pl.ds(start, SBQ), pl_] += dv_blk
            return tuple(new)

        zero = (jnp.zeros((SBQ, 1), F32), jnp.zeros((SBQ, 1), F32), jnp.zeros((SBQ, LANE), F32))
        first = qi - n_ref[grp * nq + qi].astype(jnp.int32)
        carry = lax.fori_loop(first, qi, lambda kb, c: block(kb, c, False), tuple(zero for _ in chains))
        carry = block(qi, carry, True)
        for p in range(npair):
            dq_ref[:, p * LANE:(p + 1) * LANE] = jnp.where(_head_mask(0), carry[2 * p][2], carry[2 * p + 1][2]).astype(BF16)

        @pl.when(qi == nq - 1)
        def _():
            dk_ref[...] = dk_acc[...].astype(BF16)
            dv_ref[...] = dv_acc[...].astype(BF16)

    full = jax.ShapeDtypeStruct((t, 512), BF16)
    return pl.pallas_call(
        body, name="sb_bwd", grid=(ngrp, nq),
        in_specs=[_spec((SBQ, wide), lambda g, i: (i, g)),
                  _spec((t, wide), lambda g, i: (0, ngrp + g)),
                  _spec((t, wide), lambda g, i: (0, 2 * ngrp + g)),
                  _spec((SBQ, wide), lambda g, i: (i, ngrp + g)),
                  _spec((npair, SBQ, LANE), lambda g, i: (g, i, 0)),
                  pl.BlockSpec(memory_space=pltpu.SMEM)],
        out_specs=[_spec((SBQ, wide), lambda g, i: (i, g)),
                   _spec((t, wide), lambda g, i: (0, g)), _spec((t, wide), lambda g, i: (0, g))],
        out_shape=[full, full, full],
        scratch_shapes=[pltpu.VMEM((t, wide), F32), pltpu.VMEM((t, wide), F32)],
        compiler_params=_params(("parallel", "arbitrary")),
    )(proj_b, proj_b, proj_b, dy, ltot, nblk)


def _hgrn_gates(qr, fr, c0, c1):
    mx = jnp.maximum(c0, c1)
    e0, e1 = jnp.exp(c0 - mx), jnp.exp(c1 - mx)
    lb = e1 / (e0 + e1)
    sx = _sigmoid(fr)
    f = lb + (1.0 - lb) * sx
    k = (1.0 - lb) * (1.0 - sx)
    sq = _sigmoid(qr)
    return lb, sx, f, k, sq, qr * sq


def _hgrn_decays(f):
    row = lax.broadcasted_iota(jnp.int32, (CHUNK, CHUNK), 0)
    col = lax.broadcasted_iota(jnp.int32, (CHUNK, CHUNK), 1)
    tril = col <= row
    hi, lo = _split(jnp.log(f))
    trib = _ones_where(tril)
    b = _dg(trib, hi, NN) + _dg(trib, lo, NN)
    bm = b[CHUNK // 2 - 1:CHUNK // 2]
    bl = b[CHUNK - 1:CHUNK]
    return tril, col >= row, b, bm, bl


def _hgrn_fwd(proj_c, small):
    t = proj_c.shape[1]
    nc = t // CHUNK
    nh = D // HD

    hps = HGRN_HPS
    wide = hps * HD

    def body(p_ref, c0_ref, c1_ref, gam_ref, o_ref, y_ref, sst_ref, st_ref):
        c = pl.program_id(1)

        @pl.when(c == 0)
        def _():
            st_ref[...] = jnp.zeros_like(st_ref)

        for j in range(hps):
            ln = slice(j * HD, (j + 1) * HD)
            st0 = st_ref[j]
            sst_ref[j] = st0
            qr, fr, v, g = p_ref[0, :, ln], p_ref[1, :, ln], p_ref[2, :, ln], p_ref[3, :, ln]
            lb, sx, f, k, sq, q = _hgrn_gates(qr, fr, c0_ref[:, ln], c1_ref[:, ln])
            tril, _, b, bm, bl = _hgrn_decays(f)
            qd = q * jnp.exp(b)
            qt = q * jnp.exp(b - bm)
            kt = k * jnp.exp(bm - b)
            kl = k * jnp.exp(bl - b)
            vb = v.astype(BF16)
            att = jnp.where(tril, _dot3(qt, kt, NT), 0.0)
            o = _dg(qd.astype(BF16), st0.astype(BF16), NT) + _dg(att.astype(BF16), vb, NN)
            st_ref[j] = st0 * jnp.exp(bl) + _dg(vb, kl.astype(BF16), TN)
            o_ref[:, ln] = o
            r = lax.rsqrt(jnp.mean(o * o, axis=-1, keepdims=True) + RMS_EPS)
            y_ref[:, ln] = (o * r * gam_ref[...] * (g * _sigmoid(g))).astype(BF16)

    return pl.pallas_call(
        body, name="hgrn_fwd", grid=(nh // hps, nc),
        in_specs=[_spec((4, CHUNK, wide), lambda h, c: (0, c, h)),
                  _spec((None, 1, wide), lambda h, c: (R_CLB, 0, h)),
                  _spec((None, 1, wide), lambda h, c: (R_CLB + 1, 0, h)),
                  _spec((None, 1, HD), lambda h, c: (R_GAM, 0, 0))],
        out_specs=[_spec((CHUNK, wide), lambda h, c: (c, h)), _spec((CHUNK, wide), lambda h, c: (c, h)),
                   _spec((None, hps, HD, HD), lambda h, c: (c, h, 0, 0))],
        out_shape=[jax.ShapeDtypeStruct((t, D), F32), jax.ShapeDtypeStruct((t, D), BF16),
                   jax.ShapeDtypeStruct((nc, nh, HD, HD), F32)],
        scratch_shapes=[pltpu.VMEM((hps, HD, HD), F32)],
        compiler_params=_params(("parallel", "arbitrary")),
    )(proj_c, small, small, small)


def _hgrn_bwd(proj_c, small, o, sst, dyc):
    t = proj_c.shape[1]
    nc = t // CHUNK
    nh = D // HD

    hps = HGRN_HPS
    wide = hps * HD
    ng = nh // hps

    def body(p_ref, c0_ref, c1_ref, gam_ref, o_ref, sst_ref, dy_ref, dp_ref, dclb_ref, dgam_ref, dst_ref, dlb_acc, dgam_acc):
        group = pl.program_id(0)
        step = pl.program_id(1)

        @pl.when(step == 0)
        def _():
            dst_ref[...] = jnp.zeros_like(dst_ref)
            dlb_acc[...] = jnp.zeros_like(dlb_acc)

        @pl.when((step == 0) & (group == 0))
        def _():
            dgam_acc[...] = jnp.zeros_like(dgam_acc)

        gam = gam_ref[...]
        for j in range(hps):
            ln = slice(j * HD, (j + 1) * HD)
            st0 = sst_ref[j]
            dst1 = dst_ref[j]
            qr, fr, v, g = p_ref[0, :, ln], p_ref[1, :, ln], p_ref[2, :, ln], p_ref[3, :, ln]
            lb, sx, f, k, sq, q = _hgrn_gates(qr, fr, c0_ref[:, ln], c1_ref[:, ln])
            tril, triu, b, bm, bl = _hgrn_decays(f)
            eb = jnp.exp(b)
            e_qt = jnp.exp(b - bm)
            e_kt = jnp.exp(bm - b)
            e_kl = jnp.exp(bl - b)
            e_bl = jnp.exp(bl)
            qd, qt, kt, kl = q * eb, q * e_qt, k * e_kt, k * e_kl
            ov = o_ref[:, ln]
            r = lax.rsqrt(jnp.mean(ov * ov, axis=-1, keepdims=True) + RMS_EPS)
            oh = ov * r
            sg = _sigmoid(g)
            dy = dy_ref[:, ln]
            dp_ref[3, :, ln] = (dy * oh * gam * (sg * (1.0 + g * (1.0 - sg)))).astype(BF16)
            dyv = dy * (g * sg)
            dgam_acc[...] += jnp.sum(dyv * oh, axis=0, keepdims=True)
            gdy = dyv * gam
            do = (gdy - oh * jnp.mean(gdy * oh, axis=-1, keepdims=True)) * r
            dob, vb = do.astype(BF16), v.astype(BF16)
            st0b, dst1b = st0.astype(BF16), dst1.astype(BF16)
            st1 = st0 * e_bl + _dg(vb, kl.astype(BF16), TN)
            att = jnp.where(tril, _dot3(qt, kt, NT), 0.0)
            datt = jnp.where(tril, _dg(dob, vb, NT), 0.0)
            dv = _dg(att.astype(BF16), dob, TN) + _dg(kl.astype(BF16), dst1b, NT)
            dq = _dot3(datt, kt, NN) * e_qt + _dg(dob, st0b, NN) * eb
            dk = _dot3(datt, qt, TN) * e_kt + _dg(vb, dst1b, NN) * e_kl
            db = q * dq - k * dk
            last = lax.broadcasted_iota(jnp.int32, (CHUNK, 1), 0) == CHUNK - 1
            db = db + jnp.where(last, jnp.sum(dst1 * st1, axis=0, keepdims=True), 0.0)
            dbh, dbl = _split(db)
            triub = _ones_where(triu)
            dlf = _dg(triub, dbh, NN) + _dg(triub, dbl, NN)
            dst_ref[j] = dst1 * e_bl + _dg(dob, qd.astype(BF16), TN)
            dp_ref[0, :, ln] = (dq * (sq * (1.0 + qr * (1.0 - sq)))).astype(BF16)
            tmp = dlf / f - dk
            dp_ref[1, :, ln] = (tmp * (1.0 - lb) * sx * (1.0 - sx)).astype(BF16)
            dp_ref[2, :, ln] = dv.astype(BF16)
            dlb_acc[:, ln] += jnp.sum((1.0 - sx) * tmp, axis=0, keepdims=True)

            @pl.when(step == nc - 1)
            def _():
                d1 = dlb_acc[:, ln] * lb * (1.0 - lb)
                dclb_ref[:, ln] = jnp.where(lax.broadcasted_iota(jnp.int32, (2, HD), 0) == 0, -d1, d1)

        @pl.when((step == nc - 1) & (group == ng - 1))
        def _():
            dgam_ref[...] = dgam_acc[...]

    rev = lambda h, s: (nc - 1 - s, h)
    return pl.pallas_call(
        body, name="hgrn_bwd", grid=(ng, nc),
        in_specs=[_spec((4, CHUNK, wide), lambda h, s: (0, nc - 1 - s, h)),
                  _spec((None, 1, wide), lambda h, s: (R_CLB, 0, h)),
                  _spec((None, 1, wide), lambda h, s: (R_CLB + 1, 0, h)),
                  _spec((None, 1, HD), lambda h, s: (R_GAM, 0, 0)),
                  _spec((CHUNK, wide), rev),
                  _spec((None, hps, HD, HD), lambda h, s: (nc - 1 - s, h, 0, 0)),
                  _spec((CHUNK, wide), rev)],
        out_specs=[_spec((4, CHUNK, wide), lambda h, s: (0, nc - 1 - s, h)),
                   _spec((2, wide), lambda h, s: (0, h)),
                   _spec((1, HD), lambda h, s: (0, 0))],
        out_shape=[jax.ShapeDtypeStruct((4, t, D), BF16), jax.ShapeDtypeStruct((2, D), F32),
                   jax.ShapeDtypeStruct((1, HD), F32)],
        scratch_shapes=[pltpu.VMEM((hps, HD, HD), F32), pltpu.VMEM((1, wide), F32), pltpu.VMEM((1, HD), F32)],
        compiler_params=_params(("arbitrary", "arbitrary")),
    )(proj_c, small, small, small, o, sst, dyc)


def _adamw(name, w, g, m, v):
    rows, cols = w.shape
    br = rows
    for cand in (512, 352, 256):
        if rows % cand == 0:
            br = cand
            break
    c1 = 1.0 - ADAM_B1 ** ADAM_STEP
    c2 = 1.0 - ADAM_B2 ** ADAM_STEP

    def body(w_ref, g_ref, m_ref, v_ref, d_ref, mo_ref, vo_ref):
        gv = g_ref[...]
        mn = ADAM_B1 * m_ref[...] + (1.0 - ADAM_B1) * gv
        vn = ADAM_B2 * v_ref[...] + (1.0 - ADAM_B2) * (gv * gv)
        mo_ref[...] = mn
        vo_ref[...] = vn
        d_ref[...] = -ADAM_LR * ((mn / c1) / (jnp.sqrt(vn / c2) + ADAM_EPS) + ADAM_WD * w_ref[...])

    blk = _spec((br, cols), lambda i: (i, 0))
    shape = jax.ShapeDtypeStruct((rows, cols), F32)
    return pl.pallas_call(
        body, name=name, grid=(rows // br,), in_specs=[blk] * 4, out_specs=[blk] * 3, out_shape=[shape] * 3,
        compiler_params=_params(("parallel",)),
    )(w, g, m, v)


def _place():
    x, y, c = lax.axis_index("x"), lax.axis_index("y"), lax.axis_index("c")
    chips = [(1 - x, y), (x, 1 - y), (1 - x, 1 - y)]
    return x, y, c, chips


def _gather_weights(groups):
    ng = len(groups)

    def body(*refs):
        ins, outs = refs[:ng], refs[ng:2 * ng]
        send, recv = refs[2 * ng:]
        x, y, c, chips = _place()
        sibling = (x, y, 1 - c)

        def half(gi, chip, hc):
            return outs[gi].at[2 * chip[0] + chip[1], hc]

        def copy(gi, k, src, dst, to):
            return pltpu.make_async_remote_copy(src_ref=src, dst_ref=dst, send_sem=send.at[6 * gi + k],
                                                recv_sem=recv.at[6 * gi + k], device_id=to, device_id_type=MESH)

        first = [copy(gi, j, ins[gi].at[c], half(gi, (x, y), c), (*chip, c))
                 for gi in range(ng) for j, chip in enumerate(chips)]
        for cp in first:
            cp.start()
        passed = []
        for gi in range(ng):
            for j, chip in enumerate(chips):
                copy(gi, j, half(gi, chip, c), half(gi, chip, c), sibling).wait_recv()
                fwd = copy(gi, 3 + j, half(gi, chip, c), half(gi, chip, c), sibling)
                fwd.start()
                passed.append(fwd)
        for gi in range(ng):
            for j, chip in enumerate(chips):
                copy(gi, 3 + j, half(gi, chip, 1 - c), half(gi, chip, 1 - c), sibling).wait_recv()
        for cp in first + passed:
            cp.wait_send()

    outs = pl.pallas_call(
        body, name="gather_weights",
        in_specs=[ANY] * ng, out_specs=[ANY] * ng,
        out_shape=[jax.ShapeDtypeStruct((NSH,) + g.shape, g.dtype) for g in groups],
        scratch_shapes=[pltpu.SemaphoreType.DMA((6 * ng,)), pltpu.SemaphoreType.DMA((6 * ng,))],
        compiler_params=pltpu.CompilerParams(has_side_effects=True),
    )(*groups)
    me = 2 * lax.axis_index("x") + lax.axis_index("y")
    return [lax.dynamic_update_index_in_dim(o, g, me, 0) for o, g in zip(outs, groups)]


def _swap_halves(groups):
    ng = len(groups)

    def body(*refs):
        ins, outs = refs[:ng], refs[ng:2 * ng]
        send, recv = refs[2 * ng:]
        x, y, c, _ = _place()
        cps = []
        for gi in range(ng):
            half = ins[gi].shape[1] // 2
            rows = pl.ds(pl.multiple_of((1 - c) * half, half), half)
            cps.append(pltpu.make_async_remote_copy(src_ref=ins[gi].at[:, rows], dst_ref=outs[gi], send_sem=send.at[gi],
                                                    recv_sem=recv.at[gi], device_id=(x, y, 1 - c), device_id_type=MESH))
        for cp in cps:
            cp.start()
        for cp in cps:
            cp.wait()

    return pl.pallas_call(
        body, name="grad_swap_halves", in_specs=[ANY] * ng, out_specs=[ANY] * ng,
        out_shape=[jax.ShapeDtypeStruct((g.shape[0], g.shape[1] // 2, g.shape[2]), g.dtype) for g in groups],
        scratch_shapes=[pltpu.SemaphoreType.DMA((ng,)), pltpu.SemaphoreType.DMA((ng,))],
        compiler_params=pltpu.CompilerParams(has_side_effects=True),
    )(*groups)


def _send_to_owners(groups):
    ng = len(groups)

    def body(*refs):
        ins, outs = refs[:ng], refs[ng:2 * ng]
        send, recv = refs[2 * ng:]
        x, y, c, chips = _place()
        cps = [pltpu.make_async_remote_copy(src_ref=ins[gi].at[2 * chip[0] + chip[1]], dst_ref=outs[gi].at[j],
                                            send_sem=send.at[3 * gi + j], recv_sem=recv.at[3 * gi + j],
                                            device_id=(*chip, c), device_id_type=MESH)
               for gi in range(ng) for j, chip in enumerate(chips)]
        for cp in cps:
            cp.start()
        for cp in cps:
            cp.wait()

    return pl.pallas_call(
        body, name="grad_send_to_owners", in_specs=[ANY] * ng, out_specs=[ANY] * ng,
        out_shape=[jax.ShapeDtypeStruct((3,) + g.shape[1:], g.dtype) for g in groups],
        scratch_shapes=[pltpu.SemaphoreType.DMA((3 * ng,)), pltpu.SemaphoreType.DMA((3 * ng,))],
        compiler_params=pltpu.CompilerParams(has_side_effects=True),
    )(*groups)


def _share_halves(groups):
    ng = len(groups)

    def body(*refs):
        ins, outs = refs[:ng], refs[ng:2 * ng]
        send, recv = refs[2 * ng:]
        x, y, c, _ = _place()
        cps = [pltpu.make_async_remote_copy(src_ref=ins[gi], dst_ref=outs[gi].at[c], send_sem=send.at[gi],
                                            recv_sem=recv.at[gi], device_id=(x, y, 1 - c), device_id_type=MESH)
               for gi in range(ng)]
        for cp in cps:
            cp.start()
        for gi in range(ng):
            pltpu.make_async_remote_copy(src_ref=ins[gi], dst_ref=outs[gi].at[1 - c], send_sem=send.at[gi],
                                         recv_sem=recv.at[gi], device_id=(x, y, 1 - c), device_id_type=MESH).wait_recv()
        for cp in cps:
            cp.wait_send()

    outs = pl.pallas_call(
        body, name="grad_share_halves", in_specs=[ANY] * ng, out_specs=[ANY] * ng,
        out_shape=[jax.ShapeDtypeStruct((2,) + g.shape, g.dtype) for g in groups],
        scratch_shapes=[pltpu.SemaphoreType.DMA((ng,)), pltpu.SemaphoreType.DMA((ng,))],
        compiler_params=pltpu.CompilerParams(has_side_effects=True),
    )(*groups)
    c = lax.axis_index("c")
    return [lax.dynamic_update_index_in_dim(o, g, c, 0) for o, g in zip(outs, groups)]


def _pair_sum(name, g, got, c_idx):
    nsh, rows, cols = got.shape
    br = rows // 4

    def body(c_ref, a_ref, b_ref, o_ref):
        o_ref[...] = (a_ref[...].astype(F32) + b_ref[...].astype(F32)).astype(BF16)

    return pl.pallas_call(
        body, name=name,
        grid_spec=pltpu.PrefetchScalarGridSpec(
            num_scalar_prefetch=1, grid=(nsh, 4),
            in_specs=[pl.BlockSpec((None, br, cols), lambda q, i, cr: (q, 4 * cr[0] + i, 0)),
                      pl.BlockSpec((None, br, cols), lambda q, i, cr: (q, i, 0))],
            out_specs=pl.BlockSpec((None, br, cols), lambda q, i, cr: (q, i, 0))),
        out_shape=jax.ShapeDtypeStruct((nsh, rows, cols), BF16),
        compiler_params=_params(("parallel", "parallel")),
    )(c_idx, g, got)


def _owner_sum(name, own, got, p_idx):
    _, rows, cols = own.shape
    br = rows // 4

    def body(p_ref, a_ref, b_ref, o_ref):
        o_ref[...] = ((a_ref[...].astype(F32) + b_ref[0].astype(F32)) + b_ref[1].astype(F32)) + b_ref[2].astype(F32)

    return pl.pallas_call(
        body, name=name,
        grid_spec=pltpu.PrefetchScalarGridSpec(
            num_scalar_prefetch=1, grid=(4,),
            in_specs=[pl.BlockSpec((None, br, cols), lambda i, pr: (pr[0], i, 0)),
                      pl.BlockSpec((3, br, cols), lambda i, pr: (0, i, 0))],
            out_specs=pl.BlockSpec((br, cols), lambda i, pr: (i, 0))),
        out_shape=jax.ShapeDtypeStruct((rows, cols), F32),
        compiler_params=_params(("parallel",)),
    )(p_idx, own, got)


def _sum_small(slab):
    def body(in_ref, out_ref, all_ref, send, recv):
        x, y, c, _ = _place()
        me = 4 * x + 2 * y + c
        all_ref[me] = in_ref[...]
        cps = []
        for k in range(1, 8):
            peer = (x ^ (k >> 2), y ^ ((k >> 1) & 1), c ^ (k & 1))
            cps.append(pltpu.make_async_remote_copy(src_ref=in_ref, dst_ref=all_ref.at[me], send_sem=send.at[k - 1],
                                                    recv_sem=recv.at[k - 1], device_id=peer, device_id_type=MESH))
        for cp in cps:
            cp.start()
        for cp in cps:
            cp.wait()
        total = all_ref[0]
        for d in range(1, 8):
            total = total + all_ref[d]
        out_ref[...] = total

    return pl.pallas_call(
        body, name="sum_small",
        in_specs=[pl.BlockSpec(memory_space=pltpu.VMEM)], out_specs=pl.BlockSpec(memory_space=pltpu.VMEM),
        out_shape=jax.ShapeDtypeStruct(slab.shape, F32),
        scratch_shapes=[pltpu.VMEM((8,) + slab.shape, F32), pltpu.SemaphoreType.DMA((7,)), pltpu.SemaphoreType.DMA((7,))],
        compiler_params=pltpu.CompilerParams(has_side_effects=True),
    )(slab)


def _local_step(x, target, wa, wb, wc, conv_w, small):
    t = x.shape[0]
    tm = min(ROW_TILE, t)
    nt = t // tm
    tok_si = _spec((tm, D), lambda s, i: (i, 0))

    def ffn_forward(tag, h, norm_row, gate_idx, up_idx, down_idx):
        hn = _norm_fwd(tag + "_norm", h, small, norm_row)
        g, u, a = _ffn_up(tag + "_up", hn, wa, gate_idx, up_idx)
        return _ffn_down(tag + "_down", a, wb, down_idx, h), (h, hn, g, u, a)

    def out_proj(name, y, blk, h):
        return _mm(name, [(y, _spec((tm, 256), lambda i, k: (i, k)), wb, _spec((None, 256, D), lambda i, k: (k, blk, 0)))],
                   grid=(nt, NSH), o_shape=(t, D), o_dtype=F32, o_spec=_spec((tm, D), lambda i, k: (i, 0)),
                   dims=NN, kaxis=1, nk=NSH, acc_shape=(tm, D), res=(h, _spec((tm, D), lambda i, k: (i, 0))))

    def out_proj_bwd(tag, dhb, y, blk, grad_b):
        dy = _mm(tag + "_dy", [(dhb, tok_si, wb, _spec((None, 256, D), lambda s, i: (s, blk, 0)))],
                 grid=(NSH, nt), o_shape=(t, D), o_dtype=F32, o_spec=_spec((tm, 256), lambda s, i: (i, s)),
                 dims=NT, kaxis=1, nk=1)
        grad_b = _wgrad(tag + "_dwout", y, _spec((tm, 256), lambda s, k: (k, s)), dhb, _spec((tm, D), lambda s, k: (k, 0)),
                        256, D, t, tm, grad_b, blk)
        return dy, grad_b

    h0 = x
    h1, pre0 = ffn_forward("pre0", h0, R_PRE, 0, 2, 0)
    hn_ab = _norm_fwd("mix0_norm", h1, small, R_MIX)
    proj_a = _mm("ab_proj_a", [(hn_ab, tok_si, wc, _spec((None, D, 768), lambda s, i: (s, 0, 0)))],
                 grid=(2, nt), o_shape=(t, 1536), o_dtype=F32, o_spec=_spec((tm, 768), lambda s, i: (i, s)),
                 dims=NN, kaxis=1, nk=1)
    proj_b = _mm("ab_proj_b", [(hn_ab, tok_si, wc, _spec((None, D, 768), lambda s, i: (s + 2, 0, 0)))],
                 grid=(2, nt), o_shape=(t, 1536), o_dtype=BF16, o_spec=_spec((tm, 768), lambda s, i: (i, s)),
                 dims=NN, kaxis=1, nk=1)
    y_a = _conv_fwd(proj_a, conv_w)
    y_b, ltot, nblk = _sb_fwd(proj_b)
    y_ab = jnp.concatenate([y_a, y_b], axis=1)
    h2 = out_proj("ab_out", y_ab, B_ABOUT // 256, h1)
    h3, post0 = ffn_forward("post0", h2, R_POST, 4, 6, 2)
    h4, pre1 = ffn_forward("pre1", h3, R_PRE + 1, 1, 3, 1)
    hn_c = _norm_fwd("mix1_norm", h4, small, R_MIX + 1)
    proj_c = _mm("c_proj", [(hn_c, tok_si, wb, _spec((None, D, D), lambda s, i: (s, B_CIN // D, 0)))],
                 grid=(NSH, nt), o_shape=(NSH, t, D), o_dtype=F32, o_spec=_spec((None, tm, D), lambda s, i: (s, i, 0)),
                 dims=NN, kaxis=1, nk=1)
    o_c, y_c, sst = _hgrn_fwd(proj_c, small)
    h5 = out_proj("c_out", y_c, B_COUT // 256, h4)
    h6, post1 = ffn_forward("post1", h5, R_POST + 1, 5, 7, 3)
    dh, dhb, d_fin, loss = _final_loss(h6, small, target)

    dh, dhb, grad_a, grad_b, dn_post1 = _ffn_backward("post1", dh, dhb, *post1, wa, wb, 5, 7, 3, small, R_POST + 1,
                                                      8 * D, B_ROWS)
    dy_c, grad_b = out_proj_bwd("c", dhb, y_c, B_COUT // 256, grad_b)
    dproj_c, d_clb, d_gam = _hgrn_bwd(proj_c, small, o_c, sst, dy_c)
    grad_b = _wgrad("c_dwin", hn_c, _spec((tm, D), lambda s, k: (k, 0)), dproj_c, _spec((None, tm, D), lambda s, k: (s, k, 0)),
                    D, D, t, tm, grad_b, B_CIN // D)
    dhn = _mm("c_dhn", [(dproj_c, _spec((None, tm, D), lambda i, k: (k, i, 0)),
                         wb, _spec((None, D, D), lambda i, k: (k, B_CIN // D, 0)))],
              grid=(nt, NSH), o_shape=(t, D), o_dtype=F32, o_spec=_spec((tm, D), lambda i, k: (i, 0)),
              dims=NT, kaxis=1, nk=NSH, acc_shape=(tm, D))
    dh, dhb, dn_mix1 = _norm_bwd("mix1_norm_bwd", dhn, h4, small, R_MIX + 1, dh)
    dh, dhb, grad_a, grad_b, dn_pre1 = _ffn_backward("pre1", dh, dhb, *pre1, wa, wb, 1, 3, 1, small, R_PRE + 1,
                                                     grad_a, grad_b)
    dh, dhb, grad_a, grad_b, dn_post0 = _ffn_backward("post0", dh, dhb, *post0, wa, wb, 4, 6, 2, small, R_POST,
                                                      grad_a, grad_b)
    dy_ab, grad_b = out_proj_bwd("ab", dhb, y_ab, B_ABOUT // 256, grad_b)
    dab, dac, dax, d_conv = _conv_bwd(proj_a, conv_w, dy_ab)
    dq, dk, dv = _sb_bwd(proj_b, dy_ab, ltot, nblk)
    dproj_ab = jnp.concatenate([dab, dac, dax, dq, dk, dv], axis=1)
    grad_c = _wgrad("ab_dwin", hn_ab, _spec((tm, D), lambda s, k: (k, 0)), dproj_ab, _spec((tm, 768), lambda s, k: (k, s)),
                    D, 768, t, tm, D, 0)
    dhn = _mm("ab_dhn", [(dproj_ab, _spec((tm, 768), lambda i, k: (i, k)), wc, _spec((None, D, 768), lambda i, k: (k, 0, 0)))],
              grid=(nt, NSH), o_shape=(t, D), o_dtype=F32, o_spec=_spec((tm, D), lambda i, k: (i, 0)),
              dims=NT, kaxis=1, nk=NSH, acc_shape=(tm, D))
    dh, dhb, dn_mix0 = _norm_bwd("mix0_norm_bwd", dhn, h1, small, R_MIX, dh)
    dh, dhb, grad_a, grad_b, dn_pre0 = _ffn_backward("pre0", dh, dhb, *pre0, wa, wb, 0, 2, 0, small, R_PRE,
                                                     grad_a, grad_b)
    zero = jnp.zeros((1, D), F32)
    conv_rows = jnp.pad(jnp.transpose(d_conv[:, :3, :], (1, 0, 2)).reshape(3, 512), ((0, 0), (0, D - 512)))
    small_grad = jnp.concatenate([
        dn_pre0, dn_pre1, dn_mix0, dn_mix1, dn_post0, dn_post1, d_clb, d_fin,
        jnp.pad(d_gam, ((0, 0), (0, D - HD))), conv_rows,
        jnp.pad(loss, ((0, 0), (0, D - 1))), zero, zero], axis=0)
    return dh, grad_a, grad_b, grad_c, small_grad


def _small_slab(rows):
    parts = [jnp.pad(r.astype(F32), ((0, 0), (0, D - r.shape[1]))) for r in rows]
    slab = jnp.concatenate(parts, axis=0)
    return jnp.pad(slab, ((0, SMALL_ROWS - slab.shape[0]), (0, 0)))


def kernel(x, ffn_pre_norm, ffn_pre_w_gate, ffn_pre_w_up, ffn_pre_w_down, mix_norm, ffn_post_norm, ffn_post_w_gate, ffn_post_w_up, ffn_post_w_down, ab_w_in, ab_conv_w, ab_w_out, c_w_in, c_lower_bounds, c_out_norm, c_w_out, final_norm, loss_target, m_ffn_pre_norm, m_ffn_pre_w_gate, m_ffn_pre_w_up, m_ffn_pre_w_down, m_mix_norm, m_ffn_post_norm, m_ffn_post_w_gate, m_ffn_post_w_up, m_ffn_post_w_down, m_ab_w_in, m_ab_conv_w, m_ab_w_out, m_c_w_in, m_c_lower_bounds, m_c_out_norm, m_c_w_out, m_final_norm, v_ffn_pre_norm, v_ffn_pre_w_gate, v_ffn_pre_w_up, v_ffn_pre_w_down, v_mix_norm, v_ffn_post_norm, v_ffn_post_w_gate, v_ffn_post_w_up, v_ffn_post_w_down, v_ab_w_in, v_ab_conv_w, v_ab_w_out, v_c_w_in, v_c_lower_bounds, v_c_out_norm, v_c_w_out, v_final_norm):
    t = x.shape[1]
    xi, yi, ci = lax.axis_index("x"), lax.axis_index("y"), lax.axis_index("c")
    p_idx = (2 * xi + yi).astype(jnp.int32).reshape(1)
    c_idx = ci.astype(jnp.int32).reshape(1)

    ga = jnp.concatenate([ffn_pre_w_gate, ffn_pre_w_up, ffn_post_w_gate, ffn_post_w_up], axis=0).astype(BF16)
    gb = jnp.concatenate([ffn_pre_w_down.reshape(2 * FS, D), ffn_post_w_down.reshape(2 * FS, D),
                          ab_w_out[0], c_w_in[0], c_w_out[0]], axis=0).astype(BF16)
    gc = ab_w_in[0].astype(BF16)
    gd = jnp.pad(ab_conv_w[0], ((0, 5), (0, 0)))
    wa, wb, wc, conv_w = _gather_weights([ga.reshape(2, 4 * D, FS), gb.reshape(2, B_ROWS // 2, D),
                                          gc.reshape(2, D // 2, 768), jnp.stack([gd, jnp.zeros_like(gd)])])
    wa = wa.reshape(NSH, 8, D, FS)
    wb = wb.reshape(NSH, B_ROWS, D)
    wc = wc.reshape(NSH, D, 768)

    small = _small_slab([ffn_pre_norm, mix_norm, ffn_post_norm, c_lower_bounds, final_norm.reshape(1, D), c_out_norm])
    small = small.reshape(SMALL_ROWS, 1, D)

    grad_x, grad_a, grad_b, grad_c, small_grad = _local_step(x[0], loss_target[0], wa, wb, wc, conv_w, small)

    full = [grad_a, grad_b, grad_c]
    got = _swap_halves(full)
    pair = [_pair_sum("grad_pair_sum_%d" % n, g, r, c_idx) for n, (g, r) in enumerate(zip(full, got))]
    got = _send_to_owners(pair)
    mine = [_owner_sum("grad_owner_sum_%d" % n, g, r, p_idx) for n, (g, r) in enumerate(zip(pair, got))]
    ra, rb, rc = _share_halves(mine)
    ra = ra.reshape(8, D, FS)
    rb = rb.reshape(B_ROWS, D)
    rc = rc.reshape(D, 768)
    small_sum = _sum_small(small_grad)

    my_conv = lax.dynamic_slice(small_sum[R_CONV:R_CONV + 3], (0, (2 * xi + yi) * 128), (3, 128))
    grads = {
        "ffn_pre_norm": small_sum[R_PRE:R_PRE + 2], "ffn_pre_w_gate": ra[0:2], "ffn_pre_w_up": ra[2:4],
        "ffn_pre_w_down": rb[0:2 * FS].reshape(2, FS, D), "mix_norm": small_sum[R_MIX:R_MIX + 2],
        "ffn_post_norm": small_sum[R_POST:R_POST + 2], "ffn_post_w_gate": ra[4:6], "ffn_post_w_up": ra[6:8],
        "ffn_post_w_down": rb[2 * FS:4 * FS].reshape(2, FS, D), "ab_w_in": rc.reshape(1, D, 768),
        "ab_conv_w": my_conv.reshape(1, 3, 128), "ab_w_out": rb[B_ABOUT:B_ABOUT + 256].reshape(1, 256, D),
        "c_w_in": rb[B_CIN:B_CIN + D].reshape(1, D, D), "c_lower_bounds": small_sum[R_CLB:R_CLB + 2],
        "c_out_norm": small_sum[R_GAM:R_GAM + 1, :HD], "c_w_out": rb[B_COUT:B_COUT + 256].reshape(1, 256, D),
        "final_norm": small_sum[R_FIN],
    }
    weights = dict(ffn_pre_norm=ffn_pre_norm, ffn_pre_w_gate=ffn_pre_w_gate, ffn_pre_w_up=ffn_pre_w_up, ffn_pre_w_down=ffn_pre_w_down, mix_norm=mix_norm, ffn_post_norm=ffn_post_norm, ffn_post_w_gate=ffn_post_w_gate, ffn_post_w_up=ffn_post_w_up, ffn_post_w_down=ffn_post_w_down, ab_w_in=ab_w_in, ab_conv_w=ab_conv_w, ab_w_out=ab_w_out, c_w_in=c_w_in, c_lower_bounds=c_lower_bounds, c_out_norm=c_out_norm, c_w_out=c_w_out, final_norm=final_norm)
    m_in = dict(ffn_pre_norm=m_ffn_pre_norm, ffn_pre_w_gate=m_ffn_pre_w_gate, ffn_pre_w_up=m_ffn_pre_w_up, ffn_pre_w_down=m_ffn_pre_w_down, mix_norm=m_mix_norm, ffn_post_norm=m_ffn_post_norm, ffn_post_w_gate=m_ffn_post_w_gate, ffn_post_w_up=m_ffn_post_w_up, ffn_post_w_down=m_ffn_post_w_down, ab_w_in=m_ab_w_in, ab_conv_w=m_ab_conv_w, ab_w_out=m_ab_w_out, c_w_in=m_c_w_in, c_lower_bounds=m_c_lower_bounds, c_out_norm=m_c_out_norm, c_w_out=m_c_w_out, final_norm=m_final_norm)
    v_in = dict(ffn_pre_norm=v_ffn_pre_norm, ffn_pre_w_gate=v_ffn_pre_w_gate, ffn_pre_w_up=v_ffn_pre_w_up, ffn_pre_w_down=v_ffn_pre_w_down, mix_norm=v_mix_norm, ffn_post_norm=v_ffn_post_norm, ffn_post_w_gate=v_ffn_post_w_gate, ffn_post_w_up=v_ffn_post_w_up, ffn_post_w_down=v_ffn_post_w_down, ab_w_in=v_ab_w_in, ab_conv_w=v_ab_conv_w, ab_w_out=v_ab_w_out, c_w_in=v_c_w_in, c_lower_bounds=v_c_lower_bounds, c_out_norm=v_c_out_norm, c_w_out=v_c_w_out, final_norm=v_final_norm)
    names = list(weights)
    big = [n for n in names if weights[n].size >= 65536]
    tiny = [n for n in names if n not in big]

    delta, new_m, new_v = {}, {}, {}
    for n in big:
        shape = weights[n].shape
        two_d = (shape[0] * shape[1], shape[2])
        d, m2, v2 = _adamw("adamw_" + n, weights[n].reshape(two_d), grads[n].reshape(two_d),
                           m_in[n].reshape(two_d), v_in[n].reshape(two_d))
        delta[n], new_m[n], new_v[n] = d.reshape(shape), m2.reshape(shape), v2.reshape(shape)

    def tiny_slab(src):
        return _small_slab([src[n].reshape(-1, src[n].shape[-1]) for n in tiny])

    offs, row = {}, 0
    for n in tiny:
        nrows = weights[n].size // weights[n].shape[-1]
        offs[n] = (row, nrows)
        row += nrows
    d, m2, v2 = _adamw("adamw_small", tiny_slab(weights), tiny_slab(grads), tiny_slab(m_in), tiny_slab(v_in))
    for n in tiny:
        r0, nr = offs[n]
        shape = weights[n].shape
        for dst, src in ((delta, d), (new_m, m2), (new_v, v2)):
            dst[n] = src[r0:r0 + nr, :shape[-1]].reshape(shape)

    loss = small_sum[R_LOSS, 0]
    return (loss, grad_x.reshape(1, t, D), *[grads[n] for n in names], *[delta[n] for n in names],
            *[new_m[n] for n in names], *[new_v[n] for n in names])
```

```python
import functools
import math

import jax
import jax.numpy as jnp
from jax import lax
from jax.experimental import pallas as pl
from jax.experimental.pallas import tpu as pltpu

F32 = jnp.float32
BF16 = jnp.bfloat16
MESH = pl.DeviceIdType.MESH
ANY = pl.BlockSpec(memory_space=pl.ANY)

D = 1024
FS = 704
NSH = 4
RMS_EPS = 1e-6
MACARON = 0.5
CHUNK = 64
HD = 128
HGRN_HPS = 4
SBQ = 128
SB_PAIRS = 2
SB_DEAD = -105.0
CONV_HALO = 8
LANE = 128
ROW_TILE = 512
MM_TILE = 1024
WGRAD_TILE = 4096
VMEM_LIMIT = 48 * 1024 * 1024

ADAM_LR, ADAM_B1, ADAM_B2, ADAM_EPS, ADAM_WD, ADAM_STEP = 0.001, 0.9, 0.999, 1e-08, 0.01, 10

NN = ((1,), (0,))
NT = ((1,), (1,))
TN = ((0,), (0,))

SMALL_ROWS = 16
R_PRE, R_MIX, R_POST, R_CLB, R_FIN, R_GAM, R_CONV, R_LOSS = 0, 2, 4, 6, 8, 9, 10, 13

B_DOWN = 0
B_ABOUT = 4 * FS
B_CIN = B_ABOUT + 256
B_COUT = B_CIN + 1024
B_ROWS = B_COUT + 256


def _dg(a, b, dims):
    return lax.dot_general(a, b, (dims, ((), ())), preferred_element_type=F32)


def _split(x):
    hi = x.astype(BF16)
    lo = (x - hi.astype(F32)).astype(BF16)
    return hi, lo


def _dot3(a, b, dims):
    ah, al = _split(a)
    bh, bl = _split(b)
    return _dg(ah, bh, dims) + _dg(ah, bl, dims) + _dg(al, bh, dims)


def _sigmoid(x):
    return 1.0 / (1.0 + jnp.exp(-x))


def _params(sem):
    return pltpu.CompilerParams(dimension_semantics=sem, vmem_limit_bytes=VMEM_LIMIT)


def _spec(shape, imap):
    return pl.BlockSpec(shape, imap)


def _mm(name, pairs, *, grid, o_shape, o_dtype, o_spec, dims, kaxis, nk, acc_shape=None, res=None, scale=None,
        into=None):
    npairs = len(pairs)
    operands, specs = [], []
    for a, a_spec, b, b_spec in pairs:
        operands += [a, b]
        specs += [a_spec, b_spec]
    if res is not None:
        operands.append(res[0])
        specs.append(res[1])
    aliases = {}
    if into is not None:
        aliases = {len(operands): 0}
        operands.append(into)
        specs.append(ANY)

    def body(*refs):
        o_ref = refs[len(operands)]
        part = None
        for n in range(npairs):
            d = _dg(refs[2 * n][...], refs[2 * n + 1][...], dims)
            part = d if part is None else part + d

        def finish(val):
            if scale is not None:
                val = val * scale
            if res is not None:
                val = val + refs[2 * npairs][...]
            o_ref[...] = val.astype(o_dtype)

        if nk == 1:
            finish(part)
        else:
            acc_ref = refs[len(operands) + 1]
            k = pl.program_id(kaxis)

            @pl.when(k == 0)
            def _():
                acc_ref[...] = part

            @pl.when(k > 0)
            def _():
                acc_ref[...] += part

            @pl.when(k == nk - 1)
            def _():
                finish(acc_ref[...])

    sem = tuple("arbitrary" if (ax == kaxis and nk > 1) else "parallel" for ax in range(len(grid)))
    return pl.pallas_call(
        body, name=name, grid=grid, in_specs=specs, out_specs=o_spec,
        out_shape=jax.ShapeDtypeStruct(o_shape, o_dtype),
        scratch_shapes=[pltpu.VMEM(acc_shape, F32)] if nk > 1 else [],
        input_output_aliases=aliases,
        compiler_params=_params(sem),
    )(*operands)


def _norm_fwd(name, h, gain_slab, row):
    t = h.shape[0]
    tm = min(ROW_TILE, t)

    def body(h_ref, g_ref, o_ref):
        x = h_ref[...]
        r = lax.rsqrt(jnp.mean(x * x, axis=-1, keepdims=True) + RMS_EPS)
        o_ref[...] = (x * r * g_ref[...]).astype(BF16)

    return pl.pallas_call(
        body, name=name, grid=(t // tm,),
        in_specs=[_spec((tm, D), lambda i: (i, 0)), _spec((None, 1, D), lambda i: (row, 0, 0))],
        out_specs=_spec((tm, D), lambda i: (i, 0)),
        out_shape=jax.ShapeDtypeStruct((t, D), BF16),
        compiler_params=_params(("parallel",)),
    )(h, gain_slab)


def _norm_bwd(name, dhn, h, gain_slab, row, dres):
    t = h.shape[0]
    tm = min(ROW_TILE, t)
    nt = t // tm

    def body(dhn_ref, h_ref, g_ref, dres_ref, dh_ref, dhb_ref, dg_ref, acc_ref):
        i = pl.program_id(0)
        x = h_ref[...]
        r = lax.rsqrt(jnp.mean(x * x, axis=-1, keepdims=True) + RMS_EPS)
        xh = x * r
        dy = dhn_ref[...]
        gdy = dy * g_ref[...]
        dx = (gdy - xh * jnp.mean(gdy * xh, axis=-1, keepdims=True)) * r
        dh = dres_ref[...] + dx
        dh_ref[...] = dh
        dhb_ref[...] = dh.astype(BF16)
        part = jnp.sum((dy * xh).reshape(tm // 8, 8, D), axis=0)

        @pl.when(i == 0)
        def _():
            acc_ref[...] = part

        @pl.when(i > 0)
        def _():
            acc_ref[...] += part

        @pl.when(i == nt - 1)
        def _():
            dg_ref[...] = jnp.sum(acc_ref[...], axis=0, keepdims=True)

    row_spec = _spec((tm, D), lambda i: (i, 0))
    return pl.pallas_call(
        body, name=name, grid=(nt,),
        in_specs=[row_spec, row_spec, _spec((None, 1, D), lambda i: (row, 0, 0)), row_spec],
        out_specs=[row_spec, row_spec, _spec((1, D), lambda i: (0, 0))],
        out_shape=[jax.ShapeDtypeStruct((t, D), F32), jax.ShapeDtypeStruct((t, D), BF16),
                   jax.ShapeDtypeStruct((1, D), F32)],
        scratch_shapes=[pltpu.VMEM((8, D), F32)],
        compiler_params=_params(("arbitrary",)),
    )(dhn, h, gain_slab, dres)


def _final_loss(h, gain_slab, target):
    t = h.shape[0]
    tm = min(ROW_TILE, t)
    nt = t // tm

    def body(h_ref, g_ref, t_ref, dh_ref, dhb_ref, dg_ref, loss_ref, acc_ref, lacc_ref):
        i = pl.program_id(0)
        x = h_ref[...]
        g = g_ref[...]
        r = lax.rsqrt(jnp.mean(x * x, axis=-1, keepdims=True) + RMS_EPS)
        xh = x * r
        err = xh * g - t_ref[...]
        dy = err * (1.0 / D)
        gdy = dy * g
        dh = (gdy - xh * jnp.mean(gdy * xh, axis=-1, keepdims=True)) * r
        dh_ref[...] = dh
        dhb_ref[...] = dh.astype(BF16)
        part = jnp.sum((dy * xh).reshape(tm // 8, 8, D), axis=0)
        lpart = jnp.sum((err * err).reshape(tm // 8, 8, D), axis=0)

        @pl.when(i == 0)
        def _():
            acc_ref[...] = part
            lacc_ref[...] = lpart

        @pl.when(i > 0)
        def _():
            acc_ref[...] += part
            lacc_ref[...] += lpart

        @pl.when(i == nt - 1)
        def _():
            dg_ref[...] = jnp.sum(acc_ref[...], axis=0, keepdims=True)
            rows = jnp.sum(lacc_ref[...], axis=0, keepdims=True)
            loss_ref[...] = jnp.sum(rows, axis=1, keepdims=True) * (0.5 / D)

    row_spec = _spec((tm, D), lambda i: (i, 0))
    return pl.pallas_call(
        body, name="final_loss", grid=(nt,),
        in_specs=[row_spec, _spec((None, 1, D), lambda i: (R_FIN, 0, 0)), row_spec],
        out_specs=[row_spec, row_spec, _spec((1, D), lambda i: (0, 0)), _spec((1, 1), lambda i: (0, 0))],
        out_shape=[jax.ShapeDtypeStruct((t, D), F32), jax.ShapeDtypeStruct((t, D), BF16),
                   jax.ShapeDtypeStruct((1, D), F32), jax.ShapeDtypeStruct((1, 1), F32)],
        scratch_shapes=[pltpu.VMEM((8, D), F32), pltpu.VMEM((8, D), F32)],
        compiler_params=_params(("arbitrary",)),
    )(h, gain_slab, target)


def _ffn_up(name, hn, wa, gate_idx, up_idx):
    t = hn.shape[0]
    tm = min(MM_TILE, t)

    def body(x_ref, wg_ref, wu_ref, s_up_ref, s_gate_ref, a_ref):
        x = x_ref[...]
        g = _dg(x, wg_ref[...], NN)
        u = _dg(x, wu_ref[...], NN)
        sg = _sigmoid(g)
        silu = g * sg
        s_up_ref[...] = (MACARON * silu).astype(BF16)
        s_gate_ref[...] = (MACARON * u * (sg * (1.0 + g * (1.0 - sg)))).astype(BF16)
        a_ref[...] = (silu * u).astype(BF16)

    act = _spec((None, tm, FS), lambda s, i: (s, i, 0))
    shape = jax.ShapeDtypeStruct((NSH, t, FS), BF16)
    return pl.pallas_call(
        body, name=name, grid=(NSH, t // tm),
        in_specs=[_spec((tm, D), lambda s, i: (i, 0)),
                  _spec((None, None, D, FS), lambda s, i: (s, gate_idx, 0, 0)),
                  _spec((None, None, D, FS), lambda s, i: (s, up_idx, 0, 0))],
        out_specs=[act, act, act], out_shape=[shape, shape, shape],
        compiler_params=_params(("parallel", "parallel")),
    )(hn, wa, wa)


def _ffn_down(name, a, wb, down_idx, h):
    t = h.shape[0]
    tm = min(MM_TILE, t)
    return _mm(name, [(a, _spec((None, tm, FS), lambda i, k: (k, i, 0)),
                       wb, _spec((None, FS, D), lambda i, k: (k, down_idx, 0)))],
               grid=(t // tm, NSH), o_shape=(t, D), o_dtype=F32, o_spec=_spec((tm, D), lambda i, k: (i, 0)),
               dims=NN, kaxis=1, nk=NSH, acc_shape=(tm, D), res=(h, _spec((tm, D), lambda i, k: (i, 0))),
               scale=MACARON)


def _ffn_bwd_up(name, dhb, wb, down_idx, s_up, s_gate):
    t = dhb.shape[0]
    tm = min(MM_TILE, t)

    def body(dh_ref, wd_ref, s_up_ref, s_gate_ref, dg_ref, du_ref):
        da = _dg(dh_ref[...], wd_ref[...], NT)
        du_ref[...] = (da * s_up_ref[...].astype(F32)).astype(BF16)
        dg_ref[...] = (da * s_gate_ref[...].astype(F32)).astype(BF16)

    act = _spec((None, tm, FS), lambda s, i: (s, i, 0))
    shape = jax.ShapeDtypeStruct((NSH, t, FS), BF16)
    return pl.pallas_call(
        body, name=name, grid=(NSH, t // tm),
        in_specs=[_spec((tm, D), lambda s, i: (i, 0)), _spec((None, FS, D), lambda s, i: (s, down_idx, 0)), act, act],
        out_specs=[act, act], out_shape=[shape, shape],
        compiler_params=_params(("parallel", "parallel")),
    )(dhb, wb, s_up, s_gate)


def _wgrad(name, a, a_spec, b, b_spec, out_rows, out_cols, t, tt, group, slot, scale=None):
    first = isinstance(group, int)
    rows = group if first else group.shape[1]
    return _mm(name, [(a, a_spec, b, b_spec)], grid=(NSH, t // tt),
               o_shape=(NSH, rows, out_cols), o_dtype=BF16,
               o_spec=_spec((None, out_rows, out_cols), lambda s, k: (s, slot, 0)),
               dims=TN, kaxis=1, nk=t // tt, acc_shape=(out_rows, out_cols), scale=scale,
               into=None if first else group)


def _ffn_backward(tag, dh, dhb, h_in, hn, s_up, s_gate, a, wa, wb, gate_idx, up_idx, down_idx, small, norm_row,
                  grad_a, grad_b):
    t = dh.shape[0]
    tm = min(MM_TILE, t)
    tt = min(WGRAD_TILE, t)
    dg, du = _ffn_bwd_up(tag + "_bwd_up", dhb, wb, down_idx, s_up, s_gate)
    tok = _spec((tt, D), lambda s, k: (k, 0))
    hid = _spec((None, tt, FS), lambda s, k: (s, k, 0))
    grad_b = _wgrad(tag + "_dwd", a, hid, dhb, tok, FS, D, t, tt, grad_b, down_idx, scale=MACARON)
    grad_a = _wgrad(tag + "_dwg", hn, tok, dg, hid, D, FS, t, tt, grad_a, gate_idx)
    grad_a = _wgrad(tag + "_dwu", hn, tok, du, hid, D, FS, t, tt, grad_a, up_idx)
    act = _spec((None, tm, FS), lambda i, k: (k, i, 0))
    dhn = _mm(tag + "_dhn",
              [(dg, act, wa, _spec((None, None, D, FS), lambda i, k: (k, gate_idx, 0, 0))),
               (du, act, wa, _spec((None, None, D, FS), lambda i, k: (k, up_idx, 0, 0)))],
              grid=(t // tm, NSH), o_shape=(t, D), o_dtype=F32, o_spec=_spec((tm, D), lambda i, k: (i, 0)),
              dims=NT, kaxis=1, nk=NSH, acc_shape=(tm, D))
    dh_in, dhb_in, d_gain = _norm_bwd(tag + "_norm_bwd", dhn, h_in, small, norm_row, dh)
    return dh_in, dhb_in, grad_a, grad_b, d_gain


def _conv_fwd(proj_a, conv_w):
    t = proj_a.shape[0]
    tm = min(ROW_TILE, t)
    hb = tm // CONV_HALO

    def body(ab_ref, ac_ref, ax_ref, acp_ref, axp_ref, w_ref, y_ref):
        i = pl.program_id(1)
        u = ac_ref[...] * ax_ref[...]
        up = jnp.where(i > 0, acp_ref[...] * axp_ref[...], 0.0)
        ext = jnp.concatenate([up, u], axis=0)
        u1 = pltpu.roll(ext, 1, 0)[CONV_HALO:]
        u2 = pltpu.roll(ext, 2, 0)[CONV_HALO:]
        w = w_ref[...]
        conv = w[0:1] * u2 + w[1:2] * u1 + w[2:3] * u
        y_ref[...] = (ab_ref[...] * conv).astype(BF16)

    def cur(off):
        return _spec((tm, LANE), lambda j, i: (i, off + j))

    def prev(off):
        return _spec((CONV_HALO, LANE), lambda j, i: (jnp.maximum(i * hb - 1, 0), off + j))

    return pl.pallas_call(
        body, name="conv_fwd", grid=(4, t // tm),
        in_specs=[cur(0), cur(4), cur(8), prev(4), prev(8),
                  _spec((None, None, 8, LANE), lambda j, i: (j, 0, 0, 0))],
        out_specs=_spec((tm, LANE), lambda j, i: (i, j)),
        out_shape=jax.ShapeDtypeStruct((t, 512), BF16),
        compiler_params=_params(("parallel", "parallel")),
    )(proj_a, proj_a, proj_a, proj_a, proj_a, conv_w)


def _conv_bwd(proj_a, conv_w, dy):
    t = proj_a.shape[0]
    tm = min(ROW_TILE, t)
    hb = tm // CONV_HALO
    nt = t // tm

    def body(ab_ref, ac_ref, ax_ref, dy_ref, acp_ref, axp_ref, abn_ref, dyn_ref, w_ref,
             dab_ref, dac_ref, dax_ref, dw_ref, acc_ref):
        i = pl.program_id(1)
        ab, ac, ax = ab_ref[...], ac_ref[...], ax_ref[...]
        u = ac * ax
        up = jnp.where(i > 0, acp_ref[...] * axp_ref[...], 0.0)
        ext = jnp.concatenate([up, u], axis=0)
        u1 = pltpu.roll(ext, 1, 0)[CONV_HALO:]
        u2 = pltpu.roll(ext, 2, 0)[CONV_HALO:]
        w = w_ref[...]
        conv = w[0:1] * u2 + w[1:2] * u1 + w[2:3] * u
        dy_v = dy_ref[...]
        dab_ref[...] = (dy_v * conv).astype(BF16)
        dc = dy_v * ab
        dcn = jnp.where(i < nt - 1, dyn_ref[...] * abn_ref[...], 0.0)
        extn = jnp.concatenate([dc, dcn], axis=0)
        n = tm + CONV_HALO
        dc1 = pltpu.roll(extn, n - 1, 0)[:tm]
        dc2 = pltpu.roll(extn, n - 2, 0)[:tm]
        du = w[2:3] * dc + w[1:2] * dc1 + w[0:1] * dc2
        dac_ref[...] = (du * ax).astype(BF16)
        dax_ref[...] = (du * ac).astype(BF16)
        rid = lax.broadcasted_iota(jnp.int32, (8, LANE), 0)
        part = jnp.where(rid == 0, jnp.sum(dc * u2, axis=0, keepdims=True),
                         jnp.where(rid == 1, jnp.sum(dc * u1, axis=0, keepdims=True),
                                   jnp.where(rid == 2, jnp.sum(dc * u, axis=0, keepdims=True), 0.0)))

        @pl.when(i == 0)
        def _():
            acc_ref[...] = part

        @pl.when(i > 0)
        def _():
            acc_ref[...] += part

        @pl.when(i == nt - 1)
        def _():
            dw_ref[...] = acc_ref[...]

    def cur(off):
        return _spec((tm, LANE), lambda j, i: (i, off + j))

    def prev(off):
        return _spec((CONV_HALO, LANE), lambda j, i: (jnp.maximum(i * hb - 1, 0), off + j))

    def nxt(off):
        return _spec((CONV_HALO, LANE), lambda j, i: (jnp.minimum((i + 1) * hb, nt * hb - 1), off + j))

    outs = pl.pallas_call(
        body, name="conv_bwd", grid=(4, nt),
        in_specs=[cur(0), cur(4), cur(8), cur(0), prev(4), prev(8), nxt(0), nxt(0),
                  _spec((None, None, 8, LANE), lambda j, i: (j, 0, 0, 0))],
        out_specs=[_spec((tm, LANE), lambda j, i: (i, j)), _spec((tm, LANE), lambda j, i: (i, j)),
                   _spec((tm, LANE), lambda j, i: (i, j)), _spec((None, 8, LANE), lambda j, i: (j, 0, 0))],
        out_shape=[jax.ShapeDtypeStruct((t, 512), BF16), jax.ShapeDtypeStruct((t, 512), BF16),
                   jax.ShapeDtypeStruct((t, 512), BF16), jax.ShapeDtypeStruct((4, 8, LANE), F32)],
        scratch_shapes=[pltpu.VMEM((8, LANE), F32)],
        compiler_params=_params(("parallel", "arbitrary")),
    )(proj_a, proj_a, proj_a, dy, proj_a, proj_a, proj_a, dy, conv_w)
    return outs


def _log_sigmoid(z):
    return jnp.minimum(z, 0.0) - jnp.log(1.0 + jnp.exp(-jnp.abs(z)))


def _sb_masks():
    row = lax.broadcasted_iota(jnp.int32, (SBQ, SBQ), 0)
    col = lax.broadcasted_iota(jnp.int32, (SBQ, SBQ), 1)
    return row, col


def _ones_where(mask):
    return jnp.where(mask, 1.0, 0.0).astype(BF16)


def _head_mask(head):
    lane = lax.broadcasted_iota(jnp.int32, (1, LANE), 1)
    return lane >= 64 if head else lane < 64


def _sb_fwd(proj_b):
    t = proj_b.shape[0]
    nq = t // SBQ
    scale = 1.0 / math.sqrt(64.0)

    npair = SB_PAIRS
    wide = npair * LANE
    ngrp = 4 // npair
    chains = [(p, head) for p in range(npair) for head in range(2)]

    def body(q_ref, k_ref, v_ref, y_ref, l_ref, n_ref):
        grp = pl.program_id(0)
        qi = pl.program_id(1)
        row, col = _sb_masks()
        strict = col < row
        m_suffix = _ones_where(row > col)
        qh = []
        for p, head in chains:
            q_all = q_ref[:, p * LANE:(p + 1) * LANE]
            qh.append(jnp.where(_head_mask(head), q_all, jnp.zeros_like(q_all)))

        def block(kb, carry, diag):
            start = pl.multiple_of(kb * SBQ, SBQ)
            new = []
            for n, (p, head) in enumerate(chains):
                run, acc = carry[n]
                kk = k_ref[pl.ds(start, SBQ), p * LANE:(p + 1) * LANE]
                vv = v_ref[pl.ds(start, SBQ), p * LANE:(p + 1) * LANE]
                z = _dg(qh[n], kk, NT) * scale
                lb = _log_sigmoid(z)
                lk = lb - z
                if diag:
                    lk = jnp.where(strict, lk, 0.0)
                hi, lo = _split(lk)
                later = run + _dg(hi, m_suffix, NN) + _dg(lo, m_suffix, NN)
                w = jnp.exp(lb + later)
                if diag:
                    w = jnp.where(strict, w, 0.0)
                acc = acc + _dg(w.astype(BF16), vv, NN)
                run = run + jnp.sum(hi.astype(F32) + lo.astype(F32), axis=1, keepdims=True)
                new.append((run, acc))
            return tuple(new)

        zero = (jnp.zeros((SBQ, 1), F32), jnp.zeros((SBQ, LANE), F32))
        carry = block(qi, tuple(zero for _ in chains), True)

        def live(c):
            top = c[1][0][0]
            for n in range(1, len(chains)):
                top = jnp.maximum(top, c[1][n][0])
            return jnp.logical_and(c[0] < qi, jnp.max(top) > SB_DEAD)

        def step(c):
            return c[0] + 1, block(qi - 1 - c[0], c[1], False)

        count, carry = lax.while_loop(live, step, (jnp.int32(0), carry))
        n_ref[grp * nq + qi] = count.astype(F32)
        for p in range(npair):
            (run0, acc0), (run1, acc1) = carry[2 * p], carry[2 * p + 1]
            hm = _head_mask(0)
            y_ref[:, p * LANE:(p + 1) * LANE] = jnp.where(hm, acc0, acc1).astype(BF16)
            l_ref[p] = jnp.where(hm, run0, run1)

    return pl.pallas_call(
        body, name="sb_fwd", grid=(ngrp, nq),
        in_specs=[_spec((SBQ, wide), lambda g, i: (i, g)),
                  _spec((t, wide), lambda g, i: (0, ngrp + g)),
                  _spec((t, wide), lambda g, i: (0, 2 * ngrp + g))],
        out_specs=[_spec((SBQ, wide), lambda g, i: (i, g)), _spec((npair, SBQ, LANE), lambda g, i: (g, i, 0)),
                   pl.BlockSpec(memory_space=pltpu.SMEM)],
        out_shape=[jax.ShapeDtypeStruct((t, 512), BF16), jax.ShapeDtypeStruct((4, t, LANE), F32),
                   jax.ShapeDtypeStruct((ngrp * nq,), F32)],
        compiler_params=_params(("arbitrary", "arbitrary")),
    )(proj_b, proj_b, proj_b)


def _sb_bwd(proj_b, dy, ltot, nblk):
    t = proj_b.shape[0]
    nq = t // SBQ
    scale = 1.0 / math.sqrt(64.0)

    npair = SB_PAIRS
    wide = npair * LANE
    ngrp = 4 // npair
    chains = [(p, head) for p in range(npair) for head in range(2)]

    def body(q_ref, k_ref, v_ref, dy_ref, l_ref, n_ref, dq_ref, dk_ref, dv_ref, dk_acc, dv_acc):
        grp = pl.program_id(0)
        qi = pl.program_id(1)

        @pl.when(qi == 0)
        def _():
            dk_acc[...] = jnp.zeros_like(dk_acc)
            dv_acc[...] = jnp.zeros_like(dv_acc)

        row, col = _sb_masks()
        strict = col < row
        m_prefix = _ones_where(row <= col)
        m_before = _ones_where(row < col)
        qh, doh, ltot_h = [], [], []
        for p, head in chains:
            pl_ = slice(p * LANE, (p + 1) * LANE)
            q_all = q_ref[:, pl_]
            do_all = dy_ref[:, pl_].astype(BF16)
            hm = _head_mask(head)
            qh.append(jnp.where(hm, q_all, jnp.zeros_like(q_all)))
            doh.append(jnp.where(hm, do_all, jnp.zeros_like(do_all)))
            ltot_h.append(l_ref[p][:, head * 64:head * 64 + 1])

        def block(kb, carry, diag):
            start = pl.multiple_of(kb * SBQ, SBQ)
            new = []
            for p in range(npair):
                pl_ = slice(p * LANE, (p + 1) * LANE)
                kk = k_ref[pl.ds(start, SBQ), pl_]
                vv = v_ref[pl.ds(start, SBQ), pl_]
                dk_blk, dv_blk = None, None
                for head in range(2):
                    n = 2 * p + head
                    seen, dseen, dq = carry[n]
                    z = _dg(qh[n], kk, NT) * scale
                    lb = _log_sigmoid(z)
                    lk = lb - z
                    if diag:
                        lk = jnp.where(strict, lk, 0.0)
                    hi, lo = _split(lk)
                    later = (ltot_h[n] - seen) - (_dg(hi, m_prefix, NN) + _dg(lo, m_prefix, NN))
                    w = jnp.exp(lb + later)
                    if diag:
                        w = jnp.where(strict, w, 0.0)
                    da = w * _dg(doh[n], vv, NT)
                    dv_h = _dg(w.astype(BF16), doh[n], TN)
                    dah, dal = _split(da)
                    dlk = dseen + _dg(dah, m_before, NN) + _dg(dal, m_before, NN)
                    sig = jnp.exp(lb)
                    dz = (da * (1.0 - sig) - dlk * sig) * scale
                    if diag:
                        dz = jnp.where(strict, dz, 0.0)
                    dzb = dz.astype(BF16)
                    dk_h = _dg(dzb, qh[n], TN)
                    dk_blk = dk_h if dk_blk is None else dk_blk + dk_h
                    dv_blk = dv_h if dv_blk is None else dv_blk + dv_h
                    new.append((seen + jnp.sum(hi.astype(F32) + lo.astype(F32), axis=1, keepdims=True),
                                dseen + jnp.sum(da, axis=1, keepdims=True),
                                dq + _dg(dzb, kk, NN)))
                dk_acc[pl.ds(start, SBQ), pl_] += dk_blk
                dv_acc[pl.ds(start, SBQ), pl_] += dv_blk
            return tuple(new)

        zero = (jnp.zeros((SBQ, 1), F32), jnp.zeros((SBQ, 1), F32), jnp.zeros((SBQ, LANE), F32))
        first = qi - n_ref[grp * nq + qi].astype(jnp.int32)
        carry = lax.fori_loop(first, qi, lambda kb, c: block(kb, c, False), tuple(zero for _ in chains))
        carry = block(qi, carry, True)
        for p in range(npair):
            dq_ref[:, p * LANE:(p + 1) * LANE] = jnp.where(_head_mask(0), carry[2 * p][2], carry[2 * p + 1][2]).astype(BF16)

        @pl.when(qi == nq - 1)
        def _():
            dk_ref[...] = dk_acc[...].astype(BF16)
            dv_ref[...] = dv_acc[...].astype(BF16)

    full = jax.ShapeDtypeStruct((t, 512), BF16)
    return pl.pallas_call(
        body, name="sb_bwd", grid=(ngrp, nq),
        in_specs=[_spec((SBQ, wide), lambda g, i: (i, g)),
                  _spec((t, wide), lambda g, i: (0, ngrp + g)),
                  _spec((t, wide), lambda g, i: (0, 2 * ngrp + g)),
                  _spec((SBQ, wide), lambda g, i: (i, ngrp + g)),
                  _spec((npair, SBQ, LANE), lambda g, i: (g, i, 0)),
                  pl.BlockSpec(memory_space=pltpu.SMEM)],
        out_specs=[_spec((SBQ, wide), lambda g, i: (i, g)),
                   _spec((t, wide), lambda g, i: (0, g)), _spec((t, wide), lambda g, i: (0, g))],
        out_shape=[full, full, full],
        scratch_shapes=[pltpu.VMEM((t, wide), F32), pltpu.VMEM((t, wide), F32)],
        compiler_params=_params(("parallel", "arbitrary")),
    )(proj_b, proj_b, proj_b, dy, ltot, nblk)


def _hgrn_gates(qr, fr, c0, c1):
    mx = jnp.maximum(c0, c1)
    e0, e1 = jnp.exp(c0 - mx), jnp.exp(c1 - mx)
    lb = e1 / (e0 + e1)
    sx = _sigmoid(fr)
    f = lb + (1.0 - lb) * sx
    k = (1.0 - lb) * (1.0 - sx)
    sq = _sigmoid(qr)
    return lb, sx, f, k, sq, qr * sq


def _hgrn_decays(f):
    row = lax.broadcasted_iota(jnp.int32, (CHUNK, CHUNK), 0)
    col = lax.broadcasted_iota(jnp.int32, (CHUNK, CHUNK), 1)
    tril = col <= row
    hi, lo = _split(jnp.log(f))
    trib = _ones_where(tril)
    b = _dg(trib, hi, NN) + _dg(trib, lo, NN)
    bm = b[CHUNK // 2 - 1:CHUNK // 2]
    bl = b[CHUNK - 1:CHUNK]
    return tril, col >= row, b, bm, bl


def _hgrn_fwd(proj_c, small):
    t = proj_c.shape[1]
    nc = t // CHUNK
    nh = D // HD

    hps = HGRN_HPS
    wide = hps * HD

    def body(p_ref, c0_ref, c1_ref, gam_ref, o_ref, y_ref, sst_ref, st_ref):
        c = pl.program_id(1)

        @pl.when(c == 0)
        def _():
            st_ref[...] = jnp.zeros_like(st_ref)

        for j in range(hps):
            ln = slice(j * HD, (j + 1) * HD)
            st0 = st_ref[j]
            sst_ref[j] = st0
            qr, fr, v, g = p_ref[0, :, ln], p_ref[1, :, ln], p_ref[2, :, ln], p_ref[3, :, ln]
            lb, sx, f, k, sq, q = _hgrn_gates(qr, fr, c0_ref[:, ln], c1_ref[:, ln])
            tril, _, b, bm, bl = _hgrn_decays(f)
            qd = q * jnp.exp(b)
            qt = q * jnp.exp(b - bm)
            kt = k * jnp.exp(bm - b)
            kl = k * jnp.exp(bl - b)
            vb = v.astype(BF16)
            att = jnp.where(tril, _dot3(qt, kt, NT), 0.0)
            o = _dg(qd.astype(BF16), st0.astype(BF16), NT) + _dg(att.astype(BF16), vb, NN)
            st_ref[j] = st0 * jnp.exp(bl) + _dg(vb, kl.astype(BF16), TN)
            o_ref[:, ln] = o
            r = lax.rsqrt(jnp.mean(o * o, axis=-1, keepdims=True) + RMS_EPS)
            y_ref[:, ln] = (o * r * gam_ref[...] * (g * _sigmoid(g))).astype(BF16)

    return pl.pallas_call(
        body, name="hgrn_fwd", grid=(nh // hps, nc),
        in_specs=[_spec((4, CHUNK, wide), lambda h, c: (0, c, h)),
                  _spec((None, 1, wide), lambda h, c: (R_CLB, 0, h)),
                  _spec((None, 1, wide), lambda h, c: (R_CLB + 1, 0, h)),
                  _spec((None, 1, HD), lambda h, c: (R_GAM, 0, 0))],
        out_specs=[_spec((CHUNK, wide), lambda h, c: (c, h)), _spec((CHUNK, wide), lambda h, c: (c, h)),
                   _spec((None, hps, HD, HD), lambda h, c: (c, h, 0, 0))],
        out_shape=[jax.ShapeDtypeStruct((t, D), F32), jax.ShapeDtypeStruct((t, D), BF16),
                   jax.ShapeDtypeStruct((nc, nh, HD, HD), F32)],
        scratch_shapes=[pltpu.VMEM((hps, HD, HD), F32)],
        compiler_params=_params(("parallel", "arbitrary")),
    )(proj_c, small, small, small)


def _hgrn_bwd(proj_c, small, o, sst, dyc):
    t = proj_c.shape[1]
    nc = t // CHUNK
    nh = D // HD

    hps = HGRN_HPS
    wide = hps * HD
    ng = nh // hps

    def body(p_ref, c0_ref, c1_ref, gam_ref, o_ref, sst_ref, dy_ref, dp_ref, dclb_ref, dgam_ref, dst_ref, dlb_acc, dgam_acc):
        group = pl.program_id(0)
        step = pl.program_id(1)

        @pl.when(step == 0)
        def _():
            dst_ref[...] = jnp.zeros_like(dst_ref)
            dlb_acc[...] = jnp.zeros_like(dlb_acc)

        @pl.when((step == 0) & (group == 0))
        def _():
            dgam_acc[...] = jnp.zeros_like(dgam_acc)

        gam = gam_ref[...]
        for j in range(hps):
            ln = slice(j * HD, (j + 1) * HD)
            st0 = sst_ref[j]
            dst1 = dst_ref[j]
            qr, fr, v, g = p_ref[0, :, ln], p_ref[1, :, ln], p_ref[2, :, ln], p_ref[3, :, ln]
            lb, sx, f, k, sq, q = _hgrn_gates(qr, fr, c0_ref[:, ln], c1_ref[:, ln])
            tril, triu, b, bm, bl = _hgrn_decays(f)
            eb = jnp.exp(b)
            e_qt = jnp.exp(b - bm)
            e_kt = jnp.exp(bm - b)
            e_kl = jnp.exp(bl - b)
            e_bl = jnp.exp(bl)
            qd, qt, kt, kl = q * eb, q * e_qt, k * e_kt, k * e_kl
            ov = o_ref[:, ln]
            r = lax.rsqrt(jnp.mean(ov * ov, axis=-1, keepdims=True) + RMS_EPS)
            oh = ov * r
            sg = _sigmoid(g)
            dy = dy_ref[:, ln]
            dp_ref[3, :, ln] = (dy * oh * gam * (sg * (1.0 + g * (1.0 - sg)))).astype(BF16)
            dyv = dy * (g * sg)
            dgam_acc[...] += jnp.sum(dyv * oh, axis=0, keepdims=True)
            gdy = dyv * gam
            do = (gdy - oh * jnp.mean(gdy * oh, axis=-1, keepdims=True)) * r
            dob, vb = do.astype(BF16), v.astype(BF16)
            st0b, dst1b = st0.astype(BF16), dst1.astype(BF16)
            st1 = st0 * e_bl + _dg(vb, kl.astype(BF16), TN)
            att = jnp.where(tril, _dot3(qt, kt, NT), 0.0)
            datt = jnp.where(tril, _dg(dob, vb, NT), 0.0)
            dv = _dg(att.astype(BF16), dob, TN) + _dg(kl.astype(BF16), dst1b, NT)
            dq = _dot3(datt, kt, NN) * e_qt + _dg(dob, st0b, NN) * eb
            dk = _dot3(datt, qt, TN) * e_kt + _dg(vb, dst1b, NN) * e_kl
            db = q * dq - k * dk
            last = lax.broadcasted_iota(jnp.int32, (CHUNK, 1), 0) == CHUNK - 1
            db = db + jnp.where(last, jnp.sum(dst1 * st1, axis=0, keepdims=True), 0.0)
            dbh, dbl = _split(db)
            triub = _ones_where(triu)
            dlf = _dg(triub, dbh, NN) + _dg(triub, dbl, NN)
            dst_ref[j] = dst1 * e_bl + _dg(dob, qd.astype(BF16), TN)
            dp_ref[0, :, ln] = (dq * (sq * (1.0 + qr * (1.0 - sq)))).astype(BF16)
            tmp = dlf / f - dk
            dp_ref[1, :, ln] = (tmp * (1.0 - lb) * sx * (1.0 - sx)).astype(BF16)
            dp_ref[2, :, ln] = dv.astype(BF16)
            dlb_acc[:, ln] += jnp.sum((1.0 - sx) * tmp, axis=0, keepdims=True)

            @pl.when(step == nc - 1)
            def _():
                d1 = dlb_acc[:, ln] * lb * (1.0 - lb)
                dclb_ref[:, ln] = jnp.where(lax.broadcasted_iota(jnp.int32, (2, HD), 0) == 0, -d1, d1)

        @pl.when((step == nc - 1) & (group == ng - 1))
        def _():
            dgam_ref[...] = dgam_acc[...]

    rev = lambda h, s: (nc - 1 - s, h)
    return pl.pallas_call(
        body, name="hgrn_bwd", grid=(ng, nc),
        in_specs=[_spec((4, CHUNK, wide), lambda h, s: (0, nc - 1 - s, h)),
                  _spec((None, 1, wide), lambda h, s: (R_CLB, 0, h)),
                  _spec((None, 1, wide), lambda h, s: (R_CLB + 1, 0, h)),
                  _spec((None, 1, HD), lambda h, s: (R_GAM, 0, 0)),
                  _spec((CHUNK, wide), rev),
                  _spec((None, hps, HD, HD), lambda h, s: (nc - 1 - s, h, 0, 0)),
                  _spec((CHUNK, wide), rev)],
        out_specs=[_spec((4, CHUNK, wide), lambda h, s: (0, nc - 1 - s, h)),
                   _spec((2, wide), lambda h, s: (0, h)),
                   _spec((1, HD), lambda h, s: (0, 0))],
        out_shape=[jax.ShapeDtypeStruct((4, t, D), BF16), jax.ShapeDtypeStruct((2, D), F32),
                   jax.ShapeDtypeStruct((1, HD), F32)],
        scratch_shapes=[pltpu.VMEM((hps, HD, HD), F32), pltpu.VMEM((1, wide), F32), pltpu.VMEM((1, HD), F32)],
        compiler_params=_params(("arbitrary", "arbitrary")),
    )(proj_c, small, small, small, o, sst, dyc)


def _adamw(name, w, g, m, v):
    rows, cols = w.shape
    br = rows
    for cand in (512, 352, 256):
        if rows % cand == 0:
            br = cand
            break
    c1 = 1.0 - ADAM_B1 ** ADAM_STEP
    c2 = 1.0 - ADAM_B2 ** ADAM_STEP

    def body(w_ref, g_ref, m_ref, v_ref, d_ref, mo_ref, vo_ref):
        gv = g_ref[...]
        mn = ADAM_B1 * m_ref[...] + (1.0 - ADAM_B1) * gv
        vn = ADAM_B2 * v_ref[...] + (1.0 - ADAM_B2) * (gv * gv)
        mo_ref[...] = mn
        vo_ref[...] = vn
        d_ref[...] = -ADAM_LR * ((mn / c1) / (jnp.sqrt(vn / c2) + ADAM_EPS) + ADAM_WD * w_ref[...])

    blk = _spec((br, cols), lambda i: (i, 0))
    shape = jax.ShapeDtypeStruct((rows, cols), F32)
    return pl.pallas_call(
        body, name=name, grid=(rows // br,), in_specs=[blk] * 4, out_specs=[blk] * 3, out_shape=[shape] * 3,
        compiler_params=_params(("parallel",)),
    )(w, g, m, v)


def _place():
    x, y, c = lax.axis_index("x"), lax.axis_index("y"), lax.axis_index("c")
    chips = [(1 - x, y), (x, 1 - y), (1 - x, 1 - y)]
    return x, y, c, chips


def _gather_weights(groups):
    ng = len(groups)

    def body(*refs):
        ins, outs = refs[:ng], refs[ng:2 * ng]
        send, recv = refs[2 * ng:]
        x, y, c, chips = _place()
        sibling = (x, y, 1 - c)

        def half(gi, chip, hc):
            return outs[gi].at[2 * chip[0] + chip[1], hc]

        def copy(gi, k, src, dst, to):
            return pltpu.make_async_remote_copy(src_ref=src, dst_ref=dst, send_sem=send.at[6 * gi + k],
                                                recv_sem=recv.at[6 * gi + k], device_id=to, device_id_type=MESH)

        first = [copy(gi, j, ins[gi].at[c], half(gi, (x, y), c), (*chip, c))
                 for gi in range(ng) for j, chip in enumerate(chips)]
        for cp in first:
            cp.start()
        passed = []
        for gi in range(ng):
            for j, chip in enumerate(chips):
                copy(gi, j, half(gi, chip, c), half(gi, chip, c), sibling).wait_recv()
                fwd = copy(gi, 3 + j, half(gi, chip, c), half(gi, chip, c), sibling)
                fwd.start()
                passed.append(fwd)
        for gi in range(ng):
            for j, chip in enumerate(chips):
                copy(gi, 3 + j, half(gi, chip, 1 - c), half(gi, chip, 1 - c), sibling).wait_recv()
        for cp in first + passed:
            cp.wait_send()

    outs = pl.pallas_call(
        body, name="gather_weights",
        in_specs=[ANY] * ng, out_specs=[ANY] * ng,
        out_shape=[jax.ShapeDtypeStruct((NSH,) + g.shape, g.dtype) for g in groups],
        scratch_shapes=[pltpu.SemaphoreType.DMA((6 * ng,)), pltpu.SemaphoreType.DMA((6 * ng,))],
        compiler_params=pltpu.CompilerParams(has_side_effects=True),
    )(*groups)
    me = 2 * lax.axis_index("x") + lax.axis_index("y")
    return [lax.dynamic_update_index_in_dim(o, g, me, 0) for o, g in zip(outs, groups)]


def _swap_halves(groups):
    ng = len(groups)

    def body(*refs):
        ins, outs = refs[:ng], refs[ng:2 * ng]
        send, recv = refs[2 * ng:]
        x, y, c, _ = _place()
        cps = []
        for gi in range(ng):
            half = ins[gi].shape[1] // 2
            rows = pl.ds(pl.multiple_of((1 - c) * half, half), half)
            cps.append(pltpu.make_async_remote_copy(src_ref=ins[gi].at[:, rows], dst_ref=outs[gi], send_sem=send.at[gi],
                                                    recv_sem=recv.at[gi], device_id=(x, y, 1 - c), device_id_type=MESH))
        for cp in cps:
            cp.start()
        for cp in cps:
            cp.wait()

    return pl.pallas_call(
        body, name="grad_swap_halves", in_specs=[ANY] * ng, out_specs=[ANY] * ng,
        out_shape=[jax.ShapeDtypeStruct((g.shape[0], g.shape[1] // 2, g.shape[2]), g.dtype) for g in groups],
        scratch_shapes=[pltpu.SemaphoreType.DMA((ng,)), pltpu.SemaphoreType.DMA((ng,))],
        compiler_params=pltpu.CompilerParams(has_side_effects=True),
    )(*groups)


def _send_to_owners(groups):
    ng = len(groups)

    def body(*refs):
        ins, outs = refs[:ng], refs[ng:2 * ng]
        send, recv = refs[2 * ng:]
        x, y, c, chips = _place()
        cps = [pltpu.make_async_remote_copy(src_ref=ins[gi].at[2 * chip[0] + chip[1]], dst_ref=outs[gi].at[j],
                                            send_sem=send.at[3 * gi + j], recv_sem=recv.at[3 * gi + j],
                                            device_id=(*chip, c), device_id_type=MESH)
               for gi in range(ng) for j, chip in enumerate(chips)]
        for cp in cps:
            cp.start()
        for cp in cps:
            cp.wait()

    return pl.pallas_call(
        body, name="grad_send_to_owners", in_specs=[ANY] * ng, out_specs=[ANY] * ng,
        out_shape=[jax.ShapeDtypeStruct((3,) + g.shape[1:], g.dtype) for g in groups],
        scratch_shapes=[pltpu.SemaphoreType.DMA((3 * ng,)), pltpu.SemaphoreType.DMA((3 * ng,))],
        compiler_params=pltpu.CompilerParams(has_side_effects=True),
    )(*groups)


def _share_halves(groups):
    ng = len(groups)

    def body(*refs):
        ins, outs = refs[:ng], refs[ng:2 * ng]
        send, recv = refs[2 * ng:]
        x, y, c, _ = _place()
        cps = [pltpu.make_async_remote_copy(src_ref=ins[gi], dst_ref=outs[gi].at[c], send_sem=send.at[gi],
                                            recv_sem=recv.at[gi], device_id=(x, y, 1 - c), device_id_type=MESH)
               for gi in range(ng)]
        for cp in cps:
            cp.start()
        for gi in range(ng):
            pltpu.make_async_remote_copy(src_ref=ins[gi], dst_ref=outs[gi].at[1 - c], send_sem=send.at[gi],
                                         recv_sem=recv.at[gi], device_id=(x, y, 1 - c), device_id_type=MESH).wait_recv()
        for cp in cps:
            cp.wait_send()

    outs = pl.pallas_call(
        body, name="grad_share_halves", in_specs=[ANY] * ng, out_specs=[ANY] * ng,
        out_shape=[jax.ShapeDtypeStruct((2,) + g.shape, g.dtype) for g in groups],
        scratch_shapes=[pltpu.SemaphoreType.DMA((ng,)), pltpu.SemaphoreType.DMA((ng,))],
        compiler_params=pltpu.CompilerParams(has_side_effects=True),
    )(*groups)
    c = lax.axis_index("c")
    return [lax.dynamic_update_index_in_dim(o, g, c, 0) for o, g in zip(outs, groups)]


def _pair_sum(name, g, got, c_idx):
    nsh, rows, cols = got.shape
    br = rows // 4

    def body(c_ref, a_ref, b_ref, o_ref):
        o_ref[...] = (a_ref[...].astype(F32) + b_ref[...].astype(F32)).astype(BF16)

    return pl.pallas_call(
        body, name=name,
        grid_spec=pltpu.PrefetchScalarGridSpec(
            num_scalar_prefetch=1, grid=(nsh, 4),
            in_specs=[pl.BlockSpec((None, br, cols), lambda q, i, cr: (q, 4 * cr[0] + i, 0)),
                      pl.BlockSpec((None, br, cols), lambda q, i, cr: (q, i, 0))],
            out_specs=pl.BlockSpec((None, br, cols), lambda q, i, cr: (q, i, 0))),
        out_shape=jax.ShapeDtypeStruct((nsh, rows, cols), BF16),
        compiler_params=_params(("parallel", "parallel")),
    )(c_idx, g, got)


def _owner_sum(name, own, got, p_idx):
    _, rows, cols = own.shape
    br = rows // 4

    def body(p_ref, a_ref, b_ref, o_ref):
        o_ref[...] = ((a_ref[...].astype(F32) + b_ref[0].astype(F32)) + b_ref[1].astype(F32)) + b_ref[2].astype(F32)

    return pl.pallas_call(
        body, name=name,
        grid_spec=pltpu.PrefetchScalarGridSpec(
            num_scalar_prefetch=1, grid=(4,),
            in_specs=[pl.BlockSpec((None, br, cols), lambda i, pr: (pr[0], i, 0)),
                      pl.BlockSpec((3, br, cols), lambda i, pr: (0, i, 0))],
            out_specs=pl.BlockSpec((br, cols), lambda i, pr: (i, 0))),
        out_shape=jax.ShapeDtypeStruct((rows, cols), F32),
        compiler_params=_params(("parallel",)),
    )(p_idx, own, got)


def _sum_small(slab):
    def body(in_ref, out_ref, all_ref, send, recv):
        x, y, c, _ = _place()
        me = 4 * x + 2 * y + c
        all_ref[me] = in_ref[...]
        cps = []
        for k in range(1, 8):
            peer = (x ^ (k >> 2), y ^ ((k >> 1) & 1), c ^ (k & 1))
            cps.append(pltpu.make_async_remote_copy(src_ref=in_ref, dst_ref=all_ref.at[me], send_sem=send.at[k - 1],
                                                    recv_sem=recv.at[k - 1], device_id=peer, device_id_type=MESH))
        for cp in cps:
            cp.start()
        for cp in cps:
            cp.wait()
        total = all_ref[0]
        for d in range(1, 8):
            total = total + all_ref[d]
        out_ref[...] = total

    return pl.pallas_call(
        body, name="sum_small",
        in_specs=[pl.BlockSpec(memory_space=pltpu.VMEM)], out_specs=pl.BlockSpec(memory_space=pltpu.VMEM),
        out_shape=jax.ShapeDtypeStruct(slab.shape, F32),
        scratch_shapes=[pltpu.VMEM((8,) + slab.shape, F32), pltpu.SemaphoreType.DMA((7,)), pltpu.SemaphoreType.DMA((7,))],
        compiler_params=pltpu.CompilerParams(has_side_effects=True),
    )(slab)


def _local_step(x, target, wa, wb, wc, conv_w, small):
    t = x.shape[0]
    tm = min(MM_TILE, t)
    tw = min(WGRAD_TILE, t)
    nt = t // tm
    tok_si = _spec((tm, D), lambda s, i: (i, 0))

    def ffn_forward(tag, h, norm_row, gate_idx, up_idx, down_idx):
        hn = _norm_fwd(tag + "_norm", h, small, norm_row)
        g, u, a = _ffn_up(tag + "_up", hn, wa, gate_idx, up_idx)
        return _ffn_down(tag + "_down", a, wb, down_idx, h), (h, hn, g, u, a)

    def out_proj(name, y, blk, h):
        return _mm(name, [(y, _spec((tm, 256), lambda i, k: (i, k)), wb, _spec((None, 256, D), lambda i, k: (k, blk, 0)))],
                   grid=(nt, NSH), o_shape=(t, D), o_dtype=F32, o_spec=_spec((tm, D), lambda i, k: (i, 0)),
                   dims=NN, kaxis=1, nk=NSH, acc_shape=(tm, D), res=(h, _spec((tm, D), lambda i, k: (i, 0))))

    def out_proj_bwd(tag, dhb, y, blk, grad_b):
        dy = _mm(tag + "_dy", [(dhb, tok_si, wb, _spec((None, 256, D), lambda s, i: (s, blk, 0)))],
                 grid=(NSH, nt), o_shape=(t, D), o_dtype=F32, o_spec=_spec((tm, 256), lambda s, i: (i, s)),
                 dims=NT, kaxis=1, nk=1)
        grad_b = _wgrad(tag + "_dwout", y, _spec((tw, 256), lambda s, k: (k, s)), dhb, _spec((tw, D), lambda s, k: (k, 0)),
                        256, D, t, tw, grad_b, blk)
        return dy, grad_b

    h0 = x
    h1, pre0 = ffn_forward("pre0", h0, R_PRE, 0, 2, 0)
    hn_ab = _norm_fwd("mix0_norm", h1, small, R_MIX)
    proj_a = _mm("ab_proj_a", [(hn_ab, tok_si, wc, _spec((None, D, 768), lambda s, i: (s, 0, 0)))],
                 grid=(2, nt), o_shape=(t, 1536), o_dtype=F32, o_spec=_spec((tm, 768), lambda s, i: (i, s)),
                 dims=NN, kaxis=1, nk=1)
    proj_b = _mm("ab_proj_b", [(hn_ab, tok_si, wc, _spec((None, D, 768), lambda s, i: (s + 2, 0, 0)))],
                 grid=(2, nt), o_shape=(t, 1536), o_dtype=BF16, o_spec=_spec((tm, 768), lambda s, i: (i, s)),
                 dims=NN, kaxis=1, nk=1)
    y_a = _conv_fwd(proj_a, conv_w)
    y_b, ltot, nblk = _sb_fwd(proj_b)
    y_ab = jnp.concatenate([y_a, y_b], axis=1)
    h2 = out_proj("ab_out", y_ab, B_ABOUT // 256, h1)
    h3, post0 = ffn_forward("post0", h2, R_POST, 4, 6, 2)
    h4, pre1 = ffn_forward("pre1", h3, R_PRE + 1, 1, 3, 1)
    hn_c = _norm_fwd("mix1_norm", h4, small, R_MIX + 1)
    proj_c = _mm("c_proj", [(hn_c, tok_si, wb, _spec((None, D, D), lambda s, i: (s, B_CIN // D, 0)))],
                 grid=(NSH, nt), o_shape=(NSH, t, D), o_dtype=F32, o_spec=_spec((None, tm, D), lambda s, i: (s, i, 0)),
                 dims=NN, kaxis=1, nk=1)
    o_c, y_c, sst = _hgrn_fwd(proj_c, small)
    h5 = out_proj("c_out", y_c, B_COUT // 256, h4)
    h6, post1 = ffn_forward("post1", h5, R_POST + 1, 5, 7, 3)
    dh, dhb, d_fin, loss = _final_loss(h6, small, target)

    dh, dhb, grad_a, grad_b, dn_post1 = _ffn_backward("post1", dh, dhb, *post1, wa, wb, 5, 7, 3, small, R_POST + 1,
                                                      8 * D, B_ROWS)
    dy_c, grad_b = out_proj_bwd("c", dhb, y_c, B_COUT // 256, grad_b)
    dproj_c, d_clb, d_gam = _hgrn_bwd(proj_c, small, o_c, sst, dy_c)
    grad_b = _wgrad("c_dwin", hn_c, _spec((tw, D), lambda s, k: (k, 0)), dproj_c, _spec((None, tw, D), lambda s, k: (s, k, 0)),
                    D, D, t, tw, grad_b, B_CIN // D)
    dhn = _mm("c_dhn", [(dproj_c, _spec((None, tm, D), lambda i, k: (k, i, 0)),
                         wb, _spec((None, D, D), lambda i, k: (k, B_CIN // D, 0)))],
              grid=(nt, NSH), o_shape=(t, D), o_dtype=F32, o_spec=_spec((tm, D), lambda i, k: (i, 0)),
              dims=NT, kaxis=1, nk=NSH, acc_shape=(tm, D))
    dh, dhb, dn_mix1 = _norm_bwd("mix1_norm_bwd", dhn, h4, small, R_MIX + 1, dh)
    dh, dhb, grad_a, grad_b, dn_pre1 = _ffn_backward("pre1", dh, dhb, *pre1, wa, wb, 1, 3, 1, small, R_PRE + 1,
                                                     grad_a, grad_b)
    dh, dhb, grad_a, grad_b, dn_post0 = _ffn_backward("post0", dh, dhb, *post0, wa, wb, 4, 6, 2, small, R_POST,
                                                      grad_a, grad_b)
    dy_ab, grad_b = out_proj_bwd("ab", dhb, y_ab, B_ABOUT // 256, grad_b)
    dab, dac, dax, d_conv = _conv_bwd(proj_a, conv_w, dy_ab)
    dq, dk, dv = _sb_bwd(proj_b, dy_ab, ltot, nblk)
    dproj_ab = jnp.concatenate([dab, dac, dax, dq, dk, dv], axis=1)
    grad_c = _wgrad("ab_dwin", hn_ab, _spec((tw, D), lambda s, k: (k, 0)), dproj_ab, _spec((tw, 768), lambda s, k: (k, s)),
                    D, 768, t, tw, D, 0)
    dhn = _mm("ab_dhn", [(dproj_ab, _spec((tm, 768), lambda i, k: (i, k)), wc, _spec((None, D, 768), lambda i, k: (k, 0, 0)))],
              grid=(nt, NSH), o_shape=(t, D), o_dtype=F32, o_spec=_spec((tm, D), lambda i, k: (i, 0)),
              dims=NT, kaxis=1, nk=NSH, acc_shape=(tm, D))
    dh, dhb, dn_mix0 = _norm_bwd("mix0_norm_bwd", dhn, h1, small, R_MIX, dh)
    dh, dhb, grad_a, grad_b, dn_pre0 = _ffn_backward("pre0", dh, dhb, *pre0, wa, wb, 0, 2, 0, small, R_PRE,
                                                     grad_a, grad_b)
    zero = jnp.zeros((1, D), F32)
    conv_rows = jnp.pad(jnp.transpose(d_conv[:, :3, :], (1, 0, 2)).reshape(3, 512), ((0, 0), (0, D - 512)))
    small_grad = jnp.concatenate([
        dn_pre0, dn_pre1, dn_mix0, dn_mix1, dn_post0, dn_post1, d_clb, d_fin,
        jnp.pad(d_gam, ((0, 0), (0, D - HD))), conv_rows,
        jnp.pad(loss, ((0, 0), (0, D - 1))), zero, zero], axis=0)
    return dh, grad_a, grad_b, grad_c, small_grad


def _small_slab(rows):
    parts = [jnp.pad(r.astype(F32), ((0, 0), (0, D - r.shape[1]))) for r in rows]
    slab = jnp.concatenate(parts, axis=0)
    return jnp.pad(slab, ((0, SMALL_ROWS - slab.shape[0]), (0, 0)))


def kernel(x, ffn_pre_norm, ffn_pre_w_gate, ffn_pre_w_up, ffn_pre_w_down, mix_norm, ffn_post_norm, ffn_post_w_gate, ffn_post_w_up, ffn_post_w_down, ab_w_in, ab_conv_w, ab_w_out, c_w_in, c_lower_bounds, c_out_norm, c_w_out, final_norm, loss_target, m_ffn_pre_norm, m_ffn_pre_w_gate, m_ffn_pre_w_up, m_ffn_pre_w_down, m_mix_norm, m_ffn_post_norm, m_ffn_post_w_gate, m_ffn_post_w_up, m_ffn_post_w_down, m_ab_w_in, m_ab_conv_w, m_ab_w_out, m_c_w_in, m_c_lower_bounds, m_c_out_norm, m_c_w_out, m_final_norm, v_ffn_pre_norm, v_ffn_pre_w_gate, v_ffn_pre_w_up, v_ffn_pre_w_down, v_mix_norm, v_ffn_post_norm, v_ffn_post_w_gate, v_ffn_post_w_up, v_ffn_post_w_down, v_ab_w_in, v_ab_conv_w, v_ab_w_out, v_c_w_in, v_c_lower_bounds, v_c_out_norm, v_c_w_out, v_final_norm):
    t = x.shape[1]
    xi, yi, ci = lax.axis_index("x"), lax.axis_index("y"), lax.axis_index("c")
    p_idx = (2 * xi + yi).astype(jnp.int32).reshape(1)
    c_idx = ci.astype(jnp.int32).reshape(1)

    ga = jnp.concatenate([ffn_pre_w_gate, ffn_pre_w_up, ffn_post_w_gate, ffn_post_w_up], axis=0).astype(BF16)
    gb = jnp.concatenate([ffn_pre_w_down.reshape(2 * FS, D), ffn_post_w_down.reshape(2 * FS, D),
                          ab_w_out[0], c_w_in[0], c_w_out[0]], axis=0).astype(BF16)
    gc = ab_w_in[0].astype(BF16)
    gd = jnp.pad(ab_conv_w[0], ((0, 5), (0, 0)))
    wa, wb, wc, conv_w = _gather_weights([ga.reshape(2, 4 * D, FS), gb.reshape(2, B_ROWS // 2, D),
                                          gc.reshape(2, D // 2, 768), jnp.stack([gd, jnp.zeros_like(gd)])])
    wa = wa.reshape(NSH, 8, D, FS)
    wb = wb.reshape(NSH, B_ROWS, D)
    wc = wc.reshape(NSH, D, 768)

    small = _small_slab([ffn_pre_norm, mix_norm, ffn_post_norm, c_lower_bounds, final_norm.reshape(1, D), c_out_norm])
    small = small.reshape(SMALL_ROWS, 1, D)

    grad_x, grad_a, grad_b, grad_c, small_grad = _local_step(x[0], loss_target[0], wa, wb, wc, conv_w, small)

    full = [grad_a, grad_b, grad_c]
    got = _swap_halves(full)
    pair = [_pair_sum("grad_pair_sum_%d" % n, g, r, c_idx) for n, (g, r) in enumerate(zip(full, got))]
    got = _send_to_owners(pair)
    mine = [_owner_sum("grad_owner_sum_%d" % n, g, r, p_idx) for n, (g, r) in enumerate(zip(pair, got))]
    ra, rb, rc = _share_halves(mine)
    ra = ra.reshape(8, D, FS)
    rb = rb.reshape(B_ROWS, D)
    rc = rc.reshape(D, 768)
    small_sum = _sum_small(small_grad)

    my_conv = lax.dynamic_slice(small_sum[R_CONV:R_CONV + 3], (0, (2 * xi + yi) * 128), (3, 128))
    grads = {
        "ffn_pre_norm": small_sum[R_PRE:R_PRE + 2], "ffn_pre_w_gate": ra[0:2], "ffn_pre_w_up": ra[2:4],
        "ffn_pre_w_down": rb[0:2 * FS].reshape(2, FS, D), "mix_norm": small_sum[R_MIX:R_MIX + 2],
        "ffn_post_norm": small_sum[R_POST:R_POST + 2], "ffn_post_w_gate": ra[4:6], "ffn_post_w_up": ra[6:8],
        "ffn_post_w_down": rb[2 * FS:4 * FS].reshape(2, FS, D), "ab_w_in": rc.reshape(1, D, 768),
        "ab_conv_w": my_conv.reshape(1, 3, 128), "ab_w_out": rb[B_ABOUT:B_ABOUT + 256].reshape(1, 256, D),
        "c_w_in": rb[B_CIN:B_CIN + D].reshape(1, D, D), "c_lower_bounds": small_sum[R_CLB:R_CLB + 2],
        "c_out_norm": small_sum[R_GAM:R_GAM + 1, :HD], "c_w_out": rb[B_COUT:B_COUT + 256].reshape(1, 256, D),
        "final_norm": small_sum[R_FIN],
    }
    weights = dict(ffn_pre_norm=ffn_pre_norm, ffn_pre_w_gate=ffn_pre_w_gate, ffn_pre_w_up=ffn_pre_w_up, ffn_pre_w_down=ffn_pre_w_down, mix_norm=mix_norm, ffn_post_norm=ffn_post_norm, ffn_post_w_gate=ffn_post_w_gate, ffn_post_w_up=ffn_post_w_up, ffn_post_w_down=ffn_post_w_down, ab_w_in=ab_w_in, ab_conv_w=ab_conv_w, ab_w_out=ab_w_out, c_w_in=c_w_in, c_lower_bounds=c_lower_bounds, c_out_norm=c_out_norm, c_w_out=c_w_out, final_norm=final_norm)
    m_in = dict(ffn_pre_norm=m_ffn_pre_norm, ffn_pre_w_gate=m_ffn_pre_w_gate, ffn_pre_w_up=m_ffn_pre_w_up, ffn_pre_w_down=m_ffn_pre_w_down, mix_norm=m_mix_norm, ffn_post_norm=m_ffn_post_norm, ffn_post_w_gate=m_ffn_post_w_gate, ffn_post_w_up=m_ffn_post_w_up, ffn_post_w_down=m_ffn_post_w_down, ab_w_in=m_ab_w_in, ab_conv_w=m_ab_conv_w, ab_w_out=m_ab_w_out, c_w_in=m_c_w_in, c_lower_bounds=m_c_lower_bounds, c_out_norm=m_c_out_norm, c_w_out=m_c_w_out, final_norm=m_final_norm)
    v_in = dict(ffn_pre_norm=v_ffn_pre_norm, ffn_pre_w_gate=v_ffn_pre_w_gate, ffn_pre_w_up=v_ffn_pre_w_up, ffn_pre_w_down=v_ffn_pre_w_down, mix_norm=v_mix_norm, ffn_post_norm=v_ffn_post_norm, ffn_post_w_gate=v_ffn_post_w_gate, ffn_post_w_up=v_ffn_post_w_up, ffn_post_w_down=v_ffn_post_w_down, ab_w_in=v_ab_w_in, ab_conv_w=v_ab_conv_w, ab_w_out=v_ab_w_out, c_w_in=v_c_w_in, c_lower_bounds=v_c_lower_bounds, c_out_norm=v_c_out_norm, c_w_out=v_c_w_out, final_norm=v_final_norm)
    names = list(weights)
    big = [n for n in names if weights[n].size >= 65536]
    tiny = [n for n in names if n not in big]

    delta, new_m, new_v = {}, {}, {}
    for n in big:
        shape = weights[n].shape
        two_d = (shape[0] * shape[1], shape[2])
        d, m2, v2 = _adamw("adamw_" + n, weights[n].reshape(two_d), grads[n].reshape(two_d),
                           m_in[n].reshape(two_d), v_in[n].reshape(two_d))
        delta[n], new_m[n], new_v[n] = d.reshape(shape), m2.reshape(shape), v2.reshape(shape)

    def tiny_slab(src):
        return _small_slab([src[n].reshape(-1, src[n].shape[-1]) for n in tiny])

    offs, row = {}, 0
    for n in tiny:
        nrows = weights[n].size // weights[n].shape[-1]
        offs[n] = (row, nrows)
        row += nrows
    d, m2, v2 = _adamw("adamw_small", tiny_slab(weights), tiny_slab(grads), tiny_slab(m_in), tiny_slab(v_in))
    for n in tiny:
        r0, nr = offs[n]
        shape = weights[n].shape
        for dst, src in ((delta, d), (new_m, m2), (new_v, v2)):
            dst[n] = src[r0:r0 + nr, :shape[-1]].reshape(shape)

    loss = small_sum[R_LOSS, 0]
    return (loss, grad_x.reshape(1, t, D), *[grads[n] for n in names], *[delta[n] for n in names],
            *[new_m[n] for n in names], *[new_v[n] for n in names])
```

```python
import functools
import math

import jax
import jax.numpy as jnp
from jax import lax
from jax.experimental import pallas as pl
from jax.experimental.pallas import tpu as pltpu

F32 = jnp.float32
BF16 = jnp.bfloat16
MESH = pl.DeviceIdType.MESH
ANY = pl.BlockSpec(memory_space=pl.ANY)

D = 1024
FS = 704
NSH = 4
RMS_EPS = 1e-6
MACARON = 0.5
CHUNK = 64
HD = 128
HGRN_HPS = 4
SBQ = 128
SB_PAIRS = 2
SB_DEAD = -105.0
CONV_HALO = 8
LANE = 128
ROW_TILE = 512
MM_TILE = 1024
WGRAD_TILE = 4096
VMEM_LIMIT = 48 * 1024 * 1024

ADAM_LR, ADAM_B1, ADAM_B2, ADAM_EPS, ADAM_WD, ADAM_STEP = 0.001, 0.9, 0.999, 1e-08, 0.01, 10

NN = ((1,), (0,))
NT = ((1,), (1,))
TN = ((0,), (0,))

SMALL_ROWS = 16
R_PRE, R_MIX, R_POST, R_CLB, R_FIN, R_GAM, R_CONV, R_LOSS = 0, 2, 4, 6, 8, 9, 10, 13

B_DOWN = 0
B_ABOUT = 4 * FS
B_CIN = B_ABOUT + 256
B_COUT = B_CIN + 1024
B_ROWS = B_COUT + 256


def _dg(a, b, dims):
    return lax.dot_general(a, b, (dims, ((), ())), preferred_element_type=F32)


def _split(x):
    hi = x.astype(BF16)
    lo = (x - hi.astype(F32)).astype(BF16)
    return hi, lo


def _dot3(a, b, dims):
    ah, al = _split(a)
    bh, bl = _split(b)
    return _dg(ah, bh, dims) + _dg(ah, bl, dims) + _dg(al, bh, dims)


def _sigmoid(x):
    return 1.0 / (1.0 + jnp.exp(-x))


def _params(sem):
    return pltpu.CompilerParams(dimension_semantics=sem, vmem_limit_bytes=VMEM_LIMIT)


def _spec(shape, imap):
    return pl.BlockSpec(shape, imap)


def _mm(name, pairs, *, grid, o_shape, o_dtype, o_spec, dims, kaxis, nk, acc_shape=None, res=None, scale=None,
        into=None, carry=None):
    npairs = len(pairs)
    operands, specs = [], []
    for a, a_spec, b, b_spec in pairs:
        operands += [a, b]
        specs += [a_spec, b_spec]
    if res is not None:
        operands.append(res[0])
        specs.append(res[1])
    aliases = {}
    if into is not None:
        aliases = {len(operands): 0}
        operands.append(into)
        specs.append(ANY)
    n_own = len(operands)
    nc = carry.n if carry is not None else 0
    if nc:
        operands += carry.groups
        specs += carry.in_specs

    def body(*refs):
        o_ref = refs[n_own + nc]
        riders = (refs[n_own:n_own + nc], refs[n_own + nc + 1:n_own + 2 * nc + 1], refs[-2], refs[-1]) if nc else None
        if nc:
            carry.ride(grid, riders, "start")
        part = None
        for n in range(npairs):
            d = _dg(refs[2 * n][...], refs[2 * n + 1][...], dims)
            part = d if part is None else part + d

        def finish(val):
            if scale is not None:
                val = val * scale
            if res is not None:
                val = val + refs[2 * npairs][...]
            o_ref[...] = val.astype(o_dtype)

        if nk == 1:
            finish(part)
        else:
            acc_ref = refs[n_own + 2 * nc + 1]
            k = pl.program_id(kaxis)

            @pl.when(k == 0)
            def _():
                acc_ref[...] = part

            @pl.when(k > 0)
            def _():
                acc_ref[...] += part

            @pl.when(k == nk - 1)
            def _():
                finish(acc_ref[...])
        if nc:
            carry.ride(grid, riders, "finish")

    sem = tuple("arbitrary" if (nc or (ax == kaxis and nk > 1)) else "parallel" for ax in range(len(grid)))
    outs = pl.pallas_call(
        body, name=name, grid=grid, in_specs=specs, out_specs=[o_spec] + (carry.out_specs if nc else []),
        out_shape=[jax.ShapeDtypeStruct(o_shape, o_dtype)] + (carry.out_shape if nc else []),
        scratch_shapes=([pltpu.VMEM(acc_shape, F32)] if nk > 1 else []) + (carry.scratch if nc else []),
        input_output_aliases=aliases,
        compiler_params=_params(sem),
    )(*operands)
    return (outs[0], carry.place(outs[1:])) if nc else outs[0]


def _norm_fwd(name, h, gain_slab, row):
    t = h.shape[0]
    tm = min(ROW_TILE, t)

    def body(h_ref, g_ref, o_ref):
        x = h_ref[...]
        r = lax.rsqrt(jnp.mean(x * x, axis=-1, keepdims=True) + RMS_EPS)
        o_ref[...] = (x * r * g_ref[...]).astype(BF16)

    return pl.pallas_call(
        body, name=name, grid=(t // tm,),
        in_specs=[_spec((tm, D), lambda i: (i, 0)), _spec((None, 1, D), lambda i: (row, 0, 0))],
        out_specs=_spec((tm, D), lambda i: (i, 0)),
        out_shape=jax.ShapeDtypeStruct((t, D), BF16),
        compiler_params=_params(("parallel",)),
    )(h, gain_slab)


def _norm_bwd(name, dhn, h, gain_slab, row, dres):
    t = h.shape[0]
    tm = min(ROW_TILE, t)
    nt = t // tm

    def body(dhn_ref, h_ref, g_ref, dres_ref, dh_ref, dhb_ref, dg_ref, acc_ref):
        i = pl.program_id(0)
        x = h_ref[...]
        r = lax.rsqrt(jnp.mean(x * x, axis=-1, keepdims=True) + RMS_EPS)
        xh = x * r
        dy = dhn_ref[...]
        gdy = dy * g_ref[...]
        dx = (gdy - xh * jnp.mean(gdy * xh, axis=-1, keepdims=True)) * r
        dh = dres_ref[...] + dx
        dh_ref[...] = dh
        dhb_ref[...] = dh.astype(BF16)
        part = jnp.sum((dy * xh).reshape(tm // 8, 8, D), axis=0)

        @pl.when(i == 0)
        def _():
            acc_ref[...] = part

        @pl.when(i > 0)
        def _():
            acc_ref[...] += part

        @pl.when(i == nt - 1)
        def _():
            dg_ref[...] = jnp.sum(acc_ref[...], axis=0, keepdims=True)

    row_spec = _spec((tm, D), lambda i: (i, 0))
    return pl.pallas_call(
        body, name=name, grid=(nt,),
        in_specs=[row_spec, row_spec, _spec((None, 1, D), lambda i: (row, 0, 0)), row_spec],
        out_specs=[row_spec, row_spec, _spec((1, D), lambda i: (0, 0))],
        out_shape=[jax.ShapeDtypeStruct((t, D), F32), jax.ShapeDtypeStruct((t, D), BF16),
                   jax.ShapeDtypeStruct((1, D), F32)],
        scratch_shapes=[pltpu.VMEM((8, D), F32)],
        compiler_params=_params(("arbitrary",)),
    )(dhn, h, gain_slab, dres)


def _final_loss(h, gain_slab, target):
    t = h.shape[0]
    tm = min(ROW_TILE, t)
    nt = t // tm

    def body(h_ref, g_ref, t_ref, dh_ref, dhb_ref, dg_ref, loss_ref, acc_ref, lacc_ref):
        i = pl.program_id(0)
        x = h_ref[...]
        g = g_ref[...]
        r = lax.rsqrt(jnp.mean(x * x, axis=-1, keepdims=True) + RMS_EPS)
        xh = x * r
        err = xh * g - t_ref[...]
        dy = err * (1.0 / D)
        gdy = dy * g
        dh = (gdy - xh * jnp.mean(gdy * xh, axis=-1, keepdims=True)) * r
        dh_ref[...] = dh
        dhb_ref[...] = dh.astype(BF16)
        part = jnp.sum((dy * xh).reshape(tm // 8, 8, D), axis=0)
        lpart = jnp.sum((err * err).reshape(tm // 8, 8, D), axis=0)

        @pl.when(i == 0)
        def _():
            acc_ref[...] = part
            lacc_ref[...] = lpart

        @pl.when(i > 0)
        def _():
            acc_ref[...] += part
            lacc_ref[...] += lpart

        @pl.when(i == nt - 1)
        def _():
            dg_ref[...] = jnp.sum(acc_ref[...], axis=0, keepdims=True)
            rows = jnp.sum(lacc_ref[...], axis=0, keepdims=True)
            loss_ref[...] = jnp.sum(rows, axis=1, keepdims=True) * (0.5 / D)

    row_spec = _spec((tm, D), lambda i: (i, 0))
    return pl.pallas_call(
        body, name="final_loss", grid=(nt,),
        in_specs=[row_spec, _spec((None, 1, D), lambda i: (R_FIN, 0, 0)), row_spec],
        out_specs=[row_spec, row_spec, _spec((1, D), lambda i: (0, 0)), _spec((1, 1), lambda i: (0, 0))],
        out_shape=[jax.ShapeDtypeStruct((t, D), F32), jax.ShapeDtypeStruct((t, D), BF16),
                   jax.ShapeDtypeStruct((1, D), F32), jax.ShapeDtypeStruct((1, 1), F32)],
        scratch_shapes=[pltpu.VMEM((8, D), F32), pltpu.VMEM((8, D), F32)],
        compiler_params=_params(("arbitrary",)),
    )(h, gain_slab, target)


def _ffn_up(name, hn, wgu, carry=None):
    t = hn.shape[0]
    tm = min(MM_TILE, t)
    grid = (NSH, t // tm)
    nc = carry.n if carry is not None else 0

    def body(*refs):
        x_ref, wg_ref, wu_ref = refs[:3]
        s_up_ref, s_gate_ref, a_ref = refs[3 + nc:6 + nc]
        riders = (refs[3:3 + nc], refs[6 + nc:6 + 2 * nc], refs[-2], refs[-1]) if nc else None
        if nc:
            carry.ride(grid, riders, "start")
        x = x_ref[...]
        g = _dg(x, wg_ref[...], NN)
        u = _dg(x, wu_ref[...], NN)
        sg = _sigmoid(g)
        silu = g * sg
        s_up_ref[...] = (MACARON * silu).astype(BF16)
        s_gate_ref[...] = (MACARON * u * (sg * (1.0 + g * (1.0 - sg)))).astype(BF16)
        a_ref[...] = (silu * u).astype(BF16)
        if nc:
            carry.ride(grid, riders, "finish")

    act = _spec((None, tm, FS), lambda s, i: (s, i, 0))
    shape = jax.ShapeDtypeStruct((NSH, t, FS), BF16)
    outs = pl.pallas_call(
        body, name=name, grid=grid,
        in_specs=[_spec((tm, D), lambda s, i: (i, 0)),
                  _spec((None, None, D, FS), lambda s, i: (s, 0, 0, 0)),
                  _spec((None, None, D, FS), lambda s, i: (s, 1, 0, 0))] + (carry.in_specs if nc else []),
        out_specs=[act, act, act] + (carry.out_specs if nc else []),
        out_shape=[shape, shape, shape] + (carry.out_shape if nc else []),
        scratch_shapes=carry.scratch if nc else [],
        compiler_params=_params(("arbitrary", "arbitrary") if nc else ("parallel", "parallel")),
    )(hn, wgu, wgu, *(carry.groups if nc else []))
    return (outs[0], outs[1], outs[2], carry.place(outs[3:]) if nc else [])


def _ffn_down(name, a, wd, h, carry=None):
    t = h.shape[0]
    tm = min(MM_TILE, t)
    return _mm(name, [(a, _spec((None, tm, FS), lambda i, k: (k, i, 0)),
                       wd, _spec((None, FS, D), lambda i, k: (k, 0, 0)))],
               grid=(t // tm, NSH), o_shape=(t, D), o_dtype=F32, o_spec=_spec((tm, D), lambda i, k: (i, 0)),
               dims=NN, kaxis=1, nk=NSH, acc_shape=(tm, D), res=(h, _spec((tm, D), lambda i, k: (i, 0))),
               scale=MACARON, carry=carry)


def _ffn_bwd_up(name, dhb, wd, s_up, s_gate):
    t = dhb.shape[0]
    tm = min(MM_TILE, t)

    def body(dh_ref, wd_ref, s_up_ref, s_gate_ref, dg_ref, du_ref):
        da = _dg(dh_ref[...], wd_ref[...], NT)
        du_ref[...] = (da * s_up_ref[...].astype(F32)).astype(BF16)
        dg_ref[...] = (da * s_gate_ref[...].astype(F32)).astype(BF16)

    act = _spec((None, tm, FS), lambda s, i: (s, i, 0))
    shape = jax.ShapeDtypeStruct((NSH, t, FS), BF16)
    return pl.pallas_call(
        body, name=name, grid=(NSH, t // tm),
        in_specs=[_spec((tm, D), lambda s, i: (i, 0)), _spec((None, FS, D), lambda s, i: (s, 0, 0)), act, act],
        out_specs=[act, act], out_shape=[shape, shape],
        compiler_params=_params(("parallel", "parallel")),
    )(dhb, wd, s_up, s_gate)


def _wgrad(name, a, a_spec, b, b_spec, out_rows, out_cols, t, tt, group, slot, scale=None):
    first = isinstance(group, int)
    rows = group if first else group.shape[1]
    return _mm(name, [(a, a_spec, b, b_spec)], grid=(NSH, t // tt),
               o_shape=(NSH, rows, out_cols), o_dtype=BF16,
               o_spec=_spec((None, out_rows, out_cols), lambda s, k: (s, slot, 0)),
               dims=TN, kaxis=1, nk=t // tt, acc_shape=(out_rows, out_cols), scale=scale,
               into=None if first else group)


def _ffn_backward(tag, dh, dhb, h_in, hn, s_up, s_gate, a, wgu, wd, gate_idx, up_idx, down_idx, small, norm_row,
                  grad_a, grad_b):
    t = dh.shape[0]
    tm = min(MM_TILE, t)
    tt = min(WGRAD_TILE, t)
    dg, du = _ffn_bwd_up(tag + "_bwd_up", dhb, wd, s_up, s_gate)
    tok = _spec((tt, D), lambda s, k: (k, 0))
    hid = _spec((None, tt, FS), lambda s, k: (s, k, 0))
    grad_b = _wgrad(tag + "_dwd", a, hid, dhb, tok, FS, D, t, tt, grad_b, down_idx, scale=MACARON)
    grad_a = _wgrad(tag + "_dwg", hn, tok, dg, hid, D, FS, t, tt, grad_a, gate_idx)
    grad_a = _wgrad(tag + "_dwu", hn, tok, du, hid, D, FS, t, tt, grad_a, up_idx)
    act = _spec((None, tm, FS), lambda i, k: (k, i, 0))
    dhn = _mm(tag + "_dhn",
              [(dg, act, wgu, _spec((None, None, D, FS), lambda i, k: (k, 0, 0, 0))),
               (du, act, wgu, _spec((None, None, D, FS), lambda i, k: (k, 1, 0, 0)))],
              grid=(t // tm, NSH), o_shape=(t, D), o_dtype=F32, o_spec=_spec((tm, D), lambda i, k: (i, 0)),
              dims=NT, kaxis=1, nk=NSH, acc_shape=(tm, D))
    dh_in, dhb_in, d_gain = _norm_bwd(tag + "_norm_bwd", dhn, h_in, small, norm_row, dh)
    return dh_in, dhb_in, grad_a, grad_b, d_gain


def _conv_fwd(proj_a, conv_w):
    t = proj_a.shape[0]
    tm = min(ROW_TILE, t)
    hb = tm // CONV_HALO

    def body(ab_ref, ac_ref, ax_ref, acp_ref, axp_ref, w_ref, y_ref):
        i = pl.program_id(1)
        u = ac_ref[...] * ax_ref[...]
        up = jnp.where(i > 0, acp_ref[...] * axp_ref[...], 0.0)
        ext = jnp.concatenate([up, u], axis=0)
        u1 = pltpu.roll(ext, 1, 0)[CONV_HALO:]
        u2 = pltpu.roll(ext, 2, 0)[CONV_HALO:]
        w = w_ref[...]
        conv = w[0:1] * u2 + w[1:2] * u1 + w[2:3] * u
        y_ref[...] = (ab_ref[...] * conv).astype(BF16)

    def cur(off):
        return _spec((tm, LANE), lambda j, i: (i, off + j))

    def prev(off):
        return _spec((CONV_HALO, LANE), lambda j, i: (jnp.maximum(i * hb - 1, 0), off + j))

    return pl.pallas_call(
        body, name="conv_fwd", grid=(4, t // tm),
        in_specs=[cur(0), cur(4), cur(8), prev(4), prev(8),
                  _spec((None, None, 8, LANE), lambda j, i: (j, 0, 0, 0))],
        out_specs=_spec((tm, LANE), lambda j, i: (i, j)),
        out_shape=jax.ShapeDtypeStruct((t, 512), BF16),
        compiler_params=_params(("parallel", "parallel")),
    )(proj_a, proj_a, proj_a, proj_a, proj_a, conv_w)


def _conv_bwd(proj_a, conv_w, dy):
    t = proj_a.shape[0]
    tm = min(ROW_TILE, t)
    hb = tm // CONV_HALO
    nt = t // tm

    def body(ab_ref, ac_ref, ax_ref, dy_ref, acp_ref, axp_ref, abn_ref, dyn_ref, w_ref,
             dab_ref, dac_ref, dax_ref, dw_ref, acc_ref):
        i = pl.program_id(1)
        ab, ac, ax = ab_ref[...], ac_ref[...], ax_ref[...]
        u = ac * ax
        up = jnp.where(i > 0, acp_ref[...] * axp_ref[...], 0.0)
        ext = jnp.concatenate([up, u], axis=0)
        u1 = pltpu.roll(ext, 1, 0)[CONV_HALO:]
        u2 = pltpu.roll(ext, 2, 0)[CONV_HALO:]
        w = w_ref[...]
        conv = w[0:1] * u2 + w[1:2] * u1 + w[2:3] * u
        dy_v = dy_ref[...]
        dab_ref[...] = (dy_v * conv).astype(BF16)
        dc = dy_v * ab
        dcn = jnp.where(i < nt - 1, dyn_ref[...] * abn_ref[...], 0.0)
        extn = jnp.concatenate([dc, dcn], axis=0)
        n = tm + CONV_HALO
        dc1 = pltpu.roll(extn, n - 1, 0)[:tm]
        dc2 = pltpu.roll(extn, n - 2, 0)[:tm]
        du = w[2:3] * dc + w[1:2] * dc1 + w[0:1] * dc2
        dac_ref[...] = (du * ax).astype(BF16)
        dax_ref[...] = (du * ac).astype(BF16)
        rid = lax.broadcasted_iota(jnp.int32, (8, LANE), 0)
        part = jnp.where(rid == 0, jnp.sum(dc * u2, axis=0, keepdims=True),
                         jnp.where(rid == 1, jnp.sum(dc * u1, axis=0, keepdims=True),
                                   jnp.where(rid == 2, jnp.sum(dc * u, axis=0, keepdims=True), 0.0)))

        @pl.when(i == 0)
        def _():
            acc_ref[...] = part

        @pl.when(i > 0)
        def _():
            acc_ref[...] += part

        @pl.when(i == nt - 1)
        def _():
            dw_ref[...] = acc_ref[...]

    def cur(off):
        return _spec((tm, LANE), lambda j, i: (i, off + j))

    def prev(off):
        return _spec((CONV_HALO, LANE), lambda j, i: (jnp.maximum(i * hb - 1, 0), off + j))

    def nxt(off):
        return _spec((CONV_HALO, LANE), lambda j, i: (jnp.minimum((i + 1) * hb, nt * hb - 1), off + j))

    outs = pl.pallas_call(
        body, name="conv_bwd", grid=(4, nt),
        in_specs=[cur(0), cur(4), cur(8), cur(0), prev(4), prev(8), nxt(0), nxt(0),
                  _spec((None, None, 8, LANE), lambda j, i: (j, 0, 0, 0))],
        out_specs=[_spec((tm, LANE), lambda j, i: (i, j)), _spec((tm, LANE), lambda j, i: (i, j)),
                   _spec((tm, LANE), lambda j, i: (i, j)), _spec((None, 8, LANE), lambda j, i: (j, 0, 0))],
        out_shape=[jax.ShapeDtypeStruct((t, 512), BF16), jax.ShapeDtypeStruct((t, 512), BF16),
                   jax.ShapeDtypeStruct((t, 512), BF16), jax.ShapeDtypeStruct((4, 8, LANE), F32)],
        scratch_shapes=[pltpu.VMEM((8, LANE), F32)],
        compiler_params=_params(("parallel", "arbitrary")),
    )(proj_a, proj_a, proj_a, dy, proj_a, proj_a, proj_a, dy, conv_w)
    return outs


def _log_sigmoid(z):
    return jnp.minimum(z, 0.0) - jnp.log(1.0 + jnp.exp(-jnp.abs(z)))


def _sb_masks():
    row = lax.broadcasted_iota(jnp.int32, (SBQ, SBQ), 0)
    col = lax.broadcasted_iota(jnp.int32, (SBQ, SBQ), 1)
    return row, col


def _ones_where(mask):
    return jnp.where(mask, 1.0, 0.0).astype(BF16)


def _head_mask(head):
    lane = lax.broadcasted_iota(jnp.int32, (1, LANE), 1)
    return lane >= 64 if head else lane < 64


def _sb_fwd(proj_b, carry=None):
    t = proj_b.shape[0]
    nq = t // SBQ
    scale = 1.0 / math.sqrt(64.0)

    npair = SB_PAIRS
    wide = npair * LANE
    ngrp = 4 // npair
    chains = [(p, head) for p in range(npair) for head in range(2)]

    grid = (ngrp, nq)
    nc = carry.n if carry is not None else 0

    def body(*refs):
        q_ref, k_ref, v_ref = refs[:3]
        y_ref, l_ref, n_ref = refs[3 + nc:6 + nc]
        riders = (refs[3:3 + nc], refs[6 + nc:6 + 2 * nc], refs[-2], refs[-1]) if nc else None
        if nc:
            carry.ride(grid, riders, "start")
        grp = pl.program_id(0)
        qi = pl.program_id(1)
        row, col = _sb_masks()
        strict = col < row
        m_suffix = _ones_where(row > col)
        qh = []
        for p, head in chains:
            q_all = q_ref[:, p * LANE:(p + 1) * LANE]
            qh.append(jnp.where(_head_mask(head), q_all, jnp.zeros_like(q_all)))

        def block(kb, state, diag):
            start = pl.multiple_of(kb * SBQ, SBQ)
            new = []
            for n, (p, head) in enumerate(chains):
                run, acc = state[n]
                kk = k_ref[pl.ds(start, SBQ), p * LANE:(p + 1) * LANE]
                vv = v_ref[pl.ds(start, SBQ), p * LANE:(p + 1) * LANE]
                z = _dg(qh[n], kk, NT) * scale
                lb = _log_sigmoid(z)
                lk = lb - z
                if diag:
                    lk = jnp.where(strict, lk, 0.0)
                hi, lo = _split(lk)
                later = run + _dg(hi, m_suffix, NN) + _dg(lo, m_suffix, NN)
                w = jnp.exp(lb + later)
                if diag:
                    w = jnp.where(strict, w, 0.0)
                acc = acc + _dg(w.astype(BF16), vv, NN)
                run = run + jnp.sum(hi.astype(F32) + lo.astype(F32), axis=1, keepdims=True)
                new.append((run, acc))
            return tuple(new)

        zero = (jnp.zeros((SBQ, 1), F32), jnp.zeros((SBQ, LANE), F32))
        state = block(qi, tuple(zero for _ in chains), True)

        def live(c):
            top = c[1][0][0]
            for n in range(1, len(chains)):
                top = jnp.maximum(top, c[1][n][0])
            return jnp.logical_and(c[0] < qi, jnp.max(top) > SB_DEAD)

        def step(c):
            return c[0] + 1, block(qi - 1 - c[0], c[1], False)

        count, state = lax.while_loop(live, step, (jnp.int32(0), state))
        n_ref[grp * nq + qi] = count.astype(F32)
        for p in range(npair):
            (run0, acc0), (run1, acc1) = state[2 * p], state[2 * p + 1]
            hm = _head_mask(0)
            y_ref[:, p * LANE:(p + 1) * LANE] = jnp.where(hm, acc0, acc1).astype(BF16)
            l_ref[p] = jnp.where(hm, run0, run1)
        if nc:
            carry.ride(grid, riders, "finish")

    outs = pl.pallas_call(
        body, name="sb_fwd", grid=grid,
        in_specs=[_spec((SBQ, wide), lambda g, i: (i, g)),
                  _spec((t, wide), lambda g, i: (0, ngrp + g)),
                  _spec((t, wide), lambda g, i: (0, 2 * ngrp + g))] + (carry.in_specs if nc else []),
        out_specs=[_spec((SBQ, wide), lambda g, i: (i, g)), _spec((npair, SBQ, LANE), lambda g, i: (g, i, 0)),
                   pl.BlockSpec(memory_space=pltpu.SMEM)] + (carry.out_specs if nc else []),
        out_shape=[jax.ShapeDtypeStruct((t, 512), BF16), jax.ShapeDtypeStruct((4, t, LANE), F32),
                   jax.ShapeDtypeStruct((ngrp * nq,), F32)] + (carry.out_shape if nc else []),
        scratch_shapes=carry.scratch if nc else [],
        compiler_params=_params(("arbitrary", "arbitrary")),
    )(proj_b, proj_b, proj_b, *(carry.groups if nc else []))
    return outs[0], outs[1], outs[2], (carry.place(outs[3:]) if nc else [])


def _sb_bwd(proj_b, dy, ltot, nblk):
    t = proj_b.shape[0]
    nq = t // SBQ
    scale = 1.0 / math.sqrt(64.0)

    npair = SB_PAIRS
    wide = npair * LANE
    ngrp = 4 // npair
    chains = [(p, head) for p in range(npair) for head in range(2)]

    def body(q_ref, k_ref, v_ref, dy_ref, l_ref, n_ref, dq_ref, dk_ref, dv_ref, dk_acc, dv_acc):
        grp = pl.program_id(0)
        qi = pl.program_id(1)

        @pl.when(qi == 0)
        def _():
            dk_acc[...] = jnp.zeros_like(dk_acc)
            dv_acc[...] = jnp.zeros_like(dv_acc)

        row, col = _sb_masks()
        strict = col < row
        m_prefix = _ones_where(row <= col)
        m_before = _ones_where(row < col)
        qh, doh, ltot_h = [], [], []
        for p, head in chains:
            pl_ = slice(p * LANE, (p + 1) * LANE)
            q_all = q_ref[:, pl_]
            do_all = dy_ref[:, pl_].astype(BF16)
            hm = _head_mask(head)
            qh.append(jnp.where(hm, q_all, jnp.zeros_like(q_all)))
            doh.append(jnp.where(hm, do_all, jnp.zeros_like(do_all)))
            ltot_h.append(l_ref[p][:, head * 64:head * 64 + 1])

        def block(kb, carry, diag):
            start = pl.multiple_of(kb * SBQ, SBQ)
            new = []
            for p in range(npair):
                pl_ = slice(p * LANE, (p + 1) * LANE)
                kk = k_ref[pl.ds(start, SBQ), pl_]
                vv = v_ref[pl.ds(start, SBQ), pl_]
                dk_blk, dv_blk = None, None
                for head in range(2):
                    n = 2 * p + head
                    seen, dseen, dq = carry[n]
                    z = _dg(qh[n], kk, NT) * scale
                    lb = _log_sigmoid(z)
                    lk = lb - z
                    if diag:
                        lk = jnp.where(strict, lk, 0.0)
                    hi, lo = _split(lk)
                    later = (ltot_h[n] - seen) - (_dg(hi, m_prefix, NN) + _dg(lo, m_prefix, NN))
                    w = jnp.exp(lb + later)
                    if diag:
                        w = jnp.where(strict, w, 0.0)
                    da = w * _dg(doh[n], vv, NT)
                    dv_h = _dg(w.astype(BF16), doh[n], TN)
                    dah, dal = _split(da)
                    dlk = dseen + _dg(dah, m_before, NN) + _dg(dal, m_before, NN)
                    sig = jnp.exp(lb)
                    dz = (da * (1.0 - sig) - dlk * sig) * scale
                    if diag:
                        dz = jnp.where(strict, dz, 0.0)
                    dzb = dz.astype(BF16)
                    dk_h = _dg(dzb, qh[n], TN)
                    dk_blk = dk_h if dk_blk is None else dk_blk + dk_h
                    dv_blk = dv_h if dv_blk is None else dv_blk + dv_h
                    new.append((seen + jnp.sum(hi.astype(F32) + lo.astype(F32), axis=1, keepdims=True),
                                dseen + jnp.sum(da, axis=1, keepdims=True),
                                dq + _dg(dzb, kk, NN)))
                dk_acc[pl.ds(start, SBQ), pl_] += dk_blk
                dv_acc[pl.ds(start, SBQ), pl_] += dv_blk
            return tuple(new)

        zero = (jnp.zeros((SBQ, 1), F32), jnp.zeros((SBQ, 1), F32), jnp.zeros((SBQ, LANE), F32))
        first = qi - n_ref[grp * nq + qi].astype(jnp.int32)
        carry = lax.fori_loop(first, qi, lambda kb, c: block(kb, c, False), tuple(zero for _ in chains))
        carry = block(qi, carry, True)
        for p in range(npair):
            dq_ref[:, p * LANE:(p + 1) * LANE] = jnp.where(_head_mask(0), carry[2 * p][2], carry[2 * p + 1][2]).astype(BF16)

        @pl.when(qi == nq - 1)
        def _():
            dk_ref[...] = dk_acc[...].astype(BF16)
            dv_ref[...] = dv_acc[...].astype(BF16)

    full = jax.ShapeDtypeStruct((t, 512), BF16)
    return pl.pallas_call(
        body, name="sb_bwd", grid=(ngrp, nq),
        in_specs=[_spec((SBQ, wide), lambda g, i: (i, g)),
                  _spec((t, wide), lambda g, i: (0, ngrp + g)),
                  _spec((t, wide), lambda g, i: (0, 2 * ngrp + g)),
                  _spec((SBQ, wide), lambda g, i: (i, ngrp + g)),
                  _spec((npair, SBQ, LANE), lambda g, i: (g, i, 0)),
                  pl.BlockSpec(memory_space=pltpu.SMEM)],
        out_specs=[_spec((SBQ, wide), lambda g, i: (i, g)),
                   _spec((t, wide), lambda g, i: (0, g)), _spec((t, wide), lambda g, i: (0, g))],
        out_shape=[full, full, full],
        scratch_shapes=[pltpu.VMEM((t, wide), F32), pltpu.VMEM((t, wide), F32)],
        compiler_params=_params(("parallel", "arbitrary")),
    )(proj_b, proj_b, proj_b, dy, ltot, nblk)


def _hgrn_gates(qr, fr, c0, c1):
    mx = jnp.maximum(c0, c1)
    e0, e1 = jnp.exp(c0 - mx), jnp.exp(c1 - mx)
    lb = e1 / (e0 + e1)
    sx = _sigmoid(fr)
    f = lb + (1.0 - lb) * sx
    k = (1.0 - lb) * (1.0 - sx)
    sq = _sigmoid(qr)
    return lb, sx, f, k, sq, qr * sq


def _hgrn_decays(f):
    row = lax.broadcasted_iota(jnp.int32, (CHUNK, CHUNK), 0)
    col = lax.broadcasted_iota(jnp.int32, (CHUNK, CHUNK), 1)
    tril = col <= row
    hi, lo = _split(jnp.log(f))
    trib = _ones_where(tril)
    b = _dg(trib, hi, NN) + _dg(trib, lo, NN)
    bm = b[CHUNK // 2 - 1:CHUNK // 2]
    bl = b[CHUNK - 1:CHUNK]
    return tril, col >= row, b, bm, bl


def _hgrn_fwd(proj_c, small, carry=None):
    t = proj_c.shape[1]
    nc = t // CHUNK
    nh = D // HD

    hps = HGRN_HPS
    wide = hps * HD
    grid = (nh // hps, nc)
    nr = carry.n if carry is not None else 0

    def body(*refs):
        p_ref, c0_ref, c1_ref, gam_ref = refs[:4]
        o_ref, y_ref, sst_ref = refs[4 + nr:7 + nr]
        st_ref = refs[7 + 2 * nr]
        riders = (refs[4:4 + nr], refs[7 + nr:7 + 2 * nr], refs[-2], refs[-1]) if nr else None
        if nr:
            carry.ride(grid, riders, "start")
        c = pl.program_id(1)

        @pl.when(c == 0)
        def _():
            st_ref[...] = jnp.zeros_like(st_ref)

        for j in range(hps):
            ln = slice(j * HD, (j + 1) * HD)
            st0 = st_ref[j]
            sst_ref[j] = st0
            qr, fr, v, g = p_ref[0, :, ln], p_ref[1, :, ln], p_ref[2, :, ln], p_ref[3, :, ln]
            lb, sx, f, k, sq, q = _hgrn_gates(qr, fr, c0_ref[:, ln], c1_ref[:, ln])
            tril, _, b, bm, bl = _hgrn_decays(f)
            qd = q * jnp.exp(b)
            qt = q * jnp.exp(b - bm)
            kt = k * jnp.exp(bm - b)
            kl = k * jnp.exp(bl - b)
            vb = v.astype(BF16)
            att = jnp.where(tril, _dot3(qt, kt, NT), 0.0)
            o = _dg(qd.astype(BF16), st0.astype(BF16), NT) + _dg(att.astype(BF16), vb, NN)
            st_ref[j] = st0 * jnp.exp(bl) + _dg(vb, kl.astype(BF16), TN)
            o_ref[:, ln] = o
            r = lax.rsqrt(jnp.mean(o * o, axis=-1, keepdims=True) + RMS_EPS)
            y_ref[:, ln] = (o * r * gam_ref[...] * (g * _sigmoid(g))).astype(BF16)
        if nr:
            carry.ride(grid, riders, "finish")

    outs = pl.pallas_call(
        body, name="hgrn_fwd", grid=grid,
        in_specs=[_spec((4, CHUNK, wide), lambda h, c: (0, c, h)),
                  _spec((None, 1, wide), lambda h, c: (R_CLB, 0, h)),
                  _spec((None, 1, wide), lambda h, c: (R_CLB + 1, 0, h)),
                  _spec((None, 1, HD), lambda h, c: (R_GAM, 0, 0))] + (carry.in_specs if nr else []),
        out_specs=[_spec((CHUNK, wide), lambda h, c: (c, h)), _spec((CHUNK, wide), lambda h, c: (c, h)),
                   _spec((None, hps, HD, HD), lambda h, c: (c, h, 0, 0))] + (carry.out_specs if nr else []),
        out_shape=[jax.ShapeDtypeStruct((t, D), F32), jax.ShapeDtypeStruct((t, D), BF16),
                   jax.ShapeDtypeStruct((nc, nh, HD, HD), F32)] + (carry.out_shape if nr else []),
        scratch_shapes=[pltpu.VMEM((hps, HD, HD), F32)] + (carry.scratch if nr else []),
        compiler_params=_params(("arbitrary", "arbitrary")),
    )(proj_c, small, small, small, *(carry.groups if nr else []))
    return outs[0], outs[1], outs[2], (carry.place(outs[3:]) if nr else [])


def _hgrn_bwd(proj_c, small, o, sst, dyc):
    t = proj_c.shape[1]
    nc = t // CHUNK
    nh = D // HD

    hps = HGRN_HPS
    wide = hps * HD
    ng = nh // hps

    def body(p_ref, c0_ref, c1_ref, gam_ref, o_ref, sst_ref, dy_ref, dp_ref, dclb_ref, dgam_ref, dst_ref, dlb_acc, dgam_acc):
        group = pl.program_id(0)
        step = pl.program_id(1)

        @pl.when(step == 0)
        def _():
            dst_ref[...] = jnp.zeros_like(dst_ref)
            dlb_acc[...] = jnp.zeros_like(dlb_acc)

        @pl.when((step == 0) & (group == 0))
        def _():
            dgam_acc[...] = jnp.zeros_like(dgam_acc)

        gam = gam_ref[...]
        for j in range(hps):
            ln = slice(j * HD, (j + 1) * HD)
            st0 = sst_ref[j]
            dst1 = dst_ref[j]
            qr, fr, v, g = p_ref[0, :, ln], p_ref[1, :, ln], p_ref[2, :, ln], p_ref[3, :, ln]
            lb, sx, f, k, sq, q = _hgrn_gates(qr, fr, c0_ref[:, ln], c1_ref[:, ln])
            tril, triu, b, bm, bl = _hgrn_decays(f)
            eb = jnp.exp(b)
            e_qt = jnp.exp(b - bm)
            e_kt = jnp.exp(bm - b)
            e_kl = jnp.exp(bl - b)
            e_bl = jnp.exp(bl)
            qd, qt, kt, kl = q * eb, q * e_qt, k * e_kt, k * e_kl
            ov = o_ref[:, ln]
            r = lax.rsqrt(jnp.mean(ov * ov, axis=-1, keepdims=True) + RMS_EPS)
            oh = ov * r
            sg = _sigmoid(g)
            dy = dy_ref[:, ln]
            dp_ref[3, :, ln] = (dy * oh * gam * (sg * (1.0 + g * (1.0 - sg)))).astype(BF16)
            dyv = dy * (g * sg)
            dgam_acc[...] += jnp.sum(dyv * oh, axis=0, keepdims=True)
            gdy = dyv * gam
            do = (gdy - oh * jnp.mean(gdy * oh, axis=-1, keepdims=True)) * r
            dob, vb = do.astype(BF16), v.astype(BF16)
            st0b, dst1b = st0.astype(BF16), dst1.astype(BF16)
            st1 = st0 * e_bl + _dg(vb, kl.astype(BF16), TN)
            att = jnp.where(tril, _dot3(qt, kt, NT), 0.0)
            datt = jnp.where(tril, _dg(dob, vb, NT), 0.0)
            dv = _dg(att.astype(BF16), dob, TN) + _dg(kl.astype(BF16), dst1b, NT)
            dq = _dot3(datt, kt, NN) * e_qt + _dg(dob, st0b, NN) * eb
            dk = _dot3(datt, qt, TN) * e_kt + _dg(vb, dst1b, NN) * e_kl
            db = q * dq - k * dk
            last = lax.broadcasted_iota(jnp.int32, (CHUNK, 1), 0) == CHUNK - 1
            db = db + jnp.where(last, jnp.sum(dst1 * st1, axis=0, keepdims=True), 0.0)
            dbh, dbl = _split(db)
            triub = _ones_where(triu)
            dlf = _dg(triub, dbh, NN) + _dg(triub, dbl, NN)
            dst_ref[j] = dst1 * e_bl + _dg(dob, qd.astype(BF16), TN)
            dp_ref[0, :, ln] = (dq * (sq * (1.0 + qr * (1.0 - sq)))).astype(BF16)
            tmp = dlf / f - dk
            dp_ref[1, :, ln] = (tmp * (1.0 - lb) * sx * (1.0 - sx)).astype(BF16)
            dp_ref[2, :, ln] = dv.astype(BF16)
            dlb_acc[:, ln] += jnp.sum((1.0 - sx) * tmp, axis=0, keepdims=True)

            @pl.when(step == nc - 1)
            def _():
                d1 = dlb_acc[:, ln] * lb * (1.0 - lb)
                dclb_ref[:, ln] = jnp.where(lax.broadcasted_iota(jnp.int32, (2, HD), 0) == 0, -d1, d1)

        @pl.when((step == nc - 1) & (group == ng - 1))
        def _():
            dgam_ref[...] = dgam_acc[...]

    rev = lambda h, s: (nc - 1 - s, h)
    return pl.pallas_call(
        body, name="hgrn_bwd", grid=(ng, nc),
        in_specs=[_spec((4, CHUNK, wide), lambda h, s: (0, nc - 1 - s, h)),
                  _spec((None, 1, wide), lambda h, s: (R_CLB, 0, h)),
                  _spec((None, 1, wide), lambda h, s: (R_CLB + 1, 0, h)),
                  _spec((None, 1, HD), lambda h, s: (R_GAM, 0, 0)),
                  _spec((CHUNK, wide), rev),
                  _spec((None, hps, HD, HD), lambda h, s: (nc - 1 - s, h, 0, 0)),
                  _spec((CHUNK, wide), rev)],
        out_specs=[_spec((4, CHUNK, wide), lambda h, s: (0, nc - 1 - s, h)),
                   _spec((2, wide), lambda h, s: (0, h)),
                   _spec((1, HD), lambda h, s: (0, 0))],
        out_shape=[jax.ShapeDtypeStruct((4, t, D), BF16), jax.ShapeDtypeStruct((2, D), F32),
                   jax.ShapeDtypeStruct((1, HD), F32)],
        scratch_shapes=[pltpu.VMEM((hps, HD, HD), F32), pltpu.VMEM((1, wide), F32), pltpu.VMEM((1, HD), F32)],
        compiler_params=_params(("arbitrary", "arbitrary")),
    )(proj_c, small, small, small, o, sst, dyc)


def _adamw(name, w, g, m, v):
    rows, cols = w.shape
    br = rows
    for cand in (512, 352, 256):
        if rows % cand == 0:
            br = cand
            break
    c1 = 1.0 - ADAM_B1 ** ADAM_STEP
    c2 = 1.0 - ADAM_B2 ** ADAM_STEP

    def body(w_ref, g_ref, m_ref, v_ref, d_ref, mo_ref, vo_ref):
        gv = g_ref[...]
        mn = ADAM_B1 * m_ref[...] + (1.0 - ADAM_B1) * gv
        vn = ADAM_B2 * v_ref[...] + (1.0 - ADAM_B2) * (gv * gv)
        mo_ref[...] = mn
        vo_ref[...] = vn
        d_ref[...] = -ADAM_LR * ((mn / c1) / (jnp.sqrt(vn / c2) + ADAM_EPS) + ADAM_WD * w_ref[...])

    blk = _spec((br, cols), lambda i: (i, 0))
    shape = jax.ShapeDtypeStruct((rows, cols), F32)
    return pl.pallas_call(
        body, name=name, grid=(rows // br,), in_specs=[blk] * 4, out_specs=[blk] * 3, out_shape=[shape] * 3,
        compiler_params=_params(("parallel",)),
    )(w, g, m, v)


def _place():
    x, y, c = lax.axis_index("x"), lax.axis_index("y"), lax.axis_index("c")
    chips = [(1 - x, y), (x, 1 - y), (1 - x, 1 - y)]
    return x, y, c, chips


class _Gather:
    def __init__(self, groups):
        self.groups = list(groups)
        self.n = len(self.groups)
        self.in_specs = [ANY] * self.n
        self.out_specs = [ANY] * self.n
        self.out_shape = [jax.ShapeDtypeStruct((NSH,) + g.shape, g.dtype) for g in self.groups]
        self.scratch = [pltpu.SemaphoreType.DMA((6 * self.n,)), pltpu.SemaphoreType.DMA((6 * self.n,))] if self.n else []

    def _copies(self, ins, outs, send, recv):
        x, y, c, chips = _place()
        sibling = (x, y, 1 - c)

        def half(gi, chip, hc):
            return outs[gi].at[2 * chip[0] + chip[1], hc]

        def copy(gi, k, src, dst, to):
            return pltpu.make_async_remote_copy(src_ref=src, dst_ref=dst, send_sem=send.at[6 * gi + k],
                                                recv_sem=recv.at[6 * gi + k], device_id=to, device_id_type=MESH)

        pairs = [(gi, j, chip) for gi in range(self.n) for j, chip in enumerate(chips)]
        first = [copy(gi, j, ins[gi].at[c], half(gi, (x, y), c), (*chip, c)) for gi, j, chip in pairs]
        landed = [copy(gi, j, half(gi, chip, c), half(gi, chip, c), sibling) for gi, j, chip in pairs]
        relay = [copy(gi, 3 + j, half(gi, chip, c), half(gi, chip, c), sibling) for gi, j, chip in pairs]
        relayed = [copy(gi, 3 + j, half(gi, chip, 1 - c), half(gi, chip, 1 - c), sibling) for gi, j, chip in pairs]
        return first, landed, relay, relayed

    def start(self, ins, outs, send, recv):
        for cp in self._copies(ins, outs, send, recv)[0]:
            cp.start()

    def finish(self, ins, outs, send, recv):
        first, landed, relay, relayed = self._copies(ins, outs, send, recv)
        for arrived, onward in zip(landed, relay):
            arrived.wait_recv()
            onward.start()
        for cp in relayed:
            cp.wait_recv()
        for cp in first + relay:
            cp.wait_send()

    def ride(self, grid, refs, when):
        if not self.n:
            return
        ids = [pl.program_id(a) for a in range(len(grid))]
        edge = [i == (0 if when == "start" else g - 1) for i, g in zip(ids, grid)]
        cond = functools.reduce(jnp.logical_and, edge)

        @pl.when(cond)
        def _():
            (self.start if when == "start" else self.finish)(*refs)

    def place(self, outs):
        me = 2 * lax.axis_index("x") + lax.axis_index("y")
        return [lax.dynamic_update_index_in_dim(o, g, me, 0) for o, g in zip(outs, self.groups)]


def _gather_weights(groups):
    gather = _Gather(groups)
    ng = gather.n

    def body(*refs):
        parts = (refs[:ng], refs[ng:2 * ng], refs[2 * ng], refs[2 * ng + 1])
        gather.start(*parts)
        gather.finish(*parts)

    outs = pl.pallas_call(
        body, name="gather_weights", in_specs=gather.in_specs, out_specs=gather.out_specs, out_shape=gather.out_shape,
        scratch_shapes=gather.scratch, compiler_params=pltpu.CompilerParams(has_side_effects=True),
    )(*groups)
    return gather.place(outs)


def _swap_halves(groups):
    ng = len(groups)

    def body(*refs):
        ins, outs = refs[:ng], refs[ng:2 * ng]
        send, recv = refs[2 * ng:]
        x, y, c, _ = _place()
        cps = []
        for gi in range(ng):
            half = ins[gi].shape[1] // 2
            rows = pl.ds(pl.multiple_of((1 - c) * half, half), half)
            cps.append(pltpu.make_async_remote_copy(src_ref=ins[gi].at[:, rows], dst_ref=outs[gi], send_sem=send.at[gi],
                                                    recv_sem=recv.at[gi], device_id=(x, y, 1 - c), device_id_type=MESH))
        for cp in cps:
            cp.start()
        for cp in cps:
            cp.wait()

    return pl.pallas_call(
        body, name="grad_swap_halves", in_specs=[ANY] * ng, out_specs=[ANY] * ng,
        out_shape=[jax.ShapeDtypeStruct((g.shape[0], g.shape[1] // 2, g.shape[2]), g.dtype) for g in groups],
        scratch_shapes=[pltpu.SemaphoreType.DMA((ng,)), pltpu.SemaphoreType.DMA((ng,))],
        compiler_params=pltpu.CompilerParams(has_side_effects=True),
    )(*groups)


def _send_to_owners(groups):
    ng = len(groups)

    def body(*refs):
        ins, outs = refs[:ng], refs[ng:2 * ng]
        send, recv = refs[2 * ng:]
        x, y, c, chips = _place()
        cps = [pltpu.make_async_remote_copy(src_ref=ins[gi].at[2 * chip[0] + chip[1]], dst_ref=outs[gi].at[j],
                                            send_sem=send.at[3 * gi + j], recv_sem=recv.at[3 * gi + j],
                                            device_id=(*chip, c), device_id_type=MESH)
               for gi in range(ng) for j, chip in enumerate(chips)]
        for cp in cps:
            cp.start()
        for cp in cps:
            cp.wait()

    return pl.pallas_call(
        body, name="grad_send_to_owners", in_specs=[ANY] * ng, out_specs=[ANY] * ng,
        out_shape=[jax.ShapeDtypeStruct((3,) + g.shape[1:], g.dtype) for g in groups],
        scratch_shapes=[pltpu.SemaphoreType.DMA((3 * ng,)), pltpu.SemaphoreType.DMA((3 * ng,))],
        compiler_params=pltpu.CompilerParams(has_side_effects=True),
    )(*groups)


def _share_halves(groups):
    ng = len(groups)

    def body(*refs):
        ins, outs = refs[:ng], refs[ng:2 * ng]
        send, recv = refs[2 * ng:]
        x, y, c, _ = _place()
        cps = [pltpu.make_async_remote_copy(src_ref=ins[gi], dst_ref=outs[gi].at[c], send_sem=send.at[gi],
                                            recv_sem=recv.at[gi], device_id=(x, y, 1 - c), device_id_type=MESH)
               for gi in range(ng)]
        for cp in cps:
            cp.start()
        for gi in range(ng):
            pltpu.make_async_remote_copy(src_ref=ins[gi], dst_ref=outs[gi].at[1 - c], send_sem=send.at[gi],
                                         recv_sem=recv.at[gi], device_id=(x, y, 1 - c), device_id_type=MESH).wait_recv()
        for cp in cps:
            cp.wait_send()

    outs = pl.pallas_call(
        body, name="grad_share_halves", in_specs=[ANY] * ng, out_specs=[ANY] * ng,
        out_shape=[jax.ShapeDtypeStruct((2,) + g.shape, g.dtype) for g in groups],
        scratch_shapes=[pltpu.SemaphoreType.DMA((ng,)), pltpu.SemaphoreType.DMA((ng,))],
        compiler_params=pltpu.CompilerParams(has_side_effects=True),
    )(*groups)
    c = lax.axis_index("c")
    return [lax.dynamic_update_index_in_dim(o, g, c, 0) for o, g in zip(outs, groups)]


def _pair_sum(name, g, got, c_idx):
    nsh, rows, cols = got.shape
    br = rows // 4

    def body(c_ref, a_ref, b_ref, o_ref):
        o_ref[...] = (a_ref[...].astype(F32) + b_ref[...].astype(F32)).astype(BF16)

    return pl.pallas_call(
        body, name=name,
        grid_spec=pltpu.PrefetchScalarGridSpec(
            num_scalar_prefetch=1, grid=(nsh, 4),
            in_specs=[pl.BlockSpec((None, br, cols), lambda q, i, cr: (q, 4 * cr[0] + i, 0)),
                      pl.BlockSpec((None, br, cols), lambda q, i, cr: (q, i, 0))],
            out_specs=pl.BlockSpec((None, br, cols), lambda q, i, cr: (q, i, 0))),
        out_shape=jax.ShapeDtypeStruct((nsh, rows, cols), BF16),
        compiler_params=_params(("parallel", "parallel")),
    )(c_idx, g, got)


def _owner_sum(name, own, got, p_idx):
    _, rows, cols = own.shape
    br = rows // 4

    def body(p_ref, a_ref, b_ref, o_ref):
        o_ref[...] = ((a_ref[...].astype(F32) + b_ref[0].astype(F32)) + b_ref[1].astype(F32)) + b_ref[2].astype(F32)

    return pl.pallas_call(
        body, name=name,
        grid_spec=pltpu.PrefetchScalarGridSpec(
            num_scalar_prefetch=1, grid=(4,),
            in_specs=[pl.BlockSpec((None, br, cols), lambda i, pr: (pr[0], i, 0)),
                      pl.BlockSpec((3, br, cols), lambda i, pr: (0, i, 0))],
            out_specs=pl.BlockSpec((br, cols), lambda i, pr: (i, 0))),
        out_shape=jax.ShapeDtypeStruct((rows, cols), F32),
        compiler_params=_params(("parallel",)),
    )(p_idx, own, got)


def _sum_small(slab):
    def body(in_ref, out_ref, all_ref, send, recv):
        x, y, c, _ = _place()
        me = 4 * x + 2 * y + c
        all_ref[me] = in_ref[...]
        cps = []
        for k in range(1, 8):
            peer = (x ^ (k >> 2), y ^ ((k >> 1) & 1), c ^ (k & 1))
            cps.append(pltpu.make_async_remote_copy(src_ref=in_ref, dst_ref=all_ref.at[me], send_sem=send.at[k - 1],
                                                    recv_sem=recv.at[k - 1], device_id=peer, device_id_type=MESH))
        for cp in cps:
            cp.start()
        for cp in cps:
            cp.wait()
        total = all_ref[0]
        for d in range(1, 8):
            total = total + all_ref[d]
        out_ref[...] = total

    return pl.pallas_call(
        body, name="sum_small",
        in_specs=[pl.BlockSpec(memory_space=pltpu.VMEM)], out_specs=pl.BlockSpec(memory_space=pltpu.VMEM),
        out_shape=jax.ShapeDtypeStruct(slab.shape, F32),
        scratch_shapes=[pltpu.VMEM((8,) + slab.shape, F32), pltpu.SemaphoreType.DMA((7,)), pltpu.SemaphoreType.DMA((7,))],
        compiler_params=pltpu.CompilerParams(has_side_effects=True),
    )(slab)


FFNS = ("pre0", "post0", "pre1", "post1")
W_SHAPES = dict({f + "_gu": (NSH, 2, D, FS) for f in FFNS}, **{f + "_d": (NSH, FS, D) for f in FFNS},
                ab_in=(NSH, D, 768), ab_out=(NSH, 256, D), conv=(NSH, 2, 8, LANE), c_in=(NSH, D, D), c_out=(NSH, 256, D))
CARRIED = dict(pre0_up=("pre0_d",), pre0_down=("ab_in", "ab_out", "conv"),
               sb_fwd=("post0_gu", "post0_d", "pre1_gu", "pre1_d"), post0_up=("c_in", "c_out"),
               hgrn_fwd=("post1_gu", "post1_d"))


def _local_step(x, target, weights, small, shards=None):
    t = x.shape[0]
    tm = min(MM_TILE, t)
    tw = min(WGRAD_TILE, t)
    nt = t // tm
    tok_si = _spec((tm, D), lambda s, i: (i, 0))
    w = dict(weights)

    def carried(kernel_name):
        names = [n for n in CARRIED[kernel_name] if n not in w]
        return names, (_Gather([shards[n] for n in names]) if names else None)

    def land(names, arrays):
        for n, a in zip(names, arrays):
            w[n] = a.reshape(W_SHAPES[n])

    def ffn_forward(tag, h, norm_row):
        hn = _norm_fwd(tag + "_norm", h, small, norm_row)
        names, gather = carried(tag + "_up") if tag + "_up" in CARRIED else ([], None)
        s_up, s_gate, a, got = _ffn_up(tag + "_up", hn, w[tag + "_gu"], gather)
        land(names, got)
        names, gather = carried(tag + "_down") if tag + "_down" in CARRIED else ([], None)
        out = _ffn_down(tag + "_down", a, w[tag + "_d"], h, gather)
        if gather is not None:
            out, got = out
            land(names, got)
        return out, (h, hn, s_up, s_gate, a)

    def out_proj(name, y, w_out, h):
        return _mm(name, [(y, _spec((tm, 256), lambda i, k: (i, k)), w_out, _spec((None, 256, D), lambda i, k: (k, 0, 0)))],
                   grid=(nt, NSH), o_shape=(t, D), o_dtype=F32, o_spec=_spec((tm, D), lambda i, k: (i, 0)),
                   dims=NN, kaxis=1, nk=NSH, acc_shape=(tm, D), res=(h, _spec((tm, D), lambda i, k: (i, 0))))

    def out_proj_bwd(tag, dhb, y, w_out, blk, grad_b):
        dy = _mm(tag + "_dy", [(dhb, tok_si, w_out, _spec((None, 256, D), lambda s, i: (s, 0, 0)))],
                 grid=(NSH, nt), o_shape=(t, D), o_dtype=F32, o_spec=_spec((tm, 256), lambda s, i: (i, s)),
                 dims=NT, kaxis=1, nk=1)
        grad_b = _wgrad(tag + "_dwout", y, _spec((tw, 256), lambda s, k: (k, s)), dhb, _spec((tw, D), lambda s, k: (k, 0)),
                        256, D, t, tw, grad_b, blk)
        return dy, grad_b

    h0 = x
    h1, pre0 = ffn_forward("pre0", h0, R_PRE)
    hn_ab = _norm_fwd("mix0_norm", h1, small, R_MIX)
    proj_a = _mm("ab_proj_a", [(hn_ab, tok_si, w["ab_in"], _spec((None, D, 768), lambda s, i: (s, 0, 0)))],
                 grid=(2, nt), o_shape=(t, 1536), o_dtype=F32, o_spec=_spec((tm, 768), lambda s, i: (i, s)),
                 dims=NN, kaxis=1, nk=1)
    proj_b = _mm("ab_proj_b", [(hn_ab, tok_si, w["ab_in"], _spec((None, D, 768), lambda s, i: (s + 2, 0, 0)))],
                 grid=(2, nt), o_shape=(t, 1536), o_dtype=BF16, o_spec=_spec((tm, 768), lambda s, i: (i, s)),
                 dims=NN, kaxis=1, nk=1)
    y_a = _conv_fwd(proj_a, w["conv"])
    names, gather = carried("sb_fwd")
    y_b, ltot, nblk, got = _sb_fwd(proj_b, gather)
    land(names, got)
    y_ab = jnp.concatenate([y_a, y_b], axis=1)
    h2 = out_proj("ab_out", y_ab, w["ab_out"], h1)
    h3, post0 = ffn_forward("post0", h2, R_POST)
    h4, pre1 = ffn_forward("pre1", h3, R_PRE + 1)
    hn_c = _norm_fwd("mix1_norm", h4, small, R_MIX + 1)
    proj_c = _mm("c_proj", [(hn_c, tok_si, w["c_in"], _spec((None, D, D), lambda s, i: (s, 0, 0)))],
                 grid=(NSH, nt), o_shape=(NSH, t, D), o_dtype=F32, o_spec=_spec((None, tm, D), lambda s, i: (s, i, 0)),
                 dims=NN, kaxis=1, nk=1)
    names, gather = carried("hgrn_fwd")
    o_c, y_c, sst, got = _hgrn_fwd(proj_c, small, gather)
    land(names, got)
    h5 = out_proj("c_out", y_c, w["c_out"], h4)
    h6, post1 = ffn_forward("post1", h5, R_POST + 1)
    dh, dhb, d_fin, loss = _final_loss(h6, small, target)

    dh, dhb, grad_a, grad_b, dn_post1 = _ffn_backward("post1", dh, dhb, *post1, w["post1_gu"], w["post1_d"], 5, 7, 3,
                                                      small, R_POST + 1, 8 * D, B_ROWS)
    dy_c, grad_b = out_proj_bwd("c", dhb, y_c, w["c_out"], B_COUT // 256, grad_b)
    dproj_c, d_clb, d_gam = _hgrn_bwd(proj_c, small, o_c, sst, dy_c)
    grad_b = _wgrad("c_dwin", hn_c, _spec((tw, D), lambda s, k: (k, 0)), dproj_c, _spec((None, tw, D), lambda s, k: (s, k, 0)),
                    D, D, t, tw, grad_b, B_CIN // D)
    dhn = _mm("c_dhn", [(dproj_c, _spec((None, tm, D), lambda i, k: (k, i, 0)),
                         w["c_in"], _spec((None, D, D), lambda i, k: (k, 0, 0)))],
              grid=(nt, NSH), o_shape=(t, D), o_dtype=F32, o_spec=_spec((tm, D), lambda i, k: (i, 0)),
              dims=NT, kaxis=1, nk=NSH, acc_shape=(tm, D))
    dh, dhb, dn_mix1 = _norm_bwd("mix1_norm_bwd", dhn, h4, small, R_MIX + 1, dh)
    dh, dhb, grad_a, grad_b, dn_pre1 = _ffn_backward("pre1", dh, dhb, *pre1, w["pre1_gu"], w["pre1_d"], 1, 3, 1,
                                                     small, R_PRE + 1, grad_a, grad_b)
    dh, dhb, grad_a, grad_b, dn_post0 = _ffn_backward("post0", dh, dhb, *post0, w["post0_gu"], w["post0_d"], 4, 6, 2,
                                                      small, R_POST, grad_a, grad_b)
    dy_ab, grad_b = out_proj_bwd("ab", dhb, y_ab, w["ab_out"], B_ABOUT // 256, grad_b)
    dab, dac, dax, d_conv = _conv_bwd(proj_a, w["conv"], dy_ab)
    dq, dk, dv = _sb_bwd(proj_b, dy_ab, ltot, nblk)
    dproj_ab = jnp.concatenate([dab, dac, dax, dq, dk, dv], axis=1)
    grad_c = _wgrad("ab_dwin", hn_ab, _spec((tw, D), lambda s, k: (k, 0)), dproj_ab, _spec((tw, 768), lambda s, k: (k, s)),
                    D, 768, t, tw, D, 0)
    dhn = _mm("ab_dhn", [(dproj_ab, _spec((tm, 768), lambda i, k: (i, k)),
                          w["ab_in"], _spec((None, D, 768), lambda i, k: (k, 0, 0)))],
              grid=(nt, NSH), o_shape=(t, D), o_dtype=F32, o_spec=_spec((tm, D), lambda i, k: (i, 0)),
              dims=NT, kaxis=1, nk=NSH, acc_shape=(tm, D))
    dh, dhb, dn_mix0 = _norm_bwd("mix0_norm_bwd", dhn, h1, small, R_MIX, dh)
    dh, dhb, grad_a, grad_b, dn_pre0 = _ffn_backward("pre0", dh, dhb, *pre0, w["pre0_gu"], w["pre0_d"], 0, 2, 0,
                                                     small, R_PRE, grad_a, grad_b)
    zero = jnp.zeros((1, D), F32)
    conv_rows = jnp.pad(jnp.transpose(d_conv[:, :3, :], (1, 0, 2)).reshape(3, 512), ((0, 0), (0, D - 512)))
    small_grad = jnp.concatenate([
        dn_pre0, dn_pre1, dn_mix0, dn_mix1, dn_post0, dn_post1, d_clb, d_fin,
        jnp.pad(d_gam, ((0, 0), (0, D - HD))), conv_rows,
        jnp.pad(loss, ((0, 0), (0, D - 1))), zero, zero], axis=0)
    return dh, grad_a, grad_b, grad_c, small_grad


def _small_slab(rows):
    parts = [jnp.pad(r.astype(F32), ((0, 0), (0, D - r.shape[1]))) for r in rows]
    slab = jnp.concatenate(parts, axis=0)
    return jnp.pad(slab, ((0, SMALL_ROWS - slab.shape[0]), (0, 0)))


def kernel(x, ffn_pre_norm, ffn_pre_w_gate, ffn_pre_w_up, ffn_pre_w_down, mix_norm, ffn_post_norm, ffn_post_w_gate, ffn_post_w_up, ffn_post_w_down, ab_w_in, ab_conv_w, ab_w_out, c_w_in, c_lower_bounds, c_out_norm, c_w_out, final_norm, loss_target, m_ffn_pre_norm, m_ffn_pre_w_gate, m_ffn_pre_w_up, m_ffn_pre_w_down, m_mix_norm, m_ffn_post_norm, m_ffn_post_w_gate, m_ffn_post_w_up, m_ffn_post_w_down, m_ab_w_in, m_ab_conv_w, m_ab_w_out, m_c_w_in, m_c_lower_bounds, m_c_out_norm, m_c_w_out, m_final_norm, v_ffn_pre_norm, v_ffn_pre_w_gate, v_ffn_pre_w_up, v_ffn_pre_w_down, v_mix_norm, v_ffn_post_norm, v_ffn_post_w_gate, v_ffn_post_w_up, v_ffn_post_w_down, v_ab_w_in, v_ab_conv_w, v_ab_w_out, v_c_w_in, v_c_lower_bounds, v_c_out_norm, v_c_w_out, v_final_norm):
    t = x.shape[1]
    xi, yi, ci = lax.axis_index("x"), lax.axis_index("y"), lax.axis_index("c")
    p_idx = (2 * xi + yi).astype(jnp.int32).reshape(1)
    c_idx = ci.astype(jnp.int32).reshape(1)

    def halves(m):
        return m.astype(BF16).reshape(2, m.shape[0] // 2, m.shape[1])

    shards = {}
    for name, (w_gate, w_up, w_down, layer) in dict(
            pre0=(ffn_pre_w_gate, ffn_pre_w_up, ffn_pre_w_down, 0), post0=(ffn_post_w_gate, ffn_post_w_up, ffn_post_w_down, 0),
            pre1=(ffn_pre_w_gate, ffn_pre_w_up, ffn_pre_w_down, 1), post1=(ffn_post_w_gate, ffn_post_w_up, ffn_post_w_down, 1)).items():
        shards[name + "_gu"] = jnp.stack([w_gate[layer], w_up[layer]]).astype(BF16)
        shards[name + "_d"] = halves(w_down[layer])
    conv_pad = jnp.pad(ab_conv_w[0], ((0, 5), (0, 0)))
    shards.update(ab_in=halves(ab_w_in[0]), ab_out=halves(ab_w_out[0]), c_in=halves(c_w_in[0]), c_out=halves(c_w_out[0]),
                  conv=jnp.stack([conv_pad, jnp.zeros_like(conv_pad)]))
    first = _gather_weights([shards["pre0_gu"]])[0].reshape(W_SHAPES["pre0_gu"])

    small = _small_slab([ffn_pre_norm, mix_norm, ffn_post_norm, c_lower_bounds, final_norm.reshape(1, D), c_out_norm])
    small = small.reshape(SMALL_ROWS, 1, D)

    grad_x, grad_a, grad_b, grad_c, small_grad = _local_step(x[0], loss_target[0], dict(pre0_gu=first), small, shards)

    full = [grad_a, grad_b, grad_c]
    got = _swap_halves(full)
    pair = [_pair_sum("grad_pair_sum_%d" % n, g, r, c_idx) for n, (g, r) in enumerate(zip(full, got))]
    got = _send_to_owners(pair)
    mine = [_owner_sum("grad_owner_sum_%d" % n, g, r, p_idx) for n, (g, r) in enumerate(zip(pair, got))]
    ra, rb, rc = _share_halves(mine)
    ra = ra.reshape(8, D, FS)
    rb = rb.reshape(B_ROWS, D)
    rc = rc.reshape(D, 768)
    small_sum = _sum_small(small_grad)

    my_conv = lax.dynamic_slice(small_sum[R_CONV:R_CONV + 3], (0, (2 * xi + yi) * 128), (3, 128))
    grads = {
        "ffn_pre_norm": small_sum[R_PRE:R_PRE + 2], "ffn_pre_w_gate": ra[0:2], "ffn_pre_w_up": ra[2:4],
        "ffn_pre_w_down": rb[0:2 * FS].reshape(2, FS, D), "mix_norm": small_sum[R_MIX:R_MIX + 2],
        "ffn_post_norm": small_sum[R_POST:R_POST + 2], "ffn_post_w_gate": ra[4:6], "ffn_post_w_up": ra[6:8],
        "ffn_post_w_down": rb[2 * FS:4 * FS].reshape(2, FS, D), "ab_w_in": rc.reshape(1, D, 768),
        "ab_conv_w": my_conv.reshape(1, 3, 128), "ab_w_out": rb[B_ABOUT:B_ABOUT + 256].reshape(1, 256, D),
        "c_w_in": rb[B_CIN:B_CIN + D].reshape(1, D, D), "c_lower_bounds": small_sum[R_CLB:R_CLB + 2],
        "c_out_norm": small_sum[R_GAM:R_GAM + 1, :HD], "c_w_out": rb[B_COUT:B_COUT + 256].reshape(1, 256, D),
        "final_norm": small_sum[R_FIN],
    }
    weights = dict(ffn_pre_norm=ffn_pre_norm, ffn_pre_w_gate=ffn_pre_w_gate, ffn_pre_w_up=ffn_pre_w_up, ffn_pre_w_down=ffn_pre_w_down, mix_norm=mix_norm, ffn_post_norm=ffn_post_norm, ffn_post_w_gate=ffn_post_w_gate, ffn_post_w_up=ffn_post_w_up, ffn_post_w_down=ffn_post_w_down, ab_w_in=ab_w_in, ab_conv_w=ab_conv_w, ab_w_out=ab_w_out, c_w_in=c_w_in, c_lower_bounds=c_lower_bounds, c_out_norm=c_out_norm, c_w_out=c_w_out, final_norm=final_norm)
    m_in = dict(ffn_pre_norm=m_ffn_pre_norm, ffn_pre_w_gate=m_ffn_pre_w_gate, ffn_pre_w_up=m_ffn_pre_w_up, ffn_pre_w_down=m_ffn_pre_w_down, mix_norm=m_mix_norm, ffn_post_norm=m_ffn_post_norm, ffn_post_w_gate=m_ffn_post_w_gate, ffn_post_w_up=m_ffn_post_w_up, ffn_post_w_down=m_ffn_post_w_down, ab_w_in=m_ab_w_in, ab_conv_w=m_ab_conv_w, ab_w_out=m_ab_w_out, c_w_in=m_c_w_in, c_lower_bounds=m_c_lower_bounds, c_out_norm=m_c_out_norm, c_w_out=m_c_w_out, final_norm=m_final_norm)
    v_in = dict(ffn_pre_norm=v_ffn_pre_norm, ffn_pre_w_gate=v_ffn_pre_w_gate, ffn_pre_w_up=v_ffn_pre_w_up, ffn_pre_w_down=v_ffn_pre_w_down, mix_norm=v_mix_norm, ffn_post_norm=v_ffn_post_norm, ffn_post_w_gate=v_ffn_post_w_gate, ffn_post_w_up=v_ffn_post_w_up, ffn_post_w_down=v_ffn_post_w_down, ab_w_in=v_ab_w_in, ab_conv_w=v_ab_conv_w, ab_w_out=v_ab_w_out, c_w_in=v_c_w_in, c_lower_bounds=v_c_lower_bounds, c_out_norm=v_c_out_norm, c_w_out=v_c_w_out, final_norm=v_final_norm)
    names = list(weights)
    big = [n for n in names if weights[n].size >= 65536]
    tiny = [n for n in names if n not in big]

    delta, new_m, new_v = {}, {}, {}
    for n in big:
        shape = weights[n].shape
        two_d = (shape[0] * shape[1], shape[2])
        d, m2, v2 = _adamw("adamw_" + n, weights[n].reshape(two_d), grads[n].reshape(two_d),
                           m_in[n].reshape(two_d), v_in[n].reshape(two_d))
        delta[n], new_m[n], new_v[n] = d.reshape(shape), m2.reshape(shape), v2.reshape(shape)

    def tiny_slab(src):
        return _small_slab([src[n].reshape(-1, src[n].shape[-1]) for n in tiny])

    offs, row = {}, 0
    for n in tiny:
        nrows = weights[n].size // weights[n].shape[-1]
        offs[n] = (row, nrows)
        row += nrows
    d, m2, v2 = _adamw("adamw_small", tiny_slab(weights), tiny_slab(grads), tiny_slab(m_in), tiny_slab(v_in))
    for n in tiny:
        r0, nr = offs[n]
        shape = weights[n].shape
        for dst, src in ((delta, d), (new_m, m2), (new_v, v2)):
            dst[n] = src[r0:r0 + nr, :shape[-1]].reshape(shape)

    loss = small_sum[R_LOSS, 0]
    return (loss, grad_x.reshape(1, t, D), *[grads[n] for n in names], *[delta[n] for n in names],
            *[new_m[n] for n in names], *[new_v[n] for n in names])
```

```python
import functools
import math

import jax
import jax.numpy as jnp
from jax import lax
from jax.experimental import pallas as pl
from jax.experimental.pallas import tpu as pltpu

F32 = jnp.float32
BF16 = jnp.bfloat16
MESH = pl.DeviceIdType.MESH
ANY = pl.BlockSpec(memory_space=pl.ANY)

D = 1024
FS = 704
NSH = 4
RMS_EPS = 1e-6
MACARON = 0.5
CHUNK = 64
HD = 128
HGRN_HPS = 4
SBQ = 128
SB_PAIRS = 2
SB_DEAD = -105.0
CONV_HALO = 8
LANE = 128
ROW_TILE = 512
MM_TILE = 1024
WGRAD_TILE = 4096
VMEM_LIMIT = 48 * 1024 * 1024

ADAM_LR, ADAM_B1, ADAM_B2, ADAM_EPS, ADAM_WD, ADAM_STEP = 0.001, 0.9, 0.999, 1e-08, 0.01, 10

NN = ((1,), (0,))
NT = ((1,), (1,))
TN = ((0,), (0,))

SMALL_ROWS = 16
R_PRE, R_MIX, R_POST, R_CLB, R_FIN, R_GAM, R_CONV, R_LOSS = 0, 2, 4, 6, 8, 9, 10, 13

B_DOWN = 0
B_ABOUT = 4 * FS
B_CIN = B_ABOUT + 256
B_COUT = B_CIN + 1024
B_ROWS = B_COUT + 256


def _dg(a, b, dims):
    return lax.dot_general(a, b, (dims, ((), ())), preferred_element_type=F32)


def _split(x):
    hi = x.astype(BF16)
    lo = (x - hi.astype(F32)).astype(BF16)
    return hi, lo


def _dot3(a, b, dims):
    ah, al = _split(a)
    bh, bl = _split(b)
    return _dg(ah, bh, dims) + _dg(ah, bl, dims) + _dg(al, bh, dims)


def _sigmoid(x):
    return 1.0 / (1.0 + jnp.exp(-x))


def _params(sem):
    return pltpu.CompilerParams(dimension_semantics=sem, vmem_limit_bytes=VMEM_LIMIT)


def _spec(shape, imap):
    return pl.BlockSpec(shape, imap)


def _mm(name, pairs, *, grid, o_shape, o_dtype, o_spec, dims, kaxis, nk, acc_shape=None, res=None, scale=None,
        into=None, carry=None):
    npairs = len(pairs)
    operands, specs = [], []
    for a, a_spec, b, b_spec in pairs:
        operands += [a, b]
        specs += [a_spec, b_spec]
    if res is not None:
        operands.append(res[0])
        specs.append(res[1])
    aliases = {}
    if into is not None:
        aliases = {len(operands): 0}
        operands.append(into)
        specs.append(ANY)
    n_own = len(operands)
    nc = carry.n if carry is not None else 0
    if nc:
        operands += carry.groups
        specs += carry.in_specs

    def body(*refs):
        o_ref = refs[n_own + nc]
        riders = (refs[n_own:n_own + nc], refs[n_own + nc + 1:n_own + 2 * nc + 1], refs[-2], refs[-1]) if nc else None
        if nc:
            carry.ride(grid, riders, "start")
        part = None
        for n in range(npairs):
            d = _dg(refs[2 * n][...], refs[2 * n + 1][...], dims)
            part = d if part is None else part + d

        def finish(val):
            if scale is not None:
                val = val * scale
            if res is not None:
                val = val + refs[2 * npairs][...]
            o_ref[...] = val.astype(o_dtype)

        if nk == 1:
            finish(part)
        else:
            acc_ref = refs[n_own + 2 * nc + 1]
            k = pl.program_id(kaxis)

            @pl.when(k == 0)
            def _():
                acc_ref[...] = part

            @pl.when(k > 0)
            def _():
                acc_ref[...] += part

            @pl.when(k == nk - 1)
            def _():
                finish(acc_ref[...])
        if nc:
            carry.ride(grid, riders, "finish")

    sem = tuple("arbitrary" if (nc or (ax == kaxis and nk > 1)) else "parallel" for ax in range(len(grid)))
    outs = pl.pallas_call(
        body, name=name, grid=grid, in_specs=specs, out_specs=[o_spec] + (carry.out_specs if nc else []),
        out_shape=[jax.ShapeDtypeStruct(o_shape, o_dtype)] + (carry.out_shape if nc else []),
        scratch_shapes=([pltpu.VMEM(acc_shape, F32)] if nk > 1 else []) + (carry.scratch if nc else []),
        input_output_aliases=aliases,
        compiler_params=_params(sem),
    )(*operands)
    return (outs[0], carry.place(outs[1:])) if nc else outs[0]


def _norm_fwd(name, h, gain_slab, row):
    t = h.shape[0]
    tm = min(ROW_TILE, t)

    def body(h_ref, g_ref, o_ref):
        x = h_ref[...]
        r = lax.rsqrt(jnp.mean(x * x, axis=-1, keepdims=True) + RMS_EPS)
        o_ref[...] = (x * r * g_ref[...]).astype(BF16)

    return pl.pallas_call(
        body, name=name, grid=(t // tm,),
        in_specs=[_spec((tm, D), lambda i: (i, 0)), _spec((None, 1, D), lambda i: (row, 0, 0))],
        out_specs=_spec((tm, D), lambda i: (i, 0)),
        out_shape=jax.ShapeDtypeStruct((t, D), BF16),
        compiler_params=_params(("parallel",)),
    )(h, gain_slab)


def _norm_bwd(name, dhn, h, gain_slab, row, dres):
    t = h.shape[0]
    tm = min(ROW_TILE, t)
    nt = t // tm

    def body(dhn_ref, h_ref, g_ref, dres_ref, dh_ref, dhb_ref, dg_ref, acc_ref):
        i = pl.program_id(0)
        x = h_ref[...]
        r = lax.rsqrt(jnp.mean(x * x, axis=-1, keepdims=True) + RMS_EPS)
        xh = x * r
        dy = dhn_ref[...]
        gdy = dy * g_ref[...]
        dx = (gdy - xh * jnp.mean(gdy * xh, axis=-1, keepdims=True)) * r
        dh = dres_ref[...] + dx
        dh_ref[...] = dh
        dhb_ref[...] = dh.astype(BF16)
        part = jnp.sum((dy * xh).reshape(tm // 8, 8, D), axis=0)

        @pl.when(i == 0)
        def _():
            acc_ref[...] = part

        @pl.when(i > 0)
        def _():
            acc_ref[...] += part

        @pl.when(i == nt - 1)
        def _():
            dg_ref[...] = jnp.sum(acc_ref[...], axis=0, keepdims=True)

    row_spec = _spec((tm, D), lambda i: (i, 0))
    return pl.pallas_call(
        body, name=name, grid=(nt,),
        in_specs=[row_spec, row_spec, _spec((None, 1, D), lambda i: (row, 0, 0)), row_spec],
        out_specs=[row_spec, row_spec, _spec((1, D), lambda i: (0, 0))],
        out_shape=[jax.ShapeDtypeStruct((t, D), F32), jax.ShapeDtypeStruct((t, D), BF16),
                   jax.ShapeDtypeStruct((1, D), F32)],
        scratch_shapes=[pltpu.VMEM((8, D), F32)],
        compiler_params=_params(("arbitrary",)),
    )(dhn, h, gain_slab, dres)


def _final_loss(h, gain_slab, target):
    t = h.shape[0]
    tm = min(ROW_TILE, t)
    nt = t // tm

    def body(h_ref, g_ref, t_ref, dh_ref, dhb_ref, dg_ref, loss_ref, acc_ref, lacc_ref):
        i = pl.program_id(0)
        x = h_ref[...]
        g = g_ref[...]
        r = lax.rsqrt(jnp.mean(x * x, axis=-1, keepdims=True) + RMS_EPS)
        xh = x * r
        err = xh * g - t_ref[...]
        dy = err * (1.0 / D)
        gdy = dy * g
        dh = (gdy - xh * jnp.mean(gdy * xh, axis=-1, keepdims=True)) * r
        dh_ref[...] = dh
        dhb_ref[...] = dh.astype(BF16)
        part = jnp.sum((dy * xh).reshape(tm // 8, 8, D), axis=0)
        lpart = jnp.sum((err * err).reshape(tm // 8, 8, D), axis=0)

        @pl.when(i == 0)
        def _():
            acc_ref[...] = part
            lacc_ref[...] = lpart

        @pl.when(i > 0)
        def _():
            acc_ref[...] += part
            lacc_ref[...] += lpart

        @pl.when(i == nt - 1)
        def _():
            dg_ref[...] = jnp.sum(acc_ref[...], axis=0, keepdims=True)
            rows = jnp.sum(lacc_ref[...], axis=0, keepdims=True)
            loss_ref[...] = jnp.sum(rows, axis=1, keepdims=True) * (0.5 / D)

    row_spec = _spec((tm, D), lambda i: (i, 0))
    return pl.pallas_call(
        body, name="final_loss", grid=(nt,),
        in_specs=[row_spec, _spec((None, 1, D), lambda i: (R_FIN, 0, 0)), row_spec],
        out_specs=[row_spec, row_spec, _spec((1, D), lambda i: (0, 0)), _spec((1, 1), lambda i: (0, 0))],
        out_shape=[jax.ShapeDtypeStruct((t, D), F32), jax.ShapeDtypeStruct((t, D), BF16),
                   jax.ShapeDtypeStruct((1, D), F32), jax.ShapeDtypeStruct((1, 1), F32)],
        scratch_shapes=[pltpu.VMEM((8, D), F32), pltpu.VMEM((8, D), F32)],
        compiler_params=_params(("arbitrary",)),
    )(h, gain_slab, target)


def _ffn_up(name, hn, wgu, carry=None):
    t = hn.shape[0]
    tm = min(MM_TILE, t)
    grid = (NSH, t // tm)
    nc = carry.n if carry is not None else 0

    def body(*refs):
        x_ref, wg_ref, wu_ref = refs[:3]
        s_up_ref, s_gate_ref, a_ref = refs[3 + nc:6 + nc]
        riders = (refs[3:3 + nc], refs[6 + nc:6 + 2 * nc], refs[-2], refs[-1]) if nc else None
        if nc:
            carry.ride(grid, riders, "start")
        x = x_ref[...]
        g = _dg(x, wg_ref[...], NN)
        u = _dg(x, wu_ref[...], NN)
        sg = _sigmoid(g)
        silu = g * sg
        s_up_ref[...] = (MACARON * silu).astype(BF16)
        s_gate_ref[...] = (MACARON * u * (sg * (1.0 + g * (1.0 - sg)))).astype(BF16)
        a_ref[...] = (silu * u).astype(BF16)
        if nc:
            carry.ride(grid, riders, "finish")

    act = _spec((None, tm, FS), lambda s, i: (s, i, 0))
    shape = jax.ShapeDtypeStruct((NSH, t, FS), BF16)
    outs = pl.pallas_call(
        body, name=name, grid=grid,
        in_specs=[_spec((tm, D), lambda s, i: (i, 0)),
                  _spec((None, None, D, FS), lambda s, i: (s, 0, 0, 0)),
                  _spec((None, None, D, FS), lambda s, i: (s, 1, 0, 0))] + (carry.in_specs if nc else []),
        out_specs=[act, act, act] + (carry.out_specs if nc else []),
        out_shape=[shape, shape, shape] + (carry.out_shape if nc else []),
        scratch_shapes=carry.scratch if nc else [],
        compiler_params=_params(("arbitrary", "arbitrary") if nc else ("parallel", "parallel")),
    )(hn, wgu, wgu, *(carry.groups if nc else []))
    return (outs[0], outs[1], outs[2], carry.place(outs[3:]) if nc else [])


def _ffn_down(name, a, wd, h, carry=None):
    t = h.shape[0]
    tm = min(MM_TILE, t)
    return _mm(name, [(a, _spec((None, tm, FS), lambda i, k: (k, i, 0)),
                       wd, _spec((None, FS, D), lambda i, k: (k, 0, 0)))],
               grid=(t // tm, NSH), o_shape=(t, D), o_dtype=F32, o_spec=_spec((tm, D), lambda i, k: (i, 0)),
               dims=NN, kaxis=1, nk=NSH, acc_shape=(tm, D), res=(h, _spec((tm, D), lambda i, k: (i, 0))),
               scale=MACARON, carry=carry)


def _ffn_bwd_up(name, dhb, wd, s_up, s_gate):
    t = dhb.shape[0]
    tm = min(MM_TILE, t)

    def body(dh_ref, wd_ref, s_up_ref, s_gate_ref, dg_ref, du_ref):
        da = _dg(dh_ref[...], wd_ref[...], NT)
        du_ref[...] = (da * s_up_ref[...].astype(F32)).astype(BF16)
        dg_ref[...] = (da * s_gate_ref[...].astype(F32)).astype(BF16)

    act = _spec((None, tm, FS), lambda s, i: (s, i, 0))
    shape = jax.ShapeDtypeStruct((NSH, t, FS), BF16)
    return pl.pallas_call(
        body, name=name, grid=(NSH, t // tm),
        in_specs=[_spec((tm, D), lambda s, i: (i, 0)), _spec((None, FS, D), lambda s, i: (s, 0, 0)), act, act],
        out_specs=[act, act], out_shape=[shape, shape],
        compiler_params=_params(("parallel", "parallel")),
    )(dhb, wd, s_up, s_gate)


def _wgrad(name, a, a_spec, b, b_spec, out_rows, out_cols, t, tt, group, slot, scale=None):
    first = isinstance(group, int)
    rows = group if first else group.shape[1]
    return _mm(name, [(a, a_spec, b, b_spec)], grid=(NSH, t // tt),
               o_shape=(NSH, rows, out_cols), o_dtype=BF16,
               o_spec=_spec((None, out_rows, out_cols), lambda s, k: (s, slot, 0)),
               dims=TN, kaxis=1, nk=t // tt, acc_shape=(out_rows, out_cols), scale=scale,
               into=None if first else group)


def _ffn_backward(tag, dh, dhb, h_in, hn, s_up, s_gate, a, wgu, wd, gate_idx, up_idx, down_idx, small, norm_row,
                  grad_a, grad_b):
    t = dh.shape[0]
    tm = min(MM_TILE, t)
    tt = min(WGRAD_TILE, t)
    dg, du = _ffn_bwd_up(tag + "_bwd_up", dhb, wd, s_up, s_gate)
    tok = _spec((tt, D), lambda s, k: (k, 0))
    hid = _spec((None, tt, FS), lambda s, k: (s, k, 0))
    grad_b = _wgrad(tag + "_dwd", a, hid, dhb, tok, FS, D, t, tt, grad_b, down_idx, scale=MACARON)
    grad_a = _wgrad(tag + "_dwg", hn, tok, dg, hid, D, FS, t, tt, grad_a, gate_idx)
    grad_a = _wgrad(tag + "_dwu", hn, tok, du, hid, D, FS, t, tt, grad_a, up_idx)
    act = _spec((None, tm, FS), lambda i, k: (k, i, 0))
    dhn = _mm(tag + "_dhn",
              [(dg, act, wgu, _spec((None, None, D, FS), lambda i, k: (k, 0, 0, 0))),
               (du, act, wgu, _spec((None, None, D, FS), lambda i, k: (k, 1, 0, 0)))],
              grid=(t // tm, NSH), o_shape=(t, D), o_dtype=F32, o_spec=_spec((tm, D), lambda i, k: (i, 0)),
              dims=NT, kaxis=1, nk=NSH, acc_shape=(tm, D))
    dh_in, dhb_in, d_gain = _norm_bwd(tag + "_norm_bwd", dhn, h_in, small, norm_row, dh)
    return dh_in, dhb_in, grad_a, grad_b, d_gain


def _conv_fwd(proj_a, conv_w):
    t = proj_a.shape[0]
    tm = min(ROW_TILE, t)
    hb = tm // CONV_HALO

    def body(ab_ref, ac_ref, ax_ref, acp_ref, axp_ref, w_ref, y_ref):
        i = pl.program_id(1)
        u = ac_ref[...] * ax_ref[...]
        up = jnp.where(i > 0, acp_ref[...] * axp_ref[...], 0.0)
        ext = jnp.concatenate([up, u], axis=0)
        u1 = pltpu.roll(ext, 1, 0)[CONV_HALO:]
        u2 = pltpu.roll(ext, 2, 0)[CONV_HALO:]
        w = w_ref[...]
        conv = w[0:1] * u2 + w[1:2] * u1 + w[2:3] * u
        y_ref[...] = (ab_ref[...] * conv).astype(BF16)

    def cur(off):
        return _spec((tm, LANE), lambda j, i: (i, off + j))

    def prev(off):
        return _spec((CONV_HALO, LANE), lambda j, i: (jnp.maximum(i * hb - 1, 0), off + j))

    return pl.pallas_call(
        body, name="conv_fwd", grid=(4, t // tm),
        in_specs=[cur(0), cur(4), cur(8), prev(4), prev(8),
                  _spec((None, None, 8, LANE), lambda j, i: (j, 0, 0, 0))],
        out_specs=_spec((tm, LANE), lambda j, i: (i, j)),
        out_shape=jax.ShapeDtypeStruct((t, 512), BF16),
        compiler_params=_params(("parallel", "parallel")),
    )(proj_a, proj_a, proj_a, proj_a, proj_a, conv_w)


def _conv_bwd(proj_a, conv_w, dy):
    t = proj_a.shape[0]
    tm = min(ROW_TILE, t)
    hb = tm // CONV_HALO
    nt = t // tm

    def body(ab_ref, ac_ref, ax_ref, dy_ref, acp_ref, axp_ref, abn_ref, dyn_ref, w_ref,
             dab_ref, dac_ref, dax_ref, dw_ref, acc_ref):
        i = pl.program_id(1)
        ab, ac, ax = ab_ref[...], ac_ref[...], ax_ref[...]
        u = ac * ax
        up = jnp.where(i > 0, acp_ref[...] * axp_ref[...], 0.0)
        ext = jnp.concatenate([up, u], axis=0)
        u1 = pltpu.roll(ext, 1, 0)[CONV_HALO:]
        u2 = pltpu.roll(ext, 2, 0)[CONV_HALO:]
        w = w_ref[...]
        conv = w[0:1] * u2 + w[1:2] * u1 + w[2:3] * u
        dy_v = dy_ref[...]
        dab_ref[...] = (dy_v * conv).astype(BF16)
        dc = dy_v * ab
        dcn = jnp.where(i < nt - 1, dyn_ref[...] * abn_ref[...], 0.0)
        extn = jnp.concatenate([dc, dcn], axis=0)
        n = tm + CONV_HALO
        dc1 = pltpu.roll(extn, n - 1, 0)[:tm]
        dc2 = pltpu.roll(extn, n - 2, 0)[:tm]
        du = w[2:3] * dc + w[1:2] * dc1 + w[0:1] * dc2
        dac_ref[...] = (du * ax).astype(BF16)
        dax_ref[...] = (du * ac).astype(BF16)
        rid = lax.broadcasted_iota(jnp.int32, (8, LANE), 0)
        part = jnp.where(rid == 0, jnp.sum(dc * u2, axis=0, keepdims=True),
                         jnp.where(rid == 1, jnp.sum(dc * u1, axis=0, keepdims=True),
                                   jnp.where(rid == 2, jnp.sum(dc * u, axis=0, keepdims=True), 0.0)))

        @pl.when(i == 0)
        def _():
            acc_ref[...] = part

        @pl.when(i > 0)
        def _():
            acc_ref[...] += part

        @pl.when(i == nt - 1)
        def _():
            dw_ref[...] = acc_ref[...]

    def cur(off):
        return _spec((tm, LANE), lambda j, i: (i, off + j))

    def prev(off):
        return _spec((CONV_HALO, LANE), lambda j, i: (jnp.maximum(i * hb - 1, 0), off + j))

    def nxt(off):
        return _spec((CONV_HALO, LANE), lambda j, i: (jnp.minimum((i + 1) * hb, nt * hb - 1), off + j))

    outs = pl.pallas_call(
        body, name="conv_bwd", grid=(4, nt),
        in_specs=[cur(0), cur(4), cur(8), cur(0), prev(4), prev(8), nxt(0), nxt(0),
                  _spec((None, None, 8, LANE), lambda j, i: (j, 0, 0, 0))],
        out_specs=[_spec((tm, LANE), lambda j, i: (i, j)), _spec((tm, LANE), lambda j, i: (i, j)),
                   _spec((tm, LANE), lambda j, i: (i, j)), _spec((None, 8, LANE), lambda j, i: (j, 0, 0))],
        out_shape=[jax.ShapeDtypeStruct((t, 512), BF16), jax.ShapeDtypeStruct((t, 512), BF16),
                   jax.ShapeDtypeStruct((t, 512), BF16), jax.ShapeDtypeStruct((4, 8, LANE), F32)],
        scratch_shapes=[pltpu.VMEM((8, LANE), F32)],
        compiler_params=_params(("parallel", "arbitrary")),
    )(proj_a, proj_a, proj_a, dy, proj_a, proj_a, proj_a, dy, conv_w)
    return outs


def _log_sigmoid(z):
    return jnp.minimum(z, 0.0) - jnp.log(1.0 + jnp.exp(-jnp.abs(z)))


def _sb_masks():
    row = lax.broadcasted_iota(jnp.int32, (SBQ, SBQ), 0)
    col = lax.broadcasted_iota(jnp.int32, (SBQ, SBQ), 1)
    return row, col


def _ones_where(mask):
    return jnp.where(mask, 1.0, 0.0).astype(BF16)


def _head_mask(head):
    lane = lax.broadcasted_iota(jnp.int32, (1, LANE), 1)
    return lane >= 64 if head else lane < 64


def _sb_fwd(proj_b, carry=None):
    t = proj_b.shape[0]
    nq = t // SBQ
    scale = 1.0 / math.sqrt(64.0)

    npair = SB_PAIRS
    wide = npair * LANE
    ngrp = 4 // npair
    chains = [(p, head) for p in range(npair) for head in range(2)]

    grid = (ngrp, nq)
    nc = carry.n if carry is not None else 0

    def body(*refs):
        q_ref, k_ref, v_ref = refs[:3]
        y_ref, l_ref, n_ref = refs[3 + nc:6 + nc]
        riders = (refs[3:3 + nc], refs[6 + nc:6 + 2 * nc], refs[-2], refs[-1]) if nc else None
        if nc:
            carry.ride(grid, riders, "start")
        grp = pl.program_id(0)
        qi = pl.program_id(1)
        row, col = _sb_masks()
        strict = col < row
        m_suffix = _ones_where(row > col)
        qh = []
        for p, head in chains:
            q_all = q_ref[:, p * LANE:(p + 1) * LANE]
            qh.append(jnp.where(_head_mask(head), q_all, jnp.zeros_like(q_all)))

        def block(kb, state, diag):
            start = pl.multiple_of(kb * SBQ, SBQ)
            new = []
            for n, (p, head) in enumerate(chains):
                run, acc = state[n]
                kk = k_ref[pl.ds(start, SBQ), p * LANE:(p + 1) * LANE]
                vv = v_ref[pl.ds(start, SBQ), p * LANE:(p + 1) * LANE]
                z = _dg(qh[n], kk, NT) * scale
                lb = _log_sigmoid(z)
                lk = lb - z
                if diag:
                    lk = jnp.where(strict, lk, 0.0)
                hi, lo = _split(lk)
                later = run + _dg(hi, m_suffix, NN) + _dg(lo, m_suffix, NN)
                w = jnp.exp(lb + later)
                if diag:
                    w = jnp.where(strict, w, 0.0)
                acc = acc + _dg(w.astype(BF16), vv, NN)
                run = run + jnp.sum(hi.astype(F32) + lo.astype(F32), axis=1, keepdims=True)
                new.append((run, acc))
            return tuple(new)

        zero = (jnp.zeros((SBQ, 1), F32), jnp.zeros((SBQ, LANE), F32))
        state = block(qi, tuple(zero for _ in chains), True)

        def live(c):
            top = c[1][0][0]
            for n in range(1, len(chains)):
                top = jnp.maximum(top, c[1][n][0])
            return jnp.logical_and(c[0] < qi, jnp.max(top) > SB_DEAD)

        def step(c):
            return c[0] + 1, block(qi - 1 - c[0], c[1], False)

        count, state = lax.while_loop(live, step, (jnp.int32(0), state))
        n_ref[grp * nq + qi] = count.astype(F32)
        for p in range(npair):
            (run0, acc0), (run1, acc1) = state[2 * p], state[2 * p + 1]
            hm = _head_mask(0)
            y_ref[:, p * LANE:(p + 1) * LANE] = jnp.where(hm, acc0, acc1).astype(BF16)
            l_ref[p] = jnp.where(hm, run0, run1)
        if nc:
            carry.ride(grid, riders, "finish")

    outs = pl.pallas_call(
        body, name="sb_fwd", grid=grid,
        in_specs=[_spec((SBQ, wide), lambda g, i: (i, g)),
                  _spec((t, wide), lambda g, i: (0, ngrp + g)),
                  _spec((t, wide), lambda g, i: (0, 2 * ngrp + g))] + (carry.in_specs if nc else []),
        out_specs=[_spec((SBQ, wide), lambda g, i: (i, g)), _spec((npair, SBQ, LANE), lambda g, i: (g, i, 0)),
                   pl.BlockSpec(memory_space=pltpu.SMEM)] + (carry.out_specs if nc else []),
        out_shape=[jax.ShapeDtypeStruct((t, 512), BF16), jax.ShapeDtypeStruct((4, t, LANE), F32),
                   jax.ShapeDtypeStruct((ngrp * nq,), F32)] + (carry.out_shape if nc else []),
        scratch_shapes=carry.scratch if nc else [],
        compiler_params=_params(("arbitrary", "arbitrary")),
    )(proj_b, proj_b, proj_b, *(carry.groups if nc else []))
    return outs[0], outs[1], outs[2], (carry.place(outs[3:]) if nc else [])


def _sb_bwd(proj_b, dy, ltot, nblk, rider=None):
    t = proj_b.shape[0]
    nq = t // SBQ
    scale = 1.0 / math.sqrt(64.0)

    npair = SB_PAIRS
    wide = npair * LANE
    ngrp = 4 // npair
    chains = [(p, head) for p in range(npair) for head in range(2)]
    grid = (ngrp, nq)
    nr = rider.n if rider is not None else 0

    def body(*refs):
        q_ref, k_ref, v_ref, dy_ref, l_ref, n_ref = refs[:6]
        dq_ref, dk_ref, dv_ref = refs[6 + nr:9 + nr]
        dk_acc, dv_acc = refs[9 + 2 * nr:11 + 2 * nr]
        riders = (refs[6:6 + nr], refs[9 + nr:9 + 2 * nr], refs[-2], refs[-1]) if nr else None
        if nr:
            rider.ride(grid, riders, "start")
        grp = pl.program_id(0)
        qi = pl.program_id(1)

        @pl.when(qi == 0)
        def _():
            dk_acc[...] = jnp.zeros_like(dk_acc)
            dv_acc[...] = jnp.zeros_like(dv_acc)

        row, col = _sb_masks()
        strict = col < row
        m_prefix = _ones_where(row <= col)
        m_before = _ones_where(row < col)
        qh, doh, ltot_h = [], [], []
        for p, head in chains:
            pl_ = slice(p * LANE, (p + 1) * LANE)
            q_all = q_ref[:, pl_]
            do_all = dy_ref[:, pl_].astype(BF16)
            hm = _head_mask(head)
            qh.append(jnp.where(hm, q_all, jnp.zeros_like(q_all)))
            doh.append(jnp.where(hm, do_all, jnp.zeros_like(do_all)))
            ltot_h.append(l_ref[p][:, head * 64:head * 64 + 1])

        def block(kb, carry, diag):
            start = pl.multiple_of(kb * SBQ, SBQ)
            new = []
            for p in range(npair):
                pl_ = slice(p * LANE, (p + 1) * LANE)
                kk = k_ref[pl.ds(start, SBQ), pl_]
                vv = v_ref[pl.ds(start, SBQ), pl_]
                dk_blk, dv_blk = None, None
                for head in range(2):
                    n = 2 * p + head
                    seen, dseen, dq = carry[n]
                    z = _dg(qh[n], kk, NT) * scale
                    lb = _log_sigmoid(z)
                    lk = lb - z
                    if diag:
                        lk = jnp.where(strict, lk, 0.0)
                    hi, lo = _split(lk)
                    later = (ltot_h[n] - seen) - (_dg(hi, m_prefix, NN) + _dg(lo, m_prefix, NN))
                    w = jnp.exp(lb + later)
                    if diag:
                        w = jnp.where(strict, w, 0.0)
                    da = w * _dg(doh[n], vv, NT)
                    dv_h = _dg(w.astype(BF16), doh[n], TN)
                    dah, dal = _split(da)
                    dlk = dseen + _dg(dah, m_before, NN) + _dg(dal, m_before, NN)
                    sig = jnp.exp(lb)
                    dz = (da * (1.0 - sig) - dlk * sig) * scale
                    if diag:
                        dz = jnp.where(strict, dz, 0.0)
                    dzb = dz.astype(BF16)
                    dk_h = _dg(dzb, qh[n], TN)
                    dk_blk = dk_h if dk_blk is None else dk_blk + dk_h
                    dv_blk = dv_h if dv_blk is None else dv_blk + dv_h
                    new.append((seen + jnp.sum(hi.astype(F32) + lo.astype(F32), axis=1, keepdims=True),
                                dseen + jnp.sum(da, axis=1, keepdims=True),
                                dq + _dg(dzb, kk, NN)))
                dk_acc[pl.ds(start, SBQ), pl_] += dk_blk
                dv_acc[pl.ds(start, SBQ), pl_] += dv_blk
            return tuple(new)

        zero = (jnp.zeros((SBQ, 1), F32), jnp.zeros((SBQ, 1), F32), jnp.zeros((SBQ, LANE), F32))
        first = qi - n_ref[grp * nq + qi].astype(jnp.int32)
        carry = lax.fori_loop(first, qi, lambda kb, c: block(kb, c, False), tuple(zero for _ in chains))
        carry = block(qi, carry, True)
        for p in range(npair):
            dq_ref[:, p * LANE:(p + 1) * LANE] = jnp.where(_head_mask(0), carry[2 * p][2], carry[2 * p + 1][2]).astype(BF16)

        @pl.when(qi == nq - 1)
        def _():
            dk_ref[...] = dk_acc[...].astype(BF16)
            dv_ref[...] = dv_acc[...].astype(BF16)

        if nr:
            rider.ride(grid, riders, "finish")

    full = jax.ShapeDtypeStruct((t, 512), BF16)
    outs = pl.pallas_call(
        body, name="sb_bwd", grid=grid,
        in_specs=[_spec((SBQ, wide), lambda g, i: (i, g)),
                  _spec((t, wide), lambda g, i: (0, ngrp + g)),
                  _spec((t, wide), lambda g, i: (0, 2 * ngrp + g)),
                  _spec((SBQ, wide), lambda g, i: (i, ngrp + g)),
                  _spec((npair, SBQ, LANE), lambda g, i: (g, i, 0)),
                  pl.BlockSpec(memory_space=pltpu.SMEM)] + (rider.in_specs if nr else []),
        out_specs=[_spec((SBQ, wide), lambda g, i: (i, g)),
                   _spec((t, wide), lambda g, i: (0, g)), _spec((t, wide), lambda g, i: (0, g))]
        + (rider.out_specs if nr else []),
        out_shape=[full, full, full] + (rider.out_shape if nr else []),
        scratch_shapes=[pltpu.VMEM((t, wide), F32), pltpu.VMEM((t, wide), F32)] + (rider.scratch if nr else []),
        compiler_params=_params(("arbitrary", "arbitrary")),
    )(proj_b, proj_b, proj_b, dy, ltot, nblk, *(rider.groups if nr else []))
    return outs[0], outs[1], outs[2], (rider.place(outs[3:]) if nr else [])


def _hgrn_gates(qr, fr, c0, c1):
    mx = jnp.maximum(c0, c1)
    e0, e1 = jnp.exp(c0 - mx), jnp.exp(c1 - mx)
    lb = e1 / (e0 + e1)
    sx = _sigmoid(fr)
    f = lb + (1.0 - lb) * sx
    k = (1.0 - lb) * (1.0 - sx)
    sq = _sigmoid(qr)
    return lb, sx, f, k, sq, qr * sq


def _hgrn_decays(f):
    row = lax.broadcasted_iota(jnp.int32, (CHUNK, CHUNK), 0)
    col = lax.broadcasted_iota(jnp.int32, (CHUNK, CHUNK), 1)
    tril = col <= row
    hi, lo = _split(jnp.log(f))
    trib = _ones_where(tril)
    b = _dg(trib, hi, NN) + _dg(trib, lo, NN)
    bm = b[CHUNK // 2 - 1:CHUNK // 2]
    bl = b[CHUNK - 1:CHUNK]
    return tril, col >= row, b, bm, bl


def _hgrn_fwd(proj_c, small, carry=None):
    t = proj_c.shape[1]
    nc = t // CHUNK
    nh = D // HD

    hps = HGRN_HPS
    wide = hps * HD
    grid = (nh // hps, nc)
    nr = carry.n if carry is not None else 0

    def body(*refs):
        p_ref, c0_ref, c1_ref, gam_ref = refs[:4]
        o_ref, y_ref, sst_ref = refs[4 + nr:7 + nr]
        st_ref = refs[7 + 2 * nr]
        riders = (refs[4:4 + nr], refs[7 + nr:7 + 2 * nr], refs[-2], refs[-1]) if nr else None
        if nr:
            carry.ride(grid, riders, "start")
        c = pl.program_id(1)

        @pl.when(c == 0)
        def _():
            st_ref[...] = jnp.zeros_like(st_ref)

        for j in range(hps):
            ln = slice(j * HD, (j + 1) * HD)
            st0 = st_ref[j]
            sst_ref[j] = st0
            qr, fr, v, g = p_ref[0, :, ln], p_ref[1, :, ln], p_ref[2, :, ln], p_ref[3, :, ln]
            lb, sx, f, k, sq, q = _hgrn_gates(qr, fr, c0_ref[:, ln], c1_ref[:, ln])
            tril, _, b, bm, bl = _hgrn_decays(f)
            qd = q * jnp.exp(b)
            qt = q * jnp.exp(b - bm)
            kt = k * jnp.exp(bm - b)
            kl = k * jnp.exp(bl - b)
            vb = v.astype(BF16)
            att = jnp.where(tril, _dot3(qt, kt, NT), 0.0)
            o = _dg(qd.astype(BF16), st0.astype(BF16), NT) + _dg(att.astype(BF16), vb, NN)
            st_ref[j] = st0 * jnp.exp(bl) + _dg(vb, kl.astype(BF16), TN)
            o_ref[:, ln] = o
            r = lax.rsqrt(jnp.mean(o * o, axis=-1, keepdims=True) + RMS_EPS)
            y_ref[:, ln] = (o * r * gam_ref[...] * (g * _sigmoid(g))).astype(BF16)
        if nr:
            carry.ride(grid, riders, "finish")

    outs = pl.pallas_call(
        body, name="hgrn_fwd", grid=grid,
        in_specs=[_spec((4, CHUNK, wide), lambda h, c: (0, c, h)),
                  _spec((None, 1, wide), lambda h, c: (R_CLB, 0, h)),
                  _spec((None, 1, wide), lambda h, c: (R_CLB + 1, 0, h)),
                  _spec((None, 1, HD), lambda h, c: (R_GAM, 0, 0))] + (carry.in_specs if nr else []),
        out_specs=[_spec((CHUNK, wide), lambda h, c: (c, h)), _spec((CHUNK, wide), lambda h, c: (c, h)),
                   _spec((None, hps, HD, HD), lambda h, c: (c, h, 0, 0))] + (carry.out_specs if nr else []),
        out_shape=[jax.ShapeDtypeStruct((t, D), F32), jax.ShapeDtypeStruct((t, D), BF16),
                   jax.ShapeDtypeStruct((nc, nh, HD, HD), F32)] + (carry.out_shape if nr else []),
        scratch_shapes=[pltpu.VMEM((hps, HD, HD), F32)] + (carry.scratch if nr else []),
        compiler_params=_params(("arbitrary", "arbitrary")),
    )(proj_c, small, small, small, *(carry.groups if nr else []))
    return outs[0], outs[1], outs[2], (carry.place(outs[3:]) if nr else [])


def _hgrn_bwd(proj_c, small, o, sst, dyc, rider=None):
    t = proj_c.shape[1]
    nc = t // CHUNK
    nh = D // HD

    hps = HGRN_HPS
    wide = hps * HD
    ng = nh // hps
    grid = (ng, nc)
    nr = rider.n if rider is not None else 0

    def body(*refs):
        p_ref, c0_ref, c1_ref, gam_ref, o_ref, sst_ref, dy_ref = refs[:7]
        dp_ref, dclb_ref, dgam_ref = refs[7 + nr:10 + nr]
        dst_ref, dlb_acc, dgam_acc = refs[10 + 2 * nr:13 + 2 * nr]
        riders = (refs[7:7 + nr], refs[10 + nr:10 + 2 * nr], refs[-2], refs[-1]) if nr else None
        if nr:
            rider.ride(grid, riders, "start")
        group = pl.program_id(0)
        step = pl.program_id(1)

        @pl.when(step == 0)
        def _():
            dst_ref[...] = jnp.zeros_like(dst_ref)
            dlb_acc[...] = jnp.zeros_like(dlb_acc)

        @pl.when((step == 0) & (group == 0))
        def _():
            dgam_acc[...] = jnp.zeros_like(dgam_acc)

        gam = gam_ref[...]
        for j in range(hps):
            ln = slice(j * HD, (j + 1) * HD)
            st0 = sst_ref[j]
            dst1 = dst_ref[j]
            qr, fr, v, g = p_ref[0, :, ln], p_ref[1, :, ln], p_ref[2, :, ln], p_ref[3, :, ln]
            lb, sx, f, k, sq, q = _hgrn_gates(qr, fr, c0_ref[:, ln], c1_ref[:, ln])
            tril, triu, b, bm, bl = _hgrn_decays(f)
            eb = jnp.exp(b)
            e_qt = jnp.exp(b - bm)
            e_kt = jnp.exp(bm - b)
            e_kl = jnp.exp(bl - b)
            e_bl = jnp.exp(bl)
            qd, qt, kt, kl = q * eb, q * e_qt, k * e_kt, k * e_kl
            ov = o_ref[:, ln]
            r = lax.rsqrt(jnp.mean(ov * ov, axis=-1, keepdims=True) + RMS_EPS)
            oh = ov * r
            sg = _sigmoid(g)
            dy = dy_ref[:, ln]
            dp_ref[3, :, ln] = (dy * oh * gam * (sg * (1.0 + g * (1.0 - sg)))).astype(BF16)
            dyv = dy * (g * sg)
            dgam_acc[...] += jnp.sum(dyv * oh, axis=0, keepdims=True)
            gdy = dyv * gam
            do = (gdy - oh * jnp.mean(gdy * oh, axis=-1, keepdims=True)) * r
            dob, vb = do.astype(BF16), v.astype(BF16)
            st0b, dst1b = st0.astype(BF16), dst1.astype(BF16)
            st1 = st0 * e_bl + _dg(vb, kl.astype(BF16), TN)
            att = jnp.where(tril, _dot3(qt, kt, NT), 0.0)
            datt = jnp.where(tril, _dg(dob, vb, NT), 0.0)
            dv = _dg(att.astype(BF16), dob, TN) + _dg(kl.astype(BF16), dst1b, NT)
            dq = _dot3(datt, kt, NN) * e_qt + _dg(dob, st0b, NN) * eb
            dk = _dot3(datt, qt, TN) * e_kt + _dg(vb, dst1b, NN) * e_kl
            db = q * dq - k * dk
            last = lax.broadcasted_iota(jnp.int32, (CHUNK, 1), 0) == CHUNK - 1
            db = db + jnp.where(last, jnp.sum(dst1 * st1, axis=0, keepdims=True), 0.0)
            dbh, dbl = _split(db)
            triub = _ones_where(triu)
            dlf = _dg(triub, dbh, NN) + _dg(triub, dbl, NN)
            dst_ref[j] = dst1 * e_bl + _dg(dob, qd.astype(BF16), TN)
            dp_ref[0, :, ln] = (dq * (sq * (1.0 + qr * (1.0 - sq)))).astype(BF16)
            tmp = dlf / f - dk
            dp_ref[1, :, ln] = (tmp * (1.0 - lb) * sx * (1.0 - sx)).astype(BF16)
            dp_ref[2, :, ln] = dv.astype(BF16)
            dlb_acc[:, ln] += jnp.sum((1.0 - sx) * tmp, axis=0, keepdims=True)

            @pl.when(step == nc - 1)
            def _():
                d1 = dlb_acc[:, ln] * lb * (1.0 - lb)
                dclb_ref[:, ln] = jnp.where(lax.broadcasted_iota(jnp.int32, (2, HD), 0) == 0, -d1, d1)

        @pl.when((step == nc - 1) & (group == ng - 1))
        def _():
            dgam_ref[...] = dgam_acc[...]

        if nr:
            rider.ride(grid, riders, "finish")

    rev = lambda h, s: (nc - 1 - s, h)
    outs = pl.pallas_call(
        body, name="hgrn_bwd", grid=grid,
        in_specs=[_spec((4, CHUNK, wide), lambda h, s: (0, nc - 1 - s, h)),
                  _spec((None, 1, wide), lambda h, s: (R_CLB, 0, h)),
                  _spec((None, 1, wide), lambda h, s: (R_CLB + 1, 0, h)),
                  _spec((None, 1, HD), lambda h, s: (R_GAM, 0, 0)),
                  _spec((CHUNK, wide), rev),
                  _spec((None, hps, HD, HD), lambda h, s: (nc - 1 - s, h, 0, 0)),
                  _spec((CHUNK, wide), rev)] + (rider.in_specs if nr else []),
        out_specs=[_spec((4, CHUNK, wide), lambda h, s: (0, nc - 1 - s, h)),
                   _spec((2, wide), lambda h, s: (0, h)),
                   _spec((1, HD), lambda h, s: (0, 0))] + (rider.out_specs if nr else []),
        out_shape=[jax.ShapeDtypeStruct((4, t, D), BF16), jax.ShapeDtypeStruct((2, D), F32),
                   jax.ShapeDtypeStruct((1, HD), F32)] + (rider.out_shape if nr else []),
        scratch_shapes=[pltpu.VMEM((hps, HD, HD), F32), pltpu.VMEM((1, wide), F32), pltpu.VMEM((1, HD), F32)]
        + (rider.scratch if nr else []),
        compiler_params=_params(("arbitrary", "arbitrary")),
    )(proj_c, small, small, small, o, sst, dyc, *(rider.groups if nr else []))
    return outs[0], outs[1], outs[2], (rider.place(outs[3:]) if nr else [])


def _adamw(name, w, g, m, v):
    rows, cols = w.shape
    br = rows
    for cand in (512, 352, 256):
        if rows % cand == 0:
            br = cand
            break
    c1 = 1.0 - ADAM_B1 ** ADAM_STEP
    c2 = 1.0 - ADAM_B2 ** ADAM_STEP

    def body(w_ref, g_ref, m_ref, v_ref, d_ref, mo_ref, vo_ref):
        gv = g_ref[...]
        mn = ADAM_B1 * m_ref[...] + (1.0 - ADAM_B1) * gv
        vn = ADAM_B2 * v_ref[...] + (1.0 - ADAM_B2) * (gv * gv)
        mo_ref[...] = mn
        vo_ref[...] = vn
        d_ref[...] = -ADAM_LR * ((mn / c1) / (jnp.sqrt(vn / c2) + ADAM_EPS) + ADAM_WD * w_ref[...])

    blk = _spec((br, cols), lambda i: (i, 0))
    shape = jax.ShapeDtypeStruct((rows, cols), F32)
    return pl.pallas_call(
        body, name=name, grid=(rows // br,), in_specs=[blk] * 4, out_specs=[blk] * 3, out_shape=[shape] * 3,
        compiler_params=_params(("parallel",)),
    )(w, g, m, v)


def _place():
    x, y, c = lax.axis_index("x"), lax.axis_index("y"), lax.axis_index("c")
    chips = [(1 - x, y), (x, 1 - y), (1 - x, 1 - y)]
    return x, y, c, chips


class _Rider:
    n = 0

    def ride(self, grid, refs, when):
        if not self.n:
            return
        ids = [pl.program_id(a) for a in range(len(grid))]
        edge = [i == (0 if when == "start" else g - 1) for i, g in zip(ids, grid)]
        cond = functools.reduce(jnp.logical_and, edge)

        @pl.when(cond)
        def _():
            (self.start if when == "start" else self.finish)(*refs)


class _Gather(_Rider):
    def __init__(self, groups):
        self.groups = list(groups)
        self.n = len(self.groups)
        self.in_specs = [ANY] * self.n
        self.out_specs = [ANY] * self.n
        self.out_shape = [jax.ShapeDtypeStruct((NSH,) + g.shape, g.dtype) for g in self.groups]
        self.scratch = [pltpu.SemaphoreType.DMA((6 * self.n,)), pltpu.SemaphoreType.DMA((6 * self.n,))] if self.n else []

    def _copies(self, ins, outs, send, recv):
        x, y, c, chips = _place()
        sibling = (x, y, 1 - c)

        def half(gi, chip, hc):
            return outs[gi].at[2 * chip[0] + chip[1], hc]

        def copy(gi, k, src, dst, to):
            return pltpu.make_async_remote_copy(src_ref=src, dst_ref=dst, send_sem=send.at[6 * gi + k],
                                                recv_sem=recv.at[6 * gi + k], device_id=to, device_id_type=MESH)

        pairs = [(gi, j, chip) for gi in range(self.n) for j, chip in enumerate(chips)]
        first = [copy(gi, j, ins[gi].at[c], half(gi, (x, y), c), (*chip, c)) for gi, j, chip in pairs]
        landed = [copy(gi, j, half(gi, chip, c), half(gi, chip, c), sibling) for gi, j, chip in pairs]
        relay = [copy(gi, 3 + j, half(gi, chip, c), half(gi, chip, c), sibling) for gi, j, chip in pairs]
        relayed = [copy(gi, 3 + j, half(gi, chip, 1 - c), half(gi, chip, 1 - c), sibling) for gi, j, chip in pairs]
        return first, landed, relay, relayed

    def start(self, ins, outs, send, recv):
        for cp in self._copies(ins, outs, send, recv)[0]:
            cp.start()

    def finish(self, ins, outs, send, recv):
        first, landed, relay, relayed = self._copies(ins, outs, send, recv)
        for arrived, onward in zip(landed, relay):
            arrived.wait_recv()
            onward.start()
        for cp in relayed:
            cp.wait_recv()
        for cp in first + relay:
            cp.wait_send()

    def place(self, outs):
        me = 2 * lax.axis_index("x") + lax.axis_index("y")
        return [lax.dynamic_update_index_in_dim(o, g, me, 0) for o, g in zip(outs, self.groups)]


def _gather_weights(groups):
    gather = _Gather(groups)
    ng = gather.n

    def body(*refs):
        parts = (refs[:ng], refs[ng:2 * ng], refs[2 * ng], refs[2 * ng + 1])
        gather.start(*parts)
        gather.finish(*parts)

    outs = pl.pallas_call(
        body, name="gather_weights", in_specs=gather.in_specs, out_specs=gather.out_specs, out_shape=gather.out_shape,
        scratch_shapes=gather.scratch, compiler_params=pltpu.CompilerParams(has_side_effects=True),
    )(*groups)
    return gather.place(outs)


def _swap_halves(name, slots):
    n = len(slots)

    def body(*refs):
        ins, outs = refs[:n], refs[n:2 * n]
        send, recv = refs[2 * n:]
        x, y, c, _ = _place()
        cps = []
        for i, (_, row0, rows) in enumerate(slots):
            half = rows // 2
            src = ins[i].at[:, pl.ds(pl.multiple_of(row0 + (1 - c) * half, 16), half)]
            cps.append(pltpu.make_async_remote_copy(src_ref=src, dst_ref=outs[i], send_sem=send.at[i],
                                                    recv_sem=recv.at[i], device_id=(x, y, 1 - c), device_id_type=MESH))
        for cp in cps:
            cp.start()
        for cp in cps:
            cp.wait()

    return pl.pallas_call(
        body, name=name, in_specs=[ANY] * n, out_specs=[ANY] * n,
        out_shape=[jax.ShapeDtypeStruct((NSH, rows // 2, buf.shape[2]), buf.dtype) for buf, _, rows in slots],
        scratch_shapes=[pltpu.SemaphoreType.DMA((n,)), pltpu.SemaphoreType.DMA((n,))],
        compiler_params=pltpu.CompilerParams(has_side_effects=True),
    )(*[buf for buf, _, _ in slots])


class _Send(_Rider):
    def __init__(self, arrays):
        self.groups = list(arrays)
        self.n = len(self.groups)
        self.in_specs = [ANY] * self.n
        self.out_specs = [ANY] * self.n
        self.out_shape = [jax.ShapeDtypeStruct((3,) + g.shape[1:], g.dtype) for g in self.groups]
        self.scratch = [pltpu.SemaphoreType.DMA((3 * self.n,)), pltpu.SemaphoreType.DMA((3 * self.n,))]

    def _copies(self, ins, outs, send, recv):
        x, y, c, chips = _place()
        return [pltpu.make_async_remote_copy(src_ref=ins[gi].at[2 * chip[0] + chip[1]], dst_ref=outs[gi].at[j],
                                             send_sem=send.at[3 * gi + j], recv_sem=recv.at[3 * gi + j],
                                             device_id=(*chip, c), device_id_type=MESH)
                for gi in range(self.n) for j, chip in enumerate(chips)]

    def start(self, ins, outs, send, recv):
        for cp in self._copies(ins, outs, send, recv):
            cp.start()

    def finish(self, ins, outs, send, recv):
        for cp in self._copies(ins, outs, send, recv):
            cp.wait()

    def place(self, outs):
        return list(outs)


def _send_to_owners(name, arrays):
    rider = _Send(arrays)
    n = rider.n

    def body(*refs):
        parts = (refs[:n], refs[n:2 * n], refs[2 * n], refs[2 * n + 1])
        rider.start(*parts)
        rider.finish(*parts)

    return pl.pallas_call(
        body, name=name, in_specs=rider.in_specs, out_specs=rider.out_specs, out_shape=rider.out_shape,
        scratch_shapes=rider.scratch, compiler_params=pltpu.CompilerParams(has_side_effects=True),
    )(*arrays)


def _share_halves(groups):
    ng = len(groups)

    def body(*refs):
        ins, outs = refs[:ng], refs[ng:2 * ng]
        send, recv = refs[2 * ng:]
        x, y, c, _ = _place()
        cps = [pltpu.make_async_remote_copy(src_ref=ins[gi], dst_ref=outs[gi].at[c], send_sem=send.at[gi],
                                            recv_sem=recv.at[gi], device_id=(x, y, 1 - c), device_id_type=MESH)
               for gi in range(ng)]
        for cp in cps:
            cp.start()
        for gi in range(ng):
            pltpu.make_async_remote_copy(src_ref=ins[gi], dst_ref=outs[gi].at[1 - c], send_sem=send.at[gi],
                                         recv_sem=recv.at[gi], device_id=(x, y, 1 - c), device_id_type=MESH).wait_recv()
        for cp in cps:
            cp.wait_send()

    outs = pl.pallas_call(
        body, name="grad_share_halves", in_specs=[ANY] * ng, out_specs=[ANY] * ng,
        out_shape=[jax.ShapeDtypeStruct((2,) + g.shape, g.dtype) for g in groups],
        scratch_shapes=[pltpu.SemaphoreType.DMA((ng,)), pltpu.SemaphoreType.DMA((ng,))],
        compiler_params=pltpu.CompilerParams(has_side_effects=True),
    )(*groups)
    c = lax.axis_index("c")
    return [lax.dynamic_update_index_in_dim(o, g, c, 0) for o, g in zip(outs, groups)]


def _pair_sum(name, slots, got, c_idx):
    n = len(slots)
    in_specs, out_specs, out_shape, operands = [], [], [], []
    for (buf, row0, rows), g in zip(slots, got):
        hb, cols = rows // 4, buf.shape[2]
        in_specs += [pl.BlockSpec((None, hb, cols), lambda q, i, cr, r=row0 // hb: (q, r + 2 * cr[0] + i, 0)),
                     pl.BlockSpec((None, hb, cols), lambda q, i, cr: (q, i, 0))]
        out_specs.append(pl.BlockSpec((None, hb, cols), lambda q, i, cr: (q, i, 0)))
        out_shape.append(jax.ShapeDtypeStruct(g.shape, BF16))
        operands += [buf, g]

    def body(c_ref, *refs):
        for i in range(n):
            refs[2 * n + i][...] = (refs[2 * i][...].astype(F32) + refs[2 * i + 1][...].astype(F32)).astype(BF16)

    return pl.pallas_call(
        body, name=name,
        grid_spec=pltpu.PrefetchScalarGridSpec(num_scalar_prefetch=1, grid=(NSH, 2), in_specs=in_specs, out_specs=out_specs),
        out_shape=out_shape, compiler_params=_params(("parallel", "parallel")),
    )(c_idx, *operands)


def _owner_sum(name, pairs, got, p_idx):
    n = len(pairs)
    in_specs, out_specs, out_shape, operands = [], [], [], []
    for own, g in zip(pairs, got):
        _, rows, cols = own.shape
        hb = rows // 2
        in_specs += [pl.BlockSpec((None, hb, cols), lambda i, pr: (pr[0], i, 0)),
                     pl.BlockSpec((3, hb, cols), lambda i, pr: (0, i, 0))]
        out_specs.append(pl.BlockSpec((hb, cols), lambda i, pr: (i, 0)))
        out_shape.append(jax.ShapeDtypeStruct((rows, cols), F32))
        operands += [own, g]

    def body(p_ref, *refs):
        for i in range(n):
            a_ref, b_ref = refs[2 * i], refs[2 * i + 1]
            refs[2 * n + i][...] = ((a_ref[...].astype(F32) + b_ref[0].astype(F32)) + b_ref[1].astype(F32)) + b_ref[2].astype(F32)

    return pl.pallas_call(
        body, name=name,
        grid_spec=pltpu.PrefetchScalarGridSpec(num_scalar_prefetch=1, grid=(2,), in_specs=in_specs, out_specs=out_specs),
        out_shape=out_shape, compiler_params=_params(("parallel",)),
    )(p_idx, *operands)


def _sum_small(slab):
    def body(in_ref, out_ref, all_ref, send, recv):
        x, y, c, _ = _place()
        me = 4 * x + 2 * y + c
        all_ref[me] = in_ref[...]
        cps = []
        for k in range(1, 8):
            peer = (x ^ (k >> 2), y ^ ((k >> 1) & 1), c ^ (k & 1))
            cps.append(pltpu.make_async_remote_copy(src_ref=in_ref, dst_ref=all_ref.at[me], send_sem=send.at[k - 1],
                                                    recv_sem=recv.at[k - 1], device_id=peer, device_id_type=MESH))
        for cp in cps:
            cp.start()
        for cp in cps:
            cp.wait()
        total = all_ref[0]
        for d in range(1, 8):
            total = total + all_ref[d]
        out_ref[...] = total

    return pl.pallas_call(
        body, name="sum_small",
        in_specs=[pl.BlockSpec(memory_space=pltpu.VMEM)], out_specs=pl.BlockSpec(memory_space=pltpu.VMEM),
        out_shape=jax.ShapeDtypeStruct(slab.shape, F32),
        scratch_shapes=[pltpu.VMEM((8,) + slab.shape, F32), pltpu.SemaphoreType.DMA((7,)), pltpu.SemaphoreType.DMA((7,))],
        compiler_params=pltpu.CompilerParams(has_side_effects=True),
    )(slab)


FFNS = ("pre0", "post0", "pre1", "post1")
W_SHAPES = dict({f + "_gu": (NSH, 2, D, FS) for f in FFNS}, **{f + "_d": (NSH, FS, D) for f in FFNS},
                ab_in=(NSH, D, 768), ab_out=(NSH, 256, D), conv=(NSH, 2, 8, LANE), c_in=(NSH, D, D), c_out=(NSH, 256, D))
CARRIED = dict(pre0_up=("pre0_d",), pre0_down=("ab_in", "ab_out", "conv"),
               sb_fwd=("post0_gu", "post0_d", "pre1_gu", "pre1_d"), post0_up=("c_in", "c_out"),
               hgrn_fwd=("post1_gu", "post1_d"))


FFN_SLOTS = dict(pre0=(0, 2, 0), pre1=(1, 3, 1), post0=(4, 6, 2), post1=(5, 7, 3))
GRAD_SLOTS = dict(ab_out=("b", B_ABOUT, 256), c_in=("b", B_CIN, D), c_out=("b", B_COUT, 256), ab_in=("c", 0, D))
for _f, (_g, _u, _d) in FFN_SLOTS.items():
    GRAD_SLOTS.update({_f + "_g": ("a", _g * D, D), _f + "_u": ("a", _u * D, D), _f + "_d": ("b", _d * FS, FS)})
REDUCE_STAGES = dict(x=("post1_g", "post1_u", "post1_d", "c_out"),
                     y=("c_in", "pre1_g", "pre1_u", "pre1_d", "post0_g", "post0_u", "post0_d", "ab_out"),
                     z=("ab_in", "pre0_g", "pre0_u", "pre0_d"))


def _local_step(x, target, weights, small, shards=None, place=None):
    t = x.shape[0]
    tm = min(MM_TILE, t)
    tw = min(WGRAD_TILE, t)
    nt = t // tm
    tok_si = _spec((tm, D), lambda s, i: (i, 0))
    w = dict(weights)
    reduced = {}

    def reduce_start(stage, buffers):
        if place is None:
            return (), [], None
        names = REDUCE_STAGES[stage]
        slots = [(buffers[GRAD_SLOTS[n][0]],) + GRAD_SLOTS[n][1:] for n in names]
        got = _swap_halves("grad_swap_" + stage, slots)
        pairs = _pair_sum("grad_pair_sum_" + stage, slots, got, place[0])
        return names, pairs, _Send(pairs)

    def reduce_end(stage, names, pairs, landed):
        if place is not None:
            reduced.update(zip(names, _owner_sum("grad_owner_sum_" + stage, pairs, landed, place[1])))

    def carried(kernel_name):
        names = [n for n in CARRIED[kernel_name] if n not in w]
        return names, (_Gather([shards[n] for n in names]) if names else None)

    def land(names, arrays):
        for n, a in zip(names, arrays):
            w[n] = a.reshape(W_SHAPES[n])

    def ffn_forward(tag, h, norm_row):
        hn = _norm_fwd(tag + "_norm", h, small, norm_row)
        names, gather = carried(tag + "_up") if tag + "_up" in CARRIED else ([], None)
        s_up, s_gate, a, got = _ffn_up(tag + "_up", hn, w[tag + "_gu"], gather)
        land(names, got)
        names, gather = carried(tag + "_down") if tag + "_down" in CARRIED else ([], None)
        out = _ffn_down(tag + "_down", a, w[tag + "_d"], h, gather)
        if gather is not None:
            out, got = out
            land(names, got)
        return out, (h, hn, s_up, s_gate, a)

    def out_proj(name, y, w_out, h):
        return _mm(name, [(y, _spec((tm, 256), lambda i, k: (i, k)), w_out, _spec((None, 256, D), lambda i, k: (k, 0, 0)))],
                   grid=(nt, NSH), o_shape=(t, D), o_dtype=F32, o_spec=_spec((tm, D), lambda i, k: (i, 0)),
                   dims=NN, kaxis=1, nk=NSH, acc_shape=(tm, D), res=(h, _spec((tm, D), lambda i, k: (i, 0))))

    def out_proj_bwd(tag, dhb, y, w_out, blk, grad_b):
        dy = _mm(tag + "_dy", [(dhb, tok_si, w_out, _spec((None, 256, D), lambda s, i: (s, 0, 0)))],
                 grid=(NSH, nt), o_shape=(t, D), o_dtype=F32, o_spec=_spec((tm, 256), lambda s, i: (i, s)),
                 dims=NT, kaxis=1, nk=1)
        grad_b = _wgrad(tag + "_dwout", y, _spec((tw, 256), lambda s, k: (k, s)), dhb, _spec((tw, D), lambda s, k: (k, 0)),
                        256, D, t, tw, grad_b, blk)
        return dy, grad_b

    h0 = x
    h1, pre0 = ffn_forward("pre0", h0, R_PRE)
    hn_ab = _norm_fwd("mix0_norm", h1, small, R_MIX)
    proj_a = _mm("ab_proj_a", [(hn_ab, tok_si, w["ab_in"], _spec((None, D, 768), lambda s, i: (s, 0, 0)))],
                 grid=(2, nt), o_shape=(t, 1536), o_dtype=F32, o_spec=_spec((tm, 768), lambda s, i: (i, s)),
                 dims=NN, kaxis=1, nk=1)
    proj_b = _mm("ab_proj_b", [(hn_ab, tok_si, w["ab_in"], _spec((None, D, 768), lambda s, i: (s + 2, 0, 0)))],
                 grid=(2, nt), o_shape=(t, 1536), o_dtype=BF16, o_spec=_spec((tm, 768), lambda s, i: (i, s)),
                 dims=NN, kaxis=1, nk=1)
    y_a = _conv_fwd(proj_a, w["conv"])
    names, gather = carried("sb_fwd")
    y_b, ltot, nblk, got = _sb_fwd(proj_b, gather)
    land(names, got)
    y_ab = jnp.concatenate([y_a, y_b], axis=1)
    h2 = out_proj("ab_out", y_ab, w["ab_out"], h1)
    h3, post0 = ffn_forward("post0", h2, R_POST)
    h4, pre1 = ffn_forward("pre1", h3, R_PRE + 1)
    hn_c = _norm_fwd("mix1_norm", h4, small, R_MIX + 1)
    proj_c = _mm("c_proj", [(hn_c, tok_si, w["c_in"], _spec((None, D, D), lambda s, i: (s, 0, 0)))],
                 grid=(NSH, nt), o_shape=(NSH, t, D), o_dtype=F32, o_spec=_spec((None, tm, D), lambda s, i: (s, i, 0)),
                 dims=NN, kaxis=1, nk=1)
    names, gather = carried("hgrn_fwd")
    o_c, y_c, sst, got = _hgrn_fwd(proj_c, small, gather)
    land(names, got)
    h5 = out_proj("c_out", y_c, w["c_out"], h4)
    h6, post1 = ffn_forward("post1", h5, R_POST + 1)
    dh, dhb, d_fin, loss = _final_loss(h6, small, target)

    dh, dhb, grad_a, grad_b, dn_post1 = _ffn_backward("post1", dh, dhb, *post1, w["post1_gu"], w["post1_d"],
                                                      *FFN_SLOTS["post1"], small, R_POST + 1, 8 * D, B_ROWS)
    dy_c, grad_b = out_proj_bwd("c", dhb, y_c, w["c_out"], B_COUT // 256, grad_b)
    names, pairs, rider = reduce_start("x", dict(a=grad_a, b=grad_b))
    dproj_c, d_clb, d_gam, landed = _hgrn_bwd(proj_c, small, o_c, sst, dy_c, rider)
    reduce_end("x", names, pairs, landed)
    grad_b = _wgrad("c_dwin", hn_c, _spec((tw, D), lambda s, k: (k, 0)), dproj_c, _spec((None, tw, D), lambda s, k: (s, k, 0)),
                    D, D, t, tw, grad_b, B_CIN // D)
    dhn = _mm("c_dhn", [(dproj_c, _spec((None, tm, D), lambda i, k: (k, i, 0)),
                         w["c_in"], _spec((None, D, D), lambda i, k: (k, 0, 0)))],
              grid=(nt, NSH), o_shape=(t, D), o_dtype=F32, o_spec=_spec((tm, D), lambda i, k: (i, 0)),
              dims=NT, kaxis=1, nk=NSH, acc_shape=(tm, D))
    dh, dhb, dn_mix1 = _norm_bwd("mix1_norm_bwd", dhn, h4, small, R_MIX + 1, dh)
    dh, dhb, grad_a, grad_b, dn_pre1 = _ffn_backward("pre1", dh, dhb, *pre1, w["pre1_gu"], w["pre1_d"],
                                                     *FFN_SLOTS["pre1"], small, R_PRE + 1, grad_a, grad_b)
    dh, dhb, grad_a, grad_b, dn_post0 = _ffn_backward("post0", dh, dhb, *post0, w["post0_gu"], w["post0_d"],
                                                      *FFN_SLOTS["post0"], small, R_POST, grad_a, grad_b)
    dy_ab, grad_b = out_proj_bwd("ab", dhb, y_ab, w["ab_out"], B_ABOUT // 256, grad_b)
    dab, dac, dax, d_conv = _conv_bwd(proj_a, w["conv"], dy_ab)
    names, pairs, rider = reduce_start("y", dict(a=grad_a, b=grad_b))
    dq, dk, dv, landed = _sb_bwd(proj_b, dy_ab, ltot, nblk, rider)
    reduce_end("y", names, pairs, landed)
    dproj_ab = jnp.concatenate([dab, dac, dax, dq, dk, dv], axis=1)
    grad_c = _wgrad("ab_dwin", hn_ab, _spec((tw, D), lambda s, k: (k, 0)), dproj_ab, _spec((tw, 768), lambda s, k: (k, s)),
                    D, 768, t, tw, D, 0)
    dhn = _mm("ab_dhn", [(dproj_ab, _spec((tm, 768), lambda i, k: (i, k)),
                          w["ab_in"], _spec((None, D, 768), lambda i, k: (k, 0, 0)))],
              grid=(nt, NSH), o_shape=(t, D), o_dtype=F32, o_spec=_spec((tm, D), lambda i, k: (i, 0)),
              dims=NT, kaxis=1, nk=NSH, acc_shape=(tm, D))
    dh, dhb, dn_mix0 = _norm_bwd("mix0_norm_bwd", dhn, h1, small, R_MIX, dh)
    dh, dhb, grad_a, grad_b, dn_pre0 = _ffn_backward("pre0", dh, dhb, *pre0, w["pre0_gu"], w["pre0_d"],
                                                     *FFN_SLOTS["pre0"], small, R_PRE, grad_a, grad_b)
    names, pairs, rider = reduce_start("z", dict(a=grad_a, b=grad_b, c=grad_c))
    if rider is not None:
        reduce_end("z", names, pairs, _send_to_owners("grad_send_z", pairs))
    zero = jnp.zeros((1, D), F32)
    conv_rows = jnp.pad(jnp.transpose(d_conv[:, :3, :], (1, 0, 2)).reshape(3, 512), ((0, 0), (0, D - 512)))
    small_grad = jnp.concatenate([
        dn_pre0, dn_pre1, dn_mix0, dn_mix1, dn_post0, dn_post1, d_clb, d_fin,
        jnp.pad(d_gam, ((0, 0), (0, D - HD))), conv_rows,
        jnp.pad(loss, ((0, 0), (0, D - 1))), zero, zero], axis=0)
    return dh, grad_a, grad_b, grad_c, small_grad, reduced


def _small_slab(rows):
    parts = [jnp.pad(r.astype(F32), ((0, 0), (0, D - r.shape[1]))) for r in rows]
    slab = jnp.concatenate(parts, axis=0)
    return jnp.pad(slab, ((0, SMALL_ROWS - slab.shape[0]), (0, 0)))


def kernel(x, ffn_pre_norm, ffn_pre_w_gate, ffn_pre_w_up, ffn_pre_w_down, mix_norm, ffn_post_norm, ffn_post_w_gate, ffn_post_w_up, ffn_post_w_down, ab_w_in, ab_conv_w, ab_w_out, c_w_in, c_lower_bounds, c_out_norm, c_w_out, final_norm, loss_target, m_ffn_pre_norm, m_ffn_pre_w_gate, m_ffn_pre_w_up, m_ffn_pre_w_down, m_mix_norm, m_ffn_post_norm, m_ffn_post_w_gate, m_ffn_post_w_up, m_ffn_post_w_down, m_ab_w_in, m_ab_conv_w, m_ab_w_out, m_c_w_in, m_c_lower_bounds, m_c_out_norm, m_c_w_out, m_final_norm, v_ffn_pre_norm, v_ffn_pre_w_gate, v_ffn_pre_w_up, v_ffn_pre_w_down, v_mix_norm, v_ffn_post_norm, v_ffn_post_w_gate, v_ffn_post_w_up, v_ffn_post_w_down, v_ab_w_in, v_ab_conv_w, v_ab_w_out, v_c_w_in, v_c_lower_bounds, v_c_out_norm, v_c_w_out, v_final_norm):
    t = x.shape[1]
    xi, yi, ci = lax.axis_index("x"), lax.axis_index("y"), lax.axis_index("c")
    p_idx = (2 * xi + yi).astype(jnp.int32).reshape(1)
    c_idx = ci.astype(jnp.int32).reshape(1)

    def halves(m):
        return m.astype(BF16).reshape(2, m.shape[0] // 2, m.shape[1])

    shards = {}
    for name, (w_gate, w_up, w_down, layer) in dict(
            pre0=(ffn_pre_w_gate, ffn_pre_w_up, ffn_pre_w_down, 0), post0=(ffn_post_w_gate, ffn_post_w_up, ffn_post_w_down, 0),
            pre1=(ffn_pre_w_gate, ffn_pre_w_up, ffn_pre_w_down, 1), post1=(ffn_post_w_gate, ffn_post_w_up, ffn_post_w_down, 1)).items():
        shards[name + "_gu"] = jnp.stack([w_gate[layer], w_up[layer]]).astype(BF16)
        shards[name + "_d"] = halves(w_down[layer])
    conv_pad = jnp.pad(ab_conv_w[0], ((0, 5), (0, 0)))
    shards.update(ab_in=halves(ab_w_in[0]), ab_out=halves(ab_w_out[0]), c_in=halves(c_w_in[0]), c_out=halves(c_w_out[0]),
                  conv=jnp.stack([conv_pad, jnp.zeros_like(conv_pad)]))
    first = _gather_weights([shards["pre0_gu"]])[0].reshape(W_SHAPES["pre0_gu"])

    small = _small_slab([ffn_pre_norm, mix_norm, ffn_post_norm, c_lower_bounds, final_norm.reshape(1, D), c_out_norm])
    small = small.reshape(SMALL_ROWS, 1, D)

    grad_x, _, _, _, small_grad, reduced = _local_step(x[0], loss_target[0], dict(pre0_gu=first), small, shards,
                                                        (c_idx, p_idx))
    order = list(reduced)
    whole = {n: g.reshape(2 * g.shape[1], g.shape[2]) for n, g in zip(order, _share_halves([reduced[n] for n in order]))}
    small_sum = _sum_small(small_grad)

    my_conv = lax.dynamic_slice(small_sum[R_CONV:R_CONV + 3], (0, (2 * xi + yi) * 128), (3, 128))
    grads = {
        "ffn_pre_norm": small_sum[R_PRE:R_PRE + 2], "mix_norm": small_sum[R_MIX:R_MIX + 2],
        "ffn_post_norm": small_sum[R_POST:R_POST + 2], "c_lower_bounds": small_sum[R_CLB:R_CLB + 2],
        "c_out_norm": small_sum[R_GAM:R_GAM + 1, :HD], "final_norm": small_sum[R_FIN],
        "ab_conv_w": my_conv.reshape(1, 3, 128),
        "ab_w_in": whole["ab_in"][None], "ab_w_out": whole["ab_out"][None],
        "c_w_in": whole["c_in"][None], "c_w_out": whole["c_out"][None],
    }
    for kind, key in (("gate", "_g"), ("up", "_u"), ("down", "_d")):
        grads["ffn_pre_w_" + kind] = jnp.stack([whole["pre0" + key], whole["pre1" + key]])
        grads["ffn_post_w_" + kind] = jnp.stack([whole["post0" + key], whole["post1" + key]])
    weights = dict(ffn_pre_norm=ffn_pre_norm, ffn_pre_w_gate=ffn_pre_w_gate, ffn_pre_w_up=ffn_pre_w_up, ffn_pre_w_down=ffn_pre_w_down, mix_norm=mix_norm, ffn_post_norm=ffn_post_norm, ffn_post_w_gate=ffn_post_w_gate, ffn_post_w_up=ffn_post_w_up, ffn_post_w_down=ffn_post_w_down, ab_w_in=ab_w_in, ab_conv_w=ab_conv_w, ab_w_out=ab_w_out, c_w_in=c_w_in, c_lower_bounds=c_lower_bounds, c_out_norm=c_out_norm, c_w_out=c_w_out, final_norm=final_norm)
    m_in = dict(ffn_pre_norm=m_ffn_pre_norm, ffn_pre_w_gate=m_ffn_pre_w_gate, ffn_pre_w_up=m_ffn_pre_w_up, ffn_pre_w_down=m_ffn_pre_w_down, mix_norm=m_mix_norm, ffn_post_norm=m_ffn_post_norm, ffn_post_w_gate=m_ffn_post_w_gate, ffn_post_w_up=m_ffn_post_w_up, ffn_post_w_down=m_ffn_post_w_down, ab_w_in=m_ab_w_in, ab_conv_w=m_ab_conv_w, ab_w_out=m_ab_w_out, c_w_in=m_c_w_in, c_lower_bounds=m_c_lower_bounds, c_out_norm=m_c_out_norm, c_w_out=m_c_w_out, final_norm=m_final_norm)
    v_in = dict(ffn_pre_norm=v_ffn_pre_norm, ffn_pre_w_gate=v_ffn_pre_w_gate, ffn_pre_w_up=v_ffn_pre_w_up, ffn_pre_w_down=v_ffn_pre_w_down, mix_norm=v_mix_norm, ffn_post_norm=v_ffn_post_norm, ffn_post_w_gate=v_ffn_post_w_gate, ffn_post_w_up=v_ffn_post_w_up, ffn_post_w_down=v_ffn_post_w_down, ab_w_in=v_ab_w_in, ab_conv_w=v_ab_conv_w, ab_w_out=v_ab_w_out, c_w_in=v_c_w_in, c_lower_bounds=v_c_lower_bounds, c_out_norm=v_c_out_norm, c_w_out=v_c_w_out, final_norm=v_final_norm)
    names = list(weights)
    big = [n for n in names if weights[n].size >= 65536]
    tiny = [n for n in names if n not in big]

    delta, new_m, new_v = {}, {}, {}
    for n in big:
        shape = weights[n].shape
        two_d = (shape[0] * shape[1], shape[2])
        d, m2, v2 = _adamw("adamw_" + n, weights[n].reshape(two_d), grads[n].reshape(two_d),
                           m_in[n].reshape(two_d), v_in[n].reshape(two_d))
        delta[n], new_m[n], new_v[n] = d.reshape(shape), m2.reshape(shape), v2.reshape(shape)

    def tiny_slab(src):
        return _small_slab([src[n].reshape(-1, src[n].shape[-1]) for n in tiny])

    offs, row = {}, 0
    for n in tiny:
        nrows = weights[n].size // weights[n].shape[-1]
        offs[n] = (row, nrows)
        row += nrows
    d, m2, v2 = _adamw("adamw_small", tiny_slab(weights), tiny_slab(grads), tiny_slab(m_in), tiny_slab(v_in))
    for n in tiny:
        r0, nr = offs[n]
        shape = weights[n].shape
        for dst, src in ((delta, d), (new_m, m2), (new_v, v2)):
            dst[n] = src[r0:r0 + nr, :shape[-1]].reshape(shape)

    loss = small_sum[R_LOSS, 0]
    return (loss, grad_x.reshape(1, t, D), *[grads[n] for n in names], *[delta[n] for n in names],
            *[new_m[n] for n in names], *[new_v[n] for n in names])
```

```python
import functools
import math

import jax
import jax.numpy as jnp
from jax import lax
from jax.experimental import pallas as pl
from jax.experimental.pallas import tpu as pltpu

F32 = jnp.float32
BF16 = jnp.bfloat16
MESH = pl.DeviceIdType.MESH
ANY = pl.BlockSpec(memory_space=pl.ANY)

D = 1024
FS = 704
NSH = 4
RMS_EPS = 1e-6
MACARON = 0.5
CHUNK = 64
HD = 128
HGRN_HPS = 4
SBQ = 128
SB_PAIRS = 2
SB_DEAD = -105.0
CONV_HALO = 8
LANE = 128
ROW_TILE = 512
MM_TILE = 1024
WGRAD_TILE = 4096
VMEM_LIMIT = 48 * 1024 * 1024

ADAM_LR, ADAM_B1, ADAM_B2, ADAM_EPS, ADAM_WD, ADAM_STEP = 0.001, 0.9, 0.999, 1e-08, 0.01, 10

NN = ((1,), (0,))
NT = ((1,), (1,))
TN = ((0,), (0,))

SMALL_ROWS = 16
R_PRE, R_MIX, R_POST, R_CLB, R_FIN, R_GAM, R_CONV, R_LOSS = 0, 2, 4, 6, 8, 9, 10, 13

B_DOWN = 0
B_ABOUT = 4 * FS
B_CIN = B_ABOUT + 256
B_COUT = B_CIN + 1024
B_ROWS = B_COUT + 256


def _dg(a, b, dims):
    return lax.dot_general(a, b, (dims, ((), ())), preferred_element_type=F32)


def _split(x):
    hi = x.astype(BF16)
    lo = (x - hi.astype(F32)).astype(BF16)
    return hi, lo


def _dot3(a, b, dims):
    ah, al = _split(a)
    bh, bl = _split(b)
    return _dg(ah, bh, dims) + _dg(ah, bl, dims) + _dg(al, bh, dims)


def _sigmoid(x):
    return 1.0 / (1.0 + jnp.exp(-x))


def _params(sem):
    return pltpu.CompilerParams(dimension_semantics=sem, vmem_limit_bytes=VMEM_LIMIT)


def _spec(shape, imap):
    return pl.BlockSpec(shape, imap)


def _mm(name, pairs, *, grid, o_shape, o_dtype, o_spec, dims, kaxis, nk, acc_shape=None, res=None, scale=None,
        into=None, carry=None):
    npairs = len(pairs)
    operands, specs = [], []
    for a, a_spec, b, b_spec in pairs:
        operands += [a, b]
        specs += [a_spec, b_spec]
    if res is not None:
        operands.append(res[0])
        specs.append(res[1])
    aliases = {}
    if into is not None:
        aliases = {len(operands): 0}
        operands.append(into)
        specs.append(ANY)
    n_own = len(operands)
    nc = carry.n if carry is not None else 0
    if nc:
        operands += carry.groups
        specs += carry.in_specs

    def body(*refs):
        o_ref = refs[n_own + nc]
        riders = (refs[n_own:n_own + nc], refs[n_own + nc + 1:n_own + 2 * nc + 1], refs[-2], refs[-1]) if nc else None
        if nc:
            carry.ride(grid, riders, "start")
        part = None
        for n in range(npairs):
            d = _dg(refs[2 * n][...], refs[2 * n + 1][...], dims)
            part = d if part is None else part + d

        def finish(val):
            if scale is not None:
                val = val * scale
            if res is not None:
                val = val + refs[2 * npairs][...]
            o_ref[...] = val.astype(o_dtype)

        if nk == 1:
            finish(part)
        else:
            acc_ref = refs[n_own + 2 * nc + 1]
            k = pl.program_id(kaxis)

            @pl.when(k == 0)
            def _():
                acc_ref[...] = part

            @pl.when(k > 0)
            def _():
                acc_ref[...] += part

            @pl.when(k == nk - 1)
            def _():
                finish(acc_ref[...])
        if nc:
            carry.ride(grid, riders, "finish")

    sem = tuple("arbitrary" if (nc or (ax == kaxis and nk > 1)) else "parallel" for ax in range(len(grid)))
    outs = pl.pallas_call(
        body, name=name, grid=grid, in_specs=specs, out_specs=[o_spec] + (carry.out_specs if nc else []),
        out_shape=[jax.ShapeDtypeStruct(o_shape, o_dtype)] + (carry.out_shape if nc else []),
        scratch_shapes=([pltpu.VMEM(acc_shape, F32)] if nk > 1 else []) + (carry.scratch if nc else []),
        input_output_aliases=aliases,
        compiler_params=_params(sem),
    )(*operands)
    return (outs[0], carry.place(outs[1:])) if nc else outs[0]


def _norm_fwd(name, h, gain_slab, row):
    t = h.shape[0]
    tm = min(ROW_TILE, t)

    def body(h_ref, g_ref, o_ref):
        x = h_ref[...]
        r = lax.rsqrt(jnp.mean(x * x, axis=-1, keepdims=True) + RMS_EPS)
        o_ref[...] = (x * r * g_ref[...]).astype(BF16)

    return pl.pallas_call(
        body, name=name, grid=(t // tm,),
        in_specs=[_spec((tm, D), lambda i: (i, 0)), _spec((None, 1, D), lambda i: (row, 0, 0))],
        out_specs=_spec((tm, D), lambda i: (i, 0)),
        out_shape=jax.ShapeDtypeStruct((t, D), BF16),
        compiler_params=_params(("parallel",)),
    )(h, gain_slab)


def _norm_bwd(name, dhn, h, gain_slab, row, dres):
    t = h.shape[0]
    tm = min(ROW_TILE, t)
    nt = t // tm

    def body(dhn_ref, h_ref, g_ref, dres_ref, dh_ref, dhb_ref, dg_ref, acc_ref):
        i = pl.program_id(0)
        x = h_ref[...]
        r = lax.rsqrt(jnp.mean(x * x, axis=-1, keepdims=True) + RMS_EPS)
        xh = x * r
        dy = dhn_ref[...]
        gdy = dy * g_ref[...]
        dx = (gdy - xh * jnp.mean(gdy * xh, axis=-1, keepdims=True)) * r
        dh = dres_ref[...] + dx
        dh_ref[...] = dh
        dhb_ref[...] = dh.astype(BF16)
        part = jnp.sum((dy * xh).reshape(tm // 8, 8, D), axis=0)

        @pl.when(i == 0)
        def _():
            acc_ref[...] = part

        @pl.when(i > 0)
        def _():
            acc_ref[...] += part

        @pl.when(i == nt - 1)
        def _():
            dg_ref[...] = jnp.sum(acc_ref[...], axis=0, keepdims=True)

    row_spec = _spec((tm, D), lambda i: (i, 0))
    return pl.pallas_call(
        body, name=name, grid=(nt,),
        in_specs=[row_spec, row_spec, _spec((None, 1, D), lambda i: (row, 0, 0)), row_spec],
        out_specs=[row_spec, row_spec, _spec((1, D), lambda i: (0, 0))],
        out_shape=[jax.ShapeDtypeStruct((t, D), F32), jax.ShapeDtypeStruct((t, D), BF16),
                   jax.ShapeDtypeStruct((1, D), F32)],
        scratch_shapes=[pltpu.VMEM((8, D), F32)],
        compiler_params=_params(("arbitrary",)),
    )(dhn, h, gain_slab, dres)


def _final_loss(h, gain_slab, target):
    t = h.shape[0]
    tm = min(ROW_TILE, t)
    nt = t // tm

    def body(h_ref, g_ref, t_ref, dh_ref, dhb_ref, dg_ref, loss_ref, acc_ref, lacc_ref):
        i = pl.program_id(0)
        x = h_ref[...]
        g = g_ref[...]
        r = lax.rsqrt(jnp.mean(x * x, axis=-1, keepdims=True) + RMS_EPS)
        xh = x * r
        err = xh * g - t_ref[...]
        dy = err * (1.0 / D)
        gdy = dy * g
        dh = (gdy - xh * jnp.mean(gdy * xh, axis=-1, keepdims=True)) * r
        dh_ref[...] = dh
        dhb_ref[...] = dh.astype(BF16)
        part = jnp.sum((dy * xh).reshape(tm // 8, 8, D), axis=0)
        lpart = jnp.sum((err * err).reshape(tm // 8, 8, D), axis=0)

        @pl.when(i == 0)
        def _():
            acc_ref[...] = part
            lacc_ref[...] = lpart

        @pl.when(i > 0)
        def _():
            acc_ref[...] += part
            lacc_ref[...] += lpart

        @pl.when(i == nt - 1)
        def _():
            dg_ref[...] = jnp.sum(acc_ref[...], axis=0, keepdims=True)
            rows = jnp.sum(lacc_ref[...], axis=0, keepdims=True)
            loss_ref[...] = jnp.sum(rows, axis=1, keepdims=True) * (0.5 / D)

    row_spec = _spec((tm, D), lambda i: (i, 0))
    return pl.pallas_call(
        body, name="final_loss", grid=(nt,),
        in_specs=[row_spec, _spec((None, 1, D), lambda i: (R_FIN, 0, 0)), row_spec],
        out_specs=[row_spec, row_spec, _spec((1, D), lambda i: (0, 0)), _spec((1, 1), lambda i: (0, 0))],
        out_shape=[jax.ShapeDtypeStruct((t, D), F32), jax.ShapeDtypeStruct((t, D), BF16),
                   jax.ShapeDtypeStruct((1, D), F32), jax.ShapeDtypeStruct((1, 1), F32)],
        scratch_shapes=[pltpu.VMEM((8, D), F32), pltpu.VMEM((8, D), F32)],
        compiler_params=_params(("arbitrary",)),
    )(h, gain_slab, target)


def _ffn_up(name, hn, wgu, carry=None):
    t = hn.shape[0]
    tm = min(MM_TILE, t)
    grid = (NSH, t // tm)
    nc = carry.n if carry is not None else 0

    def body(*refs):
        x_ref, wg_ref, wu_ref = refs[:3]
        s_up_ref, s_gate_ref, a_ref = refs[3 + nc:6 + nc]
        riders = (refs[3:3 + nc], refs[6 + nc:6 + 2 * nc], refs[-2], refs[-1]) if nc else None
        if nc:
            carry.ride(grid, riders, "start")
        x = x_ref[...]
        g = _dg(x, wg_ref[...], NN)
        u = _dg(x, wu_ref[...], NN)
        sg = _sigmoid(g)
        silu = g * sg
        s_up_ref[...] = (MACARON * silu).astype(BF16)
        s_gate_ref[...] = (MACARON * u * (sg * (1.0 + g * (1.0 - sg)))).astype(BF16)
        a_ref[...] = (silu * u).astype(BF16)
        if nc:
            carry.ride(grid, riders, "finish")

    act = _spec((None, tm, FS), lambda s, i: (s, i, 0))
    shape = jax.ShapeDtypeStruct((NSH, t, FS), BF16)
    outs = pl.pallas_call(
        body, name=name, grid=grid,
        in_specs=[_spec((tm, D), lambda s, i: (i, 0)),
                  _spec((None, None, D, FS), lambda s, i: (s, 0, 0, 0)),
                  _spec((None, None, D, FS), lambda s, i: (s, 1, 0, 0))] + (carry.in_specs if nc else []),
        out_specs=[act, act, act] + (carry.out_specs if nc else []),
        out_shape=[shape, shape, shape] + (carry.out_shape if nc else []),
        scratch_shapes=carry.scratch if nc else [],
        compiler_params=_params(("arbitrary", "arbitrary") if nc else ("parallel", "parallel")),
    )(hn, wgu, wgu, *(carry.groups if nc else []))
    return (outs[0], outs[1], outs[2], carry.place(outs[3:]) if nc else [])


def _ffn_down(name, a, wd, h, carry=None):
    t = h.shape[0]
    tm = min(MM_TILE, t)
    return _mm(name, [(a, _spec((None, tm, FS), lambda i, k: (k, i, 0)),
                       wd, _spec((None, FS, D), lambda i, k: (k, 0, 0)))],
               grid=(t // tm, NSH), o_shape=(t, D), o_dtype=F32, o_spec=_spec((tm, D), lambda i, k: (i, 0)),
               dims=NN, kaxis=1, nk=NSH, acc_shape=(tm, D), res=(h, _spec((tm, D), lambda i, k: (i, 0))),
               scale=MACARON, carry=carry)


def _ffn_bwd_up(name, dhb, wd, s_up, s_gate):
    t = dhb.shape[0]
    tm = min(MM_TILE, t)

    def body(dh_ref, wd_ref, s_up_ref, s_gate_ref, dg_ref, du_ref):
        da = _dg(dh_ref[...], wd_ref[...], NT)
        du_ref[...] = (da * s_up_ref[...].astype(F32)).astype(BF16)
        dg_ref[...] = (da * s_gate_ref[...].astype(F32)).astype(BF16)

    act = _spec((None, tm, FS), lambda s, i: (s, i, 0))
    shape = jax.ShapeDtypeStruct((NSH, t, FS), BF16)
    return pl.pallas_call(
        body, name=name, grid=(NSH, t // tm),
        in_specs=[_spec((tm, D), lambda s, i: (i, 0)), _spec((None, FS, D), lambda s, i: (s, 0, 0)), act, act],
        out_specs=[act, act], out_shape=[shape, shape],
        compiler_params=_params(("parallel", "parallel")),
    )(dhb, wd, s_up, s_gate)


def _wgrad(name, a, a_spec, b, b_spec, out_rows, out_cols, t, tt, group, slot, scale=None):
    first = isinstance(group, int)
    rows = group if first else group.shape[1]
    return _mm(name, [(a, a_spec, b, b_spec)], grid=(NSH, t // tt),
               o_shape=(NSH, rows, out_cols), o_dtype=BF16,
               o_spec=_spec((None, out_rows, out_cols), lambda s, k: (s, slot, 0)),
               dims=TN, kaxis=1, nk=t // tt, acc_shape=(out_rows, out_cols), scale=scale,
               into=None if first else group)


def _ffn_backward(tag, dh, dhb, h_in, hn, s_up, s_gate, a, wgu, wd, gate_idx, up_idx, down_idx, small, norm_row,
                  grad_a, grad_b):
    t = dh.shape[0]
    tm = min(MM_TILE, t)
    tt = min(WGRAD_TILE, t)
    dg, du = _ffn_bwd_up(tag + "_bwd_up", dhb, wd, s_up, s_gate)
    tok = _spec((tt, D), lambda s, k: (k, 0))
    hid = _spec((None, tt, FS), lambda s, k: (s, k, 0))
    grad_b = _wgrad(tag + "_dwd", a, hid, dhb, tok, FS, D, t, tt, grad_b, down_idx, scale=MACARON)
    grad_a = _wgrad(tag + "_dwg", hn, tok, dg, hid, D, FS, t, tt, grad_a, gate_idx)
    grad_a = _wgrad(tag + "_dwu", hn, tok, du, hid, D, FS, t, tt, grad_a, up_idx)
    act = _spec((None, tm, FS), lambda i, k: (k, i, 0))
    dhn = _mm(tag + "_dhn",
              [(dg, act, wgu, _spec((None, None, D, FS), lambda i, k: (k, 0, 0, 0))),
               (du, act, wgu, _spec((None, None, D, FS), lambda i, k: (k, 1, 0, 0)))],
              grid=(t // tm, NSH), o_shape=(t, D), o_dtype=F32, o_spec=_spec((tm, D), lambda i, k: (i, 0)),
              dims=NT, kaxis=1, nk=NSH, acc_shape=(tm, D))
    dh_in, dhb_in, d_gain = _norm_bwd(tag + "_norm_bwd", dhn, h_in, small, norm_row, dh)
    return dh_in, dhb_in, grad_a, grad_b, d_gain


def _conv_fwd(proj_a, conv_w):
    t = proj_a.shape[0]
    tm = min(ROW_TILE, t)
    hb = tm // CONV_HALO

    def body(ab_ref, ac_ref, ax_ref, acp_ref, axp_ref, w_ref, y_ref):
        i = pl.program_id(1)
        u = ac_ref[...] * ax_ref[...]
        up = jnp.where(i > 0, acp_ref[...] * axp_ref[...], 0.0)
        ext = jnp.concatenate([up, u], axis=0)
        u1 = pltpu.roll(ext, 1, 0)[CONV_HALO:]
        u2 = pltpu.roll(ext, 2, 0)[CONV_HALO:]
        w = w_ref[...]
        conv = w[0:1] * u2 + w[1:2] * u1 + w[2:3] * u
        y_ref[...] = (ab_ref[...] * conv).astype(BF16)

    def cur(off):
        return _spec((tm, LANE), lambda j, i: (i, off + j))

    def prev(off):
        return _spec((CONV_HALO, LANE), lambda j, i: (jnp.maximum(i * hb - 1, 0), off + j))

    return pl.pallas_call(
        body, name="conv_fwd", grid=(4, t // tm),
        in_specs=[cur(0), cur(4), cur(8), prev(4), prev(8),
                  _spec((None, None, 8, LANE), lambda j, i: (j, 0, 0, 0))],
        out_specs=_spec((tm, LANE), lambda j, i: (i, j)),
        out_shape=jax.ShapeDtypeStruct((t, 512), BF16),
        compiler_params=_params(("parallel", "parallel")),
    )(proj_a, proj_a, proj_a, proj_a, proj_a, conv_w)


def _conv_bwd(proj_a, conv_w, dy):
    t = proj_a.shape[0]
    tm = min(ROW_TILE, t)
    hb = tm // CONV_HALO
    nt = t // tm

    def body(ab_ref, ac_ref, ax_ref, dy_ref, acp_ref, axp_ref, abn_ref, dyn_ref, w_ref,
             dab_ref, dac_ref, dax_ref, dw_ref, acc_ref):
        i = pl.program_id(1)
        ab, ac, ax = ab_ref[...], ac_ref[...], ax_ref[...]
        u = ac * ax
        up = jnp.where(i > 0, acp_ref[...] * axp_ref[...], 0.0)
        ext = jnp.concatenate([up, u], axis=0)
        u1 = pltpu.roll(ext, 1, 0)[CONV_HALO:]
        u2 = pltpu.roll(ext, 2, 0)[CONV_HALO:]
        w = w_ref[...]
        conv = w[0:1] * u2 + w[1:2] * u1 + w[2:3] * u
        dy_v = dy_ref[...]
        dab_ref[...] = (dy_v * conv).astype(BF16)
        dc = dy_v * ab
        dcn = jnp.where(i < nt - 1, dyn_ref[...] * abn_ref[...], 0.0)
        extn = jnp.concatenate([dc, dcn], axis=0)
        n = tm + CONV_HALO
        dc1 = pltpu.roll(extn, n - 1, 0)[:tm]
        dc2 = pltpu.roll(extn, n - 2, 0)[:tm]
        du = w[2:3] * dc + w[1:2] * dc1 + w[0:1] * dc2
        dac_ref[...] = (du * ax).astype(BF16)
        dax_ref[...] = (du * ac).astype(BF16)
        rid = lax.broadcasted_iota(jnp.int32, (8, LANE), 0)
        part = jnp.where(rid == 0, jnp.sum(dc * u2, axis=0, keepdims=True),
                         jnp.where(rid == 1, jnp.sum(dc * u1, axis=0, keepdims=True),
                                   jnp.where(rid == 2, jnp.sum(dc * u, axis=0, keepdims=True), 0.0)))

        @pl.when(i == 0)
        def _():
            acc_ref[...] = part

        @pl.when(i > 0)
        def _():
            acc_ref[...] += part

        @pl.when(i == nt - 1)
        def _():
            dw_ref[...] = acc_ref[...]

    def cur(off):
        return _spec((tm, LANE), lambda j, i: (i, off + j))

    def prev(off):
        return _spec((CONV_HALO, LANE), lambda j, i: (jnp.maximum(i * hb - 1, 0), off + j))

    def nxt(off):
        return _spec((CONV_HALO, LANE), lambda j, i: (jnp.minimum((i + 1) * hb, nt * hb - 1), off + j))

    outs = pl.pallas_call(
        body, name="conv_bwd", grid=(4, nt),
        in_specs=[cur(0), cur(4), cur(8), cur(0), prev(4), prev(8), nxt(0), nxt(0),
                  _spec((None, None, 8, LANE), lambda j, i: (j, 0, 0, 0))],
        out_specs=[_spec((tm, LANE), lambda j, i: (i, j)), _spec((tm, LANE), lambda j, i: (i, j)),
                   _spec((tm, LANE), lambda j, i: (i, j)), _spec((None, 8, LANE), lambda j, i: (j, 0, 0))],
        out_shape=[jax.ShapeDtypeStruct((t, 512), BF16), jax.ShapeDtypeStruct((t, 512), BF16),
                   jax.ShapeDtypeStruct((t, 512), BF16), jax.ShapeDtypeStruct((4, 8, LANE), F32)],
        scratch_shapes=[pltpu.VMEM((8, LANE), F32)],
        compiler_params=_params(("parallel", "arbitrary")),
    )(proj_a, proj_a, proj_a, dy, proj_a, proj_a, proj_a, dy, conv_w)
    return outs


def _log_sigmoid(z):
    return jnp.minimum(z, 0.0) - jnp.log(1.0 + jnp.exp(-jnp.abs(z)))


def _sb_masks():
    row = lax.broadcasted_iota(jnp.int32, (SBQ, SBQ), 0)
    col = lax.broadcasted_iota(jnp.int32, (SBQ, SBQ), 1)
    return row, col


def _ones_where(mask):
    return jnp.where(mask, 1.0, 0.0).astype(BF16)


def _head_mask(head):
    lane = lax.broadcasted_iota(jnp.int32, (1, LANE), 1)
    return lane >= 64 if head else lane < 64


def _sb_fwd(proj_b, carry=None):
    t = proj_b.shape[0]
    nq = t // SBQ
    scale = 1.0 / math.sqrt(64.0)

    npair = SB_PAIRS
    wide = npair * LANE
    ngrp = 4 // npair
    chains = [(p, head) for p in range(npair) for head in range(2)]

    grid = (ngrp, nq)
    nc = carry.n if carry is not None else 0

    def body(*refs):
        q_ref, k_ref, v_ref = refs[:3]
        y_ref, l_ref, n_ref = refs[3 + nc:6 + nc]
        riders = (refs[3:3 + nc], refs[6 + nc:6 + 2 * nc], refs[-2], refs[-1]) if nc else None
        if nc:
            carry.ride(grid, riders, "start")
        grp = pl.program_id(0)
        qi = pl.program_id(1)
        row, col = _sb_masks()
        m_suffix = _ones_where(row > col)
        rows = len(chains) * SBQ
        strict = (lax.broadcasted_iota(jnp.int32, (rows, SBQ), 1)
                  < (lax.broadcasted_iota(jnp.int32, (rows, SBQ), 0) & (SBQ - 1)))
        q_pair = []
        for p in range(npair):
            q_all = q_ref[:, p * LANE:(p + 1) * LANE]
            q_pair.append(jnp.concatenate([jnp.where(_head_mask(head), q_all, jnp.zeros_like(q_all)) for head in range(2)],
                                          axis=0))

        def block(kb, state, diag):
            run, acc = state
            start = pl.multiple_of(kb * SBQ, SBQ)
            z = jnp.concatenate([_dg(q_pair[p], k_ref[pl.ds(start, SBQ), p * LANE:(p + 1) * LANE], NT)
                                 for p in range(npair)], axis=0) * scale
            lb = _log_sigmoid(z)
            lk = lb - z
            if diag:
                lk = jnp.where(strict, lk, 0.0)
            hi, lo = _split(lk)
            sums = _dg(jnp.concatenate([hi, lo], axis=0), m_suffix, NN)
            w = jnp.exp(lb + (run + sums[:rows] + sums[rows:]))
            if diag:
                w = jnp.where(strict, w, 0.0)
            wb = w.astype(BF16)
            acc = acc + jnp.concatenate(
                [_dg(wb[2 * p * SBQ:2 * (p + 1) * SBQ], v_ref[pl.ds(start, SBQ), p * LANE:(p + 1) * LANE], NN)
                 for p in range(npair)], axis=0)
            run = run + jnp.sum(hi.astype(F32) + lo.astype(F32), axis=1, keepdims=True)
            return run, acc

        state = block(qi, (jnp.zeros((rows, 1), F32), jnp.zeros((rows, LANE), F32)), True)

        def live(c):
            return jnp.logical_and(c[0] < qi, jnp.max(c[1][0]) > SB_DEAD)

        def step(c):
            return c[0] + 1, block(qi - 1 - c[0], c[1], False)

        count, (run, acc) = lax.while_loop(live, step, (jnp.int32(0), state))
        n_ref[grp * nq + qi] = count.astype(F32)
        hm = _head_mask(0)
        for p in range(npair):
            lo_rows, hi_rows = slice(2 * p * SBQ, (2 * p + 1) * SBQ), slice((2 * p + 1) * SBQ, (2 * p + 2) * SBQ)
            y_ref[:, p * LANE:(p + 1) * LANE] = jnp.where(hm, acc[lo_rows], acc[hi_rows]).astype(BF16)
            l_ref[p] = jnp.where(hm, run[lo_rows], run[hi_rows])
        if nc:
            carry.ride(grid, riders, "finish")

    outs = pl.pallas_call(
        body, name="sb_fwd", grid=grid,
        in_specs=[_spec((SBQ, wide), lambda g, i: (i, g)),
                  _spec((t, wide), lambda g, i: (0, ngrp + g)),
                  _spec((t, wide), lambda g, i: (0, 2 * ngrp + g))] + (carry.in_specs if nc else []),
        out_specs=[_spec((SBQ, wide), lambda g, i: (i, g)), _spec((npair, SBQ, LANE), lambda g, i: (g, i, 0)),
                   pl.BlockSpec(memory_space=pltpu.SMEM)] + (carry.out_specs if nc else []),
        out_shape=[jax.ShapeDtypeStruct((t, 512), BF16), jax.ShapeDtypeStruct((4, t, LANE), F32),
                   jax.ShapeDtypeStruct((ngrp * nq,), F32)] + (carry.out_shape if nc else []),
        scratch_shapes=carry.scratch if nc else [],
        compiler_params=_params(("arbitrary", "arbitrary")),
    )(proj_b, proj_b, proj_b, *(carry.groups if nc else []))
    return outs[0], outs[1], outs[2], (carry.place(outs[3:]) if nc else [])


def _sb_bwd(proj_b, dy, ltot, nblk, rider=None):
    t = proj_b.shape[0]
    nq = t // SBQ
    scale = 1.0 / math.sqrt(64.0)

    npair = SB_PAIRS
    wide = npair * LANE
    ngrp = 4 // npair
    chains = [(p, head) for p in range(npair) for head in range(2)]
    grid = (ngrp, nq)
    nr = rider.n if rider is not None else 0

    def body(*refs):
        q_ref, k_ref, v_ref, dy_ref, l_ref, n_ref = refs[:6]
        dq_ref, dk_ref, dv_ref = refs[6 + nr:9 + nr]
        dk_acc, dv_acc = refs[9 + 2 * nr:11 + 2 * nr]
        riders = (refs[6:6 + nr], refs[9 + nr:9 + 2 * nr], refs[-2], refs[-1]) if nr else None
        if nr:
            rider.ride(grid, riders, "start")
        grp = pl.program_id(0)
        qi = pl.program_id(1)

        @pl.when(qi == 0)
        def _():
            dk_acc[...] = jnp.zeros_like(dk_acc)
            dv_acc[...] = jnp.zeros_like(dv_acc)

        row, col = _sb_masks()
        m_prefix = _ones_where(row <= col)
        m_before = _ones_where(row < col)
        rows = len(chains) * SBQ
        strict = (lax.broadcasted_iota(jnp.int32, (rows, SBQ), 1)
                  < (lax.broadcasted_iota(jnp.int32, (rows, SBQ), 0) & (SBQ - 1)))
        q_pair, do_pair, ltot = [], [], []
        for p in range(npair):
            pl_ = slice(p * LANE, (p + 1) * LANE)
            q_all = q_ref[:, pl_]
            do_all = dy_ref[:, pl_].astype(BF16)
            q_pair.append(jnp.concatenate([jnp.where(_head_mask(h), q_all, jnp.zeros_like(q_all)) for h in range(2)], axis=0))
            do_pair.append(jnp.concatenate([jnp.where(_head_mask(h), do_all, jnp.zeros_like(do_all)) for h in range(2)], axis=0))
            ltot += [l_ref[p][:, h * 64:h * 64 + 1] for h in range(2)]
        ltot = jnp.concatenate(ltot, axis=0)

        def pair_rows(a, p):
            return a[2 * p * SBQ:2 * (p + 1) * SBQ]

        def block(kb, state, diag):
            seen, dseen, dq = state
            start = pl.multiple_of(kb * SBQ, SBQ)
            kk = [k_ref[pl.ds(start, SBQ), p * LANE:(p + 1) * LANE] for p in range(npair)]
            vv = [v_ref[pl.ds(start, SBQ), p * LANE:(p + 1) * LANE] for p in range(npair)]
            z = jnp.concatenate([_dg(q_pair[p], kk[p], NT) for p in range(npair)], axis=0) * scale
            lb = _log_sigmoid(z)
            lk = lb - z
            if diag:
                lk = jnp.where(strict, lk, 0.0)
            hi, lo = _split(lk)
            sums = _dg(jnp.concatenate([hi, lo], axis=0), m_prefix, NN)
            w = jnp.exp(lb + ((ltot - seen) - (sums[:rows] + sums[rows:])))
            if diag:
                w = jnp.where(strict, w, 0.0)
            wb = w.astype(BF16)
            da = w * jnp.concatenate([_dg(do_pair[p], vv[p], NT) for p in range(npair)], axis=0)
            dah, dal = _split(da)
            dsums = _dg(jnp.concatenate([dah, dal], axis=0), m_before, NN)
            sig = jnp.exp(lb)
            dz = (da * (1.0 - sig) - (dseen + dsums[:rows] + dsums[rows:]) * sig) * scale
            if diag:
                dz = jnp.where(strict, dz, 0.0)
            dzb = dz.astype(BF16)
            for p in range(npair):
                pl_ = slice(p * LANE, (p + 1) * LANE)
                dv_acc[pl.ds(start, SBQ), pl_] += _dg(pair_rows(wb, p), do_pair[p], TN)
                dk_acc[pl.ds(start, SBQ), pl_] += _dg(pair_rows(dzb, p), q_pair[p], TN)
            dq = dq + jnp.concatenate([_dg(pair_rows(dzb, p), kk[p], NN) for p in range(npair)], axis=0)
            seen = seen + jnp.sum(hi.astype(F32) + lo.astype(F32), axis=1, keepdims=True)
            dseen = dseen + jnp.sum(da, axis=1, keepdims=True)
            return seen, dseen, dq

        zero = (jnp.zeros((rows, 1), F32), jnp.zeros((rows, 1), F32), jnp.zeros((rows, LANE), F32))
        first = qi - n_ref[grp * nq + qi].astype(jnp.int32)
        state = lax.fori_loop(first, qi, lambda kb, c: block(kb, c, False), zero)
        _, _, dq = block(qi, state, True)
        for p in range(npair):
            dq_ref[:, p * LANE:(p + 1) * LANE] = jnp.where(
                _head_mask(0), dq[2 * p * SBQ:(2 * p + 1) * SBQ], dq[(2 * p + 1) * SBQ:(2 * p + 2) * SBQ]).astype(BF16)

        @pl.when(qi == nq - 1)
        def _():
            dk_ref[...] = dk_acc[...].astype(BF16)
            dv_ref[...] = dv_acc[...].astype(BF16)

        if nr:
            rider.ride(grid, riders, "finish")

    full = jax.ShapeDtypeStruct((t, 512), BF16)
    outs = pl.pallas_call(
        body, name="sb_bwd", grid=grid,
        in_specs=[_spec((SBQ, wide), lambda g, i: (i, g)),
                  _spec((t, wide), lambda g, i: (0, ngrp + g)),
                  _spec((t, wide), lambda g, i: (0, 2 * ngrp + g)),
                  _spec((SBQ, wide), lambda g, i: (i, ngrp + g)),
                  _spec((npair, SBQ, LANE), lambda g, i: (g, i, 0)),
                  pl.BlockSpec(memory_space=pltpu.SMEM)] + (rider.in_specs if nr else []),
        out_specs=[_spec((SBQ, wide), lambda g, i: (i, g)),
                   _spec((t, wide), lambda g, i: (0, g)), _spec((t, wide), lambda g, i: (0, g))]
        + (rider.out_specs if nr else []),
        out_shape=[full, full, full] + (rider.out_shape if nr else []),
        scratch_shapes=[pltpu.VMEM((t, wide), F32), pltpu.VMEM((t, wide), F32)] + (rider.scratch if nr else []),
        compiler_params=_params(("arbitrary", "arbitrary")),
    )(proj_b, proj_b, proj_b, dy, ltot, nblk, *(rider.groups if nr else []))
    return outs[0], outs[1], outs[2], (rider.place(outs[3:]) if nr else [])


def _hgrn_gates(qr, fr, c0, c1):
    mx = jnp.maximum(c0, c1)
    e0, e1 = jnp.exp(c0 - mx), jnp.exp(c1 - mx)
    lb = e1 / (e0 + e1)
    sx = _sigmoid(fr)
    f = lb + (1.0 - lb) * sx
    k = (1.0 - lb) * (1.0 - sx)
    sq = _sigmoid(qr)
    return lb, sx, f, k, sq, qr * sq


def _hgrn_decays(f):
    row = lax.broadcasted_iota(jnp.int32, (CHUNK, CHUNK), 0)
    col = lax.broadcasted_iota(jnp.int32, (CHUNK, CHUNK), 1)
    tril = col <= row
    hi, lo = _split(jnp.log(f))
    trib = _ones_where(tril)
    b = _dg(trib, hi, NN) + _dg(trib, lo, NN)
    bm = b[CHUNK // 2 - 1:CHUNK // 2]
    bl = b[CHUNK - 1:CHUNK]
    return tril, col >= row, b, bm, bl


def _hgrn_fwd(proj_c, small, carry=None):
    t = proj_c.shape[1]
    nc = t // CHUNK
    nh = D // HD

    hps = HGRN_HPS
    wide = hps * HD
    grid = (nh // hps, nc)
    nr = carry.n if carry is not None else 0

    def body(*refs):
        p_ref, c0_ref, c1_ref, gam_ref = refs[:4]
        o_ref, y_ref, sst_ref = refs[4 + nr:7 + nr]
        st_ref = refs[7 + 2 * nr]
        riders = (refs[4:4 + nr], refs[7 + nr:7 + 2 * nr], refs[-2], refs[-1]) if nr else None
        if nr:
            carry.ride(grid, riders, "start")
        c = pl.program_id(1)

        @pl.when(c == 0)
        def _():
            st_ref[...] = jnp.zeros_like(st_ref)

        for j in range(hps):
            ln = slice(j * HD, (j + 1) * HD)
            st0 = st_ref[j]
            sst_ref[j] = st0
            qr, fr, v, g = p_ref[0, :, ln], p_ref[1, :, ln], p_ref[2, :, ln], p_ref[3, :, ln]
            lb, sx, f, k, sq, q = _hgrn_gates(qr, fr, c0_ref[:, ln], c1_ref[:, ln])
            tril, _, b, bm, bl = _hgrn_decays(f)
            qd = q * jnp.exp(b)
            qt = q * jnp.exp(b - bm)
            kt = k * jnp.exp(bm - b)
            kl = k * jnp.exp(bl - b)
            vb = v.astype(BF16)
            att = jnp.where(tril, _dot3(qt, kt, NT), 0.0)
            o = _dg(qd.astype(BF16), st0.astype(BF16), NT) + _dg(att.astype(BF16), vb, NN)
            st_ref[j] = st0 * jnp.exp(bl) + _dg(vb, kl.astype(BF16), TN)
            o_ref[:, ln] = o
            r = lax.rsqrt(jnp.mean(o * o, axis=-1, keepdims=True) + RMS_EPS)
            y_ref[:, ln] = (o * r * gam_ref[...] * (g * _sigmoid(g))).astype(BF16)
        if nr:
            carry.ride(grid, riders, "finish")

    outs = pl.pallas_call(
        body, name="hgrn_fwd", grid=grid,
        in_specs=[_spec((4, CHUNK, wide), lambda h, c: (0, c, h)),
                  _spec((None, 1, wide), lambda h, c: (R_CLB, 0, h)),
                  _spec((None, 1, wide), lambda h, c: (R_CLB + 1, 0, h)),
                  _spec((None, 1, HD), lambda h, c: (R_GAM, 0, 0))] + (carry.in_specs if nr else []),
        out_specs=[_spec((CHUNK, wide), lambda h, c: (c, h)), _spec((CHUNK, wide), lambda h, c: (c, h)),
                   _spec((None, hps, HD, HD), lambda h, c: (c, h, 0, 0))] + (carry.out_specs if nr else []),
        out_shape=[jax.ShapeDtypeStruct((t, D), F32), jax.ShapeDtypeStruct((t, D), BF16),
                   jax.ShapeDtypeStruct((nc, nh, HD, HD), F32)] + (carry.out_shape if nr else []),
        scratch_shapes=[pltpu.VMEM((hps, HD, HD), F32)] + (carry.scratch if nr else []),
        compiler_params=_params(("arbitrary", "arbitrary")),
    )(proj_c, small, small, small, *(carry.groups if nr else []))
    return outs[0], outs[1], outs[2], (carry.place(outs[3:]) if nr else [])


def _hgrn_bwd(proj_c, small, o, sst, dyc, rider=None):
    t = proj_c.shape[1]
    nc = t // CHUNK
    nh = D // HD

    hps = HGRN_HPS
    wide = hps * HD
    ng = nh // hps
    grid = (ng, nc)
    nr = rider.n if rider is not None else 0

    def body(*refs):
        p_ref, c0_ref, c1_ref, gam_ref, o_ref, sst_ref, dy_ref = refs[:7]
        dp_ref, dclb_ref, dgam_ref = refs[7 + nr:10 + nr]
        dst_ref, dlb_acc, dgam_acc = refs[10 + 2 * nr:13 + 2 * nr]
        riders = (refs[7:7 + nr], refs[10 + nr:10 + 2 * nr], refs[-2], refs[-1]) if nr else None
        if nr:
            rider.ride(grid, riders, "start")
        group = pl.program_id(0)
        step = pl.program_id(1)

        @pl.when(step == 0)
        def _():
            dst_ref[...] = jnp.zeros_like(dst_ref)
            dlb_acc[...] = jnp.zeros_like(dlb_acc)

        @pl.when((step == 0) & (group == 0))
        def _():
            dgam_acc[...] = jnp.zeros_like(dgam_acc)

        gam = gam_ref[...]
        dlb_parts, dgam_parts, lbs = [], [], []
        for j in range(hps):
            ln = slice(j * HD, (j + 1) * HD)
            st0 = sst_ref[j]
            dst1 = dst_ref[j]
            qr, fr, v, g = p_ref[0, :, ln], p_ref[1, :, ln], p_ref[2, :, ln], p_ref[3, :, ln]
            lb, sx, f, k, sq, q = _hgrn_gates(qr, fr, c0_ref[:, ln], c1_ref[:, ln])
            tril, triu, b, bm, bl = _hgrn_decays(f)
            eb = jnp.exp(b)
            e_qt = jnp.exp(b - bm)
            e_kt = jnp.exp(bm - b)
            e_kl = jnp.exp(bl - b)
            e_bl = jnp.exp(bl)
            qd, qt, kt, kl = q * eb, q * e_qt, k * e_kt, k * e_kl
            ov = o_ref[:, ln]
            r = lax.rsqrt(jnp.mean(ov * ov, axis=-1, keepdims=True) + RMS_EPS)
            oh = ov * r
            sg = _sigmoid(g)
            dy = dy_ref[:, ln]
            dp_ref[3, :, ln] = (dy * oh * gam * (sg * (1.0 + g * (1.0 - sg)))).astype(BF16)
            dyv = dy * (g * sg)
            gdy = dyv * gam
            do = (gdy - oh * jnp.mean(gdy * oh, axis=-1, keepdims=True)) * r
            dob, vb = do.astype(BF16), v.astype(BF16)
            st0b, dst1b = st0.astype(BF16), dst1.astype(BF16)
            st1 = st0 * e_bl + _dg(vb, kl.astype(BF16), TN)
            att = jnp.where(tril, _dot3(qt, kt, NT), 0.0)
            datt = jnp.where(tril, _dg(dob, vb, NT), 0.0)
            dv = _dg(att.astype(BF16), dob, TN) + _dg(kl.astype(BF16), dst1b, NT)
            dq = _dot3(datt, kt, NN) * e_qt + _dg(dob, st0b, NN) * eb
            dk = _dot3(datt, qt, TN) * e_kt + _dg(vb, dst1b, NN) * e_kl
            db = q * dq - k * dk
            last = lax.broadcasted_iota(jnp.int32, (CHUNK, 1), 0) == CHUNK - 1
            db = db + jnp.where(last, jnp.sum(dst1 * st1, axis=0, keepdims=True), 0.0)
            dbh, dbl = _split(db)
            triub = _ones_where(triu)
            dlf = _dg(triub, dbh, NN) + _dg(triub, dbl, NN)
            dst_ref[j] = dst1 * e_bl + _dg(dob, qd.astype(BF16), TN)
            dp_ref[0, :, ln] = (dq * (sq * (1.0 + qr * (1.0 - sq)))).astype(BF16)
            tmp = dlf / f - dk
            dp_ref[1, :, ln] = (tmp * (1.0 - lb) * sx * (1.0 - sx)).astype(BF16)
            dp_ref[2, :, ln] = dv.astype(BF16)
            dlb_parts.append(jnp.sum((1.0 - sx) * tmp, axis=0, keepdims=True))
            dgam_parts.append(jnp.sum(dyv * oh, axis=0, keepdims=True))
            lbs.append(lb)

        dlb_acc[...] += jnp.concatenate(dlb_parts, axis=1)
        dgam_acc[...] += functools.reduce(lambda a, b: a + b, dgam_parts)

        @pl.when(step == nc - 1)
        def _():
            lb_all = jnp.concatenate(lbs, axis=1)
            d1 = dlb_acc[...] * lb_all * (1.0 - lb_all)
            dclb_ref[...] = jnp.where(lax.broadcasted_iota(jnp.int32, (2, wide), 0) == 0, -d1, d1)

        @pl.when((step == nc - 1) & (group == ng - 1))
        def _():
            dgam_ref[...] = dgam_acc[...]

        if nr:
            rider.ride(grid, riders, "finish")

    rev = lambda h, s: (nc - 1 - s, h)
    outs = pl.pallas_call(
        body, name="hgrn_bwd", grid=grid,
        in_specs=[_spec((4, CHUNK, wide), lambda h, s: (0, nc - 1 - s, h)),
                  _spec((None, 1, wide), lambda h, s: (R_CLB, 0, h)),
                  _spec((None, 1, wide), lambda h, s: (R_CLB + 1, 0, h)),
                  _spec((None, 1, HD), lambda h, s: (R_GAM, 0, 0)),
                  _spec((CHUNK, wide), rev),
                  _spec((None, hps, HD, HD), lambda h, s: (nc - 1 - s, h, 0, 0)),
                  _spec((CHUNK, wide), rev)] + (rider.in_specs if nr else []),
        out_specs=[_spec((4, CHUNK, wide), lambda h, s: (0, nc - 1 - s, h)),
                   _spec((2, wide), lambda h, s: (0, h)),
                   _spec((1, HD), lambda h, s: (0, 0))] + (rider.out_specs if nr else []),
        out_shape=[jax.ShapeDtypeStruct((4, t, D), BF16), jax.ShapeDtypeStruct((2, D), F32),
                   jax.ShapeDtypeStruct((1, HD), F32)] + (rider.out_shape if nr else []),
        scratch_shapes=[pltpu.VMEM((hps, HD, HD), F32), pltpu.VMEM((1, wide), F32), pltpu.VMEM((1, HD), F32)]
        + (rider.scratch if nr else []),
        compiler_params=_params(("arbitrary", "arbitrary")),
    )(proj_c, small, small, small, o, sst, dyc, *(rider.groups if nr else []))
    return outs[0], outs[1], outs[2], (rider.place(outs[3:]) if nr else [])


def _adamw(name, w, g, m, v):
    rows, cols = w.shape
    br = rows
    for cand in (512, 352, 256):
        if rows % cand == 0:
            br = cand
            break
    c1 = 1.0 - ADAM_B1 ** ADAM_STEP
    c2 = 1.0 - ADAM_B2 ** ADAM_STEP

    def body(w_ref, g_ref, m_ref, v_ref, d_ref, mo_ref, vo_ref):
        gv = g_ref[...]
        mn = ADAM_B1 * m_ref[...] + (1.0 - ADAM_B1) * gv
        vn = ADAM_B2 * v_ref[...] + (1.0 - ADAM_B2) * (gv * gv)
        mo_ref[...] = mn
        vo_ref[...] = vn
        d_ref[...] = -ADAM_LR * ((mn / c1) / (jnp.sqrt(vn / c2) + ADAM_EPS) + ADAM_WD * w_ref[...])

    blk = _spec((br, cols), lambda i: (i, 0))
    shape = jax.ShapeDtypeStruct((rows, cols), F32)
    return pl.pallas_call(
        body, name=name, grid=(rows // br,), in_specs=[blk] * 4, out_specs=[blk] * 3, out_shape=[shape] * 3,
        compiler_params=_params(("parallel",)),
    )(w, g, m, v)


def _place():
    x, y, c = lax.axis_index("x"), lax.axis_index("y"), lax.axis_index("c")
    chips = [(1 - x, y), (x, 1 - y), (1 - x, 1 - y)]
    return x, y, c, chips


class _Rider:
    n = 0

    def ride(self, grid, refs, when):
        if not self.n:
            return
        ids = [pl.program_id(a) for a in range(len(grid))]
        edge = [i == (0 if when == "start" else g - 1) for i, g in zip(ids, grid)]
        cond = functools.reduce(jnp.logical_and, edge)

        @pl.when(cond)
        def _():
            (self.start if when == "start" else self.finish)(*refs)


class _Gather(_Rider):
    def __init__(self, groups):
        self.groups = list(groups)
        self.n = len(self.groups)
        self.in_specs = [ANY] * self.n
        self.out_specs = [ANY] * self.n
        self.out_shape = [jax.ShapeDtypeStruct((NSH,) + g.shape, g.dtype) for g in self.groups]
        self.scratch = [pltpu.SemaphoreType.DMA((6 * self.n,)), pltpu.SemaphoreType.DMA((6 * self.n,))] if self.n else []

    def _copies(self, ins, outs, send, recv):
        x, y, c, chips = _place()
        sibling = (x, y, 1 - c)

        def half(gi, chip, hc):
            return outs[gi].at[2 * chip[0] + chip[1], hc]

        def copy(gi, k, src, dst, to):
            return pltpu.make_async_remote_copy(src_ref=src, dst_ref=dst, send_sem=send.at[6 * gi + k],
                                                recv_sem=recv.at[6 * gi + k], device_id=to, device_id_type=MESH)

        pairs = [(gi, j, chip) for gi in range(self.n) for j, chip in enumerate(chips)]
        first = [copy(gi, j, ins[gi].at[c], half(gi, (x, y), c), (*chip, c)) for gi, j, chip in pairs]
        landed = [copy(gi, j, half(gi, chip, c), half(gi, chip, c), sibling) for gi, j, chip in pairs]
        relay = [copy(gi, 3 + j, half(gi, chip, c), half(gi, chip, c), sibling) for gi, j, chip in pairs]
        relayed = [copy(gi, 3 + j, half(gi, chip, 1 - c), half(gi, chip, 1 - c), sibling) for gi, j, chip in pairs]
        return first, landed, relay, relayed

    def start(self, ins, outs, send, recv):
        for cp in self._copies(ins, outs, send, recv)[0]:
            cp.start()

    def finish(self, ins, outs, send, recv):
        first, landed, relay, relayed = self._copies(ins, outs, send, recv)
        for arrived, onward in zip(landed, relay):
            arrived.wait_recv()
            onward.start()
        for cp in relayed:
            cp.wait_recv()
        for cp in first + relay:
            cp.wait_send()

    def place(self, outs):
        me = 2 * lax.axis_index("x") + lax.axis_index("y")
        return [lax.dynamic_update_index_in_dim(o, g, me, 0) for o, g in zip(outs, self.groups)]


def _gather_weights(groups):
    gather = _Gather(groups)
    ng = gather.n

    def body(*refs):
        parts = (refs[:ng], refs[ng:2 * ng], refs[2 * ng], refs[2 * ng + 1])
        gather.start(*parts)
        gather.finish(*parts)

    outs = pl.pallas_call(
        body, name="gather_weights", in_specs=gather.in_specs, out_specs=gather.out_specs, out_shape=gather.out_shape,
        scratch_shapes=gather.scratch, compiler_params=pltpu.CompilerParams(has_side_effects=True),
    )(*groups)
    return gather.place(outs)


def _swap_halves(name, slots):
    n = len(slots)

    def body(*refs):
        ins, outs = refs[:n], refs[n:2 * n]
        send, recv = refs[2 * n:]
        x, y, c, _ = _place()
        cps = []
        for i, (_, row0, rows) in enumerate(slots):
            half = rows // 2
            src = ins[i].at[:, pl.ds(pl.multiple_of(row0 + (1 - c) * half, 16), half)]
            cps.append(pltpu.make_async_remote_copy(src_ref=src, dst_ref=outs[i], send_sem=send.at[i],
                                                    recv_sem=recv.at[i], device_id=(x, y, 1 - c), device_id_type=MESH))
        for cp in cps:
            cp.start()
        for cp in cps:
            cp.wait()

    return pl.pallas_call(
        body, name=name, in_specs=[ANY] * n, out_specs=[ANY] * n,
        out_shape=[jax.ShapeDtypeStruct((NSH, rows // 2, buf.shape[2]), buf.dtype) for buf, _, rows in slots],
        scratch_shapes=[pltpu.SemaphoreType.DMA((n,)), pltpu.SemaphoreType.DMA((n,))],
        compiler_params=pltpu.CompilerParams(has_side_effects=True),
    )(*[buf for buf, _, _ in slots])


class _Send(_Rider):
    def __init__(self, arrays):
        self.groups = list(arrays)
        self.n = len(self.groups)
        self.in_specs = [ANY] * self.n
        self.out_specs = [ANY] * self.n
        self.out_shape = [jax.ShapeDtypeStruct((3,) + g.shape[1:], g.dtype) for g in self.groups]
        self.scratch = [pltpu.SemaphoreType.DMA((3 * self.n,)), pltpu.SemaphoreType.DMA((3 * self.n,))]

    def _copies(self, ins, outs, send, recv):
        x, y, c, chips = _place()
        return [pltpu.make_async_remote_copy(src_ref=ins[gi].at[2 * chip[0] + chip[1]], dst_ref=outs[gi].at[j],
                                             send_sem=send.at[3 * gi + j], recv_sem=recv.at[3 * gi + j],
                                             device_id=(*chip, c), device_id_type=MESH)
                for gi in range(self.n) for j, chip in enumerate(chips)]

    def start(self, ins, outs, send, recv):
        for cp in self._copies(ins, outs, send, recv):
            cp.start()

    def finish(self, ins, outs, send, recv):
        for cp in self._copies(ins, outs, send, recv):
            cp.wait()

    def place(self, outs):
        return list(outs)


def _send_to_owners(name, arrays):
    rider = _Send(arrays)
    n = rider.n

    def body(*refs):
        parts = (refs[:n], refs[n:2 * n], refs[2 * n], refs[2 * n + 1])
        rider.start(*parts)
        rider.finish(*parts)

    return pl.pallas_call(
        body, name=name, in_specs=rider.in_specs, out_specs=rider.out_specs, out_shape=rider.out_shape,
        scratch_shapes=rider.scratch, compiler_params=pltpu.CompilerParams(has_side_effects=True),
    )(*arrays)


def _share_halves(groups):
    ng = len(groups)

    def body(*refs):
        ins, outs = refs[:ng], refs[ng:2 * ng]
        send, recv = refs[2 * ng:]
        x, y, c, _ = _place()
        cps = [pltpu.make_async_remote_copy(src_ref=ins[gi], dst_ref=outs[gi].at[c], send_sem=send.at[gi],
                                            recv_sem=recv.at[gi], device_id=(x, y, 1 - c), device_id_type=MESH)
               for gi in range(ng)]
        for cp in cps:
            cp.start()
        for gi in range(ng):
            pltpu.make_async_remote_copy(src_ref=ins[gi], dst_ref=outs[gi].at[1 - c], send_sem=send.at[gi],
                                         recv_sem=recv.at[gi], device_id=(x, y, 1 - c), device_id_type=MESH).wait_recv()
        for cp in cps:
            cp.wait_send()

    outs = pl.pallas_call(
        body, name="grad_share_halves", in_specs=[ANY] * ng, out_specs=[ANY] * ng,
        out_shape=[jax.ShapeDtypeStruct((2,) + g.shape, g.dtype) for g in groups],
        scratch_shapes=[pltpu.SemaphoreType.DMA((ng,)), pltpu.SemaphoreType.DMA((ng,))],
        compiler_params=pltpu.CompilerParams(has_side_effects=True),
    )(*groups)
    c = lax.axis_index("c")
    return [lax.dynamic_update_index_in_dim(o, g, c, 0) for o, g in zip(outs, groups)]


def _pair_sum(name, slots, got, c_idx):
    n = len(slots)
    in_specs, out_specs, out_shape, operands = [], [], [], []
    for (buf, row0, rows), g in zip(slots, got):
        hb, cols = rows // 4, buf.shape[2]
        in_specs += [pl.BlockSpec((None, hb, cols), lambda q, i, cr, r=row0 // hb: (q, r + 2 * cr[0] + i, 0)),
                     pl.BlockSpec((None, hb, cols), lambda q, i, cr: (q, i, 0))]
        out_specs.append(pl.BlockSpec((None, hb, cols), lambda q, i, cr: (q, i, 0)))
        out_shape.append(jax.ShapeDtypeStruct(g.shape, BF16))
        operands += [buf, g]

    def body(c_ref, *refs):
        for i in range(n):
            refs[2 * n + i][...] = (refs[2 * i][...].astype(F32) + refs[2 * i + 1][...].astype(F32)).astype(BF16)

    return pl.pallas_call(
        body, name=name,
        grid_spec=pltpu.PrefetchScalarGridSpec(num_scalar_prefetch=1, grid=(NSH, 2), in_specs=in_specs, out_specs=out_specs),
        out_shape=out_shape, compiler_params=_params(("parallel", "parallel")),
    )(c_idx, *operands)


def _owner_sum(name, pairs, got, p_idx):
    n = len(pairs)
    in_specs, out_specs, out_shape, operands = [], [], [], []
    for own, g in zip(pairs, got):
        _, rows, cols = own.shape
        hb = rows // 2
        in_specs += [pl.BlockSpec((None, hb, cols), lambda i, pr: (pr[0], i, 0)),
                     pl.BlockSpec((3, hb, cols), lambda i, pr: (0, i, 0))]
        out_specs.append(pl.BlockSpec((hb, cols), lambda i, pr: (i, 0)))
        out_shape.append(jax.ShapeDtypeStruct((rows, cols), F32))
        operands += [own, g]

    def body(p_ref, *refs):
        for i in range(n):
            a_ref, b_ref = refs[2 * i], refs[2 * i + 1]
            refs[2 * n + i][...] = ((a_ref[...].astype(F32) + b_ref[0].astype(F32)) + b_ref[1].astype(F32)) + b_ref[2].astype(F32)

    return pl.pallas_call(
        body, name=name,
        grid_spec=pltpu.PrefetchScalarGridSpec(num_scalar_prefetch=1, grid=(2,), in_specs=in_specs, out_specs=out_specs),
        out_shape=out_shape, compiler_params=_params(("parallel",)),
    )(p_idx, *operands)


def _sum_small(slab):
    def body(in_ref, out_ref, all_ref, send, recv):
        x, y, c, _ = _place()
        me = 4 * x + 2 * y + c
        all_ref[me] = in_ref[...]
        cps = []
        for k in range(1, 8):
            peer = (x ^ (k >> 2), y ^ ((k >> 1) & 1), c ^ (k & 1))
            cps.append(pltpu.make_async_remote_copy(src_ref=in_ref, dst_ref=all_ref.at[me], send_sem=send.at[k - 1],
                                                    recv_sem=recv.at[k - 1], device_id=peer, device_id_type=MESH))
        for cp in cps:
            cp.start()
        for cp in cps:
            cp.wait()
        total = all_ref[0]
        for d in range(1, 8):
            total = total + all_ref[d]
        out_ref[...] = total

    return pl.pallas_call(
        body, name="sum_small",
        in_specs=[pl.BlockSpec(memory_space=pltpu.VMEM)], out_specs=pl.BlockSpec(memory_space=pltpu.VMEM),
        out_shape=jax.ShapeDtypeStruct(slab.shape, F32),
        scratch_shapes=[pltpu.VMEM((8,) + slab.shape, F32), pltpu.SemaphoreType.DMA((7,)), pltpu.SemaphoreType.DMA((7,))],
        compiler_params=pltpu.CompilerParams(has_side_effects=True),
    )(slab)


FFNS = ("pre0", "post0", "pre1", "post1")
W_SHAPES = dict({f + "_gu": (NSH, 2, D, FS) for f in FFNS}, **{f + "_d": (NSH, FS, D) for f in FFNS},
                ab_in=(NSH, D, 768), ab_out=(NSH, 256, D), conv=(NSH, 2, 8, LANE), c_in=(NSH, D, D), c_out=(NSH, 256, D))
CARRIED = dict(pre0_up=("pre0_d",), pre0_down=("ab_in", "ab_out", "conv"),
               sb_fwd=("post0_gu", "post0_d", "pre1_gu", "pre1_d"), post0_up=("c_in", "c_out"),
               hgrn_fwd=("post1_gu", "post1_d"))


FFN_SLOTS = dict(pre0=(0, 2, 0), pre1=(1, 3, 1), post0=(4, 6, 2), post1=(5, 7, 3))
GRAD_SLOTS = dict(ab_out=("b", B_ABOUT, 256), c_in=("b", B_CIN, D), c_out=("b", B_COUT, 256), ab_in=("c", 0, D))
for _f, (_g, _u, _d) in FFN_SLOTS.items():
    GRAD_SLOTS.update({_f + "_g": ("a", _g * D, D), _f + "_u": ("a", _u * D, D), _f + "_d": ("b", _d * FS, FS)})
REDUCE_STAGES = dict(x=("post1_g", "post1_u", "post1_d", "c_out"),
                     y=("c_in", "pre1_g", "pre1_u", "pre1_d", "post0_g", "post0_u", "post0_d", "ab_out"),
                     z=("ab_in", "pre0_g", "pre0_u", "pre0_d"))


def _local_step(x, target, weights, small, shards=None, place=None):
    t = x.shape[0]
    tm = min(MM_TILE, t)
    tw = min(WGRAD_TILE, t)
    nt = t // tm
    tok_si = _spec((tm, D), lambda s, i: (i, 0))
    w = dict(weights)
    reduced = {}

    def reduce_start(stage, buffers):
        if place is None:
            return (), [], None
        names = REDUCE_STAGES[stage]
        slots = [(buffers[GRAD_SLOTS[n][0]],) + GRAD_SLOTS[n][1:] for n in names]
        got = _swap_halves("grad_swap_" + stage, slots)
        pairs = _pair_sum("grad_pair_sum_" + stage, slots, got, place[0])
        return names, pairs, _Send(pairs)

    def reduce_end(stage, names, pairs, landed):
        if place is not None:
            reduced.update(zip(names, _owner_sum("grad_owner_sum_" + stage, pairs, landed, place[1])))

    def carried(kernel_name):
        names = [n for n in CARRIED[kernel_name] if n not in w]
        return names, (_Gather([shards[n] for n in names]) if names else None)

    def land(names, arrays):
        for n, a in zip(names, arrays):
            w[n] = a.reshape(W_SHAPES[n])

    def ffn_forward(tag, h, norm_row):
        hn = _norm_fwd(tag + "_norm", h, small, norm_row)
        names, gather = carried(tag + "_up") if tag + "_up" in CARRIED else ([], None)
        s_up, s_gate, a, got = _ffn_up(tag + "_up", hn, w[tag + "_gu"], gather)
        land(names, got)
        names, gather = carried(tag + "_down") if tag + "_down" in CARRIED else ([], None)
        out = _ffn_down(tag + "_down", a, w[tag + "_d"], h, gather)
        if gather is not None:
            out, got = out
            land(names, got)
        return out, (h, hn, s_up, s_gate, a)

    def out_proj(name, y, w_out, h):
        return _mm(name, [(y, _spec((tm, 256), lambda i, k: (i, k)), w_out, _spec((None, 256, D), lambda i, k: (k, 0, 0)))],
                   grid=(nt, NSH), o_shape=(t, D), o_dtype=F32, o_spec=_spec((tm, D), lambda i, k: (i, 0)),
                   dims=NN, kaxis=1, nk=NSH, acc_shape=(tm, D), res=(h, _spec((tm, D), lambda i, k: (i, 0))))

    def out_proj_bwd(tag, dhb, y, w_out, blk, grad_b):
        dy = _mm(tag + "_dy", [(dhb, tok_si, w_out, _spec((None, 256, D), lambda s, i: (s, 0, 0)))],
                 grid=(NSH, nt), o_shape=(t, D), o_dtype=F32, o_spec=_spec((tm, 256), lambda s, i: (i, s)),
                 dims=NT, kaxis=1, nk=1)
        grad_b = _wgrad(tag + "_dwout", y, _spec((tw, 256), lambda s, k: (k, s)), dhb, _spec((tw, D), lambda s, k: (k, 0)),
                        256, D, t, tw, grad_b, blk)
        return dy, grad_b

    h0 = x
    h1, pre0 = ffn_forward("pre0", h0, R_PRE)
    hn_ab = _norm_fwd("mix0_norm", h1, small, R_MIX)
    proj_a = _mm("ab_proj_a", [(hn_ab, tok_si, w["ab_in"], _spec((None, D, 768), lambda s, i: (s, 0, 0)))],
                 grid=(2, nt), o_shape=(t, 1536), o_dtype=F32, o_spec=_spec((tm, 768), lambda s, i: (i, s)),
                 dims=NN, kaxis=1, nk=1)
    proj_b = _mm("ab_proj_b", [(hn_ab, tok_si, w["ab_in"], _spec((None, D, 768), lambda s, i: (s + 2, 0, 0)))],
                 grid=(2, nt), o_shape=(t, 1536), o_dtype=BF16, o_spec=_spec((tm, 768), lambda s, i: (i, s)),
                 dims=NN, kaxis=1, nk=1)
    y_a = _conv_fwd(proj_a, w["conv"])
    names, gather = carried("sb_fwd")
    y_b, ltot, nblk, got = _sb_fwd(proj_b, gather)
    land(names, got)
    y_ab = jnp.concatenate([y_a, y_b], axis=1)
    h2 = out_proj("ab_out", y_ab, w["ab_out"], h1)
    h3, post0 = ffn_forward("post0", h2, R_POST)
    h4, pre1 = ffn_forward("pre1", h3, R_PRE + 1)
    hn_c = _norm_fwd("mix1_norm", h4, small, R_MIX + 1)
    proj_c = _mm("c_proj", [(hn_c, tok_si, w["c_in"], _spec((None, D, D), lambda s, i: (s, 0, 0)))],
                 grid=(NSH, nt), o_shape=(NSH, t, D), o_dtype=F32, o_spec=_spec((None, tm, D), lambda s, i: (s, i, 0)),
                 dims=NN, kaxis=1, nk=1)
    names, gather = carried("hgrn_fwd")
    o_c, y_c, sst, got = _hgrn_fwd(proj_c, small, gather)
    land(names, got)
    h5 = out_proj("c_out", y_c, w["c_out"], h4)
    h6, post1 = ffn_forward("post1", h5, R_POST + 1)
    dh, dhb, d_fin, loss = _final_loss(h6, small, target)

    dh, dhb, grad_a, grad_b, dn_post1 = _ffn_backward("post1", dh, dhb, *post1, w["post1_gu"], w["post1_d"],
                                                      *FFN_SLOTS["post1"], small, R_POST + 1, 8 * D, B_ROWS)
    dy_c, grad_b = out_proj_bwd("c", dhb, y_c, w["c_out"], B_COUT // 256, grad_b)
    names, pairs, rider = reduce_start("x", dict(a=grad_a, b=grad_b))
    dproj_c, d_clb, d_gam, landed = _hgrn_bwd(proj_c, small, o_c, sst, dy_c, rider)
    reduce_end("x", names, pairs, landed)
    grad_b = _wgrad("c_dwin", hn_c, _spec((tw, D), lambda s, k: (k, 0)), dproj_c, _spec((None, tw, D), lambda s, k: (s, k, 0)),
                    D, D, t, tw, grad_b, B_CIN // D)
    dhn = _mm("c_dhn", [(dproj_c, _spec((None, tm, D), lambda i, k: (k, i, 0)),
                         w["c_in"], _spec((None, D, D), lambda i, k: (k, 0, 0)))],
              grid=(nt, NSH), o_shape=(t, D), o_dtype=F32, o_spec=_spec((tm, D), lambda i, k: (i, 0)),
              dims=NT, kaxis=1, nk=NSH, acc_shape=(tm, D))
    dh, dhb, dn_mix1 = _norm_bwd("mix1_norm_bwd", dhn, h4, small, R_MIX + 1, dh)
    dh, dhb, grad_a, grad_b, dn_pre1 = _ffn_backward("pre1", dh, dhb, *pre1, w["pre1_gu"], w["pre1_d"],
                                                     *FFN_SLOTS["pre1"], small, R_PRE + 1, grad_a, grad_b)
    dh, dhb, grad_a, grad_b, dn_post0 = _ffn_backward("post0", dh, dhb, *post0, w["post0_gu"], w["post0_d"],
                                                      *FFN_SLOTS["post0"], small, R_POST, grad_a, grad_b)
    dy_ab, grad_b = out_proj_bwd("ab", dhb, y_ab, w["ab_out"], B_ABOUT // 256, grad_b)
    dab, dac, dax, d_conv = _conv_bwd(proj_a, w["conv"], dy_ab)
    names, pairs, rider = reduce_start("y", dict(a=grad_a, b=grad_b))
    dq, dk, dv, landed = _sb_bwd(proj_b, dy_ab, ltot, nblk, rider)
    reduce_end("y", names, pairs, landed)
    dproj_ab = jnp.concatenate([dab, dac, dax, dq, dk, dv], axis=1)
    grad_c = _wgrad("ab_dwin", hn_ab, _spec((tw, D), lambda s, k: (k, 0)), dproj_ab, _spec((tw, 768), lambda s, k: (k, s)),
                    D, 768, t, tw, D, 0)
    dhn = _mm("ab_dhn", [(dproj_ab, _spec((tm, 768), lambda i, k: (i, k)),
                          w["ab_in"], _spec((None, D, 768), lambda i, k: (k, 0, 0)))],
              grid=(nt, NSH), o_shape=(t, D), o_dtype=F32, o_spec=_spec((tm, D), lambda i, k: (i, 0)),
              dims=NT, kaxis=1, nk=NSH, acc_shape=(tm, D))
    dh, dhb, dn_mix0 = _norm_bwd("mix0_norm_bwd", dhn, h1, small, R_MIX, dh)
    dh, dhb, grad_a, grad_b, dn_pre0 = _ffn_backward("pre0", dh, dhb, *pre0, w["pre0_gu"], w["pre0_d"],
                                                     *FFN_SLOTS["pre0"], small, R_PRE, grad_a, grad_b)
    names, pairs, rider = reduce_start("z", dict(a=grad_a, b=grad_b, c=grad_c))
    if rider is not None:
        reduce_end("z", names, pairs, _send_to_owners("grad_send_z", pairs))
    zero = jnp.zeros((1, D), F32)
    conv_rows = jnp.pad(jnp.transpose(d_conv[:, :3, :], (1, 0, 2)).reshape(3, 512), ((0, 0), (0, D - 512)))
    small_grad = jnp.concatenate([
        dn_pre0, dn_pre1, dn_mix0, dn_mix1, dn_post0, dn_post1, d_clb, d_fin,
        jnp.pad(d_gam, ((0, 0), (0, D - HD))), conv_rows,
        jnp.pad(loss, ((0, 0), (0, D - 1))), zero, zero], axis=0)
    return dh, grad_a, grad_b, grad_c, small_grad, reduced


def _small_slab(rows):
    parts = [jnp.pad(r.astype(F32), ((0, 0), (0, D - r.shape[1]))) for r in rows]
    slab = jnp.concatenate(parts, axis=0)
    return jnp.pad(slab, ((0, SMALL_ROWS - slab.shape[0]), (0, 0)))


def kernel(x, ffn_pre_norm, ffn_pre_w_gate, ffn_pre_w_up, ffn_pre_w_down, mix_norm, ffn_post_norm, ffn_post_w_gate, ffn_post_w_up, ffn_post_w_down, ab_w_in, ab_conv_w, ab_w_out, c_w_in, c_lower_bounds, c_out_norm, c_w_out, final_norm, loss_target, m_ffn_pre_norm, m_ffn_pre_w_gate, m_ffn_pre_w_up, m_ffn_pre_w_down, m_mix_norm, m_ffn_post_norm, m_ffn_post_w_gate, m_ffn_post_w_up, m_ffn_post_w_down, m_ab_w_in, m_ab_conv_w, m_ab_w_out, m_c_w_in, m_c_lower_bounds, m_c_out_norm, m_c_w_out, m_final_norm, v_ffn_pre_norm, v_ffn_pre_w_gate, v_ffn_pre_w_up, v_ffn_pre_w_down, v_mix_norm, v_ffn_post_norm, v_ffn_post_w_gate, v_ffn_post_w_up, v_ffn_post_w_down, v_ab_w_in, v_ab_conv_w, v_ab_w_out, v_c_w_in, v_c_lower_bounds, v_c_out_norm, v_c_w_out, v_final_norm):
    t = x.shape[1]
    xi, yi, ci = lax.axis_index("x"), lax.axis_index("y"), lax.axis_index("c")
    p_idx = (2 * xi + yi).astype(jnp.int32).reshape(1)
    c_idx = ci.astype(jnp.int32).reshape(1)

    def halves(m):
        return m.astype(BF16).reshape(2, m.shape[0] // 2, m.shape[1])

    shards = {}
    for name, (w_gate, w_up, w_down, layer) in dict(
            pre0=(ffn_pre_w_gate, ffn_pre_w_up, ffn_pre_w_down, 0), post0=(ffn_post_w_gate, ffn_post_w_up, ffn_post_w_down, 0),
            pre1=(ffn_pre_w_gate, ffn_pre_w_up, ffn_pre_w_down, 1), post1=(ffn_post_w_gate, ffn_post_w_up, ffn_post_w_down, 1)).items():
        shards[name + "_gu"] = jnp.stack([w_gate[layer], w_up[layer]]).astype(BF16)
        shards[name + "_d"] = halves(w_down[layer])
    conv_pad = jnp.pad(ab_conv_w[0], ((0, 5), (0, 0)))
    shards.update(ab_in=halves(ab_w_in[0]), ab_out=halves(ab_w_out[0]), c_in=halves(c_w_in[0]), c_out=halves(c_w_out[0]),
                  conv=jnp.stack([conv_pad, jnp.zeros_like(conv_pad)]))
    first = _gather_weights([shards["pre0_gu"]])[0].reshape(W_SHAPES["pre0_gu"])

    small = _small_slab([ffn_pre_norm, mix_norm, ffn_post_norm, c_lower_bounds, final_norm.reshape(1, D), c_out_norm])
    small = small.reshape(SMALL_ROWS, 1, D)

    grad_x, _, _, _, small_grad, reduced = _local_step(x[0], loss_target[0], dict(pre0_gu=first), small, shards,
                                                        (c_idx, p_idx))
    order = list(reduced)
    whole = {n: g.reshape(2 * g.shape[1], g.shape[2]) for n, g in zip(order, _share_halves([reduced[n] for n in order]))}
    small_sum = _sum_small(small_grad)

    my_conv = lax.dynamic_slice(small_sum[R_CONV:R_CONV + 3], (0, (2 * xi + yi) * 128), (3, 128))
    grads = {
        "ffn_pre_norm": small_sum[R_PRE:R_PRE + 2], "mix_norm": small_sum[R_MIX:R_MIX + 2],
        "ffn_post_norm": small_sum[R_POST:R_POST + 2], "c_lower_bounds": small_sum[R_CLB:R_CLB + 2],
        "c_out_norm": small_sum[R_GAM:R_GAM + 1, :HD], "final_norm": small_sum[R_FIN],
        "ab_conv_w": my_conv.reshape(1, 3, 128),
        "ab_w_in": whole["ab_in"][None], "ab_w_out": whole["ab_out"][None],
        "c_w_in": whole["c_in"][None], "c_w_out": whole["c_out"][None],
    }
    for kind, key in (("gate", "_g"), ("up", "_u"), ("down", "_d")):
        grads["ffn_pre_w_" + kind] = jnp.stack([whole["pre0" + key], whole["pre1" + key]])
        grads["ffn_post_w_" + kind] = jnp.stack([whole["post0" + key], whole["post1" + key]])
    weights = dict(ffn_pre_norm=ffn_pre_norm, ffn_pre_w_gate=ffn_pre_w_gate, ffn_pre_w_up=ffn_pre_w_up, ffn_pre_w_down=ffn_pre_w_down, mix_norm=mix_norm, ffn_post_norm=ffn_post_norm, ffn_post_w_gate=ffn_post_w_gate, ffn_post_w_up=ffn_post_w_up, ffn_post_w_down=ffn_post_w_down, ab_w_in=ab_w_in, ab_conv_w=ab_conv_w, ab_w_out=ab_w_out, c_w_in=c_w_in, c_lower_bounds=c_lower_bounds, c_out_norm=c_out_norm, c_w_out=c_w_out, final_norm=final_norm)
    m_in = dict(ffn_pre_norm=m_ffn_pre_norm, ffn_pre_w_gate=m_ffn_pre_w_gate, ffn_pre_w_up=m_ffn_pre_w_up, ffn_pre_w_down=m_ffn_pre_w_down, mix_norm=m_mix_norm, ffn_post_norm=m_ffn_post_norm, ffn_post_w_gate=m_ffn_post_w_gate, ffn_post_w_up=m_ffn_post_w_up, ffn_post_w_down=m_ffn_post_w_down, ab_w_in=m_ab_w_in, ab_conv_w=m_ab_conv_w, ab_w_out=m_ab_w_out, c_w_in=m_c_w_in, c_lower_bounds=m_c_lower_bounds, c_out_norm=m_c_out_norm, c_w_out=m_c_w_out, final_norm=m_final_norm)
    v_in = dict(ffn_pre_norm=v_ffn_pre_norm, ffn_pre_w_gate=v_ffn_pre_w_gate, ffn_pre_w_up=v_ffn_pre_w_up, ffn_pre_w_down=v_ffn_pre_w_down, mix_norm=v_mix_norm, ffn_post_norm=v_ffn_post_norm, ffn_post_w_gate=v_ffn_post_w_gate, ffn_post_w_up=v_ffn_post_w_up, ffn_post_w_down=v_ffn_post_w_down, ab_w_in=v_ab_w_in, ab_conv_w=v_ab_conv_w, ab_w_out=v_ab_w_out, c_w_in=v_c_w_in, c_lower_bounds=v_c_lower_bounds, c_out_norm=v_c_out_norm, c_w_out=v_c_w_out, final_norm=v_final_norm)
    names = list(weights)
    big = [n for n in names if weights[n].size >= 65536]
    tiny = [n for n in names if n not in big]

    delta, new_m, new_v = {}, {}, {}
    for n in big:
        shape = weights[n].shape
        two_d = (shape[0] * shape[1], shape[2])
        d, m2, v2 = _adamw("adamw_" + n, weights[n].reshape(two_d), grads[n].reshape(two_d),
                           m_in[n].reshape(two_d), v_in[n].reshape(two_d))
        delta[n], new_m[n], new_v[n] = d.reshape(shape), m2.reshape(shape), v2.reshape(shape)

    def tiny_slab(src):
        return _small_slab([src[n].reshape(-1, src[n].shape[-1]) for n in tiny])

    offs, row = {}, 0
    for n in tiny:
        nrows = weights[n].size // weights[n].shape[-1]
        offs[n] = (row, nrows)
        row += nrows
    d, m2, v2 = _adamw("adamw_small", tiny_slab(weights), tiny_slab(grads), tiny_slab(m_in), tiny_slab(v_in))
    for n in tiny:
        r0, nr = offs[n]
        shape = weights[n].shape
        for dst, src in ((delta, d), (new_m, m2), (new_v, v2)):
            dst[n] = src[r0:r0 + nr, :shape[-1]].reshape(shape)

    loss = small_sum[R_LOSS, 0]
    return (loss, grad_x.reshape(1, t, D), *[grads[n] for n in names], *[delta[n] for n in names],
            *[new_m[n] for n in names], *[new_v[n] for n in names])
```

```python
import functools
import math

import jax
import jax.numpy as jnp
from jax import lax
from jax.experimental import pallas as pl
from jax.experimental.pallas import tpu as pltpu

F32 = jnp.float32
BF16 = jnp.bfloat16
MESH = pl.DeviceIdType.MESH
ANY = pl.BlockSpec(memory_space=pl.ANY)

D = 1024
FS = 704
NSH = 4
RMS_EPS = 1e-6
MACARON = 0.5
CHUNK = 64
HD = 128
HGRN_HPS = 4
SBQ = 128
SB_PAIRS = 2
SB_DEAD = -105.0
CONV_HALO = 8
LANE = 128
ROW_TILE = 512
MM_TILE = 1024
WGRAD_TILE = 4096
VMEM_LIMIT = 48 * 1024 * 1024

ADAM_LR, ADAM_B1, ADAM_B2, ADAM_EPS, ADAM_WD, ADAM_STEP = 0.001, 0.9, 0.999, 1e-08, 0.01, 10

NN = ((1,), (0,))
NT = ((1,), (1,))
TN = ((0,), (0,))

SMALL_ROWS = 16
R_PRE, R_MIX, R_POST, R_CLB, R_FIN, R_GAM, R_CONV, R_LOSS = 0, 2, 4, 6, 8, 9, 10, 13

B_DOWN = 0
B_ABOUT = 4 * FS
B_CIN = B_ABOUT + 256
B_COUT = B_CIN + 1024
B_ROWS = B_COUT + 256


def _dg(a, b, dims):
    return lax.dot_general(a, b, (dims, ((), ())), preferred_element_type=F32)


def _split(x):
    hi = x.astype(BF16)
    lo = (x - hi.astype(F32)).astype(BF16)
    return hi, lo


def _dot3(a, b, dims):
    ah, al = _split(a)
    bh, bl = _split(b)
    return _dg(ah, bh, dims) + _dg(ah, bl, dims) + _dg(al, bh, dims)


def _sigmoid(x):
    return 1.0 / (1.0 + jnp.exp(-x))


def _params(sem):
    return pltpu.CompilerParams(dimension_semantics=sem, vmem_limit_bytes=VMEM_LIMIT)


def _spec(shape, imap):
    return pl.BlockSpec(shape, imap)


def _mm(name, pairs, *, grid, o_shape, o_dtype, o_spec, dims, kaxis, nk, acc_shape=None, res=None, scale=None,
        into=None, carry=None):
    npairs = len(pairs)
    operands, specs = [], []
    for a, a_spec, b, b_spec in pairs:
        operands += [a, b]
        specs += [a_spec, b_spec]
    if res is not None:
        operands.append(res[0])
        specs.append(res[1])
    aliases = {}
    if into is not None:
        aliases = {len(operands): 0}
        operands.append(into)
        specs.append(ANY)
    n_own = len(operands)
    nc = carry.n if carry is not None else 0
    if nc:
        operands += carry.groups
        specs += carry.in_specs

    def body(*refs):
        o_ref = refs[n_own + nc]
        riders = (refs[n_own:n_own + nc], refs[n_own + nc + 1:n_own + 2 * nc + 1], refs[-2], refs[-1]) if nc else None
        if nc:
            carry.ride(grid, riders, "start")
        part = None
        for n in range(npairs):
            d = _dg(refs[2 * n][...], refs[2 * n + 1][...], dims)
            part = d if part is None else part + d

        def finish(val):
            if scale is not None:
                val = val * scale
            if res is not None:
                val = val + refs[2 * npairs][...]
            o_ref[...] = val.astype(o_dtype)

        if nk == 1:
            finish(part)
        else:
            acc_ref = refs[n_own + 2 * nc + 1]
            k = pl.program_id(kaxis)

            @pl.when(k == 0)
            def _():
                acc_ref[...] = part

            @pl.when(k > 0)
            def _():
                acc_ref[...] += part

            @pl.when(k == nk - 1)
            def _():
                finish(acc_ref[...])
        if nc:
            carry.ride(grid, riders, "finish")

    sem = tuple("arbitrary" if (nc or (ax == kaxis and nk > 1)) else "parallel" for ax in range(len(grid)))
    outs = pl.pallas_call(
        body, name=name, grid=grid, in_specs=specs, out_specs=[o_spec] + (carry.out_specs if nc else []),
        out_shape=[jax.ShapeDtypeStruct(o_shape, o_dtype)] + (carry.out_shape if nc else []),
        scratch_shapes=([pltpu.VMEM(acc_shape, F32)] if nk > 1 else []) + (carry.scratch if nc else []),
        input_output_aliases=aliases,
        compiler_params=_params(sem),
    )(*operands)
    return (outs[0], carry.place(outs[1:])) if nc else outs[0]


def _norm_fwd(name, h, gain_slab, row):
    t = h.shape[0]
    tm = min(ROW_TILE, t)

    def body(h_ref, g_ref, o_ref):
        x = h_ref[...]
        r = lax.rsqrt(jnp.mean(x * x, axis=-1, keepdims=True) + RMS_EPS)
        o_ref[...] = (x * r * g_ref[...]).astype(BF16)

    return pl.pallas_call(
        body, name=name, grid=(t // tm,),
        in_specs=[_spec((tm, D), lambda i: (i, 0)), _spec((None, 1, D), lambda i: (row, 0, 0))],
        out_specs=_spec((tm, D), lambda i: (i, 0)),
        out_shape=jax.ShapeDtypeStruct((t, D), BF16),
        compiler_params=_params(("parallel",)),
    )(h, gain_slab)


def _norm_bwd(name, dhn, h, gain_slab, row, dres):
    t = h.shape[0]
    tm = min(ROW_TILE, t)
    nt = t // tm

    def body(dhn_ref, h_ref, g_ref, dres_ref, dh_ref, dhb_ref, dg_ref, acc_ref):
        i = pl.program_id(0)
        x = h_ref[...]
        r = lax.rsqrt(jnp.mean(x * x, axis=-1, keepdims=True) + RMS_EPS)
        xh = x * r
        dy = dhn_ref[...]
        gdy = dy * g_ref[...]
        dx = (gdy - xh * jnp.mean(gdy * xh, axis=-1, keepdims=True)) * r
        dh = dres_ref[...] + dx
        dh_ref[...] = dh
        dhb_ref[...] = dh.astype(BF16)
        part = jnp.sum((dy * xh).reshape(tm // 8, 8, D), axis=0)

        @pl.when(i == 0)
        def _():
            acc_ref[...] = part

        @pl.when(i > 0)
        def _():
            acc_ref[...] += part

        @pl.when(i == nt - 1)
        def _():
            dg_ref[...] = jnp.sum(acc_ref[...], axis=0, keepdims=True)

    row_spec = _spec((tm, D), lambda i: (i, 0))
    return pl.pallas_call(
        body, name=name, grid=(nt,),
        in_specs=[row_spec, row_spec, _spec((None, 1, D), lambda i: (row, 0, 0)), row_spec],
        out_specs=[row_spec, row_spec, _spec((1, D), lambda i: (0, 0))],
        out_shape=[jax.ShapeDtypeStruct((t, D), F32), jax.ShapeDtypeStruct((t, D), BF16),
                   jax.ShapeDtypeStruct((1, D), F32)],
        scratch_shapes=[pltpu.VMEM((8, D), F32)],
        compiler_params=_params(("arbitrary",)),
    )(dhn, h, gain_slab, dres)


def _final_loss(h, gain_slab, target):
    t = h.shape[0]
    tm = min(ROW_TILE, t)
    nt = t // tm

    def body(h_ref, g_ref, t_ref, dh_ref, dhb_ref, dg_ref, loss_ref, acc_ref, lacc_ref):
        i = pl.program_id(0)
        x = h_ref[...]
        g = g_ref[...]
        r = lax.rsqrt(jnp.mean(x * x, axis=-1, keepdims=True) + RMS_EPS)
        xh = x * r
        err = xh * g - t_ref[...]
        dy = err * (1.0 / D)
        gdy = dy * g
        dh = (gdy - xh * jnp.mean(gdy * xh, axis=-1, keepdims=True)) * r
        dh_ref[...] = dh
        dhb_ref[...] = dh.astype(BF16)
        part = jnp.sum((dy * xh).reshape(tm // 8, 8, D), axis=0)
        lpart = jnp.sum((err * err).reshape(tm // 8, 8, D), axis=0)

        @pl.when(i == 0)
        def _():
            acc_ref[...] = part
            lacc_ref[...] = lpart

        @pl.when(i > 0)
        def _():
            acc_ref[...] += part
            lacc_ref[...] += lpart

        @pl.when(i == nt - 1)
        def _():
            dg_ref[...] = jnp.sum(acc_ref[...], axis=0, keepdims=True)
            rows = jnp.sum(lacc_ref[...], axis=0, keepdims=True)
            loss_ref[...] = jnp.sum(rows, axis=1, keepdims=True) * (0.5 / D)

    row_spec = _spec((tm, D), lambda i: (i, 0))
    return pl.pallas_call(
        body, name="final_loss", grid=(nt,),
        in_specs=[row_spec, _spec((None, 1, D), lambda i: (R_FIN, 0, 0)), row_spec],
        out_specs=[row_spec, row_spec, _spec((1, D), lambda i: (0, 0)), _spec((1, 1), lambda i: (0, 0))],
        out_shape=[jax.ShapeDtypeStruct((t, D), F32), jax.ShapeDtypeStruct((t, D), BF16),
                   jax.ShapeDtypeStruct((1, D), F32), jax.ShapeDtypeStruct((1, 1), F32)],
        scratch_shapes=[pltpu.VMEM((8, D), F32), pltpu.VMEM((8, D), F32)],
        compiler_params=_params(("arbitrary",)),
    )(h, gain_slab, target)


def _ffn_up(name, hn, wgu, carry=None):
    t = hn.shape[0]
    tm = min(MM_TILE, t)
    grid = (NSH, t // tm)
    nc = carry.n if carry is not None else 0

    def body(*refs):
        x_ref, wg_ref, wu_ref = refs[:3]
        s_up_ref, s_gate_ref, a_ref = refs[3 + nc:6 + nc]
        riders = (refs[3:3 + nc], refs[6 + nc:6 + 2 * nc], refs[-2], refs[-1]) if nc else None
        if nc:
            carry.ride(grid, riders, "start")
        x = x_ref[...]
        g = _dg(x, wg_ref[...], NT)
        u = _dg(x, wu_ref[...], NT)
        sg = _sigmoid(g)
        silu = g * sg
        s_up_ref[...] = (MACARON * silu).astype(BF16)
        s_gate_ref[...] = (MACARON * u * (sg * (1.0 + g * (1.0 - sg)))).astype(BF16)
        a_ref[...] = (silu * u).astype(BF16)
        if nc:
            carry.ride(grid, riders, "finish")

    act = _spec((None, tm, FS), lambda s, i: (s, i, 0))
    shape = jax.ShapeDtypeStruct((NSH, t, FS), BF16)
    outs = pl.pallas_call(
        body, name=name, grid=grid,
        in_specs=[_spec((tm, D), lambda s, i: (i, 0)),
                  _spec((None, None, FS, D), lambda s, i: (s, 0, 0, 0)),
                  _spec((None, None, FS, D), lambda s, i: (s, 1, 0, 0))] + (carry.in_specs if nc else []),
        out_specs=[act, act, act] + (carry.out_specs if nc else []),
        out_shape=[shape, shape, shape] + (carry.out_shape if nc else []),
        scratch_shapes=carry.scratch if nc else [],
        compiler_params=_params(("arbitrary", "arbitrary") if nc else ("parallel", "parallel")),
    )(hn, wgu, wgu, *(carry.groups if nc else []))
    return (outs[0], outs[1], outs[2], carry.place(outs[3:]) if nc else [])


def _ffn_down(name, a, wd, h, carry=None):
    t = h.shape[0]
    tm = min(MM_TILE, t)
    return _mm(name, [(a, _spec((None, tm, FS), lambda i, k: (k, i, 0)),
                       wd, _spec((None, FS, D), lambda i, k: (k, 0, 0)))],
               grid=(t // tm, NSH), o_shape=(t, D), o_dtype=F32, o_spec=_spec((tm, D), lambda i, k: (i, 0)),
               dims=NN, kaxis=1, nk=NSH, acc_shape=(tm, D), res=(h, _spec((tm, D), lambda i, k: (i, 0))),
               scale=MACARON, carry=carry)


def _ffn_bwd_up(name, dhb, wd, s_up, s_gate):
    t = dhb.shape[0]
    tm = min(MM_TILE, t)

    def body(dh_ref, wd_ref, s_up_ref, s_gate_ref, dg_ref, du_ref):
        da = _dg(dh_ref[...], wd_ref[...], NT)
        du_ref[...] = (da * s_up_ref[...].astype(F32)).astype(BF16)
        dg_ref[...] = (da * s_gate_ref[...].astype(F32)).astype(BF16)

    act = _spec((None, tm, FS), lambda s, i: (s, i, 0))
    shape = jax.ShapeDtypeStruct((NSH, t, FS), BF16)
    return pl.pallas_call(
        body, name=name, grid=(NSH, t // tm),
        in_specs=[_spec((tm, D), lambda s, i: (i, 0)), _spec((None, FS, D), lambda s, i: (s, 0, 0)), act, act],
        out_specs=[act, act], out_shape=[shape, shape],
        compiler_params=_params(("parallel", "parallel")),
    )(dhb, wd, s_up, s_gate)


def _wgrad(name, a, a_spec, b, b_spec, out_rows, out_cols, t, tt, group, slot, scale=None):
    first = isinstance(group, int)
    rows = group if first else group.shape[1]
    return _mm(name, [(a, a_spec, b, b_spec)], grid=(NSH, t // tt),
               o_shape=(NSH, rows, out_cols), o_dtype=BF16,
               o_spec=_spec((None, out_rows, out_cols), lambda s, k: (s, slot, 0)),
               dims=TN, kaxis=1, nk=t // tt, acc_shape=(out_rows, out_cols), scale=scale,
               into=None if first else group)


def _ffn_backward(tag, dh, dhb, h_in, hn, s_up, s_gate, a, wgu, wd, gate_idx, up_idx, down_idx, small, norm_row,
                  grad_a, grad_b):
    t = dh.shape[0]
    tm = min(MM_TILE, t)
    tt = min(WGRAD_TILE, t)
    dg, du = _ffn_bwd_up(tag + "_bwd_up", dhb, wd, s_up, s_gate)
    tok = _spec((tt, D), lambda s, k: (k, 0))
    hid = _spec((None, tt, FS), lambda s, k: (s, k, 0))
    grad_b = _wgrad(tag + "_dwd", a, hid, dhb, tok, FS, D, t, tt, grad_b, down_idx, scale=MACARON)
    grad_a = _wgrad(tag + "_dwg", dg, hid, hn, tok, FS, D, t, tt, grad_a, gate_idx)
    grad_a = _wgrad(tag + "_dwu", du, hid, hn, tok, FS, D, t, tt, grad_a, up_idx)
    act = _spec((None, tm, FS), lambda i, k: (k, i, 0))
    dhn = _mm(tag + "_dhn",
              [(dg, act, wgu, _spec((None, None, FS, D), lambda i, k: (k, 0, 0, 0))),
               (du, act, wgu, _spec((None, None, FS, D), lambda i, k: (k, 1, 0, 0)))],
              grid=(t // tm, NSH), o_shape=(t, D), o_dtype=F32, o_spec=_spec((tm, D), lambda i, k: (i, 0)),
              dims=NN, kaxis=1, nk=NSH, acc_shape=(tm, D))
    dh_in, dhb_in, d_gain = _norm_bwd(tag + "_norm_bwd", dhn, h_in, small, norm_row, dh)
    return dh_in, dhb_in, grad_a, grad_b, d_gain


def _conv_fwd(proj_a, conv_w):
    t = proj_a.shape[0]
    tm = min(ROW_TILE, t)
    hb = tm // CONV_HALO

    def body(ab_ref, ac_ref, ax_ref, acp_ref, axp_ref, w_ref, y_ref):
        i = pl.program_id(1)
        u = ac_ref[...] * ax_ref[...]
        up = jnp.where(i > 0, acp_ref[...] * axp_ref[...], 0.0)
        ext = jnp.concatenate([up, u], axis=0)
        u1 = pltpu.roll(ext, 1, 0)[CONV_HALO:]
        u2 = pltpu.roll(ext, 2, 0)[CONV_HALO:]
        w = w_ref[...]
        conv = w[0:1] * u2 + w[1:2] * u1 + w[2:3] * u
        y_ref[...] = (ab_ref[...] * conv).astype(BF16)

    def cur(off):
        return _spec((tm, LANE), lambda j, i: (i, off + j))

    def prev(off):
        return _spec((CONV_HALO, LANE), lambda j, i: (jnp.maximum(i * hb - 1, 0), off + j))

    return pl.pallas_call(
        body, name="conv_fwd", grid=(4, t // tm),
        in_specs=[cur(0), cur(4), cur(8), prev(4), prev(8),
                  _spec((None, None, 8, LANE), lambda j, i: (j, 0, 0, 0))],
        out_specs=_spec((tm, LANE), lambda j, i: (i, j)),
        out_shape=jax.ShapeDtypeStruct((t, 512), BF16),
        compiler_params=_params(("parallel", "parallel")),
    )(proj_a, proj_a, proj_a, proj_a, proj_a, conv_w)


def _conv_bwd(proj_a, conv_w, dy):
    t = proj_a.shape[0]
    tm = min(ROW_TILE, t)
    hb = tm // CONV_HALO
    nt = t // tm

    def body(ab_ref, ac_ref, ax_ref, dy_ref, acp_ref, axp_ref, abn_ref, dyn_ref, w_ref,
             dab_ref, dac_ref, dax_ref, dw_ref, acc_ref):
        i = pl.program_id(1)
        ab, ac, ax = ab_ref[...], ac_ref[...], ax_ref[...]
        u = ac * ax
        up = jnp.where(i > 0, acp_ref[...] * axp_ref[...], 0.0)
        ext = jnp.concatenate([up, u], axis=0)
        u1 = pltpu.roll(ext, 1, 0)[CONV_HALO:]
        u2 = pltpu.roll(ext, 2, 0)[CONV_HALO:]
        w = w_ref[...]
        conv = w[0:1] * u2 + w[1:2] * u1 + w[2:3] * u
        dy_v = dy_ref[...]
        dab_ref[...] = (dy_v * conv).astype(BF16)
        dc = dy_v * ab
        dcn = jnp.where(i < nt - 1, dyn_ref[...] * abn_ref[...], 0.0)
        extn = jnp.concatenate([dc, dcn], axis=0)
        n = tm + CONV_HALO
        dc1 = pltpu.roll(extn, n - 1, 0)[:tm]
        dc2 = pltpu.roll(extn, n - 2, 0)[:tm]
        du = w[2:3] * dc + w[1:2] * dc1 + w[0:1] * dc2
        dac_ref[...] = (du * ax).astype(BF16)
        dax_ref[...] = (du * ac).astype(BF16)
        rid = lax.broadcasted_iota(jnp.int32, (8, LANE), 0)
        part = jnp.where(rid == 0, jnp.sum(dc * u2, axis=0, keepdims=True),
                         jnp.where(rid == 1, jnp.sum(dc * u1, axis=0, keepdims=True),
                                   jnp.where(rid == 2, jnp.sum(dc * u, axis=0, keepdims=True), 0.0)))

        @pl.when(i == 0)
        def _():
            acc_ref[...] = part

        @pl.when(i > 0)
        def _():
            acc_ref[...] += part

        @pl.when(i == nt - 1)
        def _():
            dw_ref[...] = acc_ref[...]

    def cur(off):
        return _spec((tm, LANE), lambda j, i: (i, off + j))

    def prev(off):
        return _spec((CONV_HALO, LANE), lambda j, i: (jnp.maximum(i * hb - 1, 0), off + j))

    def nxt(off):
        return _spec((CONV_HALO, LANE), lambda j, i: (jnp.minimum((i + 1) * hb, nt * hb - 1), off + j))

    outs = pl.pallas_call(
        body, name="conv_bwd", grid=(4, nt),
        in_specs=[cur(0), cur(4), cur(8), cur(0), prev(4), prev(8), nxt(0), nxt(0),
                  _spec((None, None, 8, LANE), lambda j, i: (j, 0, 0, 0))],
        out_specs=[_spec((tm, LANE), lambda j, i: (i, j)), _spec((tm, LANE), lambda j, i: (i, j)),
                   _spec((tm, LANE), lambda j, i: (i, j)), _spec((None, 8, LANE), lambda j, i: (j, 0, 0))],
        out_shape=[jax.ShapeDtypeStruct((t, 512), BF16), jax.ShapeDtypeStruct((t, 512), BF16),
                   jax.ShapeDtypeStruct((t, 512), BF16), jax.ShapeDtypeStruct((4, 8, LANE), F32)],
        scratch_shapes=[pltpu.VMEM((8, LANE), F32)],
        compiler_params=_params(("parallel", "arbitrary")),
    )(proj_a, proj_a, proj_a, dy, proj_a, proj_a, proj_a, dy, conv_w)
    return outs


def _log_sigmoid(z):
    return jnp.minimum(z, 0.0) - jnp.log(1.0 + jnp.exp(-jnp.abs(z)))


def _sb_masks():
    row = lax.broadcasted_iota(jnp.int32, (SBQ, SBQ), 0)
    col = lax.broadcasted_iota(jnp.int32, (SBQ, SBQ), 1)
    return row, col


def _ones_where(mask):
    return jnp.where(mask, 1.0, 0.0).astype(BF16)


def _head_mask(head):
    lane = lax.broadcasted_iota(jnp.int32, (1, LANE), 1)
    return lane >= 64 if head else lane < 64


def _sb_fwd(proj_b, carry=None):
    t = proj_b.shape[0]
    nq = t // SBQ
    scale = 1.0 / math.sqrt(64.0)

    npair = SB_PAIRS
    wide = npair * LANE
    ngrp = 4 // npair
    chains = [(p, head) for p in range(npair) for head in range(2)]

    grid = (ngrp, nq)
    nc = carry.n if carry is not None else 0

    def body(*refs):
        q_ref, k_ref, v_ref = refs[:3]
        y_ref, l_ref, n_ref = refs[3 + nc:6 + nc]
        riders = (refs[3:3 + nc], refs[6 + nc:6 + 2 * nc], refs[-2], refs[-1]) if nc else None
        if nc:
            carry.ride(grid, riders, "start")
        grp = pl.program_id(0)
        qi = pl.program_id(1)
        row, col = _sb_masks()
        m_suffix = _ones_where(row > col)
        rows = len(chains) * SBQ
        strict = (lax.broadcasted_iota(jnp.int32, (rows, SBQ), 1)
                  < (lax.broadcasted_iota(jnp.int32, (rows, SBQ), 0) & (SBQ - 1)))
        q_pair = []
        for p in range(npair):
            q_all = q_ref[:, p * LANE:(p + 1) * LANE]
            q_pair.append(jnp.concatenate([jnp.where(_head_mask(head), q_all, jnp.zeros_like(q_all)) for head in range(2)],
                                          axis=0))

        def block(kb, state, diag):
            run, acc = state
            start = pl.multiple_of(kb * SBQ, SBQ)
            z = jnp.concatenate([_dg(q_pair[p], k_ref[pl.ds(start, SBQ), p * LANE:(p + 1) * LANE], NT)
                                 for p in range(npair)], axis=0) * scale
            lb = _log_sigmoid(z)
            lk = lb - z
            if diag:
                lk = jnp.where(strict, lk, 0.0)
            hi, lo = _split(lk)
            sums = _dg(jnp.concatenate([hi, lo], axis=0), m_suffix, NN)
            w = jnp.exp(lb + (run + sums[:rows] + sums[rows:]))
            if diag:
                w = jnp.where(strict, w, 0.0)
            wb = w.astype(BF16)
            acc = acc + jnp.concatenate(
                [_dg(wb[2 * p * SBQ:2 * (p + 1) * SBQ], v_ref[pl.ds(start, SBQ), p * LANE:(p + 1) * LANE], NN)
                 for p in range(npair)], axis=0)
            run = run + jnp.sum(hi.astype(F32) + lo.astype(F32), axis=1, keepdims=True)
            return run, acc

        state = block(qi, (jnp.zeros((rows, 1), F32), jnp.zeros((rows, LANE), F32)), True)

        def live(c):
            return jnp.logical_and(c[0] < qi, jnp.max(c[1][0]) > SB_DEAD)

        def step(c):
            return c[0] + 1, block(qi - 1 - c[0], c[1], False)

        count, (run, acc) = lax.while_loop(live, step, (jnp.int32(0), state))
        n_ref[grp * nq + qi] = count.astype(F32)
        hm = _head_mask(0)
        for p in range(npair):
            lo_rows, hi_rows = slice(2 * p * SBQ, (2 * p + 1) * SBQ), slice((2 * p + 1) * SBQ, (2 * p + 2) * SBQ)
            y_ref[:, p * LANE:(p + 1) * LANE] = jnp.where(hm, acc[lo_rows], acc[hi_rows]).astype(BF16)
            l_ref[p] = jnp.where(hm, run[lo_rows], run[hi_rows])
        if nc:
            carry.ride(grid, riders, "finish")

    outs = pl.pallas_call(
        body, name="sb_fwd", grid=grid,
        in_specs=[_spec((SBQ, wide), lambda g, i: (i, g)),
                  _spec((t, wide), lambda g, i: (0, ngrp + g)),
                  _spec((t, wide), lambda g, i: (0, 2 * ngrp + g))] + (carry.in_specs if nc else []),
        out_specs=[_spec((SBQ, wide), lambda g, i: (i, g)), _spec((npair, SBQ, LANE), lambda g, i: (g, i, 0)),
                   pl.BlockSpec(memory_space=pltpu.SMEM)] + (carry.out_specs if nc else []),
        out_shape=[jax.ShapeDtypeStruct((t, 512), BF16), jax.ShapeDtypeStruct((4, t, LANE), F32),
                   jax.ShapeDtypeStruct((ngrp * nq,), F32)] + (carry.out_shape if nc else []),
        scratch_shapes=carry.scratch if nc else [],
        compiler_params=_params(("arbitrary", "arbitrary")),
    )(proj_b, proj_b, proj_b, *(carry.groups if nc else []))
    return outs[0], outs[1], outs[2], (carry.place(outs[3:]) if nc else [])


def _sb_bwd(proj_b, dy, ltot, nblk, rider=None):
    t = proj_b.shape[0]
    nq = t // SBQ
    scale = 1.0 / math.sqrt(64.0)

    npair = SB_PAIRS
    wide = npair * LANE
    ngrp = 4 // npair
    chains = [(p, head) for p in range(npair) for head in range(2)]
    grid = (ngrp, nq)
    nr = rider.n if rider is not None else 0

    def body(*refs):
        q_ref, k_ref, v_ref, dy_ref, l_ref, n_ref = refs[:6]
        dq_ref, dk_ref, dv_ref = refs[6 + nr:9 + nr]
        dk_acc, dv_acc = refs[9 + 2 * nr:11 + 2 * nr]
        riders = (refs[6:6 + nr], refs[9 + nr:9 + 2 * nr], refs[-2], refs[-1]) if nr else None
        if nr:
            rider.ride(grid, riders, "start")
        grp = pl.program_id(0)
        qi = pl.program_id(1)

        @pl.when(qi == 0)
        def _():
            dk_acc[...] = jnp.zeros_like(dk_acc)
            dv_acc[...] = jnp.zeros_like(dv_acc)

        row, col = _sb_masks()
        m_prefix = _ones_where(row <= col)
        m_before = _ones_where(row < col)
        rows = len(chains) * SBQ
        strict = (lax.broadcasted_iota(jnp.int32, (rows, SBQ), 1)
                  < (lax.broadcasted_iota(jnp.int32, (rows, SBQ), 0) & (SBQ - 1)))
        q_pair, do_pair, ltot = [], [], []
        for p in range(npair):
            pl_ = slice(p * LANE, (p + 1) * LANE)
            q_all = q_ref[:, pl_]
            do_all = dy_ref[:, pl_].astype(BF16)
            q_pair.append(jnp.concatenate([jnp.where(_head_mask(h), q_all, jnp.zeros_like(q_all)) for h in range(2)], axis=0))
            do_pair.append(jnp.concatenate([jnp.where(_head_mask(h), do_all, jnp.zeros_like(do_all)) for h in range(2)], axis=0))
            ltot += [l_ref[p][:, h * 64:h * 64 + 1] for h in range(2)]
        ltot = jnp.concatenate(ltot, axis=0)

        def pair_rows(a, p):
            return a[2 * p * SBQ:2 * (p + 1) * SBQ]

        def block(kb, state, diag):
            seen, dseen, dq = state
            start = pl.multiple_of(kb * SBQ, SBQ)
            kk = [k_ref[pl.ds(start, SBQ), p * LANE:(p + 1) * LANE] for p in range(npair)]
            vv = [v_ref[pl.ds(start, SBQ), p * LANE:(p + 1) * LANE] for p in range(npair)]
            z = jnp.concatenate([_dg(q_pair[p], kk[p], NT) for p in range(npair)], axis=0) * scale
            lb = _log_sigmoid(z)
            lk = lb - z
            if diag:
                lk = jnp.where(strict, lk, 0.0)
            hi, lo = _split(lk)
            sums = _dg(jnp.concatenate([hi, lo], axis=0), m_prefix, NN)
            w = jnp.exp(lb + ((ltot - seen) - (sums[:rows] + sums[rows:])))
            if diag:
                w = jnp.where(strict, w, 0.0)
            wb = w.astype(BF16)
            da = w * jnp.concatenate([_dg(do_pair[p], vv[p], NT) for p in range(npair)], axis=0)
            dah, dal = _split(da)
            dsums = _dg(jnp.concatenate([dah, dal], axis=0), m_before, NN)
            sig = jnp.exp(lb)
            dz = (da * (1.0 - sig) - (dseen + dsums[:rows] + dsums[rows:]) * sig) * scale
            if diag:
                dz = jnp.where(strict, dz, 0.0)
            dzb = dz.astype(BF16)
            for p in range(npair):
                pl_ = slice(p * LANE, (p + 1) * LANE)
                dv_acc[pl.ds(start, SBQ), pl_] += _dg(pair_rows(wb, p), do_pair[p], TN)
                dk_acc[pl.ds(start, SBQ), pl_] += _dg(pair_rows(dzb, p), q_pair[p], TN)
            dq = dq + jnp.concatenate([_dg(pair_rows(dzb, p), kk[p], NN) for p in range(npair)], axis=0)
            seen = seen + jnp.sum(hi.astype(F32) + lo.astype(F32), axis=1, keepdims=True)
            dseen = dseen + jnp.sum(da, axis=1, keepdims=True)
            return seen, dseen, dq

        zero = (jnp.zeros((rows, 1), F32), jnp.zeros((rows, 1), F32), jnp.zeros((rows, LANE), F32))
        first = qi - n_ref[grp * nq + qi].astype(jnp.int32)
        state = lax.fori_loop(first, qi, lambda kb, c: block(kb, c, False), zero)
        _, _, dq = block(qi, state, True)
        for p in range(npair):
            dq_ref[:, p * LANE:(p + 1) * LANE] = jnp.where(
                _head_mask(0), dq[2 * p * SBQ:(2 * p + 1) * SBQ], dq[(2 * p + 1) * SBQ:(2 * p + 2) * SBQ]).astype(BF16)

        @pl.when(qi == nq - 1)
        def _():
            dk_ref[...] = dk_acc[...].astype(BF16)
            dv_ref[...] = dv_acc[...].astype(BF16)

        if nr:
            rider.ride(grid, riders, "finish")

    full = jax.ShapeDtypeStruct((t, 512), BF16)
    outs = pl.pallas_call(
        body, name="sb_bwd", grid=grid,
        in_specs=[_spec((SBQ, wide), lambda g, i: (i, g)),
                  _spec((t, wide), lambda g, i: (0, ngrp + g)),
                  _spec((t, wide), lambda g, i: (0, 2 * ngrp + g)),
                  _spec((SBQ, wide), lambda g, i: (i, ngrp + g)),
                  _spec((npair, SBQ, LANE), lambda g, i: (g, i, 0)),
                  pl.BlockSpec(memory_space=pltpu.SMEM)] + (rider.in_specs if nr else []),
        out_specs=[_spec((SBQ, wide), lambda g, i: (i, g)),
                   _spec((t, wide), lambda g, i: (0, g)), _spec((t, wide), lambda g, i: (0, g))]
        + (rider.out_specs if nr else []),
        out_shape=[full, full, full] + (rider.out_shape if nr else []),
        scratch_shapes=[pltpu.VMEM((t, wide), F32), pltpu.VMEM((t, wide), F32)] + (rider.scratch if nr else []),
        compiler_params=_params(("arbitrary", "arbitrary")),
    )(proj_b, proj_b, proj_b, dy, ltot, nblk, *(rider.groups if nr else []))
    return outs[0], outs[1], outs[2], (rider.place(outs[3:]) if nr else [])


def _hgrn_gates(qr, fr, c0, c1):
    mx = jnp.maximum(c0, c1)
    e0, e1 = jnp.exp(c0 - mx), jnp.exp(c1 - mx)
    lb = e1 / (e0 + e1)
    sx = _sigmoid(fr)
    f = lb + (1.0 - lb) * sx
    k = (1.0 - lb) * (1.0 - sx)
    sq = _sigmoid(qr)
    return lb, sx, f, k, sq, qr * sq


def _hgrn_decays(f):
    row = lax.broadcasted_iota(jnp.int32, (CHUNK, CHUNK), 0)
    col = lax.broadcasted_iota(jnp.int32, (CHUNK, CHUNK), 1)
    tril = col <= row
    hi, lo = _split(jnp.log(f))
    trib = _ones_where(tril)
    b = _dg(trib, hi, NN) + _dg(trib, lo, NN)
    bm = b[CHUNK // 2 - 1:CHUNK // 2]
    bl = b[CHUNK - 1:CHUNK]
    return tril, col >= row, b, bm, bl


def _hgrn_fwd(proj_c, small, carry=None):
    t = proj_c.shape[1]
    nc = t // CHUNK
    nh = D // HD

    hps = HGRN_HPS
    wide = hps * HD
    grid = (nh // hps, nc)
    nr = carry.n if carry is not None else 0

    def body(*refs):
        p_ref, c0_ref, c1_ref, gam_ref = refs[:4]
        o_ref, y_ref, sst_ref = refs[4 + nr:7 + nr]
        st_ref = refs[7 + 2 * nr]
        riders = (refs[4:4 + nr], refs[7 + nr:7 + 2 * nr], refs[-2], refs[-1]) if nr else None
        if nr:
            carry.ride(grid, riders, "start")
        c = pl.program_id(1)

        @pl.when(c == 0)
        def _():
            st_ref[...] = jnp.zeros_like(st_ref)

        for j in range(hps):
            ln = slice(j * HD, (j + 1) * HD)
            st0 = st_ref[j]
            sst_ref[j] = st0
            qr, fr, v, g = p_ref[0, :, ln], p_ref[1, :, ln], p_ref[2, :, ln], p_ref[3, :, ln]
            lb, sx, f, k, sq, q = _hgrn_gates(qr, fr, c0_ref[:, ln], c1_ref[:, ln])
            tril, _, b, bm, bl = _hgrn_decays(f)
            qd = q * jnp.exp(b)
            qt = q * jnp.exp(b - bm)
            kt = k * jnp.exp(bm - b)
            kl = k * jnp.exp(bl - b)
            vb = v.astype(BF16)
            att = jnp.where(tril, _dot3(qt, kt, NT), 0.0)
            o = _dg(qd.astype(BF16), st0.astype(BF16), NT) + _dg(att.astype(BF16), vb, NN)
            st_ref[j] = st0 * jnp.exp(bl) + _dg(vb, kl.astype(BF16), TN)
            o_ref[:, ln] = o
            r = lax.rsqrt(jnp.mean(o * o, axis=-1, keepdims=True) + RMS_EPS)
            y_ref[:, ln] = (o * r * gam_ref[...] * (g * _sigmoid(g))).astype(BF16)
        if nr:
            carry.ride(grid, riders, "finish")

    outs = pl.pallas_call(
        body, name="hgrn_fwd", grid=grid,
        in_specs=[_spec((4, CHUNK, wide), lambda h, c: (0, c, h)),
                  _spec((None, 1, wide), lambda h, c: (R_CLB, 0, h)),
                  _spec((None, 1, wide), lambda h, c: (R_CLB + 1, 0, h)),
                  _spec((None, 1, HD), lambda h, c: (R_GAM, 0, 0))] + (carry.in_specs if nr else []),
        out_specs=[_spec((CHUNK, wide), lambda h, c: (c, h)), _spec((CHUNK, wide), lambda h, c: (c, h)),
                   _spec((None, hps, HD, HD), lambda h, c: (c, h, 0, 0))] + (carry.out_specs if nr else []),
        out_shape=[jax.ShapeDtypeStruct((t, D), F32), jax.ShapeDtypeStruct((t, D), BF16),
                   jax.ShapeDtypeStruct((nc, nh, HD, HD), F32)] + (carry.out_shape if nr else []),
        scratch_shapes=[pltpu.VMEM((hps, HD, HD), F32)] + (carry.scratch if nr else []),
        compiler_params=_params(("arbitrary", "arbitrary")),
    )(proj_c, small, small, small, *(carry.groups if nr else []))
    return outs[0], outs[1], outs[2], (carry.place(outs[3:]) if nr else [])


def _hgrn_bwd(proj_c, small, o, sst, dyc, rider=None):
    t = proj_c.shape[1]
    nc = t // CHUNK
    nh = D // HD

    hps = HGRN_HPS
    wide = hps * HD
    ng = nh // hps
    grid = (ng, nc)
    nr = rider.n if rider is not None else 0

    def body(*refs):
        p_ref, c0_ref, c1_ref, gam_ref, o_ref, sst_ref, dy_ref = refs[:7]
        dp_ref, dclb_ref, dgam_ref = refs[7 + nr:10 + nr]
        dst_ref, dlb_acc, dgam_acc = refs[10 + 2 * nr:13 + 2 * nr]
        riders = (refs[7:7 + nr], refs[10 + nr:10 + 2 * nr], refs[-2], refs[-1]) if nr else None
        if nr:
            rider.ride(grid, riders, "start")
        group = pl.program_id(0)
        step = pl.program_id(1)

        @pl.when(step == 0)
        def _():
            dst_ref[...] = jnp.zeros_like(dst_ref)
            dlb_acc[...] = jnp.zeros_like(dlb_acc)

        @pl.when((step == 0) & (group == 0))
        def _():
            dgam_acc[...] = jnp.zeros_like(dgam_acc)

        gam = gam_ref[...]
        dlb_parts, dgam_parts, lbs = [], [], []
        for j in range(hps):
            ln = slice(j * HD, (j + 1) * HD)
            st0 = sst_ref[j]
            dst1 = dst_ref[j]
            qr, fr, v, g = p_ref[0, :, ln], p_ref[1, :, ln], p_ref[2, :, ln], p_ref[3, :, ln]
            lb, sx, f, k, sq, q = _hgrn_gates(qr, fr, c0_ref[:, ln], c1_ref[:, ln])
            tril, triu, b, bm, bl = _hgrn_decays(f)
            eb = jnp.exp(b)
            e_qt = jnp.exp(b - bm)
            e_kt = jnp.exp(bm - b)
            e_kl = jnp.exp(bl - b)
            e_bl = jnp.exp(bl)
            qd, qt, kt, kl = q * eb, q * e_qt, k * e_kt, k * e_kl
            ov = o_ref[:, ln]
            r = lax.rsqrt(jnp.mean(ov * ov, axis=-1, keepdims=True) + RMS_EPS)
            oh = ov * r
            sg = _sigmoid(g)
            dy = dy_ref[:, ln]
            dp_ref[3, :, ln] = (dy * oh * gam * (sg * (1.0 + g * (1.0 - sg)))).astype(BF16)
            dyv = dy * (g * sg)
            gdy = dyv * gam
            do = (gdy - oh * jnp.mean(gdy * oh, axis=-1, keepdims=True)) * r
            dob, vb = do.astype(BF16), v.astype(BF16)
            st0b, dst1b = st0.astype(BF16), dst1.astype(BF16)
            st1 = st0 * e_bl + _dg(vb, kl.astype(BF16), TN)
            att = jnp.where(tril, _dot3(qt, kt, NT), 0.0)
            datt = jnp.where(tril, _dg(dob, vb, NT), 0.0)
            dv = _dg(att.astype(BF16), dob, TN) + _dg(kl.astype(BF16), dst1b, NT)
            dq = _dot3(datt, kt, NN) * e_qt + _dg(dob, st0b, NN) * eb
            dk = _dot3(datt, qt, TN) * e_kt + _dg(vb, dst1b, NN) * e_kl
            db = q * dq - k * dk
            last = lax.broadcasted_iota(jnp.int32, (CHUNK, 1), 0) == CHUNK - 1
            db = db + jnp.where(last, jnp.sum(dst1 * st1, axis=0, keepdims=True), 0.0)
            dbh, dbl = _split(db)
            triub = _ones_where(triu)
            dlf = _dg(triub, dbh, NN) + _dg(triub, dbl, NN)
            dst_ref[j] = dst1 * e_bl + _dg(dob, qd.astype(BF16), TN)
            dp_ref[0, :, ln] = (dq * (sq * (1.0 + qr * (1.0 - sq)))).astype(BF16)
            tmp = dlf / f - dk
            dp_ref[1, :, ln] = (tmp * (1.0 - lb) * sx * (1.0 - sx)).astype(BF16)
            dp_ref[2, :, ln] = dv.astype(BF16)
            dlb_parts.append(jnp.sum((1.0 - sx) * tmp, axis=0, keepdims=True))
            dgam_parts.append(jnp.sum(dyv * oh, axis=0, keepdims=True))
            lbs.append(lb)

        dlb_acc[...] += jnp.concatenate(dlb_parts, axis=1)
        dgam_acc[...] += functools.reduce(lambda a, b: a + b, dgam_parts)

        @pl.when(step == nc - 1)
        def _():
            lb_all = jnp.concatenate(lbs, axis=1)
            d1 = dlb_acc[...] * lb_all * (1.0 - lb_all)
            dclb_ref[...] = jnp.where(lax.broadcasted_iota(jnp.int32, (2, wide), 0) == 0, -d1, d1)

        @pl.when((step == nc - 1) & (group == ng - 1))
        def _():
            dgam_ref[...] = dgam_acc[...]

        if nr:
            rider.ride(grid, riders, "finish")

    rev = lambda h, s: (nc - 1 - s, h)
    outs = pl.pallas_call(
        body, name="hgrn_bwd", grid=grid,
        in_specs=[_spec((4, CHUNK, wide), lambda h, s: (0, nc - 1 - s, h)),
                  _spec((None, 1, wide), lambda h, s: (R_CLB, 0, h)),
                  _spec((None, 1, wide), lambda h, s: (R_CLB + 1, 0, h)),
                  _spec((None, 1, HD), lambda h, s: (R_GAM, 0, 0)),
                  _spec((CHUNK, wide), rev),
                  _spec((None, hps, HD, HD), lambda h, s: (nc - 1 - s, h, 0, 0)),
                  _spec((CHUNK, wide), rev)] + (rider.in_specs if nr else []),
        out_specs=[_spec((4, CHUNK, wide), lambda h, s: (0, nc - 1 - s, h)),
                   _spec((2, wide), lambda h, s: (0, h)),
                   _spec((1, HD), lambda h, s: (0, 0))] + (rider.out_specs if nr else []),
        out_shape=[jax.ShapeDtypeStruct((4, t, D), BF16), jax.ShapeDtypeStruct((2, D), F32),
                   jax.ShapeDtypeStruct((1, HD), F32)] + (rider.out_shape if nr else []),
        scratch_shapes=[pltpu.VMEM((hps, HD, HD), F32), pltpu.VMEM((1, wide), F32), pltpu.VMEM((1, HD), F32)]
        + (rider.scratch if nr else []),
        compiler_params=_params(("arbitrary", "arbitrary")),
    )(proj_c, small, small, small, o, sst, dyc, *(rider.groups if nr else []))
    return outs[0], outs[1], outs[2], (rider.place(outs[3:]) if nr else [])


def _adamw(name, w, g, m, v):
    rows, cols = w.shape
    br = rows
    for cand in (512, 352, 256):
        if rows % cand == 0:
            br = cand
            break
    c1 = 1.0 - ADAM_B1 ** ADAM_STEP
    c2 = 1.0 - ADAM_B2 ** ADAM_STEP

    def body(w_ref, g_ref, m_ref, v_ref, d_ref, mo_ref, vo_ref):
        gv = g_ref[...]
        mn = ADAM_B1 * m_ref[...] + (1.0 - ADAM_B1) * gv
        vn = ADAM_B2 * v_ref[...] + (1.0 - ADAM_B2) * (gv * gv)
        mo_ref[...] = mn
        vo_ref[...] = vn
        d_ref[...] = -ADAM_LR * ((mn / c1) / (jnp.sqrt(vn / c2) + ADAM_EPS) + ADAM_WD * w_ref[...])

    blk = _spec((br, cols), lambda i: (i, 0))
    shape = jax.ShapeDtypeStruct((rows, cols), F32)
    return pl.pallas_call(
        body, name=name, grid=(rows // br,), in_specs=[blk] * 4, out_specs=[blk] * 3, out_shape=[shape] * 3,
        compiler_params=_params(("parallel",)),
    )(w, g, m, v)


def _place():
    x, y, c = lax.axis_index("x"), lax.axis_index("y"), lax.axis_index("c")
    chips = [(1 - x, y), (x, 1 - y), (1 - x, 1 - y)]
    return x, y, c, chips


class _Rider:
    n = 0

    def ride(self, grid, refs, when):
        if not self.n:
            return
        ids = [pl.program_id(a) for a in range(len(grid))]
        edge = [i == (0 if when == "start" else g - 1) for i, g in zip(ids, grid)]
        cond = functools.reduce(jnp.logical_and, edge)

        @pl.when(cond)
        def _():
            (self.start if when == "start" else self.finish)(*refs)


class _Gather(_Rider):
    def __init__(self, groups):
        self.groups = list(groups)
        self.n = len(self.groups)
        self.in_specs = [ANY] * self.n
        self.out_specs = [ANY] * self.n
        self.out_shape = [jax.ShapeDtypeStruct((NSH,) + g.shape, g.dtype) for g in self.groups]
        self.scratch = [pltpu.SemaphoreType.DMA((6 * self.n,)), pltpu.SemaphoreType.DMA((6 * self.n,))] if self.n else []

    def _copies(self, ins, outs, send, recv):
        x, y, c, chips = _place()
        sibling = (x, y, 1 - c)

        def half(gi, chip, hc):
            return outs[gi].at[2 * chip[0] + chip[1], hc]

        def copy(gi, k, src, dst, to):
            return pltpu.make_async_remote_copy(src_ref=src, dst_ref=dst, send_sem=send.at[6 * gi + k],
                                                recv_sem=recv.at[6 * gi + k], device_id=to, device_id_type=MESH)

        pairs = [(gi, j, chip) for gi in range(self.n) for j, chip in enumerate(chips)]
        first = [copy(gi, j, ins[gi].at[c], half(gi, (x, y), c), (*chip, c)) for gi, j, chip in pairs]
        landed = [copy(gi, j, half(gi, chip, c), half(gi, chip, c), sibling) for gi, j, chip in pairs]
        relay = [copy(gi, 3 + j, half(gi, chip, c), half(gi, chip, c), sibling) for gi, j, chip in pairs]
        relayed = [copy(gi, 3 + j, half(gi, chip, 1 - c), half(gi, chip, 1 - c), sibling) for gi, j, chip in pairs]
        return first, landed, relay, relayed

    def start(self, ins, outs, send, recv):
        for cp in self._copies(ins, outs, send, recv)[0]:
            cp.start()

    def finish(self, ins, outs, send, recv):
        first, landed, relay, relayed = self._copies(ins, outs, send, recv)
        for arrived, onward in zip(landed, relay):
            arrived.wait_recv()
            onward.start()
        for cp in relayed:
            cp.wait_recv()
        for cp in first + relay:
            cp.wait_send()

    def place(self, outs):
        me = 2 * lax.axis_index("x") + lax.axis_index("y")
        return [lax.dynamic_update_index_in_dim(o, g, me, 0) for o, g in zip(outs, self.groups)]


def _gather_weights(groups):
    gather = _Gather(groups)
    ng = gather.n

    def body(*refs):
        parts = (refs[:ng], refs[ng:2 * ng], refs[2 * ng], refs[2 * ng + 1])
        gather.start(*parts)
        gather.finish(*parts)

    outs = pl.pallas_call(
        body, name="gather_weights", in_specs=gather.in_specs, out_specs=gather.out_specs, out_shape=gather.out_shape,
        scratch_shapes=gather.scratch, compiler_params=pltpu.CompilerParams(has_side_effects=True),
    )(*groups)
    return gather.place(outs)


def _swap_halves(name, slots):
    n = len(slots)

    def body(*refs):
        ins, outs = refs[:n], refs[n:2 * n]
        send, recv = refs[2 * n:]
        x, y, c, _ = _place()
        cps = []
        for i, (_, row0, rows) in enumerate(slots):
            half = rows // 2
            src = ins[i].at[:, pl.ds(pl.multiple_of(row0 + (1 - c) * half, 16), half)]
            cps.append(pltpu.make_async_remote_copy(src_ref=src, dst_ref=outs[i], send_sem=send.at[i],
                                                    recv_sem=recv.at[i], device_id=(x, y, 1 - c), device_id_type=MESH))
        for cp in cps:
            cp.start()
        for cp in cps:
            cp.wait()

    return pl.pallas_call(
        body, name=name, in_specs=[ANY] * n, out_specs=[ANY] * n,
        out_shape=[jax.ShapeDtypeStruct((NSH, rows // 2, buf.shape[2]), buf.dtype) for buf, _, rows in slots],
        scratch_shapes=[pltpu.SemaphoreType.DMA((n,)), pltpu.SemaphoreType.DMA((n,))],
        compiler_params=pltpu.CompilerParams(has_side_effects=True),
    )(*[buf for buf, _, _ in slots])


class _Send(_Rider):
    def __init__(self, arrays):
        self.groups = list(arrays)
        self.n = len(self.groups)
        self.in_specs = [ANY] * self.n
        self.out_specs = [ANY] * self.n
        self.out_shape = [jax.ShapeDtypeStruct((3,) + g.shape[1:], g.dtype) for g in self.groups]
        self.scratch = [pltpu.SemaphoreType.DMA((3 * self.n,)), pltpu.SemaphoreType.DMA((3 * self.n,))]

    def _copies(self, ins, outs, send, recv):
        x, y, c, chips = _place()
        return [pltpu.make_async_remote_copy(src_ref=ins[gi].at[2 * chip[0] + chip[1]], dst_ref=outs[gi].at[j],
                                             send_sem=send.at[3 * gi + j], recv_sem=recv.at[3 * gi + j],
                                             device_id=(*chip, c), device_id_type=MESH)
                for gi in range(self.n) for j, chip in enumerate(chips)]

    def start(self, ins, outs, send, recv):
        for cp in self._copies(ins, outs, send, recv):
            cp.start()

    def finish(self, ins, outs, send, recv):
        for cp in self._copies(ins, outs, send, recv):
            cp.wait()

    def place(self, outs):
        return list(outs)


def _send_to_owners(name, arrays):
    rider = _Send(arrays)
    n = rider.n

    def body(*refs):
        parts = (refs[:n], refs[n:2 * n], refs[2 * n], refs[2 * n + 1])
        rider.start(*parts)
        rider.finish(*parts)

    return pl.pallas_call(
        body, name=name, in_specs=rider.in_specs, out_specs=rider.out_specs, out_shape=rider.out_shape,
        scratch_shapes=rider.scratch, compiler_params=pltpu.CompilerParams(has_side_effects=True),
    )(*arrays)


def _share_halves(groups):
    ng = len(groups)

    def body(*refs):
        ins, outs = refs[:ng], refs[ng:2 * ng]
        send, recv = refs[2 * ng:]
        x, y, c, _ = _place()
        cps = [pltpu.make_async_remote_copy(src_ref=ins[gi], dst_ref=outs[gi].at[c], send_sem=send.at[gi],
                                            recv_sem=recv.at[gi], device_id=(x, y, 1 - c), device_id_type=MESH)
               for gi in range(ng)]
        for cp in cps:
            cp.start()
        for gi in range(ng):
            pltpu.make_async_remote_copy(src_ref=ins[gi], dst_ref=outs[gi].at[1 - c], send_sem=send.at[gi],
                                         recv_sem=recv.at[gi], device_id=(x, y, 1 - c), device_id_type=MESH).wait_recv()
        for cp in cps:
            cp.wait_send()

    outs = pl.pallas_call(
        body, name="grad_share_halves", in_specs=[ANY] * ng, out_specs=[ANY] * ng,
        out_shape=[jax.ShapeDtypeStruct((2,) + g.shape, g.dtype) for g in groups],
        scratch_shapes=[pltpu.SemaphoreType.DMA((ng,)), pltpu.SemaphoreType.DMA((ng,))],
        compiler_params=pltpu.CompilerParams(has_side_effects=True),
    )(*groups)
    c = lax.axis_index("c")
    return [lax.dynamic_update_index_in_dim(o, g, c, 0) for o, g in zip(outs, groups)]


def _pair_sum(name, slots, got, c_idx):
    n = len(slots)
    in_specs, out_specs, out_shape, operands = [], [], [], []
    for (buf, row0, rows), g in zip(slots, got):
        hb, cols = rows // 4, buf.shape[2]
        in_specs += [pl.BlockSpec((None, hb, cols), lambda q, i, cr, r=row0 // hb: (q, r + 2 * cr[0] + i, 0)),
                     pl.BlockSpec((None, hb, cols), lambda q, i, cr: (q, i, 0))]
        out_specs.append(pl.BlockSpec((None, hb, cols), lambda q, i, cr: (q, i, 0)))
        out_shape.append(jax.ShapeDtypeStruct(g.shape, BF16))
        operands += [buf, g]

    def body(c_ref, *refs):
        for i in range(n):
            refs[2 * n + i][...] = (refs[2 * i][...].astype(F32) + refs[2 * i + 1][...].astype(F32)).astype(BF16)

    return pl.pallas_call(
        body, name=name,
        grid_spec=pltpu.PrefetchScalarGridSpec(num_scalar_prefetch=1, grid=(NSH, 2), in_specs=in_specs, out_specs=out_specs),
        out_shape=out_shape, compiler_params=_params(("parallel", "parallel")),
    )(c_idx, *operands)


def _owner_sum(name, pairs, got, p_idx):
    n = len(pairs)
    in_specs, out_specs, out_shape, operands = [], [], [], []
    for own, g in zip(pairs, got):
        _, rows, cols = own.shape
        hb = rows // 2
        in_specs += [pl.BlockSpec((None, hb, cols), lambda i, pr: (pr[0], i, 0)),
                     pl.BlockSpec((3, hb, cols), lambda i, pr: (0, i, 0))]
        out_specs.append(pl.BlockSpec((hb, cols), lambda i, pr: (i, 0)))
        out_shape.append(jax.ShapeDtypeStruct((rows, cols), F32))
        operands += [own, g]

    def body(p_ref, *refs):
        for i in range(n):
            a_ref, b_ref = refs[2 * i], refs[2 * i + 1]
            refs[2 * n + i][...] = ((a_ref[...].astype(F32) + b_ref[0].astype(F32)) + b_ref[1].astype(F32)) + b_ref[2].astype(F32)

    return pl.pallas_call(
        body, name=name,
        grid_spec=pltpu.PrefetchScalarGridSpec(num_scalar_prefetch=1, grid=(2,), in_specs=in_specs, out_specs=out_specs),
        out_shape=out_shape, compiler_params=_params(("parallel",)),
    )(p_idx, *operands)


def _sum_small(slab):
    def body(in_ref, out_ref, all_ref, send, recv):
        x, y, c, _ = _place()
        me = 4 * x + 2 * y + c
        all_ref[me] = in_ref[...]
        cps = []
        for k in range(1, 8):
            peer = (x ^ (k >> 2), y ^ ((k >> 1) & 1), c ^ (k & 1))
            cps.append(pltpu.make_async_remote_copy(src_ref=in_ref, dst_ref=all_ref.at[me], send_sem=send.at[k - 1],
                                                    recv_sem=recv.at[k - 1], device_id=peer, device_id_type=MESH))
        for cp in cps:
            cp.start()
        for cp in cps:
            cp.wait()
        total = all_ref[0]
        for d in range(1, 8):
            total = total + all_ref[d]
        out_ref[...] = total

    return pl.pallas_call(
        body, name="sum_small",
        in_specs=[pl.BlockSpec(memory_space=pltpu.VMEM)], out_specs=pl.BlockSpec(memory_space=pltpu.VMEM),
        out_shape=jax.ShapeDtypeStruct(slab.shape, F32),
        scratch_shapes=[pltpu.VMEM((8,) + slab.shape, F32), pltpu.SemaphoreType.DMA((7,)), pltpu.SemaphoreType.DMA((7,))],
        compiler_params=pltpu.CompilerParams(has_side_effects=True),
    )(slab)


FFNS = ("pre0", "post0", "pre1", "post1")
W_SHAPES = dict({f + "_gu": (NSH, 2, FS, D) for f in FFNS}, **{f + "_d": (NSH, FS, D) for f in FFNS},
                ab_in=(NSH, D, 768), ab_out=(NSH, 256, D), conv=(NSH, 2, 8, LANE), c_in=(NSH, D, D), c_out=(NSH, 256, D))
CARRIED = dict(pre0_up=("pre0_d",), pre0_down=("ab_in", "ab_out", "conv"),
               sb_fwd=("post0_gu", "post0_d", "pre1_gu"), post0_up=("c_in", "c_out"), post0_down=("pre1_d",),
               hgrn_fwd=("post1_gu", "post1_d"))


FFN_SLOTS = dict(pre0=(0, 2, 0), pre1=(1, 3, 1), post0=(4, 6, 2), post1=(5, 7, 3))
GRAD_SLOTS = dict(ab_out=("b", B_ABOUT, 256), c_in=("b", B_CIN, D), c_out=("b", B_COUT, 256), ab_in=("c", 0, D))
for _f, (_g, _u, _d) in FFN_SLOTS.items():
    GRAD_SLOTS.update({_f + "_g": ("a", _g * FS, FS), _f + "_u": ("a", _u * FS, FS), _f + "_d": ("b", _d * FS, FS)})
REDUCE_STAGES = dict(x=("post1_g", "post1_u", "post1_d", "c_out"),
                     y=("c_in", "pre1_g", "pre1_u", "pre1_d", "post0_g", "post0_u", "post0_d", "ab_out"),
                     z=("ab_in", "pre0_g", "pre0_u", "pre0_d"))


def _local_step(x, target, weights, small, shards=None, place=None):
    t = x.shape[0]
    tm = min(MM_TILE, t)
    tw = min(WGRAD_TILE, t)
    nt = t // tm
    tok_si = _spec((tm, D), lambda s, i: (i, 0))
    w = dict(weights)
    reduced = {}

    def reduce_start(stage, buffers):
        if place is None:
            return (), [], None
        names = REDUCE_STAGES[stage]
        slots = [(buffers[GRAD_SLOTS[n][0]],) + GRAD_SLOTS[n][1:] for n in names]
        got = _swap_halves("grad_swap_" + stage, slots)
        pairs = _pair_sum("grad_pair_sum_" + stage, slots, got, place[0])
        return names, pairs, _Send(pairs)

    def reduce_end(stage, names, pairs, landed):
        if place is not None:
            reduced.update(zip(names, _owner_sum("grad_owner_sum_" + stage, pairs, landed, place[1])))

    def carried(kernel_name):
        names = [n for n in CARRIED[kernel_name] if n not in w]
        return names, (_Gather([shards[n] for n in names]) if names else None)

    def land(names, arrays):
        for n, a in zip(names, arrays):
            w[n] = a.reshape(W_SHAPES[n])

    def ffn_forward(tag, h, norm_row):
        hn = _norm_fwd(tag + "_norm", h, small, norm_row)
        names, gather = carried(tag + "_up") if tag + "_up" in CARRIED else ([], None)
        s_up, s_gate, a, got = _ffn_up(tag + "_up", hn, w[tag + "_gu"], gather)
        land(names, got)
        names, gather = carried(tag + "_down") if tag + "_down" in CARRIED else ([], None)
        out = _ffn_down(tag + "_down", a, w[tag + "_d"], h, gather)
        if gather is not None:
            out, got = out
            land(names, got)
        return out, (h, hn, s_up, s_gate, a)

    def out_proj(name, y, w_out, h):
        return _mm(name, [(y, _spec((tm, 256), lambda i, k: (i, k)), w_out, _spec((None, 256, D), lambda i, k: (k, 0, 0)))],
                   grid=(nt, NSH), o_shape=(t, D), o_dtype=F32, o_spec=_spec((tm, D), lambda i, k: (i, 0)),
                   dims=NN, kaxis=1, nk=NSH, acc_shape=(tm, D), res=(h, _spec((tm, D), lambda i, k: (i, 0))))

    def out_proj_bwd(tag, dhb, y, w_out, blk, grad_b):
        dy = _mm(tag + "_dy", [(dhb, tok_si, w_out, _spec((None, 256, D), lambda s, i: (s, 0, 0)))],
                 grid=(NSH, nt), o_shape=(t, D), o_dtype=F32, o_spec=_spec((tm, 256), lambda s, i: (i, s)),
                 dims=NT, kaxis=1, nk=1)
        grad_b = _wgrad(tag + "_dwout", y, _spec((tw, 256), lambda s, k: (k, s)), dhb, _spec((tw, D), lambda s, k: (k, 0)),
                        256, D, t, tw, grad_b, blk)
        return dy, grad_b

    h0 = x
    h1, pre0 = ffn_forward("pre0", h0, R_PRE)
    hn_ab = _norm_fwd("mix0_norm", h1, small, R_MIX)
    proj_a = _mm("ab_proj_a", [(hn_ab, tok_si, w["ab_in"], _spec((None, D, 768), lambda s, i: (s, 0, 0)))],
                 grid=(2, nt), o_shape=(t, 1536), o_dtype=F32, o_spec=_spec((tm, 768), lambda s, i: (i, s)),
                 dims=NN, kaxis=1, nk=1)
    proj_b = _mm("ab_proj_b", [(hn_ab, tok_si, w["ab_in"], _spec((None, D, 768), lambda s, i: (s + 2, 0, 0)))],
                 grid=(2, nt), o_shape=(t, 1536), o_dtype=BF16, o_spec=_spec((tm, 768), lambda s, i: (i, s)),
                 dims=NN, kaxis=1, nk=1)
    y_a = _conv_fwd(proj_a, w["conv"])
    names, gather = carried("sb_fwd")
    y_b, ltot, nblk, got = _sb_fwd(proj_b, gather)
    land(names, got)
    y_ab = jnp.concatenate([y_a, y_b], axis=1)
    h2 = out_proj("ab_out", y_ab, w["ab_out"], h1)
    h3, post0 = ffn_forward("post0", h2, R_POST)
    h4, pre1 = ffn_forward("pre1", h3, R_PRE + 1)
    hn_c = _norm_fwd("mix1_norm", h4, small, R_MIX + 1)
    proj_c = _mm("c_proj", [(hn_c, tok_si, w["c_in"], _spec((None, D, D), lambda s, i: (s, 0, 0)))],
                 grid=(NSH, nt), o_shape=(NSH, t, D), o_dtype=F32, o_spec=_spec((None, tm, D), lambda s, i: (s, i, 0)),
                 dims=NN, kaxis=1, nk=1)
    names, gather = carried("hgrn_fwd")
    o_c, y_c, sst, got = _hgrn_fwd(proj_c, small, gather)
    land(names, got)
    h5 = out_proj("c_out", y_c, w["c_out"], h4)
    h6, post1 = ffn_forward("post1", h5, R_POST + 1)
    dh, dhb, d_fin, loss = _final_loss(h6, small, target)

    dh, dhb, grad_a, grad_b, dn_post1 = _ffn_backward("post1", dh, dhb, *post1, w["post1_gu"], w["post1_d"],
                                                      *FFN_SLOTS["post1"], small, R_POST + 1, 8 * FS, B_ROWS)
    dy_c, grad_b = out_proj_bwd("c", dhb, y_c, w["c_out"], B_COUT // 256, grad_b)
    names, pairs, rider = reduce_start("x", dict(a=grad_a, b=grad_b))
    dproj_c, d_clb, d_gam, landed = _hgrn_bwd(proj_c, small, o_c, sst, dy_c, rider)
    reduce_end("x", names, pairs, landed)
    grad_b = _wgrad("c_dwin", hn_c, _spec((tw, D), lambda s, k: (k, 0)), dproj_c, _spec((None, tw, D), lambda s, k: (s, k, 0)),
                    D, D, t, tw, grad_b, B_CIN // D)
    dhn = _mm("c_dhn", [(dproj_c, _spec((None, tm, D), lambda i, k: (k, i, 0)),
                         w["c_in"], _spec((None, D, D), lambda i, k: (k, 0, 0)))],
              grid=(nt, NSH), o_shape=(t, D), o_dtype=F32, o_spec=_spec((tm, D), lambda i, k: (i, 0)),
              dims=NT, kaxis=1, nk=NSH, acc_shape=(tm, D))
    dh, dhb, dn_mix1 = _norm_bwd("mix1_norm_bwd", dhn, h4, small, R_MIX + 1, dh)
    dh, dhb, grad_a, grad_b, dn_pre1 = _ffn_backward("pre1", dh, dhb, *pre1, w["pre1_gu"], w["pre1_d"],
                                                     *FFN_SLOTS["pre1"], small, R_PRE + 1, grad_a, grad_b)
    dh, dhb, grad_a, grad_b, dn_post0 = _ffn_backward("post0", dh, dhb, *post0, w["post0_gu"], w["post0_d"],
                                                      *FFN_SLOTS["post0"], small, R_POST, grad_a, grad_b)
    dy_ab, grad_b = out_proj_bwd("ab", dhb, y_ab, w["ab_out"], B_ABOUT // 256, grad_b)
    dab, dac, dax, d_conv = _conv_bwd(proj_a, w["conv"], dy_ab)
    names, pairs, rider = reduce_start("y", dict(a=grad_a, b=grad_b))
    dq, dk, dv, landed = _sb_bwd(proj_b, dy_ab, ltot, nblk, rider)
    reduce_end("y", names, pairs, landed)
    dproj_ab = jnp.concatenate([dab, dac, dax, dq, dk, dv], axis=1)
    grad_c = _wgrad("ab_dwin", hn_ab, _spec((tw, D), lambda s, k: (k, 0)), dproj_ab, _spec((tw, 768), lambda s, k: (k, s)),
                    D, 768, t, tw, D, 0)
    dhn = _mm("ab_dhn", [(dproj_ab, _spec((tm, 768), lambda i, k: (i, k)),
                          w["ab_in"], _spec((None, D, 768), lambda i, k: (k, 0, 0)))],
              grid=(nt, NSH), o_shape=(t, D), o_dtype=F32, o_spec=_spec((tm, D), lambda i, k: (i, 0)),
              dims=NT, kaxis=1, nk=NSH, acc_shape=(tm, D))
    dh, dhb, dn_mix0 = _norm_bwd("mix0_norm_bwd", dhn, h1, small, R_MIX, dh)
    dh, dhb, grad_a, grad_b, dn_pre0 = _ffn_backward("pre0", dh, dhb, *pre0, w["pre0_gu"], w["pre0_d"],
                                                     *FFN_SLOTS["pre0"], small, R_PRE, grad_a, grad_b)
    names, pairs, rider = reduce_start("z", dict(a=grad_a, b=grad_b, c=grad_c))
    if rider is not None:
        reduce_end("z", names, pairs, _send_to_owners("grad_send_z", pairs))
    zero = jnp.zeros((1, D), F32)
    conv_rows = jnp.pad(jnp.transpose(d_conv[:, :3, :], (1, 0, 2)).reshape(3, 512), ((0, 0), (0, D - 512)))
    small_grad = jnp.concatenate([
        dn_pre0, dn_pre1, dn_mix0, dn_mix1, dn_post0, dn_post1, d_clb, d_fin,
        jnp.pad(d_gam, ((0, 0), (0, D - HD))), conv_rows,
        jnp.pad(loss, ((0, 0), (0, D - 1))), zero, zero], axis=0)
    return dh, grad_a, grad_b, grad_c, small_grad, reduced


def _small_slab(rows):
    parts = [jnp.pad(r.astype(F32), ((0, 0), (0, D - r.shape[1]))) for r in rows]
    slab = jnp.concatenate(parts, axis=0)
    return jnp.pad(slab, ((0, SMALL_ROWS - slab.shape[0]), (0, 0)))


def kernel(x, ffn_pre_norm, ffn_pre_w_gate, ffn_pre_w_up, ffn_pre_w_down, mix_norm, ffn_post_norm, ffn_post_w_gate, ffn_post_w_up, ffn_post_w_down, ab_w_in, ab_conv_w, ab_w_out, c_w_in, c_lower_bounds, c_out_norm, c_w_out, final_norm, loss_target, m_ffn_pre_norm, m_ffn_pre_w_gate, m_ffn_pre_w_up, m_ffn_pre_w_down, m_mix_norm, m_ffn_post_norm, m_ffn_post_w_gate, m_ffn_post_w_up, m_ffn_post_w_down, m_ab_w_in, m_ab_conv_w, m_ab_w_out, m_c_w_in, m_c_lower_bounds, m_c_out_norm, m_c_w_out, m_final_norm, v_ffn_pre_norm, v_ffn_pre_w_gate, v_ffn_pre_w_up, v_ffn_pre_w_down, v_mix_norm, v_ffn_post_norm, v_ffn_post_w_gate, v_ffn_post_w_up, v_ffn_post_w_down, v_ab_w_in, v_ab_conv_w, v_ab_w_out, v_c_w_in, v_c_lower_bounds, v_c_out_norm, v_c_w_out, v_final_norm):
    t = x.shape[1]
    xi, yi, ci = lax.axis_index("x"), lax.axis_index("y"), lax.axis_index("c")
    p_idx = (2 * xi + yi).astype(jnp.int32).reshape(1)
    c_idx = ci.astype(jnp.int32).reshape(1)

    def halves(m):
        return m.astype(BF16).reshape(2, m.shape[0] // 2, m.shape[1])

    transposed = ("ffn_pre_w_gate", "ffn_pre_w_up", "ffn_post_w_gate", "ffn_post_w_up")

    def flip(a):
        return jnp.swapaxes(a, 1, 2)

    shards = {}
    for name, (w_gate, w_up, w_down, layer) in dict(
            pre0=(ffn_pre_w_gate, ffn_pre_w_up, ffn_pre_w_down, 0), post0=(ffn_post_w_gate, ffn_post_w_up, ffn_post_w_down, 0),
            pre1=(ffn_pre_w_gate, ffn_pre_w_up, ffn_pre_w_down, 1), post1=(ffn_post_w_gate, ffn_post_w_up, ffn_post_w_down, 1)).items():
        shards[name + "_gu"] = jnp.stack([flip(w_gate)[layer], flip(w_up)[layer]]).astype(BF16)
        shards[name + "_d"] = halves(w_down[layer])
    conv_pad = jnp.pad(ab_conv_w[0], ((0, 5), (0, 0)))
    shards.update(ab_in=halves(ab_w_in[0]), ab_out=halves(ab_w_out[0]), c_in=halves(c_w_in[0]), c_out=halves(c_w_out[0]),
                  conv=jnp.stack([conv_pad, jnp.zeros_like(conv_pad)]))
    first = _gather_weights([shards["pre0_gu"]])[0].reshape(W_SHAPES["pre0_gu"])

    small = _small_slab([ffn_pre_norm, mix_norm, ffn_post_norm, c_lower_bounds, final_norm.reshape(1, D), c_out_norm])
    small = small.reshape(SMALL_ROWS, 1, D)

    grad_x, _, _, _, small_grad, reduced = _local_step(x[0], loss_target[0], dict(pre0_gu=first), small, shards,
                                                        (c_idx, p_idx))
    order = list(reduced)
    whole = {n: g.reshape(2 * g.shape[1], g.shape[2]) for n, g in zip(order, _share_halves([reduced[n] for n in order]))}
    small_sum = _sum_small(small_grad)

    my_conv = lax.dynamic_slice(small_sum[R_CONV:R_CONV + 3], (0, (2 * xi + yi) * 128), (3, 128))
    grads = {
        "ffn_pre_norm": small_sum[R_PRE:R_PRE + 2], "mix_norm": small_sum[R_MIX:R_MIX + 2],
        "ffn_post_norm": small_sum[R_POST:R_POST + 2], "c_lower_bounds": small_sum[R_CLB:R_CLB + 2],
        "c_out_norm": small_sum[R_GAM:R_GAM + 1, :HD], "final_norm": small_sum[R_FIN],
        "ab_conv_w": my_conv.reshape(1, 3, 128),
        "ab_w_in": whole["ab_in"][None], "ab_w_out": whole["ab_out"][None],
        "c_w_in": whole["c_in"][None], "c_w_out": whole["c_out"][None],
    }
    for kind, key in (("gate", "_g"), ("up", "_u"), ("down", "_d")):
        grads["ffn_pre_w_" + kind] = jnp.stack([whole["pre0" + key], whole["pre1" + key]])
        grads["ffn_post_w_" + kind] = jnp.stack([whole["post0" + key], whole["post1" + key]])
    weights = dict(ffn_pre_norm=ffn_pre_norm, ffn_pre_w_gate=ffn_pre_w_gate, ffn_pre_w_up=ffn_pre_w_up, ffn_pre_w_down=ffn_pre_w_down, mix_norm=mix_norm, ffn_post_norm=ffn_post_norm, ffn_post_w_gate=ffn_post_w_gate, ffn_post_w_up=ffn_post_w_up, ffn_post_w_down=ffn_post_w_down, ab_w_in=ab_w_in, ab_conv_w=ab_conv_w, ab_w_out=ab_w_out, c_w_in=c_w_in, c_lower_bounds=c_lower_bounds, c_out_norm=c_out_norm, c_w_out=c_w_out, final_norm=final_norm)
    m_in = dict(ffn_pre_norm=m_ffn_pre_norm, ffn_pre_w_gate=m_ffn_pre_w_gate, ffn_pre_w_up=m_ffn_pre_w_up, ffn_pre_w_down=m_ffn_pre_w_down, mix_norm=m_mix_norm, ffn_post_norm=m_ffn_post_norm, ffn_post_w_gate=m_ffn_post_w_gate, ffn_post_w_up=m_ffn_post_w_up, ffn_post_w_down=m_ffn_post_w_down, ab_w_in=m_ab_w_in, ab_conv_w=m_ab_conv_w, ab_w_out=m_ab_w_out, c_w_in=m_c_w_in, c_lower_bounds=m_c_lower_bounds, c_out_norm=m_c_out_norm, c_w_out=m_c_w_out, final_norm=m_final_norm)
    v_in = dict(ffn_pre_norm=v_ffn_pre_norm, ffn_pre_w_gate=v_ffn_pre_w_gate, ffn_pre_w_up=v_ffn_pre_w_up, ffn_pre_w_down=v_ffn_pre_w_down, mix_norm=v_mix_norm, ffn_post_norm=v_ffn_post_norm, ffn_post_w_gate=v_ffn_post_w_gate, ffn_post_w_up=v_ffn_post_w_up, ffn_post_w_down=v_ffn_post_w_down, ab_w_in=v_ab_w_in, ab_conv_w=v_ab_conv_w, ab_w_out=v_ab_w_out, c_w_in=v_c_w_in, c_lower_bounds=v_c_lower_bounds, c_out_norm=v_c_out_norm, c_w_out=v_c_w_out, final_norm=v_final_norm)
    names = list(weights)
    big = [n for n in names if weights[n].size >= 65536]
    tiny = [n for n in names if n not in big]

    delta, new_m, new_v = {}, {}, {}
    for n in big:
        turn = flip if n in transposed else (lambda a: a)
        shape = turn(weights[n]).shape
        two_d = (shape[0] * shape[1], shape[2])
        d, m2, v2 = _adamw("adamw_" + n, turn(weights[n]).reshape(two_d), grads[n].reshape(two_d),
                           turn(m_in[n]).reshape(two_d), turn(v_in[n]).reshape(two_d))
        delta[n], new_m[n], new_v[n] = turn(d.reshape(shape)), turn(m2.reshape(shape)), turn(v2.reshape(shape))
        grads[n] = turn(grads[n])

    def tiny_slab(src):
        return _small_slab([src[n].reshape(-1, src[n].shape[-1]) for n in tiny])

    offs, row = {}, 0
    for n in tiny:
        nrows = weights[n].size // weights[n].shape[-1]
        offs[n] = (row, nrows)
        row += nrows
    d, m2, v2 = _adamw("adamw_small", tiny_slab(weights), tiny_slab(grads), tiny_slab(m_in), tiny_slab(v_in))
    for n in tiny:
        r0, nr = offs[n]
        shape = weights[n].shape
        for dst, src in ((delta, d), (new_m, m2), (new_v, v2)):
            dst[n] = src[r0:r0 + nr, :shape[-1]].reshape(shape)

    loss = small_sum[R_LOSS, 0]
    return (loss, grad_x.reshape(1, t, D), *[grads[n] for n in names], *[delta[n] for n in names],
            *[new_m[n] for n in names], *[new_v[n] for n in names])
```

```python
import functools
import math

import jax
import jax.numpy as jnp
from jax import lax
from jax.experimental import pallas as pl
from jax.experimental.pallas import tpu as pltpu

F32 = jnp.float32
BF16 = jnp.bfloat16
MESH = pl.DeviceIdType.MESH
ANY = pl.BlockSpec(memory_space=pl.ANY)

D = 1024
FS = 704
NSH = 4
RMS_EPS = 1e-6
MACARON = 0.5
CHUNK = 64
HD = 128
HGRN_HPS = 4
SBQ = 128
SB_PAIRS = 2
SB_DEAD = -105.0
CONV_HALO = 8
LANE = 128
ROW_TILE = 512
MM_TILE = 1024
WGRAD_TILE = 4096
VMEM_LIMIT = 48 * 1024 * 1024

ADAM_LR, ADAM_B1, ADAM_B2, ADAM_EPS, ADAM_WD, ADAM_STEP = 0.001, 0.9, 0.999, 1e-08, 0.01, 10

NN = ((1,), (0,))
NT = ((1,), (1,))
TN = ((0,), (0,))

SMALL_ROWS = 16
R_PRE, R_MIX, R_POST, R_CLB, R_FIN, R_GAM, R_CONV, R_LOSS = 0, 2, 4, 6, 8, 9, 10, 13

B_DOWN = 0
B_ABOUT = 4 * FS
B_CIN = B_ABOUT + 256
B_COUT = B_CIN + 1024
B_ROWS = B_COUT + 256


def _dg(a, b, dims):
    return lax.dot_general(a, b, (dims, ((), ())), preferred_element_type=F32)


def _split(x):
    hi = x.astype(BF16)
    lo = (x - hi.astype(F32)).astype(BF16)
    return hi, lo


def _dot3(a, b, dims):
    ah, al = _split(a)
    bh, bl = _split(b)
    if dims == TN:
        n = b.shape[1]
        both = _dg(ah, jnp.concatenate([bh, bl], axis=1), dims)
        return both[:, :n] + both[:, n:] + _dg(al, bh, dims)
    m = a.shape[0]
    both = _dg(jnp.concatenate([ah, al], axis=0), bh, dims)
    return both[:m] + both[m:] + _dg(ah, bl, dims)


def _sigmoid(x):
    return 1.0 / (1.0 + jnp.exp(-x))


def _params(sem):
    return pltpu.CompilerParams(dimension_semantics=sem, vmem_limit_bytes=VMEM_LIMIT)


def _spec(shape, imap):
    return pl.BlockSpec(shape, imap)


def _mm(name, pairs, *, grid, o_shape, o_dtype, o_spec, dims, kaxis, nk, acc_shape=None, res=None, scale=None,
        into=None, carry=None):
    npairs = len(pairs)
    operands, specs = [], []
    for a, a_spec, b, b_spec in pairs:
        operands += [a, b]
        specs += [a_spec, b_spec]
    if res is not None:
        operands.append(res[0])
        specs.append(res[1])
    aliases = {}
    if into is not None:
        aliases = {len(operands): 0}
        operands.append(into)
        specs.append(ANY)
    n_own = len(operands)
    nc = carry.n if carry is not None else 0
    if nc:
        operands += carry.groups
        specs += carry.in_specs

    def body(*refs):
        o_ref = refs[n_own + nc]
        riders = (refs[n_own:n_own + nc], refs[n_own + nc + 1:n_own + 2 * nc + 1], refs[-2], refs[-1]) if nc else None
        if nc:
            carry.ride(grid, riders, "start")
        part = None
        for n in range(npairs):
            d = _dg(refs[2 * n][...], refs[2 * n + 1][...], dims)
            part = d if part is None else part + d

        def finish(val):
            if scale is not None:
                val = val * scale
            if res is not None:
                val = val + refs[2 * npairs][...]
            o_ref[...] = val.astype(o_dtype)

        if nk == 1:
            finish(part)
        else:
            acc_ref = refs[n_own + 2 * nc + 1]
            k = pl.program_id(kaxis)

            @pl.when(k == 0)
            def _():
                acc_ref[...] = part

            @pl.when(k > 0)
            def _():
                acc_ref[...] += part

            @pl.when(k == nk - 1)
            def _():
                finish(acc_ref[...])
        if nc:
            carry.ride(grid, riders, "finish")

    sem = tuple("arbitrary" if (nc or (ax == kaxis and nk > 1)) else "parallel" for ax in range(len(grid)))
    outs = pl.pallas_call(
        body, name=name, grid=grid, in_specs=specs, out_specs=[o_spec] + (carry.out_specs if nc else []),
        out_shape=[jax.ShapeDtypeStruct(o_shape, o_dtype)] + (carry.out_shape if nc else []),
        scratch_shapes=([pltpu.VMEM(acc_shape, F32)] if nk > 1 else []) + (carry.scratch if nc else []),
        input_output_aliases=aliases,
        compiler_params=_params(sem),
    )(*operands)
    return (outs[0], carry.place(outs[1:])) if nc else outs[0]


def _norm_fwd(name, h, gain_slab, row):
    t = h.shape[0]
    tm = min(ROW_TILE, t)

    def body(h_ref, g_ref, o_ref):
        x = h_ref[...]
        r = lax.rsqrt(jnp.mean(x * x, axis=-1, keepdims=True) + RMS_EPS)
        o_ref[...] = (x * r * g_ref[...]).astype(BF16)

    return pl.pallas_call(
        body, name=name, grid=(t // tm,),
        in_specs=[_spec((tm, D), lambda i: (i, 0)), _spec((None, 1, D), lambda i: (row, 0, 0))],
        out_specs=_spec((tm, D), lambda i: (i, 0)),
        out_shape=jax.ShapeDtypeStruct((t, D), BF16),
        compiler_params=_params(("parallel",)),
    )(h, gain_slab)


def _norm_bwd(name, dhn, h, gain_slab, row, dres):
    t = h.shape[0]
    tm = min(ROW_TILE, t)
    nt = t // tm

    def body(dhn_ref, h_ref, g_ref, dres_ref, dh_ref, dhb_ref, dg_ref, acc_ref):
        i = pl.program_id(0)
        x = h_ref[...]
        r = lax.rsqrt(jnp.mean(x * x, axis=-1, keepdims=True) + RMS_EPS)
        xh = x * r
        dy = dhn_ref[...]
        gdy = dy * g_ref[...]
        dx = (gdy - xh * jnp.mean(gdy * xh, axis=-1, keepdims=True)) * r
        dh = dres_ref[...] + dx
        dh_ref[...] = dh
        dhb_ref[...] = dh.astype(BF16)
        part = jnp.sum((dy * xh).reshape(tm // 8, 8, D), axis=0)

        @pl.when(i == 0)
        def _():
            acc_ref[...] = part

        @pl.when(i > 0)
        def _():
            acc_ref[...] += part

        @pl.when(i == nt - 1)
        def _():
            dg_ref[...] = jnp.sum(acc_ref[...], axis=0, keepdims=True)

    row_spec = _spec((tm, D), lambda i: (i, 0))
    return pl.pallas_call(
        body, name=name, grid=(nt,),
        in_specs=[row_spec, row_spec, _spec((None, 1, D), lambda i: (row, 0, 0)), row_spec],
        out_specs=[row_spec, row_spec, _spec((1, D), lambda i: (0, 0))],
        out_shape=[jax.ShapeDtypeStruct((t, D), F32), jax.ShapeDtypeStruct((t, D), BF16),
                   jax.ShapeDtypeStruct((1, D), F32)],
        scratch_shapes=[pltpu.VMEM((8, D), F32)],
        compiler_params=_params(("arbitrary",)),
    )(dhn, h, gain_slab, dres)


def _final_loss(h, gain_slab, target):
    t = h.shape[0]
    tm = min(ROW_TILE, t)
    nt = t // tm

    def body(h_ref, g_ref, t_ref, dh_ref, dhb_ref, dg_ref, loss_ref, acc_ref, lacc_ref):
        i = pl.program_id(0)
        x = h_ref[...]
        g = g_ref[...]
        r = lax.rsqrt(jnp.mean(x * x, axis=-1, keepdims=True) + RMS_EPS)
        xh = x * r
        err = xh * g - t_ref[...]
        dy = err * (1.0 / D)
        gdy = dy * g
        dh = (gdy - xh * jnp.mean(gdy * xh, axis=-1, keepdims=True)) * r
        dh_ref[...] = dh
        dhb_ref[...] = dh.astype(BF16)
        part = jnp.sum((dy * xh).reshape(tm // 8, 8, D), axis=0)
        lpart = jnp.sum((err * err).reshape(tm // 8, 8, D), axis=0)

        @pl.when(i == 0)
        def _():
            acc_ref[...] = part
            lacc_ref[...] = lpart

        @pl.when(i > 0)
        def _():
            acc_ref[...] += part
            lacc_ref[...] += lpart

        @pl.when(i == nt - 1)
        def _():
            dg_ref[...] = jnp.sum(acc_ref[...], axis=0, keepdims=True)
            rows = jnp.sum(lacc_ref[...], axis=0, keepdims=True)
            loss_ref[...] = jnp.sum(rows, axis=1, keepdims=True) * (0.5 / D)

    row_spec = _spec((tm, D), lambda i: (i, 0))
    return pl.pallas_call(
        body, name="final_loss", grid=(nt,),
        in_specs=[row_spec, _spec((None, 1, D), lambda i: (R_FIN, 0, 0)), row_spec],
        out_specs=[row_spec, row_spec, _spec((1, D), lambda i: (0, 0)), _spec((1, 1), lambda i: (0, 0))],
        out_shape=[jax.ShapeDtypeStruct((t, D), F32), jax.ShapeDtypeStruct((t, D), BF16),
                   jax.ShapeDtypeStruct((1, D), F32), jax.ShapeDtypeStruct((1, 1), F32)],
        scratch_shapes=[pltpu.VMEM((8, D), F32), pltpu.VMEM((8, D), F32)],
        compiler_params=_params(("arbitrary",)),
    )(h, gain_slab, target)


def _ffn_up(name, hn, wgu, carry=None):
    t = hn.shape[0]
    tm = min(MM_TILE, t)
    grid = (NSH, t // tm)
    nc = carry.n if carry is not None else 0

    def body(*refs):
        x_ref, wg_ref, wu_ref = refs[:3]
        s_up_ref, s_gate_ref, a_ref = refs[3 + nc:6 + nc]
        riders = (refs[3:3 + nc], refs[6 + nc:6 + 2 * nc], refs[-2], refs[-1]) if nc else None
        if nc:
            carry.ride(grid, riders, "start")
        x = x_ref[...]
        g = _dg(x, wg_ref[...], NT)
        u = _dg(x, wu_ref[...], NT)
        sg = _sigmoid(g)
        silu = g * sg
        s_up_ref[...] = (MACARON * silu).astype(BF16)
        s_gate_ref[...] = (MACARON * u * (sg * (1.0 + g * (1.0 - sg)))).astype(BF16)
        a_ref[...] = (silu * u).astype(BF16)
        if nc:
            carry.ride(grid, riders, "finish")

    act = _spec((None, tm, FS), lambda s, i: (s, i, 0))
    shape = jax.ShapeDtypeStruct((NSH, t, FS), BF16)
    outs = pl.pallas_call(
        body, name=name, grid=grid,
        in_specs=[_spec((tm, D), lambda s, i: (i, 0)),
                  _spec((None, None, FS, D), lambda s, i: (s, 0, 0, 0)),
                  _spec((None, None, FS, D), lambda s, i: (s, 1, 0, 0))] + (carry.in_specs if nc else []),
        out_specs=[act, act, act] + (carry.out_specs if nc else []),
        out_shape=[shape, shape, shape] + (carry.out_shape if nc else []),
        scratch_shapes=carry.scratch if nc else [],
        compiler_params=_params(("arbitrary", "arbitrary") if nc else ("parallel", "parallel")),
    )(hn, wgu, wgu, *(carry.groups if nc else []))
    return (outs[0], outs[1], outs[2], carry.place(outs[3:]) if nc else [])


def _ffn_down(name, a, wd, h, carry=None):
    t = h.shape[0]
    tm = min(MM_TILE, t)
    return _mm(name, [(a, _spec((None, tm, FS), lambda i, k: (k, i, 0)),
                       wd, _spec((None, FS, D), lambda i, k: (k, 0, 0)))],
               grid=(t // tm, NSH), o_shape=(t, D), o_dtype=F32, o_spec=_spec((tm, D), lambda i, k: (i, 0)),
               dims=NN, kaxis=1, nk=NSH, acc_shape=(tm, D), res=(h, _spec((tm, D), lambda i, k: (i, 0))),
               scale=MACARON, carry=carry)


def _ffn_bwd_up(name, dhb, wd, s_up, s_gate, rider=None):
    t = dhb.shape[0]
    tm = min(MM_TILE, t)
    grid = (NSH, t // tm)
    nr = rider.n if rider is not None else 0

    def body(*refs):
        dh_ref, wd_ref, s_up_ref, s_gate_ref = refs[:4]
        dg_ref, du_ref = refs[4 + nr:6 + nr]
        riders = (refs[4:4 + nr], refs[6 + nr:6 + 2 * nr], refs[-2], refs[-1]) if nr else None
        if nr:
            rider.ride(grid, riders, "start")
        da = _dg(dh_ref[...], wd_ref[...], NT)
        du_ref[...] = (da * s_up_ref[...].astype(F32)).astype(BF16)
        dg_ref[...] = (da * s_gate_ref[...].astype(F32)).astype(BF16)
        if nr:
            rider.ride(grid, riders, "finish")

    act = _spec((None, tm, FS), lambda s, i: (s, i, 0))
    shape = jax.ShapeDtypeStruct((NSH, t, FS), BF16)
    outs = pl.pallas_call(
        body, name=name, grid=grid,
        in_specs=[_spec((tm, D), lambda s, i: (i, 0)), _spec((None, FS, D), lambda s, i: (s, 0, 0)), act, act]
        + (rider.in_specs if nr else []),
        out_specs=[act, act] + (rider.out_specs if nr else []),
        out_shape=[shape, shape] + (rider.out_shape if nr else []),
        scratch_shapes=rider.scratch if nr else [],
        compiler_params=_params(("arbitrary", "arbitrary") if nr else ("parallel", "parallel")),
    )(dhb, wd, s_up, s_gate, *(rider.groups if nr else []))
    return outs[0], outs[1], (rider.place(outs[2:]) if nr else [])


def _wgrad(name, a, a_spec, b, b_spec, out_rows, out_cols, t, tt, group, slot, scale=None):
    first = isinstance(group, int)
    rows = group if first else group.shape[1]
    return _mm(name, [(a, a_spec, b, b_spec)], grid=(NSH, t // tt),
               o_shape=(NSH, rows, out_cols), o_dtype=BF16,
               o_spec=_spec((None, out_rows, out_cols), lambda s, k: (s, slot, 0)),
               dims=TN, kaxis=1, nk=t // tt, acc_shape=(out_rows, out_cols), scale=scale,
               into=None if first else group)


def _ffn_backward(tag, dh, dhb, h_in, hn, s_up, s_gate, a, wgu, wd, gate_idx, up_idx, down_idx, small, norm_row,
                  grad_a, grad_b, rider_up=None, rider_dhn=None):
    t = dh.shape[0]
    tm = min(MM_TILE, t)
    tt = min(WGRAD_TILE, t)
    dg, du, landed_up = _ffn_bwd_up(tag + "_bwd_up", dhb, wd, s_up, s_gate, rider_up)
    tok = _spec((tt, D), lambda s, k: (k, 0))
    hid = _spec((None, tt, FS), lambda s, k: (s, k, 0))
    grad_b = _wgrad(tag + "_dwd", a, hid, dhb, tok, FS, D, t, tt, grad_b, down_idx, scale=MACARON)
    grad_a = _wgrad(tag + "_dwg", dg, hid, hn, tok, FS, D, t, tt, grad_a, gate_idx)
    grad_a = _wgrad(tag + "_dwu", du, hid, hn, tok, FS, D, t, tt, grad_a, up_idx)
    rider = rider_dhn(grad_a, grad_b) if rider_dhn is not None else None
    act = _spec((None, tm, FS), lambda i, k: (k, i, 0))
    dhn = _mm(tag + "_dhn",
              [(dg, act, wgu, _spec((None, None, FS, D), lambda i, k: (k, 0, 0, 0))),
               (du, act, wgu, _spec((None, None, FS, D), lambda i, k: (k, 1, 0, 0)))],
              grid=(t // tm, NSH), o_shape=(t, D), o_dtype=F32, o_spec=_spec((tm, D), lambda i, k: (i, 0)),
              dims=NN, kaxis=1, nk=NSH, acc_shape=(tm, D), carry=rider)
    dhn, landed_dhn = dhn if rider is not None else (dhn, [])
    dh_in, dhb_in, d_gain = _norm_bwd(tag + "_norm_bwd", dhn, h_in, small, norm_row, dh)
    return dh_in, dhb_in, grad_a, grad_b, d_gain, landed_up, landed_dhn


def _conv_fwd(proj_a, conv_w):
    t = proj_a.shape[0]
    tm = min(ROW_TILE, t)
    hb = tm // CONV_HALO

    def body(ab_ref, ac_ref, ax_ref, acp_ref, axp_ref, w_ref, y_ref):
        i = pl.program_id(1)
        u = ac_ref[...] * ax_ref[...]
        up = jnp.where(i > 0, acp_ref[...] * axp_ref[...], 0.0)
        ext = jnp.concatenate([up, u], axis=0)
        u1 = pltpu.roll(ext, 1, 0)[CONV_HALO:]
        u2 = pltpu.roll(ext, 2, 0)[CONV_HALO:]
        w = w_ref[...]
        conv = w[0:1] * u2 + w[1:2] * u1 + w[2:3] * u
        y_ref[...] = (ab_ref[...] * conv).astype(BF16)

    def cur(off):
        return _spec((tm, LANE), lambda j, i: (i, off + j))

    def prev(off):
        return _spec((CONV_HALO, LANE), lambda j, i: (jnp.maximum(i * hb - 1, 0), off + j))

    return pl.pallas_call(
        body, name="conv_fwd", grid=(4, t // tm),
        in_specs=[cur(0), cur(4), cur(8), prev(4), prev(8),
                  _spec((None, None, 8, LANE), lambda j, i: (j, 0, 0, 0))],
        out_specs=_spec((tm, LANE), lambda j, i: (i, j)),
        out_shape=jax.ShapeDtypeStruct((t, 512), BF16),
        compiler_params=_params(("parallel", "parallel")),
    )(proj_a, proj_a, proj_a, proj_a, proj_a, conv_w)


def _conv_bwd(proj_a, conv_w, dy):
    t = proj_a.shape[0]
    tm = min(ROW_TILE, t)
    hb = tm // CONV_HALO
    nt = t // tm

    def body(ab_ref, ac_ref, ax_ref, dy_ref, acp_ref, axp_ref, abn_ref, dyn_ref, w_ref,
             dab_ref, dac_ref, dax_ref, dw_ref, acc_ref):
        i = pl.program_id(1)
        ab, ac, ax = ab_ref[...], ac_ref[...], ax_ref[...]
        u = ac * ax
        up = jnp.where(i > 0, acp_ref[...] * axp_ref[...], 0.0)
        ext = jnp.concatenate([up, u], axis=0)
        u1 = pltpu.roll(ext, 1, 0)[CONV_HALO:]
        u2 = pltpu.roll(ext, 2, 0)[CONV_HALO:]
        w = w_ref[...]
        conv = w[0:1] * u2 + w[1:2] * u1 + w[2:3] * u
        dy_v = dy_ref[...]
        dab_ref[...] = (dy_v * conv).astype(BF16)
        dc = dy_v * ab
        dcn = jnp.where(i < nt - 1, dyn_ref[...] * abn_ref[...], 0.0)
        extn = jnp.concatenate([dc, dcn], axis=0)
        n = tm + CONV_HALO
        dc1 = pltpu.roll(extn, n - 1, 0)[:tm]
        dc2 = pltpu.roll(extn, n - 2, 0)[:tm]
        du = w[2:3] * dc + w[1:2] * dc1 + w[0:1] * dc2
        dac_ref[...] = (du * ax).astype(BF16)
        dax_ref[...] = (du * ac).astype(BF16)
        rid = lax.broadcasted_iota(jnp.int32, (8, LANE), 0)
        part = jnp.where(rid == 0, jnp.sum(dc * u2, axis=0, keepdims=True),
                         jnp.where(rid == 1, jnp.sum(dc * u1, axis=0, keepdims=True),
                                   jnp.where(rid == 2, jnp.sum(dc * u, axis=0, keepdims=True), 0.0)))

        @pl.when(i == 0)
        def _():
            acc_ref[...] = part

        @pl.when(i > 0)
        def _():
            acc_ref[...] += part

        @pl.when(i == nt - 1)
        def _():
            dw_ref[...] = acc_ref[...]

    def cur(off):
        return _spec((tm, LANE), lambda j, i: (i, off + j))

    def prev(off):
        return _spec((CONV_HALO, LANE), lambda j, i: (jnp.maximum(i * hb - 1, 0), off + j))

    def nxt(off):
        return _spec((CONV_HALO, LANE), lambda j, i: (jnp.minimum((i + 1) * hb, nt * hb - 1), off + j))

    outs = pl.pallas_call(
        body, name="conv_bwd", grid=(4, nt),
        in_specs=[cur(0), cur(4), cur(8), cur(0), prev(4), prev(8), nxt(0), nxt(0),
                  _spec((None, None, 8, LANE), lambda j, i: (j, 0, 0, 0))],
        out_specs=[_spec((tm, LANE), lambda j, i: (i, j)), _spec((tm, LANE), lambda j, i: (i, j)),
                   _spec((tm, LANE), lambda j, i: (i, j)), _spec((None, 8, LANE), lambda j, i: (j, 0, 0))],
        out_shape=[jax.ShapeDtypeStruct((t, 512), BF16), jax.ShapeDtypeStruct((t, 512), BF16),
                   jax.ShapeDtypeStruct((t, 512), BF16), jax.ShapeDtypeStruct((4, 8, LANE), F32)],
        scratch_shapes=[pltpu.VMEM((8, LANE), F32)],
        compiler_params=_params(("parallel", "arbitrary")),
    )(proj_a, proj_a, proj_a, dy, proj_a, proj_a, proj_a, dy, conv_w)
    return outs


def _log_sigmoid(z):
    return jnp.minimum(z, 0.0) - jnp.log(1.0 + jnp.exp(-jnp.abs(z)))


def _sb_masks():
    row = lax.broadcasted_iota(jnp.int32, (SBQ, SBQ), 0)
    col = lax.broadcasted_iota(jnp.int32, (SBQ, SBQ), 1)
    return row, col


def _ones_where(mask):
    return jnp.where(mask, 1.0, 0.0).astype(BF16)


def _head_mask(head):
    lane = lax.broadcasted_iota(jnp.int32, (1, LANE), 1)
    return lane >= 64 if head else lane < 64


def _sb_fwd(proj_b, carry=None):
    t = proj_b.shape[0]
    nq = t // SBQ
    scale = 1.0 / math.sqrt(64.0)

    npair = SB_PAIRS
    wide = npair * LANE
    ngrp = 4 // npair
    chains = [(p, head) for p in range(npair) for head in range(2)]

    grid = (ngrp, nq)
    nc = carry.n if carry is not None else 0

    def body(*refs):
        q_ref, k_ref, v_ref = refs[:3]
        y_ref, l_ref, n_ref = refs[3 + nc:6 + nc]
        riders = (refs[3:3 + nc], refs[6 + nc:6 + 2 * nc], refs[-2], refs[-1]) if nc else None
        if nc:
            carry.ride(grid, riders, "start")
        grp = pl.program_id(0)
        qi = pl.program_id(1)
        row, col = _sb_masks()
        m_suffix = _ones_where(row > col)
        rows = len(chains) * SBQ
        strict = (lax.broadcasted_iota(jnp.int32, (rows, SBQ), 1)
                  < (lax.broadcasted_iota(jnp.int32, (rows, SBQ), 0) & (SBQ - 1)))
        q_pair = []
        for p in range(npair):
            q_all = q_ref[:, p * LANE:(p + 1) * LANE]
            q_pair.append(jnp.concatenate([jnp.where(_head_mask(head), q_all, jnp.zeros_like(q_all)) for head in range(2)],
                                          axis=0))

        def block(kb, state, diag):
            run, acc = state
            start = pl.multiple_of(kb * SBQ, SBQ)
            z = jnp.concatenate([_dg(q_pair[p], k_ref[pl.ds(start, SBQ), p * LANE:(p + 1) * LANE], NT)
                                 for p in range(npair)], axis=0) * scale
            lb = _log_sigmoid(z)
            lk = lb - z
            if diag:
                lk = jnp.where(strict, lk, 0.0)
            hi, lo = _split(lk)
            sums = _dg(jnp.concatenate([hi, lo], axis=0), m_suffix, NN)
            w = jnp.exp(lb + (run + sums[:rows] + sums[rows:]))
            if diag:
                w = jnp.where(strict, w, 0.0)
            wb = w.astype(BF16)
            acc = acc + jnp.concatenate(
                [_dg(wb[2 * p * SBQ:2 * (p + 1) * SBQ], v_ref[pl.ds(start, SBQ), p * LANE:(p + 1) * LANE], NN)
                 for p in range(npair)], axis=0)
            run = run + jnp.sum(hi.astype(F32) + lo.astype(F32), axis=1, keepdims=True)
            return run, acc

        state = block(qi, (jnp.zeros((rows, 1), F32), jnp.zeros((rows, LANE), F32)), True)

        def live(c):
            return jnp.logical_and(c[0] < qi, jnp.max(c[1][0]) > SB_DEAD)

        def step(c):
            return c[0] + 1, block(qi - 1 - c[0], c[1], False)

        count, (run, acc) = lax.while_loop(live, step, (jnp.int32(0), state))
        n_ref[grp * nq + qi] = count.astype(F32)
        hm = _head_mask(0)
        for p in range(npair):
            lo_rows, hi_rows = slice(2 * p * SBQ, (2 * p + 1) * SBQ), slice((2 * p + 1) * SBQ, (2 * p + 2) * SBQ)
            y_ref[:, p * LANE:(p + 1) * LANE] = jnp.where(hm, acc[lo_rows], acc[hi_rows]).astype(BF16)
            l_ref[p] = jnp.where(hm, run[lo_rows], run[hi_rows])
        if nc:
            carry.ride(grid, riders, "finish")

    outs = pl.pallas_call(
        body, name="sb_fwd", grid=grid,
        in_specs=[_spec((SBQ, wide), lambda g, i: (i, g)),
                  _spec((t, wide), lambda g, i: (0, ngrp + g)),
                  _spec((t, wide), lambda g, i: (0, 2 * ngrp + g))] + (carry.in_specs if nc else []),
        out_specs=[_spec((SBQ, wide), lambda g, i: (i, g)), _spec((npair, SBQ, LANE), lambda g, i: (g, i, 0)),
                   pl.BlockSpec(memory_space=pltpu.SMEM)] + (carry.out_specs if nc else []),
        out_shape=[jax.ShapeDtypeStruct((t, 512), BF16), jax.ShapeDtypeStruct((4, t, LANE), F32),
                   jax.ShapeDtypeStruct((ngrp * nq,), F32)] + (carry.out_shape if nc else []),
        scratch_shapes=carry.scratch if nc else [],
        compiler_params=_params(("arbitrary", "arbitrary")),
    )(proj_b, proj_b, proj_b, *(carry.groups if nc else []))
    return outs[0], outs[1], outs[2], (carry.place(outs[3:]) if nc else [])


def _sb_bwd(proj_b, dy, ltot, nblk, rider=None):
    t = proj_b.shape[0]
    nq = t // SBQ
    scale = 1.0 / math.sqrt(64.0)

    npair = SB_PAIRS
    wide = npair * LANE
    ngrp = 4 // npair
    chains = [(p, head) for p in range(npair) for head in range(2)]
    grid = (ngrp, nq)
    nr = rider.n if rider is not None else 0

    def body(*refs):
        q_ref, k_ref, v_ref, dy_ref, l_ref, n_ref = refs[:6]
        dq_ref, dk_ref, dv_ref = refs[6 + nr:9 + nr]
        dk_acc, dv_acc = refs[9 + 2 * nr:11 + 2 * nr]
        riders = (refs[6:6 + nr], refs[9 + nr:9 + 2 * nr], refs[-2], refs[-1]) if nr else None
        if nr:
            rider.ride(grid, riders, "start")
        grp = pl.program_id(0)
        qi = pl.program_id(1)

        @pl.when(qi == 0)
        def _():
            dk_acc[...] = jnp.zeros_like(dk_acc)
            dv_acc[...] = jnp.zeros_like(dv_acc)

        row, col = _sb_masks()
        m_prefix = _ones_where(row <= col)
        m_before = _ones_where(row < col)
        rows = len(chains) * SBQ
        strict = (lax.broadcasted_iota(jnp.int32, (rows, SBQ), 1)
                  < (lax.broadcasted_iota(jnp.int32, (rows, SBQ), 0) & (SBQ - 1)))
        q_pair, do_pair, ltot = [], [], []
        for p in range(npair):
            pl_ = slice(p * LANE, (p + 1) * LANE)
            q_all = q_ref[:, pl_]
            do_all = dy_ref[:, pl_].astype(BF16)
            q_pair.append(jnp.concatenate([jnp.where(_head_mask(h), q_all, jnp.zeros_like(q_all)) for h in range(2)], axis=0))
            do_pair.append(jnp.concatenate([jnp.where(_head_mask(h), do_all, jnp.zeros_like(do_all)) for h in range(2)], axis=0))
            ltot += [l_ref[p][:, h * 64:h * 64 + 1] for h in range(2)]
        ltot = jnp.concatenate(ltot, axis=0)

        def pair_rows(a, p):
            return a[2 * p * SBQ:2 * (p + 1) * SBQ]

        def block(kb, state, diag):
            seen, dseen, dq = state
            start = pl.multiple_of(kb * SBQ, SBQ)
            kk = [k_ref[pl.ds(start, SBQ), p * LANE:(p + 1) * LANE] for p in range(npair)]
            vv = [v_ref[pl.ds(start, SBQ), p * LANE:(p + 1) * LANE] for p in range(npair)]
            z = jnp.concatenate([_dg(q_pair[p], kk[p], NT) for p in range(npair)], axis=0) * scale
            lb = _log_sigmoid(z)
            lk = lb - z
            if diag:
                lk = jnp.where(strict, lk, 0.0)
            hi, lo = _split(lk)
            sums = _dg(jnp.concatenate([hi, lo], axis=0), m_prefix, NN)
            w = jnp.exp(lb + ((ltot - seen) - (sums[:rows] + sums[rows:])))
            if diag:
                w = jnp.where(strict, w, 0.0)
            wb = w.astype(BF16)
            da = w * jnp.concatenate([_dg(do_pair[p], vv[p], NT) for p in range(npair)], axis=0)
            dah, dal = _split(da)
            dsums = _dg(jnp.concatenate([dah, dal], axis=0), m_before, NN)
            sig = jnp.exp(lb)
            dz = (da * (1.0 - sig) - (dseen + dsums[:rows] + dsums[rows:]) * sig) * scale
            if diag:
                dz = jnp.where(strict, dz, 0.0)
            dzb = dz.astype(BF16)
            for p in range(npair):
                pl_ = slice(p * LANE, (p + 1) * LANE)
                dv_acc[pl.ds(start, SBQ), pl_] += _dg(pair_rows(wb, p), do_pair[p], TN)
                dk_acc[pl.ds(start, SBQ), pl_] += _dg(pair_rows(dzb, p), q_pair[p], TN)
            dq = dq + jnp.concatenate([_dg(pair_rows(dzb, p), kk[p], NN) for p in range(npair)], axis=0)
            seen = seen + jnp.sum(hi.astype(F32) + lo.astype(F32), axis=1, keepdims=True)
            dseen = dseen + jnp.sum(da, axis=1, keepdims=True)
            return seen, dseen, dq

        zero = (jnp.zeros((rows, 1), F32), jnp.zeros((rows, 1), F32), jnp.zeros((rows, LANE), F32))
        first = qi - n_ref[grp * nq + qi].astype(jnp.int32)
        state = lax.fori_loop(first, qi, lambda kb, c: block(kb, c, False), zero)
        _, _, dq = block(qi, state, True)
        for p in range(npair):
            dq_ref[:, p * LANE:(p + 1) * LANE] = jnp.where(
                _head_mask(0), dq[2 * p * SBQ:(2 * p + 1) * SBQ], dq[(2 * p + 1) * SBQ:(2 * p + 2) * SBQ]).astype(BF16)

        @pl.when(qi == nq - 1)
        def _():
            dk_ref[...] = dk_acc[...].astype(BF16)
            dv_ref[...] = dv_acc[...].astype(BF16)

        if nr:
            rider.ride(grid, riders, "finish")

    full = jax.ShapeDtypeStruct((t, 512), BF16)
    outs = pl.pallas_call(
        body, name="sb_bwd", grid=grid,
        in_specs=[_spec((SBQ, wide), lambda g, i: (i, g)),
                  _spec((t, wide), lambda g, i: (0, ngrp + g)),
                  _spec((t, wide), lambda g, i: (0, 2 * ngrp + g)),
                  _spec((SBQ, wide), lambda g, i: (i, ngrp + g)),
                  _spec((npair, SBQ, LANE), lambda g, i: (g, i, 0)),
                  pl.BlockSpec(memory_space=pltpu.SMEM)] + (rider.in_specs if nr else []),
        out_specs=[_spec((SBQ, wide), lambda g, i: (i, g)),
                   _spec((t, wide), lambda g, i: (0, g)), _spec((t, wide), lambda g, i: (0, g))]
        + (rider.out_specs if nr else []),
        out_shape=[full, full, full] + (rider.out_shape if nr else []),
        scratch_shapes=[pltpu.VMEM((t, wide), F32), pltpu.VMEM((t, wide), F32)] + (rider.scratch if nr else []),
        compiler_params=_params(("arbitrary", "arbitrary")),
    )(proj_b, proj_b, proj_b, dy, ltot, nblk, *(rider.groups if nr else []))
    return outs[0], outs[1], outs[2], (rider.place(outs[3:]) if nr else [])


def _hgrn_gates(qr, fr, c0, c1):
    mx = jnp.maximum(c0, c1)
    e0, e1 = jnp.exp(c0 - mx), jnp.exp(c1 - mx)
    lb = e1 / (e0 + e1)
    sx = _sigmoid(fr)
    f = lb + (1.0 - lb) * sx
    k = (1.0 - lb) * (1.0 - sx)
    sq = _sigmoid(qr)
    return lb, sx, f, k, sq, qr * sq


def _chunk_sums(mask, x):
    n = x.shape[1]
    hi, lo = _split(x)
    both = _dg(_ones_where(mask), jnp.concatenate([hi, lo], axis=1), NN)
    return both[:, :n] + both[:, n:]


def _chunk_masks():
    row = lax.broadcasted_iota(jnp.int32, (CHUNK, CHUNK), 0)
    col = lax.broadcasted_iota(jnp.int32, (CHUNK, CHUNK), 1)
    return col <= row, col >= row


def _hgrn_fwd(proj_c, small, carry=None):
    t = proj_c.shape[1]
    nc = t // CHUNK
    nh = D // HD

    hps = HGRN_HPS
    wide = hps * HD
    grid = (nh // hps, nc)
    nr = carry.n if carry is not None else 0

    def body(*refs):
        p_ref, c0_ref, c1_ref, gam_ref = refs[:4]
        o_ref, y_ref, sst_ref = refs[4 + nr:7 + nr]
        st_ref = refs[7 + 2 * nr]
        riders = (refs[4:4 + nr], refs[7 + nr:7 + 2 * nr], refs[-2], refs[-1]) if nr else None
        if nr:
            carry.ride(grid, riders, "start")
        c = pl.program_id(1)

        @pl.when(c == 0)
        def _():
            st_ref[...] = jnp.zeros_like(st_ref)

        _, _, f_all, k_all, _, q_all = _hgrn_gates(p_ref[0], p_ref[1], c0_ref[...], c1_ref[...])
        tril, _ = _chunk_masks()
        b_all = _chunk_sums(tril, jnp.log(f_all))
        for j in range(hps):
            ln = slice(j * HD, (j + 1) * HD)
            st0 = st_ref[j]
            sst_ref[j] = st0
            v, g = p_ref[2, :, ln], p_ref[3, :, ln]
            q, k, b = q_all[:, ln], k_all[:, ln], b_all[:, ln]
            bm = b[CHUNK // 2 - 1:CHUNK // 2]
            bl = b[CHUNK - 1:CHUNK]
            qd = q * jnp.exp(b)
            qt = q * jnp.exp(b - bm)
            kt = k * jnp.exp(bm - b)
            kl = k * jnp.exp(bl - b)
            vb = v.astype(BF16)
            att = jnp.where(tril, _dot3(qt, kt, NT), 0.0)
            o = _dg(qd.astype(BF16), st0.astype(BF16), NT) + _dg(att.astype(BF16), vb, NN)
            st_ref[j] = st0 * jnp.exp(bl) + _dg(vb, kl.astype(BF16), TN)
            o_ref[:, ln] = o
            r = lax.rsqrt(jnp.mean(o * o, axis=-1, keepdims=True) + RMS_EPS)
            y_ref[:, ln] = (o * r * gam_ref[...] * (g * _sigmoid(g))).astype(BF16)
        if nr:
            carry.ride(grid, riders, "finish")

    outs = pl.pallas_call(
        body, name="hgrn_fwd", grid=grid,
        in_specs=[_spec((4, CHUNK, wide), lambda h, c: (0, c, h)),
                  _spec((None, 1, wide), lambda h, c: (R_CLB, 0, h)),
                  _spec((None, 1, wide), lambda h, c: (R_CLB + 1, 0, h)),
                  _spec((None, 1, HD), lambda h, c: (R_GAM, 0, 0))] + (carry.in_specs if nr else []),
        out_specs=[_spec((CHUNK, wide), lambda h, c: (c, h)), _spec((CHUNK, wide), lambda h, c: (c, h)),
                   _spec((None, hps, HD, HD), lambda h, c: (c, h, 0, 0))] + (carry.out_specs if nr else []),
        out_shape=[jax.ShapeDtypeStruct((t, D), F32), jax.ShapeDtypeStruct((t, D), BF16),
                   jax.ShapeDtypeStruct((nc, nh, HD, HD), F32)] + (carry.out_shape if nr else []),
        scratch_shapes=[pltpu.VMEM((hps, HD, HD), F32)] + (carry.scratch if nr else []),
        compiler_params=_params(("arbitrary", "arbitrary")),
    )(proj_c, small, small, small, *(carry.groups if nr else []))
    return outs[0], outs[1], outs[2], (carry.place(outs[3:]) if nr else [])


def _hgrn_bwd(proj_c, small, o, sst, dyc, rider=None):
    t = proj_c.shape[1]
    nc = t // CHUNK
    nh = D // HD

    hps = HGRN_HPS
    wide = hps * HD
    ng = nh // hps
    grid = (ng, nc)
    nr = rider.n if rider is not None else 0

    def body(*refs):
        p_ref, c0_ref, c1_ref, gam_ref, o_ref, sst_ref, dy_ref = refs[:7]
        dp_ref, dclb_ref, dgam_ref = refs[7 + nr:10 + nr]
        dst_ref, dlb_acc, dgam_acc = refs[10 + 2 * nr:13 + 2 * nr]
        riders = (refs[7:7 + nr], refs[10 + nr:10 + 2 * nr], refs[-2], refs[-1]) if nr else None
        if nr:
            rider.ride(grid, riders, "start")
        group = pl.program_id(0)
        step = pl.program_id(1)

        @pl.when(step == 0)
        def _():
            dst_ref[...] = jnp.zeros_like(dst_ref)
            dlb_acc[...] = jnp.zeros_like(dlb_acc)

        @pl.when((step == 0) & (group == 0))
        def _():
            dgam_acc[...] = jnp.zeros_like(dgam_acc)

        gam = gam_ref[...]
        qr_all = p_ref[0]
        lb_all, sx_all, f_all, k_all, sq_all, q_all = _hgrn_gates(qr_all, p_ref[1], c0_ref[...], c1_ref[...])
        tril, triu = _chunk_masks()
        b_all = _chunk_sums(tril, jnp.log(f_all))
        dq_parts, dk_parts, db_parts, dgam_parts = [], [], [], []
        for j in range(hps):
            ln = slice(j * HD, (j + 1) * HD)
            st0 = sst_ref[j]
            dst1 = dst_ref[j]
            v, g = p_ref[2, :, ln], p_ref[3, :, ln]
            q, k, b = q_all[:, ln], k_all[:, ln], b_all[:, ln]
            bm = b[CHUNK // 2 - 1:CHUNK // 2]
            bl = b[CHUNK - 1:CHUNK]
            eb = jnp.exp(b)
            e_qt = jnp.exp(b - bm)
            e_kt = jnp.exp(bm - b)
            e_kl = jnp.exp(bl - b)
            e_bl = jnp.exp(bl)
            qd, qt, kt, kl = q * eb, q * e_qt, k * e_kt, k * e_kl
            ov = o_ref[:, ln]
            r = lax.rsqrt(jnp.mean(ov * ov, axis=-1, keepdims=True) + RMS_EPS)
            oh = ov * r
            sg = _sigmoid(g)
            dy = dy_ref[:, ln]
            dp_ref[3, :, ln] = (dy * oh * gam * (sg * (1.0 + g * (1.0 - sg)))).astype(BF16)
            dyv = dy * (g * sg)
            gdy = dyv * gam
            do = (gdy - oh * jnp.mean(gdy * oh, axis=-1, keepdims=True)) * r
            dob, vb = do.astype(BF16), v.astype(BF16)
            st0b, dst1b = st0.astype(BF16), dst1.astype(BF16)
            st1 = st0 * e_bl + _dg(vb, kl.astype(BF16), TN)
            att = jnp.where(tril, _dot3(qt, kt, NT), 0.0)
            datt = jnp.where(tril, _dg(dob, vb, NT), 0.0)
            dv = _dg(att.astype(BF16), dob, TN) + _dg(kl.astype(BF16), dst1b, NT)
            dq = _dot3(datt, kt, NN) * e_qt + _dg(dob, st0b, NN) * eb
            dk = _dot3(datt, qt, TN) * e_kt + _dg(vb, dst1b, NN) * e_kl
            db = q * dq - k * dk
            last = lax.broadcasted_iota(jnp.int32, (CHUNK, 1), 0) == CHUNK - 1
            db = db + jnp.where(last, jnp.sum(dst1 * st1, axis=0, keepdims=True), 0.0)
            dst_ref[j] = dst1 * e_bl + _dg(dob, qd.astype(BF16), TN)
            dp_ref[2, :, ln] = dv.astype(BF16)
            dq_parts.append(dq)
            dk_parts.append(dk)
            db_parts.append(db)
            dgam_parts.append(jnp.sum(dyv * oh, axis=0, keepdims=True))

        dq_all, dk_all = jnp.concatenate(dq_parts, axis=1), jnp.concatenate(dk_parts, axis=1)
        dlf = _chunk_sums(triu, jnp.concatenate(db_parts, axis=1))
        dp_ref[0] = (dq_all * (sq_all * (1.0 + qr_all * (1.0 - sq_all)))).astype(BF16)
        tmp = dlf / f_all - dk_all
        dp_ref[1] = (tmp * (1.0 - lb_all) * sx_all * (1.0 - sx_all)).astype(BF16)
        dlb_acc[...] += jnp.sum((1.0 - sx_all) * tmp, axis=0, keepdims=True)
        dgam_acc[...] += functools.reduce(lambda a, b: a + b, dgam_parts)

        @pl.when(step == nc - 1)
        def _():
            d1 = dlb_acc[...] * lb_all * (1.0 - lb_all)
            dclb_ref[...] = jnp.where(lax.broadcasted_iota(jnp.int32, (2, wide), 0) == 0, -d1, d1)

        @pl.when((step == nc - 1) & (group == ng - 1))
        def _():
            dgam_ref[...] = dgam_acc[...]

        if nr:
            rider.ride(grid, riders, "finish")

    rev = lambda h, s: (nc - 1 - s, h)
    outs = pl.pallas_call(
        body, name="hgrn_bwd", grid=grid,
        in_specs=[_spec((4, CHUNK, wide), lambda h, s: (0, nc - 1 - s, h)),
                  _spec((None, 1, wide), lambda h, s: (R_CLB, 0, h)),
                  _spec((None, 1, wide), lambda h, s: (R_CLB + 1, 0, h)),
                  _spec((None, 1, HD), lambda h, s: (R_GAM, 0, 0)),
                  _spec((CHUNK, wide), rev),
                  _spec((None, hps, HD, HD), lambda h, s: (nc - 1 - s, h, 0, 0)),
                  _spec((CHUNK, wide), rev)] + (rider.in_specs if nr else []),
        out_specs=[_spec((4, CHUNK, wide), lambda h, s: (0, nc - 1 - s, h)),
                   _spec((2, wide), lambda h, s: (0, h)),
                   _spec((1, HD), lambda h, s: (0, 0))] + (rider.out_specs if nr else []),
        out_shape=[jax.ShapeDtypeStruct((4, t, D), BF16), jax.ShapeDtypeStruct((2, D), F32),
                   jax.ShapeDtypeStruct((1, HD), F32)] + (rider.out_shape if nr else []),
        scratch_shapes=[pltpu.VMEM((hps, HD, HD), F32), pltpu.VMEM((1, wide), F32), pltpu.VMEM((1, HD), F32)]
        + (rider.scratch if nr else []),
        compiler_params=_params(("arbitrary", "arbitrary")),
    )(proj_c, small, small, small, o, sst, dyc, *(rider.groups if nr else []))
    return outs[0], outs[1], outs[2], (rider.place(outs[3:]) if nr else [])


def _adamw(name, w, g, m, v):
    rows, cols = w.shape
    br = rows
    for cand in (512, 352, 256):
        if rows % cand == 0:
            br = cand
            break
    c1 = 1.0 - ADAM_B1 ** ADAM_STEP
    c2 = 1.0 - ADAM_B2 ** ADAM_STEP

    def body(w_ref, g_ref, m_ref, v_ref, d_ref, mo_ref, vo_ref):
        gv = g_ref[...]
        mn = ADAM_B1 * m_ref[...] + (1.0 - ADAM_B1) * gv
        vn = ADAM_B2 * v_ref[...] + (1.0 - ADAM_B2) * (gv * gv)
        mo_ref[...] = mn
        vo_ref[...] = vn
        d_ref[...] = -ADAM_LR * ((mn / c1) / (jnp.sqrt(vn / c2) + ADAM_EPS) + ADAM_WD * w_ref[...])

    blk = _spec((br, cols), lambda i: (i, 0))
    shape = jax.ShapeDtypeStruct((rows, cols), F32)
    return pl.pallas_call(
        body, name=name, grid=(rows // br,), in_specs=[blk] * 4, out_specs=[blk] * 3, out_shape=[shape] * 3,
        compiler_params=_params(("parallel",)),
    )(w, g, m, v)


def _place():
    x, y, c = lax.axis_index("x"), lax.axis_index("y"), lax.axis_index("c")
    chips = [(1 - x, y), (x, 1 - y), (1 - x, 1 - y)]
    return x, y, c, chips


class _Rider:
    n = 0

    def ride(self, grid, refs, when):
        if not self.n:
            return
        ids = [pl.program_id(a) for a in range(len(grid))]
        edge = [i == (0 if when == "start" else g - 1) for i, g in zip(ids, grid)]
        cond = functools.reduce(jnp.logical_and, edge)

        @pl.when(cond)
        def _():
            (self.start if when == "start" else self.finish)(*refs)


class _Gather(_Rider):
    def __init__(self, groups):
        self.groups = list(groups)
        self.n = len(self.groups)
        self.in_specs = [ANY] * self.n
        self.out_specs = [ANY] * self.n
        self.out_shape = [jax.ShapeDtypeStruct((NSH,) + g.shape, g.dtype) for g in self.groups]
        self.scratch = [pltpu.SemaphoreType.DMA((6 * self.n,)), pltpu.SemaphoreType.DMA((6 * self.n,))] if self.n else []

    def _copies(self, ins, outs, send, recv):
        x, y, c, chips = _place()
        sibling = (x, y, 1 - c)

        def half(gi, chip, hc):
            return outs[gi].at[2 * chip[0] + chip[1], hc]

        def copy(gi, k, src, dst, to):
            return pltpu.make_async_remote_copy(src_ref=src, dst_ref=dst, send_sem=send.at[6 * gi + k],
                                                recv_sem=recv.at[6 * gi + k], device_id=to, device_id_type=MESH)

        pairs = [(gi, j, chip) for gi in range(self.n) for j, chip in enumerate(chips)]
        first = [copy(gi, j, ins[gi].at[c], half(gi, (x, y), c), (*chip, c)) for gi, j, chip in pairs]
        landed = [copy(gi, j, half(gi, chip, c), half(gi, chip, c), sibling) for gi, j, chip in pairs]
        relay = [copy(gi, 3 + j, half(gi, chip, c), half(gi, chip, c), sibling) for gi, j, chip in pairs]
        relayed = [copy(gi, 3 + j, half(gi, chip, 1 - c), half(gi, chip, 1 - c), sibling) for gi, j, chip in pairs]
        return first, landed, relay, relayed

    def start(self, ins, outs, send, recv):
        for cp in self._copies(ins, outs, send, recv)[0]:
            cp.start()

    def finish(self, ins, outs, send, recv):
        first, landed, relay, relayed = self._copies(ins, outs, send, recv)
        for arrived, onward in zip(landed, relay):
            arrived.wait_recv()
            onward.start()
        for cp in relayed:
            cp.wait_recv()
        for cp in first + relay:
            cp.wait_send()

    def place(self, outs):
        me = 2 * lax.axis_index("x") + lax.axis_index("y")
        return [lax.dynamic_update_index_in_dim(o, g, me, 0) for o, g in zip(outs, self.groups)]


def _gather_weights(groups):
    gather = _Gather(groups)
    ng = gather.n

    def body(*refs):
        parts = (refs[:ng], refs[ng:2 * ng], refs[2 * ng], refs[2 * ng + 1])
        gather.start(*parts)
        gather.finish(*parts)

    outs = pl.pallas_call(
        body, name="gather_weights", in_specs=gather.in_specs, out_specs=gather.out_specs, out_shape=gather.out_shape,
        scratch_shapes=gather.scratch, compiler_params=pltpu.CompilerParams(has_side_effects=True),
    )(*groups)
    return gather.place(outs)


def _swap_halves(name, slots):
    n = len(slots)

    def body(*refs):
        ins, outs = refs[:n], refs[n:2 * n]
        send, recv = refs[2 * n:]
        x, y, c, _ = _place()
        cps = []
        for i, (_, row0, rows) in enumerate(slots):
            half = rows // 2
            src = ins[i].at[:, pl.ds(pl.multiple_of(row0 + (1 - c) * half, 16), half)]
            cps.append(pltpu.make_async_remote_copy(src_ref=src, dst_ref=outs[i], send_sem=send.at[i],
                                                    recv_sem=recv.at[i], device_id=(x, y, 1 - c), device_id_type=MESH))
        for cp in cps:
            cp.start()
        for cp in cps:
            cp.wait()

    return pl.pallas_call(
        body, name=name, in_specs=[ANY] * n, out_specs=[ANY] * n,
        out_shape=[jax.ShapeDtypeStruct((NSH, rows // 2, buf.shape[2]), buf.dtype) for buf, _, rows in slots],
        scratch_shapes=[pltpu.SemaphoreType.DMA((n,)), pltpu.SemaphoreType.DMA((n,))],
        compiler_params=pltpu.CompilerParams(has_side_effects=True),
    )(*[buf for buf, _, _ in slots])


class _Send(_Rider):
    def __init__(self, arrays):
        self.groups = list(arrays)
        self.n = len(self.groups)
        self.in_specs = [ANY] * self.n
        self.out_specs = [ANY] * self.n
        self.out_shape = [jax.ShapeDtypeStruct((3,) + g.shape[1:], g.dtype) for g in self.groups]
        self.scratch = [pltpu.SemaphoreType.DMA((3 * self.n,)), pltpu.SemaphoreType.DMA((3 * self.n,))]

    def _copies(self, ins, outs, send, recv):
        x, y, c, chips = _place()
        return [pltpu.make_async_remote_copy(src_ref=ins[gi].at[2 * chip[0] + chip[1]], dst_ref=outs[gi].at[j],
                                             send_sem=send.at[3 * gi + j], recv_sem=recv.at[3 * gi + j],
                                             device_id=(*chip, c), device_id_type=MESH)
                for gi in range(self.n) for j, chip in enumerate(chips)]

    def start(self, ins, outs, send, recv):
        for cp in self._copies(ins, outs, send, recv):
            cp.start()

    def finish(self, ins, outs, send, recv):
        for cp in self._copies(ins, outs, send, recv):
            cp.wait()

    def place(self, outs):
        return list(outs)


def _share_halves(groups):
    ng = len(groups)

    def body(*refs):
        ins, outs = refs[:ng], refs[ng:2 * ng]
        send, recv = refs[2 * ng:]
        x, y, c, _ = _place()
        cps = [pltpu.make_async_remote_copy(src_ref=ins[gi], dst_ref=outs[gi].at[c], send_sem=send.at[gi],
                                            recv_sem=recv.at[gi], device_id=(x, y, 1 - c), device_id_type=MESH)
               for gi in range(ng)]
        for cp in cps:
            cp.start()
        for gi in range(ng):
            pltpu.make_async_remote_copy(src_ref=ins[gi], dst_ref=outs[gi].at[1 - c], send_sem=send.at[gi],
                                         recv_sem=recv.at[gi], device_id=(x, y, 1 - c), device_id_type=MESH).wait_recv()
        for cp in cps:
            cp.wait_send()

    outs = pl.pallas_call(
        body, name="grad_share_halves", in_specs=[ANY] * ng, out_specs=[ANY] * ng,
        out_shape=[jax.ShapeDtypeStruct((2,) + g.shape, g.dtype) for g in groups],
        scratch_shapes=[pltpu.SemaphoreType.DMA((ng,)), pltpu.SemaphoreType.DMA((ng,))],
        compiler_params=pltpu.CompilerParams(has_side_effects=True),
    )(*groups)
    c = lax.axis_index("c")
    return [lax.dynamic_update_index_in_dim(o, g, c, 0) for o, g in zip(outs, groups)]


def _pair_sum(name, slots, got, c_idx):
    n = len(slots)
    in_specs, out_specs, out_shape, operands = [], [], [], []
    for (buf, row0, rows), g in zip(slots, got):
        hb, cols = rows // 4, buf.shape[2]
        in_specs += [pl.BlockSpec((None, hb, cols), lambda q, i, cr, r=row0 // hb: (q, r + 2 * cr[0] + i, 0)),
                     pl.BlockSpec((None, hb, cols), lambda q, i, cr: (q, i, 0))]
        out_specs.append(pl.BlockSpec((None, hb, cols), lambda q, i, cr: (q, i, 0)))
        out_shape.append(jax.ShapeDtypeStruct(g.shape, BF16))
        operands += [buf, g]

    def body(c_ref, *refs):
        for i in range(n):
            refs[2 * n + i][...] = (refs[2 * i][...].astype(F32) + refs[2 * i + 1][...].astype(F32)).astype(BF16)

    return pl.pallas_call(
        body, name=name,
        grid_spec=pltpu.PrefetchScalarGridSpec(num_scalar_prefetch=1, grid=(NSH, 2), in_specs=in_specs, out_specs=out_specs),
        out_shape=out_shape, compiler_params=_params(("parallel", "parallel")),
    )(c_idx, *operands)


def _owner_sum(name, pairs, got, p_idx):
    n = len(pairs)
    in_specs, out_specs, out_shape, operands = [], [], [], []
    for own, g in zip(pairs, got):
        _, rows, cols = own.shape
        hb = rows // 2
        in_specs += [pl.BlockSpec((None, hb, cols), lambda i, pr: (pr[0], i, 0)),
                     pl.BlockSpec((3, hb, cols), lambda i, pr: (0, i, 0))]
        out_specs.append(pl.BlockSpec((hb, cols), lambda i, pr: (i, 0)))
        out_shape.append(jax.ShapeDtypeStruct((rows, cols), F32))
        operands += [own, g]

    def body(p_ref, *refs):
        for i in range(n):
            a_ref, b_ref = refs[2 * i], refs[2 * i + 1]
            refs[2 * n + i][...] = ((a_ref[...].astype(F32) + b_ref[0].astype(F32)) + b_ref[1].astype(F32)) + b_ref[2].astype(F32)

    return pl.pallas_call(
        body, name=name,
        grid_spec=pltpu.PrefetchScalarGridSpec(num_scalar_prefetch=1, grid=(2,), in_specs=in_specs, out_specs=out_specs),
        out_shape=out_shape, compiler_params=_params(("parallel",)),
    )(p_idx, *operands)


def _sum_small(slab):
    def body(in_ref, out_ref, all_ref, send, recv):
        x, y, c, _ = _place()
        me = 4 * x + 2 * y + c
        all_ref[me] = in_ref[...]
        cps = []
        for k in range(1, 8):
            peer = (x ^ (k >> 2), y ^ ((k >> 1) & 1), c ^ (k & 1))
            cps.append(pltpu.make_async_remote_copy(src_ref=in_ref, dst_ref=all_ref.at[me], send_sem=send.at[k - 1],
                                                    recv_sem=recv.at[k - 1], device_id=peer, device_id_type=MESH))
        for cp in cps:
            cp.start()
        for cp in cps:
            cp.wait()
        total = all_ref[0]
        for d in range(1, 8):
            total = total + all_ref[d]
        out_ref[...] = total

    return pl.pallas_call(
        body, name="sum_small",
        in_specs=[pl.BlockSpec(memory_space=pltpu.VMEM)], out_specs=pl.BlockSpec(memory_space=pltpu.VMEM),
        out_shape=jax.ShapeDtypeStruct(slab.shape, F32),
        scratch_shapes=[pltpu.VMEM((8,) + slab.shape, F32), pltpu.SemaphoreType.DMA((7,)), pltpu.SemaphoreType.DMA((7,))],
        compiler_params=pltpu.CompilerParams(has_side_effects=True),
    )(slab)


FFNS = ("pre0", "post0", "pre1", "post1")
W_SHAPES = dict({f + "_gu": (NSH, 2, FS, D) for f in FFNS}, **{f + "_d": (NSH, FS, D) for f in FFNS},
                ab_in=(NSH, D, 768), ab_out=(NSH, 256, D), conv=(NSH, 2, 8, LANE), c_in=(NSH, D, D), c_out=(NSH, 256, D))
CARRIED = dict(pre0_up=("pre0_d",), pre0_down=("ab_in", "ab_out", "conv"),
               sb_fwd=("post0_gu", "post0_d", "pre1_gu"), post0_up=("c_in", "c_out"), post0_down=("pre1_d",),
               hgrn_fwd=("post1_gu", "post1_d"))


FFN_SLOTS = dict(pre0=(0, 2, 0), pre1=(1, 3, 1), post0=(4, 6, 2), post1=(5, 7, 3))
GRAD_SLOTS = dict(ab_out=("b", B_ABOUT, 256), c_in=("b", B_CIN, D), c_out=("b", B_COUT, 256), ab_in=("c", 0, D))
for _f, (_g, _u, _d) in FFN_SLOTS.items():
    GRAD_SLOTS.update({_f + "_g": ("a", _g * FS, FS), _f + "_u": ("a", _u * FS, FS), _f + "_d": ("b", _d * FS, FS)})
REDUCE_STAGES = dict(x=("post1_g", "post1_u", "post1_d", "c_out"),
                     y=("c_in", "pre1_g", "pre1_u", "pre1_d", "post0_g", "post0_u", "post0_d", "ab_out"),
                     z1=("ab_in",), z2=("pre0_g", "pre0_u", "pre0_d"))


def _local_step(x, target, weights, small, shards=None, place=None):
    t = x.shape[0]
    tm = min(MM_TILE, t)
    tw = min(WGRAD_TILE, t)
    nt = t // tm
    tok_si = _spec((tm, D), lambda s, i: (i, 0))
    w = dict(weights)
    reduced = {}

    def reduce_start(stage, buffers):
        if place is None:
            return (), [], None
        names = REDUCE_STAGES[stage]
        slots = [(buffers[GRAD_SLOTS[n][0]],) + GRAD_SLOTS[n][1:] for n in names]
        got = _swap_halves("grad_swap_" + stage, slots)
        pairs = _pair_sum("grad_pair_sum_" + stage, slots, got, place[0])
        return names, pairs, _Send(pairs)

    def reduce_end(stage, names, pairs, landed):
        if place is not None:
            reduced.update(zip(names, _owner_sum("grad_owner_sum_" + stage, pairs, landed, place[1])))

    def carried(kernel_name):
        names = [n for n in CARRIED[kernel_name] if n not in w]
        return names, (_Gather([shards[n] for n in names]) if names else None)

    def land(names, arrays):
        for n, a in zip(names, arrays):
            w[n] = a.reshape(W_SHAPES[n])

    def ffn_forward(tag, h, norm_row):
        hn = _norm_fwd(tag + "_norm", h, small, norm_row)
        names, gather = carried(tag + "_up") if tag + "_up" in CARRIED else ([], None)
        s_up, s_gate, a, got = _ffn_up(tag + "_up", hn, w[tag + "_gu"], gather)
        land(names, got)
        names, gather = carried(tag + "_down") if tag + "_down" in CARRIED else ([], None)
        out = _ffn_down(tag + "_down", a, w[tag + "_d"], h, gather)
        if gather is not None:
            out, got = out
            land(names, got)
        return out, (h, hn, s_up, s_gate, a)

    def out_proj(name, y, w_out, h):
        return _mm(name, [(y, _spec((tm, 256), lambda i, k: (i, k)), w_out, _spec((None, 256, D), lambda i, k: (k, 0, 0)))],
                   grid=(nt, NSH), o_shape=(t, D), o_dtype=F32, o_spec=_spec((tm, D), lambda i, k: (i, 0)),
                   dims=NN, kaxis=1, nk=NSH, acc_shape=(tm, D), res=(h, _spec((tm, D), lambda i, k: (i, 0))))

    def out_proj_bwd(tag, dhb, y, w_out, blk, grad_b):
        dy = _mm(tag + "_dy", [(dhb, tok_si, w_out, _spec((None, 256, D), lambda s, i: (s, 0, 0)))],
                 grid=(NSH, nt), o_shape=(t, D), o_dtype=F32, o_spec=_spec((tm, 256), lambda s, i: (i, s)),
                 dims=NT, kaxis=1, nk=1)
        grad_b = _wgrad(tag + "_dwout", y, _spec((tw, 256), lambda s, k: (k, s)), dhb, _spec((tw, D), lambda s, k: (k, 0)),
                        256, D, t, tw, grad_b, blk)
        return dy, grad_b

    h0 = x
    h1, pre0 = ffn_forward("pre0", h0, R_PRE)
    hn_ab = _norm_fwd("mix0_norm", h1, small, R_MIX)
    proj_a = _mm("ab_proj_a", [(hn_ab, tok_si, w["ab_in"], _spec((None, D, 768), lambda s, i: (s, 0, 0)))],
                 grid=(2, nt), o_shape=(t, 1536), o_dtype=F32, o_spec=_spec((tm, 768), lambda s, i: (i, s)),
                 dims=NN, kaxis=1, nk=1)
    proj_b = _mm("ab_proj_b", [(hn_ab, tok_si, w["ab_in"], _spec((None, D, 768), lambda s, i: (s + 2, 0, 0)))],
                 grid=(2, nt), o_shape=(t, 1536), o_dtype=BF16, o_spec=_spec((tm, 768), lambda s, i: (i, s)),
                 dims=NN, kaxis=1, nk=1)
    y_a = _conv_fwd(proj_a, w["conv"])
    names, gather = carried("sb_fwd")
    y_b, ltot, nblk, got = _sb_fwd(proj_b, gather)
    land(names, got)
    y_ab = jnp.concatenate([y_a, y_b], axis=1)
    h2 = out_proj("ab_out", y_ab, w["ab_out"], h1)
    h3, post0 = ffn_forward("post0", h2, R_POST)
    h4, pre1 = ffn_forward("pre1", h3, R_PRE + 1)
    hn_c = _norm_fwd("mix1_norm", h4, small, R_MIX + 1)
    proj_c = _mm("c_proj", [(hn_c, tok_si, w["c_in"], _spec((None, D, D), lambda s, i: (s, 0, 0)))],
                 grid=(NSH, nt), o_shape=(NSH, t, D), o_dtype=F32, o_spec=_spec((None, tm, D), lambda s, i: (s, i, 0)),
                 dims=NN, kaxis=1, nk=1)
    names, gather = carried("hgrn_fwd")
    o_c, y_c, sst, got = _hgrn_fwd(proj_c, small, gather)
    land(names, got)
    h5 = out_proj("c_out", y_c, w["c_out"], h4)
    h6, post1 = ffn_forward("post1", h5, R_POST + 1)
    dh, dhb, d_fin, loss = _final_loss(h6, small, target)

    dh, dhb, grad_a, grad_b, dn_post1, _, _ = _ffn_backward("post1", dh, dhb, *post1, w["post1_gu"], w["post1_d"],
                                                            *FFN_SLOTS["post1"], small, R_POST + 1, 8 * FS, B_ROWS)
    dy_c, grad_b = out_proj_bwd("c", dhb, y_c, w["c_out"], B_COUT // 256, grad_b)
    names, pairs, rider = reduce_start("x", dict(a=grad_a, b=grad_b))
    dproj_c, d_clb, d_gam, landed = _hgrn_bwd(proj_c, small, o_c, sst, dy_c, rider)
    reduce_end("x", names, pairs, landed)
    grad_b = _wgrad("c_dwin", hn_c, _spec((tw, D), lambda s, k: (k, 0)), dproj_c, _spec((None, tw, D), lambda s, k: (s, k, 0)),
                    D, D, t, tw, grad_b, B_CIN // D)
    dhn = _mm("c_dhn", [(dproj_c, _spec((None, tm, D), lambda i, k: (k, i, 0)),
                         w["c_in"], _spec((None, D, D), lambda i, k: (k, 0, 0)))],
              grid=(nt, NSH), o_shape=(t, D), o_dtype=F32, o_spec=_spec((tm, D), lambda i, k: (i, 0)),
              dims=NT, kaxis=1, nk=NSH, acc_shape=(tm, D))
    dh, dhb, dn_mix1 = _norm_bwd("mix1_norm_bwd", dhn, h4, small, R_MIX + 1, dh)
    dh, dhb, grad_a, grad_b, dn_pre1, _, _ = _ffn_backward("pre1", dh, dhb, *pre1, w["pre1_gu"], w["pre1_d"],
                                                           *FFN_SLOTS["pre1"], small, R_PRE + 1, grad_a, grad_b)
    dh, dhb, grad_a, grad_b, dn_post0, _, _ = _ffn_backward("post0", dh, dhb, *post0, w["post0_gu"], w["post0_d"],
                                                            *FFN_SLOTS["post0"], small, R_POST, grad_a, grad_b)
    dy_ab, grad_b = out_proj_bwd("ab", dhb, y_ab, w["ab_out"], B_ABOUT // 256, grad_b)
    dab, dac, dax, d_conv = _conv_bwd(proj_a, w["conv"], dy_ab)
    names, pairs, rider = reduce_start("y", dict(a=grad_a, b=grad_b))
    dq, dk, dv, landed = _sb_bwd(proj_b, dy_ab, ltot, nblk, rider)
    reduce_end("y", names, pairs, landed)
    dproj_ab = jnp.concatenate([dab, dac, dax, dq, dk, dv], axis=1)
    grad_c = _wgrad("ab_dwin", hn_ab, _spec((tw, D), lambda s, k: (k, 0)), dproj_ab, _spec((tw, 768), lambda s, k: (k, s)),
                    D, 768, t, tw, D, 0)
    dhn = _mm("ab_dhn", [(dproj_ab, _spec((tm, 768), lambda i, k: (i, k)),
                          w["ab_in"], _spec((None, D, 768), lambda i, k: (k, 0, 0)))],
              grid=(nt, NSH), o_shape=(t, D), o_dtype=F32, o_spec=_spec((tm, D), lambda i, k: (i, 0)),
              dims=NT, kaxis=1, nk=NSH, acc_shape=(tm, D))
    dh, dhb, dn_mix0 = _norm_bwd("mix0_norm_bwd", dhn, h1, small, R_MIX, dh)
    names, pairs, rider = reduce_start("z1", dict(c=grad_c))
    last = {}

    def own_gradients(buf_a, buf_b):
        last["z2"] = reduce_start("z2", dict(a=buf_a, b=buf_b))
        return last["z2"][2]

    dh, dhb, grad_a, grad_b, dn_pre0, landed, landed_own = _ffn_backward(
        "pre0", dh, dhb, *pre0, w["pre0_gu"], w["pre0_d"], *FFN_SLOTS["pre0"], small, R_PRE, grad_a, grad_b,
        rider, own_gradients if place is not None else None)
    reduce_end("z1", names, pairs, landed)
    if place is not None:
        reduce_end("z2", last["z2"][0], last["z2"][1], landed_own)
    zero = jnp.zeros((1, D), F32)
    conv_rows = jnp.pad(jnp.transpose(d_conv[:, :3, :], (1, 0, 2)).reshape(3, 512), ((0, 0), (0, D - 512)))
    small_grad = jnp.concatenate([
        dn_pre0, dn_pre1, dn_mix0, dn_mix1, dn_post0, dn_post1, d_clb, d_fin,
        jnp.pad(d_gam, ((0, 0), (0, D - HD))), conv_rows,
        jnp.pad(loss, ((0, 0), (0, D - 1))), zero, zero], axis=0)
    return dh, grad_a, grad_b, grad_c, small_grad, reduced


def _small_slab(rows):
    parts = [jnp.pad(r.astype(F32), ((0, 0), (0, D - r.shape[1]))) for r in rows]
    slab = jnp.concatenate(parts, axis=0)
    return jnp.pad(slab, ((0, SMALL_ROWS - slab.shape[0]), (0, 0)))


def kernel(x, ffn_pre_norm, ffn_pre_w_gate, ffn_pre_w_up, ffn_pre_w_down, mix_norm, ffn_post_norm, ffn_post_w_gate, ffn_post_w_up, ffn_post_w_down, ab_w_in, ab_conv_w, ab_w_out, c_w_in, c_lower_bounds, c_out_norm, c_w_out, final_norm, loss_target, m_ffn_pre_norm, m_ffn_pre_w_gate, m_ffn_pre_w_up, m_ffn_pre_w_down, m_mix_norm, m_ffn_post_norm, m_ffn_post_w_gate, m_ffn_post_w_up, m_ffn_post_w_down, m_ab_w_in, m_ab_conv_w, m_ab_w_out, m_c_w_in, m_c_lower_bounds, m_c_out_norm, m_c_w_out, m_final_norm, v_ffn_pre_norm, v_ffn_pre_w_gate, v_ffn_pre_w_up, v_ffn_pre_w_down, v_mix_norm, v_ffn_post_norm, v_ffn_post_w_gate, v_ffn_post_w_up, v_ffn_post_w_down, v_ab_w_in, v_ab_conv_w, v_ab_w_out, v_c_w_in, v_c_lower_bounds, v_c_out_norm, v_c_w_out, v_final_norm):
    t = x.shape[1]
    xi, yi, ci = lax.axis_index("x"), lax.axis_index("y"), lax.axis_index("c")
    p_idx = (2 * xi + yi).astype(jnp.int32).reshape(1)
    c_idx = ci.astype(jnp.int32).reshape(1)

    def halves(m):
        return m.astype(BF16).reshape(2, m.shape[0] // 2, m.shape[1])

    transposed = ("ffn_pre_w_gate", "ffn_pre_w_up", "ffn_post_w_gate", "ffn_post_w_up")

    def flip(a):
        return jnp.swapaxes(a, 1, 2)

    shards = {}
    for name, (w_gate, w_up, w_down, layer) in dict(
            pre0=(ffn_pre_w_gate, ffn_pre_w_up, ffn_pre_w_down, 0), post0=(ffn_post_w_gate, ffn_post_w_up, ffn_post_w_down, 0),
            pre1=(ffn_pre_w_gate, ffn_pre_w_up, ffn_pre_w_down, 1), post1=(ffn_post_w_gate, ffn_post_w_up, ffn_post_w_down, 1)).items():
        shards[name + "_gu"] = jnp.stack([flip(w_gate)[layer], flip(w_up)[layer]]).astype(BF16)
        shards[name + "_d"] = halves(w_down[layer])
    conv_pad = jnp.pad(ab_conv_w[0], ((0, 5), (0, 0)))
    shards.update(ab_in=halves(ab_w_in[0]), ab_out=halves(ab_w_out[0]), c_in=halves(c_w_in[0]), c_out=halves(c_w_out[0]),
                  conv=jnp.stack([conv_pad, jnp.zeros_like(conv_pad)]))
    first = _gather_weights([shards["pre0_gu"]])[0].reshape(W_SHAPES["pre0_gu"])

    small = _small_slab([ffn_pre_norm, mix_norm, ffn_post_norm, c_lower_bounds, final_norm.reshape(1, D), c_out_norm])
    small = small.reshape(SMALL_ROWS, 1, D)

    grad_x, _, _, _, small_grad, reduced = _local_step(x[0], loss_target[0], dict(pre0_gu=first), small, shards,
                                                        (c_idx, p_idx))
    order = list(reduced)
    whole = {n: g.reshape(2 * g.shape[1], g.shape[2]) for n, g in zip(order, _share_halves([reduced[n] for n in order]))}
    small_sum = _sum_small(small_grad)

    my_conv = lax.dynamic_slice(small_sum[R_CONV:R_CONV + 3], (0, (2 * xi + yi) * 128), (3, 128))
    grads = {
        "ffn_pre_norm": small_sum[R_PRE:R_PRE + 2], "mix_norm": small_sum[R_MIX:R_MIX + 2],
        "ffn_post_norm": small_sum[R_POST:R_POST + 2], "c_lower_bounds": small_sum[R_CLB:R_CLB + 2],
        "c_out_norm": small_sum[R_GAM:R_GAM + 1, :HD], "final_norm": small_sum[R_FIN],
        "ab_conv_w": my_conv.reshape(1, 3, 128),
        "ab_w_in": whole["ab_in"][None], "ab_w_out": whole["ab_out"][None],
        "c_w_in": whole["c_in"][None], "c_w_out": whole["c_out"][None],
    }
    for kind, key in (("gate", "_g"), ("up", "_u"), ("down", "_d")):
        grads["ffn_pre_w_" + kind] = jnp.stack([whole["pre0" + key], whole["pre1" + key]])
        grads["ffn_post_w_" + kind] = jnp.stack([whole["post0" + key], whole["post1" + key]])
    weights = dict(ffn_pre_norm=ffn_pre_norm, ffn_pre_w_gate=ffn_pre_w_gate, ffn_pre_w_up=ffn_pre_w_up, ffn_pre_w_down=ffn_pre_w_down, mix_norm=mix_norm, ffn_post_norm=ffn_post_norm, ffn_post_w_gate=ffn_post_w_gate, ffn_post_w_up=ffn_post_w_up, ffn_post_w_down=ffn_post_w_down, ab_w_in=ab_w_in, ab_conv_w=ab_conv_w, ab_w_out=ab_w_out, c_w_in=c_w_in, c_lower_bounds=c_lower_bounds, c_out_norm=c_out_norm, c_w_out=c_w_out, final_norm=final_norm)
    m_in = dict(ffn_pre_norm=m_ffn_pre_norm, ffn_pre_w_gate=m_ffn_pre_w_gate, ffn_pre_w_up=m_ffn_pre_w_up, ffn_pre_w_down=m_ffn_pre_w_down, mix_norm=m_mix_norm, ffn_post_norm=m_ffn_post_norm, ffn_post_w_gate=m_ffn_post_w_gate, ffn_post_w_up=m_ffn_post_w_up, ffn_post_w_down=m_ffn_post_w_down, ab_w_in=m_ab_w_in, ab_conv_w=m_ab_conv_w, ab_w_out=m_ab_w_out, c_w_in=m_c_w_in, c_lower_bounds=m_c_lower_bounds, c_out_norm=m_c_out_norm, c_w_out=m_c_w_out, final_norm=m_final_norm)
    v_in = dict(ffn_pre_norm=v_ffn_pre_norm, ffn_pre_w_gate=v_ffn_pre_w_gate, ffn_pre_w_up=v_ffn_pre_w_up, ffn_pre_w_down=v_ffn_pre_w_down, mix_norm=v_mix_norm, ffn_post_norm=v_ffn_post_norm, ffn_post_w_gate=v_ffn_post_w_gate, ffn_post_w_up=v_ffn_post_w_up, ffn_post_w_down=v_ffn_post_w_down, ab_w_in=v_ab_w_in, ab_conv_w=v_ab_conv_w, ab_w_out=v_ab_w_out, c_w_in=v_c_w_in, c_lower_bounds=v_c_lower_bounds, c_out_norm=v_c_out_norm, c_w_out=v_c_w_out, final_norm=v_final_norm)
    names = list(weights)
    big = [n for n in names if weights[n].size >= 65536]
    tiny = [n for n in names if n not in big]

    delta, new_m, new_v = {}, {}, {}
    for n in big:
        turn = flip if n in transposed else (lambda a: a)
        shape = turn(weights[n]).shape
        two_d = (shape[0] * shape[1], shape[2])
        d, m2, v2 = _adamw("adamw_" + n, turn(weights[n]).reshape(two_d), grads[n].reshape(two_d),
                           turn(m_in[n]).reshape(two_d), turn(v_in[n]).reshape(two_d))
        delta[n], new_m[n], new_v[n] = turn(d.reshape(shape)), turn(m2.reshape(shape)), turn(v2.reshape(shape))
        grads[n] = turn(grads[n])

    def tiny_slab(src):
        return _small_slab([src[n].reshape(-1, src[n].shape[-1]) for n in tiny])

    offs, row = {}, 0
    for n in tiny:
        nrows = weights[n].size // weights[n].shape[-1]
        offs[n] = (row, nrows)
        row += nrows
    d, m2, v2 = _adamw("adamw_small", tiny_slab(weights), tiny_slab(grads), tiny_slab(m_in), tiny_slab(v_in))
    for n in tiny:
        r0, nr = offs[n]
        shape = weights[n].shape
        for dst, src in ((delta, d), (new_m, m2), (new_v, v2)):
            dst[n] = src[r0:r0 + nr, :shape[-1]].reshape(shape)

    loss = small_sum[R_LOSS, 0]
    return (loss, grad_x.reshape(1, t, D), *[grads[n] for n in names], *[delta[n] for n in names],
            *[new_m[n] for n in names], *[new_v[n] for n in names])
```

```python
import functools
import math

import jax
import jax.numpy as jnp
from jax import lax
from jax.experimental import pallas as pl
from jax.experimental.pallas import tpu as pltpu

F32 = jnp.float32
BF16 = jnp.bfloat16
MESH = pl.DeviceIdType.MESH
ANY = pl.BlockSpec(memory_space=pl.ANY)

D = 1024
FS = 704
NSH = 4
RMS_EPS = 1e-6
MACARON = 0.5
CHUNK = 64
HD = 128
HGRN_HPS = 4
SBQ = 128
SB_PAIRS = 2
SB_DEAD = -105.0
CONV_HALO = 8
LANE = 128
ROW_TILE = 512
MM_TILE = 1024
WGRAD_TILE = 4096
VMEM_LIMIT = 48 * 1024 * 1024
VMEM_LIMIT_BIG = 58 * 1024 * 1024

ADAM_LR, ADAM_B1, ADAM_B2, ADAM_EPS, ADAM_WD, ADAM_STEP = 0.001, 0.9, 0.999, 1e-08, 0.01, 10

NN = ((1,), (0,))
NT = ((1,), (1,))
TN = ((0,), (0,))

SMALL_ROWS = 16
R_PRE, R_MIX, R_POST, R_CLB, R_FIN, R_GAM, R_CONV, R_LOSS = 0, 2, 4, 6, 8, 9, 10, 13

B_DOWN = 0
B_ABOUT = 4 * FS
B_CIN = B_ABOUT + 256
B_COUT = B_CIN + 1024
B_ROWS = B_COUT + 256


def _dg(a, b, dims):
    return lax.dot_general(a, b, (dims, ((), ())), preferred_element_type=F32)


def _split(x):
    hi = x.astype(BF16)
    lo = (x - hi.astype(F32)).astype(BF16)
    return hi, lo


def _dot3(a, b, dims):
    ah, al = _split(a)
    bh, bl = _split(b)
    if dims == TN:
        n = b.shape[1]
        both = _dg(ah, jnp.concatenate([bh, bl], axis=1), dims)
        return both[:, :n] + both[:, n:] + _dg(al, bh, dims)
    m = a.shape[0]
    both = _dg(jnp.concatenate([ah, al], axis=0), bh, dims)
    return both[:m] + both[m:] + _dg(ah, bl, dims)


def _sigmoid(x):
    return 1.0 / (1.0 + jnp.exp(-x))


def _params(sem):
    return pltpu.CompilerParams(dimension_semantics=sem, vmem_limit_bytes=VMEM_LIMIT)


def _spec(shape, imap):
    return pl.BlockSpec(shape, imap)


def _mm(name, pairs, *, grid, o_shape, o_dtype, o_spec, dims, kaxis, nk, acc_shape=None, res=None, scale=None,
        into=None, carry=None):
    npairs = len(pairs)
    operands, specs = [], []
    for a, a_spec, b, b_spec in pairs:
        operands += [a, b]
        specs += [a_spec, b_spec]
    if res is not None:
        operands.append(res[0])
        specs.append(res[1])
    aliases = {}
    if into is not None:
        aliases = {len(operands): 0}
        operands.append(into)
        specs.append(ANY)
    n_own = len(operands)
    nc = carry.n if carry is not None else 0
    if nc:
        operands += carry.groups
        specs += carry.in_specs

    def body(*refs):
        o_ref = refs[n_own + nc]
        riders = (refs[n_own:n_own + nc], refs[n_own + nc + 1:n_own + 2 * nc + 1], refs[-2], refs[-1]) if nc else None
        if nc:
            carry.ride(grid, riders, "start")
        part = None
        for n in range(npairs):
            d = _dg(refs[2 * n][...], refs[2 * n + 1][...], dims)
            part = d if part is None else part + d

        def finish(val):
            if scale is not None:
                val = val * scale
            if res is not None:
                val = val + refs[2 * npairs][...]
            o_ref[...] = val.astype(o_dtype)

        if nk == 1:
            finish(part)
        else:
            acc_ref = refs[n_own + 2 * nc + 1]
            k = pl.program_id(kaxis)

            @pl.when(k == 0)
            def _():
                acc_ref[...] = part

            @pl.when(k > 0)
            def _():
                acc_ref[...] += part

            @pl.when(k == nk - 1)
            def _():
                finish(acc_ref[...])
        if nc:
            carry.ride(grid, riders, "finish")

    sem = tuple("arbitrary" if (nc or (ax == kaxis and nk > 1)) else "parallel" for ax in range(len(grid)))
    outs = pl.pallas_call(
        body, name=name, grid=grid, in_specs=specs, out_specs=[o_spec] + (carry.out_specs if nc else []),
        out_shape=[jax.ShapeDtypeStruct(o_shape, o_dtype)] + (carry.out_shape if nc else []),
        scratch_shapes=([pltpu.VMEM(acc_shape, F32)] if nk > 1 else []) + (carry.scratch if nc else []),
        input_output_aliases=aliases,
        compiler_params=_params(sem),
    )(*operands)
    return (outs[0], carry.place(outs[1:])) if nc else outs[0]


def _norm_fwd(name, h, gain_slab, row):
    t = h.shape[0]
    tm = min(ROW_TILE, t)

    def body(h_ref, g_ref, o_ref):
        x = h_ref[...]
        r = lax.rsqrt(jnp.mean(x * x, axis=-1, keepdims=True) + RMS_EPS)
        o_ref[...] = (x * r * g_ref[...]).astype(BF16)

    return pl.pallas_call(
        body, name=name, grid=(t // tm,),
        in_specs=[_spec((tm, D), lambda i: (i, 0)), _spec((None, 1, D), lambda i: (row, 0, 0))],
        out_specs=_spec((tm, D), lambda i: (i, 0)),
        out_shape=jax.ShapeDtypeStruct((t, D), BF16),
        compiler_params=_params(("parallel",)),
    )(h, gain_slab)


def _dhn_norm(name, pairs, dims, h, gain_slab, row, dres, rider=None):
    t = h.shape[0]
    tm = min(MM_TILE, t)
    nt = t // tm
    grid = (nt, NSH)
    npairs = len(pairs)
    nr = rider.n if rider is not None else 0
    operands, specs = [], []
    for a, a_spec, b, b_spec in pairs:
        operands += [a, b]
        specs += [a_spec, b_spec]
    row_spec = _spec((tm, D), lambda i, k: (i, 0))
    operands += [h, gain_slab, dres]
    specs += [row_spec, _spec((None, 1, D), lambda i, k: (row, 0, 0)), row_spec]
    n_own = len(operands)

    def body(*refs):
        h_ref, g_ref, dres_ref = refs[2 * npairs:n_own]
        dh_ref, dhb_ref, dg_ref = refs[n_own + nr:n_own + nr + 3]
        acc_ref, gacc_ref = refs[n_own + 2 * nr + 3:n_own + 2 * nr + 5]
        riders = (refs[n_own:n_own + nr], refs[n_own + nr + 3:n_own + 2 * nr + 3], refs[-2], refs[-1]) if nr else None
        if nr:
            rider.ride(grid, riders, "start")
        i, k = pl.program_id(0), pl.program_id(1)
        part = None
        for n in range(npairs):
            d = _dg(refs[2 * n][...], refs[2 * n + 1][...], dims)
            part = d if part is None else part + d

        @pl.when(k == 0)
        def _():
            acc_ref[...] = part

        @pl.when(k > 0)
        def _():
            acc_ref[...] += part

        @pl.when(k == NSH - 1)
        def _():
            x = h_ref[...]
            r = lax.rsqrt(jnp.mean(x * x, axis=-1, keepdims=True) + RMS_EPS)
            xh = x * r
            dy = acc_ref[...]
            gdy = dy * g_ref[...]
            dh = dres_ref[...] + (gdy - xh * jnp.mean(gdy * xh, axis=-1, keepdims=True)) * r
            dh_ref[...] = dh
            dhb_ref[...] = dh.astype(BF16)
            gpart = jnp.sum((dy * xh).reshape(tm // 8, 8, D), axis=0)

            @pl.when(i == 0)
            def _():
                gacc_ref[...] = gpart

            @pl.when(i > 0)
            def _():
                gacc_ref[...] += gpart

            @pl.when(i == nt - 1)
            def _():
                dg_ref[...] = jnp.sum(gacc_ref[...], axis=0, keepdims=True)

        if nr:
            rider.ride(grid, riders, "finish")

    outs = pl.pallas_call(
        body, name=name, grid=grid, in_specs=specs + (rider.in_specs if nr else []),
        out_specs=[row_spec, row_spec, _spec((1, D), lambda i, k: (0, 0))] + (rider.out_specs if nr else []),
        out_shape=[jax.ShapeDtypeStruct((t, D), F32), jax.ShapeDtypeStruct((t, D), BF16),
                   jax.ShapeDtypeStruct((1, D), F32)] + (rider.out_shape if nr else []),
        scratch_shapes=[pltpu.VMEM((tm, D), F32), pltpu.VMEM((8, D), F32)] + (rider.scratch if nr else []),
        compiler_params=pltpu.CompilerParams(dimension_semantics=("arbitrary", "arbitrary"),
                                             vmem_limit_bytes=VMEM_LIMIT_BIG),
    )(*operands, *(rider.groups if nr else []))
    return outs[0], outs[1], outs[2], (rider.place(outs[3:]) if nr else [])


def _final_loss(h, gain_slab, target):
    t = h.shape[0]
    tm = min(ROW_TILE, t)
    nt = t // tm

    def body(h_ref, g_ref, t_ref, dh_ref, dhb_ref, dg_ref, loss_ref, acc_ref, lacc_ref):
        i = pl.program_id(0)
        x = h_ref[...]
        g = g_ref[...]
        r = lax.rsqrt(jnp.mean(x * x, axis=-1, keepdims=True) + RMS_EPS)
        xh = x * r
        err = xh * g - t_ref[...]
        dy = err * (1.0 / D)
        gdy = dy * g
        dh = (gdy - xh * jnp.mean(gdy * xh, axis=-1, keepdims=True)) * r
        dh_ref[...] = dh
        dhb_ref[...] = dh.astype(BF16)
        part = jnp.sum((dy * xh).reshape(tm // 8, 8, D), axis=0)
        lpart = jnp.sum((err * err).reshape(tm // 8, 8, D), axis=0)

        @pl.when(i == 0)
        def _():
            acc_ref[...] = part
            lacc_ref[...] = lpart

        @pl.when(i > 0)
        def _():
            acc_ref[...] += part
            lacc_ref[...] += lpart

        @pl.when(i == nt - 1)
        def _():
            dg_ref[...] = jnp.sum(acc_ref[...], axis=0, keepdims=True)
            rows = jnp.sum(lacc_ref[...], axis=0, keepdims=True)
            loss_ref[...] = jnp.sum(rows, axis=1, keepdims=True) * (0.5 / D)

    row_spec = _spec((tm, D), lambda i: (i, 0))
    return pl.pallas_call(
        body, name="final_loss", grid=(nt,),
        in_specs=[row_spec, _spec((None, 1, D), lambda i: (R_FIN, 0, 0)), row_spec],
        out_specs=[row_spec, row_spec, _spec((1, D), lambda i: (0, 0)), _spec((1, 1), lambda i: (0, 0))],
        out_shape=[jax.ShapeDtypeStruct((t, D), F32), jax.ShapeDtypeStruct((t, D), BF16),
                   jax.ShapeDtypeStruct((1, D), F32), jax.ShapeDtypeStruct((1, 1), F32)],
        scratch_shapes=[pltpu.VMEM((8, D), F32), pltpu.VMEM((8, D), F32)],
        compiler_params=_params(("arbitrary",)),
    )(h, gain_slab, target)


def _ffn_up(name, hn, wgu, carry=None):
    t = hn.shape[0]
    tm = min(MM_TILE, t)
    grid = (NSH, t // tm)
    nc = carry.n if carry is not None else 0

    def body(*refs):
        x_ref, wg_ref, wu_ref = refs[:3]
        s_up_ref, s_gate_ref, a_ref = refs[3 + nc:6 + nc]
        riders = (refs[3:3 + nc], refs[6 + nc:6 + 2 * nc], refs[-2], refs[-1]) if nc else None
        if nc:
            carry.ride(grid, riders, "start")
        x = x_ref[...]
        g = _dg(x, wg_ref[...], NT)
        u = _dg(x, wu_ref[...], NT)
        sg = _sigmoid(g)
        silu = g * sg
        s_up_ref[...] = (MACARON * silu).astype(BF16)
        s_gate_ref[...] = (MACARON * u * (sg * (1.0 + g * (1.0 - sg)))).astype(BF16)
        a_ref[...] = (silu * u).astype(BF16)
        if nc:
            carry.ride(grid, riders, "finish")

    act = _spec((None, tm, FS), lambda s, i: (s, i, 0))
    shape = jax.ShapeDtypeStruct((NSH, t, FS), BF16)
    outs = pl.pallas_call(
        body, name=name, grid=grid,
        in_specs=[_spec((tm, D), lambda s, i: (i, 0)),
                  _spec((None, None, FS, D), lambda s, i: (s, 0, 0, 0)),
                  _spec((None, None, FS, D), lambda s, i: (s, 1, 0, 0))] + (carry.in_specs if nc else []),
        out_specs=[act, act, act] + (carry.out_specs if nc else []),
        out_shape=[shape, shape, shape] + (carry.out_shape if nc else []),
        scratch_shapes=carry.scratch if nc else [],
        compiler_params=_params(("arbitrary", "arbitrary") if nc else ("parallel", "parallel")),
    )(hn, wgu, wgu, *(carry.groups if nc else []))
    return (outs[0], outs[1], outs[2], carry.place(outs[3:]) if nc else [])


def _ffn_down(name, a, wd, h, carry=None):
    t = h.shape[0]
    tm = min(MM_TILE, t)
    return _mm(name, [(a, _spec((None, tm, FS), lambda i, k: (k, i, 0)),
                       wd, _spec((None, FS, D), lambda i, k: (k, 0, 0)))],
               grid=(t // tm, NSH), o_shape=(t, D), o_dtype=F32, o_spec=_spec((tm, D), lambda i, k: (i, 0)),
               dims=NN, kaxis=1, nk=NSH, acc_shape=(tm, D), res=(h, _spec((tm, D), lambda i, k: (i, 0))),
               scale=MACARON, carry=carry)


def _ffn_bwd_up(name, dhb, wd, s_up, s_gate, rider=None):
    t = dhb.shape[0]
    tm = min(MM_TILE, t)
    grid = (NSH, t // tm)
    nr = rider.n if rider is not None else 0

    def body(*refs):
        dh_ref, wd_ref, s_up_ref, s_gate_ref = refs[:4]
        dg_ref, du_ref = refs[4 + nr:6 + nr]
        riders = (refs[4:4 + nr], refs[6 + nr:6 + 2 * nr], refs[-2], refs[-1]) if nr else None
        if nr:
            rider.ride(grid, riders, "start")
        da = _dg(dh_ref[...], wd_ref[...], NT).astype(BF16)
        du_ref[...] = da * s_up_ref[...]
        dg_ref[...] = da * s_gate_ref[...]
        if nr:
            rider.ride(grid, riders, "finish")

    act = _spec((None, tm, FS), lambda s, i: (s, i, 0))
    shape = jax.ShapeDtypeStruct((NSH, t, FS), BF16)
    outs = pl.pallas_call(
        body, name=name, grid=grid,
        in_specs=[_spec((tm, D), lambda s, i: (i, 0)), _spec((None, FS, D), lambda s, i: (s, 0, 0)), act, act]
        + (rider.in_specs if nr else []),
        out_specs=[act, act] + (rider.out_specs if nr else []),
        out_shape=[shape, shape] + (rider.out_shape if nr else []),
        scratch_shapes=rider.scratch if nr else [],
        compiler_params=_params(("arbitrary", "arbitrary") if nr else ("parallel", "parallel")),
    )(dhb, wd, s_up, s_gate, *(rider.groups if nr else []))
    return outs[0], outs[1], (rider.place(outs[2:]) if nr else [])


def _wgrad(name, a, a_spec, b, b_spec, out_rows, out_cols, t, tt, group, slot, scale=None):
    first = isinstance(group, int)
    rows = group if first else group.shape[1]
    return _mm(name, [(a, a_spec, b, b_spec)], grid=(NSH, t // tt),
               o_shape=(NSH, rows, out_cols), o_dtype=BF16,
               o_spec=_spec((None, out_rows, out_cols), lambda s, k: (s, slot, 0)),
               dims=TN, kaxis=1, nk=t // tt, acc_shape=(out_rows, out_cols), scale=scale,
               into=None if first else group)


def _ffn_backward(tag, dh, dhb, h_in, hn, s_up, s_gate, a, wgu, wd, gate_idx, up_idx, down_idx, small, norm_row,
                  grad_a, grad_b, rider_up=None, rider_dhn=None):
    t = dh.shape[0]
    tm = min(MM_TILE, t)
    tt = min(WGRAD_TILE, t)
    dg, du, landed_up = _ffn_bwd_up(tag + "_bwd_up", dhb, wd, s_up, s_gate, rider_up)
    tok = _spec((tt, D), lambda s, k: (k, 0))
    hid = _spec((None, tt, FS), lambda s, k: (s, k, 0))
    grad_b = _wgrad(tag + "_dwd", a, hid, dhb, tok, FS, D, t, tt, grad_b, down_idx, scale=MACARON)
    grad_a = _wgrad(tag + "_dwg", dg, hid, hn, tok, FS, D, t, tt, grad_a, gate_idx)
    grad_a = _wgrad(tag + "_dwu", du, hid, hn, tok, FS, D, t, tt, grad_a, up_idx)
    rider = rider_dhn(grad_a, grad_b) if rider_dhn is not None else None
    act = _spec((None, tm, FS), lambda i, k: (k, i, 0))
    dh_in, dhb_in, d_gain, landed_dhn = _dhn_norm(
        tag + "_dhn", [(dg, act, wgu, _spec((None, None, FS, D), lambda i, k: (k, 0, 0, 0))),
                       (du, act, wgu, _spec((None, None, FS, D), lambda i, k: (k, 1, 0, 0)))],
        NN, h_in, small, norm_row, dh, rider)
    return dh_in, dhb_in, grad_a, grad_b, d_gain, landed_up, landed_dhn


def _conv_fwd(proj_a, conv_w):
    t = proj_a.shape[0]
    tm = min(ROW_TILE, t)
    hb = tm // CONV_HALO

    def body(ab_ref, ac_ref, ax_ref, acp_ref, axp_ref, w_ref, y_ref):
        i = pl.program_id(1)
        u = ac_ref[...] * ax_ref[...]
        up = jnp.where(i > 0, acp_ref[...] * axp_ref[...], 0.0)
        ext = jnp.concatenate([up, u], axis=0)
        u1 = pltpu.roll(ext, 1, 0)[CONV_HALO:]
        u2 = pltpu.roll(ext, 2, 0)[CONV_HALO:]
        w = w_ref[...]
        conv = w[0:1] * u2 + w[1:2] * u1 + w[2:3] * u
        y_ref[...] = (ab_ref[...] * conv).astype(BF16)

    def cur(off):
        return _spec((tm, LANE), lambda j, i: (i, off + j))

    def prev(off):
        return _spec((CONV_HALO, LANE), lambda j, i: (jnp.maximum(i * hb - 1, 0), off + j))

    return pl.pallas_call(
        body, name="conv_fwd", grid=(4, t // tm),
        in_specs=[cur(0), cur(4), cur(8), prev(4), prev(8),
                  _spec((None, None, 8, LANE), lambda j, i: (j, 0, 0, 0))],
        out_specs=_spec((tm, LANE), lambda j, i: (i, j)),
        out_shape=jax.ShapeDtypeStruct((t, 512), BF16),
        compiler_params=_params(("parallel", "parallel")),
    )(proj_a, proj_a, proj_a, proj_a, proj_a, conv_w)


def _conv_bwd(proj_a, conv_w, dy):
    t = proj_a.shape[0]
    tm = min(ROW_TILE, t)
    hb = tm // CONV_HALO
    nt = t // tm

    def body(ab_ref, ac_ref, ax_ref, dy_ref, acp_ref, axp_ref, abn_ref, dyn_ref, w_ref,
             dab_ref, dac_ref, dax_ref, dw_ref, acc_ref):
        i = pl.program_id(1)
        ab, ac, ax = ab_ref[...], ac_ref[...], ax_ref[...]
        u = ac * ax
        up = jnp.where(i > 0, acp_ref[...] * axp_ref[...], 0.0)
        ext = jnp.concatenate([up, u], axis=0)
        u1 = pltpu.roll(ext, 1, 0)[CONV_HALO:]
        u2 = pltpu.roll(ext, 2, 0)[CONV_HALO:]
        w = w_ref[...]
        conv = w[0:1] * u2 + w[1:2] * u1 + w[2:3] * u
        dy_v = dy_ref[...]
        dab_ref[...] = (dy_v * conv).astype(BF16)
        dc = dy_v * ab
        dcn = jnp.where(i < nt - 1, dyn_ref[...] * abn_ref[...], 0.0)
        extn = jnp.concatenate([dc, dcn], axis=0)
        n = tm + CONV_HALO
        dc1 = pltpu.roll(extn, n - 1, 0)[:tm]
        dc2 = pltpu.roll(extn, n - 2, 0)[:tm]
        du = w[2:3] * dc + w[1:2] * dc1 + w[0:1] * dc2
        dac_ref[...] = (du * ax).astype(BF16)
        dax_ref[...] = (du * ac).astype(BF16)
        rid = lax.broadcasted_iota(jnp.int32, (8, LANE), 0)
        part = jnp.where(rid == 0, jnp.sum(dc * u2, axis=0, keepdims=True),
                         jnp.where(rid == 1, jnp.sum(dc * u1, axis=0, keepdims=True),
                                   jnp.where(rid == 2, jnp.sum(dc * u, axis=0, keepdims=True), 0.0)))

        @pl.when(i == 0)
        def _():
            acc_ref[...] = part

        @pl.when(i > 0)
        def _():
            acc_ref[...] += part

        @pl.when(i == nt - 1)
        def _():
            dw_ref[...] = acc_ref[...]

    def cur(off):
        return _spec((tm, LANE), lambda j, i: (i, off + j))

    def prev(off):
        return _spec((CONV_HALO, LANE), lambda j, i: (jnp.maximum(i * hb - 1, 0), off + j))

    def nxt(off):
        return _spec((CONV_HALO, LANE), lambda j, i: (jnp.minimum((i + 1) * hb, nt * hb - 1), off + j))

    outs = pl.pallas_call(
        body, name="conv_bwd", grid=(4, nt),
        in_specs=[cur(0), cur(4), cur(8), cur(0), prev(4), prev(8), nxt(0), nxt(0),
                  _spec((None, None, 8, LANE), lambda j, i: (j, 0, 0, 0))],
        out_specs=[_spec((tm, LANE), lambda j, i: (i, j)), _spec((tm, LANE), lambda j, i: (i, j)),
                   _spec((tm, LANE), lambda j, i: (i, j)), _spec((None, 8, LANE), lambda j, i: (j, 0, 0))],
        out_shape=[jax.ShapeDtypeStruct((t, 512), BF16), jax.ShapeDtypeStruct((t, 512), BF16),
                   jax.ShapeDtypeStruct((t, 512), BF16), jax.ShapeDtypeStruct((4, 8, LANE), F32)],
        scratch_shapes=[pltpu.VMEM((8, LANE), F32)],
        compiler_params=_params(("parallel", "arbitrary")),
    )(proj_a, proj_a, proj_a, dy, proj_a, proj_a, proj_a, dy, conv_w)
    return outs


def _log_sigmoid(z):
    return jnp.minimum(z, 0.0) - jnp.log(1.0 + jnp.exp(-jnp.abs(z)))


def _sb_masks():
    row = lax.broadcasted_iota(jnp.int32, (SBQ, SBQ), 0)
    col = lax.broadcasted_iota(jnp.int32, (SBQ, SBQ), 1)
    return row, col


def _ones_where(mask):
    return jnp.where(mask, 1.0, 0.0).astype(BF16)


def _head_mask(head):
    lane = lax.broadcasted_iota(jnp.int32, (1, LANE), 1)
    return lane >= 64 if head else lane < 64


def _sb_fwd(proj_b, carry=None):
    t = proj_b.shape[0]
    nq = t // SBQ
    scale = 1.0 / math.sqrt(64.0)

    npair = SB_PAIRS
    wide = npair * LANE
    ngrp = 4 // npair
    chains = [(p, head) for p in range(npair) for head in range(2)]

    grid = (ngrp, nq)
    nc = carry.n if carry is not None else 0

    def body(*refs):
        q_ref, k_ref, v_ref = refs[:3]
        y_ref, l_ref, n_ref = refs[3 + nc:6 + nc]
        riders = (refs[3:3 + nc], refs[6 + nc:6 + 2 * nc], refs[-2], refs[-1]) if nc else None
        if nc:
            carry.ride(grid, riders, "start")
        grp = pl.program_id(0)
        qi = pl.program_id(1)
        row, col = _sb_masks()
        m_suffix = _ones_where(row > col)
        rows = len(chains) * SBQ
        strict = (lax.broadcasted_iota(jnp.int32, (rows, SBQ), 1)
                  < (lax.broadcasted_iota(jnp.int32, (rows, SBQ), 0) & (SBQ - 1)))
        q_pair = []
        for p in range(npair):
            q_all = q_ref[:, p * LANE:(p + 1) * LANE]
            q_pair.append(jnp.concatenate([jnp.where(_head_mask(head), q_all, jnp.zeros_like(q_all)) for head in range(2)],
                                          axis=0))

        def block(kb, state, diag):
            run, acc = state
            start = pl.multiple_of(kb * SBQ, SBQ)
            z = jnp.concatenate([_dg(q_pair[p], k_ref[pl.ds(start, SBQ), p * LANE:(p + 1) * LANE], NT)
                                 for p in range(npair)], axis=0) * scale
            lb = _log_sigmoid(z)
            lk = lb - z
            if diag:
                lk = jnp.where(strict, lk, 0.0)
            hi, lo = _split(lk)
            sums = _dg(jnp.concatenate([hi, lo], axis=0), m_suffix, NN)
            w = jnp.exp(lb + (run + sums[:rows] + sums[rows:]))
            if diag:
                w = jnp.where(strict, w, 0.0)
            wb = w.astype(BF16)
            acc = acc + jnp.concatenate(
                [_dg(wb[2 * p * SBQ:2 * (p + 1) * SBQ], v_ref[pl.ds(start, SBQ), p * LANE:(p + 1) * LANE], NN)
                 for p in range(npair)], axis=0)
            run = run + jnp.sum(hi.astype(F32) + lo.astype(F32), axis=1, keepdims=True)
            return run, acc

        state = block(qi, (jnp.zeros((rows, 1), F32), jnp.zeros((rows, LANE), F32)), True)

        def live(c):
            return jnp.logical_and(c[0] < qi, jnp.max(c[1][0]) > SB_DEAD)

        def step(c):
            return c[0] + 1, block(qi - 1 - c[0], c[1], False)

        count, (run, acc) = lax.while_loop(live, step, (jnp.int32(0), state))
        n_ref[grp * nq + qi] = count.astype(F32)
        hm = _head_mask(0)
        for p in range(npair):
            lo_rows, hi_rows = slice(2 * p * SBQ, (2 * p + 1) * SBQ), slice((2 * p + 1) * SBQ, (2 * p + 2) * SBQ)
            y_ref[:, p * LANE:(p + 1) * LANE] = jnp.where(hm, acc[lo_rows], acc[hi_rows]).astype(BF16)
            l_ref[p] = jnp.where(hm, run[lo_rows], run[hi_rows])
        if nc:
            carry.ride(grid, riders, "finish")

    outs = pl.pallas_call(
        body, name="sb_fwd", grid=grid,
        in_specs=[_spec((SBQ, wide), lambda g, i: (i, g)),
                  _spec((t, wide), lambda g, i: (0, ngrp + g)),
                  _spec((t, wide), lambda g, i: (0, 2 * ngrp + g))] + (carry.in_specs if nc else []),
        out_specs=[_spec((SBQ, wide), lambda g, i: (i, g)), _spec((npair, SBQ, LANE), lambda g, i: (g, i, 0)),
                   pl.BlockSpec(memory_space=pltpu.SMEM)] + (carry.out_specs if nc else []),
        out_shape=[jax.ShapeDtypeStruct((t, 512), BF16), jax.ShapeDtypeStruct((4, t, LANE), F32),
                   jax.ShapeDtypeStruct((ngrp * nq,), F32)] + (carry.out_shape if nc else []),
        scratch_shapes=carry.scratch if nc else [],
        compiler_params=_params(("arbitrary", "arbitrary")),
    )(proj_b, proj_b, proj_b, *(carry.groups if nc else []))
    return outs[0], outs[1], outs[2], (carry.place(outs[3:]) if nc else [])


def _sb_bwd(proj_b, dy, ltot, nblk, rider=None):
    t = proj_b.shape[0]
    nq = t // SBQ
    scale = 1.0 / math.sqrt(64.0)

    npair = SB_PAIRS
    wide = npair * LANE
    ngrp = 4 // npair
    chains = [(p, head) for p in range(npair) for head in range(2)]
    grid = (ngrp, nq)
    nr = rider.n if rider is not None else 0

    def body(*refs):
        q_ref, k_ref, v_ref, dy_ref, l_ref, n_ref = refs[:6]
        dq_ref, dk_ref, dv_ref = refs[6 + nr:9 + nr]
        dk_acc, dv_acc = refs[9 + 2 * nr:11 + 2 * nr]
        riders = (refs[6:6 + nr], refs[9 + nr:9 + 2 * nr], refs[-2], refs[-1]) if nr else None
        if nr:
            rider.ride(grid, riders, "start")
        grp = pl.program_id(0)
        qi = pl.program_id(1)

        @pl.when(qi == 0)
        def _():
            dk_acc[...] = jnp.zeros_like(dk_acc)
            dv_acc[...] = jnp.zeros_like(dv_acc)

        row, col = _sb_masks()
        m_prefix = _ones_where(row <= col)
        m_before = _ones_where(row < col)
        rows = len(chains) * SBQ
        strict = (lax.broadcasted_iota(jnp.int32, (rows, SBQ), 1)
                  < (lax.broadcasted_iota(jnp.int32, (rows, SBQ), 0) & (SBQ - 1)))
        q_pair, do_pair, ltot = [], [], []
        for p in range(npair):
            pl_ = slice(p * LANE, (p + 1) * LANE)
            q_all = q_ref[:, pl_]
            do_all = dy_ref[:, pl_].astype(BF16)
            q_pair.append(jnp.concatenate([jnp.where(_head_mask(h), q_all, jnp.zeros_like(q_all)) for h in range(2)], axis=0))
            do_pair.append(jnp.concatenate([jnp.where(_head_mask(h), do_all, jnp.zeros_like(do_all)) for h in range(2)], axis=0))
            ltot += [l_ref[p][:, h * 64:h * 64 + 1] for h in range(2)]
        ltot = jnp.concatenate(ltot, axis=0)

        def pair_rows(a, p):
            return a[2 * p * SBQ:2 * (p + 1) * SBQ]

        def block(kb, state, diag):
            seen, dseen, dq = state
            start = pl.multiple_of(kb * SBQ, SBQ)
            kk = [k_ref[pl.ds(start, SBQ), p * LANE:(p + 1) * LANE] for p in range(npair)]
            vv = [v_ref[pl.ds(start, SBQ), p * LANE:(p + 1) * LANE] for p in range(npair)]
            z = jnp.concatenate([_dg(q_pair[p], kk[p], NT) for p in range(npair)], axis=0) * scale
            lb = _log_sigmoid(z)
            lk = lb - z
            if diag:
                lk = jnp.where(strict, lk, 0.0)
            hi, lo = _split(lk)
            sums = _dg(jnp.concatenate([hi, lo], axis=0), m_prefix, NN)
            w = jnp.exp(lb + ((ltot - seen) - (sums[:rows] + sums[rows:])))
            if diag:
                w = jnp.where(strict, w, 0.0)
            wb = w.astype(BF16)
            da = w * jnp.concatenate([_dg(do_pair[p], vv[p], NT) for p in range(npair)], axis=0)
            dah, dal = _split(da)
            dsums = _dg(jnp.concatenate([dah, dal], axis=0), m_before, NN)
            sig = jnp.exp(lb)
            dz = (da * (1.0 - sig) - (dseen + dsums[:rows] + dsums[rows:]) * sig) * scale
            if diag:
                dz = jnp.where(strict, dz, 0.0)
            dzb = dz.astype(BF16)
            for p in range(npair):
                pl_ = slice(p * LANE, (p + 1) * LANE)
                dv_acc[pl.ds(start, SBQ), pl_] += _dg(pair_rows(wb, p), do_pair[p], TN)
                dk_acc[pl.ds(start, SBQ), pl_] += _dg(pair_rows(dzb, p), q_pair[p], TN)
            dq = dq + jnp.concatenate([_dg(pair_rows(dzb, p), kk[p], NN) for p in range(npair)], axis=0)
            seen = seen + jnp.sum(hi.astype(F32) + lo.astype(F32), axis=1, keepdims=True)
            dseen = dseen + jnp.sum(da, axis=1, keepdims=True)
            return seen, dseen, dq

        zero = (jnp.zeros((rows, 1), F32), jnp.zeros((rows, 1), F32), jnp.zeros((rows, LANE), F32))
        first = qi - n_ref[grp * nq + qi].astype(jnp.int32)
        state = lax.fori_loop(first, qi, lambda kb, c: block(kb, c, False), zero)
        _, _, dq = block(qi, state, True)
        for p in range(npair):
            dq_ref[:, p * LANE:(p + 1) * LANE] = jnp.where(
                _head_mask(0), dq[2 * p * SBQ:(2 * p + 1) * SBQ], dq[(2 * p + 1) * SBQ:(2 * p + 2) * SBQ]).astype(BF16)

        @pl.when(qi == nq - 1)
        def _():
            dk_ref[...] = dk_acc[...].astype(BF16)
            dv_ref[...] = dv_acc[...].astype(BF16)

        if nr:
            rider.ride(grid, riders, "finish")

    full = jax.ShapeDtypeStruct((t, 512), BF16)
    outs = pl.pallas_call(
        body, name="sb_bwd", grid=grid,
        in_specs=[_spec((SBQ, wide), lambda g, i: (i, g)),
                  _spec((t, wide), lambda g, i: (0, ngrp + g)),
                  _spec((t, wide), lambda g, i: (0, 2 * ngrp + g)),
                  _spec((SBQ, wide), lambda g, i: (i, ngrp + g)),
                  _spec((npair, SBQ, LANE), lambda g, i: (g, i, 0)),
                  pl.BlockSpec(memory_space=pltpu.SMEM)] + (rider.in_specs if nr else []),
        out_specs=[_spec((SBQ, wide), lambda g, i: (i, g)),
                   _spec((t, wide), lambda g, i: (0, g)), _spec((t, wide), lambda g, i: (0, g))]
        + (rider.out_specs if nr else []),
        out_shape=[full, full, full] + (rider.out_shape if nr else []),
        scratch_shapes=[pltpu.VMEM((t, wide), F32), pltpu.VMEM((t, wide), F32)] + (rider.scratch if nr else []),
        compiler_params=_params(("arbitrary", "arbitrary")),
    )(proj_b, proj_b, proj_b, dy, ltot, nblk, *(rider.groups if nr else []))
    return outs[0], outs[1], outs[2], (rider.place(outs[3:]) if nr else [])


def _hgrn_gates(qr, fr, c0, c1):
    mx = jnp.maximum(c0, c1)
    e0, e1 = jnp.exp(c0 - mx), jnp.exp(c1 - mx)
    lb = e1 / (e0 + e1)
    sx = _sigmoid(fr)
    f = lb + (1.0 - lb) * sx
    k = (1.0 - lb) * (1.0 - sx)
    sq = _sigmoid(qr)
    return lb, sx, f, k, sq, qr * sq


def _chunk_sums(mask, x):
    n = x.shape[1]
    hi, lo = _split(x)
    both = _dg(_ones_where(mask), jnp.concatenate([hi, lo], axis=1), NN)
    return both[:, :n] + both[:, n:]


def _chunk_masks():
    row = lax.broadcasted_iota(jnp.int32, (CHUNK, CHUNK), 0)
    col = lax.broadcasted_iota(jnp.int32, (CHUNK, CHUNK), 1)
    return col <= row, col >= row


def _hgrn_fwd(proj_c, small, carry=None):
    t = proj_c.shape[1]
    nc = t // CHUNK
    nh = D // HD

    hps = HGRN_HPS
    wide = hps * HD
    grid = (nh // hps, nc)
    nr = carry.n if carry is not None else 0

    def body(*refs):
        p_ref, c0_ref, c1_ref, gam_ref = refs[:4]
        o_ref, y_ref, sst_ref = refs[4 + nr:7 + nr]
        st_ref = refs[7 + 2 * nr]
        riders = (refs[4:4 + nr], refs[7 + nr:7 + 2 * nr], refs[-2], refs[-1]) if nr else None
        if nr:
            carry.ride(grid, riders, "start")
        c = pl.program_id(1)

        @pl.when(c == 0)
        def _():
            st_ref[...] = jnp.zeros_like(st_ref)

        _, _, f_all, k_all, _, q_all = _hgrn_gates(p_ref[0], p_ref[1], c0_ref[...], c1_ref[...])
        tril, _ = _chunk_masks()
        b_all = _chunk_sums(tril, jnp.log(f_all))
        for j in range(hps):
            ln = slice(j * HD, (j + 1) * HD)
            st0 = st_ref[j]
            sst_ref[j] = st0
            v, g = p_ref[2, :, ln], p_ref[3, :, ln]
            q, k, b = q_all[:, ln], k_all[:, ln], b_all[:, ln]
            bm = b[CHUNK // 2 - 1:CHUNK // 2]
            bl = b[CHUNK - 1:CHUNK]
            qd = q * jnp.exp(b)
            qt = q * jnp.exp(b - bm)
            kt = k * jnp.exp(bm - b)
            kl = k * jnp.exp(bl - b)
            vb = v.astype(BF16)
            att = jnp.where(tril, _dot3(qt, kt, NT), 0.0)
            o = _dg(qd.astype(BF16), st0.astype(BF16), NT) + _dg(att.astype(BF16), vb, NN)
            st_ref[j] = st0 * jnp.exp(bl) + _dg(vb, kl.astype(BF16), TN)
            o_ref[:, ln] = o
            r = lax.rsqrt(jnp.mean(o * o, axis=-1, keepdims=True) + RMS_EPS)
            y_ref[:, ln] = (o * r * gam_ref[...] * (g * _sigmoid(g))).astype(BF16)
        if nr:
            carry.ride(grid, riders, "finish")

    outs = pl.pallas_call(
        body, name="hgrn_fwd", grid=grid,
        in_specs=[_spec((4, CHUNK, wide), lambda h, c: (0, c, h)),
                  _spec((None, 1, wide), lambda h, c: (R_CLB, 0, h)),
                  _spec((None, 1, wide), lambda h, c: (R_CLB + 1, 0, h)),
                  _spec((None, 1, HD), lambda h, c: (R_GAM, 0, 0))] + (carry.in_specs if nr else []),
        out_specs=[_spec((CHUNK, wide), lambda h, c: (c, h)), _spec((CHUNK, wide), lambda h, c: (c, h)),
                   _spec((None, hps, HD, HD), lambda h, c: (c, h, 0, 0))] + (carry.out_specs if nr else []),
        out_shape=[jax.ShapeDtypeStruct((t, D), F32), jax.ShapeDtypeStruct((t, D), BF16),
                   jax.ShapeDtypeStruct((nc, nh, HD, HD), F32)] + (carry.out_shape if nr else []),
        scratch_shapes=[pltpu.VMEM((hps, HD, HD), F32)] + (carry.scratch if nr else []),
        compiler_params=_params(("arbitrary", "arbitrary")),
    )(proj_c, small, small, small, *(carry.groups if nr else []))
    return outs[0], outs[1], outs[2], (carry.place(outs[3:]) if nr else [])


def _hgrn_bwd(proj_c, small, o, sst, dyc, rider=None):
    t = proj_c.shape[1]
    nc = t // CHUNK
    nh = D // HD

    hps = HGRN_HPS
    wide = hps * HD
    ng = nh // hps
    grid = (ng, nc)
    nr = rider.n if rider is not None else 0

    def body(*refs):
        p_ref, c0_ref, c1_ref, gam_ref, o_ref, sst_ref, dy_ref = refs[:7]
        dp_ref, dclb_ref, dgam_ref = refs[7 + nr:10 + nr]
        dst_ref, dlb_acc, dgam_acc = refs[10 + 2 * nr:13 + 2 * nr]
        riders = (refs[7:7 + nr], refs[10 + nr:10 + 2 * nr], refs[-2], refs[-1]) if nr else None
        if nr:
            rider.ride(grid, riders, "start")
        group = pl.program_id(0)
        step = pl.program_id(1)

        @pl.when(step == 0)
        def _():
            dst_ref[...] = jnp.zeros_like(dst_ref)
            dlb_acc[...] = jnp.zeros_like(dlb_acc)

        @pl.when((step == 0) & (group == 0))
        def _():
            dgam_acc[...] = jnp.zeros_like(dgam_acc)

        gam = gam_ref[...]
        qr_all = p_ref[0]
        lb_all, sx_all, f_all, k_all, sq_all, q_all = _hgrn_gates(qr_all, p_ref[1], c0_ref[...], c1_ref[...])
        tril, triu = _chunk_masks()
        b_all = _chunk_sums(tril, jnp.log(f_all))
        dq_parts, dk_parts, db_parts, dgam_parts = [], [], [], []
        for j in range(hps):
            ln = slice(j * HD, (j + 1) * HD)
            st0 = sst_ref[j]
            dst1 = dst_ref[j]
            v, g = p_ref[2, :, ln], p_ref[3, :, ln]
            q, k, b = q_all[:, ln], k_all[:, ln], b_all[:, ln]
            bm = b[CHUNK // 2 - 1:CHUNK // 2]
            bl = b[CHUNK - 1:CHUNK]
            eb = jnp.exp(b)
            e_qt = jnp.exp(b - bm)
            e_kt = jnp.exp(bm - b)
            e_kl = jnp.exp(bl - b)
            e_bl = jnp.exp(bl)
            qd, qt, kt, kl = q * eb, q * e_qt, k * e_kt, k * e_kl
            ov = o_ref[:, ln]
            r = lax.rsqrt(jnp.mean(ov * ov, axis=-1, keepdims=True) + RMS_EPS)
            oh = ov * r
            sg = _sigmoid(g)
            dy = dy_ref[:, ln]
            dp_ref[3, :, ln] = (dy * oh * gam * (sg * (1.0 + g * (1.0 - sg)))).astype(BF16)
            dyv = dy * (g * sg)
            gdy = dyv * gam
            do = (gdy - oh * jnp.mean(gdy * oh, axis=-1, keepdims=True)) * r
            dob, vb = do.astype(BF16), v.astype(BF16)
            st0b, dst1b = st0.astype(BF16), dst1.astype(BF16)
            st1 = st0 * e_bl + _dg(vb, kl.astype(BF16), TN)
            att = jnp.where(tril, _dot3(qt, kt, NT), 0.0)
            datt = jnp.where(tril, _dg(dob, vb, NT), 0.0)
            dv = _dg(att.astype(BF16), dob, TN) + _dg(kl.astype(BF16), dst1b, NT)
            dq = _dot3(datt, kt, NN) * e_qt + _dg(dob, st0b, NN) * eb
            dk = _dot3(datt, qt, TN) * e_kt + _dg(vb, dst1b, NN) * e_kl
            db = q * dq - k * dk
            last = lax.broadcasted_iota(jnp.int32, (CHUNK, 1), 0) == CHUNK - 1
            db = db + jnp.where(last, jnp.sum(dst1 * st1, axis=0, keepdims=True), 0.0)
            dst_ref[j] = dst1 * e_bl + _dg(dob, qd.astype(BF16), TN)
            dp_ref[2, :, ln] = dv.astype(BF16)
            dq_parts.append(dq)
            dk_parts.append(dk)
            db_parts.append(db)
            dgam_parts.append(jnp.sum(dyv * oh, axis=0, keepdims=True))

        dq_all, dk_all = jnp.concatenate(dq_parts, axis=1), jnp.concatenate(dk_parts, axis=1)
        dlf = _chunk_sums(triu, jnp.concatenate(db_parts, axis=1))
        dp_ref[0] = (dq_all * (sq_all * (1.0 + qr_all * (1.0 - sq_all)))).astype(BF16)
        tmp = dlf / f_all - dk_all
        dp_ref[1] = (tmp * (1.0 - lb_all) * sx_all * (1.0 - sx_all)).astype(BF16)
        dlb_acc[...] += jnp.sum((1.0 - sx_all) * tmp, axis=0, keepdims=True)
        dgam_acc[...] += functools.reduce(lambda a, b: a + b, dgam_parts)

        @pl.when(step == nc - 1)
        def _():
            d1 = dlb_acc[...] * lb_all * (1.0 - lb_all)
            dclb_ref[...] = jnp.where(lax.broadcasted_iota(jnp.int32, (2, wide), 0) == 0, -d1, d1)

        @pl.when((step == nc - 1) & (group == ng - 1))
        def _():
            dgam_ref[...] = dgam_acc[...]

        if nr:
            rider.ride(grid, riders, "finish")

    rev = lambda h, s: (nc - 1 - s, h)
    outs = pl.pallas_call(
        body, name="hgrn_bwd", grid=grid,
        in_specs=[_spec((4, CHUNK, wide), lambda h, s: (0, nc - 1 - s, h)),
                  _spec((None, 1, wide), lambda h, s: (R_CLB, 0, h)),
                  _spec((None, 1, wide), lambda h, s: (R_CLB + 1, 0, h)),
                  _spec((None, 1, HD), lambda h, s: (R_GAM, 0, 0)),
                  _spec((CHUNK, wide), rev),
                  _spec((None, hps, HD, HD), lambda h, s: (nc - 1 - s, h, 0, 0)),
                  _spec((CHUNK, wide), rev)] + (rider.in_specs if nr else []),
        out_specs=[_spec((4, CHUNK, wide), lambda h, s: (0, nc - 1 - s, h)),
                   _spec((2, wide), lambda h, s: (0, h)),
                   _spec((1, HD), lambda h, s: (0, 0))] + (rider.out_specs if nr else []),
        out_shape=[jax.ShapeDtypeStruct((4, t, D), BF16), jax.ShapeDtypeStruct((2, D), F32),
                   jax.ShapeDtypeStruct((1, HD), F32)] + (rider.out_shape if nr else []),
        scratch_shapes=[pltpu.VMEM((hps, HD, HD), F32), pltpu.VMEM((1, wide), F32), pltpu.VMEM((1, HD), F32)]
        + (rider.scratch if nr else []),
        compiler_params=_params(("arbitrary", "arbitrary")),
    )(proj_c, small, small, small, o, sst, dyc, *(rider.groups if nr else []))
    return outs[0], outs[1], outs[2], (rider.place(outs[3:]) if nr else [])


def _adamw(name, w, g, m, v):
    rows, cols = w.shape
    br = rows
    for cand in (512, 352, 256):
        if rows % cand == 0:
            br = cand
            break
    c1 = 1.0 - ADAM_B1 ** ADAM_STEP
    c2 = 1.0 - ADAM_B2 ** ADAM_STEP

    def body(w_ref, g_ref, m_ref, v_ref, d_ref, mo_ref, vo_ref):
        gv = g_ref[...]
        mn = ADAM_B1 * m_ref[...] + (1.0 - ADAM_B1) * gv
        vn = ADAM_B2 * v_ref[...] + (1.0 - ADAM_B2) * (gv * gv)
        mo_ref[...] = mn
        vo_ref[...] = vn
        d_ref[...] = -ADAM_LR * ((mn / c1) / (jnp.sqrt(vn / c2) + ADAM_EPS) + ADAM_WD * w_ref[...])

    blk = _spec((br, cols), lambda i: (i, 0))
    shape = jax.ShapeDtypeStruct((rows, cols), F32)
    return pl.pallas_call(
        body, name=name, grid=(rows // br,), in_specs=[blk] * 4, out_specs=[blk] * 3, out_shape=[shape] * 3,
        compiler_params=_params(("parallel",)),
    )(w, g, m, v)


def _place():
    x, y, c = lax.axis_index("x"), lax.axis_index("y"), lax.axis_index("c")
    chips = [(1 - x, y), (x, 1 - y), (1 - x, 1 - y)]
    return x, y, c, chips


class _Rider:
    n = 0

    def ride(self, grid, refs, when):
        if not self.n:
            return
        ids = [pl.program_id(a) for a in range(len(grid))]
        edge = [i == (0 if when == "start" else g - 1) for i, g in zip(ids, grid)]
        cond = functools.reduce(jnp.logical_and, edge)

        @pl.when(cond)
        def _():
            (self.start if when == "start" else self.finish)(*refs)


class _Gather(_Rider):
    def __init__(self, groups):
        self.groups = list(groups)
        self.n = len(self.groups)
        self.in_specs = [ANY] * self.n
        self.out_specs = [ANY] * self.n
        self.out_shape = [jax.ShapeDtypeStruct((NSH,) + g.shape, g.dtype) for g in self.groups]
        self.scratch = [pltpu.SemaphoreType.DMA((6 * self.n,)), pltpu.SemaphoreType.DMA((6 * self.n,))] if self.n else []

    def _copies(self, ins, outs, send, recv):
        x, y, c, chips = _place()
        sibling = (x, y, 1 - c)

        def half(gi, chip, hc):
            return outs[gi].at[2 * chip[0] + chip[1], hc]

        def copy(gi, k, src, dst, to):
            return pltpu.make_async_remote_copy(src_ref=src, dst_ref=dst, send_sem=send.at[6 * gi + k],
                                                recv_sem=recv.at[6 * gi + k], device_id=to, device_id_type=MESH)

        pairs = [(gi, j, chip) for gi in range(self.n) for j, chip in enumerate(chips)]
        first = [copy(gi, j, ins[gi].at[c], half(gi, (x, y), c), (*chip, c)) for gi, j, chip in pairs]
        landed = [copy(gi, j, half(gi, chip, c), half(gi, chip, c), sibling) for gi, j, chip in pairs]
        relay = [copy(gi, 3 + j, half(gi, chip, c), half(gi, chip, c), sibling) for gi, j, chip in pairs]
        relayed = [copy(gi, 3 + j, half(gi, chip, 1 - c), half(gi, chip, 1 - c), sibling) for gi, j, chip in pairs]
        return first, landed, relay, relayed

    def start(self, ins, outs, send, recv):
        for cp in self._copies(ins, outs, send, recv)[0]:
            cp.start()

    def finish(self, ins, outs, send, recv):
        first, landed, relay, relayed = self._copies(ins, outs, send, recv)
        for arrived, onward in zip(landed, relay):
            arrived.wait_recv()
            onward.start()
        for cp in relayed:
            cp.wait_recv()
        for cp in first + relay:
            cp.wait_send()

    def place(self, outs):
        me = 2 * lax.axis_index("x") + lax.axis_index("y")
        return [lax.dynamic_update_index_in_dim(o, g, me, 0) for o, g in zip(outs, self.groups)]


def _gather_weights(groups):
    gather = _Gather(groups)
    ng = gather.n

    def body(*refs):
        parts = (refs[:ng], refs[ng:2 * ng], refs[2 * ng], refs[2 * ng + 1])
        gather.start(*parts)
        gather.finish(*parts)

    outs = pl.pallas_call(
        body, name="gather_weights", in_specs=gather.in_specs, out_specs=gather.out_specs, out_shape=gather.out_shape,
        scratch_shapes=gather.scratch, compiler_params=pltpu.CompilerParams(has_side_effects=True),
    )(*groups)
    return gather.place(outs)


def _swap_halves(name, slots):
    n = len(slots)

    def body(*refs):
        ins, outs = refs[:n], refs[n:2 * n]
        send, recv = refs[2 * n:]
        x, y, c, _ = _place()
        cps = []
        for i, (_, row0, rows) in enumerate(slots):
            half = rows // 2
            src = ins[i].at[:, pl.ds(pl.multiple_of(row0 + (1 - c) * half, 16), half)]
            cps.append(pltpu.make_async_remote_copy(src_ref=src, dst_ref=outs[i], send_sem=send.at[i],
                                                    recv_sem=recv.at[i], device_id=(x, y, 1 - c), device_id_type=MESH))
        for cp in cps:
            cp.start()
        for cp in cps:
            cp.wait()

    return pl.pallas_call(
        body, name=name, in_specs=[ANY] * n, out_specs=[ANY] * n,
        out_shape=[jax.ShapeDtypeStruct((NSH, rows // 2, buf.shape[2]), buf.dtype) for buf, _, rows in slots],
        scratch_shapes=[pltpu.SemaphoreType.DMA((n,)), pltpu.SemaphoreType.DMA((n,))],
        compiler_params=pltpu.CompilerParams(has_side_effects=True),
    )(*[buf for buf, _, _ in slots])


class _Send(_Rider):
    def __init__(self, arrays):
        self.groups = list(arrays)
        self.n = len(self.groups)
        self.in_specs = [ANY] * self.n
        self.out_specs = [ANY] * self.n
        self.out_shape = [jax.ShapeDtypeStruct((3,) + g.shape[1:], g.dtype) for g in self.groups]
        self.scratch = [pltpu.SemaphoreType.DMA((3 * self.n,)), pltpu.SemaphoreType.DMA((3 * self.n,))]

    def _copies(self, ins, outs, send, recv):
        x, y, c, chips = _place()
        return [pltpu.make_async_remote_copy(src_ref=ins[gi].at[2 * chip[0] + chip[1]], dst_ref=outs[gi].at[j],
                                             send_sem=send.at[3 * gi + j], recv_sem=recv.at[3 * gi + j],
                                             device_id=(*chip, c), device_id_type=MESH)
                for gi in range(self.n) for j, chip in enumerate(chips)]

    def start(self, ins, outs, send, recv):
        for cp in self._copies(ins, outs, send, recv):
            cp.start()

    def finish(self, ins, outs, send, recv):
        for cp in self._copies(ins, outs, send, recv):
            cp.wait()

    def place(self, outs):
        return list(outs)


def _share_halves(groups):
    ng = len(groups)

    def body(*refs):
        ins, outs = refs[:ng], refs[ng:2 * ng]
        send, recv = refs[2 * ng:]
        x, y, c, _ = _place()
        cps = [pltpu.make_async_remote_copy(src_ref=ins[gi], dst_ref=outs[gi].at[c], send_sem=send.at[gi],
                                            recv_sem=recv.at[gi], device_id=(x, y, 1 - c), device_id_type=MESH)
               for gi in range(ng)]
        for cp in cps:
            cp.start()
        for gi in range(ng):
            pltpu.make_async_remote_copy(src_ref=ins[gi], dst_ref=outs[gi].at[1 - c], send_sem=send.at[gi],
                                         recv_sem=recv.at[gi], device_id=(x, y, 1 - c), device_id_type=MESH).wait_recv()
        for cp in cps:
            cp.wait_send()

    outs = pl.pallas_call(
        body, name="grad_share_halves", in_specs=[ANY] * ng, out_specs=[ANY] * ng,
        out_shape=[jax.ShapeDtypeStruct((2,) + g.shape, g.dtype) for g in groups],
        scratch_shapes=[pltpu.SemaphoreType.DMA((ng,)), pltpu.SemaphoreType.DMA((ng,))],
        compiler_params=pltpu.CompilerParams(has_side_effects=True),
    )(*groups)
    c = lax.axis_index("c")
    return [lax.dynamic_update_index_in_dim(o, g, c, 0) for o, g in zip(outs, groups)]


def _pair_sum(name, slots, got, c_idx):
    n = len(slots)
    in_specs, out_specs, out_shape, operands = [], [], [], []
    for (buf, row0, rows), g in zip(slots, got):
        hb, cols = rows // 4, buf.shape[2]
        in_specs += [pl.BlockSpec((None, hb, cols), lambda q, i, cr, r=row0 // hb: (q, r + 2 * cr[0] + i, 0)),
                     pl.BlockSpec((None, hb, cols), lambda q, i, cr: (q, i, 0))]
        out_specs.append(pl.BlockSpec((None, hb, cols), lambda q, i, cr: (q, i, 0)))
        out_shape.append(jax.ShapeDtypeStruct(g.shape, BF16))
        operands += [buf, g]

    def body(c_ref, *refs):
        for i in range(n):
            refs[2 * n + i][...] = (refs[2 * i][...].astype(F32) + refs[2 * i + 1][...].astype(F32)).astype(BF16)

    return pl.pallas_call(
        body, name=name,
        grid_spec=pltpu.PrefetchScalarGridSpec(num_scalar_prefetch=1, grid=(NSH, 2), in_specs=in_specs, out_specs=out_specs),
        out_shape=out_shape, compiler_params=_params(("parallel", "parallel")),
    )(c_idx, *operands)


def _owner_sum(name, pairs, got, p_idx):
    n = len(pairs)
    in_specs, out_specs, out_shape, operands = [], [], [], []
    for own, g in zip(pairs, got):
        _, rows, cols = own.shape
        hb = rows // 2
        in_specs += [pl.BlockSpec((None, hb, cols), lambda i, pr: (pr[0], i, 0)),
                     pl.BlockSpec((3, hb, cols), lambda i, pr: (0, i, 0))]
        out_specs.append(pl.BlockSpec((hb, cols), lambda i, pr: (i, 0)))
        out_shape.append(jax.ShapeDtypeStruct((rows, cols), F32))
        operands += [own, g]

    def body(p_ref, *refs):
        for i in range(n):
            a_ref, b_ref = refs[2 * i], refs[2 * i + 1]
            refs[2 * n + i][...] = ((a_ref[...].astype(F32) + b_ref[0].astype(F32)) + b_ref[1].astype(F32)) + b_ref[2].astype(F32)

    return pl.pallas_call(
        body, name=name,
        grid_spec=pltpu.PrefetchScalarGridSpec(num_scalar_prefetch=1, grid=(2,), in_specs=in_specs, out_specs=out_specs),
        out_shape=out_shape, compiler_params=_params(("parallel",)),
    )(p_idx, *operands)


def _sum_small(slab):
    def body(in_ref, out_ref, all_ref, send, recv):
        x, y, c, _ = _place()
        me = 4 * x + 2 * y + c
        all_ref[me] = in_ref[...]
        cps = []
        for k in range(1, 8):
            peer = (x ^ (k >> 2), y ^ ((k >> 1) & 1), c ^ (k & 1))
            cps.append(pltpu.make_async_remote_copy(src_ref=in_ref, dst_ref=all_ref.at[me], send_sem=send.at[k - 1],
                                                    recv_sem=recv.at[k - 1], device_id=peer, device_id_type=MESH))
        for cp in cps:
            cp.start()
        for cp in cps:
            cp.wait()
        total = all_ref[0]
        for d in range(1, 8):
            total = total + all_ref[d]
        out_ref[...] = total

    return pl.pallas_call(
        body, name="sum_small",
        in_specs=[pl.BlockSpec(memory_space=pltpu.VMEM)], out_specs=pl.BlockSpec(memory_space=pltpu.VMEM),
        out_shape=jax.ShapeDtypeStruct(slab.shape, F32),
        scratch_shapes=[pltpu.VMEM((8,) + slab.shape, F32), pltpu.SemaphoreType.DMA((7,)), pltpu.SemaphoreType.DMA((7,))],
        compiler_params=pltpu.CompilerParams(has_side_effects=True),
    )(slab)


FFNS = ("pre0", "post0", "pre1", "post1")
W_SHAPES = dict({f + "_gu": (NSH, 2, FS, D) for f in FFNS}, **{f + "_d": (NSH, FS, D) for f in FFNS},
                ab_in=(NSH, D, 768), ab_out=(NSH, 256, D), conv=(NSH, 2, 8, LANE), c_in=(NSH, D, D), c_out=(NSH, 256, D))
CARRIED = dict(pre0_up=("pre0_d",), pre0_down=("ab_in", "ab_out", "conv"),
               sb_fwd=("post0_gu", "post0_d"), post0_up=("pre1_gu",), post0_down=("pre1_d",),
               pre1_up=("c_in", "c_out"), hgrn_fwd=("post1_gu", "post1_d"))


FFN_SLOTS = dict(pre0=(0, 2, 0), pre1=(1, 3, 1), post0=(4, 6, 2), post1=(5, 7, 3))
GRAD_SLOTS = dict(ab_out=("b", B_ABOUT, 256), c_in=("b", B_CIN, D), c_out=("b", B_COUT, 256), ab_in=("c", 0, D))
for _f, (_g, _u, _d) in FFN_SLOTS.items():
    GRAD_SLOTS.update({_f + "_g": ("a", _g * FS, FS), _f + "_u": ("a", _u * FS, FS), _f + "_d": ("b", _d * FS, FS)})
REDUCE_STAGES = dict(x=("post1_g", "post1_u", "post1_d", "c_out"),
                     y=("c_in", "pre1_g", "pre1_u", "pre1_d", "post0_g", "post0_u", "post0_d", "ab_out"),
                     z1=("ab_in",), z2=("pre0_g", "pre0_u", "pre0_d"))


def _local_step(x, target, weights, small, shards=None, place=None):
    t = x.shape[0]
    tm = min(MM_TILE, t)
    tw = min(WGRAD_TILE, t)
    nt = t // tm
    tok_si = _spec((tm, D), lambda s, i: (i, 0))
    w = dict(weights)
    reduced = {}

    def reduce_start(stage, buffers):
        if place is None:
            return (), [], None
        names = REDUCE_STAGES[stage]
        slots = [(buffers[GRAD_SLOTS[n][0]],) + GRAD_SLOTS[n][1:] for n in names]
        got = _swap_halves("grad_swap_" + stage, slots)
        pairs = _pair_sum("grad_pair_sum_" + stage, slots, got, place[0])
        return names, pairs, _Send(pairs)

    def reduce_end(stage, names, pairs, landed):
        if place is not None:
            reduced.update(zip(names, _owner_sum("grad_owner_sum_" + stage, pairs, landed, place[1])))

    def carried(kernel_name):
        names = [n for n in CARRIED[kernel_name] if n not in w]
        return names, (_Gather([shards[n] for n in names]) if names else None)

    def land(names, arrays):
        for n, a in zip(names, arrays):
            w[n] = a.reshape(W_SHAPES[n])

    def ffn_forward(tag, h, norm_row):
        hn = _norm_fwd(tag + "_norm", h, small, norm_row)
        names, gather = carried(tag + "_up") if tag + "_up" in CARRIED else ([], None)
        s_up, s_gate, a, got = _ffn_up(tag + "_up", hn, w[tag + "_gu"], gather)
        land(names, got)
        names, gather = carried(tag + "_down") if tag + "_down" in CARRIED else ([], None)
        out = _ffn_down(tag + "_down", a, w[tag + "_d"], h, gather)
        if gather is not None:
            out, got = out
            land(names, got)
        return out, (h, hn, s_up, s_gate, a)

    def out_proj(name, y, w_out, h):
        return _mm(name, [(y, _spec((tm, 256), lambda i, k: (i, k)), w_out, _spec((None, 256, D), lambda i, k: (k, 0, 0)))],
                   grid=(nt, NSH), o_shape=(t, D), o_dtype=F32, o_spec=_spec((tm, D), lambda i, k: (i, 0)),
                   dims=NN, kaxis=1, nk=NSH, acc_shape=(tm, D), res=(h, _spec((tm, D), lambda i, k: (i, 0))))

    def out_proj_bwd(tag, dhb, y, w_out, blk, grad_b):
        dy = _mm(tag + "_dy", [(dhb, tok_si, w_out, _spec((None, 256, D), lambda s, i: (s, 0, 0)))],
                 grid=(NSH, nt), o_shape=(t, D), o_dtype=F32, o_spec=_spec((tm, 256), lambda s, i: (i, s)),
                 dims=NT, kaxis=1, nk=1)
        grad_b = _wgrad(tag + "_dwout", y, _spec((tw, 256), lambda s, k: (k, s)), dhb, _spec((tw, D), lambda s, k: (k, 0)),
                        256, D, t, tw, grad_b, blk)
        return dy, grad_b

    h0 = x
    h1, pre0 = ffn_forward("pre0", h0, R_PRE)
    hn_ab = _norm_fwd("mix0_norm", h1, small, R_MIX)
    proj_a = _mm("ab_proj_a", [(hn_ab, tok_si, w["ab_in"], _spec((None, D, 768), lambda s, i: (s, 0, 0)))],
                 grid=(2, nt), o_shape=(t, 1536), o_dtype=F32, o_spec=_spec((tm, 768), lambda s, i: (i, s)),
                 dims=NN, kaxis=1, nk=1)
    proj_b = _mm("ab_proj_b", [(hn_ab, tok_si, w["ab_in"], _spec((None, D, 768), lambda s, i: (s + 2, 0, 0)))],
                 grid=(2, nt), o_shape=(t, 1536), o_dtype=BF16, o_spec=_spec((tm, 768), lambda s, i: (i, s)),
                 dims=NN, kaxis=1, nk=1)
    y_a = _conv_fwd(proj_a, w["conv"])
    names, gather = carried("sb_fwd")
    y_b, ltot, nblk, got = _sb_fwd(proj_b, gather)
    land(names, got)
    y_ab = jnp.concatenate([y_a, y_b], axis=1)
    h2 = out_proj("ab_out", y_ab, w["ab_out"], h1)
    h3, post0 = ffn_forward("post0", h2, R_POST)
    h4, pre1 = ffn_forward("pre1", h3, R_PRE + 1)
    hn_c = _norm_fwd("mix1_norm", h4, small, R_MIX + 1)
    proj_c = _mm("c_proj", [(hn_c, tok_si, w["c_in"], _spec((None, D, D), lambda s, i: (s, 0, 0)))],
                 grid=(NSH, nt), o_shape=(NSH, t, D), o_dtype=F32, o_spec=_spec((None, tm, D), lambda s, i: (s, i, 0)),
                 dims=NN, kaxis=1, nk=1)
    names, gather = carried("hgrn_fwd")
    o_c, y_c, sst, got = _hgrn_fwd(proj_c, small, gather)
    land(names, got)
    h5 = out_proj("c_out", y_c, w["c_out"], h4)
    h6, post1 = ffn_forward("post1", h5, R_POST + 1)
    dh, dhb, d_fin, loss = _final_loss(h6, small, target)

    dh, dhb, grad_a, grad_b, dn_post1, _, _ = _ffn_backward("post1", dh, dhb, *post1, w["post1_gu"], w["post1_d"],
                                                            *FFN_SLOTS["post1"], small, R_POST + 1, 8 * FS, B_ROWS)
    dy_c, grad_b = out_proj_bwd("c", dhb, y_c, w["c_out"], B_COUT // 256, grad_b)
    names, pairs, rider = reduce_start("x", dict(a=grad_a, b=grad_b))
    dproj_c, d_clb, d_gam, landed = _hgrn_bwd(proj_c, small, o_c, sst, dy_c, rider)
    reduce_end("x", names, pairs, landed)
    grad_b = _wgrad("c_dwin", hn_c, _spec((tw, D), lambda s, k: (k, 0)), dproj_c, _spec((None, tw, D), lambda s, k: (s, k, 0)),
                    D, D, t, tw, grad_b, B_CIN // D)
    dh, dhb, dn_mix1, _ = _dhn_norm("c_dhn", [(dproj_c, _spec((None, tm, D), lambda i, k: (k, i, 0)),
                                               w["c_in"], _spec((None, D, D), lambda i, k: (k, 0, 0)))],
                                    NT, h4, small, R_MIX + 1, dh)
    dh, dhb, grad_a, grad_b, dn_pre1, _, _ = _ffn_backward("pre1", dh, dhb, *pre1, w["pre1_gu"], w["pre1_d"],
                                                           *FFN_SLOTS["pre1"], small, R_PRE + 1, grad_a, grad_b)
    dh, dhb, grad_a, grad_b, dn_post0, _, _ = _ffn_backward("post0", dh, dhb, *post0, w["post0_gu"], w["post0_d"],
                                                            *FFN_SLOTS["post0"], small, R_POST, grad_a, grad_b)
    dy_ab, grad_b = out_proj_bwd("ab", dhb, y_ab, w["ab_out"], B_ABOUT // 256, grad_b)
    dab, dac, dax, d_conv = _conv_bwd(proj_a, w["conv"], dy_ab)
    names, pairs, rider = reduce_start("y", dict(a=grad_a, b=grad_b))
    dq, dk, dv, landed = _sb_bwd(proj_b, dy_ab, ltot, nblk, rider)
    reduce_end("y", names, pairs, landed)
    dproj_ab = jnp.concatenate([dab, dac, dax, dq, dk, dv], axis=1)
    grad_c = _wgrad("ab_dwin", hn_ab, _spec((tw, D), lambda s, k: (k, 0)), dproj_ab, _spec((tw, 768), lambda s, k: (k, s)),
                    D, 768, t, tw, D, 0)
    dh, dhb, dn_mix0, _ = _dhn_norm("ab_dhn", [(dproj_ab, _spec((tm, 768), lambda i, k: (i, k)),
                                                w["ab_in"], _spec((None, D, 768), lambda i, k: (k, 0, 0)))],
                                    NT, h1, small, R_MIX, dh)
    names, pairs, rider = reduce_start("z1", dict(c=grad_c))
    last = {}

    def own_gradients(buf_a, buf_b):
        last["z2"] = reduce_start("z2", dict(a=buf_a, b=buf_b))
        return last["z2"][2]

    dh, dhb, grad_a, grad_b, dn_pre0, landed, landed_own = _ffn_backward(
        "pre0", dh, dhb, *pre0, w["pre0_gu"], w["pre0_d"], *FFN_SLOTS["pre0"], small, R_PRE, grad_a, grad_b,
        rider, own_gradients if place is not None else None)
    reduce_end("z1", names, pairs, landed)
    if place is not None:
        reduce_end("z2", last["z2"][0], last["z2"][1], landed_own)
    zero = jnp.zeros((1, D), F32)
    conv_rows = jnp.pad(jnp.transpose(d_conv[:, :3, :], (1, 0, 2)).reshape(3, 512), ((0, 0), (0, D - 512)))
    small_grad = jnp.concatenate([
        dn_pre0, dn_pre1, dn_mix0, dn_mix1, dn_post0, dn_post1, d_clb, d_fin,
        jnp.pad(d_gam, ((0, 0), (0, D - HD))), conv_rows,
        jnp.pad(loss, ((0, 0), (0, D - 1))), zero, zero], axis=0)
    return dh, grad_a, grad_b, grad_c, small_grad, reduced


def _small_slab(rows):
    parts = [jnp.pad(r.astype(F32), ((0, 0), (0, D - r.shape[1]))) for r in rows]
    slab = jnp.concatenate(parts, axis=0)
    return jnp.pad(slab, ((0, SMALL_ROWS - slab.shape[0]), (0, 0)))


def kernel(x, ffn_pre_norm, ffn_pre_w_gate, ffn_pre_w_up, ffn_pre_w_down, mix_norm, ffn_post_norm, ffn_post_w_gate, ffn_post_w_up, ffn_post_w_down, ab_w_in, ab_conv_w, ab_w_out, c_w_in, c_lower_bounds, c_out_norm, c_w_out, final_norm, loss_target, m_ffn_pre_norm, m_ffn_pre_w_gate, m_ffn_pre_w_up, m_ffn_pre_w_down, m_mix_norm, m_ffn_post_norm, m_ffn_post_w_gate, m_ffn_post_w_up, m_ffn_post_w_down, m_ab_w_in, m_ab_conv_w, m_ab_w_out, m_c_w_in, m_c_lower_bounds, m_c_out_norm, m_c_w_out, m_final_norm, v_ffn_pre_norm, v_ffn_pre_w_gate, v_ffn_pre_w_up, v_ffn_pre_w_down, v_mix_norm, v_ffn_post_norm, v_ffn_post_w_gate, v_ffn_post_w_up, v_ffn_post_w_down, v_ab_w_in, v_ab_conv_w, v_ab_w_out, v_c_w_in, v_c_lower_bounds, v_c_out_norm, v_c_w_out, v_final_norm):
    t = x.shape[1]
    xi, yi, ci = lax.axis_index("x"), lax.axis_index("y"), lax.axis_index("c")
    p_idx = (2 * xi + yi).astype(jnp.int32).reshape(1)
    c_idx = ci.astype(jnp.int32).reshape(1)

    def halves(m):
        return m.astype(BF16).reshape(2, m.shape[0] // 2, m.shape[1])

    transposed = ("ffn_pre_w_gate", "ffn_pre_w_up", "ffn_post_w_gate", "ffn_post_w_up")

    def flip(a):
        return jnp.swapaxes(a, 1, 2)

    shards = {}
    for name, (w_gate, w_up, w_down, layer) in dict(
            pre0=(ffn_pre_w_gate, ffn_pre_w_up, ffn_pre_w_down, 0), post0=(ffn_post_w_gate, ffn_post_w_up, ffn_post_w_down, 0),
            pre1=(ffn_pre_w_gate, ffn_pre_w_up, ffn_pre_w_down, 1), post1=(ffn_post_w_gate, ffn_post_w_up, ffn_post_w_down, 1)).items():
        shards[name + "_gu"] = jnp.stack([flip(w_gate)[layer], flip(w_up)[layer]]).astype(BF16)
        shards[name + "_d"] = halves(w_down[layer])
    conv_pad = jnp.pad(ab_conv_w[0], ((0, 5), (0, 0)))
    shards.update(ab_in=halves(ab_w_in[0]), ab_out=halves(ab_w_out[0]), c_in=halves(c_w_in[0]), c_out=halves(c_w_out[0]),
                  conv=jnp.stack([conv_pad, jnp.zeros_like(conv_pad)]))
    first = _gather_weights([shards["pre0_gu"]])[0].reshape(W_SHAPES["pre0_gu"])

    small = _small_slab([ffn_pre_norm, mix_norm, ffn_post_norm, c_lower_bounds, final_norm.reshape(1, D), c_out_norm])
    small = small.reshape(SMALL_ROWS, 1, D)

    grad_x, _, _, _, small_grad, reduced = _local_step(x[0], loss_target[0], dict(pre0_gu=first), small, shards,
                                                        (c_idx, p_idx))
    order = list(reduced)
    whole = {n: g.reshape(2 * g.shape[1], g.shape[2]) for n, g in zip(order, _share_halves([reduced[n] for n in order]))}
    small_sum = _sum_small(small_grad)

    my_conv = lax.dynamic_slice(small_sum[R_CONV:R_CONV + 3], (0, (2 * xi + yi) * 128), (3, 128))
    grads = {
        "ffn_pre_norm": small_sum[R_PRE:R_PRE + 2], "mix_norm": small_sum[R_MIX:R_MIX + 2],
        "ffn_post_norm": small_sum[R_POST:R_POST + 2], "c_lower_bounds": small_sum[R_CLB:R_CLB + 2],
        "c_out_norm": small_sum[R_GAM:R_GAM + 1, :HD], "final_norm": small_sum[R_FIN],
        "ab_conv_w": my_conv.reshape(1, 3, 128),
        "ab_w_in": whole["ab_in"][None], "ab_w_out": whole["ab_out"][None],
        "c_w_in": whole["c_in"][None], "c_w_out": whole["c_out"][None],
    }
    for kind, key in (("gate", "_g"), ("up", "_u"), ("down", "_d")):
        grads["ffn_pre_w_" + kind] = jnp.stack([whole["pre0" + key], whole["pre1" + key]])
        grads["ffn_post_w_" + kind] = jnp.stack([whole["post0" + key], whole["post1" + key]])
    weights = dict(ffn_pre_norm=ffn_pre_norm, ffn_pre_w_gate=ffn_pre_w_gate, ffn_pre_w_up=ffn_pre_w_up, ffn_pre_w_down=ffn_pre_w_down, mix_norm=mix_norm, ffn_post_norm=ffn_post_norm, ffn_post_w_gate=ffn_post_w_gate, ffn_post_w_up=ffn_post_w_up, ffn_post_w_down=ffn_post_w_down, ab_w_in=ab_w_in, ab_conv_w=ab_conv_w, ab_w_out=ab_w_out, c_w_in=c_w_in, c_lower_bounds=c_lower_bounds, c_out_norm=c_out_norm, c_w_out=c_w_out, final_norm=final_norm)
    m_in = dict(ffn_pre_norm=m_ffn_pre_norm, ffn_pre_w_gate=m_ffn_pre_w_gate, ffn_pre_w_up=m_ffn_pre_w_up, ffn_pre_w_down=m_ffn_pre_w_down, mix_norm=m_mix_norm, ffn_post_norm=m_ffn_post_norm, ffn_post_w_gate=m_ffn_post_w_gate, ffn_post_w_up=m_ffn_post_w_up, ffn_post_w_down=m_ffn_post_w_down, ab_w_in=m_ab_w_in, ab_conv_w=m_ab_conv_w, ab_w_out=m_ab_w_out, c_w_in=m_c_w_in, c_lower_bounds=m_c_lower_bounds, c_out_norm=m_c_out_norm, c_w_out=m_c_w_out, final_norm=m_final_norm)
    v_in = dict(ffn_pre_norm=v_ffn_pre_norm, ffn_pre_w_gate=v_ffn_pre_w_gate, ffn_pre_w_up=v_ffn_pre_w_up, ffn_pre_w_down=v_ffn_pre_w_down, mix_norm=v_mix_norm, ffn_post_norm=v_ffn_post_norm, ffn_post_w_gate=v_ffn_post_w_gate, ffn_post_w_up=v_ffn_post_w_up, ffn_post_w_down=v_ffn_post_w_down, ab_w_in=v_ab_w_in, ab_conv_w=v_ab_conv_w, ab_w_out=v_ab_w_out, c_w_in=v_c_w_in, c_lower_bounds=v_c_lower_bounds, c_out_norm=v_c_out_norm, c_w_out=v_c_w_out, final_norm=v_final_norm)
    names = list(weights)
    big = [n for n in names if weights[n].size >= 65536]
    tiny = [n for n in names if n not in big]

    delta, new_m, new_v = {}, {}, {}
    for n in big:
        turn = flip if n in transposed else (lambda a: a)
        shape = turn(weights[n]).shape
        two_d = (shape[0] * shape[1], shape[2])
        d, m2, v2 = _adamw("adamw_" + n, turn(weights[n]).reshape(two_d), grads[n].reshape(two_d),
                           turn(m_in[n]).reshape(two_d), turn(v_in[n]).reshape(two_d))
        delta[n], new_m[n], new_v[n] = turn(d.reshape(shape)), turn(m2.reshape(shape)), turn(v2.reshape(shape))
        grads[n] = turn(grads[n])

    def tiny_slab(src):
        return _small_slab([src[n].reshape(-1, src[n].shape[-1]) for n in tiny])

    offs, row = {}, 0
    for n in tiny:
        nrows = weights[n].size // weights[n].shape[-1]
        offs[n] = (row, nrows)
        row += nrows
    d, m2, v2 = _adamw("adamw_small", tiny_slab(weights), tiny_slab(grads), tiny_slab(m_in), tiny_slab(v_in))
    for n in tiny:
        r0, nr = offs[n]
        shape = weights[n].shape
        for dst, src in ((delta, d), (new_m, m2), (new_v, v2)):
            dst[n] = src[r0:r0 + nr, :shape[-1]].reshape(shape)

    loss = small_sum[R_LOSS, 0]
    return (loss, grad_x.reshape(1, t, D), *[grads[n] for n in names], *[delta[n] for n in names],
            *[new_m[n] for n in names], *[new_v[n] for n in names])
```

```python
import functools
import math

import jax
import jax.numpy as jnp
from jax import lax
from jax.experimental import pallas as pl
from jax.experimental.pallas import tpu as pltpu

F32 = jnp.float32
BF16 = jnp.bfloat16
MESH = pl.DeviceIdType.MESH
ANY = pl.BlockSpec(memory_space=pl.ANY)

D = 1024
FS = 704
NSH = 4
RMS_EPS = 1e-6
MACARON = 0.5
CHUNK = 64
HD = 128
HGRN_HPS = 8
SBQ = 128
SB_PAIRS_FWD = 4
SB_PAIRS_BWD = 2
SB_DEAD = -105.0
CONV_HALO = 8
LANE = 128
ROW_TILE = 512
MM_TILE = 1024
WGRAD_TILE = 4096
VMEM_LIMIT = 48 * 1024 * 1024
VMEM_LIMIT_BIG = 58 * 1024 * 1024

ADAM_LR, ADAM_B1, ADAM_B2, ADAM_EPS, ADAM_WD, ADAM_STEP = 0.001, 0.9, 0.999, 1e-08, 0.01, 10

NN = ((1,), (0,))
NT = ((1,), (1,))
TN = ((0,), (0,))

SMALL_ROWS = 16
R_PRE, R_MIX, R_POST, R_CLB, R_FIN, R_GAM, R_CONV, R_LOSS = 0, 2, 4, 6, 8, 9, 10, 13

B_DOWN = 0
B_ABOUT = 4 * FS
B_CIN = B_ABOUT + 256
B_COUT = B_CIN + 1024
B_ROWS = B_COUT + 256


def _dg(a, b, dims):
    return lax.dot_general(a, b, (dims, ((), ())), preferred_element_type=F32)


def _split(x):
    hi = x.astype(BF16)
    lo = (x - hi.astype(F32)).astype(BF16)
    return hi, lo


def _dot3(a, b, dims):
    ah, al = _split(a)
    bh, bl = _split(b)
    if dims == TN:
        n = b.shape[1]
        both = _dg(ah, jnp.concatenate([bh, bl], axis=1), dims)
        return both[:, :n] + both[:, n:] + _dg(al, bh, dims)
    m = a.shape[0]
    both = _dg(jnp.concatenate([ah, al], axis=0), bh, dims)
    return both[:m] + both[m:] + _dg(ah, bl, dims)


def _sigmoid(x):
    return 1.0 / (1.0 + jnp.exp(-x))


def _params(sem):
    return pltpu.CompilerParams(dimension_semantics=sem, vmem_limit_bytes=VMEM_LIMIT)


def _spec(shape, imap):
    return pl.BlockSpec(shape, imap)


def _mm(name, pairs, *, grid, o_shape, o_dtype, o_spec, dims, kaxis, nk, acc_shape=None, res=None, scale=None,
        into=None, carry=None):
    npairs = len(pairs)
    operands, specs = [], []
    for a, a_spec, b, b_spec in pairs:
        operands += [a, b]
        specs += [a_spec, b_spec]
    if res is not None:
        operands.append(res[0])
        specs.append(res[1])
    aliases = {}
    if into is not None:
        aliases = {len(operands): 0}
        operands.append(into)
        specs.append(ANY)
    n_own = len(operands)
    nc = carry.n if carry is not None else 0
    if nc:
        operands += carry.groups
        specs += carry.in_specs

    def body(*refs):
        o_ref = refs[n_own + nc]
        riders = (refs[n_own:n_own + nc], refs[n_own + nc + 1:n_own + 2 * nc + 1], refs[-2], refs[-1]) if nc else None
        if nc:
            carry.ride(grid, riders, "start")
        part = None
        for n in range(npairs):
            d = _dg(refs[2 * n][...], refs[2 * n + 1][...], dims)
            part = d if part is None else part + d

        def finish(val):
            if scale is not None:
                val = val * scale
            if res is not None:
                val = val + refs[2 * npairs][...]
            o_ref[...] = val.astype(o_dtype)

        if nk == 1:
            finish(part)
        else:
            acc_ref = refs[n_own + 2 * nc + 1]
            k = pl.program_id(kaxis)

            @pl.when(k == 0)
            def _():
                acc_ref[...] = part

            @pl.when(k > 0)
            def _():
                acc_ref[...] += part

            @pl.when(k == nk - 1)
            def _():
                finish(acc_ref[...])
        if nc:
            carry.ride(grid, riders, "finish")

    sem = tuple("arbitrary" if (nc or (ax == kaxis and nk > 1)) else "parallel" for ax in range(len(grid)))
    outs = pl.pallas_call(
        body, name=name, grid=grid, in_specs=specs, out_specs=[o_spec] + (carry.out_specs if nc else []),
        out_shape=[jax.ShapeDtypeStruct(o_shape, o_dtype)] + (carry.out_shape if nc else []),
        scratch_shapes=([pltpu.VMEM(acc_shape, F32)] if nk > 1 else []) + (carry.scratch if nc else []),
        input_output_aliases=aliases,
        compiler_params=_params(sem),
    )(*operands)
    return (outs[0], carry.place(outs[1:])) if nc else outs[0]


def _norm_fwd(name, h, gain_slab, row):
    t = h.shape[0]
    tm = min(ROW_TILE, t)

    def body(h_ref, g_ref, o_ref):
        x = h_ref[...]
        r = lax.rsqrt(jnp.mean(x * x, axis=-1, keepdims=True) + RMS_EPS)
        o_ref[...] = (x * r * g_ref[...]).astype(BF16)

    return pl.pallas_call(
        body, name=name, grid=(t // tm,),
        in_specs=[_spec((tm, D), lambda i: (i, 0)), _spec((None, 1, D), lambda i: (row, 0, 0))],
        out_specs=_spec((tm, D), lambda i: (i, 0)),
        out_shape=jax.ShapeDtypeStruct((t, D), BF16),
        compiler_params=_params(("parallel",)),
    )(h, gain_slab)


def _dhn_norm(name, pairs, dims, h, gain_slab, row, dres, rider=None):
    t = h.shape[0]
    tm = min(MM_TILE, t)
    nt = t // tm
    grid = (nt, NSH)
    npairs = len(pairs)
    nr = rider.n if rider is not None else 0
    operands, specs = [], []
    for a, a_spec, b, b_spec in pairs:
        operands += [a, b]
        specs += [a_spec, b_spec]
    row_spec = _spec((tm, D), lambda i, k: (i, 0))
    operands += [h, gain_slab, dres]
    specs += [row_spec, _spec((None, 1, D), lambda i, k: (row, 0, 0)), row_spec]
    n_own = len(operands)

    def body(*refs):
        h_ref, g_ref, dres_ref = refs[2 * npairs:n_own]
        dh_ref, dhb_ref, dg_ref = refs[n_own + nr:n_own + nr + 3]
        acc_ref, gacc_ref = refs[n_own + 2 * nr + 3:n_own + 2 * nr + 5]
        riders = (refs[n_own:n_own + nr], refs[n_own + nr + 3:n_own + 2 * nr + 3], refs[-2], refs[-1]) if nr else None
        if nr:
            rider.ride(grid, riders, "start")
        i, k = pl.program_id(0), pl.program_id(1)
        part = None
        for n in range(npairs):
            d = _dg(refs[2 * n][...], refs[2 * n + 1][...], dims)
            part = d if part is None else part + d

        @pl.when(k == 0)
        def _():
            acc_ref[...] = part

        @pl.when(k > 0)
        def _():
            acc_ref[...] += part

        @pl.when(k == NSH - 1)
        def _():
            x = h_ref[...]
            r = lax.rsqrt(jnp.mean(x * x, axis=-1, keepdims=True) + RMS_EPS)
            xh = x * r
            dy = acc_ref[...]
            gdy = dy * g_ref[...]
            dh = dres_ref[...] + (gdy - xh * jnp.mean(gdy * xh, axis=-1, keepdims=True)) * r
            dh_ref[...] = dh
            dhb_ref[...] = dh.astype(BF16)
            gpart = jnp.sum((dy * xh).reshape(tm // 8, 8, D), axis=0)

            @pl.when(i == 0)
            def _():
                gacc_ref[...] = gpart

            @pl.when(i > 0)
            def _():
                gacc_ref[...] += gpart

            @pl.when(i == nt - 1)
            def _():
                dg_ref[...] = jnp.sum(gacc_ref[...], axis=0, keepdims=True)

        if nr:
            rider.ride(grid, riders, "finish")

    outs = pl.pallas_call(
        body, name=name, grid=grid, in_specs=specs + (rider.in_specs if nr else []),
        out_specs=[row_spec, row_spec, _spec((1, D), lambda i, k: (0, 0))] + (rider.out_specs if nr else []),
        out_shape=[jax.ShapeDtypeStruct((t, D), F32), jax.ShapeDtypeStruct((t, D), BF16),
                   jax.ShapeDtypeStruct((1, D), F32)] + (rider.out_shape if nr else []),
        scratch_shapes=[pltpu.VMEM((tm, D), F32), pltpu.VMEM((8, D), F32)] + (rider.scratch if nr else []),
        compiler_params=pltpu.CompilerParams(dimension_semantics=("arbitrary", "arbitrary"),
                                             vmem_limit_bytes=VMEM_LIMIT_BIG),
    )(*operands, *(rider.groups if nr else []))
    return outs[0], outs[1], outs[2], (rider.place(outs[3:]) if nr else [])


def _final_loss(h, gain_slab, target):
    t = h.shape[0]
    tm = min(ROW_TILE, t)
    nt = t // tm

    def body(h_ref, g_ref, t_ref, dh_ref, dhb_ref, dg_ref, loss_ref, acc_ref, lacc_ref):
        i = pl.program_id(0)
        x = h_ref[...]
        g = g_ref[...]
        r = lax.rsqrt(jnp.mean(x * x, axis=-1, keepdims=True) + RMS_EPS)
        xh = x * r
        err = xh * g - t_ref[...]
        dy = err * (1.0 / D)
        gdy = dy * g
        dh = (gdy - xh * jnp.mean(gdy * xh, axis=-1, keepdims=True)) * r
        dh_ref[...] = dh
        dhb_ref[...] = dh.astype(BF16)
        part = jnp.sum((dy * xh).reshape(tm // 8, 8, D), axis=0)
        lpart = jnp.sum((err * err).reshape(tm // 8, 8, D), axis=0)

        @pl.when(i == 0)
        def _():
            acc_ref[...] = part
            lacc_ref[...] = lpart

        @pl.when(i > 0)
        def _():
            acc_ref[...] += part
            lacc_ref[...] += lpart

        @pl.when(i == nt - 1)
        def _():
            dg_ref[...] = jnp.sum(acc_ref[...], axis=0, keepdims=True)
            rows = jnp.sum(lacc_ref[...], axis=0, keepdims=True)
            loss_ref[...] = jnp.sum(rows, axis=1, keepdims=True) * (0.5 / D)

    row_spec = _spec((tm, D), lambda i: (i, 0))
    return pl.pallas_call(
        body, name="final_loss", grid=(nt,),
        in_specs=[row_spec, _spec((None, 1, D), lambda i: (R_FIN, 0, 0)), row_spec],
        out_specs=[row_spec, row_spec, _spec((1, D), lambda i: (0, 0)), _spec((1, 1), lambda i: (0, 0))],
        out_shape=[jax.ShapeDtypeStruct((t, D), F32), jax.ShapeDtypeStruct((t, D), BF16),
                   jax.ShapeDtypeStruct((1, D), F32), jax.ShapeDtypeStruct((1, 1), F32)],
        scratch_shapes=[pltpu.VMEM((8, D), F32), pltpu.VMEM((8, D), F32)],
        compiler_params=_params(("arbitrary",)),
    )(h, gain_slab, target)


def _ffn_up(name, hn, wgu, carry=None):
    t = hn.shape[0]
    tm = min(MM_TILE, t)
    grid = (NSH, t // tm)
    nc = carry.n if carry is not None else 0

    def body(*refs):
        x_ref, wg_ref, wu_ref = refs[:3]
        s_up_ref, s_gate_ref, a_ref = refs[3 + nc:6 + nc]
        riders = (refs[3:3 + nc], refs[6 + nc:6 + 2 * nc], refs[-2], refs[-1]) if nc else None
        if nc:
            carry.ride(grid, riders, "start")
        x = x_ref[...]
        g = _dg(x, wg_ref[...], NT)
        u = _dg(x, wu_ref[...], NT)
        sg = _sigmoid(g)
        silu = g * sg
        s_up_ref[...] = (MACARON * silu).astype(BF16)
        s_gate_ref[...] = (MACARON * u * (sg * (1.0 + g * (1.0 - sg)))).astype(BF16)
        a_ref[...] = (silu * u).astype(BF16)
        if nc:
            carry.ride(grid, riders, "finish")

    act = _spec((None, tm, FS), lambda s, i: (s, i, 0))
    shape = jax.ShapeDtypeStruct((NSH, t, FS), BF16)
    outs = pl.pallas_call(
        body, name=name, grid=grid,
        in_specs=[_spec((tm, D), lambda s, i: (i, 0)),
                  _spec((None, None, FS, D), lambda s, i: (s, 0, 0, 0)),
                  _spec((None, None, FS, D), lambda s, i: (s, 1, 0, 0))] + (carry.in_specs if nc else []),
        out_specs=[act, act, act] + (carry.out_specs if nc else []),
        out_shape=[shape, shape, shape] + (carry.out_shape if nc else []),
        scratch_shapes=carry.scratch if nc else [],
        compiler_params=_params(("arbitrary", "arbitrary") if nc else ("parallel", "parallel")),
    )(hn, wgu, wgu, *(carry.groups if nc else []))
    return (outs[0], outs[1], outs[2], carry.place(outs[3:]) if nc else [])


def _ffn_down(name, a, wd, h, carry=None):
    t = h.shape[0]
    tm = min(MM_TILE, t)
    return _mm(name, [(a, _spec((None, tm, FS), lambda i, k: (k, i, 0)),
                       wd, _spec((None, FS, D), lambda i, k: (k, 0, 0)))],
               grid=(t // tm, NSH), o_shape=(t, D), o_dtype=F32, o_spec=_spec((tm, D), lambda i, k: (i, 0)),
               dims=NN, kaxis=1, nk=NSH, acc_shape=(tm, D), res=(h, _spec((tm, D), lambda i, k: (i, 0))),
               scale=MACARON, carry=carry)


def _ffn_bwd_up(name, dhb, wd, s_up, s_gate, rider=None):
    t = dhb.shape[0]
    tm = min(MM_TILE, t)
    grid = (NSH, t // tm)
    nr = rider.n if rider is not None else 0

    def body(*refs):
        dh_ref, wd_ref, s_up_ref, s_gate_ref = refs[:4]
        dg_ref, du_ref = refs[4 + nr:6 + nr]
        riders = (refs[4:4 + nr], refs[6 + nr:6 + 2 * nr], refs[-2], refs[-1]) if nr else None
        if nr:
            rider.ride(grid, riders, "start")
        da = _dg(dh_ref[...], wd_ref[...], NT).astype(BF16)
        du_ref[...] = da * s_up_ref[...]
        dg_ref[...] = da * s_gate_ref[...]
        if nr:
            rider.ride(grid, riders, "finish")

    act = _spec((None, tm, FS), lambda s, i: (s, i, 0))
    shape = jax.ShapeDtypeStruct((NSH, t, FS), BF16)
    outs = pl.pallas_call(
        body, name=name, grid=grid,
        in_specs=[_spec((tm, D), lambda s, i: (i, 0)), _spec((None, FS, D), lambda s, i: (s, 0, 0)), act, act]
        + (rider.in_specs if nr else []),
        out_specs=[act, act] + (rider.out_specs if nr else []),
        out_shape=[shape, shape] + (rider.out_shape if nr else []),
        scratch_shapes=rider.scratch if nr else [],
        compiler_params=_params(("arbitrary", "arbitrary") if nr else ("parallel", "parallel")),
    )(dhb, wd, s_up, s_gate, *(rider.groups if nr else []))
    return outs[0], outs[1], (rider.place(outs[2:]) if nr else [])


def _wgrad(name, a, a_spec, b, b_spec, out_rows, out_cols, t, tt, group, slot, scale=None):
    first = isinstance(group, int)
    rows = group if first else group.shape[1]
    return _mm(name, [(a, a_spec, b, b_spec)], grid=(NSH, t // tt),
               o_shape=(NSH, rows, out_cols), o_dtype=BF16,
               o_spec=_spec((None, out_rows, out_cols), lambda s, k: (s, slot, 0)),
               dims=TN, kaxis=1, nk=t // tt, acc_shape=(out_rows, out_cols), scale=scale,
               into=None if first else group)


def _ffn_backward(tag, dh, dhb, h_in, hn, s_up, s_gate, a, wgu, wd, gate_idx, up_idx, down_idx, small, norm_row,
                  grad_a, grad_b, rider_up=None, rider_dhn=None):
    t = dh.shape[0]
    tm = min(MM_TILE, t)
    tt = min(WGRAD_TILE, t)
    dg, du, landed_up = _ffn_bwd_up(tag + "_bwd_up", dhb, wd, s_up, s_gate, rider_up)
    tok = _spec((tt, D), lambda s, k: (k, 0))
    hid = _spec((None, tt, FS), lambda s, k: (s, k, 0))
    grad_b = _wgrad(tag + "_dwd", a, hid, dhb, tok, FS, D, t, tt, grad_b, down_idx, scale=MACARON)
    grad_a = _wgrad(tag + "_dwg", dg, hid, hn, tok, FS, D, t, tt, grad_a, gate_idx)
    grad_a = _wgrad(tag + "_dwu", du, hid, hn, tok, FS, D, t, tt, grad_a, up_idx)
    rider = rider_dhn(grad_a, grad_b) if rider_dhn is not None else None
    act = _spec((None, tm, FS), lambda i, k: (k, i, 0))
    dh_in, dhb_in, d_gain, landed_dhn = _dhn_norm(
        tag + "_dhn", [(dg, act, wgu, _spec((None, None, FS, D), lambda i, k: (k, 0, 0, 0))),
                       (du, act, wgu, _spec((None, None, FS, D), lambda i, k: (k, 1, 0, 0)))],
        NN, h_in, small, norm_row, dh, rider)
    return dh_in, dhb_in, grad_a, grad_b, d_gain, landed_up, landed_dhn


def _conv_fwd(proj_a, conv_w):
    t = proj_a.shape[0]
    tm = min(ROW_TILE, t)
    hb = tm // CONV_HALO

    def body(ab_ref, ac_ref, ax_ref, acp_ref, axp_ref, w_ref, y_ref):
        i = pl.program_id(1)
        u = ac_ref[...] * ax_ref[...]
        up = jnp.where(i > 0, acp_ref[...] * axp_ref[...], 0.0)
        ext = jnp.concatenate([up, u], axis=0)
        u1 = pltpu.roll(ext, 1, 0)[CONV_HALO:]
        u2 = pltpu.roll(ext, 2, 0)[CONV_HALO:]
        w = w_ref[...]
        conv = w[0:1] * u2 + w[1:2] * u1 + w[2:3] * u
        y_ref[...] = (ab_ref[...] * conv).astype(BF16)

    def cur(off):
        return _spec((tm, LANE), lambda j, i: (i, off + j))

    def prev(off):
        return _spec((CONV_HALO, LANE), lambda j, i: (jnp.maximum(i * hb - 1, 0), off + j))

    return pl.pallas_call(
        body, name="conv_fwd", grid=(4, t // tm),
        in_specs=[cur(0), cur(4), cur(8), prev(4), prev(8),
                  _spec((None, None, 8, LANE), lambda j, i: (j, 0, 0, 0))],
        out_specs=_spec((tm, LANE), lambda j, i: (i, j)),
        out_shape=jax.ShapeDtypeStruct((t, 512), BF16),
        compiler_params=_params(("parallel", "parallel")),
    )(proj_a, proj_a, proj_a, proj_a, proj_a, conv_w)


def _conv_bwd(proj_a, conv_w, dy):
    t = proj_a.shape[0]
    tm = min(ROW_TILE, t)
    hb = tm // CONV_HALO
    nt = t // tm

    def body(ab_ref, ac_ref, ax_ref, dy_ref, acp_ref, axp_ref, abn_ref, dyn_ref, w_ref,
             dab_ref, dac_ref, dax_ref, dw_ref, acc_ref):
        i = pl.program_id(1)
        ab, ac, ax = ab_ref[...], ac_ref[...], ax_ref[...]
        u = ac * ax
        up = jnp.where(i > 0, acp_ref[...] * axp_ref[...], 0.0)
        ext = jnp.concatenate([up, u], axis=0)
        u1 = pltpu.roll(ext, 1, 0)[CONV_HALO:]
        u2 = pltpu.roll(ext, 2, 0)[CONV_HALO:]
        w = w_ref[...]
        conv = w[0:1] * u2 + w[1:2] * u1 + w[2:3] * u
        dy_v = dy_ref[...]
        dab_ref[...] = (dy_v * conv).astype(BF16)
        dc = dy_v * ab
        dcn = jnp.where(i < nt - 1, dyn_ref[...] * abn_ref[...], 0.0)
        extn = jnp.concatenate([dc, dcn], axis=0)
        n = tm + CONV_HALO
        dc1 = pltpu.roll(extn, n - 1, 0)[:tm]
        dc2 = pltpu.roll(extn, n - 2, 0)[:tm]
        du = w[2:3] * dc + w[1:2] * dc1 + w[0:1] * dc2
        dac_ref[...] = (du * ax).astype(BF16)
        dax_ref[...] = (du * ac).astype(BF16)
        rid = lax.broadcasted_iota(jnp.int32, (8, LANE), 0)
        part = jnp.where(rid == 0, jnp.sum(dc * u2, axis=0, keepdims=True),
                         jnp.where(rid == 1, jnp.sum(dc * u1, axis=0, keepdims=True),
                                   jnp.where(rid == 2, jnp.sum(dc * u, axis=0, keepdims=True), 0.0)))

        @pl.when(i == 0)
        def _():
            acc_ref[...] = part

        @pl.when(i > 0)
        def _():
            acc_ref[...] += part

        @pl.when(i == nt - 1)
        def _():
            dw_ref[...] = acc_ref[...]

    def cur(off):
        return _spec((tm, LANE), lambda j, i: (i, off + j))

    def prev(off):
        return _spec((CONV_HALO, LANE), lambda j, i: (jnp.maximum(i * hb - 1, 0), off + j))

    def nxt(off):
        return _spec((CONV_HALO, LANE), lambda j, i: (jnp.minimum((i + 1) * hb, nt * hb - 1), off + j))

    outs = pl.pallas_call(
        body, name="conv_bwd", grid=(4, nt),
        in_specs=[cur(0), cur(4), cur(8), cur(0), prev(4), prev(8), nxt(0), nxt(0),
                  _spec((None, None, 8, LANE), lambda j, i: (j, 0, 0, 0))],
        out_specs=[_spec((tm, LANE), lambda j, i: (i, j)), _spec((tm, LANE), lambda j, i: (i, j)),
                   _spec((tm, LANE), lambda j, i: (i, j)), _spec((None, 8, LANE), lambda j, i: (j, 0, 0))],
        out_shape=[jax.ShapeDtypeStruct((t, 512), BF16), jax.ShapeDtypeStruct((t, 512), BF16),
                   jax.ShapeDtypeStruct((t, 512), BF16), jax.ShapeDtypeStruct((4, 8, LANE), F32)],
        scratch_shapes=[pltpu.VMEM((8, LANE), F32)],
        compiler_params=_params(("parallel", "arbitrary")),
    )(proj_a, proj_a, proj_a, dy, proj_a, proj_a, proj_a, dy, conv_w)
    return outs


def _log_sigmoid(z):
    return jnp.minimum(z, 0.0) - jnp.log(1.0 + jnp.exp(-jnp.abs(z)))


def _sb_masks():
    row = lax.broadcasted_iota(jnp.int32, (SBQ, SBQ), 0)
    col = lax.broadcasted_iota(jnp.int32, (SBQ, SBQ), 1)
    return row, col


def _ones_where(mask):
    return jnp.where(mask, 1.0, 0.0).astype(BF16)


def _head_mask(head):
    lane = lax.broadcasted_iota(jnp.int32, (1, LANE), 1)
    return lane >= 64 if head else lane < 64


def _sb_fwd(proj_b, carry=None):
    t = proj_b.shape[0]
    nq = t // SBQ
    scale = 1.0 / math.sqrt(64.0)

    npair = SB_PAIRS_FWD
    wide = npair * LANE
    ngrp = 4 // npair
    chains = [(p, head) for p in range(npair) for head in range(2)]

    grid = (ngrp, nq)
    nc = carry.n if carry is not None else 0

    def body(*refs):
        q_ref, k_ref, v_ref = refs[:3]
        y_ref, l_ref, n_ref = refs[3 + nc:6 + nc]
        riders = (refs[3:3 + nc], refs[6 + nc:6 + 2 * nc], refs[-2], refs[-1]) if nc else None
        if nc:
            carry.ride(grid, riders, "start")
        grp = pl.program_id(0)
        qi = pl.program_id(1)
        row, col = _sb_masks()
        m_suffix = _ones_where(row > col)
        rows = len(chains) * SBQ
        strict = (lax.broadcasted_iota(jnp.int32, (rows, SBQ), 1)
                  < (lax.broadcasted_iota(jnp.int32, (rows, SBQ), 0) & (SBQ - 1)))
        q_pair = []
        for p in range(npair):
            q_all = q_ref[:, p * LANE:(p + 1) * LANE]
            q_pair.append(jnp.concatenate([jnp.where(_head_mask(head), q_all, jnp.zeros_like(q_all)) for head in range(2)],
                                          axis=0))

        def block(kb, state, diag):
            run, acc = state
            start = pl.multiple_of(kb * SBQ, SBQ)
            z = jnp.concatenate([_dg(q_pair[p], k_ref[pl.ds(start, SBQ), p * LANE:(p + 1) * LANE], NT)
                                 for p in range(npair)], axis=0) * scale
            lb = _log_sigmoid(z)
            lk = lb - z
            if diag:
                lk = jnp.where(strict, lk, 0.0)
            hi, lo = _split(lk)
            sums = _dg(jnp.concatenate([hi, lo], axis=0), m_suffix, NN)
            w = jnp.exp(lb + (run + sums[:rows] + sums[rows:]))
            if diag:
                w = jnp.where(strict, w, 0.0)
            wb = w.astype(BF16)
            acc = acc + jnp.concatenate(
                [_dg(wb[2 * p * SBQ:2 * (p + 1) * SBQ], v_ref[pl.ds(start, SBQ), p * LANE:(p + 1) * LANE], NN)
                 for p in range(npair)], axis=0)
            run = run + jnp.sum(hi.astype(F32) + lo.astype(F32), axis=1, keepdims=True)
            return run, acc

        state = block(qi, (jnp.zeros((rows, 1), F32), jnp.zeros((rows, LANE), F32)), True)

        def live(c):
            return jnp.logical_and(c[0] < qi, jnp.max(c[1][0]) > SB_DEAD)

        def step(c):
            return c[0] + 1, block(qi - 1 - c[0], c[1], False)

        count, (run, acc) = lax.while_loop(live, step, (jnp.int32(0), state))
        n_ref[grp * nq + qi] = count.astype(F32)
        hm = _head_mask(0)
        for p in range(npair):
            lo_rows, hi_rows = slice(2 * p * SBQ, (2 * p + 1) * SBQ), slice((2 * p + 1) * SBQ, (2 * p + 2) * SBQ)
            y_ref[:, p * LANE:(p + 1) * LANE] = jnp.where(hm, acc[lo_rows], acc[hi_rows]).astype(BF16)
            l_ref[p] = jnp.where(hm, run[lo_rows], run[hi_rows])
        if nc:
            carry.ride(grid, riders, "finish")

    outs = pl.pallas_call(
        body, name="sb_fwd", grid=grid,
        in_specs=[_spec((SBQ, wide), lambda g, i: (i, g)),
                  _spec((t, wide), lambda g, i: (0, ngrp + g)),
                  _spec((t, wide), lambda g, i: (0, 2 * ngrp + g))] + (carry.in_specs if nc else []),
        out_specs=[_spec((SBQ, wide), lambda g, i: (i, g)), _spec((npair, SBQ, LANE), lambda g, i: (g, i, 0)),
                   pl.BlockSpec(memory_space=pltpu.SMEM)] + (carry.out_specs if nc else []),
        out_shape=[jax.ShapeDtypeStruct((t, 512), BF16), jax.ShapeDtypeStruct((4, t, LANE), F32),
                   jax.ShapeDtypeStruct((ngrp * nq,), F32)] + (carry.out_shape if nc else []),
        scratch_shapes=carry.scratch if nc else [],
        compiler_params=_params(("arbitrary", "arbitrary")),
    )(proj_b, proj_b, proj_b, *(carry.groups if nc else []))
    return outs[0], outs[1], outs[2], (carry.place(outs[3:]) if nc else [])


def _sb_bwd(proj_b, dy, ltot, nblk, rider=None):
    t = proj_b.shape[0]
    nq = t // SBQ
    scale = 1.0 / math.sqrt(64.0)

    npair = SB_PAIRS_BWD
    wide = npair * LANE
    ngrp = 4 // npair
    chains = [(p, head) for p in range(npair) for head in range(2)]
    grid = (ngrp, nq)
    nr = rider.n if rider is not None else 0
    per_count = SB_PAIRS_FWD // SB_PAIRS_BWD

    def body(*refs):
        q_ref, k_ref, v_ref, dy_ref, l_ref, n_ref = refs[:6]
        dq_ref, dk_ref, dv_ref = refs[6 + nr:9 + nr]
        dk_acc, dv_acc = refs[9 + 2 * nr:11 + 2 * nr]
        riders = (refs[6:6 + nr], refs[9 + nr:9 + 2 * nr], refs[-2], refs[-1]) if nr else None
        if nr:
            rider.ride(grid, riders, "start")
        grp = pl.program_id(0)
        qi = pl.program_id(1)

        @pl.when(qi == 0)
        def _():
            dk_acc[...] = jnp.zeros_like(dk_acc)
            dv_acc[...] = jnp.zeros_like(dv_acc)

        row, col = _sb_masks()
        m_prefix = _ones_where(row <= col)
        m_before = _ones_where(row < col)
        rows = len(chains) * SBQ
        strict = (lax.broadcasted_iota(jnp.int32, (rows, SBQ), 1)
                  < (lax.broadcasted_iota(jnp.int32, (rows, SBQ), 0) & (SBQ - 1)))
        q_pair, do_pair, ltot = [], [], []
        for p in range(npair):
            pl_ = slice(p * LANE, (p + 1) * LANE)
            q_all = q_ref[:, pl_]
            do_all = dy_ref[:, pl_].astype(BF16)
            q_pair.append(jnp.concatenate([jnp.where(_head_mask(h), q_all, jnp.zeros_like(q_all)) for h in range(2)], axis=0))
            do_pair.append(jnp.concatenate([jnp.where(_head_mask(h), do_all, jnp.zeros_like(do_all)) for h in range(2)], axis=0))
            ltot += [l_ref[p][:, h * 64:h * 64 + 1] for h in range(2)]
        ltot = jnp.concatenate(ltot, axis=0)

        def pair_rows(a, p):
            return a[2 * p * SBQ:2 * (p + 1) * SBQ]

        def block(kb, state, diag):
            seen, dseen, dq = state
            start = pl.multiple_of(kb * SBQ, SBQ)
            kk = [k_ref[pl.ds(start, SBQ), p * LANE:(p + 1) * LANE] for p in range(npair)]
            vv = [v_ref[pl.ds(start, SBQ), p * LANE:(p + 1) * LANE] for p in range(npair)]
            z = jnp.concatenate([_dg(q_pair[p], kk[p], NT) for p in range(npair)], axis=0) * scale
            lb = _log_sigmoid(z)
            lk = lb - z
            if diag:
                lk = jnp.where(strict, lk, 0.0)
            hi, lo = _split(lk)
            sums = _dg(jnp.concatenate([hi, lo], axis=0), m_prefix, NN)
            w = jnp.exp(lb + ((ltot - seen) - (sums[:rows] + sums[rows:])))
            if diag:
                w = jnp.where(strict, w, 0.0)
            wb = w.astype(BF16)
            da = w * jnp.concatenate([_dg(do_pair[p], vv[p], NT) for p in range(npair)], axis=0)
            dah, dal = _split(da)
            dsums = _dg(jnp.concatenate([dah, dal], axis=0), m_before, NN)
            sig = jnp.exp(lb)
            dz = (da * (1.0 - sig) - (dseen + dsums[:rows] + dsums[rows:]) * sig) * scale
            if diag:
                dz = jnp.where(strict, dz, 0.0)
            dzb = dz.astype(BF16)
            for p in range(npair):
                pl_ = slice(p * LANE, (p + 1) * LANE)
                dv_acc[pl.ds(start, SBQ), pl_] += _dg(pair_rows(wb, p), do_pair[p], TN)
                dk_acc[pl.ds(start, SBQ), pl_] += _dg(pair_rows(dzb, p), q_pair[p], TN)
            dq = dq + jnp.concatenate([_dg(pair_rows(dzb, p), kk[p], NN) for p in range(npair)], axis=0)
            seen = seen + jnp.sum(hi.astype(F32) + lo.astype(F32), axis=1, keepdims=True)
            dseen = dseen + jnp.sum(da, axis=1, keepdims=True)
            return seen, dseen, dq

        zero = (jnp.zeros((rows, 1), F32), jnp.zeros((rows, 1), F32), jnp.zeros((rows, LANE), F32))
        first = qi - n_ref[(grp // per_count) * nq + qi].astype(jnp.int32)
        state = lax.fori_loop(first, qi, lambda kb, c: block(kb, c, False), zero)
        _, _, dq = block(qi, state, True)
        for p in range(npair):
            dq_ref[:, p * LANE:(p + 1) * LANE] = jnp.where(
                _head_mask(0), dq[2 * p * SBQ:(2 * p + 1) * SBQ], dq[(2 * p + 1) * SBQ:(2 * p + 2) * SBQ]).astype(BF16)

        @pl.when(qi == nq - 1)
        def _():
            dk_ref[...] = dk_acc[...].astype(BF16)
            dv_ref[...] = dv_acc[...].astype(BF16)

        if nr:
            rider.ride(grid, riders, "finish")

    full = jax.ShapeDtypeStruct((t, 512), BF16)
    outs = pl.pallas_call(
        body, name="sb_bwd", grid=grid,
        in_specs=[_spec((SBQ, wide), lambda g, i: (i, g)),
                  _spec((t, wide), lambda g, i: (0, ngrp + g)),
                  _spec((t, wide), lambda g, i: (0, 2 * ngrp + g)),
                  _spec((SBQ, wide), lambda g, i: (i, ngrp + g)),
                  _spec((npair, SBQ, LANE), lambda g, i: (g, i, 0)),
                  pl.BlockSpec(memory_space=pltpu.SMEM)] + (rider.in_specs if nr else []),
        out_specs=[_spec((SBQ, wide), lambda g, i: (i, g)),
                   _spec((t, wide), lambda g, i: (0, g)), _spec((t, wide), lambda g, i: (0, g))]
        + (rider.out_specs if nr else []),
        out_shape=[full, full, full] + (rider.out_shape if nr else []),
        scratch_shapes=[pltpu.VMEM((t, wide), F32), pltpu.VMEM((t, wide), F32)] + (rider.scratch if nr else []),
        compiler_params=_params(("arbitrary", "arbitrary")),
    )(proj_b, proj_b, proj_b, dy, ltot, nblk, *(rider.groups if nr else []))
    return outs[0], outs[1], outs[2], (rider.place(outs[3:]) if nr else [])


def _hgrn_gates(qr, fr, c0, c1):
    mx = jnp.maximum(c0, c1)
    e0, e1 = jnp.exp(c0 - mx), jnp.exp(c1 - mx)
    lb = e1 / (e0 + e1)
    sx = _sigmoid(fr)
    f = lb + (1.0 - lb) * sx
    k = (1.0 - lb) * (1.0 - sx)
    sq = _sigmoid(qr)
    return lb, sx, f, k, sq, qr * sq


def _chunk_sums(mask, x):
    n = x.shape[1]
    hi, lo = _split(x)
    both = _dg(_ones_where(mask), jnp.concatenate([hi, lo], axis=1), NN)
    return both[:, :n] + both[:, n:]


def _chunk_masks():
    row = lax.broadcasted_iota(jnp.int32, (CHUNK, CHUNK), 0)
    col = lax.broadcasted_iota(jnp.int32, (CHUNK, CHUNK), 1)
    return col <= row, col >= row


def _hgrn_fwd(proj_c, small, carry=None):
    t = proj_c.shape[1]
    nc = t // CHUNK
    nh = D // HD

    hps = HGRN_HPS
    wide = hps * HD
    grid = (nh // hps, nc)
    nr = carry.n if carry is not None else 0

    def body(*refs):
        p_ref, c0_ref, c1_ref, gam_ref = refs[:4]
        o_ref, y_ref, sst_ref = refs[4 + nr:7 + nr]
        st_ref = refs[7 + 2 * nr]
        riders = (refs[4:4 + nr], refs[7 + nr:7 + 2 * nr], refs[-2], refs[-1]) if nr else None
        if nr:
            carry.ride(grid, riders, "start")
        c = pl.program_id(1)

        @pl.when(c == 0)
        def _():
            st_ref[...] = jnp.zeros_like(st_ref)

        _, _, f_all, k_all, _, q_all = _hgrn_gates(p_ref[0], p_ref[1], c0_ref[...], c1_ref[...])
        tril, _ = _chunk_masks()
        b_all = _chunk_sums(tril, jnp.log(f_all))
        for j in range(hps):
            ln = slice(j * HD, (j + 1) * HD)
            st0 = st_ref[j]
            sst_ref[j] = st0
            v, g = p_ref[2, :, ln], p_ref[3, :, ln]
            q, k, b = q_all[:, ln], k_all[:, ln], b_all[:, ln]
            bm = b[CHUNK // 2 - 1:CHUNK // 2]
            bl = b[CHUNK - 1:CHUNK]
            qd = q * jnp.exp(b)
            qt = q * jnp.exp(b - bm)
            kt = k * jnp.exp(bm - b)
            kl = k * jnp.exp(bl - b)
            vb = v.astype(BF16)
            att = jnp.where(tril, _dot3(qt, kt, NT), 0.0)
            o = _dg(qd.astype(BF16), st0.astype(BF16), NT) + _dg(att.astype(BF16), vb, NN)
            st_ref[j] = st0 * jnp.exp(bl) + _dg(vb, kl.astype(BF16), TN)
            o_ref[:, ln] = o
            r = lax.rsqrt(jnp.mean(o * o, axis=-1, keepdims=True) + RMS_EPS)
            y_ref[:, ln] = (o * r * gam_ref[...] * (g * _sigmoid(g))).astype(BF16)
        if nr:
            carry.ride(grid, riders, "finish")

    outs = pl.pallas_call(
        body, name="hgrn_fwd", grid=grid,
        in_specs=[_spec((4, CHUNK, wide), lambda h, c: (0, c, h)),
                  _spec((None, 1, wide), lambda h, c: (R_CLB, 0, h)),
                  _spec((None, 1, wide), lambda h, c: (R_CLB + 1, 0, h)),
                  _spec((None, 1, HD), lambda h, c: (R_GAM, 0, 0))] + (carry.in_specs if nr else []),
        out_specs=[_spec((CHUNK, wide), lambda h, c: (c, h)), _spec((CHUNK, wide), lambda h, c: (c, h)),
                   _spec((None, hps, HD, HD), lambda h, c: (c, h, 0, 0))] + (carry.out_specs if nr else []),
        out_shape=[jax.ShapeDtypeStruct((t, D), F32), jax.ShapeDtypeStruct((t, D), BF16),
                   jax.ShapeDtypeStruct((nc, nh, HD, HD), F32)] + (carry.out_shape if nr else []),
        scratch_shapes=[pltpu.VMEM((hps, HD, HD), F32)] + (carry.scratch if nr else []),
        compiler_params=_params(("arbitrary", "arbitrary")),
    )(proj_c, small, small, small, *(carry.groups if nr else []))
    return outs[0], outs[1], outs[2], (carry.place(outs[3:]) if nr else [])


def _hgrn_bwd(proj_c, small, o, sst, dyc, rider=None):
    t = proj_c.shape[1]
    nc = t // CHUNK
    nh = D // HD

    hps = HGRN_HPS
    wide = hps * HD
    ng = nh // hps
    grid = (ng, nc)
    nr = rider.n if rider is not None else 0

    def body(*refs):
        p_ref, c0_ref, c1_ref, gam_ref, o_ref, sst_ref, dy_ref = refs[:7]
        dp_ref, dclb_ref, dgam_ref = refs[7 + nr:10 + nr]
        dst_ref, dlb_acc, dgam_acc = refs[10 + 2 * nr:13 + 2 * nr]
        riders = (refs[7:7 + nr], refs[10 + nr:10 + 2 * nr], refs[-2], refs[-1]) if nr else None
        if nr:
            rider.ride(grid, riders, "start")
        group = pl.program_id(0)
        step = pl.program_id(1)

        @pl.when(step == 0)
        def _():
            dst_ref[...] = jnp.zeros_like(dst_ref)
            dlb_acc[...] = jnp.zeros_like(dlb_acc)

        @pl.when((step == 0) & (group == 0))
        def _():
            dgam_acc[...] = jnp.zeros_like(dgam_acc)

        gam = gam_ref[...]
        qr_all = p_ref[0]
        lb_all, sx_all, f_all, k_all, sq_all, q_all = _hgrn_gates(qr_all, p_ref[1], c0_ref[...], c1_ref[...])
        tril, triu = _chunk_masks()
        b_all = _chunk_sums(tril, jnp.log(f_all))
        dq_parts, dk_parts, db_parts, dgam_parts = [], [], [], []
        for j in range(hps):
            ln = slice(j * HD, (j + 1) * HD)
            st0 = sst_ref[j]
            dst1 = dst_ref[j]
            v, g = p_ref[2, :, ln], p_ref[3, :, ln]
            q, k, b = q_all[:, ln], k_all[:, ln], b_all[:, ln]
            bm = b[CHUNK // 2 - 1:CHUNK // 2]
            bl = b[CHUNK - 1:CHUNK]
            eb = jnp.exp(b)
            e_qt = jnp.exp(b - bm)
            e_kt = jnp.exp(bm - b)
            e_kl = jnp.exp(bl - b)
            e_bl = jnp.exp(bl)
            qd, qt, kt, kl = q * eb, q * e_qt, k * e_kt, k * e_kl
            ov = o_ref[:, ln]
            r = lax.rsqrt(jnp.mean(ov * ov, axis=-1, keepdims=True) + RMS_EPS)
            oh = ov * r
            sg = _sigmoid(g)
            dy = dy_ref[:, ln]
            dp_ref[3, :, ln] = (dy * oh * gam * (sg * (1.0 + g * (1.0 - sg)))).astype(BF16)
            dyv = dy * (g * sg)
            gdy = dyv * gam
            do = (gdy - oh * jnp.mean(gdy * oh, axis=-1, keepdims=True)) * r
            dob, vb = do.astype(BF16), v.astype(BF16)
            st0b, dst1b = st0.astype(BF16), dst1.astype(BF16)
            st1 = st0 * e_bl + _dg(vb, kl.astype(BF16), TN)
            att = jnp.where(tril, _dot3(qt, kt, NT), 0.0)
            datt = jnp.where(tril, _dg(dob, vb, NT), 0.0)
            dv = _dg(att.astype(BF16), dob, TN) + _dg(kl.astype(BF16), dst1b, NT)
            dq = _dot3(datt, kt, NN) * e_qt + _dg(dob, st0b, NN) * eb
            dk = _dot3(datt, qt, TN) * e_kt + _dg(vb, dst1b, NN) * e_kl
            db = q * dq - k * dk
            last = lax.broadcasted_iota(jnp.int32, (CHUNK, 1), 0) == CHUNK - 1
            db = db + jnp.where(last, jnp.sum(dst1 * st1, axis=0, keepdims=True), 0.0)
            dst_ref[j] = dst1 * e_bl + _dg(dob, qd.astype(BF16), TN)
            dp_ref[2, :, ln] = dv.astype(BF16)
            dq_parts.append(dq)
            dk_parts.append(dk)
            db_parts.append(db)
            dgam_parts.append(jnp.sum(dyv * oh, axis=0, keepdims=True))

        dq_all, dk_all = jnp.concatenate(dq_parts, axis=1), jnp.concatenate(dk_parts, axis=1)
        dlf = _chunk_sums(triu, jnp.concatenate(db_parts, axis=1))
        dp_ref[0] = (dq_all * (sq_all * (1.0 + qr_all * (1.0 - sq_all)))).astype(BF16)
        tmp = dlf / f_all - dk_all
        dp_ref[1] = (tmp * (1.0 - lb_all) * sx_all * (1.0 - sx_all)).astype(BF16)
        dlb_acc[...] += jnp.sum((1.0 - sx_all) * tmp, axis=0, keepdims=True)
        dgam_acc[...] += functools.reduce(lambda a, b: a + b, dgam_parts)

        @pl.when(step == nc - 1)
        def _():
            d1 = dlb_acc[...] * lb_all * (1.0 - lb_all)
            dclb_ref[...] = jnp.where(lax.broadcasted_iota(jnp.int32, (2, wide), 0) == 0, -d1, d1)

        @pl.when((step == nc - 1) & (group == ng - 1))
        def _():
            dgam_ref[...] = dgam_acc[...]

        if nr:
            rider.ride(grid, riders, "finish")

    rev = lambda h, s: (nc - 1 - s, h)
    outs = pl.pallas_call(
        body, name="hgrn_bwd", grid=grid,
        in_specs=[_spec((4, CHUNK, wide), lambda h, s: (0, nc - 1 - s, h)),
                  _spec((None, 1, wide), lambda h, s: (R_CLB, 0, h)),
                  _spec((None, 1, wide), lambda h, s: (R_CLB + 1, 0, h)),
                  _spec((None, 1, HD), lambda h, s: (R_GAM, 0, 0)),
                  _spec((CHUNK, wide), rev),
                  _spec((None, hps, HD, HD), lambda h, s: (nc - 1 - s, h, 0, 0)),
                  _spec((CHUNK, wide), rev)] + (rider.in_specs if nr else []),
        out_specs=[_spec((4, CHUNK, wide), lambda h, s: (0, nc - 1 - s, h)),
                   _spec((2, wide), lambda h, s: (0, h)),
                   _spec((1, HD), lambda h, s: (0, 0))] + (rider.out_specs if nr else []),
        out_shape=[jax.ShapeDtypeStruct((4, t, D), BF16), jax.ShapeDtypeStruct((2, D), F32),
                   jax.ShapeDtypeStruct((1, HD), F32)] + (rider.out_shape if nr else []),
        scratch_shapes=[pltpu.VMEM((hps, HD, HD), F32), pltpu.VMEM((1, wide), F32), pltpu.VMEM((1, HD), F32)]
        + (rider.scratch if nr else []),
        compiler_params=_params(("arbitrary", "arbitrary")),
    )(proj_c, small, small, small, o, sst, dyc, *(rider.groups if nr else []))
    return outs[0], outs[1], outs[2], (rider.place(outs[3:]) if nr else [])


def _adamw(name, w, g, m, v):
    rows, cols = w.shape
    br = rows
    for cand in (512, 352, 256):
        if rows % cand == 0:
            br = cand
            break
    c1 = 1.0 - ADAM_B1 ** ADAM_STEP
    c2 = 1.0 - ADAM_B2 ** ADAM_STEP

    def body(w_ref, g_ref, m_ref, v_ref, d_ref, mo_ref, vo_ref):
        gv = g_ref[...]
        mn = ADAM_B1 * m_ref[...] + (1.0 - ADAM_B1) * gv
        vn = ADAM_B2 * v_ref[...] + (1.0 - ADAM_B2) * (gv * gv)
        mo_ref[...] = mn
        vo_ref[...] = vn
        d_ref[...] = -ADAM_LR * ((mn / c1) / (jnp.sqrt(vn / c2) + ADAM_EPS) + ADAM_WD * w_ref[...])

    blk = _spec((br, cols), lambda i: (i, 0))
    shape = jax.ShapeDtypeStruct((rows, cols), F32)
    return pl.pallas_call(
        body, name=name, grid=(rows // br,), in_specs=[blk] * 4, out_specs=[blk] * 3, out_shape=[shape] * 3,
        compiler_params=_params(("parallel",)),
    )(w, g, m, v)


def _place():
    x, y, c = lax.axis_index("x"), lax.axis_index("y"), lax.axis_index("c")
    chips = [(1 - x, y), (x, 1 - y), (1 - x, 1 - y)]
    return x, y, c, chips


class _Rider:
    n = 0

    def ride(self, grid, refs, when):
        if not self.n:
            return
        ids = [pl.program_id(a) for a in range(len(grid))]
        edge = [i == (0 if when == "start" else g - 1) for i, g in zip(ids, grid)]
        cond = functools.reduce(jnp.logical_and, edge)

        @pl.when(cond)
        def _():
            (self.start if when == "start" else self.finish)(*refs)


class _Gather(_Rider):
    def __init__(self, groups):
        self.groups = list(groups)
        self.n = len(self.groups)
        self.in_specs = [ANY] * self.n
        self.out_specs = [ANY] * self.n
        self.out_shape = [jax.ShapeDtypeStruct((NSH,) + g.shape, g.dtype) for g in self.groups]
        self.scratch = [pltpu.SemaphoreType.DMA((6 * self.n,)), pltpu.SemaphoreType.DMA((6 * self.n,))] if self.n else []

    def _copies(self, ins, outs, send, recv):
        x, y, c, chips = _place()
        sibling = (x, y, 1 - c)

        def half(gi, chip, hc):
            return outs[gi].at[2 * chip[0] + chip[1], hc]

        def copy(gi, k, src, dst, to):
            return pltpu.make_async_remote_copy(src_ref=src, dst_ref=dst, send_sem=send.at[6 * gi + k],
                                                recv_sem=recv.at[6 * gi + k], device_id=to, device_id_type=MESH)

        pairs = [(gi, j, chip) for gi in range(self.n) for j, chip in enumerate(chips)]
        first = [copy(gi, j, ins[gi].at[c], half(gi, (x, y), c), (*chip, c)) for gi, j, chip in pairs]
        landed = [copy(gi, j, half(gi, chip, c), half(gi, chip, c), sibling) for gi, j, chip in pairs]
        relay = [copy(gi, 3 + j, half(gi, chip, c), half(gi, chip, c), sibling) for gi, j, chip in pairs]
        relayed = [copy(gi, 3 + j, half(gi, chip, 1 - c), half(gi, chip, 1 - c), sibling) for gi, j, chip in pairs]
        return first, landed, relay, relayed

    def start(self, ins, outs, send, recv):
        for cp in self._copies(ins, outs, send, recv)[0]:
            cp.start()

    def finish(self, ins, outs, send, recv):
        first, landed, relay, relayed = self._copies(ins, outs, send, recv)
        for arrived, onward in zip(landed, relay):
            arrived.wait_recv()
            onward.start()
        for cp in relayed:
            cp.wait_recv()
        for cp in first + relay:
            cp.wait_send()

    def place(self, outs):
        me = 2 * lax.axis_index("x") + lax.axis_index("y")
        return [lax.dynamic_update_index_in_dim(o, g, me, 0) for o, g in zip(outs, self.groups)]


def _gather_weights(groups):
    gather = _Gather(groups)
    ng = gather.n

    def body(*refs):
        parts = (refs[:ng], refs[ng:2 * ng], refs[2 * ng], refs[2 * ng + 1])
        gather.start(*parts)
        gather.finish(*parts)

    outs = pl.pallas_call(
        body, name="gather_weights", in_specs=gather.in_specs, out_specs=gather.out_specs, out_shape=gather.out_shape,
        scratch_shapes=gather.scratch, compiler_params=pltpu.CompilerParams(has_side_effects=True),
    )(*groups)
    return gather.place(outs)


def _swap_halves(name, slots):
    n = len(slots)

    def body(*refs):
        ins, outs = refs[:n], refs[n:2 * n]
        send, recv = refs[2 * n:]
        x, y, c, _ = _place()
        cps = []
        for i, (_, row0, rows) in enumerate(slots):
            half = rows // 2
            src = ins[i].at[:, pl.ds(pl.multiple_of(row0 + (1 - c) * half, 16), half)]
            cps.append(pltpu.make_async_remote_copy(src_ref=src, dst_ref=outs[i], send_sem=send.at[i],
                                                    recv_sem=recv.at[i], device_id=(x, y, 1 - c), device_id_type=MESH))
        for cp in cps:
            cp.start()
        for cp in cps:
            cp.wait()

    return pl.pallas_call(
        body, name=name, in_specs=[ANY] * n, out_specs=[ANY] * n,
        out_shape=[jax.ShapeDtypeStruct((NSH, rows // 2, buf.shape[2]), buf.dtype) for buf, _, rows in slots],
        scratch_shapes=[pltpu.SemaphoreType.DMA((n,)), pltpu.SemaphoreType.DMA((n,))],
        compiler_params=pltpu.CompilerParams(has_side_effects=True),
    )(*[buf for buf, _, _ in slots])


class _Send(_Rider):
    def __init__(self, arrays):
        self.groups = list(arrays)
        self.n = len(self.groups)
        self.in_specs = [ANY] * self.n
        self.out_specs = [ANY] * self.n
        self.out_shape = [jax.ShapeDtypeStruct((3,) + g.shape[1:], g.dtype) for g in self.groups]
        self.scratch = [pltpu.SemaphoreType.DMA((3 * self.n,)), pltpu.SemaphoreType.DMA((3 * self.n,))]

    def _copies(self, ins, outs, send, recv):
        x, y, c, chips = _place()
        return [pltpu.make_async_remote_copy(src_ref=ins[gi].at[2 * chip[0] + chip[1]], dst_ref=outs[gi].at[j],
                                             send_sem=send.at[3 * gi + j], recv_sem=recv.at[3 * gi + j],
                                             device_id=(*chip, c), device_id_type=MESH)
                for gi in range(self.n) for j, chip in enumerate(chips)]

    def start(self, ins, outs, send, recv):
        for cp in self._copies(ins, outs, send, recv):
            cp.start()

    def finish(self, ins, outs, send, recv):
        for cp in self._copies(ins, outs, send, recv):
            cp.wait()

    def place(self, outs):
        return list(outs)


def _share_halves(groups):
    ng = len(groups)

    def body(*refs):
        ins, outs = refs[:ng], refs[ng:2 * ng]
        send, recv = refs[2 * ng:]
        x, y, c, _ = _place()
        cps = [pltpu.make_async_remote_copy(src_ref=ins[gi], dst_ref=outs[gi].at[c], send_sem=send.at[gi],
                                            recv_sem=recv.at[gi], device_id=(x, y, 1 - c), device_id_type=MESH)
               for gi in range(ng)]
        for cp in cps:
            cp.start()
        for gi in range(ng):
            pltpu.make_async_remote_copy(src_ref=ins[gi], dst_ref=outs[gi].at[1 - c], send_sem=send.at[gi],
                                         recv_sem=recv.at[gi], device_id=(x, y, 1 - c), device_id_type=MESH).wait_recv()
        for cp in cps:
            cp.wait_send()

    outs = pl.pallas_call(
        body, name="grad_share_halves", in_specs=[ANY] * ng, out_specs=[ANY] * ng,
        out_shape=[jax.ShapeDtypeStruct((2,) + g.shape, g.dtype) for g in groups],
        scratch_shapes=[pltpu.SemaphoreType.DMA((ng,)), pltpu.SemaphoreType.DMA((ng,))],
        compiler_params=pltpu.CompilerParams(has_side_effects=True),
    )(*groups)
    c = lax.axis_index("c")
    return [lax.dynamic_update_index_in_dim(o, g, c, 0) for o, g in zip(outs, groups)]


def _pair_sum(name, slots, got, c_idx):
    n = len(slots)
    in_specs, out_specs, out_shape, operands = [], [], [], []
    for (buf, row0, rows), g in zip(slots, got):
        hb, cols = rows // 4, buf.shape[2]
        in_specs += [pl.BlockSpec((None, hb, cols), lambda q, i, cr, r=row0 // hb: (q, r + 2 * cr[0] + i, 0)),
                     pl.BlockSpec((None, hb, cols), lambda q, i, cr: (q, i, 0))]
        out_specs.append(pl.BlockSpec((None, hb, cols), lambda q, i, cr: (q, i, 0)))
        out_shape.append(jax.ShapeDtypeStruct(g.shape, BF16))
        operands += [buf, g]

    def body(c_ref, *refs):
        for i in range(n):
            refs[2 * n + i][...] = (refs[2 * i][...].astype(F32) + refs[2 * i + 1][...].astype(F32)).astype(BF16)

    return pl.pallas_call(
        body, name=name,
        grid_spec=pltpu.PrefetchScalarGridSpec(num_scalar_prefetch=1, grid=(NSH, 2), in_specs=in_specs, out_specs=out_specs),
        out_shape=out_shape, compiler_params=_params(("parallel", "parallel")),
    )(c_idx, *operands)


def _owner_sum(name, pairs, got, p_idx):
    n = len(pairs)
    in_specs, out_specs, out_shape, operands = [], [], [], []
    for own, g in zip(pairs, got):
        _, rows, cols = own.shape
        hb = rows // 2
        in_specs += [pl.BlockSpec((None, hb, cols), lambda i, pr: (pr[0], i, 0)),
                     pl.BlockSpec((3, hb, cols), lambda i, pr: (0, i, 0))]
        out_specs.append(pl.BlockSpec((hb, cols), lambda i, pr: (i, 0)))
        out_shape.append(jax.ShapeDtypeStruct((rows, cols), F32))
        operands += [own, g]

    def body(p_ref, *refs):
        for i in range(n):
            a_ref, b_ref = refs[2 * i], refs[2 * i + 1]
            refs[2 * n + i][...] = ((a_ref[...].astype(F32) + b_ref[0].astype(F32)) + b_ref[1].astype(F32)) + b_ref[2].astype(F32)

    return pl.pallas_call(
        body, name=name,
        grid_spec=pltpu.PrefetchScalarGridSpec(num_scalar_prefetch=1, grid=(2,), in_specs=in_specs, out_specs=out_specs),
        out_shape=out_shape, compiler_params=_params(("parallel",)),
    )(p_idx, *operands)


def _sum_small(slab):
    def body(in_ref, out_ref, all_ref, send, recv):
        x, y, c, _ = _place()
        me = 4 * x + 2 * y + c
        all_ref[me] = in_ref[...]
        cps = []
        for k in range(1, 8):
            peer = (x ^ (k >> 2), y ^ ((k >> 1) & 1), c ^ (k & 1))
            cps.append(pltpu.make_async_remote_copy(src_ref=in_ref, dst_ref=all_ref.at[me], send_sem=send.at[k - 1],
                                                    recv_sem=recv.at[k - 1], device_id=peer, device_id_type=MESH))
        for cp in cps:
            cp.start()
        for cp in cps:
            cp.wait()
        total = all_ref[0]
        for d in range(1, 8):
            total = total + all_ref[d]
        out_ref[...] = total

    return pl.pallas_call(
        body, name="sum_small",
        in_specs=[pl.BlockSpec(memory_space=pltpu.VMEM)], out_specs=pl.BlockSpec(memory_space=pltpu.VMEM),
        out_shape=jax.ShapeDtypeStruct(slab.shape, F32),
        scratch_shapes=[pltpu.VMEM((8,) + slab.shape, F32), pltpu.SemaphoreType.DMA((7,)), pltpu.SemaphoreType.DMA((7,))],
        compiler_params=pltpu.CompilerParams(has_side_effects=True),
    )(slab)


FFNS = ("pre0", "post0", "pre1", "post1")
W_SHAPES = dict({f + "_gu": (NSH, 2, FS, D) for f in FFNS}, **{f + "_d": (NSH, FS, D) for f in FFNS},
                ab_in=(NSH, D, 768), ab_out=(NSH, 256, D), conv=(NSH, 2, 8, LANE), c_in=(NSH, D, D), c_out=(NSH, 256, D))
CARRIED = dict(pre0_up=("pre0_d",), pre0_down=("ab_in", "ab_out", "conv"),
               sb_fwd=("post0_gu", "post0_d"), post0_up=("pre1_gu",), post0_down=("pre1_d",),
               pre1_up=("c_in", "c_out"), hgrn_fwd=("post1_gu", "post1_d"))


FFN_SLOTS = dict(pre0=(0, 2, 0), pre1=(1, 3, 1), post0=(4, 6, 2), post1=(5, 7, 3))
GRAD_SLOTS = dict(ab_out=("b", B_ABOUT, 256), c_in=("b", B_CIN, D), c_out=("b", B_COUT, 256), ab_in=("c", 0, D))
for _f, (_g, _u, _d) in FFN_SLOTS.items():
    GRAD_SLOTS.update({_f + "_g": ("a", _g * FS, FS), _f + "_u": ("a", _u * FS, FS), _f + "_d": ("b", _d * FS, FS)})
REDUCE_STAGES = dict(x=("post1_g", "post1_u", "post1_d", "c_out"),
                     y=("c_in", "pre1_g", "pre1_u", "pre1_d", "post0_g", "post0_u", "post0_d", "ab_out"),
                     z1=("ab_in",), z2=("pre0_g", "pre0_u", "pre0_d"))


def _local_step(x, target, weights, small, shards=None, place=None):
    t = x.shape[0]
    tm = min(MM_TILE, t)
    tw = min(WGRAD_TILE, t)
    nt = t // tm
    tok_si = _spec((tm, D), lambda s, i: (i, 0))
    w = dict(weights)
    reduced = {}

    def reduce_start(stage, buffers):
        if place is None:
            return (), [], None
        names = REDUCE_STAGES[stage]
        slots = [(buffers[GRAD_SLOTS[n][0]],) + GRAD_SLOTS[n][1:] for n in names]
        got = _swap_halves("grad_swap_" + stage, slots)
        pairs = _pair_sum("grad_pair_sum_" + stage, slots, got, place[0])
        return names, pairs, _Send(pairs)

    def reduce_end(stage, names, pairs, landed):
        if place is not None:
            reduced.update(zip(names, _owner_sum("grad_owner_sum_" + stage, pairs, landed, place[1])))

    def carried(kernel_name):
        names = [n for n in CARRIED[kernel_name] if n not in w]
        return names, (_Gather([shards[n] for n in names]) if names else None)

    def land(names, arrays):
        for n, a in zip(names, arrays):
            w[n] = a.reshape(W_SHAPES[n])

    def ffn_forward(tag, h, norm_row):
        hn = _norm_fwd(tag + "_norm", h, small, norm_row)
        names, gather = carried(tag + "_up") if tag + "_up" in CARRIED else ([], None)
        s_up, s_gate, a, got = _ffn_up(tag + "_up", hn, w[tag + "_gu"], gather)
        land(names, got)
        names, gather = carried(tag + "_down") if tag + "_down" in CARRIED else ([], None)
        out = _ffn_down(tag + "_down", a, w[tag + "_d"], h, gather)
        if gather is not None:
            out, got = out
            land(names, got)
        return out, (h, hn, s_up, s_gate, a)

    def out_proj(name, y, w_out, h):
        return _mm(name, [(y, _spec((tm, 256), lambda i, k: (i, k)), w_out, _spec((None, 256, D), lambda i, k: (k, 0, 0)))],
                   grid=(nt, NSH), o_shape=(t, D), o_dtype=F32, o_spec=_spec((tm, D), lambda i, k: (i, 0)),
                   dims=NN, kaxis=1, nk=NSH, acc_shape=(tm, D), res=(h, _spec((tm, D), lambda i, k: (i, 0))))

    def out_proj_bwd(tag, dhb, y, w_out, blk, grad_b):
        dy = _mm(tag + "_dy", [(dhb, tok_si, w_out, _spec((None, 256, D), lambda s, i: (s, 0, 0)))],
                 grid=(NSH, nt), o_shape=(t, D), o_dtype=F32, o_spec=_spec((tm, 256), lambda s, i: (i, s)),
                 dims=NT, kaxis=1, nk=1)
        grad_b = _wgrad(tag + "_dwout", y, _spec((tw, 256), lambda s, k: (k, s)), dhb, _spec((tw, D), lambda s, k: (k, 0)),
                        256, D, t, tw, grad_b, blk)
        return dy, grad_b

    h0 = x
    h1, pre0 = ffn_forward("pre0", h0, R_PRE)
    hn_ab = _norm_fwd("mix0_norm", h1, small, R_MIX)
    proj_a = _mm("ab_proj_a", [(hn_ab, tok_si, w["ab_in"], _spec((None, D, 768), lambda s, i: (s, 0, 0)))],
                 grid=(2, nt), o_shape=(t, 1536), o_dtype=F32, o_spec=_spec((tm, 768), lambda s, i: (i, s)),
                 dims=NN, kaxis=1, nk=1)
    proj_b = _mm("ab_proj_b", [(hn_ab, tok_si, w["ab_in"], _spec((None, D, 768), lambda s, i: (s + 2, 0, 0)))],
                 grid=(2, nt), o_shape=(t, 1536), o_dtype=BF16, o_spec=_spec((tm, 768), lambda s, i: (i, s)),
                 dims=NN, kaxis=1, nk=1)
    y_a = _conv_fwd(proj_a, w["conv"])
    names, gather = carried("sb_fwd")
    y_b, ltot, nblk, got = _sb_fwd(proj_b, gather)
    land(names, got)
    y_ab = jnp.concatenate([y_a, y_b], axis=1)
    h2 = out_proj("ab_out", y_ab, w["ab_out"], h1)
    h3, post0 = ffn_forward("post0", h2, R_POST)
    h4, pre1 = ffn_forward("pre1", h3, R_PRE + 1)
    hn_c = _norm_fwd("mix1_norm", h4, small, R_MIX + 1)
    proj_c = _mm("c_proj", [(hn_c, tok_si, w["c_in"], _spec((None, D, D), lambda s, i: (s, 0, 0)))],
                 grid=(NSH, nt), o_shape=(NSH, t, D), o_dtype=F32, o_spec=_spec((None, tm, D), lambda s, i: (s, i, 0)),
                 dims=NN, kaxis=1, nk=1)
    names, gather = carried("hgrn_fwd")
    o_c, y_c, sst, got = _hgrn_fwd(proj_c, small, gather)
    land(names, got)
    h5 = out_proj("c_out", y_c, w["c_out"], h4)
    h6, post1 = ffn_forward("post1", h5, R_POST + 1)
    dh, dhb, d_fin, loss = _final_loss(h6, small, target)

    dh, dhb, grad_a, grad_b, dn_post1, _, _ = _ffn_backward("post1", dh, dhb, *post1, w["post1_gu"], w["post1_d"],
                                                            *FFN_SLOTS["post1"], small, R_POST + 1, 8 * FS, B_ROWS)
    dy_c, grad_b = out_proj_bwd("c", dhb, y_c, w["c_out"], B_COUT // 256, grad_b)
    names, pairs, rider = reduce_start("x", dict(a=grad_a, b=grad_b))
    dproj_c, d_clb, d_gam, landed = _hgrn_bwd(proj_c, small, o_c, sst, dy_c, rider)
    reduce_end("x", names, pairs, landed)
    grad_b = _wgrad("c_dwin", hn_c, _spec((tw, D), lambda s, k: (k, 0)), dproj_c, _spec((None, tw, D), lambda s, k: (s, k, 0)),
                    D, D, t, tw, grad_b, B_CIN // D)
    dh, dhb, dn_mix1, _ = _dhn_norm("c_dhn", [(dproj_c, _spec((None, tm, D), lambda i, k: (k, i, 0)),
                                               w["c_in"], _spec((None, D, D), lambda i, k: (k, 0, 0)))],
                                    NT, h4, small, R_MIX + 1, dh)
    dh, dhb, grad_a, grad_b, dn_pre1, _, _ = _ffn_backward("pre1", dh, dhb, *pre1, w["pre1_gu"], w["pre1_d"],
                                                           *FFN_SLOTS["pre1"], small, R_PRE + 1, grad_a, grad_b)
    dh, dhb, grad_a, grad_b, dn_post0, _, _ = _ffn_backward("post0", dh, dhb, *post0, w["post0_gu"], w["post0_d"],
                                                            *FFN_SLOTS["post0"], small, R_POST, grad_a, grad_b)
    dy_ab, grad_b = out_proj_bwd("ab", dhb, y_ab, w["ab_out"], B_ABOUT // 256, grad_b)
    dab, dac, dax, d_conv = _conv_bwd(proj_a, w["conv"], dy_ab)
    names, pairs, rider = reduce_start("y", dict(a=grad_a, b=grad_b))
    dq, dk, dv, landed = _sb_bwd(proj_b, dy_ab, ltot, nblk, rider)
    reduce_end("y", names, pairs, landed)
    dproj_ab = jnp.concatenate([dab, dac, dax, dq, dk, dv], axis=1)
    grad_c = _wgrad("ab_dwin", hn_ab, _spec((tw, D), lambda s, k: (k, 0)), dproj_ab, _spec((tw, 768), lambda s, k: (k, s)),
                    D, 768, t, tw, D, 0)
    dh, dhb, dn_mix0, _ = _dhn_norm("ab_dhn", [(dproj_ab, _spec((tm, 768), lambda i, k: (i, k)),
                                                w["ab_in"], _spec((None, D, 768), lambda i, k: (k, 0, 0)))],
                                    NT, h1, small, R_MIX, dh)
    names, pairs, rider = reduce_start("z1", dict(c=grad_c))
    last = {}

    def own_gradients(buf_a, buf_b):
        last["z2"] = reduce_start("z2", dict(a=buf_a, b=buf_b))
        return last["z2"][2]

    dh, dhb, grad_a, grad_b, dn_pre0, landed, landed_own = _ffn_backward(
        "pre0", dh, dhb, *pre0, w["pre0_gu"], w["pre0_d"], *FFN_SLOTS["pre0"], small, R_PRE, grad_a, grad_b,
        rider, own_gradients if place is not None else None)
    reduce_end("z1", names, pairs, landed)
    if place is not None:
        reduce_end("z2", last["z2"][0], last["z2"][1], landed_own)
    zero = jnp.zeros((1, D), F32)
    conv_rows = jnp.pad(jnp.transpose(d_conv[:, :3, :], (1, 0, 2)).reshape(3, 512), ((0, 0), (0, D - 512)))
    small_grad = jnp.concatenate([
        dn_pre0, dn_pre1, dn_mix0, dn_mix1, dn_post0, dn_post1, d_clb, d_fin,
        jnp.pad(d_gam, ((0, 0), (0, D - HD))), conv_rows,
        jnp.pad(loss, ((0, 0), (0, D - 1))), zero, zero], axis=0)
    return dh, grad_a, grad_b, grad_c, small_grad, reduced


def _small_slab(rows):
    parts = [jnp.pad(r.astype(F32), ((0, 0), (0, D - r.shape[1]))) for r in rows]
    slab = jnp.concatenate(parts, axis=0)
    return jnp.pad(slab, ((0, SMALL_ROWS - slab.shape[0]), (0, 0)))


def kernel(x, ffn_pre_norm, ffn_pre_w_gate, ffn_pre_w_up, ffn_pre_w_down, mix_norm, ffn_post_norm, ffn_post_w_gate, ffn_post_w_up, ffn_post_w_down, ab_w_in, ab_conv_w, ab_w_out, c_w_in, c_lower_bounds, c_out_norm, c_w_out, final_norm, loss_target, m_ffn_pre_norm, m_ffn_pre_w_gate, m_ffn_pre_w_up, m_ffn_pre_w_down, m_mix_norm, m_ffn_post_norm, m_ffn_post_w_gate, m_ffn_post_w_up, m_ffn_post_w_down, m_ab_w_in, m_ab_conv_w, m_ab_w_out, m_c_w_in, m_c_lower_bounds, m_c_out_norm, m_c_w_out, m_final_norm, v_ffn_pre_norm, v_ffn_pre_w_gate, v_ffn_pre_w_up, v_ffn_pre_w_down, v_mix_norm, v_ffn_post_norm, v_ffn_post_w_gate, v_ffn_post_w_up, v_ffn_post_w_down, v_ab_w_in, v_ab_conv_w, v_ab_w_out, v_c_w_in, v_c_lower_bounds, v_c_out_norm, v_c_w_out, v_final_norm):
    t = x.shape[1]
    xi, yi, ci = lax.axis_index("x"), lax.axis_index("y"), lax.axis_index("c")
    p_idx = (2 * xi + yi).astype(jnp.int32).reshape(1)
    c_idx = ci.astype(jnp.int32).reshape(1)

    def halves(m):
        return m.astype(BF16).reshape(2, m.shape[0] // 2, m.shape[1])

    transposed = ("ffn_pre_w_gate", "ffn_pre_w_up", "ffn_post_w_gate", "ffn_post_w_up")

    def flip(a):
        return jnp.swapaxes(a, 1, 2)

    shards = {}
    for name, (w_gate, w_up, w_down, layer) in dict(
            pre0=(ffn_pre_w_gate, ffn_pre_w_up, ffn_pre_w_down, 0), post0=(ffn_post_w_gate, ffn_post_w_up, ffn_post_w_down, 0),
            pre1=(ffn_pre_w_gate, ffn_pre_w_up, ffn_pre_w_down, 1), post1=(ffn_post_w_gate, ffn_post_w_up, ffn_post_w_down, 1)).items():
        shards[name + "_gu"] = jnp.stack([flip(w_gate)[layer], flip(w_up)[layer]]).astype(BF16)
        shards[name + "_d"] = halves(w_down[layer])
    conv_pad = jnp.pad(ab_conv_w[0], ((0, 5), (0, 0)))
    shards.update(ab_in=halves(ab_w_in[0]), ab_out=halves(ab_w_out[0]), c_in=halves(c_w_in[0]), c_out=halves(c_w_out[0]),
                  conv=jnp.stack([conv_pad, jnp.zeros_like(conv_pad)]))
    first = _gather_weights([shards["pre0_gu"]])[0].reshape(W_SHAPES["pre0_gu"])

    small = _small_slab([ffn_pre_norm, mix_norm, ffn_post_norm, c_lower_bounds, final_norm.reshape(1, D), c_out_norm])
    small = small.reshape(SMALL_ROWS, 1, D)

    grad_x, _, _, _, small_grad, reduced = _local_step(x[0], loss_target[0], dict(pre0_gu=first), small, shards,
                                                        (c_idx, p_idx))
    order = list(reduced)
    whole = {n: g.reshape(2 * g.shape[1], g.shape[2]) for n, g in zip(order, _share_halves([reduced[n] for n in order]))}
    small_sum = _sum_small(small_grad)

    my_conv = lax.dynamic_slice(small_sum[R_CONV:R_CONV + 3], (0, (2 * xi + yi) * 128), (3, 128))
    grads = {
        "ffn_pre_norm": small_sum[R_PRE:R_PRE + 2], "mix_norm": small_sum[R_MIX:R_MIX + 2],
        "ffn_post_norm": small_sum[R_POST:R_POST + 2], "c_lower_bounds": small_sum[R_CLB:R_CLB + 2],
        "c_out_norm": small_sum[R_GAM:R_GAM + 1, :HD], "final_norm": small_sum[R_FIN],
        "ab_conv_w": my_conv.reshape(1, 3, 128),
        "ab_w_in": whole["ab_in"][None], "ab_w_out": whole["ab_out"][None],
        "c_w_in": whole["c_in"][None], "c_w_out": whole["c_out"][None],
    }
    for kind, key in (("gate", "_g"), ("up", "_u"), ("down", "_d")):
        grads["ffn_pre_w_" + kind] = jnp.stack([whole["pre0" + key], whole["pre1" + key]])
        grads["ffn_post_w_" + kind] = jnp.stack([whole["post0" + key], whole["post1" + key]])
    weights = dict(ffn_pre_norm=ffn_pre_norm, ffn_pre_w_gate=ffn_pre_w_gate, ffn_pre_w_up=ffn_pre_w_up, ffn_pre_w_down=ffn_pre_w_down, mix_norm=mix_norm, ffn_post_norm=ffn_post_norm, ffn_post_w_gate=ffn_post_w_gate, ffn_post_w_up=ffn_post_w_up, ffn_post_w_down=ffn_post_w_down, ab_w_in=ab_w_in, ab_conv_w=ab_conv_w, ab_w_out=ab_w_out, c_w_in=c_w_in, c_lower_bounds=c_lower_bounds, c_out_norm=c_out_norm, c_w_out=c_w_out, final_norm=final_norm)
    m_in = dict(ffn_pre_norm=m_ffn_pre_norm, ffn_pre_w_gate=m_ffn_pre_w_gate, ffn_pre_w_up=m_ffn_pre_w_up, ffn_pre_w_down=m_ffn_pre_w_down, mix_norm=m_mix_norm, ffn_post_norm=m_ffn_post_norm, ffn_post_w_gate=m_ffn_post_w_gate, ffn_post_w_up=m_ffn_post_w_up, ffn_post_w_down=m_ffn_post_w_down, ab_w_in=m_ab_w_in, ab_conv_w=m_ab_conv_w, ab_w_out=m_ab_w_out, c_w_in=m_c_w_in, c_lower_bounds=m_c_lower_bounds, c_out_norm=m_c_out_norm, c_w_out=m_c_w_out, final_norm=m_final_norm)
    v_in = dict(ffn_pre_norm=v_ffn_pre_norm, ffn_pre_w_gate=v_ffn_pre_w_gate, ffn_pre_w_up=v_ffn_pre_w_up, ffn_pre_w_down=v_ffn_pre_w_down, mix_norm=v_mix_norm, ffn_post_norm=v_ffn_post_norm, ffn_post_w_gate=v_ffn_post_w_gate, ffn_post_w_up=v_ffn_post_w_up, ffn_post_w_down=v_ffn_post_w_down, ab_w_in=v_ab_w_in, ab_conv_w=v_ab_conv_w, ab_w_out=v_ab_w_out, c_w_in=v_c_w_in, c_lower_bounds=v_c_lower_bounds, c_out_norm=v_c_out_norm, c_w_out=v_c_w_out, final_norm=v_final_norm)
    names = list(weights)
    big = [n for n in names if weights[n].size >= 65536]
    tiny = [n for n in names if n not in big]

    delta, new_m, new_v = {}, {}, {}
    for n in big:
        turn = flip if n in transposed else (lambda a: a)
        shape = turn(weights[n]).shape
        two_d = (shape[0] * shape[1], shape[2])
        d, m2, v2 = _adamw("adamw_" + n, turn(weights[n]).reshape(two_d), grads[n].reshape(two_d),
                           turn(m_in[n]).reshape(two_d), turn(v_in[n]).reshape(two_d))
        delta[n], new_m[n], new_v[n] = turn(d.reshape(shape)), turn(m2.reshape(shape)), turn(v2.reshape(shape))
        grads[n] = turn(grads[n])

    def tiny_slab(src):
        return _small_slab([src[n].reshape(-1, src[n].shape[-1]) for n in tiny])

    offs, row = {}, 0
    for n in tiny:
        nrows = weights[n].size // weights[n].shape[-1]
        offs[n] = (row, nrows)
        row += nrows
    d, m2, v2 = _adamw("adamw_small", tiny_slab(weights), tiny_slab(grads), tiny_slab(m_in), tiny_slab(v_in))
    for n in tiny:
        r0, nr = offs[n]
        shape = weights[n].shape
        for dst, src in ((delta, d), (new_m, m2), (new_v, v2)):
            dst[n] = src[r0:r0 + nr, :shape[-1]].reshape(shape)

    loss = small_sum[R_LOSS, 0]
    return (loss, grad_x.reshape(1, t, D), *[grads[n] for n in names], *[delta[n] for n in names],
            *[new_m[n] for n in names], *[new_v[n] for n in names])
```

```python
import functools
import math

import jax
import jax.numpy as jnp
from jax import lax
from jax.experimental import pallas as pl
from jax.experimental.pallas import tpu as pltpu

F32 = jnp.float32
BF16 = jnp.bfloat16
MESH = pl.DeviceIdType.MESH
ANY = pl.BlockSpec(memory_space=pl.ANY)

D = 1024
FS = 704
NSH = 4
RMS_EPS = 1e-6
MACARON = 0.5
CHUNK = 64
HD = 128
HGRN_HPS = 8
SBQ = 128
SB_PAIRS_FWD = 4
SB_PAIRS_BWD = 2
SB_DEAD = -105.0
CONV_HALO = 8
LANE = 128
ROW_TILE = 512
MM_TILE = 1024
WGRAD_TILE = 4096
VMEM_LIMIT = 48 * 1024 * 1024
VMEM_LIMIT_BIG = 58 * 1024 * 1024

ADAM_LR, ADAM_B1, ADAM_B2, ADAM_EPS, ADAM_WD, ADAM_STEP = 0.001, 0.9, 0.999, 1e-08, 0.01, 10

NN = ((1,), (0,))
NT = ((1,), (1,))
TN = ((0,), (0,))

SMALL_ROWS = 16
R_PRE, R_MIX, R_POST, R_CLB, R_FIN, R_GAM, R_CONV, R_LOSS = 0, 2, 4, 6, 8, 9, 10, 13

B_DOWN = 0
B_ABOUT = 4 * FS
B_CIN = B_ABOUT + 256
B_COUT = B_CIN + 1024
B_ROWS = B_COUT + 256


def _dg(a, b, dims):
    return lax.dot_general(a, b, (dims, ((), ())), preferred_element_type=F32)


def _split(x):
    hi = x.astype(BF16)
    lo = (x - hi.astype(F32)).astype(BF16)
    return hi, lo


def _dot3(a, b, dims):
    ah, al = _split(a)
    bh, bl = _split(b)
    if dims == TN:
        n = b.shape[1]
        both = _dg(ah, jnp.concatenate([bh, bl], axis=1), dims)
        return both[:, :n] + both[:, n:] + _dg(al, bh, dims)
    m = a.shape[0]
    both = _dg(jnp.concatenate([ah, al], axis=0), bh, dims)
    return both[:m] + both[m:] + _dg(ah, bl, dims)


def _sigmoid(x):
    return 1.0 / (1.0 + jnp.exp(-x))


def _params(sem):
    return pltpu.CompilerParams(dimension_semantics=sem, vmem_limit_bytes=VMEM_LIMIT)


def _spec(shape, imap):
    return pl.BlockSpec(shape, imap)


def _mm(name, pairs, *, grid, o_shape, o_dtype, o_spec, dims, kaxis, nk, acc_shape=None, res=None, scale=None,
        into=None, carry=None):
    npairs = len(pairs)
    operands, specs = [], []
    for a, a_spec, b, b_spec in pairs:
        operands += [a, b]
        specs += [a_spec, b_spec]
    if res is not None:
        operands.append(res[0])
        specs.append(res[1])
    aliases = {}
    if into is not None:
        aliases = {len(operands): 0}
        operands.append(into)
        specs.append(ANY)
    n_own = len(operands)
    nc = carry.n if carry is not None else 0
    if nc:
        operands += carry.groups
        specs += carry.in_specs

    def body(*refs):
        o_ref = refs[n_own + nc]
        riders = (refs[n_own:n_own + nc], refs[n_own + nc + 1:n_own + 2 * nc + 1], refs[-2], refs[-1]) if nc else None
        if nc:
            carry.ride(grid, riders, "start")
        part = None
        for n in range(npairs):
            d = _dg(refs[2 * n][...], refs[2 * n + 1][...], dims)
            part = d if part is None else part + d

        def finish(val):
            if scale is not None:
                val = val * scale
            if res is not None:
                val = val + refs[2 * npairs][...]
            o_ref[...] = val.astype(o_dtype)

        if nk == 1:
            finish(part)
        else:
            acc_ref = refs[n_own + 2 * nc + 1]
            k = pl.program_id(kaxis)

            @pl.when(k == 0)
            def _():
                acc_ref[...] = part

            @pl.when(k > 0)
            def _():
                acc_ref[...] += part

            @pl.when(k == nk - 1)
            def _():
                finish(acc_ref[...])
        if nc:
            carry.ride(grid, riders, "finish")

    sem = tuple("arbitrary" if (nc or (ax == kaxis and nk > 1)) else "parallel" for ax in range(len(grid)))
    outs = pl.pallas_call(
        body, name=name, grid=grid, in_specs=specs, out_specs=[o_spec] + (carry.out_specs if nc else []),
        out_shape=[jax.ShapeDtypeStruct(o_shape, o_dtype)] + (carry.out_shape if nc else []),
        scratch_shapes=([pltpu.VMEM(acc_shape, F32)] if nk > 1 else []) + (carry.scratch if nc else []),
        input_output_aliases=aliases,
        compiler_params=_params(sem),
    )(*operands)
    return (outs[0], carry.place(outs[1:])) if nc else outs[0]


def _norm_fwd(name, h, gain_slab, row):
    t = h.shape[0]
    tm = min(ROW_TILE, t)

    def body(h_ref, g_ref, o_ref):
        x = h_ref[...]
        r = lax.rsqrt(jnp.mean(x * x, axis=-1, keepdims=True) + RMS_EPS)
        o_ref[...] = (x * r * g_ref[...]).astype(BF16)

    return pl.pallas_call(
        body, name=name, grid=(t // tm,),
        in_specs=[_spec((tm, D), lambda i: (i, 0)), _spec((None, 1, D), lambda i: (row, 0, 0))],
        out_specs=_spec((tm, D), lambda i: (i, 0)),
        out_shape=jax.ShapeDtypeStruct((t, D), BF16),
        compiler_params=_params(("parallel",)),
    )(h, gain_slab)


def _dhn_norm(name, pairs, dims, h, gain_slab, row, dres, rider=None):
    t = h.shape[0]
    tm = min(MM_TILE, t)
    nt = t // tm
    grid = (nt, NSH)
    npairs = len(pairs)
    nr = rider.n if rider is not None else 0
    operands, specs = [], []
    for a, a_spec, b, b_spec in pairs:
        operands += [a, b]
        specs += [a_spec, b_spec]
    row_spec = _spec((tm, D), lambda i, k: (i, 0))
    operands += [h, gain_slab, dres]
    specs += [row_spec, _spec((None, 1, D), lambda i, k: (row, 0, 0)), row_spec]
    n_own = len(operands)

    def body(*refs):
        h_ref, g_ref, dres_ref = refs[2 * npairs:n_own]
        dh_ref, dhb_ref, dg_ref = refs[n_own + nr:n_own + nr + 3]
        acc_ref, gacc_ref = refs[n_own + 2 * nr + 3:n_own + 2 * nr + 5]
        riders = (refs[n_own:n_own + nr], refs[n_own + nr + 3:n_own + 2 * nr + 3], refs[-2], refs[-1]) if nr else None
        if nr:
            rider.ride(grid, riders, "start")
        i, k = pl.program_id(0), pl.program_id(1)
        part = None
        for n in range(npairs):
            d = _dg(refs[2 * n][...], refs[2 * n + 1][...], dims)
            part = d if part is None else part + d

        @pl.when(k == 0)
        def _():
            acc_ref[...] = part

        @pl.when(k > 0)
        def _():
            acc_ref[...] += part

        @pl.when(k == NSH - 1)
        def _():
            x = h_ref[...]
            r = lax.rsqrt(jnp.mean(x * x, axis=-1, keepdims=True) + RMS_EPS)
            xh = x * r
            dy = acc_ref[...]
            gdy = dy * g_ref[...]
            dh = dres_ref[...] + (gdy - xh * jnp.mean(gdy * xh, axis=-1, keepdims=True)) * r
            dh_ref[...] = dh
            dhb_ref[...] = dh.astype(BF16)
            gpart = jnp.sum((dy * xh).reshape(tm // 8, 8, D), axis=0)

            @pl.when(i == 0)
            def _():
                gacc_ref[...] = gpart

            @pl.when(i > 0)
            def _():
                gacc_ref[...] += gpart

            @pl.when(i == nt - 1)
            def _():
                dg_ref[...] = jnp.sum(gacc_ref[...], axis=0, keepdims=True)

        if nr:
            rider.ride(grid, riders, "finish")

    outs = pl.pallas_call(
        body, name=name, grid=grid, in_specs=specs + (rider.in_specs if nr else []),
        out_specs=[row_spec, row_spec, _spec((1, D), lambda i, k: (0, 0))] + (rider.out_specs if nr else []),
        out_shape=[jax.ShapeDtypeStruct((t, D), F32), jax.ShapeDtypeStruct((t, D), BF16),
                   jax.ShapeDtypeStruct((1, D), F32)] + (rider.out_shape if nr else []),
        scratch_shapes=[pltpu.VMEM((tm, D), F32), pltpu.VMEM((8, D), F32)] + (rider.scratch if nr else []),
        compiler_params=pltpu.CompilerParams(dimension_semantics=("arbitrary", "arbitrary"),
                                             vmem_limit_bytes=VMEM_LIMIT_BIG),
    )(*operands, *(rider.groups if nr else []))
    return outs[0], outs[1], outs[2], (rider.place(outs[3:]) if nr else [])


def _final_loss(h, gain_slab, target):
    t = h.shape[0]
    tm = min(ROW_TILE, t)
    nt = t // tm

    def body(h_ref, g_ref, t_ref, dh_ref, dhb_ref, dg_ref, loss_ref, acc_ref, lacc_ref):
        i = pl.program_id(0)
        x = h_ref[...]
        g = g_ref[...]
        r = lax.rsqrt(jnp.mean(x * x, axis=-1, keepdims=True) + RMS_EPS)
        xh = x * r
        err = xh * g - t_ref[...]
        dy = err * (1.0 / D)
        gdy = dy * g
        dh = (gdy - xh * jnp.mean(gdy * xh, axis=-1, keepdims=True)) * r
        dh_ref[...] = dh
        dhb_ref[...] = dh.astype(BF16)
        part = jnp.sum((dy * xh).reshape(tm // 8, 8, D), axis=0)
        lpart = jnp.sum((err * err).reshape(tm // 8, 8, D), axis=0)

        @pl.when(i == 0)
        def _():
            acc_ref[...] = part
            lacc_ref[...] = lpart

        @pl.when(i > 0)
        def _():
            acc_ref[...] += part
            lacc_ref[...] += lpart

        @pl.when(i == nt - 1)
        def _():
            dg_ref[...] = jnp.sum(acc_ref[...], axis=0, keepdims=True)
            rows = jnp.sum(lacc_ref[...], axis=0, keepdims=True)
            loss_ref[...] = jnp.sum(rows, axis=1, keepdims=True) * (0.5 / D)

    row_spec = _spec((tm, D), lambda i: (i, 0))
    return pl.pallas_call(
        body, name="final_loss", grid=(nt,),
        in_specs=[row_spec, _spec((None, 1, D), lambda i: (R_FIN, 0, 0)), row_spec],
        out_specs=[row_spec, row_spec, _spec((1, D), lambda i: (0, 0)), _spec((1, 1), lambda i: (0, 0))],
        out_shape=[jax.ShapeDtypeStruct((t, D), F32), jax.ShapeDtypeStruct((t, D), BF16),
                   jax.ShapeDtypeStruct((1, D), F32), jax.ShapeDtypeStruct((1, 1), F32)],
        scratch_shapes=[pltpu.VMEM((8, D), F32), pltpu.VMEM((8, D), F32)],
        compiler_params=_params(("arbitrary",)),
    )(h, gain_slab, target)


def _ffn_up(name, hn, wgu, carry=None):
    t = hn.shape[0]
    tm = min(MM_TILE, t)
    grid = (NSH, t // tm)
    nc = carry.n if carry is not None else 0

    def body(*refs):
        x_ref, wg_ref, wu_ref = refs[:3]
        s_up_ref, s_gate_ref, a_ref = refs[3 + nc:6 + nc]
        riders = (refs[3:3 + nc], refs[6 + nc:6 + 2 * nc], refs[-2], refs[-1]) if nc else None
        if nc:
            carry.ride(grid, riders, "start")
        x = x_ref[...]
        g = _dg(x, wg_ref[...], NT)
        u = _dg(x, wu_ref[...], NT)
        sg = _sigmoid(g)
        silu = g * sg
        s_up_ref[...] = (MACARON * silu).astype(BF16)
        s_gate_ref[...] = (MACARON * u * (sg * (1.0 + g * (1.0 - sg)))).astype(BF16)
        a_ref[...] = (silu * u).astype(BF16)
        if nc:
            carry.ride(grid, riders, "finish")

    act = _spec((None, tm, FS), lambda s, i: (s, i, 0))
    shape = jax.ShapeDtypeStruct((NSH, t, FS), BF16)
    outs = pl.pallas_call(
        body, name=name, grid=grid,
        in_specs=[_spec((tm, D), lambda s, i: (i, 0)),
                  _spec((None, None, FS, D), lambda s, i: (s, 0, 0, 0)),
                  _spec((None, None, FS, D), lambda s, i: (s, 1, 0, 0))] + (carry.in_specs if nc else []),
        out_specs=[act, act, act] + (carry.out_specs if nc else []),
        out_shape=[shape, shape, shape] + (carry.out_shape if nc else []),
        scratch_shapes=carry.scratch if nc else [],
        compiler_params=_params(("arbitrary", "arbitrary") if nc else ("parallel", "parallel")),
    )(hn, wgu, wgu, *(carry.groups if nc else []))
    return (outs[0], outs[1], outs[2], carry.place(outs[3:]) if nc else [])


def _ffn_down(name, a, wd, h, carry=None):
    t = h.shape[0]
    tm = min(MM_TILE, t)
    return _mm(name, [(a, _spec((None, tm, FS), lambda i, k: (k, i, 0)),
                       wd, _spec((None, FS, D), lambda i, k: (k, 0, 0)))],
               grid=(t // tm, NSH), o_shape=(t, D), o_dtype=F32, o_spec=_spec((tm, D), lambda i, k: (i, 0)),
               dims=NN, kaxis=1, nk=NSH, acc_shape=(tm, D), res=(h, _spec((tm, D), lambda i, k: (i, 0))),
               scale=MACARON, carry=carry)


def _ffn_bwd_up(name, dhb, wd, s_up, s_gate, rider=None):
    t = dhb.shape[0]
    tm = min(MM_TILE, t)
    grid = (NSH, t // tm)
    nr = rider.n if rider is not None else 0

    def body(*refs):
        dh_ref, wd_ref, s_up_ref, s_gate_ref = refs[:4]
        dg_ref, du_ref = refs[4 + nr:6 + nr]
        riders = (refs[4:4 + nr], refs[6 + nr:6 + 2 * nr], refs[-2], refs[-1]) if nr else None
        if nr:
            rider.ride(grid, riders, "start")
        da = _dg(dh_ref[...], wd_ref[...], NT).astype(BF16)
        du_ref[...] = da * s_up_ref[...]
        dg_ref[...] = da * s_gate_ref[...]
        if nr:
            rider.ride(grid, riders, "finish")

    act = _spec((None, tm, FS), lambda s, i: (s, i, 0))
    shape = jax.ShapeDtypeStruct((NSH, t, FS), BF16)
    outs = pl.pallas_call(
        body, name=name, grid=grid,
        in_specs=[_spec((tm, D), lambda s, i: (i, 0)), _spec((None, FS, D), lambda s, i: (s, 0, 0)), act, act]
        + (rider.in_specs if nr else []),
        out_specs=[act, act] + (rider.out_specs if nr else []),
        out_shape=[shape, shape] + (rider.out_shape if nr else []),
        scratch_shapes=rider.scratch if nr else [],
        compiler_params=_params(("arbitrary", "arbitrary") if nr else ("parallel", "parallel")),
    )(dhb, wd, s_up, s_gate, *(rider.groups if nr else []))
    return outs[0], outs[1], (rider.place(outs[2:]) if nr else [])


def _wgrad(name, a, a_spec, b, b_spec, out_rows, out_cols, t, tt, group, slot, scale=None, carry=None):
    first = isinstance(group, int)
    rows = group if first else group.shape[1]
    return _mm(name, [(a, a_spec, b, b_spec)], grid=(NSH, t // tt),
               o_shape=(NSH, rows, out_cols), o_dtype=BF16,
               o_spec=_spec((None, out_rows, out_cols), lambda s, k: (s, slot, 0)),
               dims=TN, kaxis=1, nk=t // tt, acc_shape=(out_rows, out_cols), scale=scale,
               into=None if first else group, carry=carry)


def _ffn_backward(tag, dh, dhb, h_in, hn, s_up, s_gate, a, wgu, wd, gate_idx, up_idx, down_idx, small, norm_row,
                  grad_a, grad_b, rider_up=None, rider_dhn=None):
    t = dh.shape[0]
    tm = min(MM_TILE, t)
    tt = min(WGRAD_TILE, t)
    dg, du, landed_up = _ffn_bwd_up(tag + "_bwd_up", dhb, wd, s_up, s_gate, rider_up)
    tok = _spec((tt, D), lambda s, k: (k, 0))
    hid = _spec((None, tt, FS), lambda s, k: (s, k, 0))
    grad_b = _wgrad(tag + "_dwd", a, hid, dhb, tok, FS, D, t, tt, grad_b, down_idx, scale=MACARON)
    grad_a = _wgrad(tag + "_dwg", dg, hid, hn, tok, FS, D, t, tt, grad_a, gate_idx)
    grad_a = _wgrad(tag + "_dwu", du, hid, hn, tok, FS, D, t, tt, grad_a, up_idx)
    rider = rider_dhn(grad_a, grad_b) if rider_dhn is not None else None
    act = _spec((None, tm, FS), lambda i, k: (k, i, 0))
    dh_in, dhb_in, d_gain, landed_dhn = _dhn_norm(
        tag + "_dhn", [(dg, act, wgu, _spec((None, None, FS, D), lambda i, k: (k, 0, 0, 0))),
                       (du, act, wgu, _spec((None, None, FS, D), lambda i, k: (k, 1, 0, 0)))],
        NN, h_in, small, norm_row, dh, rider)
    return dh_in, dhb_in, grad_a, grad_b, d_gain, landed_up, landed_dhn


def _conv_fwd(proj_a, conv_w):
    t = proj_a.shape[0]
    tm = min(ROW_TILE, t)
    hb = tm // CONV_HALO

    def body(ab_ref, ac_ref, ax_ref, acp_ref, axp_ref, w_ref, y_ref):
        i = pl.program_id(1)
        u = ac_ref[...] * ax_ref[...]
        up = jnp.where(i > 0, acp_ref[...] * axp_ref[...], 0.0)
        ext = jnp.concatenate([up, u], axis=0)
        u1 = pltpu.roll(ext, 1, 0)[CONV_HALO:]
        u2 = pltpu.roll(ext, 2, 0)[CONV_HALO:]
        w = w_ref[...]
        conv = w[0:1] * u2 + w[1:2] * u1 + w[2:3] * u
        y_ref[...] = (ab_ref[...] * conv).astype(BF16)

    def cur(off):
        return _spec((tm, LANE), lambda j, i: (i, off + j))

    def prev(off):
        return _spec((CONV_HALO, LANE), lambda j, i: (jnp.maximum(i * hb - 1, 0), off + j))

    return pl.pallas_call(
        body, name="conv_fwd", grid=(4, t // tm),
        in_specs=[cur(0), cur(4), cur(8), prev(4), prev(8),
                  _spec((None, None, 8, LANE), lambda j, i: (j, 0, 0, 0))],
        out_specs=_spec((tm, LANE), lambda j, i: (i, j)),
        out_shape=jax.ShapeDtypeStruct((t, 512), BF16),
        compiler_params=_params(("parallel", "parallel")),
    )(proj_a, proj_a, proj_a, proj_a, proj_a, conv_w)


def _conv_bwd(proj_a, conv_w, dy, rider=None):
    t = proj_a.shape[0]
    tm = min(ROW_TILE, t)
    hb = tm // CONV_HALO
    nt = t // tm
    grid = (4, nt)
    nr = rider.n if rider is not None else 0

    def body(*refs):
        ab_ref, ac_ref, ax_ref, dy_ref, acp_ref, axp_ref, abn_ref, dyn_ref, w_ref = refs[:9]
        dab_ref, dac_ref, dax_ref, dw_ref = refs[9 + nr:13 + nr]
        acc_ref = refs[13 + 2 * nr]
        riders = (refs[9:9 + nr], refs[13 + nr:13 + 2 * nr], refs[-2], refs[-1]) if nr else None
        if nr:
            rider.ride(grid, riders, "start")
        i = pl.program_id(1)
        ab, ac, ax = ab_ref[...], ac_ref[...], ax_ref[...]
        u = ac * ax
        up = jnp.where(i > 0, acp_ref[...] * axp_ref[...], 0.0)
        ext = jnp.concatenate([up, u], axis=0)
        u1 = pltpu.roll(ext, 1, 0)[CONV_HALO:]
        u2 = pltpu.roll(ext, 2, 0)[CONV_HALO:]
        w = w_ref[...]
        conv = w[0:1] * u2 + w[1:2] * u1 + w[2:3] * u
        dy_v = dy_ref[...]
        dab_ref[...] = (dy_v * conv).astype(BF16)
        dc = dy_v * ab
        dcn = jnp.where(i < nt - 1, dyn_ref[...] * abn_ref[...], 0.0)
        extn = jnp.concatenate([dc, dcn], axis=0)
        n = tm + CONV_HALO
        dc1 = pltpu.roll(extn, n - 1, 0)[:tm]
        dc2 = pltpu.roll(extn, n - 2, 0)[:tm]
        du = w[2:3] * dc + w[1:2] * dc1 + w[0:1] * dc2
        dac_ref[...] = (du * ax).astype(BF16)
        dax_ref[...] = (du * ac).astype(BF16)
        rid = lax.broadcasted_iota(jnp.int32, (8, LANE), 0)
        part = jnp.where(rid == 0, jnp.sum(dc * u2, axis=0, keepdims=True),
                         jnp.where(rid == 1, jnp.sum(dc * u1, axis=0, keepdims=True),
                                   jnp.where(rid == 2, jnp.sum(dc * u, axis=0, keepdims=True), 0.0)))

        @pl.when(i == 0)
        def _():
            acc_ref[...] = part

        @pl.when(i > 0)
        def _():
            acc_ref[...] += part

        @pl.when(i == nt - 1)
        def _():
            dw_ref[...] = acc_ref[...]

        if nr:
            rider.ride(grid, riders, "finish")

    def cur(off):
        return _spec((tm, LANE), lambda j, i: (i, off + j))

    def prev(off):
        return _spec((CONV_HALO, LANE), lambda j, i: (jnp.maximum(i * hb - 1, 0), off + j))

    def nxt(off):
        return _spec((CONV_HALO, LANE), lambda j, i: (jnp.minimum((i + 1) * hb, nt * hb - 1), off + j))

    outs = pl.pallas_call(
        body, name="conv_bwd", grid=grid,
        in_specs=[cur(0), cur(4), cur(8), cur(0), prev(4), prev(8), nxt(0), nxt(0),
                  _spec((None, None, 8, LANE), lambda j, i: (j, 0, 0, 0))] + (rider.in_specs if nr else []),
        out_specs=[_spec((tm, LANE), lambda j, i: (i, j)), _spec((tm, LANE), lambda j, i: (i, j)),
                   _spec((tm, LANE), lambda j, i: (i, j)), _spec((None, 8, LANE), lambda j, i: (j, 0, 0))]
        + (rider.out_specs if nr else []),
        out_shape=[jax.ShapeDtypeStruct((t, 512), BF16), jax.ShapeDtypeStruct((t, 512), BF16),
                   jax.ShapeDtypeStruct((t, 512), BF16), jax.ShapeDtypeStruct((4, 8, LANE), F32)]
        + (rider.out_shape if nr else []),
        scratch_shapes=[pltpu.VMEM((8, LANE), F32)] + (rider.scratch if nr else []),
        compiler_params=_params(("arbitrary", "arbitrary") if nr else ("parallel", "arbitrary")),
    )(proj_a, proj_a, proj_a, dy, proj_a, proj_a, proj_a, dy, conv_w, *(rider.groups if nr else []))
    return outs[0], outs[1], outs[2], outs[3], (rider.place(outs[4:]) if nr else [])


def _log_sigmoid(z):
    return jnp.minimum(z, 0.0) - jnp.log(1.0 + jnp.exp(-jnp.abs(z)))


def _sb_masks():
    row = lax.broadcasted_iota(jnp.int32, (SBQ, SBQ), 0)
    col = lax.broadcasted_iota(jnp.int32, (SBQ, SBQ), 1)
    return row, col


def _ones_where(mask):
    return jnp.where(mask, 1.0, 0.0).astype(BF16)


def _head_mask(head):
    lane = lax.broadcasted_iota(jnp.int32, (1, LANE), 1)
    return lane >= 64 if head else lane < 64


def _sb_fwd(proj_b, carry=None):
    t = proj_b.shape[0]
    nq = t // SBQ
    scale = 1.0 / math.sqrt(64.0)

    npair = SB_PAIRS_FWD
    wide = npair * LANE
    ngrp = 4 // npair
    chains = [(p, head) for p in range(npair) for head in range(2)]

    grid = (ngrp, nq)
    nc = carry.n if carry is not None else 0

    def body(*refs):
        q_ref, k_ref, v_ref = refs[:3]
        y_ref, l_ref, n_ref = refs[3 + nc:6 + nc]
        riders = (refs[3:3 + nc], refs[6 + nc:6 + 2 * nc], refs[-2], refs[-1]) if nc else None
        if nc:
            carry.ride(grid, riders, "start")
        grp = pl.program_id(0)
        qi = pl.program_id(1)
        row, col = _sb_masks()
        m_suffix = _ones_where(row > col)
        rows = len(chains) * SBQ
        strict = (lax.broadcasted_iota(jnp.int32, (rows, SBQ), 1)
                  < (lax.broadcasted_iota(jnp.int32, (rows, SBQ), 0) & (SBQ - 1)))
        q_pair = []
        for p in range(npair):
            q_all = q_ref[:, p * LANE:(p + 1) * LANE]
            q_pair.append(jnp.concatenate([jnp.where(_head_mask(head), q_all, jnp.zeros_like(q_all)) for head in range(2)],
                                          axis=0))

        def block(kb, state, diag):
            run, acc = state
            start = pl.multiple_of(kb * SBQ, SBQ)
            z = jnp.concatenate([_dg(q_pair[p], k_ref[pl.ds(start, SBQ), p * LANE:(p + 1) * LANE], NT)
                                 for p in range(npair)], axis=0) * scale
            lb = _log_sigmoid(z)
            lk = lb - z
            if diag:
                lk = jnp.where(strict, lk, 0.0)
            hi, lo = _split(lk)
            sums = _dg(jnp.concatenate([hi, lo], axis=0), m_suffix, NN)
            w = jnp.exp(lb + (run + sums[:rows] + sums[rows:]))
            if diag:
                w = jnp.where(strict, w, 0.0)
            wb = w.astype(BF16)
            acc = acc + jnp.concatenate(
                [_dg(wb[2 * p * SBQ:2 * (p + 1) * SBQ], v_ref[pl.ds(start, SBQ), p * LANE:(p + 1) * LANE], NN)
                 for p in range(npair)], axis=0)
            run = run + jnp.sum(hi.astype(F32) + lo.astype(F32), axis=1, keepdims=True)
            return run, acc

        state = block(qi, (jnp.zeros((rows, 1), F32), jnp.zeros((rows, LANE), F32)), True)

        def live(c):
            return jnp.logical_and(c[0] < qi, jnp.max(c[1][0]) > SB_DEAD)

        def step(c):
            return c[0] + 1, block(qi - 1 - c[0], c[1], False)

        count, (run, acc) = lax.while_loop(live, step, (jnp.int32(0), state))
        n_ref[grp * nq + qi] = count.astype(F32)
        hm = _head_mask(0)
        for p in range(npair):
            lo_rows, hi_rows = slice(2 * p * SBQ, (2 * p + 1) * SBQ), slice((2 * p + 1) * SBQ, (2 * p + 2) * SBQ)
            y_ref[:, p * LANE:(p + 1) * LANE] = jnp.where(hm, acc[lo_rows], acc[hi_rows]).astype(BF16)
            l_ref[p] = jnp.where(hm, run[lo_rows], run[hi_rows])
        if nc:
            carry.ride(grid, riders, "finish")

    outs = pl.pallas_call(
        body, name="sb_fwd", grid=grid,
        in_specs=[_spec((SBQ, wide), lambda g, i: (i, g)),
                  _spec((t, wide), lambda g, i: (0, ngrp + g)),
                  _spec((t, wide), lambda g, i: (0, 2 * ngrp + g))] + (carry.in_specs if nc else []),
        out_specs=[_spec((SBQ, wide), lambda g, i: (i, g)), _spec((npair, SBQ, LANE), lambda g, i: (g, i, 0)),
                   pl.BlockSpec(memory_space=pltpu.SMEM)] + (carry.out_specs if nc else []),
        out_shape=[jax.ShapeDtypeStruct((t, 512), BF16), jax.ShapeDtypeStruct((4, t, LANE), F32),
                   jax.ShapeDtypeStruct((ngrp * nq,), F32)] + (carry.out_shape if nc else []),
        scratch_shapes=carry.scratch if nc else [],
        compiler_params=_params(("arbitrary", "arbitrary")),
    )(proj_b, proj_b, proj_b, *(carry.groups if nc else []))
    return outs[0], outs[1], outs[2], (carry.place(outs[3:]) if nc else [])


def _sb_bwd(proj_b, dy, ltot, nblk, rider=None):
    t = proj_b.shape[0]
    nq = t // SBQ
    scale = 1.0 / math.sqrt(64.0)

    npair = SB_PAIRS_BWD
    wide = npair * LANE
    ngrp = 4 // npair
    chains = [(p, head) for p in range(npair) for head in range(2)]
    grid = (ngrp, nq)
    nr = rider.n if rider is not None else 0
    per_count = SB_PAIRS_FWD // SB_PAIRS_BWD

    def body(*refs):
        q_ref, k_ref, v_ref, dy_ref, l_ref, n_ref = refs[:6]
        dq_ref, dk_ref, dv_ref = refs[6 + nr:9 + nr]
        dk_acc, dv_acc = refs[9 + 2 * nr:11 + 2 * nr]
        riders = (refs[6:6 + nr], refs[9 + nr:9 + 2 * nr], refs[-2], refs[-1]) if nr else None
        if nr:
            rider.ride(grid, riders, "start")
        grp = pl.program_id(0)
        qi = pl.program_id(1)

        @pl.when(qi == 0)
        def _():
            dk_acc[...] = jnp.zeros_like(dk_acc)
            dv_acc[...] = jnp.zeros_like(dv_acc)

        row, col = _sb_masks()
        m_prefix = _ones_where(row <= col)
        m_before = _ones_where(row < col)
        rows = len(chains) * SBQ
        strict = (lax.broadcasted_iota(jnp.int32, (rows, SBQ), 1)
                  < (lax.broadcasted_iota(jnp.int32, (rows, SBQ), 0) & (SBQ - 1)))
        q_pair, do_pair, ltot = [], [], []
        for p in range(npair):
            pl_ = slice(p * LANE, (p + 1) * LANE)
            q_all = q_ref[:, pl_]
            do_all = dy_ref[:, pl_].astype(BF16)
            q_pair.append(jnp.concatenate([jnp.where(_head_mask(h), q_all, jnp.zeros_like(q_all)) for h in range(2)], axis=0))
            do_pair.append(jnp.concatenate([jnp.where(_head_mask(h), do_all, jnp.zeros_like(do_all)) for h in range(2)], axis=0))
            ltot += [l_ref[p][:, h * 64:h * 64 + 1] for h in range(2)]
        ltot = jnp.concatenate(ltot, axis=0)

        def pair_rows(a, p):
            return a[2 * p * SBQ:2 * (p + 1) * SBQ]

        def block(kb, state, diag):
            seen, dseen, dq = state
            start = pl.multiple_of(kb * SBQ, SBQ)
            kk = [k_ref[pl.ds(start, SBQ), p * LANE:(p + 1) * LANE] for p in range(npair)]
            vv = [v_ref[pl.ds(start, SBQ), p * LANE:(p + 1) * LANE] for p in range(npair)]
            z = jnp.concatenate([_dg(q_pair[p], kk[p], NT) for p in range(npair)], axis=0) * scale
            lb = _log_sigmoid(z)
            lk = lb - z
            if diag:
                lk = jnp.where(strict, lk, 0.0)
            hi, lo = _split(lk)
            sums = _dg(jnp.concatenate([hi, lo], axis=0), m_prefix, NN)
            w = jnp.exp(lb + ((ltot - seen) - (sums[:rows] + sums[rows:])))
            if diag:
                w = jnp.where(strict, w, 0.0)
            wb = w.astype(BF16)
            da = w * jnp.concatenate([_dg(do_pair[p], vv[p], NT) for p in range(npair)], axis=0)
            dah, dal = _split(da)
            dsums = _dg(jnp.concatenate([dah, dal], axis=0), m_before, NN)
            sig = jnp.exp(lb)
            dz = (da * (1.0 - sig) - (dseen + dsums[:rows] + dsums[rows:]) * sig) * scale
            if diag:
                dz = jnp.where(strict, dz, 0.0)
            dzb = dz.astype(BF16)
            for p in range(npair):
                pl_ = slice(p * LANE, (p + 1) * LANE)
                dv_acc[pl.ds(start, SBQ), pl_] += _dg(pair_rows(wb, p), do_pair[p], TN)
                dk_acc[pl.ds(start, SBQ), pl_] += _dg(pair_rows(dzb, p), q_pair[p], TN)
            dq = dq + jnp.concatenate([_dg(pair_rows(dzb, p), kk[p], NN) for p in range(npair)], axis=0)
            seen = seen + jnp.sum(hi.astype(F32) + lo.astype(F32), axis=1, keepdims=True)
            dseen = dseen + jnp.sum(da, axis=1, keepdims=True)
            return seen, dseen, dq

        zero = (jnp.zeros((rows, 1), F32), jnp.zeros((rows, 1), F32), jnp.zeros((rows, LANE), F32))
        first = qi - n_ref[(grp // per_count) * nq + qi].astype(jnp.int32)
        state = lax.fori_loop(first, qi, lambda kb, c: block(kb, c, False), zero)
        _, _, dq = block(qi, state, True)
        for p in range(npair):
            dq_ref[:, p * LANE:(p + 1) * LANE] = jnp.where(
                _head_mask(0), dq[2 * p * SBQ:(2 * p + 1) * SBQ], dq[(2 * p + 1) * SBQ:(2 * p + 2) * SBQ]).astype(BF16)

        @pl.when(qi == nq - 1)
        def _():
            dk_ref[...] = dk_acc[...].astype(BF16)
            dv_ref[...] = dv_acc[...].astype(BF16)

        if nr:
            rider.ride(grid, riders, "finish")

    full = jax.ShapeDtypeStruct((t, 512), BF16)
    outs = pl.pallas_call(
        body, name="sb_bwd", grid=grid,
        in_specs=[_spec((SBQ, wide), lambda g, i: (i, g)),
                  _spec((t, wide), lambda g, i: (0, ngrp + g)),
                  _spec((t, wide), lambda g, i: (0, 2 * ngrp + g)),
                  _spec((SBQ, wide), lambda g, i: (i, ngrp + g)),
                  _spec((npair, SBQ, LANE), lambda g, i: (g, i, 0)),
                  pl.BlockSpec(memory_space=pltpu.SMEM)] + (rider.in_specs if nr else []),
        out_specs=[_spec((SBQ, wide), lambda g, i: (i, g)),
                   _spec((t, wide), lambda g, i: (0, g)), _spec((t, wide), lambda g, i: (0, g))]
        + (rider.out_specs if nr else []),
        out_shape=[full, full, full] + (rider.out_shape if nr else []),
        scratch_shapes=[pltpu.VMEM((t, wide), F32), pltpu.VMEM((t, wide), F32)] + (rider.scratch if nr else []),
        compiler_params=_params(("arbitrary", "arbitrary")),
    )(proj_b, proj_b, proj_b, dy, ltot, nblk, *(rider.groups if nr else []))
    return outs[0], outs[1], outs[2], (rider.place(outs[3:]) if nr else [])


def _hgrn_gates(qr, fr, c0, c1):
    mx = jnp.maximum(c0, c1)
    e0, e1 = jnp.exp(c0 - mx), jnp.exp(c1 - mx)
    lb = e1 / (e0 + e1)
    sx = _sigmoid(fr)
    f = lb + (1.0 - lb) * sx
    k = (1.0 - lb) * (1.0 - sx)
    sq = _sigmoid(qr)
    return lb, sx, f, k, sq, qr * sq


def _chunk_sums(mask, x):
    n = x.shape[1]
    hi, lo = _split(x)
    both = _dg(_ones_where(mask), jnp.concatenate([hi, lo], axis=1), NN)
    return both[:, :n] + both[:, n:]


def _chunk_masks():
    row = lax.broadcasted_iota(jnp.int32, (CHUNK, CHUNK), 0)
    col = lax.broadcasted_iota(jnp.int32, (CHUNK, CHUNK), 1)
    return col <= row, col >= row


def _hgrn_fwd(proj_c, small, carry=None):
    t = proj_c.shape[1]
    nc = t // CHUNK
    nh = D // HD

    hps = HGRN_HPS
    wide = hps * HD
    grid = (nh // hps, nc)
    nr = carry.n if carry is not None else 0

    def body(*refs):
        p_ref, c0_ref, c1_ref, gam_ref = refs[:4]
        o_ref, y_ref, sst_ref = refs[4 + nr:7 + nr]
        st_ref = refs[7 + 2 * nr]
        riders = (refs[4:4 + nr], refs[7 + nr:7 + 2 * nr], refs[-2], refs[-1]) if nr else None
        if nr:
            carry.ride(grid, riders, "start")
        c = pl.program_id(1)

        @pl.when(c == 0)
        def _():
            st_ref[...] = jnp.zeros_like(st_ref)

        _, _, f_all, k_all, _, q_all = _hgrn_gates(p_ref[0], p_ref[1], c0_ref[...], c1_ref[...])
        tril, _ = _chunk_masks()
        b_all = _chunk_sums(tril, jnp.log(f_all))
        for j in range(hps):
            ln = slice(j * HD, (j + 1) * HD)
            st0 = st_ref[j]
            sst_ref[j] = st0
            v, g = p_ref[2, :, ln], p_ref[3, :, ln]
            q, k, b = q_all[:, ln], k_all[:, ln], b_all[:, ln]
            bm = b[CHUNK // 2 - 1:CHUNK // 2]
            bl = b[CHUNK - 1:CHUNK]
            qd = q * jnp.exp(b)
            qt = q * jnp.exp(b - bm)
            kt = k * jnp.exp(bm - b)
            kl = k * jnp.exp(bl - b)
            vb = v.astype(BF16)
            att = jnp.where(tril, _dot3(qt, kt, NT), 0.0)
            o = _dg(qd.astype(BF16), st0.astype(BF16), NT) + _dg(att.astype(BF16), vb, NN)
            st_ref[j] = st0 * jnp.exp(bl) + _dg(vb, kl.astype(BF16), TN)
            o_ref[:, ln] = o
            r = lax.rsqrt(jnp.mean(o * o, axis=-1, keepdims=True) + RMS_EPS)
            y_ref[:, ln] = (o * r * gam_ref[...] * (g * _sigmoid(g))).astype(BF16)
        if nr:
            carry.ride(grid, riders, "finish")

    outs = pl.pallas_call(
        body, name="hgrn_fwd", grid=grid,
        in_specs=[_spec((4, CHUNK, wide), lambda h, c: (0, c, h)),
                  _spec((None, 1, wide), lambda h, c: (R_CLB, 0, h)),
                  _spec((None, 1, wide), lambda h, c: (R_CLB + 1, 0, h)),
                  _spec((None, 1, HD), lambda h, c: (R_GAM, 0, 0))] + (carry.in_specs if nr else []),
        out_specs=[_spec((CHUNK, wide), lambda h, c: (c, h)), _spec((CHUNK, wide), lambda h, c: (c, h)),
                   _spec((None, hps, HD, HD), lambda h, c: (c, h, 0, 0))] + (carry.out_specs if nr else []),
        out_shape=[jax.ShapeDtypeStruct((t, D), F32), jax.ShapeDtypeStruct((t, D), BF16),
                   jax.ShapeDtypeStruct((nc, nh, HD, HD), F32)] + (carry.out_shape if nr else []),
        scratch_shapes=[pltpu.VMEM((hps, HD, HD), F32)] + (carry.scratch if nr else []),
        compiler_params=_params(("arbitrary", "arbitrary")),
    )(proj_c, small, small, small, *(carry.groups if nr else []))
    return outs[0], outs[1], outs[2], (carry.place(outs[3:]) if nr else [])


def _hgrn_bwd(proj_c, small, o, sst, dyc, rider=None):
    t = proj_c.shape[1]
    nc = t // CHUNK
    nh = D // HD

    hps = HGRN_HPS
    wide = hps * HD
    ng = nh // hps
    grid = (ng, nc)
    nr = rider.n if rider is not None else 0

    def body(*refs):
        p_ref, c0_ref, c1_ref, gam_ref, o_ref, sst_ref, dy_ref = refs[:7]
        dp_ref, dclb_ref, dgam_ref = refs[7 + nr:10 + nr]
        dst_ref, dlb_acc, dgam_acc = refs[10 + 2 * nr:13 + 2 * nr]
        riders = (refs[7:7 + nr], refs[10 + nr:10 + 2 * nr], refs[-2], refs[-1]) if nr else None
        if nr:
            rider.ride(grid, riders, "start")
        group = pl.program_id(0)
        step = pl.program_id(1)

        @pl.when(step == 0)
        def _():
            dst_ref[...] = jnp.zeros_like(dst_ref)
            dlb_acc[...] = jnp.zeros_like(dlb_acc)

        @pl.when((step == 0) & (group == 0))
        def _():
            dgam_acc[...] = jnp.zeros_like(dgam_acc)

        gam = gam_ref[...]
        qr_all = p_ref[0]
        lb_all, sx_all, f_all, k_all, sq_all, q_all = _hgrn_gates(qr_all, p_ref[1], c0_ref[...], c1_ref[...])
        tril, triu = _chunk_masks()
        b_all = _chunk_sums(tril, jnp.log(f_all))
        dq_parts, dk_parts, db_parts, dgam_parts = [], [], [], []
        for j in range(hps):
            ln = slice(j * HD, (j + 1) * HD)
            st0 = sst_ref[j]
            dst1 = dst_ref[j]
            v, g = p_ref[2, :, ln], p_ref[3, :, ln]
            q, k, b = q_all[:, ln], k_all[:, ln], b_all[:, ln]
            bm = b[CHUNK // 2 - 1:CHUNK // 2]
            bl = b[CHUNK - 1:CHUNK]
            eb = jnp.exp(b)
            e_qt = jnp.exp(b - bm)
            e_kt = jnp.exp(bm - b)
            e_kl = jnp.exp(bl - b)
            e_bl = jnp.exp(bl)
            qd, qt, kt, kl = q * eb, q * e_qt, k * e_kt, k * e_kl
            ov = o_ref[:, ln]
            r = lax.rsqrt(jnp.mean(ov * ov, axis=-1, keepdims=True) + RMS_EPS)
            oh = ov * r
            sg = _sigmoid(g)
            dy = dy_ref[:, ln]
            dp_ref[3, :, ln] = (dy * oh * gam * (sg * (1.0 + g * (1.0 - sg)))).astype(BF16)
            dyv = dy * (g * sg)
            gdy = dyv * gam
            do = (gdy - oh * jnp.mean(gdy * oh, axis=-1, keepdims=True)) * r
            dob, vb = do.astype(BF16), v.astype(BF16)
            st0b, dst1b = st0.astype(BF16), dst1.astype(BF16)
            st1 = st0 * e_bl + _dg(vb, kl.astype(BF16), TN)
            att = jnp.where(tril, _dot3(qt, kt, NT), 0.0)
            datt = jnp.where(tril, _dg(dob, vb, NT), 0.0)
            dv = _dg(att.astype(BF16), dob, TN) + _dg(kl.astype(BF16), dst1b, NT)
            dq = _dot3(datt, kt, NN) * e_qt + _dg(dob, st0b, NN) * eb
            dk = _dot3(datt, qt, TN) * e_kt + _dg(vb, dst1b, NN) * e_kl
            db = q * dq - k * dk
            last = lax.broadcasted_iota(jnp.int32, (CHUNK, 1), 0) == CHUNK - 1
            db = db + jnp.where(last, jnp.sum(dst1 * st1, axis=0, keepdims=True), 0.0)
            dst_ref[j] = dst1 * e_bl + _dg(dob, qd.astype(BF16), TN)
            dp_ref[2, :, ln] = dv.astype(BF16)
            dq_parts.append(dq)
            dk_parts.append(dk)
            db_parts.append(db)
            dgam_parts.append(jnp.sum(dyv * oh, axis=0, keepdims=True))

        dq_all, dk_all = jnp.concatenate(dq_parts, axis=1), jnp.concatenate(dk_parts, axis=1)
        dlf = _chunk_sums(triu, jnp.concatenate(db_parts, axis=1))
        dp_ref[0] = (dq_all * (sq_all * (1.0 + qr_all * (1.0 - sq_all)))).astype(BF16)
        tmp = dlf / f_all - dk_all
        dp_ref[1] = (tmp * (1.0 - lb_all) * sx_all * (1.0 - sx_all)).astype(BF16)
        dlb_acc[...] += jnp.sum((1.0 - sx_all) * tmp, axis=0, keepdims=True)
        dgam_acc[...] += functools.reduce(lambda a, b: a + b, dgam_parts)

        @pl.when(step == nc - 1)
        def _():
            d1 = dlb_acc[...] * lb_all * (1.0 - lb_all)
            dclb_ref[...] = jnp.where(lax.broadcasted_iota(jnp.int32, (2, wide), 0) == 0, -d1, d1)

        @pl.when((step == nc - 1) & (group == ng - 1))
        def _():
            dgam_ref[...] = dgam_acc[...]

        if nr:
            rider.ride(grid, riders, "finish")

    rev = lambda h, s: (nc - 1 - s, h)
    outs = pl.pallas_call(
        body, name="hgrn_bwd", grid=grid,
        in_specs=[_spec((4, CHUNK, wide), lambda h, s: (0, nc - 1 - s, h)),
                  _spec((None, 1, wide), lambda h, s: (R_CLB, 0, h)),
                  _spec((None, 1, wide), lambda h, s: (R_CLB + 1, 0, h)),
                  _spec((None, 1, HD), lambda h, s: (R_GAM, 0, 0)),
                  _spec((CHUNK, wide), rev),
                  _spec((None, hps, HD, HD), lambda h, s: (nc - 1 - s, h, 0, 0)),
                  _spec((CHUNK, wide), rev)] + (rider.in_specs if nr else []),
        out_specs=[_spec((4, CHUNK, wide), lambda h, s: (0, nc - 1 - s, h)),
                   _spec((2, wide), lambda h, s: (0, h)),
                   _spec((1, HD), lambda h, s: (0, 0))] + (rider.out_specs if nr else []),
        out_shape=[jax.ShapeDtypeStruct((4, t, D), BF16), jax.ShapeDtypeStruct((2, D), F32),
                   jax.ShapeDtypeStruct((1, HD), F32)] + (rider.out_shape if nr else []),
        scratch_shapes=[pltpu.VMEM((hps, HD, HD), F32), pltpu.VMEM((1, wide), F32), pltpu.VMEM((1, HD), F32)]
        + (rider.scratch if nr else []),
        compiler_params=_params(("arbitrary", "arbitrary")),
    )(proj_c, small, small, small, o, sst, dyc, *(rider.groups if nr else []))
    return outs[0], outs[1], outs[2], (rider.place(outs[3:]) if nr else [])


def _adamw(name, w, g, m, v):
    rows, cols = w.shape
    br = rows
    for cand in (512, 352, 256):
        if rows % cand == 0:
            br = cand
            break
    c1 = 1.0 - ADAM_B1 ** ADAM_STEP
    c2 = 1.0 - ADAM_B2 ** ADAM_STEP

    def body(w_ref, g_ref, m_ref, v_ref, d_ref, mo_ref, vo_ref):
        gv = g_ref[...]
        mn = ADAM_B1 * m_ref[...] + (1.0 - ADAM_B1) * gv
        vn = ADAM_B2 * v_ref[...] + (1.0 - ADAM_B2) * (gv * gv)
        mo_ref[...] = mn
        vo_ref[...] = vn
        d_ref[...] = -ADAM_LR * ((mn / c1) / (jnp.sqrt(vn / c2) + ADAM_EPS) + ADAM_WD * w_ref[...])

    blk = _spec((br, cols), lambda i: (i, 0))
    shape = jax.ShapeDtypeStruct((rows, cols), F32)
    return pl.pallas_call(
        body, name=name, grid=(rows // br,), in_specs=[blk] * 4, out_specs=[blk] * 3, out_shape=[shape] * 3,
        compiler_params=_params(("parallel",)),
    )(w, g, m, v)


def _place():
    x, y, c = lax.axis_index("x"), lax.axis_index("y"), lax.axis_index("c")
    chips = [(1 - x, y), (x, 1 - y), (1 - x, 1 - y)]
    return x, y, c, chips


class _Rider:
    n = 0

    def ride(self, grid, refs, when):
        if not self.n:
            return
        ids = [pl.program_id(a) for a in range(len(grid))]
        edge = [i == (0 if when == "start" else g - 1) for i, g in zip(ids, grid)]
        cond = functools.reduce(jnp.logical_and, edge)

        @pl.when(cond)
        def _():
            (self.start if when == "start" else self.finish)(*refs)


class _Gather(_Rider):
    def __init__(self, groups):
        self.groups = list(groups)
        self.n = len(self.groups)
        self.in_specs = [ANY] * self.n
        self.out_specs = [ANY] * self.n
        self.out_shape = [jax.ShapeDtypeStruct((NSH,) + g.shape, g.dtype) for g in self.groups]
        self.scratch = [pltpu.SemaphoreType.DMA((6 * self.n,)), pltpu.SemaphoreType.DMA((6 * self.n,))] if self.n else []

    def _copies(self, ins, outs, send, recv):
        x, y, c, chips = _place()
        sibling = (x, y, 1 - c)

        def half(gi, chip, hc):
            return outs[gi].at[2 * chip[0] + chip[1], hc]

        def copy(gi, k, src, dst, to):
            return pltpu.make_async_remote_copy(src_ref=src, dst_ref=dst, send_sem=send.at[6 * gi + k],
                                                recv_sem=recv.at[6 * gi + k], device_id=to, device_id_type=MESH)

        pairs = [(gi, j, chip) for gi in range(self.n) for j, chip in enumerate(chips)]
        first = [copy(gi, j, ins[gi].at[c], half(gi, (x, y), c), (*chip, c)) for gi, j, chip in pairs]
        landed = [copy(gi, j, half(gi, chip, c), half(gi, chip, c), sibling) for gi, j, chip in pairs]
        relay = [copy(gi, 3 + j, half(gi, chip, c), half(gi, chip, c), sibling) for gi, j, chip in pairs]
        relayed = [copy(gi, 3 + j, half(gi, chip, 1 - c), half(gi, chip, 1 - c), sibling) for gi, j, chip in pairs]
        return first, landed, relay, relayed

    def start(self, ins, outs, send, recv):
        for cp in self._copies(ins, outs, send, recv)[0]:
            cp.start()

    def finish(self, ins, outs, send, recv):
        first, landed, relay, relayed = self._copies(ins, outs, send, recv)
        for arrived, onward in zip(landed, relay):
            arrived.wait_recv()
            onward.start()
        for cp in relayed:
            cp.wait_recv()
        for cp in first + relay:
            cp.wait_send()

    def place(self, outs):
        me = 2 * lax.axis_index("x") + lax.axis_index("y")
        return [lax.dynamic_update_index_in_dim(o, g, me, 0) for o, g in zip(outs, self.groups)]


def _alone(name, rider):
    n = rider.n

    def body(*refs):
        parts = (refs[:n], refs[n:2 * n], refs[2 * n], refs[2 * n + 1])
        rider.start(*parts)
        rider.finish(*parts)

    outs = pl.pallas_call(
        body, name=name, in_specs=rider.in_specs, out_specs=rider.out_specs, out_shape=rider.out_shape,
        scratch_shapes=rider.scratch, compiler_params=pltpu.CompilerParams(has_side_effects=True),
    )(*rider.groups)
    return rider.place(outs)


class _Swap(_Rider):
    def __init__(self, slots):
        self.slots = list(slots)
        self.groups = [buf for buf, _, _ in self.slots]
        self.n = len(self.slots)
        self.in_specs = [ANY] * self.n
        self.out_specs = [ANY] * self.n
        self.out_shape = [jax.ShapeDtypeStruct((NSH, rows // 2, buf.shape[2]), buf.dtype) for buf, _, rows in self.slots]
        self.scratch = [pltpu.SemaphoreType.DMA((self.n,)), pltpu.SemaphoreType.DMA((self.n,))]

    def _copies(self, ins, outs, send, recv):
        x, y, c, _ = _place()
        cps = []
        for i, (_, row0, rows) in enumerate(self.slots):
            half = rows // 2
            src = ins[i].at[:, pl.ds(pl.multiple_of(row0 + (1 - c) * half, 16), half)]
            cps.append(pltpu.make_async_remote_copy(src_ref=src, dst_ref=outs[i], send_sem=send.at[i],
                                                    recv_sem=recv.at[i], device_id=(x, y, 1 - c), device_id_type=MESH))
        return cps

    def start(self, ins, outs, send, recv):
        for cp in self._copies(ins, outs, send, recv):
            cp.start()

    def finish(self, ins, outs, send, recv):
        for cp in self._copies(ins, outs, send, recv):
            cp.wait()

    def place(self, outs):
        return list(outs)


class _Share(_Rider):
    def __init__(self, arrays):
        self.groups = list(arrays)
        self.n = len(self.groups)
        self.in_specs = [ANY] * self.n
        self.out_specs = [ANY] * self.n
        self.out_shape = [jax.ShapeDtypeStruct((2,) + g.shape, g.dtype) for g in self.groups]
        self.scratch = [pltpu.SemaphoreType.DMA((self.n,)), pltpu.SemaphoreType.DMA((self.n,))]

    def _copies(self, ins, outs, send, recv, half):
        x, y, c, _ = _place()
        return [pltpu.make_async_remote_copy(src_ref=ins[gi], dst_ref=outs[gi].at[c if half == "mine" else 1 - c],
                                             send_sem=send.at[gi], recv_sem=recv.at[gi], device_id=(x, y, 1 - c),
                                             device_id_type=MESH) for gi in range(self.n)]

    def start(self, ins, outs, send, recv):
        for cp in self._copies(ins, outs, send, recv, "mine"):
            cp.start()

    def finish(self, ins, outs, send, recv):
        for cp in self._copies(ins, outs, send, recv, "theirs"):
            cp.wait_recv()
        for cp in self._copies(ins, outs, send, recv, "mine"):
            cp.wait_send()

    def place(self, outs):
        c = lax.axis_index("c")
        return [lax.dynamic_update_index_in_dim(o, g, c, 0) for o, g in zip(outs, self.groups)]


class _Send(_Rider):
    def __init__(self, arrays):
        self.groups = list(arrays)
        self.n = len(self.groups)
        self.in_specs = [ANY] * self.n
        self.out_specs = [ANY] * self.n
        self.out_shape = [jax.ShapeDtypeStruct((3,) + g.shape[1:], g.dtype) for g in self.groups]
        self.scratch = [pltpu.SemaphoreType.DMA((3 * self.n,)), pltpu.SemaphoreType.DMA((3 * self.n,))]

    def _copies(self, ins, outs, send, recv):
        x, y, c, chips = _place()
        return [pltpu.make_async_remote_copy(src_ref=ins[gi].at[2 * chip[0] + chip[1]], dst_ref=outs[gi].at[j],
                                             send_sem=send.at[3 * gi + j], recv_sem=recv.at[3 * gi + j],
                                             device_id=(*chip, c), device_id_type=MESH)
                for gi in range(self.n) for j, chip in enumerate(chips)]

    def start(self, ins, outs, send, recv):
        for cp in self._copies(ins, outs, send, recv):
            cp.start()

    def finish(self, ins, outs, send, recv):
        for cp in self._copies(ins, outs, send, recv):
            cp.wait()

    def place(self, outs):
        return list(outs)


def _pair_sum(name, slots, got, c_idx):
    n = len(slots)
    in_specs, out_specs, out_shape, operands = [], [], [], []
    for (buf, row0, rows), g in zip(slots, got):
        hb, cols = rows // 4, buf.shape[2]
        in_specs += [pl.BlockSpec((None, hb, cols), lambda q, i, cr, r=row0 // hb: (q, r + 2 * cr[0] + i, 0)),
                     pl.BlockSpec((None, hb, cols), lambda q, i, cr: (q, i, 0))]
        out_specs.append(pl.BlockSpec((None, hb, cols), lambda q, i, cr: (q, i, 0)))
        out_shape.append(jax.ShapeDtypeStruct(g.shape, BF16))
        operands += [buf, g]

    def body(c_ref, *refs):
        for i in range(n):
            refs[2 * n + i][...] = (refs[2 * i][...].astype(F32) + refs[2 * i + 1][...].astype(F32)).astype(BF16)

    return pl.pallas_call(
        body, name=name,
        grid_spec=pltpu.PrefetchScalarGridSpec(num_scalar_prefetch=1, grid=(NSH, 2), in_specs=in_specs, out_specs=out_specs),
        out_shape=out_shape, compiler_params=_params(("parallel", "parallel")),
    )(c_idx, *operands)


def _owner_sum(name, pairs, got, p_idx):
    n = len(pairs)
    in_specs, out_specs, out_shape, operands = [], [], [], []
    for own, g in zip(pairs, got):
        _, rows, cols = own.shape
        hb = rows // 2
        in_specs += [pl.BlockSpec((None, hb, cols), lambda i, pr: (pr[0], i, 0)),
                     pl.BlockSpec((3, hb, cols), lambda i, pr: (0, i, 0))]
        out_specs.append(pl.BlockSpec((hb, cols), lambda i, pr: (i, 0)))
        out_shape.append(jax.ShapeDtypeStruct((rows, cols), F32))
        operands += [own, g]

    def body(p_ref, *refs):
        for i in range(n):
            a_ref, b_ref = refs[2 * i], refs[2 * i + 1]
            refs[2 * n + i][...] = ((a_ref[...].astype(F32) + b_ref[0].astype(F32)) + b_ref[1].astype(F32)) + b_ref[2].astype(F32)

    return pl.pallas_call(
        body, name=name,
        grid_spec=pltpu.PrefetchScalarGridSpec(num_scalar_prefetch=1, grid=(2,), in_specs=in_specs, out_specs=out_specs),
        out_shape=out_shape, compiler_params=_params(("parallel",)),
    )(p_idx, *operands)


def _sum_small(slab):
    def body(in_ref, out_ref, all_ref, send, recv):
        x, y, c, _ = _place()
        me = 4 * x + 2 * y + c
        all_ref[me] = in_ref[...]
        cps = []
        for k in range(1, 8):
            peer = (x ^ (k >> 2), y ^ ((k >> 1) & 1), c ^ (k & 1))
            cps.append(pltpu.make_async_remote_copy(src_ref=in_ref, dst_ref=all_ref.at[me], send_sem=send.at[k - 1],
                                                    recv_sem=recv.at[k - 1], device_id=peer, device_id_type=MESH))
        for cp in cps:
            cp.start()
        for cp in cps:
            cp.wait()
        total = all_ref[0]
        for d in range(1, 8):
            total = total + all_ref[d]
        out_ref[...] = total

    return pl.pallas_call(
        body, name="sum_small",
        in_specs=[pl.BlockSpec(memory_space=pltpu.VMEM)], out_specs=pl.BlockSpec(memory_space=pltpu.VMEM),
        out_shape=jax.ShapeDtypeStruct(slab.shape, F32),
        scratch_shapes=[pltpu.VMEM((8,) + slab.shape, F32), pltpu.SemaphoreType.DMA((7,)), pltpu.SemaphoreType.DMA((7,))],
        compiler_params=pltpu.CompilerParams(has_side_effects=True),
    )(slab)


FFNS = ("pre0", "post0", "pre1", "post1")
W_SHAPES = dict({f + "_gu": (NSH, 2, FS, D) for f in FFNS}, **{f + "_d": (NSH, FS, D) for f in FFNS},
                ab_in=(NSH, D, 768), ab_out=(NSH, 256, D), conv=(NSH, 2, 8, LANE), c_in=(NSH, D, D), c_out=(NSH, 256, D))
CARRIED = dict(pre0_up=("pre0_d",), pre0_down=("ab_in", "ab_out", "conv"),
               sb_fwd=("post0_gu", "post0_d"), post0_up=("pre1_gu",), post0_down=("pre1_d",),
               pre1_up=("c_in", "c_out"), hgrn_fwd=("post1_gu", "post1_d"))


FFN_SLOTS = dict(pre0=(0, 2, 0), pre1=(1, 3, 1), post0=(4, 6, 2), post1=(5, 7, 3))
GRAD_SLOTS = dict(ab_out=("b", B_ABOUT, 256), c_in=("b", B_CIN, D), c_out=("b", B_COUT, 256), ab_in=("c", 0, D))
for _f, (_g, _u, _d) in FFN_SLOTS.items():
    GRAD_SLOTS.update({_f + "_g": ("a", _g * FS, FS), _f + "_u": ("a", _u * FS, FS), _f + "_d": ("b", _d * FS, FS)})
REDUCE_STAGES = dict(x=("post1_g", "post1_u", "post1_d", "c_out"),
                     y=("c_in", "pre1_g", "pre1_u", "pre1_d", "post0_g", "post0_u", "post0_d", "ab_out"),
                     z1=("ab_in",), z2=("pre0_g", "pre0_u", "pre0_d"))


def _local_step(x, target, weights, small, shards=None, place=None):
    t = x.shape[0]
    tm = min(MM_TILE, t)
    tw = min(WGRAD_TILE, t)
    nt = t // tm
    tok_si = _spec((tm, D), lambda s, i: (i, 0))
    w = dict(weights)
    reduced = {}

    def stage_slots(stage, buffers):
        return [(buffers[GRAD_SLOTS[n][0]],) + GRAD_SLOTS[n][1:] for n in REDUCE_STAGES[stage]]

    def swap_rider(stage, buffers):
        return _Swap(stage_slots(stage, buffers)) if place is not None else None

    def reduce_start(stage, buffers, swapped=None):
        if place is None:
            return [], None
        slots = stage_slots(stage, buffers)
        if swapped is None:
            swapped = _alone("grad_swap_" + stage, _Swap(slots))
        pairs = _pair_sum("grad_pair_sum_" + stage, slots, swapped, place[0])
        return pairs, _Send(pairs)

    def reduce_end(stage, pairs, landed):
        if place is None:
            return None
        mine = _owner_sum("grad_owner_sum_" + stage, pairs, landed, place[1])
        return _Share(mine)

    def shared(stage, landed):
        if place is not None:
            reduced.update(zip(REDUCE_STAGES[stage], landed))

    def carried(kernel_name):
        names = [n for n in CARRIED[kernel_name] if n not in w]
        return names, (_Gather([shards[n] for n in names]) if names else None)

    def land(names, arrays):
        for n, a in zip(names, arrays):
            w[n] = a.reshape(W_SHAPES[n])

    def ffn_forward(tag, h, norm_row):
        hn = _norm_fwd(tag + "_norm", h, small, norm_row)
        names, gather = carried(tag + "_up") if tag + "_up" in CARRIED else ([], None)
        s_up, s_gate, a, got = _ffn_up(tag + "_up", hn, w[tag + "_gu"], gather)
        land(names, got)
        names, gather = carried(tag + "_down") if tag + "_down" in CARRIED else ([], None)
        out = _ffn_down(tag + "_down", a, w[tag + "_d"], h, gather)
        if gather is not None:
            out, got = out
            land(names, got)
        return out, (h, hn, s_up, s_gate, a)

    def out_proj(name, y, w_out, h):
        return _mm(name, [(y, _spec((tm, 256), lambda i, k: (i, k)), w_out, _spec((None, 256, D), lambda i, k: (k, 0, 0)))],
                   grid=(nt, NSH), o_shape=(t, D), o_dtype=F32, o_spec=_spec((tm, D), lambda i, k: (i, 0)),
                   dims=NN, kaxis=1, nk=NSH, acc_shape=(tm, D), res=(h, _spec((tm, D), lambda i, k: (i, 0))))

    def out_proj_bwd(tag, dhb, y, w_out, blk, grad_b, make_rider=None):
        grad_b = _wgrad(tag + "_dwout", y, _spec((tw, 256), lambda s, k: (k, s)), dhb, _spec((tw, D), lambda s, k: (k, 0)),
                        256, D, t, tw, grad_b, blk)
        rider = make_rider(grad_b) if make_rider is not None else None
        dy = _mm(tag + "_dy", [(dhb, tok_si, w_out, _spec((None, 256, D), lambda s, i: (s, 0, 0)))],
                 grid=(NSH, nt), o_shape=(t, D), o_dtype=F32, o_spec=_spec((tm, 256), lambda s, i: (i, s)),
                 dims=NT, kaxis=1, nk=1, carry=rider)
        dy, landed = dy if rider is not None else (dy, None)
        return dy, grad_b, landed

    h0 = x
    h1, pre0 = ffn_forward("pre0", h0, R_PRE)
    hn_ab = _norm_fwd("mix0_norm", h1, small, R_MIX)
    proj_a = _mm("ab_proj_a", [(hn_ab, tok_si, w["ab_in"], _spec((None, D, 768), lambda s, i: (s, 0, 0)))],
                 grid=(2, nt), o_shape=(t, 1536), o_dtype=F32, o_spec=_spec((tm, 768), lambda s, i: (i, s)),
                 dims=NN, kaxis=1, nk=1)
    proj_b = _mm("ab_proj_b", [(hn_ab, tok_si, w["ab_in"], _spec((None, D, 768), lambda s, i: (s + 2, 0, 0)))],
                 grid=(2, nt), o_shape=(t, 1536), o_dtype=BF16, o_spec=_spec((tm, 768), lambda s, i: (i, s)),
                 dims=NN, kaxis=1, nk=1)
    y_a = _conv_fwd(proj_a, w["conv"])
    names, gather = carried("sb_fwd")
    y_b, ltot, nblk, got = _sb_fwd(proj_b, gather)
    land(names, got)
    y_ab = jnp.concatenate([y_a, y_b], axis=1)
    h2 = out_proj("ab_out", y_ab, w["ab_out"], h1)
    h3, post0 = ffn_forward("post0", h2, R_POST)
    h4, pre1 = ffn_forward("pre1", h3, R_PRE + 1)
    hn_c = _norm_fwd("mix1_norm", h4, small, R_MIX + 1)
    proj_c = _mm("c_proj", [(hn_c, tok_si, w["c_in"], _spec((None, D, D), lambda s, i: (s, 0, 0)))],
                 grid=(NSH, nt), o_shape=(NSH, t, D), o_dtype=F32, o_spec=_spec((None, tm, D), lambda s, i: (s, i, 0)),
                 dims=NN, kaxis=1, nk=1)
    names, gather = carried("hgrn_fwd")
    o_c, y_c, sst, got = _hgrn_fwd(proj_c, small, gather)
    land(names, got)
    h5 = out_proj("c_out", y_c, w["c_out"], h4)
    h6, post1 = ffn_forward("post1", h5, R_POST + 1)
    dh, dhb, d_fin, loss = _final_loss(h6, small, target)

    dh, dhb, grad_a, grad_b, dn_post1, _, _ = _ffn_backward("post1", dh, dhb, *post1, w["post1_gu"], w["post1_d"],
                                                            *FFN_SLOTS["post1"], small, R_POST + 1, 8 * FS, B_ROWS)
    dy_c, grad_b, swapped = out_proj_bwd("c", dhb, y_c, w["c_out"], B_COUT // 256, grad_b,
                                         lambda buf_b: swap_rider("x", dict(a=grad_a, b=buf_b)))
    pairs, rider = reduce_start("x", dict(a=grad_a, b=grad_b), swapped)
    dproj_c, d_clb, d_gam, landed = _hgrn_bwd(proj_c, small, o_c, sst, dy_c, rider)
    share = reduce_end("x", pairs, landed)
    grad_b = _wgrad("c_dwin", hn_c, _spec((tw, D), lambda s, k: (k, 0)), dproj_c, _spec((None, tw, D), lambda s, k: (s, k, 0)),
                    D, D, t, tw, grad_b, B_CIN // D, carry=share)
    if share is not None:
        grad_b, landed = grad_b
        shared("x", landed)
    dh, dhb, dn_mix1, _ = _dhn_norm("c_dhn", [(dproj_c, _spec((None, tm, D), lambda i, k: (k, i, 0)),
                                               w["c_in"], _spec((None, D, D), lambda i, k: (k, 0, 0)))],
                                    NT, h4, small, R_MIX + 1, dh)
    dh, dhb, grad_a, grad_b, dn_pre1, _, _ = _ffn_backward("pre1", dh, dhb, *pre1, w["pre1_gu"], w["pre1_d"],
                                                           *FFN_SLOTS["pre1"], small, R_PRE + 1, grad_a, grad_b)
    dh, dhb, grad_a, grad_b, dn_post0, _, _ = _ffn_backward("post0", dh, dhb, *post0, w["post0_gu"], w["post0_d"],
                                                            *FFN_SLOTS["post0"], small, R_POST, grad_a, grad_b)
    dy_ab, grad_b, _ = out_proj_bwd("ab", dhb, y_ab, w["ab_out"], B_ABOUT // 256, grad_b)
    dab, dac, dax, d_conv, swapped = _conv_bwd(proj_a, w["conv"], dy_ab, swap_rider("y", dict(a=grad_a, b=grad_b)))
    pairs, rider = reduce_start("y", dict(a=grad_a, b=grad_b), swapped)
    dq, dk, dv, landed = _sb_bwd(proj_b, dy_ab, ltot, nblk, rider)
    share = reduce_end("y", pairs, landed)
    dproj_ab = jnp.concatenate([dab, dac, dax, dq, dk, dv], axis=1)
    grad_c = _wgrad("ab_dwin", hn_ab, _spec((tw, D), lambda s, k: (k, 0)), dproj_ab, _spec((tw, 768), lambda s, k: (k, s)),
                    D, 768, t, tw, D, 0, carry=share)
    if share is not None:
        grad_c, landed = grad_c
        shared("y", landed)
    dh, dhb, dn_mix0, _ = _dhn_norm("ab_dhn", [(dproj_ab, _spec((tm, 768), lambda i, k: (i, k)),
                                                w["ab_in"], _spec((None, D, 768), lambda i, k: (k, 0, 0)))],
                                    NT, h1, small, R_MIX, dh)
    pairs, rider = reduce_start("z1", dict(c=grad_c))
    last = {}

    def own_gradients(buf_a, buf_b):
        last["z2"] = reduce_start("z2", dict(a=buf_a, b=buf_b))
        return last["z2"][1]

    dh, dhb, grad_a, grad_b, dn_pre0, landed, landed_own = _ffn_backward(
        "pre0", dh, dhb, *pre0, w["pre0_gu"], w["pre0_d"], *FFN_SLOTS["pre0"], small, R_PRE, grad_a, grad_b,
        rider, own_gradients if place is not None else None)
    if place is not None:
        mine = (_owner_sum("grad_owner_sum_z1", pairs, landed, place[1])
                + _owner_sum("grad_owner_sum_z2", last["z2"][0], landed_own, place[1]))
        reduced.update(zip(REDUCE_STAGES["z1"] + REDUCE_STAGES["z2"], _alone("grad_share_z", _Share(mine))))
    zero = jnp.zeros((1, D), F32)
    conv_rows = jnp.pad(jnp.transpose(d_conv[:, :3, :], (1, 0, 2)).reshape(3, 512), ((0, 0), (0, D - 512)))
    small_grad = jnp.concatenate([
        dn_pre0, dn_pre1, dn_mix0, dn_mix1, dn_post0, dn_post1, d_clb, d_fin,
        jnp.pad(d_gam, ((0, 0), (0, D - HD))), conv_rows,
        jnp.pad(loss, ((0, 0), (0, D - 1))), zero, zero], axis=0)
    return dh, grad_a, grad_b, grad_c, small_grad, reduced


def _small_slab(rows):
    parts = [jnp.pad(r.astype(F32), ((0, 0), (0, D - r.shape[1]))) for r in rows]
    slab = jnp.concatenate(parts, axis=0)
    return jnp.pad(slab, ((0, SMALL_ROWS - slab.shape[0]), (0, 0)))


def kernel(x, ffn_pre_norm, ffn_pre_w_gate, ffn_pre_w_up, ffn_pre_w_down, mix_norm, ffn_post_norm, ffn_post_w_gate, ffn_post_w_up, ffn_post_w_down, ab_w_in, ab_conv_w, ab_w_out, c_w_in, c_lower_bounds, c_out_norm, c_w_out, final_norm, loss_target, m_ffn_pre_norm, m_ffn_pre_w_gate, m_ffn_pre_w_up, m_ffn_pre_w_down, m_mix_norm, m_ffn_post_norm, m_ffn_post_w_gate, m_ffn_post_w_up, m_ffn_post_w_down, m_ab_w_in, m_ab_conv_w, m_ab_w_out, m_c_w_in, m_c_lower_bounds, m_c_out_norm, m_c_w_out, m_final_norm, v_ffn_pre_norm, v_ffn_pre_w_gate, v_ffn_pre_w_up, v_ffn_pre_w_down, v_mix_norm, v_ffn_post_norm, v_ffn_post_w_gate, v_ffn_post_w_up, v_ffn_post_w_down, v_ab_w_in, v_ab_conv_w, v_ab_w_out, v_c_w_in, v_c_lower_bounds, v_c_out_norm, v_c_w_out, v_final_norm):
    t = x.shape[1]
    xi, yi, ci = lax.axis_index("x"), lax.axis_index("y"), lax.axis_index("c")
    p_idx = (2 * xi + yi).astype(jnp.int32).reshape(1)
    c_idx = ci.astype(jnp.int32).reshape(1)

    def halves(m):
        return m.astype(BF16).reshape(2, m.shape[0] // 2, m.shape[1])

    transposed = ("ffn_pre_w_gate", "ffn_pre_w_up", "ffn_post_w_gate", "ffn_post_w_up")

    def flip(a):
        return jnp.swapaxes(a, 1, 2)

    shards = {}
    for name, (w_gate, w_up, w_down, layer) in dict(
            pre0=(ffn_pre_w_gate, ffn_pre_w_up, ffn_pre_w_down, 0), post0=(ffn_post_w_gate, ffn_post_w_up, ffn_post_w_down, 0),
            pre1=(ffn_pre_w_gate, ffn_pre_w_up, ffn_pre_w_down, 1), post1=(ffn_post_w_gate, ffn_post_w_up, ffn_post_w_down, 1)).items():
        shards[name + "_gu"] = jnp.stack([flip(w_gate)[layer], flip(w_up)[layer]]).astype(BF16)
        shards[name + "_d"] = halves(w_down[layer])
    conv_pad = jnp.pad(ab_conv_w[0], ((0, 5), (0, 0)))
    shards.update(ab_in=halves(ab_w_in[0]), ab_out=halves(ab_w_out[0]), c_in=halves(c_w_in[0]), c_out=halves(c_w_out[0]),
                  conv=jnp.stack([conv_pad, jnp.zeros_like(conv_pad)]))
    first = _alone("gather_weights", _Gather([shards["pre0_gu"]]))[0].reshape(W_SHAPES["pre0_gu"])

    small = _small_slab([ffn_pre_norm, mix_norm, ffn_post_norm, c_lower_bounds, final_norm.reshape(1, D), c_out_norm])
    small = small.reshape(SMALL_ROWS, 1, D)

    grad_x, _, _, _, small_grad, reduced = _local_step(x[0], loss_target[0], dict(pre0_gu=first), small, shards,
                                                        (c_idx, p_idx))
    whole = {n: g.reshape(2 * g.shape[1], g.shape[2]) for n, g in reduced.items()}
    small_sum = _sum_small(small_grad)

    my_conv = lax.dynamic_slice(small_sum[R_CONV:R_CONV + 3], (0, (2 * xi + yi) * 128), (3, 128))
    grads = {
        "ffn_pre_norm": small_sum[R_PRE:R_PRE + 2], "mix_norm": small_sum[R_MIX:R_MIX + 2],
        "ffn_post_norm": small_sum[R_POST:R_POST + 2], "c_lower_bounds": small_sum[R_CLB:R_CLB + 2],
        "c_out_norm": small_sum[R_GAM:R_GAM + 1, :HD], "final_norm": small_sum[R_FIN],
        "ab_conv_w": my_conv.reshape(1, 3, 128),
        "ab_w_in": whole["ab_in"][None], "ab_w_out": whole["ab_out"][None],
        "c_w_in": whole["c_in"][None], "c_w_out": whole["c_out"][None],
    }
    for kind, key in (("gate", "_g"), ("up", "_u"), ("down", "_d")):
        grads["ffn_pre_w_" + kind] = jnp.stack([whole["pre0" + key], whole["pre1" + key]])
        grads["ffn_post_w_" + kind] = jnp.stack([whole["post0" + key], whole["post1" + key]])
    weights = dict(ffn_pre_norm=ffn_pre_norm, ffn_pre_w_gate=ffn_pre_w_gate, ffn_pre_w_up=ffn_pre_w_up, ffn_pre_w_down=ffn_pre_w_down, mix_norm=mix_norm, ffn_post_norm=ffn_post_norm, ffn_post_w_gate=ffn_post_w_gate, ffn_post_w_up=ffn_post_w_up, ffn_post_w_down=ffn_post_w_down, ab_w_in=ab_w_in, ab_conv_w=ab_conv_w, ab_w_out=ab_w_out, c_w_in=c_w_in, c_lower_bounds=c_lower_bounds, c_out_norm=c_out_norm, c_w_out=c_w_out, final_norm=final_norm)
    m_in = dict(ffn_pre_norm=m_ffn_pre_norm, ffn_pre_w_gate=m_ffn_pre_w_gate, ffn_pre_w_up=m_ffn_pre_w_up, ffn_pre_w_down=m_ffn_pre_w_down, mix_norm=m_mix_norm, ffn_post_norm=m_ffn_post_norm, ffn_post_w_gate=m_ffn_post_w_gate, ffn_post_w_up=m_ffn_post_w_up, ffn_post_w_down=m_ffn_post_w_down, ab_w_in=m_ab_w_in, ab_conv_w=m_ab_conv_w, ab_w_out=m_ab_w_out, c_w_in=m_c_w_in, c_lower_bounds=m_c_lower_bounds, c_out_norm=m_c_out_norm, c_w_out=m_c_w_out, final_norm=m_final_norm)
    v_in = dict(ffn_pre_norm=v_ffn_pre_norm, ffn_pre_w_gate=v_ffn_pre_w_gate, ffn_pre_w_up=v_ffn_pre_w_up, ffn_pre_w_down=v_ffn_pre_w_down, mix_norm=v_mix_norm, ffn_post_norm=v_ffn_post_norm, ffn_post_w_gate=v_ffn_post_w_gate, ffn_post_w_up=v_ffn_post_w_up, ffn_post_w_down=v_ffn_post_w_down, ab_w_in=v_ab_w_in, ab_conv_w=v_ab_conv_w, ab_w_out=v_ab_w_out, c_w_in=v_c_w_in, c_lower_bounds=v_c_lower_bounds, c_out_norm=v_c_out_norm, c_w_out=v_c_w_out, final_norm=v_final_norm)
    names = list(weights)
    big = [n for n in names if weights[n].size >= 65536]
    tiny = [n for n in names if n not in big]

    delta, new_m, new_v = {}, {}, {}
    for n in big:
        turn = flip if n in transposed else (lambda a: a)
        shape = turn(weights[n]).shape
        two_d = (shape[0] * shape[1], shape[2])
        d, m2, v2 = _adamw("adamw_" + n, turn(weights[n]).reshape(two_d), grads[n].reshape(two_d),
                           turn(m_in[n]).reshape(two_d), turn(v_in[n]).reshape(two_d))
        delta[n], new_m[n], new_v[n] = turn(d.reshape(shape)), turn(m2.reshape(shape)), turn(v2.reshape(shape))
        grads[n] = turn(grads[n])

    def tiny_slab(src):
        return _small_slab([src[n].reshape(-1, src[n].shape[-1]) for n in tiny])

    offs, row = {}, 0
    for n in tiny:
        nrows = weights[n].size // weights[n].shape[-1]
        offs[n] = (row, nrows)
        row += nrows
    d, m2, v2 = _adamw("adamw_small", tiny_slab(weights), tiny_slab(grads), tiny_slab(m_in), tiny_slab(v_in))
    for n in tiny:
        r0, nr = offs[n]
        shape = weights[n].shape
        for dst, src in ((delta, d), (new_m, m2), (new_v, v2)):
            dst[n] = src[r0:r0 + nr, :shape[-1]].reshape(shape)

    loss = small_sum[R_LOSS, 0]
    return (loss, grad_x.reshape(1, t, D), *[grads[n] for n in names], *[delta[n] for n in names],
            *[new_m[n] for n in names], *[new_v[n] for n in names])
```

```python
import functools
import math

import jax
import jax.numpy as jnp
from jax import lax
from jax.experimental import pallas as pl
from jax.experimental.pallas import tpu as pltpu

F32 = jnp.float32
BF16 = jnp.bfloat16
MESH = pl.DeviceIdType.MESH
ANY = pl.BlockSpec(memory_space=pl.ANY)

D = 1024
FS = 704
NSH = 4
RMS_EPS = 1e-6
MACARON = 0.5
CHUNK = 64
HD = 128
HGRN_HPS = 8
SBQ = 128
SB_PAIRS_FWD = 4
SB_PAIRS_BWD = 2
SB_DEAD = -105.0
CONV_HALO = 8
LANE = 128
ROW_TILE = 512
MM_TILE = 1024
WGRAD_TILE = 4096
VMEM_LIMIT = 48 * 1024 * 1024
VMEM_LIMIT_BIG = 58 * 1024 * 1024

ADAM_LR, ADAM_B1, ADAM_B2, ADAM_EPS, ADAM_WD, ADAM_STEP = 0.001, 0.9, 0.999, 1e-08, 0.01, 10

NN = ((1,), (0,))
NT = ((1,), (1,))
TN = ((0,), (0,))

SMALL_ROWS = 16
R_PRE, R_MIX, R_POST, R_CLB, R_FIN, R_GAM, R_CONV, R_LOSS = 0, 2, 4, 6, 8, 9, 10, 13

B_DOWN = 0
B_ABOUT = 4 * FS
B_CIN = B_ABOUT + 256
B_COUT = B_CIN + 1024
B_ROWS = B_COUT + 256


def _dg(a, b, dims):
    return lax.dot_general(a, b, (dims, ((), ())), preferred_element_type=F32)


def _split(x):
    hi = x.astype(BF16)
    lo = (x - hi.astype(F32)).astype(BF16)
    return hi, lo


def _dot3(a, b, dims):
    ah, al = _split(a)
    bh, bl = _split(b)
    if dims == TN:
        n = b.shape[1]
        both = _dg(ah, jnp.concatenate([bh, bl], axis=1), dims)
        return both[:, :n] + both[:, n:] + _dg(al, bh, dims)
    m = a.shape[0]
    both = _dg(jnp.concatenate([ah, al], axis=0), bh, dims)
    return both[:m] + both[m:] + _dg(ah, bl, dims)


def _sigmoid(x):
    return 1.0 / (1.0 + jnp.exp(-x))


def _params(sem):
    return pltpu.CompilerParams(dimension_semantics=sem, vmem_limit_bytes=VMEM_LIMIT)


def _spec(shape, imap):
    return pl.BlockSpec(shape, imap)


def _accumulate(acc_ref, pairs, dims):
    part = None
    for a_ref, b_ref in pairs:
        d = _dg(a_ref[...], b_ref[...], dims)
        part = d if part is None else part + d
    acc_ref[...] += part


def _mm(name, pairs, *, grid, o_shape, o_dtype, o_spec, dims, kaxis, nk, acc_shape=None, res=None, scale=None,
        into=None, carry=None):
    npairs = len(pairs)
    operands, specs = [], []
    for a, a_spec, b, b_spec in pairs:
        operands += [a, b]
        specs += [a_spec, b_spec]
    if res is not None:
        operands.append(res[0])
        specs.append(res[1])
    aliases = {}
    if into is not None:
        aliases = {len(operands): 0}
        operands.append(into)
        specs.append(ANY)
    n_own = len(operands)
    nc = carry.n if carry is not None else 0
    if nc:
        operands += carry.groups
        specs += carry.in_specs

    def body(*refs):
        o_ref = refs[n_own + nc]
        riders = (refs[n_own:n_own + nc], refs[n_own + nc + 1:n_own + 2 * nc + 1], refs[-2], refs[-1]) if nc else None
        if nc:
            carry.ride(grid, riders, "start")
        def finish(val):
            if scale is not None:
                val = val * scale
            if res is not None:
                val = val + refs[2 * npairs][...]
            o_ref[...] = val.astype(o_dtype)

        if nk == 1:
            part = None
            for n in range(npairs):
                d = _dg(refs[2 * n][...], refs[2 * n + 1][...], dims)
                part = d if part is None else part + d
            finish(part)
        else:
            acc_ref = refs[n_own + 2 * nc + 1]
            k = pl.program_id(kaxis)

            @pl.when(k == 0)
            def _():
                acc_ref[...] = jnp.zeros_like(acc_ref)

            _accumulate(acc_ref, [(refs[2 * n], refs[2 * n + 1]) for n in range(npairs)], dims)

            @pl.when(k == nk - 1)
            def _():
                finish(acc_ref[...])
        if nc:
            carry.ride(grid, riders, "finish")

    sem = tuple("arbitrary" if (nc or (ax == kaxis and nk > 1)) else "parallel" for ax in range(len(grid)))
    outs = pl.pallas_call(
        body, name=name, grid=grid, in_specs=specs, out_specs=[o_spec] + (carry.out_specs if nc else []),
        out_shape=[jax.ShapeDtypeStruct(o_shape, o_dtype)] + (carry.out_shape if nc else []),
        scratch_shapes=([pltpu.VMEM(acc_shape, F32)] if nk > 1 else []) + (carry.scratch if nc else []),
        input_output_aliases=aliases,
        compiler_params=_params(sem),
    )(*operands)
    return (outs[0], carry.place(outs[1:])) if nc else outs[0]


def _norm_fwd(name, h, gain_slab, row):
    t = h.shape[0]
    tm = min(ROW_TILE, t)

    def body(h_ref, g_ref, o_ref):
        x = h_ref[...]
        r = lax.rsqrt(jnp.mean(x * x, axis=-1, keepdims=True) + RMS_EPS)
        o_ref[...] = (x * r * g_ref[...]).astype(BF16)

    return pl.pallas_call(
        body, name=name, grid=(t // tm,),
        in_specs=[_spec((tm, D), lambda i: (i, 0)), _spec((None, 1, D), lambda i: (row, 0, 0))],
        out_specs=_spec((tm, D), lambda i: (i, 0)),
        out_shape=jax.ShapeDtypeStruct((t, D), BF16),
        compiler_params=_params(("parallel",)),
    )(h, gain_slab)


def _dhn_norm(name, pairs, dims, h, gain_slab, row, dres, rider=None):
    t = h.shape[0]
    tm = min(MM_TILE, t)
    nt = t // tm
    grid = (nt, NSH)
    npairs = len(pairs)
    nr = rider.n if rider is not None else 0
    operands, specs = [], []
    for a, a_spec, b, b_spec in pairs:
        operands += [a, b]
        specs += [a_spec, b_spec]
    row_spec = _spec((tm, D), lambda i, k: (i, 0))
    operands += [h, gain_slab, dres]
    specs += [row_spec, _spec((None, 1, D), lambda i, k: (row, 0, 0)), row_spec]
    n_own = len(operands)

    def body(*refs):
        h_ref, g_ref, dres_ref = refs[2 * npairs:n_own]
        dh_ref, dhb_ref, dg_ref = refs[n_own + nr:n_own + nr + 3]
        acc_ref, gacc_ref = refs[n_own + 2 * nr + 3:n_own + 2 * nr + 5]
        riders = (refs[n_own:n_own + nr], refs[n_own + nr + 3:n_own + 2 * nr + 3], refs[-2], refs[-1]) if nr else None
        if nr:
            rider.ride(grid, riders, "start")
        i, k = pl.program_id(0), pl.program_id(1)

        @pl.when(k == 0)
        def _():
            acc_ref[...] = jnp.zeros_like(acc_ref)

        _accumulate(acc_ref, [(refs[2 * n], refs[2 * n + 1]) for n in range(npairs)], dims)

        @pl.when(k == NSH - 1)
        def _():
            x = h_ref[...]
            r = lax.rsqrt(jnp.mean(x * x, axis=-1, keepdims=True) + RMS_EPS)
            xh = x * r
            dy = acc_ref[...]
            gdy = dy * g_ref[...]
            dh = dres_ref[...] + (gdy - xh * jnp.mean(gdy * xh, axis=-1, keepdims=True)) * r
            dh_ref[...] = dh
            dhb_ref[...] = dh.astype(BF16)
            gpart = jnp.sum((dy * xh).reshape(tm // 8, 8, D), axis=0)

            @pl.when(i == 0)
            def _():
                gacc_ref[...] = gpart

            @pl.when(i > 0)
            def _():
                gacc_ref[...] += gpart

            @pl.when(i == nt - 1)
            def _():
                dg_ref[...] = jnp.sum(gacc_ref[...], axis=0, keepdims=True)

        if nr:
            rider.ride(grid, riders, "finish")

    outs = pl.pallas_call(
        body, name=name, grid=grid, in_specs=specs + (rider.in_specs if nr else []),
        out_specs=[row_spec, row_spec, _spec((1, D), lambda i, k: (0, 0))] + (rider.out_specs if nr else []),
        out_shape=[jax.ShapeDtypeStruct((t, D), F32), jax.ShapeDtypeStruct((t, D), BF16),
                   jax.ShapeDtypeStruct((1, D), F32)] + (rider.out_shape if nr else []),
        scratch_shapes=[pltpu.VMEM((tm, D), F32), pltpu.VMEM((8, D), F32)] + (rider.scratch if nr else []),
        compiler_params=pltpu.CompilerParams(dimension_semantics=("arbitrary", "arbitrary"),
                                             vmem_limit_bytes=VMEM_LIMIT_BIG),
    )(*operands, *(rider.groups if nr else []))
    return outs[0], outs[1], outs[2], (rider.place(outs[3:]) if nr else [])


def _final_loss(h, gain_slab, target):
    t = h.shape[0]
    tm = min(ROW_TILE, t)
    nt = t // tm

    def body(h_ref, g_ref, t_ref, dh_ref, dhb_ref, dg_ref, loss_ref, acc_ref, lacc_ref):
        i = pl.program_id(0)
        x = h_ref[...]
        g = g_ref[...]
        r = lax.rsqrt(jnp.mean(x * x, axis=-1, keepdims=True) + RMS_EPS)
        xh = x * r
        err = xh * g - t_ref[...]
        dy = err * (1.0 / D)
        gdy = dy * g
        dh = (gdy - xh * jnp.mean(gdy * xh, axis=-1, keepdims=True)) * r
        dh_ref[...] = dh
        dhb_ref[...] = dh.astype(BF16)
        part = jnp.sum((dy * xh).reshape(tm // 8, 8, D), axis=0)
        lpart = jnp.sum((err * err).reshape(tm // 8, 8, D), axis=0)

        @pl.when(i == 0)
        def _():
            acc_ref[...] = part
            lacc_ref[...] = lpart

        @pl.when(i > 0)
        def _():
            acc_ref[...] += part
            lacc_ref[...] += lpart

        @pl.when(i == nt - 1)
        def _():
            dg_ref[...] = jnp.sum(acc_ref[...], axis=0, keepdims=True)
            rows = jnp.sum(lacc_ref[...], axis=0, keepdims=True)
            loss_ref[...] = jnp.sum(rows, axis=1, keepdims=True) * (0.5 / D)

    row_spec = _spec((tm, D), lambda i: (i, 0))
    return pl.pallas_call(
        body, name="final_loss", grid=(nt,),
        in_specs=[row_spec, _spec((None, 1, D), lambda i: (R_FIN, 0, 0)), row_spec],
        out_specs=[row_spec, row_spec, _spec((1, D), lambda i: (0, 0)), _spec((1, 1), lambda i: (0, 0))],
        out_shape=[jax.ShapeDtypeStruct((t, D), F32), jax.ShapeDtypeStruct((t, D), BF16),
                   jax.ShapeDtypeStruct((1, D), F32), jax.ShapeDtypeStruct((1, 1), F32)],
        scratch_shapes=[pltpu.VMEM((8, D), F32), pltpu.VMEM((8, D), F32)],
        compiler_params=_params(("arbitrary",)),
    )(h, gain_slab, target)


def _ffn_up(name, hn, wgu, carry=None):
    t = hn.shape[0]
    tm = min(MM_TILE, t)
    grid = (NSH, t // tm)
    nc = carry.n if carry is not None else 0

    def body(*refs):
        x_ref, wg_ref, wu_ref = refs[:3]
        s_up_ref, s_gate_ref, a_ref = refs[3 + nc:6 + nc]
        riders = (refs[3:3 + nc], refs[6 + nc:6 + 2 * nc], refs[-2], refs[-1]) if nc else None
        if nc:
            carry.ride(grid, riders, "start")
        x = x_ref[...]
        g = _dg(x, wg_ref[...], NT)
        u = _dg(x, wu_ref[...], NT)
        sg = _sigmoid(g)
        silu = g * sg
        s_up_ref[...] = (MACARON * silu).astype(BF16)
        s_gate_ref[...] = (MACARON * u * (sg * (1.0 + g * (1.0 - sg)))).astype(BF16)
        a_ref[...] = (silu * u).astype(BF16)
        if nc:
            carry.ride(grid, riders, "finish")

    act = _spec((None, tm, FS), lambda s, i: (s, i, 0))
    shape = jax.ShapeDtypeStruct((NSH, t, FS), BF16)
    outs = pl.pallas_call(
        body, name=name, grid=grid,
        in_specs=[_spec((tm, D), lambda s, i: (i, 0)),
                  _spec((None, None, FS, D), lambda s, i: (s, 0, 0, 0)),
                  _spec((None, None, FS, D), lambda s, i: (s, 1, 0, 0))] + (carry.in_specs if nc else []),
        out_specs=[act, act, act] + (carry.out_specs if nc else []),
        out_shape=[shape, shape, shape] + (carry.out_shape if nc else []),
        scratch_shapes=carry.scratch if nc else [],
        compiler_params=_params(("arbitrary", "arbitrary") if nc else ("parallel", "parallel")),
    )(hn, wgu, wgu, *(carry.groups if nc else []))
    return (outs[0], outs[1], outs[2], carry.place(outs[3:]) if nc else [])


def _ffn_down(name, a, wd, h, carry=None):
    t = h.shape[0]
    tm = min(MM_TILE, t)
    return _mm(name, [(a, _spec((None, tm, FS), lambda i, k: (k, i, 0)),
                       wd, _spec((None, FS, D), lambda i, k: (k, 0, 0)))],
               grid=(t // tm, NSH), o_shape=(t, D), o_dtype=F32, o_spec=_spec((tm, D), lambda i, k: (i, 0)),
               dims=NN, kaxis=1, nk=NSH, acc_shape=(tm, D), res=(h, _spec((tm, D), lambda i, k: (i, 0))),
               scale=MACARON, carry=carry)


def _ffn_bwd_up(name, dhb, wd, s_up, s_gate, rider=None):
    t = dhb.shape[0]
    tm = min(MM_TILE, t)
    grid = (NSH, t // tm)
    nr = rider.n if rider is not None else 0

    def body(*refs):
        dh_ref, wd_ref, s_up_ref, s_gate_ref = refs[:4]
        dg_ref, du_ref = refs[4 + nr:6 + nr]
        riders = (refs[4:4 + nr], refs[6 + nr:6 + 2 * nr], refs[-2], refs[-1]) if nr else None
        if nr:
            rider.ride(grid, riders, "start")
        da = _dg(dh_ref[...], wd_ref[...], NT).astype(BF16)
        du_ref[...] = da * s_up_ref[...]
        dg_ref[...] = da * s_gate_ref[...]
        if nr:
            rider.ride(grid, riders, "finish")

    act = _spec((None, tm, FS), lambda s, i: (s, i, 0))
    shape = jax.ShapeDtypeStruct((NSH, t, FS), BF16)
    outs = pl.pallas_call(
        body, name=name, grid=grid,
        in_specs=[_spec((tm, D), lambda s, i: (i, 0)), _spec((None, FS, D), lambda s, i: (s, 0, 0)), act, act]
        + (rider.in_specs if nr else []),
        out_specs=[act, act] + (rider.out_specs if nr else []),
        out_shape=[shape, shape] + (rider.out_shape if nr else []),
        scratch_shapes=rider.scratch if nr else [],
        compiler_params=_params(("arbitrary", "arbitrary") if nr else ("parallel", "parallel")),
    )(dhb, wd, s_up, s_gate, *(rider.groups if nr else []))
    return outs[0], outs[1], (rider.place(outs[2:]) if nr else [])


def _wgrad(name, a, a_spec, b, b_spec, out_rows, out_cols, t, tt, group, slot, scale=None, carry=None):
    first = isinstance(group, int)
    rows = group if first else group.shape[1]
    return _mm(name, [(a, a_spec, b, b_spec)], grid=(NSH, t // tt),
               o_shape=(NSH, rows, out_cols), o_dtype=BF16,
               o_spec=_spec((None, out_rows, out_cols), lambda s, k: (s, slot, 0)),
               dims=TN, kaxis=1, nk=t // tt, acc_shape=(out_rows, out_cols), scale=scale,
               into=None if first else group, carry=carry)


def _ffn_backward(tag, dh, dhb, h_in, hn, s_up, s_gate, a, wgu, wd, gate_idx, up_idx, down_idx, small, norm_row,
                  grad_a, grad_b, rider_up=None, rider_dhn=None):
    t = dh.shape[0]
    tm = min(MM_TILE, t)
    tt = min(WGRAD_TILE, t)
    dg, du, landed_up = _ffn_bwd_up(tag + "_bwd_up", dhb, wd, s_up, s_gate, rider_up)
    tok = _spec((tt, D), lambda s, k: (k, 0))
    hid = _spec((None, tt, FS), lambda s, k: (s, k, 0))
    grad_b = _wgrad(tag + "_dwd", a, hid, dhb, tok, FS, D, t, tt, grad_b, down_idx, scale=MACARON)
    grad_a = _wgrad(tag + "_dwg", dg, hid, hn, tok, FS, D, t, tt, grad_a, gate_idx)
    grad_a = _wgrad(tag + "_dwu", du, hid, hn, tok, FS, D, t, tt, grad_a, up_idx)
    rider = rider_dhn(grad_a, grad_b) if rider_dhn is not None else None
    act = _spec((None, tm, FS), lambda i, k: (k, i, 0))
    dh_in, dhb_in, d_gain, landed_dhn = _dhn_norm(
        tag + "_dhn", [(dg, act, wgu, _spec((None, None, FS, D), lambda i, k: (k, 0, 0, 0))),
                       (du, act, wgu, _spec((None, None, FS, D), lambda i, k: (k, 1, 0, 0)))],
        NN, h_in, small, norm_row, dh, rider)
    return dh_in, dhb_in, grad_a, grad_b, d_gain, landed_up, landed_dhn


def _conv_fwd(proj_a, conv_w):
    t = proj_a.shape[0]
    tm = min(ROW_TILE, t)
    hb = tm // CONV_HALO

    def body(ab_ref, ac_ref, ax_ref, acp_ref, axp_ref, w_ref, y_ref):
        i = pl.program_id(1)
        u = ac_ref[...] * ax_ref[...]
        up = jnp.where(i > 0, acp_ref[...] * axp_ref[...], 0.0)
        ext = jnp.concatenate([up, u], axis=0)
        u1 = pltpu.roll(ext, 1, 0)[CONV_HALO:]
        u2 = pltpu.roll(ext, 2, 0)[CONV_HALO:]
        w = w_ref[...]
        conv = w[0:1] * u2 + w[1:2] * u1 + w[2:3] * u
        y_ref[...] = (ab_ref[...] * conv).astype(BF16)

    def cur(off):
        return _spec((tm, LANE), lambda j, i: (i, off + j))

    def prev(off):
        return _spec((CONV_HALO, LANE), lambda j, i: (jnp.maximum(i * hb - 1, 0), off + j))

    return pl.pallas_call(
        body, name="conv_fwd", grid=(4, t // tm),
        in_specs=[cur(0), cur(4), cur(8), prev(4), prev(8),
                  _spec((None, None, 8, LANE), lambda j, i: (j, 0, 0, 0))],
        out_specs=_spec((tm, LANE), lambda j, i: (i, j)),
        out_shape=jax.ShapeDtypeStruct((t, 512), BF16),
        compiler_params=_params(("parallel", "parallel")),
    )(proj_a, proj_a, proj_a, proj_a, proj_a, conv_w)


def _conv_bwd(proj_a, conv_w, dy, rider=None):
    t = proj_a.shape[0]
    tm = min(ROW_TILE, t)
    hb = tm // CONV_HALO
    nt = t // tm
    grid = (4, nt)
    nr = rider.n if rider is not None else 0

    def body(*refs):
        ab_ref, ac_ref, ax_ref, dy_ref, acp_ref, axp_ref, abn_ref, dyn_ref, w_ref = refs[:9]
        dab_ref, dac_ref, dax_ref, dw_ref = refs[9 + nr:13 + nr]
        acc_ref = refs[13 + 2 * nr]
        riders = (refs[9:9 + nr], refs[13 + nr:13 + 2 * nr], refs[-2], refs[-1]) if nr else None
        if nr:
            rider.ride(grid, riders, "start")
        i = pl.program_id(1)
        ab, ac, ax = ab_ref[...], ac_ref[...], ax_ref[...]
        u = ac * ax
        up = jnp.where(i > 0, acp_ref[...] * axp_ref[...], 0.0)
        ext = jnp.concatenate([up, u], axis=0)
        u1 = pltpu.roll(ext, 1, 0)[CONV_HALO:]
        u2 = pltpu.roll(ext, 2, 0)[CONV_HALO:]
        w = w_ref[...]
        conv = w[0:1] * u2 + w[1:2] * u1 + w[2:3] * u
        dy_v = dy_ref[...]
        dab_ref[...] = (dy_v * conv).astype(BF16)
        dc = dy_v * ab
        dcn = jnp.where(i < nt - 1, dyn_ref[...] * abn_ref[...], 0.0)
        extn = jnp.concatenate([dc, dcn], axis=0)
        n = tm + CONV_HALO
        dc1 = pltpu.roll(extn, n - 1, 0)[:tm]
        dc2 = pltpu.roll(extn, n - 2, 0)[:tm]
        du = w[2:3] * dc + w[1:2] * dc1 + w[0:1] * dc2
        dac_ref[...] = (du * ax).astype(BF16)
        dax_ref[...] = (du * ac).astype(BF16)
        rid = lax.broadcasted_iota(jnp.int32, (8, LANE), 0)
        part = jnp.where(rid == 0, jnp.sum(dc * u2, axis=0, keepdims=True),
                         jnp.where(rid == 1, jnp.sum(dc * u1, axis=0, keepdims=True),
                                   jnp.where(rid == 2, jnp.sum(dc * u, axis=0, keepdims=True), 0.0)))

        @pl.when(i == 0)
        def _():
            acc_ref[...] = part

        @pl.when(i > 0)
        def _():
            acc_ref[...] += part

        @pl.when(i == nt - 1)
        def _():
            dw_ref[...] = acc_ref[...]

        if nr:
            rider.ride(grid, riders, "finish")

    def cur(off):
        return _spec((tm, LANE), lambda j, i: (i, off + j))

    def prev(off):
        return _spec((CONV_HALO, LANE), lambda j, i: (jnp.maximum(i * hb - 1, 0), off + j))

    def nxt(off):
        return _spec((CONV_HALO, LANE), lambda j, i: (jnp.minimum((i + 1) * hb, nt * hb - 1), off + j))

    outs = pl.pallas_call(
        body, name="conv_bwd", grid=grid,
        in_specs=[cur(0), cur(4), cur(8), cur(0), prev(4), prev(8), nxt(0), nxt(0),
                  _spec((None, None, 8, LANE), lambda j, i: (j, 0, 0, 0))] + (rider.in_specs if nr else []),
        out_specs=[_spec((tm, LANE), lambda j, i: (i, j)), _spec((tm, LANE), lambda j, i: (i, j)),
                   _spec((tm, LANE), lambda j, i: (i, j)), _spec((None, 8, LANE), lambda j, i: (j, 0, 0))]
        + (rider.out_specs if nr else []),
        out_shape=[jax.ShapeDtypeStruct((t, 512), BF16), jax.ShapeDtypeStruct((t, 512), BF16),
                   jax.ShapeDtypeStruct((t, 512), BF16), jax.ShapeDtypeStruct((4, 8, LANE), F32)]
        + (rider.out_shape if nr else []),
        scratch_shapes=[pltpu.VMEM((8, LANE), F32)] + (rider.scratch if nr else []),
        compiler_params=_params(("arbitrary", "arbitrary") if nr else ("parallel", "arbitrary")),
    )(proj_a, proj_a, proj_a, dy, proj_a, proj_a, proj_a, dy, conv_w, *(rider.groups if nr else []))
    return outs[0], outs[1], outs[2], outs[3], (rider.place(outs[4:]) if nr else [])


def _log_sigmoid(z):
    return jnp.minimum(z, 0.0) - jnp.log(1.0 + jnp.exp(-jnp.abs(z)))


def _sb_masks():
    row = lax.broadcasted_iota(jnp.int32, (SBQ, SBQ), 0)
    col = lax.broadcasted_iota(jnp.int32, (SBQ, SBQ), 1)
    return row, col


def _ones_where(mask):
    return jnp.where(mask, 1.0, 0.0).astype(BF16)


def _head_mask(head):
    lane = lax.broadcasted_iota(jnp.int32, (1, LANE), 1)
    return lane >= 64 if head else lane < 64


def _sb_fwd(proj_b, carry=None):
    t = proj_b.shape[0]
    nq = t // SBQ
    scale = 1.0 / math.sqrt(64.0)

    npair = SB_PAIRS_FWD
    wide = npair * LANE
    ngrp = 4 // npair
    chains = [(p, head) for p in range(npair) for head in range(2)]

    grid = (ngrp, nq)
    nc = carry.n if carry is not None else 0

    def body(*refs):
        q_ref, k_ref, v_ref = refs[:3]
        y_ref, l_ref, n_ref = refs[3 + nc:6 + nc]
        riders = (refs[3:3 + nc], refs[6 + nc:6 + 2 * nc], refs[-2], refs[-1]) if nc else None
        if nc:
            carry.ride(grid, riders, "start")
        grp = pl.program_id(0)
        qi = pl.program_id(1)
        row, col = _sb_masks()
        m_suffix = _ones_where(row > col)
        rows = len(chains) * SBQ
        strict = (lax.broadcasted_iota(jnp.int32, (rows, SBQ), 1)
                  < (lax.broadcasted_iota(jnp.int32, (rows, SBQ), 0) & (SBQ - 1)))
        q_pair = []
        for p in range(npair):
            q_all = q_ref[:, p * LANE:(p + 1) * LANE]
            q_pair.append(jnp.concatenate([jnp.where(_head_mask(head), q_all, jnp.zeros_like(q_all)) for head in range(2)],
                                          axis=0))

        def block(kb, state, diag):
            run, acc = state
            start = pl.multiple_of(kb * SBQ, SBQ)
            z = jnp.concatenate([_dg(q_pair[p], k_ref[pl.ds(start, SBQ), p * LANE:(p + 1) * LANE], NT)
                                 for p in range(npair)], axis=0) * scale
            lb = _log_sigmoid(z)
            lk = lb - z
            if diag:
                lk = jnp.where(strict, lk, 0.0)
            hi, lo = _split(lk)
            sums = _dg(jnp.concatenate([hi, lo], axis=0), m_suffix, NN)
            w = jnp.exp(lb + (run + sums[:rows] + sums[rows:]))
            if diag:
                w = jnp.where(strict, w, 0.0)
            wb = w.astype(BF16)
            acc = acc + jnp.concatenate(
                [_dg(wb[2 * p * SBQ:2 * (p + 1) * SBQ], v_ref[pl.ds(start, SBQ), p * LANE:(p + 1) * LANE], NN)
                 for p in range(npair)], axis=0)
            run = run + jnp.sum(hi.astype(F32) + lo.astype(F32), axis=1, keepdims=True)
            return run, acc

        state = block(qi, (jnp.zeros((rows, 1), F32), jnp.zeros((rows, LANE), F32)), True)

        def live(c):
            return jnp.logical_and(c[0] < qi, jnp.max(c[1][0]) > SB_DEAD)

        def step(c):
            return c[0] + 1, block(qi - 1 - c[0], c[1], False)

        count, (run, acc) = lax.while_loop(live, step, (jnp.int32(0), state))
        n_ref[grp * nq + qi] = count.astype(F32)
        hm = _head_mask(0)
        for p in range(npair):
            lo_rows, hi_rows = slice(2 * p * SBQ, (2 * p + 1) * SBQ), slice((2 * p + 1) * SBQ, (2 * p + 2) * SBQ)
            y_ref[:, p * LANE:(p + 1) * LANE] = jnp.where(hm, acc[lo_rows], acc[hi_rows]).astype(BF16)
            l_ref[p] = jnp.where(hm, run[lo_rows], run[hi_rows])
        if nc:
            carry.ride(grid, riders, "finish")

    outs = pl.pallas_call(
        body, name="sb_fwd", grid=grid,
        in_specs=[_spec((SBQ, wide), lambda g, i: (i, g)),
                  _spec((t, wide), lambda g, i: (0, ngrp + g)),
                  _spec((t, wide), lambda g, i: (0, 2 * ngrp + g))] + (carry.in_specs if nc else []),
        out_specs=[_spec((SBQ, wide), lambda g, i: (i, g)), _spec((npair, SBQ, LANE), lambda g, i: (g, i, 0)),
                   pl.BlockSpec(memory_space=pltpu.SMEM)] + (carry.out_specs if nc else []),
        out_shape=[jax.ShapeDtypeStruct((t, 512), BF16), jax.ShapeDtypeStruct((4, t, LANE), F32),
                   jax.ShapeDtypeStruct((ngrp * nq,), F32)] + (carry.out_shape if nc else []),
        scratch_shapes=carry.scratch if nc else [],
        compiler_params=_params(("arbitrary", "arbitrary")),
    )(proj_b, proj_b, proj_b, *(carry.groups if nc else []))
    return outs[0], outs[1], outs[2], (carry.place(outs[3:]) if nc else [])


def _sb_bwd(proj_b, dy, ltot, nblk, rider=None):
    t = proj_b.shape[0]
    nq = t // SBQ
    scale = 1.0 / math.sqrt(64.0)

    npair = SB_PAIRS_BWD
    wide = npair * LANE
    ngrp = 4 // npair
    chains = [(p, head) for p in range(npair) for head in range(2)]
    grid = (ngrp, nq)
    nr = rider.n if rider is not None else 0
    per_count = SB_PAIRS_FWD // SB_PAIRS_BWD

    def body(*refs):
        q_ref, k_ref, v_ref, dy_ref, l_ref, n_ref = refs[:6]
        dq_ref, dk_ref, dv_ref = refs[6 + nr:9 + nr]
        dk_acc, dv_acc = refs[9 + 2 * nr:11 + 2 * nr]
        riders = (refs[6:6 + nr], refs[9 + nr:9 + 2 * nr], refs[-2], refs[-1]) if nr else None
        if nr:
            rider.ride(grid, riders, "start")
        grp = pl.program_id(0)
        qi = pl.program_id(1)

        @pl.when(qi == 0)
        def _():
            dk_acc[...] = jnp.zeros_like(dk_acc)
            dv_acc[...] = jnp.zeros_like(dv_acc)

        row, col = _sb_masks()
        m_prefix = _ones_where(row <= col)
        m_before = _ones_where(row < col)
        rows = len(chains) * SBQ
        strict = (lax.broadcasted_iota(jnp.int32, (rows, SBQ), 1)
                  < (lax.broadcasted_iota(jnp.int32, (rows, SBQ), 0) & (SBQ - 1)))
        q_pair, do_pair, ltot = [], [], []
        for p in range(npair):
            pl_ = slice(p * LANE, (p + 1) * LANE)
            q_all = q_ref[:, pl_]
            do_all = dy_ref[:, pl_].astype(BF16)
            q_pair.append(jnp.concatenate([jnp.where(_head_mask(h), q_all, jnp.zeros_like(q_all)) for h in range(2)], axis=0))
            do_pair.append(jnp.concatenate([jnp.where(_head_mask(h), do_all, jnp.zeros_like(do_all)) for h in range(2)], axis=0))
            ltot += [l_ref[p][:, h * 64:h * 64 + 1] for h in range(2)]
        ltot = jnp.concatenate(ltot, axis=0)

        def pair_rows(a, p):
            return a[2 * p * SBQ:2 * (p + 1) * SBQ]

        def block(kb, state, diag):
            seen, dseen, dq = state
            start = pl.multiple_of(kb * SBQ, SBQ)
            kk = [k_ref[pl.ds(start, SBQ), p * LANE:(p + 1) * LANE] for p in range(npair)]
            vv = [v_ref[pl.ds(start, SBQ), p * LANE:(p + 1) * LANE] for p in range(npair)]
            z = jnp.concatenate([_dg(q_pair[p], kk[p], NT) for p in range(npair)], axis=0) * scale
            lb = _log_sigmoid(z)
            lk = lb - z
            if diag:
                lk = jnp.where(strict, lk, 0.0)
            hi, lo = _split(lk)
            sums = _dg(jnp.concatenate([hi, lo], axis=0), m_prefix, NN)
            w = jnp.exp(lb + ((ltot - seen) - (sums[:rows] + sums[rows:])))
            if diag:
                w = jnp.where(strict, w, 0.0)
            wb = w.astype(BF16)
            da = w * jnp.concatenate([_dg(do_pair[p], vv[p], NT) for p in range(npair)], axis=0)
            dah, dal = _split(da)
            dsums = _dg(jnp.concatenate([dah, dal], axis=0), m_before, NN)
            sig = jnp.exp(lb)
            dz = (da * (1.0 - sig) - (dseen + dsums[:rows] + dsums[rows:]) * sig) * scale
            if diag:
                dz = jnp.where(strict, dz, 0.0)
            dzb = dz.astype(BF16)
            for p in range(npair):
                pl_ = slice(p * LANE, (p + 1) * LANE)
                dv_acc[pl.ds(start, SBQ), pl_] += _dg(pair_rows(wb, p), do_pair[p], TN)
                dk_acc[pl.ds(start, SBQ), pl_] += _dg(pair_rows(dzb, p), q_pair[p], TN)
            dq = dq + jnp.concatenate([_dg(pair_rows(dzb, p), kk[p], NN) for p in range(npair)], axis=0)
            seen = seen + jnp.sum(hi.astype(F32) + lo.astype(F32), axis=1, keepdims=True)
            dseen = dseen + jnp.sum(da, axis=1, keepdims=True)
            return seen, dseen, dq

        zero = (jnp.zeros((rows, 1), F32), jnp.zeros((rows, 1), F32), jnp.zeros((rows, LANE), F32))
        first = qi - n_ref[(grp // per_count) * nq + qi].astype(jnp.int32)
        state = lax.fori_loop(first, qi, lambda kb, c: block(kb, c, False), zero)
        _, _, dq = block(qi, state, True)
        for p in range(npair):
            dq_ref[:, p * LANE:(p + 1) * LANE] = jnp.where(
                _head_mask(0), dq[2 * p * SBQ:(2 * p + 1) * SBQ], dq[(2 * p + 1) * SBQ:(2 * p + 2) * SBQ]).astype(BF16)

        @pl.when(qi == nq - 1)
        def _():
            dk_ref[...] = dk_acc[...].astype(BF16)
            dv_ref[...] = dv_acc[...].astype(BF16)

        if nr:
            rider.ride(grid, riders, "finish")

    full = jax.ShapeDtypeStruct((t, 512), BF16)
    outs = pl.pallas_call(
        body, name="sb_bwd", grid=grid,
        in_specs=[_spec((SBQ, wide), lambda g, i: (i, g)),
                  _spec((t, wide), lambda g, i: (0, ngrp + g)),
                  _spec((t, wide), lambda g, i: (0, 2 * ngrp + g)),
                  _spec((SBQ, wide), lambda g, i: (i, ngrp + g)),
                  _spec((npair, SBQ, LANE), lambda g, i: (g, i, 0)),
                  pl.BlockSpec(memory_space=pltpu.SMEM)] + (rider.in_specs if nr else []),
        out_specs=[_spec((SBQ, wide), lambda g, i: (i, g)),
                   _spec((t, wide), lambda g, i: (0, g)), _spec((t, wide), lambda g, i: (0, g))]
        + (rider.out_specs if nr else []),
        out_shape=[full, full, full] + (rider.out_shape if nr else []),
        scratch_shapes=[pltpu.VMEM((t, wide), F32), pltpu.VMEM((t, wide), F32)] + (rider.scratch if nr else []),
        compiler_params=_params(("arbitrary", "arbitrary")),
    )(proj_b, proj_b, proj_b, dy, ltot, nblk, *(rider.groups if nr else []))
    return outs[0], outs[1], outs[2], (rider.place(outs[3:]) if nr else [])


def _hgrn_gates(qr, fr, c0, c1):
    mx = jnp.maximum(c0, c1)
    e0, e1 = jnp.exp(c0 - mx), jnp.exp(c1 - mx)
    lb = e1 / (e0 + e1)
    sx = _sigmoid(fr)
    f = lb + (1.0 - lb) * sx
    k = (1.0 - lb) * (1.0 - sx)
    sq = _sigmoid(qr)
    return lb, sx, f, k, sq, qr * sq


def _chunk_sums(mask, x):
    n = x.shape[1]
    hi, lo = _split(x)
    both = _dg(_ones_where(mask), jnp.concatenate([hi, lo], axis=1), NN)
    return both[:, :n] + both[:, n:]


def _chunk_masks():
    row = lax.broadcasted_iota(jnp.int32, (CHUNK, CHUNK), 0)
    col = lax.broadcasted_iota(jnp.int32, (CHUNK, CHUNK), 1)
    return col <= row, col >= row


def _hgrn_fwd(proj_c, small, carry=None):
    t = proj_c.shape[1]
    nc = t // CHUNK
    nh = D // HD

    hps = HGRN_HPS
    wide = hps * HD
    grid = (nh // hps, nc)
    nr = carry.n if carry is not None else 0

    def body(*refs):
        p_ref, c0_ref, c1_ref, gam_ref = refs[:4]
        o_ref, y_ref, sst_ref = refs[4 + nr:7 + nr]
        st_ref = refs[7 + 2 * nr]
        riders = (refs[4:4 + nr], refs[7 + nr:7 + 2 * nr], refs[-2], refs[-1]) if nr else None
        if nr:
            carry.ride(grid, riders, "start")
        c = pl.program_id(1)

        @pl.when(c == 0)
        def _():
            st_ref[...] = jnp.zeros_like(st_ref)

        _, _, f_all, k_all, _, q_all = _hgrn_gates(p_ref[0], p_ref[1], c0_ref[...], c1_ref[...])
        tril, _ = _chunk_masks()
        b_all = _chunk_sums(tril, jnp.log(f_all))
        for j in range(hps):
            ln = slice(j * HD, (j + 1) * HD)
            st0 = st_ref[j]
            sst_ref[j] = st0
            v, g = p_ref[2, :, ln], p_ref[3, :, ln]
            q, k, b = q_all[:, ln], k_all[:, ln], b_all[:, ln]
            bm = b[CHUNK // 2 - 1:CHUNK // 2]
            bl = b[CHUNK - 1:CHUNK]
            qd = q * jnp.exp(b)
            qt = q * jnp.exp(b - bm)
            kt = k * jnp.exp(bm - b)
            kl = k * jnp.exp(bl - b)
            vb = v.astype(BF16)
            att = jnp.where(tril, _dot3(qt, kt, NT), 0.0)
            o = _dg(qd.astype(BF16), st0.astype(BF16), NT) + _dg(att.astype(BF16), vb, NN)
            st_ref[j] = st0 * jnp.exp(bl) + _dg(vb, kl.astype(BF16), TN)
            o_ref[:, ln] = o
            r = lax.rsqrt(jnp.mean(o * o, axis=-1, keepdims=True) + RMS_EPS)
            y_ref[:, ln] = (o * r * gam_ref[...] * (g * _sigmoid(g))).astype(BF16)
        if nr:
            carry.ride(grid, riders, "finish")

    outs = pl.pallas_call(
        body, name="hgrn_fwd", grid=grid,
        in_specs=[_spec((4, CHUNK, wide), lambda h, c: (0, c, h)),
                  _spec((None, 1, wide), lambda h, c: (R_CLB, 0, h)),
                  _spec((None, 1, wide), lambda h, c: (R_CLB + 1, 0, h)),
                  _spec((None, 1, HD), lambda h, c: (R_GAM, 0, 0))] + (carry.in_specs if nr else []),
        out_specs=[_spec((CHUNK, wide), lambda h, c: (c, h)), _spec((CHUNK, wide), lambda h, c: (c, h)),
                   _spec((None, hps, HD, HD), lambda h, c: (c, h, 0, 0))] + (carry.out_specs if nr else []),
        out_shape=[jax.ShapeDtypeStruct((t, D), F32), jax.ShapeDtypeStruct((t, D), BF16),
                   jax.ShapeDtypeStruct((nc, nh, HD, HD), F32)] + (carry.out_shape if nr else []),
        scratch_shapes=[pltpu.VMEM((hps, HD, HD), F32)] + (carry.scratch if nr else []),
        compiler_params=_params(("arbitrary", "arbitrary")),
    )(proj_c, small, small, small, *(carry.groups if nr else []))
    return outs[0], outs[1], outs[2], (carry.place(outs[3:]) if nr else [])


def _hgrn_bwd(proj_c, small, o, sst, dyc, rider=None):
    t = proj_c.shape[1]
    nc = t // CHUNK
    nh = D // HD

    hps = HGRN_HPS
    wide = hps * HD
    ng = nh // hps
    grid = (ng, nc)
    nr = rider.n if rider is not None else 0

    def body(*refs):
        p_ref, c0_ref, c1_ref, gam_ref, o_ref, sst_ref, dy_ref = refs[:7]
        dp_ref, dclb_ref, dgam_ref = refs[7 + nr:10 + nr]
        dst_ref, dlb_acc, dgam_acc = refs[10 + 2 * nr:13 + 2 * nr]
        riders = (refs[7:7 + nr], refs[10 + nr:10 + 2 * nr], refs[-2], refs[-1]) if nr else None
        if nr:
            rider.ride(grid, riders, "start")
        group = pl.program_id(0)
        step = pl.program_id(1)

        @pl.when(step == 0)
        def _():
            dst_ref[...] = jnp.zeros_like(dst_ref)
            dlb_acc[...] = jnp.zeros_like(dlb_acc)

        @pl.when((step == 0) & (group == 0))
        def _():
            dgam_acc[...] = jnp.zeros_like(dgam_acc)

        gam = gam_ref[...]
        qr_all = p_ref[0]
        lb_all, sx_all, f_all, k_all, sq_all, q_all = _hgrn_gates(qr_all, p_ref[1], c0_ref[...], c1_ref[...])
        tril, triu = _chunk_masks()
        b_all = _chunk_sums(tril, jnp.log(f_all))
        dq_parts, dk_parts, db_parts, dgam_parts = [], [], [], []
        for j in range(hps):
            ln = slice(j * HD, (j + 1) * HD)
            st0 = sst_ref[j]
            dst1 = dst_ref[j]
            v, g = p_ref[2, :, ln], p_ref[3, :, ln]
            q, k, b = q_all[:, ln], k_all[:, ln], b_all[:, ln]
            bm = b[CHUNK // 2 - 1:CHUNK // 2]
            bl = b[CHUNK - 1:CHUNK]
            eb = jnp.exp(b)
            e_qt = jnp.exp(b - bm)
            e_kt = jnp.exp(bm - b)
            e_kl = jnp.exp(bl - b)
            e_bl = jnp.exp(bl)
            qd, qt, kt, kl = q * eb, q * e_qt, k * e_kt, k * e_kl
            ov = o_ref[:, ln]
            r = lax.rsqrt(jnp.mean(ov * ov, axis=-1, keepdims=True) + RMS_EPS)
            oh = ov * r
            sg = _sigmoid(g)
            dy = dy_ref[:, ln]
            dp_ref[3, :, ln] = (dy * oh * gam * (sg * (1.0 + g * (1.0 - sg)))).astype(BF16)
            dyv = dy * (g * sg)
            gdy = dyv * gam
            do = (gdy - oh * jnp.mean(gdy * oh, axis=-1, keepdims=True)) * r
            dob, vb = do.astype(BF16), v.astype(BF16)
            st0b, dst1b = st0.astype(BF16), dst1.astype(BF16)
            st1 = st0 * e_bl + _dg(vb, kl.astype(BF16), TN)
            att = jnp.where(tril, _dot3(qt, kt, NT), 0.0)
            datt = jnp.where(tril, _dg(dob, vb, NT), 0.0)
            dv = _dg(att.astype(BF16), dob, TN) + _dg(kl.astype(BF16), dst1b, NT)
            dq = _dot3(datt, kt, NN) * e_qt + _dg(dob, st0b, NN) * eb
            dk = _dot3(datt, qt, TN) * e_kt + _dg(vb, dst1b, NN) * e_kl
            db = q * dq - k * dk
            last = lax.broadcasted_iota(jnp.int32, (CHUNK, 1), 0) == CHUNK - 1
            db = db + jnp.where(last, jnp.sum(dst1 * st1, axis=0, keepdims=True), 0.0)
            dst_ref[j] = dst1 * e_bl + _dg(dob, qd.astype(BF16), TN)
            dp_ref[2, :, ln] = dv.astype(BF16)
            dq_parts.append(dq)
            dk_parts.append(dk)
            db_parts.append(db)
            dgam_parts.append(jnp.sum(dyv * oh, axis=0, keepdims=True))

        dq_all, dk_all = jnp.concatenate(dq_parts, axis=1), jnp.concatenate(dk_parts, axis=1)
        dlf = _chunk_sums(triu, jnp.concatenate(db_parts, axis=1))
        dp_ref[0] = (dq_all * (sq_all * (1.0 + qr_all * (1.0 - sq_all)))).astype(BF16)
        tmp = dlf / f_all - dk_all
        dp_ref[1] = (tmp * (1.0 - lb_all) * sx_all * (1.0 - sx_all)).astype(BF16)
        dlb_acc[...] += jnp.sum((1.0 - sx_all) * tmp, axis=0, keepdims=True)
        dgam_acc[...] += functools.reduce(lambda a, b: a + b, dgam_parts)

        @pl.when(step == nc - 1)
        def _():
            d1 = dlb_acc[...] * lb_all * (1.0 - lb_all)
            dclb_ref[...] = jnp.where(lax.broadcasted_iota(jnp.int32, (2, wide), 0) == 0, -d1, d1)

        @pl.when((step == nc - 1) & (group == ng - 1))
        def _():
            dgam_ref[...] = dgam_acc[...]

        if nr:
            rider.ride(grid, riders, "finish")

    rev = lambda h, s: (nc - 1 - s, h)
    outs = pl.pallas_call(
        body, name="hgrn_bwd", grid=grid,
        in_specs=[_spec((4, CHUNK, wide), lambda h, s: (0, nc - 1 - s, h)),
                  _spec((None, 1, wide), lambda h, s: (R_CLB, 0, h)),
                  _spec((None, 1, wide), lambda h, s: (R_CLB + 1, 0, h)),
                  _spec((None, 1, HD), lambda h, s: (R_GAM, 0, 0)),
                  _spec((CHUNK, wide), rev),
                  _spec((None, hps, HD, HD), lambda h, s: (nc - 1 - s, h, 0, 0)),
                  _spec((CHUNK, wide), rev)] + (rider.in_specs if nr else []),
        out_specs=[_spec((4, CHUNK, wide), lambda h, s: (0, nc - 1 - s, h)),
                   _spec((2, wide), lambda h, s: (0, h)),
                   _spec((1, HD), lambda h, s: (0, 0))] + (rider.out_specs if nr else []),
        out_shape=[jax.ShapeDtypeStruct((4, t, D), BF16), jax.ShapeDtypeStruct((2, D), F32),
                   jax.ShapeDtypeStruct((1, HD), F32)] + (rider.out_shape if nr else []),
        scratch_shapes=[pltpu.VMEM((hps, HD, HD), F32), pltpu.VMEM((1, wide), F32), pltpu.VMEM((1, HD), F32)]
        + (rider.scratch if nr else []),
        compiler_params=_params(("arbitrary", "arbitrary")),
    )(proj_c, small, small, small, o, sst, dyc, *(rider.groups if nr else []))
    return outs[0], outs[1], outs[2], (rider.place(outs[3:]) if nr else [])


def _adamw(name, w, g, m, v):
    rows, cols = w.shape
    br = rows
    for cand in (512, 352, 256):
        if rows % cand == 0:
            br = cand
            break
    c1 = 1.0 - ADAM_B1 ** ADAM_STEP
    c2 = 1.0 - ADAM_B2 ** ADAM_STEP

    def body(w_ref, g_ref, m_ref, v_ref, d_ref, mo_ref, vo_ref):
        gv = g_ref[...]
        mn = ADAM_B1 * m_ref[...] + (1.0 - ADAM_B1) * gv
        vn = ADAM_B2 * v_ref[...] + (1.0 - ADAM_B2) * (gv * gv)
        mo_ref[...] = mn
        vo_ref[...] = vn
        d_ref[...] = -ADAM_LR * ((mn / c1) / (jnp.sqrt(vn / c2) + ADAM_EPS) + ADAM_WD * w_ref[...])

    blk = _spec((br, cols), lambda i: (i, 0))
    shape = jax.ShapeDtypeStruct((rows, cols), F32)
    return pl.pallas_call(
        body, name=name, grid=(rows // br,), in_specs=[blk] * 4, out_specs=[blk] * 3, out_shape=[shape] * 3,
        compiler_params=_params(("parallel",)),
    )(w, g, m, v)


def _place():
    x, y, c = lax.axis_index("x"), lax.axis_index("y"), lax.axis_index("c")
    chips = [(1 - x, y), (x, 1 - y), (1 - x, 1 - y)]
    return x, y, c, chips


class _Rider:
    n = 0
    relay = None

    def ride(self, grid, refs, when):
        if not self.n:
            return
        ids = [pl.program_id(a) for a in range(len(grid))]
        first = functools.reduce(jnp.logical_and, [i == 0 for i in ids])
        last = functools.reduce(jnp.logical_and, [i == g - 1 for i, g in zip(ids, grid)])
        phases = [(first, self.start), (last, self.relay)] if when == "start" else [(last, self.finish)]
        for cond, phase in phases:
            if phase is not None:
                pl.when(cond)(functools.partial(phase, *refs))


class _Gather(_Rider):
    def __init__(self, groups):
        self.groups = list(groups)
        self.n = len(self.groups)
        self.in_specs = [ANY] * self.n
        self.out_specs = [ANY] * self.n
        self.out_shape = [jax.ShapeDtypeStruct((NSH,) + g.shape, g.dtype) for g in self.groups]
        self.scratch = [pltpu.SemaphoreType.DMA((6 * self.n,)), pltpu.SemaphoreType.DMA((6 * self.n,))] if self.n else []

    def _copies(self, ins, outs, send, recv):
        x, y, c, chips = _place()
        sibling = (x, y, 1 - c)

        def half(gi, chip, hc):
            return outs[gi].at[2 * chip[0] + chip[1], hc]

        def copy(gi, k, src, dst, to):
            return pltpu.make_async_remote_copy(src_ref=src, dst_ref=dst, send_sem=send.at[6 * gi + k],
                                                recv_sem=recv.at[6 * gi + k], device_id=to, device_id_type=MESH)

        pairs = [(gi, j, chip) for gi in range(self.n) for j, chip in enumerate(chips)]
        first = [copy(gi, j, ins[gi].at[c], half(gi, (x, y), c), (*chip, c)) for gi, j, chip in pairs]
        landed = [copy(gi, j, half(gi, chip, c), half(gi, chip, c), sibling) for gi, j, chip in pairs]
        relay = [copy(gi, 3 + j, half(gi, chip, c), half(gi, chip, c), sibling) for gi, j, chip in pairs]
        relayed = [copy(gi, 3 + j, half(gi, chip, 1 - c), half(gi, chip, 1 - c), sibling) for gi, j, chip in pairs]
        return first, landed, relay, relayed

    def start(self, ins, outs, send, recv):
        for cp in self._copies(ins, outs, send, recv)[0]:
            cp.start()

    def relay(self, ins, outs, send, recv):
        _, landed, relay, _ = self._copies(ins, outs, send, recv)
        for arrived, onward in zip(landed, relay):
            arrived.wait_recv()
            onward.start()

    def finish(self, ins, outs, send, recv):
        first, _, relay, relayed = self._copies(ins, outs, send, recv)
        for cp in relayed:
            cp.wait_recv()
        for cp in first + relay:
            cp.wait_send()

    def place(self, outs):
        me = 2 * lax.axis_index("x") + lax.axis_index("y")
        return [lax.dynamic_update_index_in_dim(o, g, me, 0) for o, g in zip(outs, self.groups)]


def _alone(name, rider):
    n = rider.n

    def body(*refs):
        parts = (refs[:n], refs[n:2 * n], refs[2 * n], refs[2 * n + 1])
        rider.start(*parts)
        if rider.relay is not None:
            rider.relay(*parts)
        rider.finish(*parts)

    outs = pl.pallas_call(
        body, name=name, in_specs=rider.in_specs, out_specs=rider.out_specs, out_shape=rider.out_shape,
        scratch_shapes=rider.scratch, compiler_params=pltpu.CompilerParams(has_side_effects=True),
    )(*rider.groups)
    return rider.place(outs)


class _Swap(_Rider):
    def __init__(self, slots):
        self.slots = list(slots)
        self.groups = [buf for buf, _, _ in self.slots]
        self.n = len(self.slots)
        self.in_specs = [ANY] * self.n
        self.out_specs = [ANY] * self.n
        self.out_shape = [jax.ShapeDtypeStruct((NSH, rows // 2, buf.shape[2]), buf.dtype) for buf, _, rows in self.slots]
        self.scratch = [pltpu.SemaphoreType.DMA((self.n,)), pltpu.SemaphoreType.DMA((self.n,))]

    def _copies(self, ins, outs, send, recv):
        x, y, c, _ = _place()
        cps = []
        for i, (_, row0, rows) in enumerate(self.slots):
            half = rows // 2
            src = ins[i].at[:, pl.ds(pl.multiple_of(row0 + (1 - c) * half, 16), half)]
            cps.append(pltpu.make_async_remote_copy(src_ref=src, dst_ref=outs[i], send_sem=send.at[i],
                                                    recv_sem=recv.at[i], device_id=(x, y, 1 - c), device_id_type=MESH))
        return cps

    def start(self, ins, outs, send, recv):
        for cp in self._copies(ins, outs, send, recv):
            cp.start()

    def finish(self, ins, outs, send, recv):
        for cp in self._copies(ins, outs, send, recv):
            cp.wait()

    def place(self, outs):
        return list(outs)


class _Share(_Rider):
    def __init__(self, arrays):
        self.groups = list(arrays)
        self.n = len(self.groups)
        self.in_specs = [ANY] * self.n
        self.out_specs = [ANY] * self.n
        self.out_shape = [jax.ShapeDtypeStruct((2,) + g.shape, g.dtype) for g in self.groups]
        self.scratch = [pltpu.SemaphoreType.DMA((self.n,)), pltpu.SemaphoreType.DMA((self.n,))]

    def _copies(self, ins, outs, send, recv, half):
        x, y, c, _ = _place()
        return [pltpu.make_async_remote_copy(src_ref=ins[gi], dst_ref=outs[gi].at[c if half == "mine" else 1 - c],
                                             send_sem=send.at[gi], recv_sem=recv.at[gi], device_id=(x, y, 1 - c),
                                             device_id_type=MESH) for gi in range(self.n)]

    def start(self, ins, outs, send, recv):
        for cp in self._copies(ins, outs, send, recv, "mine"):
            cp.start()

    def finish(self, ins, outs, send, recv):
        for cp in self._copies(ins, outs, send, recv, "theirs"):
            cp.wait_recv()
        for cp in self._copies(ins, outs, send, recv, "mine"):
            cp.wait_send()

    def place(self, outs):
        c = lax.axis_index("c")
        return [lax.dynamic_update_index_in_dim(o, g, c, 0) for o, g in zip(outs, self.groups)]


class _Send(_Rider):
    def __init__(self, arrays):
        self.groups = list(arrays)
        self.n = len(self.groups)
        self.in_specs = [ANY] * self.n
        self.out_specs = [ANY] * self.n
        self.out_shape = [jax.ShapeDtypeStruct((3,) + g.shape[1:], g.dtype) for g in self.groups]
        self.scratch = [pltpu.SemaphoreType.DMA((3 * self.n,)), pltpu.SemaphoreType.DMA((3 * self.n,))]

    def _copies(self, ins, outs, send, recv):
        x, y, c, chips = _place()
        return [pltpu.make_async_remote_copy(src_ref=ins[gi].at[2 * chip[0] + chip[1]], dst_ref=outs[gi].at[j],
                                             send_sem=send.at[3 * gi + j], recv_sem=recv.at[3 * gi + j],
                                             device_id=(*chip, c), device_id_type=MESH)
                for gi in range(self.n) for j, chip in enumerate(chips)]

    def start(self, ins, outs, send, recv):
        for cp in self._copies(ins, outs, send, recv):
            cp.start()

    def finish(self, ins, outs, send, recv):
        for cp in self._copies(ins, outs, send, recv):
            cp.wait()

    def place(self, outs):
        return list(outs)


def _pair_sum(name, slots, got, c_idx):
    n = len(slots)
    in_specs, out_specs, out_shape, operands = [], [], [], []
    for (buf, row0, rows), g in zip(slots, got):
        hb, cols = rows // 4, buf.shape[2]
        in_specs += [pl.BlockSpec((None, hb, cols), lambda q, i, cr, r=row0 // hb: (q, r + 2 * cr[0] + i, 0)),
                     pl.BlockSpec((None, hb, cols), lambda q, i, cr: (q, i, 0))]
        out_specs.append(pl.BlockSpec((None, hb, cols), lambda q, i, cr: (q, i, 0)))
        out_shape.append(jax.ShapeDtypeStruct(g.shape, BF16))
        operands += [buf, g]

    def body(c_ref, *refs):
        for i in range(n):
            refs[2 * n + i][...] = (refs[2 * i][...].astype(F32) + refs[2 * i + 1][...].astype(F32)).astype(BF16)

    return pl.pallas_call(
        body, name=name,
        grid_spec=pltpu.PrefetchScalarGridSpec(num_scalar_prefetch=1, grid=(NSH, 2), in_specs=in_specs, out_specs=out_specs),
        out_shape=out_shape, compiler_params=_params(("parallel", "parallel")),
    )(c_idx, *operands)


def _owner_sum(name, pairs, got, p_idx):
    n = len(pairs)
    in_specs, out_specs, out_shape, operands = [], [], [], []
    for own, g in zip(pairs, got):
        _, rows, cols = own.shape
        hb = rows // 2
        in_specs += [pl.BlockSpec((None, hb, cols), lambda i, pr: (pr[0], i, 0)),
                     pl.BlockSpec((3, hb, cols), lambda i, pr: (0, i, 0))]
        out_specs.append(pl.BlockSpec((hb, cols), lambda i, pr: (i, 0)))
        out_shape.append(jax.ShapeDtypeStruct((rows, cols), F32))
        operands += [own, g]

    def body(p_ref, *refs):
        for i in range(n):
            a_ref, b_ref = refs[2 * i], refs[2 * i + 1]
            refs[2 * n + i][...] = ((a_ref[...].astype(F32) + b_ref[0].astype(F32)) + b_ref[1].astype(F32)) + b_ref[2].astype(F32)

    return pl.pallas_call(
        body, name=name,
        grid_spec=pltpu.PrefetchScalarGridSpec(num_scalar_prefetch=1, grid=(2,), in_specs=in_specs, out_specs=out_specs),
        out_shape=out_shape, compiler_params=_params(("parallel",)),
    )(p_idx, *operands)


def _sum_small(slab):
    def body(in_ref, out_ref, all_ref, send, recv):
        x, y, c, _ = _place()
        me = 4 * x + 2 * y + c
        all_ref[me] = in_ref[...]
        cps = []
        for k in range(1, 8):
            peer = (x ^ (k >> 2), y ^ ((k >> 1) & 1), c ^ (k & 1))
            cps.append(pltpu.make_async_remote_copy(src_ref=in_ref, dst_ref=all_ref.at[me], send_sem=send.at[k - 1],
                                                    recv_sem=recv.at[k - 1], device_id=peer, device_id_type=MESH))
        for cp in cps:
            cp.start()
        for cp in cps:
            cp.wait()
        total = all_ref[0]
        for d in range(1, 8):
            total = total + all_ref[d]
        out_ref[...] = total

    return pl.pallas_call(
        body, name="sum_small",
        in_specs=[pl.BlockSpec(memory_space=pltpu.VMEM)], out_specs=pl.BlockSpec(memory_space=pltpu.VMEM),
        out_shape=jax.ShapeDtypeStruct(slab.shape, F32),
        scratch_shapes=[pltpu.VMEM((8,) + slab.shape, F32), pltpu.SemaphoreType.DMA((7,)), pltpu.SemaphoreType.DMA((7,))],
        compiler_params=pltpu.CompilerParams(has_side_effects=True),
    )(slab)


FFNS = ("pre0", "post0", "pre1", "post1")
W_SHAPES = dict({f + "_gu": (NSH, 2, FS, D) for f in FFNS}, **{f + "_d": (NSH, FS, D) for f in FFNS},
                ab_in=(NSH, D, 768), ab_out=(NSH, 256, D), conv=(NSH, 2, 8, LANE), c_in=(NSH, D, D), c_out=(NSH, 256, D))
CARRIED = dict(pre0_up=("pre0_d",), pre0_down=("ab_in", "ab_out", "conv"),
               sb_fwd=("post0_gu", "post0_d"), post0_up=("pre1_gu",), post0_down=("pre1_d",),
               pre1_up=("c_in", "c_out"), hgrn_fwd=("post1_gu", "post1_d"))


FFN_SLOTS = dict(pre0=(0, 2, 0), pre1=(1, 3, 1), post0=(4, 6, 2), post1=(5, 7, 3))
GRAD_SLOTS = dict(ab_out=("b", B_ABOUT, 256), c_in=("b", B_CIN, D), c_out=("b", B_COUT, 256), ab_in=("c", 0, D))
for _f, (_g, _u, _d) in FFN_SLOTS.items():
    GRAD_SLOTS.update({_f + "_g": ("a", _g * FS, FS), _f + "_u": ("a", _u * FS, FS), _f + "_d": ("b", _d * FS, FS)})
REDUCE_STAGES = dict(x=("post1_g", "post1_u", "post1_d", "c_out"),
                     y=("c_in", "pre1_g", "pre1_u", "pre1_d", "post0_g", "post0_u", "post0_d", "ab_out"),
                     z1=("ab_in",), z2=("pre0_g", "pre0_u", "pre0_d"))


def _local_step(x, target, weights, small, shards=None, place=None):
    t = x.shape[0]
    tm = min(MM_TILE, t)
    tw = min(WGRAD_TILE, t)
    nt = t // tm
    tok_si = _spec((tm, D), lambda s, i: (i, 0))
    w = dict(weights)
    reduced = {}

    def stage_slots(stage, buffers):
        return [(buffers[GRAD_SLOTS[n][0]],) + GRAD_SLOTS[n][1:] for n in REDUCE_STAGES[stage]]

    def swap_rider(stage, buffers):
        return _Swap(stage_slots(stage, buffers)) if place is not None else None

    def reduce_start(stage, buffers, swapped=None):
        if place is None:
            return [], None
        slots = stage_slots(stage, buffers)
        if swapped is None:
            swapped = _alone("grad_swap_" + stage, _Swap(slots))
        pairs = _pair_sum("grad_pair_sum_" + stage, slots, swapped, place[0])
        return pairs, _Send(pairs)

    def reduce_end(stage, pairs, landed):
        if place is None:
            return None
        mine = _owner_sum("grad_owner_sum_" + stage, pairs, landed, place[1])
        return _Share(mine)

    def shared(stage, landed):
        if place is not None:
            reduced.update(zip(REDUCE_STAGES[stage], landed))

    def carried(kernel_name):
        names = [n for n in CARRIED[kernel_name] if n not in w]
        return names, (_Gather([shards[n] for n in names]) if names else None)

    def land(names, arrays):
        for n, a in zip(names, arrays):
            w[n] = a.reshape(W_SHAPES[n])

    def ffn_forward(tag, h, norm_row):
        hn = _norm_fwd(tag + "_norm", h, small, norm_row)
        names, gather = carried(tag + "_up") if tag + "_up" in CARRIED else ([], None)
        s_up, s_gate, a, got = _ffn_up(tag + "_up", hn, w[tag + "_gu"], gather)
        land(names, got)
        names, gather = carried(tag + "_down") if tag + "_down" in CARRIED else ([], None)
        out = _ffn_down(tag + "_down", a, w[tag + "_d"], h, gather)
        if gather is not None:
            out, got = out
            land(names, got)
        return out, (h, hn, s_up, s_gate, a)

    def out_proj(name, y, w_out, h):
        return _mm(name, [(y, _spec((tm, 256), lambda i, k: (i, k)), w_out, _spec((None, 256, D), lambda i, k: (k, 0, 0)))],
                   grid=(nt, NSH), o_shape=(t, D), o_dtype=F32, o_spec=_spec((tm, D), lambda i, k: (i, 0)),
                   dims=NN, kaxis=1, nk=NSH, acc_shape=(tm, D), res=(h, _spec((tm, D), lambda i, k: (i, 0))))

    def out_proj_bwd(tag, dhb, y, w_out, blk, grad_b, make_rider=None):
        grad_b = _wgrad(tag + "_dwout", y, _spec((tw, 256), lambda s, k: (k, s)), dhb, _spec((tw, D), lambda s, k: (k, 0)),
                        256, D, t, tw, grad_b, blk)
        rider = make_rider(grad_b) if make_rider is not None else None
        dy = _mm(tag + "_dy", [(dhb, tok_si, w_out, _spec((None, 256, D), lambda s, i: (s, 0, 0)))],
                 grid=(NSH, nt), o_shape=(t, D), o_dtype=F32, o_spec=_spec((tm, 256), lambda s, i: (i, s)),
                 dims=NT, kaxis=1, nk=1, carry=rider)
        dy, landed = dy if rider is not None else (dy, None)
        return dy, grad_b, landed

    h0 = x
    h1, pre0 = ffn_forward("pre0", h0, R_PRE)
    hn_ab = _norm_fwd("mix0_norm", h1, small, R_MIX)
    proj_a = _mm("ab_proj_a", [(hn_ab, tok_si, w["ab_in"], _spec((None, D, 768), lambda s, i: (s, 0, 0)))],
                 grid=(2, nt), o_shape=(t, 1536), o_dtype=F32, o_spec=_spec((tm, 768), lambda s, i: (i, s)),
                 dims=NN, kaxis=1, nk=1)
    proj_b = _mm("ab_proj_b", [(hn_ab, tok_si, w["ab_in"], _spec((None, D, 768), lambda s, i: (s + 2, 0, 0)))],
                 grid=(2, nt), o_shape=(t, 1536), o_dtype=BF16, o_spec=_spec((tm, 768), lambda s, i: (i, s)),
                 dims=NN, kaxis=1, nk=1)
    y_a = _conv_fwd(proj_a, w["conv"])
    names, gather = carried("sb_fwd")
    y_b, ltot, nblk, got = _sb_fwd(proj_b, gather)
    land(names, got)
    y_ab = jnp.concatenate([y_a, y_b], axis=1)
    h2 = out_proj("ab_out", y_ab, w["ab_out"], h1)
    h3, post0 = ffn_forward("post0", h2, R_POST)
    h4, pre1 = ffn_forward("pre1", h3, R_PRE + 1)
    hn_c = _norm_fwd("mix1_norm", h4, small, R_MIX + 1)
    proj_c = _mm("c_proj", [(hn_c, tok_si, w["c_in"], _spec((None, D, D), lambda s, i: (s, 0, 0)))],
                 grid=(NSH, nt), o_shape=(NSH, t, D), o_dtype=F32, o_spec=_spec((None, tm, D), lambda s, i: (s, i, 0)),
                 dims=NN, kaxis=1, nk=1)
    names, gather = carried("hgrn_fwd")
    o_c, y_c, sst, got = _hgrn_fwd(proj_c, small, gather)
    land(names, got)
    h5 = out_proj("c_out", y_c, w["c_out"], h4)
    h6, post1 = ffn_forward("post1", h5, R_POST + 1)
    dh, dhb, d_fin, loss = _final_loss(h6, small, target)

    dh, dhb, grad_a, grad_b, dn_post1, _, _ = _ffn_backward("post1", dh, dhb, *post1, w["post1_gu"], w["post1_d"],
                                                            *FFN_SLOTS["post1"], small, R_POST + 1, 8 * FS, B_ROWS)
    dy_c, grad_b, swapped = out_proj_bwd("c", dhb, y_c, w["c_out"], B_COUT // 256, grad_b,
                                         lambda buf_b: swap_rider("x", dict(a=grad_a, b=buf_b)))
    pairs, rider = reduce_start("x", dict(a=grad_a, b=grad_b), swapped)
    dproj_c, d_clb, d_gam, landed = _hgrn_bwd(proj_c, small, o_c, sst, dy_c, rider)
    share = reduce_end("x", pairs, landed)
    grad_b = _wgrad("c_dwin", hn_c, _spec((tw, D), lambda s, k: (k, 0)), dproj_c, _spec((None, tw, D), lambda s, k: (s, k, 0)),
                    D, D, t, tw, grad_b, B_CIN // D, carry=share)
    if share is not None:
        grad_b, landed = grad_b
        shared("x", landed)
    dh, dhb, dn_mix1, _ = _dhn_norm("c_dhn", [(dproj_c, _spec((None, tm, D), lambda i, k: (k, i, 0)),
                                               w["c_in"], _spec((None, D, D), lambda i, k: (k, 0, 0)))],
                                    NT, h4, small, R_MIX + 1, dh)
    dh, dhb, grad_a, grad_b, dn_pre1, _, _ = _ffn_backward("pre1", dh, dhb, *pre1, w["pre1_gu"], w["pre1_d"],
                                                           *FFN_SLOTS["pre1"], small, R_PRE + 1, grad_a, grad_b)
    dh, dhb, grad_a, grad_b, dn_post0, _, _ = _ffn_backward("post0", dh, dhb, *post0, w["post0_gu"], w["post0_d"],
                                                            *FFN_SLOTS["post0"], small, R_POST, grad_a, grad_b)
    dy_ab, grad_b, _ = out_proj_bwd("ab", dhb, y_ab, w["ab_out"], B_ABOUT // 256, grad_b)
    dab, dac, dax, d_conv, swapped = _conv_bwd(proj_a, w["conv"], dy_ab, swap_rider("y", dict(a=grad_a, b=grad_b)))
    pairs, rider = reduce_start("y", dict(a=grad_a, b=grad_b), swapped)
    dq, dk, dv, landed = _sb_bwd(proj_b, dy_ab, ltot, nblk, rider)
    share = reduce_end("y", pairs, landed)
    dproj_ab = jnp.concatenate([dab, dac, dax, dq, dk, dv], axis=1)
    grad_c = _wgrad("ab_dwin", hn_ab, _spec((tw, D), lambda s, k: (k, 0)), dproj_ab, _spec((tw, 768), lambda s, k: (k, s)),
                    D, 768, t, tw, D, 0, carry=share)
    if share is not None:
        grad_c, landed = grad_c
        shared("y", landed)
    dh, dhb, dn_mix0, _ = _dhn_norm("ab_dhn", [(dproj_ab, _spec((tm, 768), lambda i, k: (i, k)),
                                                w["ab_in"], _spec((None, D, 768), lambda i, k: (k, 0, 0)))],
                                    NT, h1, small, R_MIX, dh)
    pairs, rider = reduce_start("z1", dict(c=grad_c))
    last = {}

    def own_gradients(buf_a, buf_b):
        last["z2"] = reduce_start("z2", dict(a=buf_a, b=buf_b))
        return last["z2"][1]

    dh, dhb, grad_a, grad_b, dn_pre0, landed, landed_own = _ffn_backward(
        "pre0", dh, dhb, *pre0, w["pre0_gu"], w["pre0_d"], *FFN_SLOTS["pre0"], small, R_PRE, grad_a, grad_b,
        rider, own_gradients if place is not None else None)
    if place is not None:
        mine = (_owner_sum("grad_owner_sum_z1", pairs, landed, place[1])
                + _owner_sum("grad_owner_sum_z2", last["z2"][0], landed_own, place[1]))
        reduced.update(zip(REDUCE_STAGES["z1"] + REDUCE_STAGES["z2"], _alone("grad_share_z", _Share(mine))))
    zero = jnp.zeros((1, D), F32)
    conv_rows = jnp.pad(jnp.transpose(d_conv[:, :3, :], (1, 0, 2)).reshape(3, 512), ((0, 0), (0, D - 512)))
    small_grad = jnp.concatenate([
        dn_pre0, dn_pre1, dn_mix0, dn_mix1, dn_post0, dn_post1, d_clb, d_fin,
        jnp.pad(d_gam, ((0, 0), (0, D - HD))), conv_rows,
        jnp.pad(loss, ((0, 0), (0, D - 1))), zero, zero], axis=0)
    return dh, grad_a, grad_b, grad_c, small_grad, reduced


def _small_slab(rows):
    parts = [jnp.pad(r.astype(F32), ((0, 0), (0, D - r.shape[1]))) for r in rows]
    slab = jnp.concatenate(parts, axis=0)
    return jnp.pad(slab, ((0, SMALL_ROWS - slab.shape[0]), (0, 0)))


def kernel(x, ffn_pre_norm, ffn_pre_w_gate, ffn_pre_w_up, ffn_pre_w_down, mix_norm, ffn_post_norm, ffn_post_w_gate, ffn_post_w_up, ffn_post_w_down, ab_w_in, ab_conv_w, ab_w_out, c_w_in, c_lower_bounds, c_out_norm, c_w_out, final_norm, loss_target, m_ffn_pre_norm, m_ffn_pre_w_gate, m_ffn_pre_w_up, m_ffn_pre_w_down, m_mix_norm, m_ffn_post_norm, m_ffn_post_w_gate, m_ffn_post_w_up, m_ffn_post_w_down, m_ab_w_in, m_ab_conv_w, m_ab_w_out, m_c_w_in, m_c_lower_bounds, m_c_out_norm, m_c_w_out, m_final_norm, v_ffn_pre_norm, v_ffn_pre_w_gate, v_ffn_pre_w_up, v_ffn_pre_w_down, v_mix_norm, v_ffn_post_norm, v_ffn_post_w_gate, v_ffn_post_w_up, v_ffn_post_w_down, v_ab_w_in, v_ab_conv_w, v_ab_w_out, v_c_w_in, v_c_lower_bounds, v_c_out_norm, v_c_w_out, v_final_norm):
    t = x.shape[1]
    xi, yi, ci = lax.axis_index("x"), lax.axis_index("y"), lax.axis_index("c")
    p_idx = (2 * xi + yi).astype(jnp.int32).reshape(1)
    c_idx = ci.astype(jnp.int32).reshape(1)

    def halves(m):
        return m.astype(BF16).reshape(2, m.shape[0] // 2, m.shape[1])

    transposed = ("ffn_pre_w_gate", "ffn_pre_w_up", "ffn_post_w_gate", "ffn_post_w_up")

    def flip(a):
        return jnp.swapaxes(a, 1, 2)

    shards = {}
    for name, (w_gate, w_up, w_down, layer) in dict(
            pre0=(ffn_pre_w_gate, ffn_pre_w_up, ffn_pre_w_down, 0), post0=(ffn_post_w_gate, ffn_post_w_up, ffn_post_w_down, 0),
            pre1=(ffn_pre_w_gate, ffn_pre_w_up, ffn_pre_w_down, 1), post1=(ffn_post_w_gate, ffn_post_w_up, ffn_post_w_down, 1)).items():
        shards[name + "_gu"] = jnp.stack([flip(w_gate)[layer], flip(w_up)[layer]]).astype(BF16)
        shards[name + "_d"] = halves(w_down[layer])
    conv_pad = jnp.pad(ab_conv_w[0], ((0, 5), (0, 0)))
    shards.update(ab_in=halves(ab_w_in[0]), ab_out=halves(ab_w_out[0]), c_in=halves(c_w_in[0]), c_out=halves(c_w_out[0]),
                  conv=jnp.stack([conv_pad, jnp.zeros_like(conv_pad)]))
    first = _alone("gather_weights", _Gather([shards["pre0_gu"]]))[0].reshape(W_SHAPES["pre0_gu"])

    small = _small_slab([ffn_pre_norm, mix_norm, ffn_post_norm, c_lower_bounds, final_norm.reshape(1, D), c_out_norm])
    small = small.reshape(SMALL_ROWS, 1, D)

    grad_x, _, _, _, small_grad, reduced = _local_step(x[0], loss_target[0], dict(pre0_gu=first), small, shards,
                                                        (c_idx, p_idx))
    whole = {n: g.reshape(2 * g.shape[1], g.shape[2]) for n, g in reduced.items()}
    small_sum = _sum_small(small_grad)

    my_conv = lax.dynamic_slice(small_sum[R_CONV:R_CONV + 3], (0, (2 * xi + yi) * 128), (3, 128))
    grads = {
        "ffn_pre_norm": small_sum[R_PRE:R_PRE + 2], "mix_norm": small_sum[R_MIX:R_MIX + 2],
        "ffn_post_norm": small_sum[R_POST:R_POST + 2], "c_lower_bounds": small_sum[R_CLB:R_CLB + 2],
        "c_out_norm": small_sum[R_GAM:R_GAM + 1, :HD], "final_norm": small_sum[R_FIN],
        "ab_conv_w": my_conv.reshape(1, 3, 128),
        "ab_w_in": whole["ab_in"][None], "ab_w_out": whole["ab_out"][None],
        "c_w_in": whole["c_in"][None], "c_w_out": whole["c_out"][None],
    }
    for kind, key in (("gate", "_g"), ("up", "_u"), ("down", "_d")):
        grads["ffn_pre_w_" + kind] = jnp.stack([whole["pre0" + key], whole["pre1" + key]])
        grads["ffn_post_w_" + kind] = jnp.stack([whole["post0" + key], whole["post1" + key]])
    weights = dict(ffn_pre_norm=ffn_pre_norm, ffn_pre_w_gate=ffn_pre_w_gate, ffn_pre_w_up=ffn_pre_w_up, ffn_pre_w_down=ffn_pre_w_down, mix_norm=mix_norm, ffn_post_norm=ffn_post_norm, ffn_post_w_gate=ffn_post_w_gate, ffn_post_w_up=ffn_post_w_up, ffn_post_w_down=ffn_post_w_down, ab_w_in=ab_w_in, ab_conv_w=ab_conv_w, ab_w_out=ab_w_out, c_w_in=c_w_in, c_lower_bounds=c_lower_bounds, c_out_norm=c_out_norm, c_w_out=c_w_out, final_norm=final_norm)
    m_in = dict(ffn_pre_norm=m_ffn_pre_norm, ffn_pre_w_gate=m_ffn_pre_w_gate, ffn_pre_w_up=m_ffn_pre_w_up, ffn_pre_w_down=m_ffn_pre_w_down, mix_norm=m_mix_norm, ffn_post_norm=m_ffn_post_norm, ffn_post_w_gate=m_ffn_post_w_gate, ffn_post_w_up=m_ffn_post_w_up, ffn_post_w_down=m_ffn_post_w_down, ab_w_in=m_ab_w_in, ab_conv_w=m_ab_conv_w, ab_w_out=m_ab_w_out, c_w_in=m_c_w_in, c_lower_bounds=m_c_lower_bounds, c_out_norm=m_c_out_norm, c_w_out=m_c_w_out, final_norm=m_final_norm)
    v_in = dict(ffn_pre_norm=v_ffn_pre_norm, ffn_pre_w_gate=v_ffn_pre_w_gate, ffn_pre_w_up=v_ffn_pre_w_up, ffn_pre_w_down=v_ffn_pre_w_down, mix_norm=v_mix_norm, ffn_post_norm=v_ffn_post_norm, ffn_post_w_gate=v_ffn_post_w_gate, ffn_post_w_up=v_ffn_post_w_up, ffn_post_w_down=v_ffn_post_w_down, ab_w_in=v_ab_w_in, ab_conv_w=v_ab_conv_w, ab_w_out=v_ab_w_out, c_w_in=v_c_w_in, c_lower_bounds=v_c_lower_bounds, c_out_norm=v_c_out_norm, c_w_out=v_c_w_out, final_norm=v_final_norm)
    names = list(weights)
    big = [n for n in names if weights[n].size >= 65536]
    tiny = [n for n in names if n not in big]

    delta, new_m, new_v = {}, {}, {}
    for n in big:
        turn = flip if n in transposed else (lambda a: a)
        shape = turn(weights[n]).shape
        two_d = (shape[0] * shape[1], shape[2])
        d, m2, v2 = _adamw("adamw_" + n, turn(weights[n]).reshape(two_d), grads[n].reshape(two_d),
                           turn(m_in[n]).reshape(two_d), turn(v_in[n]).reshape(two_d))
        delta[n], new_m[n], new_v[n] = turn(d.reshape(shape)), turn(m2.reshape(shape)), turn(v2.reshape(shape))
        grads[n] = turn(grads[n])

    def tiny_slab(src):
        return _small_slab([src[n].reshape(-1, src[n].shape[-1]) for n in tiny])

    offs, row = {}, 0
    for n in tiny:
        nrows = weights[n].size // weights[n].shape[-1]
        offs[n] = (row, nrows)
        row += nrows
    d, m2, v2 = _adamw("adamw_small", tiny_slab(weights), tiny_slab(grads), tiny_slab(m_in), tiny_slab(v_in))
    for n in tiny:
        r0, nr = offs[n]
        shape = weights[n].shape
        for dst, src in ((delta, d), (new_m, m2), (new_v, v2)):
            dst[n] = src[r0:r0 + nr, :shape[-1]].reshape(shape)

    loss = small_sum[R_LOSS, 0]
    return (loss, grad_x.reshape(1, t, D), *[grads[n] for n in names], *[delta[n] for n in names],
            *[new_m[n] for n in names], *[new_v[n] for n in names])
```

```python
import functools
import math

import jax
import jax.numpy as jnp
from jax import lax
from jax.experimental import pallas as pl
from jax.experimental.pallas import tpu as pltpu

F32 = jnp.float32
BF16 = jnp.bfloat16
MESH = pl.DeviceIdType.MESH
ANY = pl.BlockSpec(memory_space=pl.ANY)

D = 1024
FS = 704
NSH = 4
RMS_EPS = 1e-6
MACARON = 0.5
CHUNK = 64
HD = 128
HGRN_HPS = 8
SBQ = 128
SB_PAIRS_FWD = 4
SB_PAIRS_BWD = 2
SB_DEAD = -105.0
CONV_HALO = 8
LANE = 128
ROW_TILE = 512
MM_TILE = 1024
WGRAD_TILE = 4096
VMEM_LIMIT = 48 * 1024 * 1024
VMEM_LIMIT_BIG = 58 * 1024 * 1024

ADAM_LR, ADAM_B1, ADAM_B2, ADAM_EPS, ADAM_WD, ADAM_STEP = 0.001, 0.9, 0.999, 1e-08, 0.01, 10

NN = ((1,), (0,))
NT = ((1,), (1,))
TN = ((0,), (0,))

SMALL_ROWS = 16
R_PRE, R_MIX, R_POST, R_CLB, R_FIN, R_GAM, R_CONV, R_LOSS = 0, 2, 4, 6, 8, 9, 10, 13

B_DOWN = 0
B_ABOUT = 4 * FS
B_CIN = B_ABOUT + 256
B_COUT = B_CIN + 1024
B_ROWS = B_COUT + 256


def _dg(a, b, dims):
    return lax.dot_general(a, b, (dims, ((), ())), preferred_element_type=F32)


def _split(x):
    hi = x.astype(BF16)
    lo = (x - hi.astype(F32)).astype(BF16)
    return hi, lo


def _dot3(a, b, dims):
    ah, al = _split(a)
    bh, bl = _split(b)
    if dims == TN:
        n = b.shape[1]
        both = _dg(ah, jnp.concatenate([bh, bl], axis=1), dims)
        return both[:, :n] + both[:, n:] + _dg(al, bh, dims)
    m = a.shape[0]
    both = _dg(jnp.concatenate([ah, al], axis=0), bh, dims)
    return both[:m] + both[m:] + _dg(ah, bl, dims)


def _sigmoid(x):
    return 1.0 / (1.0 + jnp.exp(-x))


def _params(sem):
    return pltpu.CompilerParams(dimension_semantics=sem, vmem_limit_bytes=VMEM_LIMIT)


def _spec(shape, imap):
    return pl.BlockSpec(shape, imap)


def _accumulate(acc_ref, pairs, dims):
    part = None
    for a_ref, b_ref in pairs:
        d = _dg(a_ref[...], b_ref[...], dims)
        part = d if part is None else part + d
    acc_ref[...] += part


def _mm(name, pairs, *, grid, o_shape, o_dtype, o_spec, dims, kaxis, nk, acc_shape=None, res=None, scale=None,
        into=None, carry=None, norm=None):
    npairs = len(pairs)
    operands, specs = [], []
    for a, a_spec, b, b_spec in pairs:
        operands += [a, b]
        specs += [a_spec, b_spec]
    if res is not None:
        operands.append(res[0])
        specs.append(res[1])
    aliases = {}
    if into is not None:
        aliases = {len(operands): 0}
        operands.append(into)
        specs.append(ANY)
    if norm is not None:
        operands.append(norm[0])
        specs.append(_spec((None, 1, D), lambda *_: (norm[1], 0, 0)))
    n_own = len(operands)
    n_out = 2 if norm is not None else 1
    nc = carry.n if carry is not None else 0
    if nc:
        operands += carry.groups
        specs += carry.in_specs

    def body(*refs):
        o_ref = refs[n_own + nc]
        riders = ((refs[n_own:n_own + nc], refs[n_own + nc + n_out:n_own + 2 * nc + n_out], refs[-2], refs[-1])
                  if nc else None)
        if nc:
            carry.ride(grid, riders, "start")

        def finish(val):
            if scale is not None:
                val = val * scale
            if res is not None:
                val = val + refs[2 * npairs][...]
            o_ref[...] = val.astype(o_dtype)
            if norm is not None:
                r = lax.rsqrt(jnp.mean(val * val, axis=-1, keepdims=True) + RMS_EPS)
                refs[n_own + nc + 1][...] = (val * r * refs[n_own - 1][...]).astype(BF16)

        if nk == 1:
            part = None
            for n in range(npairs):
                d = _dg(refs[2 * n][...], refs[2 * n + 1][...], dims)
                part = d if part is None else part + d
            finish(part)
        else:
            acc_ref = refs[n_own + 2 * nc + n_out]
            k = pl.program_id(kaxis)

            @pl.when(k == 0)
            def _():
                acc_ref[...] = jnp.zeros_like(acc_ref)

            _accumulate(acc_ref, [(refs[2 * n], refs[2 * n + 1]) for n in range(npairs)], dims)

            @pl.when(k == nk - 1)
            def _():
                finish(acc_ref[...])
        if nc:
            carry.ride(grid, riders, "finish")

    sem = tuple("arbitrary" if (nc or (ax == kaxis and nk > 1)) else "parallel" for ax in range(len(grid)))
    outs = pl.pallas_call(
        body, name=name, grid=grid, in_specs=specs,
        out_specs=[o_spec] * n_out + (carry.out_specs if nc else []),
        out_shape=[jax.ShapeDtypeStruct(o_shape, o_dtype)] + ([jax.ShapeDtypeStruct(o_shape, BF16)] if norm is not None else [])
        + (carry.out_shape if nc else []),
        scratch_shapes=([pltpu.VMEM(acc_shape, F32)] if nk > 1 else []) + (carry.scratch if nc else []),
        input_output_aliases=aliases,
        compiler_params=_params(sem),
    )(*operands)
    main = (outs[0], outs[1]) if norm is not None else outs[0]
    return (main, carry.place(outs[n_out:])) if nc else main


def _norm_fwd(name, h, gain_slab, row):
    t = h.shape[0]
    tm = min(ROW_TILE, t)

    def body(h_ref, g_ref, o_ref):
        x = h_ref[...]
        r = lax.rsqrt(jnp.mean(x * x, axis=-1, keepdims=True) + RMS_EPS)
        o_ref[...] = (x * r * g_ref[...]).astype(BF16)

    return pl.pallas_call(
        body, name=name, grid=(t // tm,),
        in_specs=[_spec((tm, D), lambda i: (i, 0)), _spec((None, 1, D), lambda i: (row, 0, 0))],
        out_specs=_spec((tm, D), lambda i: (i, 0)),
        out_shape=jax.ShapeDtypeStruct((t, D), BF16),
        compiler_params=_params(("parallel",)),
    )(h, gain_slab)


def _dhn_norm(name, pairs, dims, h, gain_slab, row, dres, rider=None):
    t = h.shape[0]
    tm = min(MM_TILE, t)
    nt = t // tm
    grid = (nt, NSH)
    npairs = len(pairs)
    nr = rider.n if rider is not None else 0
    operands, specs = [], []
    for a, a_spec, b, b_spec in pairs:
        operands += [a, b]
        specs += [a_spec, b_spec]
    row_spec = _spec((tm, D), lambda i, k: (i, 0))
    operands += [h, gain_slab, dres]
    specs += [row_spec, _spec((None, 1, D), lambda i, k: (row, 0, 0)), row_spec]
    n_own = len(operands)

    def body(*refs):
        h_ref, g_ref, dres_ref = refs[2 * npairs:n_own]
        dh_ref, dhb_ref, dg_ref = refs[n_own + nr:n_own + nr + 3]
        acc_ref, gacc_ref = refs[n_own + 2 * nr + 3:n_own + 2 * nr + 5]
        riders = (refs[n_own:n_own + nr], refs[n_own + nr + 3:n_own + 2 * nr + 3], refs[-2], refs[-1]) if nr else None
        if nr:
            rider.ride(grid, riders, "start")
        i, k = pl.program_id(0), pl.program_id(1)

        @pl.when(k == 0)
        def _():
            acc_ref[...] = jnp.zeros_like(acc_ref)

        _accumulate(acc_ref, [(refs[2 * n], refs[2 * n + 1]) for n in range(npairs)], dims)

        @pl.when(k == NSH - 1)
        def _():
            x = h_ref[...]
            r = lax.rsqrt(jnp.mean(x * x, axis=-1, keepdims=True) + RMS_EPS)
            xh = x * r
            dy = acc_ref[...]
            gdy = dy * g_ref[...]
            dh = dres_ref[...] + (gdy - xh * jnp.mean(gdy * xh, axis=-1, keepdims=True)) * r
            dh_ref[...] = dh
            dhb_ref[...] = dh.astype(BF16)
            gpart = jnp.sum((dy * xh).reshape(tm // 8, 8, D), axis=0)

            @pl.when(i == 0)
            def _():
                gacc_ref[...] = gpart

            @pl.when(i > 0)
            def _():
                gacc_ref[...] += gpart

            @pl.when(i == nt - 1)
            def _():
                dg_ref[...] = jnp.sum(gacc_ref[...], axis=0, keepdims=True)

        if nr:
            rider.ride(grid, riders, "finish")

    outs = pl.pallas_call(
        body, name=name, grid=grid, in_specs=specs + (rider.in_specs if nr else []),
        out_specs=[row_spec, row_spec, _spec((1, D), lambda i, k: (0, 0))] + (rider.out_specs if nr else []),
        out_shape=[jax.ShapeDtypeStruct((t, D), F32), jax.ShapeDtypeStruct((t, D), BF16),
                   jax.ShapeDtypeStruct((1, D), F32)] + (rider.out_shape if nr else []),
        scratch_shapes=[pltpu.VMEM((tm, D), F32), pltpu.VMEM((8, D), F32)] + (rider.scratch if nr else []),
        compiler_params=pltpu.CompilerParams(dimension_semantics=("arbitrary", "arbitrary"),
                                             vmem_limit_bytes=VMEM_LIMIT_BIG),
    )(*operands, *(rider.groups if nr else []))
    return outs[0], outs[1], outs[2], (rider.place(outs[3:]) if nr else [])


def _final_loss(h, gain_slab, target):
    t = h.shape[0]
    tm = min(ROW_TILE, t)
    nt = t // tm

    def body(h_ref, g_ref, t_ref, dh_ref, dhb_ref, dg_ref, loss_ref, acc_ref, lacc_ref):
        i = pl.program_id(0)
        x = h_ref[...]
        g = g_ref[...]
        r = lax.rsqrt(jnp.mean(x * x, axis=-1, keepdims=True) + RMS_EPS)
        xh = x * r
        err = xh * g - t_ref[...]
        dy = err * (1.0 / D)
        gdy = dy * g
        dh = (gdy - xh * jnp.mean(gdy * xh, axis=-1, keepdims=True)) * r
        dh_ref[...] = dh
        dhb_ref[...] = dh.astype(BF16)
        part = jnp.sum((dy * xh).reshape(tm // 8, 8, D), axis=0)
        lpart = jnp.sum((err * err).reshape(tm // 8, 8, D), axis=0)

        @pl.when(i == 0)
        def _():
            acc_ref[...] = part
            lacc_ref[...] = lpart

        @pl.when(i > 0)
        def _():
            acc_ref[...] += part
            lacc_ref[...] += lpart

        @pl.when(i == nt - 1)
        def _():
            dg_ref[...] = jnp.sum(acc_ref[...], axis=0, keepdims=True)
            rows = jnp.sum(lacc_ref[...], axis=0, keepdims=True)
            loss_ref[...] = jnp.sum(rows, axis=1, keepdims=True) * (0.5 / D)

    row_spec = _spec((tm, D), lambda i: (i, 0))
    return pl.pallas_call(
        body, name="final_loss", grid=(nt,),
        in_specs=[row_spec, _spec((None, 1, D), lambda i: (R_FIN, 0, 0)), row_spec],
        out_specs=[row_spec, row_spec, _spec((1, D), lambda i: (0, 0)), _spec((1, 1), lambda i: (0, 0))],
        out_shape=[jax.ShapeDtypeStruct((t, D), F32), jax.ShapeDtypeStruct((t, D), BF16),
                   jax.ShapeDtypeStruct((1, D), F32), jax.ShapeDtypeStruct((1, 1), F32)],
        scratch_shapes=[pltpu.VMEM((8, D), F32), pltpu.VMEM((8, D), F32)],
        compiler_params=_params(("arbitrary",)),
    )(h, gain_slab, target)


def _ffn_up(name, hn, wgu, carry=None):
    t = hn.shape[0]
    tm = min(MM_TILE, t)
    grid = (NSH, t // tm)
    nc = carry.n if carry is not None else 0

    def body(*refs):
        x_ref, wg_ref, wu_ref = refs[:3]
        s_up_ref, s_gate_ref, a_ref = refs[3 + nc:6 + nc]
        riders = (refs[3:3 + nc], refs[6 + nc:6 + 2 * nc], refs[-2], refs[-1]) if nc else None
        if nc:
            carry.ride(grid, riders, "start")
        x = x_ref[...]
        g = _dg(x, wg_ref[...], NT)
        u = _dg(x, wu_ref[...], NT)
        sg = _sigmoid(g)
        silu = g * sg
        s_up_ref[...] = (MACARON * silu).astype(BF16)
        s_gate_ref[...] = (MACARON * u * (sg * (1.0 + g * (1.0 - sg)))).astype(BF16)
        a_ref[...] = (silu * u).astype(BF16)
        if nc:
            carry.ride(grid, riders, "finish")

    act = _spec((None, tm, FS), lambda s, i: (s, i, 0))
    shape = jax.ShapeDtypeStruct((NSH, t, FS), BF16)
    outs = pl.pallas_call(
        body, name=name, grid=grid,
        in_specs=[_spec((tm, D), lambda s, i: (i, 0)),
                  _spec((None, None, FS, D), lambda s, i: (s, 0, 0, 0)),
                  _spec((None, None, FS, D), lambda s, i: (s, 1, 0, 0))] + (carry.in_specs if nc else []),
        out_specs=[act, act, act] + (carry.out_specs if nc else []),
        out_shape=[shape, shape, shape] + (carry.out_shape if nc else []),
        scratch_shapes=carry.scratch if nc else [],
        compiler_params=_params(("arbitrary", "arbitrary") if nc else ("parallel", "parallel")),
    )(hn, wgu, wgu, *(carry.groups if nc else []))
    return (outs[0], outs[1], outs[2], carry.place(outs[3:]) if nc else [])


def _ffn_down(name, a, wd, h, carry=None, norm=None):
    t = h.shape[0]
    tm = min(MM_TILE, t)
    return _mm(name, [(a, _spec((None, tm, FS), lambda i, k: (k, i, 0)),
                       wd, _spec((None, FS, D), lambda i, k: (k, 0, 0)))],
               grid=(t // tm, NSH), o_shape=(t, D), o_dtype=F32, o_spec=_spec((tm, D), lambda i, k: (i, 0)),
               dims=NN, kaxis=1, nk=NSH, acc_shape=(tm, D), res=(h, _spec((tm, D), lambda i, k: (i, 0))),
               scale=MACARON, carry=carry, norm=norm)


def _ffn_bwd_up(name, dhb, wd, s_up, s_gate, rider=None):
    t = dhb.shape[0]
    tm = min(MM_TILE, t)
    grid = (NSH, t // tm)
    nr = rider.n if rider is not None else 0

    def body(*refs):
        dh_ref, wd_ref, s_up_ref, s_gate_ref = refs[:4]
        dg_ref, du_ref = refs[4 + nr:6 + nr]
        riders = (refs[4:4 + nr], refs[6 + nr:6 + 2 * nr], refs[-2], refs[-1]) if nr else None
        if nr:
            rider.ride(grid, riders, "start")
        da = _dg(dh_ref[...], wd_ref[...], NT).astype(BF16)
        du_ref[...] = da * s_up_ref[...]
        dg_ref[...] = da * s_gate_ref[...]
        if nr:
            rider.ride(grid, riders, "finish")

    act = _spec((None, tm, FS), lambda s, i: (s, i, 0))
    shape = jax.ShapeDtypeStruct((NSH, t, FS), BF16)
    outs = pl.pallas_call(
        body, name=name, grid=grid,
        in_specs=[_spec((tm, D), lambda s, i: (i, 0)), _spec((None, FS, D), lambda s, i: (s, 0, 0)), act, act]
        + (rider.in_specs if nr else []),
        out_specs=[act, act] + (rider.out_specs if nr else []),
        out_shape=[shape, shape] + (rider.out_shape if nr else []),
        scratch_shapes=rider.scratch if nr else [],
        compiler_params=_params(("arbitrary", "arbitrary") if nr else ("parallel", "parallel")),
    )(dhb, wd, s_up, s_gate, *(rider.groups if nr else []))
    return outs[0], outs[1], (rider.place(outs[2:]) if nr else [])


def _wgrad(name, a, a_spec, b, b_spec, out_rows, out_cols, t, tt, group, slot, scale=None, carry=None):
    first = isinstance(group, int)
    rows = group if first else group.shape[1]
    return _mm(name, [(a, a_spec, b, b_spec)], grid=(NSH, t // tt),
               o_shape=(NSH, rows, out_cols), o_dtype=BF16,
               o_spec=_spec((None, out_rows, out_cols), lambda s, k: (s, slot, 0)),
               dims=TN, kaxis=1, nk=t // tt, acc_shape=(out_rows, out_cols), scale=scale,
               into=None if first else group, carry=carry)


def _ffn_backward(tag, dh, dhb, h_in, hn, s_up, s_gate, a, wgu, wd, gate_idx, up_idx, down_idx, small, norm_row,
                  grad_a, grad_b, rider_up=None, rider_dhn=None):
    t = dh.shape[0]
    tm = min(MM_TILE, t)
    tt = min(WGRAD_TILE, t)
    tok = _spec((tt, D), lambda s, k: (k, 0))
    hid = _spec((None, tt, FS), lambda s, k: (s, k, 0))
    grad_b = _wgrad(tag + "_dwd", a, hid, dhb, tok, FS, D, t, tt, grad_b, down_idx, scale=MACARON)
    dg, du, landed_up = _ffn_bwd_up(tag + "_bwd_up", dhb, wd, s_up, s_gate, rider_up)
    grad_a = _wgrad(tag + "_dwg", dg, hid, hn, tok, FS, D, t, tt, grad_a, gate_idx)
    grad_a = _wgrad(tag + "_dwu", du, hid, hn, tok, FS, D, t, tt, grad_a, up_idx)
    rider = rider_dhn(grad_a, grad_b) if rider_dhn is not None else None
    act = _spec((None, tm, FS), lambda i, k: (k, i, 0))
    dh_in, dhb_in, d_gain, landed_dhn = _dhn_norm(
        tag + "_dhn", [(dg, act, wgu, _spec((None, None, FS, D), lambda i, k: (k, 0, 0, 0))),
                       (du, act, wgu, _spec((None, None, FS, D), lambda i, k: (k, 1, 0, 0)))],
        NN, h_in, small, norm_row, dh, rider)
    return dh_in, dhb_in, grad_a, grad_b, d_gain, landed_up, landed_dhn


def _conv_fwd(proj_a, conv_w):
    t = proj_a.shape[0]
    tm = min(ROW_TILE, t)
    hb = tm // CONV_HALO

    def body(ab_ref, ac_ref, ax_ref, acp_ref, axp_ref, w_ref, y_ref):
        i = pl.program_id(1)
        u = ac_ref[...] * ax_ref[...]
        up = jnp.where(i > 0, acp_ref[...] * axp_ref[...], 0.0)
        ext = jnp.concatenate([up, u], axis=0)
        u1 = pltpu.roll(ext, 1, 0)[CONV_HALO:]
        u2 = pltpu.roll(ext, 2, 0)[CONV_HALO:]
        w = w_ref[...]
        conv = w[0:1] * u2 + w[1:2] * u1 + w[2:3] * u
        y_ref[...] = (ab_ref[...] * conv).astype(BF16)

    def cur(off):
        return _spec((tm, LANE), lambda j, i: (i, off + j))

    def prev(off):
        return _spec((CONV_HALO, LANE), lambda j, i: (jnp.maximum(i * hb - 1, 0), off + j))

    return pl.pallas_call(
        body, name="conv_fwd", grid=(4, t // tm),
        in_specs=[cur(0), cur(4), cur(8), prev(4), prev(8),
                  _spec((None, None, 8, LANE), lambda j, i: (j, 0, 0, 0))],
        out_specs=_spec((tm, LANE), lambda j, i: (i, j)),
        out_shape=jax.ShapeDtypeStruct((t, 512), BF16),
        compiler_params=_params(("parallel", "parallel")),
    )(proj_a, proj_a, proj_a, proj_a, proj_a, conv_w)


def _conv_bwd(proj_a, conv_w, dy, rider=None):
    t = proj_a.shape[0]
    tm = min(ROW_TILE, t)
    hb = tm // CONV_HALO
    nt = t // tm
    grid = (4, nt)
    nr = rider.n if rider is not None else 0

    def body(*refs):
        ab_ref, ac_ref, ax_ref, dy_ref, acp_ref, axp_ref, abn_ref, dyn_ref, w_ref = refs[:9]
        dab_ref, dac_ref, dax_ref, dw_ref = refs[9 + nr:13 + nr]
        acc_ref = refs[13 + 2 * nr]
        riders = (refs[9:9 + nr], refs[13 + nr:13 + 2 * nr], refs[-2], refs[-1]) if nr else None
        if nr:
            rider.ride(grid, riders, "start")
        i = pl.program_id(1)
        ab, ac, ax = ab_ref[...], ac_ref[...], ax_ref[...]
        u = ac * ax
        up = jnp.where(i > 0, acp_ref[...] * axp_ref[...], 0.0)
        ext = jnp.concatenate([up, u], axis=0)
        u1 = pltpu.roll(ext, 1, 0)[CONV_HALO:]
        u2 = pltpu.roll(ext, 2, 0)[CONV_HALO:]
        w = w_ref[...]
        conv = w[0:1] * u2 + w[1:2] * u1 + w[2:3] * u
        dy_v = dy_ref[...]
        dab_ref[...] = (dy_v * conv).astype(BF16)
        dc = dy_v * ab
        dcn = jnp.where(i < nt - 1, dyn_ref[...] * abn_ref[...], 0.0)
        extn = jnp.concatenate([dc, dcn], axis=0)
        n = tm + CONV_HALO
        dc1 = pltpu.roll(extn, n - 1, 0)[:tm]
        dc2 = pltpu.roll(extn, n - 2, 0)[:tm]
        du = w[2:3] * dc + w[1:2] * dc1 + w[0:1] * dc2
        dac_ref[...] = (du * ax).astype(BF16)
        dax_ref[...] = (du * ac).astype(BF16)
        rid = lax.broadcasted_iota(jnp.int32, (8, LANE), 0)
        part = jnp.where(rid == 0, jnp.sum(dc * u2, axis=0, keepdims=True),
                         jnp.where(rid == 1, jnp.sum(dc * u1, axis=0, keepdims=True),
                                   jnp.where(rid == 2, jnp.sum(dc * u, axis=0, keepdims=True), 0.0)))

        @pl.when(i == 0)
        def _():
            acc_ref[...] = part

        @pl.when(i > 0)
        def _():
            acc_ref[...] += part

        @pl.when(i == nt - 1)
        def _():
            dw_ref[...] = acc_ref[...]

        if nr:
            rider.ride(grid, riders, "finish")

    def cur(off):
        return _spec((tm, LANE), lambda j, i: (i, off + j))

    def prev(off):
        return _spec((CONV_HALO, LANE), lambda j, i: (jnp.maximum(i * hb - 1, 0), off + j))

    def nxt(off):
        return _spec((CONV_HALO, LANE), lambda j, i: (jnp.minimum((i + 1) * hb, nt * hb - 1), off + j))

    outs = pl.pallas_call(
        body, name="conv_bwd", grid=grid,
        in_specs=[cur(0), cur(4), cur(8), cur(0), prev(4), prev(8), nxt(0), nxt(0),
                  _spec((None, None, 8, LANE), lambda j, i: (j, 0, 0, 0))] + (rider.in_specs if nr else []),
        out_specs=[_spec((tm, LANE), lambda j, i: (i, j)), _spec((tm, LANE), lambda j, i: (i, j)),
                   _spec((tm, LANE), lambda j, i: (i, j)), _spec((None, 8, LANE), lambda j, i: (j, 0, 0))]
        + (rider.out_specs if nr else []),
        out_shape=[jax.ShapeDtypeStruct((t, 512), BF16), jax.ShapeDtypeStruct((t, 512), BF16),
                   jax.ShapeDtypeStruct((t, 512), BF16), jax.ShapeDtypeStruct((4, 8, LANE), F32)]
        + (rider.out_shape if nr else []),
        scratch_shapes=[pltpu.VMEM((8, LANE), F32)] + (rider.scratch if nr else []),
        compiler_params=_params(("arbitrary", "arbitrary") if nr else ("parallel", "arbitrary")),
    )(proj_a, proj_a, proj_a, dy, proj_a, proj_a, proj_a, dy, conv_w, *(rider.groups if nr else []))
    return outs[0], outs[1], outs[2], outs[3], (rider.place(outs[4:]) if nr else [])


def _log_sigmoid(z):
    return jnp.minimum(z, 0.0) - jnp.log(1.0 + jnp.exp(-jnp.abs(z)))


def _sb_masks():
    row = lax.broadcasted_iota(jnp.int32, (SBQ, SBQ), 0)
    col = lax.broadcasted_iota(jnp.int32, (SBQ, SBQ), 1)
    return row, col


def _ones_where(mask):
    return jnp.where(mask, 1.0, 0.0).astype(BF16)


def _head_mask(head):
    lane = lax.broadcasted_iota(jnp.int32, (1, LANE), 1)
    return lane >= 64 if head else lane < 64


def _sb_fwd(proj_b, carry=None):
    t = proj_b.shape[0]
    nq = t // SBQ
    scale = 1.0 / math.sqrt(64.0)

    npair = SB_PAIRS_FWD
    wide = npair * LANE
    ngrp = 4 // npair
    chains = [(p, head) for p in range(npair) for head in range(2)]

    grid = (ngrp, nq)
    nc = carry.n if carry is not None else 0

    def body(*refs):
        q_ref, k_ref, v_ref = refs[:3]
        y_ref, l_ref, n_ref = refs[3 + nc:6 + nc]
        riders = (refs[3:3 + nc], refs[6 + nc:6 + 2 * nc], refs[-2], refs[-1]) if nc else None
        if nc:
            carry.ride(grid, riders, "start")
        grp = pl.program_id(0)
        qi = pl.program_id(1)
        row, col = _sb_masks()
        m_suffix = _ones_where(row > col)
        rows = len(chains) * SBQ
        strict = (lax.broadcasted_iota(jnp.int32, (rows, SBQ), 1)
                  < (lax.broadcasted_iota(jnp.int32, (rows, SBQ), 0) & (SBQ - 1)))
        q_pair = []
        for p in range(npair):
            q_all = q_ref[:, p * LANE:(p + 1) * LANE]
            q_pair.append(jnp.concatenate([jnp.where(_head_mask(head), q_all, jnp.zeros_like(q_all)) for head in range(2)],
                                          axis=0))

        def block(kb, state, diag):
            run, acc = state
            start = pl.multiple_of(kb * SBQ, SBQ)
            z = jnp.concatenate([_dg(q_pair[p], k_ref[pl.ds(start, SBQ), p * LANE:(p + 1) * LANE], NT)
                                 for p in range(npair)], axis=0) * scale
            lb = _log_sigmoid(z)
            lk = lb - z
            if diag:
                lk = jnp.where(strict, lk, 0.0)
            hi, lo = _split(lk)
            sums = _dg(jnp.concatenate([hi, lo], axis=0), m_suffix, NN)
            w = jnp.exp(lb + (run + sums[:rows] + sums[rows:]))
            if diag:
                w = jnp.where(strict, w, 0.0)
            wb = w.astype(BF16)
            acc = acc + jnp.concatenate(
                [_dg(wb[2 * p * SBQ:2 * (p + 1) * SBQ], v_ref[pl.ds(start, SBQ), p * LANE:(p + 1) * LANE], NN)
                 for p in range(npair)], axis=0)
            run = run + jnp.sum(hi.astype(F32) + lo.astype(F32), axis=1, keepdims=True)
            return run, acc

        state = block(qi, (jnp.zeros((rows, 1), F32), jnp.zeros((rows, LANE), F32)), True)

        def live(c):
            return jnp.logical_and(c[0] < qi, jnp.max(c[1][0]) > SB_DEAD)

        def step(c):
            return c[0] + 1, block(qi - 1 - c[0], c[1], False)

        count, (run, acc) = lax.while_loop(live, step, (jnp.int32(0), state))
        n_ref[grp * nq + qi] = count.astype(F32)
        hm = _head_mask(0)
        for p in range(npair):
            lo_rows, hi_rows = slice(2 * p * SBQ, (2 * p + 1) * SBQ), slice((2 * p + 1) * SBQ, (2 * p + 2) * SBQ)
            y_ref[:, p * LANE:(p + 1) * LANE] = jnp.where(hm, acc[lo_rows], acc[hi_rows]).astype(BF16)
            l_ref[p] = jnp.where(hm, run[lo_rows], run[hi_rows])
        if nc:
            carry.ride(grid, riders, "finish")

    outs = pl.pallas_call(
        body, name="sb_fwd", grid=grid,
        in_specs=[_spec((SBQ, wide), lambda g, i: (i, g)),
                  _spec((t, wide), lambda g, i: (0, ngrp + g)),
                  _spec((t, wide), lambda g, i: (0, 2 * ngrp + g))] + (carry.in_specs if nc else []),
        out_specs=[_spec((SBQ, wide), lambda g, i: (i, g)), _spec((npair, SBQ, LANE), lambda g, i: (g, i, 0)),
                   pl.BlockSpec(memory_space=pltpu.SMEM)] + (carry.out_specs if nc else []),
        out_shape=[jax.ShapeDtypeStruct((t, 512), BF16), jax.ShapeDtypeStruct((4, t, LANE), F32),
                   jax.ShapeDtypeStruct((ngrp * nq,), F32)] + (carry.out_shape if nc else []),
        scratch_shapes=carry.scratch if nc else [],
        compiler_params=_params(("arbitrary", "arbitrary")),
    )(proj_b, proj_b, proj_b, *(carry.groups if nc else []))
    return outs[0], outs[1], outs[2], (carry.place(outs[3:]) if nc else [])


def _sb_bwd(proj_b, dy, ltot, nblk, rider=None):
    t = proj_b.shape[0]
    nq = t // SBQ
    scale = 1.0 / math.sqrt(64.0)

    npair = SB_PAIRS_BWD
    wide = npair * LANE
    ngrp = 4 // npair
    chains = [(p, head) for p in range(npair) for head in range(2)]
    grid = (ngrp, nq)
    nr = rider.n if rider is not None else 0
    per_count = SB_PAIRS_FWD // SB_PAIRS_BWD

    def body(*refs):
        q_ref, k_ref, v_ref, dy_ref, l_ref, n_ref = refs[:6]
        dq_ref, dk_ref, dv_ref = refs[6 + nr:9 + nr]
        dk_acc, dv_acc = refs[9 + 2 * nr:11 + 2 * nr]
        riders = (refs[6:6 + nr], refs[9 + nr:9 + 2 * nr], refs[-2], refs[-1]) if nr else None
        if nr:
            rider.ride(grid, riders, "start")
        grp = pl.program_id(0)
        qi = pl.program_id(1)

        @pl.when(qi == 0)
        def _():
            dk_acc[...] = jnp.zeros_like(dk_acc)
            dv_acc[...] = jnp.zeros_like(dv_acc)

        row, col = _sb_masks()
        m_prefix = _ones_where(row <= col)
        m_before = _ones_where(row < col)
        rows = len(chains) * SBQ
        strict = (lax.broadcasted_iota(jnp.int32, (rows, SBQ), 1)
                  < (lax.broadcasted_iota(jnp.int32, (rows, SBQ), 0) & (SBQ - 1)))
        q_pair, do_pair, ltot = [], [], []
        for p in range(npair):
            pl_ = slice(p * LANE, (p + 1) * LANE)
            q_all = q_ref[:, pl_]
            do_all = dy_ref[:, pl_].astype(BF16)
            q_pair.append(jnp.concatenate([jnp.where(_head_mask(h), q_all, jnp.zeros_like(q_all)) for h in range(2)], axis=0))
            do_pair.append(jnp.concatenate([jnp.where(_head_mask(h), do_all, jnp.zeros_like(do_all)) for h in range(2)], axis=0))
            ltot += [l_ref[p][:, h * 64:h * 64 + 1] for h in range(2)]
        ltot = jnp.concatenate(ltot, axis=0)

        def pair_rows(a, p):
            return a[2 * p * SBQ:2 * (p + 1) * SBQ]

        def block(kb, state, diag):
            seen, dseen, dq = state
            start = pl.multiple_of(kb * SBQ, SBQ)
            kk = [k_ref[pl.ds(start, SBQ), p * LANE:(p + 1) * LANE] for p in range(npair)]
            vv = [v_ref[pl.ds(start, SBQ), p * LANE:(p + 1) * LANE] for p in range(npair)]
            z = jnp.concatenate([_dg(q_pair[p], kk[p], NT) for p in range(npair)], axis=0) * scale
            lb = _log_sigmoid(z)
            lk = lb - z
            if diag:
                lk = jnp.where(strict, lk, 0.0)
            hi, lo = _split(lk)
            sums = _dg(jnp.concatenate([hi, lo], axis=0), m_prefix, NN)
            w = jnp.exp(lb + ((ltot - seen) - (sums[:rows] + sums[rows:])))
            if diag:
                w = jnp.where(strict, w, 0.0)
            wb = w.astype(BF16)
            da = w * jnp.concatenate([_dg(do_pair[p], vv[p], NT) for p in range(npair)], axis=0)
            dah, dal = _split(da)
            dsums = _dg(jnp.concatenate([dah, dal], axis=0), m_before, NN)
            sig = jnp.exp(lb)
            dz = (da * (1.0 - sig) - (dseen + dsums[:rows] + dsums[rows:]) * sig) * scale
            if diag:
                dz = jnp.where(strict, dz, 0.0)
            dzb = dz.astype(BF16)
            for p in range(npair):
                pl_ = slice(p * LANE, (p + 1) * LANE)
                dv_acc[pl.ds(start, SBQ), pl_] += _dg(pair_rows(wb, p), do_pair[p], TN)
                dk_acc[pl.ds(start, SBQ), pl_] += _dg(pair_rows(dzb, p), q_pair[p], TN)
            dq = dq + jnp.concatenate([_dg(pair_rows(dzb, p), kk[p], NN) for p in range(npair)], axis=0)
            seen = seen + jnp.sum(hi.astype(F32) + lo.astype(F32), axis=1, keepdims=True)
            dseen = dseen + jnp.sum(da, axis=1, keepdims=True)
            return seen, dseen, dq

        zero = (jnp.zeros((rows, 1), F32), jnp.zeros((rows, 1), F32), jnp.zeros((rows, LANE), F32))
        first = qi - n_ref[(grp // per_count) * nq + qi].astype(jnp.int32)
        state = lax.fori_loop(first, qi, lambda kb, c: block(kb, c, False), zero)
        _, _, dq = block(qi, state, True)
        for p in range(npair):
            dq_ref[:, p * LANE:(p + 1) * LANE] = jnp.where(
                _head_mask(0), dq[2 * p * SBQ:(2 * p + 1) * SBQ], dq[(2 * p + 1) * SBQ:(2 * p + 2) * SBQ]).astype(BF16)

        @pl.when(qi == nq - 1)
        def _():
            dk_ref[...] = dk_acc[...].astype(BF16)
            dv_ref[...] = dv_acc[...].astype(BF16)

        if nr:
            rider.ride(grid, riders, "finish")

    full = jax.ShapeDtypeStruct((t, 512), BF16)
    outs = pl.pallas_call(
        body, name="sb_bwd", grid=grid,
        in_specs=[_spec((SBQ, wide), lambda g, i: (i, g)),
                  _spec((t, wide), lambda g, i: (0, ngrp + g)),
                  _spec((t, wide), lambda g, i: (0, 2 * ngrp + g)),
                  _spec((SBQ, wide), lambda g, i: (i, ngrp + g)),
                  _spec((npair, SBQ, LANE), lambda g, i: (g, i, 0)),
                  pl.BlockSpec(memory_space=pltpu.SMEM)] + (rider.in_specs if nr else []),
        out_specs=[_spec((SBQ, wide), lambda g, i: (i, g)),
                   _spec((t, wide), lambda g, i: (0, g)), _spec((t, wide), lambda g, i: (0, g))]
        + (rider.out_specs if nr else []),
        out_shape=[full, full, full] + (rider.out_shape if nr else []),
        scratch_shapes=[pltpu.VMEM((t, wide), F32), pltpu.VMEM((t, wide), F32)] + (rider.scratch if nr else []),
        compiler_params=_params(("arbitrary", "arbitrary")),
    )(proj_b, proj_b, proj_b, dy, ltot, nblk, *(rider.groups if nr else []))
    return outs[0], outs[1], outs[2], (rider.place(outs[3:]) if nr else [])


def _hgrn_gates(qr, fr, c0, c1):
    mx = jnp.maximum(c0, c1)
    e0, e1 = jnp.exp(c0 - mx), jnp.exp(c1 - mx)
    lb = e1 / (e0 + e1)
    sx = _sigmoid(fr)
    f = lb + (1.0 - lb) * sx
    k = (1.0 - lb) * (1.0 - sx)
    sq = _sigmoid(qr)
    return lb, sx, f, k, sq, qr * sq


def _chunk_sums(mask, x):
    n = x.shape[1]
    hi, lo = _split(x)
    both = _dg(_ones_where(mask), jnp.concatenate([hi, lo], axis=1), NN)
    return both[:, :n] + both[:, n:]


def _chunk_masks():
    row = lax.broadcasted_iota(jnp.int32, (CHUNK, CHUNK), 0)
    col = lax.broadcasted_iota(jnp.int32, (CHUNK, CHUNK), 1)
    return col <= row, col >= row


def _hgrn_fwd(proj_c, small, carry=None):
    t = proj_c.shape[1]
    nc = t // CHUNK
    nh = D // HD

    hps = HGRN_HPS
    wide = hps * HD
    grid = (nh // hps, nc)
    nr = carry.n if carry is not None else 0

    def body(*refs):
        p_ref, c0_ref, c1_ref, gam_ref = refs[:4]
        o_ref, y_ref, sst_ref = refs[4 + nr:7 + nr]
        st_ref = refs[7 + 2 * nr]
        riders = (refs[4:4 + nr], refs[7 + nr:7 + 2 * nr], refs[-2], refs[-1]) if nr else None
        if nr:
            carry.ride(grid, riders, "start")
        c = pl.program_id(1)

        @pl.when(c == 0)
        def _():
            st_ref[...] = jnp.zeros_like(st_ref)

        _, _, f_all, k_all, _, q_all = _hgrn_gates(p_ref[0], p_ref[1], c0_ref[...], c1_ref[...])
        tril, _ = _chunk_masks()
        b_all = _chunk_sums(tril, jnp.log(f_all))
        for j in range(hps):
            ln = slice(j * HD, (j + 1) * HD)
            st0 = st_ref[j]
            sst_ref[j] = st0
            v, g = p_ref[2, :, ln], p_ref[3, :, ln]
            q, k, b = q_all[:, ln], k_all[:, ln], b_all[:, ln]
            bm = b[CHUNK // 2 - 1:CHUNK // 2]
            bl = b[CHUNK - 1:CHUNK]
            qd = q * jnp.exp(b)
            qt = q * jnp.exp(b - bm)
            kt = k * jnp.exp(bm - b)
            kl = k * jnp.exp(bl - b)
            vb = v.astype(BF16)
            att = jnp.where(tril, _dot3(qt, kt, NT), 0.0)
            o = _dg(qd.astype(BF16), st0.astype(BF16), NT) + _dg(att.astype(BF16), vb, NN)
            st_ref[j] = st0 * jnp.exp(bl) + _dg(vb, kl.astype(BF16), TN)
            o_ref[:, ln] = o
            r = lax.rsqrt(jnp.mean(o * o, axis=-1, keepdims=True) + RMS_EPS)
            y_ref[:, ln] = (o * r * gam_ref[...] * (g * _sigmoid(g))).astype(BF16)
        if nr:
            carry.ride(grid, riders, "finish")

    outs = pl.pallas_call(
        body, name="hgrn_fwd", grid=grid,
        in_specs=[_spec((4, CHUNK, wide), lambda h, c: (0, c, h)),
                  _spec((None, 1, wide), lambda h, c: (R_CLB, 0, h)),
                  _spec((None, 1, wide), lambda h, c: (R_CLB + 1, 0, h)),
                  _spec((None, 1, HD), lambda h, c: (R_GAM, 0, 0))] + (carry.in_specs if nr else []),
        out_specs=[_spec((CHUNK, wide), lambda h, c: (c, h)), _spec((CHUNK, wide), lambda h, c: (c, h)),
                   _spec((None, hps, HD, HD), lambda h, c: (c, h, 0, 0))] + (carry.out_specs if nr else []),
        out_shape=[jax.ShapeDtypeStruct((t, D), F32), jax.ShapeDtypeStruct((t, D), BF16),
                   jax.ShapeDtypeStruct((nc, nh, HD, HD), F32)] + (carry.out_shape if nr else []),
        scratch_shapes=[pltpu.VMEM((hps, HD, HD), F32)] + (carry.scratch if nr else []),
        compiler_params=_params(("arbitrary", "arbitrary")),
    )(proj_c, small, small, small, *(carry.groups if nr else []))
    return outs[0], outs[1], outs[2], (carry.place(outs[3:]) if nr else [])


def _hgrn_bwd(proj_c, small, o, sst, dyc, rider=None):
    t = proj_c.shape[1]
    nc = t // CHUNK
    nh = D // HD

    hps = HGRN_HPS
    wide = hps * HD
    ng = nh // hps
    grid = (ng, nc)
    nr = rider.n if rider is not None else 0

    def body(*refs):
        p_ref, c0_ref, c1_ref, gam_ref, o_ref, sst_ref, dy_ref = refs[:7]
        dp_ref, dclb_ref, dgam_ref = refs[7 + nr:10 + nr]
        dst_ref, dlb_acc, dgam_acc = refs[10 + 2 * nr:13 + 2 * nr]
        riders = (refs[7:7 + nr], refs[10 + nr:10 + 2 * nr], refs[-2], refs[-1]) if nr else None
        if nr:
            rider.ride(grid, riders, "start")
        group = pl.program_id(0)
        step = pl.program_id(1)

        @pl.when(step == 0)
        def _():
            dst_ref[...] = jnp.zeros_like(dst_ref)
            dlb_acc[...] = jnp.zeros_like(dlb_acc)

        @pl.when((step == 0) & (group == 0))
        def _():
            dgam_acc[...] = jnp.zeros_like(dgam_acc)

        gam = gam_ref[...]
        qr_all = p_ref[0]
        lb_all, sx_all, f_all, k_all, sq_all, q_all = _hgrn_gates(qr_all, p_ref[1], c0_ref[...], c1_ref[...])
        tril, triu = _chunk_masks()
        b_all = _chunk_sums(tril, jnp.log(f_all))
        dq_parts, dk_parts, db_parts, dgam_parts = [], [], [], []
        for j in range(hps):
            ln = slice(j * HD, (j + 1) * HD)
            st0 = sst_ref[j]
            dst1 = dst_ref[j]
            v, g = p_ref[2, :, ln], p_ref[3, :, ln]
            q, k, b = q_all[:, ln], k_all[:, ln], b_all[:, ln]
            bm = b[CHUNK // 2 - 1:CHUNK // 2]
            bl = b[CHUNK - 1:CHUNK]
            eb = jnp.exp(b)
            e_qt = jnp.exp(b - bm)
            e_kt = jnp.exp(bm - b)
            e_kl = jnp.exp(bl - b)
            e_bl = jnp.exp(bl)
            qd, qt, kt, kl = q * eb, q * e_qt, k * e_kt, k * e_kl
            ov = o_ref[:, ln]
            r = lax.rsqrt(jnp.mean(ov * ov, axis=-1, keepdims=True) + RMS_EPS)
            oh = ov * r
            sg = _sigmoid(g)
            dy = dy_ref[:, ln]
            dp_ref[3, :, ln] = (dy * oh * gam * (sg * (1.0 + g * (1.0 - sg)))).astype(BF16)
            dyv = dy * (g * sg)
            gdy = dyv * gam
            do = (gdy - oh * jnp.mean(gdy * oh, axis=-1, keepdims=True)) * r
            dob, vb = do.astype(BF16), v.astype(BF16)
            st0b, dst1b = st0.astype(BF16), dst1.astype(BF16)
            st1 = st0 * e_bl + _dg(vb, kl.astype(BF16), TN)
            att = jnp.where(tril, _dot3(qt, kt, NT), 0.0)
            datt = jnp.where(tril, _dg(dob, vb, NT), 0.0)
            dv = _dg(att.astype(BF16), dob, TN) + _dg(kl.astype(BF16), dst1b, NT)
            dq = _dot3(datt, kt, NN) * e_qt + _dg(dob, st0b, NN) * eb
            dk = _dot3(datt, qt, TN) * e_kt + _dg(vb, dst1b, NN) * e_kl
            db = q * dq - k * dk
            last = lax.broadcasted_iota(jnp.int32, (CHUNK, 1), 0) == CHUNK - 1
            db = db + jnp.where(last, jnp.sum(dst1 * st1, axis=0, keepdims=True), 0.0)
            dst_ref[j] = dst1 * e_bl + _dg(dob, qd.astype(BF16), TN)
            dp_ref[2, :, ln] = dv.astype(BF16)
            dq_parts.append(dq)
            dk_parts.append(dk)
            db_parts.append(db)
            dgam_parts.append(jnp.sum(dyv * oh, axis=0, keepdims=True))

        dq_all, dk_all = jnp.concatenate(dq_parts, axis=1), jnp.concatenate(dk_parts, axis=1)
        dlf = _chunk_sums(triu, jnp.concatenate(db_parts, axis=1))
        dp_ref[0] = (dq_all * (sq_all * (1.0 + qr_all * (1.0 - sq_all)))).astype(BF16)
        tmp = dlf / f_all - dk_all
        dp_ref[1] = (tmp * (1.0 - lb_all) * sx_all * (1.0 - sx_all)).astype(BF16)
        dlb_acc[...] += jnp.sum((1.0 - sx_all) * tmp, axis=0, keepdims=True)
        dgam_acc[...] += functools.reduce(lambda a, b: a + b, dgam_parts)

        @pl.when(step == nc - 1)
        def _():
            d1 = dlb_acc[...] * lb_all * (1.0 - lb_all)
            dclb_ref[...] = jnp.where(lax.broadcasted_iota(jnp.int32, (2, wide), 0) == 0, -d1, d1)

        @pl.when((step == nc - 1) & (group == ng - 1))
        def _():
            dgam_ref[...] = dgam_acc[...]

        if nr:
            rider.ride(grid, riders, "finish")

    rev = lambda h, s: (nc - 1 - s, h)
    outs = pl.pallas_call(
        body, name="hgrn_bwd", grid=grid,
        in_specs=[_spec((4, CHUNK, wide), lambda h, s: (0, nc - 1 - s, h)),
                  _spec((None, 1, wide), lambda h, s: (R_CLB, 0, h)),
                  _spec((None, 1, wide), lambda h, s: (R_CLB + 1, 0, h)),
                  _spec((None, 1, HD), lambda h, s: (R_GAM, 0, 0)),
                  _spec((CHUNK, wide), rev),
                  _spec((None, hps, HD, HD), lambda h, s: (nc - 1 - s, h, 0, 0)),
                  _spec((CHUNK, wide), rev)] + (rider.in_specs if nr else []),
        out_specs=[_spec((4, CHUNK, wide), lambda h, s: (0, nc - 1 - s, h)),
                   _spec((2, wide), lambda h, s: (0, h)),
                   _spec((1, HD), lambda h, s: (0, 0))] + (rider.out_specs if nr else []),
        out_shape=[jax.ShapeDtypeStruct((4, t, D), BF16), jax.ShapeDtypeStruct((2, D), F32),
                   jax.ShapeDtypeStruct((1, HD), F32)] + (rider.out_shape if nr else []),
        scratch_shapes=[pltpu.VMEM((hps, HD, HD), F32), pltpu.VMEM((1, wide), F32), pltpu.VMEM((1, HD), F32)]
        + (rider.scratch if nr else []),
        compiler_params=_params(("arbitrary", "arbitrary")),
    )(proj_c, small, small, small, o, sst, dyc, *(rider.groups if nr else []))
    return outs[0], outs[1], outs[2], (rider.place(outs[3:]) if nr else [])


def _adamw(name, w, g, m, v):
    rows, cols = w.shape
    br = rows
    for cand in (512, 352, 256):
        if rows % cand == 0:
            br = cand
            break
    c1 = 1.0 - ADAM_B1 ** ADAM_STEP
    c2 = 1.0 - ADAM_B2 ** ADAM_STEP

    def body(w_ref, g_ref, m_ref, v_ref, d_ref, mo_ref, vo_ref):
        gv = g_ref[...]
        mn = ADAM_B1 * m_ref[...] + (1.0 - ADAM_B1) * gv
        vn = ADAM_B2 * v_ref[...] + (1.0 - ADAM_B2) * (gv * gv)
        mo_ref[...] = mn
        vo_ref[...] = vn
        d_ref[...] = -ADAM_LR * ((mn / c1) / (jnp.sqrt(vn / c2) + ADAM_EPS) + ADAM_WD * w_ref[...])

    blk = _spec((br, cols), lambda i: (i, 0))
    shape = jax.ShapeDtypeStruct((rows, cols), F32)
    return pl.pallas_call(
        body, name=name, grid=(rows // br,), in_specs=[blk] * 4, out_specs=[blk] * 3, out_shape=[shape] * 3,
        compiler_params=_params(("parallel",)),
    )(w, g, m, v)


def _place():
    x, y, c = lax.axis_index("x"), lax.axis_index("y"), lax.axis_index("c")
    chips = [(1 - x, y), (x, 1 - y), (1 - x, 1 - y)]
    return x, y, c, chips


class _Rider:
    n = 0
    relay = None

    def ride(self, grid, refs, when):
        if not self.n:
            return
        ids = [pl.program_id(a) for a in range(len(grid))]
        first = functools.reduce(jnp.logical_and, [i == 0 for i in ids])
        last = functools.reduce(jnp.logical_and, [i == g - 1 for i, g in zip(ids, grid)])
        phases = [(first, self.start), (last, self.relay)] if when == "start" else [(last, self.finish)]
        for cond, phase in phases:
            if phase is not None:
                pl.when(cond)(functools.partial(phase, *refs))


class _Gather(_Rider):
    def __init__(self, groups):
        self.groups = list(groups)
        self.n = len(self.groups)
        self.in_specs = [ANY] * self.n
        self.out_specs = [ANY] * self.n
        self.out_shape = [jax.ShapeDtypeStruct((NSH,) + g.shape, g.dtype) for g in self.groups]
        self.scratch = [pltpu.SemaphoreType.DMA((6 * self.n,)), pltpu.SemaphoreType.DMA((6 * self.n,))] if self.n else []

    def _copies(self, ins, outs, send, recv):
        x, y, c, chips = _place()
        sibling = (x, y, 1 - c)

        def half(gi, chip, hc):
            return outs[gi].at[2 * chip[0] + chip[1], hc]

        def copy(gi, k, src, dst, to):
            return pltpu.make_async_remote_copy(src_ref=src, dst_ref=dst, send_sem=send.at[6 * gi + k],
                                                recv_sem=recv.at[6 * gi + k], device_id=to, device_id_type=MESH)

        pairs = [(gi, j, chip) for gi in range(self.n) for j, chip in enumerate(chips)]
        first = [copy(gi, j, ins[gi].at[c], half(gi, (x, y), c), (*chip, c)) for gi, j, chip in pairs]
        landed = [copy(gi, j, half(gi, chip, c), half(gi, chip, c), sibling) for gi, j, chip in pairs]
        relay = [copy(gi, 3 + j, half(gi, chip, c), half(gi, chip, c), sibling) for gi, j, chip in pairs]
        relayed = [copy(gi, 3 + j, half(gi, chip, 1 - c), half(gi, chip, 1 - c), sibling) for gi, j, chip in pairs]
        return first, landed, relay, relayed

    def start(self, ins, outs, send, recv):
        for cp in self._copies(ins, outs, send, recv)[0]:
            cp.start()

    def relay(self, ins, outs, send, recv):
        _, landed, relay, _ = self._copies(ins, outs, send, recv)
        for arrived, onward in zip(landed, relay):
            arrived.wait_recv()
            onward.start()

    def finish(self, ins, outs, send, recv):
        first, _, relay, relayed = self._copies(ins, outs, send, recv)
        for cp in relayed:
            cp.wait_recv()
        for cp in first + relay:
            cp.wait_send()

    def place(self, outs):
        me = 2 * lax.axis_index("x") + lax.axis_index("y")
        return [lax.dynamic_update_index_in_dim(o, g, me, 0) for o, g in zip(outs, self.groups)]


def _alone(name, rider):
    n = rider.n

    def body(*refs):
        parts = (refs[:n], refs[n:2 * n], refs[2 * n], refs[2 * n + 1])
        rider.start(*parts)
        if rider.relay is not None:
            rider.relay(*parts)
        rider.finish(*parts)

    outs = pl.pallas_call(
        body, name=name, in_specs=rider.in_specs, out_specs=rider.out_specs, out_shape=rider.out_shape,
        scratch_shapes=rider.scratch, compiler_params=pltpu.CompilerParams(has_side_effects=True),
    )(*rider.groups)
    return rider.place(outs)


class _Swap(_Rider):
    def __init__(self, slots):
        self.slots = list(slots)
        self.groups = [buf for buf, _, _ in self.slots]
        self.n = len(self.slots)
        self.in_specs = [ANY] * self.n
        self.out_specs = [ANY] * self.n
        self.out_shape = [jax.ShapeDtypeStruct((NSH, rows // 2, buf.shape[2]), buf.dtype) for buf, _, rows in self.slots]
        self.scratch = [pltpu.SemaphoreType.DMA((self.n,)), pltpu.SemaphoreType.DMA((self.n,))]

    def _copies(self, ins, outs, send, recv):
        x, y, c, _ = _place()
        cps = []
        for i, (_, row0, rows) in enumerate(self.slots):
            half = rows // 2
            src = ins[i].at[:, pl.ds(pl.multiple_of(row0 + (1 - c) * half, 16), half)]
            cps.append(pltpu.make_async_remote_copy(src_ref=src, dst_ref=outs[i], send_sem=send.at[i],
                                                    recv_sem=recv.at[i], device_id=(x, y, 1 - c), device_id_type=MESH))
        return cps

    def start(self, ins, outs, send, recv):
        for cp in self._copies(ins, outs, send, recv):
            cp.start()

    def finish(self, ins, outs, send, recv):
        for cp in self._copies(ins, outs, send, recv):
            cp.wait()

    def place(self, outs):
        return list(outs)


class _Share(_Rider):
    def __init__(self, arrays):
        self.groups = list(arrays)
        self.n = len(self.groups)
        self.in_specs = [ANY] * self.n
        self.out_specs = [ANY] * self.n
        self.out_shape = [jax.ShapeDtypeStruct((2,) + g.shape, g.dtype) for g in self.groups]
        self.scratch = [pltpu.SemaphoreType.DMA((self.n,)), pltpu.SemaphoreType.DMA((self.n,))]

    def _copies(self, ins, outs, send, recv, half):
        x, y, c, _ = _place()
        return [pltpu.make_async_remote_copy(src_ref=ins[gi], dst_ref=outs[gi].at[c if half == "mine" else 1 - c],
                                             send_sem=send.at[gi], recv_sem=recv.at[gi], device_id=(x, y, 1 - c),
                                             device_id_type=MESH) for gi in range(self.n)]

    def start(self, ins, outs, send, recv):
        for cp in self._copies(ins, outs, send, recv, "mine"):
            cp.start()

    def finish(self, ins, outs, send, recv):
        for cp in self._copies(ins, outs, send, recv, "theirs"):
            cp.wait_recv()
        for cp in self._copies(ins, outs, send, recv, "mine"):
            cp.wait_send()

    def place(self, outs):
        c = lax.axis_index("c")
        return [lax.dynamic_update_index_in_dim(o, g, c, 0) for o, g in zip(outs, self.groups)]


class _Send(_Rider):
    def __init__(self, arrays):
        self.groups = list(arrays)
        self.n = len(self.groups)
        self.in_specs = [ANY] * self.n
        self.out_specs = [ANY] * self.n
        self.out_shape = [jax.ShapeDtypeStruct((3,) + g.shape[1:], g.dtype) for g in self.groups]
        self.scratch = [pltpu.SemaphoreType.DMA((3 * self.n,)), pltpu.SemaphoreType.DMA((3 * self.n,))]

    def _copies(self, ins, outs, send, recv):
        x, y, c, chips = _place()
        return [pltpu.make_async_remote_copy(src_ref=ins[gi].at[2 * chip[0] + chip[1]], dst_ref=outs[gi].at[j],
                                             send_sem=send.at[3 * gi + j], recv_sem=recv.at[3 * gi + j],
                                             device_id=(*chip, c), device_id_type=MESH)
                for gi in range(self.n) for j, chip in enumerate(chips)]

    def start(self, ins, outs, send, recv):
        for cp in self._copies(ins, outs, send, recv):
            cp.start()

    def finish(self, ins, outs, send, recv):
        for cp in self._copies(ins, outs, send, recv):
            cp.wait()

    def place(self, outs):
        return list(outs)


def _pair_sum(name, slots, got, c_idx):
    n = len(slots)
    in_specs, out_specs, out_shape, operands = [], [], [], []
    for (buf, row0, rows), g in zip(slots, got):
        hb, cols = rows // 4, buf.shape[2]
        in_specs += [pl.BlockSpec((None, hb, cols), lambda q, i, cr, r=row0 // hb: (q, r + 2 * cr[0] + i, 0)),
                     pl.BlockSpec((None, hb, cols), lambda q, i, cr: (q, i, 0))]
        out_specs.append(pl.BlockSpec((None, hb, cols), lambda q, i, cr: (q, i, 0)))
        out_shape.append(jax.ShapeDtypeStruct(g.shape, BF16))
        operands += [buf, g]

    def body(c_ref, *refs):
        for i in range(n):
            refs[2 * n + i][...] = (refs[2 * i][...].astype(F32) + refs[2 * i + 1][...].astype(F32)).astype(BF16)

    return pl.pallas_call(
        body, name=name,
        grid_spec=pltpu.PrefetchScalarGridSpec(num_scalar_prefetch=1, grid=(NSH, 2), in_specs=in_specs, out_specs=out_specs),
        out_shape=out_shape, compiler_params=_params(("parallel", "parallel")),
    )(c_idx, *operands)


def _owner_sum(name, pairs, got, p_idx):
    n = len(pairs)
    in_specs, out_specs, out_shape, operands = [], [], [], []
    for own, g in zip(pairs, got):
        _, rows, cols = own.shape
        hb = rows // 2
        in_specs += [pl.BlockSpec((None, hb, cols), lambda i, pr: (pr[0], i, 0)),
                     pl.BlockSpec((3, hb, cols), lambda i, pr: (0, i, 0))]
        out_specs.append(pl.BlockSpec((hb, cols), lambda i, pr: (i, 0)))
        out_shape.append(jax.ShapeDtypeStruct((rows, cols), F32))
        operands += [own, g]

    def body(p_ref, *refs):
        for i in range(n):
            a_ref, b_ref = refs[2 * i], refs[2 * i + 1]
            refs[2 * n + i][...] = ((a_ref[...].astype(F32) + b_ref[0].astype(F32)) + b_ref[1].astype(F32)) + b_ref[2].astype(F32)

    return pl.pallas_call(
        body, name=name,
        grid_spec=pltpu.PrefetchScalarGridSpec(num_scalar_prefetch=1, grid=(2,), in_specs=in_specs, out_specs=out_specs),
        out_shape=out_shape, compiler_params=_params(("parallel",)),
    )(p_idx, *operands)


def _sum_small(slab):
    def body(in_ref, out_ref, all_ref, send, recv):
        x, y, c, _ = _place()
        me = 4 * x + 2 * y + c
        all_ref[me] = in_ref[...]
        cps = []
        for k in range(1, 8):
            peer = (x ^ (k >> 2), y ^ ((k >> 1) & 1), c ^ (k & 1))
            cps.append(pltpu.make_async_remote_copy(src_ref=in_ref, dst_ref=all_ref.at[me], send_sem=send.at[k - 1],
                                                    recv_sem=recv.at[k - 1], device_id=peer, device_id_type=MESH))
        for cp in cps:
            cp.start()
        for cp in cps:
            cp.wait()
        total = all_ref[0]
        for d in range(1, 8):
            total = total + all_ref[d]
        out_ref[...] = total

    return pl.pallas_call(
        body, name="sum_small",
        in_specs=[pl.BlockSpec(memory_space=pltpu.VMEM)], out_specs=pl.BlockSpec(memory_space=pltpu.VMEM),
        out_shape=jax.ShapeDtypeStruct(slab.shape, F32),
        scratch_shapes=[pltpu.VMEM((8,) + slab.shape, F32), pltpu.SemaphoreType.DMA((7,)), pltpu.SemaphoreType.DMA((7,))],
        compiler_params=pltpu.CompilerParams(has_side_effects=True),
    )(slab)


FFNS = ("pre0", "post0", "pre1", "post1")
W_SHAPES = dict({f + "_gu": (NSH, 2, FS, D) for f in FFNS}, **{f + "_d": (NSH, FS, D) for f in FFNS},
                ab_in=(NSH, D, 768), ab_out=(NSH, 256, D), conv=(NSH, 2, 8, LANE), c_in=(NSH, D, D), c_out=(NSH, 256, D))
CARRIED = dict(pre0_up=("pre0_d",), pre0_down=("ab_in", "ab_out", "conv"),
               sb_fwd=("post0_gu", "post0_d"), post0_up=("pre1_gu",), post0_down=("pre1_d",),
               pre1_up=("c_in", "c_out"), hgrn_fwd=("post1_gu", "post1_d"))


FFN_SLOTS = dict(pre0=(0, 2, 0), pre1=(1, 3, 1), post0=(4, 6, 2), post1=(5, 7, 3))
GRAD_SLOTS = dict(ab_out=("b", B_ABOUT, 256), c_in=("b", B_CIN, D), c_out=("b", B_COUT, 256), ab_in=("c", 0, D))
for _f, (_g, _u, _d) in FFN_SLOTS.items():
    GRAD_SLOTS.update({_f + "_g": ("a", _g * FS, FS), _f + "_u": ("a", _u * FS, FS), _f + "_d": ("b", _d * FS, FS)})
REDUCE_STAGES = dict(x=("post1_g", "post1_u", "post1_d", "c_out"),
                     y=("c_in", "pre1_g", "pre1_u", "pre1_d", "post0_g", "post0_u", "post0_d", "ab_out"),
                     z1=("ab_in",), z2=("pre0_g", "pre0_u", "pre0_d"))


def _local_step(x, target, weights, small, shards=None, place=None):
    t = x.shape[0]
    tm = min(MM_TILE, t)
    tw = min(WGRAD_TILE, t)
    nt = t // tm
    tok_si = _spec((tm, D), lambda s, i: (i, 0))
    w = dict(weights)
    reduced = {}

    def stage_slots(stage, buffers):
        return [(buffers[GRAD_SLOTS[n][0]],) + GRAD_SLOTS[n][1:] for n in REDUCE_STAGES[stage]]

    def swap_rider(stage, buffers):
        return _Swap(stage_slots(stage, buffers)) if place is not None else None

    def reduce_start(stage, buffers, swapped=None):
        if place is None:
            return [], None
        slots = stage_slots(stage, buffers)
        if swapped is None:
            swapped = _alone("grad_swap_" + stage, _Swap(slots))
        pairs = _pair_sum("grad_pair_sum_" + stage, slots, swapped, place[0])
        return pairs, _Send(pairs)

    def reduce_end(stage, pairs, landed):
        if place is None:
            return None
        mine = _owner_sum("grad_owner_sum_" + stage, pairs, landed, place[1])
        return _Share(mine)

    def shared(stage, landed):
        if place is not None:
            reduced.update(zip(REDUCE_STAGES[stage], landed))

    def carried(kernel_name):
        names = [n for n in CARRIED[kernel_name] if n not in w]
        return names, (_Gather([shards[n] for n in names]) if names else None)

    def land(names, arrays):
        for n, a in zip(names, arrays):
            w[n] = a.reshape(W_SHAPES[n])

    def ffn_forward(tag, h, hn, next_row):
        names, gather = carried(tag + "_up") if tag + "_up" in CARRIED else ([], None)
        s_up, s_gate, a, got = _ffn_up(tag + "_up", hn, w[tag + "_gu"], gather)
        land(names, got)
        names, gather = carried(tag + "_down") if tag + "_down" in CARRIED else ([], None)
        out = _ffn_down(tag + "_down", a, w[tag + "_d"], h, gather, (small, next_row) if next_row is not None else None)
        if gather is not None:
            out, got = out
            land(names, got)
        out, hn_next = out if next_row is not None else (out, None)
        return out, hn_next, (h, hn, s_up, s_gate, a)

    def out_proj(name, y, w_out, h, next_row):
        return _mm(name, [(y, _spec((tm, 256), lambda i, k: (i, k)), w_out, _spec((None, 256, D), lambda i, k: (k, 0, 0)))],
                   grid=(nt, NSH), o_shape=(t, D), o_dtype=F32, o_spec=_spec((tm, D), lambda i, k: (i, 0)),
                   dims=NN, kaxis=1, nk=NSH, acc_shape=(tm, D), res=(h, _spec((tm, D), lambda i, k: (i, 0))),
                   norm=(small, next_row))

    def out_proj_bwd(tag, dhb, y, w_out, blk, grad_b, make_rider=None):
        grad_b = _wgrad(tag + "_dwout", y, _spec((tw, 256), lambda s, k: (k, s)), dhb, _spec((tw, D), lambda s, k: (k, 0)),
                        256, D, t, tw, grad_b, blk)
        rider = make_rider(grad_b) if make_rider is not None else None
        dy = _mm(tag + "_dy", [(dhb, tok_si, w_out, _spec((None, 256, D), lambda s, i: (s, 0, 0)))],
                 grid=(NSH, nt), o_shape=(t, D), o_dtype=F32, o_spec=_spec((tm, 256), lambda s, i: (i, s)),
                 dims=NT, kaxis=1, nk=1, carry=rider)
        dy, landed = dy if rider is not None else (dy, None)
        return dy, grad_b, landed

    h0 = x
    h1, hn_ab, pre0 = ffn_forward("pre0", h0, _norm_fwd("pre0_norm", h0, small, R_PRE), R_MIX)
    proj_a = _mm("ab_proj_a", [(hn_ab, tok_si, w["ab_in"], _spec((None, D, 768), lambda s, i: (s, 0, 0)))],
                 grid=(2, nt), o_shape=(t, 1536), o_dtype=F32, o_spec=_spec((tm, 768), lambda s, i: (i, s)),
                 dims=NN, kaxis=1, nk=1)
    proj_b = _mm("ab_proj_b", [(hn_ab, tok_si, w["ab_in"], _spec((None, D, 768), lambda s, i: (s + 2, 0, 0)))],
                 grid=(2, nt), o_shape=(t, 1536), o_dtype=BF16, o_spec=_spec((tm, 768), lambda s, i: (i, s)),
                 dims=NN, kaxis=1, nk=1)
    y_a = _conv_fwd(proj_a, w["conv"])
    names, gather = carried("sb_fwd")
    y_b, ltot, nblk, got = _sb_fwd(proj_b, gather)
    land(names, got)
    y_ab = jnp.concatenate([y_a, y_b], axis=1)
    h2, hn = out_proj("ab_out", y_ab, w["ab_out"], h1, R_POST)
    h3, hn, post0 = ffn_forward("post0", h2, hn, R_PRE + 1)
    h4, hn_c, pre1 = ffn_forward("pre1", h3, hn, R_MIX + 1)
    proj_c = _mm("c_proj", [(hn_c, tok_si, w["c_in"], _spec((None, D, D), lambda s, i: (s, 0, 0)))],
                 grid=(NSH, nt), o_shape=(NSH, t, D), o_dtype=F32, o_spec=_spec((None, tm, D), lambda s, i: (s, i, 0)),
                 dims=NN, kaxis=1, nk=1)
    names, gather = carried("hgrn_fwd")
    o_c, y_c, sst, got = _hgrn_fwd(proj_c, small, gather)
    land(names, got)
    h5, hn = out_proj("c_out", y_c, w["c_out"], h4, R_POST + 1)
    h6, _, post1 = ffn_forward("post1", h5, hn, None)
    dh, dhb, d_fin, loss = _final_loss(h6, small, target)

    dh, dhb, grad_a, grad_b, dn_post1, _, _ = _ffn_backward("post1", dh, dhb, *post1, w["post1_gu"], w["post1_d"],
                                                            *FFN_SLOTS["post1"], small, R_POST + 1, 8 * FS, B_ROWS)
    dy_c, grad_b, swapped = out_proj_bwd("c", dhb, y_c, w["c_out"], B_COUT // 256, grad_b,
                                         lambda buf_b: swap_rider("x", dict(a=grad_a, b=buf_b)))
    pairs, rider = reduce_start("x", dict(a=grad_a, b=grad_b), swapped)
    dproj_c, d_clb, d_gam, landed = _hgrn_bwd(proj_c, small, o_c, sst, dy_c, rider)
    share = reduce_end("x", pairs, landed)
    grad_b = _wgrad("c_dwin", hn_c, _spec((tw, D), lambda s, k: (k, 0)), dproj_c, _spec((None, tw, D), lambda s, k: (s, k, 0)),
                    D, D, t, tw, grad_b, B_CIN // D, carry=share)
    if share is not None:
        grad_b, landed = grad_b
        shared("x", landed)
    dh, dhb, dn_mix1, _ = _dhn_norm("c_dhn", [(dproj_c, _spec((None, tm, D), lambda i, k: (k, i, 0)),
                                               w["c_in"], _spec((None, D, D), lambda i, k: (k, 0, 0)))],
                                    NT, h4, small, R_MIX + 1, dh)
    dh, dhb, grad_a, grad_b, dn_pre1, _, _ = _ffn_backward("pre1", dh, dhb, *pre1, w["pre1_gu"], w["pre1_d"],
                                                           *FFN_SLOTS["pre1"], small, R_PRE + 1, grad_a, grad_b)
    dh, dhb, grad_a, grad_b, dn_post0, _, _ = _ffn_backward("post0", dh, dhb, *post0, w["post0_gu"], w["post0_d"],
                                                            *FFN_SLOTS["post0"], small, R_POST, grad_a, grad_b)
    dy_ab, grad_b, _ = out_proj_bwd("ab", dhb, y_ab, w["ab_out"], B_ABOUT // 256, grad_b)
    dab, dac, dax, d_conv, swapped = _conv_bwd(proj_a, w["conv"], dy_ab, swap_rider("y", dict(a=grad_a, b=grad_b)))
    pairs, rider = reduce_start("y", dict(a=grad_a, b=grad_b), swapped)
    dq, dk, dv, landed = _sb_bwd(proj_b, dy_ab, ltot, nblk, rider)
    share = reduce_end("y", pairs, landed)
    dproj_ab = jnp.concatenate([dab, dac, dax, dq, dk, dv], axis=1)
    grad_c = _wgrad("ab_dwin", hn_ab, _spec((tw, D), lambda s, k: (k, 0)), dproj_ab, _spec((tw, 768), lambda s, k: (k, s)),
                    D, 768, t, tw, D, 0, carry=share)
    if share is not None:
        grad_c, landed = grad_c
        shared("y", landed)
    dh, dhb, dn_mix0, _ = _dhn_norm("ab_dhn", [(dproj_ab, _spec((tm, 768), lambda i, k: (i, k)),
                                                w["ab_in"], _spec((None, D, 768), lambda i, k: (k, 0, 0)))],
                                    NT, h1, small, R_MIX, dh)
    pairs, rider = reduce_start("z1", dict(c=grad_c))
    last = {}

    def own_gradients(buf_a, buf_b):
        last["z2"] = reduce_start("z2", dict(a=buf_a, b=buf_b))
        return last["z2"][1]

    dh, dhb, grad_a, grad_b, dn_pre0, landed, landed_own = _ffn_backward(
        "pre0", dh, dhb, *pre0, w["pre0_gu"], w["pre0_d"], *FFN_SLOTS["pre0"], small, R_PRE, grad_a, grad_b,
        rider, own_gradients if place is not None else None)
    if place is not None:
        mine = (_owner_sum("grad_owner_sum_z1", pairs, landed, place[1])
                + _owner_sum("grad_owner_sum_z2", last["z2"][0], landed_own, place[1]))
        reduced.update(zip(REDUCE_STAGES["z1"] + REDUCE_STAGES["z2"], _alone("grad_share_z", _Share(mine))))
    zero = jnp.zeros((1, D), F32)
    conv_rows = jnp.pad(jnp.transpose(d_conv[:, :3, :], (1, 0, 2)).reshape(3, 512), ((0, 0), (0, D - 512)))
    small_grad = jnp.concatenate([
        dn_pre0, dn_pre1, dn_mix0, dn_mix1, dn_post0, dn_post1, d_clb, d_fin,
        jnp.pad(d_gam, ((0, 0), (0, D - HD))), conv_rows,
        jnp.pad(loss, ((0, 0), (0, D - 1))), zero, zero], axis=0)
    return dh, grad_a, grad_b, grad_c, small_grad, reduced


def _small_slab(rows):
    parts = [jnp.pad(r.astype(F32), ((0, 0), (0, D - r.shape[1]))) for r in rows]
    slab = jnp.concatenate(parts, axis=0)
    return jnp.pad(slab, ((0, SMALL_ROWS - slab.shape[0]), (0, 0)))


def kernel(x, ffn_pre_norm, ffn_pre_w_gate, ffn_pre_w_up, ffn_pre_w_down, mix_norm, ffn_post_norm, ffn_post_w_gate, ffn_post_w_up, ffn_post_w_down, ab_w_in, ab_conv_w, ab_w_out, c_w_in, c_lower_bounds, c_out_norm, c_w_out, final_norm, loss_target, m_ffn_pre_norm, m_ffn_pre_w_gate, m_ffn_pre_w_up, m_ffn_pre_w_down, m_mix_norm, m_ffn_post_norm, m_ffn_post_w_gate, m_ffn_post_w_up, m_ffn_post_w_down, m_ab_w_in, m_ab_conv_w, m_ab_w_out, m_c_w_in, m_c_lower_bounds, m_c_out_norm, m_c_w_out, m_final_norm, v_ffn_pre_norm, v_ffn_pre_w_gate, v_ffn_pre_w_up, v_ffn_pre_w_down, v_mix_norm, v_ffn_post_norm, v_ffn_post_w_gate, v_ffn_post_w_up, v_ffn_post_w_down, v_ab_w_in, v_ab_conv_w, v_ab_w_out, v_c_w_in, v_c_lower_bounds, v_c_out_norm, v_c_w_out, v_final_norm):
    t = x.shape[1]
    xi, yi, ci = lax.axis_index("x"), lax.axis_index("y"), lax.axis_index("c")
    p_idx = (2 * xi + yi).astype(jnp.int32).reshape(1)
    c_idx = ci.astype(jnp.int32).reshape(1)

    def halves(m):
        return m.astype(BF16).reshape(2, m.shape[0] // 2, m.shape[1])

    transposed = ("ffn_pre_w_gate", "ffn_pre_w_up", "ffn_post_w_gate", "ffn_post_w_up")

    def flip(a):
        return jnp.swapaxes(a, 1, 2)

    shards = {}
    for name, (w_gate, w_up, w_down, layer) in dict(
            pre0=(ffn_pre_w_gate, ffn_pre_w_up, ffn_pre_w_down, 0), post0=(ffn_post_w_gate, ffn_post_w_up, ffn_post_w_down, 0),
            pre1=(ffn_pre_w_gate, ffn_pre_w_up, ffn_pre_w_down, 1), post1=(ffn_post_w_gate, ffn_post_w_up, ffn_post_w_down, 1)).items():
        shards[name + "_gu"] = jnp.stack([flip(w_gate)[layer], flip(w_up)[layer]]).astype(BF16)
        shards[name + "_d"] = halves(w_down[layer])
    conv_pad = jnp.pad(ab_conv_w[0], ((0, 5), (0, 0)))
    shards.update(ab_in=halves(ab_w_in[0]), ab_out=halves(ab_w_out[0]), c_in=halves(c_w_in[0]), c_out=halves(c_w_out[0]),
                  conv=jnp.stack([conv_pad, jnp.zeros_like(conv_pad)]))
    first = _alone("gather_weights", _Gather([shards["pre0_gu"]]))[0].reshape(W_SHAPES["pre0_gu"])

    small = _small_slab([ffn_pre_norm, mix_norm, ffn_post_norm, c_lower_bounds, final_norm.reshape(1, D), c_out_norm])
    small = small.reshape(SMALL_ROWS, 1, D)

    grad_x, _, _, _, small_grad, reduced = _local_step(x[0], loss_target[0], dict(pre0_gu=first), small, shards,
                                                        (c_idx, p_idx))
    whole = {n: g.reshape(2 * g.shape[1], g.shape[2]) for n, g in reduced.items()}
    small_sum = _sum_small(small_grad)

    my_conv = lax.dynamic_slice(small_sum[R_CONV:R_CONV + 3], (0, (2 * xi + yi) * 128), (3, 128))
    grads = {
        "ffn_pre_norm": small_sum[R_PRE:R_PRE + 2], "mix_norm": small_sum[R_MIX:R_MIX + 2],
        "ffn_post_norm": small_sum[R_POST:R_POST + 2], "c_lower_bounds": small_sum[R_CLB:R_CLB + 2],
        "c_out_norm": small_sum[R_GAM:R_GAM + 1, :HD], "final_norm": small_sum[R_FIN],
        "ab_conv_w": my_conv.reshape(1, 3, 128),
        "ab_w_in": whole["ab_in"][None], "ab_w_out": whole["ab_out"][None],
        "c_w_in": whole["c_in"][None], "c_w_out": whole["c_out"][None],
    }
    for kind, key in (("gate", "_g"), ("up", "_u"), ("down", "_d")):
        grads["ffn_pre_w_" + kind] = jnp.stack([whole["pre0" + key], whole["pre1" + key]])
        grads["ffn_post_w_" + kind] = jnp.stack([whole["post0" + key], whole["post1" + key]])
    weights = dict(ffn_pre_norm=ffn_pre_norm, ffn_pre_w_gate=ffn_pre_w_gate, ffn_pre_w_up=ffn_pre_w_up, ffn_pre_w_down=ffn_pre_w_down, mix_norm=mix_norm, ffn_post_norm=ffn_post_norm, ffn_post_w_gate=ffn_post_w_gate, ffn_post_w_up=ffn_post_w_up, ffn_post_w_down=ffn_post_w_down, ab_w_in=ab_w_in, ab_conv_w=ab_conv_w, ab_w_out=ab_w_out, c_w_in=c_w_in, c_lower_bounds=c_lower_bounds, c_out_norm=c_out_norm, c_w_out=c_w_out, final_norm=final_norm)
    m_in = dict(ffn_pre_norm=m_ffn_pre_norm, ffn_pre_w_gate=m_ffn_pre_w_gate, ffn_pre_w_up=m_ffn_pre_w_up, ffn_pre_w_down=m_ffn_pre_w_down, mix_norm=m_mix_norm, ffn_post_norm=m_ffn_post_norm, ffn_post_w_gate=m_ffn_post_w_gate, ffn_post_w_up=m_ffn_post_w_up, ffn_post_w_down=m_ffn_post_w_down, ab_w_in=m_ab_w_in, ab_conv_w=m_ab_conv_w, ab_w_out=m_ab_w_out, c_w_in=m_c_w_in, c_lower_bounds=m_c_lower_bounds, c_out_norm=m_c_out_norm, c_w_out=m_c_w_out, final_norm=m_final_norm)
    v_in = dict(ffn_pre_norm=v_ffn_pre_norm, ffn_pre_w_gate=v_ffn_pre_w_gate, ffn_pre_w_up=v_ffn_pre_w_up, ffn_pre_w_down=v_ffn_pre_w_down, mix_norm=v_mix_norm, ffn_post_norm=v_ffn_post_norm, ffn_post_w_gate=v_ffn_post_w_gate, ffn_post_w_up=v_ffn_post_w_up, ffn_post_w_down=v_ffn_post_w_down, ab_w_in=v_ab_w_in, ab_conv_w=v_ab_conv_w, ab_w_out=v_ab_w_out, c_w_in=v_c_w_in, c_lower_bounds=v_c_lower_bounds, c_out_norm=v_c_out_norm, c_w_out=v_c_w_out, final_norm=v_final_norm)
    names = list(weights)
    big = [n for n in names if weights[n].size >= 65536]
    tiny = [n for n in names if n not in big]

    delta, new_m, new_v = {}, {}, {}
    for n in big:
        turn = flip if n in transposed else (lambda a: a)
        shape = turn(weights[n]).shape
        two_d = (shape[0] * shape[1], shape[2])
        d, m2, v2 = _adamw("adamw_" + n, turn(weights[n]).reshape(two_d), grads[n].reshape(two_d),
                           turn(m_in[n]).reshape(two_d), turn(v_in[n]).reshape(two_d))
        delta[n], new_m[n], new_v[n] = turn(d.reshape(shape)), turn(m2.reshape(shape)), turn(v2.reshape(shape))
        grads[n] = turn(grads[n])

    def tiny_slab(src):
        return _small_slab([src[n].reshape(-1, src[n].shape[-1]) for n in tiny])

    offs, row = {}, 0
    for n in tiny:
        nrows = weights[n].size // weights[n].shape[-1]
        offs[n] = (row, nrows)
        row += nrows
    d, m2, v2 = _adamw("adamw_small", tiny_slab(weights), tiny_slab(grads), tiny_slab(m_in), tiny_slab(v_in))
    for n in tiny:
        r0, nr = offs[n]
        shape = weights[n].shape
        for dst, src in ((delta, d), (new_m, m2), (new_v, v2)):
            dst[n] = src[r0:r0 + nr, :shape[-1]].reshape(shape)

    loss = small_sum[R_LOSS, 0]
    return (loss, grad_x.reshape(1, t, D), *[grads[n] for n in names], *[delta[n] for n in names],
            *[new_m[n] for n in names], *[new_v[n] for n in names])
```

```python
import functools
import math

import jax
import jax.numpy as jnp
from jax import lax
from jax.experimental import pallas as pl
from jax.experimental.pallas import tpu as pltpu

F32 = jnp.float32
BF16 = jnp.bfloat16
MESH = pl.DeviceIdType.MESH
ANY = pl.BlockSpec(memory_space=pl.ANY)

D = 1024
FS = 704
NSH = 4
RMS_EPS = 1e-6
MACARON = 0.5
CHUNK = 64
HD = 128
HGRN_HPS = 8
SBQ = 128
SB_PAIRS_FWD = 4
SB_PAIRS_BWD = 2
SB_DEAD = -105.0
CONV_HALO = 8
LANE = 128
ROW_TILE = 512
MM_TILE = 1024
WGRAD_TILE = 4096
VMEM_LIMIT = 48 * 1024 * 1024
VMEM_LIMIT_BIG = 58 * 1024 * 1024

ADAM_LR, ADAM_B1, ADAM_B2, ADAM_EPS, ADAM_WD, ADAM_STEP = 0.001, 0.9, 0.999, 1e-08, 0.01, 10

NN = ((1,), (0,))
NT = ((1,), (1,))
TN = ((0,), (0,))

SMALL_ROWS = 16
R_PRE, R_MIX, R_POST, R_CLB, R_FIN, R_GAM, R_CONV, R_LOSS = 0, 2, 4, 6, 8, 9, 10, 13

B_DOWN = 0
B_ABOUT = 4 * FS
B_CIN = B_ABOUT + 256
B_COUT = B_CIN + 1024
B_ROWS = B_COUT + 256


def _dg(a, b, dims):
    return lax.dot_general(a, b, (dims, ((), ())), preferred_element_type=F32)


def _split(x):
    hi = x.astype(BF16)
    lo = (x - hi.astype(F32)).astype(BF16)
    return hi, lo


def _dot3(a, b, dims):
    ah, al = _split(a)
    bh, bl = _split(b)
    if dims == TN:
        n = b.shape[1]
        both = _dg(ah, jnp.concatenate([bh, bl], axis=1), dims)
        return both[:, :n] + both[:, n:] + _dg(al, bh, dims)
    m = a.shape[0]
    both = _dg(jnp.concatenate([ah, al], axis=0), bh, dims)
    return both[:m] + both[m:] + _dg(ah, bl, dims)


def _sigmoid(x):
    return 1.0 / (1.0 + jnp.exp(-x))


def _params(sem):
    return pltpu.CompilerParams(dimension_semantics=sem, vmem_limit_bytes=VMEM_LIMIT)


def _spec(shape, imap):
    return pl.BlockSpec(shape, imap)


def _accumulate(acc_ref, pairs, dims):
    part = None
    for a_ref, b_ref in pairs:
        d = _dg(a_ref[...], b_ref[...], dims)
        part = d if part is None else part + d
    acc_ref[...] += part


def _mm(name, pairs, *, grid, o_shape, o_dtype, o_spec, dims, kaxis, nk, acc_shape=None, res=None, scale=None,
        into=None, carry=None, norm=None):
    npairs = len(pairs)
    operands, specs = [], []
    for a, a_spec, b, b_spec in pairs:
        operands += [a, b]
        specs += [a_spec, b_spec]
    if res is not None:
        operands.append(res[0])
        specs.append(res[1])
    aliases = {}
    if into is not None:
        aliases = {len(operands): 0}
        operands.append(into)
        specs.append(ANY)
    if norm is not None:
        operands.append(norm[0])
        specs.append(_spec((None, 1, D), lambda *_: (norm[1], 0, 0)))
    n_own = len(operands)
    n_out = 2 if norm is not None else 1
    nc = carry.n if carry is not None else 0
    if nc:
        operands += carry.groups
        specs += carry.in_specs

    def body(*refs):
        o_ref = refs[n_own + nc]
        riders = ((refs[n_own:n_own + nc], refs[n_own + nc + n_out:n_own + 2 * nc + n_out], refs[-2], refs[-1])
                  if nc else None)
        if nc:
            carry.ride(grid, riders, "start")

        def finish(val):
            if scale is not None:
                val = val * scale
            if res is not None:
                val = val + refs[2 * npairs][...]
            o_ref[...] = val.astype(o_dtype)
            if norm is not None:
                r = lax.rsqrt(jnp.mean(val * val, axis=-1, keepdims=True) + RMS_EPS)
                refs[n_own + nc + 1][...] = (val * r * refs[n_own - 1][...]).astype(BF16)

        if nk == 1:
            part = None
            for n in range(npairs):
                d = _dg(refs[2 * n][...], refs[2 * n + 1][...], dims)
                part = d if part is None else part + d
            finish(part)
        else:
            acc_ref = refs[n_own + 2 * nc + n_out]
            k = pl.program_id(kaxis)

            @pl.when(k == 0)
            def _():
                acc_ref[...] = jnp.zeros_like(acc_ref)

            _accumulate(acc_ref, [(refs[2 * n], refs[2 * n + 1]) for n in range(npairs)], dims)

            @pl.when(k == nk - 1)
            def _():
                finish(acc_ref[...])
        if nc:
            carry.ride(grid, riders, "finish")

    sem = tuple("arbitrary" if (nc or (ax == kaxis and nk > 1)) else "parallel" for ax in range(len(grid)))
    outs = pl.pallas_call(
        body, name=name, grid=grid, in_specs=specs,
        out_specs=[o_spec] * n_out + (carry.out_specs if nc else []),
        out_shape=[jax.ShapeDtypeStruct(o_shape, o_dtype)] + ([jax.ShapeDtypeStruct(o_shape, BF16)] if norm is not None else [])
        + (carry.out_shape if nc else []),
        scratch_shapes=([pltpu.VMEM(acc_shape, F32)] if nk > 1 else []) + (carry.scratch if nc else []),
        input_output_aliases=aliases,
        compiler_params=_params(sem),
    )(*operands)
    main = (outs[0], outs[1]) if norm is not None else outs[0]
    return (main, carry.place(outs[n_out:])) if nc else main


def _norm_fwd(name, h, gain_slab, row):
    t = h.shape[0]
    tm = min(ROW_TILE, t)

    def body(h_ref, g_ref, o_ref):
        x = h_ref[...]
        r = lax.rsqrt(jnp.mean(x * x, axis=-1, keepdims=True) + RMS_EPS)
        o_ref[...] = (x * r * g_ref[...]).astype(BF16)

    return pl.pallas_call(
        body, name=name, grid=(t // tm,),
        in_specs=[_spec((tm, D), lambda i: (i, 0)), _spec((None, 1, D), lambda i: (row, 0, 0))],
        out_specs=_spec((tm, D), lambda i: (i, 0)),
        out_shape=jax.ShapeDtypeStruct((t, D), BF16),
        compiler_params=_params(("parallel",)),
    )(h, gain_slab)


def _dhn_norm(name, pairs, dims, h, gain_slab, row, dres, rider=None):
    t = h.shape[0]
    tm = min(MM_TILE, t)
    nt = t // tm
    grid = (nt, NSH)
    npairs = len(pairs)
    nr = rider.n if rider is not None else 0
    operands, specs = [], []
    for a, a_spec, b, b_spec in pairs:
        operands += [a, b]
        specs += [a_spec, b_spec]
    row_spec = _spec((tm, D), lambda i, k: (i, 0))
    operands += [h, gain_slab, dres]
    specs += [row_spec, _spec((None, 1, D), lambda i, k: (row, 0, 0)), row_spec]
    n_own = len(operands)

    def body(*refs):
        h_ref, g_ref, dres_ref = refs[2 * npairs:n_own]
        dh_ref, dhb_ref, dg_ref = refs[n_own + nr:n_own + nr + 3]
        acc_ref, gacc_ref = refs[n_own + 2 * nr + 3:n_own + 2 * nr + 5]
        riders = (refs[n_own:n_own + nr], refs[n_own + nr + 3:n_own + 2 * nr + 3], refs[-2], refs[-1]) if nr else None
        if nr:
            rider.ride(grid, riders, "start")
        i, k = pl.program_id(0), pl.program_id(1)

        @pl.when(k == 0)
        def _():
            acc_ref[...] = jnp.zeros_like(acc_ref)

        _accumulate(acc_ref, [(refs[2 * n], refs[2 * n + 1]) for n in range(npairs)], dims)

        @pl.when(k == NSH - 1)
        def _():
            x = h_ref[...]
            r = lax.rsqrt(jnp.mean(x * x, axis=-1, keepdims=True) + RMS_EPS)
            xh = x * r
            dy = acc_ref[...]
            gdy = dy * g_ref[...]
            dh = dres_ref[...] + (gdy - xh * jnp.mean(gdy * xh, axis=-1, keepdims=True)) * r
            dh_ref[...] = dh
            dhb_ref[...] = dh.astype(BF16)
            gpart = jnp.sum((dy * xh).reshape(tm // 8, 8, D), axis=0)

            @pl.when(i == 0)
            def _():
                gacc_ref[...] = gpart

            @pl.when(i > 0)
            def _():
                gacc_ref[...] += gpart

            @pl.when(i == nt - 1)
            def _():
                dg_ref[...] = jnp.sum(gacc_ref[...], axis=0, keepdims=True)

        if nr:
            rider.ride(grid, riders, "finish")

    outs = pl.pallas_call(
        body, name=name, grid=grid, in_specs=specs + (rider.in_specs if nr else []),
        out_specs=[row_spec, row_spec, _spec((1, D), lambda i, k: (0, 0))] + (rider.out_specs if nr else []),
        out_shape=[jax.ShapeDtypeStruct((t, D), F32), jax.ShapeDtypeStruct((t, D), BF16),
                   jax.ShapeDtypeStruct((1, D), F32)] + (rider.out_shape if nr else []),
        scratch_shapes=[pltpu.VMEM((tm, D), F32), pltpu.VMEM((8, D), F32)] + (rider.scratch if nr else []),
        compiler_params=pltpu.CompilerParams(dimension_semantics=("arbitrary", "arbitrary"),
                                             vmem_limit_bytes=VMEM_LIMIT_BIG),
    )(*operands, *(rider.groups if nr else []))
    return outs[0], outs[1], outs[2], (rider.place(outs[3:]) if nr else [])


def _final_loss(h, gain_slab, target):
    t = h.shape[0]
    tm = min(ROW_TILE, t)
    nt = t // tm

    def body(h_ref, g_ref, t_ref, dh_ref, dhb_ref, dg_ref, loss_ref, acc_ref, lacc_ref):
        i = pl.program_id(0)
        x = h_ref[...]
        g = g_ref[...]
        r = lax.rsqrt(jnp.mean(x * x, axis=-1, keepdims=True) + RMS_EPS)
        xh = x * r
        err = xh * g - t_ref[...]
        dy = err * (1.0 / D)
        gdy = dy * g
        dh = (gdy - xh * jnp.mean(gdy * xh, axis=-1, keepdims=True)) * r
        dh_ref[...] = dh
        dhb_ref[...] = dh.astype(BF16)
        part = jnp.sum((dy * xh).reshape(tm // 8, 8, D), axis=0)
        lpart = jnp.sum((err * err).reshape(tm // 8, 8, D), axis=0)

        @pl.when(i == 0)
        def _():
            acc_ref[...] = part
            lacc_ref[...] = lpart

        @pl.when(i > 0)
        def _():
            acc_ref[...] += part
            lacc_ref[...] += lpart

        @pl.when(i == nt - 1)
        def _():
            dg_ref[...] = jnp.sum(acc_ref[...], axis=0, keepdims=True)
            rows = jnp.sum(lacc_ref[...], axis=0, keepdims=True)
            loss_ref[...] = jnp.sum(rows, axis=1, keepdims=True) * (0.5 / D)

    row_spec = _spec((tm, D), lambda i: (i, 0))
    return pl.pallas_call(
        body, name="final_loss", grid=(nt,),
        in_specs=[row_spec, _spec((None, 1, D), lambda i: (R_FIN, 0, 0)), row_spec],
        out_specs=[row_spec, row_spec, _spec((1, D), lambda i: (0, 0)), _spec((1, 1), lambda i: (0, 0))],
        out_shape=[jax.ShapeDtypeStruct((t, D), F32), jax.ShapeDtypeStruct((t, D), BF16),
                   jax.ShapeDtypeStruct((1, D), F32), jax.ShapeDtypeStruct((1, 1), F32)],
        scratch_shapes=[pltpu.VMEM((8, D), F32), pltpu.VMEM((8, D), F32)],
        compiler_params=_params(("arbitrary",)),
    )(h, gain_slab, target)


def _ffn_up(name, hn, wgu, carry=None):
    t = hn.shape[0]
    tm = min(MM_TILE, t)
    grid = (NSH, t // tm)
    nc = carry.n if carry is not None else 0

    def body(*refs):
        x_ref, wg_ref, wu_ref = refs[:3]
        s_up_ref, s_gate_ref, a_ref = refs[3 + nc:6 + nc]
        riders = (refs[3:3 + nc], refs[6 + nc:6 + 2 * nc], refs[-2], refs[-1]) if nc else None
        if nc:
            carry.ride(grid, riders, "start")
        x = x_ref[...]
        g = _dg(x, wg_ref[...], NT)
        u = _dg(x, wu_ref[...], NT)
        sg = _sigmoid(g)
        silu = g * sg
        s_up_ref[...] = (MACARON * silu).astype(BF16)
        s_gate_ref[...] = (MACARON * u * (sg * (1.0 + g * (1.0 - sg)))).astype(BF16)
        a_ref[...] = (silu * u).astype(BF16)
        if nc:
            carry.ride(grid, riders, "finish")

    act = _spec((None, tm, FS), lambda s, i: (s, i, 0))
    shape = jax.ShapeDtypeStruct((NSH, t, FS), BF16)
    outs = pl.pallas_call(
        body, name=name, grid=grid,
        in_specs=[_spec((tm, D), lambda s, i: (i, 0)),
                  _spec((None, None, FS, D), lambda s, i: (s, 0, 0, 0)),
                  _spec((None, None, FS, D), lambda s, i: (s, 1, 0, 0))] + (carry.in_specs if nc else []),
        out_specs=[act, act, act] + (carry.out_specs if nc else []),
        out_shape=[shape, shape, shape] + (carry.out_shape if nc else []),
        scratch_shapes=carry.scratch if nc else [],
        compiler_params=_params(("arbitrary", "arbitrary") if nc else ("parallel", "parallel")),
    )(hn, wgu, wgu, *(carry.groups if nc else []))
    return (outs[0], outs[1], outs[2], carry.place(outs[3:]) if nc else [])


def _ffn_down(name, a, wd, h, carry=None, norm=None):
    t = h.shape[0]
    tm = min(MM_TILE, t)
    return _mm(name, [(a, _spec((None, tm, FS), lambda i, k: (k, i, 0)),
                       wd, _spec((None, FS, D), lambda i, k: (k, 0, 0)))],
               grid=(t // tm, NSH), o_shape=(t, D), o_dtype=F32, o_spec=_spec((tm, D), lambda i, k: (i, 0)),
               dims=NN, kaxis=1, nk=NSH, acc_shape=(tm, D), res=(h, _spec((tm, D), lambda i, k: (i, 0))),
               scale=MACARON, carry=carry, norm=norm)


def _ffn_bwd_up(name, dhb, wd, s_up, s_gate, rider=None):
    t = dhb.shape[0]
    tm = min(MM_TILE, t)
    grid = (NSH, t // tm)
    nr = rider.n if rider is not None else 0

    def body(*refs):
        dh_ref, wd_ref, s_up_ref, s_gate_ref = refs[:4]
        dg_ref, du_ref = refs[4 + nr:6 + nr]
        riders = (refs[4:4 + nr], refs[6 + nr:6 + 2 * nr], refs[-2], refs[-1]) if nr else None
        if nr:
            rider.ride(grid, riders, "start")
        da = _dg(dh_ref[...], wd_ref[...], NT).astype(BF16)
        du_ref[...] = da * s_up_ref[...]
        dg_ref[...] = da * s_gate_ref[...]
        if nr:
            rider.ride(grid, riders, "finish")

    act = _spec((None, tm, FS), lambda s, i: (s, i, 0))
    shape = jax.ShapeDtypeStruct((NSH, t, FS), BF16)
    outs = pl.pallas_call(
        body, name=name, grid=grid,
        in_specs=[_spec((tm, D), lambda s, i: (i, 0)), _spec((None, FS, D), lambda s, i: (s, 0, 0)), act, act]
        + (rider.in_specs if nr else []),
        out_specs=[act, act] + (rider.out_specs if nr else []),
        out_shape=[shape, shape] + (rider.out_shape if nr else []),
        scratch_shapes=rider.scratch if nr else [],
        compiler_params=_params(("arbitrary", "arbitrary") if nr else ("parallel", "parallel")),
    )(dhb, wd, s_up, s_gate, *(rider.groups if nr else []))
    return outs[0], outs[1], (rider.place(outs[2:]) if nr else [])


def _wgrad(name, a, a_spec, b, b_spec, out_rows, out_cols, t, tt, group, slot, scale=None, carry=None):
    first = isinstance(group, int)
    rows = group if first else group.shape[1]
    return _mm(name, [(a, a_spec, b, b_spec)], grid=(NSH, t // tt),
               o_shape=(NSH, rows, out_cols), o_dtype=BF16,
               o_spec=_spec((None, out_rows, out_cols), lambda s, k: (s, slot, 0)),
               dims=TN, kaxis=1, nk=t // tt, acc_shape=(out_rows, out_cols), scale=scale,
               into=None if first else group, carry=carry)


def _ffn_backward(tag, dh, dhb, h_in, hn, s_up, s_gate, a, wgu, wd, gate_idx, up_idx, down_idx, small, norm_row,
                  grad_a, grad_b, rider_up=None, rider_dhn=None):
    t = dh.shape[0]
    tm = min(MM_TILE, t)
    tt = min(WGRAD_TILE, t)
    tok = _spec((tt, D), lambda s, k: (k, 0))
    hid = _spec((None, tt, FS), lambda s, k: (s, k, 0))
    grad_b = _wgrad(tag + "_dwd", a, hid, dhb, tok, FS, D, t, tt, grad_b, down_idx, scale=MACARON)
    dg, du, landed_up = _ffn_bwd_up(tag + "_bwd_up", dhb, wd, s_up, s_gate, rider_up)
    grad_a = _wgrad(tag + "_dwg", dg, hid, hn, tok, FS, D, t, tt, grad_a, gate_idx)
    grad_a = _wgrad(tag + "_dwu", du, hid, hn, tok, FS, D, t, tt, grad_a, up_idx)
    rider = rider_dhn(grad_a, grad_b) if rider_dhn is not None else None
    act = _spec((None, tm, FS), lambda i, k: (k, i, 0))
    dh_in, dhb_in, d_gain, landed_dhn = _dhn_norm(
        tag + "_dhn", [(dg, act, wgu, _spec((None, None, FS, D), lambda i, k: (k, 0, 0, 0))),
                       (du, act, wgu, _spec((None, None, FS, D), lambda i, k: (k, 1, 0, 0)))],
        NN, h_in, small, norm_row, dh, rider)
    return dh_in, dhb_in, grad_a, grad_b, d_gain, landed_up, landed_dhn


def _conv_fwd(proj_a, conv_w):
    t = proj_a.shape[0]
    tm = min(ROW_TILE, t)
    hb = tm // CONV_HALO

    def body(ab_ref, ac_ref, ax_ref, acp_ref, axp_ref, w_ref, y_ref):
        i = pl.program_id(1)
        u = ac_ref[...] * ax_ref[...]
        up = jnp.where(i > 0, acp_ref[...] * axp_ref[...], 0.0)
        ext = jnp.concatenate([up, u], axis=0)
        u1 = pltpu.roll(ext, 1, 0)[CONV_HALO:]
        u2 = pltpu.roll(ext, 2, 0)[CONV_HALO:]
        w = w_ref[...]
        conv = w[0:1] * u2 + w[1:2] * u1 + w[2:3] * u
        y_ref[...] = (ab_ref[...] * conv).astype(BF16)

    def cur(off):
        return _spec((tm, LANE), lambda j, i: (i, off + j))

    def prev(off):
        return _spec((CONV_HALO, LANE), lambda j, i: (jnp.maximum(i * hb - 1, 0), off + j))

    return pl.pallas_call(
        body, name="conv_fwd", grid=(4, t // tm),
        in_specs=[cur(0), cur(4), cur(8), prev(4), prev(8),
                  _spec((None, None, 8, LANE), lambda j, i: (j, 0, 0, 0))],
        out_specs=_spec((tm, LANE), lambda j, i: (i, j)),
        out_shape=jax.ShapeDtypeStruct((t, 512), BF16),
        compiler_params=_params(("parallel", "parallel")),
    )(proj_a, proj_a, proj_a, proj_a, proj_a, conv_w)


def _conv_bwd(proj_a, conv_w, dy, rider=None):
    t = proj_a.shape[0]
    tm = min(ROW_TILE, t)
    hb = tm // CONV_HALO
    nt = t // tm
    grid = (4, nt)
    nr = rider.n if rider is not None else 0

    def body(*refs):
        ab_ref, ac_ref, ax_ref, dy_ref, acp_ref, axp_ref, abn_ref, dyn_ref, w_ref = refs[:9]
        dab_ref, dac_ref, dax_ref, dw_ref = refs[9 + nr:13 + nr]
        acc_ref = refs[13 + 2 * nr]
        riders = (refs[9:9 + nr], refs[13 + nr:13 + 2 * nr], refs[-2], refs[-1]) if nr else None
        if nr:
            rider.ride(grid, riders, "start")
        i = pl.program_id(1)
        ab, ac, ax = ab_ref[...], ac_ref[...], ax_ref[...]
        u = ac * ax
        up = jnp.where(i > 0, acp_ref[...] * axp_ref[...], 0.0)
        ext = jnp.concatenate([up, u], axis=0)
        u1 = pltpu.roll(ext, 1, 0)[CONV_HALO:]
        u2 = pltpu.roll(ext, 2, 0)[CONV_HALO:]
        w = w_ref[...]
        conv = w[0:1] * u2 + w[1:2] * u1 + w[2:3] * u
        dy_v = dy_ref[...]
        dab_ref[...] = (dy_v * conv).astype(BF16)
        dc = dy_v * ab
        dcn = jnp.where(i < nt - 1, dyn_ref[...] * abn_ref[...], 0.0)
        extn = jnp.concatenate([dc, dcn], axis=0)
        n = tm + CONV_HALO
        dc1 = pltpu.roll(extn, n - 1, 0)[:tm]
        dc2 = pltpu.roll(extn, n - 2, 0)[:tm]
        du = w[2:3] * dc + w[1:2] * dc1 + w[0:1] * dc2
        dac_ref[...] = (du * ax).astype(BF16)
        dax_ref[...] = (du * ac).astype(BF16)
        rid = lax.broadcasted_iota(jnp.int32, (8, LANE), 0)
        part = jnp.where(rid == 0, jnp.sum(dc * u2, axis=0, keepdims=True),
                         jnp.where(rid == 1, jnp.sum(dc * u1, axis=0, keepdims=True),
                                   jnp.where(rid == 2, jnp.sum(dc * u, axis=0, keepdims=True), 0.0)))

        @pl.when(i == 0)
        def _():
            acc_ref[...] = part

        @pl.when(i > 0)
        def _():
            acc_ref[...] += part

        @pl.when(i == nt - 1)
        def _():
            dw_ref[...] = acc_ref[...]

        if nr:
            rider.ride(grid, riders, "finish")

    def cur(off):
        return _spec((tm, LANE), lambda j, i: (i, off + j))

    def prev(off):
        return _spec((CONV_HALO, LANE), lambda j, i: (jnp.maximum(i * hb - 1, 0), off + j))

    def nxt(off):
        return _spec((CONV_HALO, LANE), lambda j, i: (jnp.minimum((i + 1) * hb, nt * hb - 1), off + j))

    outs = pl.pallas_call(
        body, name="conv_bwd", grid=grid,
        in_specs=[cur(0), cur(4), cur(8), cur(0), prev(4), prev(8), nxt(0), nxt(0),
                  _spec((None, None, 8, LANE), lambda j, i: (j, 0, 0, 0))] + (rider.in_specs if nr else []),
        out_specs=[_spec((tm, LANE), lambda j, i: (i, j)), _spec((tm, LANE), lambda j, i: (i, j)),
                   _spec((tm, LANE), lambda j, i: (i, j)), _spec((None, 8, LANE), lambda j, i: (j, 0, 0))]
        + (rider.out_specs if nr else []),
        out_shape=[jax.ShapeDtypeStruct((t, 512), BF16), jax.ShapeDtypeStruct((t, 512), BF16),
                   jax.ShapeDtypeStruct((t, 512), BF16), jax.ShapeDtypeStruct((4, 8, LANE), F32)]
        + (rider.out_shape if nr else []),
        scratch_shapes=[pltpu.VMEM((8, LANE), F32)] + (rider.scratch if nr else []),
        compiler_params=_params(("arbitrary", "arbitrary") if nr else ("parallel", "arbitrary")),
    )(proj_a, proj_a, proj_a, dy, proj_a, proj_a, proj_a, dy, conv_w, *(rider.groups if nr else []))
    return outs[0], outs[1], outs[2], outs[3], (rider.place(outs[4:]) if nr else [])


def _log_sigmoid(z):
    return jnp.minimum(z, 0.0) - jnp.log(1.0 + jnp.exp(-jnp.abs(z)))


def _sb_masks():
    row = lax.broadcasted_iota(jnp.int32, (SBQ, SBQ), 0)
    col = lax.broadcasted_iota(jnp.int32, (SBQ, SBQ), 1)
    return row, col


def _ones_where(mask):
    return jnp.where(mask, 1.0, 0.0).astype(BF16)


def _head_mask(head):
    lane = lax.broadcasted_iota(jnp.int32, (1, LANE), 1)
    return lane >= 64 if head else lane < 64


def _sb_fwd(proj_b, carry=None):
    t = proj_b.shape[0]
    nq = t // SBQ
    scale = 1.0 / math.sqrt(64.0)

    npair = SB_PAIRS_FWD
    wide = npair * LANE
    ngrp = 4 // npair
    chains = [(p, head) for p in range(npair) for head in range(2)]

    grid = (ngrp, nq)
    nc = carry.n if carry is not None else 0

    def body(*refs):
        q_ref, k_ref, v_ref = refs[:3]
        y_ref, l_ref, n_ref = refs[3 + nc:6 + nc]
        riders = (refs[3:3 + nc], refs[6 + nc:6 + 2 * nc], refs[-2], refs[-1]) if nc else None
        if nc:
            carry.ride(grid, riders, "start")
        grp = pl.program_id(0)
        qi = pl.program_id(1)
        row, col = _sb_masks()
        m_suffix = _ones_where(row > col)
        rows = len(chains) * SBQ
        strict = (lax.broadcasted_iota(jnp.int32, (rows, SBQ), 1)
                  < (lax.broadcasted_iota(jnp.int32, (rows, SBQ), 0) & (SBQ - 1)))
        q_pair = []
        for p in range(npair):
            q_all = q_ref[:, p * LANE:(p + 1) * LANE]
            q_pair.append(jnp.concatenate([jnp.where(_head_mask(head), q_all, jnp.zeros_like(q_all)) for head in range(2)],
                                          axis=0))

        def block(kb, state, diag):
            run, acc = state
            start = pl.multiple_of(kb * SBQ, SBQ)
            z = jnp.concatenate([_dg(q_pair[p], k_ref[pl.ds(start, SBQ), p * LANE:(p + 1) * LANE], NT)
                                 for p in range(npair)], axis=0) * scale
            lb = _log_sigmoid(z)
            lk = lb - z
            if diag:
                lk = jnp.where(strict, lk, 0.0)
            hi, lo = _split(lk)
            sums = _dg(jnp.concatenate([hi, lo], axis=0), m_suffix, NN)
            w = jnp.exp(lb + (run + sums[:rows] + sums[rows:]))
            if diag:
                w = jnp.where(strict, w, 0.0)
            wb = w.astype(BF16)
            acc = acc + jnp.concatenate(
                [_dg(wb[2 * p * SBQ:2 * (p + 1) * SBQ], v_ref[pl.ds(start, SBQ), p * LANE:(p + 1) * LANE], NN)
                 for p in range(npair)], axis=0)
            run = run + jnp.sum(hi.astype(F32) + lo.astype(F32), axis=1, keepdims=True)
            return run, acc

        state = block(qi, (jnp.zeros((rows, 1), F32), jnp.zeros((rows, LANE), F32)), True)

        def live(c):
            return jnp.logical_and(c[0] < qi, jnp.max(c[1][0]) > SB_DEAD)

        def step(c):
            return c[0] + 1, block(qi - 1 - c[0], c[1], False)

        count, (run, acc) = lax.while_loop(live, step, (jnp.int32(0), state))
        n_ref[grp * nq + qi] = count.astype(F32)
        hm = _head_mask(0)
        for p in range(npair):
            lo_rows, hi_rows = slice(2 * p * SBQ, (2 * p + 1) * SBQ), slice((2 * p + 1) * SBQ, (2 * p + 2) * SBQ)
            y_ref[:, p * LANE:(p + 1) * LANE] = jnp.where(hm, acc[lo_rows], acc[hi_rows]).astype(BF16)
            l_ref[p] = jnp.where(hm, run[lo_rows], run[hi_rows])
        if nc:
            carry.ride(grid, riders, "finish")

    outs = pl.pallas_call(
        body, name="sb_fwd", grid=grid,
        in_specs=[_spec((SBQ, wide), lambda g, i: (i, g)),
                  _spec((t, wide), lambda g, i: (0, ngrp + g)),
                  _spec((t, wide), lambda g, i: (0, 2 * ngrp + g))] + (carry.in_specs if nc else []),
        out_specs=[_spec((SBQ, wide), lambda g, i: (i, g)), _spec((npair, SBQ, LANE), lambda g, i: (g, i, 0)),
                   pl.BlockSpec(memory_space=pltpu.SMEM)] + (carry.out_specs if nc else []),
        out_shape=[jax.ShapeDtypeStruct((t, 512), BF16), jax.ShapeDtypeStruct((4, t, LANE), F32),
                   jax.ShapeDtypeStruct((ngrp * nq,), F32)] + (carry.out_shape if nc else []),
        scratch_shapes=carry.scratch if nc else [],
        compiler_params=_params(("arbitrary", "arbitrary")),
    )(proj_b, proj_b, proj_b, *(carry.groups if nc else []))
    return outs[0], outs[1], outs[2], (carry.place(outs[3:]) if nc else [])


def _sb_bwd(proj_b, dy, ltot, nblk, rider=None):
    t = proj_b.shape[0]
    nq = t // SBQ
    scale = 1.0 / math.sqrt(64.0)

    npair = SB_PAIRS_BWD
    wide = npair * LANE
    ngrp = 4 // npair
    chains = [(p, head) for p in range(npair) for head in range(2)]
    grid = (ngrp, nq)
    nr = rider.n if rider is not None else 0
    per_count = SB_PAIRS_FWD // SB_PAIRS_BWD

    def body(*refs):
        q_ref, k_ref, v_ref, dy_ref, l_ref, n_ref = refs[:6]
        dq_ref, dk_ref, dv_ref = refs[6 + nr:9 + nr]
        dk_acc, dv_acc = refs[9 + 2 * nr:11 + 2 * nr]
        riders = (refs[6:6 + nr], refs[9 + nr:9 + 2 * nr], refs[-2], refs[-1]) if nr else None
        if nr:
            rider.ride(grid, riders, "start")
        grp = pl.program_id(0)
        qi = pl.program_id(1)

        @pl.when(qi == 0)
        def _():
            dk_acc[...] = jnp.zeros_like(dk_acc)
            dv_acc[...] = jnp.zeros_like(dv_acc)

        row, col = _sb_masks()
        m_prefix = _ones_where(row <= col)
        m_before = _ones_where(row < col)
        rows = len(chains) * SBQ
        strict = (lax.broadcasted_iota(jnp.int32, (rows, SBQ), 1)
                  < (lax.broadcasted_iota(jnp.int32, (rows, SBQ), 0) & (SBQ - 1)))
        q_pair, do_pair, ltot = [], [], []
        for p in range(npair):
            pl_ = slice(p * LANE, (p + 1) * LANE)
            q_all = q_ref[:, pl_]
            do_all = dy_ref[:, pl_].astype(BF16)
            q_pair.append(jnp.concatenate([jnp.where(_head_mask(h), q_all, jnp.zeros_like(q_all)) for h in range(2)], axis=0))
            do_pair.append(jnp.concatenate([jnp.where(_head_mask(h), do_all, jnp.zeros_like(do_all)) for h in range(2)], axis=0))
            ltot += [l_ref[p][:, h * 64:h * 64 + 1] for h in range(2)]
        ltot = jnp.concatenate(ltot, axis=0)

        def pair_rows(a, p):
            return a[2 * p * SBQ:2 * (p + 1) * SBQ]

        def block(kb, state, diag):
            seen, dseen, dq = state
            start = pl.multiple_of(kb * SBQ, SBQ)
            kk = [k_ref[pl.ds(start, SBQ), p * LANE:(p + 1) * LANE] for p in range(npair)]
            vv = [v_ref[pl.ds(start, SBQ), p * LANE:(p + 1) * LANE] for p in range(npair)]
            z = jnp.concatenate([_dg(q_pair[p], kk[p], NT) for p in range(npair)], axis=0) * scale
            lb = _log_sigmoid(z)
            lk = lb - z
            if diag:
                lk = jnp.where(strict, lk, 0.0)
            hi, lo = _split(lk)
            sums = _dg(jnp.concatenate([hi, lo], axis=0), m_prefix, NN)
            w = jnp.exp(lb + ((ltot - seen) - (sums[:rows] + sums[rows:])))
            if diag:
                w = jnp.where(strict, w, 0.0)
            wb = w.astype(BF16)
            da = w * jnp.concatenate([_dg(do_pair[p], vv[p], NT) for p in range(npair)], axis=0)
            dah, dal = _split(da)
            dsums = _dg(jnp.concatenate([dah, dal], axis=0), m_before, NN)
            sig = jnp.exp(lb)
            dz = (da * (1.0 - sig) - (dseen + dsums[:rows] + dsums[rows:]) * sig) * scale
            if diag:
                dz = jnp.where(strict, dz, 0.0)
            dzb = dz.astype(BF16)
            for p in range(npair):
                pl_ = slice(p * LANE, (p + 1) * LANE)
                dv_acc[pl.ds(start, SBQ), pl_] += _dg(pair_rows(wb, p), do_pair[p], TN)
                dk_acc[pl.ds(start, SBQ), pl_] += _dg(pair_rows(dzb, p), q_pair[p], TN)
            dq = dq + jnp.concatenate([_dg(pair_rows(dzb, p), kk[p], NN) for p in range(npair)], axis=0)
            seen = seen + jnp.sum(hi.astype(F32) + lo.astype(F32), axis=1, keepdims=True)
            dseen = dseen + jnp.sum(da, axis=1, keepdims=True)
            return seen, dseen, dq

        zero = (jnp.zeros((rows, 1), F32), jnp.zeros((rows, 1), F32), jnp.zeros((rows, LANE), F32))
        first = qi - n_ref[(grp // per_count) * nq + qi].astype(jnp.int32)
        state = lax.fori_loop(first, qi, lambda kb, c: block(kb, c, False), zero)
        _, _, dq = block(qi, state, True)
        for p in range(npair):
            dq_ref[:, p * LANE:(p + 1) * LANE] = jnp.where(
                _head_mask(0), dq[2 * p * SBQ:(2 * p + 1) * SBQ], dq[(2 * p + 1) * SBQ:(2 * p + 2) * SBQ]).astype(BF16)

        @pl.when(qi == nq - 1)
        def _():
            dk_ref[...] = dk_acc[...].astype(BF16)
            dv_ref[...] = dv_acc[...].astype(BF16)

        if nr:
            rider.ride(grid, riders, "finish")

    full = jax.ShapeDtypeStruct((t, 512), BF16)
    outs = pl.pallas_call(
        body, name="sb_bwd", grid=grid,
        in_specs=[_spec((SBQ, wide), lambda g, i: (i, g)),
                  _spec((t, wide), lambda g, i: (0, ngrp + g)),
                  _spec((t, wide), lambda g, i: (0, 2 * ngrp + g)),
                  _spec((SBQ, wide), lambda g, i: (i, ngrp + g)),
                  _spec((npair, SBQ, LANE), lambda g, i: (g, i, 0)),
                  pl.BlockSpec(memory_space=pltpu.SMEM)] + (rider.in_specs if nr else []),
        out_specs=[_spec((SBQ, wide), lambda g, i: (i, g)),
                   _spec((t, wide), lambda g, i: (0, g)), _spec((t, wide), lambda g, i: (0, g))]
        + (rider.out_specs if nr else []),
        out_shape=[full, full, full] + (rider.out_shape if nr else []),
        scratch_shapes=[pltpu.VMEM((t, wide), F32), pltpu.VMEM((t, wide), F32)] + (rider.scratch if nr else []),
        compiler_params=_params(("arbitrary", "arbitrary")),
    )(proj_b, proj_b, proj_b, dy, ltot, nblk, *(rider.groups if nr else []))
    return outs[0], outs[1], outs[2], (rider.place(outs[3:]) if nr else [])


def _hgrn_gates(qr, fr, c0, c1):
    mx = jnp.maximum(c0, c1)
    e0, e1 = jnp.exp(c0 - mx), jnp.exp(c1 - mx)
    lb = e1 / (e0 + e1)
    sx = _sigmoid(fr)
    f = lb + (1.0 - lb) * sx
    k = (1.0 - lb) * (1.0 - sx)
    sq = _sigmoid(qr)
    return lb, sx, f, k, sq, qr * sq


def _chunk_sums(mask, x):
    n = x.shape[1]
    hi, lo = _split(x)
    both = _dg(_ones_where(mask), jnp.concatenate([hi, lo], axis=1), NN)
    return both[:, :n] + both[:, n:]


def _chunk_masks():
    row = lax.broadcasted_iota(jnp.int32, (CHUNK, CHUNK), 0)
    col = lax.broadcasted_iota(jnp.int32, (CHUNK, CHUNK), 1)
    return col <= row, col >= row


def _hgrn_fwd(proj_c, small, carry=None):
    t = proj_c.shape[1]
    nc = t // CHUNK
    nh = D // HD

    hps = HGRN_HPS
    wide = hps * HD
    grid = (nh // hps, nc)
    nr = carry.n if carry is not None else 0

    def body(*refs):
        p_ref, c0_ref, c1_ref, gam_ref = refs[:4]
        o_ref, y_ref, sst_ref = refs[4 + nr:7 + nr]
        st_ref = refs[7 + 2 * nr]
        riders = (refs[4:4 + nr], refs[7 + nr:7 + 2 * nr], refs[-2], refs[-1]) if nr else None
        if nr:
            carry.ride(grid, riders, "start")
        c = pl.program_id(1)

        @pl.when(c == 0)
        def _():
            st_ref[...] = jnp.zeros_like(st_ref)

        _, _, f_all, k_all, _, q_all = _hgrn_gates(p_ref[0], p_ref[1], c0_ref[...], c1_ref[...])
        tril, _ = _chunk_masks()
        b_all = _chunk_sums(tril, jnp.log(f_all))
        for j in range(hps):
            ln = slice(j * HD, (j + 1) * HD)
            st0 = st_ref[j]
            sst_ref[j] = st0
            v, g = p_ref[2, :, ln], p_ref[3, :, ln]
            q, k, b = q_all[:, ln], k_all[:, ln], b_all[:, ln]
            bm = b[CHUNK // 2 - 1:CHUNK // 2]
            bl = b[CHUNK - 1:CHUNK]
            qd = q * jnp.exp(b)
            qt = q * jnp.exp(b - bm)
            kt = k * jnp.exp(bm - b)
            kl = k * jnp.exp(bl - b)
            vb = v.astype(BF16)
            att = jnp.where(tril, _dot3(qt, kt, NT), 0.0)
            o = _dg(qd.astype(BF16), st0.astype(BF16), NT) + _dg(att.astype(BF16), vb, NN)
            st_ref[j] = st0 * jnp.exp(bl) + _dg(vb, kl.astype(BF16), TN)
            o_ref[:, ln] = o
            r = lax.rsqrt(jnp.mean(o * o, axis=-1, keepdims=True) + RMS_EPS)
            y_ref[:, ln] = (o * r * gam_ref[...] * (g * _sigmoid(g))).astype(BF16)
        if nr:
            carry.ride(grid, riders, "finish")

    outs = pl.pallas_call(
        body, name="hgrn_fwd", grid=grid,
        in_specs=[_spec((4, CHUNK, wide), lambda h, c: (0, c, h)),
                  _spec((None, 1, wide), lambda h, c: (R_CLB, 0, h)),
                  _spec((None, 1, wide), lambda h, c: (R_CLB + 1, 0, h)),
                  _spec((None, 1, HD), lambda h, c: (R_GAM, 0, 0))] + (carry.in_specs if nr else []),
        out_specs=[_spec((CHUNK, wide), lambda h, c: (c, h)), _spec((CHUNK, wide), lambda h, c: (c, h)),
                   _spec((None, hps, HD, HD), lambda h, c: (c, h, 0, 0))] + (carry.out_specs if nr else []),
        out_shape=[jax.ShapeDtypeStruct((t, D), F32), jax.ShapeDtypeStruct((t, D), BF16),
                   jax.ShapeDtypeStruct((nc, nh, HD, HD), F32)] + (carry.out_shape if nr else []),
        scratch_shapes=[pltpu.VMEM((hps, HD, HD), F32)] + (carry.scratch if nr else []),
        compiler_params=_params(("arbitrary", "arbitrary")),
    )(proj_c, small, small, small, *(carry.groups if nr else []))
    return outs[0], outs[1], outs[2], (carry.place(outs[3:]) if nr else [])


def _hgrn_bwd(proj_c, small, o, sst, dyc, rider=None):
    t = proj_c.shape[1]
    nc = t // CHUNK
    nh = D // HD

    hps = HGRN_HPS
    wide = hps * HD
    ng = nh // hps
    grid = (ng, nc)
    nr = rider.n if rider is not None else 0

    def body(*refs):
        p_ref, c0_ref, c1_ref, gam_ref, o_ref, sst_ref, dy_ref = refs[:7]
        dp_ref, dclb_ref, dgam_ref = refs[7 + nr:10 + nr]
        dst_ref, dlb_acc, dgam_acc = refs[10 + 2 * nr:13 + 2 * nr]
        riders = (refs[7:7 + nr], refs[10 + nr:10 + 2 * nr], refs[-2], refs[-1]) if nr else None
        if nr:
            rider.ride(grid, riders, "start")
        group = pl.program_id(0)
        step = pl.program_id(1)

        @pl.when(step == 0)
        def _():
            dst_ref[...] = jnp.zeros_like(dst_ref)
            dlb_acc[...] = jnp.zeros_like(dlb_acc)

        @pl.when((step == 0) & (group == 0))
        def _():
            dgam_acc[...] = jnp.zeros_like(dgam_acc)

        gam = gam_ref[...]
        qr_all = p_ref[0]
        lb_all, sx_all, f_all, k_all, sq_all, q_all = _hgrn_gates(qr_all, p_ref[1], c0_ref[...], c1_ref[...])
        tril, triu = _chunk_masks()
        b_all = _chunk_sums(tril, jnp.log(f_all))
        dq_parts, dk_parts, db_parts, dgam_parts = [], [], [], []
        for j in range(hps):
            ln = slice(j * HD, (j + 1) * HD)
            st0 = sst_ref[j]
            dst1 = dst_ref[j]
            v, g = p_ref[2, :, ln], p_ref[3, :, ln]
            q, k, b = q_all[:, ln], k_all[:, ln], b_all[:, ln]
            bm = b[CHUNK // 2 - 1:CHUNK // 2]
            bl = b[CHUNK - 1:CHUNK]
            eb = jnp.exp(b)
            e_qt = jnp.exp(b - bm)
            e_kt = jnp.exp(bm - b)
            e_kl = jnp.exp(bl - b)
            e_bl = jnp.exp(bl)
            qd, qt, kt, kl = q * eb, q * e_qt, k * e_kt, k * e_kl
            ov = o_ref[:, ln]
            r = lax.rsqrt(jnp.mean(ov * ov, axis=-1, keepdims=True) + RMS_EPS)
            oh = ov * r
            sg = _sigmoid(g)
            dy = dy_ref[:, ln]
            dp_ref[3, :, ln] = (dy * oh * gam * (sg * (1.0 + g * (1.0 - sg)))).astype(BF16)
            dyv = dy * (g * sg)
            gdy = dyv * gam
            do = (gdy - oh * jnp.mean(gdy * oh, axis=-1, keepdims=True)) * r
            dob, vb = do.astype(BF16), v.astype(BF16)
            st0b, dst1b = st0.astype(BF16), dst1.astype(BF16)
            st1 = st0 * e_bl + _dg(vb, kl.astype(BF16), TN)
            att = jnp.where(tril, _dot3(qt, kt, NT), 0.0)
            datt = jnp.where(tril, _dg(dob, vb, NT), 0.0)
            dv = _dg(att.astype(BF16), dob, TN) + _dg(kl.astype(BF16), dst1b, NT)
            dq = _dot3(datt, kt, NN) * e_qt + _dg(dob, st0b, NN) * eb
            dk = _dot3(datt, qt, TN) * e_kt + _dg(vb, dst1b, NN) * e_kl
            db = q * dq - k * dk
            last = lax.broadcasted_iota(jnp.int32, (CHUNK, 1), 0) == CHUNK - 1
            db = db + jnp.where(last, jnp.sum(dst1 * st1, axis=0, keepdims=True), 0.0)
            dst_ref[j] = dst1 * e_bl + _dg(dob, qd.astype(BF16), TN)
            dp_ref[2, :, ln] = dv.astype(BF16)
            dq_parts.append(dq)
            dk_parts.append(dk)
            db_parts.append(db)
            dgam_parts.append(jnp.sum(dyv * oh, axis=0, keepdims=True))

        dq_all, dk_all = jnp.concatenate(dq_parts, axis=1), jnp.concatenate(dk_parts, axis=1)
        dlf = _chunk_sums(triu, jnp.concatenate(db_parts, axis=1))
        dp_ref[0] = (dq_all * (sq_all * (1.0 + qr_all * (1.0 - sq_all)))).astype(BF16)
        tmp = dlf / f_all - dk_all
        dp_ref[1] = (tmp * (1.0 - lb_all) * sx_all * (1.0 - sx_all)).astype(BF16)
        dlb_acc[...] += jnp.sum((1.0 - sx_all) * tmp, axis=0, keepdims=True)
        dgam_acc[...] += functools.reduce(lambda a, b: a + b, dgam_parts)

        @pl.when(step == nc - 1)
        def _():
            d1 = dlb_acc[...] * lb_all * (1.0 - lb_all)
            dclb_ref[...] = jnp.where(lax.broadcasted_iota(jnp.int32, (2, wide), 0) == 0, -d1, d1)

        @pl.when((step == nc - 1) & (group == ng - 1))
        def _():
            dgam_ref[...] = dgam_acc[...]

        if nr:
            rider.ride(grid, riders, "finish")

    rev = lambda h, s: (nc - 1 - s, h)
    outs = pl.pallas_call(
        body, name="hgrn_bwd", grid=grid,
        in_specs=[_spec((4, CHUNK, wide), lambda h, s: (0, nc - 1 - s, h)),
                  _spec((None, 1, wide), lambda h, s: (R_CLB, 0, h)),
                  _spec((None, 1, wide), lambda h, s: (R_CLB + 1, 0, h)),
                  _spec((None, 1, HD), lambda h, s: (R_GAM, 0, 0)),
                  _spec((CHUNK, wide), rev),
                  _spec((None, hps, HD, HD), lambda h, s: (nc - 1 - s, h, 0, 0)),
                  _spec((CHUNK, wide), rev)] + (rider.in_specs if nr else []),
        out_specs=[_spec((4, CHUNK, wide), lambda h, s: (0, nc - 1 - s, h)),
                   _spec((2, wide), lambda h, s: (0, h)),
                   _spec((1, HD), lambda h, s: (0, 0))] + (rider.out_specs if nr else []),
        out_shape=[jax.ShapeDtypeStruct((4, t, D), BF16), jax.ShapeDtypeStruct((2, D), F32),
                   jax.ShapeDtypeStruct((1, HD), F32)] + (rider.out_shape if nr else []),
        scratch_shapes=[pltpu.VMEM((hps, HD, HD), F32), pltpu.VMEM((1, wide), F32), pltpu.VMEM((1, HD), F32)]
        + (rider.scratch if nr else []),
        compiler_params=_params(("arbitrary", "arbitrary")),
    )(proj_c, small, small, small, o, sst, dyc, *(rider.groups if nr else []))
    return outs[0], outs[1], outs[2], (rider.place(outs[3:]) if nr else [])


def _adamw(name, w, grads, m, v):
    nl = len(grads)
    rows, cols = grads[0].shape
    br = rows
    for cand in (512, 352, 256):
        if rows % cand == 0:
            br = cand
            break
    nb = rows // br
    c1 = 1.0 - ADAM_B1 ** ADAM_STEP
    c2 = 1.0 - ADAM_B2 ** ADAM_STEP

    def body(w_ref, m_ref, v_ref, *refs):
        g_refs, (d_ref, mo_ref, vo_ref, go_ref) = refs[:nl], refs[nl:]
        layer = pl.program_id(0)
        gv = g_refs[0][...]
        for k in range(1, nl):
            gv = jnp.where(layer == k, g_refs[k][...], gv)
        mn = ADAM_B1 * m_ref[...] + (1.0 - ADAM_B1) * gv
        vn = ADAM_B2 * v_ref[...] + (1.0 - ADAM_B2) * (gv * gv)
        mo_ref[...] = mn
        vo_ref[...] = vn
        go_ref[...] = gv
        d_ref[...] = -ADAM_LR * ((mn / c1) / (jnp.sqrt(vn / c2) + ADAM_EPS) + ADAM_WD * w_ref[...])

    blk = _spec((br, cols), lambda l, i: (l * nb + i, 0))
    g_specs = [_spec((br, cols), lambda l, i, k=k: (jnp.where(l == k, i, 0), 0)) for k in range(nl)]
    shape = jax.ShapeDtypeStruct((nl * rows, cols), F32)
    return pl.pallas_call(
        body, name=name, grid=(nl, nb), in_specs=[blk] * 3 + g_specs, out_specs=[blk] * 4, out_shape=[shape] * 4,
        compiler_params=_params(("arbitrary", "arbitrary")),
    )(w, m, v, *grads)


def _place():
    x, y, c = lax.axis_index("x"), lax.axis_index("y"), lax.axis_index("c")
    chips = [(1 - x, y), (x, 1 - y), (1 - x, 1 - y)]
    return x, y, c, chips


class _Rider:
    n = 0
    relay = None

    def ride(self, grid, refs, when):
        if not self.n:
            return
        ids = [pl.program_id(a) for a in range(len(grid))]
        first = functools.reduce(jnp.logical_and, [i == 0 for i in ids])
        last = functools.reduce(jnp.logical_and, [i == g - 1 for i, g in zip(ids, grid)])
        phases = [(first, self.start), (last, self.relay)] if when == "start" else [(last, self.finish)]
        for cond, phase in phases:
            if phase is not None:
                pl.when(cond)(functools.partial(phase, *refs))


class _Gather(_Rider):
    def __init__(self, groups):
        self.groups = list(groups)
        self.n = len(self.groups)
        self.in_specs = [ANY] * self.n
        self.out_specs = [ANY] * self.n
        self.out_shape = [jax.ShapeDtypeStruct((NSH,) + g.shape, g.dtype) for g in self.groups]
        self.scratch = [pltpu.SemaphoreType.DMA((6 * self.n,)), pltpu.SemaphoreType.DMA((6 * self.n,))] if self.n else []

    def _copies(self, ins, outs, send, recv):
        x, y, c, chips = _place()
        sibling = (x, y, 1 - c)

        def half(gi, chip, hc):
            return outs[gi].at[2 * chip[0] + chip[1], hc]

        def copy(gi, k, src, dst, to):
            return pltpu.make_async_remote_copy(src_ref=src, dst_ref=dst, send_sem=send.at[6 * gi + k],
                                                recv_sem=recv.at[6 * gi + k], device_id=to, device_id_type=MESH)

        pairs = [(gi, j, chip) for gi in range(self.n) for j, chip in enumerate(chips)]
        first = [copy(gi, j, ins[gi].at[c], half(gi, (x, y), c), (*chip, c)) for gi, j, chip in pairs]
        landed = [copy(gi, j, half(gi, chip, c), half(gi, chip, c), sibling) for gi, j, chip in pairs]
        relay = [copy(gi, 3 + j, half(gi, chip, c), half(gi, chip, c), sibling) for gi, j, chip in pairs]
        relayed = [copy(gi, 3 + j, half(gi, chip, 1 - c), half(gi, chip, 1 - c), sibling) for gi, j, chip in pairs]
        return first, landed, relay, relayed

    def start(self, ins, outs, send, recv):
        for cp in self._copies(ins, outs, send, recv)[0]:
            cp.start()

    def relay(self, ins, outs, send, recv):
        _, landed, relay, _ = self._copies(ins, outs, send, recv)
        for arrived, onward in zip(landed, relay):
            arrived.wait_recv()
            onward.start()

    def finish(self, ins, outs, send, recv):
        first, _, relay, relayed = self._copies(ins, outs, send, recv)
        for cp in relayed:
            cp.wait_recv()
        for cp in first + relay:
            cp.wait_send()

    def place(self, outs):
        me = 2 * lax.axis_index("x") + lax.axis_index("y")
        return [lax.dynamic_update_index_in_dim(o, g, me, 0) for o, g in zip(outs, self.groups)]


def _alone(name, rider):
    n = rider.n

    def body(*refs):
        parts = (refs[:n], refs[n:2 * n], refs[2 * n], refs[2 * n + 1])
        rider.start(*parts)
        if rider.relay is not None:
            rider.relay(*parts)
        rider.finish(*parts)

    outs = pl.pallas_call(
        body, name=name, in_specs=rider.in_specs, out_specs=rider.out_specs, out_shape=rider.out_shape,
        scratch_shapes=rider.scratch, compiler_params=pltpu.CompilerParams(has_side_effects=True),
    )(*rider.groups)
    return rider.place(outs)


class _Swap(_Rider):
    def __init__(self, slots):
        self.slots = list(slots)
        self.groups = [buf for buf, _, _ in self.slots]
        self.n = len(self.slots)
        self.in_specs = [ANY] * self.n
        self.out_specs = [ANY] * self.n
        self.out_shape = [jax.ShapeDtypeStruct((NSH, rows // 2, buf.shape[2]), buf.dtype) for buf, _, rows in self.slots]
        self.scratch = [pltpu.SemaphoreType.DMA((self.n,)), pltpu.SemaphoreType.DMA((self.n,))]

    def _copies(self, ins, outs, send, recv):
        x, y, c, _ = _place()
        cps = []
        for i, (_, row0, rows) in enumerate(self.slots):
            half = rows // 2
            src = ins[i].at[:, pl.ds(pl.multiple_of(row0 + (1 - c) * half, 16), half)]
            cps.append(pltpu.make_async_remote_copy(src_ref=src, dst_ref=outs[i], send_sem=send.at[i],
                                                    recv_sem=recv.at[i], device_id=(x, y, 1 - c), device_id_type=MESH))
        return cps

    def start(self, ins, outs, send, recv):
        for cp in self._copies(ins, outs, send, recv):
            cp.start()

    def finish(self, ins, outs, send, recv):
        for cp in self._copies(ins, outs, send, recv):
            cp.wait()

    def place(self, outs):
        return list(outs)


class _Share(_Rider):
    def __init__(self, arrays):
        self.groups = list(arrays)
        self.n = len(self.groups)
        self.in_specs = [ANY] * self.n
        self.out_specs = [ANY] * self.n
        self.out_shape = [jax.ShapeDtypeStruct((2,) + g.shape, g.dtype) for g in self.groups]
        self.scratch = [pltpu.SemaphoreType.DMA((self.n,)), pltpu.SemaphoreType.DMA((self.n,))]

    def _copies(self, ins, outs, send, recv, half):
        x, y, c, _ = _place()
        return [pltpu.make_async_remote_copy(src_ref=ins[gi], dst_ref=outs[gi].at[c if half == "mine" else 1 - c],
                                             send_sem=send.at[gi], recv_sem=recv.at[gi], device_id=(x, y, 1 - c),
                                             device_id_type=MESH) for gi in range(self.n)]

    def start(self, ins, outs, send, recv):
        for cp in self._copies(ins, outs, send, recv, "mine"):
            cp.start()

    def finish(self, ins, outs, send, recv):
        for cp in self._copies(ins, outs, send, recv, "theirs"):
            cp.wait_recv()
        for cp in self._copies(ins, outs, send, recv, "mine"):
            cp.wait_send()

    def place(self, outs):
        c = lax.axis_index("c")
        return [lax.dynamic_update_index_in_dim(o, g, c, 0) for o, g in zip(outs, self.groups)]


class _Send(_Rider):
    def __init__(self, arrays):
        self.groups = list(arrays)
        self.n = len(self.groups)
        self.in_specs = [ANY] * self.n
        self.out_specs = [ANY] * self.n
        self.out_shape = [jax.ShapeDtypeStruct((3,) + g.shape[1:], g.dtype) for g in self.groups]
        self.scratch = [pltpu.SemaphoreType.DMA((3 * self.n,)), pltpu.SemaphoreType.DMA((3 * self.n,))]

    def _copies(self, ins, outs, send, recv):
        x, y, c, chips = _place()
        return [pltpu.make_async_remote_copy(src_ref=ins[gi].at[2 * chip[0] + chip[1]], dst_ref=outs[gi].at[j],
                                             send_sem=send.at[3 * gi + j], recv_sem=recv.at[3 * gi + j],
                                             device_id=(*chip, c), device_id_type=MESH)
                for gi in range(self.n) for j, chip in enumerate(chips)]

    def start(self, ins, outs, send, recv):
        for cp in self._copies(ins, outs, send, recv):
            cp.start()

    def finish(self, ins, outs, send, recv):
        for cp in self._copies(ins, outs, send, recv):
            cp.wait()

    def place(self, outs):
        return list(outs)


def _pair_sum(name, slots, got, c_idx):
    n = len(slots)
    in_specs, out_specs, out_shape, operands = [], [], [], []
    for (buf, row0, rows), g in zip(slots, got):
        hb, cols = rows // 4, buf.shape[2]
        in_specs += [pl.BlockSpec((None, hb, cols), lambda q, i, cr, r=row0 // hb: (q, r + 2 * cr[0] + i, 0)),
                     pl.BlockSpec((None, hb, cols), lambda q, i, cr: (q, i, 0))]
        out_specs.append(pl.BlockSpec((None, hb, cols), lambda q, i, cr: (q, i, 0)))
        out_shape.append(jax.ShapeDtypeStruct(g.shape, BF16))
        operands += [buf, g]

    def body(c_ref, *refs):
        for i in range(n):
            refs[2 * n + i][...] = (refs[2 * i][...].astype(F32) + refs[2 * i + 1][...].astype(F32)).astype(BF16)

    return pl.pallas_call(
        body, name=name,
        grid_spec=pltpu.PrefetchScalarGridSpec(num_scalar_prefetch=1, grid=(NSH, 2), in_specs=in_specs, out_specs=out_specs),
        out_shape=out_shape, compiler_params=_params(("parallel", "parallel")),
    )(c_idx, *operands)


def _owner_sum(name, pairs, got, p_idx):
    n = len(pairs)
    in_specs, out_specs, out_shape, operands = [], [], [], []
    for own, g in zip(pairs, got):
        _, rows, cols = own.shape
        hb = rows // 2
        in_specs += [pl.BlockSpec((None, hb, cols), lambda i, pr: (pr[0], i, 0)),
                     pl.BlockSpec((3, hb, cols), lambda i, pr: (0, i, 0))]
        out_specs.append(pl.BlockSpec((hb, cols), lambda i, pr: (i, 0)))
        out_shape.append(jax.ShapeDtypeStruct((rows, cols), F32))
        operands += [own, g]

    def body(p_ref, *refs):
        for i in range(n):
            a_ref, b_ref = refs[2 * i], refs[2 * i + 1]
            refs[2 * n + i][...] = ((a_ref[...].astype(F32) + b_ref[0].astype(F32)) + b_ref[1].astype(F32)) + b_ref[2].astype(F32)

    return pl.pallas_call(
        body, name=name,
        grid_spec=pltpu.PrefetchScalarGridSpec(num_scalar_prefetch=1, grid=(2,), in_specs=in_specs, out_specs=out_specs),
        out_shape=out_shape, compiler_params=_params(("parallel",)),
    )(p_idx, *operands)


def _sum_small(slab):
    def body(in_ref, out_ref, all_ref, send, recv):
        x, y, c, _ = _place()
        me = 4 * x + 2 * y + c
        all_ref[me] = in_ref[...]
        cps = []
        for k in range(1, 8):
            peer = (x ^ (k >> 2), y ^ ((k >> 1) & 1), c ^ (k & 1))
            cps.append(pltpu.make_async_remote_copy(src_ref=in_ref, dst_ref=all_ref.at[me], send_sem=send.at[k - 1],
                                                    recv_sem=recv.at[k - 1], device_id=peer, device_id_type=MESH))
        for cp in cps:
            cp.start()
        for cp in cps:
            cp.wait()
        total = all_ref[0]
        for d in range(1, 8):
            total = total + all_ref[d]
        out_ref[...] = total

    return pl.pallas_call(
        body, name="sum_small",
        in_specs=[pl.BlockSpec(memory_space=pltpu.VMEM)], out_specs=pl.BlockSpec(memory_space=pltpu.VMEM),
        out_shape=jax.ShapeDtypeStruct(slab.shape, F32),
        scratch_shapes=[pltpu.VMEM((8,) + slab.shape, F32), pltpu.SemaphoreType.DMA((7,)), pltpu.SemaphoreType.DMA((7,))],
        compiler_params=pltpu.CompilerParams(has_side_effects=True),
    )(slab)


FFNS = ("pre0", "post0", "pre1", "post1")
W_SHAPES = dict({f + "_gu": (NSH, 2, FS, D) for f in FFNS}, **{f + "_d": (NSH, FS, D) for f in FFNS},
                ab_in=(NSH, D, 768), ab_out=(NSH, 256, D), conv=(NSH, 2, 8, LANE), c_in=(NSH, D, D), c_out=(NSH, 256, D))
CARRIED = dict(pre0_up=("pre0_d",), pre0_down=("ab_in", "ab_out", "conv"),
               sb_fwd=("post0_gu", "post0_d"), post0_up=("pre1_gu",), post0_down=("pre1_d",),
               pre1_up=("c_in", "c_out"), hgrn_fwd=("post1_gu", "post1_d"))


FFN_SLOTS = dict(pre0=(0, 2, 0), pre1=(1, 3, 1), post0=(4, 6, 2), post1=(5, 7, 3))
GRAD_SLOTS = dict(ab_out=("b", B_ABOUT, 256), c_in=("b", B_CIN, D), c_out=("b", B_COUT, 256), ab_in=("c", 0, D))
for _f, (_g, _u, _d) in FFN_SLOTS.items():
    GRAD_SLOTS.update({_f + "_g": ("a", _g * FS, FS), _f + "_u": ("a", _u * FS, FS), _f + "_d": ("b", _d * FS, FS)})
REDUCE_STAGES = dict(x=("post1_g", "post1_u", "post1_d", "c_out"),
                     y=("c_in", "pre1_g", "pre1_u", "pre1_d", "post0_g", "post0_u", "post0_d", "ab_out"),
                     z1=("ab_in",), z2=("pre0_g", "pre0_u", "pre0_d"))


def _local_step(x, target, weights, small, shards=None, place=None):
    t = x.shape[0]
    tm = min(MM_TILE, t)
    tw = min(WGRAD_TILE, t)
    nt = t // tm
    tok_si = _spec((tm, D), lambda s, i: (i, 0))
    w = dict(weights)
    reduced = {}

    def stage_slots(stage, buffers):
        return [(buffers[GRAD_SLOTS[n][0]],) + GRAD_SLOTS[n][1:] for n in REDUCE_STAGES[stage]]

    def swap_rider(stage, buffers):
        return _Swap(stage_slots(stage, buffers)) if place is not None else None

    def reduce_start(stage, buffers, swapped=None):
        if place is None:
            return [], None
        slots = stage_slots(stage, buffers)
        if swapped is None:
            swapped = _alone("grad_swap_" + stage, _Swap(slots))
        pairs = _pair_sum("grad_pair_sum_" + stage, slots, swapped, place[0])
        return pairs, _Send(pairs)

    def reduce_end(stage, pairs, landed):
        if place is None:
            return None
        mine = _owner_sum("grad_owner_sum_" + stage, pairs, landed, place[1])
        return _Share(mine)

    def shared(stage, landed):
        if place is not None:
            reduced.update(zip(REDUCE_STAGES[stage], landed))

    def carried(kernel_name):
        names = [n for n in CARRIED[kernel_name] if n not in w]
        return names, (_Gather([shards[n] for n in names]) if names else None)

    def land(names, arrays):
        for n, a in zip(names, arrays):
            w[n] = a.reshape(W_SHAPES[n])

    def ffn_forward(tag, h, hn, next_row):
        names, gather = carried(tag + "_up") if tag + "_up" in CARRIED else ([], None)
        s_up, s_gate, a, got = _ffn_up(tag + "_up", hn, w[tag + "_gu"], gather)
        land(names, got)
        names, gather = carried(tag + "_down") if tag + "_down" in CARRIED else ([], None)
        out = _ffn_down(tag + "_down", a, w[tag + "_d"], h, gather, (small, next_row) if next_row is not None else None)
        if gather is not None:
            out, got = out
            land(names, got)
        out, hn_next = out if next_row is not None else (out, None)
        return out, hn_next, (h, hn, s_up, s_gate, a)

    def out_proj(name, y, w_out, h, next_row):
        return _mm(name, [(y, _spec((tm, 256), lambda i, k: (i, k)), w_out, _spec((None, 256, D), lambda i, k: (k, 0, 0)))],
                   grid=(nt, NSH), o_shape=(t, D), o_dtype=F32, o_spec=_spec((tm, D), lambda i, k: (i, 0)),
                   dims=NN, kaxis=1, nk=NSH, acc_shape=(tm, D), res=(h, _spec((tm, D), lambda i, k: (i, 0))),
                   norm=(small, next_row))

    def out_proj_bwd(tag, dhb, y, w_out, blk, grad_b, make_rider=None):
        grad_b = _wgrad(tag + "_dwout", y, _spec((tw, 256), lambda s, k: (k, s)), dhb, _spec((tw, D), lambda s, k: (k, 0)),
                        256, D, t, tw, grad_b, blk)
        rider = make_rider(grad_b) if make_rider is not None else None
        dy = _mm(tag + "_dy", [(dhb, tok_si, w_out, _spec((None, 256, D), lambda s, i: (s, 0, 0)))],
                 grid=(NSH, nt), o_shape=(t, D), o_dtype=F32, o_spec=_spec((tm, 256), lambda s, i: (i, s)),
                 dims=NT, kaxis=1, nk=1, carry=rider)
        dy, landed = dy if rider is not None else (dy, None)
        return dy, grad_b, landed

    h0 = x
    h1, hn_ab, pre0 = ffn_forward("pre0", h0, _norm_fwd("pre0_norm", h0, small, R_PRE), R_MIX)
    proj_a = _mm("ab_proj_a", [(hn_ab, tok_si, w["ab_in"], _spec((None, D, 768), lambda s, i: (s, 0, 0)))],
                 grid=(2, nt), o_shape=(t, 1536), o_dtype=F32, o_spec=_spec((tm, 768), lambda s, i: (i, s)),
                 dims=NN, kaxis=1, nk=1)
    proj_b = _mm("ab_proj_b", [(hn_ab, tok_si, w["ab_in"], _spec((None, D, 768), lambda s, i: (s + 2, 0, 0)))],
                 grid=(2, nt), o_shape=(t, 1536), o_dtype=BF16, o_spec=_spec((tm, 768), lambda s, i: (i, s)),
                 dims=NN, kaxis=1, nk=1)
    y_a = _conv_fwd(proj_a, w["conv"])
    names, gather = carried("sb_fwd")
    y_b, ltot, nblk, got = _sb_fwd(proj_b, gather)
    land(names, got)
    y_ab = jnp.concatenate([y_a, y_b], axis=1)
    h2, hn = out_proj("ab_out", y_ab, w["ab_out"], h1, R_POST)
    h3, hn, post0 = ffn_forward("post0", h2, hn, R_PRE + 1)
    h4, hn_c, pre1 = ffn_forward("pre1", h3, hn, R_MIX + 1)
    proj_c = _mm("c_proj", [(hn_c, tok_si, w["c_in"], _spec((None, D, D), lambda s, i: (s, 0, 0)))],
                 grid=(NSH, nt), o_shape=(NSH, t, D), o_dtype=F32, o_spec=_spec((None, tm, D), lambda s, i: (s, i, 0)),
                 dims=NN, kaxis=1, nk=1)
    names, gather = carried("hgrn_fwd")
    o_c, y_c, sst, got = _hgrn_fwd(proj_c, small, gather)
    land(names, got)
    h5, hn = out_proj("c_out", y_c, w["c_out"], h4, R_POST + 1)
    h6, _, post1 = ffn_forward("post1", h5, hn, None)
    dh, dhb, d_fin, loss = _final_loss(h6, small, target)

    dh, dhb, grad_a, grad_b, dn_post1, _, _ = _ffn_backward("post1", dh, dhb, *post1, w["post1_gu"], w["post1_d"],
                                                            *FFN_SLOTS["post1"], small, R_POST + 1, 8 * FS, B_ROWS)
    dy_c, grad_b, swapped = out_proj_bwd("c", dhb, y_c, w["c_out"], B_COUT // 256, grad_b,
                                         lambda buf_b: swap_rider("x", dict(a=grad_a, b=buf_b)))
    pairs, rider = reduce_start("x", dict(a=grad_a, b=grad_b), swapped)
    dproj_c, d_clb, d_gam, landed = _hgrn_bwd(proj_c, small, o_c, sst, dy_c, rider)
    share = reduce_end("x", pairs, landed)
    grad_b = _wgrad("c_dwin", hn_c, _spec((tw, D), lambda s, k: (k, 0)), dproj_c, _spec((None, tw, D), lambda s, k: (s, k, 0)),
                    D, D, t, tw, grad_b, B_CIN // D, carry=share)
    if share is not None:
        grad_b, landed = grad_b
        shared("x", landed)
    dh, dhb, dn_mix1, _ = _dhn_norm("c_dhn", [(dproj_c, _spec((None, tm, D), lambda i, k: (k, i, 0)),
                                               w["c_in"], _spec((None, D, D), lambda i, k: (k, 0, 0)))],
                                    NT, h4, small, R_MIX + 1, dh)
    dh, dhb, grad_a, grad_b, dn_pre1, _, _ = _ffn_backward("pre1", dh, dhb, *pre1, w["pre1_gu"], w["pre1_d"],
                                                           *FFN_SLOTS["pre1"], small, R_PRE + 1, grad_a, grad_b)
    dh, dhb, grad_a, grad_b, dn_post0, _, _ = _ffn_backward("post0", dh, dhb, *post0, w["post0_gu"], w["post0_d"],
                                                            *FFN_SLOTS["post0"], small, R_POST, grad_a, grad_b)
    dy_ab, grad_b, _ = out_proj_bwd("ab", dhb, y_ab, w["ab_out"], B_ABOUT // 256, grad_b)
    dab, dac, dax, d_conv, swapped = _conv_bwd(proj_a, w["conv"], dy_ab, swap_rider("y", dict(a=grad_a, b=grad_b)))
    pairs, rider = reduce_start("y", dict(a=grad_a, b=grad_b), swapped)
    dq, dk, dv, landed = _sb_bwd(proj_b, dy_ab, ltot, nblk, rider)
    share = reduce_end("y", pairs, landed)
    dproj_ab = jnp.concatenate([dab, dac, dax, dq, dk, dv], axis=1)
    grad_c = _wgrad("ab_dwin", hn_ab, _spec((tw, D), lambda s, k: (k, 0)), dproj_ab, _spec((tw, 768), lambda s, k: (k, s)),
                    D, 768, t, tw, D, 0, carry=share)
    if share is not None:
        grad_c, landed = grad_c
        shared("y", landed)
    dh, dhb, dn_mix0, _ = _dhn_norm("ab_dhn", [(dproj_ab, _spec((tm, 768), lambda i, k: (i, k)),
                                                w["ab_in"], _spec((None, D, 768), lambda i, k: (k, 0, 0)))],
                                    NT, h1, small, R_MIX, dh)
    pairs, rider = reduce_start("z1", dict(c=grad_c))
    last = {}

    def own_gradients(buf_a, buf_b):
        last["z2"] = reduce_start("z2", dict(a=buf_a, b=buf_b))
        return last["z2"][1]

    dh, dhb, grad_a, grad_b, dn_pre0, landed, landed_own = _ffn_backward(
        "pre0", dh, dhb, *pre0, w["pre0_gu"], w["pre0_d"], *FFN_SLOTS["pre0"], small, R_PRE, grad_a, grad_b,
        rider, own_gradients if place is not None else None)
    if place is not None:
        mine = (_owner_sum("grad_owner_sum_z1", pairs, landed, place[1])
                + _owner_sum("grad_owner_sum_z2", last["z2"][0], landed_own, place[1]))
        reduced.update(zip(REDUCE_STAGES["z1"] + REDUCE_STAGES["z2"], _alone("grad_share_z", _Share(mine))))
    zero = jnp.zeros((1, D), F32)
    conv_rows = jnp.pad(jnp.transpose(d_conv[:, :3, :], (1, 0, 2)).reshape(3, 512), ((0, 0), (0, D - 512)))
    small_grad = jnp.concatenate([
        dn_pre0, dn_pre1, dn_mix0, dn_mix1, dn_post0, dn_post1, d_clb, d_fin,
        jnp.pad(d_gam, ((0, 0), (0, D - HD))), conv_rows,
        jnp.pad(loss, ((0, 0), (0, D - 1))), zero, zero], axis=0)
    return dh, grad_a, grad_b, grad_c, small_grad, reduced


def _small_slab(rows):
    parts = [jnp.pad(r.astype(F32), ((0, 0), (0, D - r.shape[1]))) for r in rows]
    slab = jnp.concatenate(parts, axis=0)
    return jnp.pad(slab, ((0, SMALL_ROWS - slab.shape[0]), (0, 0)))


def kernel(x, ffn_pre_norm, ffn_pre_w_gate, ffn_pre_w_up, ffn_pre_w_down, mix_norm, ffn_post_norm, ffn_post_w_gate, ffn_post_w_up, ffn_post_w_down, ab_w_in, ab_conv_w, ab_w_out, c_w_in, c_lower_bounds, c_out_norm, c_w_out, final_norm, loss_target, m_ffn_pre_norm, m_ffn_pre_w_gate, m_ffn_pre_w_up, m_ffn_pre_w_down, m_mix_norm, m_ffn_post_norm, m_ffn_post_w_gate, m_ffn_post_w_up, m_ffn_post_w_down, m_ab_w_in, m_ab_conv_w, m_ab_w_out, m_c_w_in, m_c_lower_bounds, m_c_out_norm, m_c_w_out, m_final_norm, v_ffn_pre_norm, v_ffn_pre_w_gate, v_ffn_pre_w_up, v_ffn_pre_w_down, v_mix_norm, v_ffn_post_norm, v_ffn_post_w_gate, v_ffn_post_w_up, v_ffn_post_w_down, v_ab_w_in, v_ab_conv_w, v_ab_w_out, v_c_w_in, v_c_lower_bounds, v_c_out_norm, v_c_w_out, v_final_norm):
    t = x.shape[1]
    xi, yi, ci = lax.axis_index("x"), lax.axis_index("y"), lax.axis_index("c")
    p_idx = (2 * xi + yi).astype(jnp.int32).reshape(1)
    c_idx = ci.astype(jnp.int32).reshape(1)

    def halves(m):
        return m.astype(BF16).reshape(2, m.shape[0] // 2, m.shape[1])

    transposed = ("ffn_pre_w_gate", "ffn_pre_w_up", "ffn_post_w_gate", "ffn_post_w_up")

    def flip(a):
        return jnp.swapaxes(a, 1, 2)

    shards = {}
    for name, (w_gate, w_up, w_down, layer) in dict(
            pre0=(ffn_pre_w_gate, ffn_pre_w_up, ffn_pre_w_down, 0), post0=(ffn_post_w_gate, ffn_post_w_up, ffn_post_w_down, 0),
            pre1=(ffn_pre_w_gate, ffn_pre_w_up, ffn_pre_w_down, 1), post1=(ffn_post_w_gate, ffn_post_w_up, ffn_post_w_down, 1)).items():
        shards[name + "_gu"] = jnp.stack([flip(w_gate)[layer], flip(w_up)[layer]]).astype(BF16)
        shards[name + "_d"] = halves(w_down[layer])
    conv_pad = jnp.pad(ab_conv_w[0], ((0, 5), (0, 0)))
    shards.update(ab_in=halves(ab_w_in[0]), ab_out=halves(ab_w_out[0]), c_in=halves(c_w_in[0]), c_out=halves(c_w_out[0]),
                  conv=jnp.stack([conv_pad, jnp.zeros_like(conv_pad)]))
    first = _alone("gather_weights", _Gather([shards["pre0_gu"]]))[0].reshape(W_SHAPES["pre0_gu"])

    small = _small_slab([ffn_pre_norm, mix_norm, ffn_post_norm, c_lower_bounds, final_norm.reshape(1, D), c_out_norm])
    small = small.reshape(SMALL_ROWS, 1, D)

    grad_x, _, _, _, small_grad, reduced = _local_step(x[0], loss_target[0], dict(pre0_gu=first), small, shards,
                                                        (c_idx, p_idx))
    whole = {n: g.reshape(2 * g.shape[1], g.shape[2]) for n, g in reduced.items()}
    small_sum = _sum_small(small_grad)

    my_conv = lax.dynamic_slice(small_sum[R_CONV:R_CONV + 3], (0, (2 * xi + yi) * 128), (3, 128))
    grads = {
        "ffn_pre_norm": small_sum[R_PRE:R_PRE + 2], "mix_norm": small_sum[R_MIX:R_MIX + 2],
        "ffn_post_norm": small_sum[R_POST:R_POST + 2], "c_lower_bounds": small_sum[R_CLB:R_CLB + 2],
        "c_out_norm": small_sum[R_GAM:R_GAM + 1, :HD], "final_norm": small_sum[R_FIN],
        "ab_conv_w": my_conv.reshape(1, 3, 128),
    }
    layers = dict(ab_w_in=[whole["ab_in"]], ab_w_out=[whole["ab_out"]], c_w_in=[whole["c_in"]], c_w_out=[whole["c_out"]])
    for kind, key in (("gate", "_g"), ("up", "_u"), ("down", "_d")):
        layers["ffn_pre_w_" + kind] = [whole["pre0" + key], whole["pre1" + key]]
        layers["ffn_post_w_" + kind] = [whole["post0" + key], whole["post1" + key]]
    weights = dict(ffn_pre_norm=ffn_pre_norm, ffn_pre_w_gate=ffn_pre_w_gate, ffn_pre_w_up=ffn_pre_w_up, ffn_pre_w_down=ffn_pre_w_down, mix_norm=mix_norm, ffn_post_norm=ffn_post_norm, ffn_post_w_gate=ffn_post_w_gate, ffn_post_w_up=ffn_post_w_up, ffn_post_w_down=ffn_post_w_down, ab_w_in=ab_w_in, ab_conv_w=ab_conv_w, ab_w_out=ab_w_out, c_w_in=c_w_in, c_lower_bounds=c_lower_bounds, c_out_norm=c_out_norm, c_w_out=c_w_out, final_norm=final_norm)
    m_in = dict(ffn_pre_norm=m_ffn_pre_norm, ffn_pre_w_gate=m_ffn_pre_w_gate, ffn_pre_w_up=m_ffn_pre_w_up, ffn_pre_w_down=m_ffn_pre_w_down, mix_norm=m_mix_norm, ffn_post_norm=m_ffn_post_norm, ffn_post_w_gate=m_ffn_post_w_gate, ffn_post_w_up=m_ffn_post_w_up, ffn_post_w_down=m_ffn_post_w_down, ab_w_in=m_ab_w_in, ab_conv_w=m_ab_conv_w, ab_w_out=m_ab_w_out, c_w_in=m_c_w_in, c_lower_bounds=m_c_lower_bounds, c_out_norm=m_c_out_norm, c_w_out=m_c_w_out, final_norm=m_final_norm)
    v_in = dict(ffn_pre_norm=v_ffn_pre_norm, ffn_pre_w_gate=v_ffn_pre_w_gate, ffn_pre_w_up=v_ffn_pre_w_up, ffn_pre_w_down=v_ffn_pre_w_down, mix_norm=v_mix_norm, ffn_post_norm=v_ffn_post_norm, ffn_post_w_gate=v_ffn_post_w_gate, ffn_post_w_up=v_ffn_post_w_up, ffn_post_w_down=v_ffn_post_w_down, ab_w_in=v_ab_w_in, ab_conv_w=v_ab_conv_w, ab_w_out=v_ab_w_out, c_w_in=v_c_w_in, c_lower_bounds=v_c_lower_bounds, c_out_norm=v_c_out_norm, c_w_out=v_c_w_out, final_norm=v_final_norm)
    names = list(weights)
    big = [n for n in names if weights[n].size >= 65536]
    tiny = [n for n in names if n not in big]

    delta, new_m, new_v = {}, {}, {}
    for n in big:
        turn = flip if n in transposed else (lambda a: a)
        shape = turn(weights[n]).shape
        two_d = (shape[0] * shape[1], shape[2])
        d, m2, v2, g2 = _adamw("adamw_" + n, turn(weights[n]).reshape(two_d), layers[n],
                               turn(m_in[n]).reshape(two_d), turn(v_in[n]).reshape(two_d))
        delta[n], new_m[n], new_v[n] = turn(d.reshape(shape)), turn(m2.reshape(shape)), turn(v2.reshape(shape))
        grads[n] = turn(g2.reshape(shape))

    def tiny_slab(src):
        return _small_slab([src[n].reshape(-1, src[n].shape[-1]) for n in tiny])

    offs, row = {}, 0
    for n in tiny:
        nrows = weights[n].size // weights[n].shape[-1]
        offs[n] = (row, nrows)
        row += nrows
    d, m2, v2, _ = _adamw("adamw_small", tiny_slab(weights), [tiny_slab(grads)], tiny_slab(m_in), tiny_slab(v_in))
    for n in tiny:
        r0, nr = offs[n]
        shape = weights[n].shape
        for dst, src in ((delta, d), (new_m, m2), (new_v, v2)):
            dst[n] = src[r0:r0 + nr, :shape[-1]].reshape(shape)

    loss = small_sum[R_LOSS, 0]
    return (loss, grad_x.reshape(1, t, D), *[grads[n] for n in names], *[delta[n] for n in names],
            *[new_m[n] for n in names], *[new_v[n] for n in names])
```

```python
import functools
import math

import jax
import jax.numpy as jnp
from jax import lax
from jax.experimental import pallas as pl
from jax.experimental.pallas import tpu as pltpu

F32 = jnp.float32
BF16 = jnp.bfloat16
MESH = pl.DeviceIdType.MESH
ANY = pl.BlockSpec(memory_space=pl.ANY)

D = 1024
FS = 704
NSH = 4
RMS_EPS = 1e-6
MACARON = 0.5
CHUNK = 64
HD = 128
HGRN_HPS = 8
SBQ = 128
SB_PAIRS_FWD = 4
SB_PAIRS_BWD = 2
SB_DEAD = -105.0
CONV_HALO = 8
LANE = 128
ROW_TILE = 512
MM_TILE = 1024
WGRAD_TILE = 4096
VMEM_LIMIT = 48 * 1024 * 1024
VMEM_LIMIT_BIG = 58 * 1024 * 1024

ADAM_LR, ADAM_B1, ADAM_B2, ADAM_EPS, ADAM_WD, ADAM_STEP = 0.001, 0.9, 0.999, 1e-08, 0.01, 10

NN = ((1,), (0,))
NT = ((1,), (1,))
TN = ((0,), (0,))

SMALL_ROWS = 16
R_PRE, R_MIX, R_POST, R_CLB, R_FIN, R_GAM, R_CONV, R_LOSS = 0, 2, 4, 6, 8, 9, 10, 13

B_DOWN = 0
B_ABOUT = 4 * FS
B_CIN = B_ABOUT + 256
B_COUT = B_CIN + 1024
B_ROWS = B_COUT + 256


def _dg(a, b, dims):
    return lax.dot_general(a, b, (dims, ((), ())), preferred_element_type=F32)


def _split(x):
    hi = x.astype(BF16)
    lo = (x - hi.astype(F32)).astype(BF16)
    return hi, lo


def _dot3(a, b, dims):
    ah, al = _split(a)
    bh, bl = _split(b)
    if dims == TN:
        n = b.shape[1]
        both = _dg(ah, jnp.concatenate([bh, bl], axis=1), dims)
        return both[:, :n] + both[:, n:] + _dg(al, bh, dims)
    m = a.shape[0]
    both = _dg(jnp.concatenate([ah, al], axis=0), bh, dims)
    return both[:m] + both[m:] + _dg(ah, bl, dims)


def _sigmoid(x):
    return 1.0 / (1.0 + jnp.exp(-x))


def _params(sem):
    return pltpu.CompilerParams(dimension_semantics=sem, vmem_limit_bytes=VMEM_LIMIT)


def _spec(shape, imap):
    return pl.BlockSpec(shape, imap)


def _accumulate(acc_ref, pairs, dims):
    part = None
    for a_ref, b_ref in pairs:
        d = _dg(a_ref[...], b_ref[...], dims)
        part = d if part is None else part + d
    acc_ref[...] += part


def _mm(name, pairs, *, grid, o_shape, o_dtype, o_spec, dims, kaxis, nk, acc_shape=None, res=None, scale=None,
        into=None, carry=None, norm=None):
    npairs = len(pairs)
    operands, specs = [], []
    for a, a_spec, b, b_spec in pairs:
        operands += [a, b]
        specs += [a_spec, b_spec]
    if res is not None:
        operands.append(res[0])
        specs.append(res[1])
    aliases = {}
    if into is not None:
        aliases = {len(operands): 0}
        operands.append(into)
        specs.append(ANY)
    if norm is not None:
        operands.append(norm[0])
        specs.append(_spec((None, 1, D), lambda *_: (norm[1], 0, 0)))
    n_own = len(operands)
    n_out = 2 if norm is not None else 1
    nc = carry.n if carry is not None else 0
    if nc:
        operands += carry.groups
        specs += carry.in_specs

    def body(*refs):
        o_ref = refs[n_own + nc]
        riders = ((refs[n_own:n_own + nc], refs[n_own + nc + n_out:n_own + 2 * nc + n_out], refs[-2], refs[-1])
                  if nc else None)
        if nc:
            carry.ride(grid, riders, "start")

        def finish(val):
            if scale is not None:
                val = val * scale
            if res is not None:
                val = val + refs[2 * npairs][...]
            o_ref[...] = val.astype(o_dtype)
            if norm is not None:
                r = lax.rsqrt(jnp.mean(val * val, axis=-1, keepdims=True) + RMS_EPS)
                refs[n_own + nc + 1][...] = (val * r * refs[n_own - 1][...]).astype(BF16)

        if nk == 1:
            part = None
            for n in range(npairs):
                d = _dg(refs[2 * n][...], refs[2 * n + 1][...], dims)
                part = d if part is None else part + d
            finish(part)
        else:
            acc_ref = refs[n_own + 2 * nc + n_out]
            k = pl.program_id(kaxis)

            @pl.when(k == 0)
            def _():
                acc_ref[...] = jnp.zeros_like(acc_ref)

            _accumulate(acc_ref, [(refs[2 * n], refs[2 * n + 1]) for n in range(npairs)], dims)

            @pl.when(k == nk - 1)
            def _():
                finish(acc_ref[...])
        if nc:
            carry.ride(grid, riders, "finish")

    sem = tuple("arbitrary" if (nc or (ax == kaxis and nk > 1)) else "parallel" for ax in range(len(grid)))
    outs = pl.pallas_call(
        body, name=name, grid=grid, in_specs=specs,
        out_specs=[o_spec] * n_out + (carry.out_specs if nc else []),
        out_shape=[jax.ShapeDtypeStruct(o_shape, o_dtype)] + ([jax.ShapeDtypeStruct(o_shape, BF16)] if norm is not None else [])
        + (carry.out_shape if nc else []),
        scratch_shapes=([pltpu.VMEM(acc_shape, F32)] if nk > 1 else []) + (carry.scratch if nc else []),
        input_output_aliases=aliases,
        compiler_params=_params(sem),
    )(*operands)
    main = (outs[0], outs[1]) if norm is not None else outs[0]
    return (main, carry.place(outs[n_out:])) if nc else main


def _norm_fwd(name, h, gain_slab, row):
    t = h.shape[0]
    tm = min(ROW_TILE, t)

    def body(h_ref, g_ref, o_ref):
        x = h_ref[...]
        r = lax.rsqrt(jnp.mean(x * x, axis=-1, keepdims=True) + RMS_EPS)
        o_ref[...] = (x * r * g_ref[...]).astype(BF16)

    return pl.pallas_call(
        body, name=name, grid=(t // tm,),
        in_specs=[_spec((tm, D), lambda i: (i, 0)), _spec((None, 1, D), lambda i: (row, 0, 0))],
        out_specs=_spec((tm, D), lambda i: (i, 0)),
        out_shape=jax.ShapeDtypeStruct((t, D), BF16),
        compiler_params=_params(("parallel",)),
    )(h, gain_slab)


def _dhn_norm(name, pairs, dims, h, gain_slab, row, dres, rider=None):
    t = h.shape[0]
    tm = min(MM_TILE, t)
    nt = t // tm
    grid = (nt, NSH)
    npairs = len(pairs)
    nr = rider.n if rider is not None else 0
    operands, specs = [], []
    for a, a_spec, b, b_spec in pairs:
        operands += [a, b]
        specs += [a_spec, b_spec]
    row_spec = _spec((tm, D), lambda i, k: (i, 0))
    operands += [h, gain_slab, dres]
    specs += [row_spec, _spec((None, 1, D), lambda i, k: (row, 0, 0)), row_spec]
    n_own = len(operands)

    def body(*refs):
        h_ref, g_ref, dres_ref = refs[2 * npairs:n_own]
        dh_ref, dhb_ref, dg_ref = refs[n_own + nr:n_own + nr + 3]
        acc_ref, gacc_ref = refs[n_own + 2 * nr + 3:n_own + 2 * nr + 5]
        riders = (refs[n_own:n_own + nr], refs[n_own + nr + 3:n_own + 2 * nr + 3], refs[-2], refs[-1]) if nr else None
        if nr:
            rider.ride(grid, riders, "start")
        i, k = pl.program_id(0), pl.program_id(1)

        @pl.when(k == 0)
        def _():
            acc_ref[...] = jnp.zeros_like(acc_ref)

        _accumulate(acc_ref, [(refs[2 * n], refs[2 * n + 1]) for n in range(npairs)], dims)

        @pl.when(k == NSH - 1)
        def _():
            x = h_ref[...]
            r = lax.rsqrt(jnp.mean(x * x, axis=-1, keepdims=True) + RMS_EPS)
            xh = x * r
            dy = acc_ref[...]
            gdy = dy * g_ref[...]
            dh = dres_ref[...] + (gdy - xh * jnp.mean(gdy * xh, axis=-1, keepdims=True)) * r
            dh_ref[...] = dh
            dhb_ref[...] = dh.astype(BF16)
            gpart = jnp.sum((dy * xh).reshape(tm // 8, 8, D), axis=0)

            @pl.when(i == 0)
            def _():
                gacc_ref[...] = gpart

            @pl.when(i > 0)
            def _():
                gacc_ref[...] += gpart

            @pl.when(i == nt - 1)
            def _():
                dg_ref[...] = jnp.sum(gacc_ref[...], axis=0, keepdims=True)

        if nr:
            rider.ride(grid, riders, "finish")

    outs = pl.pallas_call(
        body, name=name, grid=grid, in_specs=specs + (rider.in_specs if nr else []),
        out_specs=[row_spec, row_spec, _spec((1, D), lambda i, k: (0, 0))] + (rider.out_specs if nr else []),
        out_shape=[jax.ShapeDtypeStruct((t, D), F32), jax.ShapeDtypeStruct((t, D), BF16),
                   jax.ShapeDtypeStruct((1, D), F32)] + (rider.out_shape if nr else []),
        scratch_shapes=[pltpu.VMEM((tm, D), F32), pltpu.VMEM((8, D), F32)] + (rider.scratch if nr else []),
        compiler_params=pltpu.CompilerParams(dimension_semantics=("arbitrary", "arbitrary"),
                                             vmem_limit_bytes=VMEM_LIMIT_BIG),
    )(*operands, *(rider.groups if nr else []))
    return outs[0], outs[1], outs[2], (rider.place(outs[3:]) if nr else [])


def _final_loss(h, gain_slab, target):
    t = h.shape[0]
    tm = min(ROW_TILE, t)
    nt = t // tm

    def body(h_ref, g_ref, t_ref, dh_ref, dhb_ref, dg_ref, loss_ref, acc_ref, lacc_ref):
        i = pl.program_id(0)
        x = h_ref[...]
        g = g_ref[...]
        r = lax.rsqrt(jnp.mean(x * x, axis=-1, keepdims=True) + RMS_EPS)
        xh = x * r
        err = xh * g - t_ref[...]
        dy = err * (1.0 / D)
        gdy = dy * g
        dh = (gdy - xh * jnp.mean(gdy * xh, axis=-1, keepdims=True)) * r
        dh_ref[...] = dh
        dhb_ref[...] = dh.astype(BF16)
        part = jnp.sum((dy * xh).reshape(tm // 8, 8, D), axis=0)
        lpart = jnp.sum((err * err).reshape(tm // 8, 8, D), axis=0)

        @pl.when(i == 0)
        def _():
            acc_ref[...] = part
            lacc_ref[...] = lpart

        @pl.when(i > 0)
        def _():
            acc_ref[...] += part
            lacc_ref[...] += lpart

        @pl.when(i == nt - 1)
        def _():
            dg_ref[...] = jnp.sum(acc_ref[...], axis=0, keepdims=True)
            rows = jnp.sum(lacc_ref[...], axis=0, keepdims=True)
            loss_ref[...] = jnp.sum(rows, axis=1, keepdims=True) * (0.5 / D)

    row_spec = _spec((tm, D), lambda i: (i, 0))
    return pl.pallas_call(
        body, name="final_loss", grid=(nt,),
        in_specs=[row_spec, _spec((None, 1, D), lambda i: (R_FIN, 0, 0)), row_spec],
        out_specs=[row_spec, row_spec, _spec((1, D), lambda i: (0, 0)), _spec((1, 1), lambda i: (0, 0))],
        out_shape=[jax.ShapeDtypeStruct((t, D), F32), jax.ShapeDtypeStruct((t, D), BF16),
                   jax.ShapeDtypeStruct((1, D), F32), jax.ShapeDtypeStruct((1, 1), F32)],
        scratch_shapes=[pltpu.VMEM((8, D), F32), pltpu.VMEM((8, D), F32)],
        compiler_params=_params(("arbitrary",)),
    )(h, gain_slab, target)


def _ffn_up(name, hn, wgu, carry=None):
    t = hn.shape[0]
    tm = min(MM_TILE, t)
    grid = (NSH, t // tm)
    nc = carry.n if carry is not None else 0

    def body(*refs):
        x_ref, wg_ref, wu_ref = refs[:3]
        s_up_ref, s_gate_ref, a_ref = refs[3 + nc:6 + nc]
        riders = (refs[3:3 + nc], refs[6 + nc:6 + 2 * nc], refs[-2], refs[-1]) if nc else None
        if nc:
            carry.ride(grid, riders, "start")
        x = x_ref[...]
        g = _dg(x, wg_ref[...], NT)
        u = _dg(x, wu_ref[...], NT)
        sg = _sigmoid(g)
        silu = g * sg
        s_up_ref[...] = (MACARON * silu).astype(BF16)
        s_gate_ref[...] = (MACARON * u * (sg * (1.0 + g * (1.0 - sg)))).astype(BF16)
        a_ref[...] = (silu * u).astype(BF16)
        if nc:
            carry.ride(grid, riders, "finish")

    act = _spec((None, tm, FS), lambda s, i: (s, i, 0))
    shape = jax.ShapeDtypeStruct((NSH, t, FS), BF16)
    outs = pl.pallas_call(
        body, name=name, grid=grid,
        in_specs=[_spec((tm, D), lambda s, i: (i, 0)),
                  _spec((None, None, FS, D), lambda s, i: (s, 0, 0, 0)),
                  _spec((None, None, FS, D), lambda s, i: (s, 1, 0, 0))] + (carry.in_specs if nc else []),
        out_specs=[act, act, act] + (carry.out_specs if nc else []),
        out_shape=[shape, shape, shape] + (carry.out_shape if nc else []),
        scratch_shapes=carry.scratch if nc else [],
        compiler_params=_params(("arbitrary", "arbitrary") if nc else ("parallel", "parallel")),
    )(hn, wgu, wgu, *(carry.groups if nc else []))
    return (outs[0], outs[1], outs[2], carry.place(outs[3:]) if nc else [])


def _ffn_down(name, a, wd, h, carry=None, norm=None):
    t = h.shape[0]
    tm = min(MM_TILE, t)
    return _mm(name, [(a, _spec((None, tm, FS), lambda i, k: (k, i, 0)),
                       wd, _spec((None, FS, D), lambda i, k: (k, 0, 0)))],
               grid=(t // tm, NSH), o_shape=(t, D), o_dtype=F32, o_spec=_spec((tm, D), lambda i, k: (i, 0)),
               dims=NN, kaxis=1, nk=NSH, acc_shape=(tm, D), res=(h, _spec((tm, D), lambda i, k: (i, 0))),
               scale=MACARON, carry=carry, norm=norm)


def _ffn_bwd_up(name, dhb, wd, s_up, s_gate, rider=None):
    t = dhb.shape[0]
    tm = min(MM_TILE, t)
    grid = (NSH, t // tm)
    nr = rider.n if rider is not None else 0

    def body(*refs):
        dh_ref, wd_ref, s_up_ref, s_gate_ref = refs[:4]
        dg_ref, du_ref = refs[4 + nr:6 + nr]
        riders = (refs[4:4 + nr], refs[6 + nr:6 + 2 * nr], refs[-2], refs[-1]) if nr else None
        if nr:
            rider.ride(grid, riders, "start")
        da = _dg(dh_ref[...], wd_ref[...], NT).astype(BF16)
        du_ref[...] = da * s_up_ref[...]
        dg_ref[...] = da * s_gate_ref[...]
        if nr:
            rider.ride(grid, riders, "finish")

    act = _spec((None, tm, FS), lambda s, i: (s, i, 0))
    shape = jax.ShapeDtypeStruct((NSH, t, FS), BF16)
    outs = pl.pallas_call(
        body, name=name, grid=grid,
        in_specs=[_spec((tm, D), lambda s, i: (i, 0)), _spec((None, FS, D), lambda s, i: (s, 0, 0)), act, act]
        + (rider.in_specs if nr else []),
        out_specs=[act, act] + (rider.out_specs if nr else []),
        out_shape=[shape, shape] + (rider.out_shape if nr else []),
        scratch_shapes=rider.scratch if nr else [],
        compiler_params=_params(("arbitrary", "arbitrary") if nr else ("parallel", "parallel")),
    )(dhb, wd, s_up, s_gate, *(rider.groups if nr else []))
    return outs[0], outs[1], (rider.place(outs[2:]) if nr else [])


def _wgrad(name, a, a_spec, b, b_spec, out_rows, out_cols, t, tt, group, slot, scale=None, carry=None):
    first = isinstance(group, int)
    rows = group if first else group.shape[1]
    return _mm(name, [(a, a_spec, b, b_spec)], grid=(NSH, t // tt),
               o_shape=(NSH, rows, out_cols), o_dtype=BF16,
               o_spec=_spec((None, out_rows, out_cols), lambda s, k: (s, slot, 0)),
               dims=TN, kaxis=1, nk=t // tt, acc_shape=(out_rows, out_cols), scale=scale,
               into=None if first else group, carry=carry)


def _ffn_backward(tag, dh, dhb, h_in, hn, s_up, s_gate, a, wgu, wd, gate_idx, up_idx, down_idx, small, norm_row,
                  grad_a, grad_b, rider_up=None, rider_dhn=None):
    t = dh.shape[0]
    tm = min(MM_TILE, t)
    tt = min(WGRAD_TILE, t)
    tok = _spec((tt, D), lambda s, k: (k, 0))
    hid = _spec((None, tt, FS), lambda s, k: (s, k, 0))
    grad_b = _wgrad(tag + "_dwd", a, hid, dhb, tok, FS, D, t, tt, grad_b, down_idx, scale=MACARON)
    dg, du, landed_up = _ffn_bwd_up(tag + "_bwd_up", dhb, wd, s_up, s_gate, rider_up)
    grad_a = _wgrad(tag + "_dwg", dg, hid, hn, tok, FS, D, t, tt, grad_a, gate_idx)
    grad_a = _wgrad(tag + "_dwu", du, hid, hn, tok, FS, D, t, tt, grad_a, up_idx)
    rider = rider_dhn(grad_a, grad_b) if rider_dhn is not None else None
    act = _spec((None, tm, FS), lambda i, k: (k, i, 0))
    dh_in, dhb_in, d_gain, landed_dhn = _dhn_norm(
        tag + "_dhn", [(dg, act, wgu, _spec((None, None, FS, D), lambda i, k: (k, 0, 0, 0))),
                       (du, act, wgu, _spec((None, None, FS, D), lambda i, k: (k, 1, 0, 0)))],
        NN, h_in, small, norm_row, dh, rider)
    return dh_in, dhb_in, grad_a, grad_b, d_gain, landed_up, landed_dhn


def _conv_fwd(proj_a, conv_w):
    t = proj_a.shape[0]
    tm = min(ROW_TILE, t)
    hb = tm // CONV_HALO

    def body(ab_ref, ac_ref, ax_ref, acp_ref, axp_ref, w_ref, y_ref):
        i = pl.program_id(1)
        u = ac_ref[...] * ax_ref[...]
        up = jnp.where(i > 0, acp_ref[...] * axp_ref[...], 0.0)
        ext = jnp.concatenate([up, u], axis=0)
        u1 = pltpu.roll(ext, 1, 0)[CONV_HALO:]
        u2 = pltpu.roll(ext, 2, 0)[CONV_HALO:]
        w = w_ref[...]
        conv = w[0:1] * u2 + w[1:2] * u1 + w[2:3] * u
        y_ref[...] = (ab_ref[...] * conv).astype(BF16)

    def cur(off):
        return _spec((tm, LANE), lambda j, i: (i, off + j))

    def prev(off):
        return _spec((CONV_HALO, LANE), lambda j, i: (jnp.maximum(i * hb - 1, 0), off + j))

    return pl.pallas_call(
        body, name="conv_fwd", grid=(4, t // tm),
        in_specs=[cur(0), cur(4), cur(8), prev(4), prev(8),
                  _spec((None, None, 8, LANE), lambda j, i: (j, 0, 0, 0))],
        out_specs=_spec((tm, LANE), lambda j, i: (i, j)),
        out_shape=jax.ShapeDtypeStruct((t, 512), BF16),
        compiler_params=_params(("parallel", "parallel")),
    )(proj_a, proj_a, proj_a, proj_a, proj_a, conv_w)


def _conv_bwd(proj_a, conv_w, dy, rider=None):
    t = proj_a.shape[0]
    tm = min(ROW_TILE, t)
    hb = tm // CONV_HALO
    nt = t // tm
    grid = (4, nt)
    nr = rider.n if rider is not None else 0

    def body(*refs):
        ab_ref, ac_ref, ax_ref, dy_ref, acp_ref, axp_ref, abn_ref, dyn_ref, w_ref = refs[:9]
        dab_ref, dac_ref, dax_ref, dw_ref = refs[9 + nr:13 + nr]
        acc_ref = refs[13 + 2 * nr]
        riders = (refs[9:9 + nr], refs[13 + nr:13 + 2 * nr], refs[-2], refs[-1]) if nr else None
        if nr:
            rider.ride(grid, riders, "start")
        i = pl.program_id(1)
        ab, ac, ax = ab_ref[...], ac_ref[...], ax_ref[...]
        u = ac * ax
        up = jnp.where(i > 0, acp_ref[...] * axp_ref[...], 0.0)
        ext = jnp.concatenate([up, u], axis=0)
        u1 = pltpu.roll(ext, 1, 0)[CONV_HALO:]
        u2 = pltpu.roll(ext, 2, 0)[CONV_HALO:]
        w = w_ref[...]
        conv = w[0:1] * u2 + w[1:2] * u1 + w[2:3] * u
        dy_v = dy_ref[...]
        dab_ref[...] = (dy_v * conv).astype(BF16)
        dc = dy_v * ab
        dcn = jnp.where(i < nt - 1, dyn_ref[...] * abn_ref[...], 0.0)
        extn = jnp.concatenate([dc, dcn], axis=0)
        n = tm + CONV_HALO
        dc1 = pltpu.roll(extn, n - 1, 0)[:tm]
        dc2 = pltpu.roll(extn, n - 2, 0)[:tm]
        du = w[2:3] * dc + w[1:2] * dc1 + w[0:1] * dc2
        dac_ref[...] = (du * ax).astype(BF16)
        dax_ref[...] = (du * ac).astype(BF16)
        rid = lax.broadcasted_iota(jnp.int32, (8, LANE), 0)
        part = jnp.where(rid == 0, jnp.sum(dc * u2, axis=0, keepdims=True),
                         jnp.where(rid == 1, jnp.sum(dc * u1, axis=0, keepdims=True),
                                   jnp.where(rid == 2, jnp.sum(dc * u, axis=0, keepdims=True), 0.0)))

        @pl.when(i == 0)
        def _():
            acc_ref[...] = part

        @pl.when(i > 0)
        def _():
            acc_ref[...] += part

        @pl.when(i == nt - 1)
        def _():
            dw_ref[...] = acc_ref[...]

        if nr:
            rider.ride(grid, riders, "finish")

    def cur(off):
        return _spec((tm, LANE), lambda j, i: (i, off + j))

    def prev(off):
        return _spec((CONV_HALO, LANE), lambda j, i: (jnp.maximum(i * hb - 1, 0), off + j))

    def nxt(off):
        return _spec((CONV_HALO, LANE), lambda j, i: (jnp.minimum((i + 1) * hb, nt * hb - 1), off + j))

    outs = pl.pallas_call(
        body, name="conv_bwd", grid=grid,
        in_specs=[cur(0), cur(4), cur(8), cur(0), prev(4), prev(8), nxt(0), nxt(0),
                  _spec((None, None, 8, LANE), lambda j, i: (j, 0, 0, 0))] + (rider.in_specs if nr else []),
        out_specs=[_spec((tm, LANE), lambda j, i: (i, j)), _spec((tm, LANE), lambda j, i: (i, j)),
                   _spec((tm, LANE), lambda j, i: (i, j)), _spec((None, 8, LANE), lambda j, i: (j, 0, 0))]
        + (rider.out_specs if nr else []),
        out_shape=[jax.ShapeDtypeStruct((t, 512), BF16), jax.ShapeDtypeStruct((t, 512), BF16),
                   jax.ShapeDtypeStruct((t, 512), BF16), jax.ShapeDtypeStruct((4, 8, LANE), F32)]
        + (rider.out_shape if nr else []),
        scratch_shapes=[pltpu.VMEM((8, LANE), F32)] + (rider.scratch if nr else []),
        compiler_params=_params(("arbitrary", "arbitrary") if nr else ("parallel", "arbitrary")),
    )(proj_a, proj_a, proj_a, dy, proj_a, proj_a, proj_a, dy, conv_w, *(rider.groups if nr else []))
    return outs[0], outs[1], outs[2], outs[3], (rider.place(outs[4:]) if nr else [])


def _log_sigmoid(z):
    return jnp.minimum(z, 0.0) - jnp.log(1.0 + jnp.exp(-jnp.abs(z)))


def _sb_masks():
    row = lax.broadcasted_iota(jnp.int32, (SBQ, SBQ), 0)
    col = lax.broadcasted_iota(jnp.int32, (SBQ, SBQ), 1)
    return row, col


def _ones_where(mask):
    return jnp.where(mask, 1.0, 0.0).astype(BF16)


def _head_mask(head):
    lane = lax.broadcasted_iota(jnp.int32, (1, LANE), 1)
    return lane >= 64 if head else lane < 64


def _sb_fwd(proj_b, carry=None):
    t = proj_b.shape[0]
    nq = t // SBQ
    scale = 1.0 / math.sqrt(64.0)

    npair = SB_PAIRS_FWD
    wide = npair * LANE
    ngrp = 4 // npair
    chains = [(p, head) for p in range(npair) for head in range(2)]

    grid = (ngrp, nq)
    nc = carry.n if carry is not None else 0

    def body(*refs):
        q_ref, k_ref, v_ref = refs[:3]
        y_ref, l_ref, n_ref = refs[3 + nc:6 + nc]
        riders = (refs[3:3 + nc], refs[6 + nc:6 + 2 * nc], refs[-2], refs[-1]) if nc else None
        if nc:
            carry.ride(grid, riders, "start")
        grp = pl.program_id(0)
        qi = pl.program_id(1)
        row, col = _sb_masks()
        m_suffix = _ones_where(row > col)
        rows = len(chains) * SBQ
        strict = (lax.broadcasted_iota(jnp.int32, (rows, SBQ), 1)
                  < (lax.broadcasted_iota(jnp.int32, (rows, SBQ), 0) & (SBQ - 1)))
        q_pair = []
        for p in range(npair):
            q_all = q_ref[:, p * LANE:(p + 1) * LANE]
            q_pair.append(jnp.concatenate([jnp.where(_head_mask(head), q_all, jnp.zeros_like(q_all)) for head in range(2)],
                                          axis=0))

        def block(kb, state, diag):
            run, acc = state
            start = pl.multiple_of(kb * SBQ, SBQ)
            z = jnp.concatenate([_dg(q_pair[p], k_ref[pl.ds(start, SBQ), p * LANE:(p + 1) * LANE], NT)
                                 for p in range(npair)], axis=0) * scale
            lb = _log_sigmoid(z)
            lk = lb - z
            if diag:
                lk = jnp.where(strict, lk, 0.0)
            hi, lo = _split(lk)
            sums = _dg(jnp.concatenate([hi, lo], axis=0), m_suffix, NN)
            w = jnp.exp(lb + (run + sums[:rows] + sums[rows:]))
            if diag:
                w = jnp.where(strict, w, 0.0)
            wb = w.astype(BF16)
            acc = acc + jnp.concatenate(
                [_dg(wb[2 * p * SBQ:2 * (p + 1) * SBQ], v_ref[pl.ds(start, SBQ), p * LANE:(p + 1) * LANE], NN)
                 for p in range(npair)], axis=0)
            run = run + jnp.sum(hi.astype(F32) + lo.astype(F32), axis=1, keepdims=True)
            return run, acc

        state = block(qi, (jnp.zeros((rows, 1), F32), jnp.zeros((rows, LANE), F32)), True)

        def live(c):
            return jnp.logical_and(c[0] < qi, jnp.max(c[1][0]) > SB_DEAD)

        def step(c):
            return c[0] + 1, block(qi - 1 - c[0], c[1], False)

        count, (run, acc) = lax.while_loop(live, step, (jnp.int32(0), state))
        n_ref[grp * nq + qi] = count.astype(F32)
        hm = _head_mask(0)
        for p in range(npair):
            lo_rows, hi_rows = slice(2 * p * SBQ, (2 * p + 1) * SBQ), slice((2 * p + 1) * SBQ, (2 * p + 2) * SBQ)
            y_ref[:, p * LANE:(p + 1) * LANE] = jnp.where(hm, acc[lo_rows], acc[hi_rows]).astype(BF16)
            l_ref[p] = jnp.where(hm, run[lo_rows], run[hi_rows])
        if nc:
            carry.ride(grid, riders, "finish")

    outs = pl.pallas_call(
        body, name="sb_fwd", grid=grid,
        in_specs=[_spec((SBQ, wide), lambda g, i: (i, g)),
                  _spec((t, wide), lambda g, i: (0, ngrp + g)),
                  _spec((t, wide), lambda g, i: (0, 2 * ngrp + g))] + (carry.in_specs if nc else []),
        out_specs=[_spec((SBQ, wide), lambda g, i: (i, g)), _spec((npair, SBQ, LANE), lambda g, i: (g, i, 0)),
                   pl.BlockSpec(memory_space=pltpu.SMEM)] + (carry.out_specs if nc else []),
        out_shape=[jax.ShapeDtypeStruct((t, 512), BF16), jax.ShapeDtypeStruct((4, t, LANE), F32),
                   jax.ShapeDtypeStruct((ngrp * nq,), F32)] + (carry.out_shape if nc else []),
        scratch_shapes=carry.scratch if nc else [],
        compiler_params=_params(("arbitrary", "arbitrary")),
    )(proj_b, proj_b, proj_b, *(carry.groups if nc else []))
    return outs[0], outs[1], outs[2], (carry.place(outs[3:]) if nc else [])


def _sb_bwd(proj_b, dy, ltot, nblk, rider=None):
    t = proj_b.shape[0]
    nq = t // SBQ
    scale = 1.0 / math.sqrt(64.0)

    npair = SB_PAIRS_BWD
    wide = npair * LANE
    ngrp = 4 // npair
    chains = [(p, head) for p in range(npair) for head in range(2)]
    grid = (ngrp, nq)
    nr = rider.n if rider is not None else 0
    per_count = SB_PAIRS_FWD // SB_PAIRS_BWD

    def body(*refs):
        q_ref, k_ref, v_ref, dy_ref, l_ref, n_ref = refs[:6]
        dq_ref, dk_ref, dv_ref = refs[6 + nr:9 + nr]
        dk_acc, dv_acc = refs[9 + 2 * nr:11 + 2 * nr]
        riders = (refs[6:6 + nr], refs[9 + nr:9 + 2 * nr], refs[-2], refs[-1]) if nr else None
        if nr:
            rider.ride(grid, riders, "start")
        grp = pl.program_id(0)
        qi = pl.program_id(1)

        @pl.when(qi == 0)
        def _():
            dk_acc[...] = jnp.zeros_like(dk_acc)
            dv_acc[...] = jnp.zeros_like(dv_acc)

        row, col = _sb_masks()
        m_prefix = _ones_where(row <= col)
        m_before = _ones_where(row < col)
        rows = len(chains) * SBQ
        strict = (lax.broadcasted_iota(jnp.int32, (rows, SBQ), 1)
                  < (lax.broadcasted_iota(jnp.int32, (rows, SBQ), 0) & (SBQ - 1)))
        q_pair, do_pair, ltot = [], [], []
        for p in range(npair):
            pl_ = slice(p * LANE, (p + 1) * LANE)
            q_all = q_ref[:, pl_]
            do_all = dy_ref[:, pl_].astype(BF16)
            q_pair.append(jnp.concatenate([jnp.where(_head_mask(h), q_all, jnp.zeros_like(q_all)) for h in range(2)], axis=0))
            do_pair.append(jnp.concatenate([jnp.where(_head_mask(h), do_all, jnp.zeros_like(do_all)) for h in range(2)], axis=0))
            ltot += [l_ref[p][:, h * 64:h * 64 + 1] for h in range(2)]
        ltot = jnp.concatenate(ltot, axis=0)

        def pair_rows(a, p):
            return a[2 * p * SBQ:2 * (p + 1) * SBQ]

        def block(kb, state, diag):
            seen, dseen, dq = state
            start = pl.multiple_of(kb * SBQ, SBQ)
            kk = [k_ref[pl.ds(start, SBQ), p * LANE:(p + 1) * LANE] for p in range(npair)]
            vv = [v_ref[pl.ds(start, SBQ), p * LANE:(p + 1) * LANE] for p in range(npair)]
            z = jnp.concatenate([_dg(q_pair[p], kk[p], NT) for p in range(npair)], axis=0) * scale
            lb = _log_sigmoid(z)
            lk = lb - z
            if diag:
                lk = jnp.where(strict, lk, 0.0)
            hi, lo = _split(lk)
            sums = _dg(jnp.concatenate([hi, lo], axis=0), m_prefix, NN)
            w = jnp.exp(lb + ((ltot - seen) - (sums[:rows] + sums[rows:])))
            if diag:
                w = jnp.where(strict, w, 0.0)
            wb = w.astype(BF16)
            da = w * jnp.concatenate([_dg(do_pair[p], vv[p], NT) for p in range(npair)], axis=0)
            dah, dal = _split(da)
            dsums = _dg(jnp.concatenate([dah, dal], axis=0), m_before, NN)
            sig = jnp.exp(lb)
            dz = (da * (1.0 - sig) - (dseen + dsums[:rows] + dsums[rows:]) * sig) * scale
            if diag:
                dz = jnp.where(strict, dz, 0.0)
            dzb = dz.astype(BF16)
            for p in range(npair):
                pl_ = slice(p * LANE, (p + 1) * LANE)
                dv_acc[pl.ds(start, SBQ), pl_] += _dg(pair_rows(wb, p), do_pair[p], TN)
                dk_acc[pl.ds(start, SBQ), pl_] += _dg(pair_rows(dzb, p), q_pair[p], TN)
            dq = dq + jnp.concatenate([_dg(pair_rows(dzb, p), kk[p], NN) for p in range(npair)], axis=0)
            seen = seen + jnp.sum(hi.astype(F32) + lo.astype(F32), axis=1, keepdims=True)
            dseen = dseen + jnp.sum(da, axis=1, keepdims=True)
            return seen, dseen, dq

        zero = (jnp.zeros((rows, 1), F32), jnp.zeros((rows, 1), F32), jnp.zeros((rows, LANE), F32))
        first = qi - n_ref[(grp // per_count) * nq + qi].astype(jnp.int32)
        state = lax.fori_loop(first, qi, lambda kb, c: block(kb, c, False), zero)
        _, _, dq = block(qi, state, True)
        for p in range(npair):
            dq_ref[:, p * LANE:(p + 1) * LANE] = jnp.where(
                _head_mask(0), dq[2 * p * SBQ:(2 * p + 1) * SBQ], dq[(2 * p + 1) * SBQ:(2 * p + 2) * SBQ]).astype(BF16)

        @pl.when(qi == nq - 1)
        def _():
            dk_ref[...] = dk_acc[...].astype(BF16)
            dv_ref[...] = dv_acc[...].astype(BF16)

        if nr:
            rider.ride(grid, riders, "finish")

    full = jax.ShapeDtypeStruct((t, 512), BF16)
    outs = pl.pallas_call(
        body, name="sb_bwd", grid=grid,
        in_specs=[_spec((SBQ, wide), lambda g, i: (i, g)),
                  _spec((t, wide), lambda g, i: (0, ngrp + g)),
                  _spec((t, wide), lambda g, i: (0, 2 * ngrp + g)),
                  _spec((SBQ, wide), lambda g, i: (i, ngrp + g)),
                  _spec((npair, SBQ, LANE), lambda g, i: (g, i, 0)),
                  pl.BlockSpec(memory_space=pltpu.SMEM)] + (rider.in_specs if nr else []),
        out_specs=[_spec((SBQ, wide), lambda g, i: (i, g)),
                   _spec((t, wide), lambda g, i: (0, g)), _spec((t, wide), lambda g, i: (0, g))]
        + (rider.out_specs if nr else []),
        out_shape=[full, full, full] + (rider.out_shape if nr else []),
        scratch_shapes=[pltpu.VMEM((t, wide), F32), pltpu.VMEM((t, wide), F32)] + (rider.scratch if nr else []),
        compiler_params=_params(("arbitrary", "arbitrary")),
    )(proj_b, proj_b, proj_b, dy, ltot, nblk, *(rider.groups if nr else []))
    return outs[0], outs[1], outs[2], (rider.place(outs[3:]) if nr else [])


def _hgrn_gates(qr, fr, c0, c1):
    mx = jnp.maximum(c0, c1)
    e0, e1 = jnp.exp(c0 - mx), jnp.exp(c1 - mx)
    lb = e1 / (e0 + e1)
    sx = _sigmoid(fr)
    f = lb + (1.0 - lb) * sx
    k = (1.0 - lb) * (1.0 - sx)
    sq = _sigmoid(qr)
    return lb, sx, f, k, sq, qr * sq


def _chunk_sums(mask, x):
    n = x.shape[1]
    hi, lo = _split(x)
    both = _dg(_ones_where(mask), jnp.concatenate([hi, lo], axis=1), NN)
    return both[:, :n] + both[:, n:]


def _chunk_masks():
    row = lax.broadcasted_iota(jnp.int32, (CHUNK, CHUNK), 0)
    col = lax.broadcasted_iota(jnp.int32, (CHUNK, CHUNK), 1)
    return col <= row, col >= row


def _hgrn_fwd(proj_c, small, carry=None):
    t = proj_c.shape[1]
    nc = t // CHUNK
    nh = D // HD

    hps = HGRN_HPS
    wide = hps * HD
    grid = (nh // hps, nc)
    nr = carry.n if carry is not None else 0

    def body(*refs):
        p_ref, c0_ref, c1_ref, gam_ref = refs[:4]
        o_ref, y_ref, sst_ref = refs[4 + nr:7 + nr]
        st_ref = refs[7 + 2 * nr]
        riders = (refs[4:4 + nr], refs[7 + nr:7 + 2 * nr], refs[-2], refs[-1]) if nr else None
        if nr:
            carry.ride(grid, riders, "start")
        c = pl.program_id(1)

        @pl.when(c == 0)
        def _():
            st_ref[...] = jnp.zeros_like(st_ref)

        _, _, f_all, k_all, _, q_all = _hgrn_gates(p_ref[0], p_ref[1], c0_ref[...], c1_ref[...])
        tril, _ = _chunk_masks()
        b_all = _chunk_sums(tril, jnp.log(f_all))
        for j in range(hps):
            ln = slice(j * HD, (j + 1) * HD)
            st0 = st_ref[j]
            sst_ref[j] = st0
            v, g = p_ref[2, :, ln], p_ref[3, :, ln]
            q, k, b = q_all[:, ln], k_all[:, ln], b_all[:, ln]
            bm = b[CHUNK // 2 - 1:CHUNK // 2]
            bl = b[CHUNK - 1:CHUNK]
            qd = q * jnp.exp(b)
            qt = q * jnp.exp(b - bm)
            kt = k * jnp.exp(bm - b)
            kl = k * jnp.exp(bl - b)
            vb = v.astype(BF16)
            att = jnp.where(tril, _dot3(qt, kt, NT), 0.0)
            o = _dg(qd.astype(BF16), st0.astype(BF16), NT) + _dg(att.astype(BF16), vb, NN)
            st_ref[j] = st0 * jnp.exp(bl) + _dg(vb, kl.astype(BF16), TN)
            o_ref[:, ln] = o
            r = lax.rsqrt(jnp.mean(o * o, axis=-1, keepdims=True) + RMS_EPS)
            y_ref[:, ln] = (o * r * gam_ref[...] * (g * _sigmoid(g))).astype(BF16)
        if nr:
            carry.ride(grid, riders, "finish")

    outs = pl.pallas_call(
        body, name="hgrn_fwd", grid=grid,
        in_specs=[_spec((4, CHUNK, wide), lambda h, c: (0, c, h)),
                  _spec((None, 1, wide), lambda h, c: (R_CLB, 0, h)),
                  _spec((None, 1, wide), lambda h, c: (R_CLB + 1, 0, h)),
                  _spec((None, 1, HD), lambda h, c: (R_GAM, 0, 0))] + (carry.in_specs if nr else []),
        out_specs=[_spec((CHUNK, wide), lambda h, c: (c, h)), _spec((CHUNK, wide), lambda h, c: (c, h)),
                   _spec((None, hps, HD, HD), lambda h, c: (c, h, 0, 0))] + (carry.out_specs if nr else []),
        out_shape=[jax.ShapeDtypeStruct((t, D), F32), jax.ShapeDtypeStruct((t, D), BF16),
                   jax.ShapeDtypeStruct((nc, nh, HD, HD), F32)] + (carry.out_shape if nr else []),
        scratch_shapes=[pltpu.VMEM((hps, HD, HD), F32)] + (carry.scratch if nr else []),
        compiler_params=_params(("arbitrary", "arbitrary")),
    )(proj_c, small, small, small, *(carry.groups if nr else []))
    return outs[0], outs[1], outs[2], (carry.place(outs[3:]) if nr else [])


def _hgrn_bwd(proj_c, small, o, sst, dyc, rider=None):
    t = proj_c.shape[1]
    nc = t // CHUNK
    nh = D // HD

    hps = HGRN_HPS
    wide = hps * HD
    ng = nh // hps
    grid = (ng, nc)
    nr = rider.n if rider is not None else 0

    def body(*refs):
        p_ref, c0_ref, c1_ref, gam_ref, o_ref, sst_ref, dy_ref = refs[:7]
        dp_ref, dclb_ref, dgam_ref = refs[7 + nr:10 + nr]
        dst_ref, dlb_acc, dgam_acc = refs[10 + 2 * nr:13 + 2 * nr]
        riders = (refs[7:7 + nr], refs[10 + nr:10 + 2 * nr], refs[-2], refs[-1]) if nr else None
        if nr:
            rider.ride(grid, riders, "start")
        group = pl.program_id(0)
        step = pl.program_id(1)

        @pl.when(step == 0)
        def _():
            dst_ref[...] = jnp.zeros_like(dst_ref)
            dlb_acc[...] = jnp.zeros_like(dlb_acc)

        @pl.when((step == 0) & (group == 0))
        def _():
            dgam_acc[...] = jnp.zeros_like(dgam_acc)

        gam = gam_ref[...]
        qr_all = p_ref[0]
        lb_all, sx_all, f_all, k_all, sq_all, q_all = _hgrn_gates(qr_all, p_ref[1], c0_ref[...], c1_ref[...])
        tril, triu = _chunk_masks()
        b_all = _chunk_sums(tril, jnp.log(f_all))
        dq_parts, dk_parts, db_parts, dgam_parts = [], [], [], []
        for j in range(hps):
            ln = slice(j * HD, (j + 1) * HD)
            st0 = sst_ref[j]
            dst1 = dst_ref[j]
            v, g = p_ref[2, :, ln], p_ref[3, :, ln]
            q, k, b = q_all[:, ln], k_all[:, ln], b_all[:, ln]
            bm = b[CHUNK // 2 - 1:CHUNK // 2]
            bl = b[CHUNK - 1:CHUNK]
            eb = jnp.exp(b)
            e_qt = jnp.exp(b - bm)
            e_kt = jnp.exp(bm - b)
            e_kl = jnp.exp(bl - b)
            e_bl = jnp.exp(bl)
            qd, qt, kt, kl = q * eb, q * e_qt, k * e_kt, k * e_kl
            ov = o_ref[:, ln]
            r = lax.rsqrt(jnp.mean(ov * ov, axis=-1, keepdims=True) + RMS_EPS)
            oh = ov * r
            sg = _sigmoid(g)
            dy = dy_ref[:, ln]
            dp_ref[3, :, ln] = (dy * oh * gam * (sg * (1.0 + g * (1.0 - sg)))).astype(BF16)
            dyv = dy * (g * sg)
            gdy = dyv * gam
            do = (gdy - oh * jnp.mean(gdy * oh, axis=-1, keepdims=True)) * r
            dob, vb = do.astype(BF16), v.astype(BF16)
            st0b, dst1b = st0.astype(BF16), dst1.astype(BF16)
            st1 = st0 * e_bl + _dg(vb, kl.astype(BF16), TN)
            att = jnp.where(tril, _dot3(qt, kt, NT), 0.0)
            datt = jnp.where(tril, _dg(dob, vb, NT), 0.0)
            dv = _dg(att.astype(BF16), dob, TN) + _dg(kl.astype(BF16), dst1b, NT)
            dq = _dot3(datt, kt, NN) * e_qt + _dg(dob, st0b, NN) * eb
            dk = _dot3(datt, qt, TN) * e_kt + _dg(vb, dst1b, NN) * e_kl
            db = q * dq - k * dk
            last = lax.broadcasted_iota(jnp.int32, (CHUNK, 1), 0) == CHUNK - 1
            db = db + jnp.where(last, jnp.sum(dst1 * st1, axis=0, keepdims=True), 0.0)
            dst_ref[j] = dst1 * e_bl + _dg(dob, qd.astype(BF16), TN)
            dp_ref[2, :, ln] = dv.astype(BF16)
            dq_parts.append(dq)
            dk_parts.append(dk)
            db_parts.append(db)
            dgam_parts.append(jnp.sum(dyv * oh, axis=0, keepdims=True))

        dq_all, dk_all = jnp.concatenate(dq_parts, axis=1), jnp.concatenate(dk_parts, axis=1)
        dlf = _chunk_sums(triu, jnp.concatenate(db_parts, axis=1))
        dp_ref[0] = (dq_all * (sq_all * (1.0 + qr_all * (1.0 - sq_all)))).astype(BF16)
        tmp = dlf / f_all - dk_all
        dp_ref[1] = (tmp * (1.0 - lb_all) * sx_all * (1.0 - sx_all)).astype(BF16)
        dlb_acc[...] += jnp.sum((1.0 - sx_all) * tmp, axis=0, keepdims=True)
        dgam_acc[...] += functools.reduce(lambda a, b: a + b, dgam_parts)

        @pl.when(step == nc - 1)
        def _():
            d1 = dlb_acc[...] * lb_all * (1.0 - lb_all)
            dclb_ref[...] = jnp.where(lax.broadcasted_iota(jnp.int32, (2, wide), 0) == 0, -d1, d1)

        @pl.when((step == nc - 1) & (group == ng - 1))
        def _():
            dgam_ref[...] = dgam_acc[...]

        if nr:
            rider.ride(grid, riders, "finish")

    rev = lambda h, s: (nc - 1 - s, h)
    outs = pl.pallas_call(
        body, name="hgrn_bwd", grid=grid,
        in_specs=[_spec((4, CHUNK, wide), lambda h, s: (0, nc - 1 - s, h)),
                  _spec((None, 1, wide), lambda h, s: (R_CLB, 0, h)),
                  _spec((None, 1, wide), lambda h, s: (R_CLB + 1, 0, h)),
                  _spec((None, 1, HD), lambda h, s: (R_GAM, 0, 0)),
                  _spec((CHUNK, wide), rev),
                  _spec((None, hps, HD, HD), lambda h, s: (nc - 1 - s, h, 0, 0)),
                  _spec((CHUNK, wide), rev)] + (rider.in_specs if nr else []),
        out_specs=[_spec((4, CHUNK, wide), lambda h, s: (0, nc - 1 - s, h)),
                   _spec((2, wide), lambda h, s: (0, h)),
                   _spec((1, HD), lambda h, s: (0, 0))] + (rider.out_specs if nr else []),
        out_shape=[jax.ShapeDtypeStruct((4, t, D), BF16), jax.ShapeDtypeStruct((2, D), F32),
                   jax.ShapeDtypeStruct((1, HD), F32)] + (rider.out_shape if nr else []),
        scratch_shapes=[pltpu.VMEM((hps, HD, HD), F32), pltpu.VMEM((1, wide), F32), pltpu.VMEM((1, HD), F32)]
        + (rider.scratch if nr else []),
        compiler_params=_params(("arbitrary", "arbitrary")),
    )(proj_c, small, small, small, o, sst, dyc, *(rider.groups if nr else []))
    return outs[0], outs[1], outs[2], (rider.place(outs[3:]) if nr else [])


def _adamw(name, w, grads, m, v):
    nl = len(grads)
    rows, cols = grads[0].shape
    br = rows
    for cand in (512, 352, 256):
        if rows % cand == 0:
            br = cand
            break
    nb = rows // br
    c1 = 1.0 - ADAM_B1 ** ADAM_STEP
    c2 = 1.0 - ADAM_B2 ** ADAM_STEP

    def body(w_ref, m_ref, v_ref, *refs):
        g_refs, (d_ref, mo_ref, vo_ref, go_ref) = refs[:nl], refs[nl:]
        layer = pl.program_id(0)
        gv = g_refs[0][...]
        for k in range(1, nl):
            gv = jnp.where(layer == k, g_refs[k][...], gv)
        mn = ADAM_B1 * m_ref[...] + (1.0 - ADAM_B1) * gv
        vn = ADAM_B2 * v_ref[...] + (1.0 - ADAM_B2) * (gv * gv)
        mo_ref[...] = mn
        vo_ref[...] = vn
        go_ref[...] = gv
        d_ref[...] = -ADAM_LR * ((mn / c1) / (jnp.sqrt(vn / c2) + ADAM_EPS) + ADAM_WD * w_ref[...])

    blk = _spec((br, cols), lambda l, i: (l * nb + i, 0))
    g_specs = [_spec((br, cols), lambda l, i, k=k: (jnp.where(l == k, i, 0), 0)) for k in range(nl)]
    shape = jax.ShapeDtypeStruct((nl * rows, cols), F32)
    return pl.pallas_call(
        body, name=name, grid=(nl, nb), in_specs=[blk] * 3 + g_specs, out_specs=[blk] * 4, out_shape=[shape] * 4,
        compiler_params=_params(("arbitrary", "arbitrary")),
    )(w, m, v, *grads)


def _place():
    x, y, c = lax.axis_index("x"), lax.axis_index("y"), lax.axis_index("c")
    chips = [(1 - x, y), (x, 1 - y), (1 - x, 1 - y)]
    return x, y, c, chips


class _Rider:
    n = 0
    relay = None

    def ride(self, grid, refs, when):
        if not self.n:
            return
        ids = [pl.program_id(a) for a in range(len(grid))]
        first = functools.reduce(jnp.logical_and, [i == 0 for i in ids])
        last = functools.reduce(jnp.logical_and, [i == g - 1 for i, g in zip(ids, grid)])
        phases = [(first, self.start), (last, self.relay)] if when == "start" else [(last, self.finish)]
        for cond, phase in phases:
            if phase is not None:
                pl.when(cond)(functools.partial(phase, *refs))


class _Gather(_Rider):
    PER_GROUP = 7

    def __init__(self, groups):
        self.groups = list(groups)
        self.n = len(self.groups)
        self.in_specs = [ANY] * self.n
        self.out_specs = [ANY] * self.n
        self.out_shape = [jax.ShapeDtypeStruct((NSH,) + g.shape, g.dtype) for g in self.groups]
        sems = pltpu.SemaphoreType.DMA((self.PER_GROUP * self.n,))
        self.scratch = [sems, sems] if self.n else []

    def _copies(self, ins, outs, send, recv):
        x, y, c, chips = _place()
        sibling = (x, y, 1 - c)

        def half(gi, chip, hc):
            return outs[gi].at[2 * chip[0] + chip[1], hc]

        def copy(gi, k, src, dst, to):
            sem = self.PER_GROUP * gi + k
            return pltpu.make_async_remote_copy(src_ref=src, dst_ref=dst, send_sem=send.at[sem], recv_sem=recv.at[sem],
                                                device_id=to, device_id_type=MESH)

        pairs = [(gi, j, chip) for gi in range(self.n) for j, chip in enumerate(chips)]
        first = [copy(gi, j, ins[gi].at[c], half(gi, (x, y), c), (*chip, c)) for gi, j, chip in pairs]
        first += [copy(gi, 6, ins[gi], outs[gi].at[2 * x + y], sibling) for gi in range(self.n)]
        landed = [copy(gi, j, half(gi, chip, c), half(gi, chip, c), sibling) for gi, j, chip in pairs]
        relay = [copy(gi, 3 + j, half(gi, chip, c), half(gi, chip, c), sibling) for gi, j, chip in pairs]
        relayed = [copy(gi, 3 + j, half(gi, chip, 1 - c), half(gi, chip, 1 - c), sibling) for gi, j, chip in pairs]
        relayed += [copy(gi, 6, ins[gi], outs[gi].at[2 * x + y], sibling) for gi in range(self.n)]
        return first, landed, relay, relayed

    def start(self, ins, outs, send, recv):
        for cp in self._copies(ins, outs, send, recv)[0]:
            cp.start()

    def relay(self, ins, outs, send, recv):
        _, landed, relay, _ = self._copies(ins, outs, send, recv)
        for arrived, onward in zip(landed, relay):
            arrived.wait_recv()
            onward.start()

    def finish(self, ins, outs, send, recv):
        first, _, relay, relayed = self._copies(ins, outs, send, recv)
        for cp in relayed:
            cp.wait_recv()
        for cp in first + relay:
            cp.wait_send()

    def place(self, outs):
        return list(outs)


def _alone(name, rider):
    n = rider.n

    def body(*refs):
        parts = (refs[:n], refs[n:2 * n], refs[2 * n], refs[2 * n + 1])
        rider.start(*parts)
        if rider.relay is not None:
            rider.relay(*parts)
        rider.finish(*parts)

    outs = pl.pallas_call(
        body, name=name, in_specs=rider.in_specs, out_specs=rider.out_specs, out_shape=rider.out_shape,
        scratch_shapes=rider.scratch, compiler_params=pltpu.CompilerParams(has_side_effects=True),
    )(*rider.groups)
    return rider.place(outs)


class _Swap(_Rider):
    def __init__(self, slots):
        self.slots = list(slots)
        self.groups = [buf for buf, _, _ in self.slots]
        self.n = len(self.slots)
        self.in_specs = [ANY] * self.n
        self.out_specs = [ANY] * self.n
        self.out_shape = [jax.ShapeDtypeStruct((NSH, rows // 2, buf.shape[2]), buf.dtype) for buf, _, rows in self.slots]
        self.scratch = [pltpu.SemaphoreType.DMA((self.n,)), pltpu.SemaphoreType.DMA((self.n,))]

    def _copies(self, ins, outs, send, recv):
        x, y, c, _ = _place()
        cps = []
        for i, (_, row0, rows) in enumerate(self.slots):
            half = rows // 2
            src = ins[i].at[:, pl.ds(pl.multiple_of(row0 + (1 - c) * half, 16), half)]
            cps.append(pltpu.make_async_remote_copy(src_ref=src, dst_ref=outs[i], send_sem=send.at[i],
                                                    recv_sem=recv.at[i], device_id=(x, y, 1 - c), device_id_type=MESH))
        return cps

    def start(self, ins, outs, send, recv):
        for cp in self._copies(ins, outs, send, recv):
            cp.start()

    def finish(self, ins, outs, send, recv):
        for cp in self._copies(ins, outs, send, recv):
            cp.wait()

    def place(self, outs):
        return list(outs)


class _Share(_Rider):
    def __init__(self, arrays):
        self.groups = list(arrays)
        self.n = len(self.groups)
        self.in_specs = [ANY] * self.n
        self.out_specs = [ANY] * self.n
        self.out_shape = [jax.ShapeDtypeStruct((2,) + g.shape, g.dtype) for g in self.groups]
        self.scratch = [pltpu.SemaphoreType.DMA((self.n,)), pltpu.SemaphoreType.DMA((self.n,))]

    def _copies(self, ins, outs, send, recv, half):
        x, y, c, _ = _place()
        return [pltpu.make_async_remote_copy(src_ref=ins[gi], dst_ref=outs[gi].at[c if half == "mine" else 1 - c],
                                             send_sem=send.at[gi], recv_sem=recv.at[gi], device_id=(x, y, 1 - c),
                                             device_id_type=MESH) for gi in range(self.n)]

    def start(self, ins, outs, send, recv):
        for cp in self._copies(ins, outs, send, recv, "mine"):
            cp.start()

    def finish(self, ins, outs, send, recv):
        for cp in self._copies(ins, outs, send, recv, "theirs"):
            cp.wait_recv()
        for cp in self._copies(ins, outs, send, recv, "mine"):
            cp.wait_send()

    def place(self, outs):
        c = lax.axis_index("c")
        return [lax.dynamic_update_index_in_dim(o, g, c, 0) for o, g in zip(outs, self.groups)]


class _Send(_Rider):
    def __init__(self, arrays):
        self.groups = list(arrays)
        self.n = len(self.groups)
        self.in_specs = [ANY] * self.n
        self.out_specs = [ANY] * self.n
        self.out_shape = [jax.ShapeDtypeStruct((3,) + g.shape[1:], g.dtype) for g in self.groups]
        self.scratch = [pltpu.SemaphoreType.DMA((3 * self.n,)), pltpu.SemaphoreType.DMA((3 * self.n,))]

    def _copies(self, ins, outs, send, recv):
        x, y, c, chips = _place()
        return [pltpu.make_async_remote_copy(src_ref=ins[gi].at[2 * chip[0] + chip[1]], dst_ref=outs[gi].at[j],
                                             send_sem=send.at[3 * gi + j], recv_sem=recv.at[3 * gi + j],
                                             device_id=(*chip, c), device_id_type=MESH)
                for gi in range(self.n) for j, chip in enumerate(chips)]

    def start(self, ins, outs, send, recv):
        for cp in self._copies(ins, outs, send, recv):
            cp.start()

    def finish(self, ins, outs, send, recv):
        for cp in self._copies(ins, outs, send, recv):
            cp.wait()

    def place(self, outs):
        return list(outs)


def _pair_sum(name, slots, got, c_idx):
    n = len(slots)
    in_specs, out_specs, out_shape, operands = [], [], [], []
    for (buf, row0, rows), g in zip(slots, got):
        hb, cols = rows // 4, buf.shape[2]
        in_specs += [pl.BlockSpec((None, hb, cols), lambda q, i, cr, r=row0 // hb: (q, r + 2 * cr[0] + i, 0)),
                     pl.BlockSpec((None, hb, cols), lambda q, i, cr: (q, i, 0))]
        out_specs.append(pl.BlockSpec((None, hb, cols), lambda q, i, cr: (q, i, 0)))
        out_shape.append(jax.ShapeDtypeStruct(g.shape, BF16))
        operands += [buf, g]

    def body(c_ref, *refs):
        for i in range(n):
            refs[2 * n + i][...] = (refs[2 * i][...].astype(F32) + refs[2 * i + 1][...].astype(F32)).astype(BF16)

    return pl.pallas_call(
        body, name=name,
        grid_spec=pltpu.PrefetchScalarGridSpec(num_scalar_prefetch=1, grid=(NSH, 2), in_specs=in_specs, out_specs=out_specs),
        out_shape=out_shape, compiler_params=_params(("parallel", "parallel")),
    )(c_idx, *operands)


def _owner_sum(name, pairs, got, p_idx):
    n = len(pairs)
    in_specs, out_specs, out_shape, operands = [], [], [], []
    for own, g in zip(pairs, got):
        _, rows, cols = own.shape
        hb = rows // 2
        in_specs += [pl.BlockSpec((None, hb, cols), lambda i, pr: (pr[0], i, 0)),
                     pl.BlockSpec((3, hb, cols), lambda i, pr: (0, i, 0))]
        out_specs.append(pl.BlockSpec((hb, cols), lambda i, pr: (i, 0)))
        out_shape.append(jax.ShapeDtypeStruct((rows, cols), F32))
        operands += [own, g]

    def body(p_ref, *refs):
        for i in range(n):
            a_ref, b_ref = refs[2 * i], refs[2 * i + 1]
            refs[2 * n + i][...] = ((a_ref[...].astype(F32) + b_ref[0].astype(F32)) + b_ref[1].astype(F32)) + b_ref[2].astype(F32)

    return pl.pallas_call(
        body, name=name,
        grid_spec=pltpu.PrefetchScalarGridSpec(num_scalar_prefetch=1, grid=(2,), in_specs=in_specs, out_specs=out_specs),
        out_shape=out_shape, compiler_params=_params(("parallel",)),
    )(p_idx, *operands)


def _sum_small(slab):
    def body(in_ref, out_ref, all_ref, send, recv):
        x, y, c, _ = _place()
        me = 4 * x + 2 * y + c
        all_ref[me] = in_ref[...]
        cps = []
        for k in range(1, 8):
            peer = (x ^ (k >> 2), y ^ ((k >> 1) & 1), c ^ (k & 1))
            cps.append(pltpu.make_async_remote_copy(src_ref=in_ref, dst_ref=all_ref.at[me], send_sem=send.at[k - 1],
                                                    recv_sem=recv.at[k - 1], device_id=peer, device_id_type=MESH))
        for cp in cps:
            cp.start()
        for cp in cps:
            cp.wait()
        total = all_ref[0]
        for d in range(1, 8):
            total = total + all_ref[d]
        out_ref[...] = total

    return pl.pallas_call(
        body, name="sum_small",
        in_specs=[pl.BlockSpec(memory_space=pltpu.VMEM)], out_specs=pl.BlockSpec(memory_space=pltpu.VMEM),
        out_shape=jax.ShapeDtypeStruct(slab.shape, F32),
        scratch_shapes=[pltpu.VMEM((8,) + slab.shape, F32), pltpu.SemaphoreType.DMA((7,)), pltpu.SemaphoreType.DMA((7,))],
        compiler_params=pltpu.CompilerParams(has_side_effects=True),
    )(slab)


FFNS = ("pre0", "post0", "pre1", "post1")
W_SHAPES = dict({f + "_gu": (NSH, 2, FS, D) for f in FFNS}, **{f + "_d": (NSH, FS, D) for f in FFNS},
                ab_in=(NSH, D, 768), ab_out=(NSH, 256, D), conv=(NSH, 2, 8, LANE), c_in=(NSH, D, D), c_out=(NSH, 256, D))
CARRIED = dict(pre0_up=("pre0_d",), pre0_down=("ab_in", "ab_out", "conv"),
               sb_fwd=("post0_gu", "post0_d"), post0_up=("pre1_gu",), post0_down=("pre1_d",),
               pre1_up=("c_in", "c_out"), hgrn_fwd=("post1_gu", "post1_d"))


FFN_SLOTS = dict(pre0=(0, 2, 0), pre1=(1, 3, 1), post0=(4, 6, 2), post1=(5, 7, 3))
GRAD_SLOTS = dict(ab_out=("b", B_ABOUT, 256), c_in=("b", B_CIN, D), c_out=("b", B_COUT, 256), ab_in=("c", 0, D))
for _f, (_g, _u, _d) in FFN_SLOTS.items():
    GRAD_SLOTS.update({_f + "_g": ("a", _g * FS, FS), _f + "_u": ("a", _u * FS, FS), _f + "_d": ("b", _d * FS, FS)})
REDUCE_STAGES = dict(x=("post1_g", "post1_u", "post1_d", "c_out"),
                     y=("c_in", "pre1_g", "pre1_u", "pre1_d", "post0_g", "post0_u", "post0_d", "ab_out"),
                     z1=("ab_in",), z2=("pre0_g", "pre0_u", "pre0_d"))


def _local_step(x, target, weights, small, shards=None, place=None):
    t = x.shape[0]
    tm = min(MM_TILE, t)
    tw = min(WGRAD_TILE, t)
    nt = t // tm
    tok_si = _spec((tm, D), lambda s, i: (i, 0))
    w = dict(weights)
    reduced = {}

    def stage_slots(stage, buffers):
        return [(buffers[GRAD_SLOTS[n][0]],) + GRAD_SLOTS[n][1:] for n in REDUCE_STAGES[stage]]

    def swap_rider(stage, buffers):
        return _Swap(stage_slots(stage, buffers)) if place is not None else None

    def reduce_start(stage, buffers, swapped=None):
        if place is None:
            return [], None
        slots = stage_slots(stage, buffers)
        if swapped is None:
            swapped = _alone("grad_swap_" + stage, _Swap(slots))
        pairs = _pair_sum("grad_pair_sum_" + stage, slots, swapped, place[0])
        return pairs, _Send(pairs)

    def reduce_end(stage, pairs, landed):
        if place is None:
            return None
        mine = _owner_sum("grad_owner_sum_" + stage, pairs, landed, place[1])
        return _Share(mine)

    def shared(stage, landed):
        if place is not None:
            reduced.update(zip(REDUCE_STAGES[stage], landed))

    def carried(kernel_name):
        names = [n for n in CARRIED[kernel_name] if n not in w]
        return names, (_Gather([shards[n] for n in names]) if names else None)

    def land(names, arrays):
        for n, a in zip(names, arrays):
            w[n] = a.reshape(W_SHAPES[n])

    def ffn_forward(tag, h, hn, next_row):
        names, gather = carried(tag + "_up") if tag + "_up" in CARRIED else ([], None)
        s_up, s_gate, a, got = _ffn_up(tag + "_up", hn, w[tag + "_gu"], gather)
        land(names, got)
        names, gather = carried(tag + "_down") if tag + "_down" in CARRIED else ([], None)
        out = _ffn_down(tag + "_down", a, w[tag + "_d"], h, gather, (small, next_row) if next_row is not None else None)
        if gather is not None:
            out, got = out
            land(names, got)
        out, hn_next = out if next_row is not None else (out, None)
        return out, hn_next, (h, hn, s_up, s_gate, a)

    def out_proj(name, y, w_out, h, next_row):
        return _mm(name, [(y, _spec((tm, 256), lambda i, k: (i, k)), w_out, _spec((None, 256, D), lambda i, k: (k, 0, 0)))],
                   grid=(nt, NSH), o_shape=(t, D), o_dtype=F32, o_spec=_spec((tm, D), lambda i, k: (i, 0)),
                   dims=NN, kaxis=1, nk=NSH, acc_shape=(tm, D), res=(h, _spec((tm, D), lambda i, k: (i, 0))),
                   norm=(small, next_row))

    def out_proj_bwd(tag, dhb, y, w_out, blk, grad_b, make_rider=None):
        grad_b = _wgrad(tag + "_dwout", y, _spec((tw, 256), lambda s, k: (k, s)), dhb, _spec((tw, D), lambda s, k: (k, 0)),
                        256, D, t, tw, grad_b, blk)
        rider = make_rider(grad_b) if make_rider is not None else None
        dy = _mm(tag + "_dy", [(dhb, tok_si, w_out, _spec((None, 256, D), lambda s, i: (s, 0, 0)))],
                 grid=(NSH, nt), o_shape=(t, D), o_dtype=F32, o_spec=_spec((tm, 256), lambda s, i: (i, s)),
                 dims=NT, kaxis=1, nk=1, carry=rider)
        dy, landed = dy if rider is not None else (dy, None)
        return dy, grad_b, landed

    h0 = x
    h1, hn_ab, pre0 = ffn_forward("pre0", h0, _norm_fwd("pre0_norm", h0, small, R_PRE), R_MIX)
    proj_a = _mm("ab_proj_a", [(hn_ab, tok_si, w["ab_in"], _spec((None, D, 768), lambda s, i: (s, 0, 0)))],
                 grid=(2, nt), o_shape=(t, 1536), o_dtype=F32, o_spec=_spec((tm, 768), lambda s, i: (i, s)),
                 dims=NN, kaxis=1, nk=1)
    proj_b = _mm("ab_proj_b", [(hn_ab, tok_si, w["ab_in"], _spec((None, D, 768), lambda s, i: (s + 2, 0, 0)))],
                 grid=(2, nt), o_shape=(t, 1536), o_dtype=BF16, o_spec=_spec((tm, 768), lambda s, i: (i, s)),
                 dims=NN, kaxis=1, nk=1)
    y_a = _conv_fwd(proj_a, w["conv"])
    names, gather = carried("sb_fwd")
    y_b, ltot, nblk, got = _sb_fwd(proj_b, gather)
    land(names, got)
    y_ab = jnp.concatenate([y_a, y_b], axis=1)
    h2, hn = out_proj("ab_out", y_ab, w["ab_out"], h1, R_POST)
    h3, hn, post0 = ffn_forward("post0", h2, hn, R_PRE + 1)
    h4, hn_c, pre1 = ffn_forward("pre1", h3, hn, R_MIX + 1)
    proj_c = _mm("c_proj", [(hn_c, tok_si, w["c_in"], _spec((None, D, D), lambda s, i: (s, 0, 0)))],
                 grid=(NSH, nt), o_shape=(NSH, t, D), o_dtype=F32, o_spec=_spec((None, tm, D), lambda s, i: (s, i, 0)),
                 dims=NN, kaxis=1, nk=1)
    names, gather = carried("hgrn_fwd")
    o_c, y_c, sst, got = _hgrn_fwd(proj_c, small, gather)
    land(names, got)
    h5, hn = out_proj("c_out", y_c, w["c_out"], h4, R_POST + 1)
    h6, _, post1 = ffn_forward("post1", h5, hn, None)
    dh, dhb, d_fin, loss = _final_loss(h6, small, target)

    dh, dhb, grad_a, grad_b, dn_post1, _, _ = _ffn_backward("post1", dh, dhb, *post1, w["post1_gu"], w["post1_d"],
                                                            *FFN_SLOTS["post1"], small, R_POST + 1, 8 * FS, B_ROWS)
    dy_c, grad_b, swapped = out_proj_bwd("c", dhb, y_c, w["c_out"], B_COUT // 256, grad_b,
                                         lambda buf_b: swap_rider("x", dict(a=grad_a, b=buf_b)))
    pairs, rider = reduce_start("x", dict(a=grad_a, b=grad_b), swapped)
    dproj_c, d_clb, d_gam, landed = _hgrn_bwd(proj_c, small, o_c, sst, dy_c, rider)
    share = reduce_end("x", pairs, landed)
    grad_b = _wgrad("c_dwin", hn_c, _spec((tw, D), lambda s, k: (k, 0)), dproj_c, _spec((None, tw, D), lambda s, k: (s, k, 0)),
                    D, D, t, tw, grad_b, B_CIN // D, carry=share)
    if share is not None:
        grad_b, landed = grad_b
        shared("x", landed)
    dh, dhb, dn_mix1, _ = _dhn_norm("c_dhn", [(dproj_c, _spec((None, tm, D), lambda i, k: (k, i, 0)),
                                               w["c_in"], _spec((None, D, D), lambda i, k: (k, 0, 0)))],
                                    NT, h4, small, R_MIX + 1, dh)
    dh, dhb, grad_a, grad_b, dn_pre1, _, _ = _ffn_backward("pre1", dh, dhb, *pre1, w["pre1_gu"], w["pre1_d"],
                                                           *FFN_SLOTS["pre1"], small, R_PRE + 1, grad_a, grad_b)
    dh, dhb, grad_a, grad_b, dn_post0, _, _ = _ffn_backward("post0", dh, dhb, *post0, w["post0_gu"], w["post0_d"],
                                                            *FFN_SLOTS["post0"], small, R_POST, grad_a, grad_b)
    dy_ab, grad_b, _ = out_proj_bwd("ab", dhb, y_ab, w["ab_out"], B_ABOUT // 256, grad_b)
    dab, dac, dax, d_conv, swapped = _conv_bwd(proj_a, w["conv"], dy_ab, swap_rider("y", dict(a=grad_a, b=grad_b)))
    pairs, rider = reduce_start("y", dict(a=grad_a, b=grad_b), swapped)
    dq, dk, dv, landed = _sb_bwd(proj_b, dy_ab, ltot, nblk, rider)
    share = reduce_end("y", pairs, landed)
    dproj_ab = jnp.concatenate([dab, dac, dax, dq, dk, dv], axis=1)
    grad_c = _wgrad("ab_dwin", hn_ab, _spec((tw, D), lambda s, k: (k, 0)), dproj_ab, _spec((tw, 768), lambda s, k: (k, s)),
                    D, 768, t, tw, D, 0, carry=share)
    if share is not None:
        grad_c, landed = grad_c
        shared("y", landed)
    dh, dhb, dn_mix0, _ = _dhn_norm("ab_dhn", [(dproj_ab, _spec((tm, 768), lambda i, k: (i, k)),
                                                w["ab_in"], _spec((None, D, 768), lambda i, k: (k, 0, 0)))],
                                    NT, h1, small, R_MIX, dh)
    pairs, rider = reduce_start("z1", dict(c=grad_c))
    last = {}

    def own_gradients(buf_a, buf_b):
        last["z2"] = reduce_start("z2", dict(a=buf_a, b=buf_b))
        return last["z2"][1]

    dh, dhb, grad_a, grad_b, dn_pre0, landed, landed_own = _ffn_backward(
        "pre0", dh, dhb, *pre0, w["pre0_gu"], w["pre0_d"], *FFN_SLOTS["pre0"], small, R_PRE, grad_a, grad_b,
        rider, own_gradients if place is not None else None)
    if place is not None:
        mine = (_owner_sum("grad_owner_sum_z1", pairs, landed, place[1])
                + _owner_sum("grad_owner_sum_z2", last["z2"][0], landed_own, place[1]))
        reduced.update(zip(REDUCE_STAGES["z1"] + REDUCE_STAGES["z2"], _alone("grad_share_z", _Share(mine))))
    zero = jnp.zeros((1, D), F32)
    conv_rows = jnp.pad(jnp.transpose(d_conv[:, :3, :], (1, 0, 2)).reshape(3, 512), ((0, 0), (0, D - 512)))
    small_grad = jnp.concatenate([
        dn_pre0, dn_pre1, dn_mix0, dn_mix1, dn_post0, dn_post1, d_clb, d_fin,
        jnp.pad(d_gam, ((0, 0), (0, D - HD))), conv_rows,
        jnp.pad(loss, ((0, 0), (0, D - 1))), zero, zero], axis=0)
    return dh, grad_a, grad_b, grad_c, small_grad, reduced


def _small_slab(rows):
    parts = [jnp.pad(r.astype(F32), ((0, 0), (0, D - r.shape[1]))) for r in rows]
    slab = jnp.concatenate(parts, axis=0)
    return jnp.pad(slab, ((0, SMALL_ROWS - slab.shape[0]), (0, 0)))


def kernel(x, ffn_pre_norm, ffn_pre_w_gate, ffn_pre_w_up, ffn_pre_w_down, mix_norm, ffn_post_norm, ffn_post_w_gate, ffn_post_w_up, ffn_post_w_down, ab_w_in, ab_conv_w, ab_w_out, c_w_in, c_lower_bounds, c_out_norm, c_w_out, final_norm, loss_target, m_ffn_pre_norm, m_ffn_pre_w_gate, m_ffn_pre_w_up, m_ffn_pre_w_down, m_mix_norm, m_ffn_post_norm, m_ffn_post_w_gate, m_ffn_post_w_up, m_ffn_post_w_down, m_ab_w_in, m_ab_conv_w, m_ab_w_out, m_c_w_in, m_c_lower_bounds, m_c_out_norm, m_c_w_out, m_final_norm, v_ffn_pre_norm, v_ffn_pre_w_gate, v_ffn_pre_w_up, v_ffn_pre_w_down, v_mix_norm, v_ffn_post_norm, v_ffn_post_w_gate, v_ffn_post_w_up, v_ffn_post_w_down, v_ab_w_in, v_ab_conv_w, v_ab_w_out, v_c_w_in, v_c_lower_bounds, v_c_out_norm, v_c_w_out, v_final_norm):
    t = x.shape[1]
    xi, yi, ci = lax.axis_index("x"), lax.axis_index("y"), lax.axis_index("c")
    p_idx = (2 * xi + yi).astype(jnp.int32).reshape(1)
    c_idx = ci.astype(jnp.int32).reshape(1)

    def halves(m):
        return m.astype(BF16).reshape(2, m.shape[0] // 2, m.shape[1])

    transposed = ("ffn_pre_w_gate", "ffn_pre_w_up", "ffn_post_w_gate", "ffn_post_w_up")

    def flip(a):
        return jnp.swapaxes(a, 1, 2)

    shards = {}
    for name, (w_gate, w_up, w_down, layer) in dict(
            pre0=(ffn_pre_w_gate, ffn_pre_w_up, ffn_pre_w_down, 0), post0=(ffn_post_w_gate, ffn_post_w_up, ffn_post_w_down, 0),
            pre1=(ffn_pre_w_gate, ffn_pre_w_up, ffn_pre_w_down, 1), post1=(ffn_post_w_gate, ffn_post_w_up, ffn_post_w_down, 1)).items():
        shards[name + "_gu"] = jnp.stack([flip(w_gate)[layer], flip(w_up)[layer]]).astype(BF16)
        shards[name + "_d"] = halves(w_down[layer])
    conv_pad = jnp.pad(ab_conv_w[0], ((0, 5), (0, 0)))
    shards.update(ab_in=halves(ab_w_in[0]), ab_out=halves(ab_w_out[0]), c_in=halves(c_w_in[0]), c_out=halves(c_w_out[0]),
                  conv=jnp.stack([conv_pad, jnp.zeros_like(conv_pad)]))
    first = _alone("gather_weights", _Gather([shards["pre0_gu"]]))[0].reshape(W_SHAPES["pre0_gu"])

    small = _small_slab([ffn_pre_norm, mix_norm, ffn_post_norm, c_lower_bounds, final_norm.reshape(1, D), c_out_norm])
    small = small.reshape(SMALL_ROWS, 1, D)

    grad_x, _, _, _, small_grad, reduced = _local_step(x[0], loss_target[0], dict(pre0_gu=first), small, shards,
                                                        (c_idx, p_idx))
    whole = {n: g.reshape(2 * g.shape[1], g.shape[2]) for n, g in reduced.items()}
    small_sum = _sum_small(small_grad)

    my_conv = lax.dynamic_slice(small_sum[R_CONV:R_CONV + 3], (0, (2 * xi + yi) * 128), (3, 128))
    grads = {
        "ffn_pre_norm": small_sum[R_PRE:R_PRE + 2], "mix_norm": small_sum[R_MIX:R_MIX + 2],
        "ffn_post_norm": small_sum[R_POST:R_POST + 2], "c_lower_bounds": small_sum[R_CLB:R_CLB + 2],
        "c_out_norm": small_sum[R_GAM:R_GAM + 1, :HD], "final_norm": small_sum[R_FIN],
        "ab_conv_w": my_conv.reshape(1, 3, 128),
    }
    layers = dict(ab_w_in=[whole["ab_in"]], ab_w_out=[whole["ab_out"]], c_w_in=[whole["c_in"]], c_w_out=[whole["c_out"]])
    for kind, key in (("gate", "_g"), ("up", "_u"), ("down", "_d")):
        layers["ffn_pre_w_" + kind] = [whole["pre0" + key], whole["pre1" + key]]
        layers["ffn_post_w_" + kind] = [whole["post0" + key], whole["post1" + key]]
    weights = dict(ffn_pre_norm=ffn_pre_norm, ffn_pre_w_gate=ffn_pre_w_gate, ffn_pre_w_up=ffn_pre_w_up, ffn_pre_w_down=ffn_pre_w_down, mix_norm=mix_norm, ffn_post_norm=ffn_post_norm, ffn_post_w_gate=ffn_post_w_gate, ffn_post_w_up=ffn_post_w_up, ffn_post_w_down=ffn_post_w_down, ab_w_in=ab_w_in, ab_conv_w=ab_conv_w, ab_w_out=ab_w_out, c_w_in=c_w_in, c_lower_bounds=c_lower_bounds, c_out_norm=c_out_norm, c_w_out=c_w_out, final_norm=final_norm)
    m_in = dict(ffn_pre_norm=m_ffn_pre_norm, ffn_pre_w_gate=m_ffn_pre_w_gate, ffn_pre_w_up=m_ffn_pre_w_up, ffn_pre_w_down=m_ffn_pre_w_down, mix_norm=m_mix_norm, ffn_post_norm=m_ffn_post_norm, ffn_post_w_gate=m_ffn_post_w_gate, ffn_post_w_up=m_ffn_post_w_up, ffn_post_w_down=m_ffn_post_w_down, ab_w_in=m_ab_w_in, ab_conv_w=m_ab_conv_w, ab_w_out=m_ab_w_out, c_w_in=m_c_w_in, c_lower_bounds=m_c_lower_bounds, c_out_norm=m_c_out_norm, c_w_out=m_c_w_out, final_norm=m_final_norm)
    v_in = dict(ffn_pre_norm=v_ffn_pre_norm, ffn_pre_w_gate=v_ffn_pre_w_gate, ffn_pre_w_up=v_ffn_pre_w_up, ffn_pre_w_down=v_ffn_pre_w_down, mix_norm=v_mix_norm, ffn_post_norm=v_ffn_post_norm, ffn_post_w_gate=v_ffn_post_w_gate, ffn_post_w_up=v_ffn_post_w_up, ffn_post_w_down=v_ffn_post_w_down, ab_w_in=v_ab_w_in, ab_conv_w=v_ab_conv_w, ab_w_out=v_ab_w_out, c_w_in=v_c_w_in, c_lower_bounds=v_c_lower_bounds, c_out_norm=v_c_out_norm, c_w_out=v_c_w_out, final_norm=v_final_norm)
    names = list(weights)
    big = [n for n in names if weights[n].size >= 65536]
    tiny = [n for n in names if n not in big]

    delta, new_m, new_v = {}, {}, {}
    for n in big:
        turn = flip if n in transposed else (lambda a: a)
        shape = turn(weights[n]).shape
        two_d = (shape[0] * shape[1], shape[2])
        d, m2, v2, g2 = _adamw("adamw_" + n, turn(weights[n]).reshape(two_d), layers[n],
                               turn(m_in[n]).reshape(two_d), turn(v_in[n]).reshape(two_d))
        delta[n], new_m[n], new_v[n] = turn(d.reshape(shape)), turn(m2.reshape(shape)), turn(v2.reshape(shape))
        grads[n] = turn(g2.reshape(shape))

    def tiny_slab(src):
        return _small_slab([src[n].reshape(-1, src[n].shape[-1]) for n in tiny])

    offs, row = {}, 0
    for n in tiny:
        nrows = weights[n].size // weights[n].shape[-1]
        offs[n] = (row, nrows)
        row += nrows
    d, m2, v2, _ = _adamw("adamw_small", tiny_slab(weights), [tiny_slab(grads)], tiny_slab(m_in), tiny_slab(v_in))
    for n in tiny:
        r0, nr = offs[n]
        shape = weights[n].shape
        for dst, src in ((delta, d), (new_m, m2), (new_v, v2)):
            dst[n] = src[r0:r0 + nr, :shape[-1]].reshape(shape)

    loss = small_sum[R_LOSS, 0]
    return (loss, grad_x.reshape(1, t, D), *[grads[n] for n in names], *[delta[n] for n in names],
            *[new_m[n] for n in names], *[new_v[n] for n in names])
```

```python
import functools
import math

import jax
import jax.numpy as jnp
from jax import lax
from jax.experimental import pallas as pl
from jax.experimental.pallas import tpu as pltpu

F32 = jnp.float32
BF16 = jnp.bfloat16
MESH = pl.DeviceIdType.MESH
ANY = pl.BlockSpec(memory_space=pl.ANY)

D = 1024
FS = 704
NSH = 4
RMS_EPS = 1e-6
MACARON = 0.5
CHUNK = 64
HD = 128
HGRN_HPS = 8
SBQ = 128
SB_PAIRS_FWD = 4
SB_PAIRS_BWD = 4
SB_DEAD = -105.0
CONV_HALO = 8
LANE = 128
ROW_TILE = 512
MM_TILE = 1024
WGRAD_TILE = 4096
VMEM_LIMIT = 48 * 1024 * 1024
VMEM_LIMIT_BIG = 58 * 1024 * 1024

ADAM_LR, ADAM_B1, ADAM_B2, ADAM_EPS, ADAM_WD, ADAM_STEP = 0.001, 0.9, 0.999, 1e-08, 0.01, 10

NN = ((1,), (0,))
NT = ((1,), (1,))
TN = ((0,), (0,))

SMALL_ROWS = 16
R_PRE, R_MIX, R_POST, R_CLB, R_FIN, R_GAM, R_CONV, R_LOSS = 0, 2, 4, 6, 8, 9, 10, 13

B_DOWN = 0
B_ABOUT = 4 * FS
B_CIN = B_ABOUT + 256
B_COUT = B_CIN + 1024
B_ROWS = B_COUT + 256


def _dg(a, b, dims):
    return lax.dot_general(a, b, (dims, ((), ())), preferred_element_type=F32)


def _split(x):
    hi = x.astype(BF16)
    lo = (x - hi.astype(F32)).astype(BF16)
    return hi, lo


def _dot3(a, b, dims):
    ah, al = _split(a)
    bh, bl = _split(b)
    if dims == TN:
        n = b.shape[1]
        both = _dg(ah, jnp.concatenate([bh, bl], axis=1), dims)
        return both[:, :n] + both[:, n:] + _dg(al, bh, dims)
    m = a.shape[0]
    both = _dg(jnp.concatenate([ah, al], axis=0), bh, dims)
    return both[:m] + both[m:] + _dg(ah, bl, dims)


def _sigmoid(x):
    return 1.0 / (1.0 + jnp.exp(-x))


def _params(sem):
    return pltpu.CompilerParams(dimension_semantics=sem, vmem_limit_bytes=VMEM_LIMIT)


def _spec(shape, imap):
    return pl.BlockSpec(shape, imap)


def _accumulate(acc_ref, pairs, dims):
    part = None
    for a_ref, b_ref in pairs:
        d = _dg(a_ref[...], b_ref[...], dims)
        part = d if part is None else part + d
    acc_ref[...] += part


def _mm(name, pairs, *, grid, o_shape, o_dtype, o_spec, dims, kaxis, nk, acc_shape=None, res=None, scale=None,
        into=None, carry=None, norm=None):
    npairs = len(pairs)
    operands, specs = [], []
    for a, a_spec, b, b_spec in pairs:
        operands += [a, b]
        specs += [a_spec, b_spec]
    if res is not None:
        operands.append(res[0])
        specs.append(res[1])
    aliases = {}
    if into is not None:
        aliases = {len(operands): 0}
        operands.append(into)
        specs.append(ANY)
    if norm is not None:
        operands.append(norm[0])
        specs.append(_spec((None, 1, D), lambda *_: (norm[1], 0, 0)))
    n_own = len(operands)
    n_out = 2 if norm is not None else 1
    nc = carry.n if carry is not None else 0
    if nc:
        operands += carry.groups
        specs += carry.in_specs

    def body(*refs):
        o_ref = refs[n_own + nc]
        riders = ((refs[n_own:n_own + nc], refs[n_own + nc + n_out:n_own + 2 * nc + n_out], refs[-2], refs[-1])
                  if nc else None)
        if nc:
            carry.ride(grid, riders, "start")

        def finish(val):
            if scale is not None:
                val = val * scale
            if res is not None:
                val = val + refs[2 * npairs][...]
            o_ref[...] = val.astype(o_dtype)
            if norm is not None:
                r = lax.rsqrt(jnp.mean(val * val, axis=-1, keepdims=True) + RMS_EPS)
                refs[n_own + nc + 1][...] = (val * r * refs[n_own - 1][...]).astype(BF16)

        if nk == 1:
            part = None
            for n in range(npairs):
                d = _dg(refs[2 * n][...], refs[2 * n + 1][...], dims)
                part = d if part is None else part + d
            finish(part)
        else:
            acc_ref = refs[n_own + 2 * nc + n_out]
            k = pl.program_id(kaxis)

            @pl.when(k == 0)
            def _():
                acc_ref[...] = jnp.zeros_like(acc_ref)

            _accumulate(acc_ref, [(refs[2 * n], refs[2 * n + 1]) for n in range(npairs)], dims)

            @pl.when(k == nk - 1)
            def _():
                finish(acc_ref[...])
        if nc:
            carry.ride(grid, riders, "finish")

    sem = tuple("arbitrary" if (nc or (ax == kaxis and nk > 1)) else "parallel" for ax in range(len(grid)))
    outs = pl.pallas_call(
        body, name=name, grid=grid, in_specs=specs,
        out_specs=[o_spec] * n_out + (carry.out_specs if nc else []),
        out_shape=[jax.ShapeDtypeStruct(o_shape, o_dtype)] + ([jax.ShapeDtypeStruct(o_shape, BF16)] if norm is not None else [])
        + (carry.out_shape if nc else []),
        scratch_shapes=([pltpu.VMEM(acc_shape, F32)] if nk > 1 else []) + (carry.scratch if nc else []),
        input_output_aliases=aliases,
        compiler_params=_params(sem),
    )(*operands)
    main = (outs[0], outs[1]) if norm is not None else outs[0]
    return (main, carry.place(outs[n_out:])) if nc else main


def _norm_fwd(name, h, gain_slab, row):
    t = h.shape[0]
    tm = min(ROW_TILE, t)

    def body(h_ref, g_ref, o_ref):
        x = h_ref[...]
        r = lax.rsqrt(jnp.mean(x * x, axis=-1, keepdims=True) + RMS_EPS)
        o_ref[...] = (x * r * g_ref[...]).astype(BF16)

    return pl.pallas_call(
        body, name=name, grid=(t // tm,),
        in_specs=[_spec((tm, D), lambda i: (i, 0)), _spec((None, 1, D), lambda i: (row, 0, 0))],
        out_specs=_spec((tm, D), lambda i: (i, 0)),
        out_shape=jax.ShapeDtypeStruct((t, D), BF16),
        compiler_params=_params(("parallel",)),
    )(h, gain_slab)


def _dhn_norm(name, pairs, dims, h, gain_slab, row, dres, rider=None):
    t = h.shape[0]
    tm = min(MM_TILE, t)
    nt = t // tm
    grid = (nt, NSH)
    npairs = len(pairs)
    nr = rider.n if rider is not None else 0
    operands, specs = [], []
    for a, a_spec, b, b_spec in pairs:
        operands += [a, b]
        specs += [a_spec, b_spec]
    row_spec = _spec((tm, D), lambda i, k: (i, 0))
    operands += [h, gain_slab, dres]
    specs += [row_spec, _spec((None, 1, D), lambda i, k: (row, 0, 0)), row_spec]
    n_own = len(operands)

    def body(*refs):
        h_ref, g_ref, dres_ref = refs[2 * npairs:n_own]
        dh_ref, dhb_ref, dg_ref = refs[n_own + nr:n_own + nr + 3]
        acc_ref, gacc_ref = refs[n_own + 2 * nr + 3:n_own + 2 * nr + 5]
        riders = (refs[n_own:n_own + nr], refs[n_own + nr + 3:n_own + 2 * nr + 3], refs[-2], refs[-1]) if nr else None
        if nr:
            rider.ride(grid, riders, "start")
        i, k = pl.program_id(0), pl.program_id(1)

        @pl.when(k == 0)
        def _():
            acc_ref[...] = jnp.zeros_like(acc_ref)

        _accumulate(acc_ref, [(refs[2 * n], refs[2 * n + 1]) for n in range(npairs)], dims)

        @pl.when(k == NSH - 1)
        def _():
            x = h_ref[...]
            r = lax.rsqrt(jnp.mean(x * x, axis=-1, keepdims=True) + RMS_EPS)
            xh = x * r
            dy = acc_ref[...]
            gdy = dy * g_ref[...]
            dh = dres_ref[...] + (gdy - xh * jnp.mean(gdy * xh, axis=-1, keepdims=True)) * r
            dh_ref[...] = dh
            dhb_ref[...] = dh.astype(BF16)
            gpart = jnp.sum((dy * xh).reshape(tm // 8, 8, D), axis=0)

            @pl.when(i == 0)
            def _():
                gacc_ref[...] = gpart

            @pl.when(i > 0)
            def _():
                gacc_ref[...] += gpart

            @pl.when(i == nt - 1)
            def _():
                dg_ref[...] = jnp.sum(gacc_ref[...], axis=0, keepdims=True)

        if nr:
            rider.ride(grid, riders, "finish")

    outs = pl.pallas_call(
        body, name=name, grid=grid, in_specs=specs + (rider.in_specs if nr else []),
        out_specs=[row_spec, row_spec, _spec((1, D), lambda i, k: (0, 0))] + (rider.out_specs if nr else []),
        out_shape=[jax.ShapeDtypeStruct((t, D), F32), jax.ShapeDtypeStruct((t, D), BF16),
                   jax.ShapeDtypeStruct((1, D), F32)] + (rider.out_shape if nr else []),
        scratch_shapes=[pltpu.VMEM((tm, D), F32), pltpu.VMEM((8, D), F32)] + (rider.scratch if nr else []),
        compiler_params=pltpu.CompilerParams(dimension_semantics=("arbitrary", "arbitrary"),
                                             vmem_limit_bytes=VMEM_LIMIT_BIG),
    )(*operands, *(rider.groups if nr else []))
    return outs[0], outs[1], outs[2], (rider.place(outs[3:]) if nr else [])


def _final_loss(h, gain_slab, target):
    t = h.shape[0]
    tm = min(ROW_TILE, t)
    nt = t // tm

    def body(h_ref, g_ref, t_ref, dh_ref, dhb_ref, dg_ref, loss_ref, acc_ref, lacc_ref):
        i = pl.program_id(0)
        x = h_ref[...]
        g = g_ref[...]
        r = lax.rsqrt(jnp.mean(x * x, axis=-1, keepdims=True) + RMS_EPS)
        xh = x * r
        err = xh * g - t_ref[...]
        dy = err * (1.0 / D)
        gdy = dy * g
        dh = (gdy - xh * jnp.mean(gdy * xh, axis=-1, keepdims=True)) * r
        dh_ref[...] = dh
        dhb_ref[...] = dh.astype(BF16)
        part = jnp.sum((dy * xh).reshape(tm // 8, 8, D), axis=0)
        lpart = jnp.sum((err * err).reshape(tm // 8, 8, D), axis=0)

        @pl.when(i == 0)
        def _():
            acc_ref[...] = part
            lacc_ref[...] = lpart

        @pl.when(i > 0)
        def _():
            acc_ref[...] += part
            lacc_ref[...] += lpart

        @pl.when(i == nt - 1)
        def _():
            dg_ref[...] = jnp.sum(acc_ref[...], axis=0, keepdims=True)
            rows = jnp.sum(lacc_ref[...], axis=0, keepdims=True)
            loss_ref[...] = jnp.sum(rows, axis=1, keepdims=True) * (0.5 / D)

    row_spec = _spec((tm, D), lambda i: (i, 0))
    return pl.pallas_call(
        body, name="final_loss", grid=(nt,),
        in_specs=[row_spec, _spec((None, 1, D), lambda i: (R_FIN, 0, 0)), row_spec],
        out_specs=[row_spec, row_spec, _spec((1, D), lambda i: (0, 0)), _spec((1, 1), lambda i: (0, 0))],
        out_shape=[jax.ShapeDtypeStruct((t, D), F32), jax.ShapeDtypeStruct((t, D), BF16),
                   jax.ShapeDtypeStruct((1, D), F32), jax.ShapeDtypeStruct((1, 1), F32)],
        scratch_shapes=[pltpu.VMEM((8, D), F32), pltpu.VMEM((8, D), F32)],
        compiler_params=_params(("arbitrary",)),
    )(h, gain_slab, target)


def _ffn_up(name, hn, wgu, carry=None):
    t = hn.shape[0]
    tm = min(MM_TILE, t)
    grid = (NSH, t // tm)
    nc = carry.n if carry is not None else 0

    def body(*refs):
        x_ref, wg_ref, wu_ref = refs[:3]
        s_up_ref, s_gate_ref, a_ref = refs[3 + nc:6 + nc]
        riders = (refs[3:3 + nc], refs[6 + nc:6 + 2 * nc], refs[-2], refs[-1]) if nc else None
        if nc:
            carry.ride(grid, riders, "start")
        x = x_ref[...]
        g = _dg(x, wg_ref[...], NT)
        u = _dg(x, wu_ref[...], NT)
        sg = _sigmoid(g)
        silu = g * sg
        s_up_ref[...] = (MACARON * silu).astype(BF16)
        s_gate_ref[...] = (MACARON * u * (sg * (1.0 + g * (1.0 - sg)))).astype(BF16)
        a_ref[...] = (silu * u).astype(BF16)
        if nc:
            carry.ride(grid, riders, "finish")

    act = _spec((None, tm, FS), lambda s, i: (s, i, 0))
    shape = jax.ShapeDtypeStruct((NSH, t, FS), BF16)
    outs = pl.pallas_call(
        body, name=name, grid=grid,
        in_specs=[_spec((tm, D), lambda s, i: (i, 0)),
                  _spec((None, None, FS, D), lambda s, i: (s, 0, 0, 0)),
                  _spec((None, None, FS, D), lambda s, i: (s, 1, 0, 0))] + (carry.in_specs if nc else []),
        out_specs=[act, act, act] + (carry.out_specs if nc else []),
        out_shape=[shape, shape, shape] + (carry.out_shape if nc else []),
        scratch_shapes=carry.scratch if nc else [],
        compiler_params=_params(("arbitrary", "arbitrary") if nc else ("parallel", "parallel")),
    )(hn, wgu, wgu, *(carry.groups if nc else []))
    return (outs[0], outs[1], outs[2], carry.place(outs[3:]) if nc else [])


def _ffn_down(name, a, wd, h, carry=None, norm=None):
    t = h.shape[0]
    tm = min(MM_TILE, t)
    return _mm(name, [(a, _spec((None, tm, FS), lambda i, k: (k, i, 0)),
                       wd, _spec((None, FS, D), lambda i, k: (k, 0, 0)))],
               grid=(t // tm, NSH), o_shape=(t, D), o_dtype=F32, o_spec=_spec((tm, D), lambda i, k: (i, 0)),
               dims=NN, kaxis=1, nk=NSH, acc_shape=(tm, D), res=(h, _spec((tm, D), lambda i, k: (i, 0))),
               scale=MACARON, carry=carry, norm=norm)


def _ffn_bwd_up(name, dhb, wd, s_up, s_gate, rider=None):
    t = dhb.shape[0]
    tm = min(MM_TILE, t)
    grid = (NSH, t // tm)
    nr = rider.n if rider is not None else 0

    def body(*refs):
        dh_ref, wd_ref, s_up_ref, s_gate_ref = refs[:4]
        dg_ref, du_ref = refs[4 + nr:6 + nr]
        riders = (refs[4:4 + nr], refs[6 + nr:6 + 2 * nr], refs[-2], refs[-1]) if nr else None
        if nr:
            rider.ride(grid, riders, "start")
        da = _dg(dh_ref[...], wd_ref[...], NT).astype(BF16)
        du_ref[...] = da * s_up_ref[...]
        dg_ref[...] = da * s_gate_ref[...]
        if nr:
            rider.ride(grid, riders, "finish")

    act = _spec((None, tm, FS), lambda s, i: (s, i, 0))
    shape = jax.ShapeDtypeStruct((NSH, t, FS), BF16)
    outs = pl.pallas_call(
        body, name=name, grid=grid,
        in_specs=[_spec((tm, D), lambda s, i: (i, 0)), _spec((None, FS, D), lambda s, i: (s, 0, 0)), act, act]
        + (rider.in_specs if nr else []),
        out_specs=[act, act] + (rider.out_specs if nr else []),
        out_shape=[shape, shape] + (rider.out_shape if nr else []),
        scratch_shapes=rider.scratch if nr else [],
        compiler_params=_params(("arbitrary", "arbitrary") if nr else ("parallel", "parallel")),
    )(dhb, wd, s_up, s_gate, *(rider.groups if nr else []))
    return outs[0], outs[1], (rider.place(outs[2:]) if nr else [])


def _wgrad(name, a, a_spec, b, b_spec, out_rows, out_cols, t, tt, group, slot, scale=None, carry=None):
    first = isinstance(group, int)
    rows = group if first else group.shape[1]
    return _mm(name, [(a, a_spec, b, b_spec)], grid=(NSH, t // tt),
               o_shape=(NSH, rows, out_cols), o_dtype=BF16,
               o_spec=_spec((None, out_rows, out_cols), lambda s, k: (s, slot, 0)),
               dims=TN, kaxis=1, nk=t // tt, acc_shape=(out_rows, out_cols), scale=scale,
               into=None if first else group, carry=carry)


def _ffn_backward(tag, dh, dhb, h_in, hn, s_up, s_gate, a, wgu, wd, gate_idx, up_idx, down_idx, small, norm_row,
                  grad_a, grad_b, rider_up=None, rider_dhn=None):
    t = dh.shape[0]
    tm = min(MM_TILE, t)
    tt = min(WGRAD_TILE, t)
    tok = _spec((tt, D), lambda s, k: (k, 0))
    hid = _spec((None, tt, FS), lambda s, k: (s, k, 0))
    grad_b = _wgrad(tag + "_dwd", a, hid, dhb, tok, FS, D, t, tt, grad_b, down_idx, scale=MACARON)
    rider = rider_up(grad_a, grad_b) if rider_up is not None else None
    dg, du, landed_up = _ffn_bwd_up(tag + "_bwd_up", dhb, wd, s_up, s_gate, rider)
    grad_a = _wgrad(tag + "_dwg", dg, hid, hn, tok, FS, D, t, tt, grad_a, gate_idx)
    grad_a = _wgrad(tag + "_dwu", du, hid, hn, tok, FS, D, t, tt, grad_a, up_idx)
    rider = rider_dhn(grad_a, grad_b, landed_up) if rider_dhn is not None else None
    act = _spec((None, tm, FS), lambda i, k: (k, i, 0))
    dh_in, dhb_in, d_gain, landed_dhn = _dhn_norm(
        tag + "_dhn", [(dg, act, wgu, _spec((None, None, FS, D), lambda i, k: (k, 0, 0, 0))),
                       (du, act, wgu, _spec((None, None, FS, D), lambda i, k: (k, 1, 0, 0)))],
        NN, h_in, small, norm_row, dh, rider)
    return dh_in, dhb_in, grad_a, grad_b, d_gain, landed_up, landed_dhn


def _conv_fwd(proj_a, conv_w):
    t = proj_a.shape[0]
    tm = min(ROW_TILE, t)
    hb = tm // CONV_HALO

    def body(ab_ref, ac_ref, ax_ref, acp_ref, axp_ref, w_ref, y_ref):
        i = pl.program_id(1)
        u = ac_ref[...] * ax_ref[...]
        up = jnp.where(i > 0, acp_ref[...] * axp_ref[...], 0.0)
        ext = jnp.concatenate([up, u], axis=0)
        u1 = pltpu.roll(ext, 1, 0)[CONV_HALO:]
        u2 = pltpu.roll(ext, 2, 0)[CONV_HALO:]
        w = w_ref[...]
        conv = w[0:1] * u2 + w[1:2] * u1 + w[2:3] * u
        y_ref[...] = (ab_ref[...] * conv).astype(BF16)

    def cur(off):
        return _spec((tm, LANE), lambda j, i: (i, off + j))

    def prev(off):
        return _spec((CONV_HALO, LANE), lambda j, i: (jnp.maximum(i * hb - 1, 0), off + j))

    return pl.pallas_call(
        body, name="conv_fwd", grid=(4, t // tm),
        in_specs=[cur(0), cur(4), cur(8), prev(4), prev(8),
                  _spec((None, None, 8, LANE), lambda j, i: (j, 0, 0, 0))],
        out_specs=_spec((tm, LANE), lambda j, i: (i, j)),
        out_shape=jax.ShapeDtypeStruct((t, 512), BF16),
        compiler_params=_params(("parallel", "parallel")),
    )(proj_a, proj_a, proj_a, proj_a, proj_a, conv_w)


def _conv_bwd(proj_a, conv_w, dy, rider=None):
    t = proj_a.shape[0]
    tm = min(ROW_TILE, t)
    hb = tm // CONV_HALO
    nt = t // tm
    grid = (4, nt)
    nr = rider.n if rider is not None else 0

    def body(*refs):
        ab_ref, ac_ref, ax_ref, dy_ref, acp_ref, axp_ref, abn_ref, dyn_ref, w_ref = refs[:9]
        dab_ref, dac_ref, dax_ref, dw_ref = refs[9 + nr:13 + nr]
        acc_ref = refs[13 + 2 * nr]
        riders = (refs[9:9 + nr], refs[13 + nr:13 + 2 * nr], refs[-2], refs[-1]) if nr else None
        if nr:
            rider.ride(grid, riders, "start")
        i = pl.program_id(1)
        ab, ac, ax = ab_ref[...], ac_ref[...], ax_ref[...]
        u = ac * ax
        up = jnp.where(i > 0, acp_ref[...] * axp_ref[...], 0.0)
        ext = jnp.concatenate([up, u], axis=0)
        u1 = pltpu.roll(ext, 1, 0)[CONV_HALO:]
        u2 = pltpu.roll(ext, 2, 0)[CONV_HALO:]
        w = w_ref[...]
        conv = w[0:1] * u2 + w[1:2] * u1 + w[2:3] * u
        dy_v = dy_ref[...]
        dab_ref[...] = (dy_v * conv).astype(BF16)
        dc = dy_v * ab
        dcn = jnp.where(i < nt - 1, dyn_ref[...] * abn_ref[...], 0.0)
        extn = jnp.concatenate([dc, dcn], axis=0)
        n = tm + CONV_HALO
        dc1 = pltpu.roll(extn, n - 1, 0)[:tm]
        dc2 = pltpu.roll(extn, n - 2, 0)[:tm]
        du = w[2:3] * dc + w[1:2] * dc1 + w[0:1] * dc2
        dac_ref[...] = (du * ax).astype(BF16)
        dax_ref[...] = (du * ac).astype(BF16)
        rid = lax.broadcasted_iota(jnp.int32, (8, LANE), 0)
        part = jnp.where(rid == 0, jnp.sum(dc * u2, axis=0, keepdims=True),
                         jnp.where(rid == 1, jnp.sum(dc * u1, axis=0, keepdims=True),
                                   jnp.where(rid == 2, jnp.sum(dc * u, axis=0, keepdims=True), 0.0)))

        @pl.when(i == 0)
        def _():
            acc_ref[...] = part

        @pl.when(i > 0)
        def _():
            acc_ref[...] += part

        @pl.when(i == nt - 1)
        def _():
            dw_ref[...] = acc_ref[...]

        if nr:
            rider.ride(grid, riders, "finish")

    def cur(off):
        return _spec((tm, LANE), lambda j, i: (i, off + j))

    def prev(off):
        return _spec((CONV_HALO, LANE), lambda j, i: (jnp.maximum(i * hb - 1, 0), off + j))

    def nxt(off):
        return _spec((CONV_HALO, LANE), lambda j, i: (jnp.minimum((i + 1) * hb, nt * hb - 1), off + j))

    outs = pl.pallas_call(
        body, name="conv_bwd", grid=grid,
        in_specs=[cur(0), cur(4), cur(8), cur(0), prev(4), prev(8), nxt(0), nxt(0),
                  _spec((None, None, 8, LANE), lambda j, i: (j, 0, 0, 0))] + (rider.in_specs if nr else []),
        out_specs=[_spec((tm, LANE), lambda j, i: (i, j)), _spec((tm, LANE), lambda j, i: (i, j)),
                   _spec((tm, LANE), lambda j, i: (i, j)), _spec((None, 8, LANE), lambda j, i: (j, 0, 0))]
        + (rider.out_specs if nr else []),
        out_shape=[jax.ShapeDtypeStruct((t, 512), BF16), jax.ShapeDtypeStruct((t, 512), BF16),
                   jax.ShapeDtypeStruct((t, 512), BF16), jax.ShapeDtypeStruct((4, 8, LANE), F32)]
        + (rider.out_shape if nr else []),
        scratch_shapes=[pltpu.VMEM((8, LANE), F32)] + (rider.scratch if nr else []),
        compiler_params=_params(("arbitrary", "arbitrary") if nr else ("parallel", "arbitrary")),
    )(proj_a, proj_a, proj_a, dy, proj_a, proj_a, proj_a, dy, conv_w, *(rider.groups if nr else []))
    return outs[0], outs[1], outs[2], outs[3], (rider.place(outs[4:]) if nr else [])


def _log_sigmoid(z):
    return jnp.minimum(z, 0.0) - jnp.log(1.0 + jnp.exp(-jnp.abs(z)))


def _sb_masks():
    row = lax.broadcasted_iota(jnp.int32, (SBQ, SBQ), 0)
    col = lax.broadcasted_iota(jnp.int32, (SBQ, SBQ), 1)
    return row, col


def _ones_where(mask):
    return jnp.where(mask, 1.0, 0.0).astype(BF16)


def _head_mask(head):
    lane = lax.broadcasted_iota(jnp.int32, (1, LANE), 1)
    return lane >= 64 if head else lane < 64


def _sb_fwd(proj_b, carry=None):
    t = proj_b.shape[0]
    nq = t // SBQ
    scale = 1.0 / math.sqrt(64.0)

    npair = SB_PAIRS_FWD
    wide = npair * LANE
    ngrp = 4 // npair
    chains = [(p, head) for p in range(npair) for head in range(2)]

    grid = (ngrp, nq)
    nc = carry.n if carry is not None else 0

    def body(*refs):
        q_ref, k_ref, v_ref = refs[:3]
        y_ref, l_ref, n_ref = refs[3 + nc:6 + nc]
        riders = (refs[3:3 + nc], refs[6 + nc:6 + 2 * nc], refs[-2], refs[-1]) if nc else None
        if nc:
            carry.ride(grid, riders, "start")
        grp = pl.program_id(0)
        qi = pl.program_id(1)
        row, col = _sb_masks()
        m_suffix = _ones_where(row > col)
        rows = len(chains) * SBQ
        strict = (lax.broadcasted_iota(jnp.int32, (rows, SBQ), 1)
                  < (lax.broadcasted_iota(jnp.int32, (rows, SBQ), 0) & (SBQ - 1)))
        q_pair = []
        for p in range(npair):
            q_all = q_ref[:, p * LANE:(p + 1) * LANE]
            q_pair.append(jnp.concatenate([jnp.where(_head_mask(head), q_all, jnp.zeros_like(q_all)) for head in range(2)],
                                          axis=0))

        def block(kb, state, diag):
            run, acc = state
            start = pl.multiple_of(kb * SBQ, SBQ)
            z = jnp.concatenate([_dg(q_pair[p], k_ref[pl.ds(start, SBQ), p * LANE:(p + 1) * LANE], NT)
                                 for p in range(npair)], axis=0) * scale
            lb = _log_sigmoid(z)
            lk = lb - z
            if diag:
                lk = jnp.where(strict, lk, 0.0)
            hi, lo = _split(lk)
            sums = _dg(jnp.concatenate([hi, lo], axis=0), m_suffix, NN)
            w = jnp.exp(lb + (run + sums[:rows] + sums[rows:]))
            if diag:
                w = jnp.where(strict, w, 0.0)
            wb = w.astype(BF16)
            acc = acc + jnp.concatenate(
                [_dg(wb[2 * p * SBQ:2 * (p + 1) * SBQ], v_ref[pl.ds(start, SBQ), p * LANE:(p + 1) * LANE], NN)
                 for p in range(npair)], axis=0)
            run = run + jnp.sum(hi.astype(F32) + lo.astype(F32), axis=1, keepdims=True)
            return run, acc

        state = block(qi, (jnp.zeros((rows, 1), F32), jnp.zeros((rows, LANE), F32)), True)

        def live(c):
            return jnp.logical_and(c[0] < qi, jnp.max(c[1][0]) > SB_DEAD)

        def step(c):
            return c[0] + 1, block(qi - 1 - c[0], c[1], False)

        count, (run, acc) = lax.while_loop(live, step, (jnp.int32(0), state))
        n_ref[grp * nq + qi] = count.astype(F32)
        hm = _head_mask(0)
        for p in range(npair):
            lo_rows, hi_rows = slice(2 * p * SBQ, (2 * p + 1) * SBQ), slice((2 * p + 1) * SBQ, (2 * p + 2) * SBQ)
            y_ref[:, p * LANE:(p + 1) * LANE] = jnp.where(hm, acc[lo_rows], acc[hi_rows]).astype(BF16)
            l_ref[p] = jnp.where(hm, run[lo_rows], run[hi_rows])
        if nc:
            carry.ride(grid, riders, "finish")

    outs = pl.pallas_call(
        body, name="sb_fwd", grid=grid,
        in_specs=[_spec((SBQ, wide), lambda g, i: (i, g)),
                  _spec((t, wide), lambda g, i: (0, ngrp + g)),
                  _spec((t, wide), lambda g, i: (0, 2 * ngrp + g))] + (carry.in_specs if nc else []),
        out_specs=[_spec((SBQ, wide), lambda g, i: (i, g)), _spec((npair, SBQ, LANE), lambda g, i: (g, i, 0)),
                   pl.BlockSpec(memory_space=pltpu.SMEM)] + (carry.out_specs if nc else []),
        out_shape=[jax.ShapeDtypeStruct((t, 512), BF16), jax.ShapeDtypeStruct((4, t, LANE), F32),
                   jax.ShapeDtypeStruct((ngrp * nq,), F32)] + (carry.out_shape if nc else []),
        scratch_shapes=carry.scratch if nc else [],
        compiler_params=_params(("arbitrary", "arbitrary")),
    )(proj_b, proj_b, proj_b, *(carry.groups if nc else []))
    return outs[0], outs[1], outs[2], (carry.place(outs[3:]) if nc else [])


def _sb_bwd(proj_b, dy, ltot, nblk, rider=None):
    t = proj_b.shape[0]
    nq = t // SBQ
    scale = 1.0 / math.sqrt(64.0)

    npair = SB_PAIRS_BWD
    wide = npair * LANE
    ngrp = 4 // npair
    chains = [(p, head) for p in range(npair) for head in range(2)]
    grid = (ngrp, nq)
    nr = rider.n if rider is not None else 0
    per_count = SB_PAIRS_FWD // SB_PAIRS_BWD

    def body(*refs):
        q_ref, k_ref, v_ref, dy_ref, l_ref, n_ref = refs[:6]
        dq_ref, dk_ref, dv_ref = refs[6 + nr:9 + nr]
        dk_acc, dv_acc = refs[9 + 2 * nr:11 + 2 * nr]
        riders = (refs[6:6 + nr], refs[9 + nr:9 + 2 * nr], refs[-2], refs[-1]) if nr else None
        if nr:
            rider.ride(grid, riders, "start")
        grp = pl.program_id(0)
        qi = pl.program_id(1)

        @pl.when(qi == 0)
        def _():
            dk_acc[...] = jnp.zeros_like(dk_acc)
            dv_acc[...] = jnp.zeros_like(dv_acc)

        row, col = _sb_masks()
        m_prefix = _ones_where(row <= col)
        m_before = _ones_where(row < col)
        rows = len(chains) * SBQ
        strict = (lax.broadcasted_iota(jnp.int32, (rows, SBQ), 1)
                  < (lax.broadcasted_iota(jnp.int32, (rows, SBQ), 0) & (SBQ - 1)))
        q_pair, do_pair, ltot = [], [], []
        for p in range(npair):
            pl_ = slice(p * LANE, (p + 1) * LANE)
            q_all = q_ref[:, pl_]
            do_all = dy_ref[:, pl_].astype(BF16)
            q_pair.append(jnp.concatenate([jnp.where(_head_mask(h), q_all, jnp.zeros_like(q_all)) for h in range(2)], axis=0))
            do_pair.append(jnp.concatenate([jnp.where(_head_mask(h), do_all, jnp.zeros_like(do_all)) for h in range(2)], axis=0))
            ltot += [l_ref[p][:, h * 64:h * 64 + 1] for h in range(2)]
        ltot = jnp.concatenate(ltot, axis=0)

        def pair_rows(a, p):
            return a[2 * p * SBQ:2 * (p + 1) * SBQ]

        def block(kb, state, diag):
            seen, dseen, dq = state
            start = pl.multiple_of(kb * SBQ, SBQ)
            kk = [k_ref[pl.ds(start, SBQ), p * LANE:(p + 1) * LANE] for p in range(npair)]
            vv = [v_ref[pl.ds(start, SBQ), p * LANE:(p + 1) * LANE] for p in range(npair)]
            z = jnp.concatenate([_dg(q_pair[p], kk[p], NT) for p in range(npair)], axis=0) * scale
            lb = _log_sigmoid(z)
            lk = lb - z
            if diag:
                lk = jnp.where(strict, lk, 0.0)
            hi, lo = _split(lk)
            sums = _dg(jnp.concatenate([hi, lo], axis=0), m_prefix, NN)
            w = jnp.exp(lb + ((ltot - seen) - (sums[:rows] + sums[rows:])))
            if diag:
                w = jnp.where(strict, w, 0.0)
            wb = w.astype(BF16)
            da = w * jnp.concatenate([_dg(do_pair[p], vv[p], NT) for p in range(npair)], axis=0)
            dah, dal = _split(da)
            dsums = _dg(jnp.concatenate([dah, dal], axis=0), m_before, NN)
            sig = jnp.exp(lb)
            dz = (da * (1.0 - sig) - (dseen + dsums[:rows] + dsums[rows:]) * sig) * scale
            if diag:
                dz = jnp.where(strict, dz, 0.0)
            dzb = dz.astype(BF16)
            for p in range(npair):
                pl_ = slice(p * LANE, (p + 1) * LANE)
                dv_acc[pl.ds(start, SBQ), pl_] += _dg(pair_rows(wb, p), do_pair[p], TN)
                dk_acc[pl.ds(start, SBQ), pl_] += _dg(pair_rows(dzb, p), q_pair[p], TN)
            dq = dq + jnp.concatenate([_dg(pair_rows(dzb, p), kk[p], NN) for p in range(npair)], axis=0)
            seen = seen + jnp.sum(hi.astype(F32) + lo.astype(F32), axis=1, keepdims=True)
            dseen = dseen + jnp.sum(da, axis=1, keepdims=True)
            return seen, dseen, dq

        zero = (jnp.zeros((rows, 1), F32), jnp.zeros((rows, 1), F32), jnp.zeros((rows, LANE), F32))
        first = qi - n_ref[(grp // per_count) * nq + qi].astype(jnp.int32)
        state = lax.fori_loop(first, qi, lambda kb, c: block(kb, c, False), zero)
        _, _, dq = block(qi, state, True)
        for p in range(npair):
            dq_ref[:, p * LANE:(p + 1) * LANE] = jnp.where(
                _head_mask(0), dq[2 * p * SBQ:(2 * p + 1) * SBQ], dq[(2 * p + 1) * SBQ:(2 * p + 2) * SBQ]).astype(BF16)

        @pl.when(qi == nq - 1)
        def _():
            dk_ref[...] = dk_acc[...].astype(BF16)
            dv_ref[...] = dv_acc[...].astype(BF16)

        if nr:
            rider.ride(grid, riders, "finish")

    full = jax.ShapeDtypeStruct((t, 512), BF16)
    outs = pl.pallas_call(
        body, name="sb_bwd", grid=grid,
        in_specs=[_spec((SBQ, wide), lambda g, i: (i, g)),
                  _spec((t, wide), lambda g, i: (0, ngrp + g)),
                  _spec((t, wide), lambda g, i: (0, 2 * ngrp + g)),
                  _spec((SBQ, wide), lambda g, i: (i, ngrp + g)),
                  _spec((npair, SBQ, LANE), lambda g, i: (g, i, 0)),
                  pl.BlockSpec(memory_space=pltpu.SMEM)] + (rider.in_specs if nr else []),
        out_specs=[_spec((SBQ, wide), lambda g, i: (i, g)),
                   _spec((t, wide), lambda g, i: (0, g)), _spec((t, wide), lambda g, i: (0, g))]
        + (rider.out_specs if nr else []),
        out_shape=[full, full, full] + (rider.out_shape if nr else []),
        scratch_shapes=[pltpu.VMEM((t, wide), F32), pltpu.VMEM((t, wide), F32)] + (rider.scratch if nr else []),
        compiler_params=pltpu.CompilerParams(dimension_semantics=("arbitrary", "arbitrary"),
                                             vmem_limit_bytes=VMEM_LIMIT_BIG),
    )(proj_b, proj_b, proj_b, dy, ltot, nblk, *(rider.groups if nr else []))
    return outs[0], outs[1], outs[2], (rider.place(outs[3:]) if nr else [])


def _hgrn_gates(qr, fr, c0, c1):
    mx = jnp.maximum(c0, c1)
    e0, e1 = jnp.exp(c0 - mx), jnp.exp(c1 - mx)
    lb = e1 / (e0 + e1)
    sx = _sigmoid(fr)
    f = lb + (1.0 - lb) * sx
    k = (1.0 - lb) * (1.0 - sx)
    sq = _sigmoid(qr)
    return lb, sx, f, k, sq, qr * sq


def _chunk_sums(mask, x):
    n = x.shape[1]
    hi, lo = _split(x)
    both = _dg(_ones_where(mask), jnp.concatenate([hi, lo], axis=1), NN)
    return both[:, :n] + both[:, n:]


def _chunk_masks():
    row = lax.broadcasted_iota(jnp.int32, (CHUNK, CHUNK), 0)
    col = lax.broadcasted_iota(jnp.int32, (CHUNK, CHUNK), 1)
    return col <= row, col >= row


def _hgrn_fwd(proj_c, small, carry=None):
    t = proj_c.shape[1]
    nc = t // CHUNK
    nh = D // HD

    hps = HGRN_HPS
    wide = hps * HD
    grid = (nh // hps, nc)
    nr = carry.n if carry is not None else 0

    def body(*refs):
        p_ref, c0_ref, c1_ref, gam_ref = refs[:4]
        o_ref, y_ref, sst_ref = refs[4 + nr:7 + nr]
        st_ref = refs[7 + 2 * nr]
        riders = (refs[4:4 + nr], refs[7 + nr:7 + 2 * nr], refs[-2], refs[-1]) if nr else None
        if nr:
            carry.ride(grid, riders, "start")
        c = pl.program_id(1)

        @pl.when(c == 0)
        def _():
            st_ref[...] = jnp.zeros_like(st_ref)

        _, _, f_all, k_all, _, q_all = _hgrn_gates(p_ref[0], p_ref[1], c0_ref[...], c1_ref[...])
        tril, _ = _chunk_masks()
        b_all = _chunk_sums(tril, jnp.log(f_all))
        for j in range(hps):
            ln = slice(j * HD, (j + 1) * HD)
            st0 = st_ref[j]
            sst_ref[j] = st0
            v, g = p_ref[2, :, ln], p_ref[3, :, ln]
            q, k, b = q_all[:, ln], k_all[:, ln], b_all[:, ln]
            bm = b[CHUNK // 2 - 1:CHUNK // 2]
            bl = b[CHUNK - 1:CHUNK]
            qd = q * jnp.exp(b)
            qt = q * jnp.exp(b - bm)
            kt = k * jnp.exp(bm - b)
            kl = k * jnp.exp(bl - b)
            vb = v.astype(BF16)
            att = jnp.where(tril, _dot3(qt, kt, NT), 0.0)
            o = _dg(qd.astype(BF16), st0.astype(BF16), NT) + _dg(att.astype(BF16), vb, NN)
            st_ref[j] = st0 * jnp.exp(bl) + _dg(vb, kl.astype(BF16), TN)
            o_ref[:, ln] = o
            r = lax.rsqrt(jnp.mean(o * o, axis=-1, keepdims=True) + RMS_EPS)
            y_ref[:, ln] = (o * r * gam_ref[...] * (g * _sigmoid(g))).astype(BF16)
        if nr:
            carry.ride(grid, riders, "finish")

    outs = pl.pallas_call(
        body, name="hgrn_fwd", grid=grid,
        in_specs=[_spec((4, CHUNK, wide), lambda h, c: (0, c, h)),
                  _spec((None, 1, wide), lambda h, c: (R_CLB, 0, h)),
                  _spec((None, 1, wide), lambda h, c: (R_CLB + 1, 0, h)),
                  _spec((None, 1, HD), lambda h, c: (R_GAM, 0, 0))] + (carry.in_specs if nr else []),
        out_specs=[_spec((CHUNK, wide), lambda h, c: (c, h)), _spec((CHUNK, wide), lambda h, c: (c, h)),
                   _spec((None, hps, HD, HD), lambda h, c: (c, h, 0, 0))] + (carry.out_specs if nr else []),
        out_shape=[jax.ShapeDtypeStruct((t, D), F32), jax.ShapeDtypeStruct((t, D), BF16),
                   jax.ShapeDtypeStruct((nc, nh, HD, HD), F32)] + (carry.out_shape if nr else []),
        scratch_shapes=[pltpu.VMEM((hps, HD, HD), F32)] + (carry.scratch if nr else []),
        compiler_params=_params(("arbitrary", "arbitrary")),
    )(proj_c, small, small, small, *(carry.groups if nr else []))
    return outs[0], outs[1], outs[2], (carry.place(outs[3:]) if nr else [])


def _hgrn_bwd(proj_c, small, o, sst, dyc, rider=None):
    t = proj_c.shape[1]
    nc = t // CHUNK
    nh = D // HD

    hps = HGRN_HPS
    wide = hps * HD
    ng = nh // hps
    grid = (ng, nc)
    nr = rider.n if rider is not None else 0

    def body(*refs):
        p_ref, c0_ref, c1_ref, gam_ref, o_ref, sst_ref, dy_ref = refs[:7]
        dp_ref, dclb_ref, dgam_ref = refs[7 + nr:10 + nr]
        dst_ref, dlb_acc, dgam_acc = refs[10 + 2 * nr:13 + 2 * nr]
        riders = (refs[7:7 + nr], refs[10 + nr:10 + 2 * nr], refs[-2], refs[-1]) if nr else None
        if nr:
            rider.ride(grid, riders, "start")
        group = pl.program_id(0)
        step = pl.program_id(1)

        @pl.when(step == 0)
        def _():
            dst_ref[...] = jnp.zeros_like(dst_ref)
            dlb_acc[...] = jnp.zeros_like(dlb_acc)

        @pl.when((step == 0) & (group == 0))
        def _():
            dgam_acc[...] = jnp.zeros_like(dgam_acc)

        gam = gam_ref[...]
        qr_all = p_ref[0]
        lb_all, sx_all, f_all, k_all, sq_all, q_all = _hgrn_gates(qr_all, p_ref[1], c0_ref[...], c1_ref[...])
        tril, triu = _chunk_masks()
        b_all = _chunk_sums(tril, jnp.log(f_all))
        dq_parts, dk_parts, db_parts, dgam_parts = [], [], [], []
        for j in range(hps):
            ln = slice(j * HD, (j + 1) * HD)
            st0 = sst_ref[j]
            dst1 = dst_ref[j]
            v, g = p_ref[2, :, ln], p_ref[3, :, ln]
            q, k, b = q_all[:, ln], k_all[:, ln], b_all[:, ln]
            bm = b[CHUNK // 2 - 1:CHUNK // 2]
            bl = b[CHUNK - 1:CHUNK]
            eb = jnp.exp(b)
            e_qt = jnp.exp(b - bm)
            e_kt = jnp.exp(bm - b)
            e_kl = jnp.exp(bl - b)
            e_bl = jnp.exp(bl)
            qd, qt, kt, kl = q * eb, q * e_qt, k * e_kt, k * e_kl
            ov = o_ref[:, ln]
            r = lax.rsqrt(jnp.mean(ov * ov, axis=-1, keepdims=True) + RMS_EPS)
            oh = ov * r
            sg = _sigmoid(g)
            dy = dy_ref[:, ln]
            dp_ref[3, :, ln] = (dy * oh * gam * (sg * (1.0 + g * (1.0 - sg)))).astype(BF16)
            dyv = dy * (g * sg)
            gdy = dyv * gam
            do = (gdy - oh * jnp.mean(gdy * oh, axis=-1, keepdims=True)) * r
            dob, vb = do.astype(BF16), v.astype(BF16)
            st0b, dst1b = st0.astype(BF16), dst1.astype(BF16)
            st1 = st0 * e_bl + _dg(vb, kl.astype(BF16), TN)
            att = jnp.where(tril, _dot3(qt, kt, NT), 0.0)
            datt = jnp.where(tril, _dg(dob, vb, NT), 0.0)
            dv = _dg(att.astype(BF16), dob, TN) + _dg(kl.astype(BF16), dst1b, NT)
            dq = _dot3(datt, kt, NN) * e_qt + _dg(dob, st0b, NN) * eb
            dk = _dot3(datt, qt, TN) * e_kt + _dg(vb, dst1b, NN) * e_kl
            db = q * dq - k * dk
            last = lax.broadcasted_iota(jnp.int32, (CHUNK, 1), 0) == CHUNK - 1
            db = db + jnp.where(last, jnp.sum(dst1 * st1, axis=0, keepdims=True), 0.0)
            dst_ref[j] = dst1 * e_bl + _dg(dob, qd.astype(BF16), TN)
            dp_ref[2, :, ln] = dv.astype(BF16)
            dq_parts.append(dq)
            dk_parts.append(dk)
            db_parts.append(db)
            dgam_parts.append(jnp.sum(dyv * oh, axis=0, keepdims=True))

        dq_all, dk_all = jnp.concatenate(dq_parts, axis=1), jnp.concatenate(dk_parts, axis=1)
        dlf = _chunk_sums(triu, jnp.concatenate(db_parts, axis=1))
        dp_ref[0] = (dq_all * (sq_all * (1.0 + qr_all * (1.0 - sq_all)))).astype(BF16)
        tmp = dlf / f_all - dk_all
        dp_ref[1] = (tmp * (1.0 - lb_all) * sx_all * (1.0 - sx_all)).astype(BF16)
        dlb_acc[...] += jnp.sum((1.0 - sx_all) * tmp, axis=0, keepdims=True)
        dgam_acc[...] += functools.reduce(lambda a, b: a + b, dgam_parts)

        @pl.when(step == nc - 1)
        def _():
            d1 = dlb_acc[...] * lb_all * (1.0 - lb_all)
            dclb_ref[...] = jnp.where(lax.broadcasted_iota(jnp.int32, (2, wide), 0) == 0, -d1, d1)

        @pl.when((step == nc - 1) & (group == ng - 1))
        def _():
            dgam_ref[...] = dgam_acc[...]

        if nr:
            rider.ride(grid, riders, "finish")

    rev = lambda h, s: (nc - 1 - s, h)
    outs = pl.pallas_call(
        body, name="hgrn_bwd", grid=grid,
        in_specs=[_spec((4, CHUNK, wide), lambda h, s: (0, nc - 1 - s, h)),
                  _spec((None, 1, wide), lambda h, s: (R_CLB, 0, h)),
                  _spec((None, 1, wide), lambda h, s: (R_CLB + 1, 0, h)),
                  _spec((None, 1, HD), lambda h, s: (R_GAM, 0, 0)),
                  _spec((CHUNK, wide), rev),
                  _spec((None, hps, HD, HD), lambda h, s: (nc - 1 - s, h, 0, 0)),
                  _spec((CHUNK, wide), rev)] + (rider.in_specs if nr else []),
        out_specs=[_spec((4, CHUNK, wide), lambda h, s: (0, nc - 1 - s, h)),
                   _spec((2, wide), lambda h, s: (0, h)),
                   _spec((1, HD), lambda h, s: (0, 0))] + (rider.out_specs if nr else []),
        out_shape=[jax.ShapeDtypeStruct((4, t, D), BF16), jax.ShapeDtypeStruct((2, D), F32),
                   jax.ShapeDtypeStruct((1, HD), F32)] + (rider.out_shape if nr else []),
        scratch_shapes=[pltpu.VMEM((hps, HD, HD), F32), pltpu.VMEM((1, wide), F32), pltpu.VMEM((1, HD), F32)]
        + (rider.scratch if nr else []),
        compiler_params=_params(("arbitrary", "arbitrary")),
    )(proj_c, small, small, small, o, sst, dyc, *(rider.groups if nr else []))
    return outs[0], outs[1], outs[2], (rider.place(outs[3:]) if nr else [])


def _adamw(name, w, grads, m, v):
    nl = len(grads)
    rows, cols = grads[0].shape
    br = rows
    for cand in (512, 352, 256):
        if rows % cand == 0:
            br = cand
            break
    nb = rows // br
    c1 = 1.0 - ADAM_B1 ** ADAM_STEP
    c2 = 1.0 - ADAM_B2 ** ADAM_STEP

    def body(w_ref, m_ref, v_ref, *refs):
        g_refs, (d_ref, mo_ref, vo_ref, go_ref) = refs[:nl], refs[nl:]
        layer = pl.program_id(0)
        gv = g_refs[0][...]
        for k in range(1, nl):
            gv = jnp.where(layer == k, g_refs[k][...], gv)
        mn = ADAM_B1 * m_ref[...] + (1.0 - ADAM_B1) * gv
        vn = ADAM_B2 * v_ref[...] + (1.0 - ADAM_B2) * (gv * gv)
        mo_ref[...] = mn
        vo_ref[...] = vn
        go_ref[...] = gv
        d_ref[...] = -ADAM_LR * ((mn / c1) / (jnp.sqrt(vn / c2) + ADAM_EPS) + ADAM_WD * w_ref[...])

    blk = _spec((br, cols), lambda l, i: (l * nb + i, 0))
    g_specs = [_spec((br, cols), lambda l, i, k=k: (jnp.where(l == k, i, 0), 0)) for k in range(nl)]
    shape = jax.ShapeDtypeStruct((nl * rows, cols), F32)
    return pl.pallas_call(
        body, name=name, grid=(nl, nb), in_specs=[blk] * 3 + g_specs, out_specs=[blk] * 4, out_shape=[shape] * 4,
        compiler_params=_params(("arbitrary", "arbitrary")),
    )(w, m, v, *grads)


def _place():
    x, y, c = lax.axis_index("x"), lax.axis_index("y"), lax.axis_index("c")
    chips = [(1 - x, y), (x, 1 - y), (1 - x, 1 - y)]
    return x, y, c, chips


class _Rider:
    n = 0
    relay = None

    def ride(self, grid, refs, when):
        if not self.n:
            return
        ids = [pl.program_id(a) for a in range(len(grid))]
        first = functools.reduce(jnp.logical_and, [i == 0 for i in ids])
        last = functools.reduce(jnp.logical_and, [i == g - 1 for i, g in zip(ids, grid)])
        phases = [(first, self.start), (last, self.relay)] if when == "start" else [(last, self.finish)]
        for cond, phase in phases:
            if phase is not None:
                pl.when(cond)(functools.partial(phase, *refs))


class _Gather(_Rider):
    PER_GROUP = 7

    def __init__(self, groups):
        self.groups = list(groups)
        self.n = len(self.groups)
        self.in_specs = [ANY] * self.n
        self.out_specs = [ANY] * self.n
        self.out_shape = [jax.ShapeDtypeStruct((NSH,) + g.shape, g.dtype) for g in self.groups]
        sems = pltpu.SemaphoreType.DMA((self.PER_GROUP * self.n,))
        self.scratch = [sems, sems] if self.n else []

    def _copies(self, ins, outs, send, recv):
        x, y, c, chips = _place()
        sibling = (x, y, 1 - c)

        def half(gi, chip, hc):
            return outs[gi].at[2 * chip[0] + chip[1], hc]

        def copy(gi, k, src, dst, to):
            sem = self.PER_GROUP * gi + k
            return pltpu.make_async_remote_copy(src_ref=src, dst_ref=dst, send_sem=send.at[sem], recv_sem=recv.at[sem],
                                                device_id=to, device_id_type=MESH)

        pairs = [(gi, j, chip) for gi in range(self.n) for j, chip in enumerate(chips)]
        first = [copy(gi, j, ins[gi].at[c], half(gi, (x, y), c), (*chip, c)) for gi, j, chip in pairs]
        first += [copy(gi, 6, ins[gi], outs[gi].at[2 * x + y], sibling) for gi in range(self.n)]
        landed = [copy(gi, j, half(gi, chip, c), half(gi, chip, c), sibling) for gi, j, chip in pairs]
        relay = [copy(gi, 3 + j, half(gi, chip, c), half(gi, chip, c), sibling) for gi, j, chip in pairs]
        relayed = [copy(gi, 3 + j, half(gi, chip, 1 - c), half(gi, chip, 1 - c), sibling) for gi, j, chip in pairs]
        relayed += [copy(gi, 6, ins[gi], outs[gi].at[2 * x + y], sibling) for gi in range(self.n)]
        return first, landed, relay, relayed

    def start(self, ins, outs, send, recv):
        for cp in self._copies(ins, outs, send, recv)[0]:
            cp.start()

    def relay(self, ins, outs, send, recv):
        _, landed, relay, _ = self._copies(ins, outs, send, recv)
        for arrived, onward in zip(landed, relay):
            arrived.wait_recv()
            onward.start()

    def finish(self, ins, outs, send, recv):
        first, _, relay, relayed = self._copies(ins, outs, send, recv)
        for cp in relayed:
            cp.wait_recv()
        for cp in first + relay:
            cp.wait_send()

    def place(self, outs):
        return list(outs)


def _alone(name, rider):
    n = rider.n

    def body(*refs):
        parts = (refs[:n], refs[n:2 * n], refs[2 * n], refs[2 * n + 1])
        rider.start(*parts)
        if rider.relay is not None:
            rider.relay(*parts)
        rider.finish(*parts)

    outs = pl.pallas_call(
        body, name=name, in_specs=rider.in_specs, out_specs=rider.out_specs, out_shape=rider.out_shape,
        scratch_shapes=rider.scratch, compiler_params=pltpu.CompilerParams(has_side_effects=True),
    )(*rider.groups)
    return rider.place(outs)


class _Swap(_Rider):
    def __init__(self, slots):
        self.slots = list(slots)
        self.groups = [buf for buf, _, _ in self.slots]
        self.n = len(self.slots)
        self.in_specs = [ANY] * self.n
        self.out_specs = [ANY] * self.n
        self.out_shape = [jax.ShapeDtypeStruct((NSH, rows // 2, buf.shape[2]), buf.dtype) for buf, _, rows in self.slots]
        self.scratch = [pltpu.SemaphoreType.DMA((self.n,)), pltpu.SemaphoreType.DMA((self.n,))]

    def _copies(self, ins, outs, send, recv):
        x, y, c, _ = _place()
        cps = []
        for i, (_, row0, rows) in enumerate(self.slots):
            half = rows // 2
            src = ins[i].at[:, pl.ds(pl.multiple_of(row0 + (1 - c) * half, 16), half)]
            cps.append(pltpu.make_async_remote_copy(src_ref=src, dst_ref=outs[i], send_sem=send.at[i],
                                                    recv_sem=recv.at[i], device_id=(x, y, 1 - c), device_id_type=MESH))
        return cps

    def start(self, ins, outs, send, recv):
        for cp in self._copies(ins, outs, send, recv):
            cp.start()

    def finish(self, ins, outs, send, recv):
        for cp in self._copies(ins, outs, send, recv):
            cp.wait()

    def place(self, outs):
        return list(outs)


class _Share(_Rider):
    def __init__(self, arrays):
        self.groups = list(arrays)
        self.n = len(self.groups)
        self.in_specs = [ANY] * self.n
        self.out_specs = [ANY] * self.n
        self.out_shape = [jax.ShapeDtypeStruct((2,) + g.shape, g.dtype) for g in self.groups]
        self.scratch = [pltpu.SemaphoreType.DMA((self.n,)), pltpu.SemaphoreType.DMA((self.n,))]

    def _copies(self, ins, outs, send, recv, half):
        x, y, c, _ = _place()
        return [pltpu.make_async_remote_copy(src_ref=ins[gi], dst_ref=outs[gi].at[c if half == "mine" else 1 - c],
                                             send_sem=send.at[gi], recv_sem=recv.at[gi], device_id=(x, y, 1 - c),
                                             device_id_type=MESH) for gi in range(self.n)]

    def start(self, ins, outs, send, recv):
        for cp in self._copies(ins, outs, send, recv, "mine"):
            cp.start()

    def finish(self, ins, outs, send, recv):
        for cp in self._copies(ins, outs, send, recv, "theirs"):
            cp.wait_recv()
        for cp in self._copies(ins, outs, send, recv, "mine"):
            cp.wait_send()

    def place(self, outs):
        c = lax.axis_index("c")
        return [lax.dynamic_update_index_in_dim(o, g, c, 0) for o, g in zip(outs, self.groups)]


class _Send(_Rider):
    def __init__(self, arrays):
        self.groups = list(arrays)
        self.n = len(self.groups)
        self.in_specs = [ANY] * self.n
        self.out_specs = [ANY] * self.n
        self.out_shape = [jax.ShapeDtypeStruct((3,) + g.shape[1:], g.dtype) for g in self.groups]
        self.scratch = [pltpu.SemaphoreType.DMA((3 * self.n,)), pltpu.SemaphoreType.DMA((3 * self.n,))]

    def _copies(self, ins, outs, send, recv):
        x, y, c, chips = _place()
        return [pltpu.make_async_remote_copy(src_ref=ins[gi].at[2 * chip[0] + chip[1]], dst_ref=outs[gi].at[j],
                                             send_sem=send.at[3 * gi + j], recv_sem=recv.at[3 * gi + j],
                                             device_id=(*chip, c), device_id_type=MESH)
                for gi in range(self.n) for j, chip in enumerate(chips)]

    def start(self, ins, outs, send, recv):
        for cp in self._copies(ins, outs, send, recv):
            cp.start()

    def finish(self, ins, outs, send, recv):
        for cp in self._copies(ins, outs, send, recv):
            cp.wait()

    def place(self, outs):
        return list(outs)


def _pair_sum(name, slots, got, c_idx):
    n = len(slots)
    in_specs, out_specs, out_shape, operands = [], [], [], []
    for (buf, row0, rows), g in zip(slots, got):
        hb, cols = rows // 4, buf.shape[2]
        in_specs += [pl.BlockSpec((None, hb, cols), lambda q, i, cr, r=row0 // hb: (q, r + 2 * cr[0] + i, 0)),
                     pl.BlockSpec((None, hb, cols), lambda q, i, cr: (q, i, 0))]
        out_specs.append(pl.BlockSpec((None, hb, cols), lambda q, i, cr: (q, i, 0)))
        out_shape.append(jax.ShapeDtypeStruct(g.shape, BF16))
        operands += [buf, g]

    def body(c_ref, *refs):
        for i in range(n):
            refs[2 * n + i][...] = (refs[2 * i][...].astype(F32) + refs[2 * i + 1][...].astype(F32)).astype(BF16)

    return pl.pallas_call(
        body, name=name,
        grid_spec=pltpu.PrefetchScalarGridSpec(num_scalar_prefetch=1, grid=(NSH, 2), in_specs=in_specs, out_specs=out_specs),
        out_shape=out_shape, compiler_params=_params(("parallel", "parallel")),
    )(c_idx, *operands)


def _owner_sum(name, pairs, got, p_idx):
    n = len(pairs)
    in_specs, out_specs, out_shape, operands = [], [], [], []
    for own, g in zip(pairs, got):
        _, rows, cols = own.shape
        hb = rows // 2
        in_specs += [pl.BlockSpec((None, hb, cols), lambda i, pr: (pr[0], i, 0)),
                     pl.BlockSpec((3, hb, cols), lambda i, pr: (0, i, 0))]
        out_specs.append(pl.BlockSpec((hb, cols), lambda i, pr: (i, 0)))
        out_shape.append(jax.ShapeDtypeStruct((rows, cols), F32))
        operands += [own, g]

    def body(p_ref, *refs):
        for i in range(n):
            a_ref, b_ref = refs[2 * i], refs[2 * i + 1]
            refs[2 * n + i][...] = ((a_ref[...].astype(F32) + b_ref[0].astype(F32)) + b_ref[1].astype(F32)) + b_ref[2].astype(F32)

    return pl.pallas_call(
        body, name=name,
        grid_spec=pltpu.PrefetchScalarGridSpec(num_scalar_prefetch=1, grid=(2,), in_specs=in_specs, out_specs=out_specs),
        out_shape=out_shape, compiler_params=_params(("parallel",)),
    )(p_idx, *operands)


def _sum_small(slab):
    def body(in_ref, out_ref, all_ref, send, recv):
        x, y, c, _ = _place()
        me = 4 * x + 2 * y + c
        all_ref[me] = in_ref[...]
        cps = []
        for k in range(1, 8):
            peer = (x ^ (k >> 2), y ^ ((k >> 1) & 1), c ^ (k & 1))
            cps.append(pltpu.make_async_remote_copy(src_ref=in_ref, dst_ref=all_ref.at[me], send_sem=send.at[k - 1],
                                                    recv_sem=recv.at[k - 1], device_id=peer, device_id_type=MESH))
        for cp in cps:
            cp.start()
        for cp in cps:
            cp.wait()
        total = all_ref[0]
        for d in range(1, 8):
            total = total + all_ref[d]
        out_ref[...] = total

    return pl.pallas_call(
        body, name="sum_small",
        in_specs=[pl.BlockSpec(memory_space=pltpu.VMEM)], out_specs=pl.BlockSpec(memory_space=pltpu.VMEM),
        out_shape=jax.ShapeDtypeStruct(slab.shape, F32),
        scratch_shapes=[pltpu.VMEM((8,) + slab.shape, F32), pltpu.SemaphoreType.DMA((7,)), pltpu.SemaphoreType.DMA((7,))],
        compiler_params=pltpu.CompilerParams(has_side_effects=True),
    )(slab)


FFNS = ("pre0", "post0", "pre1", "post1")
W_SHAPES = dict({f + "_gu": (NSH, 2, FS, D) for f in FFNS}, **{f + "_d": (NSH, FS, D) for f in FFNS},
                ab_in=(NSH, D, 768), ab_out=(NSH, 256, D), conv=(NSH, 2, 8, LANE), c_in=(NSH, D, D), c_out=(NSH, 256, D))
CARRIED = dict(pre0_up=("pre0_d",), pre0_down=("ab_in", "ab_out", "conv"),
               sb_fwd=("post0_gu", "post0_d"), post0_up=("pre1_gu",), post0_down=("pre1_d",),
               pre1_up=("c_in", "c_out"), hgrn_fwd=("post1_gu", "post1_d"))


FFN_SLOTS = dict(pre0=(0, 2, 0), pre1=(1, 3, 1), post0=(4, 6, 2), post1=(5, 7, 3))
GRAD_SLOTS = dict(ab_out=("b", B_ABOUT, 256), c_in=("b", B_CIN, D), c_out=("b", B_COUT, 256), ab_in=("c", 0, D))
for _f, (_g, _u, _d) in FFN_SLOTS.items():
    GRAD_SLOTS.update({_f + "_g": ("a", _g * FS, FS), _f + "_u": ("a", _u * FS, FS), _f + "_d": ("b", _d * FS, FS)})
REDUCE_STAGES = dict(x=("post1_g", "post1_u", "post1_d", "c_out"), w=("c_in", "pre1_d"),
                     y=("pre1_g", "pre1_u", "post0_g", "post0_u", "post0_d", "ab_out"),
                     z1=("ab_in",), z2=("pre0_g", "pre0_u", "pre0_d"))


def _local_step(x, target, weights, small, shards=None, place=None):
    t = x.shape[0]
    tm = min(MM_TILE, t)
    tw = min(WGRAD_TILE, t)
    nt = t // tm
    tok_si = _spec((tm, D), lambda s, i: (i, 0))
    w = dict(weights)
    reduced = {}

    def stage_slots(stage, buffers):
        return [(buffers[GRAD_SLOTS[n][0]],) + GRAD_SLOTS[n][1:] for n in REDUCE_STAGES[stage]]

    def swap_rider(stage, buffers):
        return _Swap(stage_slots(stage, buffers)) if place is not None else None

    def reduce_start(stage, buffers, swapped=None):
        if place is None:
            return [], None
        slots = stage_slots(stage, buffers)
        if swapped is None:
            swapped = _alone("grad_swap_" + stage, _Swap(slots))
        pairs = _pair_sum("grad_pair_sum_" + stage, slots, swapped, place[0])
        return pairs, _Send(pairs)

    def reduce_end(stage, pairs, landed):
        if place is None:
            return None
        mine = _owner_sum("grad_owner_sum_" + stage, pairs, landed, place[1])
        return _Share(mine)

    def shared(stage, landed):
        if place is not None:
            reduced.update(zip(REDUCE_STAGES[stage], landed))

    def carried(kernel_name):
        names = [n for n in CARRIED[kernel_name] if n not in w]
        return names, (_Gather([shards[n] for n in names]) if names else None)

    def land(names, arrays):
        for n, a in zip(names, arrays):
            w[n] = a.reshape(W_SHAPES[n])

    def ffn_forward(tag, h, hn, next_row):
        names, gather = carried(tag + "_up") if tag + "_up" in CARRIED else ([], None)
        s_up, s_gate, a, got = _ffn_up(tag + "_up", hn, w[tag + "_gu"], gather)
        land(names, got)
        names, gather = carried(tag + "_down") if tag + "_down" in CARRIED else ([], None)
        out = _ffn_down(tag + "_down", a, w[tag + "_d"], h, gather, (small, next_row) if next_row is not None else None)
        if gather is not None:
            out, got = out
            land(names, got)
        out, hn_next = out if next_row is not None else (out, None)
        return out, hn_next, (h, hn, s_up, s_gate, a)

    def out_proj(name, y, w_out, h, next_row):
        return _mm(name, [(y, _spec((tm, 256), lambda i, k: (i, k)), w_out, _spec((None, 256, D), lambda i, k: (k, 0, 0)))],
                   grid=(nt, NSH), o_shape=(t, D), o_dtype=F32, o_spec=_spec((tm, D), lambda i, k: (i, 0)),
                   dims=NN, kaxis=1, nk=NSH, acc_shape=(tm, D), res=(h, _spec((tm, D), lambda i, k: (i, 0))),
                   norm=(small, next_row))

    def out_proj_bwd(tag, dhb, y, w_out, blk, grad_b, make_rider=None):
        grad_b = _wgrad(tag + "_dwout", y, _spec((tw, 256), lambda s, k: (k, s)), dhb, _spec((tw, D), lambda s, k: (k, 0)),
                        256, D, t, tw, grad_b, blk)
        rider = make_rider(grad_b) if make_rider is not None else None
        dy = _mm(tag + "_dy", [(dhb, tok_si, w_out, _spec((None, 256, D), lambda s, i: (s, 0, 0)))],
                 grid=(NSH, nt), o_shape=(t, D), o_dtype=F32, o_spec=_spec((tm, 256), lambda s, i: (i, s)),
                 dims=NT, kaxis=1, nk=1, carry=rider)
        dy, landed = dy if rider is not None else (dy, None)
        return dy, grad_b, landed

    h0 = x
    h1, hn_ab, pre0 = ffn_forward("pre0", h0, _norm_fwd("pre0_norm", h0, small, R_PRE), R_MIX)
    proj_a = _mm("ab_proj_a", [(hn_ab, tok_si, w["ab_in"], _spec((None, D, 768), lambda s, i: (s, 0, 0)))],
                 grid=(2, nt), o_shape=(t, 1536), o_dtype=F32, o_spec=_spec((tm, 768), lambda s, i: (i, s)),
                 dims=NN, kaxis=1, nk=1)
    proj_b = _mm("ab_proj_b", [(hn_ab, tok_si, w["ab_in"], _spec((None, D, 768), lambda s, i: (s + 2, 0, 0)))],
                 grid=(2, nt), o_shape=(t, 1536), o_dtype=BF16, o_spec=_spec((tm, 768), lambda s, i: (i, s)),
                 dims=NN, kaxis=1, nk=1)
    y_a = _conv_fwd(proj_a, w["conv"])
    names, gather = carried("sb_fwd")
    y_b, ltot, nblk, got = _sb_fwd(proj_b, gather)
    land(names, got)
    y_ab = jnp.concatenate([y_a, y_b], axis=1)
    h2, hn = out_proj("ab_out", y_ab, w["ab_out"], h1, R_POST)
    h3, hn, post0 = ffn_forward("post0", h2, hn, R_PRE + 1)
    h4, hn_c, pre1 = ffn_forward("pre1", h3, hn, R_MIX + 1)
    proj_c = _mm("c_proj", [(hn_c, tok_si, w["c_in"], _spec((None, D, D), lambda s, i: (s, 0, 0)))],
                 grid=(NSH, nt), o_shape=(NSH, t, D), o_dtype=F32, o_spec=_spec((None, tm, D), lambda s, i: (s, i, 0)),
                 dims=NN, kaxis=1, nk=1)
    names, gather = carried("hgrn_fwd")
    o_c, y_c, sst, got = _hgrn_fwd(proj_c, small, gather)
    land(names, got)
    h5, hn = out_proj("c_out", y_c, w["c_out"], h4, R_POST + 1)
    h6, _, post1 = ffn_forward("post1", h5, hn, None)
    dh, dhb, d_fin, loss = _final_loss(h6, small, target)

    dh, dhb, grad_a, grad_b, dn_post1, _, _ = _ffn_backward("post1", dh, dhb, *post1, w["post1_gu"], w["post1_d"],
                                                            *FFN_SLOTS["post1"], small, R_POST + 1, 8 * FS, B_ROWS)
    dy_c, grad_b, swapped = out_proj_bwd("c", dhb, y_c, w["c_out"], B_COUT // 256, grad_b,
                                         lambda buf_b: swap_rider("x", dict(a=grad_a, b=buf_b)))
    pairs, rider = reduce_start("x", dict(a=grad_a, b=grad_b), swapped)
    dproj_c, d_clb, d_gam, landed = _hgrn_bwd(proj_c, small, o_c, sst, dy_c, rider)
    share = reduce_end("x", pairs, landed)
    grad_b = _wgrad("c_dwin", hn_c, _spec((tw, D), lambda s, k: (k, 0)), dproj_c, _spec((None, tw, D), lambda s, k: (s, k, 0)),
                    D, D, t, tw, grad_b, B_CIN // D, carry=share)
    if share is not None:
        grad_b, landed = grad_b
        shared("x", landed)
    dh, dhb, dn_mix1, _ = _dhn_norm("c_dhn", [(dproj_c, _spec((None, tm, D), lambda i, k: (k, i, 0)),
                                               w["c_in"], _spec((None, D, D), lambda i, k: (k, 0, 0)))],
                                    NT, h4, small, R_MIX + 1, dh)
    stage_w = {}

    def w_swap(buf_a, buf_b):
        return swap_rider("w", dict(a=buf_a, b=buf_b))

    def w_send(buf_a, buf_b, swapped):
        stage_w["pairs"], rider = reduce_start("w", dict(a=buf_a, b=buf_b), swapped)
        return rider

    on = place is not None
    dh, dhb, grad_a, grad_b, dn_pre1, _, landed = _ffn_backward(
        "pre1", dh, dhb, *pre1, w["pre1_gu"], w["pre1_d"], *FFN_SLOTS["pre1"], small, R_PRE + 1, grad_a, grad_b,
        w_swap if on else None, w_send if on else None)
    share = reduce_end("w", stage_w.get("pairs"), landed)
    dh, dhb, grad_a, grad_b, dn_post0, landed, _ = _ffn_backward(
        "post0", dh, dhb, *post0, w["post0_gu"], w["post0_d"], *FFN_SLOTS["post0"], small, R_POST, grad_a, grad_b,
        (lambda buf_a, buf_b: share) if on else None)
    shared("w", landed)
    dy_ab, grad_b, _ = out_proj_bwd("ab", dhb, y_ab, w["ab_out"], B_ABOUT // 256, grad_b)
    dab, dac, dax, d_conv, swapped = _conv_bwd(proj_a, w["conv"], dy_ab, swap_rider("y", dict(a=grad_a, b=grad_b)))
    pairs, rider = reduce_start("y", dict(a=grad_a, b=grad_b), swapped)
    dq, dk, dv, landed = _sb_bwd(proj_b, dy_ab, ltot, nblk, rider)
    share = reduce_end("y", pairs, landed)
    dproj_ab = jnp.concatenate([dab, dac, dax, dq, dk, dv], axis=1)
    grad_c = _wgrad("ab_dwin", hn_ab, _spec((tw, D), lambda s, k: (k, 0)), dproj_ab, _spec((tw, 768), lambda s, k: (k, s)),
                    D, 768, t, tw, D, 0, carry=share)
    if share is not None:
        grad_c, landed = grad_c
        shared("y", landed)
    dh, dhb, dn_mix0, _ = _dhn_norm("ab_dhn", [(dproj_ab, _spec((tm, 768), lambda i, k: (i, k)),
                                                w["ab_in"], _spec((None, D, 768), lambda i, k: (k, 0, 0)))],
                                    NT, h1, small, R_MIX, dh)
    pairs, rider = reduce_start("z1", dict(c=grad_c))
    last = {}

    def own_gradients(buf_a, buf_b, _):
        last["z2"] = reduce_start("z2", dict(a=buf_a, b=buf_b))
        return last["z2"][1]

    dh, dhb, grad_a, grad_b, dn_pre0, landed, landed_own = _ffn_backward(
        "pre0", dh, dhb, *pre0, w["pre0_gu"], w["pre0_d"], *FFN_SLOTS["pre0"], small, R_PRE, grad_a, grad_b,
        (lambda buf_a, buf_b: rider) if on else None, own_gradients if on else None)
    if place is not None:
        mine = (_owner_sum("grad_owner_sum_z1", pairs, landed, place[1])
                + _owner_sum("grad_owner_sum_z2", last["z2"][0], landed_own, place[1]))
        reduced.update(zip(REDUCE_STAGES["z1"] + REDUCE_STAGES["z2"], _alone("grad_share_z", _Share(mine))))
    zero = jnp.zeros((1, D), F32)
    conv_rows = jnp.pad(jnp.transpose(d_conv[:, :3, :], (1, 0, 2)).reshape(3, 512), ((0, 0), (0, D - 512)))
    small_grad = jnp.concatenate([
        dn_pre0, dn_pre1, dn_mix0, dn_mix1, dn_post0, dn_post1, d_clb, d_fin,
        jnp.pad(d_gam, ((0, 0), (0, D - HD))), conv_rows,
        jnp.pad(loss, ((0, 0), (0, D - 1))), zero, zero], axis=0)
    return dh, grad_a, grad_b, grad_c, small_grad, reduced


def _small_slab(rows):
    parts = [jnp.pad(r.astype(F32), ((0, 0), (0, D - r.shape[1]))) for r in rows]
    slab = jnp.concatenate(parts, axis=0)
    return jnp.pad(slab, ((0, SMALL_ROWS - slab.shape[0]), (0, 0)))


def kernel(x, ffn_pre_norm, ffn_pre_w_gate, ffn_pre_w_up, ffn_pre_w_down, mix_norm, ffn_post_norm, ffn_post_w_gate, ffn_post_w_up, ffn_post_w_down, ab_w_in, ab_conv_w, ab_w_out, c_w_in, c_lower_bounds, c_out_norm, c_w_out, final_norm, loss_target, m_ffn_pre_norm, m_ffn_pre_w_gate, m_ffn_pre_w_up, m_ffn_pre_w_down, m_mix_norm, m_ffn_post_norm, m_ffn_post_w_gate, m_ffn_post_w_up, m_ffn_post_w_down, m_ab_w_in, m_ab_conv_w, m_ab_w_out, m_c_w_in, m_c_lower_bounds, m_c_out_norm, m_c_w_out, m_final_norm, v_ffn_pre_norm, v_ffn_pre_w_gate, v_ffn_pre_w_up, v_ffn_pre_w_down, v_mix_norm, v_ffn_post_norm, v_ffn_post_w_gate, v_ffn_post_w_up, v_ffn_post_w_down, v_ab_w_in, v_ab_conv_w, v_ab_w_out, v_c_w_in, v_c_lower_bounds, v_c_out_norm, v_c_w_out, v_final_norm):
    t = x.shape[1]
    xi, yi, ci = lax.axis_index("x"), lax.axis_index("y"), lax.axis_index("c")
    p_idx = (2 * xi + yi).astype(jnp.int32).reshape(1)
    c_idx = ci.astype(jnp.int32).reshape(1)

    def halves(m):
        return m.astype(BF16).reshape(2, m.shape[0] // 2, m.shape[1])

    transposed = ("ffn_pre_w_gate", "ffn_pre_w_up", "ffn_post_w_gate", "ffn_post_w_up")

    def flip(a):
        return jnp.swapaxes(a, 1, 2)

    shards = {}
    for name, (w_gate, w_up, w_down, layer) in dict(
            pre0=(ffn_pre_w_gate, ffn_pre_w_up, ffn_pre_w_down, 0), post0=(ffn_post_w_gate, ffn_post_w_up, ffn_post_w_down, 0),
            pre1=(ffn_pre_w_gate, ffn_pre_w_up, ffn_pre_w_down, 1), post1=(ffn_post_w_gate, ffn_post_w_up, ffn_post_w_down, 1)).items():
        shards[name + "_gu"] = jnp.stack([flip(w_gate)[layer], flip(w_up)[layer]]).astype(BF16)
        shards[name + "_d"] = halves(w_down[layer])
    conv_pad = jnp.pad(ab_conv_w[0], ((0, 5), (0, 0)))
    shards.update(ab_in=halves(ab_w_in[0]), ab_out=halves(ab_w_out[0]), c_in=halves(c_w_in[0]), c_out=halves(c_w_out[0]),
                  conv=jnp.stack([conv_pad, jnp.zeros_like(conv_pad)]))
    first = _alone("gather_weights", _Gather([shards["pre0_gu"]]))[0].reshape(W_SHAPES["pre0_gu"])

    small = _small_slab([ffn_pre_norm, mix_norm, ffn_post_norm, c_lower_bounds, final_norm.reshape(1, D), c_out_norm])
    small = small.reshape(SMALL_ROWS, 1, D)

    grad_x, _, _, _, small_grad, reduced = _local_step(x[0], loss_target[0], dict(pre0_gu=first), small, shards,
                                                        (c_idx, p_idx))
    whole = {n: g.reshape(2 * g.shape[1], g.shape[2]) for n, g in reduced.items()}
    small_sum = _sum_small(small_grad)

    my_conv = lax.dynamic_slice(small_sum[R_CONV:R_CONV + 3], (0, (2 * xi + yi) * 128), (3, 128))
    grads = {
        "ffn_pre_norm": small_sum[R_PRE:R_PRE + 2], "mix_norm": small_sum[R_MIX:R_MIX + 2],
        "ffn_post_norm": small_sum[R_POST:R_POST + 2], "c_lower_bounds": small_sum[R_CLB:R_CLB + 2],
        "c_out_norm": small_sum[R_GAM:R_GAM + 1, :HD], "final_norm": small_sum[R_FIN],
        "ab_conv_w": my_conv.reshape(1, 3, 128),
    }
    layers = dict(ab_w_in=[whole["ab_in"]], ab_w_out=[whole["ab_out"]], c_w_in=[whole["c_in"]], c_w_out=[whole["c_out"]])
    for kind, key in (("gate", "_g"), ("up", "_u"), ("down", "_d")):
        layers["ffn_pre_w_" + kind] = [whole["pre0" + key], whole["pre1" + key]]
        layers["ffn_post_w_" + kind] = [whole["post0" + key], whole["post1" + key]]
    weights = dict(ffn_pre_norm=ffn_pre_norm, ffn_pre_w_gate=ffn_pre_w_gate, ffn_pre_w_up=ffn_pre_w_up, ffn_pre_w_down=ffn_pre_w_down, mix_norm=mix_norm, ffn_post_norm=ffn_post_norm, ffn_post_w_gate=ffn_post_w_gate, ffn_post_w_up=ffn_post_w_up, ffn_post_w_down=ffn_post_w_down, ab_w_in=ab_w_in, ab_conv_w=ab_conv_w, ab_w_out=ab_w_out, c_w_in=c_w_in, c_lower_bounds=c_lower_bounds, c_out_norm=c_out_norm, c_w_out=c_w_out, final_norm=final_norm)
    m_in = dict(ffn_pre_norm=m_ffn_pre_norm, ffn_pre_w_gate=m_ffn_pre_w_gate, ffn_pre_w_up=m_ffn_pre_w_up, ffn_pre_w_down=m_ffn_pre_w_down, mix_norm=m_mix_norm, ffn_post_norm=m_ffn_post_norm, ffn_post_w_gate=m_ffn_post_w_gate, ffn_post_w_up=m_ffn_post_w_up, ffn_post_w_down=m_ffn_post_w_down, ab_w_in=m_ab_w_in, ab_conv_w=m_ab_conv_w, ab_w_out=m_ab_w_out, c_w_in=m_c_w_in, c_lower_bounds=m_c_lower_bounds, c_out_norm=m_c_out_norm, c_w_out=m_c_w_out, final_norm=m_final_norm)
    v_in = dict(ffn_pre_norm=v_ffn_pre_norm, ffn_pre_w_gate=v_ffn_pre_w_gate, ffn_pre_w_up=v_ffn_pre_w_up, ffn_pre_w_down=v_ffn_pre_w_down, mix_norm=v_mix_norm, ffn_post_norm=v_ffn_post_norm, ffn_post_w_gate=v_ffn_post_w_gate, ffn_post_w_up=v_ffn_post_w_up, ffn_post_w_down=v_ffn_post_w_down, ab_w_in=v_ab_w_in, ab_conv_w=v_ab_conv_w, ab_w_out=v_ab_w_out, c_w_in=v_c_w_in, c_lower_bounds=v_c_lower_bounds, c_out_norm=v_c_out_norm, c_w_out=v_c_w_out, final_norm=v_final_norm)
    names = list(weights)
    big = [n for n in names if weights[n].size >= 65536]
    tiny = [n for n in names if n not in big]

    delta, new_m, new_v = {}, {}, {}
    for n in big:
        turn = flip if n in transposed else (lambda a: a)
        shape = turn(weights[n]).shape
        two_d = (shape[0] * shape[1], shape[2])
        d, m2, v2, g2 = _adamw("adamw_" + n, turn(weights[n]).reshape(two_d), layers[n],
                               turn(m_in[n]).reshape(two_d), turn(v_in[n]).reshape(two_d))
        delta[n], new_m[n], new_v[n] = turn(d.reshape(shape)), turn(m2.reshape(shape)), turn(v2.reshape(shape))
        grads[n] = turn(g2.reshape(shape))

    def tiny_slab(src):
        return _small_slab([src[n].reshape(-1, src[n].shape[-1]) for n in tiny])

    offs, row = {}, 0
    for n in tiny:
        nrows = weights[n].size // weights[n].shape[-1]
        offs[n] = (row, nrows)
        row += nrows
    d, m2, v2, _ = _adamw("adamw_small", tiny_slab(weights), [tiny_slab(grads)], tiny_slab(m_in), tiny_slab(v_in))
    for n in tiny:
        r0, nr = offs[n]
        shape = weights[n].shape
        for dst, src in ((delta, d), (new_m, m2), (new_v, v2)):
            dst[n] = src[r0:r0 + nr, :shape[-1]].reshape(shape)

    loss = small_sum[R_LOSS, 0]
    return (loss, grad_x.reshape(1, t, D), *[grads[n] for n in names], *[delta[n] for n in names],
            *[new_m[n] for n in names], *[new_v[n] for n in names])
```

```python
import functools
import math

import jax
import jax.numpy as jnp
from jax import lax
from jax.experimental import pallas as pl
from jax.experimental.pallas import tpu as pltpu

F32 = jnp.float32
BF16 = jnp.bfloat16
MESH = pl.DeviceIdType.MESH
ANY = pl.BlockSpec(memory_space=pl.ANY)

D = 1024
FS = 704
NSH = 4
RMS_EPS = 1e-6
MACARON = 0.5
CHUNK = 64
HD = 128
HGRN_HPS = 8
SBQ = 128
SB_PAIRS_FWD = 4
SB_PAIRS_BWD = 4
SB_DEAD = -105.0
CONV_HALO = 8
LANE = 128
ROW_TILE = 512
MM_TILE = 1024
WGRAD_TILE = 4096
VMEM_LIMIT = 48 * 1024 * 1024
VMEM_LIMIT_BIG = 58 * 1024 * 1024

ADAM_LR, ADAM_B1, ADAM_B2, ADAM_EPS, ADAM_WD, ADAM_STEP = 0.001, 0.9, 0.999, 1e-08, 0.01, 10

NN = ((1,), (0,))
NT = ((1,), (1,))
TN = ((0,), (0,))

SMALL_ROWS = 16
R_PRE, R_MIX, R_POST, R_CLB, R_FIN, R_GAM, R_CONV, R_LOSS = 0, 2, 4, 6, 8, 9, 10, 13

B_DOWN = 0
B_ABOUT = 4 * FS
B_CIN = B_ABOUT + 256
B_COUT = B_CIN + 1024
B_ROWS = B_COUT + 256


def _dg(a, b, dims):
    return lax.dot_general(a, b, (dims, ((), ())), preferred_element_type=F32)


def _split(x):
    hi = x.astype(BF16)
    lo = (x - hi.astype(F32)).astype(BF16)
    return hi, lo


def _dot3(a, b, dims):
    ah, al = _split(a)
    bh, bl = _split(b)
    if dims == TN:
        n = b.shape[1]
        both = _dg(ah, jnp.concatenate([bh, bl], axis=1), dims)
        return both[:, :n] + both[:, n:] + _dg(al, bh, dims)
    m = a.shape[0]
    both = _dg(jnp.concatenate([ah, al], axis=0), bh, dims)
    return both[:m] + both[m:] + _dg(ah, bl, dims)


def _sigmoid(x):
    return 1.0 / (1.0 + jnp.exp(-x))


def _params(sem):
    return pltpu.CompilerParams(dimension_semantics=sem, vmem_limit_bytes=VMEM_LIMIT)


def _spec(shape, imap):
    return pl.BlockSpec(shape, imap)


def _accumulate(acc_ref, pairs, dims):
    part = None
    for a_ref, b_ref in pairs:
        d = _dg(a_ref[...], b_ref[...], dims)
        part = d if part is None else part + d
    acc_ref[...] += part


def _mm(name, pairs, *, grid, o_shape, o_dtype, o_spec, dims, kaxis, nk, acc_shape=None, res=None, scale=None,
        into=None, carry=None, norm=None):
    npairs = len(pairs)
    operands, specs = [], []
    for a, a_spec, b, b_spec in pairs:
        operands += [a, b]
        specs += [a_spec, b_spec]
    if res is not None:
        operands.append(res[0])
        specs.append(res[1])
    aliases = {}
    if into is not None:
        aliases = {len(operands): 0}
        operands.append(into)
        specs.append(ANY)
    if norm is not None:
        operands.append(norm[0])
        specs.append(_spec((None, 1, D), lambda *_: (norm[1], 0, 0)))
    n_own = len(operands)
    n_out = 2 if norm is not None else 1
    nc = carry.n if carry is not None else 0
    if nc:
        operands += carry.groups
        specs += carry.in_specs

    def body(*refs):
        o_ref = refs[n_own + nc]
        riders = ((refs[n_own:n_own + nc], refs[n_own + nc + n_out:n_own + 2 * nc + n_out], refs[-2], refs[-1])
                  if nc else None)
        if nc:
            carry.ride(grid, riders, "start")

        def finish(val):
            if scale is not None:
                val = val * scale
            if res is not None:
                val = val + refs[2 * npairs][...]
            o_ref[...] = val.astype(o_dtype)
            if norm is not None:
                r = lax.rsqrt(jnp.mean(val * val, axis=-1, keepdims=True) + RMS_EPS)
                refs[n_own + nc + 1][...] = (val * r * refs[n_own - 1][...]).astype(BF16)

        if nk == 1:
            part = None
            for n in range(npairs):
                d = _dg(refs[2 * n][...], refs[2 * n + 1][...], dims)
                part = d if part is None else part + d
            finish(part)
        else:
            acc_ref = refs[n_own + 2 * nc + n_out]
            k = pl.program_id(kaxis)

            @pl.when(k == 0)
            def _():
                acc_ref[...] = jnp.zeros_like(acc_ref)

            _accumulate(acc_ref, [(refs[2 * n], refs[2 * n + 1]) for n in range(npairs)], dims)

            @pl.when(k == nk - 1)
            def _():
                finish(acc_ref[...])
        if nc:
            carry.ride(grid, riders, "finish")

    sem = tuple("arbitrary" if (nc or (ax == kaxis and nk > 1)) else "parallel" for ax in range(len(grid)))
    outs = pl.pallas_call(
        body, name=name, grid=grid, in_specs=specs,
        out_specs=[o_spec] * n_out + (carry.out_specs if nc else []),
        out_shape=[jax.ShapeDtypeStruct(o_shape, o_dtype)] + ([jax.ShapeDtypeStruct(o_shape, BF16)] if norm is not None else [])
        + (carry.out_shape if nc else []),
        scratch_shapes=([pltpu.VMEM(acc_shape, F32)] if nk > 1 else []) + (carry.scratch if nc else []),
        input_output_aliases=aliases,
        compiler_params=_params(sem),
    )(*operands)
    main = (outs[0], outs[1]) if norm is not None else outs[0]
    return (main, carry.place(outs[n_out:])) if nc else main


def _norm_fwd(name, h, gain_slab, row, rider=None):
    t = h.shape[0]
    tm = min(ROW_TILE, t)
    grid = (t // tm,)
    nr = rider.n if rider is not None else 0

    def body(*refs):
        h_ref, g_ref, o_ref = refs[0], refs[1], refs[2 + nr]
        riders = (refs[2:2 + nr], refs[3 + nr:3 + 2 * nr], refs[-2], refs[-1]) if nr else None
        if nr:
            rider.ride(grid, riders, "start")
        x = h_ref[...]
        r = lax.rsqrt(jnp.mean(x * x, axis=-1, keepdims=True) + RMS_EPS)
        o_ref[...] = (x * r * g_ref[...]).astype(BF16)
        if nr:
            rider.ride(grid, riders, "finish")

    outs = pl.pallas_call(
        body, name=name, grid=grid,
        in_specs=[_spec((tm, D), lambda i: (i, 0)), _spec((None, 1, D), lambda i: (row, 0, 0))]
        + (rider.in_specs if nr else []),
        out_specs=[_spec((tm, D), lambda i: (i, 0))] + (rider.out_specs if nr else []),
        out_shape=[jax.ShapeDtypeStruct((t, D), BF16)] + (rider.out_shape if nr else []),
        scratch_shapes=rider.scratch if nr else [],
        compiler_params=_params(("arbitrary",) if nr else ("parallel",)),
    )(h, gain_slab, *(rider.groups if nr else []))
    return outs[0], (rider.place(outs[1:]) if nr else [])


def _dhn_norm(name, pairs, dims, h, gain_slab, row, dres, rider=None):
    t = h.shape[0]
    tm = min(MM_TILE, t)
    nt = t // tm
    grid = (nt, NSH)
    npairs = len(pairs)
    nr = rider.n if rider is not None else 0
    operands, specs = [], []
    for a, a_spec, b, b_spec in pairs:
        operands += [a, b]
        specs += [a_spec, b_spec]
    row_spec = _spec((tm, D), lambda i, k: (i, 0))
    operands += [h, gain_slab, dres]
    specs += [row_spec, _spec((None, 1, D), lambda i, k: (row, 0, 0)), row_spec]
    n_own = len(operands)

    def body(*refs):
        h_ref, g_ref, dres_ref = refs[2 * npairs:n_own]
        dh_ref, dhb_ref, dg_ref = refs[n_own + nr:n_own + nr + 3]
        acc_ref, gacc_ref = refs[n_own + 2 * nr + 3:n_own + 2 * nr + 5]
        riders = (refs[n_own:n_own + nr], refs[n_own + nr + 3:n_own + 2 * nr + 3], refs[-2], refs[-1]) if nr else None
        if nr:
            rider.ride(grid, riders, "start")
        i, k = pl.program_id(0), pl.program_id(1)

        @pl.when(k == 0)
        def _():
            acc_ref[...] = jnp.zeros_like(acc_ref)

        _accumulate(acc_ref, [(refs[2 * n], refs[2 * n + 1]) for n in range(npairs)], dims)

        @pl.when(k == NSH - 1)
        def _():
            x = h_ref[...]
            r = lax.rsqrt(jnp.mean(x * x, axis=-1, keepdims=True) + RMS_EPS)
            xh = x * r
            dy = acc_ref[...]
            gdy = dy * g_ref[...]
            dh = dres_ref[...] + (gdy - xh * jnp.mean(gdy * xh, axis=-1, keepdims=True)) * r
            dh_ref[...] = dh
            dhb_ref[...] = dh.astype(BF16)
            gpart = jnp.sum((dy * xh).reshape(tm // 8, 8, D), axis=0)

            @pl.when(i == 0)
            def _():
                gacc_ref[...] = gpart

            @pl.when(i > 0)
            def _():
                gacc_ref[...] += gpart

            @pl.when(i == nt - 1)
            def _():
                dg_ref[...] = jnp.sum(gacc_ref[...], axis=0, keepdims=True)

        if nr:
            rider.ride(grid, riders, "finish")

    outs = pl.pallas_call(
        body, name=name, grid=grid, in_specs=specs + (rider.in_specs if nr else []),
        out_specs=[row_spec, row_spec, _spec((1, D), lambda i, k: (0, 0))] + (rider.out_specs if nr else []),
        out_shape=[jax.ShapeDtypeStruct((t, D), F32), jax.ShapeDtypeStruct((t, D), BF16),
                   jax.ShapeDtypeStruct((1, D), F32)] + (rider.out_shape if nr else []),
        scratch_shapes=[pltpu.VMEM((tm, D), F32), pltpu.VMEM((8, D), F32)] + (rider.scratch if nr else []),
        compiler_params=pltpu.CompilerParams(dimension_semantics=("arbitrary", "arbitrary"),
                                             vmem_limit_bytes=VMEM_LIMIT_BIG),
    )(*operands, *(rider.groups if nr else []))
    return outs[0], outs[1], outs[2], (rider.place(outs[3:]) if nr else [])


def _final_loss(h, gain_slab, target):
    t = h.shape[0]
    tm = min(ROW_TILE, t)
    nt = t // tm

    def body(h_ref, g_ref, t_ref, dh_ref, dhb_ref, dg_ref, loss_ref, acc_ref, lacc_ref):
        i = pl.program_id(0)
        x = h_ref[...]
        g = g_ref[...]
        r = lax.rsqrt(jnp.mean(x * x, axis=-1, keepdims=True) + RMS_EPS)
        xh = x * r
        err = xh * g - t_ref[...]
        dy = err * (1.0 / D)
        gdy = dy * g
        dh = (gdy - xh * jnp.mean(gdy * xh, axis=-1, keepdims=True)) * r
        dh_ref[...] = dh
        dhb_ref[...] = dh.astype(BF16)
        part = jnp.sum((dy * xh).reshape(tm // 8, 8, D), axis=0)
        lpart = jnp.sum((err * err).reshape(tm // 8, 8, D), axis=0)

        @pl.when(i == 0)
        def _():
            acc_ref[...] = part
            lacc_ref[...] = lpart

        @pl.when(i > 0)
        def _():
            acc_ref[...] += part
            lacc_ref[...] += lpart

        @pl.when(i == nt - 1)
        def _():
            dg_ref[...] = jnp.sum(acc_ref[...], axis=0, keepdims=True)
            rows = jnp.sum(lacc_ref[...], axis=0, keepdims=True)
            loss_ref[...] = jnp.sum(rows, axis=1, keepdims=True) * (0.5 / D)

    row_spec = _spec((tm, D), lambda i: (i, 0))
    return pl.pallas_call(
        body, name="final_loss", grid=(nt,),
        in_specs=[row_spec, _spec((None, 1, D), lambda i: (R_FIN, 0, 0)), row_spec],
        out_specs=[row_spec, row_spec, _spec((1, D), lambda i: (0, 0)), _spec((1, 1), lambda i: (0, 0))],
        out_shape=[jax.ShapeDtypeStruct((t, D), F32), jax.ShapeDtypeStruct((t, D), BF16),
                   jax.ShapeDtypeStruct((1, D), F32), jax.ShapeDtypeStruct((1, 1), F32)],
        scratch_shapes=[pltpu.VMEM((8, D), F32), pltpu.VMEM((8, D), F32)],
        compiler_params=_params(("arbitrary",)),
    )(h, gain_slab, target)


def _ffn_up(name, hn, wgu, carry=None):
    t = hn.shape[0]
    tm = min(MM_TILE, t)
    grid = (NSH, t // tm)
    nc = carry.n if carry is not None else 0

    def body(*refs):
        x_ref, wg_ref, wu_ref = refs[:3]
        s_up_ref, s_gate_ref, a_ref = refs[3 + nc:6 + nc]
        riders = (refs[3:3 + nc], refs[6 + nc:6 + 2 * nc], refs[-2], refs[-1]) if nc else None
        if nc:
            carry.ride(grid, riders, "start")
        x = x_ref[...]
        g = _dg(x, wg_ref[...], NT)
        u = _dg(x, wu_ref[...], NT)
        sg = _sigmoid(g)
        silu = g * sg
        s_up_ref[...] = (MACARON * silu).astype(BF16)
        s_gate_ref[...] = (MACARON * u * (sg * (1.0 + g * (1.0 - sg)))).astype(BF16)
        a_ref[...] = (silu * u).astype(BF16)
        if nc:
            carry.ride(grid, riders, "finish")

    act = _spec((None, tm, FS), lambda s, i: (s, i, 0))
    shape = jax.ShapeDtypeStruct((NSH, t, FS), BF16)
    outs = pl.pallas_call(
        body, name=name, grid=grid,
        in_specs=[_spec((tm, D), lambda s, i: (i, 0)),
                  _spec((None, None, FS, D), lambda s, i: (s, 0, 0, 0)),
                  _spec((None, None, FS, D), lambda s, i: (s, 1, 0, 0))] + (carry.in_specs if nc else []),
        out_specs=[act, act, act] + (carry.out_specs if nc else []),
        out_shape=[shape, shape, shape] + (carry.out_shape if nc else []),
        scratch_shapes=carry.scratch if nc else [],
        compiler_params=_params(("arbitrary", "arbitrary") if nc else ("parallel", "parallel")),
    )(hn, wgu, wgu, *(carry.groups if nc else []))
    return (outs[0], outs[1], outs[2], carry.place(outs[3:]) if nc else [])


def _ffn_down(name, a, wd, h, carry=None, norm=None):
    t = h.shape[0]
    tm = min(MM_TILE, t)
    return _mm(name, [(a, _spec((None, tm, FS), lambda i, k: (k, i, 0)),
                       wd, _spec((None, FS, D), lambda i, k: (k, 0, 0)))],
               grid=(t // tm, NSH), o_shape=(t, D), o_dtype=F32, o_spec=_spec((tm, D), lambda i, k: (i, 0)),
               dims=NN, kaxis=1, nk=NSH, acc_shape=(tm, D), res=(h, _spec((tm, D), lambda i, k: (i, 0))),
               scale=MACARON, carry=carry, norm=norm)


def _ffn_bwd_up(name, dhb, wd, s_up, s_gate, rider=None):
    t = dhb.shape[0]
    tm = min(MM_TILE, t)
    grid = (NSH, t // tm)
    nr = rider.n if rider is not None else 0

    def body(*refs):
        dh_ref, wd_ref, s_up_ref, s_gate_ref = refs[:4]
        dg_ref, du_ref = refs[4 + nr:6 + nr]
        riders = (refs[4:4 + nr], refs[6 + nr:6 + 2 * nr], refs[-2], refs[-1]) if nr else None
        if nr:
            rider.ride(grid, riders, "start")
        da = _dg(dh_ref[...], wd_ref[...], NT).astype(BF16)
        du_ref[...] = da * s_up_ref[...]
        dg_ref[...] = da * s_gate_ref[...]
        if nr:
            rider.ride(grid, riders, "finish")

    act = _spec((None, tm, FS), lambda s, i: (s, i, 0))
    shape = jax.ShapeDtypeStruct((NSH, t, FS), BF16)
    outs = pl.pallas_call(
        body, name=name, grid=grid,
        in_specs=[_spec((tm, D), lambda s, i: (i, 0)), _spec((None, FS, D), lambda s, i: (s, 0, 0)), act, act]
        + (rider.in_specs if nr else []),
        out_specs=[act, act] + (rider.out_specs if nr else []),
        out_shape=[shape, shape] + (rider.out_shape if nr else []),
        scratch_shapes=rider.scratch if nr else [],
        compiler_params=_params(("arbitrary", "arbitrary") if nr else ("parallel", "parallel")),
    )(dhb, wd, s_up, s_gate, *(rider.groups if nr else []))
    return outs[0], outs[1], (rider.place(outs[2:]) if nr else [])


def _wgrad(name, a, a_spec, b, b_spec, out_rows, out_cols, t, tt, group, slot, scale=None, carry=None):
    first = isinstance(group, int)
    rows = group if first else group.shape[1]
    return _mm(name, [(a, a_spec, b, b_spec)], grid=(NSH, t // tt),
               o_shape=(NSH, rows, out_cols), o_dtype=BF16,
               o_spec=_spec((None, out_rows, out_cols), lambda s, k: (s, slot, 0)),
               dims=TN, kaxis=1, nk=t // tt, acc_shape=(out_rows, out_cols), scale=scale,
               into=None if first else group, carry=carry)


def _ffn_backward(tag, dh, dhb, h_in, hn, s_up, s_gate, a, wgu, wd, gate_idx, up_idx, down_idx, small, norm_row,
                  grad_a, grad_b, rider_up=None, rider_dhn=None):
    t = dh.shape[0]
    tm = min(MM_TILE, t)
    tt = min(WGRAD_TILE, t)
    tok = _spec((tt, D), lambda s, k: (k, 0))
    hid = _spec((None, tt, FS), lambda s, k: (s, k, 0))
    grad_b = _wgrad(tag + "_dwd", a, hid, dhb, tok, FS, D, t, tt, grad_b, down_idx, scale=MACARON)
    rider = rider_up(grad_a, grad_b) if rider_up is not None else None
    dg, du, landed_up = _ffn_bwd_up(tag + "_bwd_up", dhb, wd, s_up, s_gate, rider)
    grad_a = _wgrad(tag + "_dwg", dg, hid, hn, tok, FS, D, t, tt, grad_a, gate_idx)
    grad_a = _wgrad(tag + "_dwu", du, hid, hn, tok, FS, D, t, tt, grad_a, up_idx)
    rider = rider_dhn(grad_a, grad_b, landed_up) if rider_dhn is not None else None
    act = _spec((None, tm, FS), lambda i, k: (k, i, 0))
    dh_in, dhb_in, d_gain, landed_dhn = _dhn_norm(
        tag + "_dhn", [(dg, act, wgu, _spec((None, None, FS, D), lambda i, k: (k, 0, 0, 0))),
                       (du, act, wgu, _spec((None, None, FS, D), lambda i, k: (k, 1, 0, 0)))],
        NN, h_in, small, norm_row, dh, rider)
    return dh_in, dhb_in, grad_a, grad_b, d_gain, landed_up, landed_dhn


def _conv_fwd(proj_a, conv_w):
    t = proj_a.shape[0]
    tm = min(ROW_TILE, t)
    hb = tm // CONV_HALO

    def body(ab_ref, ac_ref, ax_ref, acp_ref, axp_ref, w_ref, y_ref):
        i = pl.program_id(1)
        u = ac_ref[...] * ax_ref[...]
        up = jnp.where(i > 0, acp_ref[...] * axp_ref[...], 0.0)
        ext = jnp.concatenate([up, u], axis=0)
        u1 = pltpu.roll(ext, 1, 0)[CONV_HALO:]
        u2 = pltpu.roll(ext, 2, 0)[CONV_HALO:]
        w = w_ref[...]
        conv = w[0:1] * u2 + w[1:2] * u1 + w[2:3] * u
        y_ref[...] = (ab_ref[...] * conv).astype(BF16)

    def cur(off):
        return _spec((tm, LANE), lambda j, i: (i, off + j))

    def prev(off):
        return _spec((CONV_HALO, LANE), lambda j, i: (jnp.maximum(i * hb - 1, 0), off + j))

    return pl.pallas_call(
        body, name="conv_fwd", grid=(4, t // tm),
        in_specs=[cur(0), cur(4), cur(8), prev(4), prev(8),
                  _spec((None, None, 8, LANE), lambda j, i: (j, 0, 0, 0))],
        out_specs=_spec((tm, LANE), lambda j, i: (i, j)),
        out_shape=jax.ShapeDtypeStruct((t, 512), BF16),
        compiler_params=_params(("parallel", "parallel")),
    )(proj_a, proj_a, proj_a, proj_a, proj_a, conv_w)


def _conv_bwd(proj_a, conv_w, dy, rider=None):
    t = proj_a.shape[0]
    tm = min(ROW_TILE, t)
    hb = tm // CONV_HALO
    nt = t // tm
    grid = (4, nt)
    nr = rider.n if rider is not None else 0

    def body(*refs):
        ab_ref, ac_ref, ax_ref, dy_ref, acp_ref, axp_ref, abn_ref, dyn_ref, w_ref = refs[:9]
        dab_ref, dac_ref, dax_ref, dw_ref = refs[9 + nr:13 + nr]
        acc_ref = refs[13 + 2 * nr]
        riders = (refs[9:9 + nr], refs[13 + nr:13 + 2 * nr], refs[-2], refs[-1]) if nr else None
        if nr:
            rider.ride(grid, riders, "start")
        i = pl.program_id(1)
        ab, ac, ax = ab_ref[...], ac_ref[...], ax_ref[...]
        u = ac * ax
        up = jnp.where(i > 0, acp_ref[...] * axp_ref[...], 0.0)
        ext = jnp.concatenate([up, u], axis=0)
        u1 = pltpu.roll(ext, 1, 0)[CONV_HALO:]
        u2 = pltpu.roll(ext, 2, 0)[CONV_HALO:]
        w = w_ref[...]
        conv = w[0:1] * u2 + w[1:2] * u1 + w[2:3] * u
        dy_v = dy_ref[...]
        dab_ref[...] = (dy_v * conv).astype(BF16)
        dc = dy_v * ab
        dcn = jnp.where(i < nt - 1, dyn_ref[...] * abn_ref[...], 0.0)
        extn = jnp.concatenate([dc, dcn], axis=0)
        n = tm + CONV_HALO
        dc1 = pltpu.roll(extn, n - 1, 0)[:tm]
        dc2 = pltpu.roll(extn, n - 2, 0)[:tm]
        du = w[2:3] * dc + w[1:2] * dc1 + w[0:1] * dc2
        dac_ref[...] = (du * ax).astype(BF16)
        dax_ref[...] = (du * ac).astype(BF16)
        rid = lax.broadcasted_iota(jnp.int32, (8, LANE), 0)
        part = jnp.where(rid == 0, jnp.sum(dc * u2, axis=0, keepdims=True),
                         jnp.where(rid == 1, jnp.sum(dc * u1, axis=0, keepdims=True),
                                   jnp.where(rid == 2, jnp.sum(dc * u, axis=0, keepdims=True), 0.0)))

        @pl.when(i == 0)
        def _():
            acc_ref[...] = part

        @pl.when(i > 0)
        def _():
            acc_ref[...] += part

        @pl.when(i == nt - 1)
        def _():
            dw_ref[...] = acc_ref[...]

        if nr:
            rider.ride(grid, riders, "finish")

    def cur(off):
        return _spec((tm, LANE), lambda j, i: (i, off + j))

    def prev(off):
        return _spec((CONV_HALO, LANE), lambda j, i: (jnp.maximum(i * hb - 1, 0), off + j))

    def nxt(off):
        return _spec((CONV_HALO, LANE), lambda j, i: (jnp.minimum((i + 1) * hb, nt * hb - 1), off + j))

    outs = pl.pallas_call(
        body, name="conv_bwd", grid=grid,
        in_specs=[cur(0), cur(4), cur(8), cur(0), prev(4), prev(8), nxt(0), nxt(0),
                  _spec((None, None, 8, LANE), lambda j, i: (j, 0, 0, 0))] + (rider.in_specs if nr else []),
        out_specs=[_spec((tm, LANE), lambda j, i: (i, j)), _spec((tm, LANE), lambda j, i: (i, j)),
                   _spec((tm, LANE), lambda j, i: (i, j)), _spec((None, 8, LANE), lambda j, i: (j, 0, 0))]
        + (rider.out_specs if nr else []),
        out_shape=[jax.ShapeDtypeStruct((t, 512), BF16), jax.ShapeDtypeStruct((t, 512), BF16),
                   jax.ShapeDtypeStruct((t, 512), BF16), jax.ShapeDtypeStruct((4, 8, LANE), F32)]
        + (rider.out_shape if nr else []),
        scratch_shapes=[pltpu.VMEM((8, LANE), F32)] + (rider.scratch if nr else []),
        compiler_params=_params(("arbitrary", "arbitrary") if nr else ("parallel", "arbitrary")),
    )(proj_a, proj_a, proj_a, dy, proj_a, proj_a, proj_a, dy, conv_w, *(rider.groups if nr else []))
    return outs[0], outs[1], outs[2], outs[3], (rider.place(outs[4:]) if nr else [])


def _log_sigmoid(z):
    return jnp.minimum(z, 0.0) - jnp.log(1.0 + jnp.exp(-jnp.abs(z)))


def _sb_masks():
    row = lax.broadcasted_iota(jnp.int32, (SBQ, SBQ), 0)
    col = lax.broadcasted_iota(jnp.int32, (SBQ, SBQ), 1)
    return row, col


def _ones_where(mask):
    return jnp.where(mask, 1.0, 0.0).astype(BF16)


def _head_mask(head):
    lane = lax.broadcasted_iota(jnp.int32, (1, LANE), 1)
    return lane >= 64 if head else lane < 64


def _sb_fwd(proj_b, carry=None):
    t = proj_b.shape[0]
    nq = t // SBQ
    scale = 1.0 / math.sqrt(64.0)

    npair = SB_PAIRS_FWD
    wide = npair * LANE
    ngrp = 4 // npair
    chains = [(p, head) for p in range(npair) for head in range(2)]

    grid = (ngrp, nq)
    nc = carry.n if carry is not None else 0

    def body(*refs):
        q_ref, k_ref, v_ref = refs[:3]
        y_ref, l_ref, n_ref = refs[3 + nc:6 + nc]
        riders = (refs[3:3 + nc], refs[6 + nc:6 + 2 * nc], refs[-2], refs[-1]) if nc else None
        if nc:
            carry.ride(grid, riders, "start")
        grp = pl.program_id(0)
        qi = pl.program_id(1)
        row, col = _sb_masks()
        m_suffix = _ones_where(row > col)
        rows = len(chains) * SBQ
        strict = (lax.broadcasted_iota(jnp.int32, (rows, SBQ), 1)
                  < (lax.broadcasted_iota(jnp.int32, (rows, SBQ), 0) & (SBQ - 1)))
        q_pair = []
        for p in range(npair):
            q_all = q_ref[:, p * LANE:(p + 1) * LANE]
            q_pair.append(jnp.concatenate([jnp.where(_head_mask(head), q_all, jnp.zeros_like(q_all)) for head in range(2)],
                                          axis=0))

        def block(kb, state, diag):
            run, acc = state
            start = pl.multiple_of(kb * SBQ, SBQ)
            z = jnp.concatenate([_dg(q_pair[p], k_ref[pl.ds(start, SBQ), p * LANE:(p + 1) * LANE], NT)
                                 for p in range(npair)], axis=0) * scale
            lb = _log_sigmoid(z)
            lk = lb - z
            if diag:
                lk = jnp.where(strict, lk, 0.0)
            hi, lo = _split(lk)
            sums = _dg(jnp.concatenate([hi, lo], axis=0), m_suffix, NN)
            w = jnp.exp(lb + (run + sums[:rows] + sums[rows:]))
            if diag:
                w = jnp.where(strict, w, 0.0)
            wb = w.astype(BF16)
            acc = acc + jnp.concatenate(
                [_dg(wb[2 * p * SBQ:2 * (p + 1) * SBQ], v_ref[pl.ds(start, SBQ), p * LANE:(p + 1) * LANE], NN)
                 for p in range(npair)], axis=0)
            run = run + jnp.sum(hi.astype(F32) + lo.astype(F32), axis=1, keepdims=True)
            return run, acc

        state = block(qi, (jnp.zeros((rows, 1), F32), jnp.zeros((rows, LANE), F32)), True)

        def live(c):
            return jnp.logical_and(c[0] < qi, jnp.max(c[1][0]) > SB_DEAD)

        def step(c):
            return c[0] + 1, block(qi - 1 - c[0], c[1], False)

        count, (run, acc) = lax.while_loop(live, step, (jnp.int32(0), state))
        n_ref[grp * nq + qi] = count.astype(F32)
        hm = _head_mask(0)
        for p in range(npair):
            lo_rows, hi_rows = slice(2 * p * SBQ, (2 * p + 1) * SBQ), slice((2 * p + 1) * SBQ, (2 * p + 2) * SBQ)
            y_ref[:, p * LANE:(p + 1) * LANE] = jnp.where(hm, acc[lo_rows], acc[hi_rows]).astype(BF16)
            l_ref[p] = jnp.where(hm, run[lo_rows], run[hi_rows])
        if nc:
            carry.ride(grid, riders, "finish")

    outs = pl.pallas_call(
        body, name="sb_fwd", grid=grid,
        in_specs=[_spec((SBQ, wide), lambda g, i: (i, g)),
                  _spec((t, wide), lambda g, i: (0, ngrp + g)),
                  _spec((t, wide), lambda g, i: (0, 2 * ngrp + g))] + (carry.in_specs if nc else []),
        out_specs=[_spec((SBQ, wide), lambda g, i: (i, g)), _spec((npair, SBQ, LANE), lambda g, i: (g, i, 0)),
                   pl.BlockSpec(memory_space=pltpu.SMEM)] + (carry.out_specs if nc else []),
        out_shape=[jax.ShapeDtypeStruct((t, 512), BF16), jax.ShapeDtypeStruct((4, t, LANE), F32),
                   jax.ShapeDtypeStruct((ngrp * nq,), F32)] + (carry.out_shape if nc else []),
        scratch_shapes=carry.scratch if nc else [],
        compiler_params=_params(("arbitrary", "arbitrary")),
    )(proj_b, proj_b, proj_b, *(carry.groups if nc else []))
    return outs[0], outs[1], outs[2], (carry.place(outs[3:]) if nc else [])


def _sb_bwd(proj_b, dy, ltot, nblk, rider=None):
    t = proj_b.shape[0]
    nq = t // SBQ
    scale = 1.0 / math.sqrt(64.0)

    npair = SB_PAIRS_BWD
    wide = npair * LANE
    ngrp = 4 // npair
    chains = [(p, head) for p in range(npair) for head in range(2)]
    grid = (ngrp, nq)
    nr = rider.n if rider is not None else 0
    per_count = SB_PAIRS_FWD // SB_PAIRS_BWD

    def body(*refs):
        q_ref, k_ref, v_ref, dy_ref, l_ref, n_ref = refs[:6]
        dq_ref, dk_ref, dv_ref = refs[6 + nr:9 + nr]
        dk_acc, dv_acc = refs[9 + 2 * nr:11 + 2 * nr]
        riders = (refs[6:6 + nr], refs[9 + nr:9 + 2 * nr], refs[-2], refs[-1]) if nr else None
        if nr:
            rider.ride(grid, riders, "start")
        grp = pl.program_id(0)
        qi = pl.program_id(1)

        @pl.when(qi == 0)
        def _():
            dk_acc[...] = jnp.zeros_like(dk_acc)
            dv_acc[...] = jnp.zeros_like(dv_acc)

        row, col = _sb_masks()
        m_prefix = _ones_where(row <= col)
        m_before = _ones_where(row < col)
        rows = len(chains) * SBQ
        strict = (lax.broadcasted_iota(jnp.int32, (rows, SBQ), 1)
                  < (lax.broadcasted_iota(jnp.int32, (rows, SBQ), 0) & (SBQ - 1)))
        q_pair, do_pair, ltot = [], [], []
        for p in range(npair):
            pl_ = slice(p * LANE, (p + 1) * LANE)
            q_all = q_ref[:, pl_]
            do_all = dy_ref[:, pl_].astype(BF16)
            q_pair.append(jnp.concatenate([jnp.where(_head_mask(h), q_all, jnp.zeros_like(q_all)) for h in range(2)], axis=0))
            do_pair.append(jnp.concatenate([jnp.where(_head_mask(h), do_all, jnp.zeros_like(do_all)) for h in range(2)], axis=0))
            ltot += [l_ref[p][:, h * 64:h * 64 + 1] for h in range(2)]
        ltot = jnp.concatenate(ltot, axis=0)

        def pair_rows(a, p):
            return a[2 * p * SBQ:2 * (p + 1) * SBQ]

        def block(kb, state, diag):
            seen, dseen, dq = state
            start = pl.multiple_of(kb * SBQ, SBQ)
            kk = [k_ref[pl.ds(start, SBQ), p * LANE:(p + 1) * LANE] for p in range(npair)]
            vv = [v_ref[pl.ds(start, SBQ), p * LANE:(p + 1) * LANE] for p in range(npair)]
            z = jnp.concatenate([_dg(q_pair[p], kk[p], NT) for p in range(npair)], axis=0) * scale
            lb = _log_sigmoid(z)
            lk = lb - z
            if diag:
                lk = jnp.where(strict, lk, 0.0)
            hi, lo = _split(lk)
            sums = _dg(jnp.concatenate([hi, lo], axis=0), m_prefix, NN)
            w = jnp.exp(lb + ((ltot - seen) - (sums[:rows] + sums[rows:])))
            if diag:
                w = jnp.where(strict, w, 0.0)
            wb = w.astype(BF16)
            da = w * jnp.concatenate([_dg(do_pair[p], vv[p], NT) for p in range(npair)], axis=0)
            dah, dal = _split(da)
            dsums = _dg(jnp.concatenate([dah, dal], axis=0), m_before, NN)
            sig = jnp.exp(lb)
            dz = (da * (1.0 - sig) - (dseen + dsums[:rows] + dsums[rows:]) * sig) * scale
            if diag:
                dz = jnp.where(strict, dz, 0.0)
            dzb = dz.astype(BF16)
            for p in range(npair):
                pl_ = slice(p * LANE, (p + 1) * LANE)
                dv_acc[pl.ds(start, SBQ), pl_] += _dg(pair_rows(wb, p), do_pair[p], TN)
                dk_acc[pl.ds(start, SBQ), pl_] += _dg(pair_rows(dzb, p), q_pair[p], TN)
            dq = dq + jnp.concatenate([_dg(pair_rows(dzb, p), kk[p], NN) for p in range(npair)], axis=0)
            seen = seen + jnp.sum(hi.astype(F32) + lo.astype(F32), axis=1, keepdims=True)
            dseen = dseen + jnp.sum(da, axis=1, keepdims=True)
            return seen, dseen, dq

        zero = (jnp.zeros((rows, 1), F32), jnp.zeros((rows, 1), F32), jnp.zeros((rows, LANE), F32))
        first = qi - n_ref[(grp // per_count) * nq + qi].astype(jnp.int32)
        state = lax.fori_loop(first, qi, lambda kb, c: block(kb, c, False), zero)
        _, _, dq = block(qi, state, True)
        for p in range(npair):
            dq_ref[:, p * LANE:(p + 1) * LANE] = jnp.where(
                _head_mask(0), dq[2 * p * SBQ:(2 * p + 1) * SBQ], dq[(2 * p + 1) * SBQ:(2 * p + 2) * SBQ]).astype(BF16)

        @pl.when(qi == nq - 1)
        def _():
            dk_ref[...] = dk_acc[...].astype(BF16)
            dv_ref[...] = dv_acc[...].astype(BF16)

        if nr:
            rider.ride(grid, riders, "finish")

    full = jax.ShapeDtypeStruct((t, 512), BF16)
    outs = pl.pallas_call(
        body, name="sb_bwd", grid=grid,
        in_specs=[_spec((SBQ, wide), lambda g, i: (i, g)),
                  _spec((t, wide), lambda g, i: (0, ngrp + g)),
                  _spec((t, wide), lambda g, i: (0, 2 * ngrp + g)),
                  _spec((SBQ, wide), lambda g, i: (i, ngrp + g)),
                  _spec((npair, SBQ, LANE), lambda g, i: (g, i, 0)),
                  pl.BlockSpec(memory_space=pltpu.SMEM)] + (rider.in_specs if nr else []),
        out_specs=[_spec((SBQ, wide), lambda g, i: (i, g)),
                   _spec((t, wide), lambda g, i: (0, g)), _spec((t, wide), lambda g, i: (0, g))]
        + (rider.out_specs if nr else []),
        out_shape=[full, full, full] + (rider.out_shape if nr else []),
        scratch_shapes=[pltpu.VMEM((t, wide), F32), pltpu.VMEM((t, wide), F32)] + (rider.scratch if nr else []),
        compiler_params=pltpu.CompilerParams(dimension_semantics=("arbitrary", "arbitrary"),
                                             vmem_limit_bytes=VMEM_LIMIT_BIG),
    )(proj_b, proj_b, proj_b, dy, ltot, nblk, *(rider.groups if nr else []))
    return outs[0], outs[1], outs[2], (rider.place(outs[3:]) if nr else [])


def _hgrn_gates(qr, fr, c0, c1):
    mx = jnp.maximum(c0, c1)
    e0, e1 = jnp.exp(c0 - mx), jnp.exp(c1 - mx)
    lb = e1 / (e0 + e1)
    sx = _sigmoid(fr)
    f = lb + (1.0 - lb) * sx
    k = (1.0 - lb) * (1.0 - sx)
    sq = _sigmoid(qr)
    return lb, sx, f, k, sq, qr * sq


def _chunk_sums(mask, x):
    n = x.shape[1]
    hi, lo = _split(x)
    both = _dg(_ones_where(mask), jnp.concatenate([hi, lo], axis=1), NN)
    return both[:, :n] + both[:, n:]


def _chunk_masks():
    row = lax.broadcasted_iota(jnp.int32, (CHUNK, CHUNK), 0)
    col = lax.broadcasted_iota(jnp.int32, (CHUNK, CHUNK), 1)
    return col <= row, col >= row


def _hgrn_fwd(proj_c, small, carry=None):
    t = proj_c.shape[1]
    nc = t // CHUNK
    nh = D // HD

    hps = HGRN_HPS
    wide = hps * HD
    grid = (nh // hps, nc)
    nr = carry.n if carry is not None else 0

    def body(*refs):
        p_ref, c0_ref, c1_ref, gam_ref = refs[:4]
        o_ref, y_ref, sst_ref = refs[4 + nr:7 + nr]
        st_ref = refs[7 + 2 * nr]
        riders = (refs[4:4 + nr], refs[7 + nr:7 + 2 * nr], refs[-2], refs[-1]) if nr else None
        if nr:
            carry.ride(grid, riders, "start")
        c = pl.program_id(1)

        @pl.when(c == 0)
        def _():
            st_ref[...] = jnp.zeros_like(st_ref)

        _, _, f_all, k_all, _, q_all = _hgrn_gates(p_ref[0], p_ref[1], c0_ref[...], c1_ref[...])
        tril, _ = _chunk_masks()
        b_all = _chunk_sums(tril, jnp.log(f_all))
        for j in range(hps):
            ln = slice(j * HD, (j + 1) * HD)
            st0 = st_ref[j]
            sst_ref[j] = st0
            v, g = p_ref[2, :, ln], p_ref[3, :, ln]
            q, k, b = q_all[:, ln], k_all[:, ln], b_all[:, ln]
            bm = b[CHUNK // 2 - 1:CHUNK // 2]
            bl = b[CHUNK - 1:CHUNK]
            qd = q * jnp.exp(b)
            qt = q * jnp.exp(b - bm)
            kt = k * jnp.exp(bm - b)
            kl = k * jnp.exp(bl - b)
            vb = v.astype(BF16)
            att = jnp.where(tril, _dot3(qt, kt, NT), 0.0)
            o = _dg(qd.astype(BF16), st0.astype(BF16), NT) + _dg(att.astype(BF16), vb, NN)
            st_ref[j] = st0 * jnp.exp(bl) + _dg(vb, kl.astype(BF16), TN)
            o_ref[:, ln] = o
            r = lax.rsqrt(jnp.mean(o * o, axis=-1, keepdims=True) + RMS_EPS)
            y_ref[:, ln] = (o * r * gam_ref[...] * (g * _sigmoid(g))).astype(BF16)
        if nr:
            carry.ride(grid, riders, "finish")

    outs = pl.pallas_call(
        body, name="hgrn_fwd", grid=grid,
        in_specs=[_spec((4, CHUNK, wide), lambda h, c: (0, c, h)),
                  _spec((None, 1, wide), lambda h, c: (R_CLB, 0, h)),
                  _spec((None, 1, wide), lambda h, c: (R_CLB + 1, 0, h)),
                  _spec((None, 1, HD), lambda h, c: (R_GAM, 0, 0))] + (carry.in_specs if nr else []),
        out_specs=[_spec((CHUNK, wide), lambda h, c: (c, h)), _spec((CHUNK, wide), lambda h, c: (c, h)),
                   _spec((None, hps, HD, HD), lambda h, c: (c, h, 0, 0))] + (carry.out_specs if nr else []),
        out_shape=[jax.ShapeDtypeStruct((t, D), F32), jax.ShapeDtypeStruct((t, D), BF16),
                   jax.ShapeDtypeStruct((nc, nh, HD, HD), F32)] + (carry.out_shape if nr else []),
        scratch_shapes=[pltpu.VMEM((hps, HD, HD), F32)] + (carry.scratch if nr else []),
        compiler_params=_params(("arbitrary", "arbitrary")),
    )(proj_c, small, small, small, *(carry.groups if nr else []))
    return outs[0], outs[1], outs[2], (carry.place(outs[3:]) if nr else [])


def _hgrn_bwd(proj_c, small, o, sst, dyc, rider=None):
    t = proj_c.shape[1]
    nc = t // CHUNK
    nh = D // HD

    hps = HGRN_HPS
    wide = hps * HD
    ng = nh // hps
    grid = (ng, nc)
    nr = rider.n if rider is not None else 0

    def body(*refs):
        p_ref, c0_ref, c1_ref, gam_ref, o_ref, sst_ref, dy_ref = refs[:7]
        dp_ref, dclb_ref, dgam_ref = refs[7 + nr:10 + nr]
        dst_ref, dlb_acc, dgam_acc = refs[10 + 2 * nr:13 + 2 * nr]
        riders = (refs[7:7 + nr], refs[10 + nr:10 + 2 * nr], refs[-2], refs[-1]) if nr else None
        if nr:
            rider.ride(grid, riders, "start")
        group = pl.program_id(0)
        step = pl.program_id(1)

        @pl.when(step == 0)
        def _():
            dst_ref[...] = jnp.zeros_like(dst_ref)
            dlb_acc[...] = jnp.zeros_like(dlb_acc)

        @pl.when((step == 0) & (group == 0))
        def _():
            dgam_acc[...] = jnp.zeros_like(dgam_acc)

        gam = gam_ref[...]
        qr_all = p_ref[0]
        lb_all, sx_all, f_all, k_all, sq_all, q_all = _hgrn_gates(qr_all, p_ref[1], c0_ref[...], c1_ref[...])
        tril, triu = _chunk_masks()
        b_all = _chunk_sums(tril, jnp.log(f_all))
        dq_parts, dk_parts, db_parts, dgam_parts = [], [], [], []
        for j in range(hps):
            ln = slice(j * HD, (j + 1) * HD)
            st0 = sst_ref[j]
            dst1 = dst_ref[j]
            v, g = p_ref[2, :, ln], p_ref[3, :, ln]
            q, k, b = q_all[:, ln], k_all[:, ln], b_all[:, ln]
            bm = b[CHUNK // 2 - 1:CHUNK // 2]
            bl = b[CHUNK - 1:CHUNK]
            eb = jnp.exp(b)
            e_qt = jnp.exp(b - bm)
            e_kt = jnp.exp(bm - b)
            e_kl = jnp.exp(bl - b)
            e_bl = jnp.exp(bl)
            qd, qt, kt, kl = q * eb, q * e_qt, k * e_kt, k * e_kl
            ov = o_ref[:, ln]
            r = lax.rsqrt(jnp.mean(ov * ov, axis=-1, keepdims=True) + RMS_EPS)
            oh = ov * r
            sg = _sigmoid(g)
            dy = dy_ref[:, ln]
            dp_ref[3, :, ln] = (dy * oh * gam * (sg * (1.0 + g * (1.0 - sg)))).astype(BF16)
            dyv = dy * (g * sg)
            gdy = dyv * gam
            do = (gdy - oh * jnp.mean(gdy * oh, axis=-1, keepdims=True)) * r
            dob, vb = do.astype(BF16), v.astype(BF16)
            st0b, dst1b = st0.astype(BF16), dst1.astype(BF16)
            st1 = st0 * e_bl + _dg(vb, kl.astype(BF16), TN)
            att = jnp.where(tril, _dot3(qt, kt, NT), 0.0)
            datt = jnp.where(tril, _dg(dob, vb, NT), 0.0)
            dv = _dg(att.astype(BF16), dob, TN) + _dg(kl.astype(BF16), dst1b, NT)
            dq = _dot3(datt, kt, NN) * e_qt + _dg(dob, st0b, NN) * eb
            dk = _dot3(datt, qt, TN) * e_kt + _dg(vb, dst1b, NN) * e_kl
            db = q * dq - k * dk
            last = lax.broadcasted_iota(jnp.int32, (CHUNK, 1), 0) == CHUNK - 1
            db = db + jnp.where(last, jnp.sum(dst1 * st1, axis=0, keepdims=True), 0.0)
            dst_ref[j] = dst1 * e_bl + _dg(dob, qd.astype(BF16), TN)
            dp_ref[2, :, ln] = dv.astype(BF16)
            dq_parts.append(dq)
            dk_parts.append(dk)
            db_parts.append(db)
            dgam_parts.append(jnp.sum(dyv * oh, axis=0, keepdims=True))

        dq_all, dk_all = jnp.concatenate(dq_parts, axis=1), jnp.concatenate(dk_parts, axis=1)
        dlf = _chunk_sums(triu, jnp.concatenate(db_parts, axis=1))
        dp_ref[0] = (dq_all * (sq_all * (1.0 + qr_all * (1.0 - sq_all)))).astype(BF16)
        tmp = dlf / f_all - dk_all
        dp_ref[1] = (tmp * (1.0 - lb_all) * sx_all * (1.0 - sx_all)).astype(BF16)
        dlb_acc[...] += jnp.sum((1.0 - sx_all) * tmp, axis=0, keepdims=True)
        dgam_acc[...] += functools.reduce(lambda a, b: a + b, dgam_parts)

        @pl.when(step == nc - 1)
        def _():
            d1 = dlb_acc[...] * lb_all * (1.0 - lb_all)
            dclb_ref[...] = jnp.where(lax.broadcasted_iota(jnp.int32, (2, wide), 0) == 0, -d1, d1)

        @pl.when((step == nc - 1) & (group == ng - 1))
        def _():
            dgam_ref[...] = dgam_acc[...]

        if nr:
            rider.ride(grid, riders, "finish")

    rev = lambda h, s: (nc - 1 - s, h)
    outs = pl.pallas_call(
        body, name="hgrn_bwd", grid=grid,
        in_specs=[_spec((4, CHUNK, wide), lambda h, s: (0, nc - 1 - s, h)),
                  _spec((None, 1, wide), lambda h, s: (R_CLB, 0, h)),
                  _spec((None, 1, wide), lambda h, s: (R_CLB + 1, 0, h)),
                  _spec((None, 1, HD), lambda h, s: (R_GAM, 0, 0)),
                  _spec((CHUNK, wide), rev),
                  _spec((None, hps, HD, HD), lambda h, s: (nc - 1 - s, h, 0, 0)),
                  _spec((CHUNK, wide), rev)] + (rider.in_specs if nr else []),
        out_specs=[_spec((4, CHUNK, wide), lambda h, s: (0, nc - 1 - s, h)),
                   _spec((2, wide), lambda h, s: (0, h)),
                   _spec((1, HD), lambda h, s: (0, 0))] + (rider.out_specs if nr else []),
        out_shape=[jax.ShapeDtypeStruct((4, t, D), BF16), jax.ShapeDtypeStruct((2, D), F32),
                   jax.ShapeDtypeStruct((1, HD), F32)] + (rider.out_shape if nr else []),
        scratch_shapes=[pltpu.VMEM((hps, HD, HD), F32), pltpu.VMEM((1, wide), F32), pltpu.VMEM((1, HD), F32)]
        + (rider.scratch if nr else []),
        compiler_params=_params(("arbitrary", "arbitrary")),
    )(proj_c, small, small, small, o, sst, dyc, *(rider.groups if nr else []))
    return outs[0], outs[1], outs[2], (rider.place(outs[3:]) if nr else [])


def _adamw(name, w, grads, m, v):
    nl = len(grads)
    rows, cols = grads[0].shape
    br = rows
    for cand in (512, 352, 256):
        if rows % cand == 0:
            br = cand
            break
    nb = rows // br
    c1 = 1.0 - ADAM_B1 ** ADAM_STEP
    c2 = 1.0 - ADAM_B2 ** ADAM_STEP

    def body(w_ref, m_ref, v_ref, *refs):
        g_refs, (d_ref, mo_ref, vo_ref, go_ref) = refs[:nl], refs[nl:]
        layer = pl.program_id(0)
        gv = g_refs[0][...]
        for k in range(1, nl):
            gv = jnp.where(layer == k, g_refs[k][...], gv)
        mn = ADAM_B1 * m_ref[...] + (1.0 - ADAM_B1) * gv
        vn = ADAM_B2 * v_ref[...] + (1.0 - ADAM_B2) * (gv * gv)
        mo_ref[...] = mn
        vo_ref[...] = vn
        go_ref[...] = gv
        d_ref[...] = -ADAM_LR * ((mn / c1) / (jnp.sqrt(vn / c2) + ADAM_EPS) + ADAM_WD * w_ref[...])

    blk = _spec((br, cols), lambda l, i: (l * nb + i, 0))
    g_specs = [_spec((br, cols), lambda l, i, k=k: (jnp.where(l == k, i, 0), 0)) for k in range(nl)]
    shape = jax.ShapeDtypeStruct((nl * rows, cols), F32)
    return pl.pallas_call(
        body, name=name, grid=(nl, nb), in_specs=[blk] * 3 + g_specs, out_specs=[blk] * 4, out_shape=[shape] * 4,
        compiler_params=_params(("arbitrary", "arbitrary")),
    )(w, m, v, *grads)


def _place():
    x, y, c = lax.axis_index("x"), lax.axis_index("y"), lax.axis_index("c")
    chips = [(1 - x, y), (x, 1 - y), (1 - x, 1 - y)]
    return x, y, c, chips


class _Rider:
    n = 0
    relay = None

    def ride(self, grid, refs, when):
        if not self.n:
            return
        ids = [pl.program_id(a) for a in range(len(grid))]
        first = functools.reduce(jnp.logical_and, [i == 0 for i in ids])
        last = functools.reduce(jnp.logical_and, [i == g - 1 for i, g in zip(ids, grid)])
        phases = [(first, self.start), (last, self.relay)] if when == "start" else [(last, self.finish)]
        for cond, phase in phases:
            if phase is not None:
                pl.when(cond)(functools.partial(phase, *refs))


class _Gather(_Rider):
    PER_GROUP = 7

    def __init__(self, groups):
        self.groups = list(groups)
        self.n = len(self.groups)
        self.in_specs = [ANY] * self.n
        self.out_specs = [ANY] * self.n
        self.out_shape = [jax.ShapeDtypeStruct((NSH,) + g.shape, g.dtype) for g in self.groups]
        sems = pltpu.SemaphoreType.DMA((self.PER_GROUP * self.n,))
        self.scratch = [sems, sems] if self.n else []

    def _copies(self, ins, outs, send, recv):
        x, y, c, chips = _place()
        sibling = (x, y, 1 - c)

        def half(gi, chip, hc):
            return outs[gi].at[2 * chip[0] + chip[1], hc]

        def copy(gi, k, src, dst, to):
            sem = self.PER_GROUP * gi + k
            return pltpu.make_async_remote_copy(src_ref=src, dst_ref=dst, send_sem=send.at[sem], recv_sem=recv.at[sem],
                                                device_id=to, device_id_type=MESH)

        pairs = [(gi, j, chip) for gi in range(self.n) for j, chip in enumerate(chips)]
        first = [copy(gi, j, ins[gi].at[c], half(gi, (x, y), c), (*chip, c)) for gi, j, chip in pairs]
        first += [copy(gi, 6, ins[gi], outs[gi].at[2 * x + y], sibling) for gi in range(self.n)]
        landed = [copy(gi, j, half(gi, chip, c), half(gi, chip, c), sibling) for gi, j, chip in pairs]
        relay = [copy(gi, 3 + j, half(gi, chip, c), half(gi, chip, c), sibling) for gi, j, chip in pairs]
        relayed = [copy(gi, 3 + j, half(gi, chip, 1 - c), half(gi, chip, 1 - c), sibling) for gi, j, chip in pairs]
        relayed += [copy(gi, 6, ins[gi], outs[gi].at[2 * x + y], sibling) for gi in range(self.n)]
        return first, landed, relay, relayed

    def start(self, ins, outs, send, recv):
        for cp in self._copies(ins, outs, send, recv)[0]:
            cp.start()

    def relay(self, ins, outs, send, recv):
        _, landed, relay, _ = self._copies(ins, outs, send, recv)
        for arrived, onward in zip(landed, relay):
            arrived.wait_recv()
            onward.start()

    def finish(self, ins, outs, send, recv):
        first, _, relay, relayed = self._copies(ins, outs, send, recv)
        for cp in relayed:
            cp.wait_recv()
        for cp in first + relay:
            cp.wait_send()

    def place(self, outs):
        return list(outs)


def _alone(name, rider):
    n = rider.n

    def body(*refs):
        parts = (refs[:n], refs[n:2 * n], refs[2 * n], refs[2 * n + 1])
        rider.start(*parts)
        if rider.relay is not None:
            rider.relay(*parts)
        rider.finish(*parts)

    outs = pl.pallas_call(
        body, name=name, in_specs=rider.in_specs, out_specs=rider.out_specs, out_shape=rider.out_shape,
        scratch_shapes=rider.scratch, compiler_params=pltpu.CompilerParams(has_side_effects=True),
    )(*rider.groups)
    return rider.place(outs)


class _Swap(_Rider):
    def __init__(self, slots):
        self.slots = list(slots)
        self.groups = [buf for buf, _, _ in self.slots]
        self.n = len(self.slots)
        self.in_specs = [ANY] * self.n
        self.out_specs = [ANY] * self.n
        self.out_shape = [jax.ShapeDtypeStruct((NSH, rows // 2, buf.shape[2]), buf.dtype) for buf, _, rows in self.slots]
        self.scratch = [pltpu.SemaphoreType.DMA((self.n,)), pltpu.SemaphoreType.DMA((self.n,))]

    def _copies(self, ins, outs, send, recv):
        x, y, c, _ = _place()
        cps = []
        for i, (_, row0, rows) in enumerate(self.slots):
            half = rows // 2
            src = ins[i].at[:, pl.ds(pl.multiple_of(row0 + (1 - c) * half, 16), half)]
            cps.append(pltpu.make_async_remote_copy(src_ref=src, dst_ref=outs[i], send_sem=send.at[i],
                                                    recv_sem=recv.at[i], device_id=(x, y, 1 - c), device_id_type=MESH))
        return cps

    def start(self, ins, outs, send, recv):
        for cp in self._copies(ins, outs, send, recv):
            cp.start()

    def finish(self, ins, outs, send, recv):
        for cp in self._copies(ins, outs, send, recv):
            cp.wait()

    def place(self, outs):
        return list(outs)


class _Share(_Rider):
    def __init__(self, arrays):
        self.groups = list(arrays)
        self.n = len(self.groups)
        self.in_specs = [ANY] * self.n
        self.out_specs = [ANY] * self.n
        self.out_shape = [jax.ShapeDtypeStruct((2,) + g.shape, g.dtype) for g in self.groups]
        self.scratch = [pltpu.SemaphoreType.DMA((self.n,)), pltpu.SemaphoreType.DMA((self.n,))]

    def _copies(self, ins, outs, send, recv, half):
        x, y, c, _ = _place()
        return [pltpu.make_async_remote_copy(src_ref=ins[gi], dst_ref=outs[gi].at[c if half == "mine" else 1 - c],
                                             send_sem=send.at[gi], recv_sem=recv.at[gi], device_id=(x, y, 1 - c),
                                             device_id_type=MESH) for gi in range(self.n)]

    def start(self, ins, outs, send, recv):
        for cp in self._copies(ins, outs, send, recv, "mine"):
            cp.start()

    def finish(self, ins, outs, send, recv):
        for cp in self._copies(ins, outs, send, recv, "theirs"):
            cp.wait_recv()
        for cp in self._copies(ins, outs, send, recv, "mine"):
            cp.wait_send()

    def place(self, outs):
        c = lax.axis_index("c")
        return [lax.dynamic_update_index_in_dim(o, g, c, 0) for o, g in zip(outs, self.groups)]


class _Send(_Rider):
    def __init__(self, arrays):
        self.groups = list(arrays)
        self.n = len(self.groups)
        self.in_specs = [ANY] * self.n
        self.out_specs = [ANY] * self.n
        self.out_shape = [jax.ShapeDtypeStruct((3,) + g.shape[1:], g.dtype) for g in self.groups]
        self.scratch = [pltpu.SemaphoreType.DMA((3 * self.n,)), pltpu.SemaphoreType.DMA((3 * self.n,))]

    def _copies(self, ins, outs, send, recv):
        x, y, c, chips = _place()
        return [pltpu.make_async_remote_copy(src_ref=ins[gi].at[2 * chip[0] + chip[1]], dst_ref=outs[gi].at[j],
                                             send_sem=send.at[3 * gi + j], recv_sem=recv.at[3 * gi + j],
                                             device_id=(*chip, c), device_id_type=MESH)
                for gi in range(self.n) for j, chip in enumerate(chips)]

    def start(self, ins, outs, send, recv):
        for cp in self._copies(ins, outs, send, recv):
            cp.start()

    def finish(self, ins, outs, send, recv):
        for cp in self._copies(ins, outs, send, recv):
            cp.wait()

    def place(self, outs):
        return list(outs)


def _pair_sum(name, slots, got, c_idx):
    n = len(slots)
    in_specs, out_specs, out_shape, operands = [], [], [], []
    for (buf, row0, rows), g in zip(slots, got):
        hb, cols = rows // 4, buf.shape[2]
        in_specs += [pl.BlockSpec((None, hb, cols), lambda q, i, cr, r=row0 // hb: (q, r + 2 * cr[0] + i, 0)),
                     pl.BlockSpec((None, hb, cols), lambda q, i, cr: (q, i, 0))]
        out_specs.append(pl.BlockSpec((None, hb, cols), lambda q, i, cr: (q, i, 0)))
        out_shape.append(jax.ShapeDtypeStruct(g.shape, BF16))
        operands += [buf, g]

    def body(c_ref, *refs):
        for i in range(n):
            refs[2 * n + i][...] = (refs[2 * i][...].astype(F32) + refs[2 * i + 1][...].astype(F32)).astype(BF16)

    return pl.pallas_call(
        body, name=name,
        grid_spec=pltpu.PrefetchScalarGridSpec(num_scalar_prefetch=1, grid=(NSH, 2), in_specs=in_specs, out_specs=out_specs),
        out_shape=out_shape, compiler_params=_params(("parallel", "parallel")),
    )(c_idx, *operands)


def _owner_sum(name, pairs, got, p_idx):
    n = len(pairs)
    in_specs, out_specs, out_shape, operands = [], [], [], []
    for own, g in zip(pairs, got):
        _, rows, cols = own.shape
        hb = rows // 2
        in_specs += [pl.BlockSpec((None, hb, cols), lambda i, pr: (pr[0], i, 0)),
                     pl.BlockSpec((3, hb, cols), lambda i, pr: (0, i, 0))]
        out_specs.append(pl.BlockSpec((hb, cols), lambda i, pr: (i, 0)))
        out_shape.append(jax.ShapeDtypeStruct((rows, cols), F32))
        operands += [own, g]

    def body(p_ref, *refs):
        for i in range(n):
            a_ref, b_ref = refs[2 * i], refs[2 * i + 1]
            refs[2 * n + i][...] = ((a_ref[...].astype(F32) + b_ref[0].astype(F32)) + b_ref[1].astype(F32)) + b_ref[2].astype(F32)

    return pl.pallas_call(
        body, name=name,
        grid_spec=pltpu.PrefetchScalarGridSpec(num_scalar_prefetch=1, grid=(2,), in_specs=in_specs, out_specs=out_specs),
        out_shape=out_shape, compiler_params=_params(("parallel",)),
    )(p_idx, *operands)


def _sum_small(slab):
    def body(in_ref, out_ref, all_ref, send, recv):
        x, y, c, _ = _place()
        me = 4 * x + 2 * y + c
        all_ref[me] = in_ref[...]
        cps = []
        for k in range(1, 8):
            peer = (x ^ (k >> 2), y ^ ((k >> 1) & 1), c ^ (k & 1))
            cps.append(pltpu.make_async_remote_copy(src_ref=in_ref, dst_ref=all_ref.at[me], send_sem=send.at[k - 1],
                                                    recv_sem=recv.at[k - 1], device_id=peer, device_id_type=MESH))
        for cp in cps:
            cp.start()
        for cp in cps:
            cp.wait()
        total = all_ref[0]
        for d in range(1, 8):
            total = total + all_ref[d]
        out_ref[...] = total

    return pl.pallas_call(
        body, name="sum_small",
        in_specs=[pl.BlockSpec(memory_space=pltpu.VMEM)], out_specs=pl.BlockSpec(memory_space=pltpu.VMEM),
        out_shape=jax.ShapeDtypeStruct(slab.shape, F32),
        scratch_shapes=[pltpu.VMEM((8,) + slab.shape, F32), pltpu.SemaphoreType.DMA((7,)), pltpu.SemaphoreType.DMA((7,))],
        compiler_params=pltpu.CompilerParams(has_side_effects=True),
    )(slab)


FFNS = ("pre0", "post0", "pre1", "post1")
W_SHAPES = dict({f + "_gu": (NSH, 2, FS, D) for f in FFNS}, **{f + "_d": (NSH, FS, D) for f in FFNS},
                ab_in=(NSH, D, 768), ab_out=(NSH, 256, D), conv=(NSH, 2, 8, LANE), c_in=(NSH, D, D), c_out=(NSH, 256, D))
CARRIED = dict(pre0_norm=("pre0_gu",), pre0_up=("pre0_d",), pre0_down=("ab_in", "ab_out", "conv"),
               sb_fwd=("post0_gu", "post0_d"), post0_up=("pre1_gu",), post0_down=("pre1_d",),
               pre1_up=("c_in", "c_out"), hgrn_fwd=("post1_gu", "post1_d"))


FFN_SLOTS = dict(pre0=(0, 2, 0), pre1=(1, 3, 1), post0=(4, 6, 2), post1=(5, 7, 3))
GRAD_SLOTS = dict(ab_out=("b", B_ABOUT, 256), c_in=("b", B_CIN, D), c_out=("b", B_COUT, 256), ab_in=("c", 0, D))
for _f, (_g, _u, _d) in FFN_SLOTS.items():
    GRAD_SLOTS.update({_f + "_g": ("a", _g * FS, FS), _f + "_u": ("a", _u * FS, FS), _f + "_d": ("b", _d * FS, FS)})
REDUCE_STAGES = dict(x=("post1_g", "post1_u", "post1_d", "c_out"), w=("c_in", "pre1_d"),
                     y=("pre1_g", "pre1_u", "post0_g", "post0_u", "post0_d", "ab_out"),
                     z1=("ab_in", "pre0_d"), z2=("pre0_g", "pre0_u"))


def _local_step(x, target, weights, small, shards=None, place=None):
    t = x.shape[0]
    tm = min(MM_TILE, t)
    tw = min(WGRAD_TILE, t)
    nt = t // tm
    tok_si = _spec((tm, D), lambda s, i: (i, 0))
    w = dict(weights)
    reduced = {}

    def stage_slots(stage, buffers):
        return [(buffers[GRAD_SLOTS[n][0]],) + GRAD_SLOTS[n][1:] for n in REDUCE_STAGES[stage]]

    def swap_rider(stage, buffers):
        return _Swap(stage_slots(stage, buffers)) if place is not None else None

    def reduce_start(stage, buffers, swapped=None):
        if place is None:
            return [], None
        slots = stage_slots(stage, buffers)
        if swapped is None:
            swapped = _alone("grad_swap_" + stage, _Swap(slots))
        pairs = _pair_sum("grad_pair_sum_" + stage, slots, swapped, place[0])
        return pairs, _Send(pairs)

    def reduce_end(stage, pairs, landed):
        if place is None:
            return None
        mine = _owner_sum("grad_owner_sum_" + stage, pairs, landed, place[1])
        return _Share(mine)

    def shared(stage, landed):
        if place is not None:
            reduced.update(zip(REDUCE_STAGES[stage], landed))

    def carried(kernel_name):
        names = [n for n in CARRIED[kernel_name] if n not in w]
        return names, (_Gather([shards[n] for n in names]) if names else None)

    def land(names, arrays):
        for n, a in zip(names, arrays):
            w[n] = a.reshape(W_SHAPES[n])

    def ffn_forward(tag, h, hn, next_row):
        names, gather = carried(tag + "_up") if tag + "_up" in CARRIED else ([], None)
        s_up, s_gate, a, got = _ffn_up(tag + "_up", hn, w[tag + "_gu"], gather)
        land(names, got)
        names, gather = carried(tag + "_down") if tag + "_down" in CARRIED else ([], None)
        out = _ffn_down(tag + "_down", a, w[tag + "_d"], h, gather, (small, next_row) if next_row is not None else None)
        if gather is not None:
            out, got = out
            land(names, got)
        out, hn_next = out if next_row is not None else (out, None)
        return out, hn_next, (h, hn, s_up, s_gate, a)

    def out_proj(name, y, w_out, h, next_row):
        return _mm(name, [(y, _spec((tm, 256), lambda i, k: (i, k)), w_out, _spec((None, 256, D), lambda i, k: (k, 0, 0)))],
                   grid=(nt, NSH), o_shape=(t, D), o_dtype=F32, o_spec=_spec((tm, D), lambda i, k: (i, 0)),
                   dims=NN, kaxis=1, nk=NSH, acc_shape=(tm, D), res=(h, _spec((tm, D), lambda i, k: (i, 0))),
                   norm=(small, next_row))

    def out_proj_bwd(tag, dhb, y, w_out, blk, grad_b, make_rider=None):
        grad_b = _wgrad(tag + "_dwout", y, _spec((tw, 256), lambda s, k: (k, s)), dhb, _spec((tw, D), lambda s, k: (k, 0)),
                        256, D, t, tw, grad_b, blk)
        rider = make_rider(grad_b) if make_rider is not None else None
        dy = _mm(tag + "_dy", [(dhb, tok_si, w_out, _spec((None, 256, D), lambda s, i: (s, 0, 0)))],
                 grid=(NSH, nt), o_shape=(t, D), o_dtype=F32, o_spec=_spec((tm, 256), lambda s, i: (i, s)),
                 dims=NT, kaxis=1, nk=1, carry=rider)
        dy, landed = dy if rider is not None else (dy, None)
        return dy, grad_b, landed

    h0 = x
    names, gather = carried("pre0_norm")
    hn, got = _norm_fwd("pre0_norm", h0, small, R_PRE, gather)
    land(names, got)
    h1, hn_ab, pre0 = ffn_forward("pre0", h0, hn, R_MIX)
    proj_a = _mm("ab_proj_a", [(hn_ab, tok_si, w["ab_in"], _spec((None, D, 768), lambda s, i: (s, 0, 0)))],
                 grid=(2, nt), o_shape=(t, 1536), o_dtype=F32, o_spec=_spec((tm, 768), lambda s, i: (i, s)),
                 dims=NN, kaxis=1, nk=1)
    proj_b = _mm("ab_proj_b", [(hn_ab, tok_si, w["ab_in"], _spec((None, D, 768), lambda s, i: (s + 2, 0, 0)))],
                 grid=(2, nt), o_shape=(t, 1536), o_dtype=BF16, o_spec=_spec((tm, 768), lambda s, i: (i, s)),
                 dims=NN, kaxis=1, nk=1)
    y_a = _conv_fwd(proj_a, w["conv"])
    names, gather = carried("sb_fwd")
    y_b, ltot, nblk, got = _sb_fwd(proj_b, gather)
    land(names, got)
    y_ab = jnp.concatenate([y_a, y_b], axis=1)
    h2, hn = out_proj("ab_out", y_ab, w["ab_out"], h1, R_POST)
    h3, hn, post0 = ffn_forward("post0", h2, hn, R_PRE + 1)
    h4, hn_c, pre1 = ffn_forward("pre1", h3, hn, R_MIX + 1)
    proj_c = _mm("c_proj", [(hn_c, tok_si, w["c_in"], _spec((None, D, D), lambda s, i: (s, 0, 0)))],
                 grid=(NSH, nt), o_shape=(NSH, t, D), o_dtype=F32, o_spec=_spec((None, tm, D), lambda s, i: (s, i, 0)),
                 dims=NN, kaxis=1, nk=1)
    names, gather = carried("hgrn_fwd")
    o_c, y_c, sst, got = _hgrn_fwd(proj_c, small, gather)
    land(names, got)
    h5, hn = out_proj("c_out", y_c, w["c_out"], h4, R_POST + 1)
    h6, _, post1 = ffn_forward("post1", h5, hn, None)
    dh, dhb, d_fin, loss = _final_loss(h6, small, target)

    dh, dhb, grad_a, grad_b, dn_post1, _, _ = _ffn_backward("post1", dh, dhb, *post1, w["post1_gu"], w["post1_d"],
                                                            *FFN_SLOTS["post1"], small, R_POST + 1, 8 * FS, B_ROWS)
    dy_c, grad_b, swapped = out_proj_bwd("c", dhb, y_c, w["c_out"], B_COUT // 256, grad_b,
                                         lambda buf_b: swap_rider("x", dict(a=grad_a, b=buf_b)))
    pairs, rider = reduce_start("x", dict(a=grad_a, b=grad_b), swapped)
    dproj_c, d_clb, d_gam, landed = _hgrn_bwd(proj_c, small, o_c, sst, dy_c, rider)
    share = reduce_end("x", pairs, landed)
    grad_b = _wgrad("c_dwin", hn_c, _spec((tw, D), lambda s, k: (k, 0)), dproj_c, _spec((None, tw, D), lambda s, k: (s, k, 0)),
                    D, D, t, tw, grad_b, B_CIN // D, carry=share)
    if share is not None:
        grad_b, landed = grad_b
        shared("x", landed)
    dh, dhb, dn_mix1, _ = _dhn_norm("c_dhn", [(dproj_c, _spec((None, tm, D), lambda i, k: (k, i, 0)),
                                               w["c_in"], _spec((None, D, D), lambda i, k: (k, 0, 0)))],
                                    NT, h4, small, R_MIX + 1, dh)
    stage_w = {}

    def w_swap(buf_a, buf_b):
        return swap_rider("w", dict(a=buf_a, b=buf_b))

    def w_send(buf_a, buf_b, swapped):
        stage_w["pairs"], rider = reduce_start("w", dict(a=buf_a, b=buf_b), swapped)
        return rider

    on = place is not None
    dh, dhb, grad_a, grad_b, dn_pre1, _, landed = _ffn_backward(
        "pre1", dh, dhb, *pre1, w["pre1_gu"], w["pre1_d"], *FFN_SLOTS["pre1"], small, R_PRE + 1, grad_a, grad_b,
        w_swap if on else None, w_send if on else None)
    share = reduce_end("w", stage_w.get("pairs"), landed)
    dh, dhb, grad_a, grad_b, dn_post0, landed, _ = _ffn_backward(
        "post0", dh, dhb, *post0, w["post0_gu"], w["post0_d"], *FFN_SLOTS["post0"], small, R_POST, grad_a, grad_b,
        (lambda buf_a, buf_b: share) if on else None)
    shared("w", landed)
    dy_ab, grad_b, _ = out_proj_bwd("ab", dhb, y_ab, w["ab_out"], B_ABOUT // 256, grad_b)
    dab, dac, dax, d_conv, swapped = _conv_bwd(proj_a, w["conv"], dy_ab, swap_rider("y", dict(a=grad_a, b=grad_b)))
    pairs, rider = reduce_start("y", dict(a=grad_a, b=grad_b), swapped)
    dq, dk, dv, landed = _sb_bwd(proj_b, dy_ab, ltot, nblk, rider)
    share = reduce_end("y", pairs, landed)
    dproj_ab = jnp.concatenate([dab, dac, dax, dq, dk, dv], axis=1)
    grad_c = _wgrad("ab_dwin", hn_ab, _spec((tw, D), lambda s, k: (k, 0)), dproj_ab, _spec((tw, 768), lambda s, k: (k, s)),
                    D, 768, t, tw, D, 0, carry=share)
    if share is not None:
        grad_c, landed = grad_c
        shared("y", landed)
    dh, dhb, dn_mix0, _ = _dhn_norm("ab_dhn", [(dproj_ab, _spec((tm, 768), lambda i, k: (i, k)),
                                                w["ab_in"], _spec((None, D, 768), lambda i, k: (k, 0, 0)))],
                                    NT, h1, small, R_MIX, dh)
    last = {}

    def z1_send(buf_a, buf_b):
        last["z1"], rider = reduce_start("z1", dict(b=buf_b, c=grad_c))
        return rider

    def z2_send(buf_a, buf_b, _):
        last["z2"], rider = reduce_start("z2", dict(a=buf_a, b=buf_b))
        return rider

    dh, dhb, grad_a, grad_b, dn_pre0, landed, landed_own = _ffn_backward(
        "pre0", dh, dhb, *pre0, w["pre0_gu"], w["pre0_d"], *FFN_SLOTS["pre0"], small, R_PRE, grad_a, grad_b,
        z1_send if on else None, z2_send if on else None)
    if place is not None:
        mine = (_owner_sum("grad_owner_sum_z1", last["z1"], landed, place[1])
                + _owner_sum("grad_owner_sum_z2", last["z2"], landed_own, place[1]))
        reduced.update(zip(REDUCE_STAGES["z1"] + REDUCE_STAGES["z2"], _alone("grad_share_z", _Share(mine))))
    zero = jnp.zeros((1, D), F32)
    conv_rows = jnp.pad(jnp.transpose(d_conv[:, :3, :], (1, 0, 2)).reshape(3, 512), ((0, 0), (0, D - 512)))
    small_grad = jnp.concatenate([
        dn_pre0, dn_pre1, dn_mix0, dn_mix1, dn_post0, dn_post1, d_clb, d_fin,
        jnp.pad(d_gam, ((0, 0), (0, D - HD))), conv_rows,
        jnp.pad(loss, ((0, 0), (0, D - 1))), zero, zero], axis=0)
    return dh, grad_a, grad_b, grad_c, small_grad, reduced


def _small_slab(rows):
    parts = [jnp.pad(r.astype(F32), ((0, 0), (0, D - r.shape[1]))) for r in rows]
    slab = jnp.concatenate(parts, axis=0)
    return jnp.pad(slab, ((0, SMALL_ROWS - slab.shape[0]), (0, 0)))


def kernel(x, ffn_pre_norm, ffn_pre_w_gate, ffn_pre_w_up, ffn_pre_w_down, mix_norm, ffn_post_norm, ffn_post_w_gate, ffn_post_w_up, ffn_post_w_down, ab_w_in, ab_conv_w, ab_w_out, c_w_in, c_lower_bounds, c_out_norm, c_w_out, final_norm, loss_target, m_ffn_pre_norm, m_ffn_pre_w_gate, m_ffn_pre_w_up, m_ffn_pre_w_down, m_mix_norm, m_ffn_post_norm, m_ffn_post_w_gate, m_ffn_post_w_up, m_ffn_post_w_down, m_ab_w_in, m_ab_conv_w, m_ab_w_out, m_c_w_in, m_c_lower_bounds, m_c_out_norm, m_c_w_out, m_final_norm, v_ffn_pre_norm, v_ffn_pre_w_gate, v_ffn_pre_w_up, v_ffn_pre_w_down, v_mix_norm, v_ffn_post_norm, v_ffn_post_w_gate, v_ffn_post_w_up, v_ffn_post_w_down, v_ab_w_in, v_ab_conv_w, v_ab_w_out, v_c_w_in, v_c_lower_bounds, v_c_out_norm, v_c_w_out, v_final_norm):
    t = x.shape[1]
    xi, yi, ci = lax.axis_index("x"), lax.axis_index("y"), lax.axis_index("c")
    p_idx = (2 * xi + yi).astype(jnp.int32).reshape(1)
    c_idx = ci.astype(jnp.int32).reshape(1)

    def halves(m):
        return m.astype(BF16).reshape(2, m.shape[0] // 2, m.shape[1])

    transposed = ("ffn_pre_w_gate", "ffn_pre_w_up", "ffn_post_w_gate", "ffn_post_w_up")

    def flip(a):
        return jnp.swapaxes(a, 1, 2)

    shards = {}
    for name, (w_gate, w_up, w_down, layer) in dict(
            pre0=(ffn_pre_w_gate, ffn_pre_w_up, ffn_pre_w_down, 0), post0=(ffn_post_w_gate, ffn_post_w_up, ffn_post_w_down, 0),
            pre1=(ffn_pre_w_gate, ffn_pre_w_up, ffn_pre_w_down, 1), post1=(ffn_post_w_gate, ffn_post_w_up, ffn_post_w_down, 1)).items():
        shards[name + "_gu"] = jnp.stack([flip(w_gate)[layer], flip(w_up)[layer]]).astype(BF16)
        shards[name + "_d"] = halves(w_down[layer])
    conv_pad = jnp.pad(ab_conv_w[0], ((0, 5), (0, 0)))
    shards.update(ab_in=halves(ab_w_in[0]), ab_out=halves(ab_w_out[0]), c_in=halves(c_w_in[0]), c_out=halves(c_w_out[0]),
                  conv=jnp.stack([conv_pad, jnp.zeros_like(conv_pad)]))
    small = _small_slab([ffn_pre_norm, mix_norm, ffn_post_norm, c_lower_bounds, final_norm.reshape(1, D), c_out_norm])
    small = small.reshape(SMALL_ROWS, 1, D)

    grad_x, _, _, _, small_grad, reduced = _local_step(x[0], loss_target[0], {}, small, shards,
                                                        (c_idx, p_idx))
    whole = {n: g.reshape(2 * g.shape[1], g.shape[2]) for n, g in reduced.items()}
    small_sum = _sum_small(small_grad)

    my_conv = lax.dynamic_slice(small_sum[R_CONV:R_CONV + 3], (0, (2 * xi + yi) * 128), (3, 128))
    grads = {
        "ffn_pre_norm": small_sum[R_PRE:R_PRE + 2], "mix_norm": small_sum[R_MIX:R_MIX + 2],
        "ffn_post_norm": small_sum[R_POST:R_POST + 2], "c_lower_bounds": small_sum[R_CLB:R_CLB + 2],
        "c_out_norm": small_sum[R_GAM:R_GAM + 1, :HD], "final_norm": small_sum[R_FIN],
        "ab_conv_w": my_conv.reshape(1, 3, 128),
    }
    layers = dict(ab_w_in=[whole["ab_in"]], ab_w_out=[whole["ab_out"]], c_w_in=[whole["c_in"]], c_w_out=[whole["c_out"]])
    for kind, key in (("gate", "_g"), ("up", "_u"), ("down", "_d")):
        layers["ffn_pre_w_" + kind] = [whole["pre0" + key], whole["pre1" + key]]
        layers["ffn_post_w_" + kind] = [whole["post0" + key], whole["post1" + key]]
    weights = dict(ffn_pre_norm=ffn_pre_norm, ffn_pre_w_gate=ffn_pre_w_gate, ffn_pre_w_up=ffn_pre_w_up, ffn_pre_w_down=ffn_pre_w_down, mix_norm=mix_norm, ffn_post_norm=ffn_post_norm, ffn_post_w_gate=ffn_post_w_gate, ffn_post_w_up=ffn_post_w_up, ffn_post_w_down=ffn_post_w_down, ab_w_in=ab_w_in, ab_conv_w=ab_conv_w, ab_w_out=ab_w_out, c_w_in=c_w_in, c_lower_bounds=c_lower_bounds, c_out_norm=c_out_norm, c_w_out=c_w_out, final_norm=final_norm)
    m_in = dict(ffn_pre_norm=m_ffn_pre_norm, ffn_pre_w_gate=m_ffn_pre_w_gate, ffn_pre_w_up=m_ffn_pre_w_up, ffn_pre_w_down=m_ffn_pre_w_down, mix_norm=m_mix_norm, ffn_post_norm=m_ffn_post_norm, ffn_post_w_gate=m_ffn_post_w_gate, ffn_post_w_up=m_ffn_post_w_up, ffn_post_w_down=m_ffn_post_w_down, ab_w_in=m_ab_w_in, ab_conv_w=m_ab_conv_w, ab_w_out=m_ab_w_out, c_w_in=m_c_w_in, c_lower_bounds=m_c_lower_bounds, c_out_norm=m_c_out_norm, c_w_out=m_c_w_out, final_norm=m_final_norm)
    v_in = dict(ffn_pre_norm=v_ffn_pre_norm, ffn_pre_w_gate=v_ffn_pre_w_gate, ffn_pre_w_up=v_ffn_pre_w_up, ffn_pre_w_down=v_ffn_pre_w_down, mix_norm=v_mix_norm, ffn_post_norm=v_ffn_post_norm, ffn_post_w_gate=v_ffn_post_w_gate, ffn_post_w_up=v_ffn_post_w_up, ffn_post_w_down=v_ffn_post_w_down, ab_w_in=v_ab_w_in, ab_conv_w=v_ab_conv_w, ab_w_out=v_ab_w_out, c_w_in=v_c_w_in, c_lower_bounds=v_c_lower_bounds, c_out_norm=v_c_out_norm, c_w_out=v_c_w_out, final_norm=v_final_norm)
    names = list(weights)
    big = [n for n in names if weights[n].size >= 65536]
    tiny = [n for n in names if n not in big]

    delta, new_m, new_v = {}, {}, {}
    for n in big:
        turn = flip if n in transposed else (lambda a: a)
        shape = turn(weights[n]).shape
        two_d = (shape[0] * shape[1], shape[2])
        d, m2, v2, g2 = _adamw("adamw_" + n, turn(weights[n]).reshape(two_d), layers[n],
                               turn(m_in[n]).reshape(two_d), turn(v_in[n]).reshape(two_d))
        delta[n], new_m[n], new_v[n] = turn(d.reshape(shape)), turn(m2.reshape(shape)), turn(v2.reshape(shape))
        grads[n] = turn(g2.reshape(shape))

    def tiny_slab(src):
        return _small_slab([src[n].reshape(-1, src[n].shape[-1]) for n in tiny])

    offs, row = {}, 0
    for n in tiny:
        nrows = weights[n].size // weights[n].shape[-1]
        offs[n] = (row, nrows)
        row += nrows
    d, m2, v2, _ = _adamw("adamw_small", tiny_slab(weights), [tiny_slab(grads)], tiny_slab(m_in), tiny_slab(v_in))
    for n in tiny:
        r0, nr = offs[n]
        shape = weights[n].shape
        for dst, src in ((delta, d), (new_m, m2), (new_v, v2)):
            dst[n] = src[r0:r0 + nr, :shape[-1]].reshape(shape)

    loss = small_sum[R_LOSS, 0]
    return (loss, grad_x.reshape(1, t, D), *[grads[n] for n in names], *[delta[n] for n in names],
            *[new_m[n] for n in names], *[new_v[n] for n in names])
```

```python
import functools
import math

import jax
import jax.numpy as jnp
from jax import lax
from jax.experimental import pallas as pl
from jax.experimental.pallas import tpu as pltpu

F32 = jnp.float32
BF16 = jnp.bfloat16
MESH = pl.DeviceIdType.MESH
ANY = pl.BlockSpec(memory_space=pl.ANY)

D = 1024
FS = 704
NSH = 4
RMS_EPS = 1e-6
MACARON = 0.5
CHUNK = 64
HD = 128
HGRN_HPS = 8
SBQ = 128
SB_PAIRS_FWD = 4
SB_PAIRS_BWD = 4
SB_DEAD = -105.0
CONV_HALO = 8
LANE = 128
ROW_TILE = 1024
MM_TILE = 1024
WGRAD_TILE = 4096
VMEM_LIMIT = 48 * 1024 * 1024
VMEM_LIMIT_BIG = 58 * 1024 * 1024

ADAM_LR, ADAM_B1, ADAM_B2, ADAM_EPS, ADAM_WD, ADAM_STEP = 0.001, 0.9, 0.999, 1e-08, 0.01, 10

NN = ((1,), (0,))
NT = ((1,), (1,))
TN = ((0,), (0,))

SMALL_ROWS = 16
R_PRE, R_MIX, R_POST, R_CLB, R_FIN, R_GAM, R_CONV, R_LOSS = 0, 2, 4, 6, 8, 9, 10, 13

B_DOWN = 0
B_ABOUT = 4 * FS
B_CIN = B_ABOUT + 256
B_COUT = B_CIN + 1024
B_ROWS = B_COUT + 256


def _dg(a, b, dims):
    return lax.dot_general(a, b, (dims, ((), ())), preferred_element_type=F32)


def _split(x):
    hi = x.astype(BF16)
    lo = (x - hi.astype(F32)).astype(BF16)
    return hi, lo


def _dot3(a, b, dims):
    ah, al = _split(a)
    bh, bl = _split(b)
    if dims == TN:
        n = b.shape[1]
        both = _dg(ah, jnp.concatenate([bh, bl], axis=1), dims)
        return both[:, :n] + both[:, n:] + _dg(al, bh, dims)
    m = a.shape[0]
    both = _dg(jnp.concatenate([ah, al], axis=0), bh, dims)
    return both[:m] + both[m:] + _dg(ah, bl, dims)


def _sigmoid(x):
    return 1.0 / (1.0 + jnp.exp(-x))


def _params(sem):
    return pltpu.CompilerParams(dimension_semantics=sem, vmem_limit_bytes=VMEM_LIMIT)


def _spec(shape, imap):
    return pl.BlockSpec(shape, imap)


def _accumulate(acc_ref, pairs, dims):
    part = None
    for a_ref, b_ref in pairs:
        d = _dg(a_ref[...], b_ref[...], dims)
        part = d if part is None else part + d
    acc_ref[...] += part


def _mm(name, pairs, *, grid, o_shape, o_dtype, o_spec, dims, kaxis, nk, acc_shape=None, res=None, scale=None,
        into=None, carry=None, norm=None):
    npairs = len(pairs)
    operands, specs = [], []
    for a, a_spec, b, b_spec in pairs:
        operands += [a, b]
        specs += [a_spec, b_spec]
    if res is not None:
        operands.append(res[0])
        specs.append(res[1])
    aliases = {}
    if into is not None:
        aliases = {len(operands): 0}
        operands.append(into)
        specs.append(ANY)
    if norm is not None:
        operands.append(norm[0])
        specs.append(_spec((None, 1, D), lambda *_: (norm[1], 0, 0)))
    n_own = len(operands)
    n_out = 2 if norm is not None else 1
    nc = carry.n if carry is not None else 0
    if nc:
        operands += carry.groups
        specs += carry.in_specs

    def body(*refs):
        o_ref = refs[n_own + nc]
        riders = ((refs[n_own:n_own + nc], refs[n_own + nc + n_out:n_own + 2 * nc + n_out], refs[-2], refs[-1])
                  if nc else None)
        if nc:
            carry.ride(grid, riders, "start")

        def finish(val):
            if scale is not None:
                val = val * scale
            if res is not None:
                val = val + refs[2 * npairs][...]
            o_ref[...] = val.astype(o_dtype)
            if norm is not None:
                r = lax.rsqrt(jnp.mean(val * val, axis=-1, keepdims=True) + RMS_EPS)
                refs[n_own + nc + 1][...] = (val * r * refs[n_own - 1][...]).astype(BF16)

        if nk == 1:
            part = None
            for n in range(npairs):
                d = _dg(refs[2 * n][...], refs[2 * n + 1][...], dims)
                part = d if part is None else part + d
            finish(part)
        else:
            acc_ref = refs[n_own + 2 * nc + n_out]
            k = pl.program_id(kaxis)

            @pl.when(k == 0)
            def _():
                acc_ref[...] = jnp.zeros_like(acc_ref)

            _accumulate(acc_ref, [(refs[2 * n], refs[2 * n + 1]) for n in range(npairs)], dims)

            @pl.when(k == nk - 1)
            def _():
                finish(acc_ref[...])
        if nc:
            carry.ride(grid, riders, "finish")

    sem = tuple("arbitrary" if (nc or (ax == kaxis and nk > 1)) else "parallel" for ax in range(len(grid)))
    outs = pl.pallas_call(
        body, name=name, grid=grid, in_specs=specs,
        out_specs=[o_spec] * n_out + (carry.out_specs if nc else []),
        out_shape=[jax.ShapeDtypeStruct(o_shape, o_dtype)] + ([jax.ShapeDtypeStruct(o_shape, BF16)] if norm is not None else [])
        + (carry.out_shape if nc else []),
        scratch_shapes=([pltpu.VMEM(acc_shape, F32)] if nk > 1 else []) + (carry.scratch if nc else []),
        input_output_aliases=aliases,
        compiler_params=_params(sem),
    )(*operands)
    main = (outs[0], outs[1]) if norm is not None else outs[0]
    return (main, carry.place(outs[n_out:])) if nc else main


def _norm_fwd(name, h, gain_slab, row, rider=None):
    t = h.shape[0]
    tm = min(ROW_TILE, t)
    grid = (t // tm,)
    nr = rider.n if rider is not None else 0

    def body(*refs):
        h_ref, g_ref, o_ref = refs[0], refs[1], refs[2 + nr]
        riders = (refs[2:2 + nr], refs[3 + nr:3 + 2 * nr], refs[-2], refs[-1]) if nr else None
        if nr:
            rider.ride(grid, riders, "start")
        x = h_ref[...]
        r = lax.rsqrt(jnp.mean(x * x, axis=-1, keepdims=True) + RMS_EPS)
        o_ref[...] = (x * r * g_ref[...]).astype(BF16)
        if nr:
            rider.ride(grid, riders, "finish")

    outs = pl.pallas_call(
        body, name=name, grid=grid,
        in_specs=[_spec((tm, D), lambda i: (i, 0)), _spec((None, 1, D), lambda i: (row, 0, 0))]
        + (rider.in_specs if nr else []),
        out_specs=[_spec((tm, D), lambda i: (i, 0))] + (rider.out_specs if nr else []),
        out_shape=[jax.ShapeDtypeStruct((t, D), BF16)] + (rider.out_shape if nr else []),
        scratch_shapes=rider.scratch if nr else [],
        compiler_params=_params(("arbitrary",) if nr else ("parallel",)),
    )(h, gain_slab, *(rider.groups if nr else []))
    return outs[0], (rider.place(outs[1:]) if nr else [])


def _dhn_norm(name, pairs, dims, h, gain_slab, row, dres, rider=None):
    t = h.shape[0]
    tm = min(MM_TILE, t)
    nt = t // tm
    grid = (nt, NSH)
    npairs = len(pairs)
    nr = rider.n if rider is not None else 0
    operands, specs = [], []
    for a, a_spec, b, b_spec in pairs:
        operands += [a, b]
        specs += [a_spec, b_spec]
    row_spec = _spec((tm, D), lambda i, k: (i, 0))
    operands += [h, gain_slab, dres]
    specs += [row_spec, _spec((None, 1, D), lambda i, k: (row, 0, 0)), row_spec]
    n_own = len(operands)

    def body(*refs):
        h_ref, g_ref, dres_ref = refs[2 * npairs:n_own]
        dh_ref, dhb_ref, dg_ref = refs[n_own + nr:n_own + nr + 3]
        acc_ref, gacc_ref = refs[n_own + 2 * nr + 3:n_own + 2 * nr + 5]
        riders = (refs[n_own:n_own + nr], refs[n_own + nr + 3:n_own + 2 * nr + 3], refs[-2], refs[-1]) if nr else None
        if nr:
            rider.ride(grid, riders, "start")
        i, k = pl.program_id(0), pl.program_id(1)

        @pl.when(k == 0)
        def _():
            acc_ref[...] = jnp.zeros_like(acc_ref)

        _accumulate(acc_ref, [(refs[2 * n], refs[2 * n + 1]) for n in range(npairs)], dims)

        @pl.when(k == NSH - 1)
        def _():
            x = h_ref[...]
            r = lax.rsqrt(jnp.mean(x * x, axis=-1, keepdims=True) + RMS_EPS)
            xh = x * r
            dy = acc_ref[...]
            gdy = dy * g_ref[...]
            dh = dres_ref[...] + (gdy - xh * jnp.mean(gdy * xh, axis=-1, keepdims=True)) * r
            dh_ref[...] = dh
            dhb_ref[...] = dh.astype(BF16)
            gpart = jnp.sum((dy * xh).reshape(tm // 8, 8, D), axis=0)

            @pl.when(i == 0)
            def _():
                gacc_ref[...] = gpart

            @pl.when(i > 0)
            def _():
                gacc_ref[...] += gpart

            @pl.when(i == nt - 1)
            def _():
                dg_ref[...] = jnp.sum(gacc_ref[...], axis=0, keepdims=True)

        if nr:
            rider.ride(grid, riders, "finish")

    outs = pl.pallas_call(
        body, name=name, grid=grid, in_specs=specs + (rider.in_specs if nr else []),
        out_specs=[row_spec, row_spec, _spec((1, D), lambda i, k: (0, 0))] + (rider.out_specs if nr else []),
        out_shape=[jax.ShapeDtypeStruct((t, D), F32), jax.ShapeDtypeStruct((t, D), BF16),
                   jax.ShapeDtypeStruct((1, D), F32)] + (rider.out_shape if nr else []),
        scratch_shapes=[pltpu.VMEM((tm, D), F32), pltpu.VMEM((8, D), F32)] + (rider.scratch if nr else []),
        compiler_params=pltpu.CompilerParams(dimension_semantics=("arbitrary", "arbitrary"),
                                             vmem_limit_bytes=VMEM_LIMIT_BIG),
    )(*operands, *(rider.groups if nr else []))
    return outs[0], outs[1], outs[2], (rider.place(outs[3:]) if nr else [])


def _final_loss(h, gain_slab, target):
    t = h.shape[0]
    tm = min(ROW_TILE, t)
    nt = t // tm

    def body(h_ref, g_ref, t_ref, dh_ref, dhb_ref, dg_ref, loss_ref, acc_ref, lacc_ref):
        i = pl.program_id(0)
        x = h_ref[...]
        g = g_ref[...]
        r = lax.rsqrt(jnp.mean(x * x, axis=-1, keepdims=True) + RMS_EPS)
        xh = x * r
        err = xh * g - t_ref[...]
        dy = err * (1.0 / D)
        gdy = dy * g
        dh = (gdy - xh * jnp.mean(gdy * xh, axis=-1, keepdims=True)) * r
        dh_ref[...] = dh
        dhb_ref[...] = dh.astype(BF16)
        part = jnp.sum((dy * xh).reshape(tm // 8, 8, D), axis=0)
        lpart = jnp.sum((err * err).reshape(tm // 8, 8, D), axis=0)

        @pl.when(i == 0)
        def _():
            acc_ref[...] = part
            lacc_ref[...] = lpart

        @pl.when(i > 0)
        def _():
            acc_ref[...] += part
            lacc_ref[...] += lpart

        @pl.when(i == nt - 1)
        def _():
            dg_ref[...] = jnp.sum(acc_ref[...], axis=0, keepdims=True)
            rows = jnp.sum(lacc_ref[...], axis=0, keepdims=True)
            loss_ref[...] = jnp.sum(rows, axis=1, keepdims=True) * (0.5 / D)

    row_spec = _spec((tm, D), lambda i: (i, 0))
    return pl.pallas_call(
        body, name="final_loss", grid=(nt,),
        in_specs=[row_spec, _spec((None, 1, D), lambda i: (R_FIN, 0, 0)), row_spec],
        out_specs=[row_spec, row_spec, _spec((1, D), lambda i: (0, 0)), _spec((1, 1), lambda i: (0, 0))],
        out_shape=[jax.ShapeDtypeStruct((t, D), F32), jax.ShapeDtypeStruct((t, D), BF16),
                   jax.ShapeDtypeStruct((1, D), F32), jax.ShapeDtypeStruct((1, 1), F32)],
        scratch_shapes=[pltpu.VMEM((8, D), F32), pltpu.VMEM((8, D), F32)],
        compiler_params=_params(("arbitrary",)),
    )(h, gain_slab, target)


def _ffn_up(name, hn, wgu, carry=None):
    t = hn.shape[0]
    tm = min(MM_TILE, t)
    grid = (NSH, t // tm)
    nc = carry.n if carry is not None else 0

    def body(*refs):
        x_ref, wg_ref, wu_ref = refs[:3]
        s_up_ref, s_gate_ref, a_ref = refs[3 + nc:6 + nc]
        riders = (refs[3:3 + nc], refs[6 + nc:6 + 2 * nc], refs[-2], refs[-1]) if nc else None
        if nc:
            carry.ride(grid, riders, "start")
        x = x_ref[...]
        g = _dg(x, wg_ref[...], NT)
        u = _dg(x, wu_ref[...], NT)
        sg = _sigmoid(g)
        silu = g * sg
        s_up_ref[...] = (MACARON * silu).astype(BF16)
        s_gate_ref[...] = (MACARON * u * (sg * (1.0 + g * (1.0 - sg)))).astype(BF16)
        a_ref[...] = (silu * u).astype(BF16)
        if nc:
            carry.ride(grid, riders, "finish")

    act = _spec((None, tm, FS), lambda s, i: (s, i, 0))
    shape = jax.ShapeDtypeStruct((NSH, t, FS), BF16)
    outs = pl.pallas_call(
        body, name=name, grid=grid,
        in_specs=[_spec((tm, D), lambda s, i: (i, 0)),
                  _spec((None, None, FS, D), lambda s, i: (s, 0, 0, 0)),
                  _spec((None, None, FS, D), lambda s, i: (s, 1, 0, 0))] + (carry.in_specs if nc else []),
        out_specs=[act, act, act] + (carry.out_specs if nc else []),
        out_shape=[shape, shape, shape] + (carry.out_shape if nc else []),
        scratch_shapes=carry.scratch if nc else [],
        compiler_params=_params(("arbitrary", "arbitrary") if nc else ("parallel", "parallel")),
    )(hn, wgu, wgu, *(carry.groups if nc else []))
    return (outs[0], outs[1], outs[2], carry.place(outs[3:]) if nc else [])


def _ffn_down(name, a, wd, h, carry=None, norm=None):
    t = h.shape[0]
    tm = min(MM_TILE, t)
    return _mm(name, [(a, _spec((None, tm, FS), lambda i, k: (k, i, 0)),
                       wd, _spec((None, FS, D), lambda i, k: (k, 0, 0)))],
               grid=(t // tm, NSH), o_shape=(t, D), o_dtype=F32, o_spec=_spec((tm, D), lambda i, k: (i, 0)),
               dims=NN, kaxis=1, nk=NSH, acc_shape=(tm, D), res=(h, _spec((tm, D), lambda i, k: (i, 0))),
               scale=MACARON, carry=carry, norm=norm)


def _ffn_bwd_up(name, dhb, wd, s_up, s_gate, rider=None):
    t = dhb.shape[0]
    tm = min(MM_TILE, t)
    grid = (NSH, t // tm)
    nr = rider.n if rider is not None else 0

    def body(*refs):
        dh_ref, wd_ref, s_up_ref, s_gate_ref = refs[:4]
        dg_ref, du_ref = refs[4 + nr:6 + nr]
        riders = (refs[4:4 + nr], refs[6 + nr:6 + 2 * nr], refs[-2], refs[-1]) if nr else None
        if nr:
            rider.ride(grid, riders, "start")
        da = _dg(dh_ref[...], wd_ref[...], NT).astype(BF16)
        du_ref[...] = da * s_up_ref[...]
        dg_ref[...] = da * s_gate_ref[...]
        if nr:
            rider.ride(grid, riders, "finish")

    act = _spec((None, tm, FS), lambda s, i: (s, i, 0))
    shape = jax.ShapeDtypeStruct((NSH, t, FS), BF16)
    outs = pl.pallas_call(
        body, name=name, grid=grid,
        in_specs=[_spec((tm, D), lambda s, i: (i, 0)), _spec((None, FS, D), lambda s, i: (s, 0, 0)), act, act]
        + (rider.in_specs if nr else []),
        out_specs=[act, act] + (rider.out_specs if nr else []),
        out_shape=[shape, shape] + (rider.out_shape if nr else []),
        scratch_shapes=rider.scratch if nr else [],
        compiler_params=_params(("arbitrary", "arbitrary") if nr else ("parallel", "parallel")),
    )(dhb, wd, s_up, s_gate, *(rider.groups if nr else []))
    return outs[0], outs[1], (rider.place(outs[2:]) if nr else [])


def _wgrad(name, a, a_spec, b, b_spec, out_rows, out_cols, t, tt, group, slot, scale=None, carry=None):
    first = isinstance(group, int)
    rows = group if first else group.shape[1]
    return _mm(name, [(a, a_spec, b, b_spec)], grid=(NSH, t // tt),
               o_shape=(NSH, rows, out_cols), o_dtype=BF16,
               o_spec=_spec((None, out_rows, out_cols), lambda s, k: (s, slot, 0)),
               dims=TN, kaxis=1, nk=t // tt, acc_shape=(out_rows, out_cols), scale=scale,
               into=None if first else group, carry=carry)


def _ffn_backward(tag, dh, dhb, h_in, hn, s_up, s_gate, a, wgu, wd, gate_idx, up_idx, down_idx, small, norm_row,
                  grad_a, grad_b, rider_up=None, rider_dhn=None):
    t = dh.shape[0]
    tm = min(MM_TILE, t)
    tt = min(WGRAD_TILE, t)
    tok = _spec((tt, D), lambda s, k: (k, 0))
    hid = _spec((None, tt, FS), lambda s, k: (s, k, 0))
    grad_b = _wgrad(tag + "_dwd", a, hid, dhb, tok, FS, D, t, tt, grad_b, down_idx, scale=MACARON)
    rider = rider_up(grad_a, grad_b) if rider_up is not None else None
    dg, du, landed_up = _ffn_bwd_up(tag + "_bwd_up", dhb, wd, s_up, s_gate, rider)
    grad_a = _wgrad(tag + "_dwg", dg, hid, hn, tok, FS, D, t, tt, grad_a, gate_idx)
    grad_a = _wgrad(tag + "_dwu", du, hid, hn, tok, FS, D, t, tt, grad_a, up_idx)
    rider = rider_dhn(grad_a, grad_b, landed_up) if rider_dhn is not None else None
    act = _spec((None, tm, FS), lambda i, k: (k, i, 0))
    dh_in, dhb_in, d_gain, landed_dhn = _dhn_norm(
        tag + "_dhn", [(dg, act, wgu, _spec((None, None, FS, D), lambda i, k: (k, 0, 0, 0))),
                       (du, act, wgu, _spec((None, None, FS, D), lambda i, k: (k, 1, 0, 0)))],
        NN, h_in, small, norm_row, dh, rider)
    return dh_in, dhb_in, grad_a, grad_b, d_gain, landed_up, landed_dhn


def _conv_fwd(proj_a, conv_w):
    t = proj_a.shape[0]
    tm = min(ROW_TILE, t)
    hb = tm // CONV_HALO

    def body(ab_ref, ac_ref, ax_ref, acp_ref, axp_ref, w_ref, y_ref):
        i = pl.program_id(1)
        u = ac_ref[...] * ax_ref[...]
        up = jnp.where(i > 0, acp_ref[...] * axp_ref[...], 0.0)
        ext = jnp.concatenate([up, u], axis=0)
        u1 = pltpu.roll(ext, 1, 0)[CONV_HALO:]
        u2 = pltpu.roll(ext, 2, 0)[CONV_HALO:]
        w = w_ref[...]
        conv = w[0:1] * u2 + w[1:2] * u1 + w[2:3] * u
        y_ref[...] = (ab_ref[...] * conv).astype(BF16)

    def cur(off):
        return _spec((tm, LANE), lambda j, i: (i, off + j))

    def prev(off):
        return _spec((CONV_HALO, LANE), lambda j, i: (jnp.maximum(i * hb - 1, 0), off + j))

    return pl.pallas_call(
        body, name="conv_fwd", grid=(4, t // tm),
        in_specs=[cur(0), cur(4), cur(8), prev(4), prev(8),
                  _spec((None, None, 8, LANE), lambda j, i: (j, 0, 0, 0))],
        out_specs=_spec((tm, LANE), lambda j, i: (i, j)),
        out_shape=jax.ShapeDtypeStruct((t, 512), BF16),
        compiler_params=_params(("parallel", "parallel")),
    )(proj_a, proj_a, proj_a, proj_a, proj_a, conv_w)


def _conv_bwd(proj_a, conv_w, dy, rider=None):
    t = proj_a.shape[0]
    tm = min(ROW_TILE, t)
    hb = tm // CONV_HALO
    nt = t // tm
    grid = (4, nt)
    nr = rider.n if rider is not None else 0

    def body(*refs):
        ab_ref, ac_ref, ax_ref, dy_ref, acp_ref, axp_ref, abn_ref, dyn_ref, w_ref = refs[:9]
        dab_ref, dac_ref, dax_ref, dw_ref = refs[9 + nr:13 + nr]
        acc_ref = refs[13 + 2 * nr]
        riders = (refs[9:9 + nr], refs[13 + nr:13 + 2 * nr], refs[-2], refs[-1]) if nr else None
        if nr:
            rider.ride(grid, riders, "start")
        i = pl.program_id(1)
        ab, ac, ax = ab_ref[...], ac_ref[...], ax_ref[...]
        u = ac * ax
        up = jnp.where(i > 0, acp_ref[...] * axp_ref[...], 0.0)
        ext = jnp.concatenate([up, u], axis=0)
        u1 = pltpu.roll(ext, 1, 0)[CONV_HALO:]
        u2 = pltpu.roll(ext, 2, 0)[CONV_HALO:]
        w = w_ref[...]
        conv = w[0:1] * u2 + w[1:2] * u1 + w[2:3] * u
        dy_v = dy_ref[...]
        dab_ref[...] = (dy_v * conv).astype(BF16)
        dc = dy_v * ab
        dcn = jnp.where(i < nt - 1, dyn_ref[...] * abn_ref[...], 0.0)
        extn = jnp.concatenate([dc, dcn], axis=0)
        n = tm + CONV_HALO
        dc1 = pltpu.roll(extn, n - 1, 0)[:tm]
        dc2 = pltpu.roll(extn, n - 2, 0)[:tm]
        du = w[2:3] * dc + w[1:2] * dc1 + w[0:1] * dc2
        dac_ref[...] = (du * ax).astype(BF16)
        dax_ref[...] = (du * ac).astype(BF16)
        rid = lax.broadcasted_iota(jnp.int32, (8, LANE), 0)
        part = jnp.where(rid == 0, jnp.sum(dc * u2, axis=0, keepdims=True),
                         jnp.where(rid == 1, jnp.sum(dc * u1, axis=0, keepdims=True),
                                   jnp.where(rid == 2, jnp.sum(dc * u, axis=0, keepdims=True), 0.0)))

        @pl.when(i == 0)
        def _():
            acc_ref[...] = part

        @pl.when(i > 0)
        def _():
            acc_ref[...] += part

        @pl.when(i == nt - 1)
        def _():
            dw_ref[...] = acc_ref[...]

        if nr:
            rider.ride(grid, riders, "finish")

    def cur(off):
        return _spec((tm, LANE), lambda j, i: (i, off + j))

    def prev(off):
        return _spec((CONV_HALO, LANE), lambda j, i: (jnp.maximum(i * hb - 1, 0), off + j))

    def nxt(off):
        return _spec((CONV_HALO, LANE), lambda j, i: (jnp.minimum((i + 1) * hb, nt * hb - 1), off + j))

    outs = pl.pallas_call(
        body, name="conv_bwd", grid=grid,
        in_specs=[cur(0), cur(4), cur(8), cur(0), prev(4), prev(8), nxt(0), nxt(0),
                  _spec((None, None, 8, LANE), lambda j, i: (j, 0, 0, 0))] + (rider.in_specs if nr else []),
        out_specs=[_spec((tm, LANE), lambda j, i: (i, j)), _spec((tm, LANE), lambda j, i: (i, j)),
                   _spec((tm, LANE), lambda j, i: (i, j)), _spec((None, 8, LANE), lambda j, i: (j, 0, 0))]
        + (rider.out_specs if nr else []),
        out_shape=[jax.ShapeDtypeStruct((t, 512), BF16), jax.ShapeDtypeStruct((t, 512), BF16),
                   jax.ShapeDtypeStruct((t, 512), BF16), jax.ShapeDtypeStruct((4, 8, LANE), F32)]
        + (rider.out_shape if nr else []),
        scratch_shapes=[pltpu.VMEM((8, LANE), F32)] + (rider.scratch if nr else []),
        compiler_params=_params(("arbitrary", "arbitrary") if nr else ("parallel", "arbitrary")),
    )(proj_a, proj_a, proj_a, dy, proj_a, proj_a, proj_a, dy, conv_w, *(rider.groups if nr else []))
    return outs[0], outs[1], outs[2], outs[3], (rider.place(outs[4:]) if nr else [])


def _log_sigmoid(z):
    return jnp.minimum(z, 0.0) - jnp.log(1.0 + jnp.exp(-jnp.abs(z)))


def _sb_masks():
    row = lax.broadcasted_iota(jnp.int32, (SBQ, SBQ), 0)
    col = lax.broadcasted_iota(jnp.int32, (SBQ, SBQ), 1)
    return row, col


def _ones_where(mask):
    return jnp.where(mask, 1.0, 0.0).astype(BF16)


def _head_mask(head):
    lane = lax.broadcasted_iota(jnp.int32, (1, LANE), 1)
    return lane >= 64 if head else lane < 64


def _sb_fwd(proj_b, carry=None):
    t = proj_b.shape[0]
    nq = t // SBQ
    scale = 1.0 / math.sqrt(64.0)

    npair = SB_PAIRS_FWD
    wide = npair * LANE
    ngrp = 4 // npair
    chains = [(p, head) for p in range(npair) for head in range(2)]

    grid = (ngrp, nq)
    nc = carry.n if carry is not None else 0

    def body(*refs):
        q_ref, k_ref, v_ref = refs[:3]
        y_ref, l_ref, n_ref = refs[3 + nc:6 + nc]
        riders = (refs[3:3 + nc], refs[6 + nc:6 + 2 * nc], refs[-2], refs[-1]) if nc else None
        if nc:
            carry.ride(grid, riders, "start")
        grp = pl.program_id(0)
        qi = pl.program_id(1)
        row, col = _sb_masks()
        m_suffix = _ones_where(row > col)
        rows = len(chains) * SBQ
        strict = (lax.broadcasted_iota(jnp.int32, (rows, SBQ), 1)
                  < (lax.broadcasted_iota(jnp.int32, (rows, SBQ), 0) & (SBQ - 1)))
        q_pair = []
        for p in range(npair):
            q_all = q_ref[:, p * LANE:(p + 1) * LANE]
            q_pair.append(jnp.concatenate([jnp.where(_head_mask(head), q_all, jnp.zeros_like(q_all)) for head in range(2)],
                                          axis=0))

        def block(kb, state, diag):
            run, acc = state
            start = pl.multiple_of(kb * SBQ, SBQ)
            z = jnp.concatenate([_dg(q_pair[p], k_ref[pl.ds(start, SBQ), p * LANE:(p + 1) * LANE], NT)
                                 for p in range(npair)], axis=0) * scale
            lb = _log_sigmoid(z)
            lk = lb - z
            if diag:
                lk = jnp.where(strict, lk, 0.0)
            hi, lo = _split(lk)
            sums = _dg(jnp.concatenate([hi, lo], axis=0), m_suffix, NN)
            w = jnp.exp(lb + (run + sums[:rows] + sums[rows:]))
            if diag:
                w = jnp.where(strict, w, 0.0)
            wb = w.astype(BF16)
            acc = acc + jnp.concatenate(
                [_dg(wb[2 * p * SBQ:2 * (p + 1) * SBQ], v_ref[pl.ds(start, SBQ), p * LANE:(p + 1) * LANE], NN)
                 for p in range(npair)], axis=0)
            run = run + jnp.sum(hi.astype(F32) + lo.astype(F32), axis=1, keepdims=True)
            return run, acc

        state = block(qi, (jnp.zeros((rows, 1), F32), jnp.zeros((rows, LANE), F32)), True)

        def live(c):
            return jnp.logical_and(c[0] < qi, jnp.max(c[1][0]) > SB_DEAD)

        def step(c):
            return c[0] + 1, block(qi - 1 - c[0], c[1], False)

        count, (run, acc) = lax.while_loop(live, step, (jnp.int32(0), state))
        n_ref[grp * nq + qi] = count.astype(F32)
        hm = _head_mask(0)
        for p in range(npair):
            lo_rows, hi_rows = slice(2 * p * SBQ, (2 * p + 1) * SBQ), slice((2 * p + 1) * SBQ, (2 * p + 2) * SBQ)
            y_ref[:, p * LANE:(p + 1) * LANE] = jnp.where(hm, acc[lo_rows], acc[hi_rows]).astype(BF16)
            l_ref[p] = jnp.where(hm, run[lo_rows], run[hi_rows])
        if nc:
            carry.ride(grid, riders, "finish")

    outs = pl.pallas_call(
        body, name="sb_fwd", grid=grid,
        in_specs=[_spec((SBQ, wide), lambda g, i: (i, g)),
                  _spec((t, wide), lambda g, i: (0, ngrp + g)),
                  _spec((t, wide), lambda g, i: (0, 2 * ngrp + g))] + (carry.in_specs if nc else []),
        out_specs=[_spec((SBQ, wide), lambda g, i: (i, g)), _spec((npair, SBQ, LANE), lambda g, i: (g, i, 0)),
                   pl.BlockSpec(memory_space=pltpu.SMEM)] + (carry.out_specs if nc else []),
        out_shape=[jax.ShapeDtypeStruct((t, 512), BF16), jax.ShapeDtypeStruct((4, t, LANE), F32),
                   jax.ShapeDtypeStruct((ngrp * nq,), F32)] + (carry.out_shape if nc else []),
        scratch_shapes=carry.scratch if nc else [],
        compiler_params=_params(("arbitrary", "arbitrary")),
    )(proj_b, proj_b, proj_b, *(carry.groups if nc else []))
    return outs[0], outs[1], outs[2], (carry.place(outs[3:]) if nc else [])


def _sb_bwd(proj_b, dy, ltot, nblk, rider=None):
    t = proj_b.shape[0]
    nq = t // SBQ
    scale = 1.0 / math.sqrt(64.0)

    npair = SB_PAIRS_BWD
    wide = npair * LANE
    ngrp = 4 // npair
    chains = [(p, head) for p in range(npair) for head in range(2)]
    grid = (ngrp, nq)
    nr = rider.n if rider is not None else 0
    per_count = SB_PAIRS_FWD // SB_PAIRS_BWD

    def body(*refs):
        q_ref, k_ref, v_ref, dy_ref, l_ref, n_ref = refs[:6]
        dq_ref, dk_ref, dv_ref = refs[6 + nr:9 + nr]
        dk_acc, dv_acc = refs[9 + 2 * nr:11 + 2 * nr]
        riders = (refs[6:6 + nr], refs[9 + nr:9 + 2 * nr], refs[-2], refs[-1]) if nr else None
        if nr:
            rider.ride(grid, riders, "start")
        grp = pl.program_id(0)
        qi = pl.program_id(1)

        @pl.when(qi == 0)
        def _():
            dk_acc[...] = jnp.zeros_like(dk_acc)
            dv_acc[...] = jnp.zeros_like(dv_acc)

        row, col = _sb_masks()
        m_prefix = _ones_where(row <= col)
        m_before = _ones_where(row < col)
        rows = len(chains) * SBQ
        strict = (lax.broadcasted_iota(jnp.int32, (rows, SBQ), 1)
                  < (lax.broadcasted_iota(jnp.int32, (rows, SBQ), 0) & (SBQ - 1)))
        q_pair, do_pair, ltot = [], [], []
        for p in range(npair):
            pl_ = slice(p * LANE, (p + 1) * LANE)
            q_all = q_ref[:, pl_]
            do_all = dy_ref[:, pl_].astype(BF16)
            q_pair.append(jnp.concatenate([jnp.where(_head_mask(h), q_all, jnp.zeros_like(q_all)) for h in range(2)], axis=0))
            do_pair.append(jnp.concatenate([jnp.where(_head_mask(h), do_all, jnp.zeros_like(do_all)) for h in range(2)], axis=0))
            ltot += [l_ref[p][:, h * 64:h * 64 + 1] for h in range(2)]
        ltot = jnp.concatenate(ltot, axis=0)

        def pair_rows(a, p):
            return a[2 * p * SBQ:2 * (p + 1) * SBQ]

        def block(kb, state, diag):
            seen, dseen, dq = state
            start = pl.multiple_of(kb * SBQ, SBQ)
            kk = [k_ref[pl.ds(start, SBQ), p * LANE:(p + 1) * LANE] for p in range(npair)]
            vv = [v_ref[pl.ds(start, SBQ), p * LANE:(p + 1) * LANE] for p in range(npair)]
            z = jnp.concatenate([_dg(q_pair[p], kk[p], NT) for p in range(npair)], axis=0) * scale
            lb = _log_sigmoid(z)
            lk = lb - z
            if diag:
                lk = jnp.where(strict, lk, 0.0)
            hi, lo = _split(lk)
            sums = _dg(jnp.concatenate([hi, lo], axis=0), m_prefix, NN)
            w = jnp.exp(lb + ((ltot - seen) - (sums[:rows] + sums[rows:])))
            if diag:
                w = jnp.where(strict, w, 0.0)
            wb = w.astype(BF16)
            da = w * jnp.concatenate([_dg(do_pair[p], vv[p], NT) for p in range(npair)], axis=0)
            dah, dal = _split(da)
            dsums = _dg(jnp.concatenate([dah, dal], axis=0), m_before, NN)
            sig = jnp.exp(lb)
            dz = (da * (1.0 - sig) - (dseen + dsums[:rows] + dsums[rows:]) * sig) * scale
            if diag:
                dz = jnp.where(strict, dz, 0.0)
            dzb = dz.astype(BF16)
            for p in range(npair):
                pl_ = slice(p * LANE, (p + 1) * LANE)
                dv_acc[pl.ds(start, SBQ), pl_] += _dg(pair_rows(wb, p), do_pair[p], TN)
                dk_acc[pl.ds(start, SBQ), pl_] += _dg(pair_rows(dzb, p), q_pair[p], TN)
            dq = dq + jnp.concatenate([_dg(pair_rows(dzb, p), kk[p], NN) for p in range(npair)], axis=0)
            seen = seen + jnp.sum(hi.astype(F32) + lo.astype(F32), axis=1, keepdims=True)
            dseen = dseen + jnp.sum(da, axis=1, keepdims=True)
            return seen, dseen, dq

        zero = (jnp.zeros((rows, 1), F32), jnp.zeros((rows, 1), F32), jnp.zeros((rows, LANE), F32))
        first = qi - n_ref[(grp // per_count) * nq + qi].astype(jnp.int32)
        state = lax.fori_loop(first, qi, lambda kb, c: block(kb, c, False), zero)
        _, _, dq = block(qi, state, True)
        for p in range(npair):
            dq_ref[:, p * LANE:(p + 1) * LANE] = jnp.where(
                _head_mask(0), dq[2 * p * SBQ:(2 * p + 1) * SBQ], dq[(2 * p + 1) * SBQ:(2 * p + 2) * SBQ]).astype(BF16)

        @pl.when(qi == nq - 1)
        def _():
            dk_ref[...] = dk_acc[...].astype(BF16)
            dv_ref[...] = dv_acc[...].astype(BF16)

        if nr:
            rider.ride(grid, riders, "finish")

    full = jax.ShapeDtypeStruct((t, 512), BF16)
    outs = pl.pallas_call(
        body, name="sb_bwd", grid=grid,
        in_specs=[_spec((SBQ, wide), lambda g, i: (i, g)),
                  _spec((t, wide), lambda g, i: (0, ngrp + g)),
                  _spec((t, wide), lambda g, i: (0, 2 * ngrp + g)),
                  _spec((SBQ, wide), lambda g, i: (i, ngrp + g)),
                  _spec((npair, SBQ, LANE), lambda g, i: (g, i, 0)),
                  pl.BlockSpec(memory_space=pltpu.SMEM)] + (rider.in_specs if nr else []),
        out_specs=[_spec((SBQ, wide), lambda g, i: (i, g)),
                   _spec((t, wide), lambda g, i: (0, g)), _spec((t, wide), lambda g, i: (0, g))]
        + (rider.out_specs if nr else []),
        out_shape=[full, full, full] + (rider.out_shape if nr else []),
        scratch_shapes=[pltpu.VMEM((t, wide), F32), pltpu.VMEM((t, wide), F32)] + (rider.scratch if nr else []),
        compiler_params=pltpu.CompilerParams(dimension_semantics=("arbitrary", "arbitrary"),
                                             vmem_limit_bytes=VMEM_LIMIT_BIG),
    )(proj_b, proj_b, proj_b, dy, ltot, nblk, *(rider.groups if nr else []))
    return outs[0], outs[1], outs[2], (rider.place(outs[3:]) if nr else [])


def _hgrn_gates(qr, fr, c0, c1):
    mx = jnp.maximum(c0, c1)
    e0, e1 = jnp.exp(c0 - mx), jnp.exp(c1 - mx)
    lb = e1 / (e0 + e1)
    sx = _sigmoid(fr)
    f = lb + (1.0 - lb) * sx
    k = (1.0 - lb) * (1.0 - sx)
    sq = _sigmoid(qr)
    return lb, sx, f, k, sq, qr * sq


def _chunk_sums(mask, x):
    n = x.shape[1]
    hi, lo = _split(x)
    both = _dg(_ones_where(mask), jnp.concatenate([hi, lo], axis=1), NN)
    return both[:, :n] + both[:, n:]


def _chunk_masks():
    row = lax.broadcasted_iota(jnp.int32, (CHUNK, CHUNK), 0)
    col = lax.broadcasted_iota(jnp.int32, (CHUNK, CHUNK), 1)
    return col <= row, col >= row


def _hgrn_fwd(proj_c, small, carry=None):
    t = proj_c.shape[1]
    nc = t // CHUNK
    nh = D // HD

    hps = HGRN_HPS
    wide = hps * HD
    grid = (nh // hps, nc)
    nr = carry.n if carry is not None else 0

    def body(*refs):
        p_ref, c0_ref, c1_ref, gam_ref = refs[:4]
        o_ref, y_ref, sst_ref = refs[4 + nr:7 + nr]
        st_ref = refs[7 + 2 * nr]
        riders = (refs[4:4 + nr], refs[7 + nr:7 + 2 * nr], refs[-2], refs[-1]) if nr else None
        if nr:
            carry.ride(grid, riders, "start")
        c = pl.program_id(1)

        @pl.when(c == 0)
        def _():
            st_ref[...] = jnp.zeros_like(st_ref)

        _, _, f_all, k_all, _, q_all = _hgrn_gates(p_ref[0], p_ref[1], c0_ref[...], c1_ref[...])
        tril, _ = _chunk_masks()
        b_all = _chunk_sums(tril, jnp.log(f_all))
        for j in range(hps):
            ln = slice(j * HD, (j + 1) * HD)
            st0 = st_ref[j]
            sst_ref[j] = st0
            v, g = p_ref[2, :, ln], p_ref[3, :, ln]
            q, k, b = q_all[:, ln], k_all[:, ln], b_all[:, ln]
            bm = b[CHUNK // 2 - 1:CHUNK // 2]
            bl = b[CHUNK - 1:CHUNK]
            qd = q * jnp.exp(b)
            qt = q * jnp.exp(b - bm)
            kt = k * jnp.exp(bm - b)
            kl = k * jnp.exp(bl - b)
            vb = v.astype(BF16)
            att = jnp.where(tril, _dot3(qt, kt, NT), 0.0)
            o = _dg(qd.astype(BF16), st0.astype(BF16), NT) + _dg(att.astype(BF16), vb, NN)
            st_ref[j] = st0 * jnp.exp(bl) + _dg(vb, kl.astype(BF16), TN)
            o_ref[:, ln] = o
            r = lax.rsqrt(jnp.mean(o * o, axis=-1, keepdims=True) + RMS_EPS)
            y_ref[:, ln] = (o * r * gam_ref[...] * (g * _sigmoid(g))).astype(BF16)
        if nr:
            carry.ride(grid, riders, "finish")

    outs = pl.pallas_call(
        body, name="hgrn_fwd", grid=grid,
        in_specs=[_spec((4, CHUNK, wide), lambda h, c: (0, c, h)),
                  _spec((None, 1, wide), lambda h, c: (R_CLB, 0, h)),
                  _spec((None, 1, wide), lambda h, c: (R_CLB + 1, 0, h)),
                  _spec((None, 1, HD), lambda h, c: (R_GAM, 0, 0))] + (carry.in_specs if nr else []),
        out_specs=[_spec((CHUNK, wide), lambda h, c: (c, h)), _spec((CHUNK, wide), lambda h, c: (c, h)),
                   _spec((None, hps, HD, HD), lambda h, c: (c, h, 0, 0))] + (carry.out_specs if nr else []),
        out_shape=[jax.ShapeDtypeStruct((t, D), F32), jax.ShapeDtypeStruct((t, D), BF16),
                   jax.ShapeDtypeStruct((nc, nh, HD, HD), F32)] + (carry.out_shape if nr else []),
        scratch_shapes=[pltpu.VMEM((hps, HD, HD), F32)] + (carry.scratch if nr else []),
        compiler_params=_params(("arbitrary", "arbitrary")),
    )(proj_c, small, small, small, *(carry.groups if nr else []))
    return outs[0], outs[1], outs[2], (carry.place(outs[3:]) if nr else [])


def _hgrn_bwd(proj_c, small, o, sst, dyc, rider=None):
    t = proj_c.shape[1]
    nc = t // CHUNK
    nh = D // HD

    hps = HGRN_HPS
    wide = hps * HD
    ng = nh // hps
    grid = (ng, nc)
    nr = rider.n if rider is not None else 0

    def body(*refs):
        p_ref, c0_ref, c1_ref, gam_ref, o_ref, sst_ref, dy_ref = refs[:7]
        dp_ref, dclb_ref, dgam_ref = refs[7 + nr:10 + nr]
        dst_ref, dlb_acc, dgam_acc = refs[10 + 2 * nr:13 + 2 * nr]
        riders = (refs[7:7 + nr], refs[10 + nr:10 + 2 * nr], refs[-2], refs[-1]) if nr else None
        if nr:
            rider.ride(grid, riders, "start")
        group = pl.program_id(0)
        step = pl.program_id(1)

        @pl.when(step == 0)
        def _():
            dst_ref[...] = jnp.zeros_like(dst_ref)
            dlb_acc[...] = jnp.zeros_like(dlb_acc)

        @pl.when((step == 0) & (group == 0))
        def _():
            dgam_acc[...] = jnp.zeros_like(dgam_acc)

        gam = gam_ref[...]
        qr_all = p_ref[0]
        lb_all, sx_all, f_all, k_all, sq_all, q_all = _hgrn_gates(qr_all, p_ref[1], c0_ref[...], c1_ref[...])
        tril, triu = _chunk_masks()
        b_all = _chunk_sums(tril, jnp.log(f_all))
        dq_parts, dk_parts, db_parts, dgam_parts = [], [], [], []
        for j in range(hps):
            ln = slice(j * HD, (j + 1) * HD)
            st0 = sst_ref[j]
            dst1 = dst_ref[j]
            v, g = p_ref[2, :, ln], p_ref[3, :, ln]
            q, k, b = q_all[:, ln], k_all[:, ln], b_all[:, ln]
            bm = b[CHUNK // 2 - 1:CHUNK // 2]
            bl = b[CHUNK - 1:CHUNK]
            eb = jnp.exp(b)
            e_qt = jnp.exp(b - bm)
            e_kt = jnp.exp(bm - b)
            e_kl = jnp.exp(bl - b)
            e_bl = jnp.exp(bl)
            qd, qt, kt, kl = q * eb, q * e_qt, k * e_kt, k * e_kl
            ov = o_ref[:, ln]
            r = lax.rsqrt(jnp.mean(ov * ov, axis=-1, keepdims=True) + RMS_EPS)
            oh = ov * r
            sg = _sigmoid(g)
            dy = dy_ref[:, ln]
            dp_ref[3, :, ln] = (dy * oh * gam * (sg * (1.0 + g * (1.0 - sg)))).astype(BF16)
            dyv = dy * (g * sg)
            gdy = dyv * gam
            do = (gdy - oh * jnp.mean(gdy * oh, axis=-1, keepdims=True)) * r
            dob, vb = do.astype(BF16), v.astype(BF16)
            st0b, dst1b = st0.astype(BF16), dst1.astype(BF16)
            st1 = st0 * e_bl + _dg(vb, kl.astype(BF16), TN)
            att = jnp.where(tril, _dot3(qt, kt, NT), 0.0)
            datt = jnp.where(tril, _dg(dob, vb, NT), 0.0)
            dv = _dg(att.astype(BF16), dob, TN) + _dg(kl.astype(BF16), dst1b, NT)
            dq = _dot3(datt, kt, NN) * e_qt + _dg(dob, st0b, NN) * eb
            dk = _dot3(datt, qt, TN) * e_kt + _dg(vb, dst1b, NN) * e_kl
            db = q * dq - k * dk
            last = lax.broadcasted_iota(jnp.int32, (CHUNK, 1), 0) == CHUNK - 1
            db = db + jnp.where(last, jnp.sum(dst1 * st1, axis=0, keepdims=True), 0.0)
            dst_ref[j] = dst1 * e_bl + _dg(dob, qd.astype(BF16), TN)
            dp_ref[2, :, ln] = dv.astype(BF16)
            dq_parts.append(dq)
            dk_parts.append(dk)
            db_parts.append(db)
            dgam_parts.append(jnp.sum(dyv * oh, axis=0, keepdims=True))

        dq_all, dk_all = jnp.concatenate(dq_parts, axis=1), jnp.concatenate(dk_parts, axis=1)
        dlf = _chunk_sums(triu, jnp.concatenate(db_parts, axis=1))
        dp_ref[0] = (dq_all * (sq_all * (1.0 + qr_all * (1.0 - sq_all)))).astype(BF16)
        tmp = dlf / f_all - dk_all
        dp_ref[1] = (tmp * (1.0 - lb_all) * sx_all * (1.0 - sx_all)).astype(BF16)
        dlb_acc[...] += jnp.sum((1.0 - sx_all) * tmp, axis=0, keepdims=True)
        dgam_acc[...] += functools.reduce(lambda a, b: a + b, dgam_parts)

        @pl.when(step == nc - 1)
        def _():
            d1 = dlb_acc[...] * lb_all * (1.0 - lb_all)
            dclb_ref[...] = jnp.where(lax.broadcasted_iota(jnp.int32, (2, wide), 0) == 0, -d1, d1)

        @pl.when((step == nc - 1) & (group == ng - 1))
        def _():
            dgam_ref[...] = dgam_acc[...]

        if nr:
            rider.ride(grid, riders, "finish")

    rev = lambda h, s: (nc - 1 - s, h)
    outs = pl.pallas_call(
        body, name="hgrn_bwd", grid=grid,
        in_specs=[_spec((4, CHUNK, wide), lambda h, s: (0, nc - 1 - s, h)),
                  _spec((None, 1, wide), lambda h, s: (R_CLB, 0, h)),
                  _spec((None, 1, wide), lambda h, s: (R_CLB + 1, 0, h)),
                  _spec((None, 1, HD), lambda h, s: (R_GAM, 0, 0)),
                  _spec((CHUNK, wide), rev),
                  _spec((None, hps, HD, HD), lambda h, s: (nc - 1 - s, h, 0, 0)),
                  _spec((CHUNK, wide), rev)] + (rider.in_specs if nr else []),
        out_specs=[_spec((4, CHUNK, wide), lambda h, s: (0, nc - 1 - s, h)),
                   _spec((2, wide), lambda h, s: (0, h)),
                   _spec((1, HD), lambda h, s: (0, 0))] + (rider.out_specs if nr else []),
        out_shape=[jax.ShapeDtypeStruct((4, t, D), BF16), jax.ShapeDtypeStruct((2, D), F32),
                   jax.ShapeDtypeStruct((1, HD), F32)] + (rider.out_shape if nr else []),
        scratch_shapes=[pltpu.VMEM((hps, HD, HD), F32), pltpu.VMEM((1, wide), F32), pltpu.VMEM((1, HD), F32)]
        + (rider.scratch if nr else []),
        compiler_params=_params(("arbitrary", "arbitrary")),
    )(proj_c, small, small, small, o, sst, dyc, *(rider.groups if nr else []))
    return outs[0], outs[1], outs[2], (rider.place(outs[3:]) if nr else [])


def _adamw(name, w, grads, m, v):
    nl = len(grads)
    rows, cols = grads[0].shape
    br = rows
    for cand in (512, 352, 256):
        if rows % cand == 0:
            br = cand
            break
    nb = rows // br
    c1 = 1.0 - ADAM_B1 ** ADAM_STEP
    c2 = 1.0 - ADAM_B2 ** ADAM_STEP

    def body(w_ref, m_ref, v_ref, *refs):
        g_refs, (d_ref, mo_ref, vo_ref, go_ref) = refs[:nl], refs[nl:]
        layer = pl.program_id(0)
        gv = g_refs[0][...]
        for k in range(1, nl):
            gv = jnp.where(layer == k, g_refs[k][...], gv)
        mn = ADAM_B1 * m_ref[...] + (1.0 - ADAM_B1) * gv
        vn = ADAM_B2 * v_ref[...] + (1.0 - ADAM_B2) * (gv * gv)
        mo_ref[...] = mn
        vo_ref[...] = vn
        go_ref[...] = gv
        d_ref[...] = -ADAM_LR * ((mn / c1) / (jnp.sqrt(vn / c2) + ADAM_EPS) + ADAM_WD * w_ref[...])

    blk = _spec((br, cols), lambda l, i: (l * nb + i, 0))
    g_specs = [_spec((br, cols), lambda l, i, k=k: (jnp.where(l == k, i, 0), 0)) for k in range(nl)]
    shape = jax.ShapeDtypeStruct((nl * rows, cols), F32)
    return pl.pallas_call(
        body, name=name, grid=(nl, nb), in_specs=[blk] * 3 + g_specs, out_specs=[blk] * 4, out_shape=[shape] * 4,
        compiler_params=_params(("arbitrary", "arbitrary")),
    )(w, m, v, *grads)


def _place():
    x, y, c = lax.axis_index("x"), lax.axis_index("y"), lax.axis_index("c")
    chips = [(1 - x, y), (x, 1 - y), (1 - x, 1 - y)]
    return x, y, c, chips


class _Rider:
    n = 0
    relay = None

    def ride(self, grid, refs, when):
        if not self.n:
            return
        ids = [pl.program_id(a) for a in range(len(grid))]
        first = functools.reduce(jnp.logical_and, [i == 0 for i in ids])
        last = functools.reduce(jnp.logical_and, [i == g - 1 for i, g in zip(ids, grid)])
        phases = [(first, self.start), (last, self.relay)] if when == "start" else [(last, self.finish)]
        for cond, phase in phases:
            if phase is not None:
                pl.when(cond)(functools.partial(phase, *refs))


class _Gather(_Rider):
    PER_GROUP = 7

    def __init__(self, groups):
        self.groups = list(groups)
        self.n = len(self.groups)
        self.in_specs = [ANY] * self.n
        self.out_specs = [ANY] * self.n
        self.out_shape = [jax.ShapeDtypeStruct((NSH,) + g.shape, g.dtype) for g in self.groups]
        sems = pltpu.SemaphoreType.DMA((self.PER_GROUP * self.n,))
        self.scratch = [sems, sems] if self.n else []

    def _copies(self, ins, outs, send, recv):
        x, y, c, chips = _place()
        sibling = (x, y, 1 - c)

        def half(gi, chip, hc):
            return outs[gi].at[2 * chip[0] + chip[1], hc]

        def copy(gi, k, src, dst, to):
            sem = self.PER_GROUP * gi + k
            return pltpu.make_async_remote_copy(src_ref=src, dst_ref=dst, send_sem=send.at[sem], recv_sem=recv.at[sem],
                                                device_id=to, device_id_type=MESH)

        pairs = [(gi, j, chip) for gi in range(self.n) for j, chip in enumerate(chips)]
        first = [copy(gi, j, ins[gi].at[c], half(gi, (x, y), c), (*chip, c)) for gi, j, chip in pairs]
        first += [copy(gi, 6, ins[gi], outs[gi].at[2 * x + y], sibling) for gi in range(self.n)]
        landed = [copy(gi, j, half(gi, chip, c), half(gi, chip, c), sibling) for gi, j, chip in pairs]
        relay = [copy(gi, 3 + j, half(gi, chip, c), half(gi, chip, c), sibling) for gi, j, chip in pairs]
        relayed = [copy(gi, 3 + j, half(gi, chip, 1 - c), half(gi, chip, 1 - c), sibling) for gi, j, chip in pairs]
        relayed += [copy(gi, 6, ins[gi], outs[gi].at[2 * x + y], sibling) for gi in range(self.n)]
        return first, landed, relay, relayed

    def start(self, ins, outs, send, recv):
        for cp in self._copies(ins, outs, send, recv)[0]:
            cp.start()

    def relay(self, ins, outs, send, recv):
        _, landed, relay, _ = self._copies(ins, outs, send, recv)
        for arrived, onward in zip(landed, relay):
            arrived.wait_recv()
            onward.start()

    def finish(self, ins, outs, send, recv):
        first, _, relay, relayed = self._copies(ins, outs, send, recv)
        for cp in relayed:
            cp.wait_recv()
        for cp in first + relay:
            cp.wait_send()

    def place(self, outs):
        return list(outs)


def _alone(name, rider):
    n = rider.n

    def body(*refs):
        parts = (refs[:n], refs[n:2 * n], refs[2 * n], refs[2 * n + 1])
        rider.start(*parts)
        if rider.relay is not None:
            rider.relay(*parts)
        rider.finish(*parts)

    outs = pl.pallas_call(
        body, name=name, in_specs=rider.in_specs, out_specs=rider.out_specs, out_shape=rider.out_shape,
        scratch_shapes=rider.scratch, compiler_params=pltpu.CompilerParams(has_side_effects=True),
    )(*rider.groups)
    return rider.place(outs)


class _Swap(_Rider):
    def __init__(self, slots):
        self.slots = list(slots)
        self.groups = [buf for buf, _, _ in self.slots]
        self.n = len(self.slots)
        self.in_specs = [ANY] * self.n
        self.out_specs = [ANY] * self.n
        self.out_shape = [jax.ShapeDtypeStruct((NSH, rows // 2, buf.shape[2]), buf.dtype) for buf, _, rows in self.slots]
        self.scratch = [pltpu.SemaphoreType.DMA((self.n,)), pltpu.SemaphoreType.DMA((self.n,))]

    def _copies(self, ins, outs, send, recv):
        x, y, c, _ = _place()
        cps = []
        for i, (_, row0, rows) in enumerate(self.slots):
            half = rows // 2
            src = ins[i].at[:, pl.ds(pl.multiple_of(row0 + (1 - c) * half, 16), half)]
            cps.append(pltpu.make_async_remote_copy(src_ref=src, dst_ref=outs[i], send_sem=send.at[i],
                                                    recv_sem=recv.at[i], device_id=(x, y, 1 - c), device_id_type=MESH))
        return cps

    def start(self, ins, outs, send, recv):
        for cp in self._copies(ins, outs, send, recv):
            cp.start()

    def finish(self, ins, outs, send, recv):
        for cp in self._copies(ins, outs, send, recv):
            cp.wait()

    def place(self, outs):
        return list(outs)


class _Share(_Rider):
    def __init__(self, arrays):
        self.groups = list(arrays)
        self.n = len(self.groups)
        self.in_specs = [ANY] * self.n
        self.out_specs = [ANY] * self.n
        self.out_shape = [jax.ShapeDtypeStruct((2,) + g.shape, g.dtype) for g in self.groups]
        self.scratch = [pltpu.SemaphoreType.DMA((self.n,)), pltpu.SemaphoreType.DMA((self.n,))]

    def _copies(self, ins, outs, send, recv, half):
        x, y, c, _ = _place()
        return [pltpu.make_async_remote_copy(src_ref=ins[gi], dst_ref=outs[gi].at[c if half == "mine" else 1 - c],
                                             send_sem=send.at[gi], recv_sem=recv.at[gi], device_id=(x, y, 1 - c),
                                             device_id_type=MESH) for gi in range(self.n)]

    def start(self, ins, outs, send, recv):
        for cp in self._copies(ins, outs, send, recv, "mine"):
            cp.start()

    def finish(self, ins, outs, send, recv):
        for cp in self._copies(ins, outs, send, recv, "theirs"):
            cp.wait_recv()
        for cp in self._copies(ins, outs, send, recv, "mine"):
            cp.wait_send()

    def place(self, outs):
        c = lax.axis_index("c")
        return [lax.dynamic_update_index_in_dim(o, g, c, 0) for o, g in zip(outs, self.groups)]


class _Send(_Rider):
    def __init__(self, arrays):
        self.groups = list(arrays)
        self.n = len(self.groups)
        self.in_specs = [ANY] * self.n
        self.out_specs = [ANY] * self.n
        self.out_shape = [jax.ShapeDtypeStruct((3,) + g.shape[1:], g.dtype) for g in self.groups]
        self.scratch = [pltpu.SemaphoreType.DMA((3 * self.n,)), pltpu.SemaphoreType.DMA((3 * self.n,))]

    def _copies(self, ins, outs, send, recv):
        x, y, c, chips = _place()
        return [pltpu.make_async_remote_copy(src_ref=ins[gi].at[2 * chip[0] + chip[1]], dst_ref=outs[gi].at[j],
                                             send_sem=send.at[3 * gi + j], recv_sem=recv.at[3 * gi + j],
                                             device_id=(*chip, c), device_id_type=MESH)
                for gi in range(self.n) for j, chip in enumerate(chips)]

    def start(self, ins, outs, send, recv):
        for cp in self._copies(ins, outs, send, recv):
            cp.start()

    def finish(self, ins, outs, send, recv):
        for cp in self._copies(ins, outs, send, recv):
            cp.wait()

    def place(self, outs):
        return list(outs)


def _pair_sum(name, slots, got, c_idx):
    n = len(slots)
    in_specs, out_specs, out_shape, operands = [], [], [], []
    for (buf, row0, rows), g in zip(slots, got):
        hb, cols = rows // 4, buf.shape[2]
        in_specs += [pl.BlockSpec((None, hb, cols), lambda q, i, cr, r=row0 // hb: (q, r + 2 * cr[0] + i, 0)),
                     pl.BlockSpec((None, hb, cols), lambda q, i, cr: (q, i, 0))]
        out_specs.append(pl.BlockSpec((None, hb, cols), lambda q, i, cr: (q, i, 0)))
        out_shape.append(jax.ShapeDtypeStruct(g.shape, BF16))
        operands += [buf, g]

    def body(c_ref, *refs):
        for i in range(n):
            refs[2 * n + i][...] = (refs[2 * i][...].astype(F32) + refs[2 * i + 1][...].astype(F32)).astype(BF16)

    return pl.pallas_call(
        body, name=name,
        grid_spec=pltpu.PrefetchScalarGridSpec(num_scalar_prefetch=1, grid=(NSH, 2), in_specs=in_specs, out_specs=out_specs),
        out_shape=out_shape, compiler_params=_params(("parallel", "parallel")),
    )(c_idx, *operands)


def _owner_sum(name, pairs, got, p_idx):
    n = len(pairs)
    in_specs, out_specs, out_shape, operands = [], [], [], []
    for own, g in zip(pairs, got):
        _, rows, cols = own.shape
        hb = rows // 2
        in_specs += [pl.BlockSpec((None, hb, cols), lambda i, pr: (pr[0], i, 0)),
                     pl.BlockSpec((3, hb, cols), lambda i, pr: (0, i, 0))]
        out_specs.append(pl.BlockSpec((hb, cols), lambda i, pr: (i, 0)))
        out_shape.append(jax.ShapeDtypeStruct((rows, cols), F32))
        operands += [own, g]

    def body(p_ref, *refs):
        for i in range(n):
            a_ref, b_ref = refs[2 * i], refs[2 * i + 1]
            refs[2 * n + i][...] = ((a_ref[...].astype(F32) + b_ref[0].astype(F32)) + b_ref[1].astype(F32)) + b_ref[2].astype(F32)

    return pl.pallas_call(
        body, name=name,
        grid_spec=pltpu.PrefetchScalarGridSpec(num_scalar_prefetch=1, grid=(2,), in_specs=in_specs, out_specs=out_specs),
        out_shape=out_shape, compiler_params=_params(("parallel",)),
    )(p_idx, *operands)


def _sum_small(slab):
    def body(in_ref, out_ref, all_ref, send, recv):
        x, y, c, _ = _place()
        me = 4 * x + 2 * y + c
        all_ref[me] = in_ref[...]
        cps = []
        for k in range(1, 8):
            peer = (x ^ (k >> 2), y ^ ((k >> 1) & 1), c ^ (k & 1))
            cps.append(pltpu.make_async_remote_copy(src_ref=in_ref, dst_ref=all_ref.at[me], send_sem=send.at[k - 1],
                                                    recv_sem=recv.at[k - 1], device_id=peer, device_id_type=MESH))
        for cp in cps:
            cp.start()
        for cp in cps:
            cp.wait()
        total = all_ref[0]
        for d in range(1, 8):
            total = total + all_ref[d]
        out_ref[...] = total

    return pl.pallas_call(
        body, name="sum_small",
        in_specs=[pl.BlockSpec(memory_space=pltpu.VMEM)], out_specs=pl.BlockSpec(memory_space=pltpu.VMEM),
        out_shape=jax.ShapeDtypeStruct(slab.shape, F32),
        scratch_shapes=[pltpu.VMEM((8,) + slab.shape, F32), pltpu.SemaphoreType.DMA((7,)), pltpu.SemaphoreType.DMA((7,))],
        compiler_params=pltpu.CompilerParams(has_side_effects=True),
    )(slab)


FFNS = ("pre0", "post0", "pre1", "post1")
W_SHAPES = dict({f + "_gu": (NSH, 2, FS, D) for f in FFNS}, **{f + "_d": (NSH, FS, D) for f in FFNS},
                ab_in=(NSH, D, 768), ab_out=(NSH, 256, D), conv=(NSH, 2, 8, LANE), c_in=(NSH, D, D), c_out=(NSH, 256, D))
CARRIED = dict(pre0_norm=("pre0_gu",), pre0_up=("pre0_d",), pre0_down=("ab_in", "ab_out", "conv"),
               sb_fwd=("post0_gu", "post0_d"), post0_up=("pre1_gu",), post0_down=("pre1_d",),
               pre1_up=("c_in", "c_out"), hgrn_fwd=("post1_gu", "post1_d"))


FFN_SLOTS = dict(pre0=(0, 2, 0), pre1=(1, 3, 1), post0=(4, 6, 2), post1=(5, 7, 3))
GRAD_SLOTS = dict(ab_out=("b", B_ABOUT, 256), c_in=("b", B_CIN, D), c_out=("b", B_COUT, 256), ab_in=("c", 0, D))
for _f, (_g, _u, _d) in FFN_SLOTS.items():
    GRAD_SLOTS.update({_f + "_g": ("a", _g * FS, FS), _f + "_u": ("a", _u * FS, FS), _f + "_d": ("b", _d * FS, FS)})
REDUCE_STAGES = dict(x=("post1_g", "post1_u", "post1_d", "c_out"), w=("c_in", "pre1_d"),
                     y=("pre1_g", "pre1_u", "post0_g", "post0_u", "post0_d", "ab_out"),
                     z1=("ab_in",), z2=("pre0_g", "pre0_u", "pre0_d"))


def _local_step(x, target, weights, small, shards=None, place=None):
    t = x.shape[0]
    tm = min(MM_TILE, t)
    tw = min(WGRAD_TILE, t)
    nt = t // tm
    tok_si = _spec((tm, D), lambda s, i: (i, 0))
    w = dict(weights)
    reduced = {}

    def stage_slots(stage, buffers):
        return [(buffers[GRAD_SLOTS[n][0]],) + GRAD_SLOTS[n][1:] for n in REDUCE_STAGES[stage]]

    def swap_rider(stage, buffers):
        return _Swap(stage_slots(stage, buffers)) if place is not None else None

    def reduce_start(stage, buffers, swapped=None):
        if place is None:
            return [], None
        slots = stage_slots(stage, buffers)
        if swapped is None:
            swapped = _alone("grad_swap_" + stage, _Swap(slots))
        pairs = _pair_sum("grad_pair_sum_" + stage, slots, swapped, place[0])
        return pairs, _Send(pairs)

    def reduce_end(stage, pairs, landed):
        if place is None:
            return None
        mine = _owner_sum("grad_owner_sum_" + stage, pairs, landed, place[1])
        return _Share(mine)

    def shared(stage, landed):
        if place is not None:
            reduced.update(zip(REDUCE_STAGES[stage], landed))

    def carried(kernel_name):
        names = [n for n in CARRIED[kernel_name] if n not in w]
        return names, (_Gather([shards[n] for n in names]) if names else None)

    def land(names, arrays):
        for n, a in zip(names, arrays):
            w[n] = a.reshape(W_SHAPES[n])

    def ffn_forward(tag, h, hn, next_row):
        names, gather = carried(tag + "_up") if tag + "_up" in CARRIED else ([], None)
        s_up, s_gate, a, got = _ffn_up(tag + "_up", hn, w[tag + "_gu"], gather)
        land(names, got)
        names, gather = carried(tag + "_down") if tag + "_down" in CARRIED else ([], None)
        out = _ffn_down(tag + "_down", a, w[tag + "_d"], h, gather, (small, next_row) if next_row is not None else None)
        if gather is not None:
            out, got = out
            land(names, got)
        out, hn_next = out if next_row is not None else (out, None)
        return out, hn_next, (h, hn, s_up, s_gate, a)

    def out_proj(name, y, w_out, h, next_row):
        return _mm(name, [(y, _spec((tm, 256), lambda i, k: (i, k)), w_out, _spec((None, 256, D), lambda i, k: (k, 0, 0)))],
                   grid=(nt, NSH), o_shape=(t, D), o_dtype=F32, o_spec=_spec((tm, D), lambda i, k: (i, 0)),
                   dims=NN, kaxis=1, nk=NSH, acc_shape=(tm, D), res=(h, _spec((tm, D), lambda i, k: (i, 0))),
                   norm=(small, next_row))

    def out_proj_bwd(tag, dhb, y, w_out, blk, grad_b, make_rider=None):
        grad_b = _wgrad(tag + "_dwout", y, _spec((tw, 256), lambda s, k: (k, s)), dhb, _spec((tw, D), lambda s, k: (k, 0)),
                        256, D, t, tw, grad_b, blk)
        rider = make_rider(grad_b) if make_rider is not None else None
        dy = _mm(tag + "_dy", [(dhb, tok_si, w_out, _spec((None, 256, D), lambda s, i: (s, 0, 0)))],
                 grid=(NSH, nt), o_shape=(t, D), o_dtype=F32, o_spec=_spec((tm, 256), lambda s, i: (i, s)),
                 dims=NT, kaxis=1, nk=1, carry=rider)
        dy, landed = dy if rider is not None else (dy, None)
        return dy, grad_b, landed

    h0 = x
    names, gather = carried("pre0_norm")
    hn, got = _norm_fwd("pre0_norm", h0, small, R_PRE, gather)
    land(names, got)
    h1, hn_ab, pre0 = ffn_forward("pre0", h0, hn, R_MIX)
    proj_a = _mm("ab_proj_a", [(hn_ab, tok_si, w["ab_in"], _spec((None, D, 768), lambda s, i: (s, 0, 0)))],
                 grid=(2, nt), o_shape=(t, 1536), o_dtype=F32, o_spec=_spec((tm, 768), lambda s, i: (i, s)),
                 dims=NN, kaxis=1, nk=1)
    proj_b = _mm("ab_proj_b", [(hn_ab, tok_si, w["ab_in"], _spec((None, D, 768), lambda s, i: (s + 2, 0, 0)))],
                 grid=(2, nt), o_shape=(t, 1536), o_dtype=BF16, o_spec=_spec((tm, 768), lambda s, i: (i, s)),
                 dims=NN, kaxis=1, nk=1)
    y_a = _conv_fwd(proj_a, w["conv"])
    names, gather = carried("sb_fwd")
    y_b, ltot, nblk, got = _sb_fwd(proj_b, gather)
    land(names, got)
    y_ab = jnp.concatenate([y_a, y_b], axis=1)
    h2, hn = out_proj("ab_out", y_ab, w["ab_out"], h1, R_POST)
    h3, hn, post0 = ffn_forward("post0", h2, hn, R_PRE + 1)
    h4, hn_c, pre1 = ffn_forward("pre1", h3, hn, R_MIX + 1)
    proj_c = _mm("c_proj", [(hn_c, tok_si, w["c_in"], _spec((None, D, D), lambda s, i: (s, 0, 0)))],
                 grid=(NSH, nt), o_shape=(NSH, t, D), o_dtype=F32, o_spec=_spec((None, tm, D), lambda s, i: (s, i, 0)),
                 dims=NN, kaxis=1, nk=1)
    names, gather = carried("hgrn_fwd")
    o_c, y_c, sst, got = _hgrn_fwd(proj_c, small, gather)
    land(names, got)
    h5, hn = out_proj("c_out", y_c, w["c_out"], h4, R_POST + 1)
    h6, _, post1 = ffn_forward("post1", h5, hn, None)
    dh, dhb, d_fin, loss = _final_loss(h6, small, target)

    dh, dhb, grad_a, grad_b, dn_post1, _, _ = _ffn_backward("post1", dh, dhb, *post1, w["post1_gu"], w["post1_d"],
                                                            *FFN_SLOTS["post1"], small, R_POST + 1, 8 * FS, B_ROWS)
    dy_c, grad_b, swapped = out_proj_bwd("c", dhb, y_c, w["c_out"], B_COUT // 256, grad_b,
                                         lambda buf_b: swap_rider("x", dict(a=grad_a, b=buf_b)))
    pairs, rider = reduce_start("x", dict(a=grad_a, b=grad_b), swapped)
    dproj_c, d_clb, d_gam, landed = _hgrn_bwd(proj_c, small, o_c, sst, dy_c, rider)
    share = reduce_end("x", pairs, landed)
    grad_b = _wgrad("c_dwin", hn_c, _spec((tw, D), lambda s, k: (k, 0)), dproj_c, _spec((None, tw, D), lambda s, k: (s, k, 0)),
                    D, D, t, tw, grad_b, B_CIN // D, carry=share)
    if share is not None:
        grad_b, landed = grad_b
        shared("x", landed)
    dh, dhb, dn_mix1, _ = _dhn_norm("c_dhn", [(dproj_c, _spec((None, tm, D), lambda i, k: (k, i, 0)),
                                               w["c_in"], _spec((None, D, D), lambda i, k: (k, 0, 0)))],
                                    NT, h4, small, R_MIX + 1, dh)
    stage_w = {}

    def w_swap(buf_a, buf_b):
        return swap_rider("w", dict(a=buf_a, b=buf_b))

    def w_send(buf_a, buf_b, swapped):
        stage_w["pairs"], rider = reduce_start("w", dict(a=buf_a, b=buf_b), swapped)
        return rider

    on = place is not None
    dh, dhb, grad_a, grad_b, dn_pre1, _, landed = _ffn_backward(
        "pre1", dh, dhb, *pre1, w["pre1_gu"], w["pre1_d"], *FFN_SLOTS["pre1"], small, R_PRE + 1, grad_a, grad_b,
        w_swap if on else None, w_send if on else None)
    share = reduce_end("w", stage_w.get("pairs"), landed)
    dh, dhb, grad_a, grad_b, dn_post0, landed, _ = _ffn_backward(
        "post0", dh, dhb, *post0, w["post0_gu"], w["post0_d"], *FFN_SLOTS["post0"], small, R_POST, grad_a, grad_b,
        (lambda buf_a, buf_b: share) if on else None)
    shared("w", landed)
    dy_ab, grad_b, _ = out_proj_bwd("ab", dhb, y_ab, w["ab_out"], B_ABOUT // 256, grad_b)
    dab, dac, dax, d_conv, swapped = _conv_bwd(proj_a, w["conv"], dy_ab, swap_rider("y", dict(a=grad_a, b=grad_b)))
    pairs, rider = reduce_start("y", dict(a=grad_a, b=grad_b), swapped)
    dq, dk, dv, landed = _sb_bwd(proj_b, dy_ab, ltot, nblk, rider)
    share = reduce_end("y", pairs, landed)
    dproj_ab = jnp.concatenate([dab, dac, dax, dq, dk, dv], axis=1)
    grad_c = _wgrad("ab_dwin", hn_ab, _spec((tw, D), lambda s, k: (k, 0)), dproj_ab, _spec((tw, 768), lambda s, k: (k, s)),
                    D, 768, t, tw, D, 0, carry=share)
    if share is not None:
        grad_c, landed = grad_c
        shared("y", landed)
    dh, dhb, dn_mix0, _ = _dhn_norm("ab_dhn", [(dproj_ab, _spec((tm, 768), lambda i, k: (i, k)),
                                                w["ab_in"], _spec((None, D, 768), lambda i, k: (k, 0, 0)))],
                                    NT, h1, small, R_MIX, dh)
    last = {}

    def z1_send(buf_a, buf_b):
        last["z1"], rider = reduce_start("z1", dict(b=buf_b, c=grad_c))
        return rider

    def z2_send(buf_a, buf_b, _):
        last["z2"], rider = reduce_start("z2", dict(a=buf_a, b=buf_b))
        return rider

    dh, dhb, grad_a, grad_b, dn_pre0, landed, landed_own = _ffn_backward(
        "pre0", dh, dhb, *pre0, w["pre0_gu"], w["pre0_d"], *FFN_SLOTS["pre0"], small, R_PRE, grad_a, grad_b,
        z1_send if on else None, z2_send if on else None)
    if place is not None:
        mine = (_owner_sum("grad_owner_sum_z1", last["z1"], landed, place[1])
                + _owner_sum("grad_owner_sum_z2", last["z2"], landed_own, place[1]))
        reduced.update(zip(REDUCE_STAGES["z1"] + REDUCE_STAGES["z2"], _alone("grad_share_z", _Share(mine))))
    zero = jnp.zeros((1, D), F32)
    conv_rows = jnp.pad(jnp.transpose(d_conv[:, :3, :], (1, 0, 2)).reshape(3, 512), ((0, 0), (0, D - 512)))
    small_grad = jnp.concatenate([
        dn_pre0, dn_pre1, dn_mix0, dn_mix1, dn_post0, dn_post1, d_clb, d_fin,
        jnp.pad(d_gam, ((0, 0), (0, D - HD))), conv_rows,
        jnp.pad(loss, ((0, 0), (0, D - 1))), zero, zero], axis=0)
    return dh, grad_a, grad_b, grad_c, small_grad, reduced


def _small_slab(rows):
    parts = [jnp.pad(r.astype(F32), ((0, 0), (0, D - r.shape[1]))) for r in rows]
    slab = jnp.concatenate(parts, axis=0)
    return jnp.pad(slab, ((0, SMALL_ROWS - slab.shape[0]), (0, 0)))


def kernel(x, ffn_pre_norm, ffn_pre_w_gate, ffn_pre_w_up, ffn_pre_w_down, mix_norm, ffn_post_norm, ffn_post_w_gate, ffn_post_w_up, ffn_post_w_down, ab_w_in, ab_conv_w, ab_w_out, c_w_in, c_lower_bounds, c_out_norm, c_w_out, final_norm, loss_target, m_ffn_pre_norm, m_ffn_pre_w_gate, m_ffn_pre_w_up, m_ffn_pre_w_down, m_mix_norm, m_ffn_post_norm, m_ffn_post_w_gate, m_ffn_post_w_up, m_ffn_post_w_down, m_ab_w_in, m_ab_conv_w, m_ab_w_out, m_c_w_in, m_c_lower_bounds, m_c_out_norm, m_c_w_out, m_final_norm, v_ffn_pre_norm, v_ffn_pre_w_gate, v_ffn_pre_w_up, v_ffn_pre_w_down, v_mix_norm, v_ffn_post_norm, v_ffn_post_w_gate, v_ffn_post_w_up, v_ffn_post_w_down, v_ab_w_in, v_ab_conv_w, v_ab_w_out, v_c_w_in, v_c_lower_bounds, v_c_out_norm, v_c_w_out, v_final_norm):
    t = x.shape[1]
    xi, yi, ci = lax.axis_index("x"), lax.axis_index("y"), lax.axis_index("c")
    p_idx = (2 * xi + yi).astype(jnp.int32).reshape(1)
    c_idx = ci.astype(jnp.int32).reshape(1)

    def halves(m):
        return m.astype(BF16).reshape(2, m.shape[0] // 2, m.shape[1])

    transposed = ("ffn_pre_w_gate", "ffn_pre_w_up", "ffn_post_w_gate", "ffn_post_w_up")

    def flip(a):
        return jnp.swapaxes(a, 1, 2)

    shards = {}
    for name, (w_gate, w_up, w_down, layer) in dict(
            pre0=(ffn_pre_w_gate, ffn_pre_w_up, ffn_pre_w_down, 0), post0=(ffn_post_w_gate, ffn_post_w_up, ffn_post_w_down, 0),
            pre1=(ffn_pre_w_gate, ffn_pre_w_up, ffn_pre_w_down, 1), post1=(ffn_post_w_gate, ffn_post_w_up, ffn_post_w_down, 1)).items():
        shards[name + "_gu"] = jnp.stack([flip(w_gate)[layer], flip(w_up)[layer]]).astype(BF16)
        shards[name + "_d"] = halves(w_down[layer])
    conv_pad = jnp.pad(ab_conv_w[0], ((0, 5), (0, 0)))
    shards.update(ab_in=halves(ab_w_in[0]), ab_out=halves(ab_w_out[0]), c_in=halves(c_w_in[0]), c_out=halves(c_w_out[0]),
                  conv=jnp.stack([conv_pad, jnp.zeros_like(conv_pad)]))
    small = _small_slab([ffn_pre_norm, mix_norm, ffn_post_norm, c_lower_bounds, final_norm.reshape(1, D), c_out_norm])
    small = small.reshape(SMALL_ROWS, 1, D)

    grad_x, _, _, _, small_grad, reduced = _local_step(x[0], loss_target[0], {}, small, shards,
                                                        (c_idx, p_idx))
    whole = {n: g.reshape(2 * g.shape[1], g.shape[2]) for n, g in reduced.items()}
    small_sum = _sum_small(small_grad)

    my_conv = lax.dynamic_slice(small_sum[R_CONV:R_CONV + 3], (0, (2 * xi + yi) * 128), (3, 128))
    grads = {
        "ffn_pre_norm": small_sum[R_PRE:R_PRE + 2], "mix_norm": small_sum[R_MIX:R_MIX + 2],
        "ffn_post_norm": small_sum[R_POST:R_POST + 2], "c_lower_bounds": small_sum[R_CLB:R_CLB + 2],
        "c_out_norm": small_sum[R_GAM:R_GAM + 1, :HD], "final_norm": small_sum[R_FIN],
        "ab_conv_w": my_conv.reshape(1, 3, 128),
    }
    layers = dict(ab_w_in=[whole["ab_in"]], ab_w_out=[whole["ab_out"]], c_w_in=[whole["c_in"]], c_w_out=[whole["c_out"]])
    for kind, key in (("gate", "_g"), ("up", "_u"), ("down", "_d")):
        layers["ffn_pre_w_" + kind] = [whole["pre0" + key], whole["pre1" + key]]
        layers["ffn_post_w_" + kind] = [whole["post0" + key], whole["post1" + key]]
    weights = dict(ffn_pre_norm=ffn_pre_norm, ffn_pre_w_gate=ffn_pre_w_gate, ffn_pre_w_up=ffn_pre_w_up, ffn_pre_w_down=ffn_pre_w_down, mix_norm=mix_norm, ffn_post_norm=ffn_post_norm, ffn_post_w_gate=ffn_post_w_gate, ffn_post_w_up=ffn_post_w_up, ffn_post_w_down=ffn_post_w_down, ab_w_in=ab_w_in, ab_conv_w=ab_conv_w, ab_w_out=ab_w_out, c_w_in=c_w_in, c_lower_bounds=c_lower_bounds, c_out_norm=c_out_norm, c_w_out=c_w_out, final_norm=final_norm)
    m_in = dict(ffn_pre_norm=m_ffn_pre_norm, ffn_pre_w_gate=m_ffn_pre_w_gate, ffn_pre_w_up=m_ffn_pre_w_up, ffn_pre_w_down=m_ffn_pre_w_down, mix_norm=m_mix_norm, ffn_post_norm=m_ffn_post_norm, ffn_post_w_gate=m_ffn_post_w_gate, ffn_post_w_up=m_ffn_post_w_up, ffn_post_w_down=m_ffn_post_w_down, ab_w_in=m_ab_w_in, ab_conv_w=m_ab_conv_w, ab_w_out=m_ab_w_out, c_w_in=m_c_w_in, c_lower_bounds=m_c_lower_bounds, c_out_norm=m_c_out_norm, c_w_out=m_c_w_out, final_norm=m_final_norm)
    v_in = dict(ffn_pre_norm=v_ffn_pre_norm, ffn_pre_w_gate=v_ffn_pre_w_gate, ffn_pre_w_up=v_ffn_pre_w_up, ffn_pre_w_down=v_ffn_pre_w_down, mix_norm=v_mix_norm, ffn_post_norm=v_ffn_post_norm, ffn_post_w_gate=v_ffn_post_w_gate, ffn_post_w_up=v_ffn_post_w_up, ffn_post_w_down=v_ffn_post_w_down, ab_w_in=v_ab_w_in, ab_conv_w=v_ab_conv_w, ab_w_out=v_ab_w_out, c_w_in=v_c_w_in, c_lower_bounds=v_c_lower_bounds, c_out_norm=v_c_out_norm, c_w_out=v_c_w_out, final_norm=v_final_norm)
    names = list(weights)
    big = [n for n in names if weights[n].size >= 65536]
    tiny = [n for n in names if n not in big]

    delta, new_m, new_v = {}, {}, {}
    for n in big:
        turn = flip if n in transposed else (lambda a: a)
        shape = turn(weights[n]).shape
        two_d = (shape[0] * shape[1], shape[2])
        d, m2, v2, g2 = _adamw("adamw_" + n, turn(weights[n]).reshape(two_d), layers[n],
                               turn(m_in[n]).reshape(two_d), turn(v_in[n]).reshape(two_d))
        delta[n], new_m[n], new_v[n] = turn(d.reshape(shape)), turn(m2.reshape(shape)), turn(v2.reshape(shape))
        grads[n] = turn(g2.reshape(shape))

    def tiny_slab(src):
        return _small_slab([src[n].reshape(-1, src[n].shape[-1]) for n in tiny])

    offs, row = {}, 0
    for n in tiny:
        nrows = weights[n].size // weights[n].shape[-1]
        offs[n] = (row, nrows)
        row += nrows
    d, m2, v2, _ = _adamw("adamw_small", tiny_slab(weights), [tiny_slab(grads)], tiny_slab(m_in), tiny_slab(v_in))
    for n in tiny:
        r0, nr = offs[n]
        shape = weights[n].shape
        for dst, src in ((delta, d), (new_m, m2), (new_v, v2)):
            dst[n] = src[r0:r0 + nr, :shape[-1]].reshape(shape)

    loss = small_sum[R_LOSS, 0]
    return (loss, grad_x.reshape(1, t, D), *[grads[n] for n in names], *[delta[n] for n in names],
            *[new_m[n] for n in names], *[new_v[n] for n in names])
```

```python
import functools
import math

import jax
import jax.numpy as jnp
from jax import lax
from jax.experimental import pallas as pl
from jax.experimental.pallas import tpu as pltpu

F32 = jnp.float32
BF16 = jnp.bfloat16
MESH = pl.DeviceIdType.MESH
ANY = pl.BlockSpec(memory_space=pl.ANY)

D = 1024
FS = 704
NSH = 4
RMS_EPS = 1e-6
MACARON = 0.5
CHUNK = 64
HD = 128
HGRN_HPS = 8
SBQ = 128
SB_PAIRS_FWD = 4
SB_PAIRS_BWD = 4
SB_DEAD = -105.0
CONV_HALO = 8
LANE = 128
ROW_TILE = 1024
MM_TILE = 1024
WGRAD_TILE = 4096
VMEM_LIMIT = 48 * 1024 * 1024
VMEM_LIMIT_BIG = 58 * 1024 * 1024

ADAM_LR, ADAM_B1, ADAM_B2, ADAM_EPS, ADAM_WD, ADAM_STEP = 0.001, 0.9, 0.999, 1e-08, 0.01, 10

NN = ((1,), (0,))
NT = ((1,), (1,))
TN = ((0,), (0,))

SMALL_ROWS = 16
R_PRE, R_MIX, R_POST, R_CLB, R_FIN, R_GAM, R_CONV, R_LOSS = 0, 2, 4, 6, 8, 9, 10, 13

B_DOWN = 0
B_ABOUT = 4 * FS
B_CIN = B_ABOUT + 256
B_COUT = B_CIN + 1024
B_ROWS = B_COUT + 256


def _dg(a, b, dims):
    return lax.dot_general(a, b, (dims, ((), ())), preferred_element_type=F32)


def _split(x):
    hi = x.astype(BF16)
    lo = (x - hi.astype(F32)).astype(BF16)
    return hi, lo


def _dot3(a, b, dims):
    ah, al = _split(a)
    bh, bl = _split(b)
    if dims == TN:
        n = b.shape[1]
        both = _dg(ah, jnp.concatenate([bh, bl], axis=1), dims)
        return both[:, :n] + both[:, n:] + _dg(al, bh, dims)
    m = a.shape[0]
    both = _dg(jnp.concatenate([ah, al], axis=0), bh, dims)
    return both[:m] + both[m:] + _dg(ah, bl, dims)


def _sigmoid(x):
    return 1.0 / (1.0 + jnp.exp(-x))


def _params(sem):
    return pltpu.CompilerParams(dimension_semantics=sem, vmem_limit_bytes=VMEM_LIMIT)


def _spec(shape, imap):
    return pl.BlockSpec(shape, imap)


def _accumulate(acc_ref, pairs, dims):
    part = None
    for a_ref, b_ref in pairs:
        d = _dg(a_ref[...], b_ref[...], dims)
        part = d if part is None else part + d
    acc_ref[...] += part


def _mm(name, pairs, *, grid, o_shape, o_dtype, o_spec, dims, kaxis, nk, acc_shape=None, res=None, scale=None,
        into=None, carry=None, norm=None):
    npairs = len(pairs)
    operands, specs = [], []
    for a, a_spec, b, b_spec in pairs:
        operands += [a, b]
        specs += [a_spec, b_spec]
    if res is not None:
        operands.append(res[0])
        specs.append(res[1])
    aliases = {}
    if into is not None:
        aliases = {len(operands): 0}
        operands.append(into)
        specs.append(ANY)
    if norm is not None:
        operands.append(norm[0])
        specs.append(_spec((None, 1, D), lambda *_: (norm[1], 0, 0)))
    n_own = len(operands)
    n_out = 2 if norm is not None else 1
    nc = carry.n if carry is not None else 0
    if nc:
        operands += carry.groups
        specs += carry.in_specs

    def body(*refs):
        o_ref = refs[n_own + nc]
        riders = ((refs[n_own:n_own + nc], refs[n_own + nc + n_out:n_own + 2 * nc + n_out], refs[-2], refs[-1])
                  if nc else None)
        if nc:
            carry.ride(grid, riders, "start")

        def finish(val):
            if scale is not None:
                val = val * scale
            if res is not None:
                val = val + refs[2 * npairs][...]
            o_ref[...] = val.astype(o_dtype)
            if norm is not None:
                r = lax.rsqrt(jnp.mean(val * val, axis=-1, keepdims=True) + RMS_EPS)
                refs[n_own + nc + 1][...] = (val * r * refs[n_own - 1][...]).astype(BF16)

        if nk == 1:
            part = None
            for n in range(npairs):
                d = _dg(refs[2 * n][...], refs[2 * n + 1][...], dims)
                part = d if part is None else part + d
            finish(part)
        else:
            acc_ref = refs[n_own + 2 * nc + n_out]
            k = pl.program_id(kaxis)

            @pl.when(k == 0)
            def _():
                acc_ref[...] = jnp.zeros_like(acc_ref)

            _accumulate(acc_ref, [(refs[2 * n], refs[2 * n + 1]) for n in range(npairs)], dims)

            @pl.when(k == nk - 1)
            def _():
                finish(acc_ref[...])
        if nc:
            carry.ride(grid, riders, "finish")

    sem = tuple("arbitrary" if (nc or (ax == kaxis and nk > 1)) else "parallel" for ax in range(len(grid)))
    outs = pl.pallas_call(
        body, name=name, grid=grid, in_specs=specs,
        out_specs=[o_spec] * n_out + (carry.out_specs if nc else []),
        out_shape=[jax.ShapeDtypeStruct(o_shape, o_dtype)] + ([jax.ShapeDtypeStruct(o_shape, BF16)] if norm is not None else [])
        + (carry.out_shape if nc else []),
        scratch_shapes=([pltpu.VMEM(acc_shape, F32)] if nk > 1 else []) + (carry.scratch if nc else []),
        input_output_aliases=aliases,
        compiler_params=_params(sem),
    )(*operands)
    main = (outs[0], outs[1]) if norm is not None else outs[0]
    return (main, carry.place(outs[n_out:])) if nc else main


def _norm_fwd(name, h, gain_slab, row, rider=None):
    t = h.shape[0]
    tm = min(ROW_TILE, t)
    grid = (t // tm,)
    nr = rider.n if rider is not None else 0

    def body(*refs):
        h_ref, g_ref, o_ref = refs[0], refs[1], refs[2 + nr]
        riders = (refs[2:2 + nr], refs[3 + nr:3 + 2 * nr], refs[-2], refs[-1]) if nr else None
        if nr:
            rider.ride(grid, riders, "start")
        x = h_ref[...]
        r = lax.rsqrt(jnp.mean(x * x, axis=-1, keepdims=True) + RMS_EPS)
        o_ref[...] = (x * r * g_ref[...]).astype(BF16)
        if nr:
            rider.ride(grid, riders, "finish")

    outs = pl.pallas_call(
        body, name=name, grid=grid,
        in_specs=[_spec((tm, D), lambda i: (i, 0)), _spec((None, 1, D), lambda i: (row, 0, 0))]
        + (rider.in_specs if nr else []),
        out_specs=[_spec((tm, D), lambda i: (i, 0))] + (rider.out_specs if nr else []),
        out_shape=[jax.ShapeDtypeStruct((t, D), BF16)] + (rider.out_shape if nr else []),
        scratch_shapes=rider.scratch if nr else [],
        compiler_params=_params(("arbitrary",) if nr else ("parallel",)),
    )(h, gain_slab, *(rider.groups if nr else []))
    return outs[0], (rider.place(outs[1:]) if nr else [])


def _dhn_norm(name, pairs, dims, h, gain_slab, row, dres, rider=None):
    t = h.shape[0]
    tm = min(MM_TILE, t)
    nt = t // tm
    grid = (nt, NSH)
    npairs = len(pairs)
    nr = rider.n if rider is not None else 0
    operands, specs = [], []
    for a, a_spec, b, b_spec in pairs:
        operands += [a, b]
        specs += [a_spec, b_spec]
    row_spec = _spec((tm, D), lambda i, k: (i, 0))
    operands += [h, gain_slab, dres]
    specs += [row_spec, _spec((None, 1, D), lambda i, k: (row, 0, 0)), row_spec]
    n_own = len(operands)

    def body(*refs):
        h_ref, g_ref, dres_ref = refs[2 * npairs:n_own]
        dh_ref, dhb_ref, dg_ref = refs[n_own + nr:n_own + nr + 3]
        acc_ref, gacc_ref = refs[n_own + 2 * nr + 3:n_own + 2 * nr + 5]
        riders = (refs[n_own:n_own + nr], refs[n_own + nr + 3:n_own + 2 * nr + 3], refs[-2], refs[-1]) if nr else None
        if nr:
            rider.ride(grid, riders, "start")
        i, k = pl.program_id(0), pl.program_id(1)

        @pl.when(k == 0)
        def _():
            acc_ref[...] = jnp.zeros_like(acc_ref)

        _accumulate(acc_ref, [(refs[2 * n], refs[2 * n + 1]) for n in range(npairs)], dims)

        @pl.when(k == NSH - 1)
        def _():
            x = h_ref[...]
            r = lax.rsqrt(jnp.mean(x * x, axis=-1, keepdims=True) + RMS_EPS)
            xh = x * r
            dy = acc_ref[...]
            gdy = dy * g_ref[...]
            dh = dres_ref[...] + (gdy - xh * jnp.mean(gdy * xh, axis=-1, keepdims=True)) * r
            dh_ref[...] = dh
            dhb_ref[...] = dh.astype(BF16)
            gpart = jnp.sum((dy * xh).reshape(tm // 8, 8, D), axis=0)

            @pl.when(i == 0)
            def _():
                gacc_ref[...] = gpart

            @pl.when(i > 0)
            def _():
                gacc_ref[...] += gpart

            @pl.when(i == nt - 1)
            def _():
                dg_ref[...] = jnp.sum(gacc_ref[...], axis=0, keepdims=True)

        if nr:
            rider.ride(grid, riders, "finish")

    outs = pl.pallas_call(
        body, name=name, grid=grid, in_specs=specs + (rider.in_specs if nr else []),
        out_specs=[row_spec, row_spec, _spec((1, D), lambda i, k: (0, 0))] + (rider.out_specs if nr else []),
        out_shape=[jax.ShapeDtypeStruct((t, D), F32), jax.ShapeDtypeStruct((t, D), BF16),
                   jax.ShapeDtypeStruct((1, D), F32)] + (rider.out_shape if nr else []),
        scratch_shapes=[pltpu.VMEM((tm, D), F32), pltpu.VMEM((8, D), F32)] + (rider.scratch if nr else []),
        compiler_params=pltpu.CompilerParams(dimension_semantics=("arbitrary", "arbitrary"),
                                             vmem_limit_bytes=VMEM_LIMIT_BIG),
    )(*operands, *(rider.groups if nr else []))
    return outs[0], outs[1], outs[2], (rider.place(outs[3:]) if nr else [])


def _final_loss(h, gain_slab, target):
    t = h.shape[0]
    tm = min(ROW_TILE, t)
    nt = t // tm

    def body(h_ref, g_ref, t_ref, dh_ref, dhb_ref, dg_ref, loss_ref, acc_ref, lacc_ref):
        i = pl.program_id(0)
        x = h_ref[...]
        g = g_ref[...]
        r = lax.rsqrt(jnp.mean(x * x, axis=-1, keepdims=True) + RMS_EPS)
        xh = x * r
        err = xh * g - t_ref[...]
        dy = err * (1.0 / D)
        gdy = dy * g
        dh = (gdy - xh * jnp.mean(gdy * xh, axis=-1, keepdims=True)) * r
        dh_ref[...] = dh
        dhb_ref[...] = dh.astype(BF16)
        part = jnp.sum((dy * xh).reshape(tm // 8, 8, D), axis=0)
        lpart = jnp.sum((err * err).reshape(tm // 8, 8, D), axis=0)

        @pl.when(i == 0)
        def _():
            acc_ref[...] = part
            lacc_ref[...] = lpart

        @pl.when(i > 0)
        def _():
            acc_ref[...] += part
            lacc_ref[...] += lpart

        @pl.when(i == nt - 1)
        def _():
            dg_ref[...] = jnp.sum(acc_ref[...], axis=0, keepdims=True)
            rows = jnp.sum(lacc_ref[...], axis=0, keepdims=True)
            loss_ref[...] = jnp.sum(rows, axis=1, keepdims=True) * (0.5 / D)

    row_spec = _spec((tm, D), lambda i: (i, 0))
    return pl.pallas_call(
        body, name="final_loss", grid=(nt,),
        in_specs=[row_spec, _spec((None, 1, D), lambda i: (R_FIN, 0, 0)), row_spec],
        out_specs=[row_spec, row_spec, _spec((1, D), lambda i: (0, 0)), _spec((1, 1), lambda i: (0, 0))],
        out_shape=[jax.ShapeDtypeStruct((t, D), F32), jax.ShapeDtypeStruct((t, D), BF16),
                   jax.ShapeDtypeStruct((1, D), F32), jax.ShapeDtypeStruct((1, 1), F32)],
        scratch_shapes=[pltpu.VMEM((8, D), F32), pltpu.VMEM((8, D), F32)],
        compiler_params=_params(("arbitrary",)),
    )(h, gain_slab, target)


def _ffn_up(name, hn, wgu, carry=None):
    t = hn.shape[0]
    tm = min(MM_TILE, t)
    grid = (NSH, t // tm)
    nc = carry.n if carry is not None else 0

    def body(*refs):
        x_ref, wg_ref, wu_ref = refs[:3]
        s_up_ref, s_gate_ref, a_ref = refs[3 + nc:6 + nc]
        riders = (refs[3:3 + nc], refs[6 + nc:6 + 2 * nc], refs[-2], refs[-1]) if nc else None
        if nc:
            carry.ride(grid, riders, "start")
        x = x_ref[...]
        g = _dg(x, wg_ref[...], NT)
        u = _dg(x, wu_ref[...], NT)
        sg = _sigmoid(g)
        silu = g * sg
        s_up_ref[...] = (MACARON * silu).astype(BF16)
        s_gate_ref[...] = (MACARON * u * (sg * (1.0 + g * (1.0 - sg)))).astype(BF16)
        a_ref[...] = (silu * u).astype(BF16)
        if nc:
            carry.ride(grid, riders, "finish")

    act = _spec((None, tm, FS), lambda s, i: (s, i, 0))
    shape = jax.ShapeDtypeStruct((NSH, t, FS), BF16)
    outs = pl.pallas_call(
        body, name=name, grid=grid,
        in_specs=[_spec((tm, D), lambda s, i: (i, 0)),
                  _spec((None, None, FS, D), lambda s, i: (s, 0, 0, 0)),
                  _spec((None, None, FS, D), lambda s, i: (s, 1, 0, 0))] + (carry.in_specs if nc else []),
        out_specs=[act, act, act] + (carry.out_specs if nc else []),
        out_shape=[shape, shape, shape] + (carry.out_shape if nc else []),
        scratch_shapes=carry.scratch if nc else [],
        compiler_params=_params(("arbitrary", "arbitrary") if nc else ("parallel", "parallel")),
    )(hn, wgu, wgu, *(carry.groups if nc else []))
    return (outs[0], outs[1], outs[2], carry.place(outs[3:]) if nc else [])


def _ffn_down(name, a, wd, h, carry=None, norm=None):
    t = h.shape[0]
    tm = min(MM_TILE, t)
    return _mm(name, [(a, _spec((None, tm, FS), lambda i, k: (k, i, 0)),
                       wd, _spec((None, FS, D), lambda i, k: (k, 0, 0)))],
               grid=(t // tm, NSH), o_shape=(t, D), o_dtype=F32, o_spec=_spec((tm, D), lambda i, k: (i, 0)),
               dims=NN, kaxis=1, nk=NSH, acc_shape=(tm, D), res=(h, _spec((tm, D), lambda i, k: (i, 0))),
               scale=MACARON, carry=carry, norm=norm)


def _ffn_bwd_up(name, dhb, wd, s_up, s_gate, rider=None):
    t = dhb.shape[0]
    tm = min(MM_TILE, t)
    grid = (NSH, t // tm)
    nr = rider.n if rider is not None else 0

    def body(*refs):
        dh_ref, wd_ref, s_up_ref, s_gate_ref = refs[:4]
        dg_ref, du_ref = refs[4 + nr:6 + nr]
        riders = (refs[4:4 + nr], refs[6 + nr:6 + 2 * nr], refs[-2], refs[-1]) if nr else None
        if nr:
            rider.ride(grid, riders, "start")
        da = _dg(dh_ref[...], wd_ref[...], NT).astype(BF16)
        du_ref[...] = da * s_up_ref[...]
        dg_ref[...] = da * s_gate_ref[...]
        if nr:
            rider.ride(grid, riders, "finish")

    act = _spec((None, tm, FS), lambda s, i: (s, i, 0))
    shape = jax.ShapeDtypeStruct((NSH, t, FS), BF16)
    outs = pl.pallas_call(
        body, name=name, grid=grid,
        in_specs=[_spec((tm, D), lambda s, i: (i, 0)), _spec((None, FS, D), lambda s, i: (s, 0, 0)), act, act]
        + (rider.in_specs if nr else []),
        out_specs=[act, act] + (rider.out_specs if nr else []),
        out_shape=[shape, shape] + (rider.out_shape if nr else []),
        scratch_shapes=rider.scratch if nr else [],
        compiler_params=_params(("arbitrary", "arbitrary") if nr else ("parallel", "parallel")),
    )(dhb, wd, s_up, s_gate, *(rider.groups if nr else []))
    return outs[0], outs[1], (rider.place(outs[2:]) if nr else [])


def _wgrad(name, a, a_spec, b, b_spec, out_rows, out_cols, t, tt, group, slot, scale=None, carry=None):
    first = isinstance(group, int)
    rows = group if first else group.shape[1]
    return _mm(name, [(a, a_spec, b, b_spec)], grid=(NSH, t // tt),
               o_shape=(NSH, rows, out_cols), o_dtype=BF16,
               o_spec=_spec((None, out_rows, out_cols), lambda s, k: (s, slot, 0)),
               dims=TN, kaxis=1, nk=t // tt, acc_shape=(out_rows, out_cols), scale=scale,
               into=None if first else group, carry=carry)


def _ffn_backward(tag, dh, dhb, h_in, hn, s_up, s_gate, a, wgu, wd, gate_idx, up_idx, down_idx, small, norm_row,
                  grad_a, grad_b, rider_up=None, rider_dhn=None):
    t = dh.shape[0]
    tm = min(MM_TILE, t)
    tt = min(WGRAD_TILE, t)
    tok = _spec((tt, D), lambda s, k: (k, 0))
    hid = _spec((None, tt, FS), lambda s, k: (s, k, 0))
    grad_b = _wgrad(tag + "_dwd", a, hid, dhb, tok, FS, D, t, tt, grad_b, down_idx, scale=MACARON)
    rider = rider_up(grad_a, grad_b) if rider_up is not None else None
    dg, du, landed_up = _ffn_bwd_up(tag + "_bwd_up", dhb, wd, s_up, s_gate, rider)
    grad_a = _wgrad(tag + "_dwg", dg, hid, hn, tok, FS, D, t, tt, grad_a, gate_idx)
    grad_a = _wgrad(tag + "_dwu", du, hid, hn, tok, FS, D, t, tt, grad_a, up_idx)
    rider = rider_dhn(grad_a, grad_b, landed_up) if rider_dhn is not None else None
    act = _spec((None, tm, FS), lambda i, k: (k, i, 0))
    dh_in, dhb_in, d_gain, landed_dhn = _dhn_norm(
        tag + "_dhn", [(dg, act, wgu, _spec((None, None, FS, D), lambda i, k: (k, 0, 0, 0))),
                       (du, act, wgu, _spec((None, None, FS, D), lambda i, k: (k, 1, 0, 0)))],
        NN, h_in, small, norm_row, dh, rider)
    return dh_in, dhb_in, grad_a, grad_b, d_gain, landed_up, landed_dhn


def _conv_fwd(proj_a, conv_w):
    t = proj_a.shape[0]
    tm = min(ROW_TILE, t)
    hb = tm // CONV_HALO

    def body(ab_ref, ac_ref, ax_ref, acp_ref, axp_ref, w_ref, y_ref):
        i = pl.program_id(1)
        u = ac_ref[...] * ax_ref[...]
        up = jnp.where(i > 0, acp_ref[...] * axp_ref[...], 0.0)
        ext = jnp.concatenate([up, u], axis=0)
        u1 = pltpu.roll(ext, 1, 0)[CONV_HALO:]
        u2 = pltpu.roll(ext, 2, 0)[CONV_HALO:]
        w = w_ref[...]
        conv = w[0:1] * u2 + w[1:2] * u1 + w[2:3] * u
        y_ref[...] = (ab_ref[...] * conv).astype(BF16)

    def cur(off):
        return _spec((tm, LANE), lambda j, i: (i, off + j))

    def prev(off):
        return _spec((CONV_HALO, LANE), lambda j, i: (jnp.maximum(i * hb - 1, 0), off + j))

    return pl.pallas_call(
        body, name="conv_fwd", grid=(4, t // tm),
        in_specs=[cur(0), cur(4), cur(8), prev(4), prev(8),
                  _spec((None, None, 8, LANE), lambda j, i: (j, 0, 0, 0))],
        out_specs=_spec((tm, LANE), lambda j, i: (i, j)),
        out_shape=jax.ShapeDtypeStruct((t, 512), BF16),
        compiler_params=_params(("parallel", "parallel")),
    )(proj_a, proj_a, proj_a, proj_a, proj_a, conv_w)


def _conv_bwd(proj_a, conv_w, dy, rider=None):
    t = proj_a.shape[0]
    tm = min(ROW_TILE, t)
    hb = tm // CONV_HALO
    nt = t // tm
    grid = (4, nt)
    nr = rider.n if rider is not None else 0

    def body(*refs):
        ab_ref, ac_ref, ax_ref, dy_ref, acp_ref, axp_ref, abn_ref, dyn_ref, w_ref = refs[:9]
        dab_ref, dac_ref, dax_ref, dw_ref = refs[9 + nr:13 + nr]
        acc_ref = refs[13 + 2 * nr]
        riders = (refs[9:9 + nr], refs[13 + nr:13 + 2 * nr], refs[-2], refs[-1]) if nr else None
        if nr:
            rider.ride(grid, riders, "start")
        i = pl.program_id(1)
        ab, ac, ax = ab_ref[...], ac_ref[...], ax_ref[...]
        u = ac * ax
        up = jnp.where(i > 0, acp_ref[...] * axp_ref[...], 0.0)
        ext = jnp.concatenate([up, u], axis=0)
        u1 = pltpu.roll(ext, 1, 0)[CONV_HALO:]
        u2 = pltpu.roll(ext, 2, 0)[CONV_HALO:]
        w = w_ref[...]
        conv = w[0:1] * u2 + w[1:2] * u1 + w[2:3] * u
        dy_v = dy_ref[...]
        dab_ref[...] = (dy_v * conv).astype(BF16)
        dc = dy_v * ab
        dcn = jnp.where(i < nt - 1, dyn_ref[...] * abn_ref[...], 0.0)
        extn = jnp.concatenate([dc, dcn], axis=0)
        n = tm + CONV_HALO
        dc1 = pltpu.roll(extn, n - 1, 0)[:tm]
        dc2 = pltpu.roll(extn, n - 2, 0)[:tm]
        du = w[2:3] * dc + w[1:2] * dc1 + w[0:1] * dc2
        dac_ref[...] = (du * ax).astype(BF16)
        dax_ref[...] = (du * ac).astype(BF16)
        rid = lax.broadcasted_iota(jnp.int32, (8, LANE), 0)
        part = jnp.where(rid == 0, jnp.sum(dc * u2, axis=0, keepdims=True),
                         jnp.where(rid == 1, jnp.sum(dc * u1, axis=0, keepdims=True),
                                   jnp.where(rid == 2, jnp.sum(dc * u, axis=0, keepdims=True), 0.0)))

        @pl.when(i == 0)
        def _():
            acc_ref[...] = part

        @pl.when(i > 0)
        def _():
            acc_ref[...] += part

        @pl.when(i == nt - 1)
        def _():
            dw_ref[...] = acc_ref[...]

        if nr:
            rider.ride(grid, riders, "finish")

    def cur(off):
        return _spec((tm, LANE), lambda j, i: (i, off + j))

    def prev(off):
        return _spec((CONV_HALO, LANE), lambda j, i: (jnp.maximum(i * hb - 1, 0), off + j))

    def nxt(off):
        return _spec((CONV_HALO, LANE), lambda j, i: (jnp.minimum((i + 1) * hb, nt * hb - 1), off + j))

    outs = pl.pallas_call(
        body, name="conv_bwd", grid=grid,
        in_specs=[cur(0), cur(4), cur(8), cur(0), prev(4), prev(8), nxt(0), nxt(0),
                  _spec((None, None, 8, LANE), lambda j, i: (j, 0, 0, 0))] + (rider.in_specs if nr else []),
        out_specs=[_spec((tm, LANE), lambda j, i: (i, j)), _spec((tm, LANE), lambda j, i: (i, j)),
                   _spec((tm, LANE), lambda j, i: (i, j)), _spec((None, 8, LANE), lambda j, i: (j, 0, 0))]
        + (rider.out_specs if nr else []),
        out_shape=[jax.ShapeDtypeStruct((t, 512), BF16), jax.ShapeDtypeStruct((t, 512), BF16),
                   jax.ShapeDtypeStruct((t, 512), BF16), jax.ShapeDtypeStruct((4, 8, LANE), F32)]
        + (rider.out_shape if nr else []),
        scratch_shapes=[pltpu.VMEM((8, LANE), F32)] + (rider.scratch if nr else []),
        compiler_params=_params(("arbitrary", "arbitrary") if nr else ("parallel", "arbitrary")),
    )(proj_a, proj_a, proj_a, dy, proj_a, proj_a, proj_a, dy, conv_w, *(rider.groups if nr else []))
    return outs[0], outs[1], outs[2], outs[3], (rider.place(outs[4:]) if nr else [])


def _log_sigmoid(z):
    return jnp.minimum(z, 0.0) - jnp.log(1.0 + jnp.exp(-jnp.abs(z)))


def _sb_masks():
    row = lax.broadcasted_iota(jnp.int32, (SBQ, SBQ), 0)
    col = lax.broadcasted_iota(jnp.int32, (SBQ, SBQ), 1)
    return row, col


def _ones_where(mask):
    return jnp.where(mask, 1.0, 0.0).astype(BF16)


def _head_mask(head):
    lane = lax.broadcasted_iota(jnp.int32, (1, LANE), 1)
    return lane >= 64 if head else lane < 64


def _sb_fwd(proj_b, carry=None):
    t = proj_b.shape[0]
    nq = t // SBQ
    scale = 1.0 / math.sqrt(64.0)

    npair = SB_PAIRS_FWD
    wide = npair * LANE
    ngrp = 4 // npair
    chains = [(p, head) for p in range(npair) for head in range(2)]

    grid = (ngrp, nq)
    nc = carry.n if carry is not None else 0

    def body(*refs):
        q_ref, k_ref, v_ref = refs[:3]
        y_ref, l_ref, n_ref = refs[3 + nc:6 + nc]
        riders = (refs[3:3 + nc], refs[6 + nc:6 + 2 * nc], refs[-2], refs[-1]) if nc else None
        if nc:
            carry.ride(grid, riders, "start")
        grp = pl.program_id(0)
        qi = pl.program_id(1)
        row, col = _sb_masks()
        m_suffix = _ones_where(row > col)
        rows = len(chains) * SBQ
        strict = (lax.broadcasted_iota(jnp.int32, (rows, SBQ), 1)
                  < (lax.broadcasted_iota(jnp.int32, (rows, SBQ), 0) & (SBQ - 1)))
        q_pair = []
        for p in range(npair):
            q_all = q_ref[:, p * LANE:(p + 1) * LANE]
            q_pair.append(jnp.concatenate([jnp.where(_head_mask(head), q_all, jnp.zeros_like(q_all)) for head in range(2)],
                                          axis=0))

        def block(kb, state, diag):
            run, acc = state
            start = pl.multiple_of(kb * SBQ, SBQ)
            z = jnp.concatenate([_dg(q_pair[p], k_ref[pl.ds(start, SBQ), p * LANE:(p + 1) * LANE], NT)
                                 for p in range(npair)], axis=0) * scale
            lb = _log_sigmoid(z)
            lk = lb - z
            if diag:
                lk = jnp.where(strict, lk, 0.0)
            hi, lo = _split(lk)
            sums = _dg(jnp.concatenate([hi, lo], axis=0), m_suffix, NN)
            w = jnp.exp(lb + (run + sums[:rows] + sums[rows:]))
            if diag:
                w = jnp.where(strict, w, 0.0)
            wb = w.astype(BF16)
            acc = acc + jnp.concatenate(
                [_dg(wb[2 * p * SBQ:2 * (p + 1) * SBQ], v_ref[pl.ds(start, SBQ), p * LANE:(p + 1) * LANE], NN)
                 for p in range(npair)], axis=0)
            run = run + jnp.sum(hi.astype(F32) + lo.astype(F32), axis=1, keepdims=True)
            return run, acc

        state = block(qi, (jnp.zeros((rows, 1), F32), jnp.zeros((rows, LANE), F32)), True)

        def live(c):
            return jnp.logical_and(c[0] < qi, jnp.max(c[1][0]) > SB_DEAD)

        def step(c):
            return c[0] + 1, block(qi - 1 - c[0], c[1], False)

        count, (run, acc) = lax.while_loop(live, step, (jnp.int32(0), state))
        n_ref[grp * nq + qi] = count.astype(F32)
        hm = _head_mask(0)
        for p in range(npair):
            lo_rows, hi_rows = slice(2 * p * SBQ, (2 * p + 1) * SBQ), slice((2 * p + 1) * SBQ, (2 * p + 2) * SBQ)
            y_ref[:, p * LANE:(p + 1) * LANE] = jnp.where(hm, acc[lo_rows], acc[hi_rows]).astype(BF16)
            l_ref[p] = jnp.where(hm, run[lo_rows], run[hi_rows])
        if nc:
            carry.ride(grid, riders, "finish")

    outs = pl.pallas_call(
        body, name="sb_fwd", grid=grid,
        in_specs=[_spec((SBQ, wide), lambda g, i: (i, g)),
                  _spec((t, wide), lambda g, i: (0, ngrp + g)),
                  _spec((t, wide), lambda g, i: (0, 2 * ngrp + g))] + (carry.in_specs if nc else []),
        out_specs=[_spec((SBQ, wide), lambda g, i: (i, g)), _spec((npair, SBQ, LANE), lambda g, i: (g, i, 0)),
                   pl.BlockSpec(memory_space=pltpu.SMEM)] + (carry.out_specs if nc else []),
        out_shape=[jax.ShapeDtypeStruct((t, 512), BF16), jax.ShapeDtypeStruct((4, t, LANE), F32),
                   jax.ShapeDtypeStruct((ngrp * nq,), F32)] + (carry.out_shape if nc else []),
        scratch_shapes=carry.scratch if nc else [],
        compiler_params=_params(("arbitrary", "arbitrary")),
    )(proj_b, proj_b, proj_b, *(carry.groups if nc else []))
    return outs[0], outs[1], outs[2], (carry.place(outs[3:]) if nc else [])


def _sb_bwd(proj_b, dy, ltot, nblk, rider=None):
    t = proj_b.shape[0]
    nq = t // SBQ
    scale = 1.0 / math.sqrt(64.0)

    npair = SB_PAIRS_BWD
    wide = npair * LANE
    ngrp = 4 // npair
    chains = [(p, head) for p in range(npair) for head in range(2)]
    grid = (ngrp, nq)
    nr = rider.n if rider is not None else 0
    per_count = SB_PAIRS_FWD // SB_PAIRS_BWD

    def body(*refs):
        q_ref, k_ref, v_ref, dy_ref, l_ref, n_ref = refs[:6]
        dq_ref, dk_ref, dv_ref = refs[6 + nr:9 + nr]
        dk_acc, dv_acc = refs[9 + 2 * nr:11 + 2 * nr]
        riders = (refs[6:6 + nr], refs[9 + nr:9 + 2 * nr], refs[-2], refs[-1]) if nr else None
        if nr:
            rider.ride(grid, riders, "start")
        grp = pl.program_id(0)
        qi = pl.program_id(1)

        @pl.when(qi == 0)
        def _():
            dk_acc[...] = jnp.zeros_like(dk_acc)
            dv_acc[...] = jnp.zeros_like(dv_acc)

        row, col = _sb_masks()
        m_prefix = _ones_where(row <= col)
        m_before = _ones_where(row < col)
        rows = len(chains) * SBQ
        strict = (lax.broadcasted_iota(jnp.int32, (rows, SBQ), 1)
                  < (lax.broadcasted_iota(jnp.int32, (rows, SBQ), 0) & (SBQ - 1)))
        q_pair, do_pair, ltot = [], [], []
        for p in range(npair):
            pl_ = slice(p * LANE, (p + 1) * LANE)
            q_all = q_ref[:, pl_]
            do_all = dy_ref[:, pl_].astype(BF16)
            q_pair.append(jnp.concatenate([jnp.where(_head_mask(h), q_all, jnp.zeros_like(q_all)) for h in range(2)], axis=0))
            do_pair.append(jnp.concatenate([jnp.where(_head_mask(h), do_all, jnp.zeros_like(do_all)) for h in range(2)], axis=0))
            ltot += [l_ref[p][:, h * 64:h * 64 + 1] for h in range(2)]
        ltot = jnp.concatenate(ltot, axis=0)

        def pair_rows(a, p):
            return a[2 * p * SBQ:2 * (p + 1) * SBQ]

        def block(kb, state, diag):
            seen, dseen, dq = state
            start = pl.multiple_of(kb * SBQ, SBQ)
            kk = [k_ref[pl.ds(start, SBQ), p * LANE:(p + 1) * LANE] for p in range(npair)]
            vv = [v_ref[pl.ds(start, SBQ), p * LANE:(p + 1) * LANE] for p in range(npair)]
            z = jnp.concatenate([_dg(q_pair[p], kk[p], NT) for p in range(npair)], axis=0) * scale
            lb = _log_sigmoid(z)
            lk = lb - z
            if diag:
                lk = jnp.where(strict, lk, 0.0)
            hi, lo = _split(lk)
            sums = _dg(jnp.concatenate([hi, lo], axis=0), m_prefix, NN)
            w = jnp.exp(lb + ((ltot - seen) - (sums[:rows] + sums[rows:])))
            if diag:
                w = jnp.where(strict, w, 0.0)
            wb = w.astype(BF16)
            da = w * jnp.concatenate([_dg(do_pair[p], vv[p], NT) for p in range(npair)], axis=0)
            dah, dal = _split(da)
            dsums = _dg(jnp.concatenate([dah, dal], axis=0), m_before, NN)
            sig = jnp.exp(lb)
            dz = (da * (1.0 - sig) - (dseen + dsums[:rows] + dsums[rows:]) * sig) * scale
            if diag:
                dz = jnp.where(strict, dz, 0.0)
            dzb = dz.astype(BF16)
            for p in range(npair):
                pl_ = slice(p * LANE, (p + 1) * LANE)
                dv_acc[pl.ds(start, SBQ), pl_] += _dg(pair_rows(wb, p), do_pair[p], TN)
                dk_acc[pl.ds(start, SBQ), pl_] += _dg(pair_rows(dzb, p), q_pair[p], TN)
            dq = dq + jnp.concatenate([_dg(pair_rows(dzb, p), kk[p], NN) for p in range(npair)], axis=0)
            seen = seen + jnp.sum(hi.astype(F32) + lo.astype(F32), axis=1, keepdims=True)
            dseen = dseen + jnp.sum(da, axis=1, keepdims=True)
            return seen, dseen, dq

        zero = (jnp.zeros((rows, 1), F32), jnp.zeros((rows, 1), F32), jnp.zeros((rows, LANE), F32))
        first = qi - n_ref[(grp // per_count) * nq + qi].astype(jnp.int32)
        state = lax.fori_loop(first, qi, lambda kb, c: block(kb, c, False), zero)
        _, _, dq = block(qi, state, True)
        for p in range(npair):
            dq_ref[:, p * LANE:(p + 1) * LANE] = jnp.where(
                _head_mask(0), dq[2 * p * SBQ:(2 * p + 1) * SBQ], dq[(2 * p + 1) * SBQ:(2 * p + 2) * SBQ]).astype(BF16)

        @pl.when(qi == nq - 1)
        def _():
            dk_ref[...] = dk_acc[...].astype(BF16)
            dv_ref[...] = dv_acc[...].astype(BF16)

        if nr:
            rider.ride(grid, riders, "finish")

    full = jax.ShapeDtypeStruct((t, 512), BF16)
    outs = pl.pallas_call(
        body, name="sb_bwd", grid=grid,
        in_specs=[_spec((SBQ, wide), lambda g, i: (i, g)),
                  _spec((t, wide), lambda g, i: (0, ngrp + g)),
                  _spec((t, wide), lambda g, i: (0, 2 * ngrp + g)),
                  _spec((SBQ, wide), lambda g, i: (i, ngrp + g)),
                  _spec((npair, SBQ, LANE), lambda g, i: (g, i, 0)),
                  pl.BlockSpec(memory_space=pltpu.SMEM)] + (rider.in_specs if nr else []),
        out_specs=[_spec((SBQ, wide), lambda g, i: (i, g)),
                   _spec((t, wide), lambda g, i: (0, g)), _spec((t, wide), lambda g, i: (0, g))]
        + (rider.out_specs if nr else []),
        out_shape=[full, full, full] + (rider.out_shape if nr else []),
        scratch_shapes=[pltpu.VMEM((t, wide), F32), pltpu.VMEM((t, wide), F32)] + (rider.scratch if nr else []),
        compiler_params=pltpu.CompilerParams(dimension_semantics=("arbitrary", "arbitrary"),
                                             vmem_limit_bytes=VMEM_LIMIT_BIG),
    )(proj_b, proj_b, proj_b, dy, ltot, nblk, *(rider.groups if nr else []))
    return outs[0], outs[1], outs[2], (rider.place(outs[3:]) if nr else [])


def _hgrn_gates(qr, fr, c0, c1):
    mx = jnp.maximum(c0, c1)
    e0, e1 = jnp.exp(c0 - mx), jnp.exp(c1 - mx)
    lb = e1 / (e0 + e1)
    sx = _sigmoid(fr)
    f = lb + (1.0 - lb) * sx
    k = (1.0 - lb) * (1.0 - sx)
    sq = _sigmoid(qr)
    return lb, sx, f, k, sq, qr * sq


def _chunk_sums(mask, x):
    n = x.shape[1]
    hi, lo = _split(x)
    both = _dg(_ones_where(mask), jnp.concatenate([hi, lo], axis=1), NN)
    return both[:, :n] + both[:, n:]


def _chunk_masks():
    row = lax.broadcasted_iota(jnp.int32, (CHUNK, CHUNK), 0)
    col = lax.broadcasted_iota(jnp.int32, (CHUNK, CHUNK), 1)
    return col <= row, col >= row


def _hgrn_fwd(proj_c, small, carry=None):
    t = proj_c.shape[1]
    nc = t // CHUNK
    nh = D // HD

    hps = HGRN_HPS
    wide = hps * HD
    grid = (nh // hps, nc)
    nr = carry.n if carry is not None else 0

    def body(*refs):
        p_ref, c0_ref, c1_ref, gam_ref = refs[:4]
        o_ref, y_ref, sst_ref = refs[4 + nr:7 + nr]
        st_ref = refs[7 + 2 * nr]
        riders = (refs[4:4 + nr], refs[7 + nr:7 + 2 * nr], refs[-2], refs[-1]) if nr else None
        if nr:
            carry.ride(grid, riders, "start")
        c = pl.program_id(1)

        @pl.when(c == 0)
        def _():
            st_ref[...] = jnp.zeros_like(st_ref)

        _, _, f_all, k_all, _, q_all = _hgrn_gates(p_ref[0], p_ref[1], c0_ref[...], c1_ref[...])
        tril, _ = _chunk_masks()
        b_all = _chunk_sums(tril, jnp.log(f_all))
        for j in range(hps):
            ln = slice(j * HD, (j + 1) * HD)
            st0 = st_ref[j]
            sst_ref[j] = st0
            v, g = p_ref[2, :, ln], p_ref[3, :, ln]
            q, k, b = q_all[:, ln], k_all[:, ln], b_all[:, ln]
            bm = b[CHUNK // 2 - 1:CHUNK // 2]
            bl = b[CHUNK - 1:CHUNK]
            qd = q * jnp.exp(b)
            qt = q * jnp.exp(b - bm)
            kt = k * jnp.exp(bm - b)
            kl = k * jnp.exp(bl - b)
            vb = v.astype(BF16)
            att = jnp.where(tril, _dot3(qt, kt, NT), 0.0)
            o = _dg(qd.astype(BF16), st0.astype(BF16), NT) + _dg(att.astype(BF16), vb, NN)
            st_ref[j] = st0 * jnp.exp(bl) + _dg(vb, kl.astype(BF16), TN)
            o_ref[:, ln] = o
            r = lax.rsqrt(jnp.mean(o * o, axis=-1, keepdims=True) + RMS_EPS)
            y_ref[:, ln] = (o * r * gam_ref[...] * (g * _sigmoid(g))).astype(BF16)
        if nr:
            carry.ride(grid, riders, "finish")

    outs = pl.pallas_call(
        body, name="hgrn_fwd", grid=grid,
        in_specs=[_spec((4, CHUNK, wide), lambda h, c: (0, c, h)),
                  _spec((None, 1, wide), lambda h, c: (R_CLB, 0, h)),
                  _spec((None, 1, wide), lambda h, c: (R_CLB + 1, 0, h)),
                  _spec((None, 1, HD), lambda h, c: (R_GAM, 0, 0))] + (carry.in_specs if nr else []),
        out_specs=[_spec((CHUNK, wide), lambda h, c: (c, h)), _spec((CHUNK, wide), lambda h, c: (c, h)),
                   _spec((None, hps, HD, HD), lambda h, c: (c, h, 0, 0))] + (carry.out_specs if nr else []),
        out_shape=[jax.ShapeDtypeStruct((t, D), F32), jax.ShapeDtypeStruct((t, D), BF16),
                   jax.ShapeDtypeStruct((nc, nh, HD, HD), F32)] + (carry.out_shape if nr else []),
        scratch_shapes=[pltpu.VMEM((hps, HD, HD), F32)] + (carry.scratch if nr else []),
        compiler_params=_params(("arbitrary", "arbitrary")),
    )(proj_c, small, small, small, *(carry.groups if nr else []))
    return outs[0], outs[1], outs[2], (carry.place(outs[3:]) if nr else [])


def _hgrn_bwd(proj_c, small, o, sst, dyc, rider=None):
    t = proj_c.shape[1]
    nc = t // CHUNK
    nh = D // HD

    hps = HGRN_HPS
    wide = hps * HD
    ng = nh // hps
    grid = (ng, nc)
    nr = rider.n if rider is not None else 0

    def body(*refs):
        p_ref, c0_ref, c1_ref, gam_ref, o_ref, sst_ref, dy_ref = refs[:7]
        dp_ref, dclb_ref, dgam_ref = refs[7 + nr:10 + nr]
        dst_ref, dlb_acc, dgam_acc = refs[10 + 2 * nr:13 + 2 * nr]
        riders = (refs[7:7 + nr], refs[10 + nr:10 + 2 * nr], refs[-2], refs[-1]) if nr else None
        if nr:
            rider.ride(grid, riders, "start")
        group = pl.program_id(0)
        step = pl.program_id(1)

        @pl.when(step == 0)
        def _():
            dst_ref[...] = jnp.zeros_like(dst_ref)
            dlb_acc[...] = jnp.zeros_like(dlb_acc)

        @pl.when((step == 0) & (group == 0))
        def _():
            dgam_acc[...] = jnp.zeros_like(dgam_acc)

        gam = gam_ref[...]
        qr_all = p_ref[0]
        lb_all, sx_all, f_all, k_all, sq_all, q_all = _hgrn_gates(qr_all, p_ref[1], c0_ref[...], c1_ref[...])
        tril, triu = _chunk_masks()
        b_all = _chunk_sums(tril, jnp.log(f_all))
        dq_parts, dk_parts, db_parts, dgam_parts = [], [], [], []
        for j in range(hps):
            ln = slice(j * HD, (j + 1) * HD)
            st0 = sst_ref[j]
            dst1 = dst_ref[j]
            v, g = p_ref[2, :, ln], p_ref[3, :, ln]
            q, k, b = q_all[:, ln], k_all[:, ln], b_all[:, ln]
            bm = b[CHUNK // 2 - 1:CHUNK // 2]
            bl = b[CHUNK - 1:CHUNK]
            eb = jnp.exp(b)
            e_qt = jnp.exp(b - bm)
            e_kt = jnp.exp(bm - b)
            e_kl = jnp.exp(bl - b)
            e_bl = jnp.exp(bl)
            qd, qt, kt, kl = q * eb, q * e_qt, k * e_kt, k * e_kl
            ov = o_ref[:, ln]
            r = lax.rsqrt(jnp.mean(ov * ov, axis=-1, keepdims=True) + RMS_EPS)
            oh = ov * r
            sg = _sigmoid(g)
            dy = dy_ref[:, ln]
            dp_ref[3, :, ln] = (dy * oh * gam * (sg * (1.0 + g * (1.0 - sg)))).astype(BF16)
            dyv = dy * (g * sg)
            gdy = dyv * gam
            do = (gdy - oh * jnp.mean(gdy * oh, axis=-1, keepdims=True)) * r
            dob, vb = do.astype(BF16), v.astype(BF16)
            st0b, dst1b = st0.astype(BF16), dst1.astype(BF16)
            st1 = st0 * e_bl + _dg(vb, kl.astype(BF16), TN)
            att = jnp.where(tril, _dot3(qt, kt, NT), 0.0)
            datt = jnp.where(tril, _dg(dob, vb, NT), 0.0)
            dv = _dg(att.astype(BF16), dob, TN) + _dg(kl.astype(BF16), dst1b, NT)
            dq = _dot3(datt, kt, NN) * e_qt + _dg(dob, st0b, NN) * eb
            dk = _dot3(datt, qt, TN) * e_kt + _dg(vb, dst1b, NN) * e_kl
            db = q * dq - k * dk
            last = lax.broadcasted_iota(jnp.int32, (CHUNK, 1), 0) == CHUNK - 1
            db = db + jnp.where(last, jnp.sum(dst1 * st1, axis=0, keepdims=True), 0.0)
            dst_ref[j] = dst1 * e_bl + _dg(dob, qd.astype(BF16), TN)
            dp_ref[2, :, ln] = dv.astype(BF16)
            dq_parts.append(dq)
            dk_parts.append(dk)
            db_parts.append(db)
            dgam_parts.append(jnp.sum(dyv * oh, axis=0, keepdims=True))

        dq_all, dk_all = jnp.concatenate(dq_parts, axis=1), jnp.concatenate(dk_parts, axis=1)
        dlf = _chunk_sums(triu, jnp.concatenate(db_parts, axis=1))
        dp_ref[0] = (dq_all * (sq_all * (1.0 + qr_all * (1.0 - sq_all)))).astype(BF16)
        tmp = dlf / f_all - dk_all
        dp_ref[1] = (tmp * (1.0 - lb_all) * sx_all * (1.0 - sx_all)).astype(BF16)
        dlb_acc[...] += jnp.sum((1.0 - sx_all) * tmp, axis=0, keepdims=True)
        dgam_acc[...] += functools.reduce(lambda a, b: a + b, dgam_parts)

        @pl.when(step == nc - 1)
        def _():
            d1 = dlb_acc[...] * lb_all * (1.0 - lb_all)
            dclb_ref[...] = jnp.where(lax.broadcasted_iota(jnp.int32, (2, wide), 0) == 0, -d1, d1)

        @pl.when((step == nc - 1) & (group == ng - 1))
        def _():
            dgam_ref[...] = dgam_acc[...]

        if nr:
            rider.ride(grid, riders, "finish")

    rev = lambda h, s: (nc - 1 - s, h)
    outs = pl.pallas_call(
        body, name="hgrn_bwd", grid=grid,
        in_specs=[_spec((4, CHUNK, wide), lambda h, s: (0, nc - 1 - s, h)),
                  _spec((None, 1, wide), lambda h, s: (R_CLB, 0, h)),
                  _spec((None, 1, wide), lambda h, s: (R_CLB + 1, 0, h)),
                  _spec((None, 1, HD), lambda h, s: (R_GAM, 0, 0)),
                  _spec((CHUNK, wide), rev),
                  _spec((None, hps, HD, HD), lambda h, s: (nc - 1 - s, h, 0, 0)),
                  _spec((CHUNK, wide), rev)] + (rider.in_specs if nr else []),
        out_specs=[_spec((4, CHUNK, wide), lambda h, s: (0, nc - 1 - s, h)),
                   _spec((2, wide), lambda h, s: (0, h)),
                   _spec((1, HD), lambda h, s: (0, 0))] + (rider.out_specs if nr else []),
        out_shape=[jax.ShapeDtypeStruct((4, t, D), BF16), jax.ShapeDtypeStruct((2, D), F32),
                   jax.ShapeDtypeStruct((1, HD), F32)] + (rider.out_shape if nr else []),
        scratch_shapes=[pltpu.VMEM((hps, HD, HD), F32), pltpu.VMEM((1, wide), F32), pltpu.VMEM((1, HD), F32)]
        + (rider.scratch if nr else []),
        compiler_params=_params(("arbitrary", "arbitrary")),
    )(proj_c, small, small, small, o, sst, dyc, *(rider.groups if nr else []))
    return outs[0], outs[1], outs[2], (rider.place(outs[3:]) if nr else [])


def _adamw(name, w, grads, m, v):
    nl = len(grads)
    rows, cols = grads[0].shape
    br = rows
    for cand in (512, 352, 256):
        if rows % cand == 0:
            br = cand
            break
    nb = rows // br
    c1 = 1.0 - ADAM_B1 ** ADAM_STEP
    c2 = 1.0 - ADAM_B2 ** ADAM_STEP

    def body(w_ref, m_ref, v_ref, *refs):
        g_refs, (d_ref, mo_ref, vo_ref, go_ref) = refs[:nl], refs[nl:]
        layer = pl.program_id(0)
        gv = g_refs[0][...]
        for k in range(1, nl):
            gv = jnp.where(layer == k, g_refs[k][...], gv)
        mn = ADAM_B1 * m_ref[...] + (1.0 - ADAM_B1) * gv
        vn = ADAM_B2 * v_ref[...] + (1.0 - ADAM_B2) * (gv * gv)
        mo_ref[...] = mn
        vo_ref[...] = vn
        go_ref[...] = gv
        d_ref[...] = -ADAM_LR * ((mn / c1) / (jnp.sqrt(vn / c2) + ADAM_EPS) + ADAM_WD * w_ref[...])

    blk = _spec((br, cols), lambda l, i: (l * nb + i, 0))
    g_specs = [_spec((br, cols), lambda l, i, k=k: (jnp.where(l == k, i, 0), 0)) for k in range(nl)]
    shape = jax.ShapeDtypeStruct((nl * rows, cols), F32)
    return pl.pallas_call(
        body, name=name, grid=(nl, nb), in_specs=[blk] * 3 + g_specs, out_specs=[blk] * 4, out_shape=[shape] * 4,
        compiler_params=_params(("arbitrary", "arbitrary")),
    )(w, m, v, *grads)


def _place():
    x, y, c = lax.axis_index("x"), lax.axis_index("y"), lax.axis_index("c")
    chips = [(1 - x, y), (x, 1 - y), (1 - x, 1 - y)]
    return x, y, c, chips


class _Rider:
    n = 0
    relay = None

    def ride(self, grid, refs, when):
        if not self.n:
            return
        ids = [pl.program_id(a) for a in range(len(grid))]
        first = functools.reduce(jnp.logical_and, [i == 0 for i in ids])
        last = functools.reduce(jnp.logical_and, [i == g - 1 for i, g in zip(ids, grid)])
        phases = [(first, self.start), (last, self.relay)] if when == "start" else [(last, self.finish)]
        for cond, phase in phases:
            if phase is not None:
                pl.when(cond)(functools.partial(phase, *refs))


class _Gather(_Rider):
    PER_GROUP = 7

    def __init__(self, groups):
        self.groups = list(groups)
        self.n = len(self.groups)
        self.in_specs = [ANY] * self.n
        self.out_specs = [ANY] * self.n
        self.out_shape = [jax.ShapeDtypeStruct((NSH,) + g.shape, g.dtype) for g in self.groups]
        sems = pltpu.SemaphoreType.DMA((self.PER_GROUP * self.n,))
        self.scratch = [sems, sems] if self.n else []

    def _copies(self, ins, outs, send, recv):
        x, y, c, chips = _place()
        sibling = (x, y, 1 - c)

        def half(gi, chip, hc):
            return outs[gi].at[2 * chip[0] + chip[1], hc]

        def copy(gi, k, src, dst, to):
            sem = self.PER_GROUP * gi + k
            return pltpu.make_async_remote_copy(src_ref=src, dst_ref=dst, send_sem=send.at[sem], recv_sem=recv.at[sem],
                                                device_id=to, device_id_type=MESH)

        pairs = [(gi, j, chip) for gi in range(self.n) for j, chip in enumerate(chips)]
        first = [copy(gi, j, ins[gi].at[c], half(gi, (x, y), c), (*chip, c)) for gi, j, chip in pairs]
        first += [copy(gi, 6, ins[gi], outs[gi].at[2 * x + y], sibling) for gi in range(self.n)]
        landed = [copy(gi, j, half(gi, chip, c), half(gi, chip, c), sibling) for gi, j, chip in pairs]
        relay = [copy(gi, 3 + j, half(gi, chip, c), half(gi, chip, c), sibling) for gi, j, chip in pairs]
        relayed = [copy(gi, 3 + j, half(gi, chip, 1 - c), half(gi, chip, 1 - c), sibling) for gi, j, chip in pairs]
        relayed += [copy(gi, 6, ins[gi], outs[gi].at[2 * x + y], sibling) for gi in range(self.n)]
        return first, landed, relay, relayed

    def start(self, ins, outs, send, recv):
        for cp in self._copies(ins, outs, send, recv)[0]:
            cp.start()

    def relay(self, ins, outs, send, recv):
        _, landed, relay, _ = self._copies(ins, outs, send, recv)
        for arrived, onward in zip(landed, relay):
            arrived.wait_recv()
            onward.start()

    def finish(self, ins, outs, send, recv):
        first, _, relay, relayed = self._copies(ins, outs, send, recv)
        for cp in relayed:
            cp.wait_recv()
        for cp in first + relay:
            cp.wait_send()

    def place(self, outs):
        return list(outs)


def _alone(name, rider):
    n = rider.n

    def body(*refs):
        parts = (refs[:n], refs[n:2 * n], refs[2 * n], refs[2 * n + 1])
        rider.start(*parts)
        if rider.relay is not None:
            rider.relay(*parts)
        rider.finish(*parts)

    outs = pl.pallas_call(
        body, name=name, in_specs=rider.in_specs, out_specs=rider.out_specs, out_shape=rider.out_shape,
        scratch_shapes=rider.scratch, compiler_params=pltpu.CompilerParams(has_side_effects=True),
    )(*rider.groups)
    return rider.place(outs)


class _Swap(_Rider):
    def __init__(self, slots):
        self.slots = list(slots)
        self.groups = [buf for buf, _, _ in self.slots]
        self.n = len(self.slots)
        self.in_specs = [ANY] * self.n
        self.out_specs = [ANY] * self.n
        self.out_shape = [jax.ShapeDtypeStruct((NSH, rows // 2, buf.shape[2]), buf.dtype) for buf, _, rows in self.slots]
        self.scratch = [pltpu.SemaphoreType.DMA((self.n,)), pltpu.SemaphoreType.DMA((self.n,))]

    def _copies(self, ins, outs, send, recv):
        x, y, c, _ = _place()
        cps = []
        for i, (_, row0, rows) in enumerate(self.slots):
            half = rows // 2
            src = ins[i].at[:, pl.ds(pl.multiple_of(row0 + (1 - c) * half, 16), half)]
            cps.append(pltpu.make_async_remote_copy(src_ref=src, dst_ref=outs[i], send_sem=send.at[i],
                                                    recv_sem=recv.at[i], device_id=(x, y, 1 - c), device_id_type=MESH))
        return cps

    def start(self, ins, outs, send, recv):
        for cp in self._copies(ins, outs, send, recv):
            cp.start()

    def finish(self, ins, outs, send, recv):
        for cp in self._copies(ins, outs, send, recv):
            cp.wait()

    def place(self, outs):
        return list(outs)


class _Share(_Rider):
    def __init__(self, arrays):
        self.groups = list(arrays)
        self.n = len(self.groups)
        self.in_specs = [ANY] * self.n
        self.out_specs = [ANY] * self.n
        self.out_shape = [jax.ShapeDtypeStruct((2,) + g.shape, g.dtype) for g in self.groups]
        self.scratch = [pltpu.SemaphoreType.DMA((self.n,)), pltpu.SemaphoreType.DMA((self.n,))]

    def _copies(self, ins, outs, send, recv, half):
        x, y, c, _ = _place()
        return [pltpu.make_async_remote_copy(src_ref=ins[gi], dst_ref=outs[gi].at[c if half == "mine" else 1 - c],
                                             send_sem=send.at[gi], recv_sem=recv.at[gi], device_id=(x, y, 1 - c),
                                             device_id_type=MESH) for gi in range(self.n)]

    def start(self, ins, outs, send, recv):
        for cp in self._copies(ins, outs, send, recv, "mine"):
            cp.start()

    def finish(self, ins, outs, send, recv):
        for cp in self._copies(ins, outs, send, recv, "theirs"):
            cp.wait_recv()
        for cp in self._copies(ins, outs, send, recv, "mine"):
            cp.wait_send()

    def place(self, outs):
        c = lax.axis_index("c")
        return [lax.dynamic_update_index_in_dim(o, g, c, 0) for o, g in zip(outs, self.groups)]


class _Send(_Rider):
    def __init__(self, arrays):
        self.groups = list(arrays)
        self.n = len(self.groups)
        self.in_specs = [ANY] * self.n
        self.out_specs = [ANY] * self.n
        self.out_shape = [jax.ShapeDtypeStruct((3,) + g.shape[1:], g.dtype) for g in self.groups]
        self.scratch = [pltpu.SemaphoreType.DMA((3 * self.n,)), pltpu.SemaphoreType.DMA((3 * self.n,))]

    def _copies(self, ins, outs, send, recv):
        x, y, c, chips = _place()
        return [pltpu.make_async_remote_copy(src_ref=ins[gi].at[2 * chip[0] + chip[1]], dst_ref=outs[gi].at[j],
                                             send_sem=send.at[3 * gi + j], recv_sem=recv.at[3 * gi + j],
                                             device_id=(*chip, c), device_id_type=MESH)
                for gi in range(self.n) for j, chip in enumerate(chips)]

    def start(self, ins, outs, send, recv):
        for cp in self._copies(ins, outs, send, recv):
            cp.start()

    def finish(self, ins, outs, send, recv):
        for cp in self._copies(ins, outs, send, recv):
            cp.wait()

    def place(self, outs):
        return list(outs)


def _pair_sum(name, slots, got, c_idx):
    n = len(slots)
    in_specs, out_specs, out_shape, operands = [], [], [], []
    for (buf, row0, rows), g in zip(slots, got):
        hb, cols = rows // 4, buf.shape[2]
        in_specs += [pl.BlockSpec((None, hb, cols), lambda q, i, cr, r=row0 // hb: (q, r + 2 * cr[0] + i, 0)),
                     pl.BlockSpec((None, hb, cols), lambda q, i, cr: (q, i, 0))]
        out_specs.append(pl.BlockSpec((None, hb, cols), lambda q, i, cr: (q, i, 0)))
        out_shape.append(jax.ShapeDtypeStruct(g.shape, BF16))
        operands += [buf, g]

    def body(c_ref, *refs):
        for i in range(n):
            refs[2 * n + i][...] = (refs[2 * i][...].astype(F32) + refs[2 * i + 1][...].astype(F32)).astype(BF16)

    return pl.pallas_call(
        body, name=name,
        grid_spec=pltpu.PrefetchScalarGridSpec(num_scalar_prefetch=1, grid=(NSH, 2), in_specs=in_specs, out_specs=out_specs),
        out_shape=out_shape, compiler_params=_params(("parallel", "parallel")),
    )(c_idx, *operands)


def _owner_sum(name, pairs, got, p_idx):
    n = len(pairs)
    in_specs, out_specs, out_shape, operands = [], [], [], []
    for own, g in zip(pairs, got):
        _, rows, cols = own.shape
        hb = rows // 2
        in_specs += [pl.BlockSpec((None, hb, cols), lambda i, pr: (pr[0], i, 0)),
                     pl.BlockSpec((3, hb, cols), lambda i, pr: (0, i, 0))]
        out_specs.append(pl.BlockSpec((hb, cols), lambda i, pr: (i, 0)))
        out_shape.append(jax.ShapeDtypeStruct((rows, cols), F32))
        operands += [own, g]

    def body(p_ref, *refs):
        for i in range(n):
            a_ref, b_ref = refs[2 * i], refs[2 * i + 1]
            refs[2 * n + i][...] = ((a_ref[...].astype(F32) + b_ref[0].astype(F32)) + b_ref[1].astype(F32)) + b_ref[2].astype(F32)

    return pl.pallas_call(
        body, name=name,
        grid_spec=pltpu.PrefetchScalarGridSpec(num_scalar_prefetch=1, grid=(2,), in_specs=in_specs, out_specs=out_specs),
        out_shape=out_shape, compiler_params=_params(("parallel",)),
    )(p_idx, *operands)


def _sum_small(slab):
    def body(in_ref, out_ref, all_ref, send, recv):
        x, y, c, _ = _place()
        me = 4 * x + 2 * y + c
        all_ref[me] = in_ref[...]
        cps = []
        for k in range(1, 8):
            peer = (x ^ (k >> 2), y ^ ((k >> 1) & 1), c ^ (k & 1))
            cps.append(pltpu.make_async_remote_copy(src_ref=in_ref, dst_ref=all_ref.at[me], send_sem=send.at[k - 1],
                                                    recv_sem=recv.at[k - 1], device_id=peer, device_id_type=MESH))
        for cp in cps:
            cp.start()
        for cp in cps:
            cp.wait()
        total = all_ref[0]
        for d in range(1, 8):
            total = total + all_ref[d]
        out_ref[...] = total

    return pl.pallas_call(
        body, name="sum_small",
        in_specs=[pl.BlockSpec(memory_space=pltpu.VMEM)], out_specs=pl.BlockSpec(memory_space=pltpu.VMEM),
        out_shape=jax.ShapeDtypeStruct(slab.shape, F32),
        scratch_shapes=[pltpu.VMEM((8,) + slab.shape, F32), pltpu.SemaphoreType.DMA((7,)), pltpu.SemaphoreType.DMA((7,))],
        compiler_params=pltpu.CompilerParams(has_side_effects=True),
    )(slab)


FFNS = ("pre0", "post0", "pre1", "post1")
W_SHAPES = dict({f + "_gu": (NSH, 2, FS, D) for f in FFNS}, **{f + "_d": (NSH, FS, D) for f in FFNS},
                ab_in=(NSH, D, 768), ab_out=(NSH, 256, D), conv=(NSH, 2, 8, LANE), c_in=(NSH, D, D), c_out=(NSH, 256, D))
CARRIED = dict(pre0_norm=("pre0_gu",), pre0_up=("pre0_d",), pre0_down=("ab_in", "conv"), ab_proj_a=("ab_out",),
               c_proj=("post1_d",),
               sb_fwd=("post0_gu", "post0_d"), post0_up=("pre1_gu",), post0_down=("pre1_d",),
               pre1_up=("c_in", "c_out"), hgrn_fwd=("post1_gu",))


FFN_SLOTS = dict(pre0=(0, 2, 0), pre1=(1, 3, 1), post0=(4, 6, 2), post1=(5, 7, 3))
GRAD_SLOTS = dict(ab_out=("b", B_ABOUT, 256), c_in=("b", B_CIN, D), c_out=("b", B_COUT, 256), ab_in=("c", 0, D))
for _f, (_g, _u, _d) in FFN_SLOTS.items():
    GRAD_SLOTS.update({_f + "_g": ("a", _g * FS, FS), _f + "_u": ("a", _u * FS, FS), _f + "_d": ("b", _d * FS, FS)})
REDUCE_STAGES = dict(x=("post1_g", "post1_u", "post1_d", "c_out"), w=("c_in", "pre1_d"),
                     y=("pre1_g", "pre1_u", "post0_g", "post0_u", "post0_d", "ab_out"),
                     z1=("ab_in",), z2=("pre0_g", "pre0_u", "pre0_d"))


def _local_step(x, target, weights, small, shards=None, place=None):
    t = x.shape[0]
    tm = min(MM_TILE, t)
    tw = min(WGRAD_TILE, t)
    nt = t // tm
    tok_si = _spec((tm, D), lambda s, i: (i, 0))
    w = dict(weights)
    reduced = {}

    def stage_slots(stage, buffers):
        return [(buffers[GRAD_SLOTS[n][0]],) + GRAD_SLOTS[n][1:] for n in REDUCE_STAGES[stage]]

    def swap_rider(stage, buffers):
        return _Swap(stage_slots(stage, buffers)) if place is not None else None

    def reduce_start(stage, buffers, swapped=None):
        if place is None:
            return [], None
        slots = stage_slots(stage, buffers)
        if swapped is None:
            swapped = _alone("grad_swap_" + stage, _Swap(slots))
        pairs = _pair_sum("grad_pair_sum_" + stage, slots, swapped, place[0])
        return pairs, _Send(pairs)

    def reduce_end(stage, pairs, landed):
        if place is None:
            return None
        mine = _owner_sum("grad_owner_sum_" + stage, pairs, landed, place[1])
        return _Share(mine)

    def shared(stage, landed):
        if place is not None:
            reduced.update(zip(REDUCE_STAGES[stage], landed))

    def carried(kernel_name):
        names = [n for n in CARRIED[kernel_name] if n not in w]
        return names, (_Gather([shards[n] for n in names]) if names else None)

    def land(names, arrays):
        for n, a in zip(names, arrays):
            w[n] = a.reshape(W_SHAPES[n])

    def ffn_forward(tag, h, hn, next_row):
        names, gather = carried(tag + "_up") if tag + "_up" in CARRIED else ([], None)
        s_up, s_gate, a, got = _ffn_up(tag + "_up", hn, w[tag + "_gu"], gather)
        land(names, got)
        names, gather = carried(tag + "_down") if tag + "_down" in CARRIED else ([], None)
        out = _ffn_down(tag + "_down", a, w[tag + "_d"], h, gather, (small, next_row) if next_row is not None else None)
        if gather is not None:
            out, got = out
            land(names, got)
        out, hn_next = out if next_row is not None else (out, None)
        return out, hn_next, (h, hn, s_up, s_gate, a)

    def out_proj(name, y, w_out, h, next_row):
        return _mm(name, [(y, _spec((tm, 256), lambda i, k: (i, k)), w_out, _spec((None, 256, D), lambda i, k: (k, 0, 0)))],
                   grid=(nt, NSH), o_shape=(t, D), o_dtype=F32, o_spec=_spec((tm, D), lambda i, k: (i, 0)),
                   dims=NN, kaxis=1, nk=NSH, acc_shape=(tm, D), res=(h, _spec((tm, D), lambda i, k: (i, 0))),
                   norm=(small, next_row))

    def out_proj_bwd(tag, dhb, y, w_out, blk, grad_b, make_rider=None):
        grad_b = _wgrad(tag + "_dwout", y, _spec((tw, 256), lambda s, k: (k, s)), dhb, _spec((tw, D), lambda s, k: (k, 0)),
                        256, D, t, tw, grad_b, blk)
        rider = make_rider(grad_b) if make_rider is not None else None
        dy = _mm(tag + "_dy", [(dhb, tok_si, w_out, _spec((None, 256, D), lambda s, i: (s, 0, 0)))],
                 grid=(NSH, nt), o_shape=(t, D), o_dtype=F32, o_spec=_spec((tm, 256), lambda s, i: (i, s)),
                 dims=NT, kaxis=1, nk=1, carry=rider)
        dy, landed = dy if rider is not None else (dy, None)
        return dy, grad_b, landed

    h0 = x
    names, gather = carried("pre0_norm")
    hn, got = _norm_fwd("pre0_norm", h0, small, R_PRE, gather)
    land(names, got)
    h1, hn_ab, pre0 = ffn_forward("pre0", h0, hn, R_MIX)
    def carrying(kernel_name, make):
        names, gather = carried(kernel_name)
        out = make(gather)
        if gather is not None:
            out, got = out
            land(names, got)
        return out

    proj_a = carrying("ab_proj_a", lambda gather: _mm(
        "ab_proj_a", [(hn_ab, tok_si, w["ab_in"], _spec((None, D, 768), lambda s, i: (s, 0, 0)))],
        grid=(2, nt), o_shape=(t, 1536), o_dtype=F32, o_spec=_spec((tm, 768), lambda s, i: (i, s)),
        dims=NN, kaxis=1, nk=1, carry=gather))
    proj_b = _mm("ab_proj_b", [(hn_ab, tok_si, w["ab_in"], _spec((None, D, 768), lambda s, i: (s + 2, 0, 0)))],
                 grid=(2, nt), o_shape=(t, 1536), o_dtype=BF16, o_spec=_spec((tm, 768), lambda s, i: (i, s)),
                 dims=NN, kaxis=1, nk=1)
    y_a = _conv_fwd(proj_a, w["conv"])
    names, gather = carried("sb_fwd")
    y_b, ltot, nblk, got = _sb_fwd(proj_b, gather)
    land(names, got)
    y_ab = jnp.concatenate([y_a, y_b], axis=1)
    h2, hn = out_proj("ab_out", y_ab, w["ab_out"], h1, R_POST)
    h3, hn, post0 = ffn_forward("post0", h2, hn, R_PRE + 1)
    h4, hn_c, pre1 = ffn_forward("pre1", h3, hn, R_MIX + 1)
    proj_c = carrying("c_proj", lambda gather: _mm(
        "c_proj", [(hn_c, tok_si, w["c_in"], _spec((None, D, D), lambda s, i: (s, 0, 0)))],
        grid=(NSH, nt), o_shape=(NSH, t, D), o_dtype=F32, o_spec=_spec((None, tm, D), lambda s, i: (s, i, 0)),
        dims=NN, kaxis=1, nk=1, carry=gather))
    names, gather = carried("hgrn_fwd")
    o_c, y_c, sst, got = _hgrn_fwd(proj_c, small, gather)
    land(names, got)
    h5, hn = out_proj("c_out", y_c, w["c_out"], h4, R_POST + 1)
    h6, _, post1 = ffn_forward("post1", h5, hn, None)
    dh, dhb, d_fin, loss = _final_loss(h6, small, target)

    dh, dhb, grad_a, grad_b, dn_post1, _, _ = _ffn_backward("post1", dh, dhb, *post1, w["post1_gu"], w["post1_d"],
                                                            *FFN_SLOTS["post1"], small, R_POST + 1, 8 * FS, B_ROWS)
    dy_c, grad_b, swapped = out_proj_bwd("c", dhb, y_c, w["c_out"], B_COUT // 256, grad_b,
                                         lambda buf_b: swap_rider("x", dict(a=grad_a, b=buf_b)))
    pairs, rider = reduce_start("x", dict(a=grad_a, b=grad_b), swapped)
    dproj_c, d_clb, d_gam, landed = _hgrn_bwd(proj_c, small, o_c, sst, dy_c, rider)
    share = reduce_end("x", pairs, landed)
    grad_b = _wgrad("c_dwin", hn_c, _spec((tw, D), lambda s, k: (k, 0)), dproj_c, _spec((None, tw, D), lambda s, k: (s, k, 0)),
                    D, D, t, tw, grad_b, B_CIN // D, carry=share)
    if share is not None:
        grad_b, landed = grad_b
        shared("x", landed)
    dh, dhb, dn_mix1, _ = _dhn_norm("c_dhn", [(dproj_c, _spec((None, tm, D), lambda i, k: (k, i, 0)),
                                               w["c_in"], _spec((None, D, D), lambda i, k: (k, 0, 0)))],
                                    NT, h4, small, R_MIX + 1, dh)
    stage_w = {}

    def w_swap(buf_a, buf_b):
        return swap_rider("w", dict(a=buf_a, b=buf_b))

    def w_send(buf_a, buf_b, swapped):
        stage_w["pairs"], rider = reduce_start("w", dict(a=buf_a, b=buf_b), swapped)
        return rider

    on = place is not None
    dh, dhb, grad_a, grad_b, dn_pre1, _, landed = _ffn_backward(
        "pre1", dh, dhb, *pre1, w["pre1_gu"], w["pre1_d"], *FFN_SLOTS["pre1"], small, R_PRE + 1, grad_a, grad_b,
        w_swap if on else None, w_send if on else None)
    share = reduce_end("w", stage_w.get("pairs"), landed)
    dh, dhb, grad_a, grad_b, dn_post0, landed, _ = _ffn_backward(
        "post0", dh, dhb, *post0, w["post0_gu"], w["post0_d"], *FFN_SLOTS["post0"], small, R_POST, grad_a, grad_b,
        (lambda buf_a, buf_b: share) if on else None)
    shared("w", landed)
    dy_ab, grad_b, _ = out_proj_bwd("ab", dhb, y_ab, w["ab_out"], B_ABOUT // 256, grad_b)
    dab, dac, dax, d_conv, swapped = _conv_bwd(proj_a, w["conv"], dy_ab, swap_rider("y", dict(a=grad_a, b=grad_b)))
    pairs, rider = reduce_start("y", dict(a=grad_a, b=grad_b), swapped)
    dq, dk, dv, landed = _sb_bwd(proj_b, dy_ab, ltot, nblk, rider)
    share = reduce_end("y", pairs, landed)
    dproj_ab = jnp.concatenate([dab, dac, dax, dq, dk, dv], axis=1)
    grad_c = _wgrad("ab_dwin", hn_ab, _spec((tw, D), lambda s, k: (k, 0)), dproj_ab, _spec((tw, 768), lambda s, k: (k, s)),
                    D, 768, t, tw, D, 0, carry=share)
    if share is not None:
        grad_c, landed = grad_c
        shared("y", landed)
    dh, dhb, dn_mix0, _ = _dhn_norm("ab_dhn", [(dproj_ab, _spec((tm, 768), lambda i, k: (i, k)),
                                                w["ab_in"], _spec((None, D, 768), lambda i, k: (k, 0, 0)))],
                                    NT, h1, small, R_MIX, dh)
    last = {}

    def z1_send(buf_a, buf_b):
        last["z1"], rider = reduce_start("z1", dict(b=buf_b, c=grad_c))
        return rider

    def z2_send(buf_a, buf_b, _):
        last["z2"], rider = reduce_start("z2", dict(a=buf_a, b=buf_b))
        return rider

    dh, dhb, grad_a, grad_b, dn_pre0, landed, landed_own = _ffn_backward(
        "pre0", dh, dhb, *pre0, w["pre0_gu"], w["pre0_d"], *FFN_SLOTS["pre0"], small, R_PRE, grad_a, grad_b,
        z1_send if on else None, z2_send if on else None)
    if place is not None:
        mine = (_owner_sum("grad_owner_sum_z1", last["z1"], landed, place[1])
                + _owner_sum("grad_owner_sum_z2", last["z2"], landed_own, place[1]))
        reduced.update(zip(REDUCE_STAGES["z1"] + REDUCE_STAGES["z2"], _alone("grad_share_z", _Share(mine))))
    zero = jnp.zeros((1, D), F32)
    conv_rows = jnp.pad(jnp.transpose(d_conv[:, :3, :], (1, 0, 2)).reshape(3, 512), ((0, 0), (0, D - 512)))
    small_grad = jnp.concatenate([
        dn_pre0, dn_pre1, dn_mix0, dn_mix1, dn_post0, dn_post1, d_clb, d_fin,
        jnp.pad(d_gam, ((0, 0), (0, D - HD))), conv_rows,
        jnp.pad(loss, ((0, 0), (0, D - 1))), zero, zero], axis=0)
    return dh, grad_a, grad_b, grad_c, small_grad, reduced


def _small_slab(rows):
    parts = [jnp.pad(r.astype(F32), ((0, 0), (0, D - r.shape[1]))) for r in rows]
    slab = jnp.concatenate(parts, axis=0)
    return jnp.pad(slab, ((0, SMALL_ROWS - slab.shape[0]), (0, 0)))


def kernel(x, ffn_pre_norm, ffn_pre_w_gate, ffn_pre_w_up, ffn_pre_w_down, mix_norm, ffn_post_norm, ffn_post_w_gate, ffn_post_w_up, ffn_post_w_down, ab_w_in, ab_conv_w, ab_w_out, c_w_in, c_lower_bounds, c_out_norm, c_w_out, final_norm, loss_target, m_ffn_pre_norm, m_ffn_pre_w_gate, m_ffn_pre_w_up, m_ffn_pre_w_down, m_mix_norm, m_ffn_post_norm, m_ffn_post_w_gate, m_ffn_post_w_up, m_ffn_post_w_down, m_ab_w_in, m_ab_conv_w, m_ab_w_out, m_c_w_in, m_c_lower_bounds, m_c_out_norm, m_c_w_out, m_final_norm, v_ffn_pre_norm, v_ffn_pre_w_gate, v_ffn_pre_w_up, v_ffn_pre_w_down, v_mix_norm, v_ffn_post_norm, v_ffn_post_w_gate, v_ffn_post_w_up, v_ffn_post_w_down, v_ab_w_in, v_ab_conv_w, v_ab_w_out, v_c_w_in, v_c_lower_bounds, v_c_out_norm, v_c_w_out, v_final_norm):
    t = x.shape[1]
    xi, yi, ci = lax.axis_index("x"), lax.axis_index("y"), lax.axis_index("c")
    p_idx = (2 * xi + yi).astype(jnp.int32).reshape(1)
    c_idx = ci.astype(jnp.int32).reshape(1)

    def halves(m):
        return m.astype(BF16).reshape(2, m.shape[0] // 2, m.shape[1])

    transposed = ("ffn_pre_w_gate", "ffn_pre_w_up", "ffn_post_w_gate", "ffn_post_w_up")

    def flip(a):
        return jnp.swapaxes(a, 1, 2)

    shards = {}
    for name, (w_gate, w_up, w_down, layer) in dict(
            pre0=(ffn_pre_w_gate, ffn_pre_w_up, ffn_pre_w_down, 0), post0=(ffn_post_w_gate, ffn_post_w_up, ffn_post_w_down, 0),
            pre1=(ffn_pre_w_gate, ffn_pre_w_up, ffn_pre_w_down, 1), post1=(ffn_post_w_gate, ffn_post_w_up, ffn_post_w_down, 1)).items():
        shards[name + "_gu"] = jnp.stack([flip(w_gate)[layer], flip(w_up)[layer]]).astype(BF16)
        shards[name + "_d"] = halves(w_down[layer])
    conv_pad = jnp.pad(ab_conv_w[0], ((0, 5), (0, 0)))
    shards.update(ab_in=halves(ab_w_in[0]), ab_out=halves(ab_w_out[0]), c_in=halves(c_w_in[0]), c_out=halves(c_w_out[0]),
                  conv=jnp.stack([conv_pad, jnp.zeros_like(conv_pad)]))
    small = _small_slab([ffn_pre_norm, mix_norm, ffn_post_norm, c_lower_bounds, final_norm.reshape(1, D), c_out_norm])
    small = small.reshape(SMALL_ROWS, 1, D)

    grad_x, _, _, _, small_grad, reduced = _local_step(x[0], loss_target[0], {}, small, shards,
                                                        (c_idx, p_idx))
    whole = {n: g.reshape(2 * g.shape[1], g.shape[2]) for n, g in reduced.items()}
    small_sum = _sum_small(small_grad)

    my_conv = lax.dynamic_slice(small_sum[R_CONV:R_CONV + 3], (0, (2 * xi + yi) * 128), (3, 128))
    grads = {
        "ffn_pre_norm": small_sum[R_PRE:R_PRE + 2], "mix_norm": small_sum[R_MIX:R_MIX + 2],
        "ffn_post_norm": small_sum[R_POST:R_POST + 2], "c_lower_bounds": small_sum[R_CLB:R_CLB + 2],
        "c_out_norm": small_sum[R_GAM:R_GAM + 1, :HD], "final_norm": small_sum[R_FIN],
        "ab_conv_w": my_conv.reshape(1, 3, 128),
    }
    layers = dict(ab_w_in=[whole["ab_in"]], ab_w_out=[whole["ab_out"]], c_w_in=[whole["c_in"]], c_w_out=[whole["c_out"]])
    for kind, key in (("gate", "_g"), ("up", "_u"), ("down", "_d")):
        layers["ffn_pre_w_" + kind] = [whole["pre0" + key], whole["pre1" + key]]
        layers["ffn_post_w_" + kind] = [whole["post0" + key], whole["post1" + key]]
    weights = dict(ffn_pre_norm=ffn_pre_norm, ffn_pre_w_gate=ffn_pre_w_gate, ffn_pre_w_up=ffn_pre_w_up, ffn_pre_w_down=ffn_pre_w_down, mix_norm=mix_norm, ffn_post_norm=ffn_post_norm, ffn_post_w_gate=ffn_post_w_gate, ffn_post_w_up=ffn_post_w_up, ffn_post_w_down=ffn_post_w_down, ab_w_in=ab_w_in, ab_conv_w=ab_conv_w, ab_w_out=ab_w_out, c_w_in=c_w_in, c_lower_bounds=c_lower_bounds, c_out_norm=c_out_norm, c_w_out=c_w_out, final_norm=final_norm)
    m_in = dict(ffn_pre_norm=m_ffn_pre_norm, ffn_pre_w_gate=m_ffn_pre_w_gate, ffn_pre_w_up=m_ffn_pre_w_up, ffn_pre_w_down=m_ffn_pre_w_down, mix_norm=m_mix_norm, ffn_post_norm=m_ffn_post_norm, ffn_post_w_gate=m_ffn_post_w_gate, ffn_post_w_up=m_ffn_post_w_up, ffn_post_w_down=m_ffn_post_w_down, ab_w_in=m_ab_w_in, ab_conv_w=m_ab_conv_w, ab_w_out=m_ab_w_out, c_w_in=m_c_w_in, c_lower_bounds=m_c_lower_bounds, c_out_norm=m_c_out_norm, c_w_out=m_c_w_out, final_norm=m_final_norm)
    v_in = dict(ffn_pre_norm=v_ffn_pre_norm, ffn_pre_w_gate=v_ffn_pre_w_gate, ffn_pre_w_up=v_ffn_pre_w_up, ffn_pre_w_down=v_ffn_pre_w_down, mix_norm=v_mix_norm, ffn_post_norm=v_ffn_post_norm, ffn_post_w_gate=v_ffn_post_w_gate, ffn_post_w_up=v_ffn_post_w_up, ffn_post_w_down=v_ffn_post_w_down, ab_w_in=v_ab_w_in, ab_conv_w=v_ab_conv_w, ab_w_out=v_ab_w_out, c_w_in=v_c_w_in, c_lower_bounds=v_c_lower_bounds, c_out_norm=v_c_out_norm, c_w_out=v_c_w_out, final_norm=v_final_norm)
    names = list(weights)
    big = [n for n in names if weights[n].size >= 65536]
    tiny = [n for n in names if n not in big]

    delta, new_m, new_v = {}, {}, {}
    for n in big:
        turn = flip if n in transposed else (lambda a: a)
        shape = turn(weights[n]).shape
        two_d = (shape[0] * shape[1], shape[2])
        d, m2, v2, g2 = _adamw("adamw_" + n, turn(weights[n]).reshape(two_d), layers[n],
                               turn(m_in[n]).reshape(two_d), turn(v_in[n]).reshape(two_d))
        delta[n], new_m[n], new_v[n] = turn(d.reshape(shape)), turn(m2.reshape(shape)), turn(v2.reshape(shape))
        grads[n] = turn(g2.reshape(shape))

    def tiny_slab(src):
        return _small_slab([src[n].reshape(-1, src[n].shape[-1]) for n in tiny])

    offs, row = {}, 0
    for n in tiny:
        nrows = weights[n].size // weights[n].shape[-1]
        offs[n] = (row, nrows)
        row += nrows
    d, m2, v2, _ = _adamw("adamw_small", tiny_slab(weights), [tiny_slab(grads)], tiny_slab(m_in), tiny_slab(v_in))
    for n in tiny:
        r0, nr = offs[n]
        shape = weights[n].shape
        for dst, src in ((delta, d), (new_m, m2), (new_v, v2)):
            dst[n] = src[r0:r0 + nr, :shape[-1]].reshape(shape)

    loss = small_sum[R_LOSS, 0]
    return (loss, grad_x.reshape(1, t, D), *[grads[n] for n in names], *[delta[n] for n in names],
            *[new_m[n] for n in names], *[new_v[n] for n in names])
```

```python
import functools
import math

import jax
import jax.numpy as jnp
from jax import lax
from jax.experimental import pallas as pl
from jax.experimental.pallas import tpu as pltpu

F32 = jnp.float32
BF16 = jnp.bfloat16
MESH = pl.DeviceIdType.MESH
ANY = pl.BlockSpec(memory_space=pl.ANY)

D = 1024
FS = 704
NSH = 4
RMS_EPS = 1e-6
MACARON = 0.5
CHUNK = 64
HD = 128
HGRN_HPS = 8
SBQ = 128
SB_PAIRS_FWD = 4
SB_PAIRS_BWD = 4
SB_DEAD = -105.0
CONV_HALO = 8
LANE = 128
ROW_TILE = 1024
MM_TILE = 1024
WGRAD_TILE = 4096
VMEM_LIMIT = 48 * 1024 * 1024
VMEM_LIMIT_BIG = 58 * 1024 * 1024

ADAM_LR, ADAM_B1, ADAM_B2, ADAM_EPS, ADAM_WD, ADAM_STEP = 0.001, 0.9, 0.999, 1e-08, 0.01, 10

NN = ((1,), (0,))
NT = ((1,), (1,))
TN = ((0,), (0,))

SMALL_ROWS = 16
R_PRE, R_MIX, R_POST, R_CLB, R_FIN, R_GAM, R_CONV, R_LOSS = 0, 2, 4, 6, 8, 9, 10, 13

B_ABOUT = 4 * FS
B_CIN = B_ABOUT + 256
B_COUT = B_CIN + 1024
B_ROWS = B_COUT + 256


def _dg(a, b, dims):
    return lax.dot_general(a, b, (dims, ((), ())), preferred_element_type=F32)


def _split(x):
    hi = x.astype(BF16)
    lo = (x - hi.astype(F32)).astype(BF16)
    return hi, lo


def _dot3(a, b, dims):
    ah, al = _split(a)
    bh, bl = _split(b)
    if dims == TN:
        n = b.shape[1]
        both = _dg(ah, jnp.concatenate([bh, bl], axis=1), dims)
        return both[:, :n] + both[:, n:] + _dg(al, bh, dims)
    m = a.shape[0]
    both = _dg(jnp.concatenate([ah, al], axis=0), bh, dims)
    return both[:m] + both[m:] + _dg(ah, bl, dims)


def _sigmoid(x):
    return 1.0 / (1.0 + jnp.exp(-x))


def _params(sem):
    return pltpu.CompilerParams(dimension_semantics=sem, vmem_limit_bytes=VMEM_LIMIT)


def _spec(shape, imap):
    return pl.BlockSpec(shape, imap)


def _accumulate(acc_ref, pairs, dims):
    part = None
    for a_ref, b_ref in pairs:
        d = _dg(a_ref[...], b_ref[...], dims)
        part = d if part is None else part + d
    acc_ref[...] += part


def _mm(name, pairs, *, grid, o_shape, o_dtype, o_spec, dims, kaxis, nk, acc_shape=None, res=None, scale=None,
        into=None, carry=None, norm=None):
    npairs = len(pairs)
    operands, specs = [], []
    for a, a_spec, b, b_spec in pairs:
        operands += [a, b]
        specs += [a_spec, b_spec]
    if res is not None:
        operands.append(res[0])
        specs.append(res[1])
    aliases = {}
    if into is not None:
        aliases = {len(operands): 0}
        operands.append(into)
        specs.append(ANY)
    if norm is not None:
        operands.append(norm[0])
        specs.append(_spec((None, 1, D), lambda *_: (norm[1], 0, 0)))
    n_own = len(operands)
    n_out = 2 if norm is not None else 1
    nc = carry.n if carry is not None else 0
    if nc:
        operands += carry.groups
        specs += carry.in_specs

    def body(*refs):
        o_ref = refs[n_own + nc]
        riders = ((refs[n_own:n_own + nc], refs[n_own + nc + n_out:n_own + 2 * nc + n_out], refs[-2], refs[-1])
                  if nc else None)
        if nc:
            carry.ride(grid, riders, "start")

        def finish(val):
            if scale is not None:
                val = val * scale
            if res is not None:
                val = val + refs[2 * npairs][...]
            o_ref[...] = val.astype(o_dtype)
            if norm is not None:
                r = lax.rsqrt(jnp.mean(val * val, axis=-1, keepdims=True) + RMS_EPS)
                refs[n_own + nc + 1][...] = (val * r * refs[n_own - 1][...]).astype(BF16)

        if nk == 1:
            part = None
            for n in range(npairs):
                d = _dg(refs[2 * n][...], refs[2 * n + 1][...], dims)
                part = d if part is None else part + d
            finish(part)
        else:
            acc_ref = refs[n_own + 2 * nc + n_out]
            k = pl.program_id(kaxis)

            @pl.when(k == 0)
            def _():
                acc_ref[...] = jnp.zeros_like(acc_ref)

            _accumulate(acc_ref, [(refs[2 * n], refs[2 * n + 1]) for n in range(npairs)], dims)

            @pl.when(k == nk - 1)
            def _():
                finish(acc_ref[...])
        if nc:
            carry.ride(grid, riders, "finish")

    sem = tuple("arbitrary" if (nc or (ax == kaxis and nk > 1)) else "parallel" for ax in range(len(grid)))
    outs = pl.pallas_call(
        body, name=name, grid=grid, in_specs=specs,
        out_specs=[o_spec] * n_out + (carry.out_specs if nc else []),
        out_shape=[jax.ShapeDtypeStruct(o_shape, o_dtype)] + ([jax.ShapeDtypeStruct(o_shape, BF16)] if norm is not None else [])
        + (carry.out_shape if nc else []),
        scratch_shapes=([pltpu.VMEM(acc_shape, F32)] if nk > 1 else []) + (carry.scratch if nc else []),
        input_output_aliases=aliases,
        compiler_params=_params(sem),
    )(*operands)
    main = (outs[0], outs[1]) if norm is not None else outs[0]
    return (main, carry.place(outs[n_out:])) if nc else main


def _norm_fwd(name, h, gain_slab, row, rider=None):
    t = h.shape[0]
    tm = min(ROW_TILE, t)
    grid = (t // tm,)
    nr = rider.n if rider is not None else 0

    def body(*refs):
        h_ref, g_ref, o_ref = refs[0], refs[1], refs[2 + nr]
        riders = (refs[2:2 + nr], refs[3 + nr:3 + 2 * nr], refs[-2], refs[-1]) if nr else None
        if nr:
            rider.ride(grid, riders, "start")
        x = h_ref[...]
        r = lax.rsqrt(jnp.mean(x * x, axis=-1, keepdims=True) + RMS_EPS)
        o_ref[...] = (x * r * g_ref[...]).astype(BF16)
        if nr:
            rider.ride(grid, riders, "finish")

    outs = pl.pallas_call(
        body, name=name, grid=grid,
        in_specs=[_spec((tm, D), lambda i: (i, 0)), _spec((None, 1, D), lambda i: (row, 0, 0))]
        + (rider.in_specs if nr else []),
        out_specs=[_spec((tm, D), lambda i: (i, 0))] + (rider.out_specs if nr else []),
        out_shape=[jax.ShapeDtypeStruct((t, D), BF16)] + (rider.out_shape if nr else []),
        scratch_shapes=rider.scratch if nr else [],
        compiler_params=_params(("arbitrary",) if nr else ("parallel",)),
    )(h, gain_slab, *(rider.groups if nr else []))
    return outs[0], (rider.place(outs[1:]) if nr else [])


def _dhn_norm(name, pairs, dims, h, gain_slab, row, dres, rider=None):
    t = h.shape[0]
    tm = min(MM_TILE, t)
    nt = t // tm
    grid = (nt, NSH)
    npairs = len(pairs)
    nr = rider.n if rider is not None else 0
    operands, specs = [], []
    for a, a_spec, b, b_spec in pairs:
        operands += [a, b]
        specs += [a_spec, b_spec]
    row_spec = _spec((tm, D), lambda i, k: (i, 0))
    operands += [h, gain_slab, dres]
    specs += [row_spec, _spec((None, 1, D), lambda i, k: (row, 0, 0)), row_spec]
    n_own = len(operands)

    def body(*refs):
        h_ref, g_ref, dres_ref = refs[2 * npairs:n_own]
        dh_ref, dhb_ref, dg_ref = refs[n_own + nr:n_own + nr + 3]
        acc_ref, gacc_ref = refs[n_own + 2 * nr + 3:n_own + 2 * nr + 5]
        riders = (refs[n_own:n_own + nr], refs[n_own + nr + 3:n_own + 2 * nr + 3], refs[-2], refs[-1]) if nr else None
        if nr:
            rider.ride(grid, riders, "start")
        i, k = pl.program_id(0), pl.program_id(1)

        @pl.when(k == 0)
        def _():
            acc_ref[...] = jnp.zeros_like(acc_ref)

        _accumulate(acc_ref, [(refs[2 * n], refs[2 * n + 1]) for n in range(npairs)], dims)

        @pl.when(k == NSH - 1)
        def _():
            x = h_ref[...]
            r = lax.rsqrt(jnp.mean(x * x, axis=-1, keepdims=True) + RMS_EPS)
            xh = x * r
            dy = acc_ref[...]
            gdy = dy * g_ref[...]
            dh = dres_ref[...] + (gdy - xh * jnp.mean(gdy * xh, axis=-1, keepdims=True)) * r
            dh_ref[...] = dh
            dhb_ref[...] = dh.astype(BF16)
            gpart = jnp.sum((dy * xh).reshape(tm // 8, 8, D), axis=0)

            @pl.when(i == 0)
            def _():
                gacc_ref[...] = gpart

            @pl.when(i > 0)
            def _():
                gacc_ref[...] += gpart

            @pl.when(i == nt - 1)
            def _():
                dg_ref[...] = jnp.sum(gacc_ref[...], axis=0, keepdims=True)

        if nr:
            rider.ride(grid, riders, "finish")

    outs = pl.pallas_call(
        body, name=name, grid=grid, in_specs=specs + (rider.in_specs if nr else []),
        out_specs=[row_spec, row_spec, _spec((1, D), lambda i, k: (0, 0))] + (rider.out_specs if nr else []),
        out_shape=[jax.ShapeDtypeStruct((t, D), F32), jax.ShapeDtypeStruct((t, D), BF16),
                   jax.ShapeDtypeStruct((1, D), F32)] + (rider.out_shape if nr else []),
        scratch_shapes=[pltpu.VMEM((tm, D), F32), pltpu.VMEM((8, D), F32)] + (rider.scratch if nr else []),
        compiler_params=pltpu.CompilerParams(dimension_semantics=("arbitrary", "arbitrary"),
                                             vmem_limit_bytes=VMEM_LIMIT_BIG),
    )(*operands, *(rider.groups if nr else []))
    return outs[0], outs[1], outs[2], (rider.place(outs[3:]) if nr else [])


def _final_loss(h, gain_slab, target):
    t = h.shape[0]
    tm = min(ROW_TILE, t)
    nt = t // tm

    def body(h_ref, g_ref, t_ref, dh_ref, dhb_ref, dg_ref, loss_ref, acc_ref, lacc_ref):
        i = pl.program_id(0)
        x = h_ref[...]
        g = g_ref[...]
        r = lax.rsqrt(jnp.mean(x * x, axis=-1, keepdims=True) + RMS_EPS)
        xh = x * r
        err = xh * g - t_ref[...]
        dy = err * (1.0 / D)
        gdy = dy * g
        dh = (gdy - xh * jnp.mean(gdy * xh, axis=-1, keepdims=True)) * r
        dh_ref[...] = dh
        dhb_ref[...] = dh.astype(BF16)
        part = jnp.sum((dy * xh).reshape(tm // 8, 8, D), axis=0)
        lpart = jnp.sum((err * err).reshape(tm // 8, 8, D), axis=0)

        @pl.when(i == 0)
        def _():
            acc_ref[...] = part
            lacc_ref[...] = lpart

        @pl.when(i > 0)
        def _():
            acc_ref[...] += part
            lacc_ref[...] += lpart

        @pl.when(i == nt - 1)
        def _():
            dg_ref[...] = jnp.sum(acc_ref[...], axis=0, keepdims=True)
            rows = jnp.sum(lacc_ref[...], axis=0, keepdims=True)
            loss_ref[...] = jnp.sum(rows, axis=1, keepdims=True) * (0.5 / D)

    row_spec = _spec((tm, D), lambda i: (i, 0))
    return pl.pallas_call(
        body, name="final_loss", grid=(nt,),
        in_specs=[row_spec, _spec((None, 1, D), lambda i: (R_FIN, 0, 0)), row_spec],
        out_specs=[row_spec, row_spec, _spec((1, D), lambda i: (0, 0)), _spec((1, 1), lambda i: (0, 0))],
        out_shape=[jax.ShapeDtypeStruct((t, D), F32), jax.ShapeDtypeStruct((t, D), BF16),
                   jax.ShapeDtypeStruct((1, D), F32), jax.ShapeDtypeStruct((1, 1), F32)],
        scratch_shapes=[pltpu.VMEM((8, D), F32), pltpu.VMEM((8, D), F32)],
        compiler_params=_params(("arbitrary",)),
    )(h, gain_slab, target)


def _ffn_up(name, hn, wgu, carry=None):
    t = hn.shape[0]
    tm = min(MM_TILE, t)
    grid = (NSH, t // tm)
    nc = carry.n if carry is not None else 0

    def body(*refs):
        x_ref, wg_ref, wu_ref = refs[:3]
        s_up_ref, s_gate_ref, a_ref = refs[3 + nc:6 + nc]
        riders = (refs[3:3 + nc], refs[6 + nc:6 + 2 * nc], refs[-2], refs[-1]) if nc else None
        if nc:
            carry.ride(grid, riders, "start")
        x = x_ref[...]
        g = _dg(x, wg_ref[...], NT)
        u = _dg(x, wu_ref[...], NT)
        sg = _sigmoid(g)
        silu = g * sg
        s_up_ref[...] = (MACARON * silu).astype(BF16)
        s_gate_ref[...] = (MACARON * u * (sg * (1.0 + g * (1.0 - sg)))).astype(BF16)
        a_ref[...] = (silu * u).astype(BF16)
        if nc:
            carry.ride(grid, riders, "finish")

    act = _spec((None, tm, FS), lambda s, i: (s, i, 0))
    shape = jax.ShapeDtypeStruct((NSH, t, FS), BF16)
    outs = pl.pallas_call(
        body, name=name, grid=grid,
        in_specs=[_spec((tm, D), lambda s, i: (i, 0)),
                  _spec((None, None, FS, D), lambda s, i: (s, 0, 0, 0)),
                  _spec((None, None, FS, D), lambda s, i: (s, 1, 0, 0))] + (carry.in_specs if nc else []),
        out_specs=[act, act, act] + (carry.out_specs if nc else []),
        out_shape=[shape, shape, shape] + (carry.out_shape if nc else []),
        scratch_shapes=carry.scratch if nc else [],
        compiler_params=_params(("arbitrary", "arbitrary") if nc else ("parallel", "parallel")),
    )(hn, wgu, wgu, *(carry.groups if nc else []))
    return (outs[0], outs[1], outs[2], carry.place(outs[3:]) if nc else [])


def _ffn_down(name, a, wd, h, carry=None, norm=None):
    t = h.shape[0]
    tm = min(MM_TILE, t)
    return _mm(name, [(a, _spec((None, tm, FS), lambda i, k: (k, i, 0)),
                       wd, _spec((None, FS, D), lambda i, k: (k, 0, 0)))],
               grid=(t // tm, NSH), o_shape=(t, D), o_dtype=F32, o_spec=_spec((tm, D), lambda i, k: (i, 0)),
               dims=NN, kaxis=1, nk=NSH, acc_shape=(tm, D), res=(h, _spec((tm, D), lambda i, k: (i, 0))),
               scale=MACARON, carry=carry, norm=norm)


def _ffn_bwd_up(name, dhb, wd, s_up, s_gate, rider=None):
    t = dhb.shape[0]
    tm = min(MM_TILE, t)
    grid = (NSH, t // tm)
    nr = rider.n if rider is not None else 0

    def body(*refs):
        dh_ref, wd_ref, s_up_ref, s_gate_ref = refs[:4]
        dg_ref, du_ref = refs[4 + nr:6 + nr]
        riders = (refs[4:4 + nr], refs[6 + nr:6 + 2 * nr], refs[-2], refs[-1]) if nr else None
        if nr:
            rider.ride(grid, riders, "start")
        da = _dg(dh_ref[...], wd_ref[...], NT).astype(BF16)
        du_ref[...] = da * s_up_ref[...]
        dg_ref[...] = da * s_gate_ref[...]
        if nr:
            rider.ride(grid, riders, "finish")

    act = _spec((None, tm, FS), lambda s, i: (s, i, 0))
    shape = jax.ShapeDtypeStruct((NSH, t, FS), BF16)
    outs = pl.pallas_call(
        body, name=name, grid=grid,
        in_specs=[_spec((tm, D), lambda s, i: (i, 0)), _spec((None, FS, D), lambda s, i: (s, 0, 0)), act, act]
        + (rider.in_specs if nr else []),
        out_specs=[act, act] + (rider.out_specs if nr else []),
        out_shape=[shape, shape] + (rider.out_shape if nr else []),
        scratch_shapes=rider.scratch if nr else [],
        compiler_params=_params(("arbitrary", "arbitrary") if nr else ("parallel", "parallel")),
    )(dhb, wd, s_up, s_gate, *(rider.groups if nr else []))
    return outs[0], outs[1], (rider.place(outs[2:]) if nr else [])


def _wgrad(name, a, a_spec, b, b_spec, out_rows, out_cols, t, tt, group, slot, scale=None, carry=None):
    first = isinstance(group, int)
    rows = group if first else group.shape[1]
    return _mm(name, [(a, a_spec, b, b_spec)], grid=(NSH, t // tt),
               o_shape=(NSH, rows, out_cols), o_dtype=BF16,
               o_spec=_spec((None, out_rows, out_cols), lambda s, k: (s, slot, 0)),
               dims=TN, kaxis=1, nk=t // tt, acc_shape=(out_rows, out_cols), scale=scale,
               into=None if first else group, carry=carry)


def _wgrad_gate_up(name, dg, du, hn, group, gate_idx):
    t = hn.shape[0]
    first = isinstance(group, int)
    rows = group if first else group.shape[1]

    def body(dg_ref, du_ref, hn_ref, *refs):
        o_ref = refs[-1]
        x = hn_ref[...]
        o_ref[...] = jnp.concatenate([_dg(dg_ref[...], x, TN), _dg(du_ref[...], x, TN)], axis=0).astype(BF16)

    hid = _spec((None, t, FS), lambda s: (s, 0, 0))
    return pl.pallas_call(
        body, name=name, grid=(NSH,),
        in_specs=[hid, hid, _spec((t, D), lambda s: (0, 0))] + ([] if first else [ANY]),
        out_specs=_spec((None, 2 * FS, D), lambda s: (s, gate_idx // 2, 0)),
        out_shape=jax.ShapeDtypeStruct((NSH, rows, D), BF16),
        input_output_aliases={} if first else {3: 0},
        compiler_params=pltpu.CompilerParams(dimension_semantics=("parallel",), vmem_limit_bytes=VMEM_LIMIT_BIG),
    )(dg, du, hn, *([] if first else [group]))


def _ffn_backward(tag, dh, dhb, h_in, hn, s_up, s_gate, a, wgu, wd, gate_idx, up_idx, down_idx, small, norm_row,
                  grad_a, grad_b, rider_up=None, rider_dhn=None):
    t = dh.shape[0]
    tm = min(MM_TILE, t)
    tt = min(WGRAD_TILE, t)
    tok = _spec((tt, D), lambda s, k: (k, 0))
    hid = _spec((None, tt, FS), lambda s, k: (s, k, 0))
    grad_b = _wgrad(tag + "_dwd", a, hid, dhb, tok, FS, D, t, tt, grad_b, down_idx, scale=MACARON)
    rider = rider_up(grad_a, grad_b) if rider_up is not None else None
    dg, du, landed_up = _ffn_bwd_up(tag + "_bwd_up", dhb, wd, s_up, s_gate, rider)
    assert up_idx == gate_idx + 1 and gate_idx % 2 == 0
    grad_a = _wgrad_gate_up(tag + "_dwgu", dg, du, hn, grad_a, gate_idx)
    rider = rider_dhn(grad_a, grad_b, landed_up) if rider_dhn is not None else None
    act = _spec((None, tm, FS), lambda i, k: (k, i, 0))
    dh_in, dhb_in, d_gain, landed_dhn = _dhn_norm(
        tag + "_dhn", [(dg, act, wgu, _spec((None, None, FS, D), lambda i, k: (k, 0, 0, 0))),
                       (du, act, wgu, _spec((None, None, FS, D), lambda i, k: (k, 1, 0, 0)))],
        NN, h_in, small, norm_row, dh, rider)
    return dh_in, dhb_in, grad_a, grad_b, d_gain, landed_up, landed_dhn


def _conv_fwd(proj_a, conv_w):
    t = proj_a.shape[0]
    tm = min(ROW_TILE, t)
    hb = tm // CONV_HALO

    def body(ab_ref, ac_ref, ax_ref, acp_ref, axp_ref, w_ref, y_ref):
        i = pl.program_id(1)
        u = ac_ref[...] * ax_ref[...]
        up = jnp.where(i > 0, acp_ref[...] * axp_ref[...], 0.0)
        ext = jnp.concatenate([up, u], axis=0)
        u1 = pltpu.roll(ext, 1, 0)[CONV_HALO:]
        u2 = pltpu.roll(ext, 2, 0)[CONV_HALO:]
        w = w_ref[...]
        conv = w[0:1] * u2 + w[1:2] * u1 + w[2:3] * u
        y_ref[...] = (ab_ref[...] * conv).astype(BF16)

    def cur(off):
        return _spec((tm, LANE), lambda j, i: (i, off + j))

    def prev(off):
        return _spec((CONV_HALO, LANE), lambda j, i: (jnp.maximum(i * hb - 1, 0), off + j))

    return pl.pallas_call(
        body, name="conv_fwd", grid=(4, t // tm),
        in_specs=[cur(0), cur(4), cur(8), prev(4), prev(8),
                  _spec((None, None, 8, LANE), lambda j, i: (j, 0, 0, 0))],
        out_specs=_spec((tm, LANE), lambda j, i: (i, j)),
        out_shape=jax.ShapeDtypeStruct((t, 512), BF16),
        compiler_params=_params(("parallel", "parallel")),
    )(proj_a, proj_a, proj_a, proj_a, proj_a, conv_w)


def _conv_bwd(proj_a, conv_w, dy, rider=None):
    t = proj_a.shape[0]
    tm = min(ROW_TILE, t)
    hb = tm // CONV_HALO
    nt = t // tm
    grid = (4, nt)
    nr = rider.n if rider is not None else 0

    def body(*refs):
        ab_ref, ac_ref, ax_ref, dy_ref, acp_ref, axp_ref, abn_ref, dyn_ref, w_ref = refs[:9]
        dab_ref, dac_ref, dax_ref, dw_ref = refs[9 + nr:13 + nr]
        acc_ref = refs[13 + 2 * nr]
        riders = (refs[9:9 + nr], refs[13 + nr:13 + 2 * nr], refs[-2], refs[-1]) if nr else None
        if nr:
            rider.ride(grid, riders, "start")
        i = pl.program_id(1)
        ab, ac, ax = ab_ref[...], ac_ref[...], ax_ref[...]
        u = ac * ax
        up = jnp.where(i > 0, acp_ref[...] * axp_ref[...], 0.0)
        ext = jnp.concatenate([up, u], axis=0)
        u1 = pltpu.roll(ext, 1, 0)[CONV_HALO:]
        u2 = pltpu.roll(ext, 2, 0)[CONV_HALO:]
        w = w_ref[...]
        conv = w[0:1] * u2 + w[1:2] * u1 + w[2:3] * u
        dy_v = dy_ref[...]
        dab_ref[...] = (dy_v * conv).astype(BF16)
        dc = dy_v * ab
        dcn = jnp.where(i < nt - 1, dyn_ref[...] * abn_ref[...], 0.0)
        extn = jnp.concatenate([dc, dcn], axis=0)
        n = tm + CONV_HALO
        dc1 = pltpu.roll(extn, n - 1, 0)[:tm]
        dc2 = pltpu.roll(extn, n - 2, 0)[:tm]
        du = w[2:3] * dc + w[1:2] * dc1 + w[0:1] * dc2
        dac_ref[...] = (du * ax).astype(BF16)
        dax_ref[...] = (du * ac).astype(BF16)
        rid = lax.broadcasted_iota(jnp.int32, (8, LANE), 0)
        part = jnp.where(rid == 0, jnp.sum(dc * u2, axis=0, keepdims=True),
                         jnp.where(rid == 1, jnp.sum(dc * u1, axis=0, keepdims=True),
                                   jnp.where(rid == 2, jnp.sum(dc * u, axis=0, keepdims=True), 0.0)))

        @pl.when(i == 0)
        def _():
            acc_ref[...] = part

        @pl.when(i > 0)
        def _():
            acc_ref[...] += part

        @pl.when(i == nt - 1)
        def _():
            dw_ref[...] = acc_ref[...]

        if nr:
            rider.ride(grid, riders, "finish")

    def cur(off):
        return _spec((tm, LANE), lambda j, i: (i, off + j))

    def prev(off):
        return _spec((CONV_HALO, LANE), lambda j, i: (jnp.maximum(i * hb - 1, 0), off + j))

    def nxt(off):
        return _spec((CONV_HALO, LANE), lambda j, i: (jnp.minimum((i + 1) * hb, nt * hb - 1), off + j))

    outs = pl.pallas_call(
        body, name="conv_bwd", grid=grid,
        in_specs=[cur(0), cur(4), cur(8), cur(0), prev(4), prev(8), nxt(0), nxt(0),
                  _spec((None, None, 8, LANE), lambda j, i: (j, 0, 0, 0))] + (rider.in_specs if nr else []),
        out_specs=[_spec((tm, LANE), lambda j, i: (i, j)), _spec((tm, LANE), lambda j, i: (i, j)),
                   _spec((tm, LANE), lambda j, i: (i, j)), _spec((None, 8, LANE), lambda j, i: (j, 0, 0))]
        + (rider.out_specs if nr else []),
        out_shape=[jax.ShapeDtypeStruct((t, 512), BF16), jax.ShapeDtypeStruct((t, 512), BF16),
                   jax.ShapeDtypeStruct((t, 512), BF16), jax.ShapeDtypeStruct((4, 8, LANE), F32)]
        + (rider.out_shape if nr else []),
        scratch_shapes=[pltpu.VMEM((8, LANE), F32)] + (rider.scratch if nr else []),
        compiler_params=_params(("arbitrary", "arbitrary") if nr else ("parallel", "arbitrary")),
    )(proj_a, proj_a, proj_a, dy, proj_a, proj_a, proj_a, dy, conv_w, *(rider.groups if nr else []))
    return outs[0], outs[1], outs[2], outs[3], (rider.place(outs[4:]) if nr else [])


def _log_sigmoid(z):
    return jnp.minimum(z, 0.0) - jnp.log(1.0 + jnp.exp(-jnp.abs(z)))


def _sb_masks():
    row = lax.broadcasted_iota(jnp.int32, (SBQ, SBQ), 0)
    col = lax.broadcasted_iota(jnp.int32, (SBQ, SBQ), 1)
    return row, col


def _ones_where(mask):
    return jnp.where(mask, 1.0, 0.0).astype(BF16)


def _head_mask(head):
    lane = lax.broadcasted_iota(jnp.int32, (1, LANE), 1)
    return lane >= 64 if head else lane < 64


def _sb_fwd(proj_b, carry=None):
    t = proj_b.shape[0]
    nq = t // SBQ
    scale = 1.0 / math.sqrt(64.0)

    npair = SB_PAIRS_FWD
    wide = npair * LANE
    ngrp = 4 // npair
    chains = [(p, head) for p in range(npair) for head in range(2)]

    grid = (ngrp, nq)
    nc = carry.n if carry is not None else 0

    def body(*refs):
        q_ref, k_ref, v_ref = refs[:3]
        y_ref, l_ref, n_ref = refs[3 + nc:6 + nc]
        riders = (refs[3:3 + nc], refs[6 + nc:6 + 2 * nc], refs[-2], refs[-1]) if nc else None
        if nc:
            carry.ride(grid, riders, "start")
        grp = pl.program_id(0)
        qi = pl.program_id(1)
        row, col = _sb_masks()
        m_suffix = _ones_where(row > col)
        rows = len(chains) * SBQ
        strict = (lax.broadcasted_iota(jnp.int32, (rows, SBQ), 1)
                  < (lax.broadcasted_iota(jnp.int32, (rows, SBQ), 0) & (SBQ - 1)))
        q_pair = []
        for p in range(npair):
            q_all = q_ref[:, p * LANE:(p + 1) * LANE]
            q_pair.append(jnp.concatenate([jnp.where(_head_mask(head), q_all, jnp.zeros_like(q_all)) for head in range(2)],
                                          axis=0))

        def block(kb, state, diag):
            run, acc = state
            start = pl.multiple_of(kb * SBQ, SBQ)
            z = jnp.concatenate([_dg(q_pair[p], k_ref[pl.ds(start, SBQ), p * LANE:(p + 1) * LANE], NT)
                                 for p in range(npair)], axis=0) * scale
            lb = _log_sigmoid(z)
            lk = lb - z
            if diag:
                lk = jnp.where(strict, lk, 0.0)
            hi, lo = _split(lk)
            sums = _dg(jnp.concatenate([hi, lo], axis=0), m_suffix, NN)
            w = jnp.exp(lb + (run + sums[:rows] + sums[rows:]))
            if diag:
                w = jnp.where(strict, w, 0.0)
            wb = w.astype(BF16)
            acc = acc + jnp.concatenate(
                [_dg(wb[2 * p * SBQ:2 * (p + 1) * SBQ], v_ref[pl.ds(start, SBQ), p * LANE:(p + 1) * LANE], NN)
                 for p in range(npair)], axis=0)
            run = run + jnp.sum(hi.astype(F32) + lo.astype(F32), axis=1, keepdims=True)
            return run, acc

        state = block(qi, (jnp.zeros((rows, 1), F32), jnp.zeros((rows, LANE), F32)), True)

        def live(c):
            return jnp.logical_and(c[0] < qi, jnp.max(c[1][0]) > SB_DEAD)

        def step(c):
            return c[0] + 1, block(qi - 1 - c[0], c[1], False)

        count, (run, acc) = lax.while_loop(live, step, (jnp.int32(0), state))
        n_ref[grp * nq + qi] = count.astype(F32)
        hm = _head_mask(0)
        for p in range(npair):
            lo_rows, hi_rows = slice(2 * p * SBQ, (2 * p + 1) * SBQ), slice((2 * p + 1) * SBQ, (2 * p + 2) * SBQ)
            y_ref[:, p * LANE:(p + 1) * LANE] = jnp.where(hm, acc[lo_rows], acc[hi_rows]).astype(BF16)
            l_ref[p] = jnp.where(hm, run[lo_rows], run[hi_rows])
        if nc:
            carry.ride(grid, riders, "finish")

    outs = pl.pallas_call(
        body, name="sb_fwd", grid=grid,
        in_specs=[_spec((SBQ, wide), lambda g, i: (i, g)),
                  _spec((t, wide), lambda g, i: (0, ngrp + g)),
                  _spec((t, wide), lambda g, i: (0, 2 * ngrp + g))] + (carry.in_specs if nc else []),
        out_specs=[_spec((SBQ, wide), lambda g, i: (i, g)), _spec((npair, SBQ, LANE), lambda g, i: (g, i, 0)),
                   pl.BlockSpec(memory_space=pltpu.SMEM)] + (carry.out_specs if nc else []),
        out_shape=[jax.ShapeDtypeStruct((t, 512), BF16), jax.ShapeDtypeStruct((4, t, LANE), F32),
                   jax.ShapeDtypeStruct((ngrp * nq,), F32)] + (carry.out_shape if nc else []),
        scratch_shapes=carry.scratch if nc else [],
        compiler_params=_params(("arbitrary", "arbitrary")),
    )(proj_b, proj_b, proj_b, *(carry.groups if nc else []))
    return outs[0], outs[1], outs[2], (carry.place(outs[3:]) if nc else [])


def _sb_bwd(proj_b, dy, ltot, nblk, rider=None):
    t = proj_b.shape[0]
    nq = t // SBQ
    scale = 1.0 / math.sqrt(64.0)

    npair = SB_PAIRS_BWD
    wide = npair * LANE
    ngrp = 4 // npair
    chains = [(p, head) for p in range(npair) for head in range(2)]
    grid = (ngrp, nq)
    nr = rider.n if rider is not None else 0
    per_count = SB_PAIRS_FWD // SB_PAIRS_BWD

    def body(*refs):
        q_ref, k_ref, v_ref, dy_ref, l_ref, n_ref = refs[:6]
        dq_ref, dk_ref, dv_ref = refs[6 + nr:9 + nr]
        dk_acc, dv_acc = refs[9 + 2 * nr:11 + 2 * nr]
        riders = (refs[6:6 + nr], refs[9 + nr:9 + 2 * nr], refs[-2], refs[-1]) if nr else None
        if nr:
            rider.ride(grid, riders, "start")
        grp = pl.program_id(0)
        qi = pl.program_id(1)

        @pl.when(qi == 0)
        def _():
            dk_acc[...] = jnp.zeros_like(dk_acc)
            dv_acc[...] = jnp.zeros_like(dv_acc)

        row, col = _sb_masks()
        m_prefix = _ones_where(row <= col)
        m_before = _ones_where(row < col)
        rows = len(chains) * SBQ
        strict = (lax.broadcasted_iota(jnp.int32, (rows, SBQ), 1)
                  < (lax.broadcasted_iota(jnp.int32, (rows, SBQ), 0) & (SBQ - 1)))
        q_pair, do_pair, ltot = [], [], []
        for p in range(npair):
            pl_ = slice(p * LANE, (p + 1) * LANE)
            q_all = q_ref[:, pl_]
            do_all = dy_ref[:, pl_].astype(BF16)
            q_pair.append(jnp.concatenate([jnp.where(_head_mask(h), q_all, jnp.zeros_like(q_all)) for h in range(2)], axis=0))
            do_pair.append(jnp.concatenate([jnp.where(_head_mask(h), do_all, jnp.zeros_like(do_all)) for h in range(2)], axis=0))
            ltot += [l_ref[p][:, h * 64:h * 64 + 1] for h in range(2)]
        ltot = jnp.concatenate(ltot, axis=0)

        def pair_rows(a, p):
            return a[2 * p * SBQ:2 * (p + 1) * SBQ]

        def block(kb, state, diag):
            seen, dseen, dq = state
            start = pl.multiple_of(kb * SBQ, SBQ)
            kk = [k_ref[pl.ds(start, SBQ), p * LANE:(p + 1) * LANE] for p in range(npair)]
            vv = [v_ref[pl.ds(start, SBQ), p * LANE:(p + 1) * LANE] for p in range(npair)]
            z = jnp.concatenate([_dg(q_pair[p], kk[p], NT) for p in range(npair)], axis=0) * scale
            lb = _log_sigmoid(z)
            lk = lb - z
            if diag:
                lk = jnp.where(strict, lk, 0.0)
            hi, lo = _split(lk)
            sums = _dg(jnp.concatenate([hi, lo], axis=0), m_prefix, NN)
            w = jnp.exp(lb + ((ltot - seen) - (sums[:rows] + sums[rows:])))
            if diag:
                w = jnp.where(strict, w, 0.0)
            wb = w.astype(BF16)
            da = w * jnp.concatenate([_dg(do_pair[p], vv[p], NT) for p in range(npair)], axis=0)
            dah, dal = _split(da)
            dsums = _dg(jnp.concatenate([dah, dal], axis=0), m_before, NN)
            sig = jnp.exp(lb)
            dz = (da * (1.0 - sig) - (dseen + dsums[:rows] + dsums[rows:]) * sig) * scale
            if diag:
                dz = jnp.where(strict, dz, 0.0)
            dzb = dz.astype(BF16)
            for p in range(npair):
                pl_ = slice(p * LANE, (p + 1) * LANE)
                dv_acc[pl.ds(start, SBQ), pl_] += _dg(pair_rows(wb, p), do_pair[p], TN)
                dk_acc[pl.ds(start, SBQ), pl_] += _dg(pair_rows(dzb, p), q_pair[p], TN)
            dq = dq + jnp.concatenate([_dg(pair_rows(dzb, p), kk[p], NN) for p in range(npair)], axis=0)
            seen = seen + jnp.sum(hi.astype(F32) + lo.astype(F32), axis=1, keepdims=True)
            dseen = dseen + jnp.sum(da, axis=1, keepdims=True)
            return seen, dseen, dq

        zero = (jnp.zeros((rows, 1), F32), jnp.zeros((rows, 1), F32), jnp.zeros((rows, LANE), F32))
        first = qi - n_ref[(grp // per_count) * nq + qi].astype(jnp.int32)
        state = lax.fori_loop(first, qi, lambda kb, c: block(kb, c, False), zero)
        _, _, dq = block(qi, state, True)
        for p in range(npair):
            dq_ref[:, p * LANE:(p + 1) * LANE] = jnp.where(
                _head_mask(0), dq[2 * p * SBQ:(2 * p + 1) * SBQ], dq[(2 * p + 1) * SBQ:(2 * p + 2) * SBQ]).astype(BF16)

        @pl.when(qi == nq - 1)
        def _():
            dk_ref[...] = dk_acc[...].astype(BF16)
            dv_ref[...] = dv_acc[...].astype(BF16)

        if nr:
            rider.ride(grid, riders, "finish")

    full = jax.ShapeDtypeStruct((t, 512), BF16)
    outs = pl.pallas_call(
        body, name="sb_bwd", grid=grid,
        in_specs=[_spec((SBQ, wide), lambda g, i: (i, g)),
                  _spec((t, wide), lambda g, i: (0, ngrp + g)),
                  _spec((t, wide), lambda g, i: (0, 2 * ngrp + g)),
                  _spec((SBQ, wide), lambda g, i: (i, ngrp + g)),
                  _spec((npair, SBQ, LANE), lambda g, i: (g, i, 0)),
                  pl.BlockSpec(memory_space=pltpu.SMEM)] + (rider.in_specs if nr else []),
        out_specs=[_spec((SBQ, wide), lambda g, i: (i, g)),
                   _spec((t, wide), lambda g, i: (0, g)), _spec((t, wide), lambda g, i: (0, g))]
        + (rider.out_specs if nr else []),
        out_shape=[full, full, full] + (rider.out_shape if nr else []),
        scratch_shapes=[pltpu.VMEM((t, wide), F32), pltpu.VMEM((t, wide), F32)] + (rider.scratch if nr else []),
        compiler_params=pltpu.CompilerParams(dimension_semantics=("arbitrary", "arbitrary"),
                                             vmem_limit_bytes=VMEM_LIMIT_BIG),
    )(proj_b, proj_b, proj_b, dy, ltot, nblk, *(rider.groups if nr else []))
    return outs[0], outs[1], outs[2], (rider.place(outs[3:]) if nr else [])


def _hgrn_gates(qr, fr, c0, c1):
    mx = jnp.maximum(c0, c1)
    e0, e1 = jnp.exp(c0 - mx), jnp.exp(c1 - mx)
    lb = e1 / (e0 + e1)
    sx = _sigmoid(fr)
    f = lb + (1.0 - lb) * sx
    k = (1.0 - lb) * (1.0 - sx)
    sq = _sigmoid(qr)
    return lb, sx, f, k, sq, qr * sq


def _chunk_sums(mask, x):
    n = x.shape[1]
    hi, lo = _split(x)
    both = _dg(_ones_where(mask), jnp.concatenate([hi, lo], axis=1), NN)
    return both[:, :n] + both[:, n:]


def _chunk_masks():
    row = lax.broadcasted_iota(jnp.int32, (CHUNK, CHUNK), 0)
    col = lax.broadcasted_iota(jnp.int32, (CHUNK, CHUNK), 1)
    return col <= row, col >= row


def _hgrn_fwd(proj_c, small, carry=None):
    t = proj_c.shape[1]
    nc = t // CHUNK
    nh = D // HD

    hps = HGRN_HPS
    wide = hps * HD
    grid = (nh // hps, nc)
    nr = carry.n if carry is not None else 0

    def body(*refs):
        p_ref, c0_ref, c1_ref, gam_ref = refs[:4]
        o_ref, y_ref, sst_ref = refs[4 + nr:7 + nr]
        st_ref = refs[7 + 2 * nr]
        riders = (refs[4:4 + nr], refs[7 + nr:7 + 2 * nr], refs[-2], refs[-1]) if nr else None
        if nr:
            carry.ride(grid, riders, "start")
        c = pl.program_id(1)

        @pl.when(c == 0)
        def _():
            st_ref[...] = jnp.zeros_like(st_ref)

        _, _, f_all, k_all, _, q_all = _hgrn_gates(p_ref[0], p_ref[1], c0_ref[...], c1_ref[...])
        tril, _ = _chunk_masks()
        b_all = _chunk_sums(tril, jnp.log(f_all))
        for j in range(hps):
            ln = slice(j * HD, (j + 1) * HD)
            st0 = st_ref[j]
            sst_ref[j] = st0
            v, g = p_ref[2, :, ln], p_ref[3, :, ln]
            q, k, b = q_all[:, ln], k_all[:, ln], b_all[:, ln]
            bm = b[CHUNK // 2 - 1:CHUNK // 2]
            bl = b[CHUNK - 1:CHUNK]
            qd = q * jnp.exp(b)
            qt = q * jnp.exp(b - bm)
            kt = k * jnp.exp(bm - b)
            kl = k * jnp.exp(bl - b)
            vb = v.astype(BF16)
            att = jnp.where(tril, _dot3(qt, kt, NT), 0.0)
            o = _dg(qd.astype(BF16), st0.astype(BF16), NT) + _dg(att.astype(BF16), vb, NN)
            st_ref[j] = st0 * jnp.exp(bl) + _dg(vb, kl.astype(BF16), TN)
            o_ref[:, ln] = o
            r = lax.rsqrt(jnp.mean(o * o, axis=-1, keepdims=True) + RMS_EPS)
            y_ref[:, ln] = (o * r * gam_ref[...] * (g * _sigmoid(g))).astype(BF16)
        if nr:
            carry.ride(grid, riders, "finish")

    outs = pl.pallas_call(
        body, name="hgrn_fwd", grid=grid,
        in_specs=[_spec((4, CHUNK, wide), lambda h, c: (0, c, h)),
                  _spec((None, 1, wide), lambda h, c: (R_CLB, 0, h)),
                  _spec((None, 1, wide), lambda h, c: (R_CLB + 1, 0, h)),
                  _spec((None, 1, HD), lambda h, c: (R_GAM, 0, 0))] + (carry.in_specs if nr else []),
        out_specs=[_spec((CHUNK, wide), lambda h, c: (c, h)), _spec((CHUNK, wide), lambda h, c: (c, h)),
                   _spec((None, hps, HD, HD), lambda h, c: (c, h, 0, 0))] + (carry.out_specs if nr else []),
        out_shape=[jax.ShapeDtypeStruct((t, D), F32), jax.ShapeDtypeStruct((t, D), BF16),
                   jax.ShapeDtypeStruct((nc, nh, HD, HD), F32)] + (carry.out_shape if nr else []),
        scratch_shapes=[pltpu.VMEM((hps, HD, HD), F32)] + (carry.scratch if nr else []),
        compiler_params=_params(("arbitrary", "arbitrary")),
    )(proj_c, small, small, small, *(carry.groups if nr else []))
    return outs[0], outs[1], outs[2], (carry.place(outs[3:]) if nr else [])


def _hgrn_bwd(proj_c, small, o, sst, dyc, rider=None):
    t = proj_c.shape[1]
    nc = t // CHUNK
    nh = D // HD

    hps = HGRN_HPS
    wide = hps * HD
    ng = nh // hps
    grid = (ng, nc)
    nr = rider.n if rider is not None else 0

    def body(*refs):
        p_ref, c0_ref, c1_ref, gam_ref, o_ref, sst_ref, dy_ref = refs[:7]
        dp_ref, dclb_ref, dgam_ref = refs[7 + nr:10 + nr]
        dst_ref, dlb_acc, dgam_acc = refs[10 + 2 * nr:13 + 2 * nr]
        riders = (refs[7:7 + nr], refs[10 + nr:10 + 2 * nr], refs[-2], refs[-1]) if nr else None
        if nr:
            rider.ride(grid, riders, "start")
        group = pl.program_id(0)
        step = pl.program_id(1)

        @pl.when(step == 0)
        def _():
            dst_ref[...] = jnp.zeros_like(dst_ref)
            dlb_acc[...] = jnp.zeros_like(dlb_acc)

        @pl.when((step == 0) & (group == 0))
        def _():
            dgam_acc[...] = jnp.zeros_like(dgam_acc)

        gam = gam_ref[...]
        qr_all = p_ref[0]
        lb_all, sx_all, f_all, k_all, sq_all, q_all = _hgrn_gates(qr_all, p_ref[1], c0_ref[...], c1_ref[...])
        tril, triu = _chunk_masks()
        b_all = _chunk_sums(tril, jnp.log(f_all))
        dq_parts, dk_parts, db_parts, dgam_parts = [], [], [], []
        for j in range(hps):
            ln = slice(j * HD, (j + 1) * HD)
            st0 = sst_ref[j]
            dst1 = dst_ref[j]
            v, g = p_ref[2, :, ln], p_ref[3, :, ln]
            q, k, b = q_all[:, ln], k_all[:, ln], b_all[:, ln]
            bm = b[CHUNK // 2 - 1:CHUNK // 2]
            bl = b[CHUNK - 1:CHUNK]
            eb = jnp.exp(b)
            e_qt = jnp.exp(b - bm)
            e_kt = jnp.exp(bm - b)
            e_kl = jnp.exp(bl - b)
            e_bl = jnp.exp(bl)
            qd, qt, kt, kl = q * eb, q * e_qt, k * e_kt, k * e_kl
            ov = o_ref[:, ln]
            r = lax.rsqrt(jnp.mean(ov * ov, axis=-1, keepdims=True) + RMS_EPS)
            oh = ov * r
            sg = _sigmoid(g)
            dy = dy_ref[:, ln]
            dp_ref[3, :, ln] = (dy * oh * gam * (sg * (1.0 + g * (1.0 - sg)))).astype(BF16)
            dyv = dy * (g * sg)
            gdy = dyv * gam
            do = (gdy - oh * jnp.mean(gdy * oh, axis=-1, keepdims=True)) * r
            dob, vb = do.astype(BF16), v.astype(BF16)
            st0b, dst1b = st0.astype(BF16), dst1.astype(BF16)
            st1 = st0 * e_bl + _dg(vb, kl.astype(BF16), TN)
            att = jnp.where(tril, _dot3(qt, kt, NT), 0.0)
            datt = jnp.where(tril, _dg(dob, vb, NT), 0.0)
            dv = _dg(att.astype(BF16), dob, TN) + _dg(kl.astype(BF16), dst1b, NT)
            dq = _dot3(datt, kt, NN) * e_qt + _dg(dob, st0b, NN) * eb
            dk = _dot3(datt, qt, TN) * e_kt + _dg(vb, dst1b, NN) * e_kl
            db = q * dq - k * dk
            last = lax.broadcasted_iota(jnp.int32, (CHUNK, 1), 0) == CHUNK - 1
            db = db + jnp.where(last, jnp.sum(dst1 * st1, axis=0, keepdims=True), 0.0)
            dst_ref[j] = dst1 * e_bl + _dg(dob, qd.astype(BF16), TN)
            dp_ref[2, :, ln] = dv.astype(BF16)
            dq_parts.append(dq)
            dk_parts.append(dk)
            db_parts.append(db)
            dgam_parts.append(jnp.sum(dyv * oh, axis=0, keepdims=True))

        dq_all, dk_all = jnp.concatenate(dq_parts, axis=1), jnp.concatenate(dk_parts, axis=1)
        dlf = _chunk_sums(triu, jnp.concatenate(db_parts, axis=1))
        dp_ref[0] = (dq_all * (sq_all * (1.0 + qr_all * (1.0 - sq_all)))).astype(BF16)
        tmp = dlf / f_all - dk_all
        dp_ref[1] = (tmp * (1.0 - lb_all) * sx_all * (1.0 - sx_all)).astype(BF16)
        dlb_acc[...] += jnp.sum((1.0 - sx_all) * tmp, axis=0, keepdims=True)
        dgam_acc[...] += functools.reduce(lambda a, b: a + b, dgam_parts)

        @pl.when(step == nc - 1)
        def _():
            d1 = dlb_acc[...] * lb_all * (1.0 - lb_all)
            dclb_ref[...] = jnp.where(lax.broadcasted_iota(jnp.int32, (2, wide), 0) == 0, -d1, d1)

        @pl.when((step == nc - 1) & (group == ng - 1))
        def _():
            dgam_ref[...] = dgam_acc[...]

        if nr:
            rider.ride(grid, riders, "finish")

    rev = lambda h, s: (nc - 1 - s, h)
    outs = pl.pallas_call(
        body, name="hgrn_bwd", grid=grid,
        in_specs=[_spec((4, CHUNK, wide), lambda h, s: (0, nc - 1 - s, h)),
                  _spec((None, 1, wide), lambda h, s: (R_CLB, 0, h)),
                  _spec((None, 1, wide), lambda h, s: (R_CLB + 1, 0, h)),
                  _spec((None, 1, HD), lambda h, s: (R_GAM, 0, 0)),
                  _spec((CHUNK, wide), rev),
                  _spec((None, hps, HD, HD), lambda h, s: (nc - 1 - s, h, 0, 0)),
                  _spec((CHUNK, wide), rev)] + (rider.in_specs if nr else []),
        out_specs=[_spec((4, CHUNK, wide), lambda h, s: (0, nc - 1 - s, h)),
                   _spec((2, wide), lambda h, s: (0, h)),
                   _spec((1, HD), lambda h, s: (0, 0))] + (rider.out_specs if nr else []),
        out_shape=[jax.ShapeDtypeStruct((4, t, D), BF16), jax.ShapeDtypeStruct((2, D), F32),
                   jax.ShapeDtypeStruct((1, HD), F32)] + (rider.out_shape if nr else []),
        scratch_shapes=[pltpu.VMEM((hps, HD, HD), F32), pltpu.VMEM((1, wide), F32), pltpu.VMEM((1, HD), F32)]
        + (rider.scratch if nr else []),
        compiler_params=_params(("arbitrary", "arbitrary")),
    )(proj_c, small, small, small, o, sst, dyc, *(rider.groups if nr else []))
    return outs[0], outs[1], outs[2], (rider.place(outs[3:]) if nr else [])


def _adamw(name, w, grads, m, v):
    nl = len(grads)
    rows, cols = grads[0].shape
    br = rows
    for cand in (512, 352, 256):
        if rows % cand == 0:
            br = cand
            break
    nb = rows // br
    c1 = 1.0 - ADAM_B1 ** ADAM_STEP
    c2 = 1.0 - ADAM_B2 ** ADAM_STEP

    def body(w_ref, m_ref, v_ref, *refs):
        g_refs, (d_ref, mo_ref, vo_ref, go_ref) = refs[:nl], refs[nl:]
        layer = pl.program_id(0)
        gv = g_refs[0][...]
        for k in range(1, nl):
            gv = jnp.where(layer == k, g_refs[k][...], gv)
        mn = ADAM_B1 * m_ref[...] + (1.0 - ADAM_B1) * gv
        vn = ADAM_B2 * v_ref[...] + (1.0 - ADAM_B2) * (gv * gv)
        mo_ref[...] = mn
        vo_ref[...] = vn
        go_ref[...] = gv
        d_ref[...] = -ADAM_LR * ((mn / c1) / (jnp.sqrt(vn / c2) + ADAM_EPS) + ADAM_WD * w_ref[...])

    blk = _spec((br, cols), lambda l, i: (l * nb + i, 0))
    g_specs = [_spec((br, cols), lambda l, i, k=k: (jnp.where(l == k, i, 0), 0)) for k in range(nl)]
    shape = jax.ShapeDtypeStruct((nl * rows, cols), F32)
    return pl.pallas_call(
        body, name=name, grid=(nl, nb), in_specs=[blk] * 3 + g_specs, out_specs=[blk] * 4, out_shape=[shape] * 4,
        compiler_params=_params(("arbitrary", "arbitrary")),
    )(w, m, v, *grads)


def _place():
    x, y, c = lax.axis_index("x"), lax.axis_index("y"), lax.axis_index("c")
    chips = [(1 - x, y), (x, 1 - y), (1 - x, 1 - y)]
    return x, y, c, chips


class _Rider:
    n = 0
    relay = None

    def ride(self, grid, refs, when):
        if not self.n:
            return
        ids = [pl.program_id(a) for a in range(len(grid))]
        first = functools.reduce(jnp.logical_and, [i == 0 for i in ids])
        last = functools.reduce(jnp.logical_and, [i == g - 1 for i, g in zip(ids, grid)])
        phases = [(first, self.start), (last, self.relay)] if when == "start" else [(last, self.finish)]
        for cond, phase in phases:
            if phase is not None:
                pl.when(cond)(functools.partial(phase, *refs))


class _Gather(_Rider):
    PER_GROUP = 7

    def __init__(self, groups):
        self.groups = list(groups)
        self.n = len(self.groups)
        self.in_specs = [ANY] * self.n
        self.out_specs = [ANY] * self.n
        self.out_shape = [jax.ShapeDtypeStruct((NSH,) + g.shape, g.dtype) for g in self.groups]
        sems = pltpu.SemaphoreType.DMA((self.PER_GROUP * self.n,))
        self.scratch = [sems, sems] if self.n else []

    def _copies(self, ins, outs, send, recv):
        x, y, c, chips = _place()
        sibling = (x, y, 1 - c)

        def half(gi, chip, hc):
            return outs[gi].at[2 * chip[0] + chip[1], hc]

        def copy(gi, k, src, dst, to):
            sem = self.PER_GROUP * gi + k
            return pltpu.make_async_remote_copy(src_ref=src, dst_ref=dst, send_sem=send.at[sem], recv_sem=recv.at[sem],
                                                device_id=to, device_id_type=MESH)

        pairs = [(gi, j, chip) for gi in range(self.n) for j, chip in enumerate(chips)]
        first = [copy(gi, j, ins[gi].at[c], half(gi, (x, y), c), (*chip, c)) for gi, j, chip in pairs]
        first += [copy(gi, 6, ins[gi], outs[gi].at[2 * x + y], sibling) for gi in range(self.n)]
        landed = [copy(gi, j, half(gi, chip, c), half(gi, chip, c), sibling) for gi, j, chip in pairs]
        relay = [copy(gi, 3 + j, half(gi, chip, c), half(gi, chip, c), sibling) for gi, j, chip in pairs]
        relayed = [copy(gi, 3 + j, half(gi, chip, 1 - c), half(gi, chip, 1 - c), sibling) for gi, j, chip in pairs]
        relayed += [copy(gi, 6, ins[gi], outs[gi].at[2 * x + y], sibling) for gi in range(self.n)]
        return first, landed, relay, relayed

    def start(self, ins, outs, send, recv):
        for cp in self._copies(ins, outs, send, recv)[0]:
            cp.start()

    def relay(self, ins, outs, send, recv):
        _, landed, relay, _ = self._copies(ins, outs, send, recv)
        for arrived, onward in zip(landed, relay):
            arrived.wait_recv()
            onward.start()

    def finish(self, ins, outs, send, recv):
        first, _, relay, relayed = self._copies(ins, outs, send, recv)
        for cp in relayed:
            cp.wait_recv()
        for cp in first + relay:
            cp.wait_send()

    def place(self, outs):
        return list(outs)


def _alone(name, rider):
    n = rider.n

    def body(*refs):
        parts = (refs[:n], refs[n:2 * n], refs[2 * n], refs[2 * n + 1])
        rider.start(*parts)
        if rider.relay is not None:
            rider.relay(*parts)
        rider.finish(*parts)

    outs = pl.pallas_call(
        body, name=name, in_specs=rider.in_specs, out_specs=rider.out_specs, out_shape=rider.out_shape,
        scratch_shapes=rider.scratch, compiler_params=pltpu.CompilerParams(has_side_effects=True),
    )(*rider.groups)
    return rider.place(outs)


class _Swap(_Rider):
    def __init__(self, slots):
        self.slots = list(slots)
        self.groups = [buf for buf, _, _ in self.slots]
        self.n = len(self.slots)
        self.in_specs = [ANY] * self.n
        self.out_specs = [ANY] * self.n
        self.out_shape = [jax.ShapeDtypeStruct((NSH, rows // 2, buf.shape[2]), buf.dtype) for buf, _, rows in self.slots]
        self.scratch = [pltpu.SemaphoreType.DMA((self.n,)), pltpu.SemaphoreType.DMA((self.n,))]

    def _copies(self, ins, outs, send, recv):
        x, y, c, _ = _place()
        cps = []
        for i, (_, row0, rows) in enumerate(self.slots):
            half = rows // 2
            src = ins[i].at[:, pl.ds(pl.multiple_of(row0 + (1 - c) * half, 16), half)]
            cps.append(pltpu.make_async_remote_copy(src_ref=src, dst_ref=outs[i], send_sem=send.at[i],
                                                    recv_sem=recv.at[i], device_id=(x, y, 1 - c), device_id_type=MESH))
        return cps

    def start(self, ins, outs, send, recv):
        for cp in self._copies(ins, outs, send, recv):
            cp.start()

    def finish(self, ins, outs, send, recv):
        for cp in self._copies(ins, outs, send, recv):
            cp.wait()

    def place(self, outs):
        return list(outs)


class _Share(_Rider):
    def __init__(self, arrays):
        self.groups = list(arrays)
        self.n = len(self.groups)
        self.in_specs = [ANY] * self.n
        self.out_specs = [ANY] * self.n
        self.out_shape = [jax.ShapeDtypeStruct((2,) + g.shape, g.dtype) for g in self.groups]
        self.scratch = [pltpu.SemaphoreType.DMA((self.n,)), pltpu.SemaphoreType.DMA((self.n,))]

    def _copies(self, ins, outs, send, recv, half):
        x, y, c, _ = _place()
        return [pltpu.make_async_remote_copy(src_ref=ins[gi], dst_ref=outs[gi].at[c if half == "mine" else 1 - c],
                                             send_sem=send.at[gi], recv_sem=recv.at[gi], device_id=(x, y, 1 - c),
                                             device_id_type=MESH) for gi in range(self.n)]

    def start(self, ins, outs, send, recv):
        for cp in self._copies(ins, outs, send, recv, "mine"):
            cp.start()

    def finish(self, ins, outs, send, recv):
        for cp in self._copies(ins, outs, send, recv, "theirs"):
            cp.wait_recv()
        for cp in self._copies(ins, outs, send, recv, "mine"):
            cp.wait_send()

    def place(self, outs):
        c = lax.axis_index("c")
        return [lax.dynamic_update_index_in_dim(o, g, c, 0) for o, g in zip(outs, self.groups)]


class _Send(_Rider):
    def __init__(self, arrays):
        self.groups = list(arrays)
        self.n = len(self.groups)
        self.in_specs = [ANY] * self.n
        self.out_specs = [ANY] * self.n
        self.out_shape = [jax.ShapeDtypeStruct((3,) + g.shape[1:], g.dtype) for g in self.groups]
        self.scratch = [pltpu.SemaphoreType.DMA((3 * self.n,)), pltpu.SemaphoreType.DMA((3 * self.n,))]

    def _copies(self, ins, outs, send, recv):
        x, y, c, chips = _place()
        return [pltpu.make_async_remote_copy(src_ref=ins[gi].at[2 * chip[0] + chip[1]], dst_ref=outs[gi].at[j],
                                             send_sem=send.at[3 * gi + j], recv_sem=recv.at[3 * gi + j],
                                             device_id=(*chip, c), device_id_type=MESH)
                for gi in range(self.n) for j, chip in enumerate(chips)]

    def start(self, ins, outs, send, recv):
        for cp in self._copies(ins, outs, send, recv):
            cp.start()

    def finish(self, ins, outs, send, recv):
        for cp in self._copies(ins, outs, send, recv):
            cp.wait()

    def place(self, outs):
        return list(outs)


def _pair_sum(name, slots, got, c_idx):
    n = len(slots)
    in_specs, out_specs, out_shape, operands = [], [], [], []
    for (buf, row0, rows), g in zip(slots, got):
        hb, cols = rows // 4, buf.shape[2]
        in_specs += [pl.BlockSpec((None, hb, cols), lambda q, i, cr, r=row0 // hb: (q, r + 2 * cr[0] + i, 0)),
                     pl.BlockSpec((None, hb, cols), lambda q, i, cr: (q, i, 0))]
        out_specs.append(pl.BlockSpec((None, hb, cols), lambda q, i, cr: (q, i, 0)))
        out_shape.append(jax.ShapeDtypeStruct(g.shape, BF16))
        operands += [buf, g]

    def body(c_ref, *refs):
        for i in range(n):
            refs[2 * n + i][...] = (refs[2 * i][...].astype(F32) + refs[2 * i + 1][...].astype(F32)).astype(BF16)

    return pl.pallas_call(
        body, name=name,
        grid_spec=pltpu.PrefetchScalarGridSpec(num_scalar_prefetch=1, grid=(NSH, 2), in_specs=in_specs, out_specs=out_specs),
        out_shape=out_shape, compiler_params=_params(("parallel", "parallel")),
    )(c_idx, *operands)


def _owner_sum(name, pairs, got, p_idx):
    n = len(pairs)
    in_specs, out_specs, out_shape, operands = [], [], [], []
    for own, g in zip(pairs, got):
        _, rows, cols = own.shape
        hb = rows // 2
        in_specs += [pl.BlockSpec((None, hb, cols), lambda i, pr: (pr[0], i, 0)),
                     pl.BlockSpec((3, hb, cols), lambda i, pr: (0, i, 0))]
        out_specs.append(pl.BlockSpec((hb, cols), lambda i, pr: (i, 0)))
        out_shape.append(jax.ShapeDtypeStruct((rows, cols), F32))
        operands += [own, g]

    def body(p_ref, *refs):
        for i in range(n):
            a_ref, b_ref = refs[2 * i], refs[2 * i + 1]
            refs[2 * n + i][...] = ((a_ref[...].astype(F32) + b_ref[0].astype(F32)) + b_ref[1].astype(F32)) + b_ref[2].astype(F32)

    return pl.pallas_call(
        body, name=name,
        grid_spec=pltpu.PrefetchScalarGridSpec(num_scalar_prefetch=1, grid=(2,), in_specs=in_specs, out_specs=out_specs),
        out_shape=out_shape, compiler_params=_params(("parallel",)),
    )(p_idx, *operands)


def _sum_small(slab):
    def body(in_ref, out_ref, all_ref, send, recv):
        x, y, c, _ = _place()
        me = 4 * x + 2 * y + c
        all_ref[me] = in_ref[...]
        cps = []
        for k in range(1, 8):
            peer = (x ^ (k >> 2), y ^ ((k >> 1) & 1), c ^ (k & 1))
            cps.append(pltpu.make_async_remote_copy(src_ref=in_ref, dst_ref=all_ref.at[me], send_sem=send.at[k - 1],
                                                    recv_sem=recv.at[k - 1], device_id=peer, device_id_type=MESH))
        for cp in cps:
            cp.start()
        for cp in cps:
            cp.wait()
        total = all_ref[0]
        for d in range(1, 8):
            total = total + all_ref[d]
        out_ref[...] = total

    return pl.pallas_call(
        body, name="sum_small",
        in_specs=[pl.BlockSpec(memory_space=pltpu.VMEM)], out_specs=pl.BlockSpec(memory_space=pltpu.VMEM),
        out_shape=jax.ShapeDtypeStruct(slab.shape, F32),
        scratch_shapes=[pltpu.VMEM((8,) + slab.shape, F32), pltpu.SemaphoreType.DMA((7,)), pltpu.SemaphoreType.DMA((7,))],
        compiler_params=pltpu.CompilerParams(has_side_effects=True),
    )(slab)


FFNS = ("pre0", "post0", "pre1", "post1")
W_SHAPES = dict({f + "_gu": (NSH, 2, FS, D) for f in FFNS}, **{f + "_d": (NSH, FS, D) for f in FFNS},
                ab_in=(NSH, D, 768), ab_out=(NSH, 256, D), conv=(NSH, 2, 8, LANE), c_in=(NSH, D, D), c_out=(NSH, 256, D))
CARRIED = dict(pre0_norm=("pre0_gu",), pre0_up=("pre0_d",), pre0_down=("ab_in", "conv"), ab_proj_a=("ab_out",),
               c_proj=("post1_d",),
               sb_fwd=("post0_gu", "post0_d"), post0_up=("pre1_gu",), post0_down=("pre1_d",),
               pre1_up=("c_in", "c_out"), hgrn_fwd=("post1_gu",))


FFN_SLOTS = dict(pre0=(0, 1, 0), pre1=(2, 3, 1), post0=(4, 5, 2), post1=(6, 7, 3))
GRAD_SLOTS = dict(ab_out=("b", B_ABOUT, 256), c_in=("b", B_CIN, D), c_out=("b", B_COUT, 256), ab_in=("c", 0, D))
for _f, (_g, _u, _d) in FFN_SLOTS.items():
    GRAD_SLOTS.update({_f + "_g": ("a", _g * FS, FS), _f + "_u": ("a", _u * FS, FS), _f + "_d": ("b", _d * FS, FS)})
REDUCE_STAGES = dict(x=("post1_g", "post1_u", "post1_d", "c_out"), w=("c_in", "pre1_d"),
                     y=("pre1_g", "pre1_u", "post0_g", "post0_u", "post0_d", "ab_out"),
                     z1=("ab_in",), z2=("pre0_g", "pre0_u", "pre0_d"))


def _local_step(x, target, weights, small, shards=None, place=None):
    t = x.shape[0]
    tm = min(MM_TILE, t)
    tw = min(WGRAD_TILE, t)
    nt = t // tm
    tok_si = _spec((tm, D), lambda s, i: (i, 0))
    w = dict(weights)
    reduced = {}

    def stage_slots(stage, buffers):
        return [(buffers[GRAD_SLOTS[n][0]],) + GRAD_SLOTS[n][1:] for n in REDUCE_STAGES[stage]]

    def swap_rider(stage, buffers):
        return _Swap(stage_slots(stage, buffers)) if place is not None else None

    def reduce_start(stage, buffers, swapped=None):
        if place is None:
            return [], None
        slots = stage_slots(stage, buffers)
        if swapped is None:
            swapped = _alone("grad_swap_" + stage, _Swap(slots))
        pairs = _pair_sum("grad_pair_sum_" + stage, slots, swapped, place[0])
        return pairs, _Send(pairs)

    def reduce_end(stage, pairs, landed):
        if place is None:
            return None
        mine = _owner_sum("grad_owner_sum_" + stage, pairs, landed, place[1])
        return _Share(mine)

    def shared(stage, landed):
        if place is not None:
            reduced.update(zip(REDUCE_STAGES[stage], landed))

    def carried(kernel_name):
        names = [n for n in CARRIED[kernel_name] if n not in w]
        return names, (_Gather([shards[n] for n in names]) if names else None)

    def land(names, arrays):
        for n, a in zip(names, arrays):
            w[n] = a.reshape(W_SHAPES[n])

    def ffn_forward(tag, h, hn, next_row):
        names, gather = carried(tag + "_up") if tag + "_up" in CARRIED else ([], None)
        s_up, s_gate, a, got = _ffn_up(tag + "_up", hn, w[tag + "_gu"], gather)
        land(names, got)
        names, gather = carried(tag + "_down") if tag + "_down" in CARRIED else ([], None)
        out = _ffn_down(tag + "_down", a, w[tag + "_d"], h, gather, (small, next_row) if next_row is not None else None)
        if gather is not None:
            out, got = out
            land(names, got)
        out, hn_next = out if next_row is not None else (out, None)
        return out, hn_next, (h, hn, s_up, s_gate, a)

    def out_proj(name, y, w_out, h, next_row):
        return _mm(name, [(y, _spec((tm, 256), lambda i, k: (i, k)), w_out, _spec((None, 256, D), lambda i, k: (k, 0, 0)))],
                   grid=(nt, NSH), o_shape=(t, D), o_dtype=F32, o_spec=_spec((tm, D), lambda i, k: (i, 0)),
                   dims=NN, kaxis=1, nk=NSH, acc_shape=(tm, D), res=(h, _spec((tm, D), lambda i, k: (i, 0))),
                   norm=(small, next_row))

    def out_proj_bwd(tag, dhb, y, w_out, blk, grad_b, make_rider=None):
        grad_b = _wgrad(tag + "_dwout", y, _spec((tw, 256), lambda s, k: (k, s)), dhb, _spec((tw, D), lambda s, k: (k, 0)),
                        256, D, t, tw, grad_b, blk)
        rider = make_rider(grad_b) if make_rider is not None else None
        dy = _mm(tag + "_dy", [(dhb, tok_si, w_out, _spec((None, 256, D), lambda s, i: (s, 0, 0)))],
                 grid=(NSH, nt), o_shape=(t, D), o_dtype=F32, o_spec=_spec((tm, 256), lambda s, i: (i, s)),
                 dims=NT, kaxis=1, nk=1, carry=rider)
        dy, landed = dy if rider is not None else (dy, None)
        return dy, grad_b, landed

    h0 = x
    names, gather = carried("pre0_norm")
    hn, got = _norm_fwd("pre0_norm", h0, small, R_PRE, gather)
    land(names, got)
    h1, hn_ab, pre0 = ffn_forward("pre0", h0, hn, R_MIX)
    def carrying(kernel_name, make):
        names, gather = carried(kernel_name)
        out = make(gather)
        if gather is not None:
            out, got = out
            land(names, got)
        return out

    proj_a = carrying("ab_proj_a", lambda gather: _mm(
        "ab_proj_a", [(hn_ab, tok_si, w["ab_in"], _spec((None, D, 768), lambda s, i: (s, 0, 0)))],
        grid=(2, nt), o_shape=(t, 1536), o_dtype=F32, o_spec=_spec((tm, 768), lambda s, i: (i, s)),
        dims=NN, kaxis=1, nk=1, carry=gather))
    proj_b = _mm("ab_proj_b", [(hn_ab, tok_si, w["ab_in"], _spec((None, D, 768), lambda s, i: (s + 2, 0, 0)))],
                 grid=(2, nt), o_shape=(t, 1536), o_dtype=BF16, o_spec=_spec((tm, 768), lambda s, i: (i, s)),
                 dims=NN, kaxis=1, nk=1)
    y_a = _conv_fwd(proj_a, w["conv"])
    names, gather = carried("sb_fwd")
    y_b, ltot, nblk, got = _sb_fwd(proj_b, gather)
    land(names, got)
    y_ab = jnp.concatenate([y_a, y_b], axis=1)
    h2, hn = out_proj("ab_out", y_ab, w["ab_out"], h1, R_POST)
    h3, hn, post0 = ffn_forward("post0", h2, hn, R_PRE + 1)
    h4, hn_c, pre1 = ffn_forward("pre1", h3, hn, R_MIX + 1)
    proj_c = carrying("c_proj", lambda gather: _mm(
        "c_proj", [(hn_c, tok_si, w["c_in"], _spec((None, D, D), lambda s, i: (s, 0, 0)))],
        grid=(NSH, nt), o_shape=(NSH, t, D), o_dtype=F32, o_spec=_spec((None, tm, D), lambda s, i: (s, i, 0)),
        dims=NN, kaxis=1, nk=1, carry=gather))
    names, gather = carried("hgrn_fwd")
    o_c, y_c, sst, got = _hgrn_fwd(proj_c, small, gather)
    land(names, got)
    h5, hn = out_proj("c_out", y_c, w["c_out"], h4, R_POST + 1)
    h6, _, post1 = ffn_forward("post1", h5, hn, None)
    dh, dhb, d_fin, loss = _final_loss(h6, small, target)

    dh, dhb, grad_a, grad_b, dn_post1, _, _ = _ffn_backward("post1", dh, dhb, *post1, w["post1_gu"], w["post1_d"],
                                                            *FFN_SLOTS["post1"], small, R_POST + 1, 8 * FS, B_ROWS)
    dy_c, grad_b, swapped = out_proj_bwd("c", dhb, y_c, w["c_out"], B_COUT // 256, grad_b,
                                         lambda buf_b: swap_rider("x", dict(a=grad_a, b=buf_b)))
    pairs, rider = reduce_start("x", dict(a=grad_a, b=grad_b), swapped)
    dproj_c, d_clb, d_gam, landed = _hgrn_bwd(proj_c, small, o_c, sst, dy_c, rider)
    share = reduce_end("x", pairs, landed)
    grad_b = _wgrad("c_dwin", hn_c, _spec((tw, D), lambda s, k: (k, 0)), dproj_c, _spec((None, tw, D), lambda s, k: (s, k, 0)),
                    D, D, t, tw, grad_b, B_CIN // D, carry=share)
    if share is not None:
        grad_b, landed = grad_b
        shared("x", landed)
    dh, dhb, dn_mix1, _ = _dhn_norm("c_dhn", [(dproj_c, _spec((None, tm, D), lambda i, k: (k, i, 0)),
                                               w["c_in"], _spec((None, D, D), lambda i, k: (k, 0, 0)))],
                                    NT, h4, small, R_MIX + 1, dh)
    stage_w = {}

    def w_swap(buf_a, buf_b):
        return swap_rider("w", dict(a=buf_a, b=buf_b))

    def w_send(buf_a, buf_b, swapped):
        stage_w["pairs"], rider = reduce_start("w", dict(a=buf_a, b=buf_b), swapped)
        return rider

    on = place is not None
    dh, dhb, grad_a, grad_b, dn_pre1, _, landed = _ffn_backward(
        "pre1", dh, dhb, *pre1, w["pre1_gu"], w["pre1_d"], *FFN_SLOTS["pre1"], small, R_PRE + 1, grad_a, grad_b,
        w_swap if on else None, w_send if on else None)
    share = reduce_end("w", stage_w.get("pairs"), landed)
    dh, dhb, grad_a, grad_b, dn_post0, landed, _ = _ffn_backward(
        "post0", dh, dhb, *post0, w["post0_gu"], w["post0_d"], *FFN_SLOTS["post0"], small, R_POST, grad_a, grad_b,
        (lambda buf_a, buf_b: share) if on else None)
    shared("w", landed)
    dy_ab, grad_b, _ = out_proj_bwd("ab", dhb, y_ab, w["ab_out"], B_ABOUT // 256, grad_b)
    dab, dac, dax, d_conv, swapped = _conv_bwd(proj_a, w["conv"], dy_ab, swap_rider("y", dict(a=grad_a, b=grad_b)))
    pairs, rider = reduce_start("y", dict(a=grad_a, b=grad_b), swapped)
    dq, dk, dv, landed = _sb_bwd(proj_b, dy_ab, ltot, nblk, rider)
    share = reduce_end("y", pairs, landed)
    dproj_ab = jnp.concatenate([dab, dac, dax, dq, dk, dv], axis=1)
    grad_c = _wgrad("ab_dwin", hn_ab, _spec((tw, D), lambda s, k: (k, 0)), dproj_ab, _spec((tw, 768), lambda s, k: (k, s)),
                    D, 768, t, tw, D, 0, carry=share)
    if share is not None:
        grad_c, landed = grad_c
        shared("y", landed)
    dh, dhb, dn_mix0, _ = _dhn_norm("ab_dhn", [(dproj_ab, _spec((tm, 768), lambda i, k: (i, k)),
                                                w["ab_in"], _spec((None, D, 768), lambda i, k: (k, 0, 0)))],
                                    NT, h1, small, R_MIX, dh)
    last = {}

    def z1_send(buf_a, buf_b):
        last["z1"], rider = reduce_start("z1", dict(b=buf_b, c=grad_c))
        return rider

    def z2_send(buf_a, buf_b, _):
        last["z2"], rider = reduce_start("z2", dict(a=buf_a, b=buf_b))
        return rider

    dh, dhb, grad_a, grad_b, dn_pre0, landed, landed_own = _ffn_backward(
        "pre0", dh, dhb, *pre0, w["pre0_gu"], w["pre0_d"], *FFN_SLOTS["pre0"], small, R_PRE, grad_a, grad_b,
        z1_send if on else None, z2_send if on else None)
    if place is not None:
        mine = (_owner_sum("grad_owner_sum_z1", last["z1"], landed, place[1])
                + _owner_sum("grad_owner_sum_z2", last["z2"], landed_own, place[1]))
        reduced.update(zip(REDUCE_STAGES["z1"] + REDUCE_STAGES["z2"], _alone("grad_share_z", _Share(mine))))
    zero = jnp.zeros((1, D), F32)
    conv_rows = jnp.pad(jnp.transpose(d_conv[:, :3, :], (1, 0, 2)).reshape(3, 512), ((0, 0), (0, D - 512)))
    small_grad = jnp.concatenate([
        dn_pre0, dn_pre1, dn_mix0, dn_mix1, dn_post0, dn_post1, d_clb, d_fin,
        jnp.pad(d_gam, ((0, 0), (0, D - HD))), conv_rows,
        jnp.pad(loss, ((0, 0), (0, D - 1))), zero, zero], axis=0)
    return dh, grad_a, grad_b, grad_c, small_grad, reduced


def _small_slab(rows):
    parts = [jnp.pad(r.astype(F32), ((0, 0), (0, D - r.shape[1]))) for r in rows]
    slab = jnp.concatenate(parts, axis=0)
    return jnp.pad(slab, ((0, SMALL_ROWS - slab.shape[0]), (0, 0)))


def kernel(x, ffn_pre_norm, ffn_pre_w_gate, ffn_pre_w_up, ffn_pre_w_down, mix_norm, ffn_post_norm, ffn_post_w_gate, ffn_post_w_up, ffn_post_w_down, ab_w_in, ab_conv_w, ab_w_out, c_w_in, c_lower_bounds, c_out_norm, c_w_out, final_norm, loss_target, m_ffn_pre_norm, m_ffn_pre_w_gate, m_ffn_pre_w_up, m_ffn_pre_w_down, m_mix_norm, m_ffn_post_norm, m_ffn_post_w_gate, m_ffn_post_w_up, m_ffn_post_w_down, m_ab_w_in, m_ab_conv_w, m_ab_w_out, m_c_w_in, m_c_lower_bounds, m_c_out_norm, m_c_w_out, m_final_norm, v_ffn_pre_norm, v_ffn_pre_w_gate, v_ffn_pre_w_up, v_ffn_pre_w_down, v_mix_norm, v_ffn_post_norm, v_ffn_post_w_gate, v_ffn_post_w_up, v_ffn_post_w_down, v_ab_w_in, v_ab_conv_w, v_ab_w_out, v_c_w_in, v_c_lower_bounds, v_c_out_norm, v_c_w_out, v_final_norm):
    t = x.shape[1]
    xi, yi, ci = lax.axis_index("x"), lax.axis_index("y"), lax.axis_index("c")
    p_idx = (2 * xi + yi).astype(jnp.int32).reshape(1)
    c_idx = ci.astype(jnp.int32).reshape(1)

    def halves(m):
        return m.astype(BF16).reshape(2, m.shape[0] // 2, m.shape[1])

    transposed = ("ffn_pre_w_gate", "ffn_pre_w_up", "ffn_post_w_gate", "ffn_post_w_up")

    def flip(a):
        return jnp.swapaxes(a, 1, 2)

    shards = {}
    for name, (w_gate, w_up, w_down, layer) in dict(
            pre0=(ffn_pre_w_gate, ffn_pre_w_up, ffn_pre_w_down, 0), post0=(ffn_post_w_gate, ffn_post_w_up, ffn_post_w_down, 0),
            pre1=(ffn_pre_w_gate, ffn_pre_w_up, ffn_pre_w_down, 1), post1=(ffn_post_w_gate, ffn_post_w_up, ffn_post_w_down, 1)).items():
        shards[name + "_gu"] = jnp.stack([flip(w_gate)[layer], flip(w_up)[layer]]).astype(BF16)
        shards[name + "_d"] = halves(w_down[layer])
    conv_pad = jnp.pad(ab_conv_w[0], ((0, 5), (0, 0)))
    shards.update(ab_in=halves(ab_w_in[0]), ab_out=halves(ab_w_out[0]), c_in=halves(c_w_in[0]), c_out=halves(c_w_out[0]),
                  conv=jnp.stack([conv_pad, jnp.zeros_like(conv_pad)]))
    small = _small_slab([ffn_pre_norm, mix_norm, ffn_post_norm, c_lower_bounds, final_norm.reshape(1, D), c_out_norm])
    small = small.reshape(SMALL_ROWS, 1, D)

    grad_x, _, _, _, small_grad, reduced = _local_step(x[0], loss_target[0], {}, small, shards,
                                                        (c_idx, p_idx))
    whole = {n: g.reshape(2 * g.shape[1], g.shape[2]) for n, g in reduced.items()}
    small_sum = _sum_small(small_grad)

    my_conv = lax.dynamic_slice(small_sum[R_CONV:R_CONV + 3], (0, (2 * xi + yi) * 128), (3, 128))
    grads = {
        "ffn_pre_norm": small_sum[R_PRE:R_PRE + 2], "mix_norm": small_sum[R_MIX:R_MIX + 2],
        "ffn_post_norm": small_sum[R_POST:R_POST + 2], "c_lower_bounds": small_sum[R_CLB:R_CLB + 2],
        "c_out_norm": small_sum[R_GAM:R_GAM + 1, :HD], "final_norm": small_sum[R_FIN],
        "ab_conv_w": my_conv.reshape(1, 3, 128),
    }
    layers = dict(ab_w_in=[whole["ab_in"]], ab_w_out=[whole["ab_out"]], c_w_in=[whole["c_in"]], c_w_out=[whole["c_out"]])
    for kind, key in (("gate", "_g"), ("up", "_u"), ("down", "_d")):
        layers["ffn_pre_w_" + kind] = [whole["pre0" + key], whole["pre1" + key]]
        layers["ffn_post_w_" + kind] = [whole["post0" + key], whole["post1" + key]]
    weights = dict(ffn_pre_norm=ffn_pre_norm, ffn_pre_w_gate=ffn_pre_w_gate, ffn_pre_w_up=ffn_pre_w_up, ffn_pre_w_down=ffn_pre_w_down, mix_norm=mix_norm, ffn_post_norm=ffn_post_norm, ffn_post_w_gate=ffn_post_w_gate, ffn_post_w_up=ffn_post_w_up, ffn_post_w_down=ffn_post_w_down, ab_w_in=ab_w_in, ab_conv_w=ab_conv_w, ab_w_out=ab_w_out, c_w_in=c_w_in, c_lower_bounds=c_lower_bounds, c_out_norm=c_out_norm, c_w_out=c_w_out, final_norm=final_norm)
    m_in = dict(ffn_pre_norm=m_ffn_pre_norm, ffn_pre_w_gate=m_ffn_pre_w_gate, ffn_pre_w_up=m_ffn_pre_w_up, ffn_pre_w_down=m_ffn_pre_w_down, mix_norm=m_mix_norm, ffn_post_norm=m_ffn_post_norm, ffn_post_w_gate=m_ffn_post_w_gate, ffn_post_w_up=m_ffn_post_w_up, ffn_post_w_down=m_ffn_post_w_down, ab_w_in=m_ab_w_in, ab_conv_w=m_ab_conv_w, ab_w_out=m_ab_w_out, c_w_in=m_c_w_in, c_lower_bounds=m_c_lower_bounds, c_out_norm=m_c_out_norm, c_w_out=m_c_w_out, final_norm=m_final_norm)
    v_in = dict(ffn_pre_norm=v_ffn_pre_norm, ffn_pre_w_gate=v_ffn_pre_w_gate, ffn_pre_w_up=v_ffn_pre_w_up, ffn_pre_w_down=v_ffn_pre_w_down, mix_norm=v_mix_norm, ffn_post_norm=v_ffn_post_norm, ffn_post_w_gate=v_ffn_post_w_gate, ffn_post_w_up=v_ffn_post_w_up, ffn_post_w_down=v_ffn_post_w_down, ab_w_in=v_ab_w_in, ab_conv_w=v_ab_conv_w, ab_w_out=v_ab_w_out, c_w_in=v_c_w_in, c_lower_bounds=v_c_lower_bounds, c_out_norm=v_c_out_norm, c_w_out=v_c_w_out, final_norm=v_final_norm)
    names = list(weights)
    big = [n for n in names if weights[n].size >= 65536]
    tiny = [n for n in names if n not in big]

    delta, new_m, new_v = {}, {}, {}
    for n in big:
        turn = flip if n in transposed else (lambda a: a)
        shape = turn(weights[n]).shape
        two_d = (shape[0] * shape[1], shape[2])
        d, m2, v2, g2 = _adamw("adamw_" + n, turn(weights[n]).reshape(two_d), layers[n],
                               turn(m_in[n]).reshape(two_d), turn(v_in[n]).reshape(two_d))
        delta[n], new_m[n], new_v[n] = turn(d.reshape(shape)), turn(m2.reshape(shape)), turn(v2.reshape(shape))
        grads[n] = turn(g2.reshape(shape))

    def tiny_slab(src):
        return _small_slab([src[n].reshape(-1, src[n].shape[-1]) for n in tiny])

    offs, row = {}, 0
    for n in tiny:
        nrows = weights[n].size // weights[n].shape[-1]
        offs[n] = (row, nrows)
        row += nrows
    d, m2, v2, _ = _adamw("adamw_small", tiny_slab(weights), [tiny_slab(grads)], tiny_slab(m_in), tiny_slab(v_in))
    for n in tiny:
        r0, nr = offs[n]
        shape = weights[n].shape
        for dst, src in ((delta, d), (new_m, m2), (new_v, v2)):
            dst[n] = src[r0:r0 + nr, :shape[-1]].reshape(shape)

    loss = small_sum[R_LOSS, 0]
    return (loss, grad_x.reshape(1, t, D), *[grads[n] for n in names], *[delta[n] for n in names],
            *[new_m[n] for n in names], *[new_v[n] for n in names])
```

```python
import functools
import math

import jax
import jax.numpy as jnp
from jax import lax
from jax.experimental import pallas as pl
from jax.experimental.pallas import tpu as pltpu

F32 = jnp.float32
BF16 = jnp.bfloat16
MESH = pl.DeviceIdType.MESH
ANY = pl.BlockSpec(memory_space=pl.ANY)

D = 1024
FS = 704
NSH = 4
RMS_EPS = 1e-6
MACARON = 0.5
CHUNK = 64
HD = 128
HGRN_HPS = 8
SBQ = 128
SB_PAIRS_FWD = 4
SB_PAIRS_BWD = 4
SB_DEAD = -105.0
CONV_HALO = 8
LANE = 128
ROW_TILE = 1024
MM_TILE = 1024
WGRAD_TILE = 4096
VMEM_LIMIT = 48 * 1024 * 1024
VMEM_LIMIT_BIG = 58 * 1024 * 1024

ADAM_LR, ADAM_B1, ADAM_B2, ADAM_EPS, ADAM_WD, ADAM_STEP = 0.001, 0.9, 0.999, 1e-08, 0.01, 10

NN = ((1,), (0,))
NT = ((1,), (1,))
TN = ((0,), (0,))

SMALL_ROWS = 16
R_PRE, R_MIX, R_POST, R_CLB, R_FIN, R_GAM, R_CONV, R_LOSS = 0, 2, 4, 6, 8, 9, 10, 13

B_ABOUT = 4 * FS
B_CIN = B_ABOUT + 256
B_COUT = B_CIN + 1024
B_ROWS = B_COUT + 256


def _dg(a, b, dims):
    return lax.dot_general(a, b, (dims, ((), ())), preferred_element_type=F32)


def _split(x):
    hi = x.astype(BF16)
    lo = (x - hi.astype(F32)).astype(BF16)
    return hi, lo


def _dot3(a, b, dims):
    ah, al = _split(a)
    bh, bl = _split(b)
    if dims == TN:
        n = b.shape[1]
        both = _dg(ah, jnp.concatenate([bh, bl], axis=1), dims)
        return both[:, :n] + both[:, n:] + _dg(al, bh, dims)
    m = a.shape[0]
    both = _dg(jnp.concatenate([ah, al], axis=0), bh, dims)
    return both[:m] + both[m:] + _dg(ah, bl, dims)


def _sigmoid(x):
    return 1.0 / (1.0 + jnp.exp(-x))


def _params(sem):
    return pltpu.CompilerParams(dimension_semantics=sem, vmem_limit_bytes=VMEM_LIMIT)


def _spec(shape, imap):
    return pl.BlockSpec(shape, imap)


def _accumulate(acc_ref, pairs, dims):
    part = None
    for a_ref, b_ref in pairs:
        d = _dg(a_ref[...], b_ref[...], dims)
        part = d if part is None else part + d
    acc_ref[...] += part


def _mm(name, pairs, *, grid, o_shape, o_dtype, o_spec, dims, kaxis, nk, acc_shape=None, res=None, scale=None,
        into=None, carry=None, norm=None):
    npairs = len(pairs)
    operands, specs = [], []
    for a, a_spec, b, b_spec in pairs:
        operands += [a, b]
        specs += [a_spec, b_spec]
    if res is not None:
        operands.append(res[0])
        specs.append(res[1])
    aliases = {}
    if into is not None:
        aliases = {len(operands): 0}
        operands.append(into)
        specs.append(ANY)
    if norm is not None:
        operands.append(norm[0])
        specs.append(_spec((None, 1, D), lambda *_: (norm[1], 0, 0)))
    n_own = len(operands)
    n_out = 2 if norm is not None else 1
    nc = carry.n if carry is not None else 0
    if nc:
        operands += carry.groups
        specs += carry.in_specs

    def body(*refs):
        o_ref = refs[n_own + nc]
        riders = ((refs[n_own:n_own + nc], refs[n_own + nc + n_out:n_own + 2 * nc + n_out], refs[-2], refs[-1])
                  if nc else None)
        if nc:
            carry.ride(grid, riders, "start")

        def finish(val):
            if scale is not None:
                val = val * scale
            if res is not None:
                val = val + refs[2 * npairs][...]
            o_ref[...] = val.astype(o_dtype)
            if norm is not None:
                r = lax.rsqrt(jnp.mean(val * val, axis=-1, keepdims=True) + RMS_EPS)
                refs[n_own + nc + 1][...] = (val * r * refs[n_own - 1][...]).astype(BF16)

        if nk == 1:
            part = None
            for n in range(npairs):
                d = _dg(refs[2 * n][...], refs[2 * n + 1][...], dims)
                part = d if part is None else part + d
            finish(part)
        else:
            acc_ref = refs[n_own + 2 * nc + n_out]
            k = pl.program_id(kaxis)

            @pl.when(k == 0)
            def _():
                acc_ref[...] = jnp.zeros_like(acc_ref)

            _accumulate(acc_ref, [(refs[2 * n], refs[2 * n + 1]) for n in range(npairs)], dims)

            @pl.when(k == nk - 1)
            def _():
                finish(acc_ref[...])
        if nc:
            carry.ride(grid, riders, "finish")

    sem = tuple("arbitrary" if (nc or (ax == kaxis and nk > 1)) else "parallel" for ax in range(len(grid)))
    outs = pl.pallas_call(
        body, name=name, grid=grid, in_specs=specs,
        out_specs=[o_spec] * n_out + (carry.out_specs if nc else []),
        out_shape=[jax.ShapeDtypeStruct(o_shape, o_dtype)] + ([jax.ShapeDtypeStruct(o_shape, BF16)] if norm is not None else [])
        + (carry.out_shape if nc else []),
        scratch_shapes=([pltpu.VMEM(acc_shape, F32)] if nk > 1 else []) + (carry.scratch if nc else []),
        input_output_aliases=aliases,
        compiler_params=_params(sem),
    )(*operands)
    main = (outs[0], outs[1]) if norm is not None else outs[0]
    return (main, carry.place(outs[n_out:])) if nc else main


def _norm_fwd(name, h, gain_slab, row, rider=None):
    t = h.shape[0]
    tm = min(ROW_TILE, t)
    grid = (t // tm,)
    nr = rider.n if rider is not None else 0

    def body(*refs):
        h_ref, g_ref, o_ref = refs[0], refs[1], refs[2 + nr]
        riders = (refs[2:2 + nr], refs[3 + nr:3 + 2 * nr], refs[-2], refs[-1]) if nr else None
        if nr:
            rider.ride(grid, riders, "start")
        x = h_ref[...]
        r = lax.rsqrt(jnp.mean(x * x, axis=-1, keepdims=True) + RMS_EPS)
        o_ref[...] = (x * r * g_ref[...]).astype(BF16)
        if nr:
            rider.ride(grid, riders, "finish")

    outs = pl.pallas_call(
        body, name=name, grid=grid,
        in_specs=[_spec((tm, D), lambda i: (i, 0)), _spec((None, 1, D), lambda i: (row, 0, 0))]
        + (rider.in_specs if nr else []),
        out_specs=[_spec((tm, D), lambda i: (i, 0))] + (rider.out_specs if nr else []),
        out_shape=[jax.ShapeDtypeStruct((t, D), BF16)] + (rider.out_shape if nr else []),
        scratch_shapes=rider.scratch if nr else [],
        compiler_params=_params(("arbitrary",) if nr else ("parallel",)),
    )(h, gain_slab, *(rider.groups if nr else []))
    return outs[0], (rider.place(outs[1:]) if nr else [])


def _dhn_norm(name, pairs, dims, h, gain_slab, row, dres, rider=None):
    t = h.shape[0]
    tm = min(MM_TILE, t)
    nt = t // tm
    grid = (nt, NSH)
    npairs = len(pairs)
    nr = rider.n if rider is not None else 0
    operands, specs = [], []
    for a, a_spec, b, b_spec in pairs:
        operands += [a, b]
        specs += [a_spec, b_spec]
    row_spec = _spec((tm, D), lambda i, k: (i, 0))
    operands += [h, gain_slab, dres]
    specs += [row_spec, _spec((None, 1, D), lambda i, k: (row, 0, 0)), row_spec]
    n_own = len(operands)

    def body(*refs):
        h_ref, g_ref, dres_ref = refs[2 * npairs:n_own]
        dh_ref, dhb_ref, dg_ref = refs[n_own + nr:n_own + nr + 3]
        acc_ref, gacc_ref = refs[n_own + 2 * nr + 3:n_own + 2 * nr + 5]
        riders = (refs[n_own:n_own + nr], refs[n_own + nr + 3:n_own + 2 * nr + 3], refs[-2], refs[-1]) if nr else None
        if nr:
            rider.ride(grid, riders, "start")
        i, k = pl.program_id(0), pl.program_id(1)

        @pl.when(k == 0)
        def _():
            acc_ref[...] = jnp.zeros_like(acc_ref)

        _accumulate(acc_ref, [(refs[2 * n], refs[2 * n + 1]) for n in range(npairs)], dims)

        @pl.when(k == NSH - 1)
        def _():
            x = h_ref[...]
            r = lax.rsqrt(jnp.mean(x * x, axis=-1, keepdims=True) + RMS_EPS)
            xh = x * r
            dy = acc_ref[...]
            gdy = dy * g_ref[...]
            dh = dres_ref[...] + (gdy - xh * jnp.mean(gdy * xh, axis=-1, keepdims=True)) * r
            dh_ref[...] = dh
            dhb_ref[...] = dh.astype(BF16)
            gpart = jnp.sum((dy * xh).reshape(tm // 8, 8, D), axis=0)

            @pl.when(i == 0)
            def _():
                gacc_ref[...] = gpart

            @pl.when(i > 0)
            def _():
                gacc_ref[...] += gpart

            @pl.when(i == nt - 1)
            def _():
                dg_ref[...] = jnp.sum(gacc_ref[...], axis=0, keepdims=True)

        if nr:
            rider.ride(grid, riders, "finish")

    outs = pl.pallas_call(
        body, name=name, grid=grid, in_specs=specs + (rider.in_specs if nr else []),
        out_specs=[row_spec, row_spec, _spec((1, D), lambda i, k: (0, 0))] + (rider.out_specs if nr else []),
        out_shape=[jax.ShapeDtypeStruct((t, D), F32), jax.ShapeDtypeStruct((t, D), BF16),
                   jax.ShapeDtypeStruct((1, D), F32)] + (rider.out_shape if nr else []),
        scratch_shapes=[pltpu.VMEM((tm, D), F32), pltpu.VMEM((8, D), F32)] + (rider.scratch if nr else []),
        compiler_params=pltpu.CompilerParams(dimension_semantics=("arbitrary", "arbitrary"),
                                             vmem_limit_bytes=VMEM_LIMIT_BIG),
    )(*operands, *(rider.groups if nr else []))
    return outs[0], outs[1], outs[2], (rider.place(outs[3:]) if nr else [])


def _final_loss(h, gain_slab, target):
    t = h.shape[0]
    tm = min(ROW_TILE, t)
    nt = t // tm

    def body(h_ref, g_ref, t_ref, dh_ref, dhb_ref, dg_ref, loss_ref, acc_ref, lacc_ref):
        i = pl.program_id(0)
        x = h_ref[...]
        g = g_ref[...]
        r = lax.rsqrt(jnp.mean(x * x, axis=-1, keepdims=True) + RMS_EPS)
        xh = x * r
        err = xh * g - t_ref[...]
        dy = err * (1.0 / D)
        gdy = dy * g
        dh = (gdy - xh * jnp.mean(gdy * xh, axis=-1, keepdims=True)) * r
        dh_ref[...] = dh
        dhb_ref[...] = dh.astype(BF16)
        part = jnp.sum((dy * xh).reshape(tm // 8, 8, D), axis=0)
        lpart = jnp.sum((err * err).reshape(tm // 8, 8, D), axis=0)

        @pl.when(i == 0)
        def _():
            acc_ref[...] = part
            lacc_ref[...] = lpart

        @pl.when(i > 0)
        def _():
            acc_ref[...] += part
            lacc_ref[...] += lpart

        @pl.when(i == nt - 1)
        def _():
            dg_ref[...] = jnp.sum(acc_ref[...], axis=0, keepdims=True)
            rows = jnp.sum(lacc_ref[...], axis=0, keepdims=True)
            loss_ref[...] = jnp.sum(rows, axis=1, keepdims=True) * (0.5 / D)

    row_spec = _spec((tm, D), lambda i: (i, 0))
    return pl.pallas_call(
        body, name="final_loss", grid=(nt,),
        in_specs=[row_spec, _spec((None, 1, D), lambda i: (R_FIN, 0, 0)), row_spec],
        out_specs=[row_spec, row_spec, _spec((1, D), lambda i: (0, 0)), _spec((1, 1), lambda i: (0, 0))],
        out_shape=[jax.ShapeDtypeStruct((t, D), F32), jax.ShapeDtypeStruct((t, D), BF16),
                   jax.ShapeDtypeStruct((1, D), F32), jax.ShapeDtypeStruct((1, 1), F32)],
        scratch_shapes=[pltpu.VMEM((8, D), F32), pltpu.VMEM((8, D), F32)],
        compiler_params=_params(("arbitrary",)),
    )(h, gain_slab, target)


def _ffn_up(name, hn, wgu, carry=None):
    t = hn.shape[0]
    tm = min(MM_TILE, t)
    grid = (NSH, t // tm)
    nc = carry.n if carry is not None else 0

    def body(*refs):
        x_ref, wg_ref, wu_ref = refs[:3]
        s_up_ref, s_gate_ref, a_ref = refs[3 + nc:6 + nc]
        riders = (refs[3:3 + nc], refs[6 + nc:6 + 2 * nc], refs[-2], refs[-1]) if nc else None
        if nc:
            carry.ride(grid, riders, "start")
        x = x_ref[...]
        g = _dg(x, wg_ref[...], NT)
        u = _dg(x, wu_ref[...], NT)
        sg = _sigmoid(g)
        silu = g * sg
        s_up_ref[...] = (MACARON * silu).astype(BF16)
        s_gate_ref[...] = (MACARON * u * (sg * (1.0 + g * (1.0 - sg)))).astype(BF16)
        a_ref[...] = (silu * u).astype(BF16)
        if nc:
            carry.ride(grid, riders, "finish")

    act = _spec((None, tm, FS), lambda s, i: (s, i, 0))
    shape = jax.ShapeDtypeStruct((NSH, t, FS), BF16)
    outs = pl.pallas_call(
        body, name=name, grid=grid,
        in_specs=[_spec((tm, D), lambda s, i: (i, 0)),
                  _spec((None, None, FS, D), lambda s, i: (s, 0, 0, 0)),
                  _spec((None, None, FS, D), lambda s, i: (s, 1, 0, 0))] + (carry.in_specs if nc else []),
        out_specs=[act, act, act] + (carry.out_specs if nc else []),
        out_shape=[shape, shape, shape] + (carry.out_shape if nc else []),
        scratch_shapes=carry.scratch if nc else [],
        compiler_params=_params(("arbitrary", "arbitrary") if nc else ("parallel", "parallel")),
    )(hn, wgu, wgu, *(carry.groups if nc else []))
    return (outs[0], outs[1], outs[2], carry.place(outs[3:]) if nc else [])


def _ffn_down(name, a, wd, h, carry=None, norm=None):
    t = h.shape[0]
    tm = min(MM_TILE, t)
    return _mm(name, [(a, _spec((None, tm, FS), lambda i, k: (k, i, 0)),
                       wd, _spec((None, FS, D), lambda i, k: (k, 0, 0)))],
               grid=(t // tm, NSH), o_shape=(t, D), o_dtype=F32, o_spec=_spec((tm, D), lambda i, k: (i, 0)),
               dims=NN, kaxis=1, nk=NSH, acc_shape=(tm, D), res=(h, _spec((tm, D), lambda i, k: (i, 0))),
               scale=MACARON, carry=carry, norm=norm)


def _ffn_bwd_up(name, dhb, wd, s_up, s_gate, rider=None):
    t = dhb.shape[0]
    tm = min(2 * MM_TILE, t)
    grid = (t // tm, NSH)
    nr = rider.n if rider is not None else 0

    def body(*refs):
        dh_ref, wd_ref, s_up_ref, s_gate_ref = refs[:4]
        dg_ref, du_ref = refs[4 + nr:6 + nr]
        riders = (refs[4:4 + nr], refs[6 + nr:6 + 2 * nr], refs[-2], refs[-1]) if nr else None
        if nr:
            rider.ride(grid, riders, "start")
        da = _dg(dh_ref[...], wd_ref[...], NT).astype(BF16)
        du_ref[...] = da * s_up_ref[...]
        dg_ref[...] = da * s_gate_ref[...]
        if nr:
            rider.ride(grid, riders, "finish")

    act = _spec((None, tm, FS), lambda i, s: (s, i, 0))
    shape = jax.ShapeDtypeStruct((NSH, t, FS), BF16)
    outs = pl.pallas_call(
        body, name=name, grid=grid,
        in_specs=[_spec((tm, D), lambda i, s: (i, 0)), _spec((None, FS, D), lambda i, s: (s, 0, 0)), act, act]
        + (rider.in_specs if nr else []),
        out_specs=[act, act] + (rider.out_specs if nr else []),
        out_shape=[shape, shape] + (rider.out_shape if nr else []),
        scratch_shapes=rider.scratch if nr else [],
        compiler_params=pltpu.CompilerParams(
            dimension_semantics=("arbitrary", "arbitrary") if nr else ("parallel", "parallel"),
            vmem_limit_bytes=VMEM_LIMIT_BIG),
    )(dhb, wd, s_up, s_gate, *(rider.groups if nr else []))
    return outs[0], outs[1], (rider.place(outs[2:]) if nr else [])


def _wgrad(name, a, a_spec, b, b_spec, out_rows, out_cols, t, tt, group, slot, scale=None, carry=None):
    first = isinstance(group, int)
    rows = group if first else group.shape[1]
    return _mm(name, [(a, a_spec, b, b_spec)], grid=(NSH, t // tt),
               o_shape=(NSH, rows, out_cols), o_dtype=BF16,
               o_spec=_spec((None, out_rows, out_cols), lambda s, k: (s, slot, 0)),
               dims=TN, kaxis=1, nk=t // tt, acc_shape=(out_rows, out_cols), scale=scale,
               into=None if first else group, carry=carry)


def _wgrad_gate_up(name, dg, du, hn, group, gate_idx):
    t = hn.shape[0]
    first = isinstance(group, int)
    rows = group if first else group.shape[1]

    def body(dg_ref, du_ref, hn_ref, *refs):
        o_ref = refs[-1]
        x = hn_ref[...]
        o_ref[...] = jnp.concatenate([_dg(dg_ref[...], x, TN), _dg(du_ref[...], x, TN)], axis=0).astype(BF16)

    hid = _spec((None, t, FS), lambda s: (s, 0, 0))
    return pl.pallas_call(
        body, name=name, grid=(NSH,),
        in_specs=[hid, hid, _spec((t, D), lambda s: (0, 0))] + ([] if first else [ANY]),
        out_specs=_spec((None, 2 * FS, D), lambda s: (s, gate_idx // 2, 0)),
        out_shape=jax.ShapeDtypeStruct((NSH, rows, D), BF16),
        input_output_aliases={} if first else {3: 0},
        compiler_params=pltpu.CompilerParams(dimension_semantics=("parallel",), vmem_limit_bytes=VMEM_LIMIT_BIG),
    )(dg, du, hn, *([] if first else [group]))


def _ffn_backward(tag, dh, dhb, h_in, hn, s_up, s_gate, a, wgu, wd, gate_idx, up_idx, down_idx, small, norm_row,
                  grad_a, grad_b, rider_up=None, rider_dhn=None):
    t = dh.shape[0]
    tm = min(MM_TILE, t)
    tt = min(WGRAD_TILE, t)
    tok = _spec((tt, D), lambda s, k: (k, 0))
    hid = _spec((None, tt, FS), lambda s, k: (s, k, 0))
    grad_b = _wgrad(tag + "_dwd", a, hid, dhb, tok, FS, D, t, tt, grad_b, down_idx, scale=MACARON)
    rider = rider_up(grad_a, grad_b) if rider_up is not None else None
    dg, du, landed_up = _ffn_bwd_up(tag + "_bwd_up", dhb, wd, s_up, s_gate, rider)
    assert up_idx == gate_idx + 1 and gate_idx % 2 == 0
    grad_a = _wgrad_gate_up(tag + "_dwgu", dg, du, hn, grad_a, gate_idx)
    rider = rider_dhn(grad_a, grad_b, landed_up) if rider_dhn is not None else None
    act = _spec((None, tm, FS), lambda i, k: (k, i, 0))
    dh_in, dhb_in, d_gain, landed_dhn = _dhn_norm(
        tag + "_dhn", [(dg, act, wgu, _spec((None, None, FS, D), lambda i, k: (k, 0, 0, 0))),
                       (du, act, wgu, _spec((None, None, FS, D), lambda i, k: (k, 1, 0, 0)))],
        NN, h_in, small, norm_row, dh, rider)
    return dh_in, dhb_in, grad_a, grad_b, d_gain, landed_up, landed_dhn


def _conv_fwd(proj_a, conv_w):
    t = proj_a.shape[0]
    tm = min(ROW_TILE, t)
    hb = tm // CONV_HALO

    def body(ab_ref, ac_ref, ax_ref, acp_ref, axp_ref, w_ref, y_ref):
        i = pl.program_id(1)
        u = ac_ref[...] * ax_ref[...]
        up = jnp.where(i > 0, acp_ref[...] * axp_ref[...], 0.0)
        ext = jnp.concatenate([up, u], axis=0)
        u1 = pltpu.roll(ext, 1, 0)[CONV_HALO:]
        u2 = pltpu.roll(ext, 2, 0)[CONV_HALO:]
        w = w_ref[...]
        conv = w[0:1] * u2 + w[1:2] * u1 + w[2:3] * u
        y_ref[...] = (ab_ref[...] * conv).astype(BF16)

    def cur(off):
        return _spec((tm, LANE), lambda j, i: (i, off + j))

    def prev(off):
        return _spec((CONV_HALO, LANE), lambda j, i: (jnp.maximum(i * hb - 1, 0), off + j))

    return pl.pallas_call(
        body, name="conv_fwd", grid=(4, t // tm),
        in_specs=[cur(0), cur(4), cur(8), prev(4), prev(8),
                  _spec((None, None, 8, LANE), lambda j, i: (j, 0, 0, 0))],
        out_specs=_spec((tm, LANE), lambda j, i: (i, j)),
        out_shape=jax.ShapeDtypeStruct((t, 512), BF16),
        compiler_params=_params(("parallel", "parallel")),
    )(proj_a, proj_a, proj_a, proj_a, proj_a, conv_w)


def _conv_bwd(proj_a, conv_w, dy, rider=None):
    t = proj_a.shape[0]
    tm = min(ROW_TILE, t)
    hb = tm // CONV_HALO
    nt = t // tm
    grid = (4, nt)
    nr = rider.n if rider is not None else 0

    def body(*refs):
        ab_ref, ac_ref, ax_ref, dy_ref, acp_ref, axp_ref, abn_ref, dyn_ref, w_ref = refs[:9]
        dab_ref, dac_ref, dax_ref, dw_ref = refs[9 + nr:13 + nr]
        acc_ref = refs[13 + 2 * nr]
        riders = (refs[9:9 + nr], refs[13 + nr:13 + 2 * nr], refs[-2], refs[-1]) if nr else None
        if nr:
            rider.ride(grid, riders, "start")
        i = pl.program_id(1)
        ab, ac, ax = ab_ref[...], ac_ref[...], ax_ref[...]
        u = ac * ax
        up = jnp.where(i > 0, acp_ref[...] * axp_ref[...], 0.0)
        ext = jnp.concatenate([up, u], axis=0)
        u1 = pltpu.roll(ext, 1, 0)[CONV_HALO:]
        u2 = pltpu.roll(ext, 2, 0)[CONV_HALO:]
        w = w_ref[...]
        conv = w[0:1] * u2 + w[1:2] * u1 + w[2:3] * u
        dy_v = dy_ref[...]
        dab_ref[...] = (dy_v * conv).astype(BF16)
        dc = dy_v * ab
        dcn = jnp.where(i < nt - 1, dyn_ref[...] * abn_ref[...], 0.0)
        extn = jnp.concatenate([dc, dcn], axis=0)
        n = tm + CONV_HALO
        dc1 = pltpu.roll(extn, n - 1, 0)[:tm]
        dc2 = pltpu.roll(extn, n - 2, 0)[:tm]
        du = w[2:3] * dc + w[1:2] * dc1 + w[0:1] * dc2
        dac_ref[...] = (du * ax).astype(BF16)
        dax_ref[...] = (du * ac).astype(BF16)
        rid = lax.broadcasted_iota(jnp.int32, (8, LANE), 0)
        part = jnp.where(rid == 0, jnp.sum(dc * u2, axis=0, keepdims=True),
                         jnp.where(rid == 1, jnp.sum(dc * u1, axis=0, keepdims=True),
                                   jnp.where(rid == 2, jnp.sum(dc * u, axis=0, keepdims=True), 0.0)))

        @pl.when(i == 0)
        def _():
            acc_ref[...] = part

        @pl.when(i > 0)
        def _():
            acc_ref[...] += part

        @pl.when(i == nt - 1)
        def _():
            dw_ref[...] = acc_ref[...]

        if nr:
            rider.ride(grid, riders, "finish")

    def cur(off):
        return _spec((tm, LANE), lambda j, i: (i, off + j))

    def prev(off):
        return _spec((CONV_HALO, LANE), lambda j, i: (jnp.maximum(i * hb - 1, 0), off + j))

    def nxt(off):
        return _spec((CONV_HALO, LANE), lambda j, i: (jnp.minimum((i + 1) * hb, nt * hb - 1), off + j))

    outs = pl.pallas_call(
        body, name="conv_bwd", grid=grid,
        in_specs=[cur(0), cur(4), cur(8), cur(0), prev(4), prev(8), nxt(0), nxt(0),
                  _spec((None, None, 8, LANE), lambda j, i: (j, 0, 0, 0))] + (rider.in_specs if nr else []),
        out_specs=[_spec((tm, LANE), lambda j, i: (i, j)), _spec((tm, LANE), lambda j, i: (i, j)),
                   _spec((tm, LANE), lambda j, i: (i, j)), _spec((None, 8, LANE), lambda j, i: (j, 0, 0))]
        + (rider.out_specs if nr else []),
        out_shape=[jax.ShapeDtypeStruct((t, 512), BF16), jax.ShapeDtypeStruct((t, 512), BF16),
                   jax.ShapeDtypeStruct((t, 512), BF16), jax.ShapeDtypeStruct((4, 8, LANE), F32)]
        + (rider.out_shape if nr else []),
        scratch_shapes=[pltpu.VMEM((8, LANE), F32)] + (rider.scratch if nr else []),
        compiler_params=_params(("arbitrary", "arbitrary") if nr else ("parallel", "arbitrary")),
    )(proj_a, proj_a, proj_a, dy, proj_a, proj_a, proj_a, dy, conv_w, *(rider.groups if nr else []))
    return outs[0], outs[1], outs[2], outs[3], (rider.place(outs[4:]) if nr else [])


def _log_sigmoid(z):
    return jnp.minimum(z, 0.0) - jnp.log(1.0 + jnp.exp(-jnp.abs(z)))


def _sb_masks():
    row = lax.broadcasted_iota(jnp.int32, (SBQ, SBQ), 0)
    col = lax.broadcasted_iota(jnp.int32, (SBQ, SBQ), 1)
    return row, col


def _ones_where(mask):
    return jnp.where(mask, 1.0, 0.0).astype(BF16)


def _head_mask(head):
    lane = lax.broadcasted_iota(jnp.int32, (1, LANE), 1)
    return lane >= 64 if head else lane < 64


def _sb_fwd(proj_b, carry=None):
    t = proj_b.shape[0]
    nq = t // SBQ
    scale = 1.0 / math.sqrt(64.0)

    npair = SB_PAIRS_FWD
    wide = npair * LANE
    ngrp = 4 // npair
    chains = [(p, head) for p in range(npair) for head in range(2)]

    grid = (ngrp, nq)
    nc = carry.n if carry is not None else 0

    def body(*refs):
        q_ref, k_ref, v_ref = refs[:3]
        y_ref, l_ref, n_ref = refs[3 + nc:6 + nc]
        riders = (refs[3:3 + nc], refs[6 + nc:6 + 2 * nc], refs[-2], refs[-1]) if nc else None
        if nc:
            carry.ride(grid, riders, "start")
        grp = pl.program_id(0)
        qi = pl.program_id(1)
        row, col = _sb_masks()
        m_suffix = _ones_where(row > col)
        rows = len(chains) * SBQ
        strict = (lax.broadcasted_iota(jnp.int32, (rows, SBQ), 1)
                  < (lax.broadcasted_iota(jnp.int32, (rows, SBQ), 0) & (SBQ - 1)))
        q_pair = []
        for p in range(npair):
            q_all = q_ref[:, p * LANE:(p + 1) * LANE]
            q_pair.append(jnp.concatenate([jnp.where(_head_mask(head), q_all, jnp.zeros_like(q_all)) for head in range(2)],
                                          axis=0))

        def block(kb, state, diag):
            run, acc = state
            start = pl.multiple_of(kb * SBQ, SBQ)
            z = jnp.concatenate([_dg(q_pair[p], k_ref[pl.ds(start, SBQ), p * LANE:(p + 1) * LANE], NT)
                                 for p in range(npair)], axis=0) * scale
            lb = _log_sigmoid(z)
            lk = lb - z
            if diag:
                lk = jnp.where(strict, lk, 0.0)
            hi, lo = _split(lk)
            sums = _dg(jnp.concatenate([hi, lo], axis=0), m_suffix, NN)
            w = jnp.exp(lb + (run + sums[:rows] + sums[rows:]))
            if diag:
                w = jnp.where(strict, w, 0.0)
            wb = w.astype(BF16)
            acc = acc + jnp.concatenate(
                [_dg(wb[2 * p * SBQ:2 * (p + 1) * SBQ], v_ref[pl.ds(start, SBQ), p * LANE:(p + 1) * LANE], NN)
                 for p in range(npair)], axis=0)
            run = run + jnp.sum(hi.astype(F32) + lo.astype(F32), axis=1, keepdims=True)
            return run, acc

        state = block(qi, (jnp.zeros((rows, 1), F32), jnp.zeros((rows, LANE), F32)), True)

        def live(c):
            return jnp.logical_and(c[0] < qi, jnp.max(c[1][0]) > SB_DEAD)

        def step(c):
            return c[0] + 1, block(qi - 1 - c[0], c[1], False)

        count, (run, acc) = lax.while_loop(live, step, (jnp.int32(0), state))
        n_ref[grp * nq + qi] = count.astype(F32)
        hm = _head_mask(0)
        for p in range(npair):
            lo_rows, hi_rows = slice(2 * p * SBQ, (2 * p + 1) * SBQ), slice((2 * p + 1) * SBQ, (2 * p + 2) * SBQ)
            y_ref[:, p * LANE:(p + 1) * LANE] = jnp.where(hm, acc[lo_rows], acc[hi_rows]).astype(BF16)
            l_ref[p] = jnp.where(hm, run[lo_rows], run[hi_rows])
        if nc:
            carry.ride(grid, riders, "finish")

    outs = pl.pallas_call(
        body, name="sb_fwd", grid=grid,
        in_specs=[_spec((SBQ, wide), lambda g, i: (i, g)),
                  _spec((t, wide), lambda g, i: (0, ngrp + g)),
                  _spec((t, wide), lambda g, i: (0, 2 * ngrp + g))] + (carry.in_specs if nc else []),
        out_specs=[_spec((SBQ, wide), lambda g, i: (i, g)), _spec((npair, SBQ, LANE), lambda g, i: (g, i, 0)),
                   pl.BlockSpec(memory_space=pltpu.SMEM)] + (carry.out_specs if nc else []),
        out_shape=[jax.ShapeDtypeStruct((t, 512), BF16), jax.ShapeDtypeStruct((4, t, LANE), F32),
                   jax.ShapeDtypeStruct((ngrp * nq,), F32)] + (carry.out_shape if nc else []),
        scratch_shapes=carry.scratch if nc else [],
        compiler_params=_params(("arbitrary", "arbitrary")),
    )(proj_b, proj_b, proj_b, *(carry.groups if nc else []))
    return outs[0], outs[1], outs[2], (carry.place(outs[3:]) if nc else [])


def _sb_bwd(proj_b, dy, ltot, nblk, rider=None):
    t = proj_b.shape[0]
    nq = t // SBQ
    scale = 1.0 / math.sqrt(64.0)

    npair = SB_PAIRS_BWD
    wide = npair * LANE
    ngrp = 4 // npair
    chains = [(p, head) for p in range(npair) for head in range(2)]
    grid = (ngrp, nq)
    nr = rider.n if rider is not None else 0
    per_count = SB_PAIRS_FWD // SB_PAIRS_BWD

    def body(*refs):
        q_ref, k_ref, v_ref, dy_ref, l_ref, n_ref = refs[:6]
        dq_ref, dk_ref, dv_ref = refs[6 + nr:9 + nr]
        dk_acc, dv_acc = refs[9 + 2 * nr:11 + 2 * nr]
        riders = (refs[6:6 + nr], refs[9 + nr:9 + 2 * nr], refs[-2], refs[-1]) if nr else None
        if nr:
            rider.ride(grid, riders, "start")
        grp = pl.program_id(0)
        qi = pl.program_id(1)

        @pl.when(qi == 0)
        def _():
            dk_acc[...] = jnp.zeros_like(dk_acc)
            dv_acc[...] = jnp.zeros_like(dv_acc)

        row, col = _sb_masks()
        m_prefix = _ones_where(row <= col)
        m_before = _ones_where(row < col)
        rows = len(chains) * SBQ
        strict = (lax.broadcasted_iota(jnp.int32, (rows, SBQ), 1)
                  < (lax.broadcasted_iota(jnp.int32, (rows, SBQ), 0) & (SBQ - 1)))
        q_pair, do_pair, ltot = [], [], []
        for p in range(npair):
            pl_ = slice(p * LANE, (p + 1) * LANE)
            q_all = q_ref[:, pl_]
            do_all = dy_ref[:, pl_].astype(BF16)
            q_pair.append(jnp.concatenate([jnp.where(_head_mask(h), q_all, jnp.zeros_like(q_all)) for h in range(2)], axis=0))
            do_pair.append(jnp.concatenate([jnp.where(_head_mask(h), do_all, jnp.zeros_like(do_all)) for h in range(2)], axis=0))
            ltot += [l_ref[p][:, h * 64:h * 64 + 1] for h in range(2)]
        ltot = jnp.concatenate(ltot, axis=0)

        def pair_rows(a, p):
            return a[2 * p * SBQ:2 * (p + 1) * SBQ]

        def block(kb, state, diag):
            seen, dseen, dq = state
            start = pl.multiple_of(kb * SBQ, SBQ)
            kk = [k_ref[pl.ds(start, SBQ), p * LANE:(p + 1) * LANE] for p in range(npair)]
            vv = [v_ref[pl.ds(start, SBQ), p * LANE:(p + 1) * LANE] for p in range(npair)]
            z = jnp.concatenate([_dg(q_pair[p], kk[p], NT) for p in range(npair)], axis=0) * scale
            lb = _log_sigmoid(z)
            lk = lb - z
            if diag:
                lk = jnp.where(strict, lk, 0.0)
            hi, lo = _split(lk)
            sums = _dg(jnp.concatenate([hi, lo], axis=0), m_prefix, NN)
            w = jnp.exp(lb + ((ltot - seen) - (sums[:rows] + sums[rows:])))
            if diag:
                w = jnp.where(strict, w, 0.0)
            wb = w.astype(BF16)
            da = w * jnp.concatenate([_dg(do_pair[p], vv[p], NT) for p in range(npair)], axis=0)
            dah, dal = _split(da)
            dsums = _dg(jnp.concatenate([dah, dal], axis=0), m_before, NN)
            sig = jnp.exp(lb)
            dz = (da * (1.0 - sig) - (dseen + dsums[:rows] + dsums[rows:]) * sig) * scale
            if diag:
                dz = jnp.where(strict, dz, 0.0)
            dzb = dz.astype(BF16)
            for p in range(npair):
                pl_ = slice(p * LANE, (p + 1) * LANE)
                dv_acc[pl.ds(start, SBQ), pl_] += _dg(pair_rows(wb, p), do_pair[p], TN)
                dk_acc[pl.ds(start, SBQ), pl_] += _dg(pair_rows(dzb, p), q_pair[p], TN)
            dq = dq + jnp.concatenate([_dg(pair_rows(dzb, p), kk[p], NN) for p in range(npair)], axis=0)
            seen = seen + jnp.sum(hi.astype(F32) + lo.astype(F32), axis=1, keepdims=True)
            dseen = dseen + jnp.sum(da, axis=1, keepdims=True)
            return seen, dseen, dq

        zero = (jnp.zeros((rows, 1), F32), jnp.zeros((rows, 1), F32), jnp.zeros((rows, LANE), F32))
        first = qi - n_ref[(grp // per_count) * nq + qi].astype(jnp.int32)
        state = lax.fori_loop(first, qi, lambda kb, c: block(kb, c, False), zero)
        _, _, dq = block(qi, state, True)
        for p in range(npair):
            dq_ref[:, p * LANE:(p + 1) * LANE] = jnp.where(
                _head_mask(0), dq[2 * p * SBQ:(2 * p + 1) * SBQ], dq[(2 * p + 1) * SBQ:(2 * p + 2) * SBQ]).astype(BF16)

        @pl.when(qi == nq - 1)
        def _():
            dk_ref[...] = dk_acc[...].astype(BF16)
            dv_ref[...] = dv_acc[...].astype(BF16)

        if nr:
            rider.ride(grid, riders, "finish")

    full = jax.ShapeDtypeStruct((t, 512), BF16)
    outs = pl.pallas_call(
        body, name="sb_bwd", grid=grid,
        in_specs=[_spec((SBQ, wide), lambda g, i: (i, g)),
                  _spec((t, wide), lambda g, i: (0, ngrp + g)),
                  _spec((t, wide), lambda g, i: (0, 2 * ngrp + g)),
                  _spec((SBQ, wide), lambda g, i: (i, ngrp + g)),
                  _spec((npair, SBQ, LANE), lambda g, i: (g, i, 0)),
                  pl.BlockSpec(memory_space=pltpu.SMEM)] + (rider.in_specs if nr else []),
        out_specs=[_spec((SBQ, wide), lambda g, i: (i, g)),
                   _spec((t, wide), lambda g, i: (0, g)), _spec((t, wide), lambda g, i: (0, g))]
        + (rider.out_specs if nr else []),
        out_shape=[full, full, full] + (rider.out_shape if nr else []),
        scratch_shapes=[pltpu.VMEM((t, wide), F32), pltpu.VMEM((t, wide), F32)] + (rider.scratch if nr else []),
        compiler_params=pltpu.CompilerParams(dimension_semantics=("arbitrary", "arbitrary"),
                                             vmem_limit_bytes=VMEM_LIMIT_BIG),
    )(proj_b, proj_b, proj_b, dy, ltot, nblk, *(rider.groups if nr else []))
    return outs[0], outs[1], outs[2], (rider.place(outs[3:]) if nr else [])


def _hgrn_gates(qr, fr, c0, c1):
    mx = jnp.maximum(c0, c1)
    e0, e1 = jnp.exp(c0 - mx), jnp.exp(c1 - mx)
    lb = e1 / (e0 + e1)
    sx = _sigmoid(fr)
    f = lb + (1.0 - lb) * sx
    k = (1.0 - lb) * (1.0 - sx)
    sq = _sigmoid(qr)
    return lb, sx, f, k, sq, qr * sq


def _chunk_sums(mask, x):
    n = x.shape[1]
    hi, lo = _split(x)
    both = _dg(_ones_where(mask), jnp.concatenate([hi, lo], axis=1), NN)
    return both[:, :n] + both[:, n:]


def _chunk_masks():
    row = lax.broadcasted_iota(jnp.int32, (CHUNK, CHUNK), 0)
    col = lax.broadcasted_iota(jnp.int32, (CHUNK, CHUNK), 1)
    return col <= row, col >= row


def _hgrn_fwd(proj_c, small, carry=None):
    t = proj_c.shape[1]
    nc = t // CHUNK
    nh = D // HD

    hps = HGRN_HPS
    wide = hps * HD
    grid = (nh // hps, nc)
    nr = carry.n if carry is not None else 0

    def body(*refs):
        p_ref, c0_ref, c1_ref, gam_ref = refs[:4]
        o_ref, y_ref, sst_ref = refs[4 + nr:7 + nr]
        st_ref = refs[7 + 2 * nr]
        riders = (refs[4:4 + nr], refs[7 + nr:7 + 2 * nr], refs[-2], refs[-1]) if nr else None
        if nr:
            carry.ride(grid, riders, "start")
        c = pl.program_id(1)

        @pl.when(c == 0)
        def _():
            st_ref[...] = jnp.zeros_like(st_ref)

        _, _, f_all, k_all, _, q_all = _hgrn_gates(p_ref[0], p_ref[1], c0_ref[...], c1_ref[...])
        tril, _ = _chunk_masks()
        b_all = _chunk_sums(tril, jnp.log(f_all))
        for j in range(hps):
            ln = slice(j * HD, (j + 1) * HD)
            st0 = st_ref[j]
            sst_ref[j] = st0
            v, g = p_ref[2, :, ln], p_ref[3, :, ln]
            q, k, b = q_all[:, ln], k_all[:, ln], b_all[:, ln]
            bm = b[CHUNK // 2 - 1:CHUNK // 2]
            bl = b[CHUNK - 1:CHUNK]
            qd = q * jnp.exp(b)
            qt = q * jnp.exp(b - bm)
            kt = k * jnp.exp(bm - b)
            kl = k * jnp.exp(bl - b)
            vb = v.astype(BF16)
            att = jnp.where(tril, _dot3(qt, kt, NT), 0.0)
            o = _dg(qd.astype(BF16), st0.astype(BF16), NT) + _dg(att.astype(BF16), vb, NN)
            st_ref[j] = st0 * jnp.exp(bl) + _dg(vb, kl.astype(BF16), TN)
            o_ref[:, ln] = o
            r = lax.rsqrt(jnp.mean(o * o, axis=-1, keepdims=True) + RMS_EPS)
            y_ref[:, ln] = (o * r * gam_ref[...] * (g * _sigmoid(g))).astype(BF16)
        if nr:
            carry.ride(grid, riders, "finish")

    outs = pl.pallas_call(
        body, name="hgrn_fwd", grid=grid,
        in_specs=[_spec((4, CHUNK, wide), lambda h, c: (0, c, h)),
                  _spec((None, 1, wide), lambda h, c: (R_CLB, 0, h)),
                  _spec((None, 1, wide), lambda h, c: (R_CLB + 1, 0, h)),
                  _spec((None, 1, HD), lambda h, c: (R_GAM, 0, 0))] + (carry.in_specs if nr else []),
        out_specs=[_spec((CHUNK, wide), lambda h, c: (c, h)), _spec((CHUNK, wide), lambda h, c: (c, h)),
                   _spec((None, hps, HD, HD), lambda h, c: (c, h, 0, 0))] + (carry.out_specs if nr else []),
        out_shape=[jax.ShapeDtypeStruct((t, D), F32), jax.ShapeDtypeStruct((t, D), BF16),
                   jax.ShapeDtypeStruct((nc, nh, HD, HD), F32)] + (carry.out_shape if nr else []),
        scratch_shapes=[pltpu.VMEM((hps, HD, HD), F32)] + (carry.scratch if nr else []),
        compiler_params=_params(("arbitrary", "arbitrary")),
    )(proj_c, small, small, small, *(carry.groups if nr else []))
    return outs[0], outs[1], outs[2], (carry.place(outs[3:]) if nr else [])


def _hgrn_bwd(proj_c, small, o, sst, dyc, rider=None):
    t = proj_c.shape[1]
    nc = t // CHUNK
    nh = D // HD

    hps = HGRN_HPS
    wide = hps * HD
    ng = nh // hps
    grid = (ng, nc)
    nr = rider.n if rider is not None else 0

    def body(*refs):
        p_ref, c0_ref, c1_ref, gam_ref, o_ref, sst_ref, dy_ref = refs[:7]
        dp_ref, dclb_ref, dgam_ref = refs[7 + nr:10 + nr]
        dst_ref, dlb_acc, dgam_acc = refs[10 + 2 * nr:13 + 2 * nr]
        riders = (refs[7:7 + nr], refs[10 + nr:10 + 2 * nr], refs[-2], refs[-1]) if nr else None
        if nr:
            rider.ride(grid, riders, "start")
        group = pl.program_id(0)
        step = pl.program_id(1)

        @pl.when(step == 0)
        def _():
            dst_ref[...] = jnp.zeros_like(dst_ref)
            dlb_acc[...] = jnp.zeros_like(dlb_acc)

        @pl.when((step == 0) & (group == 0))
        def _():
            dgam_acc[...] = jnp.zeros_like(dgam_acc)

        gam = gam_ref[...]
        qr_all = p_ref[0]
        lb_all, sx_all, f_all, k_all, sq_all, q_all = _hgrn_gates(qr_all, p_ref[1], c0_ref[...], c1_ref[...])
        tril, triu = _chunk_masks()
        b_all = _chunk_sums(tril, jnp.log(f_all))
        dq_parts, dk_parts, db_parts, dgam_parts = [], [], [], []
        for j in range(hps):
            ln = slice(j * HD, (j + 1) * HD)
            st0 = sst_ref[j]
            dst1 = dst_ref[j]
            v, g = p_ref[2, :, ln], p_ref[3, :, ln]
            q, k, b = q_all[:, ln], k_all[:, ln], b_all[:, ln]
            bm = b[CHUNK // 2 - 1:CHUNK // 2]
            bl = b[CHUNK - 1:CHUNK]
            eb = jnp.exp(b)
            e_qt = jnp.exp(b - bm)
            e_kt = jnp.exp(bm - b)
            e_kl = jnp.exp(bl - b)
            e_bl = jnp.exp(bl)
            qd, qt, kt, kl = q * eb, q * e_qt, k * e_kt, k * e_kl
            ov = o_ref[:, ln]
            r = lax.rsqrt(jnp.mean(ov * ov, axis=-1, keepdims=True) + RMS_EPS)
            oh = ov * r
            sg = _sigmoid(g)
            dy = dy_ref[:, ln]
            dp_ref[3, :, ln] = (dy * oh * gam * (sg * (1.0 + g * (1.0 - sg)))).astype(BF16)
            dyv = dy * (g * sg)
            gdy = dyv * gam
            do = (gdy - oh * jnp.mean(gdy * oh, axis=-1, keepdims=True)) * r
            dob, vb = do.astype(BF16), v.astype(BF16)
            st0b, dst1b = st0.astype(BF16), dst1.astype(BF16)
            st1 = st0 * e_bl + _dg(vb, kl.astype(BF16), TN)
            att = jnp.where(tril, _dot3(qt, kt, NT), 0.0)
            datt = jnp.where(tril, _dg(dob, vb, NT), 0.0)
            dv = _dg(att.astype(BF16), dob, TN) + _dg(kl.astype(BF16), dst1b, NT)
            dq = _dot3(datt, kt, NN) * e_qt + _dg(dob, st0b, NN) * eb
            dk = _dot3(datt, qt, TN) * e_kt + _dg(vb, dst1b, NN) * e_kl
            db = q * dq - k * dk
            last = lax.broadcasted_iota(jnp.int32, (CHUNK, 1), 0) == CHUNK - 1
            db = db + jnp.where(last, jnp.sum(dst1 * st1, axis=0, keepdims=True), 0.0)
            dst_ref[j] = dst1 * e_bl + _dg(dob, qd.astype(BF16), TN)
            dp_ref[2, :, ln] = dv.astype(BF16)
            dq_parts.append(dq)
            dk_parts.append(dk)
            db_parts.append(db)
            dgam_parts.append(jnp.sum(dyv * oh, axis=0, keepdims=True))

        dq_all, dk_all = jnp.concatenate(dq_parts, axis=1), jnp.concatenate(dk_parts, axis=1)
        dlf = _chunk_sums(triu, jnp.concatenate(db_parts, axis=1))
        dp_ref[0] = (dq_all * (sq_all * (1.0 + qr_all * (1.0 - sq_all)))).astype(BF16)
        tmp = dlf / f_all - dk_all
        dp_ref[1] = (tmp * (1.0 - lb_all) * sx_all * (1.0 - sx_all)).astype(BF16)
        dlb_acc[...] += jnp.sum((1.0 - sx_all) * tmp, axis=0, keepdims=True)
        dgam_acc[...] += functools.reduce(lambda a, b: a + b, dgam_parts)

        @pl.when(step == nc - 1)
        def _():
            d1 = dlb_acc[...] * lb_all * (1.0 - lb_all)
            dclb_ref[...] = jnp.where(lax.broadcasted_iota(jnp.int32, (2, wide), 0) == 0, -d1, d1)

        @pl.when((step == nc - 1) & (group == ng - 1))
        def _():
            dgam_ref[...] = dgam_acc[...]

        if nr:
            rider.ride(grid, riders, "finish")

    rev = lambda h, s: (nc - 1 - s, h)
    outs = pl.pallas_call(
        body, name="hgrn_bwd", grid=grid,
        in_specs=[_spec((4, CHUNK, wide), lambda h, s: (0, nc - 1 - s, h)),
                  _spec((None, 1, wide), lambda h, s: (R_CLB, 0, h)),
                  _spec((None, 1, wide), lambda h, s: (R_CLB + 1, 0, h)),
                  _spec((None, 1, HD), lambda h, s: (R_GAM, 0, 0)),
                  _spec((CHUNK, wide), rev),
                  _spec((None, hps, HD, HD), lambda h, s: (nc - 1 - s, h, 0, 0)),
                  _spec((CHUNK, wide), rev)] + (rider.in_specs if nr else []),
        out_specs=[_spec((4, CHUNK, wide), lambda h, s: (0, nc - 1 - s, h)),
                   _spec((2, wide), lambda h, s: (0, h)),
                   _spec((1, HD), lambda h, s: (0, 0))] + (rider.out_specs if nr else []),
        out_shape=[jax.ShapeDtypeStruct((4, t, D), BF16), jax.ShapeDtypeStruct((2, D), F32),
                   jax.ShapeDtypeStruct((1, HD), F32)] + (rider.out_shape if nr else []),
        scratch_shapes=[pltpu.VMEM((hps, HD, HD), F32), pltpu.VMEM((1, wide), F32), pltpu.VMEM((1, HD), F32)]
        + (rider.scratch if nr else []),
        compiler_params=_params(("arbitrary", "arbitrary")),
    )(proj_c, small, small, small, o, sst, dyc, *(rider.groups if nr else []))
    return outs[0], outs[1], outs[2], (rider.place(outs[3:]) if nr else [])


def _adamw(name, w, grads, m, v):
    nl = len(grads)
    rows, cols = grads[0].shape
    br = rows
    for cand in (512, 352, 256):
        if rows % cand == 0:
            br = cand
            break
    nb = rows // br
    c1 = 1.0 - ADAM_B1 ** ADAM_STEP
    c2 = 1.0 - ADAM_B2 ** ADAM_STEP

    def body(w_ref, m_ref, v_ref, *refs):
        g_refs, (d_ref, mo_ref, vo_ref, go_ref) = refs[:nl], refs[nl:]
        layer = pl.program_id(0)
        gv = g_refs[0][...]
        for k in range(1, nl):
            gv = jnp.where(layer == k, g_refs[k][...], gv)
        mn = ADAM_B1 * m_ref[...] + (1.0 - ADAM_B1) * gv
        vn = ADAM_B2 * v_ref[...] + (1.0 - ADAM_B2) * (gv * gv)
        mo_ref[...] = mn
        vo_ref[...] = vn
        go_ref[...] = gv
        d_ref[...] = -ADAM_LR * ((mn / c1) / (jnp.sqrt(vn / c2) + ADAM_EPS) + ADAM_WD * w_ref[...])

    blk = _spec((br, cols), lambda l, i: (l * nb + i, 0))
    g_specs = [_spec((br, cols), lambda l, i, k=k: (jnp.where(l == k, i, 0), 0)) for k in range(nl)]
    shape = jax.ShapeDtypeStruct((nl * rows, cols), F32)
    return pl.pallas_call(
        body, name=name, grid=(nl, nb), in_specs=[blk] * 3 + g_specs, out_specs=[blk] * 4, out_shape=[shape] * 4,
        compiler_params=_params(("arbitrary", "arbitrary")),
    )(w, m, v, *grads)


def _place():
    x, y, c = lax.axis_index("x"), lax.axis_index("y"), lax.axis_index("c")
    chips = [(1 - x, y), (x, 1 - y), (1 - x, 1 - y)]
    return x, y, c, chips


class _Rider:
    n = 0
    relay = None

    def ride(self, grid, refs, when):
        if not self.n:
            return
        ids = [pl.program_id(a) for a in range(len(grid))]
        first = functools.reduce(jnp.logical_and, [i == 0 for i in ids])
        last = functools.reduce(jnp.logical_and, [i == g - 1 for i, g in zip(ids, grid)])
        phases = [(first, self.start), (last, self.relay)] if when == "start" else [(last, self.finish)]
        for cond, phase in phases:
            if phase is not None:
                pl.when(cond)(functools.partial(phase, *refs))


class _Gather(_Rider):
    PER_GROUP = 7

    def __init__(self, groups):
        self.groups = list(groups)
        self.n = len(self.groups)
        self.in_specs = [ANY] * self.n
        self.out_specs = [ANY] * self.n
        self.out_shape = [jax.ShapeDtypeStruct((NSH,) + g.shape, g.dtype) for g in self.groups]
        sems = pltpu.SemaphoreType.DMA((self.PER_GROUP * self.n,))
        self.scratch = [sems, sems] if self.n else []

    def _copies(self, ins, outs, send, recv):
        x, y, c, chips = _place()
        sibling = (x, y, 1 - c)

        def half(gi, chip, hc):
            return outs[gi].at[2 * chip[0] + chip[1], hc]

        def copy(gi, k, src, dst, to):
            sem = self.PER_GROUP * gi + k
            return pltpu.make_async_remote_copy(src_ref=src, dst_ref=dst, send_sem=send.at[sem], recv_sem=recv.at[sem],
                                                device_id=to, device_id_type=MESH)

        pairs = [(gi, j, chip) for gi in range(self.n) for j, chip in enumerate(chips)]
        first = [copy(gi, j, ins[gi].at[c], half(gi, (x, y), c), (*chip, c)) for gi, j, chip in pairs]
        first += [copy(gi, 6, ins[gi], outs[gi].at[2 * x + y], sibling) for gi in range(self.n)]
        landed = [copy(gi, j, half(gi, chip, c), half(gi, chip, c), sibling) for gi, j, chip in pairs]
        relay = [copy(gi, 3 + j, half(gi, chip, c), half(gi, chip, c), sibling) for gi, j, chip in pairs]
        relayed = [copy(gi, 3 + j, half(gi, chip, 1 - c), half(gi, chip, 1 - c), sibling) for gi, j, chip in pairs]
        relayed += [copy(gi, 6, ins[gi], outs[gi].at[2 * x + y], sibling) for gi in range(self.n)]
        return first, landed, relay, relayed

    def start(self, ins, outs, send, recv):
        for cp in self._copies(ins, outs, send, recv)[0]:
            cp.start()

    def relay(self, ins, outs, send, recv):
        _, landed, relay, _ = self._copies(ins, outs, send, recv)
        for arrived, onward in zip(landed, relay):
            arrived.wait_recv()
            onward.start()

    def finish(self, ins, outs, send, recv):
        first, _, relay, relayed = self._copies(ins, outs, send, recv)
        for cp in relayed:
            cp.wait_recv()
        for cp in first + relay:
            cp.wait_send()

    def place(self, outs):
        return list(outs)


def _alone(name, rider):
    n = rider.n

    def body(*refs):
        parts = (refs[:n], refs[n:2 * n], refs[2 * n], refs[2 * n + 1])
        rider.start(*parts)
        if rider.relay is not None:
            rider.relay(*parts)
        rider.finish(*parts)

    outs = pl.pallas_call(
        body, name=name, in_specs=rider.in_specs, out_specs=rider.out_specs, out_shape=rider.out_shape,
        scratch_shapes=rider.scratch, compiler_params=pltpu.CompilerParams(has_side_effects=True),
    )(*rider.groups)
    return rider.place(outs)


class _Swap(_Rider):
    def __init__(self, slots):
        self.slots = list(slots)
        self.groups = [buf for buf, _, _ in self.slots]
        self.n = len(self.slots)
        self.in_specs = [ANY] * self.n
        self.out_specs = [ANY] * self.n
        self.out_shape = [jax.ShapeDtypeStruct((NSH, rows // 2, buf.shape[2]), buf.dtype) for buf, _, rows in self.slots]
        self.scratch = [pltpu.SemaphoreType.DMA((self.n,)), pltpu.SemaphoreType.DMA((self.n,))]

    def _copies(self, ins, outs, send, recv):
        x, y, c, _ = _place()
        cps = []
        for i, (_, row0, rows) in enumerate(self.slots):
            half = rows // 2
            src = ins[i].at[:, pl.ds(pl.multiple_of(row0 + (1 - c) * half, 16), half)]
            cps.append(pltpu.make_async_remote_copy(src_ref=src, dst_ref=outs[i], send_sem=send.at[i],
                                                    recv_sem=recv.at[i], device_id=(x, y, 1 - c), device_id_type=MESH))
        return cps

    def start(self, ins, outs, send, recv):
        for cp in self._copies(ins, outs, send, recv):
            cp.start()

    def finish(self, ins, outs, send, recv):
        for cp in self._copies(ins, outs, send, recv):
            cp.wait()

    def place(self, outs):
        return list(outs)


class _Share(_Rider):
    def __init__(self, arrays):
        self.groups = list(arrays)
        self.n = len(self.groups)
        self.in_specs = [ANY] * self.n
        self.out_specs = [ANY] * self.n
        self.out_shape = [jax.ShapeDtypeStruct((2,) + g.shape, g.dtype) for g in self.groups]
        self.scratch = [pltpu.SemaphoreType.DMA((self.n,)), pltpu.SemaphoreType.DMA((self.n,))]

    def _copies(self, ins, outs, send, recv, half):
        x, y, c, _ = _place()
        return [pltpu.make_async_remote_copy(src_ref=ins[gi], dst_ref=outs[gi].at[c if half == "mine" else 1 - c],
                                             send_sem=send.at[gi], recv_sem=recv.at[gi], device_id=(x, y, 1 - c),
                                             device_id_type=MESH) for gi in range(self.n)]

    def start(self, ins, outs, send, recv):
        for cp in self._copies(ins, outs, send, recv, "mine"):
            cp.start()

    def finish(self, ins, outs, send, recv):
        for cp in self._copies(ins, outs, send, recv, "theirs"):
            cp.wait_recv()
        for cp in self._copies(ins, outs, send, recv, "mine"):
            cp.wait_send()

    def place(self, outs):
        c = lax.axis_index("c")
        return [lax.dynamic_update_index_in_dim(o, g, c, 0) for o, g in zip(outs, self.groups)]


class _Send(_Rider):
    def __init__(self, arrays):
        self.groups = list(arrays)
        self.n = len(self.groups)
        self.in_specs = [ANY] * self.n
        self.out_specs = [ANY] * self.n
        self.out_shape = [jax.ShapeDtypeStruct((3,) + g.shape[1:], g.dtype) for g in self.groups]
        self.scratch = [pltpu.SemaphoreType.DMA((3 * self.n,)), pltpu.SemaphoreType.DMA((3 * self.n,))]

    def _copies(self, ins, outs, send, recv):
        x, y, c, chips = _place()
        return [pltpu.make_async_remote_copy(src_ref=ins[gi].at[2 * chip[0] + chip[1]], dst_ref=outs[gi].at[j],
                                             send_sem=send.at[3 * gi + j], recv_sem=recv.at[3 * gi + j],
                                             device_id=(*chip, c), device_id_type=MESH)
                for gi in range(self.n) for j, chip in enumerate(chips)]

    def start(self, ins, outs, send, recv):
        for cp in self._copies(ins, outs, send, recv):
            cp.start()

    def finish(self, ins, outs, send, recv):
        for cp in self._copies(ins, outs, send, recv):
            cp.wait()

    def place(self, outs):
        return list(outs)


def _pair_sum(name, slots, got, c_idx):
    n = len(slots)
    in_specs, out_specs, out_shape, operands = [], [], [], []
    for (buf, row0, rows), g in zip(slots, got):
        hb, cols = rows // 4, buf.shape[2]
        in_specs += [pl.BlockSpec((None, hb, cols), lambda q, i, cr, r=row0 // hb: (q, r + 2 * cr[0] + i, 0)),
                     pl.BlockSpec((None, hb, cols), lambda q, i, cr: (q, i, 0))]
        out_specs.append(pl.BlockSpec((None, hb, cols), lambda q, i, cr: (q, i, 0)))
        out_shape.append(jax.ShapeDtypeStruct(g.shape, BF16))
        operands += [buf, g]

    def body(c_ref, *refs):
        for i in range(n):
            refs[2 * n + i][...] = (refs[2 * i][...].astype(F32) + refs[2 * i + 1][...].astype(F32)).astype(BF16)

    return pl.pallas_call(
        body, name=name,
        grid_spec=pltpu.PrefetchScalarGridSpec(num_scalar_prefetch=1, grid=(NSH, 2), in_specs=in_specs, out_specs=out_specs),
        out_shape=out_shape, compiler_params=_params(("parallel", "parallel")),
    )(c_idx, *operands)


def _owner_sum(name, pairs, got, p_idx):
    n = len(pairs)
    in_specs, out_specs, out_shape, operands = [], [], [], []
    for own, g in zip(pairs, got):
        _, rows, cols = own.shape
        hb = rows // 2
        in_specs += [pl.BlockSpec((None, hb, cols), lambda i, pr: (pr[0], i, 0)),
                     pl.BlockSpec((3, hb, cols), lambda i, pr: (0, i, 0))]
        out_specs.append(pl.BlockSpec((hb, cols), lambda i, pr: (i, 0)))
        out_shape.append(jax.ShapeDtypeStruct((rows, cols), F32))
        operands += [own, g]

    def body(p_ref, *refs):
        for i in range(n):
            a_ref, b_ref = refs[2 * i], refs[2 * i + 1]
            refs[2 * n + i][...] = ((a_ref[...].astype(F32) + b_ref[0].astype(F32)) + b_ref[1].astype(F32)) + b_ref[2].astype(F32)

    return pl.pallas_call(
        body, name=name,
        grid_spec=pltpu.PrefetchScalarGridSpec(num_scalar_prefetch=1, grid=(2,), in_specs=in_specs, out_specs=out_specs),
        out_shape=out_shape, compiler_params=_params(("parallel",)),
    )(p_idx, *operands)


def _sum_small(slab):
    def body(in_ref, out_ref, all_ref, send, recv):
        x, y, c, _ = _place()
        me = 4 * x + 2 * y + c
        all_ref[me] = in_ref[...]
        cps = []
        for k in range(1, 8):
            peer = (x ^ (k >> 2), y ^ ((k >> 1) & 1), c ^ (k & 1))
            cps.append(pltpu.make_async_remote_copy(src_ref=in_ref, dst_ref=all_ref.at[me], send_sem=send.at[k - 1],
                                                    recv_sem=recv.at[k - 1], device_id=peer, device_id_type=MESH))
        for cp in cps:
            cp.start()
        for cp in cps:
            cp.wait()
        total = all_ref[0]
        for d in range(1, 8):
            total = total + all_ref[d]
        out_ref[...] = total

    return pl.pallas_call(
        body, name="sum_small",
        in_specs=[pl.BlockSpec(memory_space=pltpu.VMEM)], out_specs=pl.BlockSpec(memory_space=pltpu.VMEM),
        out_shape=jax.ShapeDtypeStruct(slab.shape, F32),
        scratch_shapes=[pltpu.VMEM((8,) + slab.shape, F32), pltpu.SemaphoreType.DMA((7,)), pltpu.SemaphoreType.DMA((7,))],
        compiler_params=pltpu.CompilerParams(has_side_effects=True),
    )(slab)


FFNS = ("pre0", "post0", "pre1", "post1")
W_SHAPES = dict({f + "_gu": (NSH, 2, FS, D) for f in FFNS}, **{f + "_d": (NSH, FS, D) for f in FFNS},
                ab_in=(NSH, D, 768), ab_out=(NSH, 256, D), conv=(NSH, 2, 8, LANE), c_in=(NSH, D, D), c_out=(NSH, 256, D))
CARRIED = dict(pre0_norm=("pre0_gu",), pre0_up=("pre0_d",), pre0_down=("ab_in", "conv"), ab_proj_a=("ab_out",),
               c_proj=("post1_d",),
               sb_fwd=("post0_gu", "post0_d"), post0_up=("pre1_gu",), post0_down=("pre1_d",),
               pre1_up=("c_in", "c_out"), hgrn_fwd=("post1_gu",))


FFN_SLOTS = dict(pre0=(0, 1, 0), pre1=(2, 3, 1), post0=(4, 5, 2), post1=(6, 7, 3))
GRAD_SLOTS = dict(ab_out=("b", B_ABOUT, 256), c_in=("b", B_CIN, D), c_out=("b", B_COUT, 256), ab_in=("c", 0, D))
for _f, (_g, _u, _d) in FFN_SLOTS.items():
    GRAD_SLOTS.update({_f + "_g": ("a", _g * FS, FS), _f + "_u": ("a", _u * FS, FS), _f + "_d": ("b", _d * FS, FS)})
REDUCE_STAGES = dict(x=("post1_g", "post1_u", "post1_d", "c_out"), w=("c_in", "pre1_d"),
                     y=("pre1_g", "pre1_u", "post0_g", "post0_u", "post0_d", "ab_out"),
                     z1=("ab_in",), z2=("pre0_g", "pre0_u", "pre0_d"))


def _local_step(x, target, weights, small, shards=None, place=None):
    t = x.shape[0]
    tm = min(MM_TILE, t)
    tw = min(WGRAD_TILE, t)
    nt = t // tm
    tok_si = _spec((tm, D), lambda s, i: (i, 0))
    w = dict(weights)
    reduced = {}

    def stage_slots(stage, buffers):
        return [(buffers[GRAD_SLOTS[n][0]],) + GRAD_SLOTS[n][1:] for n in REDUCE_STAGES[stage]]

    def swap_rider(stage, buffers):
        return _Swap(stage_slots(stage, buffers)) if place is not None else None

    def reduce_start(stage, buffers, swapped=None):
        if place is None:
            return [], None
        slots = stage_slots(stage, buffers)
        if swapped is None:
            swapped = _alone("grad_swap_" + stage, _Swap(slots))
        pairs = _pair_sum("grad_pair_sum_" + stage, slots, swapped, place[0])
        return pairs, _Send(pairs)

    def reduce_end(stage, pairs, landed):
        if place is None:
            return None
        mine = _owner_sum("grad_owner_sum_" + stage, pairs, landed, place[1])
        return _Share(mine)

    def shared(stage, landed):
        if place is not None:
            reduced.update(zip(REDUCE_STAGES[stage], landed))

    def carried(kernel_name):
        names = [n for n in CARRIED[kernel_name] if n not in w]
        return names, (_Gather([shards[n] for n in names]) if names else None)

    def land(names, arrays):
        for n, a in zip(names, arrays):
            w[n] = a.reshape(W_SHAPES[n])

    def ffn_forward(tag, h, hn, next_row):
        names, gather = carried(tag + "_up") if tag + "_up" in CARRIED else ([], None)
        s_up, s_gate, a, got = _ffn_up(tag + "_up", hn, w[tag + "_gu"], gather)
        land(names, got)
        names, gather = carried(tag + "_down") if tag + "_down" in CARRIED else ([], None)
        out = _ffn_down(tag + "_down", a, w[tag + "_d"], h, gather, (small, next_row) if next_row is not None else None)
        if gather is not None:
            out, got = out
            land(names, got)
        out, hn_next = out if next_row is not None else (out, None)
        return out, hn_next, (h, hn, s_up, s_gate, a)

    def out_proj(name, y, w_out, h, next_row):
        return _mm(name, [(y, _spec((tm, 256), lambda i, k: (i, k)), w_out, _spec((None, 256, D), lambda i, k: (k, 0, 0)))],
                   grid=(nt, NSH), o_shape=(t, D), o_dtype=F32, o_spec=_spec((tm, D), lambda i, k: (i, 0)),
                   dims=NN, kaxis=1, nk=NSH, acc_shape=(tm, D), res=(h, _spec((tm, D), lambda i, k: (i, 0))),
                   norm=(small, next_row))

    def out_proj_bwd(tag, dhb, y, w_out, blk, grad_b, make_rider=None):
        grad_b = _wgrad(tag + "_dwout", y, _spec((tw, 256), lambda s, k: (k, s)), dhb, _spec((tw, D), lambda s, k: (k, 0)),
                        256, D, t, tw, grad_b, blk)
        rider = make_rider(grad_b) if make_rider is not None else None
        dy = _mm(tag + "_dy", [(dhb, tok_si, w_out, _spec((None, 256, D), lambda s, i: (s, 0, 0)))],
                 grid=(NSH, nt), o_shape=(t, D), o_dtype=F32, o_spec=_spec((tm, 256), lambda s, i: (i, s)),
                 dims=NT, kaxis=1, nk=1, carry=rider)
        dy, landed = dy if rider is not None else (dy, None)
        return dy, grad_b, landed

    h0 = x
    names, gather = carried("pre0_norm")
    hn, got = _norm_fwd("pre0_norm", h0, small, R_PRE, gather)
    land(names, got)
    h1, hn_ab, pre0 = ffn_forward("pre0", h0, hn, R_MIX)
    def carrying(kernel_name, make):
        names, gather = carried(kernel_name)
        out = make(gather)
        if gather is not None:
            out, got = out
            land(names, got)
        return out

    proj_a = carrying("ab_proj_a", lambda gather: _mm(
        "ab_proj_a", [(hn_ab, tok_si, w["ab_in"], _spec((None, D, 768), lambda s, i: (s, 0, 0)))],
        grid=(2, nt), o_shape=(t, 1536), o_dtype=F32, o_spec=_spec((tm, 768), lambda s, i: (i, s)),
        dims=NN, kaxis=1, nk=1, carry=gather))
    proj_b = _mm("ab_proj_b", [(hn_ab, tok_si, w["ab_in"], _spec((None, D, 768), lambda s, i: (s + 2, 0, 0)))],
                 grid=(2, nt), o_shape=(t, 1536), o_dtype=BF16, o_spec=_spec((tm, 768), lambda s, i: (i, s)),
                 dims=NN, kaxis=1, nk=1)
    y_a = _conv_fwd(proj_a, w["conv"])
    names, gather = carried("sb_fwd")
    y_b, ltot, nblk, got = _sb_fwd(proj_b, gather)
    land(names, got)
    y_ab = jnp.concatenate([y_a, y_b], axis=1)
    h2, hn = out_proj("ab_out", y_ab, w["ab_out"], h1, R_POST)
    h3, hn, post0 = ffn_forward("post0", h2, hn, R_PRE + 1)
    h4, hn_c, pre1 = ffn_forward("pre1", h3, hn, R_MIX + 1)
    proj_c = carrying("c_proj", lambda gather: _mm(
        "c_proj", [(hn_c, tok_si, w["c_in"], _spec((None, D, D), lambda s, i: (s, 0, 0)))],
        grid=(NSH, nt), o_shape=(NSH, t, D), o_dtype=F32, o_spec=_spec((None, tm, D), lambda s, i: (s, i, 0)),
        dims=NN, kaxis=1, nk=1, carry=gather))
    names, gather = carried("hgrn_fwd")
    o_c, y_c, sst, got = _hgrn_fwd(proj_c, small, gather)
    land(names, got)
    h5, hn = out_proj("c_out", y_c, w["c_out"], h4, R_POST + 1)
    h6, _, post1 = ffn_forward("post1", h5, hn, None)
    dh, dhb, d_fin, loss = _final_loss(h6, small, target)

    dh, dhb, grad_a, grad_b, dn_post1, _, _ = _ffn_backward("post1", dh, dhb, *post1, w["post1_gu"], w["post1_d"],
                                                            *FFN_SLOTS["post1"], small, R_POST + 1, 8 * FS, B_ROWS)
    dy_c, grad_b, swapped = out_proj_bwd("c", dhb, y_c, w["c_out"], B_COUT // 256, grad_b,
                                         lambda buf_b: swap_rider("x", dict(a=grad_a, b=buf_b)))
    pairs, rider = reduce_start("x", dict(a=grad_a, b=grad_b), swapped)
    dproj_c, d_clb, d_gam, landed = _hgrn_bwd(proj_c, small, o_c, sst, dy_c, rider)
    share = reduce_end("x", pairs, landed)
    grad_b = _wgrad("c_dwin", hn_c, _spec((tw, D), lambda s, k: (k, 0)), dproj_c, _spec((None, tw, D), lambda s, k: (s, k, 0)),
                    D, D, t, tw, grad_b, B_CIN // D, carry=share)
    if share is not None:
        grad_b, landed = grad_b
        shared("x", landed)
    dh, dhb, dn_mix1, _ = _dhn_norm("c_dhn", [(dproj_c, _spec((None, tm, D), lambda i, k: (k, i, 0)),
                                               w["c_in"], _spec((None, D, D), lambda i, k: (k, 0, 0)))],
                                    NT, h4, small, R_MIX + 1, dh)
    stage_w = {}

    def w_swap(buf_a, buf_b):
        return swap_rider("w", dict(a=buf_a, b=buf_b))

    def w_send(buf_a, buf_b, swapped):
        stage_w["pairs"], rider = reduce_start("w", dict(a=buf_a, b=buf_b), swapped)
        return rider

    on = place is not None
    dh, dhb, grad_a, grad_b, dn_pre1, _, landed = _ffn_backward(
        "pre1", dh, dhb, *pre1, w["pre1_gu"], w["pre1_d"], *FFN_SLOTS["pre1"], small, R_PRE + 1, grad_a, grad_b,
        w_swap if on else None, w_send if on else None)
    share = reduce_end("w", stage_w.get("pairs"), landed)
    dh, dhb, grad_a, grad_b, dn_post0, landed, _ = _ffn_backward(
        "post0", dh, dhb, *post0, w["post0_gu"], w["post0_d"], *FFN_SLOTS["post0"], small, R_POST, grad_a, grad_b,
        (lambda buf_a, buf_b: share) if on else None)
    shared("w", landed)
    dy_ab, grad_b, _ = out_proj_bwd("ab", dhb, y_ab, w["ab_out"], B_ABOUT // 256, grad_b)
    dab, dac, dax, d_conv, swapped = _conv_bwd(proj_a, w["conv"], dy_ab, swap_rider("y", dict(a=grad_a, b=grad_b)))
    pairs, rider = reduce_start("y", dict(a=grad_a, b=grad_b), swapped)
    dq, dk, dv, landed = _sb_bwd(proj_b, dy_ab, ltot, nblk, rider)
    share = reduce_end("y", pairs, landed)
    dproj_ab = jnp.concatenate([dab, dac, dax, dq, dk, dv], axis=1)
    grad_c = _wgrad("ab_dwin", hn_ab, _spec((tw, D), lambda s, k: (k, 0)), dproj_ab, _spec((tw, 768), lambda s, k: (k, s)),
                    D, 768, t, tw, D, 0, carry=share)
    if share is not None:
        grad_c, landed = grad_c
        shared("y", landed)
    dh, dhb, dn_mix0, _ = _dhn_norm("ab_dhn", [(dproj_ab, _spec((tm, 768), lambda i, k: (i, k)),
                                                w["ab_in"], _spec((None, D, 768), lambda i, k: (k, 0, 0)))],
                                    NT, h1, small, R_MIX, dh)
    last = {}

    def z1_send(buf_a, buf_b):
        last["z1"], rider = reduce_start("z1", dict(b=buf_b, c=grad_c))
        return rider

    def z2_send(buf_a, buf_b, _):
        last["z2"], rider = reduce_start("z2", dict(a=buf_a, b=buf_b))
        return rider

    dh, dhb, grad_a, grad_b, dn_pre0, landed, landed_own = _ffn_backward(
        "pre0", dh, dhb, *pre0, w["pre0_gu"], w["pre0_d"], *FFN_SLOTS["pre0"], small, R_PRE, grad_a, grad_b,
        z1_send if on else None, z2_send if on else None)
    if place is not None:
        mine = (_owner_sum("grad_owner_sum_z1", last["z1"], landed, place[1])
                + _owner_sum("grad_owner_sum_z2", last["z2"], landed_own, place[1]))
        reduced.update(zip(REDUCE_STAGES["z1"] + REDUCE_STAGES["z2"], _alone("grad_share_z", _Share(mine))))
    zero = jnp.zeros((1, D), F32)
    conv_rows = jnp.pad(jnp.transpose(d_conv[:, :3, :], (1, 0, 2)).reshape(3, 512), ((0, 0), (0, D - 512)))
    small_grad = jnp.concatenate([
        dn_pre0, dn_pre1, dn_mix0, dn_mix1, dn_post0, dn_post1, d_clb, d_fin,
        jnp.pad(d_gam, ((0, 0), (0, D - HD))), conv_rows,
        jnp.pad(loss, ((0, 0), (0, D - 1))), zero, zero], axis=0)
    return dh, grad_a, grad_b, grad_c, small_grad, reduced


def _small_slab(rows):
    parts = [jnp.pad(r.astype(F32), ((0, 0), (0, D - r.shape[1]))) for r in rows]
    slab = jnp.concatenate(parts, axis=0)
    return jnp.pad(slab, ((0, SMALL_ROWS - slab.shape[0]), (0, 0)))


def kernel(x, ffn_pre_norm, ffn_pre_w_gate, ffn_pre_w_up, ffn_pre_w_down, mix_norm, ffn_post_norm, ffn_post_w_gate, ffn_post_w_up, ffn_post_w_down, ab_w_in, ab_conv_w, ab_w_out, c_w_in, c_lower_bounds, c_out_norm, c_w_out, final_norm, loss_target, m_ffn_pre_norm, m_ffn_pre_w_gate, m_ffn_pre_w_up, m_ffn_pre_w_down, m_mix_norm, m_ffn_post_norm, m_ffn_post_w_gate, m_ffn_post_w_up, m_ffn_post_w_down, m_ab_w_in, m_ab_conv_w, m_ab_w_out, m_c_w_in, m_c_lower_bounds, m_c_out_norm, m_c_w_out, m_final_norm, v_ffn_pre_norm, v_ffn_pre_w_gate, v_ffn_pre_w_up, v_ffn_pre_w_down, v_mix_norm, v_ffn_post_norm, v_ffn_post_w_gate, v_ffn_post_w_up, v_ffn_post_w_down, v_ab_w_in, v_ab_conv_w, v_ab_w_out, v_c_w_in, v_c_lower_bounds, v_c_out_norm, v_c_w_out, v_final_norm):
    t = x.shape[1]
    xi, yi, ci = lax.axis_index("x"), lax.axis_index("y"), lax.axis_index("c")
    p_idx = (2 * xi + yi).astype(jnp.int32).reshape(1)
    c_idx = ci.astype(jnp.int32).reshape(1)

    def halves(m):
        return m.astype(BF16).reshape(2, m.shape[0] // 2, m.shape[1])

    transposed = ("ffn_pre_w_gate", "ffn_pre_w_up", "ffn_post_w_gate", "ffn_post_w_up")

    def flip(a):
        return jnp.swapaxes(a, 1, 2)

    shards = {}
    for name, (w_gate, w_up, w_down, layer) in dict(
            pre0=(ffn_pre_w_gate, ffn_pre_w_up, ffn_pre_w_down, 0), post0=(ffn_post_w_gate, ffn_post_w_up, ffn_post_w_down, 0),
            pre1=(ffn_pre_w_gate, ffn_pre_w_up, ffn_pre_w_down, 1), post1=(ffn_post_w_gate, ffn_post_w_up, ffn_post_w_down, 1)).items():
        shards[name + "_gu"] = jnp.stack([flip(w_gate)[layer], flip(w_up)[layer]]).astype(BF16)
        shards[name + "_d"] = halves(w_down[layer])
    conv_pad = jnp.pad(ab_conv_w[0], ((0, 5), (0, 0)))
    shards.update(ab_in=halves(ab_w_in[0]), ab_out=halves(ab_w_out[0]), c_in=halves(c_w_in[0]), c_out=halves(c_w_out[0]),
                  conv=jnp.stack([conv_pad, jnp.zeros_like(conv_pad)]))
    small = _small_slab([ffn_pre_norm, mix_norm, ffn_post_norm, c_lower_bounds, final_norm.reshape(1, D), c_out_norm])
    small = small.reshape(SMALL_ROWS, 1, D)

    grad_x, _, _, _, small_grad, reduced = _local_step(x[0], loss_target[0], {}, small, shards,
                                                        (c_idx, p_idx))
    whole = {n: g.reshape(2 * g.shape[1], g.shape[2]) for n, g in reduced.items()}
    small_sum = _sum_small(small_grad)

    my_conv = lax.dynamic_slice(small_sum[R_CONV:R_CONV + 3], (0, (2 * xi + yi) * 128), (3, 128))
    grads = {
        "ffn_pre_norm": small_sum[R_PRE:R_PRE + 2], "mix_norm": small_sum[R_MIX:R_MIX + 2],
        "ffn_post_norm": small_sum[R_POST:R_POST + 2], "c_lower_bounds": small_sum[R_CLB:R_CLB + 2],
        "c_out_norm": small_sum[R_GAM:R_GAM + 1, :HD], "final_norm": small_sum[R_FIN],
        "ab_conv_w": my_conv.reshape(1, 3, 128),
    }
    layers = dict(ab_w_in=[whole["ab_in"]], ab_w_out=[whole["ab_out"]], c_w_in=[whole["c_in"]], c_w_out=[whole["c_out"]])
    for kind, key in (("gate", "_g"), ("up", "_u"), ("down", "_d")):
        layers["ffn_pre_w_" + kind] = [whole["pre0" + key], whole["pre1" + key]]
        layers["ffn_post_w_" + kind] = [whole["post0" + key], whole["post1" + key]]
    weights = dict(ffn_pre_norm=ffn_pre_norm, ffn_pre_w_gate=ffn_pre_w_gate, ffn_pre_w_up=ffn_pre_w_up, ffn_pre_w_down=ffn_pre_w_down, mix_norm=mix_norm, ffn_post_norm=ffn_post_norm, ffn_post_w_gate=ffn_post_w_gate, ffn_post_w_up=ffn_post_w_up, ffn_post_w_down=ffn_post_w_down, ab_w_in=ab_w_in, ab_conv_w=ab_conv_w, ab_w_out=ab_w_out, c_w_in=c_w_in, c_lower_bounds=c_lower_bounds, c_out_norm=c_out_norm, c_w_out=c_w_out, final_norm=final_norm)
    m_in = dict(ffn_pre_norm=m_ffn_pre_norm, ffn_pre_w_gate=m_ffn_pre_w_gate, ffn_pre_w_up=m_ffn_pre_w_up, ffn_pre_w_down=m_ffn_pre_w_down, mix_norm=m_mix_norm, ffn_post_norm=m_ffn_post_norm, ffn_post_w_gate=m_ffn_post_w_gate, ffn_post_w_up=m_ffn_post_w_up, ffn_post_w_down=m_ffn_post_w_down, ab_w_in=m_ab_w_in, ab_conv_w=m_ab_conv_w, ab_w_out=m_ab_w_out, c_w_in=m_c_w_in, c_lower_bounds=m_c_lower_bounds, c_out_norm=m_c_out_norm, c_w_out=m_c_w_out, final_norm=m_final_norm)
    v_in = dict(ffn_pre_norm=v_ffn_pre_norm, ffn_pre_w_gate=v_ffn_pre_w_gate, ffn_pre_w_up=v_ffn_pre_w_up, ffn_pre_w_down=v_ffn_pre_w_down, mix_norm=v_mix_norm, ffn_post_norm=v_ffn_post_norm, ffn_post_w_gate=v_ffn_post_w_gate, ffn_post_w_up=v_ffn_post_w_up, ffn_post_w_down=v_ffn_post_w_down, ab_w_in=v_ab_w_in, ab_conv_w=v_ab_conv_w, ab_w_out=v_ab_w_out, c_w_in=v_c_w_in, c_lower_bounds=v_c_lower_bounds, c_out_norm=v_c_out_norm, c_w_out=v_c_w_out, final_norm=v_final_norm)
    names = list(weights)
    big = [n for n in names if weights[n].size >= 65536]
    tiny = [n for n in names if n not in big]

    delta, new_m, new_v = {}, {}, {}
    for n in big:
        turn = flip if n in transposed else (lambda a: a)
        shape = turn(weights[n]).shape
        two_d = (shape[0] * shape[1], shape[2])
        d, m2, v2, g2 = _adamw("adamw_" + n, turn(weights[n]).reshape(two_d), layers[n],
                               turn(m_in[n]).reshape(two_d), turn(v_in[n]).reshape(two_d))
        delta[n], new_m[n], new_v[n] = turn(d.reshape(shape)), turn(m2.reshape(shape)), turn(v2.reshape(shape))
        grads[n] = turn(g2.reshape(shape))

    def tiny_slab(src):
        return _small_slab([src[n].reshape(-1, src[n].shape[-1]) for n in tiny])

    offs, row = {}, 0
    for n in tiny:
        nrows = weights[n].size // weights[n].shape[-1]
        offs[n] = (row, nrows)
        row += nrows
    d, m2, v2, _ = _adamw("adamw_small", tiny_slab(weights), [tiny_slab(grads)], tiny_slab(m_in), tiny_slab(v_in))
    for n in tiny:
        r0, nr = offs[n]
        shape = weights[n].shape
        for dst, src in ((delta, d), (new_m, m2), (new_v, v2)):
            dst[n] = src[r0:r0 + nr, :shape[-1]].reshape(shape)

    loss = small_sum[R_LOSS, 0]
    return (loss, grad_x.reshape(1, t, D), *[grads[n] for n in names], *[delta[n] for n in names],
            *[new_m[n] for n in names], *[new_v[n] for n in names])
```

```python
import functools
import math

import jax
import jax.numpy as jnp
from jax import lax
from jax.experimental import pallas as pl
from jax.experimental.pallas import tpu as pltpu

F32 = jnp.float32
BF16 = jnp.bfloat16
MESH = pl.DeviceIdType.MESH
ANY = pl.BlockSpec(memory_space=pl.ANY)

D = 1024
FS = 704
NSH = 4
RMS_EPS = 1e-6
MACARON = 0.5
CHUNK = 64
HD = 128
HGRN_HPS = 8
SBQ = 128
SB_PAIRS_FWD = 4
SB_PAIRS_BWD = 4
SB_DEAD = -105.0
CONV_HALO = 8
LANE = 128
ROW_TILE = 1024
MM_TILE = 1024
WGRAD_TILE = 4096
VMEM_LIMIT = 48 * 1024 * 1024
VMEM_LIMIT_BIG = 58 * 1024 * 1024

ADAM_LR, ADAM_B1, ADAM_B2, ADAM_EPS, ADAM_WD, ADAM_STEP = 0.001, 0.9, 0.999, 1e-08, 0.01, 10

NN = ((1,), (0,))
NT = ((1,), (1,))
TN = ((0,), (0,))

SMALL_ROWS = 16
R_PRE, R_MIX, R_POST, R_CLB, R_FIN, R_GAM, R_CONV, R_LOSS = 0, 2, 4, 6, 8, 9, 10, 13

B_ABOUT = 4 * FS
B_CIN = B_ABOUT + 256
B_COUT = B_CIN + 1024
B_ROWS = B_COUT + 256


def _dg(a, b, dims):
    return lax.dot_general(a, b, (dims, ((), ())), preferred_element_type=F32)


def _split(x):
    hi = x.astype(BF16)
    lo = (x - hi.astype(F32)).astype(BF16)
    return hi, lo


def _dot3(a, b, dims):
    ah, al = _split(a)
    bh, bl = _split(b)
    if dims == TN:
        n = b.shape[1]
        both = _dg(ah, jnp.concatenate([bh, bl], axis=1), dims)
        return both[:, :n] + both[:, n:] + _dg(al, bh, dims)
    m = a.shape[0]
    both = _dg(jnp.concatenate([ah, al], axis=0), bh, dims)
    return both[:m] + both[m:] + _dg(ah, bl, dims)


def _sigmoid(x):
    return 1.0 / (1.0 + jnp.exp(-x))


def _params(sem):
    return pltpu.CompilerParams(dimension_semantics=sem, vmem_limit_bytes=VMEM_LIMIT)


def _spec(shape, imap):
    return pl.BlockSpec(shape, imap)


def _accumulate(acc_ref, pairs, dims):
    part = None
    for a_ref, b_ref in pairs:
        d = _dg(a_ref[...], b_ref[...], dims)
        part = d if part is None else part + d
    acc_ref[...] += part


def _mm(name, pairs, *, grid, o_shape, o_dtype, o_spec, dims, kaxis, nk, acc_shape=None, res=None, scale=None,
        into=None, carry=None, norm=None):
    npairs = len(pairs)
    operands, specs = [], []
    for a, a_spec, b, b_spec in pairs:
        operands += [a, b]
        specs += [a_spec, b_spec]
    if res is not None:
        operands.append(res[0])
        specs.append(res[1])
    aliases = {}
    if into is not None:
        aliases = {len(operands): 0}
        operands.append(into)
        specs.append(ANY)
    if norm is not None:
        operands.append(norm[0])
        specs.append(_spec((None, 1, D), lambda *_: (norm[1], 0, 0)))
    n_own = len(operands)
    n_out = 2 if norm is not None else 1
    nc = carry.n if carry is not None else 0
    if nc:
        operands += carry.groups
        specs += carry.in_specs

    def body(*refs):
        o_ref = refs[n_own + nc]
        riders = ((refs[n_own:n_own + nc], refs[n_own + nc + n_out:n_own + 2 * nc + n_out], refs[-2], refs[-1])
                  if nc else None)
        if nc:
            carry.ride(grid, riders, "start")

        def finish(val):
            if scale is not None:
                val = val * scale
            if res is not None:
                val = val + refs[2 * npairs][...]
            o_ref[...] = val.astype(o_dtype)
            if norm is not None:
                r = lax.rsqrt(jnp.mean(val * val, axis=-1, keepdims=True) + RMS_EPS)
                refs[n_own + nc + 1][...] = (val * r * refs[n_own - 1][...]).astype(BF16)

        if nk == 1:
            part = None
            for n in range(npairs):
                d = _dg(refs[2 * n][...], refs[2 * n + 1][...], dims)
                part = d if part is None else part + d
            finish(part)
        else:
            acc_ref = refs[n_own + 2 * nc + n_out]
            k = pl.program_id(kaxis)

            @pl.when(k == 0)
            def _():
                acc_ref[...] = jnp.zeros_like(acc_ref)

            _accumulate(acc_ref, [(refs[2 * n], refs[2 * n + 1]) for n in range(npairs)], dims)

            @pl.when(k == nk - 1)
            def _():
                finish(acc_ref[...])
        if nc:
            carry.ride(grid, riders, "finish")

    sem = tuple("arbitrary" if (nc or (ax == kaxis and nk > 1)) else "parallel" for ax in range(len(grid)))
    outs = pl.pallas_call(
        body, name=name, grid=grid, in_specs=specs,
        out_specs=[o_spec] * n_out + (carry.out_specs if nc else []),
        out_shape=[jax.ShapeDtypeStruct(o_shape, o_dtype)] + ([jax.ShapeDtypeStruct(o_shape, BF16)] if norm is not None else [])
        + (carry.out_shape if nc else []),
        scratch_shapes=([pltpu.VMEM(acc_shape, F32)] if nk > 1 else []) + (carry.scratch if nc else []),
        input_output_aliases=aliases,
        compiler_params=_params(sem),
    )(*operands)
    main = (outs[0], outs[1]) if norm is not None else outs[0]
    return (main, carry.place(outs[n_out:])) if nc else main


def _norm_fwd(name, h, gain_slab, row, rider=None):
    t = h.shape[0]
    tm = min(ROW_TILE, t)
    grid = (t // tm,)
    nr = rider.n if rider is not None else 0

    def body(*refs):
        h_ref, g_ref, o_ref = refs[0], refs[1], refs[2 + nr]
        riders = (refs[2:2 + nr], refs[3 + nr:3 + 2 * nr], refs[-2], refs[-1]) if nr else None
        if nr:
            rider.ride(grid, riders, "start")
        x = h_ref[...]
        r = lax.rsqrt(jnp.mean(x * x, axis=-1, keepdims=True) + RMS_EPS)
        o_ref[...] = (x * r * g_ref[...]).astype(BF16)
        if nr:
            rider.ride(grid, riders, "finish")

    outs = pl.pallas_call(
        body, name=name, grid=grid,
        in_specs=[_spec((tm, D), lambda i: (i, 0)), _spec((None, 1, D), lambda i: (row, 0, 0))]
        + (rider.in_specs if nr else []),
        out_specs=[_spec((tm, D), lambda i: (i, 0))] + (rider.out_specs if nr else []),
        out_shape=[jax.ShapeDtypeStruct((t, D), BF16)] + (rider.out_shape if nr else []),
        scratch_shapes=rider.scratch if nr else [],
        compiler_params=_params(("arbitrary",) if nr else ("parallel",)),
    )(h, gain_slab, *(rider.groups if nr else []))
    return outs[0], (rider.place(outs[1:]) if nr else [])


def _dhn_norm(name, pairs, dims, h, gain_slab, row, dres, rider=None):
    t = h.shape[0]
    tm = min(MM_TILE, t)
    nt = t // tm
    grid = (nt, NSH)
    npairs = len(pairs)
    nr = rider.n if rider is not None else 0
    operands, specs = [], []
    for a, a_spec, b, b_spec in pairs:
        operands += [a, b]
        specs += [a_spec, b_spec]
    row_spec = _spec((tm, D), lambda i, k: (i, 0))
    operands += [h, gain_slab, dres]
    specs += [row_spec, _spec((None, 1, D), lambda i, k: (row, 0, 0)), row_spec]
    n_own = len(operands)

    def body(*refs):
        h_ref, g_ref, dres_ref = refs[2 * npairs:n_own]
        dh_ref, dhb_ref, dg_ref = refs[n_own + nr:n_own + nr + 3]
        acc_ref, gacc_ref = refs[n_own + 2 * nr + 3:n_own + 2 * nr + 5]
        riders = (refs[n_own:n_own + nr], refs[n_own + nr + 3:n_own + 2 * nr + 3], refs[-2], refs[-1]) if nr else None
        if nr:
            rider.ride(grid, riders, "start")
        i, k = pl.program_id(0), pl.program_id(1)

        @pl.when(k == 0)
        def _():
            acc_ref[...] = jnp.zeros_like(acc_ref)

        _accumulate(acc_ref, [(refs[2 * n], refs[2 * n + 1]) for n in range(npairs)], dims)

        @pl.when(k == NSH - 1)
        def _():
            x = h_ref[...]
            r = lax.rsqrt(jnp.mean(x * x, axis=-1, keepdims=True) + RMS_EPS)
            xh = x * r
            dy = acc_ref[...]
            gdy = dy * g_ref[...]
            dh = dres_ref[...] + (gdy - xh * jnp.mean(gdy * xh, axis=-1, keepdims=True)) * r
            dh_ref[...] = dh
            dhb_ref[...] = dh.astype(BF16)
            gpart = jnp.sum((dy * xh).reshape(tm // 8, 8, D), axis=0)

            @pl.when(i == 0)
            def _():
                gacc_ref[...] = gpart

            @pl.when(i > 0)
            def _():
                gacc_ref[...] += gpart

            @pl.when(i == nt - 1)
            def _():
                dg_ref[...] = jnp.sum(gacc_ref[...], axis=0, keepdims=True)

        if nr:
            rider.ride(grid, riders, "finish")

    outs = pl.pallas_call(
        body, name=name, grid=grid, in_specs=specs + (rider.in_specs if nr else []),
        out_specs=[row_spec, row_spec, _spec((1, D), lambda i, k: (0, 0))] + (rider.out_specs if nr else []),
        out_shape=[jax.ShapeDtypeStruct((t, D), F32), jax.ShapeDtypeStruct((t, D), BF16),
                   jax.ShapeDtypeStruct((1, D), F32)] + (rider.out_shape if nr else []),
        scratch_shapes=[pltpu.VMEM((tm, D), F32), pltpu.VMEM((8, D), F32)] + (rider.scratch if nr else []),
        compiler_params=pltpu.CompilerParams(dimension_semantics=("arbitrary", "arbitrary"),
                                             vmem_limit_bytes=VMEM_LIMIT_BIG),
    )(*operands, *(rider.groups if nr else []))
    return outs[0], outs[1], outs[2], (rider.place(outs[3:]) if nr else [])


def _final_loss(h, gain_slab, target):
    t = h.shape[0]
    tm = min(ROW_TILE, t)
    nt = t // tm

    def body(h_ref, g_ref, t_ref, dh_ref, dhb_ref, dg_ref, loss_ref, acc_ref, lacc_ref):
        i = pl.program_id(0)
        x = h_ref[...]
        g = g_ref[...]
        r = lax.rsqrt(jnp.mean(x * x, axis=-1, keepdims=True) + RMS_EPS)
        xh = x * r
        err = xh * g - t_ref[...]
        dy = err * (1.0 / D)
        gdy = dy * g
        dh = (gdy - xh * jnp.mean(gdy * xh, axis=-1, keepdims=True)) * r
        dh_ref[...] = dh
        dhb_ref[...] = dh.astype(BF16)
        part = jnp.sum((dy * xh).reshape(tm // 8, 8, D), axis=0)
        lpart = jnp.sum((err * err).reshape(tm // 8, 8, D), axis=0)

        @pl.when(i == 0)
        def _():
            acc_ref[...] = part
            lacc_ref[...] = lpart

        @pl.when(i > 0)
        def _():
            acc_ref[...] += part
            lacc_ref[...] += lpart

        @pl.when(i == nt - 1)
        def _():
            dg_ref[...] = jnp.sum(acc_ref[...], axis=0, keepdims=True)
            rows = jnp.sum(lacc_ref[...], axis=0, keepdims=True)
            loss_ref[...] = jnp.sum(rows, axis=1, keepdims=True) * (0.5 / D)

    row_spec = _spec((tm, D), lambda i: (i, 0))
    return pl.pallas_call(
        body, name="final_loss", grid=(nt,),
        in_specs=[row_spec, _spec((None, 1, D), lambda i: (R_FIN, 0, 0)), row_spec],
        out_specs=[row_spec, row_spec, _spec((1, D), lambda i: (0, 0)), _spec((1, 1), lambda i: (0, 0))],
        out_shape=[jax.ShapeDtypeStruct((t, D), F32), jax.ShapeDtypeStruct((t, D), BF16),
                   jax.ShapeDtypeStruct((1, D), F32), jax.ShapeDtypeStruct((1, 1), F32)],
        scratch_shapes=[pltpu.VMEM((8, D), F32), pltpu.VMEM((8, D), F32)],
        compiler_params=_params(("arbitrary",)),
    )(h, gain_slab, target)


def _ffn_up(name, hn, wgu, carry=None):
    t = hn.shape[0]
    tm = min(MM_TILE, t)
    grid = (NSH, t // tm)
    nc = carry.n if carry is not None else 0

    def body(*refs):
        x_ref, wg_ref, wu_ref = refs[:3]
        s_up_ref, s_gate_ref, a_ref = refs[3 + nc:6 + nc]
        riders = (refs[3:3 + nc], refs[6 + nc:6 + 2 * nc], refs[-2], refs[-1]) if nc else None
        if nc:
            carry.ride(grid, riders, "start")
        x = x_ref[...]
        g = _dg(x, wg_ref[...], NT)
        u = _dg(x, wu_ref[...], NT)
        sg = _sigmoid(g)
        silu = g * sg
        s_up_ref[...] = (MACARON * silu).astype(BF16)
        s_gate_ref[...] = (MACARON * u * (sg * (1.0 + g * (1.0 - sg)))).astype(BF16)
        a_ref[...] = (silu * u).astype(BF16)
        if nc:
            carry.ride(grid, riders, "finish")

    act = _spec((None, tm, FS), lambda s, i: (s, i, 0))
    shape = jax.ShapeDtypeStruct((NSH, t, FS), BF16)
    outs = pl.pallas_call(
        body, name=name, grid=grid,
        in_specs=[_spec((tm, D), lambda s, i: (i, 0)),
                  _spec((None, None, FS, D), lambda s, i: (s, 0, 0, 0)),
                  _spec((None, None, FS, D), lambda s, i: (s, 1, 0, 0))] + (carry.in_specs if nc else []),
        out_specs=[act, act, act] + (carry.out_specs if nc else []),
        out_shape=[shape, shape, shape] + (carry.out_shape if nc else []),
        scratch_shapes=carry.scratch if nc else [],
        compiler_params=_params(("arbitrary", "arbitrary") if nc else ("parallel", "parallel")),
    )(hn, wgu, wgu, *(carry.groups if nc else []))
    return (outs[0], outs[1], outs[2], carry.place(outs[3:]) if nc else [])


def _ffn_down(name, a, wd, h, carry=None, norm=None):
    t = h.shape[0]
    tm = min(MM_TILE, t)
    return _mm(name, [(a, _spec((None, tm, FS), lambda i, k: (k, i, 0)),
                       wd, _spec((None, FS, D), lambda i, k: (k, 0, 0)))],
               grid=(t // tm, NSH), o_shape=(t, D), o_dtype=F32, o_spec=_spec((tm, D), lambda i, k: (i, 0)),
               dims=NN, kaxis=1, nk=NSH, acc_shape=(tm, D), res=(h, _spec((tm, D), lambda i, k: (i, 0))),
               scale=MACARON, carry=carry, norm=norm)


def _ffn_bwd_up(name, dhb, wd, s_up, s_gate, rider=None):
    t = dhb.shape[0]
    tm = min(2 * MM_TILE, t)
    grid = (t // tm, NSH)
    nr = rider.n if rider is not None else 0

    def body(*refs):
        dh_ref, wd_ref, s_up_ref, s_gate_ref = refs[:4]
        dg_ref, du_ref = refs[4 + nr:6 + nr]
        riders = (refs[4:4 + nr], refs[6 + nr:6 + 2 * nr], refs[-2], refs[-1]) if nr else None
        if nr:
            rider.ride(grid, riders, "start")
        da = _dg(dh_ref[...], wd_ref[...], NT).astype(BF16)
        du_ref[...] = da * s_up_ref[...]
        dg_ref[...] = da * s_gate_ref[...]
        if nr:
            rider.ride(grid, riders, "finish")

    act = _spec((None, tm, FS), lambda i, s: (s, i, 0))
    shape = jax.ShapeDtypeStruct((NSH, t, FS), BF16)
    outs = pl.pallas_call(
        body, name=name, grid=grid,
        in_specs=[_spec((tm, D), lambda i, s: (i, 0)), _spec((None, FS, D), lambda i, s: (s, 0, 0)), act, act]
        + (rider.in_specs if nr else []),
        out_specs=[act, act] + (rider.out_specs if nr else []),
        out_shape=[shape, shape] + (rider.out_shape if nr else []),
        scratch_shapes=rider.scratch if nr else [],
        compiler_params=pltpu.CompilerParams(
            dimension_semantics=("arbitrary", "arbitrary") if nr else ("parallel", "parallel"),
            vmem_limit_bytes=VMEM_LIMIT_BIG),
    )(dhb, wd, s_up, s_gate, *(rider.groups if nr else []))
    return outs[0], outs[1], (rider.place(outs[2:]) if nr else [])


def _wgrad(name, a, a_spec, b, b_spec, out_rows, out_cols, t, tt, group, slot, scale=None, carry=None):
    first = isinstance(group, int)
    rows = group if first else group.shape[1]
    return _mm(name, [(a, a_spec, b, b_spec)], grid=(NSH, t // tt),
               o_shape=(NSH, rows, out_cols), o_dtype=BF16,
               o_spec=_spec((None, out_rows, out_cols), lambda s, k: (s, slot, 0)),
               dims=TN, kaxis=1, nk=t // tt, acc_shape=(out_rows, out_cols), scale=scale,
               into=None if first else group, carry=carry)


def _wgrad_gate_up(name, dg, du, hn, group, gate_idx):
    t = hn.shape[0]
    first = isinstance(group, int)
    rows = group if first else group.shape[1]

    def body(dg_ref, du_ref, hn_ref, *refs):
        o_ref = refs[-1]
        x = hn_ref[...]
        o_ref[...] = jnp.concatenate([_dg(dg_ref[...], x, TN), _dg(du_ref[...], x, TN)], axis=0).astype(BF16)

    hid = _spec((None, t, FS), lambda s: (s, 0, 0))
    return pl.pallas_call(
        body, name=name, grid=(NSH,),
        in_specs=[hid, hid, _spec((t, D), lambda s: (0, 0))] + ([] if first else [ANY]),
        out_specs=_spec((None, 2 * FS, D), lambda s: (s, gate_idx // 2, 0)),
        out_shape=jax.ShapeDtypeStruct((NSH, rows, D), BF16),
        input_output_aliases={} if first else {3: 0},
        compiler_params=pltpu.CompilerParams(dimension_semantics=("parallel",), vmem_limit_bytes=VMEM_LIMIT_BIG),
    )(dg, du, hn, *([] if first else [group]))


def _ffn_backward(tag, dh, dhb, h_in, hn, s_up, s_gate, a, wgu, wd, gate_idx, up_idx, down_idx, small, norm_row,
                  grad_a, grad_b, rider_up=None, rider_dhn=None):
    t = dh.shape[0]
    tm = min(MM_TILE, t)
    tt = min(WGRAD_TILE, t)
    tok = _spec((tt, D), lambda s, k: (k, 0))
    hid = _spec((None, tt, FS), lambda s, k: (s, k, 0))
    grad_b = _wgrad(tag + "_dwd", a, hid, dhb, tok, FS, D, t, tt, grad_b, down_idx, scale=MACARON)
    rider = rider_up(grad_a, grad_b) if rider_up is not None else None
    dg, du, landed_up = _ffn_bwd_up(tag + "_bwd_up", dhb, wd, s_up, s_gate, rider)
    assert up_idx == gate_idx + 1 and gate_idx % 2 == 0
    grad_a = _wgrad_gate_up(tag + "_dwgu", dg, du, hn, grad_a, gate_idx)
    rider = rider_dhn(grad_a, grad_b, landed_up) if rider_dhn is not None else None
    act = _spec((None, tm, FS), lambda i, k: (k, i, 0))
    dh_in, dhb_in, d_gain, landed_dhn = _dhn_norm(
        tag + "_dhn", [(dg, act, wgu, _spec((None, None, FS, D), lambda i, k: (k, 0, 0, 0))),
                       (du, act, wgu, _spec((None, None, FS, D), lambda i, k: (k, 1, 0, 0)))],
        NN, h_in, small, norm_row, dh, rider)
    return dh_in, dhb_in, grad_a, grad_b, d_gain, landed_up, landed_dhn


def _conv_fwd(proj_a, conv_w):
    t = proj_a.shape[0]
    tm = min(ROW_TILE, t)
    hb = tm // CONV_HALO

    def body(ab_ref, ac_ref, ax_ref, acp_ref, axp_ref, w_ref, y_ref):
        i = pl.program_id(1)
        u = ac_ref[...] * ax_ref[...]
        up = jnp.where(i > 0, acp_ref[...] * axp_ref[...], 0.0)
        ext = jnp.concatenate([up, u], axis=0)
        u1 = pltpu.roll(ext, 1, 0)[CONV_HALO:]
        u2 = pltpu.roll(ext, 2, 0)[CONV_HALO:]
        w = w_ref[...]
        conv = w[0:1] * u2 + w[1:2] * u1 + w[2:3] * u
        y_ref[...] = (ab_ref[...] * conv).astype(BF16)

    def cur(off):
        return _spec((tm, LANE), lambda j, i: (i, off + j))

    def prev(off):
        return _spec((CONV_HALO, LANE), lambda j, i: (jnp.maximum(i * hb - 1, 0), off + j))

    return pl.pallas_call(
        body, name="conv_fwd", grid=(4, t // tm),
        in_specs=[cur(0), cur(4), cur(8), prev(4), prev(8),
                  _spec((None, None, 8, LANE), lambda j, i: (j, 0, 0, 0))],
        out_specs=_spec((tm, LANE), lambda j, i: (i, j)),
        out_shape=jax.ShapeDtypeStruct((t, 512), BF16),
        compiler_params=_params(("parallel", "parallel")),
    )(proj_a, proj_a, proj_a, proj_a, proj_a, conv_w)


def _conv_bwd(proj_a, conv_w, dy, rider=None):
    t = proj_a.shape[0]
    tm = min(ROW_TILE, t)
    hb = tm // CONV_HALO
    nt = t // tm
    grid = (4, nt)
    nr = rider.n if rider is not None else 0

    def body(*refs):
        ab_ref, ac_ref, ax_ref, dy_ref, acp_ref, axp_ref, abn_ref, dyn_ref, w_ref = refs[:9]
        dab_ref, dac_ref, dax_ref, dw_ref = refs[9 + nr:13 + nr]
        acc_ref = refs[13 + 2 * nr]
        riders = (refs[9:9 + nr], refs[13 + nr:13 + 2 * nr], refs[-2], refs[-1]) if nr else None
        if nr:
            rider.ride(grid, riders, "start")
        i = pl.program_id(1)
        ab, ac, ax = ab_ref[...], ac_ref[...], ax_ref[...]
        u = ac * ax
        up = jnp.where(i > 0, acp_ref[...] * axp_ref[...], 0.0)
        ext = jnp.concatenate([up, u], axis=0)
        u1 = pltpu.roll(ext, 1, 0)[CONV_HALO:]
        u2 = pltpu.roll(ext, 2, 0)[CONV_HALO:]
        w = w_ref[...]
        conv = w[0:1] * u2 + w[1:2] * u1 + w[2:3] * u
        dy_v = dy_ref[...]
        dab_ref[...] = (dy_v * conv).astype(BF16)
        dc = dy_v * ab
        dcn = jnp.where(i < nt - 1, dyn_ref[...] * abn_ref[...], 0.0)
        extn = jnp.concatenate([dc, dcn], axis=0)
        n = tm + CONV_HALO
        dc1 = pltpu.roll(extn, n - 1, 0)[:tm]
        dc2 = pltpu.roll(extn, n - 2, 0)[:tm]
        du = w[2:3] * dc + w[1:2] * dc1 + w[0:1] * dc2
        dac_ref[...] = (du * ax).astype(BF16)
        dax_ref[...] = (du * ac).astype(BF16)
        rid = lax.broadcasted_iota(jnp.int32, (8, LANE), 0)
        part = jnp.where(rid == 0, jnp.sum(dc * u2, axis=0, keepdims=True),
                         jnp.where(rid == 1, jnp.sum(dc * u1, axis=0, keepdims=True),
                                   jnp.where(rid == 2, jnp.sum(dc * u, axis=0, keepdims=True), 0.0)))

        @pl.when(i == 0)
        def _():
            acc_ref[...] = part

        @pl.when(i > 0)
        def _():
            acc_ref[...] += part

        @pl.when(i == nt - 1)
        def _():
            dw_ref[...] = acc_ref[...]

        if nr:
            rider.ride(grid, riders, "finish")

    def cur(off):
        return _spec((tm, LANE), lambda j, i: (i, off + j))

    def prev(off):
        return _spec((CONV_HALO, LANE), lambda j, i: (jnp.maximum(i * hb - 1, 0), off + j))

    def nxt(off):
        return _spec((CONV_HALO, LANE), lambda j, i: (jnp.minimum((i + 1) * hb, nt * hb - 1), off + j))

    outs = pl.pallas_call(
        body, name="conv_bwd", grid=grid,
        in_specs=[cur(0), cur(4), cur(8), cur(0), prev(4), prev(8), nxt(0), nxt(0),
                  _spec((None, None, 8, LANE), lambda j, i: (j, 0, 0, 0))] + (rider.in_specs if nr else []),
        out_specs=[_spec((tm, LANE), lambda j, i: (i, j)), _spec((tm, LANE), lambda j, i: (i, j)),
                   _spec((tm, LANE), lambda j, i: (i, j)), _spec((None, 8, LANE), lambda j, i: (j, 0, 0))]
        + (rider.out_specs if nr else []),
        out_shape=[jax.ShapeDtypeStruct((t, 512), BF16), jax.ShapeDtypeStruct((t, 512), BF16),
                   jax.ShapeDtypeStruct((t, 512), BF16), jax.ShapeDtypeStruct((4, 8, LANE), F32)]
        + (rider.out_shape if nr else []),
        scratch_shapes=[pltpu.VMEM((8, LANE), F32)] + (rider.scratch if nr else []),
        compiler_params=_params(("arbitrary", "arbitrary") if nr else ("parallel", "arbitrary")),
    )(proj_a, proj_a, proj_a, dy, proj_a, proj_a, proj_a, dy, conv_w, *(rider.groups if nr else []))
    return outs[0], outs[1], outs[2], outs[3], (rider.place(outs[4:]) if nr else [])


def _log_sigmoid(z):
    return jnp.minimum(z, 0.0) - jnp.log(1.0 + jnp.exp(-jnp.abs(z)))


def _sb_masks():
    row = lax.broadcasted_iota(jnp.int32, (SBQ, SBQ), 0)
    col = lax.broadcasted_iota(jnp.int32, (SBQ, SBQ), 1)
    return row, col


def _ones_where(mask):
    return jnp.where(mask, 1.0, 0.0).astype(BF16)


def _head_mask(head):
    lane = lax.broadcasted_iota(jnp.int32, (1, LANE), 1)
    return lane >= 64 if head else lane < 64


def _sb_fwd(proj_b, carry=None):
    t = proj_b.shape[0]
    nq = t // SBQ
    scale = 1.0 / math.sqrt(64.0)

    npair = SB_PAIRS_FWD
    wide = npair * LANE
    ngrp = 4 // npair
    chains = [(p, head) for p in range(npair) for head in range(2)]

    grid = (ngrp, nq)
    nc = carry.n if carry is not None else 0

    def body(*refs):
        q_ref, k_ref, v_ref = refs[:3]
        y_ref, l_ref, n_ref = refs[3 + nc:6 + nc]
        riders = (refs[3:3 + nc], refs[6 + nc:6 + 2 * nc], refs[-2], refs[-1]) if nc else None
        if nc:
            carry.ride(grid, riders, "start")
        grp = pl.program_id(0)
        qi = pl.program_id(1)
        row, col = _sb_masks()
        m_suffix = _ones_where(row > col)
        rows = len(chains) * SBQ
        strict = (lax.broadcasted_iota(jnp.int32, (rows, SBQ), 1)
                  < (lax.broadcasted_iota(jnp.int32, (rows, SBQ), 0) & (SBQ - 1)))
        q_pair = []
        for p in range(npair):
            q_all = q_ref[:, p * LANE:(p + 1) * LANE]
            q_pair.append(jnp.concatenate([jnp.where(_head_mask(head), q_all, jnp.zeros_like(q_all)) for head in range(2)],
                                          axis=0))

        def block(kb, state, diag):
            run, acc = state
            start = pl.multiple_of(kb * SBQ, SBQ)
            z = jnp.concatenate([_dg(q_pair[p], k_ref[pl.ds(start, SBQ), p * LANE:(p + 1) * LANE], NT)
                                 for p in range(npair)], axis=0) * scale
            lb = _log_sigmoid(z)
            lk = lb - z
            if diag:
                lk = jnp.where(strict, lk, 0.0)
            hi, lo = _split(lk)
            sums = _dg(jnp.concatenate([hi, lo], axis=0), m_suffix, NN)
            w = jnp.exp(lb + (run + sums[:rows] + sums[rows:]))
            if diag:
                w = jnp.where(strict, w, 0.0)
            wb = w.astype(BF16)
            acc = acc + jnp.concatenate(
                [_dg(wb[2 * p * SBQ:2 * (p + 1) * SBQ], v_ref[pl.ds(start, SBQ), p * LANE:(p + 1) * LANE], NN)
                 for p in range(npair)], axis=0)
            run = run + jnp.sum(hi.astype(F32) + lo.astype(F32), axis=1, keepdims=True)
            return run, acc

        state = block(qi, (jnp.zeros((rows, 1), F32), jnp.zeros((rows, LANE), F32)), True)

        def live(c):
            return jnp.logical_and(c[0] < qi, jnp.max(c[1][0]) > SB_DEAD)

        def step(c):
            return c[0] + 1, block(qi - 1 - c[0], c[1], False)

        count, (run, acc) = lax.while_loop(live, step, (jnp.int32(0), state))
        n_ref[grp * nq + qi] = count.astype(F32)
        hm = _head_mask(0)
        for p in range(npair):
            lo_rows, hi_rows = slice(2 * p * SBQ, (2 * p + 1) * SBQ), slice((2 * p + 1) * SBQ, (2 * p + 2) * SBQ)
            y_ref[:, p * LANE:(p + 1) * LANE] = jnp.where(hm, acc[lo_rows], acc[hi_rows]).astype(BF16)
            l_ref[p] = jnp.where(hm, run[lo_rows], run[hi_rows])
        if nc:
            carry.ride(grid, riders, "finish")

    outs = pl.pallas_call(
        body, name="sb_fwd", grid=grid,
        in_specs=[_spec((SBQ, wide), lambda g, i: (i, g)),
                  _spec((t, wide), lambda g, i: (0, ngrp + g)),
                  _spec((t, wide), lambda g, i: (0, 2 * ngrp + g))] + (carry.in_specs if nc else []),
        out_specs=[_spec((SBQ, wide), lambda g, i: (i, g)), _spec((npair, SBQ, LANE), lambda g, i: (g, i, 0)),
                   pl.BlockSpec(memory_space=pltpu.SMEM)] + (carry.out_specs if nc else []),
        out_shape=[jax.ShapeDtypeStruct((t, 512), BF16), jax.ShapeDtypeStruct((4, t, LANE), F32),
                   jax.ShapeDtypeStruct((ngrp * nq,), F32)] + (carry.out_shape if nc else []),
        scratch_shapes=carry.scratch if nc else [],
        compiler_params=_params(("arbitrary", "arbitrary")),
    )(proj_b, proj_b, proj_b, *(carry.groups if nc else []))
    return outs[0], outs[1], outs[2], (carry.place(outs[3:]) if nc else [])


def _sb_bwd(proj_b, dy, ltot, nblk, rider=None):
    t = proj_b.shape[0]
    nq = t // SBQ
    scale = 1.0 / math.sqrt(64.0)

    npair = SB_PAIRS_BWD
    wide = npair * LANE
    ngrp = 4 // npair
    chains = [(p, head) for p in range(npair) for head in range(2)]
    grid = (ngrp, nq)
    nr = rider.n if rider is not None else 0
    per_count = SB_PAIRS_FWD // SB_PAIRS_BWD

    def body(*refs):
        q_ref, k_ref, v_ref, dy_ref, l_ref, n_ref = refs[:6]
        dq_ref, dk_ref, dv_ref = refs[6 + nr:9 + nr]
        dk_acc, dv_acc = refs[9 + 2 * nr:11 + 2 * nr]
        riders = (refs[6:6 + nr], refs[9 + nr:9 + 2 * nr], refs[-2], refs[-1]) if nr else None
        if nr:
            rider.ride(grid, riders, "start")
        grp = pl.program_id(0)
        qi = pl.program_id(1)

        @pl.when(qi == 0)
        def _():
            dk_acc[...] = jnp.zeros_like(dk_acc)
            dv_acc[...] = jnp.zeros_like(dv_acc)

        row, col = _sb_masks()
        m_prefix = _ones_where(row <= col)
        m_before = _ones_where(row < col)
        rows = len(chains) * SBQ
        strict = (lax.broadcasted_iota(jnp.int32, (rows, SBQ), 1)
                  < (lax.broadcasted_iota(jnp.int32, (rows, SBQ), 0) & (SBQ - 1)))
        q_pair, do_pair, ltot = [], [], []
        for p in range(npair):
            pl_ = slice(p * LANE, (p + 1) * LANE)
            q_all = q_ref[:, pl_]
            do_all = dy_ref[:, pl_].astype(BF16)
            q_pair.append(jnp.concatenate([jnp.where(_head_mask(h), q_all, jnp.zeros_like(q_all)) for h in range(2)], axis=0))
            do_pair.append(jnp.concatenate([jnp.where(_head_mask(h), do_all, jnp.zeros_like(do_all)) for h in range(2)], axis=0))
            ltot += [l_ref[p][:, h * 64:h * 64 + 1] for h in range(2)]
        ltot = jnp.concatenate(ltot, axis=0)

        def pair_rows(a, p):
            return a[2 * p * SBQ:2 * (p + 1) * SBQ]

        def block(kb, state, diag):
            seen, dseen, dq = state
            start = pl.multiple_of(kb * SBQ, SBQ)
            kk = [k_ref[pl.ds(start, SBQ), p * LANE:(p + 1) * LANE] for p in range(npair)]
            vv = [v_ref[pl.ds(start, SBQ), p * LANE:(p + 1) * LANE] for p in range(npair)]
            z = jnp.concatenate([_dg(q_pair[p], kk[p], NT) for p in range(npair)], axis=0) * scale
            lb = _log_sigmoid(z)
            lk = lb - z
            if diag:
                lk = jnp.where(strict, lk, 0.0)
            hi, lo = _split(lk)
            sums = _dg(jnp.concatenate([hi, lo], axis=0), m_prefix, NN)
            w = jnp.exp(lb + ((ltot - seen) - (sums[:rows] + sums[rows:])))
            if diag:
                w = jnp.where(strict, w, 0.0)
            wb = w.astype(BF16)
            da = w * jnp.concatenate([_dg(do_pair[p], vv[p], NT) for p in range(npair)], axis=0)
            dah, dal = _split(da)
            dsums = _dg(jnp.concatenate([dah, dal], axis=0), m_before, NN)
            sig = jnp.exp(lb)
            dz = (da * (1.0 - sig) - (dseen + dsums[:rows] + dsums[rows:]) * sig) * scale
            if diag:
                dz = jnp.where(strict, dz, 0.0)
            dzb = dz.astype(BF16)
            for p in range(npair):
                pl_ = slice(p * LANE, (p + 1) * LANE)
                dv_acc[pl.ds(start, SBQ), pl_] += _dg(pair_rows(wb, p), do_pair[p], TN)
                dk_acc[pl.ds(start, SBQ), pl_] += _dg(pair_rows(dzb, p), q_pair[p], TN)
            dq = dq + jnp.concatenate([_dg(pair_rows(dzb, p), kk[p], NN) for p in range(npair)], axis=0)
            seen = seen + jnp.sum(hi.astype(F32) + lo.astype(F32), axis=1, keepdims=True)
            dseen = dseen + jnp.sum(da, axis=1, keepdims=True)
            return seen, dseen, dq

        zero = (jnp.zeros((rows, 1), F32), jnp.zeros((rows, 1), F32), jnp.zeros((rows, LANE), F32))
        first = qi - n_ref[(grp // per_count) * nq + qi].astype(jnp.int32)
        state = lax.fori_loop(first, qi, lambda kb, c: block(kb, c, False), zero)
        _, _, dq = block(qi, state, True)
        for p in range(npair):
            dq_ref[:, p * LANE:(p + 1) * LANE] = jnp.where(
                _head_mask(0), dq[2 * p * SBQ:(2 * p + 1) * SBQ], dq[(2 * p + 1) * SBQ:(2 * p + 2) * SBQ]).astype(BF16)

        @pl.when(qi == nq - 1)
        def _():
            dk_ref[...] = dk_acc[...].astype(BF16)
            dv_ref[...] = dv_acc[...].astype(BF16)

        if nr:
            rider.ride(grid, riders, "finish")

    full = jax.ShapeDtypeStruct((t, 512), BF16)
    outs = pl.pallas_call(
        body, name="sb_bwd", grid=grid,
        in_specs=[_spec((SBQ, wide), lambda g, i: (i, g)),
                  _spec((t, wide), lambda g, i: (0, ngrp + g)),
                  _spec((t, wide), lambda g, i: (0, 2 * ngrp + g)),
                  _spec((SBQ, wide), lambda g, i: (i, ngrp + g)),
                  _spec((npair, SBQ, LANE), lambda g, i: (g, i, 0)),
                  pl.BlockSpec(memory_space=pltpu.SMEM)] + (rider.in_specs if nr else []),
        out_specs=[_spec((SBQ, wide), lambda g, i: (i, g)),
                   _spec((t, wide), lambda g, i: (0, g)), _spec((t, wide), lambda g, i: (0, g))]
        + (rider.out_specs if nr else []),
        out_shape=[full, full, full] + (rider.out_shape if nr else []),
        scratch_shapes=[pltpu.VMEM((t, wide), F32), pltpu.VMEM((t, wide), F32)] + (rider.scratch if nr else []),
        compiler_params=pltpu.CompilerParams(dimension_semantics=("arbitrary", "arbitrary"),
                                             vmem_limit_bytes=VMEM_LIMIT_BIG),
    )(proj_b, proj_b, proj_b, dy, ltot, nblk, *(rider.groups if nr else []))
    return outs[0], outs[1], outs[2], (rider.place(outs[3:]) if nr else [])


def _hgrn_gates(qr, fr, c0, c1):
    mx = jnp.maximum(c0, c1)
    e0, e1 = jnp.exp(c0 - mx), jnp.exp(c1 - mx)
    lb = e1 / (e0 + e1)
    sx = _sigmoid(fr)
    f = lb + (1.0 - lb) * sx
    k = (1.0 - lb) * (1.0 - sx)
    sq = _sigmoid(qr)
    return lb, sx, f, k, sq, qr * sq


def _chunk_sums(mask, x):
    n = x.shape[1]
    hi, lo = _split(x)
    both = _dg(_ones_where(mask), jnp.concatenate([hi, lo], axis=1), NN)
    return both[:, :n] + both[:, n:]


def _chunk_masks():
    row = lax.broadcasted_iota(jnp.int32, (CHUNK, CHUNK), 0)
    col = lax.broadcasted_iota(jnp.int32, (CHUNK, CHUNK), 1)
    return col <= row, col >= row


def _hgrn_fwd(proj_c, small, carry=None):
    t = proj_c.shape[1]
    nc = t // CHUNK
    nh = D // HD

    hps = HGRN_HPS
    wide = hps * HD
    grid = (nh // hps, nc)
    nr = carry.n if carry is not None else 0

    def body(*refs):
        p_ref, c0_ref, c1_ref, gam_ref = refs[:4]
        o_ref, y_ref, sst_ref = refs[4 + nr:7 + nr]
        st_ref = refs[7 + 2 * nr]
        riders = (refs[4:4 + nr], refs[7 + nr:7 + 2 * nr], refs[-2], refs[-1]) if nr else None
        if nr:
            carry.ride(grid, riders, "start")
        c = pl.program_id(1)

        @pl.when(c == 0)
        def _():
            st_ref[...] = jnp.zeros_like(st_ref)

        _, _, f_all, k_all, _, q_all = _hgrn_gates(p_ref[0], p_ref[1], c0_ref[...], c1_ref[...])
        tril, _ = _chunk_masks()
        b_all = _chunk_sums(tril, jnp.log(f_all))
        for j in range(hps):
            ln = slice(j * HD, (j + 1) * HD)
            st0 = st_ref[j]
            sst_ref[j] = st0
            v, g = p_ref[2, :, ln], p_ref[3, :, ln]
            q, k, b = q_all[:, ln], k_all[:, ln], b_all[:, ln]
            bm = b[CHUNK // 2 - 1:CHUNK // 2]
            bl = b[CHUNK - 1:CHUNK]
            qd = q * jnp.exp(b)
            qt = q * jnp.exp(b - bm)
            kt = k * jnp.exp(bm - b)
            kl = k * jnp.exp(bl - b)
            vb = v.astype(BF16)
            att = jnp.where(tril, _dot3(qt, kt, NT), 0.0)
            o = _dg(qd.astype(BF16), st0.astype(BF16), NT) + _dg(att.astype(BF16), vb, NN)
            st_ref[j] = st0 * jnp.exp(bl) + _dg(vb, kl.astype(BF16), TN)
            o_ref[:, ln] = o
            r = lax.rsqrt(jnp.mean(o * o, axis=-1, keepdims=True) + RMS_EPS)
            y_ref[:, ln] = (o * r * gam_ref[...] * (g * _sigmoid(g))).astype(BF16)
        if nr:
            carry.ride(grid, riders, "finish")

    outs = pl.pallas_call(
        body, name="hgrn_fwd", grid=grid,
        in_specs=[_spec((4, CHUNK, wide), lambda h, c: (0, c, h)),
                  _spec((None, 1, wide), lambda h, c: (R_CLB, 0, h)),
                  _spec((None, 1, wide), lambda h, c: (R_CLB + 1, 0, h)),
                  _spec((None, 1, HD), lambda h, c: (R_GAM, 0, 0))] + (carry.in_specs if nr else []),
        out_specs=[_spec((CHUNK, wide), lambda h, c: (c, h)), _spec((CHUNK, wide), lambda h, c: (c, h)),
                   _spec((None, hps, HD, HD), lambda h, c: (c, h, 0, 0))] + (carry.out_specs if nr else []),
        out_shape=[jax.ShapeDtypeStruct((t, D), F32), jax.ShapeDtypeStruct((t, D), BF16),
                   jax.ShapeDtypeStruct((nc, nh, HD, HD), F32)] + (carry.out_shape if nr else []),
        scratch_shapes=[pltpu.VMEM((hps, HD, HD), F32)] + (carry.scratch if nr else []),
        compiler_params=_params(("arbitrary", "arbitrary")),
    )(proj_c, small, small, small, *(carry.groups if nr else []))
    return outs[0], outs[1], outs[2], (carry.place(outs[3:]) if nr else [])


def _hgrn_bwd(proj_c, small, o, sst, dyc, rider=None):
    t = proj_c.shape[1]
    nc = t // CHUNK
    nh = D // HD

    hps = HGRN_HPS
    wide = hps * HD
    ng = nh // hps
    grid = (ng, nc)
    nr = rider.n if rider is not None else 0

    def body(*refs):
        p_ref, c0_ref, c1_ref, gam_ref, o_ref, sst_ref, dy_ref = refs[:7]
        dp_ref, dclb_ref, dgam_ref = refs[7 + nr:10 + nr]
        dst_ref, dlb_acc, dgam_acc = refs[10 + 2 * nr:13 + 2 * nr]
        riders = (refs[7:7 + nr], refs[10 + nr:10 + 2 * nr], refs[-2], refs[-1]) if nr else None
        if nr:
            rider.ride(grid, riders, "start")
        group = pl.program_id(0)
        step = pl.program_id(1)

        @pl.when(step == 0)
        def _():
            dst_ref[...] = jnp.zeros_like(dst_ref)
            dlb_acc[...] = jnp.zeros_like(dlb_acc)

        @pl.when((step == 0) & (group == 0))
        def _():
            dgam_acc[...] = jnp.zeros_like(dgam_acc)

        gam = gam_ref[...]
        qr_all = p_ref[0]
        lb_all, sx_all, f_all, k_all, sq_all, q_all = _hgrn_gates(qr_all, p_ref[1], c0_ref[...], c1_ref[...])
        tril, triu = _chunk_masks()
        b_all = _chunk_sums(tril, jnp.log(f_all))
        dq_parts, dk_parts, db_parts, dgam_parts = [], [], [], []
        for j in range(hps):
            ln = slice(j * HD, (j + 1) * HD)
            st0 = sst_ref[j]
            dst1 = dst_ref[j]
            v, g = p_ref[2, :, ln], p_ref[3, :, ln]
            q, k, b = q_all[:, ln], k_all[:, ln], b_all[:, ln]
            bm = b[CHUNK // 2 - 1:CHUNK // 2]
            bl = b[CHUNK - 1:CHUNK]
            eb = jnp.exp(b)
            e_qt = jnp.exp(b - bm)
            e_kt = jnp.exp(bm - b)
            e_kl = jnp.exp(bl - b)
            e_bl = jnp.exp(bl)
            qd, qt, kt, kl = q * eb, q * e_qt, k * e_kt, k * e_kl
            ov = o_ref[:, ln]
            r = lax.rsqrt(jnp.mean(ov * ov, axis=-1, keepdims=True) + RMS_EPS)
            oh = ov * r
            sg = _sigmoid(g)
            dy = dy_ref[:, ln]
            dp_ref[3, :, ln] = (dy * oh * gam * (sg * (1.0 + g * (1.0 - sg)))).astype(BF16)
            dyv = dy * (g * sg)
            gdy = dyv * gam
            do = (gdy - oh * jnp.mean(gdy * oh, axis=-1, keepdims=True)) * r
            dob, vb = do.astype(BF16), v.astype(BF16)
            st0b, dst1b = st0.astype(BF16), dst1.astype(BF16)
            st1 = st0 * e_bl + _dg(vb, kl.astype(BF16), TN)
            att = jnp.where(tril, _dot3(qt, kt, NT), 0.0)
            datt = jnp.where(tril, _dg(dob, vb, NT), 0.0)
            dv = _dg(att.astype(BF16), dob, TN) + _dg(kl.astype(BF16), dst1b, NT)
            dq = _dot3(datt, kt, NN) * e_qt + _dg(dob, st0b, NN) * eb
            dk = _dot3(datt, qt, TN) * e_kt + _dg(vb, dst1b, NN) * e_kl
            db = q * dq - k * dk
            last = lax.broadcasted_iota(jnp.int32, (CHUNK, 1), 0) == CHUNK - 1
            db = db + jnp.where(last, jnp.sum(dst1 * st1, axis=0, keepdims=True), 0.0)
            dst_ref[j] = dst1 * e_bl + _dg(dob, qd.astype(BF16), TN)
            dp_ref[2, :, ln] = dv.astype(BF16)
            dq_parts.append(dq)
            dk_parts.append(dk)
            db_parts.append(db)
            dgam_parts.append(jnp.sum(dyv * oh, axis=0, keepdims=True))

        dq_all, dk_all = jnp.concatenate(dq_parts, axis=1), jnp.concatenate(dk_parts, axis=1)
        dlf = _chunk_sums(triu, jnp.concatenate(db_parts, axis=1))
        dp_ref[0] = (dq_all * (sq_all * (1.0 + qr_all * (1.0 - sq_all)))).astype(BF16)
        tmp = dlf / f_all - dk_all
        dp_ref[1] = (tmp * (1.0 - lb_all) * sx_all * (1.0 - sx_all)).astype(BF16)
        dlb_acc[...] += jnp.sum((1.0 - sx_all) * tmp, axis=0, keepdims=True)
        dgam_acc[...] += functools.reduce(lambda a, b: a + b, dgam_parts)

        @pl.when(step == nc - 1)
        def _():
            d1 = dlb_acc[...] * lb_all * (1.0 - lb_all)
            dclb_ref[...] = jnp.where(lax.broadcasted_iota(jnp.int32, (2, wide), 0) == 0, -d1, d1)

        @pl.when((step == nc - 1) & (group == ng - 1))
        def _():
            dgam_ref[...] = dgam_acc[...]

        if nr:
            rider.ride(grid, riders, "finish")

    rev = lambda h, s: (nc - 1 - s, h)
    outs = pl.pallas_call(
        body, name="hgrn_bwd", grid=grid,
        in_specs=[_spec((4, CHUNK, wide), lambda h, s: (0, nc - 1 - s, h)),
                  _spec((None, 1, wide), lambda h, s: (R_CLB, 0, h)),
                  _spec((None, 1, wide), lambda h, s: (R_CLB + 1, 0, h)),
                  _spec((None, 1, HD), lambda h, s: (R_GAM, 0, 0)),
                  _spec((CHUNK, wide), rev),
                  _spec((None, hps, HD, HD), lambda h, s: (nc - 1 - s, h, 0, 0)),
                  _spec((CHUNK, wide), rev)] + (rider.in_specs if nr else []),
        out_specs=[_spec((4, CHUNK, wide), lambda h, s: (0, nc - 1 - s, h)),
                   _spec((2, wide), lambda h, s: (0, h)),
                   _spec((1, HD), lambda h, s: (0, 0))] + (rider.out_specs if nr else []),
        out_shape=[jax.ShapeDtypeStruct((4, t, D), BF16), jax.ShapeDtypeStruct((2, D), F32),
                   jax.ShapeDtypeStruct((1, HD), F32)] + (rider.out_shape if nr else []),
        scratch_shapes=[pltpu.VMEM((hps, HD, HD), F32), pltpu.VMEM((1, wide), F32), pltpu.VMEM((1, HD), F32)]
        + (rider.scratch if nr else []),
        compiler_params=_params(("arbitrary", "arbitrary")),
    )(proj_c, small, small, small, o, sst, dyc, *(rider.groups if nr else []))
    return outs[0], outs[1], outs[2], (rider.place(outs[3:]) if nr else [])


def _adamw(name, w, grads, m, v):
    nl = len(grads)
    rows, cols = grads[0].shape
    br = rows
    for cand in (512, 352, 256):
        if rows % cand == 0:
            br = cand
            break
    nb = rows // br
    c1 = 1.0 - ADAM_B1 ** ADAM_STEP
    c2 = 1.0 - ADAM_B2 ** ADAM_STEP

    def body(w_ref, m_ref, v_ref, *refs):
        g_refs, (d_ref, mo_ref, vo_ref, go_ref) = refs[:nl], refs[nl:]
        layer = pl.program_id(0)
        gv = g_refs[0][...]
        for k in range(1, nl):
            gv = jnp.where(layer == k, g_refs[k][...], gv)
        mn = ADAM_B1 * m_ref[...] + (1.0 - ADAM_B1) * gv
        vn = ADAM_B2 * v_ref[...] + (1.0 - ADAM_B2) * (gv * gv)
        mo_ref[...] = mn
        vo_ref[...] = vn
        go_ref[...] = gv
        d_ref[...] = -ADAM_LR * ((mn / c1) / (jnp.sqrt(vn / c2) + ADAM_EPS) + ADAM_WD * w_ref[...])

    blk = _spec((br, cols), lambda l, i: (l * nb + i, 0))
    g_specs = [_spec((br, cols), lambda l, i, k=k: (jnp.where(l == k, i, 0), 0)) for k in range(nl)]
    shape = jax.ShapeDtypeStruct((nl * rows, cols), F32)
    return pl.pallas_call(
        body, name=name, grid=(nl, nb), in_specs=[blk] * 3 + g_specs, out_specs=[blk] * 4, out_shape=[shape] * 4,
        compiler_params=_params(("arbitrary", "arbitrary")),
    )(w, m, v, *grads)


def _place():
    x, y, c = lax.axis_index("x"), lax.axis_index("y"), lax.axis_index("c")
    chips = [(1 - x, y), (x, 1 - y), (1 - x, 1 - y)]
    return x, y, c, chips


class _Rider:
    n = 0
    relay = None

    def ride(self, grid, refs, when):
        if not self.n:
            return
        ids = [pl.program_id(a) for a in range(len(grid))]
        first = functools.reduce(jnp.logical_and, [i == 0 for i in ids])
        last = functools.reduce(jnp.logical_and, [i == g - 1 for i, g in zip(ids, grid)])
        phases = [(first, self.start), (last, self.relay)] if when == "start" else [(last, self.finish)]
        for cond, phase in phases:
            if phase is not None:
                pl.when(cond)(functools.partial(phase, *refs))


class _Gather(_Rider):
    PER_GROUP = 7

    def __init__(self, groups):
        self.groups = list(groups)
        self.n = len(self.groups)
        self.in_specs = [ANY] * self.n
        self.out_specs = [ANY] * self.n
        self.out_shape = [jax.ShapeDtypeStruct((NSH,) + g.shape, g.dtype) for g in self.groups]
        sems = pltpu.SemaphoreType.DMA((self.PER_GROUP * self.n,))
        self.scratch = [sems, sems] if self.n else []

    def _copies(self, ins, outs, send, recv):
        x, y, c, chips = _place()
        sibling = (x, y, 1 - c)

        def half(gi, chip, hc):
            return outs[gi].at[2 * chip[0] + chip[1], hc]

        def copy(gi, k, src, dst, to):
            sem = self.PER_GROUP * gi + k
            return pltpu.make_async_remote_copy(src_ref=src, dst_ref=dst, send_sem=send.at[sem], recv_sem=recv.at[sem],
                                                device_id=to, device_id_type=MESH)

        pairs = [(gi, j, chip) for gi in range(self.n) for j, chip in enumerate(chips)]
        first = [copy(gi, j, ins[gi].at[c], half(gi, (x, y), c), (*chip, c)) for gi, j, chip in pairs]
        first += [copy(gi, 6, ins[gi], outs[gi].at[2 * x + y], sibling) for gi in range(self.n)]
        landed = [copy(gi, j, half(gi, chip, c), half(gi, chip, c), sibling) for gi, j, chip in pairs]
        relay = [copy(gi, 3 + j, half(gi, chip, c), half(gi, chip, c), sibling) for gi, j, chip in pairs]
        relayed = [copy(gi, 3 + j, half(gi, chip, 1 - c), half(gi, chip, 1 - c), sibling) for gi, j, chip in pairs]
        relayed += [copy(gi, 6, ins[gi], outs[gi].at[2 * x + y], sibling) for gi in range(self.n)]
        return first, landed, relay, relayed

    def start(self, ins, outs, send, recv):
        for cp in self._copies(ins, outs, send, recv)[0]:
            cp.start()

    def relay(self, ins, outs, send, recv):
        _, landed, relay, _ = self._copies(ins, outs, send, recv)
        for arrived, onward in zip(landed, relay):
            arrived.wait_recv()
            onward.start()

    def finish(self, ins, outs, send, recv):
        first, _, relay, relayed = self._copies(ins, outs, send, recv)
        for cp in relayed:
            cp.wait_recv()
        for cp in first + relay:
            cp.wait_send()

    def place(self, outs):
        return list(outs)


def _alone(name, rider):
    n = rider.n

    def body(*refs):
        parts = (refs[:n], refs[n:2 * n], refs[2 * n], refs[2 * n + 1])
        rider.start(*parts)
        if rider.relay is not None:
            rider.relay(*parts)
        rider.finish(*parts)

    outs = pl.pallas_call(
        body, name=name, in_specs=rider.in_specs, out_specs=rider.out_specs, out_shape=rider.out_shape,
        scratch_shapes=rider.scratch, compiler_params=pltpu.CompilerParams(has_side_effects=True),
    )(*rider.groups)
    return rider.place(outs)


class _Swap(_Rider):
    def __init__(self, slots):
        self.slots = list(slots)
        self.groups = [buf for buf, _, _ in self.slots]
        self.n = len(self.slots)
        self.in_specs = [ANY] * self.n
        self.out_specs = [ANY] * self.n
        self.out_shape = [jax.ShapeDtypeStruct((NSH, rows // 2, buf.shape[2]), buf.dtype) for buf, _, rows in self.slots]
        self.scratch = [pltpu.SemaphoreType.DMA((self.n,)), pltpu.SemaphoreType.DMA((self.n,))]

    def _copies(self, ins, outs, send, recv):
        x, y, c, _ = _place()
        cps = []
        for i, (_, row0, rows) in enumerate(self.slots):
            half = rows // 2
            src = ins[i].at[:, pl.ds(pl.multiple_of(row0 + (1 - c) * half, 16), half)]
            cps.append(pltpu.make_async_remote_copy(src_ref=src, dst_ref=outs[i], send_sem=send.at[i],
                                                    recv_sem=recv.at[i], device_id=(x, y, 1 - c), device_id_type=MESH))
        return cps

    def start(self, ins, outs, send, recv):
        for cp in self._copies(ins, outs, send, recv):
            cp.start()

    def finish(self, ins, outs, send, recv):
        for cp in self._copies(ins, outs, send, recv):
            cp.wait()

    def place(self, outs):
        return list(outs)


class _Share(_Rider):
    def __init__(self, arrays):
        self.groups = list(arrays)
        self.n = len(self.groups)
        self.in_specs = [ANY] * self.n
        self.out_specs = [ANY] * self.n
        self.out_shape = [jax.ShapeDtypeStruct((2,) + g.shape, g.dtype) for g in self.groups]
        self.scratch = [pltpu.SemaphoreType.DMA((self.n,)), pltpu.SemaphoreType.DMA((self.n,))]

    def _copies(self, ins, outs, send, recv, half):
        x, y, c, _ = _place()
        return [pltpu.make_async_remote_copy(src_ref=ins[gi], dst_ref=outs[gi].at[c if half == "mine" else 1 - c],
                                             send_sem=send.at[gi], recv_sem=recv.at[gi], device_id=(x, y, 1 - c),
                                             device_id_type=MESH) for gi in range(self.n)]

    def start(self, ins, outs, send, recv):
        for cp in self._copies(ins, outs, send, recv, "mine"):
            cp.start()

    def finish(self, ins, outs, send, recv):
        for cp in self._copies(ins, outs, send, recv, "theirs"):
            cp.wait_recv()
        for cp in self._copies(ins, outs, send, recv, "mine"):
            cp.wait_send()

    def place(self, outs):
        c = lax.axis_index("c")
        return [lax.dynamic_update_index_in_dim(o, g, c, 0) for o, g in zip(outs, self.groups)]


class _Send(_Rider):
    def __init__(self, arrays):
        self.groups = list(arrays)
        self.n = len(self.groups)
        self.in_specs = [ANY] * self.n
        self.out_specs = [ANY] * self.n
        self.out_shape = [jax.ShapeDtypeStruct((3,) + g.shape[1:], g.dtype) for g in self.groups]
        self.scratch = [pltpu.SemaphoreType.DMA((3 * self.n,)), pltpu.SemaphoreType.DMA((3 * self.n,))]

    def _copies(self, ins, outs, send, recv):
        x, y, c, chips = _place()
        return [pltpu.make_async_remote_copy(src_ref=ins[gi].at[2 * chip[0] + chip[1]], dst_ref=outs[gi].at[j],
                                             send_sem=send.at[3 * gi + j], recv_sem=recv.at[3 * gi + j],
                                             device_id=(*chip, c), device_id_type=MESH)
                for gi in range(self.n) for j, chip in enumerate(chips)]

    def start(self, ins, outs, send, recv):
        for cp in self._copies(ins, outs, send, recv):
            cp.start()

    def finish(self, ins, outs, send, recv):
        for cp in self._copies(ins, outs, send, recv):
            cp.wait()

    def place(self, outs):
        return list(outs)


def _pair_sum(name, slots, got, c_idx):
    n = len(slots)
    in_specs, out_specs, out_shape, operands = [], [], [], []
    for (buf, row0, rows), g in zip(slots, got):
        hb, cols = rows // 4, buf.shape[2]
        in_specs += [pl.BlockSpec((None, hb, cols), lambda q, i, cr, r=row0 // hb: (q, r + 2 * cr[0] + i, 0)),
                     pl.BlockSpec((None, hb, cols), lambda q, i, cr: (q, i, 0))]
        out_specs.append(pl.BlockSpec((None, hb, cols), lambda q, i, cr: (q, i, 0)))
        out_shape.append(jax.ShapeDtypeStruct(g.shape, BF16))
        operands += [buf, g]

    def body(c_ref, *refs):
        for i in range(n):
            refs[2 * n + i][...] = (refs[2 * i][...].astype(F32) + refs[2 * i + 1][...].astype(F32)).astype(BF16)

    return pl.pallas_call(
        body, name=name,
        grid_spec=pltpu.PrefetchScalarGridSpec(num_scalar_prefetch=1, grid=(NSH, 2), in_specs=in_specs, out_specs=out_specs),
        out_shape=out_shape, compiler_params=_params(("parallel", "parallel")),
    )(c_idx, *operands)


def _owner_sum(name, pairs, got, p_idx):
    n = len(pairs)
    in_specs, out_specs, out_shape, operands = [], [], [], []
    for own, g in zip(pairs, got):
        _, rows, cols = own.shape
        hb = rows // 2
        in_specs += [pl.BlockSpec((None, hb, cols), lambda i, pr: (pr[0], i, 0)),
                     pl.BlockSpec((3, hb, cols), lambda i, pr: (0, i, 0))]
        out_specs.append(pl.BlockSpec((hb, cols), lambda i, pr: (i, 0)))
        out_shape.append(jax.ShapeDtypeStruct((rows, cols), F32))
        operands += [own, g]

    def body(p_ref, *refs):
        for i in range(n):
            a_ref, b_ref = refs[2 * i], refs[2 * i + 1]
            refs[2 * n + i][...] = ((a_ref[...].astype(F32) + b_ref[0].astype(F32)) + b_ref[1].astype(F32)) + b_ref[2].astype(F32)

    return pl.pallas_call(
        body, name=name,
        grid_spec=pltpu.PrefetchScalarGridSpec(num_scalar_prefetch=1, grid=(2,), in_specs=in_specs, out_specs=out_specs),
        out_shape=out_shape, compiler_params=_params(("parallel",)),
    )(p_idx, *operands)


def _sum_small(slab):
    def body(in_ref, out_ref, all_ref, send, recv):
        x, y, c, _ = _place()
        me = 4 * x + 2 * y + c
        all_ref[me] = in_ref[...]
        cps = []
        for k in range(1, 8):
            peer = (x ^ (k >> 2), y ^ ((k >> 1) & 1), c ^ (k & 1))
            cps.append(pltpu.make_async_remote_copy(src_ref=in_ref, dst_ref=all_ref.at[me], send_sem=send.at[k - 1],
                                                    recv_sem=recv.at[k - 1], device_id=peer, device_id_type=MESH))
        for cp in cps:
            cp.start()
        for cp in cps:
            cp.wait()
        total = all_ref[0]
        for d in range(1, 8):
            total = total + all_ref[d]
        out_ref[...] = total

    return pl.pallas_call(
        body, name="sum_small",
        in_specs=[pl.BlockSpec(memory_space=pltpu.VMEM)], out_specs=pl.BlockSpec(memory_space=pltpu.VMEM),
        out_shape=jax.ShapeDtypeStruct(slab.shape, F32),
        scratch_shapes=[pltpu.VMEM((8,) + slab.shape, F32), pltpu.SemaphoreType.DMA((7,)), pltpu.SemaphoreType.DMA((7,))],
        compiler_params=pltpu.CompilerParams(has_side_effects=True),
    )(slab)


FFNS = ("pre0", "post0", "pre1", "post1")
W_SHAPES = dict({f + "_gu": (NSH, 2, FS, D) for f in FFNS}, **{f + "_d": (NSH, FS, D) for f in FFNS},
                ab_in=(NSH, D, 768), ab_out=(NSH, 256, D), conv=(NSH, 2, 8, LANE), c_in=(NSH, D, D), c_out=(NSH, 256, D))
CARRIED = dict(pre0_norm=("pre0_gu",), pre0_up=("pre0_d",), pre0_down=("ab_in", "conv"), ab_proj_a=("ab_out",),
               c_proj=("post1_d",),
               sb_fwd=("post0_gu", "post0_d"), post0_up=("pre1_gu",), post0_down=("pre1_d",),
               pre1_up=("c_in", "c_out"), hgrn_fwd=("post1_gu",))


FFN_SLOTS = dict(pre0=(0, 1, 0), pre1=(2, 3, 1), post0=(4, 5, 2), post1=(6, 7, 3))
GRAD_SLOTS = dict(ab_out=("b", B_ABOUT, 256), c_in=("b", B_CIN, D), c_out=("b", B_COUT, 256), ab_in=("c", 0, D))
for _f, (_g, _u, _d) in FFN_SLOTS.items():
    GRAD_SLOTS.update({_f + "_g": ("a", _g * FS, FS), _f + "_u": ("a", _u * FS, FS), _f + "_d": ("b", _d * FS, FS)})
REDUCE_STAGES = dict(x=("post1_g", "post1_u", "post1_d", "c_out"), w=("c_in", "pre1_d"),
                     y=("pre1_g", "pre1_u", "post0_g", "post0_u", "post0_d", "ab_out"),
                     z1=("ab_in",), z2=("pre0_g", "pre0_u", "pre0_d"))


def _local_step(x, target, weights, small, shards=None, place=None):
    t = x.shape[0]
    tm = min(MM_TILE, t)
    tw = min(WGRAD_TILE, t)
    nt = t // tm
    tok_si = _spec((tm, D), lambda s, i: (i, 0))
    tok_is = _spec((tm, D), lambda i, s: (i, 0))
    w = dict(weights)
    reduced = {}

    def stage_slots(stage, buffers):
        return [(buffers[GRAD_SLOTS[n][0]],) + GRAD_SLOTS[n][1:] for n in REDUCE_STAGES[stage]]

    def swap_rider(stage, buffers):
        return _Swap(stage_slots(stage, buffers)) if place is not None else None

    def reduce_start(stage, buffers, swapped=None):
        if place is None:
            return [], None
        slots = stage_slots(stage, buffers)
        if swapped is None:
            swapped = _alone("grad_swap_" + stage, _Swap(slots))
        pairs = _pair_sum("grad_pair_sum_" + stage, slots, swapped, place[0])
        return pairs, _Send(pairs)

    def reduce_end(stage, pairs, landed):
        if place is None:
            return None
        mine = _owner_sum("grad_owner_sum_" + stage, pairs, landed, place[1])
        return _Share(mine)

    def shared(stage, landed):
        if place is not None:
            reduced.update(zip(REDUCE_STAGES[stage], landed))

    def carried(kernel_name):
        names = [n for n in CARRIED[kernel_name] if n not in w]
        return names, (_Gather([shards[n] for n in names]) if names else None)

    def land(names, arrays):
        for n, a in zip(names, arrays):
            w[n] = a.reshape(W_SHAPES[n])

    def ffn_forward(tag, h, hn, next_row):
        names, gather = carried(tag + "_up") if tag + "_up" in CARRIED else ([], None)
        s_up, s_gate, a, got = _ffn_up(tag + "_up", hn, w[tag + "_gu"], gather)
        land(names, got)
        names, gather = carried(tag + "_down") if tag + "_down" in CARRIED else ([], None)
        out = _ffn_down(tag + "_down", a, w[tag + "_d"], h, gather, (small, next_row) if next_row is not None else None)
        if gather is not None:
            out, got = out
            land(names, got)
        out, hn_next = out if next_row is not None else (out, None)
        return out, hn_next, (h, hn, s_up, s_gate, a)

    def out_proj(name, y, w_out, h, next_row):
        return _mm(name, [(y, _spec((tm, 256), lambda i, k: (i, k)), w_out, _spec((None, 256, D), lambda i, k: (k, 0, 0)))],
                   grid=(nt, NSH), o_shape=(t, D), o_dtype=F32, o_spec=_spec((tm, D), lambda i, k: (i, 0)),
                   dims=NN, kaxis=1, nk=NSH, acc_shape=(tm, D), res=(h, _spec((tm, D), lambda i, k: (i, 0))),
                   norm=(small, next_row))

    def out_proj_bwd(tag, dhb, y, w_out, blk, grad_b, make_rider=None):
        grad_b = _wgrad(tag + "_dwout", y, _spec((tw, 256), lambda s, k: (k, s)), dhb, _spec((tw, D), lambda s, k: (k, 0)),
                        256, D, t, tw, grad_b, blk)
        rider = make_rider(grad_b) if make_rider is not None else None
        dy = _mm(tag + "_dy", [(dhb, tok_is, w_out, _spec((None, 256, D), lambda i, s: (s, 0, 0)))],
                 grid=(nt, NSH), o_shape=(t, D), o_dtype=F32, o_spec=_spec((tm, 256), lambda i, s: (i, s)),
                 dims=NT, kaxis=1, nk=1, carry=rider)
        dy, landed = dy if rider is not None else (dy, None)
        return dy, grad_b, landed

    h0 = x
    names, gather = carried("pre0_norm")
    hn, got = _norm_fwd("pre0_norm", h0, small, R_PRE, gather)
    land(names, got)
    h1, hn_ab, pre0 = ffn_forward("pre0", h0, hn, R_MIX)
    def carrying(kernel_name, make):
        names, gather = carried(kernel_name)
        out = make(gather)
        if gather is not None:
            out, got = out
            land(names, got)
        return out

    proj_a = carrying("ab_proj_a", lambda gather: _mm(
        "ab_proj_a", [(hn_ab, tok_is, w["ab_in"], _spec((None, D, 768), lambda i, s: (s, 0, 0)))],
        grid=(nt, 2), o_shape=(t, 1536), o_dtype=F32, o_spec=_spec((tm, 768), lambda i, s: (i, s)),
        dims=NN, kaxis=1, nk=1, carry=gather))
    proj_b = _mm("ab_proj_b", [(hn_ab, tok_is, w["ab_in"], _spec((None, D, 768), lambda i, s: (s + 2, 0, 0)))],
                 grid=(nt, 2), o_shape=(t, 1536), o_dtype=BF16, o_spec=_spec((tm, 768), lambda i, s: (i, s)),
                 dims=NN, kaxis=1, nk=1)
    y_a = _conv_fwd(proj_a, w["conv"])
    names, gather = carried("sb_fwd")
    y_b, ltot, nblk, got = _sb_fwd(proj_b, gather)
    land(names, got)
    y_ab = jnp.concatenate([y_a, y_b], axis=1)
    h2, hn = out_proj("ab_out", y_ab, w["ab_out"], h1, R_POST)
    h3, hn, post0 = ffn_forward("post0", h2, hn, R_PRE + 1)
    h4, hn_c, pre1 = ffn_forward("pre1", h3, hn, R_MIX + 1)
    proj_c = carrying("c_proj", lambda gather: _mm(
        "c_proj", [(hn_c, tok_si, w["c_in"], _spec((None, D, D), lambda s, i: (s, 0, 0)))],
        grid=(NSH, nt), o_shape=(NSH, t, D), o_dtype=F32, o_spec=_spec((None, tm, D), lambda s, i: (s, i, 0)),
        dims=NN, kaxis=1, nk=1, carry=gather))
    names, gather = carried("hgrn_fwd")
    o_c, y_c, sst, got = _hgrn_fwd(proj_c, small, gather)
    land(names, got)
    h5, hn = out_proj("c_out", y_c, w["c_out"], h4, R_POST + 1)
    h6, _, post1 = ffn_forward("post1", h5, hn, None)
    dh, dhb, d_fin, loss = _final_loss(h6, small, target)

    dh, dhb, grad_a, grad_b, dn_post1, _, _ = _ffn_backward("post1", dh, dhb, *post1, w["post1_gu"], w["post1_d"],
                                                            *FFN_SLOTS["post1"], small, R_POST + 1, 8 * FS, B_ROWS)
    dy_c, grad_b, swapped = out_proj_bwd("c", dhb, y_c, w["c_out"], B_COUT // 256, grad_b,
                                         lambda buf_b: swap_rider("x", dict(a=grad_a, b=buf_b)))
    pairs, rider = reduce_start("x", dict(a=grad_a, b=grad_b), swapped)
    dproj_c, d_clb, d_gam, landed = _hgrn_bwd(proj_c, small, o_c, sst, dy_c, rider)
    share = reduce_end("x", pairs, landed)
    grad_b = _wgrad("c_dwin", hn_c, _spec((tw, D), lambda s, k: (k, 0)), dproj_c, _spec((None, tw, D), lambda s, k: (s, k, 0)),
                    D, D, t, tw, grad_b, B_CIN // D, carry=share)
    if share is not None:
        grad_b, landed = grad_b
        shared("x", landed)
    dh, dhb, dn_mix1, _ = _dhn_norm("c_dhn", [(dproj_c, _spec((None, tm, D), lambda i, k: (k, i, 0)),
                                               w["c_in"], _spec((None, D, D), lambda i, k: (k, 0, 0)))],
                                    NT, h4, small, R_MIX + 1, dh)
    stage_w = {}

    def w_swap(buf_a, buf_b):
        return swap_rider("w", dict(a=buf_a, b=buf_b))

    def w_send(buf_a, buf_b, swapped):
        stage_w["pairs"], rider = reduce_start("w", dict(a=buf_a, b=buf_b), swapped)
        return rider

    on = place is not None
    dh, dhb, grad_a, grad_b, dn_pre1, _, landed = _ffn_backward(
        "pre1", dh, dhb, *pre1, w["pre1_gu"], w["pre1_d"], *FFN_SLOTS["pre1"], small, R_PRE + 1, grad_a, grad_b,
        w_swap if on else None, w_send if on else None)
    share = reduce_end("w", stage_w.get("pairs"), landed)
    dh, dhb, grad_a, grad_b, dn_post0, landed, _ = _ffn_backward(
        "post0", dh, dhb, *post0, w["post0_gu"], w["post0_d"], *FFN_SLOTS["post0"], small, R_POST, grad_a, grad_b,
        (lambda buf_a, buf_b: share) if on else None)
    shared("w", landed)
    dy_ab, grad_b, _ = out_proj_bwd("ab", dhb, y_ab, w["ab_out"], B_ABOUT // 256, grad_b)
    dab, dac, dax, d_conv, swapped = _conv_bwd(proj_a, w["conv"], dy_ab, swap_rider("y", dict(a=grad_a, b=grad_b)))
    pairs, rider = reduce_start("y", dict(a=grad_a, b=grad_b), swapped)
    dq, dk, dv, landed = _sb_bwd(proj_b, dy_ab, ltot, nblk, rider)
    share = reduce_end("y", pairs, landed)
    dproj_ab = jnp.concatenate([dab, dac, dax, dq, dk, dv], axis=1)
    grad_c = _wgrad("ab_dwin", hn_ab, _spec((tw, D), lambda s, k: (k, 0)), dproj_ab, _spec((tw, 768), lambda s, k: (k, s)),
                    D, 768, t, tw, D, 0, carry=share)
    if share is not None:
        grad_c, landed = grad_c
        shared("y", landed)
    dh, dhb, dn_mix0, _ = _dhn_norm("ab_dhn", [(dproj_ab, _spec((tm, 768), lambda i, k: (i, k)),
                                                w["ab_in"], _spec((None, D, 768), lambda i, k: (k, 0, 0)))],
                                    NT, h1, small, R_MIX, dh)
    last = {}

    def z1_send(buf_a, buf_b):
        last["z1"], rider = reduce_start("z1", dict(b=buf_b, c=grad_c))
        return rider

    def z2_send(buf_a, buf_b, _):
        last["z2"], rider = reduce_start("z2", dict(a=buf_a, b=buf_b))
        return rider

    dh, dhb, grad_a, grad_b, dn_pre0, landed, landed_own = _ffn_backward(
        "pre0", dh, dhb, *pre0, w["pre0_gu"], w["pre0_d"], *FFN_SLOTS["pre0"], small, R_PRE, grad_a, grad_b,
        z1_send if on else None, z2_send if on else None)
    if place is not None:
        mine = (_owner_sum("grad_owner_sum_z1", last["z1"], landed, place[1])
                + _owner_sum("grad_owner_sum_z2", last["z2"], landed_own, place[1]))
        reduced.update(zip(REDUCE_STAGES["z1"] + REDUCE_STAGES["z2"], _alone("grad_share_z", _Share(mine))))
    zero = jnp.zeros((1, D), F32)
    conv_rows = jnp.pad(jnp.transpose(d_conv[:, :3, :], (1, 0, 2)).reshape(3, 512), ((0, 0), (0, D - 512)))
    small_grad = jnp.concatenate([
        dn_pre0, dn_pre1, dn_mix0, dn_mix1, dn_post0, dn_post1, d_clb, d_fin,
        jnp.pad(d_gam, ((0, 0), (0, D - HD))), conv_rows,
        jnp.pad(loss, ((0, 0), (0, D - 1))), zero, zero], axis=0)
    return dh, grad_a, grad_b, grad_c, small_grad, reduced


def _small_slab(rows):
    parts = [jnp.pad(r.astype(F32), ((0, 0), (0, D - r.shape[1]))) for r in rows]
    slab = jnp.concatenate(parts, axis=0)
    return jnp.pad(slab, ((0, SMALL_ROWS - slab.shape[0]), (0, 0)))


def kernel(x, ffn_pre_norm, ffn_pre_w_gate, ffn_pre_w_up, ffn_pre_w_down, mix_norm, ffn_post_norm, ffn_post_w_gate, ffn_post_w_up, ffn_post_w_down, ab_w_in, ab_conv_w, ab_w_out, c_w_in, c_lower_bounds, c_out_norm, c_w_out, final_norm, loss_target, m_ffn_pre_norm, m_ffn_pre_w_gate, m_ffn_pre_w_up, m_ffn_pre_w_down, m_mix_norm, m_ffn_post_norm, m_ffn_post_w_gate, m_ffn_post_w_up, m_ffn_post_w_down, m_ab_w_in, m_ab_conv_w, m_ab_w_out, m_c_w_in, m_c_lower_bounds, m_c_out_norm, m_c_w_out, m_final_norm, v_ffn_pre_norm, v_ffn_pre_w_gate, v_ffn_pre_w_up, v_ffn_pre_w_down, v_mix_norm, v_ffn_post_norm, v_ffn_post_w_gate, v_ffn_post_w_up, v_ffn_post_w_down, v_ab_w_in, v_ab_conv_w, v_ab_w_out, v_c_w_in, v_c_lower_bounds, v_c_out_norm, v_c_w_out, v_final_norm):
    t = x.shape[1]
    xi, yi, ci = lax.axis_index("x"), lax.axis_index("y"), lax.axis_index("c")
    p_idx = (2 * xi + yi).astype(jnp.int32).reshape(1)
    c_idx = ci.astype(jnp.int32).reshape(1)

    def halves(m):
        return m.astype(BF16).reshape(2, m.shape[0] // 2, m.shape[1])

    transposed = ("ffn_pre_w_gate", "ffn_pre_w_up", "ffn_post_w_gate", "ffn_post_w_up")

    def flip(a):
        return jnp.swapaxes(a, 1, 2)

    shards = {}
    for name, (w_gate, w_up, w_down, layer) in dict(
            pre0=(ffn_pre_w_gate, ffn_pre_w_up, ffn_pre_w_down, 0), post0=(ffn_post_w_gate, ffn_post_w_up, ffn_post_w_down, 0),
            pre1=(ffn_pre_w_gate, ffn_pre_w_up, ffn_pre_w_down, 1), post1=(ffn_post_w_gate, ffn_post_w_up, ffn_post_w_down, 1)).items():
        shards[name + "_gu"] = jnp.stack([flip(w_gate)[layer], flip(w_up)[layer]]).astype(BF16)
        shards[name + "_d"] = halves(w_down[layer])
    conv_pad = jnp.pad(ab_conv_w[0], ((0, 5), (0, 0)))
    shards.update(ab_in=halves(ab_w_in[0]), ab_out=halves(ab_w_out[0]), c_in=halves(c_w_in[0]), c_out=halves(c_w_out[0]),
                  conv=jnp.stack([conv_pad, jnp.zeros_like(conv_pad)]))
    small = _small_slab([ffn_pre_norm, mix_norm, ffn_post_norm, c_lower_bounds, final_norm.reshape(1, D), c_out_norm])
    small = small.reshape(SMALL_ROWS, 1, D)

    grad_x, _, _, _, small_grad, reduced = _local_step(x[0], loss_target[0], {}, small, shards,
                                                        (c_idx, p_idx))
    whole = {n: g.reshape(2 * g.shape[1], g.shape[2]) for n, g in reduced.items()}
    small_sum = _sum_small(small_grad)

    my_conv = lax.dynamic_slice(small_sum[R_CONV:R_CONV + 3], (0, (2 * xi + yi) * 128), (3, 128))
    grads = {
        "ffn_pre_norm": small_sum[R_PRE:R_PRE + 2], "mix_norm": small_sum[R_MIX:R_MIX + 2],
        "ffn_post_norm": small_sum[R_POST:R_POST + 2], "c_lower_bounds": small_sum[R_CLB:R_CLB + 2],
        "c_out_norm": small_sum[R_GAM:R_GAM + 1, :HD], "final_norm": small_sum[R_FIN],
        "ab_conv_w": my_conv.reshape(1, 3, 128),
    }
    layers = dict(ab_w_in=[whole["ab_in"]], ab_w_out=[whole["ab_out"]], c_w_in=[whole["c_in"]], c_w_out=[whole["c_out"]])
    for kind, key in (("gate", "_g"), ("up", "_u"), ("down", "_d")):
        layers["ffn_pre_w_" + kind] = [whole["pre0" + key], whole["pre1" + key]]
        layers["ffn_post_w_" + kind] = [whole["post0" + key], whole["post1" + key]]
    weights = dict(ffn_pre_norm=ffn_pre_norm, ffn_pre_w_gate=ffn_pre_w_gate, ffn_pre_w_up=ffn_pre_w_up, ffn_pre_w_down=ffn_pre_w_down, mix_norm=mix_norm, ffn_post_norm=ffn_post_norm, ffn_post_w_gate=ffn_post_w_gate, ffn_post_w_up=ffn_post_w_up, ffn_post_w_down=ffn_post_w_down, ab_w_in=ab_w_in, ab_conv_w=ab_conv_w, ab_w_out=ab_w_out, c_w_in=c_w_in, c_lower_bounds=c_lower_bounds, c_out_norm=c_out_norm, c_w_out=c_w_out, final_norm=final_norm)
    m_in = dict(ffn_pre_norm=m_ffn_pre_norm, ffn_pre_w_gate=m_ffn_pre_w_gate, ffn_pre_w_up=m_ffn_pre_w_up, ffn_pre_w_down=m_ffn_pre_w_down, mix_norm=m_mix_norm, ffn_post_norm=m_ffn_post_norm, ffn_post_w_gate=m_ffn_post_w_gate, ffn_post_w_up=m_ffn_post_w_up, ffn_post_w_down=m_ffn_post_w_down, ab_w_in=m_ab_w_in, ab_conv_w=m_ab_conv_w, ab_w_out=m_ab_w_out, c_w_in=m_c_w_in, c_lower_bounds=m_c_lower_bounds, c_out_norm=m_c_out_norm, c_w_out=m_c_w_out, final_norm=m_final_norm)
    v_in = dict(ffn_pre_norm=v_ffn_pre_norm, ffn_pre_w_gate=v_ffn_pre_w_gate, ffn_pre_w_up=v_ffn_pre_w_up, ffn_pre_w_down=v_ffn_pre_w_down, mix_norm=v_mix_norm, ffn_post_norm=v_ffn_post_norm, ffn_post_w_gate=v_ffn_post_w_gate, ffn_post_w_up=v_ffn_post_w_up, ffn_post_w_down=v_ffn_post_w_down, ab_w_in=v_ab_w_in, ab_conv_w=v_ab_conv_w, ab_w_out=v_ab_w_out, c_w_in=v_c_w_in, c_lower_bounds=v_c_lower_bounds, c_out_norm=v_c_out_norm, c_w_out=v_c_w_out, final_norm=v_final_norm)
    names = list(weights)
    big = [n for n in names if weights[n].size >= 65536]
    tiny = [n for n in names if n not in big]

    delta, new_m, new_v = {}, {}, {}
    for n in big:
        turn = flip if n in transposed else (lambda a: a)
        shape = turn(weights[n]).shape
        two_d = (shape[0] * shape[1], shape[2])
        d, m2, v2, g2 = _adamw("adamw_" + n, turn(weights[n]).reshape(two_d), layers[n],
                               turn(m_in[n]).reshape(two_d), turn(v_in[n]).reshape(two_d))
        delta[n], new_m[n], new_v[n] = turn(d.reshape(shape)), turn(m2.reshape(shape)), turn(v2.reshape(shape))
        grads[n] = turn(g2.reshape(shape))

    def tiny_slab(src):
        return _small_slab([src[n].reshape(-1, src[n].shape[-1]) for n in tiny])

    offs, row = {}, 0
    for n in tiny:
        nrows = weights[n].size // weights[n].shape[-1]
        offs[n] = (row, nrows)
        row += nrows
    d, m2, v2, _ = _adamw("adamw_small", tiny_slab(weights), [tiny_slab(grads)], tiny_slab(m_in), tiny_slab(v_in))
    for n in tiny:
        r0, nr = offs[n]
        shape = weights[n].shape
        for dst, src in ((delta, d), (new_m, m2), (new_v, v2)):
            dst[n] = src[r0:r0 + nr, :shape[-1]].reshape(shape)

    loss = small_sum[R_LOSS, 0]
    return (loss, grad_x.reshape(1, t, D), *[grads[n] for n in names], *[delta[n] for n in names],
            *[new_m[n] for n in names], *[new_v[n] for n in names])
```

```python
import functools
import math

import jax
import jax.numpy as jnp
from jax import lax
from jax.experimental import pallas as pl
from jax.experimental.pallas import tpu as pltpu

F32 = jnp.float32
BF16 = jnp.bfloat16
MESH = pl.DeviceIdType.MESH
ANY = pl.BlockSpec(memory_space=pl.ANY)

D = 1024
FS = 704
NSH = 4
RMS_EPS = 1e-6
MACARON = 0.5
CHUNK = 64
HD = 128
HGRN_HPS = 8
SBQ = 128
SB_PAIRS_FWD = 4
SB_PAIRS_BWD = 4
SB_DEAD = -105.0
CONV_HALO = 8
LANE = 128
ROW_TILE = 1024
MM_TILE = 1024
WGRAD_TILE = 4096
VMEM_LIMIT = 48 * 1024 * 1024
VMEM_LIMIT_BIG = 58 * 1024 * 1024

ADAM_LR, ADAM_B1, ADAM_B2, ADAM_EPS, ADAM_WD, ADAM_STEP = 0.001, 0.9, 0.999, 1e-08, 0.01, 10

NN = ((1,), (0,))
NT = ((1,), (1,))
TN = ((0,), (0,))

SMALL_ROWS = 16
R_PRE, R_MIX, R_POST, R_CLB, R_FIN, R_GAM, R_CONV, R_LOSS = 0, 2, 4, 6, 8, 9, 10, 13

B_ABOUT = 4 * FS
B_CIN = B_ABOUT + 256
B_COUT = B_CIN + 1024
B_ROWS = B_COUT + 256


def _dg(a, b, dims):
    return lax.dot_general(a, b, (dims, ((), ())), preferred_element_type=F32)


def _split(x):
    hi = x.astype(BF16)
    lo = (x - hi.astype(F32)).astype(BF16)
    return hi, lo


def _dot3(a, b, dims):
    ah, al = _split(a)
    bh, bl = _split(b)
    if dims == TN:
        n = b.shape[1]
        both = _dg(ah, jnp.concatenate([bh, bl], axis=1), dims)
        return both[:, :n] + both[:, n:] + _dg(al, bh, dims)
    m = a.shape[0]
    both = _dg(jnp.concatenate([ah, al], axis=0), bh, dims)
    return both[:m] + both[m:] + _dg(ah, bl, dims)


def _sigmoid(x):
    return 1.0 / (1.0 + jnp.exp(-x))


def _params(sem):
    return pltpu.CompilerParams(dimension_semantics=sem, vmem_limit_bytes=VMEM_LIMIT)


def _spec(shape, imap):
    return pl.BlockSpec(shape, imap)


def _accumulate(acc_ref, pairs, dims):
    part = None
    for a_ref, b_ref in pairs:
        d = _dg(a_ref[...], b_ref[...], dims)
        part = d if part is None else part + d
    acc_ref[...] += part


def _mm(name, pairs, *, grid, o_shape, o_dtype, o_spec, dims, kaxis, nk, acc_shape=None, res=None, scale=None,
        into=None, carry=None, norm=None):
    npairs = len(pairs)
    operands, specs = [], []
    for a, a_spec, b, b_spec in pairs:
        operands += [a, b]
        specs += [a_spec, b_spec]
    if res is not None:
        operands.append(res[0])
        specs.append(res[1])
    aliases = {}
    if into is not None:
        aliases = {len(operands): 0}
        operands.append(into)
        specs.append(ANY)
    if norm is not None:
        operands.append(norm[0])
        specs.append(_spec((None, 1, D), lambda *_: (norm[1], 0, 0)))
    n_own = len(operands)
    n_out = 2 if norm is not None else 1
    nc = carry.n if carry is not None else 0
    if nc:
        operands += carry.groups
        specs += carry.in_specs

    def body(*refs):
        o_ref = refs[n_own + nc]
        riders = ((refs[n_own:n_own + nc], refs[n_own + nc + n_out:n_own + 2 * nc + n_out], refs[-2], refs[-1])
                  if nc else None)
        if nc:
            carry.ride(grid, riders, "start")

        def finish(val):
            if scale is not None:
                val = val * scale
            if res is not None:
                val = val + refs[2 * npairs][...]
            o_ref[...] = val.astype(o_dtype)
            if norm is not None:
                r = lax.rsqrt(jnp.mean(val * val, axis=-1, keepdims=True) + RMS_EPS)
                refs[n_own + nc + 1][...] = (val * r * refs[n_own - 1][...]).astype(BF16)

        if nk == 1:
            part = None
            for n in range(npairs):
                d = _dg(refs[2 * n][...], refs[2 * n + 1][...], dims)
                part = d if part is None else part + d
            finish(part)
        else:
            acc_ref = refs[n_own + 2 * nc + n_out]
            k = pl.program_id(kaxis)

            @pl.when(k == 0)
            def _():
                acc_ref[...] = jnp.zeros_like(acc_ref)

            _accumulate(acc_ref, [(refs[2 * n], refs[2 * n + 1]) for n in range(npairs)], dims)

            @pl.when(k == nk - 1)
            def _():
                finish(acc_ref[...])
        if nc:
            carry.ride(grid, riders, "finish")

    sem = tuple("arbitrary" if (nc or (ax == kaxis and nk > 1)) else "parallel" for ax in range(len(grid)))
    outs = pl.pallas_call(
        body, name=name, grid=grid, in_specs=specs,
        out_specs=[o_spec] * n_out + (carry.out_specs if nc else []),
        out_shape=[jax.ShapeDtypeStruct(o_shape, o_dtype)] + ([jax.ShapeDtypeStruct(o_shape, BF16)] if norm is not None else [])
        + (carry.out_shape if nc else []),
        scratch_shapes=([pltpu.VMEM(acc_shape, F32)] if nk > 1 else []) + (carry.scratch if nc else []),
        input_output_aliases=aliases,
        compiler_params=_params(sem),
    )(*operands)
    main = (outs[0], outs[1]) if norm is not None else outs[0]
    return (main, carry.place(outs[n_out:])) if nc else main


def _norm_fwd(name, h, gain_slab, row, rider=None):
    t = h.shape[0]
    tm = min(ROW_TILE, t)
    grid = (t // tm,)
    nr = rider.n if rider is not None else 0

    def body(*refs):
        h_ref, g_ref, o_ref = refs[0], refs[1], refs[2 + nr]
        riders = (refs[2:2 + nr], refs[3 + nr:3 + 2 * nr], refs[-2], refs[-1]) if nr else None
        if nr:
            rider.ride(grid, riders, "start")
        x = h_ref[...]
        r = lax.rsqrt(jnp.mean(x * x, axis=-1, keepdims=True) + RMS_EPS)
        o_ref[...] = (x * r * g_ref[...]).astype(BF16)
        if nr:
            rider.ride(grid, riders, "finish")

    outs = pl.pallas_call(
        body, name=name, grid=grid,
        in_specs=[_spec((tm, D), lambda i: (i, 0)), _spec((None, 1, D), lambda i: (row, 0, 0))]
        + (rider.in_specs if nr else []),
        out_specs=[_spec((tm, D), lambda i: (i, 0))] + (rider.out_specs if nr else []),
        out_shape=[jax.ShapeDtypeStruct((t, D), BF16)] + (rider.out_shape if nr else []),
        scratch_shapes=rider.scratch if nr else [],
        compiler_params=_params(("arbitrary",) if nr else ("parallel",)),
    )(h, gain_slab, *(rider.groups if nr else []))
    return outs[0], (rider.place(outs[1:]) if nr else [])


def _dhn_norm(name, pairs, dims, h, gain_slab, row, dres, rider=None):
    t = h.shape[0]
    tm = min(MM_TILE, t)
    nt = t // tm
    grid = (nt, NSH)
    npairs = len(pairs)
    nr = rider.n if rider is not None else 0
    operands, specs = [], []
    for a, a_spec, b, b_spec in pairs:
        operands += [a, b]
        specs += [a_spec, b_spec]
    row_spec = _spec((tm, D), lambda i, k: (i, 0))
    operands += [h, gain_slab, dres]
    specs += [row_spec, _spec((None, 1, D), lambda i, k: (row, 0, 0)), row_spec]
    n_own = len(operands)

    def body(*refs):
        h_ref, g_ref, dres_ref = refs[2 * npairs:n_own]
        dh_ref, dhb_ref, dg_ref = refs[n_own + nr:n_own + nr + 3]
        acc_ref, gacc_ref = refs[n_own + 2 * nr + 3:n_own + 2 * nr + 5]
        riders = (refs[n_own:n_own + nr], refs[n_own + nr + 3:n_own + 2 * nr + 3], refs[-2], refs[-1]) if nr else None
        if nr:
            rider.ride(grid, riders, "start")
        i, k = pl.program_id(0), pl.program_id(1)

        @pl.when(k == 0)
        def _():
            acc_ref[...] = jnp.zeros_like(acc_ref)

        _accumulate(acc_ref, [(refs[2 * n], refs[2 * n + 1]) for n in range(npairs)], dims)

        @pl.when(k == NSH - 1)
        def _():
            x = h_ref[...]
            r = lax.rsqrt(jnp.mean(x * x, axis=-1, keepdims=True) + RMS_EPS)
            xh = x * r
            dy = acc_ref[...]
            gdy = dy * g_ref[...]
            dh = dres_ref[...] + (gdy - xh * jnp.mean(gdy * xh, axis=-1, keepdims=True)) * r
            dh_ref[...] = dh
            dhb_ref[...] = dh.astype(BF16)
            gpart = jnp.sum((dy * xh).reshape(tm // 8, 8, D), axis=0)

            @pl.when(i == 0)
            def _():
                gacc_ref[...] = gpart

            @pl.when(i > 0)
            def _():
                gacc_ref[...] += gpart

            @pl.when(i == nt - 1)
            def _():
                dg_ref[...] = jnp.sum(gacc_ref[...], axis=0, keepdims=True)

        if nr:
            rider.ride(grid, riders, "finish")

    outs = pl.pallas_call(
        body, name=name, grid=grid, in_specs=specs + (rider.in_specs if nr else []),
        out_specs=[row_spec, row_spec, _spec((1, D), lambda i, k: (0, 0))] + (rider.out_specs if nr else []),
        out_shape=[jax.ShapeDtypeStruct((t, D), F32), jax.ShapeDtypeStruct((t, D), BF16),
                   jax.ShapeDtypeStruct((1, D), F32)] + (rider.out_shape if nr else []),
        scratch_shapes=[pltpu.VMEM((tm, D), F32), pltpu.VMEM((8, D), F32)] + (rider.scratch if nr else []),
        compiler_params=pltpu.CompilerParams(dimension_semantics=("arbitrary", "arbitrary"),
                                             vmem_limit_bytes=VMEM_LIMIT_BIG),
    )(*operands, *(rider.groups if nr else []))
    return outs[0], outs[1], outs[2], (rider.place(outs[3:]) if nr else [])


def _final_loss(h, gain_slab, target):
    t = h.shape[0]
    tm = min(ROW_TILE, t)
    nt = t // tm

    def body(h_ref, g_ref, t_ref, dh_ref, dhb_ref, dg_ref, loss_ref, acc_ref, lacc_ref):
        i = pl.program_id(0)
        x = h_ref[...]
        g = g_ref[...]
        r = lax.rsqrt(jnp.mean(x * x, axis=-1, keepdims=True) + RMS_EPS)
        xh = x * r
        err = xh * g - t_ref[...]
        dy = err * (1.0 / D)
        gdy = dy * g
        dh = (gdy - xh * jnp.mean(gdy * xh, axis=-1, keepdims=True)) * r
        dh_ref[...] = dh
        dhb_ref[...] = dh.astype(BF16)
        part = jnp.sum((dy * xh).reshape(tm // 8, 8, D), axis=0)
        lpart = jnp.sum((err * err).reshape(tm // 8, 8, D), axis=0)

        @pl.when(i == 0)
        def _():
            acc_ref[...] = part
            lacc_ref[...] = lpart

        @pl.when(i > 0)
        def _():
            acc_ref[...] += part
            lacc_ref[...] += lpart

        @pl.when(i == nt - 1)
        def _():
            dg_ref[...] = jnp.sum(acc_ref[...], axis=0, keepdims=True)
            rows = jnp.sum(lacc_ref[...], axis=0, keepdims=True)
            loss_ref[...] = jnp.sum(rows, axis=1, keepdims=True) * (0.5 / D)

    row_spec = _spec((tm, D), lambda i: (i, 0))
    return pl.pallas_call(
        body, name="final_loss", grid=(nt,),
        in_specs=[row_spec, _spec((None, 1, D), lambda i: (R_FIN, 0, 0)), row_spec],
        out_specs=[row_spec, row_spec, _spec((1, D), lambda i: (0, 0)), _spec((1, 1), lambda i: (0, 0))],
        out_shape=[jax.ShapeDtypeStruct((t, D), F32), jax.ShapeDtypeStruct((t, D), BF16),
                   jax.ShapeDtypeStruct((1, D), F32), jax.ShapeDtypeStruct((1, 1), F32)],
        scratch_shapes=[pltpu.VMEM((8, D), F32), pltpu.VMEM((8, D), F32)],
        compiler_params=_params(("arbitrary",)),
    )(h, gain_slab, target)


def _ffn_up(name, hn, wgu, carry=None):
    t = hn.shape[0]
    tm = min(MM_TILE, t)
    grid = (NSH, t // tm)
    nc = carry.n if carry is not None else 0

    def body(*refs):
        x_ref, wg_ref, wu_ref = refs[:3]
        s_up_ref, s_gate_ref, a_ref = refs[3 + nc:6 + nc]
        riders = (refs[3:3 + nc], refs[6 + nc:6 + 2 * nc], refs[-2], refs[-1]) if nc else None
        if nc:
            carry.ride(grid, riders, "start")
        x = x_ref[...]
        g = _dg(x, wg_ref[...], NT)
        u = _dg(x, wu_ref[...], NT)
        sg = _sigmoid(g)
        silu = g * sg
        s_up_ref[...] = (MACARON * silu).astype(BF16)
        s_gate_ref[...] = (MACARON * u * (sg * (1.0 + g * (1.0 - sg)))).astype(BF16)
        a_ref[...] = (silu * u).astype(BF16)
        if nc:
            carry.ride(grid, riders, "finish")

    act = _spec((None, tm, FS), lambda s, i: (s, i, 0))
    shape = jax.ShapeDtypeStruct((NSH, t, FS), BF16)
    outs = pl.pallas_call(
        body, name=name, grid=grid,
        in_specs=[_spec((tm, D), lambda s, i: (i, 0)),
                  _spec((None, None, FS, D), lambda s, i: (s, 0, 0, 0)),
                  _spec((None, None, FS, D), lambda s, i: (s, 1, 0, 0))] + (carry.in_specs if nc else []),
        out_specs=[act, act, act] + (carry.out_specs if nc else []),
        out_shape=[shape, shape, shape] + (carry.out_shape if nc else []),
        scratch_shapes=carry.scratch if nc else [],
        compiler_params=_params(("arbitrary", "arbitrary") if nc else ("parallel", "parallel")),
    )(hn, wgu, wgu, *(carry.groups if nc else []))
    return (outs[0], outs[1], outs[2], carry.place(outs[3:]) if nc else [])


def _ffn_down(name, a, wd, h, carry=None, norm=None):
    t = h.shape[0]
    tm = min(MM_TILE // 2, t)
    row_spec = _spec((tm, D), lambda i: (i, 0))
    return _mm(name, [(a, _spec((None, tm, FS), lambda i, s=s: (s, i, 0)), wd, _spec((None, FS, D), lambda i, s=s: (s, 0, 0)))
                      for s in range(NSH)],
               grid=(t // tm,), o_shape=(t, D), o_dtype=F32, o_spec=row_spec, dims=NN, kaxis=0, nk=1,
               res=(h, row_spec), scale=MACARON, carry=carry, norm=norm)


def _ffn_bwd_up(name, dhb, wd, s_up, s_gate, rider=None):
    t = dhb.shape[0]
    tm = min(2 * MM_TILE, t)
    grid = (t // tm, NSH)
    nr = rider.n if rider is not None else 0

    def body(*refs):
        dh_ref, wd_ref, s_up_ref, s_gate_ref = refs[:4]
        dg_ref, du_ref = refs[4 + nr:6 + nr]
        riders = (refs[4:4 + nr], refs[6 + nr:6 + 2 * nr], refs[-2], refs[-1]) if nr else None
        if nr:
            rider.ride(grid, riders, "start")
        da = _dg(dh_ref[...], wd_ref[...], NT).astype(BF16)
        du_ref[...] = da * s_up_ref[...]
        dg_ref[...] = da * s_gate_ref[...]
        if nr:
            rider.ride(grid, riders, "finish")

    act = _spec((None, tm, FS), lambda i, s: (s, i, 0))
    shape = jax.ShapeDtypeStruct((NSH, t, FS), BF16)
    outs = pl.pallas_call(
        body, name=name, grid=grid,
        in_specs=[_spec((tm, D), lambda i, s: (i, 0)), _spec((None, FS, D), lambda i, s: (s, 0, 0)), act, act]
        + (rider.in_specs if nr else []),
        out_specs=[act, act] + (rider.out_specs if nr else []),
        out_shape=[shape, shape] + (rider.out_shape if nr else []),
        scratch_shapes=rider.scratch if nr else [],
        compiler_params=pltpu.CompilerParams(
            dimension_semantics=("arbitrary", "arbitrary") if nr else ("parallel", "parallel"),
            vmem_limit_bytes=VMEM_LIMIT_BIG),
    )(dhb, wd, s_up, s_gate, *(rider.groups if nr else []))
    return outs[0], outs[1], (rider.place(outs[2:]) if nr else [])


def _wgrad(name, a, a_spec, b, b_spec, out_rows, out_cols, t, tt, group, slot, scale=None, carry=None):
    first = isinstance(group, int)
    rows = group if first else group.shape[1]
    return _mm(name, [(a, a_spec, b, b_spec)], grid=(NSH, t // tt),
               o_shape=(NSH, rows, out_cols), o_dtype=BF16,
               o_spec=_spec((None, out_rows, out_cols), lambda s, k: (s, slot, 0)),
               dims=TN, kaxis=1, nk=t // tt, acc_shape=(out_rows, out_cols), scale=scale,
               into=None if first else group, carry=carry)


def _wgrad_gate_up(name, dg, du, hn, group, gate_idx):
    t = hn.shape[0]
    first = isinstance(group, int)
    rows = group if first else group.shape[1]

    def body(dg_ref, du_ref, hn_ref, *refs):
        o_ref = refs[-1]
        x = hn_ref[...]
        o_ref[...] = jnp.concatenate([_dg(dg_ref[...], x, TN), _dg(du_ref[...], x, TN)], axis=0).astype(BF16)

    hid = _spec((None, t, FS), lambda s: (s, 0, 0))
    return pl.pallas_call(
        body, name=name, grid=(NSH,),
        in_specs=[hid, hid, _spec((t, D), lambda s: (0, 0))] + ([] if first else [ANY]),
        out_specs=_spec((None, 2 * FS, D), lambda s: (s, gate_idx // 2, 0)),
        out_shape=jax.ShapeDtypeStruct((NSH, rows, D), BF16),
        input_output_aliases={} if first else {3: 0},
        compiler_params=pltpu.CompilerParams(dimension_semantics=("parallel",), vmem_limit_bytes=VMEM_LIMIT_BIG),
    )(dg, du, hn, *([] if first else [group]))


def _ffn_backward(tag, dh, dhb, h_in, hn, s_up, s_gate, a, wgu, wd, gate_idx, up_idx, down_idx, small, norm_row,
                  grad_a, grad_b, rider_up=None, rider_dhn=None):
    t = dh.shape[0]
    tm = min(MM_TILE, t)
    tt = min(WGRAD_TILE, t)
    tok = _spec((tt, D), lambda s, k: (k, 0))
    hid = _spec((None, tt, FS), lambda s, k: (s, k, 0))
    grad_b = _wgrad(tag + "_dwd", a, hid, dhb, tok, FS, D, t, tt, grad_b, down_idx, scale=MACARON)
    rider = rider_up(grad_a, grad_b) if rider_up is not None else None
    dg, du, landed_up = _ffn_bwd_up(tag + "_bwd_up", dhb, wd, s_up, s_gate, rider)
    assert up_idx == gate_idx + 1 and gate_idx % 2 == 0
    grad_a = _wgrad_gate_up(tag + "_dwgu", dg, du, hn, grad_a, gate_idx)
    rider = rider_dhn(grad_a, grad_b, landed_up) if rider_dhn is not None else None
    act = _spec((None, tm, FS), lambda i, k: (k, i, 0))
    dh_in, dhb_in, d_gain, landed_dhn = _dhn_norm(
        tag + "_dhn", [(dg, act, wgu, _spec((None, None, FS, D), lambda i, k: (k, 0, 0, 0))),
                       (du, act, wgu, _spec((None, None, FS, D), lambda i, k: (k, 1, 0, 0)))],
        NN, h_in, small, norm_row, dh, rider)
    return dh_in, dhb_in, grad_a, grad_b, d_gain, landed_up, landed_dhn


def _conv_fwd(proj_a, conv_w):
    t = proj_a.shape[0]
    tm = min(ROW_TILE, t)
    hb = tm // CONV_HALO

    def body(ab_ref, ac_ref, ax_ref, acp_ref, axp_ref, w_ref, y_ref):
        i = pl.program_id(1)
        u = ac_ref[...] * ax_ref[...]
        up = jnp.where(i > 0, acp_ref[...] * axp_ref[...], 0.0)
        ext = jnp.concatenate([up, u], axis=0)
        u1 = pltpu.roll(ext, 1, 0)[CONV_HALO:]
        u2 = pltpu.roll(ext, 2, 0)[CONV_HALO:]
        w = w_ref[...]
        conv = w[0:1] * u2 + w[1:2] * u1 + w[2:3] * u
        y_ref[...] = (ab_ref[...] * conv).astype(BF16)

    def cur(off):
        return _spec((tm, LANE), lambda j, i: (i, off + j))

    def prev(off):
        return _spec((CONV_HALO, LANE), lambda j, i: (jnp.maximum(i * hb - 1, 0), off + j))

    return pl.pallas_call(
        body, name="conv_fwd", grid=(4, t // tm),
        in_specs=[cur(0), cur(4), cur(8), prev(4), prev(8),
                  _spec((None, None, 8, LANE), lambda j, i: (j, 0, 0, 0))],
        out_specs=_spec((tm, LANE), lambda j, i: (i, j)),
        out_shape=jax.ShapeDtypeStruct((t, 512), BF16),
        compiler_params=_params(("parallel", "parallel")),
    )(proj_a, proj_a, proj_a, proj_a, proj_a, conv_w)


def _conv_bwd(proj_a, conv_w, dy, rider=None):
    t = proj_a.shape[0]
    tm = min(ROW_TILE, t)
    hb = tm // CONV_HALO
    nt = t // tm
    grid = (4, nt)
    nr = rider.n if rider is not None else 0

    def body(*refs):
        ab_ref, ac_ref, ax_ref, dy_ref, acp_ref, axp_ref, abn_ref, dyn_ref, w_ref = refs[:9]
        dab_ref, dac_ref, dax_ref, dw_ref = refs[9 + nr:13 + nr]
        acc_ref = refs[13 + 2 * nr]
        riders = (refs[9:9 + nr], refs[13 + nr:13 + 2 * nr], refs[-2], refs[-1]) if nr else None
        if nr:
            rider.ride(grid, riders, "start")
        i = pl.program_id(1)
        ab, ac, ax = ab_ref[...], ac_ref[...], ax_ref[...]
        u = ac * ax
        up = jnp.where(i > 0, acp_ref[...] * axp_ref[...], 0.0)
        ext = jnp.concatenate([up, u], axis=0)
        u1 = pltpu.roll(ext, 1, 0)[CONV_HALO:]
        u2 = pltpu.roll(ext, 2, 0)[CONV_HALO:]
        w = w_ref[...]
        conv = w[0:1] * u2 + w[1:2] * u1 + w[2:3] * u
        dy_v = dy_ref[...]
        dab_ref[...] = (dy_v * conv).astype(BF16)
        dc = dy_v * ab
        dcn = jnp.where(i < nt - 1, dyn_ref[...] * abn_ref[...], 0.0)
        extn = jnp.concatenate([dc, dcn], axis=0)
        n = tm + CONV_HALO
        dc1 = pltpu.roll(extn, n - 1, 0)[:tm]
        dc2 = pltpu.roll(extn, n - 2, 0)[:tm]
        du = w[2:3] * dc + w[1:2] * dc1 + w[0:1] * dc2
        dac_ref[...] = (du * ax).astype(BF16)
        dax_ref[...] = (du * ac).astype(BF16)
        rid = lax.broadcasted_iota(jnp.int32, (8, LANE), 0)
        part = jnp.where(rid == 0, jnp.sum(dc * u2, axis=0, keepdims=True),
                         jnp.where(rid == 1, jnp.sum(dc * u1, axis=0, keepdims=True),
                                   jnp.where(rid == 2, jnp.sum(dc * u, axis=0, keepdims=True), 0.0)))

        @pl.when(i == 0)
        def _():
            acc_ref[...] = part

        @pl.when(i > 0)
        def _():
            acc_ref[...] += part

        @pl.when(i == nt - 1)
        def _():
            dw_ref[...] = acc_ref[...]

        if nr:
            rider.ride(grid, riders, "finish")

    def cur(off):
        return _spec((tm, LANE), lambda j, i: (i, off + j))

    def prev(off):
        return _spec((CONV_HALO, LANE), lambda j, i: (jnp.maximum(i * hb - 1, 0), off + j))

    def nxt(off):
        return _spec((CONV_HALO, LANE), lambda j, i: (jnp.minimum((i + 1) * hb, nt * hb - 1), off + j))

    outs = pl.pallas_call(
        body, name="conv_bwd", grid=grid,
        in_specs=[cur(0), cur(4), cur(8), cur(0), prev(4), prev(8), nxt(0), nxt(0),
                  _spec((None, None, 8, LANE), lambda j, i: (j, 0, 0, 0))] + (rider.in_specs if nr else []),
        out_specs=[_spec((tm, LANE), lambda j, i: (i, j)), _spec((tm, LANE), lambda j, i: (i, j)),
                   _spec((tm, LANE), lambda j, i: (i, j)), _spec((None, 8, LANE), lambda j, i: (j, 0, 0))]
        + (rider.out_specs if nr else []),
        out_shape=[jax.ShapeDtypeStruct((t, 512), BF16), jax.ShapeDtypeStruct((t, 512), BF16),
                   jax.ShapeDtypeStruct((t, 512), BF16), jax.ShapeDtypeStruct((4, 8, LANE), F32)]
        + (rider.out_shape if nr else []),
        scratch_shapes=[pltpu.VMEM((8, LANE), F32)] + (rider.scratch if nr else []),
        compiler_params=_params(("arbitrary", "arbitrary") if nr else ("parallel", "arbitrary")),
    )(proj_a, proj_a, proj_a, dy, proj_a, proj_a, proj_a, dy, conv_w, *(rider.groups if nr else []))
    return outs[0], outs[1], outs[2], outs[3], (rider.place(outs[4:]) if nr else [])


def _log_sigmoid(z):
    return jnp.minimum(z, 0.0) - jnp.log(1.0 + jnp.exp(-jnp.abs(z)))


def _sb_masks():
    row = lax.broadcasted_iota(jnp.int32, (SBQ, SBQ), 0)
    col = lax.broadcasted_iota(jnp.int32, (SBQ, SBQ), 1)
    return row, col


def _ones_where(mask):
    return jnp.where(mask, 1.0, 0.0).astype(BF16)


def _head_mask(head):
    lane = lax.broadcasted_iota(jnp.int32, (1, LANE), 1)
    return lane >= 64 if head else lane < 64


def _sb_fwd(proj_b, carry=None):
    t = proj_b.shape[0]
    nq = t // SBQ
    scale = 1.0 / math.sqrt(64.0)

    npair = SB_PAIRS_FWD
    wide = npair * LANE
    ngrp = 4 // npair
    chains = [(p, head) for p in range(npair) for head in range(2)]

    grid = (ngrp, nq)
    nc = carry.n if carry is not None else 0

    def body(*refs):
        q_ref, k_ref, v_ref = refs[:3]
        y_ref, l_ref, n_ref = refs[3 + nc:6 + nc]
        riders = (refs[3:3 + nc], refs[6 + nc:6 + 2 * nc], refs[-2], refs[-1]) if nc else None
        if nc:
            carry.ride(grid, riders, "start")
        grp = pl.program_id(0)
        qi = pl.program_id(1)
        row, col = _sb_masks()
        m_suffix = _ones_where(row > col)
        rows = len(chains) * SBQ
        strict = (lax.broadcasted_iota(jnp.int32, (rows, SBQ), 1)
                  < (lax.broadcasted_iota(jnp.int32, (rows, SBQ), 0) & (SBQ - 1)))
        q_pair = []
        for p in range(npair):
            q_all = q_ref[:, p * LANE:(p + 1) * LANE]
            q_pair.append(jnp.concatenate([jnp.where(_head_mask(head), q_all, jnp.zeros_like(q_all)) for head in range(2)],
                                          axis=0))

        def block(kb, state, diag):
            run, acc = state
            start = pl.multiple_of(kb * SBQ, SBQ)
            z = jnp.concatenate([_dg(q_pair[p], k_ref[pl.ds(start, SBQ), p * LANE:(p + 1) * LANE], NT)
                                 for p in range(npair)], axis=0) * scale
            lb = _log_sigmoid(z)
            lk = lb - z
            if diag:
                lk = jnp.where(strict, lk, 0.0)
            hi, lo = _split(lk)
            sums = _dg(jnp.concatenate([hi, lo], axis=0), m_suffix, NN)
            w = jnp.exp(lb + (run + sums[:rows] + sums[rows:]))
            if diag:
                w = jnp.where(strict, w, 0.0)
            wb = w.astype(BF16)
            acc = acc + jnp.concatenate(
                [_dg(wb[2 * p * SBQ:2 * (p + 1) * SBQ], v_ref[pl.ds(start, SBQ), p * LANE:(p + 1) * LANE], NN)
                 for p in range(npair)], axis=0)
            run = run + jnp.sum(hi.astype(F32) + lo.astype(F32), axis=1, keepdims=True)
            return run, acc

        state = block(qi, (jnp.zeros((rows, 1), F32), jnp.zeros((rows, LANE), F32)), True)

        def live(c):
            return jnp.logical_and(c[0] < qi, jnp.max(c[1][0]) > SB_DEAD)

        def step(c):
            return c[0] + 1, block(qi - 1 - c[0], c[1], False)

        count, (run, acc) = lax.while_loop(live, step, (jnp.int32(0), state))
        n_ref[grp * nq + qi] = count.astype(F32)
        hm = _head_mask(0)
        for p in range(npair):
            lo_rows, hi_rows = slice(2 * p * SBQ, (2 * p + 1) * SBQ), slice((2 * p + 1) * SBQ, (2 * p + 2) * SBQ)
            y_ref[:, p * LANE:(p + 1) * LANE] = jnp.where(hm, acc[lo_rows], acc[hi_rows]).astype(BF16)
            l_ref[p] = jnp.where(hm, run[lo_rows], run[hi_rows])
        if nc:
            carry.ride(grid, riders, "finish")

    outs = pl.pallas_call(
        body, name="sb_fwd", grid=grid,
        in_specs=[_spec((SBQ, wide), lambda g, i: (i, g)),
                  _spec((t, wide), lambda g, i: (0, ngrp + g)),
                  _spec((t, wide), lambda g, i: (0, 2 * ngrp + g))] + (carry.in_specs if nc else []),
        out_specs=[_spec((SBQ, wide), lambda g, i: (i, g)), _spec((npair, SBQ, LANE), lambda g, i: (g, i, 0)),
                   pl.BlockSpec(memory_space=pltpu.SMEM)] + (carry.out_specs if nc else []),
        out_shape=[jax.ShapeDtypeStruct((t, 512), BF16), jax.ShapeDtypeStruct((4, t, LANE), F32),
                   jax.ShapeDtypeStruct((ngrp * nq,), F32)] + (carry.out_shape if nc else []),
        scratch_shapes=carry.scratch if nc else [],
        compiler_params=_params(("arbitrary", "arbitrary")),
    )(proj_b, proj_b, proj_b, *(carry.groups if nc else []))
    return outs[0], outs[1], outs[2], (carry.place(outs[3:]) if nc else [])


def _sb_bwd(proj_b, dy, ltot, nblk, rider=None):
    t = proj_b.shape[0]
    nq = t // SBQ
    scale = 1.0 / math.sqrt(64.0)

    npair = SB_PAIRS_BWD
    wide = npair * LANE
    ngrp = 4 // npair
    chains = [(p, head) for p in range(npair) for head in range(2)]
    grid = (ngrp, nq)
    nr = rider.n if rider is not None else 0
    per_count = SB_PAIRS_FWD // SB_PAIRS_BWD

    def body(*refs):
        q_ref, k_ref, v_ref, dy_ref, l_ref, n_ref = refs[:6]
        dq_ref, dk_ref, dv_ref = refs[6 + nr:9 + nr]
        dk_acc, dv_acc = refs[9 + 2 * nr:11 + 2 * nr]
        riders = (refs[6:6 + nr], refs[9 + nr:9 + 2 * nr], refs[-2], refs[-1]) if nr else None
        if nr:
            rider.ride(grid, riders, "start")
        grp = pl.program_id(0)
        qi = pl.program_id(1)

        @pl.when(qi == 0)
        def _():
            dk_acc[...] = jnp.zeros_like(dk_acc)
            dv_acc[...] = jnp.zeros_like(dv_acc)

        row, col = _sb_masks()
        m_prefix = _ones_where(row <= col)
        m_before = _ones_where(row < col)
        rows = len(chains) * SBQ
        strict = (lax.broadcasted_iota(jnp.int32, (rows, SBQ), 1)
                  < (lax.broadcasted_iota(jnp.int32, (rows, SBQ), 0) & (SBQ - 1)))
        q_pair, do_pair, ltot = [], [], []
        for p in range(npair):
            pl_ = slice(p * LANE, (p + 1) * LANE)
            q_all = q_ref[:, pl_]
            do_all = dy_ref[:, pl_].astype(BF16)
            q_pair.append(jnp.concatenate([jnp.where(_head_mask(h), q_all, jnp.zeros_like(q_all)) for h in range(2)], axis=0))
            do_pair.append(jnp.concatenate([jnp.where(_head_mask(h), do_all, jnp.zeros_like(do_all)) for h in range(2)], axis=0))
            ltot += [l_ref[p][:, h * 64:h * 64 + 1] for h in range(2)]
        ltot = jnp.concatenate(ltot, axis=0)

        def pair_rows(a, p):
            return a[2 * p * SBQ:2 * (p + 1) * SBQ]

        def block(kb, state, diag):
            seen, dseen, dq = state
            start = pl.multiple_of(kb * SBQ, SBQ)
            kk = [k_ref[pl.ds(start, SBQ), p * LANE:(p + 1) * LANE] for p in range(npair)]
            vv = [v_ref[pl.ds(start, SBQ), p * LANE:(p + 1) * LANE] for p in range(npair)]
            z = jnp.concatenate([_dg(q_pair[p], kk[p], NT) for p in range(npair)], axis=0) * scale
            lb = _log_sigmoid(z)
            lk = lb - z
            if diag:
                lk = jnp.where(strict, lk, 0.0)
            hi, lo = _split(lk)
            sums = _dg(jnp.concatenate([hi, lo], axis=0), m_prefix, NN)
            w = jnp.exp(lb + ((ltot - seen) - (sums[:rows] + sums[rows:])))
            if diag:
                w = jnp.where(strict, w, 0.0)
            wb = w.astype(BF16)
            da = w * jnp.concatenate([_dg(do_pair[p], vv[p], NT) for p in range(npair)], axis=0)
            dah, dal = _split(da)
            dsums = _dg(jnp.concatenate([dah, dal], axis=0), m_before, NN)
            sig = jnp.exp(lb)
            dz = (da * (1.0 - sig) - (dseen + dsums[:rows] + dsums[rows:]) * sig) * scale
            if diag:
                dz = jnp.where(strict, dz, 0.0)
            dzb = dz.astype(BF16)
            for p in range(npair):
                pl_ = slice(p * LANE, (p + 1) * LANE)
                dv_acc[pl.ds(start, SBQ), pl_] += _dg(pair_rows(wb, p), do_pair[p], TN)
                dk_acc[pl.ds(start, SBQ), pl_] += _dg(pair_rows(dzb, p), q_pair[p], TN)
            dq = dq + jnp.concatenate([_dg(pair_rows(dzb, p), kk[p], NN) for p in range(npair)], axis=0)
            seen = seen + jnp.sum(hi.astype(F32) + lo.astype(F32), axis=1, keepdims=True)
            dseen = dseen + jnp.sum(da, axis=1, keepdims=True)
            return seen, dseen, dq

        zero = (jnp.zeros((rows, 1), F32), jnp.zeros((rows, 1), F32), jnp.zeros((rows, LANE), F32))
        first = qi - n_ref[(grp // per_count) * nq + qi].astype(jnp.int32)
        state = lax.fori_loop(first, qi, lambda kb, c: block(kb, c, False), zero)
        _, _, dq = block(qi, state, True)
        for p in range(npair):
            dq_ref[:, p * LANE:(p + 1) * LANE] = jnp.where(
                _head_mask(0), dq[2 * p * SBQ:(2 * p + 1) * SBQ], dq[(2 * p + 1) * SBQ:(2 * p + 2) * SBQ]).astype(BF16)

        @pl.when(qi == nq - 1)
        def _():
            dk_ref[...] = dk_acc[...].astype(BF16)
            dv_ref[...] = dv_acc[...].astype(BF16)

        if nr:
            rider.ride(grid, riders, "finish")

    full = jax.ShapeDtypeStruct((t, 512), BF16)
    outs = pl.pallas_call(
        body, name="sb_bwd", grid=grid,
        in_specs=[_spec((SBQ, wide), lambda g, i: (i, g)),
                  _spec((t, wide), lambda g, i: (0, ngrp + g)),
                  _spec((t, wide), lambda g, i: (0, 2 * ngrp + g)),
                  _spec((SBQ, wide), lambda g, i: (i, ngrp + g)),
                  _spec((npair, SBQ, LANE), lambda g, i: (g, i, 0)),
                  pl.BlockSpec(memory_space=pltpu.SMEM)] + (rider.in_specs if nr else []),
        out_specs=[_spec((SBQ, wide), lambda g, i: (i, g)),
                   _spec((t, wide), lambda g, i: (0, g)), _spec((t, wide), lambda g, i: (0, g))]
        + (rider.out_specs if nr else []),
        out_shape=[full, full, full] + (rider.out_shape if nr else []),
        scratch_shapes=[pltpu.VMEM((t, wide), F32), pltpu.VMEM((t, wide), F32)] + (rider.scratch if nr else []),
        compiler_params=pltpu.CompilerParams(dimension_semantics=("arbitrary", "arbitrary"),
                                             vmem_limit_bytes=VMEM_LIMIT_BIG),
    )(proj_b, proj_b, proj_b, dy, ltot, nblk, *(rider.groups if nr else []))
    return outs[0], outs[1], outs[2], (rider.place(outs[3:]) if nr else [])


def _hgrn_gates(qr, fr, c0, c1):
    mx = jnp.maximum(c0, c1)
    e0, e1 = jnp.exp(c0 - mx), jnp.exp(c1 - mx)
    lb = e1 / (e0 + e1)
    sx = _sigmoid(fr)
    f = lb + (1.0 - lb) * sx
    k = (1.0 - lb) * (1.0 - sx)
    sq = _sigmoid(qr)
    return lb, sx, f, k, sq, qr * sq


def _chunk_sums(mask, x):
    n = x.shape[1]
    hi, lo = _split(x)
    both = _dg(_ones_where(mask), jnp.concatenate([hi, lo], axis=1), NN)
    return both[:, :n] + both[:, n:]


def _chunk_masks():
    row = lax.broadcasted_iota(jnp.int32, (CHUNK, CHUNK), 0)
    col = lax.broadcasted_iota(jnp.int32, (CHUNK, CHUNK), 1)
    return col <= row, col >= row


def _hgrn_fwd(proj_c, small, carry=None):
    t = proj_c.shape[1]
    nc = t // CHUNK
    nh = D // HD

    hps = HGRN_HPS
    wide = hps * HD
    grid = (nh // hps, nc)
    nr = carry.n if carry is not None else 0

    def body(*refs):
        p_ref, c0_ref, c1_ref, gam_ref = refs[:4]
        o_ref, y_ref, sst_ref = refs[4 + nr:7 + nr]
        st_ref = refs[7 + 2 * nr]
        riders = (refs[4:4 + nr], refs[7 + nr:7 + 2 * nr], refs[-2], refs[-1]) if nr else None
        if nr:
            carry.ride(grid, riders, "start")
        c = pl.program_id(1)

        @pl.when(c == 0)
        def _():
            st_ref[...] = jnp.zeros_like(st_ref)

        _, _, f_all, k_all, _, q_all = _hgrn_gates(p_ref[0], p_ref[1], c0_ref[...], c1_ref[...])
        tril, _ = _chunk_masks()
        b_all = _chunk_sums(tril, jnp.log(f_all))
        for j in range(hps):
            ln = slice(j * HD, (j + 1) * HD)
            st0 = st_ref[j]
            sst_ref[j] = st0
            v, g = p_ref[2, :, ln], p_ref[3, :, ln]
            q, k, b = q_all[:, ln], k_all[:, ln], b_all[:, ln]
            bm = b[CHUNK // 2 - 1:CHUNK // 2]
            bl = b[CHUNK - 1:CHUNK]
            qd = q * jnp.exp(b)
            qt = q * jnp.exp(b - bm)
            kt = k * jnp.exp(bm - b)
            kl = k * jnp.exp(bl - b)
            vb = v.astype(BF16)
            att = jnp.where(tril, _dot3(qt, kt, NT), 0.0)
            o = _dg(qd.astype(BF16), st0.astype(BF16), NT) + _dg(att.astype(BF16), vb, NN)
            st_ref[j] = st0 * jnp.exp(bl) + _dg(vb, kl.astype(BF16), TN)
            o_ref[:, ln] = o
            r = lax.rsqrt(jnp.mean(o * o, axis=-1, keepdims=True) + RMS_EPS)
            y_ref[:, ln] = (o * r * gam_ref[...] * (g * _sigmoid(g))).astype(BF16)
        if nr:
            carry.ride(grid, riders, "finish")

    outs = pl.pallas_call(
        body, name="hgrn_fwd", grid=grid,
        in_specs=[_spec((4, CHUNK, wide), lambda h, c: (0, c, h)),
                  _spec((None, 1, wide), lambda h, c: (R_CLB, 0, h)),
                  _spec((None, 1, wide), lambda h, c: (R_CLB + 1, 0, h)),
                  _spec((None, 1, HD), lambda h, c: (R_GAM, 0, 0))] + (carry.in_specs if nr else []),
        out_specs=[_spec((CHUNK, wide), lambda h, c: (c, h)), _spec((CHUNK, wide), lambda h, c: (c, h)),
                   _spec((None, hps, HD, HD), lambda h, c: (c, h, 0, 0))] + (carry.out_specs if nr else []),
        out_shape=[jax.ShapeDtypeStruct((t, D), F32), jax.ShapeDtypeStruct((t, D), BF16),
                   jax.ShapeDtypeStruct((nc, nh, HD, HD), F32)] + (carry.out_shape if nr else []),
        scratch_shapes=[pltpu.VMEM((hps, HD, HD), F32)] + (carry.scratch if nr else []),
        compiler_params=_params(("arbitrary", "arbitrary")),
    )(proj_c, small, small, small, *(carry.groups if nr else []))
    return outs[0], outs[1], outs[2], (carry.place(outs[3:]) if nr else [])


def _hgrn_bwd(proj_c, small, o, sst, dyc, rider=None):
    t = proj_c.shape[1]
    nc = t // CHUNK
    nh = D // HD

    hps = HGRN_HPS
    wide = hps * HD
    ng = nh // hps
    grid = (ng, nc)
    nr = rider.n if rider is not None else 0

    def body(*refs):
        p_ref, c0_ref, c1_ref, gam_ref, o_ref, sst_ref, dy_ref = refs[:7]
        dp_ref, dclb_ref, dgam_ref = refs[7 + nr:10 + nr]
        dst_ref, dlb_acc, dgam_acc = refs[10 + 2 * nr:13 + 2 * nr]
        riders = (refs[7:7 + nr], refs[10 + nr:10 + 2 * nr], refs[-2], refs[-1]) if nr else None
        if nr:
            rider.ride(grid, riders, "start")
        group = pl.program_id(0)
        step = pl.program_id(1)

        @pl.when(step == 0)
        def _():
            dst_ref[...] = jnp.zeros_like(dst_ref)
            dlb_acc[...] = jnp.zeros_like(dlb_acc)

        @pl.when((step == 0) & (group == 0))
        def _():
            dgam_acc[...] = jnp.zeros_like(dgam_acc)

        gam = gam_ref[...]
        qr_all = p_ref[0]
        lb_all, sx_all, f_all, k_all, sq_all, q_all = _hgrn_gates(qr_all, p_ref[1], c0_ref[...], c1_ref[...])
        tril, triu = _chunk_masks()
        b_all = _chunk_sums(tril, jnp.log(f_all))
        dq_parts, dk_parts, db_parts, dgam_parts = [], [], [], []
        for j in range(hps):
            ln = slice(j * HD, (j + 1) * HD)
            st0 = sst_ref[j]
            dst1 = dst_ref[j]
            v, g = p_ref[2, :, ln], p_ref[3, :, ln]
            q, k, b = q_all[:, ln], k_all[:, ln], b_all[:, ln]
            bm = b[CHUNK // 2 - 1:CHUNK // 2]
            bl = b[CHUNK - 1:CHUNK]
            eb = jnp.exp(b)
            e_qt = jnp.exp(b - bm)
            e_kt = jnp.exp(bm - b)
            e_kl = jnp.exp(bl - b)
            e_bl = jnp.exp(bl)
            qd, qt, kt, kl = q * eb, q * e_qt, k * e_kt, k * e_kl
            ov = o_ref[:, ln]
            r = lax.rsqrt(jnp.mean(ov * ov, axis=-1, keepdims=True) + RMS_EPS)
            oh = ov * r
            sg = _sigmoid(g)
            dy = dy_ref[:, ln]
            dp_ref[3, :, ln] = (dy * oh * gam * (sg * (1.0 + g * (1.0 - sg)))).astype(BF16)
            dyv = dy * (g * sg)
            gdy = dyv * gam
            do = (gdy - oh * jnp.mean(gdy * oh, axis=-1, keepdims=True)) * r
            dob, vb = do.astype(BF16), v.astype(BF16)
            st0b, dst1b = st0.astype(BF16), dst1.astype(BF16)
            st1 = st0 * e_bl + _dg(vb, kl.astype(BF16), TN)
            att = jnp.where(tril, _dot3(qt, kt, NT), 0.0)
            datt = jnp.where(tril, _dg(dob, vb, NT), 0.0)
            dv = _dg(att.astype(BF16), dob, TN) + _dg(kl.astype(BF16), dst1b, NT)
            dq = _dot3(datt, kt, NN) * e_qt + _dg(dob, st0b, NN) * eb
            dk = _dot3(datt, qt, TN) * e_kt + _dg(vb, dst1b, NN) * e_kl
            db = q * dq - k * dk
            last = lax.broadcasted_iota(jnp.int32, (CHUNK, 1), 0) == CHUNK - 1
            db = db + jnp.where(last, jnp.sum(dst1 * st1, axis=0, keepdims=True), 0.0)
            dst_ref[j] = dst1 * e_bl + _dg(dob, qd.astype(BF16), TN)
            dp_ref[2, :, ln] = dv.astype(BF16)
            dq_parts.append(dq)
            dk_parts.append(dk)
            db_parts.append(db)
            dgam_parts.append(jnp.sum(dyv * oh, axis=0, keepdims=True))

        dq_all, dk_all = jnp.concatenate(dq_parts, axis=1), jnp.concatenate(dk_parts, axis=1)
        dlf = _chunk_sums(triu, jnp.concatenate(db_parts, axis=1))
        dp_ref[0] = (dq_all * (sq_all * (1.0 + qr_all * (1.0 - sq_all)))).astype(BF16)
        tmp = dlf / f_all - dk_all
        dp_ref[1] = (tmp * (1.0 - lb_all) * sx_all * (1.0 - sx_all)).astype(BF16)
        dlb_acc[...] += jnp.sum((1.0 - sx_all) * tmp, axis=0, keepdims=True)
        dgam_acc[...] += functools.reduce(lambda a, b: a + b, dgam_parts)

        @pl.when(step == nc - 1)
        def _():
            d1 = dlb_acc[...] * lb_all * (1.0 - lb_all)
            dclb_ref[...] = jnp.where(lax.broadcasted_iota(jnp.int32, (2, wide), 0) == 0, -d1, d1)

        @pl.when((step == nc - 1) & (group == ng - 1))
        def _():
            dgam_ref[...] = dgam_acc[...]

        if nr:
            rider.ride(grid, riders, "finish")

    rev = lambda h, s: (nc - 1 - s, h)
    outs = pl.pallas_call(
        body, name="hgrn_bwd", grid=grid,
        in_specs=[_spec((4, CHUNK, wide), lambda h, s: (0, nc - 1 - s, h)),
                  _spec((None, 1, wide), lambda h, s: (R_CLB, 0, h)),
                  _spec((None, 1, wide), lambda h, s: (R_CLB + 1, 0, h)),
                  _spec((None, 1, HD), lambda h, s: (R_GAM, 0, 0)),
                  _spec((CHUNK, wide), rev),
                  _spec((None, hps, HD, HD), lambda h, s: (nc - 1 - s, h, 0, 0)),
                  _spec((CHUNK, wide), rev)] + (rider.in_specs if nr else []),
        out_specs=[_spec((4, CHUNK, wide), lambda h, s: (0, nc - 1 - s, h)),
                   _spec((2, wide), lambda h, s: (0, h)),
                   _spec((1, HD), lambda h, s: (0, 0))] + (rider.out_specs if nr else []),
        out_shape=[jax.ShapeDtypeStruct((4, t, D), BF16), jax.ShapeDtypeStruct((2, D), F32),
                   jax.ShapeDtypeStruct((1, HD), F32)] + (rider.out_shape if nr else []),
        scratch_shapes=[pltpu.VMEM((hps, HD, HD), F32), pltpu.VMEM((1, wide), F32), pltpu.VMEM((1, HD), F32)]
        + (rider.scratch if nr else []),
        compiler_params=_params(("arbitrary", "arbitrary")),
    )(proj_c, small, small, small, o, sst, dyc, *(rider.groups if nr else []))
    return outs[0], outs[1], outs[2], (rider.place(outs[3:]) if nr else [])


def _adamw(name, w, grads, m, v):
    nl = len(grads)
    rows, cols = grads[0].shape
    br = rows
    for cand in (512, 352, 256):
        if rows % cand == 0:
            br = cand
            break
    nb = rows // br
    c1 = 1.0 - ADAM_B1 ** ADAM_STEP
    c2 = 1.0 - ADAM_B2 ** ADAM_STEP

    def body(w_ref, m_ref, v_ref, *refs):
        g_refs, (d_ref, mo_ref, vo_ref, go_ref) = refs[:nl], refs[nl:]
        layer = pl.program_id(0)
        gv = g_refs[0][...]
        for k in range(1, nl):
            gv = jnp.where(layer == k, g_refs[k][...], gv)
        mn = ADAM_B1 * m_ref[...] + (1.0 - ADAM_B1) * gv
        vn = ADAM_B2 * v_ref[...] + (1.0 - ADAM_B2) * (gv * gv)
        mo_ref[...] = mn
        vo_ref[...] = vn
        go_ref[...] = gv
        d_ref[...] = -ADAM_LR * ((mn / c1) / (jnp.sqrt(vn / c2) + ADAM_EPS) + ADAM_WD * w_ref[...])

    blk = _spec((br, cols), lambda l, i: (l * nb + i, 0))
    g_specs = [_spec((br, cols), lambda l, i, k=k: (jnp.where(l == k, i, 0), 0)) for k in range(nl)]
    shape = jax.ShapeDtypeStruct((nl * rows, cols), F32)
    return pl.pallas_call(
        body, name=name, grid=(nl, nb), in_specs=[blk] * 3 + g_specs, out_specs=[blk] * 4, out_shape=[shape] * 4,
        compiler_params=_params(("arbitrary", "arbitrary")),
    )(w, m, v, *grads)


def _place():
    x, y, c = lax.axis_index("x"), lax.axis_index("y"), lax.axis_index("c")
    chips = [(1 - x, y), (x, 1 - y), (1 - x, 1 - y)]
    return x, y, c, chips


class _Rider:
    n = 0
    relay = None

    def ride(self, grid, refs, when):
        if not self.n:
            return
        ids = [pl.program_id(a) for a in range(len(grid))]
        first = functools.reduce(jnp.logical_and, [i == 0 for i in ids])
        last = functools.reduce(jnp.logical_and, [i == g - 1 for i, g in zip(ids, grid)])
        phases = [(first, self.start), (last, self.relay)] if when == "start" else [(last, self.finish)]
        for cond, phase in phases:
            if phase is not None:
                pl.when(cond)(functools.partial(phase, *refs))


class _Gather(_Rider):
    PER_GROUP = 7

    def __init__(self, groups):
        self.groups = list(groups)
        self.n = len(self.groups)
        self.in_specs = [ANY] * self.n
        self.out_specs = [ANY] * self.n
        self.out_shape = [jax.ShapeDtypeStruct((NSH,) + g.shape, g.dtype) for g in self.groups]
        sems = pltpu.SemaphoreType.DMA((self.PER_GROUP * self.n,))
        self.scratch = [sems, sems] if self.n else []

    def _copies(self, ins, outs, send, recv):
        x, y, c, chips = _place()
        sibling = (x, y, 1 - c)

        def half(gi, chip, hc):
            return outs[gi].at[2 * chip[0] + chip[1], hc]

        def copy(gi, k, src, dst, to):
            sem = self.PER_GROUP * gi + k
            return pltpu.make_async_remote_copy(src_ref=src, dst_ref=dst, send_sem=send.at[sem], recv_sem=recv.at[sem],
                                                device_id=to, device_id_type=MESH)

        pairs = [(gi, j, chip) for gi in range(self.n) for j, chip in enumerate(chips)]
        first = [copy(gi, j, ins[gi].at[c], half(gi, (x, y), c), (*chip, c)) for gi, j, chip in pairs]
        first += [copy(gi, 6, ins[gi], outs[gi].at[2 * x + y], sibling) for gi in range(self.n)]
        landed = [copy(gi, j, half(gi, chip, c), half(gi, chip, c), sibling) for gi, j, chip in pairs]
        relay = [copy(gi, 3 + j, half(gi, chip, c), half(gi, chip, c), sibling) for gi, j, chip in pairs]
        relayed = [copy(gi, 3 + j, half(gi, chip, 1 - c), half(gi, chip, 1 - c), sibling) for gi, j, chip in pairs]
        relayed += [copy(gi, 6, ins[gi], outs[gi].at[2 * x + y], sibling) for gi in range(self.n)]
        return first, landed, relay, relayed

    def start(self, ins, outs, send, recv):
        for cp in self._copies(ins, outs, send, recv)[0]:
            cp.start()

    def relay(self, ins, outs, send, recv):
        _, landed, relay, _ = self._copies(ins, outs, send, recv)
        for arrived, onward in zip(landed, relay):
            arrived.wait_recv()
            onward.start()

    def finish(self, ins, outs, send, recv):
        first, _, relay, relayed = self._copies(ins, outs, send, recv)
        for cp in relayed:
            cp.wait_recv()
        for cp in first + relay:
            cp.wait_send()

    def place(self, outs):
        return list(outs)


def _alone(name, rider):
    n = rider.n

    def body(*refs):
        parts = (refs[:n], refs[n:2 * n], refs[2 * n], refs[2 * n + 1])
        rider.start(*parts)
        if rider.relay is not None:
            rider.relay(*parts)
        rider.finish(*parts)

    outs = pl.pallas_call(
        body, name=name, in_specs=rider.in_specs, out_specs=rider.out_specs, out_shape=rider.out_shape,
        scratch_shapes=rider.scratch, compiler_params=pltpu.CompilerParams(has_side_effects=True),
    )(*rider.groups)
    return rider.place(outs)


class _Swap(_Rider):
    def __init__(self, slots):
        self.slots = list(slots)
        self.groups = [buf for buf, _, _ in self.slots]
        self.n = len(self.slots)
        self.in_specs = [ANY] * self.n
        self.out_specs = [ANY] * self.n
        self.out_shape = [jax.ShapeDtypeStruct((NSH, rows // 2, buf.shape[2]), buf.dtype) for buf, _, rows in self.slots]
        self.scratch = [pltpu.SemaphoreType.DMA((self.n,)), pltpu.SemaphoreType.DMA((self.n,))]

    def _copies(self, ins, outs, send, recv):
        x, y, c, _ = _place()
        cps = []
        for i, (_, row0, rows) in enumerate(self.slots):
            half = rows // 2
            src = ins[i].at[:, pl.ds(pl.multiple_of(row0 + (1 - c) * half, 16), half)]
            cps.append(pltpu.make_async_remote_copy(src_ref=src, dst_ref=outs[i], send_sem=send.at[i],
                                                    recv_sem=recv.at[i], device_id=(x, y, 1 - c), device_id_type=MESH))
        return cps

    def start(self, ins, outs, send, recv):
        for cp in self._copies(ins, outs, send, recv):
            cp.start()

    def finish(self, ins, outs, send, recv):
        for cp in self._copies(ins, outs, send, recv):
            cp.wait()

    def place(self, outs):
        return list(outs)


class _Share(_Rider):
    def __init__(self, arrays):
        self.groups = list(arrays)
        self.n = len(self.groups)
        self.in_specs = [ANY] * self.n
        self.out_specs = [ANY] * self.n
        self.out_shape = [jax.ShapeDtypeStruct((2,) + g.shape, g.dtype) for g in self.groups]
        self.scratch = [pltpu.SemaphoreType.DMA((self.n,)), pltpu.SemaphoreType.DMA((self.n,))]

    def _copies(self, ins, outs, send, recv, half):
        x, y, c, _ = _place()
        return [pltpu.make_async_remote_copy(src_ref=ins[gi], dst_ref=outs[gi].at[c if half == "mine" else 1 - c],
                                             send_sem=send.at[gi], recv_sem=recv.at[gi], device_id=(x, y, 1 - c),
                                             device_id_type=MESH) for gi in range(self.n)]

    def start(self, ins, outs, send, recv):
        for cp in self._copies(ins, outs, send, recv, "mine"):
            cp.start()

    def finish(self, ins, outs, send, recv):
        for cp in self._copies(ins, outs, send, recv, "theirs"):
            cp.wait_recv()
        for cp in self._copies(ins, outs, send, recv, "mine"):
            cp.wait_send()

    def place(self, outs):
        c = lax.axis_index("c")
        return [lax.dynamic_update_index_in_dim(o, g, c, 0) for o, g in zip(outs, self.groups)]


class _Send(_Rider):
    def __init__(self, arrays):
        self.groups = list(arrays)
        self.n = len(self.groups)
        self.in_specs = [ANY] * self.n
        self.out_specs = [ANY] * self.n
        self.out_shape = [jax.ShapeDtypeStruct((3,) + g.shape[1:], g.dtype) for g in self.groups]
        self.scratch = [pltpu.SemaphoreType.DMA((3 * self.n,)), pltpu.SemaphoreType.DMA((3 * self.n,))]

    def _copies(self, ins, outs, send, recv):
        x, y, c, chips = _place()
        return [pltpu.make_async_remote_copy(src_ref=ins[gi].at[2 * chip[0] + chip[1]], dst_ref=outs[gi].at[j],
                                             send_sem=send.at[3 * gi + j], recv_sem=recv.at[3 * gi + j],
                                             device_id=(*chip, c), device_id_type=MESH)
                for gi in range(self.n) for j, chip in enumerate(chips)]

    def start(self, ins, outs, send, recv):
        for cp in self._copies(ins, outs, send, recv):
            cp.start()

    def finish(self, ins, outs, send, recv):
        for cp in self._copies(ins, outs, send, recv):
            cp.wait()

    def place(self, outs):
        return list(outs)


def _pair_sum(name, slots, got, c_idx):
    n = len(slots)
    in_specs, out_specs, out_shape, operands = [], [], [], []
    for (buf, row0, rows), g in zip(slots, got):
        hb, cols = rows // 4, buf.shape[2]
        in_specs += [pl.BlockSpec((None, hb, cols), lambda q, i, cr, r=row0 // hb: (q, r + 2 * cr[0] + i, 0)),
                     pl.BlockSpec((None, hb, cols), lambda q, i, cr: (q, i, 0))]
        out_specs.append(pl.BlockSpec((None, hb, cols), lambda q, i, cr: (q, i, 0)))
        out_shape.append(jax.ShapeDtypeStruct(g.shape, BF16))
        operands += [buf, g]

    def body(c_ref, *refs):
        for i in range(n):
            refs[2 * n + i][...] = (refs[2 * i][...].astype(F32) + refs[2 * i + 1][...].astype(F32)).astype(BF16)

    return pl.pallas_call(
        body, name=name,
        grid_spec=pltpu.PrefetchScalarGridSpec(num_scalar_prefetch=1, grid=(NSH, 2), in_specs=in_specs, out_specs=out_specs),
        out_shape=out_shape, compiler_params=_params(("parallel", "parallel")),
    )(c_idx, *operands)


def _owner_sum(name, pairs, got, p_idx):
    n = len(pairs)
    in_specs, out_specs, out_shape, operands = [], [], [], []
    for own, g in zip(pairs, got):
        _, rows, cols = own.shape
        hb = rows // 2
        in_specs += [pl.BlockSpec((None, hb, cols), lambda i, pr: (pr[0], i, 0)),
                     pl.BlockSpec((3, hb, cols), lambda i, pr: (0, i, 0))]
        out_specs.append(pl.BlockSpec((hb, cols), lambda i, pr: (i, 0)))
        out_shape.append(jax.ShapeDtypeStruct((rows, cols), F32))
        operands += [own, g]

    def body(p_ref, *refs):
        for i in range(n):
            a_ref, b_ref = refs[2 * i], refs[2 * i + 1]
            refs[2 * n + i][...] = ((a_ref[...].astype(F32) + b_ref[0].astype(F32)) + b_ref[1].astype(F32)) + b_ref[2].astype(F32)

    return pl.pallas_call(
        body, name=name,
        grid_spec=pltpu.PrefetchScalarGridSpec(num_scalar_prefetch=1, grid=(2,), in_specs=in_specs, out_specs=out_specs),
        out_shape=out_shape, compiler_params=_params(("parallel",)),
    )(p_idx, *operands)


def _sum_small(slab):
    def body(in_ref, out_ref, all_ref, send, recv):
        x, y, c, _ = _place()
        me = 4 * x + 2 * y + c
        all_ref[me] = in_ref[...]
        cps = []
        for k in range(1, 8):
            peer = (x ^ (k >> 2), y ^ ((k >> 1) & 1), c ^ (k & 1))
            cps.append(pltpu.make_async_remote_copy(src_ref=in_ref, dst_ref=all_ref.at[me], send_sem=send.at[k - 1],
                                                    recv_sem=recv.at[k - 1], device_id=peer, device_id_type=MESH))
        for cp in cps:
            cp.start()
        for cp in cps:
            cp.wait()
        total = all_ref[0]
        for d in range(1, 8):
            total = total + all_ref[d]
        out_ref[...] = total

    return pl.pallas_call(
        body, name="sum_small",
        in_specs=[pl.BlockSpec(memory_space=pltpu.VMEM)], out_specs=pl.BlockSpec(memory_space=pltpu.VMEM),
        out_shape=jax.ShapeDtypeStruct(slab.shape, F32),
        scratch_shapes=[pltpu.VMEM((8,) + slab.shape, F32), pltpu.SemaphoreType.DMA((7,)), pltpu.SemaphoreType.DMA((7,))],
        compiler_params=pltpu.CompilerParams(has_side_effects=True),
    )(slab)


FFNS = ("pre0", "post0", "pre1", "post1")
W_SHAPES = dict({f + "_gu": (NSH, 2, FS, D) for f in FFNS}, **{f + "_d": (NSH, FS, D) for f in FFNS},
                ab_in=(NSH, D, 768), ab_out=(NSH, 256, D), conv=(NSH, 2, 8, LANE), c_in=(NSH, D, D), c_out=(NSH, 256, D))
CARRIED = dict(pre0_norm=("pre0_gu",), pre0_up=("pre0_d",), pre0_down=("ab_in", "conv"), ab_proj_a=("ab_out",),
               c_proj=("post1_d",),
               sb_fwd=("post0_gu", "post0_d"), post0_up=("pre1_gu",), post0_down=("pre1_d",),
               pre1_up=("c_in", "c_out"), hgrn_fwd=("post1_gu",))


FFN_SLOTS = dict(pre0=(0, 1, 0), pre1=(2, 3, 1), post0=(4, 5, 2), post1=(6, 7, 3))
GRAD_SLOTS = dict(ab_out=("b", B_ABOUT, 256), c_in=("b", B_CIN, D), c_out=("b", B_COUT, 256), ab_in=("c", 0, D))
for _f, (_g, _u, _d) in FFN_SLOTS.items():
    GRAD_SLOTS.update({_f + "_g": ("a", _g * FS, FS), _f + "_u": ("a", _u * FS, FS), _f + "_d": ("b", _d * FS, FS)})
REDUCE_STAGES = dict(x=("post1_g", "post1_u", "post1_d", "c_out"), w=("c_in", "pre1_d"),
                     y=("pre1_g", "pre1_u", "post0_g", "post0_u", "post0_d", "ab_out"),
                     z1=("ab_in",), z2=("pre0_g", "pre0_u", "pre0_d"))


def _local_step(x, target, weights, small, shards=None, place=None):
    t = x.shape[0]
    tm = min(MM_TILE, t)
    tw = min(WGRAD_TILE, t)
    nt = t // tm
    tok_si = _spec((tm, D), lambda s, i: (i, 0))
    tok_is = _spec((tm, D), lambda i, s: (i, 0))
    w = dict(weights)
    reduced = {}

    def stage_slots(stage, buffers):
        return [(buffers[GRAD_SLOTS[n][0]],) + GRAD_SLOTS[n][1:] for n in REDUCE_STAGES[stage]]

    def swap_rider(stage, buffers):
        return _Swap(stage_slots(stage, buffers)) if place is not None else None

    def reduce_start(stage, buffers, swapped=None):
        if place is None:
            return [], None
        slots = stage_slots(stage, buffers)
        if swapped is None:
            swapped = _alone("grad_swap_" + stage, _Swap(slots))
        pairs = _pair_sum("grad_pair_sum_" + stage, slots, swapped, place[0])
        return pairs, _Send(pairs)

    def reduce_end(stage, pairs, landed):
        if place is None:
            return None
        mine = _owner_sum("grad_owner_sum_" + stage, pairs, landed, place[1])
        return _Share(mine)

    def shared(stage, landed):
        if place is not None:
            reduced.update(zip(REDUCE_STAGES[stage], landed))

    def carried(kernel_name):
        names = [n for n in CARRIED[kernel_name] if n not in w]
        return names, (_Gather([shards[n] for n in names]) if names else None)

    def land(names, arrays):
        for n, a in zip(names, arrays):
            w[n] = a.reshape(W_SHAPES[n])

    def ffn_forward(tag, h, hn, next_row):
        names, gather = carried(tag + "_up") if tag + "_up" in CARRIED else ([], None)
        s_up, s_gate, a, got = _ffn_up(tag + "_up", hn, w[tag + "_gu"], gather)
        land(names, got)
        names, gather = carried(tag + "_down") if tag + "_down" in CARRIED else ([], None)
        out = _ffn_down(tag + "_down", a, w[tag + "_d"], h, gather, (small, next_row) if next_row is not None else None)
        if gather is not None:
            out, got = out
            land(names, got)
        out, hn_next = out if next_row is not None else (out, None)
        return out, hn_next, (h, hn, s_up, s_gate, a)

    def out_proj(name, y, w_out, h, next_row):
        rows = tm // 2
        row_spec = _spec((rows, D), lambda i: (i, 0))
        return _mm(name, [(y, row_spec, w_out.reshape(D, D), _spec((D, D), lambda i: (0, 0)))],
                   grid=(t // rows,), o_shape=(t, D), o_dtype=F32, o_spec=row_spec, dims=NN, kaxis=0, nk=1,
                   res=(h, row_spec), norm=(small, next_row))

    def out_proj_bwd(tag, dhb, y, w_out, blk, grad_b, make_rider=None):
        grad_b = _wgrad(tag + "_dwout", y, _spec((tw, 256), lambda s, k: (k, s)), dhb, _spec((tw, D), lambda s, k: (k, 0)),
                        256, D, t, tw, grad_b, blk)
        rider = make_rider(grad_b) if make_rider is not None else None
        dy = _mm(tag + "_dy", [(dhb, tok_is, w_out, _spec((None, 256, D), lambda i, s: (s, 0, 0)))],
                 grid=(nt, NSH), o_shape=(t, D), o_dtype=F32, o_spec=_spec((tm, 256), lambda i, s: (i, s)),
                 dims=NT, kaxis=1, nk=1, carry=rider)
        dy, landed = dy if rider is not None else (dy, None)
        return dy, grad_b, landed

    h0 = x
    names, gather = carried("pre0_norm")
    hn, got = _norm_fwd("pre0_norm", h0, small, R_PRE, gather)
    land(names, got)
    h1, hn_ab, pre0 = ffn_forward("pre0", h0, hn, R_MIX)
    def carrying(kernel_name, make):
        names, gather = carried(kernel_name)
        out = make(gather)
        if gather is not None:
            out, got = out
            land(names, got)
        return out

    proj_a = carrying("ab_proj_a", lambda gather: _mm(
        "ab_proj_a", [(hn_ab, tok_is, w["ab_in"], _spec((None, D, 768), lambda i, s: (s, 0, 0)))],
        grid=(nt, 2), o_shape=(t, 1536), o_dtype=F32, o_spec=_spec((tm, 768), lambda i, s: (i, s)),
        dims=NN, kaxis=1, nk=1, carry=gather))
    proj_b = _mm("ab_proj_b", [(hn_ab, tok_is, w["ab_in"], _spec((None, D, 768), lambda i, s: (s + 2, 0, 0)))],
                 grid=(nt, 2), o_shape=(t, 1536), o_dtype=BF16, o_spec=_spec((tm, 768), lambda i, s: (i, s)),
                 dims=NN, kaxis=1, nk=1)
    y_a = _conv_fwd(proj_a, w["conv"])
    names, gather = carried("sb_fwd")
    y_b, ltot, nblk, got = _sb_fwd(proj_b, gather)
    land(names, got)
    y_ab = jnp.concatenate([y_a, y_b], axis=1)
    h2, hn = out_proj("ab_out", y_ab, w["ab_out"], h1, R_POST)
    h3, hn, post0 = ffn_forward("post0", h2, hn, R_PRE + 1)
    h4, hn_c, pre1 = ffn_forward("pre1", h3, hn, R_MIX + 1)
    proj_c = carrying("c_proj", lambda gather: _mm(
        "c_proj", [(hn_c, tok_si, w["c_in"], _spec((None, D, D), lambda s, i: (s, 0, 0)))],
        grid=(NSH, nt), o_shape=(NSH, t, D), o_dtype=F32, o_spec=_spec((None, tm, D), lambda s, i: (s, i, 0)),
        dims=NN, kaxis=1, nk=1, carry=gather))
    names, gather = carried("hgrn_fwd")
    o_c, y_c, sst, got = _hgrn_fwd(proj_c, small, gather)
    land(names, got)
    h5, hn = out_proj("c_out", y_c, w["c_out"], h4, R_POST + 1)
    h6, _, post1 = ffn_forward("post1", h5, hn, None)
    dh, dhb, d_fin, loss = _final_loss(h6, small, target)

    dh, dhb, grad_a, grad_b, dn_post1, _, _ = _ffn_backward("post1", dh, dhb, *post1, w["post1_gu"], w["post1_d"],
                                                            *FFN_SLOTS["post1"], small, R_POST + 1, 8 * FS, B_ROWS)
    dy_c, grad_b, swapped = out_proj_bwd("c", dhb, y_c, w["c_out"], B_COUT // 256, grad_b,
                                         lambda buf_b: swap_rider("x", dict(a=grad_a, b=buf_b)))
    pairs, rider = reduce_start("x", dict(a=grad_a, b=grad_b), swapped)
    dproj_c, d_clb, d_gam, landed = _hgrn_bwd(proj_c, small, o_c, sst, dy_c, rider)
    share = reduce_end("x", pairs, landed)
    grad_b = _wgrad("c_dwin", hn_c, _spec((tw, D), lambda s, k: (k, 0)), dproj_c, _spec((None, tw, D), lambda s, k: (s, k, 0)),
                    D, D, t, tw, grad_b, B_CIN // D, carry=share)
    if share is not None:
        grad_b, landed = grad_b
        shared("x", landed)
    dh, dhb, dn_mix1, _ = _dhn_norm("c_dhn", [(dproj_c, _spec((None, tm, D), lambda i, k: (k, i, 0)),
                                               w["c_in"], _spec((None, D, D), lambda i, k: (k, 0, 0)))],
                                    NT, h4, small, R_MIX + 1, dh)
    stage_w = {}

    def w_swap(buf_a, buf_b):
        return swap_rider("w", dict(a=buf_a, b=buf_b))

    def w_send(buf_a, buf_b, swapped):
        stage_w["pairs"], rider = reduce_start("w", dict(a=buf_a, b=buf_b), swapped)
        return rider

    on = place is not None
    dh, dhb, grad_a, grad_b, dn_pre1, _, landed = _ffn_backward(
        "pre1", dh, dhb, *pre1, w["pre1_gu"], w["pre1_d"], *FFN_SLOTS["pre1"], small, R_PRE + 1, grad_a, grad_b,
        w_swap if on else None, w_send if on else None)
    share = reduce_end("w", stage_w.get("pairs"), landed)
    dh, dhb, grad_a, grad_b, dn_post0, landed, _ = _ffn_backward(
        "post0", dh, dhb, *post0, w["post0_gu"], w["post0_d"], *FFN_SLOTS["post0"], small, R_POST, grad_a, grad_b,
        (lambda buf_a, buf_b: share) if on else None)
    shared("w", landed)
    dy_ab, grad_b, _ = out_proj_bwd("ab", dhb, y_ab, w["ab_out"], B_ABOUT // 256, grad_b)
    dab, dac, dax, d_conv, swapped = _conv_bwd(proj_a, w["conv"], dy_ab, swap_rider("y", dict(a=grad_a, b=grad_b)))
    pairs, rider = reduce_start("y", dict(a=grad_a, b=grad_b), swapped)
    dq, dk, dv, landed = _sb_bwd(proj_b, dy_ab, ltot, nblk, rider)
    share = reduce_end("y", pairs, landed)
    dproj_ab = jnp.concatenate([dab, dac, dax, dq, dk, dv], axis=1)
    grad_c = _wgrad("ab_dwin", hn_ab, _spec((tw, D), lambda s, k: (k, 0)), dproj_ab, _spec((tw, 768), lambda s, k: (k, s)),
                    D, 768, t, tw, D, 0, carry=share)
    if share is not None:
        grad_c, landed = grad_c
        shared("y", landed)
    dh, dhb, dn_mix0, _ = _dhn_norm("ab_dhn", [(dproj_ab, _spec((tm, 768), lambda i, k: (i, k)),
                                                w["ab_in"], _spec((None, D, 768), lambda i, k: (k, 0, 0)))],
                                    NT, h1, small, R_MIX, dh)
    last = {}

    def z1_send(buf_a, buf_b):
        last["z1"], rider = reduce_start("z1", dict(b=buf_b, c=grad_c))
        return rider

    def z2_send(buf_a, buf_b, _):
        last["z2"], rider = reduce_start("z2", dict(a=buf_a, b=buf_b))
        return rider

    dh, dhb, grad_a, grad_b, dn_pre0, landed, landed_own = _ffn_backward(
        "pre0", dh, dhb, *pre0, w["pre0_gu"], w["pre0_d"], *FFN_SLOTS["pre0"], small, R_PRE, grad_a, grad_b,
        z1_send if on else None, z2_send if on else None)
    if place is not None:
        mine = (_owner_sum("grad_owner_sum_z1", last["z1"], landed, place[1])
                + _owner_sum("grad_owner_sum_z2", last["z2"], landed_own, place[1]))
        reduced.update(zip(REDUCE_STAGES["z1"] + REDUCE_STAGES["z2"], _alone("grad_share_z", _Share(mine))))
    zero = jnp.zeros((1, D), F32)
    conv_rows = jnp.pad(jnp.transpose(d_conv[:, :3, :], (1, 0, 2)).reshape(3, 512), ((0, 0), (0, D - 512)))
    small_grad = jnp.concatenate([
        dn_pre0, dn_pre1, dn_mix0, dn_mix1, dn_post0, dn_post1, d_clb, d_fin,
        jnp.pad(d_gam, ((0, 0), (0, D - HD))), conv_rows,
        jnp.pad(loss, ((0, 0), (0, D - 1))), zero, zero], axis=0)
    return dh, grad_a, grad_b, grad_c, small_grad, reduced


def _small_slab(rows):
    parts = [jnp.pad(r.astype(F32), ((0, 0), (0, D - r.shape[1]))) for r in rows]
    slab = jnp.concatenate(parts, axis=0)
    return jnp.pad(slab, ((0, SMALL_ROWS - slab.shape[0]), (0, 0)))


def kernel(x, ffn_pre_norm, ffn_pre_w_gate, ffn_pre_w_up, ffn_pre_w_down, mix_norm, ffn_post_norm, ffn_post_w_gate, ffn_post_w_up, ffn_post_w_down, ab_w_in, ab_conv_w, ab_w_out, c_w_in, c_lower_bounds, c_out_norm, c_w_out, final_norm, loss_target, m_ffn_pre_norm, m_ffn_pre_w_gate, m_ffn_pre_w_up, m_ffn_pre_w_down, m_mix_norm, m_ffn_post_norm, m_ffn_post_w_gate, m_ffn_post_w_up, m_ffn_post_w_down, m_ab_w_in, m_ab_conv_w, m_ab_w_out, m_c_w_in, m_c_lower_bounds, m_c_out_norm, m_c_w_out, m_final_norm, v_ffn_pre_norm, v_ffn_pre_w_gate, v_ffn_pre_w_up, v_ffn_pre_w_down, v_mix_norm, v_ffn_post_norm, v_ffn_post_w_gate, v_ffn_post_w_up, v_ffn_post_w_down, v_ab_w_in, v_ab_conv_w, v_ab_w_out, v_c_w_in, v_c_lower_bounds, v_c_out_norm, v_c_w_out, v_final_norm):
    t = x.shape[1]
    xi, yi, ci = lax.axis_index("x"), lax.axis_index("y"), lax.axis_index("c")
    p_idx = (2 * xi + yi).astype(jnp.int32).reshape(1)
    c_idx = ci.astype(jnp.int32).reshape(1)

    def halves(m):
        return m.astype(BF16).reshape(2, m.shape[0] // 2, m.shape[1])

    transposed = ("ffn_pre_w_gate", "ffn_pre_w_up", "ffn_post_w_gate", "ffn_post_w_up")

    def flip(a):
        return jnp.swapaxes(a, 1, 2)

    shards = {}
    for name, (w_gate, w_up, w_down, layer) in dict(
            pre0=(ffn_pre_w_gate, ffn_pre_w_up, ffn_pre_w_down, 0), post0=(ffn_post_w_gate, ffn_post_w_up, ffn_post_w_down, 0),
            pre1=(ffn_pre_w_gate, ffn_pre_w_up, ffn_pre_w_down, 1), post1=(ffn_post_w_gate, ffn_post_w_up, ffn_post_w_down, 1)).items():
        shards[name + "_gu"] = jnp.stack([flip(w_gate)[layer], flip(w_up)[layer]]).astype(BF16)
        shards[name + "_d"] = halves(w_down[layer])
    conv_pad = jnp.pad(ab_conv_w[0], ((0, 5), (0, 0)))
    shards.update(ab_in=halves(ab_w_in[0]), ab_out=halves(ab_w_out[0]), c_in=halves(c_w_in[0]), c_out=halves(c_w_out[0]),
                  conv=jnp.stack([conv_pad, jnp.zeros_like(conv_pad)]))
    small = _small_slab([ffn_pre_norm, mix_norm, ffn_post_norm, c_lower_bounds, final_norm.reshape(1, D), c_out_norm])
    small = small.reshape(SMALL_ROWS, 1, D)

    grad_x, _, _, _, small_grad, reduced = _local_step(x[0], loss_target[0], {}, small, shards,
                                                        (c_idx, p_idx))
    whole = {n: g.reshape(2 * g.shape[1], g.shape[2]) for n, g in reduced.items()}
    small_sum = _sum_small(small_grad)

    my_conv = lax.dynamic_slice(small_sum[R_CONV:R_CONV + 3], (0, (2 * xi + yi) * 128), (3, 128))
    grads = {
        "ffn_pre_norm": small_sum[R_PRE:R_PRE + 2], "mix_norm": small_sum[R_MIX:R_MIX + 2],
        "ffn_post_norm": small_sum[R_POST:R_POST + 2], "c_lower_bounds": small_sum[R_CLB:R_CLB + 2],
        "c_out_norm": small_sum[R_GAM:R_GAM + 1, :HD], "final_norm": small_sum[R_FIN],
        "ab_conv_w": my_conv.reshape(1, 3, 128),
    }
    layers = dict(ab_w_in=[whole["ab_in"]], ab_w_out=[whole["ab_out"]], c_w_in=[whole["c_in"]], c_w_out=[whole["c_out"]])
    for kind, key in (("gate", "_g"), ("up", "_u"), ("down", "_d")):
        layers["ffn_pre_w_" + kind] = [whole["pre0" + key], whole["pre1" + key]]
        layers["ffn_post_w_" + kind] = [whole["post0" + key], whole["post1" + key]]
    weights = dict(ffn_pre_norm=ffn_pre_norm, ffn_pre_w_gate=ffn_pre_w_gate, ffn_pre_w_up=ffn_pre_w_up, ffn_pre_w_down=ffn_pre_w_down, mix_norm=mix_norm, ffn_post_norm=ffn_post_norm, ffn_post_w_gate=ffn_post_w_gate, ffn_post_w_up=ffn_post_w_up, ffn_post_w_down=ffn_post_w_down, ab_w_in=ab_w_in, ab_conv_w=ab_conv_w, ab_w_out=ab_w_out, c_w_in=c_w_in, c_lower_bounds=c_lower_bounds, c_out_norm=c_out_norm, c_w_out=c_w_out, final_norm=final_norm)
    m_in = dict(ffn_pre_norm=m_ffn_pre_norm, ffn_pre_w_gate=m_ffn_pre_w_gate, ffn_pre_w_up=m_ffn_pre_w_up, ffn_pre_w_down=m_ffn_pre_w_down, mix_norm=m_mix_norm, ffn_post_norm=m_ffn_post_norm, ffn_post_w_gate=m_ffn_post_w_gate, ffn_post_w_up=m_ffn_post_w_up, ffn_post_w_down=m_ffn_post_w_down, ab_w_in=m_ab_w_in, ab_conv_w=m_ab_conv_w, ab_w_out=m_ab_w_out, c_w_in=m_c_w_in, c_lower_bounds=m_c_lower_bounds, c_out_norm=m_c_out_norm, c_w_out=m_c_w_out, final_norm=m_final_norm)
    v_in = dict(ffn_pre_norm=v_ffn_pre_norm, ffn_pre_w_gate=v_ffn_pre_w_gate, ffn_pre_w_up=v_ffn_pre_w_up, ffn_pre_w_down=v_ffn_pre_w_down, mix_norm=v_mix_norm, ffn_post_norm=v_ffn_post_norm, ffn_post_w_gate=v_ffn_post_w_gate, ffn_post_w_up=v_ffn_post_w_up, ffn_post_w_down=v_ffn_post_w_down, ab_w_in=v_ab_w_in, ab_conv_w=v_ab_conv_w, ab_w_out=v_ab_w_out, c_w_in=v_c_w_in, c_lower_bounds=v_c_lower_bounds, c_out_norm=v_c_out_norm, c_w_out=v_c_w_out, final_norm=v_final_norm)
    names = list(weights)
    big = [n for n in names if weights[n].size >= 65536]
    tiny = [n for n in names if n not in big]

    delta, new_m, new_v = {}, {}, {}
    for n in big:
        turn = flip if n in transposed else (lambda a: a)
        shape = turn(weights[n]).shape
        two_d = (shape[0] * shape[1], shape[2])
        d, m2, v2, g2 = _adamw("adamw_" + n, turn(weights[n]).reshape(two_d), layers[n],
                               turn(m_in[n]).reshape(two_d), turn(v_in[n]).reshape(two_d))
        delta[n], new_m[n], new_v[n] = turn(d.reshape(shape)), turn(m2.reshape(shape)), turn(v2.reshape(shape))
        grads[n] = turn(g2.reshape(shape))

    def tiny_slab(src):
        return _small_slab([src[n].reshape(-1, src[n].shape[-1]) for n in tiny])

    offs, row = {}, 0
    for n in tiny:
        nrows = weights[n].size // weights[n].shape[-1]
        offs[n] = (row, nrows)
        row += nrows
    d, m2, v2, _ = _adamw("adamw_small", tiny_slab(weights), [tiny_slab(grads)], tiny_slab(m_in), tiny_slab(v_in))
    for n in tiny:
        r0, nr = offs[n]
        shape = weights[n].shape
        for dst, src in ((delta, d), (new_m, m2), (new_v, v2)):
            dst[n] = src[r0:r0 + nr, :shape[-1]].reshape(shape)

    loss = small_sum[R_LOSS, 0]
    return (loss, grad_x.reshape(1, t, D), *[grads[n] for n in names], *[delta[n] for n in names],
            *[new_m[n] for n in names], *[new_v[n] for n in names])
```

```python
import functools
import math

import jax
import jax.numpy as jnp
from jax import lax
from jax.experimental import pallas as pl
from jax.experimental.pallas import tpu as pltpu

F32 = jnp.float32
BF16 = jnp.bfloat16
MESH = pl.DeviceIdType.MESH
ANY = pl.BlockSpec(memory_space=pl.ANY)

D = 1024
FS = 704
NSH = 4
RMS_EPS = 1e-6
MACARON = 0.5
CHUNK = 64
HD = 128
HGRN_HPS = 8
SBQ = 128
SB_PAIRS_FWD = 4
SB_PAIRS_BWD = 4
SB_DEAD = -105.0
CONV_HALO = 8
LANE = 128
ROW_TILE = 1024
MM_TILE = 1024
HALF_TILE = MM_TILE // 2
WGRAD_TILE = 4096
VMEM_LIMIT = 48 * 1024 * 1024
VMEM_LIMIT_BIG = 58 * 1024 * 1024

ADAM_LR, ADAM_B1, ADAM_B2, ADAM_EPS, ADAM_WD, ADAM_STEP = 0.001, 0.9, 0.999, 1e-08, 0.01, 10

NN = ((1,), (0,))
NT = ((1,), (1,))
TN = ((0,), (0,))

SMALL_ROWS = 16
R_PRE, R_MIX, R_POST, R_CLB, R_FIN, R_GAM, R_CONV, R_LOSS = 0, 2, 4, 6, 8, 9, 10, 13

B_ABOUT = 4 * FS
B_CIN = B_ABOUT + 256
B_COUT = B_CIN + 1024
B_ROWS = B_COUT + 256


def _dg(a, b, dims):
    return lax.dot_general(a, b, (dims, ((), ())), preferred_element_type=F32)


def _split(x):
    hi = x.astype(BF16)
    lo = (x - hi.astype(F32)).astype(BF16)
    return hi, lo


def _dot3(a, b, dims):
    ah, al = _split(a)
    bh, bl = _split(b)
    if dims == TN:
        n = b.shape[1]
        both = _dg(ah, jnp.concatenate([bh, bl], axis=1), dims)
        return both[:, :n] + both[:, n:] + _dg(al, bh, dims)
    m = a.shape[0]
    both = _dg(jnp.concatenate([ah, al], axis=0), bh, dims)
    return both[:m] + both[m:] + _dg(ah, bl, dims)


def _sigmoid(x):
    return 1.0 / (1.0 + jnp.exp(-x))


def _params(sem):
    return pltpu.CompilerParams(dimension_semantics=sem, vmem_limit_bytes=VMEM_LIMIT)


def _spec(shape, imap):
    return pl.BlockSpec(shape, imap)


def _accumulate(acc_ref, pairs, dims):
    part = None
    for a_ref, b_ref in pairs:
        d = _dg(a_ref[...], b_ref[...], dims)
        part = d if part is None else part + d
    acc_ref[...] += part


def _mm(name, pairs, *, grid, o_shape, o_dtype, o_spec, dims, kaxis, nk, acc_shape=None, res=None, scale=None,
        into=None, carry=None, norm=None):
    npairs = len(pairs)
    operands, specs = [], []
    for a, a_spec, b, b_spec in pairs:
        operands += [a, b]
        specs += [a_spec, b_spec]
    if res is not None:
        operands.append(res[0])
        specs.append(res[1])
    aliases = {}
    if into is not None:
        aliases = {len(operands): 0}
        operands.append(into)
        specs.append(ANY)
    if norm is not None:
        operands.append(norm[0])
        specs.append(_spec((None, 1, D), lambda *_: (norm[1], 0, 0)))
    n_own = len(operands)
    n_out = 2 if norm is not None else 1
    nc = carry.n if carry is not None else 0
    if nc:
        operands += carry.groups
        specs += carry.in_specs

    def body(*refs):
        o_ref = refs[n_own + nc]
        riders = ((refs[n_own:n_own + nc], refs[n_own + nc + n_out:n_own + 2 * nc + n_out], refs[-2], refs[-1])
                  if nc else None)
        if nc:
            carry.ride(grid, riders, "start")

        def finish(val):
            if scale is not None:
                val = val * scale
            if res is not None:
                val = val + refs[2 * npairs][...]
            o_ref[...] = val.astype(o_dtype)
            if norm is not None:
                r = lax.rsqrt(jnp.mean(val * val, axis=-1, keepdims=True) + RMS_EPS)
                refs[n_own + nc + 1][...] = (val * r * refs[n_own - 1][...]).astype(BF16)

        if nk == 1:
            part = None
            for n in range(npairs):
                d = _dg(refs[2 * n][...], refs[2 * n + 1][...], dims)
                part = d if part is None else part + d
            finish(part)
        else:
            acc_ref = refs[n_own + 2 * nc + n_out]
            k = pl.program_id(kaxis)

            @pl.when(k == 0)
            def _():
                acc_ref[...] = jnp.zeros_like(acc_ref)

            _accumulate(acc_ref, [(refs[2 * n], refs[2 * n + 1]) for n in range(npairs)], dims)

            @pl.when(k == nk - 1)
            def _():
                finish(acc_ref[...])
        if nc:
            carry.ride(grid, riders, "finish")

    sem = tuple("arbitrary" if (nc or (ax == kaxis and nk > 1)) else "parallel" for ax in range(len(grid)))
    outs = pl.pallas_call(
        body, name=name, grid=grid, in_specs=specs,
        out_specs=[o_spec] * n_out + (carry.out_specs if nc else []),
        out_shape=[jax.ShapeDtypeStruct(o_shape, o_dtype)] + ([jax.ShapeDtypeStruct(o_shape, BF16)] if norm is not None else [])
        + (carry.out_shape if nc else []),
        scratch_shapes=([pltpu.VMEM(acc_shape, F32)] if nk > 1 else []) + (carry.scratch if nc else []),
        input_output_aliases=aliases,
        compiler_params=_params(sem),
    )(*operands)
    main = (outs[0], outs[1]) if norm is not None else outs[0]
    return (main, carry.place(outs[n_out:])) if nc else main


def _norm_fwd(name, h, gain_slab, row, rider=None):
    t = h.shape[0]
    tm = min(ROW_TILE, t)
    grid = (t // tm,)
    nr = rider.n if rider is not None else 0

    def body(*refs):
        h_ref, g_ref, o_ref = refs[0], refs[1], refs[2 + nr]
        riders = (refs[2:2 + nr], refs[3 + nr:3 + 2 * nr], refs[-2], refs[-1]) if nr else None
        if nr:
            rider.ride(grid, riders, "start")
        x = h_ref[...]
        r = lax.rsqrt(jnp.mean(x * x, axis=-1, keepdims=True) + RMS_EPS)
        o_ref[...] = (x * r * g_ref[...]).astype(BF16)
        if nr:
            rider.ride(grid, riders, "finish")

    outs = pl.pallas_call(
        body, name=name, grid=grid,
        in_specs=[_spec((tm, D), lambda i: (i, 0)), _spec((None, 1, D), lambda i: (row, 0, 0))]
        + (rider.in_specs if nr else []),
        out_specs=[_spec((tm, D), lambda i: (i, 0))] + (rider.out_specs if nr else []),
        out_shape=[jax.ShapeDtypeStruct((t, D), BF16)] + (rider.out_shape if nr else []),
        scratch_shapes=rider.scratch if nr else [],
        compiler_params=_params(("arbitrary",) if nr else ("parallel",)),
    )(h, gain_slab, *(rider.groups if nr else []))
    return outs[0], (rider.place(outs[1:]) if nr else [])


def _dhn_norm(name, pairs, dims, h, gain_slab, row, dres, rider=None):
    t = h.shape[0]
    tm = min(HALF_TILE, t)
    nt = t // tm
    grid = (nt,)
    npairs = len(pairs)
    nr = rider.n if rider is not None else 0
    operands, specs = [], []
    for a, a_spec, b, b_spec in pairs:
        operands += [a, b]
        specs += [a_spec, b_spec]
    row_spec = _spec((tm, D), lambda i: (i, 0))
    operands += [h, gain_slab, dres]
    specs += [row_spec, _spec((None, 1, D), lambda i: (row, 0, 0)), row_spec]
    n_own = len(operands)

    def body(*refs):
        h_ref, g_ref, dres_ref = refs[2 * npairs:n_own]
        dh_ref, dhb_ref, dg_ref = refs[n_own + nr:n_own + nr + 3]
        gacc_ref = refs[n_own + 2 * nr + 3]
        riders = (refs[n_own:n_own + nr], refs[n_own + nr + 3:n_own + 2 * nr + 3], refs[-2], refs[-1]) if nr else None
        if nr:
            rider.ride(grid, riders, "start")
        i = pl.program_id(0)
        dy = None
        for n in range(npairs):
            d = _dg(refs[2 * n][...], refs[2 * n + 1][...], dims)
            dy = d if dy is None else dy + d
        x = h_ref[...]
        r = lax.rsqrt(jnp.mean(x * x, axis=-1, keepdims=True) + RMS_EPS)
        xh = x * r
        gdy = dy * g_ref[...]
        dh = dres_ref[...] + (gdy - xh * jnp.mean(gdy * xh, axis=-1, keepdims=True)) * r
        dh_ref[...] = dh
        dhb_ref[...] = dh.astype(BF16)
        gpart = jnp.sum((dy * xh).reshape(tm // 8, 8, D), axis=0)

        @pl.when(i == 0)
        def _():
            gacc_ref[...] = gpart

        @pl.when(i > 0)
        def _():
            gacc_ref[...] += gpart

        @pl.when(i == nt - 1)
        def _():
            dg_ref[...] = jnp.sum(gacc_ref[...], axis=0, keepdims=True)

        if nr:
            rider.ride(grid, riders, "finish")

    outs = pl.pallas_call(
        body, name=name, grid=grid, in_specs=specs + (rider.in_specs if nr else []),
        out_specs=[row_spec, row_spec, _spec((1, D), lambda i: (0, 0))] + (rider.out_specs if nr else []),
        out_shape=[jax.ShapeDtypeStruct((t, D), F32), jax.ShapeDtypeStruct((t, D), BF16),
                   jax.ShapeDtypeStruct((1, D), F32)] + (rider.out_shape if nr else []),
        scratch_shapes=[pltpu.VMEM((8, D), F32)] + (rider.scratch if nr else []),
        compiler_params=pltpu.CompilerParams(dimension_semantics=("arbitrary",),
                                             vmem_limit_bytes=VMEM_LIMIT_BIG),
    )(*operands, *(rider.groups if nr else []))
    return outs[0], outs[1], outs[2], (rider.place(outs[3:]) if nr else [])


def _final_loss(h, gain_slab, target):
    t = h.shape[0]
    tm = min(ROW_TILE, t)
    nt = t // tm

    def body(h_ref, g_ref, t_ref, dh_ref, dhb_ref, dg_ref, loss_ref, acc_ref, lacc_ref):
        i = pl.program_id(0)
        x = h_ref[...]
        g = g_ref[...]
        r = lax.rsqrt(jnp.mean(x * x, axis=-1, keepdims=True) + RMS_EPS)
        xh = x * r
        err = xh * g - t_ref[...]
        dy = err * (1.0 / D)
        gdy = dy * g
        dh = (gdy - xh * jnp.mean(gdy * xh, axis=-1, keepdims=True)) * r
        dh_ref[...] = dh
        dhb_ref[...] = dh.astype(BF16)
        part = jnp.sum((dy * xh).reshape(tm // 8, 8, D), axis=0)
        lpart = jnp.sum((err * err).reshape(tm // 8, 8, D), axis=0)

        @pl.when(i == 0)
        def _():
            acc_ref[...] = part
            lacc_ref[...] = lpart

        @pl.when(i > 0)
        def _():
            acc_ref[...] += part
            lacc_ref[...] += lpart

        @pl.when(i == nt - 1)
        def _():
            dg_ref[...] = jnp.sum(acc_ref[...], axis=0, keepdims=True)
            rows = jnp.sum(lacc_ref[...], axis=0, keepdims=True)
            loss_ref[...] = jnp.sum(rows, axis=1, keepdims=True) * (0.5 / D)

    row_spec = _spec((tm, D), lambda i: (i, 0))
    return pl.pallas_call(
        body, name="final_loss", grid=(nt,),
        in_specs=[row_spec, _spec((None, 1, D), lambda i: (R_FIN, 0, 0)), row_spec],
        out_specs=[row_spec, row_spec, _spec((1, D), lambda i: (0, 0)), _spec((1, 1), lambda i: (0, 0))],
        out_shape=[jax.ShapeDtypeStruct((t, D), F32), jax.ShapeDtypeStruct((t, D), BF16),
                   jax.ShapeDtypeStruct((1, D), F32), jax.ShapeDtypeStruct((1, 1), F32)],
        scratch_shapes=[pltpu.VMEM((8, D), F32), pltpu.VMEM((8, D), F32)],
        compiler_params=_params(("arbitrary",)),
    )(h, gain_slab, target)


def _ffn_up(name, hn, wgu, carry=None):
    t = hn.shape[0]
    tm = min(MM_TILE, t)
    grid = (NSH, t // tm)
    nc = carry.n if carry is not None else 0

    def body(*refs):
        x_ref, wg_ref, wu_ref = refs[:3]
        s_up_ref, s_gate_ref, a_ref = refs[3 + nc:6 + nc]
        riders = (refs[3:3 + nc], refs[6 + nc:6 + 2 * nc], refs[-2], refs[-1]) if nc else None
        if nc:
            carry.ride(grid, riders, "start")
        x = x_ref[...]
        g = _dg(x, wg_ref[...], NT)
        u = _dg(x, wu_ref[...], NT)
        sg = _sigmoid(g)
        silu = g * sg
        s_up_ref[...] = (MACARON * silu).astype(BF16)
        s_gate_ref[...] = (MACARON * u * (sg * (1.0 + g * (1.0 - sg)))).astype(BF16)
        a_ref[...] = (silu * u).astype(BF16)
        if nc:
            carry.ride(grid, riders, "finish")

    act = _spec((None, tm, FS), lambda s, i: (s, i, 0))
    shape = jax.ShapeDtypeStruct((NSH, t, FS), BF16)
    outs = pl.pallas_call(
        body, name=name, grid=grid,
        in_specs=[_spec((tm, D), lambda s, i: (i, 0)),
                  _spec((None, None, FS, D), lambda s, i: (s, 0, 0, 0)),
                  _spec((None, None, FS, D), lambda s, i: (s, 1, 0, 0))] + (carry.in_specs if nc else []),
        out_specs=[act, act, act] + (carry.out_specs if nc else []),
        out_shape=[shape, shape, shape] + (carry.out_shape if nc else []),
        scratch_shapes=carry.scratch if nc else [],
        compiler_params=_params(("arbitrary", "arbitrary") if nc else ("parallel", "parallel")),
    )(hn, wgu, wgu, *(carry.groups if nc else []))
    return (outs[0], outs[1], outs[2], carry.place(outs[3:]) if nc else [])


def _ffn_down(name, a, wd, h, carry=None, norm=None):
    t = h.shape[0]
    tm = min(HALF_TILE, t)
    row_spec = _spec((tm, D), lambda i: (i, 0))
    return _mm(name, [(a, _spec((None, tm, FS), lambda i, s=s: (s, i, 0)), wd, _spec((None, FS, D), lambda i, s=s: (s, 0, 0)))
                      for s in range(NSH)],
               grid=(t // tm,), o_shape=(t, D), o_dtype=F32, o_spec=row_spec, dims=NN, kaxis=0, nk=1,
               res=(h, row_spec), scale=MACARON, carry=carry, norm=norm)


def _ffn_bwd_up(name, dhb, wd, s_up, s_gate, rider=None):
    t = dhb.shape[0]
    tm = min(2 * MM_TILE, t)
    grid = (t // tm, NSH)
    nr = rider.n if rider is not None else 0

    def body(*refs):
        dh_ref, wd_ref, s_up_ref, s_gate_ref = refs[:4]
        dg_ref, du_ref = refs[4 + nr:6 + nr]
        riders = (refs[4:4 + nr], refs[6 + nr:6 + 2 * nr], refs[-2], refs[-1]) if nr else None
        if nr:
            rider.ride(grid, riders, "start")
        da = _dg(dh_ref[...], wd_ref[...], NT).astype(BF16)
        du_ref[...] = da * s_up_ref[...]
        dg_ref[...] = da * s_gate_ref[...]
        if nr:
            rider.ride(grid, riders, "finish")

    act = _spec((None, tm, FS), lambda i, s: (s, i, 0))
    shape = jax.ShapeDtypeStruct((NSH, t, FS), BF16)
    outs = pl.pallas_call(
        body, name=name, grid=grid,
        in_specs=[_spec((tm, D), lambda i, s: (i, 0)), _spec((None, FS, D), lambda i, s: (s, 0, 0)), act, act]
        + (rider.in_specs if nr else []),
        out_specs=[act, act] + (rider.out_specs if nr else []),
        out_shape=[shape, shape] + (rider.out_shape if nr else []),
        scratch_shapes=rider.scratch if nr else [],
        compiler_params=pltpu.CompilerParams(
            dimension_semantics=("arbitrary", "arbitrary") if nr else ("parallel", "parallel"),
            vmem_limit_bytes=VMEM_LIMIT_BIG),
    )(dhb, wd, s_up, s_gate, *(rider.groups if nr else []))
    return outs[0], outs[1], (rider.place(outs[2:]) if nr else [])


def _wgrad(name, a, a_spec, b, b_spec, out_rows, out_cols, t, tt, group, slot, scale=None, carry=None):
    first = isinstance(group, int)
    rows = group if first else group.shape[1]
    return _mm(name, [(a, a_spec, b, b_spec)], grid=(NSH, t // tt),
               o_shape=(NSH, rows, out_cols), o_dtype=BF16,
               o_spec=_spec((None, out_rows, out_cols), lambda s, k: (s, slot, 0)),
               dims=TN, kaxis=1, nk=t // tt, acc_shape=(out_rows, out_cols), scale=scale,
               into=None if first else group, carry=carry)


def _wgrad_gate_up(name, dg, du, hn, group, gate_idx):
    t = hn.shape[0]
    first = isinstance(group, int)
    rows = group if first else group.shape[1]

    def body(dg_ref, du_ref, hn_ref, *refs):
        o_ref = refs[-1]
        x = hn_ref[...]
        o_ref[...] = jnp.concatenate([_dg(dg_ref[...], x, TN), _dg(du_ref[...], x, TN)], axis=0).astype(BF16)

    hid = _spec((None, t, FS), lambda s: (s, 0, 0))
    return pl.pallas_call(
        body, name=name, grid=(NSH,),
        in_specs=[hid, hid, _spec((t, D), lambda s: (0, 0))] + ([] if first else [ANY]),
        out_specs=_spec((None, 2 * FS, D), lambda s: (s, gate_idx // 2, 0)),
        out_shape=jax.ShapeDtypeStruct((NSH, rows, D), BF16),
        input_output_aliases={} if first else {3: 0},
        compiler_params=pltpu.CompilerParams(dimension_semantics=("parallel",), vmem_limit_bytes=VMEM_LIMIT_BIG),
    )(dg, du, hn, *([] if first else [group]))


def _ffn_backward(tag, dh, dhb, h_in, hn, s_up, s_gate, a, wgu, wd, gate_idx, up_idx, down_idx, small, norm_row,
                  grad_a, grad_b, rider_up=None, rider_dhn=None):
    t = dh.shape[0]
    tm = min(MM_TILE, t)
    tt = min(WGRAD_TILE, t)
    tok = _spec((tt, D), lambda s, k: (k, 0))
    hid = _spec((None, tt, FS), lambda s, k: (s, k, 0))
    grad_b = _wgrad(tag + "_dwd", a, hid, dhb, tok, FS, D, t, tt, grad_b, down_idx, scale=MACARON)
    rider = rider_up(grad_a, grad_b) if rider_up is not None else None
    dg, du, landed_up = _ffn_bwd_up(tag + "_bwd_up", dhb, wd, s_up, s_gate, rider)
    assert up_idx == gate_idx + 1 and gate_idx % 2 == 0
    grad_a = _wgrad_gate_up(tag + "_dwgu", dg, du, hn, grad_a, gate_idx)
    rider = rider_dhn(grad_a, grad_b, landed_up) if rider_dhn is not None else None
    th = min(HALF_TILE, t)
    pairs = []
    for s in range(NSH):
        act = _spec((None, th, FS), lambda i, s=s: (s, i, 0))
        pairs += [(dg, act, wgu, _spec((None, None, FS, D), lambda i, s=s: (s, 0, 0, 0))),
                  (du, act, wgu, _spec((None, None, FS, D), lambda i, s=s: (s, 1, 0, 0)))]
    dh_in, dhb_in, d_gain, landed_dhn = _dhn_norm(tag + "_dhn", pairs, NN, h_in, small, norm_row, dh, rider)
    return dh_in, dhb_in, grad_a, grad_b, d_gain, landed_up, landed_dhn


def _conv_fwd(proj_a, conv_w):
    t = proj_a.shape[0]
    tm = min(ROW_TILE, t)
    hb = tm // CONV_HALO

    def body(ab_ref, ac_ref, ax_ref, acp_ref, axp_ref, w_ref, y_ref):
        i = pl.program_id(1)
        u = ac_ref[...] * ax_ref[...]
        up = jnp.where(i > 0, acp_ref[...] * axp_ref[...], 0.0)
        ext = jnp.concatenate([up, u], axis=0)
        u1 = pltpu.roll(ext, 1, 0)[CONV_HALO:]
        u2 = pltpu.roll(ext, 2, 0)[CONV_HALO:]
        w = w_ref[...]
        conv = w[0:1] * u2 + w[1:2] * u1 + w[2:3] * u
        y_ref[...] = (ab_ref[...] * conv).astype(BF16)

    def cur(off):
        return _spec((tm, LANE), lambda j, i: (i, off + j))

    def prev(off):
        return _spec((CONV_HALO, LANE), lambda j, i: (jnp.maximum(i * hb - 1, 0), off + j))

    return pl.pallas_call(
        body, name="conv_fwd", grid=(4, t // tm),
        in_specs=[cur(0), cur(4), cur(8), prev(4), prev(8),
                  _spec((None, None, 8, LANE), lambda j, i: (j, 0, 0, 0))],
        out_specs=_spec((tm, LANE), lambda j, i: (i, j)),
        out_shape=jax.ShapeDtypeStruct((t, 512), BF16),
        compiler_params=_params(("parallel", "parallel")),
    )(proj_a, proj_a, proj_a, proj_a, proj_a, conv_w)


def _conv_bwd(proj_a, conv_w, dy, rider=None):
    t = proj_a.shape[0]
    tm = min(ROW_TILE, t)
    hb = tm // CONV_HALO
    nt = t // tm
    grid = (4, nt)
    nr = rider.n if rider is not None else 0

    def body(*refs):
        ab_ref, ac_ref, ax_ref, dy_ref, acp_ref, axp_ref, abn_ref, dyn_ref, w_ref = refs[:9]
        dab_ref, dac_ref, dax_ref, dw_ref = refs[9 + nr:13 + nr]
        acc_ref = refs[13 + 2 * nr]
        riders = (refs[9:9 + nr], refs[13 + nr:13 + 2 * nr], refs[-2], refs[-1]) if nr else None
        if nr:
            rider.ride(grid, riders, "start")
        i = pl.program_id(1)
        ab, ac, ax = ab_ref[...], ac_ref[...], ax_ref[...]
        u = ac * ax
        up = jnp.where(i > 0, acp_ref[...] * axp_ref[...], 0.0)
        ext = jnp.concatenate([up, u], axis=0)
        u1 = pltpu.roll(ext, 1, 0)[CONV_HALO:]
        u2 = pltpu.roll(ext, 2, 0)[CONV_HALO:]
        w = w_ref[...]
        conv = w[0:1] * u2 + w[1:2] * u1 + w[2:3] * u
        dy_v = dy_ref[...]
        dab_ref[...] = (dy_v * conv).astype(BF16)
        dc = dy_v * ab
        dcn = jnp.where(i < nt - 1, dyn_ref[...] * abn_ref[...], 0.0)
        extn = jnp.concatenate([dc, dcn], axis=0)
        n = tm + CONV_HALO
        dc1 = pltpu.roll(extn, n - 1, 0)[:tm]
        dc2 = pltpu.roll(extn, n - 2, 0)[:tm]
        du = w[2:3] * dc + w[1:2] * dc1 + w[0:1] * dc2
        dac_ref[...] = (du * ax).astype(BF16)
        dax_ref[...] = (du * ac).astype(BF16)
        rid = lax.broadcasted_iota(jnp.int32, (8, LANE), 0)
        part = jnp.where(rid == 0, jnp.sum(dc * u2, axis=0, keepdims=True),
                         jnp.where(rid == 1, jnp.sum(dc * u1, axis=0, keepdims=True),
                                   jnp.where(rid == 2, jnp.sum(dc * u, axis=0, keepdims=True), 0.0)))

        @pl.when(i == 0)
        def _():
            acc_ref[...] = part

        @pl.when(i > 0)
        def _():
            acc_ref[...] += part

        @pl.when(i == nt - 1)
        def _():
            dw_ref[...] = acc_ref[...]

        if nr:
            rider.ride(grid, riders, "finish")

    def cur(off):
        return _spec((tm, LANE), lambda j, i: (i, off + j))

    def prev(off):
        return _spec((CONV_HALO, LANE), lambda j, i: (jnp.maximum(i * hb - 1, 0), off + j))

    def nxt(off):
        return _spec((CONV_HALO, LANE), lambda j, i: (jnp.minimum((i + 1) * hb, nt * hb - 1), off + j))

    outs = pl.pallas_call(
        body, name="conv_bwd", grid=grid,
        in_specs=[cur(0), cur(4), cur(8), cur(0), prev(4), prev(8), nxt(0), nxt(0),
                  _spec((None, None, 8, LANE), lambda j, i: (j, 0, 0, 0))] + (rider.in_specs if nr else []),
        out_specs=[_spec((tm, LANE), lambda j, i: (i, j)), _spec((tm, LANE), lambda j, i: (i, j)),
                   _spec((tm, LANE), lambda j, i: (i, j)), _spec((None, 8, LANE), lambda j, i: (j, 0, 0))]
        + (rider.out_specs if nr else []),
        out_shape=[jax.ShapeDtypeStruct((t, 512), BF16), jax.ShapeDtypeStruct((t, 512), BF16),
                   jax.ShapeDtypeStruct((t, 512), BF16), jax.ShapeDtypeStruct((4, 8, LANE), F32)]
        + (rider.out_shape if nr else []),
        scratch_shapes=[pltpu.VMEM((8, LANE), F32)] + (rider.scratch if nr else []),
        compiler_params=_params(("arbitrary", "arbitrary") if nr else ("parallel", "arbitrary")),
    )(proj_a, proj_a, proj_a, dy, proj_a, proj_a, proj_a, dy, conv_w, *(rider.groups if nr else []))
    return outs[0], outs[1], outs[2], outs[3], (rider.place(outs[4:]) if nr else [])


def _log_sigmoid(z):
    return jnp.minimum(z, 0.0) - jnp.log(1.0 + jnp.exp(-jnp.abs(z)))


def _sb_masks():
    row = lax.broadcasted_iota(jnp.int32, (SBQ, SBQ), 0)
    col = lax.broadcasted_iota(jnp.int32, (SBQ, SBQ), 1)
    return row, col


def _ones_where(mask):
    return jnp.where(mask, 1.0, 0.0).astype(BF16)


def _head_mask(head):
    lane = lax.broadcasted_iota(jnp.int32, (1, LANE), 1)
    return lane >= 64 if head else lane < 64


def _sb_fwd(proj_b, carry=None):
    t = proj_b.shape[0]
    nq = t // SBQ
    scale = 1.0 / math.sqrt(64.0)

    npair = SB_PAIRS_FWD
    wide = npair * LANE
    ngrp = 4 // npair
    chains = [(p, head) for p in range(npair) for head in range(2)]

    grid = (ngrp, nq)
    nc = carry.n if carry is not None else 0

    def body(*refs):
        q_ref, k_ref, v_ref = refs[:3]
        y_ref, l_ref, n_ref = refs[3 + nc:6 + nc]
        riders = (refs[3:3 + nc], refs[6 + nc:6 + 2 * nc], refs[-2], refs[-1]) if nc else None
        if nc:
            carry.ride(grid, riders, "start")
        grp = pl.program_id(0)
        qi = pl.program_id(1)
        row, col = _sb_masks()
        m_suffix = _ones_where(row > col)
        rows = len(chains) * SBQ
        strict = (lax.broadcasted_iota(jnp.int32, (rows, SBQ), 1)
                  < (lax.broadcasted_iota(jnp.int32, (rows, SBQ), 0) & (SBQ - 1)))
        q_pair = []
        for p in range(npair):
            q_all = q_ref[:, p * LANE:(p + 1) * LANE]
            q_pair.append(jnp.concatenate([jnp.where(_head_mask(head), q_all, jnp.zeros_like(q_all)) for head in range(2)],
                                          axis=0))

        def block(kb, state, diag):
            run, acc = state
            start = pl.multiple_of(kb * SBQ, SBQ)
            z = jnp.concatenate([_dg(q_pair[p], k_ref[pl.ds(start, SBQ), p * LANE:(p + 1) * LANE], NT)
                                 for p in range(npair)], axis=0) * scale
            lb = _log_sigmoid(z)
            lk = lb - z
            if diag:
                lk = jnp.where(strict, lk, 0.0)
            hi, lo = _split(lk)
            sums = _dg(jnp.concatenate([hi, lo], axis=0), m_suffix, NN)
            w = jnp.exp(lb + (run + sums[:rows] + sums[rows:]))
            if diag:
                w = jnp.where(strict, w, 0.0)
            wb = w.astype(BF16)
            acc = acc + jnp.concatenate(
                [_dg(wb[2 * p * SBQ:2 * (p + 1) * SBQ], v_ref[pl.ds(start, SBQ), p * LANE:(p + 1) * LANE], NN)
                 for p in range(npair)], axis=0)
            run = run + jnp.sum(hi.astype(F32) + lo.astype(F32), axis=1, keepdims=True)
            return run, acc

        state = block(qi, (jnp.zeros((rows, 1), F32), jnp.zeros((rows, LANE), F32)), True)

        def live(c):
            return jnp.logical_and(c[0] < qi, jnp.max(c[1][0]) > SB_DEAD)

        def step(c):
            return c[0] + 1, block(qi - 1 - c[0], c[1], False)

        count, (run, acc) = lax.while_loop(live, step, (jnp.int32(0), state))
        n_ref[grp * nq + qi] = count.astype(F32)
        hm = _head_mask(0)
        for p in range(npair):
            lo_rows, hi_rows = slice(2 * p * SBQ, (2 * p + 1) * SBQ), slice((2 * p + 1) * SBQ, (2 * p + 2) * SBQ)
            y_ref[:, p * LANE:(p + 1) * LANE] = jnp.where(hm, acc[lo_rows], acc[hi_rows]).astype(BF16)
            l_ref[p] = jnp.where(hm, run[lo_rows], run[hi_rows])
        if nc:
            carry.ride(grid, riders, "finish")

    outs = pl.pallas_call(
        body, name="sb_fwd", grid=grid,
        in_specs=[_spec((SBQ, wide), lambda g, i: (i, g)),
                  _spec((t, wide), lambda g, i: (0, ngrp + g)),
                  _spec((t, wide), lambda g, i: (0, 2 * ngrp + g))] + (carry.in_specs if nc else []),
        out_specs=[_spec((SBQ, wide), lambda g, i: (i, g)), _spec((npair, SBQ, LANE), lambda g, i: (g, i, 0)),
                   pl.BlockSpec(memory_space=pltpu.SMEM)] + (carry.out_specs if nc else []),
        out_shape=[jax.ShapeDtypeStruct((t, 512), BF16), jax.ShapeDtypeStruct((4, t, LANE), F32),
                   jax.ShapeDtypeStruct((ngrp * nq,), F32)] + (carry.out_shape if nc else []),
        scratch_shapes=carry.scratch if nc else [],
        compiler_params=_params(("arbitrary", "arbitrary")),
    )(proj_b, proj_b, proj_b, *(carry.groups if nc else []))
    return outs[0], outs[1], outs[2], (carry.place(outs[3:]) if nc else [])


def _sb_bwd(proj_b, dy, ltot, nblk, rider=None):
    t = proj_b.shape[0]
    nq = t // SBQ
    scale = 1.0 / math.sqrt(64.0)

    npair = SB_PAIRS_BWD
    wide = npair * LANE
    ngrp = 4 // npair
    chains = [(p, head) for p in range(npair) for head in range(2)]
    grid = (ngrp, nq)
    nr = rider.n if rider is not None else 0
    per_count = SB_PAIRS_FWD // SB_PAIRS_BWD

    def body(*refs):
        q_ref, k_ref, v_ref, dy_ref, l_ref, n_ref = refs[:6]
        dq_ref, dk_ref, dv_ref = refs[6 + nr:9 + nr]
        dk_acc, dv_acc = refs[9 + 2 * nr:11 + 2 * nr]
        riders = (refs[6:6 + nr], refs[9 + nr:9 + 2 * nr], refs[-2], refs[-1]) if nr else None
        if nr:
            rider.ride(grid, riders, "start")
        grp = pl.program_id(0)
        qi = pl.program_id(1)

        @pl.when(qi == 0)
        def _():
            dk_acc[...] = jnp.zeros_like(dk_acc)
            dv_acc[...] = jnp.zeros_like(dv_acc)

        row, col = _sb_masks()
        m_prefix = _ones_where(row <= col)
        m_before = _ones_where(row < col)
        rows = len(chains) * SBQ
        strict = (lax.broadcasted_iota(jnp.int32, (rows, SBQ), 1)
                  < (lax.broadcasted_iota(jnp.int32, (rows, SBQ), 0) & (SBQ - 1)))
        q_pair, do_pair, ltot = [], [], []
        for p in range(npair):
            pl_ = slice(p * LANE, (p + 1) * LANE)
            q_all = q_ref[:, pl_]
            do_all = dy_ref[:, pl_].astype(BF16)
            q_pair.append(jnp.concatenate([jnp.where(_head_mask(h), q_all, jnp.zeros_like(q_all)) for h in range(2)], axis=0))
            do_pair.append(jnp.concatenate([jnp.where(_head_mask(h), do_all, jnp.zeros_like(do_all)) for h in range(2)], axis=0))
            ltot += [l_ref[p][:, h * 64:h * 64 + 1] for h in range(2)]
        ltot = jnp.concatenate(ltot, axis=0)

        def pair_rows(a, p):
            return a[2 * p * SBQ:2 * (p + 1) * SBQ]

        def block(kb, state, diag):
            seen, dseen, dq = state
            start = pl.multiple_of(kb * SBQ, SBQ)
            kk = [k_ref[pl.ds(start, SBQ), p * LANE:(p + 1) * LANE] for p in range(npair)]
            vv = [v_ref[pl.ds(start, SBQ), p * LANE:(p + 1) * LANE] for p in range(npair)]
            z = jnp.concatenate([_dg(q_pair[p], kk[p], NT) for p in range(npair)], axis=0) * scale
            lb = _log_sigmoid(z)
            lk = lb - z
            if diag:
                lk = jnp.where(strict, lk, 0.0)
            hi, lo = _split(lk)
            sums = _dg(jnp.concatenate([hi, lo], axis=0), m_prefix, NN)
            w = jnp.exp(lb + ((ltot - seen) - (sums[:rows] + sums[rows:])))
            if diag:
                w = jnp.where(strict, w, 0.0)
            wb = w.astype(BF16)
            da = w * jnp.concatenate([_dg(do_pair[p], vv[p], NT) for p in range(npair)], axis=0)
            dah, dal = _split(da)
            dsums = _dg(jnp.concatenate([dah, dal], axis=0), m_before, NN)
            sig = jnp.exp(lb)
            dz = (da * (1.0 - sig) - (dseen + dsums[:rows] + dsums[rows:]) * sig) * scale
            if diag:
                dz = jnp.where(strict, dz, 0.0)
            dzb = dz.astype(BF16)
            for p in range(npair):
                pl_ = slice(p * LANE, (p + 1) * LANE)
                dv_acc[pl.ds(start, SBQ), pl_] += _dg(pair_rows(wb, p), do_pair[p], TN)
                dk_acc[pl.ds(start, SBQ), pl_] += _dg(pair_rows(dzb, p), q_pair[p], TN)
            dq = dq + jnp.concatenate([_dg(pair_rows(dzb, p), kk[p], NN) for p in range(npair)], axis=0)
            seen = seen + jnp.sum(hi.astype(F32) + lo.astype(F32), axis=1, keepdims=True)
            dseen = dseen + jnp.sum(da, axis=1, keepdims=True)
            return seen, dseen, dq

        zero = (jnp.zeros((rows, 1), F32), jnp.zeros((rows, 1), F32), jnp.zeros((rows, LANE), F32))
        first = qi - n_ref[(grp // per_count) * nq + qi].astype(jnp.int32)
        state = lax.fori_loop(first, qi, lambda kb, c: block(kb, c, False), zero)
        _, _, dq = block(qi, state, True)
        for p in range(npair):
            dq_ref[:, p * LANE:(p + 1) * LANE] = jnp.where(
                _head_mask(0), dq[2 * p * SBQ:(2 * p + 1) * SBQ], dq[(2 * p + 1) * SBQ:(2 * p + 2) * SBQ]).astype(BF16)

        @pl.when(qi == nq - 1)
        def _():
            dk_ref[...] = dk_acc[...].astype(BF16)
            dv_ref[...] = dv_acc[...].astype(BF16)

        if nr:
            rider.ride(grid, riders, "finish")

    full = jax.ShapeDtypeStruct((t, 512), BF16)
    outs = pl.pallas_call(
        body, name="sb_bwd", grid=grid,
        in_specs=[_spec((SBQ, wide), lambda g, i: (i, g)),
                  _spec((t, wide), lambda g, i: (0, ngrp + g)),
                  _spec((t, wide), lambda g, i: (0, 2 * ngrp + g)),
                  _spec((SBQ, wide), lambda g, i: (i, ngrp + g)),
                  _spec((npair, SBQ, LANE), lambda g, i: (g, i, 0)),
                  pl.BlockSpec(memory_space=pltpu.SMEM)] + (rider.in_specs if nr else []),
        out_specs=[_spec((SBQ, wide), lambda g, i: (i, g)),
                   _spec((t, wide), lambda g, i: (0, g)), _spec((t, wide), lambda g, i: (0, g))]
        + (rider.out_specs if nr else []),
        out_shape=[full, full, full] + (rider.out_shape if nr else []),
        scratch_shapes=[pltpu.VMEM((t, wide), F32), pltpu.VMEM((t, wide), F32)] + (rider.scratch if nr else []),
        compiler_params=pltpu.CompilerParams(dimension_semantics=("arbitrary", "arbitrary"),
                                             vmem_limit_bytes=VMEM_LIMIT_BIG),
    )(proj_b, proj_b, proj_b, dy, ltot, nblk, *(rider.groups if nr else []))
    return outs[0], outs[1], outs[2], (rider.place(outs[3:]) if nr else [])


def _hgrn_gates(qr, fr, c0, c1):
    mx = jnp.maximum(c0, c1)
    e0, e1 = jnp.exp(c0 - mx), jnp.exp(c1 - mx)
    lb = e1 / (e0 + e1)
    sx = _sigmoid(fr)
    f = lb + (1.0 - lb) * sx
    k = (1.0 - lb) * (1.0 - sx)
    sq = _sigmoid(qr)
    return lb, sx, f, k, sq, qr * sq


def _chunk_sums(mask, x):
    n = x.shape[1]
    hi, lo = _split(x)
    both = _dg(_ones_where(mask), jnp.concatenate([hi, lo], axis=1), NN)
    return both[:, :n] + both[:, n:]


def _chunk_masks():
    row = lax.broadcasted_iota(jnp.int32, (CHUNK, CHUNK), 0)
    col = lax.broadcasted_iota(jnp.int32, (CHUNK, CHUNK), 1)
    return col <= row, col >= row


def _hgrn_fwd(proj_c, small, carry=None):
    t = proj_c.shape[1]
    nc = t // CHUNK
    nh = D // HD

    hps = HGRN_HPS
    wide = hps * HD
    grid = (nh // hps, nc)
    nr = carry.n if carry is not None else 0

    def body(*refs):
        p_ref, c0_ref, c1_ref, gam_ref = refs[:4]
        o_ref, y_ref, sst_ref = refs[4 + nr:7 + nr]
        st_ref = refs[7 + 2 * nr]
        riders = (refs[4:4 + nr], refs[7 + nr:7 + 2 * nr], refs[-2], refs[-1]) if nr else None
        if nr:
            carry.ride(grid, riders, "start")
        c = pl.program_id(1)

        @pl.when(c == 0)
        def _():
            st_ref[...] = jnp.zeros_like(st_ref)

        _, _, f_all, k_all, _, q_all = _hgrn_gates(p_ref[0], p_ref[1], c0_ref[...], c1_ref[...])
        tril, _ = _chunk_masks()
        b_all = _chunk_sums(tril, jnp.log(f_all))
        for j in range(hps):
            ln = slice(j * HD, (j + 1) * HD)
            st0 = st_ref[j]
            sst_ref[j] = st0
            v, g = p_ref[2, :, ln], p_ref[3, :, ln]
            q, k, b = q_all[:, ln], k_all[:, ln], b_all[:, ln]
            bm = b[CHUNK // 2 - 1:CHUNK // 2]
            bl = b[CHUNK - 1:CHUNK]
            qd = q * jnp.exp(b)
            qt = q * jnp.exp(b - bm)
            kt = k * jnp.exp(bm - b)
            kl = k * jnp.exp(bl - b)
            vb = v.astype(BF16)
            att = jnp.where(tril, _dot3(qt, kt, NT), 0.0)
            o = _dg(qd.astype(BF16), st0.astype(BF16), NT) + _dg(att.astype(BF16), vb, NN)
            st_ref[j] = st0 * jnp.exp(bl) + _dg(vb, kl.astype(BF16), TN)
            o_ref[:, ln] = o
            r = lax.rsqrt(jnp.mean(o * o, axis=-1, keepdims=True) + RMS_EPS)
            y_ref[:, ln] = (o * r * gam_ref[...] * (g * _sigmoid(g))).astype(BF16)
        if nr:
            carry.ride(grid, riders, "finish")

    outs = pl.pallas_call(
        body, name="hgrn_fwd", grid=grid,
        in_specs=[_spec((4, CHUNK, wide), lambda h, c: (0, c, h)),
                  _spec((None, 1, wide), lambda h, c: (R_CLB, 0, h)),
                  _spec((None, 1, wide), lambda h, c: (R_CLB + 1, 0, h)),
                  _spec((None, 1, HD), lambda h, c: (R_GAM, 0, 0))] + (carry.in_specs if nr else []),
        out_specs=[_spec((CHUNK, wide), lambda h, c: (c, h)), _spec((CHUNK, wide), lambda h, c: (c, h)),
                   _spec((None, hps, HD, HD), lambda h, c: (c, h, 0, 0))] + (carry.out_specs if nr else []),
        out_shape=[jax.ShapeDtypeStruct((t, D), F32), jax.ShapeDtypeStruct((t, D), BF16),
                   jax.ShapeDtypeStruct((nc, nh, HD, HD), F32)] + (carry.out_shape if nr else []),
        scratch_shapes=[pltpu.VMEM((hps, HD, HD), F32)] + (carry.scratch if nr else []),
        compiler_params=_params(("arbitrary", "arbitrary")),
    )(proj_c, small, small, small, *(carry.groups if nr else []))
    return outs[0], outs[1], outs[2], (carry.place(outs[3:]) if nr else [])


def _hgrn_bwd(proj_c, small, o, sst, dyc, rider=None):
    t = proj_c.shape[1]
    nc = t // CHUNK
    nh = D // HD

    hps = HGRN_HPS
    wide = hps * HD
    ng = nh // hps
    grid = (ng, nc)
    nr = rider.n if rider is not None else 0

    def body(*refs):
        p_ref, c0_ref, c1_ref, gam_ref, o_ref, sst_ref, dy_ref = refs[:7]
        dp_ref, dclb_ref, dgam_ref = refs[7 + nr:10 + nr]
        dst_ref, dlb_acc, dgam_acc = refs[10 + 2 * nr:13 + 2 * nr]
        riders = (refs[7:7 + nr], refs[10 + nr:10 + 2 * nr], refs[-2], refs[-1]) if nr else None
        if nr:
            rider.ride(grid, riders, "start")
        group = pl.program_id(0)
        step = pl.program_id(1)

        @pl.when(step == 0)
        def _():
            dst_ref[...] = jnp.zeros_like(dst_ref)
            dlb_acc[...] = jnp.zeros_like(dlb_acc)

        @pl.when((step == 0) & (group == 0))
        def _():
            dgam_acc[...] = jnp.zeros_like(dgam_acc)

        gam = gam_ref[...]
        qr_all = p_ref[0]
        lb_all, sx_all, f_all, k_all, sq_all, q_all = _hgrn_gates(qr_all, p_ref[1], c0_ref[...], c1_ref[...])
        tril, triu = _chunk_masks()
        b_all = _chunk_sums(tril, jnp.log(f_all))
        dq_parts, dk_parts, db_parts, dgam_parts = [], [], [], []
        for j in range(hps):
            ln = slice(j * HD, (j + 1) * HD)
            st0 = sst_ref[j]
            dst1 = dst_ref[j]
            v, g = p_ref[2, :, ln], p_ref[3, :, ln]
            q, k, b = q_all[:, ln], k_all[:, ln], b_all[:, ln]
            bm = b[CHUNK // 2 - 1:CHUNK // 2]
            bl = b[CHUNK - 1:CHUNK]
            eb = jnp.exp(b)
            e_qt = jnp.exp(b - bm)
            e_kt = jnp.exp(bm - b)
            e_kl = jnp.exp(bl - b)
            e_bl = jnp.exp(bl)
            qd, qt, kt, kl = q * eb, q * e_qt, k * e_kt, k * e_kl
            ov = o_ref[:, ln]
            r = lax.rsqrt(jnp.mean(ov * ov, axis=-1, keepdims=True) + RMS_EPS)
            oh = ov * r
            sg = _sigmoid(g)
            dy = dy_ref[:, ln]
            dp_ref[3, :, ln] = (dy * oh * gam * (sg * (1.0 + g * (1.0 - sg)))).astype(BF16)
            dyv = dy * (g * sg)
            gdy = dyv * gam
            do = (gdy - oh * jnp.mean(gdy * oh, axis=-1, keepdims=True)) * r
            dob, vb = do.astype(BF16), v.astype(BF16)
            st0b, dst1b = st0.astype(BF16), dst1.astype(BF16)
            st1 = st0 * e_bl + _dg(vb, kl.astype(BF16), TN)
            att = jnp.where(tril, _dot3(qt, kt, NT), 0.0)
            datt = jnp.where(tril, _dg(dob, vb, NT), 0.0)
            dv = _dg(att.astype(BF16), dob, TN) + _dg(kl.astype(BF16), dst1b, NT)
            dq = _dot3(datt, kt, NN) * e_qt + _dg(dob, st0b, NN) * eb
            dk = _dot3(datt, qt, TN) * e_kt + _dg(vb, dst1b, NN) * e_kl
            db = q * dq - k * dk
            last = lax.broadcasted_iota(jnp.int32, (CHUNK, 1), 0) == CHUNK - 1
            db = db + jnp.where(last, jnp.sum(dst1 * st1, axis=0, keepdims=True), 0.0)
            dst_ref[j] = dst1 * e_bl + _dg(dob, qd.astype(BF16), TN)
            dp_ref[2, :, ln] = dv.astype(BF16)
            dq_parts.append(dq)
            dk_parts.append(dk)
            db_parts.append(db)
            dgam_parts.append(jnp.sum(dyv * oh, axis=0, keepdims=True))

        dq_all, dk_all = jnp.concatenate(dq_parts, axis=1), jnp.concatenate(dk_parts, axis=1)
        dlf = _chunk_sums(triu, jnp.concatenate(db_parts, axis=1))
        dp_ref[0] = (dq_all * (sq_all * (1.0 + qr_all * (1.0 - sq_all)))).astype(BF16)
        tmp = dlf / f_all - dk_all
        dp_ref[1] = (tmp * (1.0 - lb_all) * sx_all * (1.0 - sx_all)).astype(BF16)
        dlb_acc[...] += jnp.sum((1.0 - sx_all) * tmp, axis=0, keepdims=True)
        dgam_acc[...] += functools.reduce(lambda a, b: a + b, dgam_parts)

        @pl.when(step == nc - 1)
        def _():
            d1 = dlb_acc[...] * lb_all * (1.0 - lb_all)
            dclb_ref[...] = jnp.where(lax.broadcasted_iota(jnp.int32, (2, wide), 0) == 0, -d1, d1)

        @pl.when((step == nc - 1) & (group == ng - 1))
        def _():
            dgam_ref[...] = dgam_acc[...]

        if nr:
            rider.ride(grid, riders, "finish")

    rev = lambda h, s: (nc - 1 - s, h)
    outs = pl.pallas_call(
        body, name="hgrn_bwd", grid=grid,
        in_specs=[_spec((4, CHUNK, wide), lambda h, s: (0, nc - 1 - s, h)),
                  _spec((None, 1, wide), lambda h, s: (R_CLB, 0, h)),
                  _spec((None, 1, wide), lambda h, s: (R_CLB + 1, 0, h)),
                  _spec((None, 1, HD), lambda h, s: (R_GAM, 0, 0)),
                  _spec((CHUNK, wide), rev),
                  _spec((None, hps, HD, HD), lambda h, s: (nc - 1 - s, h, 0, 0)),
                  _spec((CHUNK, wide), rev)] + (rider.in_specs if nr else []),
        out_specs=[_spec((4, CHUNK, wide), lambda h, s: (0, nc - 1 - s, h)),
                   _spec((2, wide), lambda h, s: (0, h)),
                   _spec((1, HD), lambda h, s: (0, 0))] + (rider.out_specs if nr else []),
        out_shape=[jax.ShapeDtypeStruct((4, t, D), BF16), jax.ShapeDtypeStruct((2, D), F32),
                   jax.ShapeDtypeStruct((1, HD), F32)] + (rider.out_shape if nr else []),
        scratch_shapes=[pltpu.VMEM((hps, HD, HD), F32), pltpu.VMEM((1, wide), F32), pltpu.VMEM((1, HD), F32)]
        + (rider.scratch if nr else []),
        compiler_params=_params(("arbitrary", "arbitrary")),
    )(proj_c, small, small, small, o, sst, dyc, *(rider.groups if nr else []))
    return outs[0], outs[1], outs[2], (rider.place(outs[3:]) if nr else [])


def _adamw(name, w, grads, m, v):
    nl = len(grads)
    rows, cols = grads[0].shape
    br = rows
    for cand in (512, 352, 256):
        if rows % cand == 0:
            br = cand
            break
    nb = rows // br
    c1 = 1.0 - ADAM_B1 ** ADAM_STEP
    c2 = 1.0 - ADAM_B2 ** ADAM_STEP

    def body(w_ref, m_ref, v_ref, *refs):
        g_refs, (d_ref, mo_ref, vo_ref, go_ref) = refs[:nl], refs[nl:]
        layer = pl.program_id(0)
        gv = g_refs[0][...]
        for k in range(1, nl):
            gv = jnp.where(layer == k, g_refs[k][...], gv)
        mn = ADAM_B1 * m_ref[...] + (1.0 - ADAM_B1) * gv
        vn = ADAM_B2 * v_ref[...] + (1.0 - ADAM_B2) * (gv * gv)
        mo_ref[...] = mn
        vo_ref[...] = vn
        go_ref[...] = gv
        d_ref[...] = -ADAM_LR * ((mn / c1) / (jnp.sqrt(vn / c2) + ADAM_EPS) + ADAM_WD * w_ref[...])

    blk = _spec((br, cols), lambda l, i: (l * nb + i, 0))
    g_specs = [_spec((br, cols), lambda l, i, k=k: (jnp.where(l == k, i, 0), 0)) for k in range(nl)]
    shape = jax.ShapeDtypeStruct((nl * rows, cols), F32)
    return pl.pallas_call(
        body, name=name, grid=(nl, nb), in_specs=[blk] * 3 + g_specs, out_specs=[blk] * 4, out_shape=[shape] * 4,
        compiler_params=_params(("arbitrary", "arbitrary")),
    )(w, m, v, *grads)


def _place():
    x, y, c = lax.axis_index("x"), lax.axis_index("y"), lax.axis_index("c")
    chips = [(1 - x, y), (x, 1 - y), (1 - x, 1 - y)]
    return x, y, c, chips


class _Rider:
    n = 0
    relay = None

    def ride(self, grid, refs, when):
        if not self.n:
            return
        ids = [pl.program_id(a) for a in range(len(grid))]
        first = functools.reduce(jnp.logical_and, [i == 0 for i in ids])
        last = functools.reduce(jnp.logical_and, [i == g - 1 for i, g in zip(ids, grid)])
        phases = [(first, self.start), (last, self.relay)] if when == "start" else [(last, self.finish)]
        for cond, phase in phases:
            if phase is not None:
                pl.when(cond)(functools.partial(phase, *refs))


class _Gather(_Rider):
    PER_GROUP = 7

    def __init__(self, groups):
        self.groups = list(groups)
        self.n = len(self.groups)
        self.in_specs = [ANY] * self.n
        self.out_specs = [ANY] * self.n
        self.out_shape = [jax.ShapeDtypeStruct((NSH,) + g.shape, g.dtype) for g in self.groups]
        sems = pltpu.SemaphoreType.DMA((self.PER_GROUP * self.n,))
        self.scratch = [sems, sems] if self.n else []

    def _copies(self, ins, outs, send, recv):
        x, y, c, chips = _place()
        sibling = (x, y, 1 - c)

        def half(gi, chip, hc):
            return outs[gi].at[2 * chip[0] + chip[1], hc]

        def copy(gi, k, src, dst, to):
            sem = self.PER_GROUP * gi + k
            return pltpu.make_async_remote_copy(src_ref=src, dst_ref=dst, send_sem=send.at[sem], recv_sem=recv.at[sem],
                                                device_id=to, device_id_type=MESH)

        pairs = [(gi, j, chip) for gi in range(self.n) for j, chip in enumerate(chips)]
        first = [copy(gi, j, ins[gi].at[c], half(gi, (x, y), c), (*chip, c)) for gi, j, chip in pairs]
        first += [copy(gi, 6, ins[gi], outs[gi].at[2 * x + y], sibling) for gi in range(self.n)]
        landed = [copy(gi, j, half(gi, chip, c), half(gi, chip, c), sibling) for gi, j, chip in pairs]
        relay = [copy(gi, 3 + j, half(gi, chip, c), half(gi, chip, c), sibling) for gi, j, chip in pairs]
        relayed = [copy(gi, 3 + j, half(gi, chip, 1 - c), half(gi, chip, 1 - c), sibling) for gi, j, chip in pairs]
        relayed += [copy(gi, 6, ins[gi], outs[gi].at[2 * x + y], sibling) for gi in range(self.n)]
        return first, landed, relay, relayed

    def start(self, ins, outs, send, recv):
        for cp in self._copies(ins, outs, send, recv)[0]:
            cp.start()

    def relay(self, ins, outs, send, recv):
        _, landed, relay, _ = self._copies(ins, outs, send, recv)
        for arrived, onward in zip(landed, relay):
            arrived.wait_recv()
            onward.start()

    def finish(self, ins, outs, send, recv):
        first, _, relay, relayed = self._copies(ins, outs, send, recv)
        for cp in relayed:
            cp.wait_recv()
        for cp in first + relay:
            cp.wait_send()

    def place(self, outs):
        return list(outs)


def _alone(name, rider):
    n = rider.n

    def body(*refs):
        parts = (refs[:n], refs[n:2 * n], refs[2 * n], refs[2 * n + 1])
        rider.start(*parts)
        if rider.relay is not None:
            rider.relay(*parts)
        rider.finish(*parts)

    outs = pl.pallas_call(
        body, name=name, in_specs=rider.in_specs, out_specs=rider.out_specs, out_shape=rider.out_shape,
        scratch_shapes=rider.scratch, compiler_params=pltpu.CompilerParams(has_side_effects=True),
    )(*rider.groups)
    return rider.place(outs)


class _Swap(_Rider):
    def __init__(self, slots):
        self.slots = list(slots)
        self.groups = [buf for buf, _, _ in self.slots]
        self.n = len(self.slots)
        self.in_specs = [ANY] * self.n
        self.out_specs = [ANY] * self.n
        self.out_shape = [jax.ShapeDtypeStruct((NSH, rows // 2, buf.shape[2]), buf.dtype) for buf, _, rows in self.slots]
        self.scratch = [pltpu.SemaphoreType.DMA((self.n,)), pltpu.SemaphoreType.DMA((self.n,))]

    def _copies(self, ins, outs, send, recv):
        x, y, c, _ = _place()
        cps = []
        for i, (_, row0, rows) in enumerate(self.slots):
            half = rows // 2
            src = ins[i].at[:, pl.ds(pl.multiple_of(row0 + (1 - c) * half, 16), half)]
            cps.append(pltpu.make_async_remote_copy(src_ref=src, dst_ref=outs[i], send_sem=send.at[i],
                                                    recv_sem=recv.at[i], device_id=(x, y, 1 - c), device_id_type=MESH))
        return cps

    def start(self, ins, outs, send, recv):
        for cp in self._copies(ins, outs, send, recv):
            cp.start()

    def finish(self, ins, outs, send, recv):
        for cp in self._copies(ins, outs, send, recv):
            cp.wait()

    def place(self, outs):
        return list(outs)


class _Share(_Rider):
    def __init__(self, arrays):
        self.groups = list(arrays)
        self.n = len(self.groups)
        self.in_specs = [ANY] * self.n
        self.out_specs = [ANY] * self.n
        self.out_shape = [jax.ShapeDtypeStruct((2,) + g.shape, g.dtype) for g in self.groups]
        self.scratch = [pltpu.SemaphoreType.DMA((self.n,)), pltpu.SemaphoreType.DMA((self.n,))]

    def _copies(self, ins, outs, send, recv, half):
        x, y, c, _ = _place()
        return [pltpu.make_async_remote_copy(src_ref=ins[gi], dst_ref=outs[gi].at[c if half == "mine" else 1 - c],
                                             send_sem=send.at[gi], recv_sem=recv.at[gi], device_id=(x, y, 1 - c),
                                             device_id_type=MESH) for gi in range(self.n)]

    def start(self, ins, outs, send, recv):
        for cp in self._copies(ins, outs, send, recv, "mine"):
            cp.start()

    def finish(self, ins, outs, send, recv):
        for cp in self._copies(ins, outs, send, recv, "theirs"):
            cp.wait_recv()
        for cp in self._copies(ins, outs, send, recv, "mine"):
            cp.wait_send()

    def place(self, outs):
        c = lax.axis_index("c")
        return [lax.dynamic_update_index_in_dim(o, g, c, 0) for o, g in zip(outs, self.groups)]


class _Send(_Rider):
    def __init__(self, arrays):
        self.groups = list(arrays)
        self.n = len(self.groups)
        self.in_specs = [ANY] * self.n
        self.out_specs = [ANY] * self.n
        self.out_shape = [jax.ShapeDtypeStruct((3,) + g.shape[1:], g.dtype) for g in self.groups]
        self.scratch = [pltpu.SemaphoreType.DMA((3 * self.n,)), pltpu.SemaphoreType.DMA((3 * self.n,))]

    def _copies(self, ins, outs, send, recv):
        x, y, c, chips = _place()
        return [pltpu.make_async_remote_copy(src_ref=ins[gi].at[2 * chip[0] + chip[1]], dst_ref=outs[gi].at[j],
                                             send_sem=send.at[3 * gi + j], recv_sem=recv.at[3 * gi + j],
                                             device_id=(*chip, c), device_id_type=MESH)
                for gi in range(self.n) for j, chip in enumerate(chips)]

    def start(self, ins, outs, send, recv):
        for cp in self._copies(ins, outs, send, recv):
            cp.start()

    def finish(self, ins, outs, send, recv):
        for cp in self._copies(ins, outs, send, recv):
            cp.wait()

    def place(self, outs):
        return list(outs)


def _pair_sum(name, slots, got, c_idx):
    n = len(slots)
    in_specs, out_specs, out_shape, operands = [], [], [], []
    for (buf, row0, rows), g in zip(slots, got):
        hb, cols = rows // 4, buf.shape[2]
        in_specs += [pl.BlockSpec((None, hb, cols), lambda q, i, cr, r=row0 // hb: (q, r + 2 * cr[0] + i, 0)),
                     pl.BlockSpec((None, hb, cols), lambda q, i, cr: (q, i, 0))]
        out_specs.append(pl.BlockSpec((None, hb, cols), lambda q, i, cr: (q, i, 0)))
        out_shape.append(jax.ShapeDtypeStruct(g.shape, BF16))
        operands += [buf, g]

    def body(c_ref, *refs):
        for i in range(n):
            refs[2 * n + i][...] = (refs[2 * i][...].astype(F32) + refs[2 * i + 1][...].astype(F32)).astype(BF16)

    return pl.pallas_call(
        body, name=name,
        grid_spec=pltpu.PrefetchScalarGridSpec(num_scalar_prefetch=1, grid=(NSH, 2), in_specs=in_specs, out_specs=out_specs),
        out_shape=out_shape, compiler_params=_params(("parallel", "parallel")),
    )(c_idx, *operands)


def _owner_sum(name, pairs, got, p_idx):
    n = len(pairs)
    in_specs, out_specs, out_shape, operands = [], [], [], []
    for own, g in zip(pairs, got):
        _, rows, cols = own.shape
        hb = rows // 2
        in_specs += [pl.BlockSpec((None, hb, cols), lambda i, pr: (pr[0], i, 0)),
                     pl.BlockSpec((3, hb, cols), lambda i, pr: (0, i, 0))]
        out_specs.append(pl.BlockSpec((hb, cols), lambda i, pr: (i, 0)))
        out_shape.append(jax.ShapeDtypeStruct((rows, cols), F32))
        operands += [own, g]

    def body(p_ref, *refs):
        for i in range(n):
            a_ref, b_ref = refs[2 * i], refs[2 * i + 1]
            refs[2 * n + i][...] = ((a_ref[...].astype(F32) + b_ref[0].astype(F32)) + b_ref[1].astype(F32)) + b_ref[2].astype(F32)

    return pl.pallas_call(
        body, name=name,
        grid_spec=pltpu.PrefetchScalarGridSpec(num_scalar_prefetch=1, grid=(2,), in_specs=in_specs, out_specs=out_specs),
        out_shape=out_shape, compiler_params=_params(("parallel",)),
    )(p_idx, *operands)


def _sum_small(slab):
    def body(in_ref, out_ref, all_ref, send, recv):
        x, y, c, _ = _place()
        me = 4 * x + 2 * y + c
        all_ref[me] = in_ref[...]
        cps = []
        for k in range(1, 8):
            peer = (x ^ (k >> 2), y ^ ((k >> 1) & 1), c ^ (k & 1))
            cps.append(pltpu.make_async_remote_copy(src_ref=in_ref, dst_ref=all_ref.at[me], send_sem=send.at[k - 1],
                                                    recv_sem=recv.at[k - 1], device_id=peer, device_id_type=MESH))
        for cp in cps:
            cp.start()
        for cp in cps:
            cp.wait()
        total = all_ref[0]
        for d in range(1, 8):
            total = total + all_ref[d]
        out_ref[...] = total

    return pl.pallas_call(
        body, name="sum_small",
        in_specs=[pl.BlockSpec(memory_space=pltpu.VMEM)], out_specs=pl.BlockSpec(memory_space=pltpu.VMEM),
        out_shape=jax.ShapeDtypeStruct(slab.shape, F32),
        scratch_shapes=[pltpu.VMEM((8,) + slab.shape, F32), pltpu.SemaphoreType.DMA((7,)), pltpu.SemaphoreType.DMA((7,))],
        compiler_params=pltpu.CompilerParams(has_side_effects=True),
    )(slab)


FFNS = ("pre0", "post0", "pre1", "post1")
W_SHAPES = dict({f + "_gu": (NSH, 2, FS, D) for f in FFNS}, **{f + "_d": (NSH, FS, D) for f in FFNS},
                ab_in=(NSH, D, 768), ab_out=(NSH, 256, D), conv=(NSH, 2, 8, LANE), c_in=(NSH, D, D), c_out=(NSH, 256, D))
CARRIED = dict(pre0_norm=("pre0_gu",), pre0_up=("pre0_d",), pre0_down=("ab_in", "conv"), ab_proj_a=("ab_out",),
               c_proj=("post1_d",),
               sb_fwd=("post0_gu", "post0_d"), post0_up=("pre1_gu",), post0_down=("pre1_d",),
               pre1_up=("c_in", "c_out"), hgrn_fwd=("post1_gu",))


FFN_SLOTS = dict(pre0=(0, 1, 0), pre1=(2, 3, 1), post0=(4, 5, 2), post1=(6, 7, 3))
GRAD_SLOTS = dict(ab_out=("b", B_ABOUT, 256), c_in=("b", B_CIN, D), c_out=("b", B_COUT, 256), ab_in=("c", 0, D))
for _f, (_g, _u, _d) in FFN_SLOTS.items():
    GRAD_SLOTS.update({_f + "_g": ("a", _g * FS, FS), _f + "_u": ("a", _u * FS, FS), _f + "_d": ("b", _d * FS, FS)})
REDUCE_STAGES = dict(x=("post1_g", "post1_u", "post1_d", "c_out"), w=("c_in", "pre1_d"),
                     y=("pre1_g", "pre1_u", "post0_g", "post0_u", "post0_d", "ab_out"),
                     z1=("ab_in",), z2=("pre0_g", "pre0_u", "pre0_d"))


def _local_step(x, target, weights, small, shards=None, place=None):
    t = x.shape[0]
    tm = min(MM_TILE, t)
    th = min(HALF_TILE, t)
    tw = min(WGRAD_TILE, t)
    nt = t // tm
    tok_si = _spec((tm, D), lambda s, i: (i, 0))
    tok_is = _spec((tm, D), lambda i, s: (i, 0))
    w = dict(weights)
    reduced = {}

    def stage_slots(stage, buffers):
        return [(buffers[GRAD_SLOTS[n][0]],) + GRAD_SLOTS[n][1:] for n in REDUCE_STAGES[stage]]

    def swap_rider(stage, buffers):
        return _Swap(stage_slots(stage, buffers)) if place is not None else None

    def reduce_start(stage, buffers, swapped=None):
        if place is None:
            return [], None
        slots = stage_slots(stage, buffers)
        if swapped is None:
            swapped = _alone("grad_swap_" + stage, _Swap(slots))
        pairs = _pair_sum("grad_pair_sum_" + stage, slots, swapped, place[0])
        return pairs, _Send(pairs)

    def reduce_end(stage, pairs, landed):
        if place is None:
            return None
        mine = _owner_sum("grad_owner_sum_" + stage, pairs, landed, place[1])
        return _Share(mine)

    def shared(stage, landed):
        if place is not None:
            reduced.update(zip(REDUCE_STAGES[stage], landed))

    def carried(kernel_name):
        names = [n for n in CARRIED[kernel_name] if n not in w]
        return names, (_Gather([shards[n] for n in names]) if names else None)

    def land(names, arrays):
        for n, a in zip(names, arrays):
            w[n] = a.reshape(W_SHAPES[n])

    def ffn_forward(tag, h, hn, next_row):
        names, gather = carried(tag + "_up") if tag + "_up" in CARRIED else ([], None)
        s_up, s_gate, a, got = _ffn_up(tag + "_up", hn, w[tag + "_gu"], gather)
        land(names, got)
        names, gather = carried(tag + "_down") if tag + "_down" in CARRIED else ([], None)
        out = _ffn_down(tag + "_down", a, w[tag + "_d"], h, gather, (small, next_row) if next_row is not None else None)
        if gather is not None:
            out, got = out
            land(names, got)
        out, hn_next = out if next_row is not None else (out, None)
        return out, hn_next, (h, hn, s_up, s_gate, a)

    def out_proj(name, y, w_out, h, next_row):
        row_spec = _spec((th, D), lambda i: (i, 0))
        return _mm(name, [(y, row_spec, w_out.reshape(D, D), _spec((D, D), lambda i: (0, 0)))],
                   grid=(t // th,), o_shape=(t, D), o_dtype=F32, o_spec=row_spec, dims=NN, kaxis=0, nk=1,
                   res=(h, row_spec), norm=(small, next_row))

    def out_proj_bwd(tag, dhb, y, w_out, blk, grad_b, make_rider=None):
        grad_b = _wgrad(tag + "_dwout", y, _spec((tw, 256), lambda s, k: (k, s)), dhb, _spec((tw, D), lambda s, k: (k, 0)),
                        256, D, t, tw, grad_b, blk)
        rider = make_rider(grad_b) if make_rider is not None else None
        dy = _mm(tag + "_dy", [(dhb, tok_is, w_out, _spec((None, 256, D), lambda i, s: (s, 0, 0)))],
                 grid=(nt, NSH), o_shape=(t, D), o_dtype=F32, o_spec=_spec((tm, 256), lambda i, s: (i, s)),
                 dims=NT, kaxis=1, nk=1, carry=rider)
        dy, landed = dy if rider is not None else (dy, None)
        return dy, grad_b, landed

    h0 = x
    names, gather = carried("pre0_norm")
    hn, got = _norm_fwd("pre0_norm", h0, small, R_PRE, gather)
    land(names, got)
    h1, hn_ab, pre0 = ffn_forward("pre0", h0, hn, R_MIX)
    def carrying(kernel_name, make):
        names, gather = carried(kernel_name)
        out = make(gather)
        if gather is not None:
            out, got = out
            land(names, got)
        return out

    proj_a = carrying("ab_proj_a", lambda gather: _mm(
        "ab_proj_a", [(hn_ab, tok_is, w["ab_in"], _spec((None, D, 768), lambda i, s: (s, 0, 0)))],
        grid=(nt, 2), o_shape=(t, 1536), o_dtype=F32, o_spec=_spec((tm, 768), lambda i, s: (i, s)),
        dims=NN, kaxis=1, nk=1, carry=gather))
    proj_b = _mm("ab_proj_b", [(hn_ab, tok_is, w["ab_in"], _spec((None, D, 768), lambda i, s: (s + 2, 0, 0)))],
                 grid=(nt, 2), o_shape=(t, 1536), o_dtype=BF16, o_spec=_spec((tm, 768), lambda i, s: (i, s)),
                 dims=NN, kaxis=1, nk=1)
    y_a = _conv_fwd(proj_a, w["conv"])
    names, gather = carried("sb_fwd")
    y_b, ltot, nblk, got = _sb_fwd(proj_b, gather)
    land(names, got)
    y_ab = jnp.concatenate([y_a, y_b], axis=1)
    h2, hn = out_proj("ab_out", y_ab, w["ab_out"], h1, R_POST)
    h3, hn, post0 = ffn_forward("post0", h2, hn, R_PRE + 1)
    h4, hn_c, pre1 = ffn_forward("pre1", h3, hn, R_MIX + 1)
    proj_c = carrying("c_proj", lambda gather: _mm(
        "c_proj", [(hn_c, tok_si, w["c_in"], _spec((None, D, D), lambda s, i: (s, 0, 0)))],
        grid=(NSH, nt), o_shape=(NSH, t, D), o_dtype=F32, o_spec=_spec((None, tm, D), lambda s, i: (s, i, 0)),
        dims=NN, kaxis=1, nk=1, carry=gather))
    names, gather = carried("hgrn_fwd")
    o_c, y_c, sst, got = _hgrn_fwd(proj_c, small, gather)
    land(names, got)
    h5, hn = out_proj("c_out", y_c, w["c_out"], h4, R_POST + 1)
    h6, _, post1 = ffn_forward("post1", h5, hn, None)
    dh, dhb, d_fin, loss = _final_loss(h6, small, target)

    dh, dhb, grad_a, grad_b, dn_post1, _, _ = _ffn_backward("post1", dh, dhb, *post1, w["post1_gu"], w["post1_d"],
                                                            *FFN_SLOTS["post1"], small, R_POST + 1, 8 * FS, B_ROWS)
    dy_c, grad_b, swapped = out_proj_bwd("c", dhb, y_c, w["c_out"], B_COUT // 256, grad_b,
                                         lambda buf_b: swap_rider("x", dict(a=grad_a, b=buf_b)))
    pairs, rider = reduce_start("x", dict(a=grad_a, b=grad_b), swapped)
    dproj_c, d_clb, d_gam, landed = _hgrn_bwd(proj_c, small, o_c, sst, dy_c, rider)
    share = reduce_end("x", pairs, landed)
    grad_b = _wgrad("c_dwin", hn_c, _spec((tw, D), lambda s, k: (k, 0)), dproj_c, _spec((None, tw, D), lambda s, k: (s, k, 0)),
                    D, D, t, tw, grad_b, B_CIN // D, carry=share)
    if share is not None:
        grad_b, landed = grad_b
        shared("x", landed)
    dh, dhb, dn_mix1, _ = _dhn_norm("c_dhn", [(dproj_c, _spec((None, th, D), lambda i, s=s: (s, i, 0)),
                                               w["c_in"], _spec((None, D, D), lambda i, s=s: (s, 0, 0)))
                                              for s in range(NSH)],
                                    NT, h4, small, R_MIX + 1, dh)
    stage_w = {}

    def w_swap(buf_a, buf_b):
        return swap_rider("w", dict(a=buf_a, b=buf_b))

    def w_send(buf_a, buf_b, swapped):
        stage_w["pairs"], rider = reduce_start("w", dict(a=buf_a, b=buf_b), swapped)
        return rider

    on = place is not None
    dh, dhb, grad_a, grad_b, dn_pre1, _, landed = _ffn_backward(
        "pre1", dh, dhb, *pre1, w["pre1_gu"], w["pre1_d"], *FFN_SLOTS["pre1"], small, R_PRE + 1, grad_a, grad_b,
        w_swap if on else None, w_send if on else None)
    share = reduce_end("w", stage_w.get("pairs"), landed)
    dh, dhb, grad_a, grad_b, dn_post0, landed, _ = _ffn_backward(
        "post0", dh, dhb, *post0, w["post0_gu"], w["post0_d"], *FFN_SLOTS["post0"], small, R_POST, grad_a, grad_b,
        (lambda buf_a, buf_b: share) if on else None)
    shared("w", landed)
    dy_ab, grad_b, _ = out_proj_bwd("ab", dhb, y_ab, w["ab_out"], B_ABOUT // 256, grad_b)
    dab, dac, dax, d_conv, swapped = _conv_bwd(proj_a, w["conv"], dy_ab, swap_rider("y", dict(a=grad_a, b=grad_b)))
    pairs, rider = reduce_start("y", dict(a=grad_a, b=grad_b), swapped)
    dq, dk, dv, landed = _sb_bwd(proj_b, dy_ab, ltot, nblk, rider)
    share = reduce_end("y", pairs, landed)
    dproj_ab = jnp.concatenate([dab, dac, dax, dq, dk, dv], axis=1)
    grad_c = _wgrad("ab_dwin", hn_ab, _spec((tw, D), lambda s, k: (k, 0)), dproj_ab, _spec((tw, 768), lambda s, k: (k, s)),
                    D, 768, t, tw, D, 0, carry=share)
    if share is not None:
        grad_c, landed = grad_c
        shared("y", landed)
    dh, dhb, dn_mix0, _ = _dhn_norm("ab_dhn", [(dproj_ab, _spec((th, 768), lambda i, s=s: (i, s)),
                                                w["ab_in"], _spec((None, D, 768), lambda i, s=s: (s, 0, 0)))
                                               for s in range(NSH)],
                                    NT, h1, small, R_MIX, dh)
    last = {}

    def z1_send(buf_a, buf_b):
        last["z1"], rider = reduce_start("z1", dict(b=buf_b, c=grad_c))
        return rider

    def z2_send(buf_a, buf_b, _):
        last["z2"], rider = reduce_start("z2", dict(a=buf_a, b=buf_b))
        return rider

    dh, dhb, grad_a, grad_b, dn_pre0, landed, landed_own = _ffn_backward(
        "pre0", dh, dhb, *pre0, w["pre0_gu"], w["pre0_d"], *FFN_SLOTS["pre0"], small, R_PRE, grad_a, grad_b,
        z1_send if on else None, z2_send if on else None)
    if place is not None:
        mine = (_owner_sum("grad_owner_sum_z1", last["z1"], landed, place[1])
                + _owner_sum("grad_owner_sum_z2", last["z2"], landed_own, place[1]))
        reduced.update(zip(REDUCE_STAGES["z1"] + REDUCE_STAGES["z2"], _alone("grad_share_z", _Share(mine))))
    zero = jnp.zeros((1, D), F32)
    conv_rows = jnp.pad(jnp.transpose(d_conv[:, :3, :], (1, 0, 2)).reshape(3, 512), ((0, 0), (0, D - 512)))
    small_grad = jnp.concatenate([
        dn_pre0, dn_pre1, dn_mix0, dn_mix1, dn_post0, dn_post1, d_clb, d_fin,
        jnp.pad(d_gam, ((0, 0), (0, D - HD))), conv_rows,
        jnp.pad(loss, ((0, 0), (0, D - 1))), zero, zero], axis=0)
    return dh, grad_a, grad_b, grad_c, small_grad, reduced


def _small_slab(rows):
    parts = [jnp.pad(r.astype(F32), ((0, 0), (0, D - r.shape[1]))) for r in rows]
    slab = jnp.concatenate(parts, axis=0)
    return jnp.pad(slab, ((0, SMALL_ROWS - slab.shape[0]), (0, 0)))


def kernel(x, ffn_pre_norm, ffn_pre_w_gate, ffn_pre_w_up, ffn_pre_w_down, mix_norm, ffn_post_norm, ffn_post_w_gate, ffn_post_w_up, ffn_post_w_down, ab_w_in, ab_conv_w, ab_w_out, c_w_in, c_lower_bounds, c_out_norm, c_w_out, final_norm, loss_target, m_ffn_pre_norm, m_ffn_pre_w_gate, m_ffn_pre_w_up, m_ffn_pre_w_down, m_mix_norm, m_ffn_post_norm, m_ffn_post_w_gate, m_ffn_post_w_up, m_ffn_post_w_down, m_ab_w_in, m_ab_conv_w, m_ab_w_out, m_c_w_in, m_c_lower_bounds, m_c_out_norm, m_c_w_out, m_final_norm, v_ffn_pre_norm, v_ffn_pre_w_gate, v_ffn_pre_w_up, v_ffn_pre_w_down, v_mix_norm, v_ffn_post_norm, v_ffn_post_w_gate, v_ffn_post_w_up, v_ffn_post_w_down, v_ab_w_in, v_ab_conv_w, v_ab_w_out, v_c_w_in, v_c_lower_bounds, v_c_out_norm, v_c_w_out, v_final_norm):
    t = x.shape[1]
    xi, yi, ci = lax.axis_index("x"), lax.axis_index("y"), lax.axis_index("c")
    p_idx = (2 * xi + yi).astype(jnp.int32).reshape(1)
    c_idx = ci.astype(jnp.int32).reshape(1)

    def halves(m):
        return m.astype(BF16).reshape(2, m.shape[0] // 2, m.shape[1])

    transposed = ("ffn_pre_w_gate", "ffn_pre_w_up", "ffn_post_w_gate", "ffn_post_w_up")

    def flip(a):
        return jnp.swapaxes(a, 1, 2)

    shards = {}
    for name, (w_gate, w_up, w_down, layer) in dict(
            pre0=(ffn_pre_w_gate, ffn_pre_w_up, ffn_pre_w_down, 0), post0=(ffn_post_w_gate, ffn_post_w_up, ffn_post_w_down, 0),
            pre1=(ffn_pre_w_gate, ffn_pre_w_up, ffn_pre_w_down, 1), post1=(ffn_post_w_gate, ffn_post_w_up, ffn_post_w_down, 1)).items():
        shards[name + "_gu"] = jnp.stack([flip(w_gate)[layer], flip(w_up)[layer]]).astype(BF16)
        shards[name + "_d"] = halves(w_down[layer])
    conv_pad = jnp.pad(ab_conv_w[0], ((0, 5), (0, 0)))
    shards.update(ab_in=halves(ab_w_in[0]), ab_out=halves(ab_w_out[0]), c_in=halves(c_w_in[0]), c_out=halves(c_w_out[0]),
                  conv=jnp.stack([conv_pad, jnp.zeros_like(conv_pad)]))
    small = _small_slab([ffn_pre_norm, mix_norm, ffn_post_norm, c_lower_bounds, final_norm.reshape(1, D), c_out_norm])
    small = small.reshape(SMALL_ROWS, 1, D)

    grad_x, _, _, _, small_grad, reduced = _local_step(x[0], loss_target[0], {}, small, shards,
                                                        (c_idx, p_idx))
    whole = {n: g.reshape(2 * g.shape[1], g.shape[2]) for n, g in reduced.items()}
    small_sum = _sum_small(small_grad)

    my_conv = lax.dynamic_slice(small_sum[R_CONV:R_CONV + 3], (0, (2 * xi + yi) * 128), (3, 128))
    grads = {
        "ffn_pre_norm": small_sum[R_PRE:R_PRE + 2], "mix_norm": small_sum[R_MIX:R_MIX + 2],
        "ffn_post_norm": small_sum[R_POST:R_POST + 2], "c_lower_bounds": small_sum[R_CLB:R_CLB + 2],
        "c_out_norm": small_sum[R_GAM:R_GAM + 1, :HD], "final_norm": small_sum[R_FIN],
        "ab_conv_w": my_conv.reshape(1, 3, 128),
    }
    layers = dict(ab_w_in=[whole["ab_in"]], ab_w_out=[whole["ab_out"]], c_w_in=[whole["c_in"]], c_w_out=[whole["c_out"]])
    for kind, key in (("gate", "_g"), ("up", "_u"), ("down", "_d")):
        layers["ffn_pre_w_" + kind] = [whole["pre0" + key], whole["pre1" + key]]
        layers["ffn_post_w_" + kind] = [whole["post0" + key], whole["post1" + key]]
    weights = dict(ffn_pre_norm=ffn_pre_norm, ffn_pre_w_gate=ffn_pre_w_gate, ffn_pre_w_up=ffn_pre_w_up, ffn_pre_w_down=ffn_pre_w_down, mix_norm=mix_norm, ffn_post_norm=ffn_post_norm, ffn_post_w_gate=ffn_post_w_gate, ffn_post_w_up=ffn_post_w_up, ffn_post_w_down=ffn_post_w_down, ab_w_in=ab_w_in, ab_conv_w=ab_conv_w, ab_w_out=ab_w_out, c_w_in=c_w_in, c_lower_bounds=c_lower_bounds, c_out_norm=c_out_norm, c_w_out=c_w_out, final_norm=final_norm)
    m_in = dict(ffn_pre_norm=m_ffn_pre_norm, ffn_pre_w_gate=m_ffn_pre_w_gate, ffn_pre_w_up=m_ffn_pre_w_up, ffn_pre_w_down=m_ffn_pre_w_down, mix_norm=m_mix_norm, ffn_post_norm=m_ffn_post_norm, ffn_post_w_gate=m_ffn_post_w_gate, ffn_post_w_up=m_ffn_post_w_up, ffn_post_w_down=m_ffn_post_w_down, ab_w_in=m_ab_w_in, ab_conv_w=m_ab_conv_w, ab_w_out=m_ab_w_out, c_w_in=m_c_w_in, c_lower_bounds=m_c_lower_bounds, c_out_norm=m_c_out_norm, c_w_out=m_c_w_out, final_norm=m_final_norm)
    v_in = dict(ffn_pre_norm=v_ffn_pre_norm, ffn_pre_w_gate=v_ffn_pre_w_gate, ffn_pre_w_up=v_ffn_pre_w_up, ffn_pre_w_down=v_ffn_pre_w_down, mix_norm=v_mix_norm, ffn_post_norm=v_ffn_post_norm, ffn_post_w_gate=v_ffn_post_w_gate, ffn_post_w_up=v_ffn_post_w_up, ffn_post_w_down=v_ffn_post_w_down, ab_w_in=v_ab_w_in, ab_conv_w=v_ab_conv_w, ab_w_out=v_ab_w_out, c_w_in=v_c_w_in, c_lower_bounds=v_c_lower_bounds, c_out_norm=v_c_out_norm, c_w_out=v_c_w_out, final_norm=v_final_norm)
    names = list(weights)
    big = [n for n in names if weights[n].size >= 65536]
    tiny = [n for n in names if n not in big]

    delta, new_m, new_v = {}, {}, {}
    for n in big:
        turn = flip if n in transposed else (lambda a: a)
        shape = turn(weights[n]).shape
        two_d = (shape[0] * shape[1], shape[2])
        d, m2, v2, g2 = _adamw("adamw_" + n, turn(weights[n]).reshape(two_d), layers[n],
                               turn(m_in[n]).reshape(two_d), turn(v_in[n]).reshape(two_d))
        delta[n], new_m[n], new_v[n] = turn(d.reshape(shape)), turn(m2.reshape(shape)), turn(v2.reshape(shape))
        grads[n] = turn(g2.reshape(shape))

    def tiny_slab(src):
        return _small_slab([src[n].reshape(-1, src[n].shape[-1]) for n in tiny])

    offs, row = {}, 0
    for n in tiny:
        nrows = weights[n].size // weights[n].shape[-1]
        offs[n] = (row, nrows)
        row += nrows
    d, m2, v2, _ = _adamw("adamw_small", tiny_slab(weights), [tiny_slab(grads)], tiny_slab(m_in), tiny_slab(v_in))
    for n in tiny:
        r0, nr = offs[n]
        shape = weights[n].shape
        for dst, src in ((delta, d), (new_m, m2), (new_v, v2)):
            dst[n] = src[r0:r0 + nr, :shape[-1]].reshape(shape)

    loss = small_sum[R_LOSS, 0]
    return (loss, grad_x.reshape(1, t, D), *[grads[n] for n in names], *[delta[n] for n in names],
            *[new_m[n] for n in names], *[new_v[n] for n in names])
```

```python
import functools
import math

import jax
import jax.numpy as jnp
from jax import lax
from jax.experimental import pallas as pl
from jax.experimental.pallas import tpu as pltpu

F32 = jnp.float32
BF16 = jnp.bfloat16
MESH = pl.DeviceIdType.MESH
ANY = pl.BlockSpec(memory_space=pl.ANY)

D = 1024
FS = 704
NSH = 4
RMS_EPS = 1e-6
MACARON = 0.5
CHUNK = 64
HD = 128
HGRN_HPS = 8
SBQ = 128
SB_PAIRS_FWD = 4
SB_PAIRS_BWD = 4
SB_DEAD = -105.0
CONV_HALO = 8
LANE = 128
ROW_TILE = 1024
MM_TILE = 1024
HALF_TILE = MM_TILE // 2
WGRAD_TILE = 4096
VMEM_LIMIT = 48 * 1024 * 1024
VMEM_LIMIT_BIG = 58 * 1024 * 1024

ADAM_LR, ADAM_B1, ADAM_B2, ADAM_EPS, ADAM_WD, ADAM_STEP = 0.001, 0.9, 0.999, 1e-08, 0.01, 10

NN = ((1,), (0,))
NT = ((1,), (1,))
TN = ((0,), (0,))

SMALL_ROWS = 16
R_PRE, R_MIX, R_POST, R_CLB, R_FIN, R_GAM, R_CONV, R_LOSS = 0, 2, 4, 6, 8, 9, 10, 13

B_ABOUT = 4 * FS
B_CIN = B_ABOUT + 256
B_COUT = B_CIN + 1024
B_ROWS = B_COUT + 256


def _dg(a, b, dims):
    return lax.dot_general(a, b, (dims, ((), ())), preferred_element_type=F32)


def _split(x):
    hi = x.astype(BF16)
    lo = (x - hi.astype(F32)).astype(BF16)
    return hi, lo


def _dot3(a, b, dims):
    ah, al = _split(a)
    bh, bl = _split(b)
    if dims == TN:
        n = b.shape[1]
        both = _dg(ah, jnp.concatenate([bh, bl], axis=1), dims)
        return both[:, :n] + both[:, n:] + _dg(al, bh, dims)
    m = a.shape[0]
    both = _dg(jnp.concatenate([ah, al], axis=0), bh, dims)
    return both[:m] + both[m:] + _dg(ah, bl, dims)


def _sigmoid(x):
    return 1.0 / (1.0 + jnp.exp(-x))


def _params(sem):
    return pltpu.CompilerParams(dimension_semantics=sem, vmem_limit_bytes=VMEM_LIMIT)


def _spec(shape, imap):
    return pl.BlockSpec(shape, imap)


def _accumulate(acc_ref, pairs, dims):
    part = None
    for a_ref, b_ref in pairs:
        d = _dg(a_ref[...], b_ref[...], dims)
        part = d if part is None else part + d
    acc_ref[...] += part


def _mm(name, pairs, *, grid, o_shape, o_dtype, o_spec, dims, kaxis, nk, acc_shape=None, res=None, scale=None,
        into=None, carry=None, norm=None):
    npairs = len(pairs)
    operands, specs = [], []
    for a, a_spec, b, b_spec in pairs:
        operands += [a, b]
        specs += [a_spec, b_spec]
    if res is not None:
        operands.append(res[0])
        specs.append(res[1])
    aliases = {}
    if into is not None:
        aliases = {len(operands): 0}
        operands.append(into)
        specs.append(ANY)
    if norm is not None:
        operands.append(norm[0])
        specs.append(_spec((None, 1, D), lambda *_: (norm[1], 0, 0)))
    n_own = len(operands)
    n_out = 2 if norm is not None else 1
    nc = carry.n if carry is not None else 0
    if nc:
        operands += carry.groups
        specs += carry.in_specs

    def body(*refs):
        o_ref = refs[n_own + nc]
        riders = ((refs[n_own:n_own + nc], refs[n_own + nc + n_out:n_own + 2 * nc + n_out], refs[-2], refs[-1])
                  if nc else None)
        if nc:
            carry.ride(grid, riders, "start")

        def finish(val):
            if scale is not None:
                val = val * scale
            if res is not None:
                val = val + refs[2 * npairs][...]
            o_ref[...] = val.astype(o_dtype)
            if norm is not None:
                r = lax.rsqrt(jnp.mean(val * val, axis=-1, keepdims=True) + RMS_EPS)
                refs[n_own + nc + 1][...] = (val * r * refs[n_own - 1][...]).astype(BF16)

        if nk == 1:
            part = None
            for n in range(npairs):
                d = _dg(refs[2 * n][...], refs[2 * n + 1][...], dims)
                part = d if part is None else part + d
            finish(part)
        else:
            acc_ref = refs[n_own + 2 * nc + n_out]
            k = pl.program_id(kaxis)

            @pl.when(k == 0)
            def _():
                acc_ref[...] = jnp.zeros_like(acc_ref)

            _accumulate(acc_ref, [(refs[2 * n], refs[2 * n + 1]) for n in range(npairs)], dims)

            @pl.when(k == nk - 1)
            def _():
                finish(acc_ref[...])
        if nc:
            carry.ride(grid, riders, "finish")

    sem = tuple("arbitrary" if (nc or (ax == kaxis and nk > 1)) else "parallel" for ax in range(len(grid)))
    outs = pl.pallas_call(
        body, name=name, grid=grid, in_specs=specs,
        out_specs=[o_spec] * n_out + (carry.out_specs if nc else []),
        out_shape=[jax.ShapeDtypeStruct(o_shape, o_dtype)] + ([jax.ShapeDtypeStruct(o_shape, BF16)] if norm is not None else [])
        + (carry.out_shape if nc else []),
        scratch_shapes=([pltpu.VMEM(acc_shape, F32)] if nk > 1 else []) + (carry.scratch if nc else []),
        input_output_aliases=aliases,
        compiler_params=_params(sem),
    )(*operands)
    main = (outs[0], outs[1]) if norm is not None else outs[0]
    return (main, carry.place(outs[n_out:])) if nc else main


def _norm_fwd(name, h, gain_slab, row, rider=None):
    t = h.shape[0]
    tm = min(ROW_TILE, t)
    grid = (t // tm,)
    nr = rider.n if rider is not None else 0

    def body(*refs):
        h_ref, g_ref, o_ref = refs[0], refs[1], refs[2 + nr]
        riders = (refs[2:2 + nr], refs[3 + nr:3 + 2 * nr], refs[-2], refs[-1]) if nr else None
        if nr:
            rider.ride(grid, riders, "start")
        x = h_ref[...]
        r = lax.rsqrt(jnp.mean(x * x, axis=-1, keepdims=True) + RMS_EPS)
        o_ref[...] = (x * r * g_ref[...]).astype(BF16)
        if nr:
            rider.ride(grid, riders, "finish")

    outs = pl.pallas_call(
        body, name=name, grid=grid,
        in_specs=[_spec((tm, D), lambda i: (i, 0)), _spec((None, 1, D), lambda i: (row, 0, 0))]
        + (rider.in_specs if nr else []),
        out_specs=[_spec((tm, D), lambda i: (i, 0))] + (rider.out_specs if nr else []),
        out_shape=[jax.ShapeDtypeStruct((t, D), BF16)] + (rider.out_shape if nr else []),
        scratch_shapes=rider.scratch if nr else [],
        compiler_params=_params(("arbitrary",) if nr else ("parallel",)),
    )(h, gain_slab, *(rider.groups if nr else []))
    return outs[0], (rider.place(outs[1:]) if nr else [])


def _dhn_norm(name, pairs, dims, h, gain_slab, row, dres, rider=None):
    t = h.shape[0]
    tm = min(HALF_TILE, t)
    nt = t // tm
    grid = (nt,)
    npairs = len(pairs)
    nr = rider.n if rider is not None else 0
    operands, specs = [], []
    for a, a_spec, b, b_spec in pairs:
        operands += [a, b]
        specs += [a_spec, b_spec]
    row_spec = _spec((tm, D), lambda i: (i, 0))
    operands += [h, gain_slab, dres]
    specs += [row_spec, _spec((None, 1, D), lambda i: (row, 0, 0)), row_spec]
    n_own = len(operands)

    def body(*refs):
        h_ref, g_ref, dres_ref = refs[2 * npairs:n_own]
        dh_ref, dhb_ref, dg_ref = refs[n_own + nr:n_own + nr + 3]
        gacc_ref = refs[n_own + 2 * nr + 3]
        riders = (refs[n_own:n_own + nr], refs[n_own + nr + 3:n_own + 2 * nr + 3], refs[-2], refs[-1]) if nr else None
        if nr:
            rider.ride(grid, riders, "start")
        i = pl.program_id(0)
        dy = None
        for n in range(npairs):
            d = _dg(refs[2 * n][...], refs[2 * n + 1][...], dims)
            dy = d if dy is None else dy + d
        x = h_ref[...]
        r = lax.rsqrt(jnp.mean(x * x, axis=-1, keepdims=True) + RMS_EPS)
        xh = x * r
        gdy = dy * g_ref[...]
        dh = dres_ref[...] + (gdy - xh * jnp.mean(gdy * xh, axis=-1, keepdims=True)) * r
        dh_ref[...] = dh
        dhb_ref[...] = dh.astype(BF16)
        gpart = jnp.sum((dy * xh).reshape(tm // 8, 8, D), axis=0)

        @pl.when(i == 0)
        def _():
            gacc_ref[...] = gpart

        @pl.when(i > 0)
        def _():
            gacc_ref[...] += gpart

        @pl.when(i == nt - 1)
        def _():
            dg_ref[...] = jnp.sum(gacc_ref[...], axis=0, keepdims=True)

        if nr:
            rider.ride(grid, riders, "finish")

    outs = pl.pallas_call(
        body, name=name, grid=grid, in_specs=specs + (rider.in_specs if nr else []),
        out_specs=[row_spec, row_spec, _spec((1, D), lambda i: (0, 0))] + (rider.out_specs if nr else []),
        out_shape=[jax.ShapeDtypeStruct((t, D), F32), jax.ShapeDtypeStruct((t, D), BF16),
                   jax.ShapeDtypeStruct((1, D), F32)] + (rider.out_shape if nr else []),
        scratch_shapes=[pltpu.VMEM((8, D), F32)] + (rider.scratch if nr else []),
        compiler_params=pltpu.CompilerParams(dimension_semantics=("arbitrary",),
                                             vmem_limit_bytes=VMEM_LIMIT_BIG),
    )(*operands, *(rider.groups if nr else []))
    return outs[0], outs[1], outs[2], (rider.place(outs[3:]) if nr else [])


def _final_loss(h, gain_slab, target):
    t = h.shape[0]
    tm = min(ROW_TILE, t)
    nt = t // tm

    def body(h_ref, g_ref, t_ref, dh_ref, dhb_ref, dg_ref, loss_ref, acc_ref, lacc_ref):
        i = pl.program_id(0)
        x = h_ref[...]
        g = g_ref[...]
        r = lax.rsqrt(jnp.mean(x * x, axis=-1, keepdims=True) + RMS_EPS)
        xh = x * r
        err = xh * g - t_ref[...]
        dy = err * (1.0 / D)
        gdy = dy * g
        dh = (gdy - xh * jnp.mean(gdy * xh, axis=-1, keepdims=True)) * r
        dh_ref[...] = dh
        dhb_ref[...] = dh.astype(BF16)
        part = jnp.sum((dy * xh).reshape(tm // 8, 8, D), axis=0)
        lpart = jnp.sum((err * err).reshape(tm // 8, 8, D), axis=0)

        @pl.when(i == 0)
        def _():
            acc_ref[...] = part
            lacc_ref[...] = lpart

        @pl.when(i > 0)
        def _():
            acc_ref[...] += part
            lacc_ref[...] += lpart

        @pl.when(i == nt - 1)
        def _():
            dg_ref[...] = jnp.sum(acc_ref[...], axis=0, keepdims=True)
            rows = jnp.sum(lacc_ref[...], axis=0, keepdims=True)
            loss_ref[...] = jnp.sum(rows, axis=1, keepdims=True) * (0.5 / D)

    row_spec = _spec((tm, D), lambda i: (i, 0))
    return pl.pallas_call(
        body, name="final_loss", grid=(nt,),
        in_specs=[row_spec, _spec((None, 1, D), lambda i: (R_FIN, 0, 0)), row_spec],
        out_specs=[row_spec, row_spec, _spec((1, D), lambda i: (0, 0)), _spec((1, 1), lambda i: (0, 0))],
        out_shape=[jax.ShapeDtypeStruct((t, D), F32), jax.ShapeDtypeStruct((t, D), BF16),
                   jax.ShapeDtypeStruct((1, D), F32), jax.ShapeDtypeStruct((1, 1), F32)],
        scratch_shapes=[pltpu.VMEM((8, D), F32), pltpu.VMEM((8, D), F32)],
        compiler_params=_params(("arbitrary",)),
    )(h, gain_slab, target)


def _ffn_up(name, hn, wgu, carry=None):
    t = hn.shape[0]
    tm = min(MM_TILE, t)
    grid = (NSH, t // tm)
    nc = carry.n if carry is not None else 0

    def body(*refs):
        x_ref, wg_ref, wu_ref = refs[:3]
        s_up_ref, s_gate_ref, a_ref = refs[3 + nc:6 + nc]
        riders = (refs[3:3 + nc], refs[6 + nc:6 + 2 * nc], refs[-2], refs[-1]) if nc else None
        if nc:
            carry.ride(grid, riders, "start")
        x = x_ref[...]
        g = _dg(x, wg_ref[...], NT)
        u = _dg(x, wu_ref[...], NT)
        sg = _sigmoid(g)
        silu = g * sg
        s_up_ref[...] = (MACARON * silu).astype(BF16)
        s_gate_ref[...] = (MACARON * u * (sg * (1.0 + g * (1.0 - sg)))).astype(BF16)
        a_ref[...] = (silu * u).astype(BF16)
        if nc:
            carry.ride(grid, riders, "finish")

    act = _spec((None, tm, FS), lambda s, i: (s, i, 0))
    shape = jax.ShapeDtypeStruct((NSH, t, FS), BF16)
    outs = pl.pallas_call(
        body, name=name, grid=grid,
        in_specs=[_spec((tm, D), lambda s, i: (i, 0)),
                  _spec((None, None, FS, D), lambda s, i: (s, 0, 0, 0)),
                  _spec((None, None, FS, D), lambda s, i: (s, 1, 0, 0))] + (carry.in_specs if nc else []),
        out_specs=[act, act, act] + (carry.out_specs if nc else []),
        out_shape=[shape, shape, shape] + (carry.out_shape if nc else []),
        scratch_shapes=carry.scratch if nc else [],
        compiler_params=_params(("arbitrary", "arbitrary") if nc else ("parallel", "parallel")),
    )(hn, wgu, wgu, *(carry.groups if nc else []))
    return (outs[0], outs[1], outs[2], carry.place(outs[3:]) if nc else [])


def _ffn_down(name, a, wd, h, carry=None, norm=None):
    t = h.shape[0]
    tm = min(HALF_TILE, t)
    row_spec = _spec((tm, D), lambda i: (i, 0))
    return _mm(name, [(a, _spec((None, tm, FS), lambda i, s=s: (s, i, 0)), wd, _spec((None, FS, D), lambda i, s=s: (s, 0, 0)))
                      for s in range(NSH)],
               grid=(t // tm,), o_shape=(t, D), o_dtype=F32, o_spec=row_spec, dims=NN, kaxis=0, nk=1,
               res=(h, row_spec), scale=MACARON, carry=carry, norm=norm)


def _ffn_bwd_up(name, dhb, wd, s_up, s_gate, rider=None):
    t = dhb.shape[0]
    tm = min(2 * MM_TILE, t)
    grid = (t // tm, NSH)
    nr = rider.n if rider is not None else 0

    def body(*refs):
        dh_ref, wd_ref, s_up_ref, s_gate_ref = refs[:4]
        dg_ref, du_ref = refs[4 + nr:6 + nr]
        riders = (refs[4:4 + nr], refs[6 + nr:6 + 2 * nr], refs[-2], refs[-1]) if nr else None
        if nr:
            rider.ride(grid, riders, "start")
        da = _dg(dh_ref[...], wd_ref[...], NT).astype(BF16)
        du_ref[...] = da * s_up_ref[...]
        dg_ref[...] = da * s_gate_ref[...]
        if nr:
            rider.ride(grid, riders, "finish")

    act = _spec((None, tm, FS), lambda i, s: (s, i, 0))
    shape = jax.ShapeDtypeStruct((NSH, t, FS), BF16)
    outs = pl.pallas_call(
        body, name=name, grid=grid,
        in_specs=[_spec((tm, D), lambda i, s: (i, 0)), _spec((None, FS, D), lambda i, s: (s, 0, 0)), act, act]
        + (rider.in_specs if nr else []),
        out_specs=[act, act] + (rider.out_specs if nr else []),
        out_shape=[shape, shape] + (rider.out_shape if nr else []),
        scratch_shapes=rider.scratch if nr else [],
        compiler_params=pltpu.CompilerParams(
            dimension_semantics=("arbitrary", "arbitrary") if nr else ("parallel", "parallel"),
            vmem_limit_bytes=VMEM_LIMIT_BIG),
    )(dhb, wd, s_up, s_gate, *(rider.groups if nr else []))
    return outs[0], outs[1], (rider.place(outs[2:]) if nr else [])


def _wgrad(name, a, a_spec, b, b_spec, out_rows, out_cols, t, tt, group, slot, scale=None, carry=None):
    first = isinstance(group, int)
    rows = group if first else group.shape[1]
    return _mm(name, [(a, a_spec, b, b_spec)], grid=(NSH, t // tt),
               o_shape=(NSH, rows, out_cols), o_dtype=BF16,
               o_spec=_spec((None, out_rows, out_cols), lambda s, k: (s, slot, 0)),
               dims=TN, kaxis=1, nk=t // tt, acc_shape=(out_rows, out_cols), scale=scale,
               into=None if first else group, carry=carry)


def _wgrad_gate_up(name, dg, du, hn, group, gate_idx):
    t = hn.shape[0]
    first = isinstance(group, int)
    rows = group if first else group.shape[1]

    def body(dg_ref, du_ref, hn_ref, *refs):
        o_ref = refs[-1]
        x = hn_ref[...]
        o_ref[...] = jnp.concatenate([_dg(dg_ref[...], x, TN), _dg(du_ref[...], x, TN)], axis=0).astype(BF16)

    hid = _spec((None, t, FS), lambda s: (s, 0, 0))
    return pl.pallas_call(
        body, name=name, grid=(NSH,),
        in_specs=[hid, hid, _spec((t, D), lambda s: (0, 0))] + ([] if first else [ANY]),
        out_specs=_spec((None, 2 * FS, D), lambda s: (s, gate_idx // 2, 0)),
        out_shape=jax.ShapeDtypeStruct((NSH, rows, D), BF16),
        input_output_aliases={} if first else {3: 0},
        compiler_params=pltpu.CompilerParams(dimension_semantics=("parallel",), vmem_limit_bytes=VMEM_LIMIT_BIG),
    )(dg, du, hn, *([] if first else [group]))


def _ffn_backward(tag, dh, dhb, h_in, hn, s_up, s_gate, a, wgu, wd, gate_idx, up_idx, down_idx, small, norm_row,
                  grad_a, grad_b, rider_up=None, rider_dhn=None):
    t = dh.shape[0]
    tm = min(MM_TILE, t)
    tt = min(WGRAD_TILE, t)
    tok = _spec((tt, D), lambda s, k: (k, 0))
    hid = _spec((None, tt, FS), lambda s, k: (s, k, 0))
    grad_b = _wgrad(tag + "_dwd", a, hid, dhb, tok, FS, D, t, tt, grad_b, down_idx, scale=MACARON)
    rider = rider_up(grad_a, grad_b) if rider_up is not None else None
    dg, du, landed_up = _ffn_bwd_up(tag + "_bwd_up", dhb, wd, s_up, s_gate, rider)
    assert up_idx == gate_idx + 1 and gate_idx % 2 == 0
    grad_a = _wgrad_gate_up(tag + "_dwgu", dg, du, hn, grad_a, gate_idx)
    rider = rider_dhn(grad_a, grad_b, landed_up) if rider_dhn is not None else None
    th = min(HALF_TILE, t)
    pairs = []
    for s in range(NSH):
        act = _spec((None, th, FS), lambda i, s=s: (s, i, 0))
        pairs += [(dg, act, wgu, _spec((None, None, FS, D), lambda i, s=s: (s, 0, 0, 0))),
                  (du, act, wgu, _spec((None, None, FS, D), lambda i, s=s: (s, 1, 0, 0)))]
    dh_in, dhb_in, d_gain, landed_dhn = _dhn_norm(tag + "_dhn", pairs, NN, h_in, small, norm_row, dh, rider)
    return dh_in, dhb_in, grad_a, grad_b, d_gain, landed_up, landed_dhn


def _conv_fwd(proj_a, conv_w):
    t = proj_a.shape[0]
    tm = min(ROW_TILE, t)
    hb = tm // CONV_HALO

    def body(ab_ref, ac_ref, ax_ref, acp_ref, axp_ref, w_ref, y_ref):
        i = pl.program_id(1)
        u = ac_ref[...] * ax_ref[...]
        up = jnp.where(i > 0, acp_ref[...] * axp_ref[...], 0.0)
        ext = jnp.concatenate([up, u], axis=0)
        u1 = pltpu.roll(ext, 1, 0)[CONV_HALO:]
        u2 = pltpu.roll(ext, 2, 0)[CONV_HALO:]
        w = w_ref[...]
        conv = w[0:1] * u2 + w[1:2] * u1 + w[2:3] * u
        y_ref[...] = (ab_ref[...] * conv).astype(BF16)

    def cur(off):
        return _spec((tm, LANE), lambda j, i: (i, off + j))

    def prev(off):
        return _spec((CONV_HALO, LANE), lambda j, i: (jnp.maximum(i * hb - 1, 0), off + j))

    return pl.pallas_call(
        body, name="conv_fwd", grid=(4, t // tm),
        in_specs=[cur(0), cur(4), cur(8), prev(4), prev(8),
                  _spec((None, None, 8, LANE), lambda j, i: (j, 0, 0, 0))],
        out_specs=_spec((tm, LANE), lambda j, i: (i, j)),
        out_shape=jax.ShapeDtypeStruct((t, 512), BF16),
        compiler_params=_params(("parallel", "parallel")),
    )(proj_a, proj_a, proj_a, proj_a, proj_a, conv_w)


def _conv_bwd(proj_a, conv_w, dy, rider=None):
    t = proj_a.shape[0]
    tm = min(ROW_TILE, t)
    hb = tm // CONV_HALO
    nt = t // tm
    grid = (4, nt)
    nr = rider.n if rider is not None else 0

    def body(*refs):
        ab_ref, ac_ref, ax_ref, dy_ref, acp_ref, axp_ref, abn_ref, dyn_ref, w_ref = refs[:9]
        dab_ref, dac_ref, dax_ref, dw_ref = refs[9 + nr:13 + nr]
        acc_ref = refs[13 + 2 * nr]
        riders = (refs[9:9 + nr], refs[13 + nr:13 + 2 * nr], refs[-2], refs[-1]) if nr else None
        if nr:
            rider.ride(grid, riders, "start")
        i = pl.program_id(1)
        ab, ac, ax = ab_ref[...], ac_ref[...], ax_ref[...]
        u = ac * ax
        up = jnp.where(i > 0, acp_ref[...] * axp_ref[...], 0.0)
        ext = jnp.concatenate([up, u], axis=0)
        u1 = pltpu.roll(ext, 1, 0)[CONV_HALO:]
        u2 = pltpu.roll(ext, 2, 0)[CONV_HALO:]
        w = w_ref[...]
        conv = w[0:1] * u2 + w[1:2] * u1 + w[2:3] * u
        dy_v = dy_ref[...]
        dab_ref[...] = (dy_v * conv).astype(BF16)
        dc = dy_v * ab
        dcn = jnp.where(i < nt - 1, dyn_ref[...] * abn_ref[...], 0.0)
        extn = jnp.concatenate([dc, dcn], axis=0)
        n = tm + CONV_HALO
        dc1 = pltpu.roll(extn, n - 1, 0)[:tm]
        dc2 = pltpu.roll(extn, n - 2, 0)[:tm]
        du = w[2:3] * dc + w[1:2] * dc1 + w[0:1] * dc2
        dac_ref[...] = (du * ax).astype(BF16)
        dax_ref[...] = (du * ac).astype(BF16)
        rid = lax.broadcasted_iota(jnp.int32, (8, LANE), 0)
        part = jnp.where(rid == 0, jnp.sum(dc * u2, axis=0, keepdims=True),
                         jnp.where(rid == 1, jnp.sum(dc * u1, axis=0, keepdims=True),
                                   jnp.where(rid == 2, jnp.sum(dc * u, axis=0, keepdims=True), 0.0)))

        @pl.when(i == 0)
        def _():
            acc_ref[...] = part

        @pl.when(i > 0)
        def _():
            acc_ref[...] += part

        @pl.when(i == nt - 1)
        def _():
            dw_ref[...] = acc_ref[...]

        if nr:
            rider.ride(grid, riders, "finish")

    def cur(off):
        return _spec((tm, LANE), lambda j, i: (i, off + j))

    def prev(off):
        return _spec((CONV_HALO, LANE), lambda j, i: (jnp.maximum(i * hb - 1, 0), off + j))

    def nxt(off):
        return _spec((CONV_HALO, LANE), lambda j, i: (jnp.minimum((i + 1) * hb, nt * hb - 1), off + j))

    outs = pl.pallas_call(
        body, name="conv_bwd", grid=grid,
        in_specs=[cur(0), cur(4), cur(8), cur(0), prev(4), prev(8), nxt(0), nxt(0),
                  _spec((None, None, 8, LANE), lambda j, i: (j, 0, 0, 0))] + (rider.in_specs if nr else []),
        out_specs=[_spec((tm, LANE), lambda j, i: (i, j)), _spec((tm, LANE), lambda j, i: (i, j)),
                   _spec((tm, LANE), lambda j, i: (i, j)), _spec((None, 8, LANE), lambda j, i: (j, 0, 0))]
        + (rider.out_specs if nr else []),
        out_shape=[jax.ShapeDtypeStruct((t, 512), BF16), jax.ShapeDtypeStruct((t, 512), BF16),
                   jax.ShapeDtypeStruct((t, 512), BF16), jax.ShapeDtypeStruct((4, 8, LANE), F32)]
        + (rider.out_shape if nr else []),
        scratch_shapes=[pltpu.VMEM((8, LANE), F32)] + (rider.scratch if nr else []),
        compiler_params=_params(("arbitrary", "arbitrary") if nr else ("parallel", "arbitrary")),
    )(proj_a, proj_a, proj_a, dy, proj_a, proj_a, proj_a, dy, conv_w, *(rider.groups if nr else []))
    return outs[0], outs[1], outs[2], outs[3], (rider.place(outs[4:]) if nr else [])


def _log_sigmoid(z):
    return jnp.minimum(z, 0.0) - jnp.log(1.0 + jnp.exp(-jnp.abs(z)))


def _sb_masks():
    row = lax.broadcasted_iota(jnp.int32, (SBQ, SBQ), 0)
    col = lax.broadcasted_iota(jnp.int32, (SBQ, SBQ), 1)
    return row, col


def _ones_where(mask):
    return jnp.where(mask, 1.0, 0.0).astype(BF16)


def _head_mask(head):
    lane = lax.broadcasted_iota(jnp.int32, (1, LANE), 1)
    return lane >= 64 if head else lane < 64


def _sb_fwd(proj_b, carry=None):
    t = proj_b.shape[0]
    nq = t // SBQ
    scale = 1.0 / math.sqrt(64.0)

    npair = SB_PAIRS_FWD
    wide = npair * LANE
    ngrp = 4 // npair
    chains = [(p, head) for p in range(npair) for head in range(2)]

    grid = (ngrp, nq)
    nc = carry.n if carry is not None else 0

    def body(*refs):
        q_ref, k_ref, v_ref = refs[:3]
        y_ref, l_ref, n_ref = refs[3 + nc:6 + nc]
        riders = (refs[3:3 + nc], refs[6 + nc:6 + 2 * nc], refs[-2], refs[-1]) if nc else None
        if nc:
            carry.ride(grid, riders, "start")
        grp = pl.program_id(0)
        qi = pl.program_id(1)
        row, col = _sb_masks()
        m_suffix = _ones_where(row > col)
        rows = len(chains) * SBQ
        strict = (lax.broadcasted_iota(jnp.int32, (rows, SBQ), 1)
                  < (lax.broadcasted_iota(jnp.int32, (rows, SBQ), 0) & (SBQ - 1)))
        q_pair = []
        for p in range(npair):
            q_all = q_ref[:, p * LANE:(p + 1) * LANE]
            q_pair.append(jnp.concatenate([jnp.where(_head_mask(head), q_all, jnp.zeros_like(q_all)) for head in range(2)],
                                          axis=0))

        def block(kb, state, diag):
            run, acc = state
            start = pl.multiple_of(kb * SBQ, SBQ)
            z = jnp.concatenate([_dg(q_pair[p], k_ref[pl.ds(start, SBQ), p * LANE:(p + 1) * LANE], NT)
                                 for p in range(npair)], axis=0) * scale
            lb = _log_sigmoid(z)
            lk = lb - z
            if diag:
                lk = jnp.where(strict, lk, 0.0)
            hi, lo = _split(lk)
            sums = _dg(jnp.concatenate([hi, lo], axis=0), m_suffix, NN)
            w = jnp.exp(lb + (run + sums[:rows] + sums[rows:]))
            if diag:
                w = jnp.where(strict, w, 0.0)
            wb = w.astype(BF16)
            acc = acc + jnp.concatenate(
                [_dg(wb[2 * p * SBQ:2 * (p + 1) * SBQ], v_ref[pl.ds(start, SBQ), p * LANE:(p + 1) * LANE], NN)
                 for p in range(npair)], axis=0)
            run = run + jnp.sum(hi.astype(F32) + lo.astype(F32), axis=1, keepdims=True)
            return run, acc

        state = block(qi, (jnp.zeros((rows, 1), F32), jnp.zeros((rows, LANE), F32)), True)

        def live(c):
            return jnp.logical_and(c[0] < qi, jnp.max(c[1][0]) > SB_DEAD)

        def step(c):
            return c[0] + 1, block(qi - 1 - c[0], c[1], False)

        count, (run, acc) = lax.while_loop(live, step, (jnp.int32(0), state))
        n_ref[grp * nq + qi] = count.astype(F32)
        hm = _head_mask(0)
        for p in range(npair):
            lo_rows, hi_rows = slice(2 * p * SBQ, (2 * p + 1) * SBQ), slice((2 * p + 1) * SBQ, (2 * p + 2) * SBQ)
            y_ref[:, p * LANE:(p + 1) * LANE] = jnp.where(hm, acc[lo_rows], acc[hi_rows]).astype(BF16)
            l_ref[p] = jnp.where(hm, run[lo_rows], run[hi_rows])
        if nc:
            carry.ride(grid, riders, "finish")

    outs = pl.pallas_call(
        body, name="sb_fwd", grid=grid,
        in_specs=[_spec((SBQ, wide), lambda g, i: (i, g)),
                  _spec((t, wide), lambda g, i: (0, ngrp + g)),
                  _spec((t, wide), lambda g, i: (0, 2 * ngrp + g))] + (carry.in_specs if nc else []),
        out_specs=[_spec((SBQ, wide), lambda g, i: (i, g)), _spec((npair, SBQ, LANE), lambda g, i: (g, i, 0)),
                   pl.BlockSpec(memory_space=pltpu.SMEM)] + (carry.out_specs if nc else []),
        out_shape=[jax.ShapeDtypeStruct((t, 512), BF16), jax.ShapeDtypeStruct((4, t, LANE), F32),
                   jax.ShapeDtypeStruct((ngrp * nq,), F32)] + (carry.out_shape if nc else []),
        scratch_shapes=carry.scratch if nc else [],
        compiler_params=_params(("arbitrary", "arbitrary")),
    )(proj_b, proj_b, proj_b, *(carry.groups if nc else []))
    return outs[0], outs[1], outs[2], (carry.place(outs[3:]) if nc else [])


def _sb_bwd(proj_b, dy, ltot, nblk, rider=None):
    t = proj_b.shape[0]
    nq = t // SBQ
    scale = 1.0 / math.sqrt(64.0)

    npair = SB_PAIRS_BWD
    wide = npair * LANE
    ngrp = 4 // npair
    chains = [(p, head) for p in range(npair) for head in range(2)]
    grid = (ngrp, nq)
    nr = rider.n if rider is not None else 0
    per_count = SB_PAIRS_FWD // SB_PAIRS_BWD

    def body(*refs):
        q_ref, k_ref, v_ref, dy_ref, l_ref, n_ref = refs[:6]
        dq_ref, dk_ref, dv_ref = refs[6 + nr:9 + nr]
        dk_acc, dv_acc = refs[9 + 2 * nr:11 + 2 * nr]
        riders = (refs[6:6 + nr], refs[9 + nr:9 + 2 * nr], refs[-2], refs[-1]) if nr else None
        if nr:
            rider.ride(grid, riders, "start")
        grp = pl.program_id(0)
        qi = pl.program_id(1)

        @pl.when(qi == 0)
        def _():
            dk_acc[...] = jnp.zeros_like(dk_acc)
            dv_acc[...] = jnp.zeros_like(dv_acc)

        row, col = _sb_masks()
        m_prefix = _ones_where(row <= col)
        m_before = _ones_where(row < col)
        rows = len(chains) * SBQ
        strict = (lax.broadcasted_iota(jnp.int32, (rows, SBQ), 1)
                  < (lax.broadcasted_iota(jnp.int32, (rows, SBQ), 0) & (SBQ - 1)))
        q_pair, do_pair, ltot = [], [], []
        for p in range(npair):
            pl_ = slice(p * LANE, (p + 1) * LANE)
            q_all = q_ref[:, pl_]
            do_all = dy_ref[:, pl_].astype(BF16)
            q_pair.append(jnp.concatenate([jnp.where(_head_mask(h), q_all, jnp.zeros_like(q_all)) for h in range(2)], axis=0))
            do_pair.append(jnp.concatenate([jnp.where(_head_mask(h), do_all, jnp.zeros_like(do_all)) for h in range(2)], axis=0))
            ltot += [l_ref[p][:, h * 64:h * 64 + 1] for h in range(2)]
        ltot = jnp.concatenate(ltot, axis=0)

        def pair_rows(a, p):
            return a[2 * p * SBQ:2 * (p + 1) * SBQ]

        def block(kb, state, diag):
            seen, dseen, dq = state
            start = pl.multiple_of(kb * SBQ, SBQ)
            kk = [k_ref[pl.ds(start, SBQ), p * LANE:(p + 1) * LANE] for p in range(npair)]
            vv = [v_ref[pl.ds(start, SBQ), p * LANE:(p + 1) * LANE] for p in range(npair)]
            z = jnp.concatenate([_dg(q_pair[p], kk[p], NT) for p in range(npair)], axis=0) * scale
            lb = _log_sigmoid(z)
            lk = lb - z
            if diag:
                lk = jnp.where(strict, lk, 0.0)
            hi, lo = _split(lk)
            sums = _dg(jnp.concatenate([hi, lo], axis=0), m_prefix, NN)
            w = jnp.exp(lb + ((ltot - seen) - (sums[:rows] + sums[rows:])))
            if diag:
                w = jnp.where(strict, w, 0.0)
            wb = w.astype(BF16)
            da = w * jnp.concatenate([_dg(do_pair[p], vv[p], NT) for p in range(npair)], axis=0)
            dah, dal = _split(da)
            dsums = _dg(jnp.concatenate([dah, dal], axis=0), m_before, NN)
            sig = jnp.exp(lb)
            dz = (da * (1.0 - sig) - (dseen + dsums[:rows] + dsums[rows:]) * sig) * scale
            if diag:
                dz = jnp.where(strict, dz, 0.0)
            dzb = dz.astype(BF16)
            for p in range(npair):
                pl_ = slice(p * LANE, (p + 1) * LANE)
                dv_acc[pl.ds(start, SBQ), pl_] += _dg(pair_rows(wb, p), do_pair[p], TN)
                dk_acc[pl.ds(start, SBQ), pl_] += _dg(pair_rows(dzb, p), q_pair[p], TN)
            dq = dq + jnp.concatenate([_dg(pair_rows(dzb, p), kk[p], NN) for p in range(npair)], axis=0)
            seen = seen + jnp.sum(hi.astype(F32) + lo.astype(F32), axis=1, keepdims=True)
            dseen = dseen + jnp.sum(da, axis=1, keepdims=True)
            return seen, dseen, dq

        zero = (jnp.zeros((rows, 1), F32), jnp.zeros((rows, 1), F32), jnp.zeros((rows, LANE), F32))
        first = qi - n_ref[(grp // per_count) * nq + qi].astype(jnp.int32)
        state = lax.fori_loop(first, qi, lambda kb, c: block(kb, c, False), zero)
        _, _, dq = block(qi, state, True)
        for p in range(npair):
            dq_ref[:, p * LANE:(p + 1) * LANE] = jnp.where(
                _head_mask(0), dq[2 * p * SBQ:(2 * p + 1) * SBQ], dq[(2 * p + 1) * SBQ:(2 * p + 2) * SBQ]).astype(BF16)

        @pl.when(qi == nq - 1)
        def _():
            dk_ref[...] = dk_acc[...].astype(BF16)
            dv_ref[...] = dv_acc[...].astype(BF16)

        if nr:
            rider.ride(grid, riders, "finish")

    full = jax.ShapeDtypeStruct((t, 512), BF16)
    outs = pl.pallas_call(
        body, name="sb_bwd", grid=grid,
        in_specs=[_spec((SBQ, wide), lambda g, i: (i, g)),
                  _spec((t, wide), lambda g, i: (0, ngrp + g)),
                  _spec((t, wide), lambda g, i: (0, 2 * ngrp + g)),
                  _spec((SBQ, wide), lambda g, i: (i, ngrp + g)),
                  _spec((npair, SBQ, LANE), lambda g, i: (g, i, 0)),
                  pl.BlockSpec(memory_space=pltpu.SMEM)] + (rider.in_specs if nr else []),
        out_specs=[_spec((SBQ, wide), lambda g, i: (i, g)),
                   _spec((t, wide), lambda g, i: (0, g)), _spec((t, wide), lambda g, i: (0, g))]
        + (rider.out_specs if nr else []),
        out_shape=[full, full, full] + (rider.out_shape if nr else []),
        scratch_shapes=[pltpu.VMEM((t, wide), F32), pltpu.VMEM((t, wide), F32)] + (rider.scratch if nr else []),
        compiler_params=pltpu.CompilerParams(dimension_semantics=("arbitrary", "arbitrary"),
                                             vmem_limit_bytes=VMEM_LIMIT_BIG),
    )(proj_b, proj_b, proj_b, dy, ltot, nblk, *(rider.groups if nr else []))
    return outs[0], outs[1], outs[2], (rider.place(outs[3:]) if nr else [])


def _hgrn_gates(qr, fr, c0, c1):
    mx = jnp.maximum(c0, c1)
    e0, e1 = jnp.exp(c0 - mx), jnp.exp(c1 - mx)
    lb = e1 / (e0 + e1)
    sx = _sigmoid(fr)
    f = lb + (1.0 - lb) * sx
    k = (1.0 - lb) * (1.0 - sx)
    sq = _sigmoid(qr)
    return lb, sx, f, k, sq, qr * sq


def _chunk_sums(mask, x):
    n = x.shape[1]
    hi, lo = _split(x)
    both = _dg(_ones_where(mask), jnp.concatenate([hi, lo], axis=1), NN)
    return both[:, :n] + both[:, n:]


def _chunk_masks():
    row = lax.broadcasted_iota(jnp.int32, (CHUNK, CHUNK), 0)
    col = lax.broadcasted_iota(jnp.int32, (CHUNK, CHUNK), 1)
    return col <= row, col >= row


def _hgrn_fwd(proj_c, small, carry=None):
    t = proj_c.shape[1]
    nc = t // CHUNK
    nh = D // HD

    hps = HGRN_HPS
    wide = hps * HD
    grid = (nh // hps, nc)
    nr = carry.n if carry is not None else 0

    def body(*refs):
        p_ref, c0_ref, c1_ref, gam_ref = refs[:4]
        o_ref, y_ref, sst_ref = refs[4 + nr:7 + nr]
        st_ref = refs[7 + 2 * nr]
        riders = (refs[4:4 + nr], refs[7 + nr:7 + 2 * nr], refs[-2], refs[-1]) if nr else None
        if nr:
            carry.ride(grid, riders, "start")
        c = pl.program_id(1)

        @pl.when(c == 0)
        def _():
            st_ref[...] = jnp.zeros_like(st_ref)

        _, _, f_all, k_all, _, q_all = _hgrn_gates(p_ref[0], p_ref[1], c0_ref[...], c1_ref[...])
        tril, _ = _chunk_masks()
        b_all = _chunk_sums(tril, jnp.log(f_all))
        for j in range(hps):
            ln = slice(j * HD, (j + 1) * HD)
            st0 = st_ref[j]
            sst_ref[j] = st0
            v, g = p_ref[2, :, ln], p_ref[3, :, ln]
            q, k, b = q_all[:, ln], k_all[:, ln], b_all[:, ln]
            bm = b[CHUNK // 2 - 1:CHUNK // 2]
            bl = b[CHUNK - 1:CHUNK]
            qd = q * jnp.exp(b)
            qt = q * jnp.exp(b - bm)
            kt = k * jnp.exp(bm - b)
            kl = k * jnp.exp(bl - b)
            vb = v.astype(BF16)
            att = jnp.where(tril, _dot3(qt, kt, NT), 0.0)
            o = _dg(qd.astype(BF16), st0.astype(BF16), NT) + _dg(att.astype(BF16), vb, NN)
            st_ref[j] = st0 * jnp.exp(bl) + _dg(vb, kl.astype(BF16), TN)
            o_ref[:, ln] = o
            r = lax.rsqrt(jnp.mean(o * o, axis=-1, keepdims=True) + RMS_EPS)
            y_ref[:, ln] = (o * r * gam_ref[...] * (g * _sigmoid(g))).astype(BF16)
        if nr:
            carry.ride(grid, riders, "finish")

    outs = pl.pallas_call(
        body, name="hgrn_fwd", grid=grid,
        in_specs=[_spec((4, CHUNK, wide), lambda h, c: (0, c, h)),
                  _spec((None, 1, wide), lambda h, c: (R_CLB, 0, h)),
                  _spec((None, 1, wide), lambda h, c: (R_CLB + 1, 0, h)),
                  _spec((None, 1, HD), lambda h, c: (R_GAM, 0, 0))] + (carry.in_specs if nr else []),
        out_specs=[_spec((CHUNK, wide), lambda h, c: (c, h)), _spec((CHUNK, wide), lambda h, c: (c, h)),
                   _spec((None, hps, HD, HD), lambda h, c: (c, h, 0, 0))] + (carry.out_specs if nr else []),
        out_shape=[jax.ShapeDtypeStruct((t, D), F32), jax.ShapeDtypeStruct((t, D), BF16),
                   jax.ShapeDtypeStruct((nc, nh, HD, HD), F32)] + (carry.out_shape if nr else []),
        scratch_shapes=[pltpu.VMEM((hps, HD, HD), F32)] + (carry.scratch if nr else []),
        compiler_params=_params(("arbitrary", "arbitrary")),
    )(proj_c, small, small, small, *(carry.groups if nr else []))
    return outs[0], outs[1], outs[2], (carry.place(outs[3:]) if nr else [])


def _hgrn_bwd(proj_c, small, o, sst, dyc, rider=None):
    t = proj_c.shape[1]
    nc = t // CHUNK
    nh = D // HD

    hps = HGRN_HPS
    wide = hps * HD
    ng = nh // hps
    grid = (ng, nc)
    nr = rider.n if rider is not None else 0

    def body(*refs):
        p_ref, c0_ref, c1_ref, gam_ref, o_ref, sst_ref, dy_ref = refs[:7]
        dp_ref, dclb_ref, dgam_ref = refs[7 + nr:10 + nr]
        dst_ref, dlb_acc, dgam_acc = refs[10 + 2 * nr:13 + 2 * nr]
        riders = (refs[7:7 + nr], refs[10 + nr:10 + 2 * nr], refs[-2], refs[-1]) if nr else None
        if nr:
            rider.ride(grid, riders, "start")
        group = pl.program_id(0)
        step = pl.program_id(1)

        @pl.when(step == 0)
        def _():
            dst_ref[...] = jnp.zeros_like(dst_ref)
            dlb_acc[...] = jnp.zeros_like(dlb_acc)

        @pl.when((step == 0) & (group == 0))
        def _():
            dgam_acc[...] = jnp.zeros_like(dgam_acc)

        gam = gam_ref[...]
        qr_all = p_ref[0]
        lb_all, sx_all, f_all, k_all, sq_all, q_all = _hgrn_gates(qr_all, p_ref[1], c0_ref[...], c1_ref[...])
        tril, triu = _chunk_masks()
        b_all = _chunk_sums(tril, jnp.log(f_all))
        dq_parts, dk_parts, db_parts, dgam_parts = [], [], [], []
        for j in range(hps):
            ln = slice(j * HD, (j + 1) * HD)
            st0 = sst_ref[j]
            dst1 = dst_ref[j]
            v, g = p_ref[2, :, ln], p_ref[3, :, ln]
            q, k, b = q_all[:, ln], k_all[:, ln], b_all[:, ln]
            bm = b[CHUNK // 2 - 1:CHUNK // 2]
            bl = b[CHUNK - 1:CHUNK]
            eb = jnp.exp(b)
            e_qt = jnp.exp(b - bm)
            e_kt = jnp.exp(bm - b)
            e_kl = jnp.exp(bl - b)
            e_bl = jnp.exp(bl)
            qd, qt, kt, kl = q * eb, q * e_qt, k * e_kt, k * e_kl
            ov = o_ref[:, ln]
            r = lax.rsqrt(jnp.mean(ov * ov, axis=-1, keepdims=True) + RMS_EPS)
            oh = ov * r
            sg = _sigmoid(g)
            dy = dy_ref[:, ln]
            dp_ref[3, :, ln] = (dy * oh * gam * (sg * (1.0 + g * (1.0 - sg)))).astype(BF16)
            dyv = dy * (g * sg)
            gdy = dyv * gam
            do = (gdy - oh * jnp.mean(gdy * oh, axis=-1, keepdims=True)) * r
            dob, vb = do.astype(BF16), v.astype(BF16)
            st0b, dst1b = st0.astype(BF16), dst1.astype(BF16)
            st1 = st0 * e_bl + _dg(vb, kl.astype(BF16), TN)
            att = jnp.where(tril, _dot3(qt, kt, NT), 0.0)
            datt = jnp.where(tril, _dg(dob, vb, NT), 0.0)
            dv = _dg(att.astype(BF16), dob, TN) + _dg(kl.astype(BF16), dst1b, NT)
            dq = _dot3(datt, kt, NN) * e_qt + _dg(dob, st0b, NN) * eb
            dk = _dot3(datt, qt, TN) * e_kt + _dg(vb, dst1b, NN) * e_kl
            db = q * dq - k * dk
            last = lax.broadcasted_iota(jnp.int32, (CHUNK, 1), 0) == CHUNK - 1
            db = db + jnp.where(last, jnp.sum(dst1 * st1, axis=0, keepdims=True), 0.0)
            dst_ref[j] = dst1 * e_bl + _dg(dob, qd.astype(BF16), TN)
            dp_ref[2, :, ln] = dv.astype(BF16)
            dq_parts.append(dq)
            dk_parts.append(dk)
            db_parts.append(db)
            dgam_parts.append(jnp.sum(dyv * oh, axis=0, keepdims=True))

        dq_all, dk_all = jnp.concatenate(dq_parts, axis=1), jnp.concatenate(dk_parts, axis=1)
        dlf = _chunk_sums(triu, jnp.concatenate(db_parts, axis=1))
        dp_ref[0] = (dq_all * (sq_all * (1.0 + qr_all * (1.0 - sq_all)))).astype(BF16)
        tmp = dlf / f_all - dk_all
        dp_ref[1] = (tmp * (1.0 - lb_all) * sx_all * (1.0 - sx_all)).astype(BF16)
        dlb_acc[...] += jnp.sum((1.0 - sx_all) * tmp, axis=0, keepdims=True)
        dgam_acc[...] += functools.reduce(lambda a, b: a + b, dgam_parts)

        @pl.when(step == nc - 1)
        def _():
            d1 = dlb_acc[...] * lb_all * (1.0 - lb_all)
            dclb_ref[...] = jnp.where(lax.broadcasted_iota(jnp.int32, (2, wide), 0) == 0, -d1, d1)

        @pl.when((step == nc - 1) & (group == ng - 1))
        def _():
            dgam_ref[...] = dgam_acc[...]

        if nr:
            rider.ride(grid, riders, "finish")

    rev = lambda h, s: (nc - 1 - s, h)
    outs = pl.pallas_call(
        body, name="hgrn_bwd", grid=grid,
        in_specs=[_spec((4, CHUNK, wide), lambda h, s: (0, nc - 1 - s, h)),
                  _spec((None, 1, wide), lambda h, s: (R_CLB, 0, h)),
                  _spec((None, 1, wide), lambda h, s: (R_CLB + 1, 0, h)),
                  _spec((None, 1, HD), lambda h, s: (R_GAM, 0, 0)),
                  _spec((CHUNK, wide), rev),
                  _spec((None, hps, HD, HD), lambda h, s: (nc - 1 - s, h, 0, 0)),
                  _spec((CHUNK, wide), rev)] + (rider.in_specs if nr else []),
        out_specs=[_spec((4, CHUNK, wide), lambda h, s: (0, nc - 1 - s, h)),
                   _spec((2, wide), lambda h, s: (0, h)),
                   _spec((1, HD), lambda h, s: (0, 0))] + (rider.out_specs if nr else []),
        out_shape=[jax.ShapeDtypeStruct((4, t, D), BF16), jax.ShapeDtypeStruct((2, D), F32),
                   jax.ShapeDtypeStruct((1, HD), F32)] + (rider.out_shape if nr else []),
        scratch_shapes=[pltpu.VMEM((hps, HD, HD), F32), pltpu.VMEM((1, wide), F32), pltpu.VMEM((1, HD), F32)]
        + (rider.scratch if nr else []),
        compiler_params=_params(("arbitrary", "arbitrary")),
    )(proj_c, small, small, small, o, sst, dyc, *(rider.groups if nr else []))
    return outs[0], outs[1], outs[2], (rider.place(outs[3:]) if nr else [])


def _adamw(name, w, grads, m, v):
    nl = len(grads)
    rows, cols = grads[0].shape
    br = rows
    for cand in (512, 352, 256):
        if rows % cand == 0:
            br = cand
            break
    nb = rows // br
    c1 = 1.0 - ADAM_B1 ** ADAM_STEP
    c2 = 1.0 - ADAM_B2 ** ADAM_STEP

    def body(w_ref, m_ref, v_ref, *refs):
        g_refs, (d_ref, mo_ref, vo_ref, go_ref) = refs[:nl], refs[nl:]
        layer = pl.program_id(0)
        gv = g_refs[0][...]
        for k in range(1, nl):
            gv = jnp.where(layer == k, g_refs[k][...], gv)
        mn = ADAM_B1 * m_ref[...] + (1.0 - ADAM_B1) * gv
        vn = ADAM_B2 * v_ref[...] + (1.0 - ADAM_B2) * (gv * gv)
        mo_ref[...] = mn
        vo_ref[...] = vn
        go_ref[...] = gv
        d_ref[...] = -ADAM_LR * ((mn / c1) / (jnp.sqrt(vn / c2) + ADAM_EPS) + ADAM_WD * w_ref[...])

    blk = _spec((br, cols), lambda l, i: (l * nb + i, 0))
    g_specs = [_spec((br, cols), lambda l, i, k=k: (jnp.where(l == k, i, 0), 0)) for k in range(nl)]
    shape = jax.ShapeDtypeStruct((nl * rows, cols), F32)
    return pl.pallas_call(
        body, name=name, grid=(nl, nb), in_specs=[blk] * 3 + g_specs, out_specs=[blk] * 4, out_shape=[shape] * 4,
        compiler_params=_params(("arbitrary", "arbitrary")),
    )(w, m, v, *grads)


def _place():
    x, y, c = lax.axis_index("x"), lax.axis_index("y"), lax.axis_index("c")
    chips = [(1 - x, y), (x, 1 - y), (1 - x, 1 - y)]
    return x, y, c, chips


class _Rider:
    n = 0
    relay = None

    def ride(self, grid, refs, when):
        if not self.n:
            return
        ids = [pl.program_id(a) for a in range(len(grid))]
        first = functools.reduce(jnp.logical_and, [i == 0 for i in ids])
        last = functools.reduce(jnp.logical_and, [i == g - 1 for i, g in zip(ids, grid)])
        phases = [(first, self.start), (last, self.relay)] if when == "start" else [(last, self.finish)]
        for cond, phase in phases:
            if phase is not None:
                pl.when(cond)(functools.partial(phase, *refs))


class _Gather(_Rider):
    PER_GROUP = 7

    def __init__(self, groups):
        self.groups = list(groups)
        self.n = len(self.groups)
        self.in_specs = [ANY] * self.n
        self.out_specs = [ANY] * self.n
        self.out_shape = [jax.ShapeDtypeStruct((NSH,) + g.shape, g.dtype) for g in self.groups]
        sems = pltpu.SemaphoreType.DMA((self.PER_GROUP * self.n,))
        self.scratch = [sems, sems] if self.n else []

    def _copies(self, ins, outs, send, recv):
        x, y, c, chips = _place()
        sibling = (x, y, 1 - c)

        def half(gi, chip, hc):
            return outs[gi].at[2 * chip[0] + chip[1], hc]

        def copy(gi, k, src, dst, to):
            sem = self.PER_GROUP * gi + k
            return pltpu.make_async_remote_copy(src_ref=src, dst_ref=dst, send_sem=send.at[sem], recv_sem=recv.at[sem],
                                                device_id=to, device_id_type=MESH)

        pairs = [(gi, j, chip) for gi in range(self.n) for j, chip in enumerate(chips)]
        first = [copy(gi, j, ins[gi].at[c], half(gi, (x, y), c), (*chip, c)) for gi, j, chip in pairs]
        first += [copy(gi, 6, ins[gi], outs[gi].at[2 * x + y], sibling) for gi in range(self.n)]
        landed = [copy(gi, j, half(gi, chip, c), half(gi, chip, c), sibling) for gi, j, chip in pairs]
        relay = [copy(gi, 3 + j, half(gi, chip, c), half(gi, chip, c), sibling) for gi, j, chip in pairs]
        relayed = [copy(gi, 3 + j, half(gi, chip, 1 - c), half(gi, chip, 1 - c), sibling) for gi, j, chip in pairs]
        relayed += [copy(gi, 6, ins[gi], outs[gi].at[2 * x + y], sibling) for gi in range(self.n)]
        return first, landed, relay, relayed

    def start(self, ins, outs, send, recv):
        for cp in self._copies(ins, outs, send, recv)[0]:
            cp.start()

    def relay(self, ins, outs, send, recv):
        _, landed, relay, _ = self._copies(ins, outs, send, recv)
        for arrived, onward in zip(landed, relay):
            arrived.wait_recv()
            onward.start()

    def finish(self, ins, outs, send, recv):
        first, _, relay, relayed = self._copies(ins, outs, send, recv)
        for cp in relayed:
            cp.wait_recv()
        for cp in first + relay:
            cp.wait_send()

    def place(self, outs):
        return list(outs)


def _alone(name, rider):
    n = rider.n

    def body(*refs):
        parts = (refs[:n], refs[n:2 * n], refs[2 * n], refs[2 * n + 1])
        rider.start(*parts)
        if rider.relay is not None:
            rider.relay(*parts)
        rider.finish(*parts)

    outs = pl.pallas_call(
        body, name=name, in_specs=rider.in_specs, out_specs=rider.out_specs, out_shape=rider.out_shape,
        scratch_shapes=rider.scratch, compiler_params=pltpu.CompilerParams(has_side_effects=True),
    )(*rider.groups)
    return rider.place(outs)


class _Swap(_Rider):
    def __init__(self, slots):
        self.slots = list(slots)
        self.groups = [buf for buf, _, _ in self.slots]
        self.n = len(self.slots)
        self.in_specs = [ANY] * self.n
        self.out_specs = [ANY] * self.n
        self.out_shape = [jax.ShapeDtypeStruct((NSH, rows // 2, buf.shape[2]), buf.dtype) for buf, _, rows in self.slots]
        self.scratch = [pltpu.SemaphoreType.DMA((self.n,)), pltpu.SemaphoreType.DMA((self.n,))]

    def _copies(self, ins, outs, send, recv):
        x, y, c, _ = _place()
        cps = []
        for i, (_, row0, rows) in enumerate(self.slots):
            half = rows // 2
            src = ins[i].at[:, pl.ds(pl.multiple_of(row0 + (1 - c) * half, 16), half)]
            cps.append(pltpu.make_async_remote_copy(src_ref=src, dst_ref=outs[i], send_sem=send.at[i],
                                                    recv_sem=recv.at[i], device_id=(x, y, 1 - c), device_id_type=MESH))
        return cps

    def start(self, ins, outs, send, recv):
        for cp in self._copies(ins, outs, send, recv):
            cp.start()

    def finish(self, ins, outs, send, recv):
        for cp in self._copies(ins, outs, send, recv):
            cp.wait()

    def place(self, outs):
        return list(outs)


class _Share(_Rider):
    def __init__(self, arrays):
        self.groups = list(arrays)
        self.n = len(self.groups)
        self.in_specs = [ANY] * self.n
        self.out_specs = [ANY] * self.n
        self.out_shape = [jax.ShapeDtypeStruct((2,) + g.shape, g.dtype) for g in self.groups]
        self.scratch = [pltpu.SemaphoreType.DMA((self.n,)), pltpu.SemaphoreType.DMA((self.n,))]

    def _copies(self, ins, outs, send, recv, half):
        x, y, c, _ = _place()
        return [pltpu.make_async_remote_copy(src_ref=ins[gi], dst_ref=outs[gi].at[c if half == "mine" else 1 - c],
                                             send_sem=send.at[gi], recv_sem=recv.at[gi], device_id=(x, y, 1 - c),
                                             device_id_type=MESH) for gi in range(self.n)]

    def start(self, ins, outs, send, recv):
        for cp in self._copies(ins, outs, send, recv, "mine"):
            cp.start()

    def finish(self, ins, outs, send, recv):
        for cp in self._copies(ins, outs, send, recv, "theirs"):
            cp.wait_recv()
        for cp in self._copies(ins, outs, send, recv, "mine"):
            cp.wait_send()

    def place(self, outs):
        c = lax.axis_index("c")
        return [lax.dynamic_update_index_in_dim(o, g, c, 0) for o, g in zip(outs, self.groups)]


class _Send(_Rider):
    def __init__(self, arrays):
        self.groups = list(arrays)
        self.n = len(self.groups)
        self.in_specs = [ANY] * self.n
        self.out_specs = [ANY] * self.n
        self.out_shape = [jax.ShapeDtypeStruct((3,) + g.shape[1:], g.dtype) for g in self.groups]
        self.scratch = [pltpu.SemaphoreType.DMA((3 * self.n,)), pltpu.SemaphoreType.DMA((3 * self.n,))]

    def _copies(self, ins, outs, send, recv):
        x, y, c, chips = _place()
        return [pltpu.make_async_remote_copy(src_ref=ins[gi].at[2 * chip[0] + chip[1]], dst_ref=outs[gi].at[j],
                                             send_sem=send.at[3 * gi + j], recv_sem=recv.at[3 * gi + j],
                                             device_id=(*chip, c), device_id_type=MESH)
                for gi in range(self.n) for j, chip in enumerate(chips)]

    def start(self, ins, outs, send, recv):
        for cp in self._copies(ins, outs, send, recv):
            cp.start()

    def finish(self, ins, outs, send, recv):
        for cp in self._copies(ins, outs, send, recv):
            cp.wait()

    def place(self, outs):
        return list(outs)


def _pair_sum(name, slots, got, c_idx):
    n = len(slots)
    in_specs, out_specs, out_shape, operands = [], [], [], []
    for (buf, row0, rows), g in zip(slots, got):
        hb, cols = rows // 4, buf.shape[2]
        in_specs += [pl.BlockSpec((None, hb, cols), lambda q, i, cr, r=row0 // hb: (q, r + 2 * cr[0] + i, 0)),
                     pl.BlockSpec((None, hb, cols), lambda q, i, cr: (q, i, 0))]
        out_specs.append(pl.BlockSpec((None, hb, cols), lambda q, i, cr: (q, i, 0)))
        out_shape.append(jax.ShapeDtypeStruct(g.shape, BF16))
        operands += [buf, g]

    def body(c_ref, *refs):
        for i in range(n):
            refs[2 * n + i][...] = (refs[2 * i][...].astype(F32) + refs[2 * i + 1][...].astype(F32)).astype(BF16)

    return pl.pallas_call(
        body, name=name,
        grid_spec=pltpu.PrefetchScalarGridSpec(num_scalar_prefetch=1, grid=(NSH, 2), in_specs=in_specs, out_specs=out_specs),
        out_shape=out_shape, compiler_params=_params(("parallel", "parallel")),
    )(c_idx, *operands)


def _owner_sum(name, pairs, got, p_idx):
    n = len(pairs)
    in_specs, out_specs, out_shape, operands = [], [], [], []
    for own, g in zip(pairs, got):
        _, rows, cols = own.shape
        hb = rows // 2
        in_specs += [pl.BlockSpec((None, hb, cols), lambda i, pr: (pr[0], i, 0)),
                     pl.BlockSpec((3, hb, cols), lambda i, pr: (0, i, 0))]
        out_specs.append(pl.BlockSpec((hb, cols), lambda i, pr: (i, 0)))
        out_shape.append(jax.ShapeDtypeStruct((rows, cols), F32))
        operands += [own, g]

    def body(p_ref, *refs):
        for i in range(n):
            a_ref, b_ref = refs[2 * i], refs[2 * i + 1]
            refs[2 * n + i][...] = ((a_ref[...].astype(F32) + b_ref[0].astype(F32)) + b_ref[1].astype(F32)) + b_ref[2].astype(F32)

    return pl.pallas_call(
        body, name=name,
        grid_spec=pltpu.PrefetchScalarGridSpec(num_scalar_prefetch=1, grid=(2,), in_specs=in_specs, out_specs=out_specs),
        out_shape=out_shape, compiler_params=_params(("parallel",)),
    )(p_idx, *operands)


def _sum_small(slab):
    def body(in_ref, out_ref, all_ref, send, recv):
        x, y, c, _ = _place()
        me = 4 * x + 2 * y + c
        all_ref[me] = in_ref[...]
        cps = []
        for k in range(1, 8):
            peer = (x ^ (k >> 2), y ^ ((k >> 1) & 1), c ^ (k & 1))
            cps.append(pltpu.make_async_remote_copy(src_ref=in_ref, dst_ref=all_ref.at[me], send_sem=send.at[k - 1],
                                                    recv_sem=recv.at[k - 1], device_id=peer, device_id_type=MESH))
        for cp in cps:
            cp.start()
        for cp in cps:
            cp.wait()
        total = all_ref[0]
        for d in range(1, 8):
            total = total + all_ref[d]
        out_ref[...] = total

    return pl.pallas_call(
        body, name="sum_small",
        in_specs=[pl.BlockSpec(memory_space=pltpu.VMEM)], out_specs=pl.BlockSpec(memory_space=pltpu.VMEM),
        out_shape=jax.ShapeDtypeStruct(slab.shape, F32),
        scratch_shapes=[pltpu.VMEM((8,) + slab.shape, F32), pltpu.SemaphoreType.DMA((7,)), pltpu.SemaphoreType.DMA((7,))],
        compiler_params=pltpu.CompilerParams(has_side_effects=True),
    )(slab)


FFNS = ("pre0", "post0", "pre1", "post1")
W_SHAPES = dict({f + "_gu": (NSH, 2, FS, D) for f in FFNS}, **{f + "_d": (NSH, FS, D) for f in FFNS},
                ab_in=(NSH, D, 768), ab_out=(NSH, 256, D), conv=(NSH, 2, 8, LANE), c_in=(NSH, D, D), c_out=(NSH, 256, D))
CARRIED = dict(pre0_norm=("pre0_gu",), pre0_up=("pre0_d",), pre0_down=("ab_in", "conv"), ab_proj_a=("ab_out",),
               c_proj=("post1_d",),
               sb_fwd=("post0_gu", "post0_d"), post0_up=("pre1_gu",), post0_down=("pre1_d",),
               pre1_up=("c_in", "c_out"), hgrn_fwd=("post1_gu",))


FFN_SLOTS = dict(pre0=(0, 1, 0), pre1=(2, 3, 1), post0=(4, 5, 2), post1=(6, 7, 3))
GRAD_SLOTS = dict(ab_out=("b", B_ABOUT, 256), c_in=("b", B_CIN, D), c_out=("b", B_COUT, 256), ab_in=("c", 0, D))
for _f, (_g, _u, _d) in FFN_SLOTS.items():
    GRAD_SLOTS.update({_f + "_g": ("a", _g * FS, FS), _f + "_u": ("a", _u * FS, FS), _f + "_d": ("b", _d * FS, FS)})
REDUCE_STAGES = dict(x=("post1_g", "post1_u", "post1_d", "c_out"), w=("c_in", "pre1_d"),
                     y=("pre1_g", "pre1_u", "post0_g", "post0_u", "post0_d", "ab_out"),
                     z1=("ab_in",), z2=("pre0_g", "pre0_u", "pre0_d"))


def _local_step(x, target, weights, small, shards=None, place=None):
    t = x.shape[0]
    tm = min(MM_TILE, t)
    th = min(HALF_TILE, t)
    tw = min(WGRAD_TILE, t)
    nt = t // tm
    tok_si = _spec((tm, D), lambda s, i: (i, 0))
    tok_is = _spec((tm, D), lambda i, s: (i, 0))
    w = dict(weights)
    reduced = {}

    def stage_slots(stage, buffers):
        return [(buffers[GRAD_SLOTS[n][0]],) + GRAD_SLOTS[n][1:] for n in REDUCE_STAGES[stage]]

    def swap_rider(stage, buffers):
        return _Swap(stage_slots(stage, buffers)) if place is not None else None

    def reduce_start(stage, buffers, swapped=None):
        if place is None:
            return [], None
        slots = stage_slots(stage, buffers)
        if swapped is None:
            swapped = _alone("grad_swap_" + stage, _Swap(slots))
        pairs = _pair_sum("grad_pair_sum_" + stage, slots, swapped, place[0])
        return pairs, _Send(pairs)

    def reduce_end(stage, pairs, landed):
        if place is None:
            return None
        mine = _owner_sum("grad_owner_sum_" + stage, pairs, landed, place[1])
        return _Share(mine)

    def shared(stage, landed):
        if place is not None:
            reduced.update(zip(REDUCE_STAGES[stage], landed))

    def carried(kernel_name):
        names = [n for n in CARRIED[kernel_name] if n not in w]
        return names, (_Gather([shards[n] for n in names]) if names else None)

    def land(names, arrays):
        for n, a in zip(names, arrays):
            w[n] = a.reshape(W_SHAPES[n])

    def ffn_forward(tag, h, hn, next_row):
        names, gather = carried(tag + "_up") if tag + "_up" in CARRIED else ([], None)
        s_up, s_gate, a, got = _ffn_up(tag + "_up", hn, w[tag + "_gu"], gather)
        land(names, got)
        names, gather = carried(tag + "_down") if tag + "_down" in CARRIED else ([], None)
        out = _ffn_down(tag + "_down", a, w[tag + "_d"], h, gather, (small, next_row) if next_row is not None else None)
        if gather is not None:
            out, got = out
            land(names, got)
        out, hn_next = out if next_row is not None else (out, None)
        return out, hn_next, (h, hn, s_up, s_gate, a)

    def out_proj(name, y, w_out, h, next_row):
        row_spec = _spec((th, D), lambda i: (i, 0))
        return _mm(name, [(y, row_spec, w_out.reshape(D, D), _spec((D, D), lambda i: (0, 0)))],
                   grid=(t // th,), o_shape=(t, D), o_dtype=F32, o_spec=row_spec, dims=NN, kaxis=0, nk=1,
                   res=(h, row_spec), norm=(small, next_row))

    def out_proj_bwd(tag, dhb, y, w_out, blk, grad_b, make_rider=None):
        grad_b = _wgrad(tag + "_dwout", y, _spec((tw, 256), lambda s, k: (k, s)), dhb, _spec((tw, D), lambda s, k: (k, 0)),
                        256, D, t, tw, grad_b, blk)
        rider = make_rider(grad_b) if make_rider is not None else None
        row_spec = _spec((th, D), lambda i: (i, 0))
        dy = _mm(tag + "_dy", [(dhb, row_spec, w_out.reshape(D, D), _spec((D, D), lambda i: (0, 0)))],
                 grid=(t // th,), o_shape=(t, D), o_dtype=F32, o_spec=row_spec, dims=NT, kaxis=0, nk=1, carry=rider)
        dy, landed = dy if rider is not None else (dy, None)
        return dy, grad_b, landed

    h0 = x
    names, gather = carried("pre0_norm")
    hn, got = _norm_fwd("pre0_norm", h0, small, R_PRE, gather)
    land(names, got)
    h1, hn_ab, pre0 = ffn_forward("pre0", h0, hn, R_MIX)
    def carrying(kernel_name, make):
        names, gather = carried(kernel_name)
        out = make(gather)
        if gather is not None:
            out, got = out
            land(names, got)
        return out

    proj_a = carrying("ab_proj_a", lambda gather: _mm(
        "ab_proj_a", [(hn_ab, tok_is, w["ab_in"], _spec((None, D, 768), lambda i, s: (s, 0, 0)))],
        grid=(nt, 2), o_shape=(t, 1536), o_dtype=F32, o_spec=_spec((tm, 768), lambda i, s: (i, s)),
        dims=NN, kaxis=1, nk=1, carry=gather))
    proj_b = _mm("ab_proj_b", [(hn_ab, tok_is, w["ab_in"], _spec((None, D, 768), lambda i, s: (s + 2, 0, 0)))],
                 grid=(nt, 2), o_shape=(t, 1536), o_dtype=BF16, o_spec=_spec((tm, 768), lambda i, s: (i, s)),
                 dims=NN, kaxis=1, nk=1)
    y_a = _conv_fwd(proj_a, w["conv"])
    names, gather = carried("sb_fwd")
    y_b, ltot, nblk, got = _sb_fwd(proj_b, gather)
    land(names, got)
    y_ab = jnp.concatenate([y_a, y_b], axis=1)
    h2, hn = out_proj("ab_out", y_ab, w["ab_out"], h1, R_POST)
    h3, hn, post0 = ffn_forward("post0", h2, hn, R_PRE + 1)
    h4, hn_c, pre1 = ffn_forward("pre1", h3, hn, R_MIX + 1)
    proj_c = carrying("c_proj", lambda gather: _mm(
        "c_proj", [(hn_c, tok_si, w["c_in"], _spec((None, D, D), lambda s, i: (s, 0, 0)))],
        grid=(NSH, nt), o_shape=(NSH, t, D), o_dtype=F32, o_spec=_spec((None, tm, D), lambda s, i: (s, i, 0)),
        dims=NN, kaxis=1, nk=1, carry=gather))
    names, gather = carried("hgrn_fwd")
    o_c, y_c, sst, got = _hgrn_fwd(proj_c, small, gather)
    land(names, got)
    h5, hn = out_proj("c_out", y_c, w["c_out"], h4, R_POST + 1)
    h6, _, post1 = ffn_forward("post1", h5, hn, None)
    dh, dhb, d_fin, loss = _final_loss(h6, small, target)

    dh, dhb, grad_a, grad_b, dn_post1, _, _ = _ffn_backward("post1", dh, dhb, *post1, w["post1_gu"], w["post1_d"],
                                                            *FFN_SLOTS["post1"], small, R_POST + 1, 8 * FS, B_ROWS)
    dy_c, grad_b, swapped = out_proj_bwd("c", dhb, y_c, w["c_out"], B_COUT // 256, grad_b,
                                         lambda buf_b: swap_rider("x", dict(a=grad_a, b=buf_b)))
    pairs, rider = reduce_start("x", dict(a=grad_a, b=grad_b), swapped)
    dproj_c, d_clb, d_gam, landed = _hgrn_bwd(proj_c, small, o_c, sst, dy_c, rider)
    share = reduce_end("x", pairs, landed)
    grad_b = _wgrad("c_dwin", hn_c, _spec((tw, D), lambda s, k: (k, 0)), dproj_c, _spec((None, tw, D), lambda s, k: (s, k, 0)),
                    D, D, t, tw, grad_b, B_CIN // D, carry=share)
    if share is not None:
        grad_b, landed = grad_b
        shared("x", landed)
    dh, dhb, dn_mix1, _ = _dhn_norm("c_dhn", [(dproj_c, _spec((None, th, D), lambda i, s=s: (s, i, 0)),
                                               w["c_in"], _spec((None, D, D), lambda i, s=s: (s, 0, 0)))
                                              for s in range(NSH)],
                                    NT, h4, small, R_MIX + 1, dh)
    stage_w = {}

    def w_swap(buf_a, buf_b):
        return swap_rider("w", dict(a=buf_a, b=buf_b))

    def w_send(buf_a, buf_b, swapped):
        stage_w["pairs"], rider = reduce_start("w", dict(a=buf_a, b=buf_b), swapped)
        return rider

    on = place is not None
    dh, dhb, grad_a, grad_b, dn_pre1, _, landed = _ffn_backward(
        "pre1", dh, dhb, *pre1, w["pre1_gu"], w["pre1_d"], *FFN_SLOTS["pre1"], small, R_PRE + 1, grad_a, grad_b,
        w_swap if on else None, w_send if on else None)
    share = reduce_end("w", stage_w.get("pairs"), landed)
    dh, dhb, grad_a, grad_b, dn_post0, landed, _ = _ffn_backward(
        "post0", dh, dhb, *post0, w["post0_gu"], w["post0_d"], *FFN_SLOTS["post0"], small, R_POST, grad_a, grad_b,
        (lambda buf_a, buf_b: share) if on else None)
    shared("w", landed)
    dy_ab, grad_b, _ = out_proj_bwd("ab", dhb, y_ab, w["ab_out"], B_ABOUT // 256, grad_b)
    dab, dac, dax, d_conv, swapped = _conv_bwd(proj_a, w["conv"], dy_ab, swap_rider("y", dict(a=grad_a, b=grad_b)))
    pairs, rider = reduce_start("y", dict(a=grad_a, b=grad_b), swapped)
    dq, dk, dv, landed = _sb_bwd(proj_b, dy_ab, ltot, nblk, rider)
    share = reduce_end("y", pairs, landed)
    dproj_ab = jnp.concatenate([dab, dac, dax, dq, dk, dv], axis=1)
    grad_c = _wgrad("ab_dwin", hn_ab, _spec((tw, D), lambda s, k: (k, 0)), dproj_ab, _spec((tw, 768), lambda s, k: (k, s)),
                    D, 768, t, tw, D, 0, carry=share)
    if share is not None:
        grad_c, landed = grad_c
        shared("y", landed)
    dh, dhb, dn_mix0, _ = _dhn_norm("ab_dhn", [(dproj_ab, _spec((th, 768), lambda i, s=s: (i, s)),
                                                w["ab_in"], _spec((None, D, 768), lambda i, s=s: (s, 0, 0)))
                                               for s in range(NSH)],
                                    NT, h1, small, R_MIX, dh)
    last = {}

    def z1_send(buf_a, buf_b):
        last["z1"], rider = reduce_start("z1", dict(b=buf_b, c=grad_c))
        return rider

    def z2_send(buf_a, buf_b, _):
        last["z2"], rider = reduce_start("z2", dict(a=buf_a, b=buf_b))
        return rider

    dh, dhb, grad_a, grad_b, dn_pre0, landed, landed_own = _ffn_backward(
        "pre0", dh, dhb, *pre0, w["pre0_gu"], w["pre0_d"], *FFN_SLOTS["pre0"], small, R_PRE, grad_a, grad_b,
        z1_send if on else None, z2_send if on else None)
    if place is not None:
        mine = (_owner_sum("grad_owner_sum_z1", last["z1"], landed, place[1])
                + _owner_sum("grad_owner_sum_z2", last["z2"], landed_own, place[1]))
        reduced.update(zip(REDUCE_STAGES["z1"] + REDUCE_STAGES["z2"], _alone("grad_share_z", _Share(mine))))
    zero = jnp.zeros((1, D), F32)
    conv_rows = jnp.pad(jnp.transpose(d_conv[:, :3, :], (1, 0, 2)).reshape(3, 512), ((0, 0), (0, D - 512)))
    small_grad = jnp.concatenate([
        dn_pre0, dn_pre1, dn_mix0, dn_mix1, dn_post0, dn_post1, d_clb, d_fin,
        jnp.pad(d_gam, ((0, 0), (0, D - HD))), conv_rows,
        jnp.pad(loss, ((0, 0), (0, D - 1))), zero, zero], axis=0)
    return dh, grad_a, grad_b, grad_c, small_grad, reduced


def _small_slab(rows):
    parts = [jnp.pad(r.astype(F32), ((0, 0), (0, D - r.shape[1]))) for r in rows]
    slab = jnp.concatenate(parts, axis=0)
    return jnp.pad(slab, ((0, SMALL_ROWS - slab.shape[0]), (0, 0)))


def kernel(x, ffn_pre_norm, ffn_pre_w_gate, ffn_pre_w_up, ffn_pre_w_down, mix_norm, ffn_post_norm, ffn_post_w_gate, ffn_post_w_up, ffn_post_w_down, ab_w_in, ab_conv_w, ab_w_out, c_w_in, c_lower_bounds, c_out_norm, c_w_out, final_norm, loss_target, m_ffn_pre_norm, m_ffn_pre_w_gate, m_ffn_pre_w_up, m_ffn_pre_w_down, m_mix_norm, m_ffn_post_norm, m_ffn_post_w_gate, m_ffn_post_w_up, m_ffn_post_w_down, m_ab_w_in, m_ab_conv_w, m_ab_w_out, m_c_w_in, m_c_lower_bounds, m_c_out_norm, m_c_w_out, m_final_norm, v_ffn_pre_norm, v_ffn_pre_w_gate, v_ffn_pre_w_up, v_ffn_pre_w_down, v_mix_norm, v_ffn_post_norm, v_ffn_post_w_gate, v_ffn_post_w_up, v_ffn_post_w_down, v_ab_w_in, v_ab_conv_w, v_ab_w_out, v_c_w_in, v_c_lower_bounds, v_c_out_norm, v_c_w_out, v_final_norm):
    t = x.shape[1]
    xi, yi, ci = lax.axis_index("x"), lax.axis_index("y"), lax.axis_index("c")
    p_idx = (2 * xi + yi).astype(jnp.int32).reshape(1)
    c_idx = ci.astype(jnp.int32).reshape(1)

    def halves(m):
        return m.astype(BF16).reshape(2, m.shape[0] // 2, m.shape[1])

    transposed = ("ffn_pre_w_gate", "ffn_pre_w_up", "ffn_post_w_gate", "ffn_post_w_up")

    def flip(a):
        return jnp.swapaxes(a, 1, 2)

    shards = {}
    for name, (w_gate, w_up, w_down, layer) in dict(
            pre0=(ffn_pre_w_gate, ffn_pre_w_up, ffn_pre_w_down, 0), post0=(ffn_post_w_gate, ffn_post_w_up, ffn_post_w_down, 0),
            pre1=(ffn_pre_w_gate, ffn_pre_w_up, ffn_pre_w_down, 1), post1=(ffn_post_w_gate, ffn_post_w_up, ffn_post_w_down, 1)).items():
        shards[name + "_gu"] = jnp.stack([flip(w_gate)[layer], flip(w_up)[layer]]).astype(BF16)
        shards[name + "_d"] = halves(w_down[layer])
    conv_pad = jnp.pad(ab_conv_w[0], ((0, 5), (0, 0)))
    shards.update(ab_in=halves(ab_w_in[0]), ab_out=halves(ab_w_out[0]), c_in=halves(c_w_in[0]), c_out=halves(c_w_out[0]),
                  conv=jnp.stack([conv_pad, jnp.zeros_like(conv_pad)]))
    small = _small_slab([ffn_pre_norm, mix_norm, ffn_post_norm, c_lower_bounds, final_norm.reshape(1, D), c_out_norm])
    small = small.reshape(SMALL_ROWS, 1, D)

    grad_x, _, _, _, small_grad, reduced = _local_step(x[0], loss_target[0], {}, small, shards,
                                                        (c_idx, p_idx))
    whole = {n: g.reshape(2 * g.shape[1], g.shape[2]) for n, g in reduced.items()}
    small_sum = _sum_small(small_grad)

    my_conv = lax.dynamic_slice(small_sum[R_CONV:R_CONV + 3], (0, (2 * xi + yi) * 128), (3, 128))
    grads = {
        "ffn_pre_norm": small_sum[R_PRE:R_PRE + 2], "mix_norm": small_sum[R_MIX:R_MIX + 2],
        "ffn_post_norm": small_sum[R_POST:R_POST + 2], "c_lower_bounds": small_sum[R_CLB:R_CLB + 2],
        "c_out_norm": small_sum[R_GAM:R_GAM + 1, :HD], "final_norm": small_sum[R_FIN],
        "ab_conv_w": my_conv.reshape(1, 3, 128),
    }
    layers = dict(ab_w_in=[whole["ab_in"]], ab_w_out=[whole["ab_out"]], c_w_in=[whole["c_in"]], c_w_out=[whole["c_out"]])
    for kind, key in (("gate", "_g"), ("up", "_u"), ("down", "_d")):
        layers["ffn_pre_w_" + kind] = [whole["pre0" + key], whole["pre1" + key]]
        layers["ffn_post_w_" + kind] = [whole["post0" + key], whole["post1" + key]]
    weights = dict(ffn_pre_norm=ffn_pre_norm, ffn_pre_w_gate=ffn_pre_w_gate, ffn_pre_w_up=ffn_pre_w_up, ffn_pre_w_down=ffn_pre_w_down, mix_norm=mix_norm, ffn_post_norm=ffn_post_norm, ffn_post_w_gate=ffn_post_w_gate, ffn_post_w_up=ffn_post_w_up, ffn_post_w_down=ffn_post_w_down, ab_w_in=ab_w_in, ab_conv_w=ab_conv_w, ab_w_out=ab_w_out, c_w_in=c_w_in, c_lower_bounds=c_lower_bounds, c_out_norm=c_out_norm, c_w_out=c_w_out, final_norm=final_norm)
    m_in = dict(ffn_pre_norm=m_ffn_pre_norm, ffn_pre_w_gate=m_ffn_pre_w_gate, ffn_pre_w_up=m_ffn_pre_w_up, ffn_pre_w_down=m_ffn_pre_w_down, mix_norm=m_mix_norm, ffn_post_norm=m_ffn_post_norm, ffn_post_w_gate=m_ffn_post_w_gate, ffn_post_w_up=m_ffn_post_w_up, ffn_post_w_down=m_ffn_post_w_down, ab_w_in=m_ab_w_in, ab_conv_w=m_ab_conv_w, ab_w_out=m_ab_w_out, c_w_in=m_c_w_in, c_lower_bounds=m_c_lower_bounds, c_out_norm=m_c_out_norm, c_w_out=m_c_w_out, final_norm=m_final_norm)
    v_in = dict(ffn_pre_norm=v_ffn_pre_norm, ffn_pre_w_gate=v_ffn_pre_w_gate, ffn_pre_w_up=v_ffn_pre_w_up, ffn_pre_w_down=v_ffn_pre_w_down, mix_norm=v_mix_norm, ffn_post_norm=v_ffn_post_norm, ffn_post_w_gate=v_ffn_post_w_gate, ffn_post_w_up=v_ffn_post_w_up, ffn_post_w_down=v_ffn_post_w_down, ab_w_in=v_ab_w_in, ab_conv_w=v_ab_conv_w, ab_w_out=v_ab_w_out, c_w_in=v_c_w_in, c_lower_bounds=v_c_lower_bounds, c_out_norm=v_c_out_norm, c_w_out=v_c_w_out, final_norm=v_final_norm)
    names = list(weights)
    big = [n for n in names if weights[n].size >= 65536]
    tiny = [n for n in names if n not in big]

    delta, new_m, new_v = {}, {}, {}
    for n in big:
        turn = flip if n in transposed else (lambda a: a)
        shape = turn(weights[n]).shape
        two_d = (shape[0] * shape[1], shape[2])
        d, m2, v2, g2 = _adamw("adamw_" + n, turn(weights[n]).reshape(two_d), layers[n],
                               turn(m_in[n]).reshape(two_d), turn(v_in[n]).reshape(two_d))
        delta[n], new_m[n], new_v[n] = turn(d.reshape(shape)), turn(m2.reshape(shape)), turn(v2.reshape(shape))
        grads[n] = turn(g2.reshape(shape))

    def tiny_slab(src):
        return _small_slab([src[n].reshape(-1, src[n].shape[-1]) for n in tiny])

    offs, row = {}, 0
    for n in tiny:
        nrows = weights[n].size // weights[n].shape[-1]
        offs[n] = (row, nrows)
        row += nrows
    d, m2, v2, _ = _adamw("adamw_small", tiny_slab(weights), [tiny_slab(grads)], tiny_slab(m_in), tiny_slab(v_in))
    for n in tiny:
        r0, nr = offs[n]
        shape = weights[n].shape
        for dst, src in ((delta, d), (new_m, m2), (new_v, v2)):
            dst[n] = src[r0:r0 + nr, :shape[-1]].reshape(shape)

    loss = small_sum[R_LOSS, 0]
    return (loss, grad_x.reshape(1, t, D), *[grads[n] for n in names], *[delta[n] for n in names],
            *[new_m[n] for n in names], *[new_v[n] for n in names])
```

```python
import functools
import math

import jax
import jax.numpy as jnp
from jax import lax
from jax.experimental import pallas as pl
from jax.experimental.pallas import tpu as pltpu

F32 = jnp.float32
BF16 = jnp.bfloat16
MESH = pl.DeviceIdType.MESH
ANY = pl.BlockSpec(memory_space=pl.ANY)

D = 1024
FS = 704
NSH = 4
RMS_EPS = 1e-6
MACARON = 0.5
CHUNK = 64
HD = 128
HGRN_HPS = 8
SBQ = 128
SB_PAIRS_FWD = 4
SB_PAIRS_BWD = 4
SB_DEAD = -105.0
CONV_HALO = 8
LANE = 128
ROW_TILE = 1024
MM_TILE = 1024
HALF_TILE = MM_TILE // 2
WGRAD_TILE = 4096
VMEM_LIMIT = 48 * 1024 * 1024
VMEM_LIMIT_BIG = 58 * 1024 * 1024

ADAM_LR, ADAM_B1, ADAM_B2, ADAM_EPS, ADAM_WD, ADAM_STEP = 0.001, 0.9, 0.999, 1e-08, 0.01, 10

NN = ((1,), (0,))
NT = ((1,), (1,))
TN = ((0,), (0,))

SMALL_ROWS = 16
R_PRE, R_MIX, R_POST, R_CLB, R_FIN, R_GAM, R_CONV, R_LOSS = 0, 2, 4, 6, 8, 9, 10, 13

B_ABOUT = 4 * FS
B_CIN = B_ABOUT + 256
B_COUT = B_CIN + 1024
B_ROWS = B_COUT + 256


def _dg(a, b, dims):
    return lax.dot_general(a, b, (dims, ((), ())), preferred_element_type=F32)


def _split(x):
    hi = x.astype(BF16)
    lo = (x - hi.astype(F32)).astype(BF16)
    return hi, lo


def _dot3(a, b, dims):
    ah, al = _split(a)
    bh, bl = _split(b)
    if dims == TN:
        n = b.shape[1]
        both = _dg(ah, jnp.concatenate([bh, bl], axis=1), dims)
        return both[:, :n] + both[:, n:] + _dg(al, bh, dims)
    m = a.shape[0]
    both = _dg(jnp.concatenate([ah, al], axis=0), bh, dims)
    return both[:m] + both[m:] + _dg(ah, bl, dims)


def _sigmoid(x):
    return 1.0 / (1.0 + jnp.exp(-x))


def _params(sem):
    return pltpu.CompilerParams(dimension_semantics=sem, vmem_limit_bytes=VMEM_LIMIT)


def _spec(shape, imap):
    return pl.BlockSpec(shape, imap)


def _accumulate(acc_ref, pairs, dims):
    part = None
    for a_ref, b_ref in pairs:
        d = _dg(a_ref[...], b_ref[...], dims)
        part = d if part is None else part + d
    acc_ref[...] += part


def _mm(name, pairs, *, grid, o_shape, o_dtype, o_spec, dims, kaxis, nk, acc_shape=None, res=None, scale=None,
        into=None, carry=None, norm=None):
    npairs = len(pairs)
    operands, specs = [], []
    for a, a_spec, b, b_spec in pairs:
        operands += [a, b]
        specs += [a_spec, b_spec]
    if res is not None:
        operands.append(res[0])
        specs.append(res[1])
    aliases = {}
    if into is not None:
        aliases = {len(operands): 0}
        operands.append(into)
        specs.append(ANY)
    if norm is not None:
        operands.append(norm[0])
        specs.append(_spec((None, 1, D), lambda *_: (norm[1], 0, 0)))
    n_own = len(operands)
    n_out = 2 if norm is not None else 1
    nc = carry.n if carry is not None else 0
    if nc:
        operands += carry.groups
        specs += carry.in_specs

    def body(*refs):
        o_ref = refs[n_own + nc]
        riders = ((refs[n_own:n_own + nc], refs[n_own + nc + n_out:n_own + 2 * nc + n_out], refs[-2], refs[-1])
                  if nc else None)
        if nc:
            carry.ride(grid, riders, "start")

        def finish(val):
            if scale is not None:
                val = val * scale
            if res is not None:
                val = val + refs[2 * npairs][...]
            o_ref[...] = val.astype(o_dtype)
            if norm is not None:
                r = lax.rsqrt(jnp.mean(val * val, axis=-1, keepdims=True) + RMS_EPS)
                refs[n_own + nc + 1][...] = (val * r * refs[n_own - 1][...]).astype(BF16)

        if nk == 1:
            part = None
            for n in range(npairs):
                d = _dg(refs[2 * n][...], refs[2 * n + 1][...], dims)
                part = d if part is None else part + d
            finish(part)
        else:
            acc_ref = refs[n_own + 2 * nc + n_out]
            k = pl.program_id(kaxis)

            @pl.when(k == 0)
            def _():
                acc_ref[...] = jnp.zeros_like(acc_ref)

            _accumulate(acc_ref, [(refs[2 * n], refs[2 * n + 1]) for n in range(npairs)], dims)

            @pl.when(k == nk - 1)
            def _():
                finish(acc_ref[...])
        if nc:
            carry.ride(grid, riders, "finish")

    sem = tuple("arbitrary" if (nc or (ax == kaxis and nk > 1)) else "parallel" for ax in range(len(grid)))
    outs = pl.pallas_call(
        body, name=name, grid=grid, in_specs=specs,
        out_specs=[o_spec] * n_out + (carry.out_specs if nc else []),
        out_shape=[jax.ShapeDtypeStruct(o_shape, o_dtype)] + ([jax.ShapeDtypeStruct(o_shape, BF16)] if norm is not None else [])
        + (carry.out_shape if nc else []),
        scratch_shapes=([pltpu.VMEM(acc_shape, F32)] if nk > 1 else []) + (carry.scratch if nc else []),
        input_output_aliases=aliases,
        compiler_params=_params(sem),
    )(*operands)
    main = (outs[0], outs[1]) if norm is not None else outs[0]
    return (main, carry.place(outs[n_out:])) if nc else main


def _norm_fwd(name, h, gain_slab, row, rider=None):
    t = h.shape[0]
    tm = min(ROW_TILE, t)
    grid = (t // tm,)
    nr = rider.n if rider is not None else 0

    def body(*refs):
        h_ref, g_ref, o_ref = refs[0], refs[1], refs[2 + nr]
        riders = (refs[2:2 + nr], refs[3 + nr:3 + 2 * nr], refs[-2], refs[-1]) if nr else None
        if nr:
            rider.ride(grid, riders, "start")
        x = h_ref[...]
        r = lax.rsqrt(jnp.mean(x * x, axis=-1, keepdims=True) + RMS_EPS)
        o_ref[...] = (x * r * g_ref[...]).astype(BF16)
        if nr:
            rider.ride(grid, riders, "finish")

    outs = pl.pallas_call(
        body, name=name, grid=grid,
        in_specs=[_spec((tm, D), lambda i: (i, 0)), _spec((None, 1, D), lambda i: (row, 0, 0))]
        + (rider.in_specs if nr else []),
        out_specs=[_spec((tm, D), lambda i: (i, 0))] + (rider.out_specs if nr else []),
        out_shape=[jax.ShapeDtypeStruct((t, D), BF16)] + (rider.out_shape if nr else []),
        scratch_shapes=rider.scratch if nr else [],
        compiler_params=_params(("arbitrary",) if nr else ("parallel",)),
    )(h, gain_slab, *(rider.groups if nr else []))
    return outs[0], (rider.place(outs[1:]) if nr else [])


def _dhn_norm(name, pairs, dims, h, gain_slab, row, dres, rider=None):
    t = h.shape[0]
    tm = min(HALF_TILE, t)
    nt = t // tm
    grid = (nt,)
    npairs = len(pairs)
    nr = rider.n if rider is not None else 0
    operands, specs = [], []
    for a, a_spec, b, b_spec in pairs:
        operands += [a, b]
        specs += [a_spec, b_spec]
    row_spec = _spec((tm, D), lambda i: (i, 0))
    operands += [h, gain_slab, dres]
    specs += [row_spec, _spec((None, 1, D), lambda i: (row, 0, 0)), row_spec]
    n_own = len(operands)

    def body(*refs):
        h_ref, g_ref, dres_ref = refs[2 * npairs:n_own]
        dh_ref, dhb_ref, dg_ref = refs[n_own + nr:n_own + nr + 3]
        gacc_ref = refs[n_own + 2 * nr + 3]
        riders = (refs[n_own:n_own + nr], refs[n_own + nr + 3:n_own + 2 * nr + 3], refs[-2], refs[-1]) if nr else None
        if nr:
            rider.ride(grid, riders, "start")
        i = pl.program_id(0)
        dy = None
        for n in range(npairs):
            d = _dg(refs[2 * n][...], refs[2 * n + 1][...], dims)
            dy = d if dy is None else dy + d
        x = h_ref[...]
        r = lax.rsqrt(jnp.mean(x * x, axis=-1, keepdims=True) + RMS_EPS)
        xh = x * r
        gdy = dy * g_ref[...]
        dh = dres_ref[...] + (gdy - xh * jnp.mean(gdy * xh, axis=-1, keepdims=True)) * r
        dh_ref[...] = dh
        dhb_ref[...] = dh.astype(BF16)
        gpart = jnp.sum((dy * xh).reshape(tm // 8, 8, D), axis=0)

        @pl.when(i == 0)
        def _():
            gacc_ref[...] = gpart

        @pl.when(i > 0)
        def _():
            gacc_ref[...] += gpart

        @pl.when(i == nt - 1)
        def _():
            dg_ref[...] = jnp.sum(gacc_ref[...], axis=0, keepdims=True)

        if nr:
            rider.ride(grid, riders, "finish")

    outs = pl.pallas_call(
        body, name=name, grid=grid, in_specs=specs + (rider.in_specs if nr else []),
        out_specs=[row_spec, row_spec, _spec((1, D), lambda i: (0, 0))] + (rider.out_specs if nr else []),
        out_shape=[jax.ShapeDtypeStruct((t, D), F32), jax.ShapeDtypeStruct((t, D), BF16),
                   jax.ShapeDtypeStruct((1, D), F32)] + (rider.out_shape if nr else []),
        scratch_shapes=[pltpu.VMEM((8, D), F32)] + (rider.scratch if nr else []),
        compiler_params=pltpu.CompilerParams(dimension_semantics=("arbitrary",),
                                             vmem_limit_bytes=VMEM_LIMIT_BIG),
    )(*operands, *(rider.groups if nr else []))
    return outs[0], outs[1], outs[2], (rider.place(outs[3:]) if nr else [])


def _final_loss(h, gain_slab, target):
    t = h.shape[0]
    tm = min(ROW_TILE, t)
    nt = t // tm

    def body(h_ref, g_ref, t_ref, dh_ref, dhb_ref, dg_ref, loss_ref, acc_ref, lacc_ref):
        i = pl.program_id(0)
        x = h_ref[...]
        g = g_ref[...]
        r = lax.rsqrt(jnp.mean(x * x, axis=-1, keepdims=True) + RMS_EPS)
        xh = x * r
        err = xh * g - t_ref[...]
        dy = err * (1.0 / D)
        gdy = dy * g
        dh = (gdy - xh * jnp.mean(gdy * xh, axis=-1, keepdims=True)) * r
        dh_ref[...] = dh
        dhb_ref[...] = dh.astype(BF16)
        part = jnp.sum((dy * xh).reshape(tm // 8, 8, D), axis=0)
        lpart = jnp.sum((err * err).reshape(tm // 8, 8, D), axis=0)

        @pl.when(i == 0)
        def _():
            acc_ref[...] = part
            lacc_ref[...] = lpart

        @pl.when(i > 0)
        def _():
            acc_ref[...] += part
            lacc_ref[...] += lpart

        @pl.when(i == nt - 1)
        def _():
            dg_ref[...] = jnp.sum(acc_ref[...], axis=0, keepdims=True)
            rows = jnp.sum(lacc_ref[...], axis=0, keepdims=True)
            loss_ref[...] = jnp.sum(rows, axis=1, keepdims=True) * (0.5 / D)

    row_spec = _spec((tm, D), lambda i: (i, 0))
    return pl.pallas_call(
        body, name="final_loss", grid=(nt,),
        in_specs=[row_spec, _spec((None, 1, D), lambda i: (R_FIN, 0, 0)), row_spec],
        out_specs=[row_spec, row_spec, _spec((1, D), lambda i: (0, 0)), _spec((1, 1), lambda i: (0, 0))],
        out_shape=[jax.ShapeDtypeStruct((t, D), F32), jax.ShapeDtypeStruct((t, D), BF16),
                   jax.ShapeDtypeStruct((1, D), F32), jax.ShapeDtypeStruct((1, 1), F32)],
        scratch_shapes=[pltpu.VMEM((8, D), F32), pltpu.VMEM((8, D), F32)],
        compiler_params=_params(("arbitrary",)),
    )(h, gain_slab, target)


def _ffn_up(name, hn, wgu, carry=None):
    t = hn.shape[0]
    tm = min(MM_TILE, t)
    grid = (NSH, t // tm)
    nc = carry.n if carry is not None else 0

    def body(*refs):
        x_ref, wg_ref, wu_ref = refs[:3]
        s_up_ref, s_gate_ref, a_ref = refs[3 + nc:6 + nc]
        riders = (refs[3:3 + nc], refs[6 + nc:6 + 2 * nc], refs[-2], refs[-1]) if nc else None
        if nc:
            carry.ride(grid, riders, "start")
        x = x_ref[...]
        g = _dg(x, wg_ref[...], NT)
        u = _dg(x, wu_ref[...], NT)
        sg = _sigmoid(g)
        silu = g * sg
        s_up_ref[...] = (MACARON * silu).astype(BF16)
        s_gate_ref[...] = (MACARON * u * (sg * (1.0 + g * (1.0 - sg)))).astype(BF16)
        a_ref[...] = (silu * u).astype(BF16)
        if nc:
            carry.ride(grid, riders, "finish")

    act = _spec((None, tm, FS), lambda s, i: (s, i, 0))
    shape = jax.ShapeDtypeStruct((NSH, t, FS), BF16)
    outs = pl.pallas_call(
        body, name=name, grid=grid,
        in_specs=[_spec((tm, D), lambda s, i: (i, 0)),
                  _spec((None, None, FS, D), lambda s, i: (s, 0, 0, 0)),
                  _spec((None, None, FS, D), lambda s, i: (s, 1, 0, 0))] + (carry.in_specs if nc else []),
        out_specs=[act, act, act] + (carry.out_specs if nc else []),
        out_shape=[shape, shape, shape] + (carry.out_shape if nc else []),
        scratch_shapes=carry.scratch if nc else [],
        compiler_params=_params(("arbitrary", "arbitrary") if nc else ("parallel", "parallel")),
    )(hn, wgu, wgu, *(carry.groups if nc else []))
    return (outs[0], outs[1], outs[2], carry.place(outs[3:]) if nc else [])


def _ffn_down(name, a, wd, h, carry=None, norm=None):
    t = h.shape[0]
    tm = min(HALF_TILE, t)
    row_spec = _spec((tm, D), lambda i: (i, 0))
    return _mm(name, [(a, _spec((None, tm, FS), lambda i, s=s: (s, i, 0)), wd, _spec((None, FS, D), lambda i, s=s: (s, 0, 0)))
                      for s in range(NSH)],
               grid=(t // tm,), o_shape=(t, D), o_dtype=F32, o_spec=row_spec, dims=NN, kaxis=0, nk=1,
               res=(h, row_spec), scale=MACARON, carry=carry, norm=norm)


def _ab_proj(hn, w_in, carry=None):
    t = hn.shape[0]
    tm = min(HALF_TILE, t)
    grid = (t // tm,)
    cols = w_in.shape[2]
    nc = carry.n if carry is not None else 0

    def body(*refs):
        x_ref, w_ref = refs[:2]
        a_ref, b_ref = refs[2 + nc:4 + nc]
        riders = (refs[2:2 + nc], refs[4 + nc:4 + 2 * nc], refs[-2], refs[-1]) if nc else None
        if nc:
            carry.ride(grid, riders, "start")
        x = x_ref[...]
        for s in range(2):
            a_ref[:, s * cols:(s + 1) * cols] = _dg(x, w_ref[s], NN)
            b_ref[:, s * cols:(s + 1) * cols] = _dg(x, w_ref[s + 2], NN).astype(BF16)
        if nc:
            carry.ride(grid, riders, "finish")

    out_spec = _spec((tm, 2 * cols), lambda i: (i, 0))
    outs = pl.pallas_call(
        body, name="ab_proj", grid=grid,
        in_specs=[_spec((tm, D), lambda i: (i, 0)), _spec((NSH, D, cols), lambda i: (0, 0, 0))]
        + (carry.in_specs if nc else []),
        out_specs=[out_spec, out_spec] + (carry.out_specs if nc else []),
        out_shape=[jax.ShapeDtypeStruct((t, 2 * cols), F32), jax.ShapeDtypeStruct((t, 2 * cols), BF16)]
        + (carry.out_shape if nc else []),
        scratch_shapes=carry.scratch if nc else [],
        compiler_params=_params(("arbitrary",) if nc else ("parallel",)),
    )(hn, w_in, *(carry.groups if nc else []))
    main = (outs[0], outs[1])
    return (main, carry.place(outs[2:])) if nc else main


def _ffn_bwd_up(name, dhb, wd, s_up, s_gate, rider=None):
    t = dhb.shape[0]
    tm = min(2 * MM_TILE, t)
    grid = (t // tm, NSH)
    nr = rider.n if rider is not None else 0

    def body(*refs):
        dh_ref, wd_ref, s_up_ref, s_gate_ref = refs[:4]
        dg_ref, du_ref = refs[4 + nr:6 + nr]
        riders = (refs[4:4 + nr], refs[6 + nr:6 + 2 * nr], refs[-2], refs[-1]) if nr else None
        if nr:
            rider.ride(grid, riders, "start")
        da = _dg(dh_ref[...], wd_ref[...], NT).astype(BF16)
        du_ref[...] = da * s_up_ref[...]
        dg_ref[...] = da * s_gate_ref[...]
        if nr:
            rider.ride(grid, riders, "finish")

    act = _spec((None, tm, FS), lambda i, s: (s, i, 0))
    shape = jax.ShapeDtypeStruct((NSH, t, FS), BF16)
    outs = pl.pallas_call(
        body, name=name, grid=grid,
        in_specs=[_spec((tm, D), lambda i, s: (i, 0)), _spec((None, FS, D), lambda i, s: (s, 0, 0)), act, act]
        + (rider.in_specs if nr else []),
        out_specs=[act, act] + (rider.out_specs if nr else []),
        out_shape=[shape, shape] + (rider.out_shape if nr else []),
        scratch_shapes=rider.scratch if nr else [],
        compiler_params=pltpu.CompilerParams(
            dimension_semantics=("arbitrary", "arbitrary") if nr else ("parallel", "parallel"),
            vmem_limit_bytes=VMEM_LIMIT_BIG),
    )(dhb, wd, s_up, s_gate, *(rider.groups if nr else []))
    return outs[0], outs[1], (rider.place(outs[2:]) if nr else [])


def _wgrad(name, a, a_spec, b, b_spec, out_rows, out_cols, t, tt, group, slot, scale=None, carry=None):
    first = isinstance(group, int)
    rows = group if first else group.shape[1]
    return _mm(name, [(a, a_spec, b, b_spec)], grid=(NSH, t // tt),
               o_shape=(NSH, rows, out_cols), o_dtype=BF16,
               o_spec=_spec((None, out_rows, out_cols), lambda s, k: (s, slot, 0)),
               dims=TN, kaxis=1, nk=t // tt, acc_shape=(out_rows, out_cols), scale=scale,
               into=None if first else group, carry=carry)


def _wgrad_gate_up(name, dg, du, hn, group, gate_idx):
    t = hn.shape[0]
    first = isinstance(group, int)
    rows = group if first else group.shape[1]

    def body(dg_ref, du_ref, hn_ref, *refs):
        o_ref = refs[-1]
        x = hn_ref[...]
        o_ref[...] = jnp.concatenate([_dg(dg_ref[...], x, TN), _dg(du_ref[...], x, TN)], axis=0).astype(BF16)

    hid = _spec((None, t, FS), lambda s: (s, 0, 0))
    return pl.pallas_call(
        body, name=name, grid=(NSH,),
        in_specs=[hid, hid, _spec((t, D), lambda s: (0, 0))] + ([] if first else [ANY]),
        out_specs=_spec((None, 2 * FS, D), lambda s: (s, gate_idx // 2, 0)),
        out_shape=jax.ShapeDtypeStruct((NSH, rows, D), BF16),
        input_output_aliases={} if first else {3: 0},
        compiler_params=pltpu.CompilerParams(dimension_semantics=("parallel",), vmem_limit_bytes=VMEM_LIMIT_BIG),
    )(dg, du, hn, *([] if first else [group]))


def _ffn_backward(tag, dh, dhb, h_in, hn, s_up, s_gate, a, wgu, wd, gate_idx, up_idx, down_idx, small, norm_row,
                  grad_a, grad_b, rider_up=None, rider_dhn=None):
    t = dh.shape[0]
    tt = min(WGRAD_TILE, t)
    tok = _spec((tt, D), lambda s, k: (k, 0))
    hid = _spec((None, tt, FS), lambda s, k: (s, k, 0))
    grad_b = _wgrad(tag + "_dwd", a, hid, dhb, tok, FS, D, t, tt, grad_b, down_idx, scale=MACARON)
    rider = rider_up(grad_a, grad_b) if rider_up is not None else None
    dg, du, landed_up = _ffn_bwd_up(tag + "_bwd_up", dhb, wd, s_up, s_gate, rider)
    assert up_idx == gate_idx + 1 and gate_idx % 2 == 0
    grad_a = _wgrad_gate_up(tag + "_dwgu", dg, du, hn, grad_a, gate_idx)
    rider = rider_dhn(grad_a, grad_b, landed_up) if rider_dhn is not None else None
    th = min(HALF_TILE, t)
    pairs = []
    for s in range(NSH):
        act = _spec((None, th, FS), lambda i, s=s: (s, i, 0))
        pairs += [(dg, act, wgu, _spec((None, None, FS, D), lambda i, s=s: (s, 0, 0, 0))),
                  (du, act, wgu, _spec((None, None, FS, D), lambda i, s=s: (s, 1, 0, 0)))]
    dh_in, dhb_in, d_gain, landed_dhn = _dhn_norm(tag + "_dhn", pairs, NN, h_in, small, norm_row, dh, rider)
    return dh_in, dhb_in, grad_a, grad_b, d_gain, landed_up, landed_dhn


def _conv_fwd(proj_a, conv_w):
    t = proj_a.shape[0]
    tm = min(ROW_TILE, t)
    hb = tm // CONV_HALO

    def body(ab_ref, ac_ref, ax_ref, acp_ref, axp_ref, w_ref, y_ref):
        i = pl.program_id(1)
        u = ac_ref[...] * ax_ref[...]
        up = jnp.where(i > 0, acp_ref[...] * axp_ref[...], 0.0)
        ext = jnp.concatenate([up, u], axis=0)
        u1 = pltpu.roll(ext, 1, 0)[CONV_HALO:]
        u2 = pltpu.roll(ext, 2, 0)[CONV_HALO:]
        w = w_ref[...]
        conv = w[0:1] * u2 + w[1:2] * u1 + w[2:3] * u
        y_ref[...] = (ab_ref[...] * conv).astype(BF16)

    def cur(off):
        return _spec((tm, LANE), lambda j, i: (i, off + j))

    def prev(off):
        return _spec((CONV_HALO, LANE), lambda j, i: (jnp.maximum(i * hb - 1, 0), off + j))

    return pl.pallas_call(
        body, name="conv_fwd", grid=(4, t // tm),
        in_specs=[cur(0), cur(4), cur(8), prev(4), prev(8),
                  _spec((None, None, 8, LANE), lambda j, i: (j, 0, 0, 0))],
        out_specs=_spec((tm, LANE), lambda j, i: (i, j)),
        out_shape=jax.ShapeDtypeStruct((t, 512), BF16),
        compiler_params=_params(("parallel", "parallel")),
    )(proj_a, proj_a, proj_a, proj_a, proj_a, conv_w)


def _conv_bwd(proj_a, conv_w, dy, rider=None):
    t = proj_a.shape[0]
    tm = min(ROW_TILE, t)
    hb = tm // CONV_HALO
    nt = t // tm
    grid = (4, nt)
    nr = rider.n if rider is not None else 0

    def body(*refs):
        ab_ref, ac_ref, ax_ref, dy_ref, acp_ref, axp_ref, abn_ref, dyn_ref, w_ref = refs[:9]
        dab_ref, dac_ref, dax_ref, dw_ref = refs[9 + nr:13 + nr]
        acc_ref = refs[13 + 2 * nr]
        riders = (refs[9:9 + nr], refs[13 + nr:13 + 2 * nr], refs[-2], refs[-1]) if nr else None
        if nr:
            rider.ride(grid, riders, "start")
        i = pl.program_id(1)
        ab, ac, ax = ab_ref[...], ac_ref[...], ax_ref[...]
        u = ac * ax
        up = jnp.where(i > 0, acp_ref[...] * axp_ref[...], 0.0)
        ext = jnp.concatenate([up, u], axis=0)
        u1 = pltpu.roll(ext, 1, 0)[CONV_HALO:]
        u2 = pltpu.roll(ext, 2, 0)[CONV_HALO:]
        w = w_ref[...]
        conv = w[0:1] * u2 + w[1:2] * u1 + w[2:3] * u
        dy_v = dy_ref[...]
        dab_ref[...] = (dy_v * conv).astype(BF16)
        dc = dy_v * ab
        dcn = jnp.where(i < nt - 1, dyn_ref[...] * abn_ref[...], 0.0)
        extn = jnp.concatenate([dc, dcn], axis=0)
        n = tm + CONV_HALO
        dc1 = pltpu.roll(extn, n - 1, 0)[:tm]
        dc2 = pltpu.roll(extn, n - 2, 0)[:tm]
        du = w[2:3] * dc + w[1:2] * dc1 + w[0:1] * dc2
        dac_ref[...] = (du * ax).astype(BF16)
        dax_ref[...] = (du * ac).astype(BF16)
        rid = lax.broadcasted_iota(jnp.int32, (8, LANE), 0)
        part = jnp.where(rid == 0, jnp.sum(dc * u2, axis=0, keepdims=True),
                         jnp.where(rid == 1, jnp.sum(dc * u1, axis=0, keepdims=True),
                                   jnp.where(rid == 2, jnp.sum(dc * u, axis=0, keepdims=True), 0.0)))

        @pl.when(i == 0)
        def _():
            acc_ref[...] = part

        @pl.when(i > 0)
        def _():
            acc_ref[...] += part

        @pl.when(i == nt - 1)
        def _():
            dw_ref[...] = acc_ref[...]

        if nr:
            rider.ride(grid, riders, "finish")

    def cur(off):
        return _spec((tm, LANE), lambda j, i: (i, off + j))

    def prev(off):
        return _spec((CONV_HALO, LANE), lambda j, i: (jnp.maximum(i * hb - 1, 0), off + j))

    def nxt(off):
        return _spec((CONV_HALO, LANE), lambda j, i: (jnp.minimum((i + 1) * hb, nt * hb - 1), off + j))

    outs = pl.pallas_call(
        body, name="conv_bwd", grid=grid,
        in_specs=[cur(0), cur(4), cur(8), cur(0), prev(4), prev(8), nxt(0), nxt(0),
                  _spec((None, None, 8, LANE), lambda j, i: (j, 0, 0, 0))] + (rider.in_specs if nr else []),
        out_specs=[_spec((tm, LANE), lambda j, i: (i, j)), _spec((tm, LANE), lambda j, i: (i, j)),
                   _spec((tm, LANE), lambda j, i: (i, j)), _spec((None, 8, LANE), lambda j, i: (j, 0, 0))]
        + (rider.out_specs if nr else []),
        out_shape=[jax.ShapeDtypeStruct((t, 512), BF16), jax.ShapeDtypeStruct((t, 512), BF16),
                   jax.ShapeDtypeStruct((t, 512), BF16), jax.ShapeDtypeStruct((4, 8, LANE), F32)]
        + (rider.out_shape if nr else []),
        scratch_shapes=[pltpu.VMEM((8, LANE), F32)] + (rider.scratch if nr else []),
        compiler_params=_params(("arbitrary", "arbitrary") if nr else ("parallel", "arbitrary")),
    )(proj_a, proj_a, proj_a, dy, proj_a, proj_a, proj_a, dy, conv_w, *(rider.groups if nr else []))
    return outs[0], outs[1], outs[2], outs[3], (rider.place(outs[4:]) if nr else [])


def _log_sigmoid(z):
    return jnp.minimum(z, 0.0) - jnp.log(1.0 + jnp.exp(-jnp.abs(z)))


def _sb_masks():
    row = lax.broadcasted_iota(jnp.int32, (SBQ, SBQ), 0)
    col = lax.broadcasted_iota(jnp.int32, (SBQ, SBQ), 1)
    return row, col


def _ones_where(mask):
    return jnp.where(mask, 1.0, 0.0).astype(BF16)


def _head_mask(head):
    lane = lax.broadcasted_iota(jnp.int32, (1, LANE), 1)
    return lane >= 64 if head else lane < 64


def _sb_fwd(proj_b, carry=None):
    t = proj_b.shape[0]
    nq = t // SBQ
    scale = 1.0 / math.sqrt(64.0)

    npair = SB_PAIRS_FWD
    wide = npair * LANE
    ngrp = 4 // npair
    chains = [(p, head) for p in range(npair) for head in range(2)]

    grid = (ngrp, nq)
    nc = carry.n if carry is not None else 0

    def body(*refs):
        q_ref, k_ref, v_ref = refs[:3]
        y_ref, l_ref, n_ref = refs[3 + nc:6 + nc]
        riders = (refs[3:3 + nc], refs[6 + nc:6 + 2 * nc], refs[-2], refs[-1]) if nc else None
        if nc:
            carry.ride(grid, riders, "start")
        grp = pl.program_id(0)
        qi = pl.program_id(1)
        row, col = _sb_masks()
        m_suffix = _ones_where(row > col)
        rows = len(chains) * SBQ
        strict = (lax.broadcasted_iota(jnp.int32, (rows, SBQ), 1)
                  < (lax.broadcasted_iota(jnp.int32, (rows, SBQ), 0) & (SBQ - 1)))
        q_pair = []
        for p in range(npair):
            q_all = q_ref[:, p * LANE:(p + 1) * LANE]
            q_pair.append(jnp.concatenate([jnp.where(_head_mask(head), q_all, jnp.zeros_like(q_all)) for head in range(2)],
                                          axis=0))

        def block(kb, state, diag):
            run, acc = state
            start = pl.multiple_of(kb * SBQ, SBQ)
            z = jnp.concatenate([_dg(q_pair[p], k_ref[pl.ds(start, SBQ), p * LANE:(p + 1) * LANE], NT)
                                 for p in range(npair)], axis=0) * scale
            lb = _log_sigmoid(z)
            lk = lb - z
            if diag:
                lk = jnp.where(strict, lk, 0.0)
            hi, lo = _split(lk)
            sums = _dg(jnp.concatenate([hi, lo], axis=0), m_suffix, NN)
            w = jnp.exp(lb + (run + sums[:rows] + sums[rows:]))
            if diag:
                w = jnp.where(strict, w, 0.0)
            wb = w.astype(BF16)
            acc = acc + jnp.concatenate(
                [_dg(wb[2 * p * SBQ:2 * (p + 1) * SBQ], v_ref[pl.ds(start, SBQ), p * LANE:(p + 1) * LANE], NN)
                 for p in range(npair)], axis=0)
            run = run + jnp.sum(hi.astype(F32) + lo.astype(F32), axis=1, keepdims=True)
            return run, acc

        state = block(qi, (jnp.zeros((rows, 1), F32), jnp.zeros((rows, LANE), F32)), True)

        def live(c):
            return jnp.logical_and(c[0] < qi, jnp.max(c[1][0]) > SB_DEAD)

        def step(c):
            return c[0] + 1, block(qi - 1 - c[0], c[1], False)

        count, (run, acc) = lax.while_loop(live, step, (jnp.int32(0), state))
        n_ref[grp * nq + qi] = count.astype(F32)
        hm = _head_mask(0)
        for p in range(npair):
            lo_rows, hi_rows = slice(2 * p * SBQ, (2 * p + 1) * SBQ), slice((2 * p + 1) * SBQ, (2 * p + 2) * SBQ)
            y_ref[:, p * LANE:(p + 1) * LANE] = jnp.where(hm, acc[lo_rows], acc[hi_rows]).astype(BF16)
            l_ref[p] = jnp.where(hm, run[lo_rows], run[hi_rows])
        if nc:
            carry.ride(grid, riders, "finish")

    outs = pl.pallas_call(
        body, name="sb_fwd", grid=grid,
        in_specs=[_spec((SBQ, wide), lambda g, i: (i, g)),
                  _spec((t, wide), lambda g, i: (0, ngrp + g)),
                  _spec((t, wide), lambda g, i: (0, 2 * ngrp + g))] + (carry.in_specs if nc else []),
        out_specs=[_spec((SBQ, wide), lambda g, i: (i, g)), _spec((npair, SBQ, LANE), lambda g, i: (g, i, 0)),
                   pl.BlockSpec(memory_space=pltpu.SMEM)] + (carry.out_specs if nc else []),
        out_shape=[jax.ShapeDtypeStruct((t, 512), BF16), jax.ShapeDtypeStruct((4, t, LANE), F32),
                   jax.ShapeDtypeStruct((ngrp * nq,), F32)] + (carry.out_shape if nc else []),
        scratch_shapes=carry.scratch if nc else [],
        compiler_params=_params(("arbitrary", "arbitrary")),
    )(proj_b, proj_b, proj_b, *(carry.groups if nc else []))
    return outs[0], outs[1], outs[2], (carry.place(outs[3:]) if nc else [])


def _sb_bwd(proj_b, dy, ltot, nblk, rider=None):
    t = proj_b.shape[0]
    nq = t // SBQ
    scale = 1.0 / math.sqrt(64.0)

    npair = SB_PAIRS_BWD
    wide = npair * LANE
    ngrp = 4 // npair
    chains = [(p, head) for p in range(npair) for head in range(2)]
    grid = (ngrp, nq)
    nr = rider.n if rider is not None else 0
    per_count = SB_PAIRS_FWD // SB_PAIRS_BWD

    def body(*refs):
        q_ref, k_ref, v_ref, dy_ref, l_ref, n_ref = refs[:6]
        dq_ref, dk_ref, dv_ref = refs[6 + nr:9 + nr]
        dk_acc, dv_acc = refs[9 + 2 * nr:11 + 2 * nr]
        riders = (refs[6:6 + nr], refs[9 + nr:9 + 2 * nr], refs[-2], refs[-1]) if nr else None
        if nr:
            rider.ride(grid, riders, "start")
        grp = pl.program_id(0)
        qi = pl.program_id(1)

        @pl.when(qi == 0)
        def _():
            dk_acc[...] = jnp.zeros_like(dk_acc)
            dv_acc[...] = jnp.zeros_like(dv_acc)

        row, col = _sb_masks()
        m_prefix = _ones_where(row <= col)
        m_before = _ones_where(row < col)
        rows = len(chains) * SBQ
        strict = (lax.broadcasted_iota(jnp.int32, (rows, SBQ), 1)
                  < (lax.broadcasted_iota(jnp.int32, (rows, SBQ), 0) & (SBQ - 1)))
        q_pair, do_pair, ltot = [], [], []
        for p in range(npair):
            pl_ = slice(p * LANE, (p + 1) * LANE)
            q_all = q_ref[:, pl_]
            do_all = dy_ref[:, pl_].astype(BF16)
            q_pair.append(jnp.concatenate([jnp.where(_head_mask(h), q_all, jnp.zeros_like(q_all)) for h in range(2)], axis=0))
            do_pair.append(jnp.concatenate([jnp.where(_head_mask(h), do_all, jnp.zeros_like(do_all)) for h in range(2)], axis=0))
            ltot += [l_ref[p][:, h * 64:h * 64 + 1] for h in range(2)]
        ltot = jnp.concatenate(ltot, axis=0)

        def pair_rows(a, p):
            return a[2 * p * SBQ:2 * (p + 1) * SBQ]

        def block(kb, state, diag):
            seen, dseen, dq = state
            start = pl.multiple_of(kb * SBQ, SBQ)
            kk = [k_ref[pl.ds(start, SBQ), p * LANE:(p + 1) * LANE] for p in range(npair)]
            vv = [v_ref[pl.ds(start, SBQ), p * LANE:(p + 1) * LANE] for p in range(npair)]
            z = jnp.concatenate([_dg(q_pair[p], kk[p], NT) for p in range(npair)], axis=0) * scale
            lb = _log_sigmoid(z)
            lk = lb - z
            if diag:
                lk = jnp.where(strict, lk, 0.0)
            hi, lo = _split(lk)
            sums = _dg(jnp.concatenate([hi, lo], axis=0), m_prefix, NN)
            w = jnp.exp(lb + ((ltot - seen) - (sums[:rows] + sums[rows:])))
            if diag:
                w = jnp.where(strict, w, 0.0)
            wb = w.astype(BF16)
            da = w * jnp.concatenate([_dg(do_pair[p], vv[p], NT) for p in range(npair)], axis=0)
            dah, dal = _split(da)
            dsums = _dg(jnp.concatenate([dah, dal], axis=0), m_before, NN)
            sig = jnp.exp(lb)
            dz = (da * (1.0 - sig) - (dseen + dsums[:rows] + dsums[rows:]) * sig) * scale
            if diag:
                dz = jnp.where(strict, dz, 0.0)
            dzb = dz.astype(BF16)
            for p in range(npair):
                pl_ = slice(p * LANE, (p + 1) * LANE)
                dv_acc[pl.ds(start, SBQ), pl_] += _dg(pair_rows(wb, p), do_pair[p], TN)
                dk_acc[pl.ds(start, SBQ), pl_] += _dg(pair_rows(dzb, p), q_pair[p], TN)
            dq = dq + jnp.concatenate([_dg(pair_rows(dzb, p), kk[p], NN) for p in range(npair)], axis=0)
            seen = seen + jnp.sum(hi.astype(F32) + lo.astype(F32), axis=1, keepdims=True)
            dseen = dseen + jnp.sum(da, axis=1, keepdims=True)
            return seen, dseen, dq

        zero = (jnp.zeros((rows, 1), F32), jnp.zeros((rows, 1), F32), jnp.zeros((rows, LANE), F32))
        first = qi - n_ref[(grp // per_count) * nq + qi].astype(jnp.int32)
        state = lax.fori_loop(first, qi, lambda kb, c: block(kb, c, False), zero)
        _, _, dq = block(qi, state, True)
        for p in range(npair):
            dq_ref[:, p * LANE:(p + 1) * LANE] = jnp.where(
                _head_mask(0), dq[2 * p * SBQ:(2 * p + 1) * SBQ], dq[(2 * p + 1) * SBQ:(2 * p + 2) * SBQ]).astype(BF16)

        @pl.when(qi == nq - 1)
        def _():
            dk_ref[...] = dk_acc[...].astype(BF16)
            dv_ref[...] = dv_acc[...].astype(BF16)

        if nr:
            rider.ride(grid, riders, "finish")

    full = jax.ShapeDtypeStruct((t, 512), BF16)
    outs = pl.pallas_call(
        body, name="sb_bwd", grid=grid,
        in_specs=[_spec((SBQ, wide), lambda g, i: (i, g)),
                  _spec((t, wide), lambda g, i: (0, ngrp + g)),
                  _spec((t, wide), lambda g, i: (0, 2 * ngrp + g)),
                  _spec((SBQ, wide), lambda g, i: (i, ngrp + g)),
                  _spec((npair, SBQ, LANE), lambda g, i: (g, i, 0)),
                  pl.BlockSpec(memory_space=pltpu.SMEM)] + (rider.in_specs if nr else []),
        out_specs=[_spec((SBQ, wide), lambda g, i: (i, g)),
                   _spec((t, wide), lambda g, i: (0, g)), _spec((t, wide), lambda g, i: (0, g))]
        + (rider.out_specs if nr else []),
        out_shape=[full, full, full] + (rider.out_shape if nr else []),
        scratch_shapes=[pltpu.VMEM((t, wide), F32), pltpu.VMEM((t, wide), F32)] + (rider.scratch if nr else []),
        compiler_params=pltpu.CompilerParams(dimension_semantics=("arbitrary", "arbitrary"),
                                             vmem_limit_bytes=VMEM_LIMIT_BIG),
    )(proj_b, proj_b, proj_b, dy, ltot, nblk, *(rider.groups if nr else []))
    return outs[0], outs[1], outs[2], (rider.place(outs[3:]) if nr else [])


def _hgrn_gates(qr, fr, c0, c1):
    mx = jnp.maximum(c0, c1)
    e0, e1 = jnp.exp(c0 - mx), jnp.exp(c1 - mx)
    lb = e1 / (e0 + e1)
    sx = _sigmoid(fr)
    f = lb + (1.0 - lb) * sx
    k = (1.0 - lb) * (1.0 - sx)
    sq = _sigmoid(qr)
    return lb, sx, f, k, sq, qr * sq


def _chunk_sums(mask, x):
    n = x.shape[1]
    hi, lo = _split(x)
    both = _dg(_ones_where(mask), jnp.concatenate([hi, lo], axis=1), NN)
    return both[:, :n] + both[:, n:]


def _chunk_masks():
    row = lax.broadcasted_iota(jnp.int32, (CHUNK, CHUNK), 0)
    col = lax.broadcasted_iota(jnp.int32, (CHUNK, CHUNK), 1)
    return col <= row, col >= row


def _hgrn_fwd(proj_c, small, carry=None):
    t = proj_c.shape[1]
    nc = t // CHUNK
    nh = D // HD

    hps = HGRN_HPS
    wide = hps * HD
    grid = (nh // hps, nc)
    nr = carry.n if carry is not None else 0

    def body(*refs):
        p_ref, c0_ref, c1_ref, gam_ref = refs[:4]
        o_ref, y_ref, sst_ref = refs[4 + nr:7 + nr]
        st_ref = refs[7 + 2 * nr]
        riders = (refs[4:4 + nr], refs[7 + nr:7 + 2 * nr], refs[-2], refs[-1]) if nr else None
        if nr:
            carry.ride(grid, riders, "start")
        c = pl.program_id(1)

        @pl.when(c == 0)
        def _():
            st_ref[...] = jnp.zeros_like(st_ref)

        _, _, f_all, k_all, _, q_all = _hgrn_gates(p_ref[0], p_ref[1], c0_ref[...], c1_ref[...])
        tril, _ = _chunk_masks()
        b_all = _chunk_sums(tril, jnp.log(f_all))
        for j in range(hps):
            ln = slice(j * HD, (j + 1) * HD)
            st0 = st_ref[j]
            sst_ref[j] = st0
            v, g = p_ref[2, :, ln], p_ref[3, :, ln]
            q, k, b = q_all[:, ln], k_all[:, ln], b_all[:, ln]
            bm = b[CHUNK // 2 - 1:CHUNK // 2]
            bl = b[CHUNK - 1:CHUNK]
            qd = q * jnp.exp(b)
            qt = q * jnp.exp(b - bm)
            kt = k * jnp.exp(bm - b)
            kl = k * jnp.exp(bl - b)
            vb = v.astype(BF16)
            att = jnp.where(tril, _dot3(qt, kt, NT), 0.0)
            o = _dg(qd.astype(BF16), st0.astype(BF16), NT) + _dg(att.astype(BF16), vb, NN)
            st_ref[j] = st0 * jnp.exp(bl) + _dg(vb, kl.astype(BF16), TN)
            o_ref[:, ln] = o
            r = lax.rsqrt(jnp.mean(o * o, axis=-1, keepdims=True) + RMS_EPS)
            y_ref[:, ln] = (o * r * gam_ref[...] * (g * _sigmoid(g))).astype(BF16)
        if nr:
            carry.ride(grid, riders, "finish")

    outs = pl.pallas_call(
        body, name="hgrn_fwd", grid=grid,
        in_specs=[_spec((4, CHUNK, wide), lambda h, c: (0, c, h)),
                  _spec((None, 1, wide), lambda h, c: (R_CLB, 0, h)),
                  _spec((None, 1, wide), lambda h, c: (R_CLB + 1, 0, h)),
                  _spec((None, 1, HD), lambda h, c: (R_GAM, 0, 0))] + (carry.in_specs if nr else []),
        out_specs=[_spec((CHUNK, wide), lambda h, c: (c, h)), _spec((CHUNK, wide), lambda h, c: (c, h)),
                   _spec((None, hps, HD, HD), lambda h, c: (c, h, 0, 0))] + (carry.out_specs if nr else []),
        out_shape=[jax.ShapeDtypeStruct((t, D), F32), jax.ShapeDtypeStruct((t, D), BF16),
                   jax.ShapeDtypeStruct((nc, nh, HD, HD), F32)] + (carry.out_shape if nr else []),
        scratch_shapes=[pltpu.VMEM((hps, HD, HD), F32)] + (carry.scratch if nr else []),
        compiler_params=_params(("arbitrary", "arbitrary")),
    )(proj_c, small, small, small, *(carry.groups if nr else []))
    return outs[0], outs[1], outs[2], (carry.place(outs[3:]) if nr else [])


def _hgrn_bwd(proj_c, small, o, sst, dyc, rider=None):
    t = proj_c.shape[1]
    nc = t // CHUNK
    nh = D // HD

    hps = HGRN_HPS
    wide = hps * HD
    ng = nh // hps
    grid = (ng, nc)
    nr = rider.n if rider is not None else 0

    def body(*refs):
        p_ref, c0_ref, c1_ref, gam_ref, o_ref, sst_ref, dy_ref = refs[:7]
        dp_ref, dclb_ref, dgam_ref = refs[7 + nr:10 + nr]
        dst_ref, dlb_acc, dgam_acc = refs[10 + 2 * nr:13 + 2 * nr]
        riders = (refs[7:7 + nr], refs[10 + nr:10 + 2 * nr], refs[-2], refs[-1]) if nr else None
        if nr:
            rider.ride(grid, riders, "start")
        group = pl.program_id(0)
        step = pl.program_id(1)

        @pl.when(step == 0)
        def _():
            dst_ref[...] = jnp.zeros_like(dst_ref)
            dlb_acc[...] = jnp.zeros_like(dlb_acc)

        @pl.when((step == 0) & (group == 0))
        def _():
            dgam_acc[...] = jnp.zeros_like(dgam_acc)

        gam = gam_ref[...]
        qr_all = p_ref[0]
        lb_all, sx_all, f_all, k_all, sq_all, q_all = _hgrn_gates(qr_all, p_ref[1], c0_ref[...], c1_ref[...])
        tril, triu = _chunk_masks()
        b_all = _chunk_sums(tril, jnp.log(f_all))
        dq_parts, dk_parts, db_parts, dgam_parts = [], [], [], []
        for j in range(hps):
            ln = slice(j * HD, (j + 1) * HD)
            st0 = sst_ref[j]
            dst1 = dst_ref[j]
            v, g = p_ref[2, :, ln], p_ref[3, :, ln]
            q, k, b = q_all[:, ln], k_all[:, ln], b_all[:, ln]
            bm = b[CHUNK // 2 - 1:CHUNK // 2]
            bl = b[CHUNK - 1:CHUNK]
            eb = jnp.exp(b)
            e_qt = jnp.exp(b - bm)
            e_kt = jnp.exp(bm - b)
            e_kl = jnp.exp(bl - b)
            e_bl = jnp.exp(bl)
            qd, qt, kt, kl = q * eb, q * e_qt, k * e_kt, k * e_kl
            ov = o_ref[:, ln]
            r = lax.rsqrt(jnp.mean(ov * ov, axis=-1, keepdims=True) + RMS_EPS)
            oh = ov * r
            sg = _sigmoid(g)
            dy = dy_ref[:, ln]
            dp_ref[3, :, ln] = (dy * oh * gam * (sg * (1.0 + g * (1.0 - sg)))).astype(BF16)
            dyv = dy * (g * sg)
            gdy = dyv * gam
            do = (gdy - oh * jnp.mean(gdy * oh, axis=-1, keepdims=True)) * r
            dob, vb = do.astype(BF16), v.astype(BF16)
            st0b, dst1b = st0.astype(BF16), dst1.astype(BF16)
            st1 = st0 * e_bl + _dg(vb, kl.astype(BF16), TN)
            att = jnp.where(tril, _dot3(qt, kt, NT), 0.0)
            datt = jnp.where(tril, _dg(dob, vb, NT), 0.0)
            dv = _dg(att.astype(BF16), dob, TN) + _dg(kl.astype(BF16), dst1b, NT)
            dq = _dot3(datt, kt, NN) * e_qt + _dg(dob, st0b, NN) * eb
            dk = _dot3(datt, qt, TN) * e_kt + _dg(vb, dst1b, NN) * e_kl
            db = q * dq - k * dk
            last = lax.broadcasted_iota(jnp.int32, (CHUNK, 1), 0) == CHUNK - 1
            db = db + jnp.where(last, jnp.sum(dst1 * st1, axis=0, keepdims=True), 0.0)
            dst_ref[j] = dst1 * e_bl + _dg(dob, qd.astype(BF16), TN)
            dp_ref[2, :, ln] = dv.astype(BF16)
            dq_parts.append(dq)
            dk_parts.append(dk)
            db_parts.append(db)
            dgam_parts.append(jnp.sum(dyv * oh, axis=0, keepdims=True))

        dq_all, dk_all = jnp.concatenate(dq_parts, axis=1), jnp.concatenate(dk_parts, axis=1)
        dlf = _chunk_sums(triu, jnp.concatenate(db_parts, axis=1))
        dp_ref[0] = (dq_all * (sq_all * (1.0 + qr_all * (1.0 - sq_all)))).astype(BF16)
        tmp = dlf / f_all - dk_all
        dp_ref[1] = (tmp * (1.0 - lb_all) * sx_all * (1.0 - sx_all)).astype(BF16)
        dlb_acc[...] += jnp.sum((1.0 - sx_all) * tmp, axis=0, keepdims=True)
        dgam_acc[...] += functools.reduce(lambda a, b: a + b, dgam_parts)

        @pl.when(step == nc - 1)
        def _():
            d1 = dlb_acc[...] * lb_all * (1.0 - lb_all)
            dclb_ref[...] = jnp.where(lax.broadcasted_iota(jnp.int32, (2, wide), 0) == 0, -d1, d1)

        @pl.when((step == nc - 1) & (group == ng - 1))
        def _():
            dgam_ref[...] = dgam_acc[...]

        if nr:
            rider.ride(grid, riders, "finish")

    rev = lambda h, s: (nc - 1 - s, h)
    outs = pl.pallas_call(
        body, name="hgrn_bwd", grid=grid,
        in_specs=[_spec((4, CHUNK, wide), lambda h, s: (0, nc - 1 - s, h)),
                  _spec((None, 1, wide), lambda h, s: (R_CLB, 0, h)),
                  _spec((None, 1, wide), lambda h, s: (R_CLB + 1, 0, h)),
                  _spec((None, 1, HD), lambda h, s: (R_GAM, 0, 0)),
                  _spec((CHUNK, wide), rev),
                  _spec((None, hps, HD, HD), lambda h, s: (nc - 1 - s, h, 0, 0)),
                  _spec((CHUNK, wide), rev)] + (rider.in_specs if nr else []),
        out_specs=[_spec((4, CHUNK, wide), lambda h, s: (0, nc - 1 - s, h)),
                   _spec((2, wide), lambda h, s: (0, h)),
                   _spec((1, HD), lambda h, s: (0, 0))] + (rider.out_specs if nr else []),
        out_shape=[jax.ShapeDtypeStruct((4, t, D), BF16), jax.ShapeDtypeStruct((2, D), F32),
                   jax.ShapeDtypeStruct((1, HD), F32)] + (rider.out_shape if nr else []),
        scratch_shapes=[pltpu.VMEM((hps, HD, HD), F32), pltpu.VMEM((1, wide), F32), pltpu.VMEM((1, HD), F32)]
        + (rider.scratch if nr else []),
        compiler_params=_params(("arbitrary", "arbitrary")),
    )(proj_c, small, small, small, o, sst, dyc, *(rider.groups if nr else []))
    return outs[0], outs[1], outs[2], (rider.place(outs[3:]) if nr else [])


def _adamw(name, w, grads, m, v):
    nl = len(grads)
    rows, cols = grads[0].shape
    br = rows
    for cand in (512, 352, 256):
        if rows % cand == 0:
            br = cand
            break
    nb = rows // br
    c1 = 1.0 - ADAM_B1 ** ADAM_STEP
    c2 = 1.0 - ADAM_B2 ** ADAM_STEP

    def body(w_ref, m_ref, v_ref, *refs):
        g_refs, (d_ref, mo_ref, vo_ref, go_ref) = refs[:nl], refs[nl:]
        layer = pl.program_id(0)
        gv = g_refs[0][...]
        for k in range(1, nl):
            gv = jnp.where(layer == k, g_refs[k][...], gv)
        mn = ADAM_B1 * m_ref[...] + (1.0 - ADAM_B1) * gv
        vn = ADAM_B2 * v_ref[...] + (1.0 - ADAM_B2) * (gv * gv)
        mo_ref[...] = mn
        vo_ref[...] = vn
        go_ref[...] = gv
        d_ref[...] = -ADAM_LR * ((mn / c1) / (jnp.sqrt(vn / c2) + ADAM_EPS) + ADAM_WD * w_ref[...])

    blk = _spec((br, cols), lambda l, i: (l * nb + i, 0))
    g_specs = [_spec((br, cols), lambda l, i, k=k: (jnp.where(l == k, i, 0), 0)) for k in range(nl)]
    shape = jax.ShapeDtypeStruct((nl * rows, cols), F32)
    return pl.pallas_call(
        body, name=name, grid=(nl, nb), in_specs=[blk] * 3 + g_specs, out_specs=[blk] * 4, out_shape=[shape] * 4,
        compiler_params=_params(("arbitrary", "arbitrary")),
    )(w, m, v, *grads)


def _place():
    x, y, c = lax.axis_index("x"), lax.axis_index("y"), lax.axis_index("c")
    chips = [(1 - x, y), (x, 1 - y), (1 - x, 1 - y)]
    return x, y, c, chips


class _Rider:
    n = 0
    relay = None

    def ride(self, grid, refs, when):
        if not self.n:
            return
        ids = [pl.program_id(a) for a in range(len(grid))]
        first = functools.reduce(jnp.logical_and, [i == 0 for i in ids])
        last = functools.reduce(jnp.logical_and, [i == g - 1 for i, g in zip(ids, grid)])
        phases = [(first, self.start), (last, self.relay)] if when == "start" else [(last, self.finish)]
        for cond, phase in phases:
            if phase is not None:
                pl.when(cond)(functools.partial(phase, *refs))


class _Gather(_Rider):
    PER_GROUP = 7

    def __init__(self, groups):
        self.groups = list(groups)
        self.n = len(self.groups)
        self.in_specs = [ANY] * self.n
        self.out_specs = [ANY] * self.n
        self.out_shape = [jax.ShapeDtypeStruct((NSH,) + g.shape, g.dtype) for g in self.groups]
        sems = pltpu.SemaphoreType.DMA((self.PER_GROUP * self.n,))
        self.scratch = [sems, sems] if self.n else []

    def _copies(self, ins, outs, send, recv):
        x, y, c, chips = _place()
        sibling = (x, y, 1 - c)

        def half(gi, chip, hc):
            return outs[gi].at[2 * chip[0] + chip[1], hc]

        def copy(gi, k, src, dst, to):
            sem = self.PER_GROUP * gi + k
            return pltpu.make_async_remote_copy(src_ref=src, dst_ref=dst, send_sem=send.at[sem], recv_sem=recv.at[sem],
                                                device_id=to, device_id_type=MESH)

        pairs = [(gi, j, chip) for gi in range(self.n) for j, chip in enumerate(chips)]
        first = [copy(gi, j, ins[gi].at[c], half(gi, (x, y), c), (*chip, c)) for gi, j, chip in pairs]
        first += [copy(gi, 6, ins[gi], outs[gi].at[2 * x + y], sibling) for gi in range(self.n)]
        landed = [copy(gi, j, half(gi, chip, c), half(gi, chip, c), sibling) for gi, j, chip in pairs]
        relay = [copy(gi, 3 + j, half(gi, chip, c), half(gi, chip, c), sibling) for gi, j, chip in pairs]
        relayed = [copy(gi, 3 + j, half(gi, chip, 1 - c), half(gi, chip, 1 - c), sibling) for gi, j, chip in pairs]
        relayed += [copy(gi, 6, ins[gi], outs[gi].at[2 * x + y], sibling) for gi in range(self.n)]
        return first, landed, relay, relayed

    def start(self, ins, outs, send, recv):
        for cp in self._copies(ins, outs, send, recv)[0]:
            cp.start()

    def relay(self, ins, outs, send, recv):
        _, landed, relay, _ = self._copies(ins, outs, send, recv)
        for arrived, onward in zip(landed, relay):
            arrived.wait_recv()
            onward.start()

    def finish(self, ins, outs, send, recv):
        first, _, relay, relayed = self._copies(ins, outs, send, recv)
        for cp in relayed:
            cp.wait_recv()
        for cp in first + relay:
            cp.wait_send()

    def place(self, outs):
        return list(outs)


def _alone(name, rider):
    n = rider.n

    def body(*refs):
        parts = (refs[:n], refs[n:2 * n], refs[2 * n], refs[2 * n + 1])
        rider.start(*parts)
        if rider.relay is not None:
            rider.relay(*parts)
        rider.finish(*parts)

    outs = pl.pallas_call(
        body, name=name, in_specs=rider.in_specs, out_specs=rider.out_specs, out_shape=rider.out_shape,
        scratch_shapes=rider.scratch, compiler_params=pltpu.CompilerParams(has_side_effects=True),
    )(*rider.groups)
    return rider.place(outs)


class _Swap(_Rider):
    def __init__(self, slots):
        self.slots = list(slots)
        self.groups = [buf for buf, _, _ in self.slots]
        self.n = len(self.slots)
        self.in_specs = [ANY] * self.n
        self.out_specs = [ANY] * self.n
        self.out_shape = [jax.ShapeDtypeStruct((NSH, rows // 2, buf.shape[2]), buf.dtype) for buf, _, rows in self.slots]
        self.scratch = [pltpu.SemaphoreType.DMA((self.n,)), pltpu.SemaphoreType.DMA((self.n,))]

    def _copies(self, ins, outs, send, recv):
        x, y, c, _ = _place()
        cps = []
        for i, (_, row0, rows) in enumerate(self.slots):
            half = rows // 2
            src = ins[i].at[:, pl.ds(pl.multiple_of(row0 + (1 - c) * half, 16), half)]
            cps.append(pltpu.make_async_remote_copy(src_ref=src, dst_ref=outs[i], send_sem=send.at[i],
                                                    recv_sem=recv.at[i], device_id=(x, y, 1 - c), device_id_type=MESH))
        return cps

    def start(self, ins, outs, send, recv):
        for cp in self._copies(ins, outs, send, recv):
            cp.start()

    def finish(self, ins, outs, send, recv):
        for cp in self._copies(ins, outs, send, recv):
            cp.wait()

    def place(self, outs):
        return list(outs)


class _Share(_Rider):
    def __init__(self, arrays):
        self.groups = list(arrays)
        self.n = len(self.groups)
        self.in_specs = [ANY] * self.n
        self.out_specs = [ANY] * self.n
        self.out_shape = [jax.ShapeDtypeStruct((2,) + g.shape, g.dtype) for g in self.groups]
        self.scratch = [pltpu.SemaphoreType.DMA((self.n,)), pltpu.SemaphoreType.DMA((self.n,))]

    def _copies(self, ins, outs, send, recv, half):
        x, y, c, _ = _place()
        return [pltpu.make_async_remote_copy(src_ref=ins[gi], dst_ref=outs[gi].at[c if half == "mine" else 1 - c],
                                             send_sem=send.at[gi], recv_sem=recv.at[gi], device_id=(x, y, 1 - c),
                                             device_id_type=MESH) for gi in range(self.n)]

    def start(self, ins, outs, send, recv):
        for cp in self._copies(ins, outs, send, recv, "mine"):
            cp.start()

    def finish(self, ins, outs, send, recv):
        for cp in self._copies(ins, outs, send, recv, "theirs"):
            cp.wait_recv()
        for cp in self._copies(ins, outs, send, recv, "mine"):
            cp.wait_send()

    def place(self, outs):
        c = lax.axis_index("c")
        return [lax.dynamic_update_index_in_dim(o, g, c, 0) for o, g in zip(outs, self.groups)]


class _Send(_Rider):
    def __init__(self, arrays):
        self.groups = list(arrays)
        self.n = len(self.groups)
        self.in_specs = [ANY] * self.n
        self.out_specs = [ANY] * self.n
        self.out_shape = [jax.ShapeDtypeStruct((3,) + g.shape[1:], g.dtype) for g in self.groups]
        self.scratch = [pltpu.SemaphoreType.DMA((3 * self.n,)), pltpu.SemaphoreType.DMA((3 * self.n,))]

    def _copies(self, ins, outs, send, recv):
        x, y, c, chips = _place()
        return [pltpu.make_async_remote_copy(src_ref=ins[gi].at[2 * chip[0] + chip[1]], dst_ref=outs[gi].at[j],
                                             send_sem=send.at[3 * gi + j], recv_sem=recv.at[3 * gi + j],
                                             device_id=(*chip, c), device_id_type=MESH)
                for gi in range(self.n) for j, chip in enumerate(chips)]

    def start(self, ins, outs, send, recv):
        for cp in self._copies(ins, outs, send, recv):
            cp.start()

    def finish(self, ins, outs, send, recv):
        for cp in self._copies(ins, outs, send, recv):
            cp.wait()

    def place(self, outs):
        return list(outs)


def _pair_sum(name, slots, got, c_idx):
    n = len(slots)
    in_specs, out_specs, out_shape, operands = [], [], [], []
    for (buf, row0, rows), g in zip(slots, got):
        hb, cols = rows // 4, buf.shape[2]
        in_specs += [pl.BlockSpec((None, hb, cols), lambda q, i, cr, r=row0 // hb: (q, r + 2 * cr[0] + i, 0)),
                     pl.BlockSpec((None, hb, cols), lambda q, i, cr: (q, i, 0))]
        out_specs.append(pl.BlockSpec((None, hb, cols), lambda q, i, cr: (q, i, 0)))
        out_shape.append(jax.ShapeDtypeStruct(g.shape, BF16))
        operands += [buf, g]

    def body(c_ref, *refs):
        for i in range(n):
            refs[2 * n + i][...] = (refs[2 * i][...].astype(F32) + refs[2 * i + 1][...].astype(F32)).astype(BF16)

    return pl.pallas_call(
        body, name=name,
        grid_spec=pltpu.PrefetchScalarGridSpec(num_scalar_prefetch=1, grid=(NSH, 2), in_specs=in_specs, out_specs=out_specs),
        out_shape=out_shape, compiler_params=_params(("parallel", "parallel")),
    )(c_idx, *operands)


def _owner_sum(name, pairs, got, p_idx):
    n = len(pairs)
    in_specs, out_specs, out_shape, operands = [], [], [], []
    for own, g in zip(pairs, got):
        _, rows, cols = own.shape
        hb = rows // 2
        in_specs += [pl.BlockSpec((None, hb, cols), lambda i, pr: (pr[0], i, 0)),
                     pl.BlockSpec((3, hb, cols), lambda i, pr: (0, i, 0))]
        out_specs.append(pl.BlockSpec((hb, cols), lambda i, pr: (i, 0)))
        out_shape.append(jax.ShapeDtypeStruct((rows, cols), F32))
        operands += [own, g]

    def body(p_ref, *refs):
        for i in range(n):
            a_ref, b_ref = refs[2 * i], refs[2 * i + 1]
            refs[2 * n + i][...] = ((a_ref[...].astype(F32) + b_ref[0].astype(F32)) + b_ref[1].astype(F32)) + b_ref[2].astype(F32)

    return pl.pallas_call(
        body, name=name,
        grid_spec=pltpu.PrefetchScalarGridSpec(num_scalar_prefetch=1, grid=(2,), in_specs=in_specs, out_specs=out_specs),
        out_shape=out_shape, compiler_params=_params(("parallel",)),
    )(p_idx, *operands)


def _sum_small(slab):
    def body(in_ref, out_ref, all_ref, send, recv):
        x, y, c, _ = _place()
        me = 4 * x + 2 * y + c
        all_ref[me] = in_ref[...]
        cps = []
        for k in range(1, 8):
            peer = (x ^ (k >> 2), y ^ ((k >> 1) & 1), c ^ (k & 1))
            cps.append(pltpu.make_async_remote_copy(src_ref=in_ref, dst_ref=all_ref.at[me], send_sem=send.at[k - 1],
                                                    recv_sem=recv.at[k - 1], device_id=peer, device_id_type=MESH))
        for cp in cps:
            cp.start()
        for cp in cps:
            cp.wait()
        total = all_ref[0]
        for d in range(1, 8):
            total = total + all_ref[d]
        out_ref[...] = total

    return pl.pallas_call(
        body, name="sum_small",
        in_specs=[pl.BlockSpec(memory_space=pltpu.VMEM)], out_specs=pl.BlockSpec(memory_space=pltpu.VMEM),
        out_shape=jax.ShapeDtypeStruct(slab.shape, F32),
        scratch_shapes=[pltpu.VMEM((8,) + slab.shape, F32), pltpu.SemaphoreType.DMA((7,)), pltpu.SemaphoreType.DMA((7,))],
        compiler_params=pltpu.CompilerParams(has_side_effects=True),
    )(slab)


FFNS = ("pre0", "post0", "pre1", "post1")
W_SHAPES = dict({f + "_gu": (NSH, 2, FS, D) for f in FFNS}, **{f + "_d": (NSH, FS, D) for f in FFNS},
                ab_in=(NSH, D, 768), ab_out=(NSH, 256, D), conv=(NSH, 2, 8, LANE), c_in=(NSH, D, D), c_out=(NSH, 256, D))
CARRIED = dict(pre0_norm=("pre0_gu",), pre0_up=("pre0_d",), pre0_down=("ab_in", "conv"), ab_proj=("ab_out",),
               c_proj=("post1_d",),
               sb_fwd=("post0_gu", "post0_d"), post0_up=("pre1_gu",), post0_down=("pre1_d",),
               pre1_up=("c_in", "c_out"), hgrn_fwd=("post1_gu",))


FFN_SLOTS = dict(pre0=(0, 1, 0), pre1=(2, 3, 1), post0=(4, 5, 2), post1=(6, 7, 3))
GRAD_SLOTS = dict(ab_out=("b", B_ABOUT, 256), c_in=("b", B_CIN, D), c_out=("b", B_COUT, 256), ab_in=("c", 0, D))
for _f, (_g, _u, _d) in FFN_SLOTS.items():
    GRAD_SLOTS.update({_f + "_g": ("a", _g * FS, FS), _f + "_u": ("a", _u * FS, FS), _f + "_d": ("b", _d * FS, FS)})
REDUCE_STAGES = dict(x=("post1_g", "post1_u", "post1_d", "c_out"), w=("c_in", "pre1_d"),
                     y=("pre1_g", "pre1_u", "post0_g", "post0_u", "post0_d", "ab_out"),
                     z1=("ab_in",), z2=("pre0_g", "pre0_u", "pre0_d"))


def _local_step(x, target, weights, small, shards=None, place=None):
    t = x.shape[0]
    tm = min(MM_TILE, t)
    th = min(HALF_TILE, t)
    tw = min(WGRAD_TILE, t)
    nt = t // tm
    tok_si = _spec((tm, D), lambda s, i: (i, 0))
    w = dict(weights)
    reduced = {}

    def stage_slots(stage, buffers):
        return [(buffers[GRAD_SLOTS[n][0]],) + GRAD_SLOTS[n][1:] for n in REDUCE_STAGES[stage]]

    def swap_rider(stage, buffers):
        return _Swap(stage_slots(stage, buffers)) if place is not None else None

    def reduce_start(stage, buffers, swapped=None):
        if place is None:
            return [], None
        slots = stage_slots(stage, buffers)
        if swapped is None:
            swapped = _alone("grad_swap_" + stage, _Swap(slots))
        pairs = _pair_sum("grad_pair_sum_" + stage, slots, swapped, place[0])
        return pairs, _Send(pairs)

    def reduce_end(stage, pairs, landed):
        if place is None:
            return None
        mine = _owner_sum("grad_owner_sum_" + stage, pairs, landed, place[1])
        return _Share(mine)

    def shared(stage, landed):
        if place is not None:
            reduced.update(zip(REDUCE_STAGES[stage], landed))

    def carried(kernel_name):
        names = [n for n in CARRIED[kernel_name] if n not in w]
        return names, (_Gather([shards[n] for n in names]) if names else None)

    def land(names, arrays):
        for n, a in zip(names, arrays):
            w[n] = a.reshape(W_SHAPES[n])

    def ffn_forward(tag, h, hn, next_row):
        names, gather = carried(tag + "_up") if tag + "_up" in CARRIED else ([], None)
        s_up, s_gate, a, got = _ffn_up(tag + "_up", hn, w[tag + "_gu"], gather)
        land(names, got)
        names, gather = carried(tag + "_down") if tag + "_down" in CARRIED else ([], None)
        out = _ffn_down(tag + "_down", a, w[tag + "_d"], h, gather, (small, next_row) if next_row is not None else None)
        if gather is not None:
            out, got = out
            land(names, got)
        out, hn_next = out if next_row is not None else (out, None)
        return out, hn_next, (h, hn, s_up, s_gate, a)

    def out_proj(name, y, w_out, h, next_row):
        row_spec = _spec((th, D), lambda i: (i, 0))
        return _mm(name, [(y, row_spec, w_out.reshape(D, D), _spec((D, D), lambda i: (0, 0)))],
                   grid=(t // th,), o_shape=(t, D), o_dtype=F32, o_spec=row_spec, dims=NN, kaxis=0, nk=1,
                   res=(h, row_spec), norm=(small, next_row))

    def out_proj_bwd(tag, dhb, y, w_out, blk, grad_b, make_rider=None):
        grad_b = _wgrad(tag + "_dwout", y, _spec((tw, 256), lambda s, k: (k, s)), dhb, _spec((tw, D), lambda s, k: (k, 0)),
                        256, D, t, tw, grad_b, blk)
        rider = make_rider(grad_b) if make_rider is not None else None
        row_spec = _spec((th, D), lambda i: (i, 0))
        dy = _mm(tag + "_dy", [(dhb, row_spec, w_out.reshape(D, D), _spec((D, D), lambda i: (0, 0)))],
                 grid=(t // th,), o_shape=(t, D), o_dtype=F32, o_spec=row_spec, dims=NT, kaxis=0, nk=1, carry=rider)
        dy, landed = dy if rider is not None else (dy, None)
        return dy, grad_b, landed

    h0 = x
    names, gather = carried("pre0_norm")
    hn, got = _norm_fwd("pre0_norm", h0, small, R_PRE, gather)
    land(names, got)
    h1, hn_ab, pre0 = ffn_forward("pre0", h0, hn, R_MIX)
    def carrying(kernel_name, make):
        names, gather = carried(kernel_name)
        out = make(gather)
        if gather is not None:
            out, got = out
            land(names, got)
        return out

    proj_a, proj_b = carrying("ab_proj", lambda gather: _ab_proj(hn_ab, w["ab_in"], gather))
    y_a = _conv_fwd(proj_a, w["conv"])
    names, gather = carried("sb_fwd")
    y_b, ltot, nblk, got = _sb_fwd(proj_b, gather)
    land(names, got)
    y_ab = jnp.concatenate([y_a, y_b], axis=1)
    h2, hn = out_proj("ab_out", y_ab, w["ab_out"], h1, R_POST)
    h3, hn, post0 = ffn_forward("post0", h2, hn, R_PRE + 1)
    h4, hn_c, pre1 = ffn_forward("pre1", h3, hn, R_MIX + 1)
    proj_c = carrying("c_proj", lambda gather: _mm(
        "c_proj", [(hn_c, tok_si, w["c_in"], _spec((None, D, D), lambda s, i: (s, 0, 0)))],
        grid=(NSH, nt), o_shape=(NSH, t, D), o_dtype=F32, o_spec=_spec((None, tm, D), lambda s, i: (s, i, 0)),
        dims=NN, kaxis=1, nk=1, carry=gather))
    names, gather = carried("hgrn_fwd")
    o_c, y_c, sst, got = _hgrn_fwd(proj_c, small, gather)
    land(names, got)
    h5, hn = out_proj("c_out", y_c, w["c_out"], h4, R_POST + 1)
    h6, _, post1 = ffn_forward("post1", h5, hn, None)
    dh, dhb, d_fin, loss = _final_loss(h6, small, target)

    dh, dhb, grad_a, grad_b, dn_post1, _, _ = _ffn_backward("post1", dh, dhb, *post1, w["post1_gu"], w["post1_d"],
                                                            *FFN_SLOTS["post1"], small, R_POST + 1, 8 * FS, B_ROWS)
    dy_c, grad_b, swapped = out_proj_bwd("c", dhb, y_c, w["c_out"], B_COUT // 256, grad_b,
                                         lambda buf_b: swap_rider("x", dict(a=grad_a, b=buf_b)))
    pairs, rider = reduce_start("x", dict(a=grad_a, b=grad_b), swapped)
    dproj_c, d_clb, d_gam, landed = _hgrn_bwd(proj_c, small, o_c, sst, dy_c, rider)
    share = reduce_end("x", pairs, landed)
    grad_b = _wgrad("c_dwin", hn_c, _spec((tw, D), lambda s, k: (k, 0)), dproj_c, _spec((None, tw, D), lambda s, k: (s, k, 0)),
                    D, D, t, tw, grad_b, B_CIN // D, carry=share)
    if share is not None:
        grad_b, landed = grad_b
        shared("x", landed)
    dh, dhb, dn_mix1, _ = _dhn_norm("c_dhn", [(dproj_c, _spec((None, th, D), lambda i, s=s: (s, i, 0)),
                                               w["c_in"], _spec((None, D, D), lambda i, s=s: (s, 0, 0)))
                                              for s in range(NSH)],
                                    NT, h4, small, R_MIX + 1, dh)
    stage_w = {}

    def w_swap(buf_a, buf_b):
        return swap_rider("w", dict(a=buf_a, b=buf_b))

    def w_send(buf_a, buf_b, swapped):
        stage_w["pairs"], rider = reduce_start("w", dict(a=buf_a, b=buf_b), swapped)
        return rider

    on = place is not None
    dh, dhb, grad_a, grad_b, dn_pre1, _, landed = _ffn_backward(
        "pre1", dh, dhb, *pre1, w["pre1_gu"], w["pre1_d"], *FFN_SLOTS["pre1"], small, R_PRE + 1, grad_a, grad_b,
        w_swap if on else None, w_send if on else None)
    share = reduce_end("w", stage_w.get("pairs"), landed)
    dh, dhb, grad_a, grad_b, dn_post0, landed, _ = _ffn_backward(
        "post0", dh, dhb, *post0, w["post0_gu"], w["post0_d"], *FFN_SLOTS["post0"], small, R_POST, grad_a, grad_b,
        (lambda buf_a, buf_b: share) if on else None)
    shared("w", landed)
    dy_ab, grad_b, _ = out_proj_bwd("ab", dhb, y_ab, w["ab_out"], B_ABOUT // 256, grad_b)
    dab, dac, dax, d_conv, swapped = _conv_bwd(proj_a, w["conv"], dy_ab, swap_rider("y", dict(a=grad_a, b=grad_b)))
    pairs, rider = reduce_start("y", dict(a=grad_a, b=grad_b), swapped)
    dq, dk, dv, landed = _sb_bwd(proj_b, dy_ab, ltot, nblk, rider)
    share = reduce_end("y", pairs, landed)
    dproj_ab = jnp.concatenate([dab, dac, dax, dq, dk, dv], axis=1)
    grad_c = _wgrad("ab_dwin", hn_ab, _spec((tw, D), lambda s, k: (k, 0)), dproj_ab, _spec((tw, 768), lambda s, k: (k, s)),
                    D, 768, t, tw, D, 0, carry=share)
    if share is not None:
        grad_c, landed = grad_c
        shared("y", landed)
    dh, dhb, dn_mix0, _ = _dhn_norm("ab_dhn", [(dproj_ab, _spec((th, 768), lambda i, s=s: (i, s)),
                                                w["ab_in"], _spec((None, D, 768), lambda i, s=s: (s, 0, 0)))
                                               for s in range(NSH)],
                                    NT, h1, small, R_MIX, dh)
    last = {}

    def z1_send(buf_a, buf_b):
        last["z1"], rider = reduce_start("z1", dict(b=buf_b, c=grad_c))
        return rider

    def z2_send(buf_a, buf_b, _):
        last["z2"], rider = reduce_start("z2", dict(a=buf_a, b=buf_b))
        return rider

    dh, dhb, grad_a, grad_b, dn_pre0, landed, landed_own = _ffn_backward(
        "pre0", dh, dhb, *pre0, w["pre0_gu"], w["pre0_d"], *FFN_SLOTS["pre0"], small, R_PRE, grad_a, grad_b,
        z1_send if on else None, z2_send if on else None)
    if place is not None:
        mine = (_owner_sum("grad_owner_sum_z1", last["z1"], landed, place[1])
                + _owner_sum("grad_owner_sum_z2", last["z2"], landed_own, place[1]))
        reduced.update(zip(REDUCE_STAGES["z1"] + REDUCE_STAGES["z2"], _alone("grad_share_z", _Share(mine))))
    zero = jnp.zeros((1, D), F32)
    conv_rows = jnp.pad(jnp.transpose(d_conv[:, :3, :], (1, 0, 2)).reshape(3, 512), ((0, 0), (0, D - 512)))
    small_grad = jnp.concatenate([
        dn_pre0, dn_pre1, dn_mix0, dn_mix1, dn_post0, dn_post1, d_clb, d_fin,
        jnp.pad(d_gam, ((0, 0), (0, D - HD))), conv_rows,
        jnp.pad(loss, ((0, 0), (0, D - 1))), zero, zero], axis=0)
    return dh, grad_a, grad_b, grad_c, small_grad, reduced


def _small_slab(rows):
    parts = [jnp.pad(r.astype(F32), ((0, 0), (0, D - r.shape[1]))) for r in rows]
    slab = jnp.concatenate(parts, axis=0)
    return jnp.pad(slab, ((0, SMALL_ROWS - slab.shape[0]), (0, 0)))


def kernel(x, ffn_pre_norm, ffn_pre_w_gate, ffn_pre_w_up, ffn_pre_w_down, mix_norm, ffn_post_norm, ffn_post_w_gate, ffn_post_w_up, ffn_post_w_down, ab_w_in, ab_conv_w, ab_w_out, c_w_in, c_lower_bounds, c_out_norm, c_w_out, final_norm, loss_target, m_ffn_pre_norm, m_ffn_pre_w_gate, m_ffn_pre_w_up, m_ffn_pre_w_down, m_mix_norm, m_ffn_post_norm, m_ffn_post_w_gate, m_ffn_post_w_up, m_ffn_post_w_down, m_ab_w_in, m_ab_conv_w, m_ab_w_out, m_c_w_in, m_c_lower_bounds, m_c_out_norm, m_c_w_out, m_final_norm, v_ffn_pre_norm, v_ffn_pre_w_gate, v_ffn_pre_w_up, v_ffn_pre_w_down, v_mix_norm, v_ffn_post_norm, v_ffn_post_w_gate, v_ffn_post_w_up, v_ffn_post_w_down, v_ab_w_in, v_ab_conv_w, v_ab_w_out, v_c_w_in, v_c_lower_bounds, v_c_out_norm, v_c_w_out, v_final_norm):
    t = x.shape[1]
    xi, yi, ci = lax.axis_index("x"), lax.axis_index("y"), lax.axis_index("c")
    p_idx = (2 * xi + yi).astype(jnp.int32).reshape(1)
    c_idx = ci.astype(jnp.int32).reshape(1)

    def halves(m):
        return m.astype(BF16).reshape(2, m.shape[0] // 2, m.shape[1])

    transposed = ("ffn_pre_w_gate", "ffn_pre_w_up", "ffn_post_w_gate", "ffn_post_w_up")

    def flip(a):
        return jnp.swapaxes(a, 1, 2)

    shards = {}
    for name, (w_gate, w_up, w_down, layer) in dict(
            pre0=(ffn_pre_w_gate, ffn_pre_w_up, ffn_pre_w_down, 0), post0=(ffn_post_w_gate, ffn_post_w_up, ffn_post_w_down, 0),
            pre1=(ffn_pre_w_gate, ffn_pre_w_up, ffn_pre_w_down, 1), post1=(ffn_post_w_gate, ffn_post_w_up, ffn_post_w_down, 1)).items():
        shards[name + "_gu"] = jnp.stack([flip(w_gate)[layer], flip(w_up)[layer]]).astype(BF16)
        shards[name + "_d"] = halves(w_down[layer])
    conv_pad = jnp.pad(ab_conv_w[0], ((0, 5), (0, 0)))
    shards.update(ab_in=halves(ab_w_in[0]), ab_out=halves(ab_w_out[0]), c_in=halves(c_w_in[0]), c_out=halves(c_w_out[0]),
                  conv=jnp.stack([conv_pad, jnp.zeros_like(conv_pad)]))
    small = _small_slab([ffn_pre_norm, mix_norm, ffn_post_norm, c_lower_bounds, final_norm.reshape(1, D), c_out_norm])
    small = small.reshape(SMALL_ROWS, 1, D)

    grad_x, _, _, _, small_grad, reduced = _local_step(x[0], loss_target[0], {}, small, shards,
                                                        (c_idx, p_idx))
    whole = {n: g.reshape(2 * g.shape[1], g.shape[2]) for n, g in reduced.items()}
    small_sum = _sum_small(small_grad)

    my_conv = lax.dynamic_slice(small_sum[R_CONV:R_CONV + 3], (0, (2 * xi + yi) * 128), (3, 128))
    grads = {
        "ffn_pre_norm": small_sum[R_PRE:R_PRE + 2], "mix_norm": small_sum[R_MIX:R_MIX + 2],
        "ffn_post_norm": small_sum[R_POST:R_POST + 2], "c_lower_bounds": small_sum[R_CLB:R_CLB + 2],
        "c_out_norm": small_sum[R_GAM:R_GAM + 1, :HD], "final_norm": small_sum[R_FIN],
        "ab_conv_w": my_conv.reshape(1, 3, 128),
    }
    layers = dict(ab_w_in=[whole["ab_in"]], ab_w_out=[whole["ab_out"]], c_w_in=[whole["c_in"]], c_w_out=[whole["c_out"]])
    for kind, key in (("gate", "_g"), ("up", "_u"), ("down", "_d")):
        layers["ffn_pre_w_" + kind] = [whole["pre0" + key], whole["pre1" + key]]
        layers["ffn_post_w_" + kind] = [whole["post0" + key], whole["post1" + key]]
    weights = dict(ffn_pre_norm=ffn_pre_norm, ffn_pre_w_gate=ffn_pre_w_gate, ffn_pre_w_up=ffn_pre_w_up, ffn_pre_w_down=ffn_pre_w_down, mix_norm=mix_norm, ffn_post_norm=ffn_post_norm, ffn_post_w_gate=ffn_post_w_gate, ffn_post_w_up=ffn_post_w_up, ffn_post_w_down=ffn_post_w_down, ab_w_in=ab_w_in, ab_conv_w=ab_conv_w, ab_w_out=ab_w_out, c_w_in=c_w_in, c_lower_bounds=c_lower_bounds, c_out_norm=c_out_norm, c_w_out=c_w_out, final_norm=final_norm)
    m_in = dict(ffn_pre_norm=m_ffn_pre_norm, ffn_pre_w_gate=m_ffn_pre_w_gate, ffn_pre_w_up=m_ffn_pre_w_up, ffn_pre_w_down=m_ffn_pre_w_down, mix_norm=m_mix_norm, ffn_post_norm=m_ffn_post_norm, ffn_post_w_gate=m_ffn_post_w_gate, ffn_post_w_up=m_ffn_post_w_up, ffn_post_w_down=m_ffn_post_w_down, ab_w_in=m_ab_w_in, ab_conv_w=m_ab_conv_w, ab_w_out=m_ab_w_out, c_w_in=m_c_w_in, c_lower_bounds=m_c_lower_bounds, c_out_norm=m_c_out_norm, c_w_out=m_c_w_out, final_norm=m_final_norm)
    v_in = dict(ffn_pre_norm=v_ffn_pre_norm, ffn_pre_w_gate=v_ffn_pre_w_gate, ffn_pre_w_up=v_ffn_pre_w_up, ffn_pre_w_down=v_ffn_pre_w_down, mix_norm=v_mix_norm, ffn_post_norm=v_ffn_post_norm, ffn_post_w_gate=v_ffn_post_w_gate, ffn_post_w_up=v_ffn_post_w_up, ffn_post_w_down=v_ffn_post_w_down, ab_w_in=v_ab_w_in, ab_conv_w=v_ab_conv_w, ab_w_out=v_ab_w_out, c_w_in=v_c_w_in, c_lower_bounds=v_c_lower_bounds, c_out_norm=v_c_out_norm, c_w_out=v_c_w_out, final_norm=v_final_norm)
    names = list(weights)
    big = [n for n in names if weights[n].size >= 65536]
    tiny = [n for n in names if n not in big]

    delta, new_m, new_v = {}, {}, {}
    for n in big:
        turn = flip if n in transposed else (lambda a: a)
        shape = turn(weights[n]).shape
        two_d = (shape[0] * shape[1], shape[2])
        d, m2, v2, g2 = _adamw("adamw_" + n, turn(weights[n]).reshape(two_d), layers[n],
                               turn(m_in[n]).reshape(two_d), turn(v_in[n]).reshape(two_d))
        delta[n], new_m[n], new_v[n] = turn(d.reshape(shape)), turn(m2.reshape(shape)), turn(v2.reshape(shape))
        grads[n] = turn(g2.reshape(shape))

    def tiny_slab(src):
        return _small_slab([src[n].reshape(-1, src[n].shape[-1]) for n in tiny])

    offs, row = {}, 0
    for n in tiny:
        nrows = weights[n].size // weights[n].shape[-1]
        offs[n] = (row, nrows)
        row += nrows
    d, m2, v2, _ = _adamw("adamw_small", tiny_slab(weights), [tiny_slab(grads)], tiny_slab(m_in), tiny_slab(v_in))
    for n in tiny:
        r0, nr = offs[n]
        shape = weights[n].shape
        for dst, src in ((delta, d), (new_m, m2), (new_v, v2)):
            dst[n] = src[r0:r0 + nr, :shape[-1]].reshape(shape)

    loss = small_sum[R_LOSS, 0]
    return (loss, grad_x.reshape(1, t, D), *[grads[n] for n in names], *[delta[n] for n in names],
            *[new_m[n] for n in names], *[new_v[n] for n in names])
```

```python
import functools
import math

import jax
import jax.numpy as jnp
from jax import lax
from jax.experimental import pallas as pl
from jax.experimental.pallas import tpu as pltpu

F32 = jnp.float32
BF16 = jnp.bfloat16
MESH = pl.DeviceIdType.MESH
ANY = pl.BlockSpec(memory_space=pl.ANY)

D = 1024
FS = 704
NSH = 4
RMS_EPS = 1e-6
MACARON = 0.5
CHUNK = 64
HD = 128
HGRN_HPS = 8
SBQ = 128
SB_PAIRS_FWD = 4
SB_PAIRS_BWD = 4
SB_DEAD = -105.0
CONV_HALO = 8
LANE = 128
ROW_TILE = 1024
MM_TILE = 1024
HALF_TILE = MM_TILE // 2
WGRAD_TILE = 4096
VMEM_LIMIT = 48 * 1024 * 1024
VMEM_LIMIT_BIG = 58 * 1024 * 1024

ADAM_LR, ADAM_B1, ADAM_B2, ADAM_EPS, ADAM_WD, ADAM_STEP = 0.001, 0.9, 0.999, 1e-08, 0.01, 10

NN = ((1,), (0,))
NT = ((1,), (1,))
TN = ((0,), (0,))

SMALL_ROWS = 16
R_PRE, R_MIX, R_POST, R_CLB, R_FIN, R_GAM, R_CONV, R_LOSS = 0, 2, 4, 6, 8, 9, 10, 13

B_ABOUT = 4 * FS
B_CIN = B_ABOUT + 256
B_COUT = B_CIN + 1024
B_ROWS = B_COUT + 256


def _dg(a, b, dims):
    return lax.dot_general(a, b, (dims, ((), ())), preferred_element_type=F32)


def _split(x):
    hi = x.astype(BF16)
    lo = (x - hi.astype(F32)).astype(BF16)
    return hi, lo


def _dot3(a, b, dims):
    ah, al = _split(a)
    bh, bl = _split(b)
    if dims == TN:
        n = b.shape[1]
        both = _dg(ah, jnp.concatenate([bh, bl], axis=1), dims)
        return both[:, :n] + both[:, n:] + _dg(al, bh, dims)
    m = a.shape[0]
    both = _dg(jnp.concatenate([ah, al], axis=0), bh, dims)
    return both[:m] + both[m:] + _dg(ah, bl, dims)


def _sigmoid(x):
    return 1.0 / (1.0 + jnp.exp(-x))


def _params(sem):
    return pltpu.CompilerParams(dimension_semantics=sem, vmem_limit_bytes=VMEM_LIMIT)


def _spec(shape, imap):
    return pl.BlockSpec(shape, imap)


def _accumulate(acc_ref, pairs, dims):
    part = None
    for a_ref, b_ref in pairs:
        d = _dg(a_ref[...], b_ref[...], dims)
        part = d if part is None else part + d
    acc_ref[...] += part


def _mm(name, pairs, *, grid, o_shape, o_dtype, o_spec, dims, kaxis, nk, acc_shape=None, res=None, scale=None,
        into=None, carry=None, norm=None):
    npairs = len(pairs)
    operands, specs = [], []
    for a, a_spec, b, b_spec in pairs:
        operands += [a, b]
        specs += [a_spec, b_spec]
    if res is not None:
        operands.append(res[0])
        specs.append(res[1])
    aliases = {}
    if into is not None:
        aliases = {len(operands): 0}
        operands.append(into)
        specs.append(ANY)
    if norm is not None:
        operands.append(norm[0])
        specs.append(_spec((None, 1, D), lambda *_: (norm[1], 0, 0)))
    n_own = len(operands)
    n_out = 2 if norm is not None else 1
    nc = carry.n if carry is not None else 0
    if nc:
        operands += carry.groups
        specs += carry.in_specs

    def body(*refs):
        o_ref = refs[n_own + nc]
        riders = ((refs[n_own:n_own + nc], refs[n_own + nc + n_out:n_own + 2 * nc + n_out], refs[-2], refs[-1])
                  if nc else None)
        if nc:
            carry.ride(grid, riders, "start")

        def finish(val):
            if scale is not None:
                val = val * scale
            if res is not None:
                val = val + refs[2 * npairs][...]
            o_ref[...] = val.astype(o_dtype)
            if norm is not None:
                r = lax.rsqrt(jnp.mean(val * val, axis=-1, keepdims=True) + RMS_EPS)
                refs[n_own + nc + 1][...] = (val * r * refs[n_own - 1][...]).astype(BF16)

        if nk == 1:
            part = None
            for n in range(npairs):
                d = _dg(refs[2 * n][...], refs[2 * n + 1][...], dims)
                part = d if part is None else part + d
            finish(part)
        else:
            acc_ref = refs[n_own + 2 * nc + n_out]
            k = pl.program_id(kaxis)

            @pl.when(k == 0)
            def _():
                acc_ref[...] = jnp.zeros_like(acc_ref)

            _accumulate(acc_ref, [(refs[2 * n], refs[2 * n + 1]) for n in range(npairs)], dims)

            @pl.when(k == nk - 1)
            def _():
                finish(acc_ref[...])
        if nc:
            carry.ride(grid, riders, "finish")

    sem = tuple("arbitrary" if (nc or (ax == kaxis and nk > 1)) else "parallel" for ax in range(len(grid)))
    outs = pl.pallas_call(
        body, name=name, grid=grid, in_specs=specs,
        out_specs=[o_spec] * n_out + (carry.out_specs if nc else []),
        out_shape=[jax.ShapeDtypeStruct(o_shape, o_dtype)] + ([jax.ShapeDtypeStruct(o_shape, BF16)] if norm is not None else [])
        + (carry.out_shape if nc else []),
        scratch_shapes=([pltpu.VMEM(acc_shape, F32)] if nk > 1 else []) + (carry.scratch if nc else []),
        input_output_aliases=aliases,
        compiler_params=_params(sem),
    )(*operands)
    main = (outs[0], outs[1]) if norm is not None else outs[0]
    return (main, carry.place(outs[n_out:])) if nc else main


def _norm_fwd(name, h, gain_slab, row, rider=None):
    t = h.shape[0]
    tm = min(ROW_TILE, t)
    grid = (t // tm,)
    nr = rider.n if rider is not None else 0

    def body(*refs):
        h_ref, g_ref, o_ref = refs[0], refs[1], refs[2 + nr]
        riders = (refs[2:2 + nr], refs[3 + nr:3 + 2 * nr], refs[-2], refs[-1]) if nr else None
        if nr:
            rider.ride(grid, riders, "start")
        x = h_ref[...]
        r = lax.rsqrt(jnp.mean(x * x, axis=-1, keepdims=True) + RMS_EPS)
        o_ref[...] = (x * r * g_ref[...]).astype(BF16)
        if nr:
            rider.ride(grid, riders, "finish")

    outs = pl.pallas_call(
        body, name=name, grid=grid,
        in_specs=[_spec((tm, D), lambda i: (i, 0)), _spec((None, 1, D), lambda i: (row, 0, 0))]
        + (rider.in_specs if nr else []),
        out_specs=[_spec((tm, D), lambda i: (i, 0))] + (rider.out_specs if nr else []),
        out_shape=[jax.ShapeDtypeStruct((t, D), BF16)] + (rider.out_shape if nr else []),
        scratch_shapes=rider.scratch if nr else [],
        compiler_params=_params(("arbitrary",) if nr else ("parallel",)),
    )(h, gain_slab, *(rider.groups if nr else []))
    return outs[0], (rider.place(outs[1:]) if nr else [])


def _dhn_norm(name, pairs, dims, h, gain_slab, row, dres, rider=None):
    t = h.shape[0]
    tm = min(HALF_TILE, t)
    nt = t // tm
    grid = (nt,)
    npairs = len(pairs)
    nr = rider.n if rider is not None else 0
    operands, specs = [], []
    for a, a_spec, b, b_spec in pairs:
        operands += [a, b]
        specs += [a_spec, b_spec]
    row_spec = _spec((tm, D), lambda i: (i, 0))
    operands += [h, gain_slab, dres]
    specs += [row_spec, _spec((None, 1, D), lambda i: (row, 0, 0)), row_spec]
    n_own = len(operands)

    def body(*refs):
        h_ref, g_ref, dres_ref = refs[2 * npairs:n_own]
        dh_ref, dhb_ref, dg_ref = refs[n_own + nr:n_own + nr + 3]
        gacc_ref = refs[n_own + 2 * nr + 3]
        riders = (refs[n_own:n_own + nr], refs[n_own + nr + 3:n_own + 2 * nr + 3], refs[-2], refs[-1]) if nr else None
        if nr:
            rider.ride(grid, riders, "start")
        i = pl.program_id(0)
        dy = None
        for n in range(npairs):
            d = _dg(refs[2 * n][...], refs[2 * n + 1][...], dims)
            dy = d if dy is None else dy + d
        x = h_ref[...]
        r = lax.rsqrt(jnp.mean(x * x, axis=-1, keepdims=True) + RMS_EPS)
        xh = x * r
        gdy = dy * g_ref[...]
        dh = dres_ref[...] + (gdy - xh * jnp.mean(gdy * xh, axis=-1, keepdims=True)) * r
        dh_ref[...] = dh
        dhb_ref[...] = dh.astype(BF16)
        gpart = jnp.sum((dy * xh).reshape(tm // 8, 8, D), axis=0)

        @pl.when(i == 0)
        def _():
            gacc_ref[...] = gpart

        @pl.when(i > 0)
        def _():
            gacc_ref[...] += gpart

        @pl.when(i == nt - 1)
        def _():
            dg_ref[...] = jnp.sum(gacc_ref[...], axis=0, keepdims=True)

        if nr:
            rider.ride(grid, riders, "finish")

    outs = pl.pallas_call(
        body, name=name, grid=grid, in_specs=specs + (rider.in_specs if nr else []),
        out_specs=[row_spec, row_spec, _spec((1, D), lambda i: (0, 0))] + (rider.out_specs if nr else []),
        out_shape=[jax.ShapeDtypeStruct((t, D), F32), jax.ShapeDtypeStruct((t, D), BF16),
                   jax.ShapeDtypeStruct((1, D), F32)] + (rider.out_shape if nr else []),
        scratch_shapes=[pltpu.VMEM((8, D), F32)] + (rider.scratch if nr else []),
        compiler_params=pltpu.CompilerParams(dimension_semantics=("arbitrary",),
                                             vmem_limit_bytes=VMEM_LIMIT_BIG),
    )(*operands, *(rider.groups if nr else []))
    return outs[0], outs[1], outs[2], (rider.place(outs[3:]) if nr else [])


def _final_loss(h, gain_slab, target):
    t = h.shape[0]
    tm = min(ROW_TILE, t)
    nt = t // tm

    def body(h_ref, g_ref, t_ref, dh_ref, dhb_ref, dg_ref, loss_ref, acc_ref, lacc_ref):
        i = pl.program_id(0)
        x = h_ref[...]
        g = g_ref[...]
        r = lax.rsqrt(jnp.mean(x * x, axis=-1, keepdims=True) + RMS_EPS)
        xh = x * r
        err = xh * g - t_ref[...]
        dy = err * (1.0 / D)
        gdy = dy * g
        dh = (gdy - xh * jnp.mean(gdy * xh, axis=-1, keepdims=True)) * r
        dh_ref[...] = dh
        dhb_ref[...] = dh.astype(BF16)
        part = jnp.sum((dy * xh).reshape(tm // 8, 8, D), axis=0)
        lpart = jnp.sum((err * err).reshape(tm // 8, 8, D), axis=0)

        @pl.when(i == 0)
        def _():
            acc_ref[...] = part
            lacc_ref[...] = lpart

        @pl.when(i > 0)
        def _():
            acc_ref[...] += part
            lacc_ref[...] += lpart

        @pl.when(i == nt - 1)
        def _():
            dg_ref[...] = jnp.sum(acc_ref[...], axis=0, keepdims=True)
            rows = jnp.sum(lacc_ref[...], axis=0, keepdims=True)
            loss_ref[...] = jnp.sum(rows, axis=1, keepdims=True) * (0.5 / D)

    row_spec = _spec((tm, D), lambda i: (i, 0))
    return pl.pallas_call(
        body, name="final_loss", grid=(nt,),
        in_specs=[row_spec, _spec((None, 1, D), lambda i: (R_FIN, 0, 0)), row_spec],
        out_specs=[row_spec, row_spec, _spec((1, D), lambda i: (0, 0)), _spec((1, 1), lambda i: (0, 0))],
        out_shape=[jax.ShapeDtypeStruct((t, D), F32), jax.ShapeDtypeStruct((t, D), BF16),
                   jax.ShapeDtypeStruct((1, D), F32), jax.ShapeDtypeStruct((1, 1), F32)],
        scratch_shapes=[pltpu.VMEM((8, D), F32), pltpu.VMEM((8, D), F32)],
        compiler_params=_params(("arbitrary",)),
    )(h, gain_slab, target)


def _ffn_up(name, hn, wgu, carry=None):
    t = hn.shape[0]
    tm = min(MM_TILE, t)
    grid = (NSH, t // tm)
    nc = carry.n if carry is not None else 0

    def body(*refs):
        x_ref, wg_ref, wu_ref = refs[:3]
        s_up_ref, s_gate_ref, a_ref = refs[3 + nc:6 + nc]
        riders = (refs[3:3 + nc], refs[6 + nc:6 + 2 * nc], refs[-2], refs[-1]) if nc else None
        if nc:
            carry.ride(grid, riders, "start")
        x = x_ref[...]
        g = _dg(x, wg_ref[...], NT)
        u = _dg(x, wu_ref[...], NT)
        sg = _sigmoid(g)
        silu = g * sg
        s_up_ref[...] = (MACARON * silu).astype(BF16)
        s_gate_ref[...] = (MACARON * u * (sg * (1.0 + g * (1.0 - sg)))).astype(BF16)
        a_ref[...] = (silu * u).astype(BF16)
        if nc:
            carry.ride(grid, riders, "finish")

    act = _spec((None, tm, FS), lambda s, i: (s, i, 0))
    shape = jax.ShapeDtypeStruct((NSH, t, FS), BF16)
    outs = pl.pallas_call(
        body, name=name, grid=grid,
        in_specs=[_spec((tm, D), lambda s, i: (i, 0)),
                  _spec((None, None, FS, D), lambda s, i: (s, 0, 0, 0)),
                  _spec((None, None, FS, D), lambda s, i: (s, 1, 0, 0))] + (carry.in_specs if nc else []),
        out_specs=[act, act, act] + (carry.out_specs if nc else []),
        out_shape=[shape, shape, shape] + (carry.out_shape if nc else []),
        scratch_shapes=carry.scratch if nc else [],
        compiler_params=_params(("arbitrary", "arbitrary") if nc else ("parallel", "parallel")),
    )(hn, wgu, wgu, *(carry.groups if nc else []))
    return (outs[0], outs[1], outs[2], carry.place(outs[3:]) if nc else [])


def _ffn_down(name, a, wd, h, carry=None, norm=None):
    t = h.shape[0]
    tm = min(HALF_TILE, t)
    row_spec = _spec((tm, D), lambda i: (i, 0))
    return _mm(name, [(a, _spec((None, tm, FS), lambda i, s=s: (s, i, 0)), wd, _spec((None, FS, D), lambda i, s=s: (s, 0, 0)))
                      for s in range(NSH)],
               grid=(t // tm,), o_shape=(t, D), o_dtype=F32, o_spec=row_spec, dims=NN, kaxis=0, nk=1,
               res=(h, row_spec), scale=MACARON, carry=carry, norm=norm)


def _in_proj(name, hn, w_in, two_branches, carry=None):
    t = hn.shape[0]
    tm = min(HALF_TILE, t)
    grid = (t // tm,)
    cols = w_in.shape[2]
    nc = carry.n if carry is not None else 0
    n_out = 2 if two_branches else 1

    def body(*refs):
        x_ref, w_ref = refs[:2]
        o_refs = refs[2 + nc:2 + nc + n_out]
        riders = (refs[2:2 + nc], refs[2 + nc + n_out:2 + 2 * nc + n_out], refs[-2], refs[-1]) if nc else None
        if nc:
            carry.ride(grid, riders, "start")
        x = x_ref[...]
        if two_branches:
            for s in range(2):
                o_refs[0][:, s * cols:(s + 1) * cols] = _dg(x, w_ref[s], NN)
                o_refs[1][:, s * cols:(s + 1) * cols] = _dg(x, w_ref[s + 2], NN).astype(BF16)
        else:
            for s in range(NSH):
                o_refs[0][s] = _dg(x, w_ref[s], NN)
        if nc:
            carry.ride(grid, riders, "finish")

    if two_branches:
        out_specs = [_spec((tm, 2 * cols), lambda i: (i, 0))] * 2
        out_shape = [jax.ShapeDtypeStruct((t, 2 * cols), F32), jax.ShapeDtypeStruct((t, 2 * cols), BF16)]
    else:
        out_specs = [_spec((NSH, tm, cols), lambda i: (0, i, 0))]
        out_shape = [jax.ShapeDtypeStruct((NSH, t, cols), F32)]
    outs = pl.pallas_call(
        body, name=name, grid=grid,
        in_specs=[_spec((tm, D), lambda i: (i, 0)), _spec((NSH, D, cols), lambda i: (0, 0, 0))]
        + (carry.in_specs if nc else []),
        out_specs=out_specs + (carry.out_specs if nc else []),
        out_shape=out_shape + (carry.out_shape if nc else []),
        scratch_shapes=carry.scratch if nc else [],
        compiler_params=_params(("arbitrary",) if nc else ("parallel",)),
    )(hn, w_in, *(carry.groups if nc else []))
    main = (outs[0], outs[1]) if two_branches else outs[0]
    return (main, carry.place(outs[n_out:])) if nc else main


def _ffn_bwd_up(name, dhb, wd, s_up, s_gate, rider=None):
    t = dhb.shape[0]
    tm = min(2 * MM_TILE, t)
    grid = (t // tm, NSH)
    nr = rider.n if rider is not None else 0

    def body(*refs):
        dh_ref, wd_ref, s_up_ref, s_gate_ref = refs[:4]
        dg_ref, du_ref = refs[4 + nr:6 + nr]
        riders = (refs[4:4 + nr], refs[6 + nr:6 + 2 * nr], refs[-2], refs[-1]) if nr else None
        if nr:
            rider.ride(grid, riders, "start")
        da = _dg(dh_ref[...], wd_ref[...], NT).astype(BF16)
        du_ref[...] = da * s_up_ref[...]
        dg_ref[...] = da * s_gate_ref[...]
        if nr:
            rider.ride(grid, riders, "finish")

    act = _spec((None, tm, FS), lambda i, s: (s, i, 0))
    shape = jax.ShapeDtypeStruct((NSH, t, FS), BF16)
    outs = pl.pallas_call(
        body, name=name, grid=grid,
        in_specs=[_spec((tm, D), lambda i, s: (i, 0)), _spec((None, FS, D), lambda i, s: (s, 0, 0)), act, act]
        + (rider.in_specs if nr else []),
        out_specs=[act, act] + (rider.out_specs if nr else []),
        out_shape=[shape, shape] + (rider.out_shape if nr else []),
        scratch_shapes=rider.scratch if nr else [],
        compiler_params=pltpu.CompilerParams(
            dimension_semantics=("arbitrary", "arbitrary") if nr else ("parallel", "parallel"),
            vmem_limit_bytes=VMEM_LIMIT_BIG),
    )(dhb, wd, s_up, s_gate, *(rider.groups if nr else []))
    return outs[0], outs[1], (rider.place(outs[2:]) if nr else [])


def _wgrad(name, a, a_spec, b, b_spec, out_rows, out_cols, t, tt, group, slot, scale=None, carry=None):
    first = isinstance(group, int)
    rows = group if first else group.shape[1]
    return _mm(name, [(a, a_spec, b, b_spec)], grid=(NSH, t // tt),
               o_shape=(NSH, rows, out_cols), o_dtype=BF16,
               o_spec=_spec((None, out_rows, out_cols), lambda s, k: (s, slot, 0)),
               dims=TN, kaxis=1, nk=t // tt, acc_shape=(out_rows, out_cols), scale=scale,
               into=None if first else group, carry=carry)


def _wgrad_gate_up(name, dg, du, hn, group, gate_idx):
    t = hn.shape[0]
    first = isinstance(group, int)
    rows = group if first else group.shape[1]

    def body(dg_ref, du_ref, hn_ref, *refs):
        o_ref = refs[-1]
        x = hn_ref[...]
        o_ref[...] = jnp.concatenate([_dg(dg_ref[...], x, TN), _dg(du_ref[...], x, TN)], axis=0).astype(BF16)

    hid = _spec((None, t, FS), lambda s: (s, 0, 0))
    return pl.pallas_call(
        body, name=name, grid=(NSH,),
        in_specs=[hid, hid, _spec((t, D), lambda s: (0, 0))] + ([] if first else [ANY]),
        out_specs=_spec((None, 2 * FS, D), lambda s: (s, gate_idx // 2, 0)),
        out_shape=jax.ShapeDtypeStruct((NSH, rows, D), BF16),
        input_output_aliases={} if first else {3: 0},
        compiler_params=pltpu.CompilerParams(dimension_semantics=("parallel",), vmem_limit_bytes=VMEM_LIMIT_BIG),
    )(dg, du, hn, *([] if first else [group]))


def _ffn_backward(tag, dh, dhb, h_in, hn, s_up, s_gate, a, wgu, wd, gate_idx, up_idx, down_idx, small, norm_row,
                  grad_a, grad_b, rider_up=None, rider_dhn=None):
    t = dh.shape[0]
    tt = min(WGRAD_TILE, t)
    tok = _spec((tt, D), lambda s, k: (k, 0))
    hid = _spec((None, tt, FS), lambda s, k: (s, k, 0))
    grad_b = _wgrad(tag + "_dwd", a, hid, dhb, tok, FS, D, t, tt, grad_b, down_idx, scale=MACARON)
    rider = rider_up(grad_a, grad_b) if rider_up is not None else None
    dg, du, landed_up = _ffn_bwd_up(tag + "_bwd_up", dhb, wd, s_up, s_gate, rider)
    assert up_idx == gate_idx + 1 and gate_idx % 2 == 0
    grad_a = _wgrad_gate_up(tag + "_dwgu", dg, du, hn, grad_a, gate_idx)
    rider = rider_dhn(grad_a, grad_b, landed_up) if rider_dhn is not None else None
    th = min(HALF_TILE, t)
    pairs = []
    for s in range(NSH):
        act = _spec((None, th, FS), lambda i, s=s: (s, i, 0))
        pairs += [(dg, act, wgu, _spec((None, None, FS, D), lambda i, s=s: (s, 0, 0, 0))),
                  (du, act, wgu, _spec((None, None, FS, D), lambda i, s=s: (s, 1, 0, 0)))]
    dh_in, dhb_in, d_gain, landed_dhn = _dhn_norm(tag + "_dhn", pairs, NN, h_in, small, norm_row, dh, rider)
    return dh_in, dhb_in, grad_a, grad_b, d_gain, landed_up, landed_dhn


def _conv_fwd(proj_a, conv_w):
    t = proj_a.shape[0]
    tm = min(ROW_TILE, t)
    hb = tm // CONV_HALO

    def body(ab_ref, ac_ref, ax_ref, acp_ref, axp_ref, w_ref, y_ref):
        i = pl.program_id(1)
        u = ac_ref[...] * ax_ref[...]
        up = jnp.where(i > 0, acp_ref[...] * axp_ref[...], 0.0)
        ext = jnp.concatenate([up, u], axis=0)
        u1 = pltpu.roll(ext, 1, 0)[CONV_HALO:]
        u2 = pltpu.roll(ext, 2, 0)[CONV_HALO:]
        w = w_ref[...]
        conv = w[0:1] * u2 + w[1:2] * u1 + w[2:3] * u
        y_ref[...] = (ab_ref[...] * conv).astype(BF16)

    def cur(off):
        return _spec((tm, LANE), lambda j, i: (i, off + j))

    def prev(off):
        return _spec((CONV_HALO, LANE), lambda j, i: (jnp.maximum(i * hb - 1, 0), off + j))

    return pl.pallas_call(
        body, name="conv_fwd", grid=(4, t // tm),
        in_specs=[cur(0), cur(4), cur(8), prev(4), prev(8),
                  _spec((None, None, 8, LANE), lambda j, i: (j, 0, 0, 0))],
        out_specs=_spec((tm, LANE), lambda j, i: (i, j)),
        out_shape=jax.ShapeDtypeStruct((t, 512), BF16),
        compiler_params=_params(("parallel", "parallel")),
    )(proj_a, proj_a, proj_a, proj_a, proj_a, conv_w)


def _conv_bwd(proj_a, conv_w, dy, rider=None):
    t = proj_a.shape[0]
    tm = min(ROW_TILE, t)
    hb = tm // CONV_HALO
    nt = t // tm
    grid = (4, nt)
    nr = rider.n if rider is not None else 0

    def body(*refs):
        ab_ref, ac_ref, ax_ref, dy_ref, acp_ref, axp_ref, abn_ref, dyn_ref, w_ref = refs[:9]
        dab_ref, dac_ref, dax_ref, dw_ref = refs[9 + nr:13 + nr]
        acc_ref = refs[13 + 2 * nr]
        riders = (refs[9:9 + nr], refs[13 + nr:13 + 2 * nr], refs[-2], refs[-1]) if nr else None
        if nr:
            rider.ride(grid, riders, "start")
        i = pl.program_id(1)
        ab, ac, ax = ab_ref[...], ac_ref[...], ax_ref[...]
        u = ac * ax
        up = jnp.where(i > 0, acp_ref[...] * axp_ref[...], 0.0)
        ext = jnp.concatenate([up, u], axis=0)
        u1 = pltpu.roll(ext, 1, 0)[CONV_HALO:]
        u2 = pltpu.roll(ext, 2, 0)[CONV_HALO:]
        w = w_ref[...]
        conv = w[0:1] * u2 + w[1:2] * u1 + w[2:3] * u
        dy_v = dy_ref[...]
        dab_ref[...] = (dy_v * conv).astype(BF16)
        dc = dy_v * ab
        dcn = jnp.where(i < nt - 1, dyn_ref[...] * abn_ref[...], 0.0)
        extn = jnp.concatenate([dc, dcn], axis=0)
        n = tm + CONV_HALO
        dc1 = pltpu.roll(extn, n - 1, 0)[:tm]
        dc2 = pltpu.roll(extn, n - 2, 0)[:tm]
        du = w[2:3] * dc + w[1:2] * dc1 + w[0:1] * dc2
        dac_ref[...] = (du * ax).astype(BF16)
        dax_ref[...] = (du * ac).astype(BF16)
        rid = lax.broadcasted_iota(jnp.int32, (8, LANE), 0)
        part = jnp.where(rid == 0, jnp.sum(dc * u2, axis=0, keepdims=True),
                         jnp.where(rid == 1, jnp.sum(dc * u1, axis=0, keepdims=True),
                                   jnp.where(rid == 2, jnp.sum(dc * u, axis=0, keepdims=True), 0.0)))

        @pl.when(i == 0)
        def _():
            acc_ref[...] = part

        @pl.when(i > 0)
        def _():
            acc_ref[...] += part

        @pl.when(i == nt - 1)
        def _():
            dw_ref[...] = acc_ref[...]

        if nr:
            rider.ride(grid, riders, "finish")

    def cur(off):
        return _spec((tm, LANE), lambda j, i: (i, off + j))

    def prev(off):
        return _spec((CONV_HALO, LANE), lambda j, i: (jnp.maximum(i * hb - 1, 0), off + j))

    def nxt(off):
        return _spec((CONV_HALO, LANE), lambda j, i: (jnp.minimum((i + 1) * hb, nt * hb - 1), off + j))

    outs = pl.pallas_call(
        body, name="conv_bwd", grid=grid,
        in_specs=[cur(0), cur(4), cur(8), cur(0), prev(4), prev(8), nxt(0), nxt(0),
                  _spec((None, None, 8, LANE), lambda j, i: (j, 0, 0, 0))] + (rider.in_specs if nr else []),
        out_specs=[_spec((tm, LANE), lambda j, i: (i, j)), _spec((tm, LANE), lambda j, i: (i, j)),
                   _spec((tm, LANE), lambda j, i: (i, j)), _spec((None, 8, LANE), lambda j, i: (j, 0, 0))]
        + (rider.out_specs if nr else []),
        out_shape=[jax.ShapeDtypeStruct((t, 512), BF16), jax.ShapeDtypeStruct((t, 512), BF16),
                   jax.ShapeDtypeStruct((t, 512), BF16), jax.ShapeDtypeStruct((4, 8, LANE), F32)]
        + (rider.out_shape if nr else []),
        scratch_shapes=[pltpu.VMEM((8, LANE), F32)] + (rider.scratch if nr else []),
        compiler_params=_params(("arbitrary", "arbitrary") if nr else ("parallel", "arbitrary")),
    )(proj_a, proj_a, proj_a, dy, proj_a, proj_a, proj_a, dy, conv_w, *(rider.groups if nr else []))
    return outs[0], outs[1], outs[2], outs[3], (rider.place(outs[4:]) if nr else [])


def _log_sigmoid(z):
    return jnp.minimum(z, 0.0) - jnp.log(1.0 + jnp.exp(-jnp.abs(z)))


def _sb_masks():
    row = lax.broadcasted_iota(jnp.int32, (SBQ, SBQ), 0)
    col = lax.broadcasted_iota(jnp.int32, (SBQ, SBQ), 1)
    return row, col


def _ones_where(mask):
    return jnp.where(mask, 1.0, 0.0).astype(BF16)


def _head_mask(head):
    lane = lax.broadcasted_iota(jnp.int32, (1, LANE), 1)
    return lane >= 64 if head else lane < 64


def _sb_fwd(proj_b, carry=None):
    t = proj_b.shape[0]
    nq = t // SBQ
    scale = 1.0 / math.sqrt(64.0)

    npair = SB_PAIRS_FWD
    wide = npair * LANE
    ngrp = 4 // npair
    chains = [(p, head) for p in range(npair) for head in range(2)]

    grid = (ngrp, nq)
    nc = carry.n if carry is not None else 0

    def body(*refs):
        q_ref, k_ref, v_ref = refs[:3]
        y_ref, l_ref, n_ref = refs[3 + nc:6 + nc]
        riders = (refs[3:3 + nc], refs[6 + nc:6 + 2 * nc], refs[-2], refs[-1]) if nc else None
        if nc:
            carry.ride(grid, riders, "start")
        grp = pl.program_id(0)
        qi = pl.program_id(1)
        row, col = _sb_masks()
        m_suffix = _ones_where(row > col)
        rows = len(chains) * SBQ
        strict = (lax.broadcasted_iota(jnp.int32, (rows, SBQ), 1)
                  < (lax.broadcasted_iota(jnp.int32, (rows, SBQ), 0) & (SBQ - 1)))
        q_pair = []
        for p in range(npair):
            q_all = q_ref[:, p * LANE:(p + 1) * LANE]
            q_pair.append(jnp.concatenate([jnp.where(_head_mask(head), q_all, jnp.zeros_like(q_all)) for head in range(2)],
                                          axis=0))

        def block(kb, state, diag):
            run, acc = state
            start = pl.multiple_of(kb * SBQ, SBQ)
            z = jnp.concatenate([_dg(q_pair[p], k_ref[pl.ds(start, SBQ), p * LANE:(p + 1) * LANE], NT)
                                 for p in range(npair)], axis=0) * scale
            lb = _log_sigmoid(z)
            lk = lb - z
            if diag:
                lk = jnp.where(strict, lk, 0.0)
            hi, lo = _split(lk)
            sums = _dg(jnp.concatenate([hi, lo], axis=0), m_suffix, NN)
            w = jnp.exp(lb + (run + sums[:rows] + sums[rows:]))
            if diag:
                w = jnp.where(strict, w, 0.0)
            wb = w.astype(BF16)
            acc = acc + jnp.concatenate(
                [_dg(wb[2 * p * SBQ:2 * (p + 1) * SBQ], v_ref[pl.ds(start, SBQ), p * LANE:(p + 1) * LANE], NN)
                 for p in range(npair)], axis=0)
            run = run + jnp.sum(hi.astype(F32) + lo.astype(F32), axis=1, keepdims=True)
            return run, acc

        state = block(qi, (jnp.zeros((rows, 1), F32), jnp.zeros((rows, LANE), F32)), True)

        def live(c):
            return jnp.logical_and(c[0] < qi, jnp.max(c[1][0]) > SB_DEAD)

        def step(c):
            return c[0] + 1, block(qi - 1 - c[0], c[1], False)

        count, (run, acc) = lax.while_loop(live, step, (jnp.int32(0), state))
        n_ref[grp * nq + qi] = count.astype(F32)
        hm = _head_mask(0)
        for p in range(npair):
            lo_rows, hi_rows = slice(2 * p * SBQ, (2 * p + 1) * SBQ), slice((2 * p + 1) * SBQ, (2 * p + 2) * SBQ)
            y_ref[:, p * LANE:(p + 1) * LANE] = jnp.where(hm, acc[lo_rows], acc[hi_rows]).astype(BF16)
            l_ref[p] = jnp.where(hm, run[lo_rows], run[hi_rows])
        if nc:
            carry.ride(grid, riders, "finish")

    outs = pl.pallas_call(
        body, name="sb_fwd", grid=grid,
        in_specs=[_spec((SBQ, wide), lambda g, i: (i, g)),
                  _spec((t, wide), lambda g, i: (0, ngrp + g)),
                  _spec((t, wide), lambda g, i: (0, 2 * ngrp + g))] + (carry.in_specs if nc else []),
        out_specs=[_spec((SBQ, wide), lambda g, i: (i, g)), _spec((npair, SBQ, LANE), lambda g, i: (g, i, 0)),
                   pl.BlockSpec(memory_space=pltpu.SMEM)] + (carry.out_specs if nc else []),
        out_shape=[jax.ShapeDtypeStruct((t, 512), BF16), jax.ShapeDtypeStruct((4, t, LANE), F32),
                   jax.ShapeDtypeStruct((ngrp * nq,), F32)] + (carry.out_shape if nc else []),
        scratch_shapes=carry.scratch if nc else [],
        compiler_params=_params(("arbitrary", "arbitrary")),
    )(proj_b, proj_b, proj_b, *(carry.groups if nc else []))
    return outs[0], outs[1], outs[2], (carry.place(outs[3:]) if nc else [])


def _sb_bwd(proj_b, dy, ltot, nblk, rider=None):
    t = proj_b.shape[0]
    nq = t // SBQ
    scale = 1.0 / math.sqrt(64.0)

    npair = SB_PAIRS_BWD
    wide = npair * LANE
    ngrp = 4 // npair
    chains = [(p, head) for p in range(npair) for head in range(2)]
    grid = (ngrp, nq)
    nr = rider.n if rider is not None else 0
    per_count = SB_PAIRS_FWD // SB_PAIRS_BWD

    def body(*refs):
        q_ref, k_ref, v_ref, dy_ref, l_ref, n_ref = refs[:6]
        dq_ref, dk_ref, dv_ref = refs[6 + nr:9 + nr]
        dk_acc, dv_acc = refs[9 + 2 * nr:11 + 2 * nr]
        riders = (refs[6:6 + nr], refs[9 + nr:9 + 2 * nr], refs[-2], refs[-1]) if nr else None
        if nr:
            rider.ride(grid, riders, "start")
        grp = pl.program_id(0)
        qi = pl.program_id(1)

        @pl.when(qi == 0)
        def _():
            dk_acc[...] = jnp.zeros_like(dk_acc)
            dv_acc[...] = jnp.zeros_like(dv_acc)

        row, col = _sb_masks()
        m_prefix = _ones_where(row <= col)
        m_before = _ones_where(row < col)
        rows = len(chains) * SBQ
        strict = (lax.broadcasted_iota(jnp.int32, (rows, SBQ), 1)
                  < (lax.broadcasted_iota(jnp.int32, (rows, SBQ), 0) & (SBQ - 1)))
        q_pair, do_pair, ltot = [], [], []
        for p in range(npair):
            pl_ = slice(p * LANE, (p + 1) * LANE)
            q_all = q_ref[:, pl_]
            do_all = dy_ref[:, pl_].astype(BF16)
            q_pair.append(jnp.concatenate([jnp.where(_head_mask(h), q_all, jnp.zeros_like(q_all)) for h in range(2)], axis=0))
            do_pair.append(jnp.concatenate([jnp.where(_head_mask(h), do_all, jnp.zeros_like(do_all)) for h in range(2)], axis=0))
            ltot += [l_ref[p][:, h * 64:h * 64 + 1] for h in range(2)]
        ltot = jnp.concatenate(ltot, axis=0)

        def pair_rows(a, p):
            return a[2 * p * SBQ:2 * (p + 1) * SBQ]

        def block(kb, state, diag):
            seen, dseen, dq = state
            start = pl.multiple_of(kb * SBQ, SBQ)
            kk = [k_ref[pl.ds(start, SBQ), p * LANE:(p + 1) * LANE] for p in range(npair)]
            vv = [v_ref[pl.ds(start, SBQ), p * LANE:(p + 1) * LANE] for p in range(npair)]
            z = jnp.concatenate([_dg(q_pair[p], kk[p], NT) for p in range(npair)], axis=0) * scale
            lb = _log_sigmoid(z)
            lk = lb - z
            if diag:
                lk = jnp.where(strict, lk, 0.0)
            hi, lo = _split(lk)
            sums = _dg(jnp.concatenate([hi, lo], axis=0), m_prefix, NN)
            w = jnp.exp(lb + ((ltot - seen) - (sums[:rows] + sums[rows:])))
            if diag:
                w = jnp.where(strict, w, 0.0)
            wb = w.astype(BF16)
            da = w * jnp.concatenate([_dg(do_pair[p], vv[p], NT) for p in range(npair)], axis=0)
            dah, dal = _split(da)
            dsums = _dg(jnp.concatenate([dah, dal], axis=0), m_before, NN)
            sig = jnp.exp(lb)
            dz = (da * (1.0 - sig) - (dseen + dsums[:rows] + dsums[rows:]) * sig) * scale
            if diag:
                dz = jnp.where(strict, dz, 0.0)
            dzb = dz.astype(BF16)
            for p in range(npair):
                pl_ = slice(p * LANE, (p + 1) * LANE)
                dv_acc[pl.ds(start, SBQ), pl_] += _dg(pair_rows(wb, p), do_pair[p], TN)
                dk_acc[pl.ds(start, SBQ), pl_] += _dg(pair_rows(dzb, p), q_pair[p], TN)
            dq = dq + jnp.concatenate([_dg(pair_rows(dzb, p), kk[p], NN) for p in range(npair)], axis=0)
            seen = seen + jnp.sum(hi.astype(F32) + lo.astype(F32), axis=1, keepdims=True)
            dseen = dseen + jnp.sum(da, axis=1, keepdims=True)
            return seen, dseen, dq

        zero = (jnp.zeros((rows, 1), F32), jnp.zeros((rows, 1), F32), jnp.zeros((rows, LANE), F32))
        first = qi - n_ref[(grp // per_count) * nq + qi].astype(jnp.int32)
        state = lax.fori_loop(first, qi, lambda kb, c: block(kb, c, False), zero)
        _, _, dq = block(qi, state, True)
        for p in range(npair):
            dq_ref[:, p * LANE:(p + 1) * LANE] = jnp.where(
                _head_mask(0), dq[2 * p * SBQ:(2 * p + 1) * SBQ], dq[(2 * p + 1) * SBQ:(2 * p + 2) * SBQ]).astype(BF16)

        @pl.when(qi == nq - 1)
        def _():
            dk_ref[...] = dk_acc[...].astype(BF16)
            dv_ref[...] = dv_acc[...].astype(BF16)

        if nr:
            rider.ride(grid, riders, "finish")

    full = jax.ShapeDtypeStruct((t, 512), BF16)
    outs = pl.pallas_call(
        body, name="sb_bwd", grid=grid,
        in_specs=[_spec((SBQ, wide), lambda g, i: (i, g)),
                  _spec((t, wide), lambda g, i: (0, ngrp + g)),
                  _spec((t, wide), lambda g, i: (0, 2 * ngrp + g)),
                  _spec((SBQ, wide), lambda g, i: (i, ngrp + g)),
                  _spec((npair, SBQ, LANE), lambda g, i: (g, i, 0)),
                  pl.BlockSpec(memory_space=pltpu.SMEM)] + (rider.in_specs if nr else []),
        out_specs=[_spec((SBQ, wide), lambda g, i: (i, g)),
                   _spec((t, wide), lambda g, i: (0, g)), _spec((t, wide), lambda g, i: (0, g))]
        + (rider.out_specs if nr else []),
        out_shape=[full, full, full] + (rider.out_shape if nr else []),
        scratch_shapes=[pltpu.VMEM((t, wide), F32), pltpu.VMEM((t, wide), F32)] + (rider.scratch if nr else []),
        compiler_params=pltpu.CompilerParams(dimension_semantics=("arbitrary", "arbitrary"),
                                             vmem_limit_bytes=VMEM_LIMIT_BIG),
    )(proj_b, proj_b, proj_b, dy, ltot, nblk, *(rider.groups if nr else []))
    return outs[0], outs[1], outs[2], (rider.place(outs[3:]) if nr else [])


def _hgrn_gates(qr, fr, c0, c1):
    mx = jnp.maximum(c0, c1)
    e0, e1 = jnp.exp(c0 - mx), jnp.exp(c1 - mx)
    lb = e1 / (e0 + e1)
    sx = _sigmoid(fr)
    f = lb + (1.0 - lb) * sx
    k = (1.0 - lb) * (1.0 - sx)
    sq = _sigmoid(qr)
    return lb, sx, f, k, sq, qr * sq


def _chunk_sums(mask, x):
    n = x.shape[1]
    hi, lo = _split(x)
    both = _dg(_ones_where(mask), jnp.concatenate([hi, lo], axis=1), NN)
    return both[:, :n] + both[:, n:]


def _chunk_masks():
    row = lax.broadcasted_iota(jnp.int32, (CHUNK, CHUNK), 0)
    col = lax.broadcasted_iota(jnp.int32, (CHUNK, CHUNK), 1)
    return col <= row, col >= row


def _hgrn_fwd(proj_c, small, carry=None):
    t = proj_c.shape[1]
    nc = t // CHUNK
    nh = D // HD

    hps = HGRN_HPS
    wide = hps * HD
    grid = (nh // hps, nc)
    nr = carry.n if carry is not None else 0

    def body(*refs):
        p_ref, c0_ref, c1_ref, gam_ref = refs[:4]
        o_ref, y_ref, sst_ref = refs[4 + nr:7 + nr]
        st_ref = refs[7 + 2 * nr]
        riders = (refs[4:4 + nr], refs[7 + nr:7 + 2 * nr], refs[-2], refs[-1]) if nr else None
        if nr:
            carry.ride(grid, riders, "start")
        c = pl.program_id(1)

        @pl.when(c == 0)
        def _():
            st_ref[...] = jnp.zeros_like(st_ref)

        _, _, f_all, k_all, _, q_all = _hgrn_gates(p_ref[0], p_ref[1], c0_ref[...], c1_ref[...])
        tril, _ = _chunk_masks()
        b_all = _chunk_sums(tril, jnp.log(f_all))
        for j in range(hps):
            ln = slice(j * HD, (j + 1) * HD)
            st0 = st_ref[j]
            sst_ref[j] = st0
            v, g = p_ref[2, :, ln], p_ref[3, :, ln]
            q, k, b = q_all[:, ln], k_all[:, ln], b_all[:, ln]
            bm = b[CHUNK // 2 - 1:CHUNK // 2]
            bl = b[CHUNK - 1:CHUNK]
            qd = q * jnp.exp(b)
            qt = q * jnp.exp(b - bm)
            kt = k * jnp.exp(bm - b)
            kl = k * jnp.exp(bl - b)
            vb = v.astype(BF16)
            att = jnp.where(tril, _dot3(qt, kt, NT), 0.0)
            o = _dg(qd.astype(BF16), st0.astype(BF16), NT) + _dg(att.astype(BF16), vb, NN)
            st_ref[j] = st0 * jnp.exp(bl) + _dg(vb, kl.astype(BF16), TN)
            o_ref[:, ln] = o
            r = lax.rsqrt(jnp.mean(o * o, axis=-1, keepdims=True) + RMS_EPS)
            y_ref[:, ln] = (o * r * gam_ref[...] * (g * _sigmoid(g))).astype(BF16)
        if nr:
            carry.ride(grid, riders, "finish")

    outs = pl.pallas_call(
        body, name="hgrn_fwd", grid=grid,
        in_specs=[_spec((4, CHUNK, wide), lambda h, c: (0, c, h)),
                  _spec((None, 1, wide), lambda h, c: (R_CLB, 0, h)),
                  _spec((None, 1, wide), lambda h, c: (R_CLB + 1, 0, h)),
                  _spec((None, 1, HD), lambda h, c: (R_GAM, 0, 0))] + (carry.in_specs if nr else []),
        out_specs=[_spec((CHUNK, wide), lambda h, c: (c, h)), _spec((CHUNK, wide), lambda h, c: (c, h)),
                   _spec((None, hps, HD, HD), lambda h, c: (c, h, 0, 0))] + (carry.out_specs if nr else []),
        out_shape=[jax.ShapeDtypeStruct((t, D), F32), jax.ShapeDtypeStruct((t, D), BF16),
                   jax.ShapeDtypeStruct((nc, nh, HD, HD), F32)] + (carry.out_shape if nr else []),
        scratch_shapes=[pltpu.VMEM((hps, HD, HD), F32)] + (carry.scratch if nr else []),
        compiler_params=_params(("arbitrary", "arbitrary")),
    )(proj_c, small, small, small, *(carry.groups if nr else []))
    return outs[0], outs[1], outs[2], (carry.place(outs[3:]) if nr else [])


def _hgrn_bwd(proj_c, small, o, sst, dyc, rider=None):
    t = proj_c.shape[1]
    nc = t // CHUNK
    nh = D // HD

    hps = HGRN_HPS
    wide = hps * HD
    ng = nh // hps
    grid = (ng, nc)
    nr = rider.n if rider is not None else 0

    def body(*refs):
        p_ref, c0_ref, c1_ref, gam_ref, o_ref, sst_ref, dy_ref = refs[:7]
        dp_ref, dclb_ref, dgam_ref = refs[7 + nr:10 + nr]
        dst_ref, dlb_acc, dgam_acc = refs[10 + 2 * nr:13 + 2 * nr]
        riders = (refs[7:7 + nr], refs[10 + nr:10 + 2 * nr], refs[-2], refs[-1]) if nr else None
        if nr:
            rider.ride(grid, riders, "start")
        group = pl.program_id(0)
        step = pl.program_id(1)

        @pl.when(step == 0)
        def _():
            dst_ref[...] = jnp.zeros_like(dst_ref)
            dlb_acc[...] = jnp.zeros_like(dlb_acc)

        @pl.when((step == 0) & (group == 0))
        def _():
            dgam_acc[...] = jnp.zeros_like(dgam_acc)

        gam = gam_ref[...]
        qr_all = p_ref[0]
        lb_all, sx_all, f_all, k_all, sq_all, q_all = _hgrn_gates(qr_all, p_ref[1], c0_ref[...], c1_ref[...])
        tril, triu = _chunk_masks()
        b_all = _chunk_sums(tril, jnp.log(f_all))
        dq_parts, dk_parts, db_parts, dgam_parts = [], [], [], []
        for j in range(hps):
            ln = slice(j * HD, (j + 1) * HD)
            st0 = sst_ref[j]
            dst1 = dst_ref[j]
            v, g = p_ref[2, :, ln], p_ref[3, :, ln]
            q, k, b = q_all[:, ln], k_all[:, ln], b_all[:, ln]
            bm = b[CHUNK // 2 - 1:CHUNK // 2]
            bl = b[CHUNK - 1:CHUNK]
            eb = jnp.exp(b)
            e_qt = jnp.exp(b - bm)
            e_kt = jnp.exp(bm - b)
            e_kl = jnp.exp(bl - b)
            e_bl = jnp.exp(bl)
            qd, qt, kt, kl = q * eb, q * e_qt, k * e_kt, k * e_kl
            ov = o_ref[:, ln]
            r = lax.rsqrt(jnp.mean(ov * ov, axis=-1, keepdims=True) + RMS_EPS)
            oh = ov * r
            sg = _sigmoid(g)
            dy = dy_ref[:, ln]
            dp_ref[3, :, ln] = (dy * oh * gam * (sg * (1.0 + g * (1.0 - sg)))).astype(BF16)
            dyv = dy * (g * sg)
            gdy = dyv * gam
            do = (gdy - oh * jnp.mean(gdy * oh, axis=-1, keepdims=True)) * r
            dob, vb = do.astype(BF16), v.astype(BF16)
            st0b, dst1b = st0.astype(BF16), dst1.astype(BF16)
            st1 = st0 * e_bl + _dg(vb, kl.astype(BF16), TN)
            att = jnp.where(tril, _dot3(qt, kt, NT), 0.0)
            datt = jnp.where(tril, _dg(dob, vb, NT), 0.0)
            dv = _dg(att.astype(BF16), dob, TN) + _dg(kl.astype(BF16), dst1b, NT)
            dq = _dot3(datt, kt, NN) * e_qt + _dg(dob, st0b, NN) * eb
            dk = _dot3(datt, qt, TN) * e_kt + _dg(vb, dst1b, NN) * e_kl
            db = q * dq - k * dk
            last = lax.broadcasted_iota(jnp.int32, (CHUNK, 1), 0) == CHUNK - 1
            db = db + jnp.where(last, jnp.sum(dst1 * st1, axis=0, keepdims=True), 0.0)
            dst_ref[j] = dst1 * e_bl + _dg(dob, qd.astype(BF16), TN)
            dp_ref[2, :, ln] = dv.astype(BF16)
            dq_parts.append(dq)
            dk_parts.append(dk)
            db_parts.append(db)
            dgam_parts.append(jnp.sum(dyv * oh, axis=0, keepdims=True))

        dq_all, dk_all = jnp.concatenate(dq_parts, axis=1), jnp.concatenate(dk_parts, axis=1)
        dlf = _chunk_sums(triu, jnp.concatenate(db_parts, axis=1))
        dp_ref[0] = (dq_all * (sq_all * (1.0 + qr_all * (1.0 - sq_all)))).astype(BF16)
        tmp = dlf / f_all - dk_all
        dp_ref[1] = (tmp * (1.0 - lb_all) * sx_all * (1.0 - sx_all)).astype(BF16)
        dlb_acc[...] += jnp.sum((1.0 - sx_all) * tmp, axis=0, keepdims=True)
        dgam_acc[...] += functools.reduce(lambda a, b: a + b, dgam_parts)

        @pl.when(step == nc - 1)
        def _():
            d1 = dlb_acc[...] * lb_all * (1.0 - lb_all)
            dclb_ref[...] = jnp.where(lax.broadcasted_iota(jnp.int32, (2, wide), 0) == 0, -d1, d1)

        @pl.when((step == nc - 1) & (group == ng - 1))
        def _():
            dgam_ref[...] = dgam_acc[...]

        if nr:
            rider.ride(grid, riders, "finish")

    rev = lambda h, s: (nc - 1 - s, h)
    outs = pl.pallas_call(
        body, name="hgrn_bwd", grid=grid,
        in_specs=[_spec((4, CHUNK, wide), lambda h, s: (0, nc - 1 - s, h)),
                  _spec((None, 1, wide), lambda h, s: (R_CLB, 0, h)),
                  _spec((None, 1, wide), lambda h, s: (R_CLB + 1, 0, h)),
                  _spec((None, 1, HD), lambda h, s: (R_GAM, 0, 0)),
                  _spec((CHUNK, wide), rev),
                  _spec((None, hps, HD, HD), lambda h, s: (nc - 1 - s, h, 0, 0)),
                  _spec((CHUNK, wide), rev)] + (rider.in_specs if nr else []),
        out_specs=[_spec((4, CHUNK, wide), lambda h, s: (0, nc - 1 - s, h)),
                   _spec((2, wide), lambda h, s: (0, h)),
                   _spec((1, HD), lambda h, s: (0, 0))] + (rider.out_specs if nr else []),
        out_shape=[jax.ShapeDtypeStruct((4, t, D), BF16), jax.ShapeDtypeStruct((2, D), F32),
                   jax.ShapeDtypeStruct((1, HD), F32)] + (rider.out_shape if nr else []),
        scratch_shapes=[pltpu.VMEM((hps, HD, HD), F32), pltpu.VMEM((1, wide), F32), pltpu.VMEM((1, HD), F32)]
        + (rider.scratch if nr else []),
        compiler_params=_params(("arbitrary", "arbitrary")),
    )(proj_c, small, small, small, o, sst, dyc, *(rider.groups if nr else []))
    return outs[0], outs[1], outs[2], (rider.place(outs[3:]) if nr else [])


def _adamw(name, w, grads, m, v):
    nl = len(grads)
    rows, cols = grads[0].shape
    br = rows
    for cand in (512, 352, 256):
        if rows % cand == 0:
            br = cand
            break
    nb = rows // br
    c1 = 1.0 - ADAM_B1 ** ADAM_STEP
    c2 = 1.0 - ADAM_B2 ** ADAM_STEP

    def body(w_ref, m_ref, v_ref, *refs):
        g_refs, (d_ref, mo_ref, vo_ref, go_ref) = refs[:nl], refs[nl:]
        layer = pl.program_id(0)
        gv = g_refs[0][...]
        for k in range(1, nl):
            gv = jnp.where(layer == k, g_refs[k][...], gv)
        mn = ADAM_B1 * m_ref[...] + (1.0 - ADAM_B1) * gv
        vn = ADAM_B2 * v_ref[...] + (1.0 - ADAM_B2) * (gv * gv)
        mo_ref[...] = mn
        vo_ref[...] = vn
        go_ref[...] = gv
        d_ref[...] = -ADAM_LR * ((mn / c1) / (jnp.sqrt(vn / c2) + ADAM_EPS) + ADAM_WD * w_ref[...])

    blk = _spec((br, cols), lambda l, i: (l * nb + i, 0))
    g_specs = [_spec((br, cols), lambda l, i, k=k: (jnp.where(l == k, i, 0), 0)) for k in range(nl)]
    shape = jax.ShapeDtypeStruct((nl * rows, cols), F32)
    return pl.pallas_call(
        body, name=name, grid=(nl, nb), in_specs=[blk] * 3 + g_specs, out_specs=[blk] * 4, out_shape=[shape] * 4,
        compiler_params=_params(("arbitrary", "arbitrary")),
    )(w, m, v, *grads)


def _place():
    x, y, c = lax.axis_index("x"), lax.axis_index("y"), lax.axis_index("c")
    chips = [(1 - x, y), (x, 1 - y), (1 - x, 1 - y)]
    return x, y, c, chips


class _Rider:
    n = 0
    relay = None

    def ride(self, grid, refs, when):
        if not self.n:
            return
        ids = [pl.program_id(a) for a in range(len(grid))]
        first = functools.reduce(jnp.logical_and, [i == 0 for i in ids])
        last = functools.reduce(jnp.logical_and, [i == g - 1 for i, g in zip(ids, grid)])
        phases = [(first, self.start), (last, self.relay)] if when == "start" else [(last, self.finish)]
        for cond, phase in phases:
            if phase is not None:
                pl.when(cond)(functools.partial(phase, *refs))


class _Gather(_Rider):
    PER_GROUP = 7

    def __init__(self, groups):
        self.groups = list(groups)
        self.n = len(self.groups)
        self.in_specs = [ANY] * self.n
        self.out_specs = [ANY] * self.n
        self.out_shape = [jax.ShapeDtypeStruct((NSH,) + g.shape, g.dtype) for g in self.groups]
        sems = pltpu.SemaphoreType.DMA((self.PER_GROUP * self.n,))
        self.scratch = [sems, sems] if self.n else []

    def _copies(self, ins, outs, send, recv):
        x, y, c, chips = _place()
        sibling = (x, y, 1 - c)

        def half(gi, chip, hc):
            return outs[gi].at[2 * chip[0] + chip[1], hc]

        def copy(gi, k, src, dst, to):
            sem = self.PER_GROUP * gi + k
            return pltpu.make_async_remote_copy(src_ref=src, dst_ref=dst, send_sem=send.at[sem], recv_sem=recv.at[sem],
                                                device_id=to, device_id_type=MESH)

        pairs = [(gi, j, chip) for gi in range(self.n) for j, chip in enumerate(chips)]
        first = [copy(gi, j, ins[gi].at[c], half(gi, (x, y), c), (*chip, c)) for gi, j, chip in pairs]
        first += [copy(gi, 6, ins[gi], outs[gi].at[2 * x + y], sibling) for gi in range(self.n)]
        landed = [copy(gi, j, half(gi, chip, c), half(gi, chip, c), sibling) for gi, j, chip in pairs]
        relay = [copy(gi, 3 + j, half(gi, chip, c), half(gi, chip, c), sibling) for gi, j, chip in pairs]
        relayed = [copy(gi, 3 + j, half(gi, chip, 1 - c), half(gi, chip, 1 - c), sibling) for gi, j, chip in pairs]
        relayed += [copy(gi, 6, ins[gi], outs[gi].at[2 * x + y], sibling) for gi in range(self.n)]
        return first, landed, relay, relayed

    def start(self, ins, outs, send, recv):
        for cp in self._copies(ins, outs, send, recv)[0]:
            cp.start()

    def relay(self, ins, outs, send, recv):
        _, landed, relay, _ = self._copies(ins, outs, send, recv)
        for arrived, onward in zip(landed, relay):
            arrived.wait_recv()
            onward.start()

    def finish(self, ins, outs, send, recv):
        first, _, relay, relayed = self._copies(ins, outs, send, recv)
        for cp in relayed:
            cp.wait_recv()
        for cp in first + relay:
            cp.wait_send()

    def place(self, outs):
        return list(outs)


def _alone(name, rider):
    n = rider.n

    def body(*refs):
        parts = (refs[:n], refs[n:2 * n], refs[2 * n], refs[2 * n + 1])
        rider.start(*parts)
        if rider.relay is not None:
            rider.relay(*parts)
        rider.finish(*parts)

    outs = pl.pallas_call(
        body, name=name, in_specs=rider.in_specs, out_specs=rider.out_specs, out_shape=rider.out_shape,
        scratch_shapes=rider.scratch, compiler_params=pltpu.CompilerParams(has_side_effects=True),
    )(*rider.groups)
    return rider.place(outs)


class _Swap(_Rider):
    def __init__(self, slots):
        self.slots = list(slots)
        self.groups = [buf for buf, _, _ in self.slots]
        self.n = len(self.slots)
        self.in_specs = [ANY] * self.n
        self.out_specs = [ANY] * self.n
        self.out_shape = [jax.ShapeDtypeStruct((NSH, rows // 2, buf.shape[2]), buf.dtype) for buf, _, rows in self.slots]
        self.scratch = [pltpu.SemaphoreType.DMA((self.n,)), pltpu.SemaphoreType.DMA((self.n,))]

    def _copies(self, ins, outs, send, recv):
        x, y, c, _ = _place()
        cps = []
        for i, (_, row0, rows) in enumerate(self.slots):
            half = rows // 2
            src = ins[i].at[:, pl.ds(pl.multiple_of(row0 + (1 - c) * half, 16), half)]
            cps.append(pltpu.make_async_remote_copy(src_ref=src, dst_ref=outs[i], send_sem=send.at[i],
                                                    recv_sem=recv.at[i], device_id=(x, y, 1 - c), device_id_type=MESH))
        return cps

    def start(self, ins, outs, send, recv):
        for cp in self._copies(ins, outs, send, recv):
            cp.start()

    def finish(self, ins, outs, send, recv):
        for cp in self._copies(ins, outs, send, recv):
            cp.wait()

    def place(self, outs):
        return list(outs)


class _Share(_Rider):
    def __init__(self, arrays):
        self.groups = list(arrays)
        self.n = len(self.groups)
        self.in_specs = [ANY] * self.n
        self.out_specs = [ANY] * self.n
        self.out_shape = [jax.ShapeDtypeStruct((2,) + g.shape, g.dtype) for g in self.groups]
        self.scratch = [pltpu.SemaphoreType.DMA((self.n,)), pltpu.SemaphoreType.DMA((self.n,))]

    def _copies(self, ins, outs, send, recv, half):
        x, y, c, _ = _place()
        return [pltpu.make_async_remote_copy(src_ref=ins[gi], dst_ref=outs[gi].at[c if half == "mine" else 1 - c],
                                             send_sem=send.at[gi], recv_sem=recv.at[gi], device_id=(x, y, 1 - c),
                                             device_id_type=MESH) for gi in range(self.n)]

    def start(self, ins, outs, send, recv):
        for cp in self._copies(ins, outs, send, recv, "mine"):
            cp.start()

    def finish(self, ins, outs, send, recv):
        for cp in self._copies(ins, outs, send, recv, "theirs"):
            cp.wait_recv()
        for cp in self._copies(ins, outs, send, recv, "mine"):
            cp.wait_send()

    def place(self, outs):
        c = lax.axis_index("c")
        return [lax.dynamic_update_index_in_dim(o, g, c, 0) for o, g in zip(outs, self.groups)]


class _Send(_Rider):
    def __init__(self, arrays):
        self.groups = list(arrays)
        self.n = len(self.groups)
        self.in_specs = [ANY] * self.n
        self.out_specs = [ANY] * self.n
        self.out_shape = [jax.ShapeDtypeStruct((3,) + g.shape[1:], g.dtype) for g in self.groups]
        self.scratch = [pltpu.SemaphoreType.DMA((3 * self.n,)), pltpu.SemaphoreType.DMA((3 * self.n,))]

    def _copies(self, ins, outs, send, recv):
        x, y, c, chips = _place()
        return [pltpu.make_async_remote_copy(src_ref=ins[gi].at[2 * chip[0] + chip[1]], dst_ref=outs[gi].at[j],
                                             send_sem=send.at[3 * gi + j], recv_sem=recv.at[3 * gi + j],
                                             device_id=(*chip, c), device_id_type=MESH)
                for gi in range(self.n) for j, chip in enumerate(chips)]

    def start(self, ins, outs, send, recv):
        for cp in self._copies(ins, outs, send, recv):
            cp.start()

    def finish(self, ins, outs, send, recv):
        for cp in self._copies(ins, outs, send, recv):
            cp.wait()

    def place(self, outs):
        return list(outs)


def _pair_sum(name, slots, got, c_idx):
    n = len(slots)
    in_specs, out_specs, out_shape, operands = [], [], [], []
    for (buf, row0, rows), g in zip(slots, got):
        hb, cols = rows // 4, buf.shape[2]
        in_specs += [pl.BlockSpec((None, hb, cols), lambda q, i, cr, r=row0 // hb: (q, r + 2 * cr[0] + i, 0)),
                     pl.BlockSpec((None, hb, cols), lambda q, i, cr: (q, i, 0))]
        out_specs.append(pl.BlockSpec((None, hb, cols), lambda q, i, cr: (q, i, 0)))
        out_shape.append(jax.ShapeDtypeStruct(g.shape, BF16))
        operands += [buf, g]

    def body(c_ref, *refs):
        for i in range(n):
            refs[2 * n + i][...] = (refs[2 * i][...].astype(F32) + refs[2 * i + 1][...].astype(F32)).astype(BF16)

    return pl.pallas_call(
        body, name=name,
        grid_spec=pltpu.PrefetchScalarGridSpec(num_scalar_prefetch=1, grid=(NSH, 2), in_specs=in_specs, out_specs=out_specs),
        out_shape=out_shape, compiler_params=_params(("parallel", "parallel")),
    )(c_idx, *operands)


def _owner_sum(name, pairs, got, p_idx):
    n = len(pairs)
    in_specs, out_specs, out_shape, operands = [], [], [], []
    for own, g in zip(pairs, got):
        _, rows, cols = own.shape
        hb = rows // 2
        in_specs += [pl.BlockSpec((None, hb, cols), lambda i, pr: (pr[0], i, 0)),
                     pl.BlockSpec((3, hb, cols), lambda i, pr: (0, i, 0))]
        out_specs.append(pl.BlockSpec((hb, cols), lambda i, pr: (i, 0)))
        out_shape.append(jax.ShapeDtypeStruct((rows, cols), F32))
        operands += [own, g]

    def body(p_ref, *refs):
        for i in range(n):
            a_ref, b_ref = refs[2 * i], refs[2 * i + 1]
            refs[2 * n + i][...] = ((a_ref[...].astype(F32) + b_ref[0].astype(F32)) + b_ref[1].astype(F32)) + b_ref[2].astype(F32)

    return pl.pallas_call(
        body, name=name,
        grid_spec=pltpu.PrefetchScalarGridSpec(num_scalar_prefetch=1, grid=(2,), in_specs=in_specs, out_specs=out_specs),
        out_shape=out_shape, compiler_params=_params(("parallel",)),
    )(p_idx, *operands)


def _sum_small(slab):
    def body(in_ref, out_ref, all_ref, send, recv):
        x, y, c, _ = _place()
        me = 4 * x + 2 * y + c
        all_ref[me] = in_ref[...]
        cps = []
        for k in range(1, 8):
            peer = (x ^ (k >> 2), y ^ ((k >> 1) & 1), c ^ (k & 1))
            cps.append(pltpu.make_async_remote_copy(src_ref=in_ref, dst_ref=all_ref.at[me], send_sem=send.at[k - 1],
                                                    recv_sem=recv.at[k - 1], device_id=peer, device_id_type=MESH))
        for cp in cps:
            cp.start()
        for cp in cps:
            cp.wait()
        total = all_ref[0]
        for d in range(1, 8):
            total = total + all_ref[d]
        out_ref[...] = total

    return pl.pallas_call(
        body, name="sum_small",
        in_specs=[pl.BlockSpec(memory_space=pltpu.VMEM)], out_specs=pl.BlockSpec(memory_space=pltpu.VMEM),
        out_shape=jax.ShapeDtypeStruct(slab.shape, F32),
        scratch_shapes=[pltpu.VMEM((8,) + slab.shape, F32), pltpu.SemaphoreType.DMA((7,)), pltpu.SemaphoreType.DMA((7,))],
        compiler_params=pltpu.CompilerParams(has_side_effects=True),
    )(slab)


FFNS = ("pre0", "post0", "pre1", "post1")
W_SHAPES = dict({f + "_gu": (NSH, 2, FS, D) for f in FFNS}, **{f + "_d": (NSH, FS, D) for f in FFNS},
                ab_in=(NSH, D, 768), ab_out=(NSH, 256, D), conv=(NSH, 2, 8, LANE), c_in=(NSH, D, D), c_out=(NSH, 256, D))
CARRIED = dict(pre0_norm=("pre0_gu",), pre0_up=("pre0_d",), pre0_down=("ab_in", "conv"), ab_proj=("ab_out",),
               c_proj=("post1_d",),
               sb_fwd=("post0_gu", "post0_d"), post0_up=("pre1_gu",), post0_down=("pre1_d",),
               pre1_up=("c_in", "c_out"), hgrn_fwd=("post1_gu",))


FFN_SLOTS = dict(pre0=(0, 1, 0), pre1=(2, 3, 1), post0=(4, 5, 2), post1=(6, 7, 3))
GRAD_SLOTS = dict(ab_out=("b", B_ABOUT, 256), c_in=("b", B_CIN, D), c_out=("b", B_COUT, 256), ab_in=("c", 0, D))
for _f, (_g, _u, _d) in FFN_SLOTS.items():
    GRAD_SLOTS.update({_f + "_g": ("a", _g * FS, FS), _f + "_u": ("a", _u * FS, FS), _f + "_d": ("b", _d * FS, FS)})
REDUCE_STAGES = dict(x=("post1_g", "post1_u", "post1_d", "c_out"), w=("c_in", "pre1_d"),
                     y=("pre1_g", "pre1_u", "post0_g", "post0_u", "post0_d", "ab_out"),
                     z1=("ab_in",), z2=("pre0_g", "pre0_u", "pre0_d"))


def _local_step(x, target, weights, small, shards=None, place=None):
    t = x.shape[0]
    th = min(HALF_TILE, t)
    tw = min(WGRAD_TILE, t)
    w = dict(weights)
    reduced = {}

    def stage_slots(stage, buffers):
        return [(buffers[GRAD_SLOTS[n][0]],) + GRAD_SLOTS[n][1:] for n in REDUCE_STAGES[stage]]

    def swap_rider(stage, buffers):
        return _Swap(stage_slots(stage, buffers)) if place is not None else None

    def reduce_start(stage, buffers, swapped=None):
        if place is None:
            return [], None
        slots = stage_slots(stage, buffers)
        if swapped is None:
            swapped = _alone("grad_swap_" + stage, _Swap(slots))
        pairs = _pair_sum("grad_pair_sum_" + stage, slots, swapped, place[0])
        return pairs, _Send(pairs)

    def reduce_end(stage, pairs, landed):
        if place is None:
            return None
        mine = _owner_sum("grad_owner_sum_" + stage, pairs, landed, place[1])
        return _Share(mine)

    def shared(stage, landed):
        if place is not None:
            reduced.update(zip(REDUCE_STAGES[stage], landed))

    def carried(kernel_name):
        names = [n for n in CARRIED[kernel_name] if n not in w]
        return names, (_Gather([shards[n] for n in names]) if names else None)

    def land(names, arrays):
        for n, a in zip(names, arrays):
            w[n] = a.reshape(W_SHAPES[n])

    def ffn_forward(tag, h, hn, next_row):
        names, gather = carried(tag + "_up") if tag + "_up" in CARRIED else ([], None)
        s_up, s_gate, a, got = _ffn_up(tag + "_up", hn, w[tag + "_gu"], gather)
        land(names, got)
        names, gather = carried(tag + "_down") if tag + "_down" in CARRIED else ([], None)
        out = _ffn_down(tag + "_down", a, w[tag + "_d"], h, gather, (small, next_row) if next_row is not None else None)
        if gather is not None:
            out, got = out
            land(names, got)
        out, hn_next = out if next_row is not None else (out, None)
        return out, hn_next, (h, hn, s_up, s_gate, a)

    def out_proj(name, y, w_out, h, next_row):
        row_spec = _spec((th, D), lambda i: (i, 0))
        return _mm(name, [(y, row_spec, w_out.reshape(D, D), _spec((D, D), lambda i: (0, 0)))],
                   grid=(t // th,), o_shape=(t, D), o_dtype=F32, o_spec=row_spec, dims=NN, kaxis=0, nk=1,
                   res=(h, row_spec), norm=(small, next_row))

    def out_proj_bwd(tag, dhb, y, w_out, blk, grad_b, make_rider=None):
        grad_b = _wgrad(tag + "_dwout", y, _spec((tw, 256), lambda s, k: (k, s)), dhb, _spec((tw, D), lambda s, k: (k, 0)),
                        256, D, t, tw, grad_b, blk)
        rider = make_rider(grad_b) if make_rider is not None else None
        row_spec = _spec((th, D), lambda i: (i, 0))
        dy = _mm(tag + "_dy", [(dhb, row_spec, w_out.reshape(D, D), _spec((D, D), lambda i: (0, 0)))],
                 grid=(t // th,), o_shape=(t, D), o_dtype=F32, o_spec=row_spec, dims=NT, kaxis=0, nk=1, carry=rider)
        dy, landed = dy if rider is not None else (dy, None)
        return dy, grad_b, landed

    h0 = x
    names, gather = carried("pre0_norm")
    hn, got = _norm_fwd("pre0_norm", h0, small, R_PRE, gather)
    land(names, got)
    h1, hn_ab, pre0 = ffn_forward("pre0", h0, hn, R_MIX)
    def carrying(kernel_name, make):
        names, gather = carried(kernel_name)
        out = make(gather)
        if gather is not None:
            out, got = out
            land(names, got)
        return out

    proj_a, proj_b = carrying("ab_proj", lambda gather: _in_proj("ab_proj", hn_ab, w["ab_in"], True, gather))
    y_a = _conv_fwd(proj_a, w["conv"])
    names, gather = carried("sb_fwd")
    y_b, ltot, nblk, got = _sb_fwd(proj_b, gather)
    land(names, got)
    y_ab = jnp.concatenate([y_a, y_b], axis=1)
    h2, hn = out_proj("ab_out", y_ab, w["ab_out"], h1, R_POST)
    h3, hn, post0 = ffn_forward("post0", h2, hn, R_PRE + 1)
    h4, hn_c, pre1 = ffn_forward("pre1", h3, hn, R_MIX + 1)
    proj_c = carrying("c_proj", lambda gather: _in_proj("c_proj", hn_c, w["c_in"], False, gather))
    names, gather = carried("hgrn_fwd")
    o_c, y_c, sst, got = _hgrn_fwd(proj_c, small, gather)
    land(names, got)
    h5, hn = out_proj("c_out", y_c, w["c_out"], h4, R_POST + 1)
    h6, _, post1 = ffn_forward("post1", h5, hn, None)
    dh, dhb, d_fin, loss = _final_loss(h6, small, target)

    dh, dhb, grad_a, grad_b, dn_post1, _, _ = _ffn_backward("post1", dh, dhb, *post1, w["post1_gu"], w["post1_d"],
                                                            *FFN_SLOTS["post1"], small, R_POST + 1, 8 * FS, B_ROWS)
    dy_c, grad_b, swapped = out_proj_bwd("c", dhb, y_c, w["c_out"], B_COUT // 256, grad_b,
                                         lambda buf_b: swap_rider("x", dict(a=grad_a, b=buf_b)))
    pairs, rider = reduce_start("x", dict(a=grad_a, b=grad_b), swapped)
    dproj_c, d_clb, d_gam, landed = _hgrn_bwd(proj_c, small, o_c, sst, dy_c, rider)
    share = reduce_end("x", pairs, landed)
    grad_b = _wgrad("c_dwin", hn_c, _spec((tw, D), lambda s, k: (k, 0)), dproj_c, _spec((None, tw, D), lambda s, k: (s, k, 0)),
                    D, D, t, tw, grad_b, B_CIN // D, carry=share)
    if share is not None:
        grad_b, landed = grad_b
        shared("x", landed)
    dh, dhb, dn_mix1, _ = _dhn_norm("c_dhn", [(dproj_c, _spec((None, th, D), lambda i, s=s: (s, i, 0)),
                                               w["c_in"], _spec((None, D, D), lambda i, s=s: (s, 0, 0)))
                                              for s in range(NSH)],
                                    NT, h4, small, R_MIX + 1, dh)
    stage_w = {}

    def w_swap(buf_a, buf_b):
        return swap_rider("w", dict(a=buf_a, b=buf_b))

    def w_send(buf_a, buf_b, swapped):
        stage_w["pairs"], rider = reduce_start("w", dict(a=buf_a, b=buf_b), swapped)
        return rider

    on = place is not None
    dh, dhb, grad_a, grad_b, dn_pre1, _, landed = _ffn_backward(
        "pre1", dh, dhb, *pre1, w["pre1_gu"], w["pre1_d"], *FFN_SLOTS["pre1"], small, R_PRE + 1, grad_a, grad_b,
        w_swap if on else None, w_send if on else None)
    share = reduce_end("w", stage_w.get("pairs"), landed)
    dh, dhb, grad_a, grad_b, dn_post0, landed, _ = _ffn_backward(
        "post0", dh, dhb, *post0, w["post0_gu"], w["post0_d"], *FFN_SLOTS["post0"], small, R_POST, grad_a, grad_b,
        (lambda buf_a, buf_b: share) if on else None)
    shared("w", landed)
    dy_ab, grad_b, _ = out_proj_bwd("ab", dhb, y_ab, w["ab_out"], B_ABOUT // 256, grad_b)
    dab, dac, dax, d_conv, swapped = _conv_bwd(proj_a, w["conv"], dy_ab, swap_rider("y", dict(a=grad_a, b=grad_b)))
    pairs, rider = reduce_start("y", dict(a=grad_a, b=grad_b), swapped)
    dq, dk, dv, landed = _sb_bwd(proj_b, dy_ab, ltot, nblk, rider)
    share = reduce_end("y", pairs, landed)
    dproj_ab = jnp.concatenate([dab, dac, dax, dq, dk, dv], axis=1)
    grad_c = _wgrad("ab_dwin", hn_ab, _spec((tw, D), lambda s, k: (k, 0)), dproj_ab, _spec((tw, 768), lambda s, k: (k, s)),
                    D, 768, t, tw, D, 0, carry=share)
    if share is not None:
        grad_c, landed = grad_c
        shared("y", landed)
    dh, dhb, dn_mix0, _ = _dhn_norm("ab_dhn", [(dproj_ab, _spec((th, 768), lambda i, s=s: (i, s)),
                                                w["ab_in"], _spec((None, D, 768), lambda i, s=s: (s, 0, 0)))
                                               for s in range(NSH)],
                                    NT, h1, small, R_MIX, dh)
    last = {}

    def z1_send(buf_a, buf_b):
        last["z1"], rider = reduce_start("z1", dict(b=buf_b, c=grad_c))
        return rider

    def z2_send(buf_a, buf_b, _):
        last["z2"], rider = reduce_start("z2", dict(a=buf_a, b=buf_b))
        return rider

    dh, dhb, grad_a, grad_b, dn_pre0, landed, landed_own = _ffn_backward(
        "pre0", dh, dhb, *pre0, w["pre0_gu"], w["pre0_d"], *FFN_SLOTS["pre0"], small, R_PRE, grad_a, grad_b,
        z1_send if on else None, z2_send if on else None)
    if place is not None:
        mine = (_owner_sum("grad_owner_sum_z1", last["z1"], landed, place[1])
                + _owner_sum("grad_owner_sum_z2", last["z2"], landed_own, place[1]))
        reduced.update(zip(REDUCE_STAGES["z1"] + REDUCE_STAGES["z2"], _alone("grad_share_z", _Share(mine))))
    zero = jnp.zeros((1, D), F32)
    conv_rows = jnp.pad(jnp.transpose(d_conv[:, :3, :], (1, 0, 2)).reshape(3, 512), ((0, 0), (0, D - 512)))
    small_grad = jnp.concatenate([
        dn_pre0, dn_pre1, dn_mix0, dn_mix1, dn_post0, dn_post1, d_clb, d_fin,
        jnp.pad(d_gam, ((0, 0), (0, D - HD))), conv_rows,
        jnp.pad(loss, ((0, 0), (0, D - 1))), zero, zero], axis=0)
    return dh, grad_a, grad_b, grad_c, small_grad, reduced


def _small_slab(rows):
    parts = [jnp.pad(r.astype(F32), ((0, 0), (0, D - r.shape[1]))) for r in rows]
    slab = jnp.concatenate(parts, axis=0)
    return jnp.pad(slab, ((0, SMALL_ROWS - slab.shape[0]), (0, 0)))


def kernel(x, ffn_pre_norm, ffn_pre_w_gate, ffn_pre_w_up, ffn_pre_w_down, mix_norm, ffn_post_norm, ffn_post_w_gate, ffn_post_w_up, ffn_post_w_down, ab_w_in, ab_conv_w, ab_w_out, c_w_in, c_lower_bounds, c_out_norm, c_w_out, final_norm, loss_target, m_ffn_pre_norm, m_ffn_pre_w_gate, m_ffn_pre_w_up, m_ffn_pre_w_down, m_mix_norm, m_ffn_post_norm, m_ffn_post_w_gate, m_ffn_post_w_up, m_ffn_post_w_down, m_ab_w_in, m_ab_conv_w, m_ab_w_out, m_c_w_in, m_c_lower_bounds, m_c_out_norm, m_c_w_out, m_final_norm, v_ffn_pre_norm, v_ffn_pre_w_gate, v_ffn_pre_w_up, v_ffn_pre_w_down, v_mix_norm, v_ffn_post_norm, v_ffn_post_w_gate, v_ffn_post_w_up, v_ffn_post_w_down, v_ab_w_in, v_ab_conv_w, v_ab_w_out, v_c_w_in, v_c_lower_bounds, v_c_out_norm, v_c_w_out, v_final_norm):
    t = x.shape[1]
    xi, yi, ci = lax.axis_index("x"), lax.axis_index("y"), lax.axis_index("c")
    p_idx = (2 * xi + yi).astype(jnp.int32).reshape(1)
    c_idx = ci.astype(jnp.int32).reshape(1)

    def halves(m):
        return m.astype(BF16).reshape(2, m.shape[0] // 2, m.shape[1])

    transposed = ("ffn_pre_w_gate", "ffn_pre_w_up", "ffn_post_w_gate", "ffn_post_w_up")

    def flip(a):
        return jnp.swapaxes(a, 1, 2)

    shards = {}
    for name, (w_gate, w_up, w_down, layer) in dict(
            pre0=(ffn_pre_w_gate, ffn_pre_w_up, ffn_pre_w_down, 0), post0=(ffn_post_w_gate, ffn_post_w_up, ffn_post_w_down, 0),
            pre1=(ffn_pre_w_gate, ffn_pre_w_up, ffn_pre_w_down, 1), post1=(ffn_post_w_gate, ffn_post_w_up, ffn_post_w_down, 1)).items():
        shards[name + "_gu"] = jnp.stack([flip(w_gate)[layer], flip(w_up)[layer]]).astype(BF16)
        shards[name + "_d"] = halves(w_down[layer])
    conv_pad = jnp.pad(ab_conv_w[0], ((0, 5), (0, 0)))
    shards.update(ab_in=halves(ab_w_in[0]), ab_out=halves(ab_w_out[0]), c_in=halves(c_w_in[0]), c_out=halves(c_w_out[0]),
                  conv=jnp.stack([conv_pad, jnp.zeros_like(conv_pad)]))
    small = _small_slab([ffn_pre_norm, mix_norm, ffn_post_norm, c_lower_bounds, final_norm.reshape(1, D), c_out_norm])
    small = small.reshape(SMALL_ROWS, 1, D)

    grad_x, _, _, _, small_grad, reduced = _local_step(x[0], loss_target[0], {}, small, shards,
                                                        (c_idx, p_idx))
    whole = {n: g.reshape(2 * g.shape[1], g.shape[2]) for n, g in reduced.items()}
    small_sum = _sum_small(small_grad)

    my_conv = lax.dynamic_slice(small_sum[R_CONV:R_CONV + 3], (0, (2 * xi + yi) * 128), (3, 128))
    grads = {
        "ffn_pre_norm": small_sum[R_PRE:R_PRE + 2], "mix_norm": small_sum[R_MIX:R_MIX + 2],
        "ffn_post_norm": small_sum[R_POST:R_POST + 2], "c_lower_bounds": small_sum[R_CLB:R_CLB + 2],
        "c_out_norm": small_sum[R_GAM:R_GAM + 1, :HD], "final_norm": small_sum[R_FIN],
        "ab_conv_w": my_conv.reshape(1, 3, 128),
    }
    layers = dict(ab_w_in=[whole["ab_in"]], ab_w_out=[whole["ab_out"]], c_w_in=[whole["c_in"]], c_w_out=[whole["c_out"]])
    for kind, key in (("gate", "_g"), ("up", "_u"), ("down", "_d")):
        layers["ffn_pre_w_" + kind] = [whole["pre0" + key], whole["pre1" + key]]
        layers["ffn_post_w_" + kind] = [whole["post0" + key], whole["post1" + key]]
    weights = dict(ffn_pre_norm=ffn_pre_norm, ffn_pre_w_gate=ffn_pre_w_gate, ffn_pre_w_up=ffn_pre_w_up, ffn_pre_w_down=ffn_pre_w_down, mix_norm=mix_norm, ffn_post_norm=ffn_post_norm, ffn_post_w_gate=ffn_post_w_gate, ffn_post_w_up=ffn_post_w_up, ffn_post_w_down=ffn_post_w_down, ab_w_in=ab_w_in, ab_conv_w=ab_conv_w, ab_w_out=ab_w_out, c_w_in=c_w_in, c_lower_bounds=c_lower_bounds, c_out_norm=c_out_norm, c_w_out=c_w_out, final_norm=final_norm)
    m_in = dict(ffn_pre_norm=m_ffn_pre_norm, ffn_pre_w_gate=m_ffn_pre_w_gate, ffn_pre_w_up=m_ffn_pre_w_up, ffn_pre_w_down=m_ffn_pre_w_down, mix_norm=m_mix_norm, ffn_post_norm=m_ffn_post_norm, ffn_post_w_gate=m_ffn_post_w_gate, ffn_post_w_up=m_ffn_post_w_up, ffn_post_w_down=m_ffn_post_w_down, ab_w_in=m_ab_w_in, ab_conv_w=m_ab_conv_w, ab_w_out=m_ab_w_out, c_w_in=m_c_w_in, c_lower_bounds=m_c_lower_bounds, c_out_norm=m_c_out_norm, c_w_out=m_c_w_out, final_norm=m_final_norm)
    v_in = dict(ffn_pre_norm=v_ffn_pre_norm, ffn_pre_w_gate=v_ffn_pre_w_gate, ffn_pre_w_up=v_ffn_pre_w_up, ffn_pre_w_down=v_ffn_pre_w_down, mix_norm=v_mix_norm, ffn_post_norm=v_ffn_post_norm, ffn_post_w_gate=v_ffn_post_w_gate, ffn_post_w_up=v_ffn_post_w_up, ffn_post_w_down=v_ffn_post_w_down, ab_w_in=v_ab_w_in, ab_conv_w=v_ab_conv_w, ab_w_out=v_ab_w_out, c_w_in=v_c_w_in, c_lower_bounds=v_c_lower_bounds, c_out_norm=v_c_out_norm, c_w_out=v_c_w_out, final_norm=v_final_norm)
    names = list(weights)
    big = [n for n in names if weights[n].size >= 65536]
    tiny = [n for n in names if n not in big]

    delta, new_m, new_v = {}, {}, {}
    for n in big:
        turn = flip if n in transposed else (lambda a: a)
        shape = turn(weights[n]).shape
        two_d = (shape[0] * shape[1], shape[2])
        d, m2, v2, g2 = _adamw("adamw_" + n, turn(weights[n]).reshape(two_d), layers[n],
                               turn(m_in[n]).reshape(two_d), turn(v_in[n]).reshape(two_d))
        delta[n], new_m[n], new_v[n] = turn(d.reshape(shape)), turn(m2.reshape(shape)), turn(v2.reshape(shape))
        grads[n] = turn(g2.reshape(shape))

    def tiny_slab(src):
        return _small_slab([src[n].reshape(-1, src[n].shape[-1]) for n in tiny])

    offs, row = {}, 0
    for n in tiny:
        nrows = weights[n].size // weights[n].shape[-1]
        offs[n] = (row, nrows)
        row += nrows
    d, m2, v2, _ = _adamw("adamw_small", tiny_slab(weights), [tiny_slab(grads)], tiny_slab(m_in), tiny_slab(v_in))
    for n in tiny:
        r0, nr = offs[n]
        shape = weights[n].shape
        for dst, src in ((delta, d), (new_m, m2), (new_v, v2)):
            dst[n] = src[r0:r0 + nr, :shape[-1]].reshape(shape)

    loss = small_sum[R_LOSS, 0]
    return (loss, grad_x.reshape(1, t, D), *[grads[n] for n in names], *[delta[n] for n in names],
            *[new_m[n] for n in names], *[new_v[n] for n in names])
```

```python
import functools
import math

import jax
import jax.numpy as jnp
from jax import lax
from jax.experimental import pallas as pl
from jax.experimental.pallas import tpu as pltpu

F32 = jnp.float32
BF16 = jnp.bfloat16
MESH = pl.DeviceIdType.MESH
ANY = pl.BlockSpec(memory_space=pl.ANY)

D = 1024
FS = 704
NSH = 4
RMS_EPS = 1e-6
MACARON = 0.5
CHUNK = 64
HD = 128
HGRN_HPS = 8
SBQ = 128
SB_PAIRS_FWD = 4
SB_PAIRS_BWD = 4
SB_DEAD = -105.0
CONV_HALO = 8
LANE = 128
ROW_TILE = 1024
MM_TILE = 1024
HALF_TILE = MM_TILE // 2
WGRAD_TILE = 4096
VMEM_LIMIT = 48 * 1024 * 1024
VMEM_LIMIT_BIG = 58 * 1024 * 1024

ADAM_LR, ADAM_B1, ADAM_B2, ADAM_EPS, ADAM_WD, ADAM_STEP = 0.001, 0.9, 0.999, 1e-08, 0.01, 10

NN = ((1,), (0,))
NT = ((1,), (1,))
TN = ((0,), (0,))

SMALL_ROWS = 16
R_PRE, R_MIX, R_POST, R_CLB, R_FIN, R_GAM, R_CONV, R_LOSS = 0, 2, 4, 6, 8, 9, 10, 13

B_ABOUT = 4 * FS
B_CIN = B_ABOUT + 256
B_COUT = B_CIN + 1024
B_ROWS = B_COUT + 256


def _dg(a, b, dims):
    return lax.dot_general(a, b, (dims, ((), ())), preferred_element_type=F32)


def _split(x):
    hi = x.astype(BF16)
    lo = (x - hi.astype(F32)).astype(BF16)
    return hi, lo


def _dot3(a, b, dims):
    ah, al = _split(a)
    bh, bl = _split(b)
    if dims == TN:
        n = b.shape[1]
        both = _dg(ah, jnp.concatenate([bh, bl], axis=1), dims)
        return both[:, :n] + both[:, n:] + _dg(al, bh, dims)
    m = a.shape[0]
    both = _dg(jnp.concatenate([ah, al], axis=0), bh, dims)
    return both[:m] + both[m:] + _dg(ah, bl, dims)


def _sigmoid(x):
    return 1.0 / (1.0 + jnp.exp(-x))


def _params(sem):
    return pltpu.CompilerParams(dimension_semantics=sem, vmem_limit_bytes=VMEM_LIMIT)


def _spec(shape, imap):
    return pl.BlockSpec(shape, imap)


def _accumulate(acc_ref, pairs, dims):
    part = None
    for a_ref, b_ref in pairs:
        d = _dg(a_ref[...], b_ref[...], dims)
        part = d if part is None else part + d
    acc_ref[...] += part


def _mm(name, pairs, *, grid, o_shape, o_dtype, o_spec, dims, kaxis, nk, acc_shape=None, res=None, scale=None,
        into=None, carry=None, norm=None):
    npairs = len(pairs)
    operands, specs = [], []
    for a, a_spec, b, b_spec in pairs:
        operands += [a, b]
        specs += [a_spec, b_spec]
    if res is not None:
        operands.append(res[0])
        specs.append(res[1])
    aliases = {}
    if into is not None:
        aliases = {len(operands): 0}
        operands.append(into)
        specs.append(ANY)
    if norm is not None:
        operands.append(norm[0])
        specs.append(_spec((None, 1, D), lambda *_: (norm[1], 0, 0)))
    n_own = len(operands)
    n_out = 2 if norm is not None else 1
    nc = carry.n if carry is not None else 0
    if nc:
        operands += carry.groups
        specs += carry.in_specs

    def body(*refs):
        o_ref = refs[n_own + nc]
        riders = ((refs[n_own:n_own + nc], refs[n_own + nc + n_out:n_own + 2 * nc + n_out], refs[-2], refs[-1])
                  if nc else None)
        if nc:
            carry.ride(grid, riders, "start")

        def finish(val):
            if scale is not None:
                val = val * scale
            if res is not None:
                val = val + refs[2 * npairs][...]
            o_ref[...] = val.astype(o_dtype)
            if norm is not None:
                r = lax.rsqrt(jnp.mean(val * val, axis=-1, keepdims=True) + RMS_EPS)
                refs[n_own + nc + 1][...] = (val * r * refs[n_own - 1][...]).astype(BF16)

        if nk == 1:
            part = None
            for n in range(npairs):
                d = _dg(refs[2 * n][...], refs[2 * n + 1][...], dims)
                part = d if part is None else part + d
            finish(part)
        else:
            acc_ref = refs[n_own + 2 * nc + n_out]
            k = pl.program_id(kaxis)

            @pl.when(k == 0)
            def _():
                acc_ref[...] = jnp.zeros_like(acc_ref)

            _accumulate(acc_ref, [(refs[2 * n], refs[2 * n + 1]) for n in range(npairs)], dims)

            @pl.when(k == nk - 1)
            def _():
                finish(acc_ref[...])
        if nc:
            carry.ride(grid, riders, "finish")

    sem = tuple("arbitrary" if (nc or (ax == kaxis and nk > 1)) else "parallel" for ax in range(len(grid)))
    outs = pl.pallas_call(
        body, name=name, grid=grid, in_specs=specs,
        out_specs=[o_spec] * n_out + (carry.out_specs if nc else []),
        out_shape=[jax.ShapeDtypeStruct(o_shape, o_dtype)] + ([jax.ShapeDtypeStruct(o_shape, BF16)] if norm is not None else [])
        + (carry.out_shape if nc else []),
        scratch_shapes=([pltpu.VMEM(acc_shape, F32)] if nk > 1 else []) + (carry.scratch if nc else []),
        input_output_aliases=aliases,
        compiler_params=_params(sem),
    )(*operands)
    main = (outs[0], outs[1]) if norm is not None else outs[0]
    return (main, carry.place(outs[n_out:])) if nc else main


def _norm_fwd(name, h, gain_slab, row, rider=None):
    t = h.shape[0]
    tm = min(ROW_TILE, t)
    grid = (t // tm,)
    nr = rider.n if rider is not None else 0

    def body(*refs):
        h_ref, g_ref, o_ref = refs[0], refs[1], refs[2 + nr]
        riders = (refs[2:2 + nr], refs[3 + nr:3 + 2 * nr], refs[-2], refs[-1]) if nr else None
        if nr:
            rider.ride(grid, riders, "start")
        x = h_ref[...]
        r = lax.rsqrt(jnp.mean(x * x, axis=-1, keepdims=True) + RMS_EPS)
        o_ref[...] = (x * r * g_ref[...]).astype(BF16)
        if nr:
            rider.ride(grid, riders, "finish")

    outs = pl.pallas_call(
        body, name=name, grid=grid,
        in_specs=[_spec((tm, D), lambda i: (i, 0)), _spec((None, 1, D), lambda i: (row, 0, 0))]
        + (rider.in_specs if nr else []),
        out_specs=[_spec((tm, D), lambda i: (i, 0))] + (rider.out_specs if nr else []),
        out_shape=[jax.ShapeDtypeStruct((t, D), BF16)] + (rider.out_shape if nr else []),
        scratch_shapes=rider.scratch if nr else [],
        compiler_params=_params(("arbitrary",) if nr else ("parallel",)),
    )(h, gain_slab, *(rider.groups if nr else []))
    return outs[0], (rider.place(outs[1:]) if nr else [])


def _dhn_norm(name, pairs, dims, h, gain_slab, row, dres, rider=None):
    t = h.shape[0]
    tm = min(HALF_TILE, t)
    nt = t // tm
    grid = (nt,)
    npairs = len(pairs)
    nr = rider.n if rider is not None else 0
    operands, specs = [], []
    for a, a_spec, b, b_spec in pairs:
        operands += [a, b]
        specs += [a_spec, b_spec]
    row_spec = _spec((tm, D), lambda i: (i, 0))
    operands += [h, gain_slab, dres]
    specs += [row_spec, _spec((None, 1, D), lambda i: (row, 0, 0)), row_spec]
    n_own = len(operands)

    def body(*refs):
        h_ref, g_ref, dres_ref = refs[2 * npairs:n_own]
        dh_ref, dhb_ref, dg_ref = refs[n_own + nr:n_own + nr + 3]
        gacc_ref = refs[n_own + 2 * nr + 3]
        riders = (refs[n_own:n_own + nr], refs[n_own + nr + 3:n_own + 2 * nr + 3], refs[-2], refs[-1]) if nr else None
        if nr:
            rider.ride(grid, riders, "start")
        i = pl.program_id(0)
        dy = None
        for n in range(npairs):
            d = _dg(refs[2 * n][...], refs[2 * n + 1][...], dims)
            dy = d if dy is None else dy + d
        x = h_ref[...]
        r = lax.rsqrt(jnp.mean(x * x, axis=-1, keepdims=True) + RMS_EPS)
        xh = x * r
        gdy = dy * g_ref[...]
        dh = dres_ref[...] + (gdy - xh * jnp.mean(gdy * xh, axis=-1, keepdims=True)) * r
        dh_ref[...] = dh
        dhb_ref[...] = dh.astype(BF16)
        gpart = jnp.sum((dy * xh).reshape(tm // 8, 8, D), axis=0)

        @pl.when(i == 0)
        def _():
            gacc_ref[...] = gpart

        @pl.when(i > 0)
        def _():
            gacc_ref[...] += gpart

        @pl.when(i == nt - 1)
        def _():
            dg_ref[...] = jnp.sum(gacc_ref[...], axis=0, keepdims=True)

        if nr:
            rider.ride(grid, riders, "finish")

    outs = pl.pallas_call(
        body, name=name, grid=grid, in_specs=specs + (rider.in_specs if nr else []),
        out_specs=[row_spec, row_spec, _spec((1, D), lambda i: (0, 0))] + (rider.out_specs if nr else []),
        out_shape=[jax.ShapeDtypeStruct((t, D), F32), jax.ShapeDtypeStruct((t, D), BF16),
                   jax.ShapeDtypeStruct((1, D), F32)] + (rider.out_shape if nr else []),
        scratch_shapes=[pltpu.VMEM((8, D), F32)] + (rider.scratch if nr else []),
        compiler_params=pltpu.CompilerParams(dimension_semantics=("arbitrary",),
                                             vmem_limit_bytes=VMEM_LIMIT_BIG),
    )(*operands, *(rider.groups if nr else []))
    return outs[0], outs[1], outs[2], (rider.place(outs[3:]) if nr else [])


def _final_loss(h, gain_slab, target):
    t = h.shape[0]
    tm = min(ROW_TILE, t)
    nt = t // tm

    def body(h_ref, g_ref, t_ref, dh_ref, dhb_ref, dg_ref, loss_ref, acc_ref, lacc_ref):
        i = pl.program_id(0)
        x = h_ref[...]
        g = g_ref[...]
        r = lax.rsqrt(jnp.mean(x * x, axis=-1, keepdims=True) + RMS_EPS)
        xh = x * r
        err = xh * g - t_ref[...]
        dy = err * (1.0 / D)
        gdy = dy * g
        dh = (gdy - xh * jnp.mean(gdy * xh, axis=-1, keepdims=True)) * r
        dh_ref[...] = dh
        dhb_ref[...] = dh.astype(BF16)
        part = jnp.sum((dy * xh).reshape(tm // 8, 8, D), axis=0)
        lpart = jnp.sum((err * err).reshape(tm // 8, 8, D), axis=0)

        @pl.when(i == 0)
        def _():
            acc_ref[...] = part
            lacc_ref[...] = lpart

        @pl.when(i > 0)
        def _():
            acc_ref[...] += part
            lacc_ref[...] += lpart

        @pl.when(i == nt - 1)
        def _():
            dg_ref[...] = jnp.sum(acc_ref[...], axis=0, keepdims=True)
            rows = jnp.sum(lacc_ref[...], axis=0, keepdims=True)
            loss_ref[...] = jnp.sum(rows, axis=1, keepdims=True) * (0.5 / D)

    row_spec = _spec((tm, D), lambda i: (i, 0))
    return pl.pallas_call(
        body, name="final_loss", grid=(nt,),
        in_specs=[row_spec, _spec((None, 1, D), lambda i: (R_FIN, 0, 0)), row_spec],
        out_specs=[row_spec, row_spec, _spec((1, D), lambda i: (0, 0)), _spec((1, 1), lambda i: (0, 0))],
        out_shape=[jax.ShapeDtypeStruct((t, D), F32), jax.ShapeDtypeStruct((t, D), BF16),
                   jax.ShapeDtypeStruct((1, D), F32), jax.ShapeDtypeStruct((1, 1), F32)],
        scratch_shapes=[pltpu.VMEM((8, D), F32), pltpu.VMEM((8, D), F32)],
        compiler_params=_params(("arbitrary",)),
    )(h, gain_slab, target)


def _ffn_up(name, hn, wgu, carry=None):
    t = hn.shape[0]
    tm = min(HALF_TILE, t)
    grid = (t // tm,)
    nc = carry.n if carry is not None else 0

    def body(*refs):
        x_ref, w_ref = refs[:2]
        s_up_ref, s_gate_ref, a_ref = refs[2 + nc:5 + nc]
        riders = (refs[2:2 + nc], refs[5 + nc:5 + 2 * nc], refs[-2], refs[-1]) if nc else None
        if nc:
            carry.ride(grid, riders, "start")
        x = x_ref[...]
        for s in range(NSH):
            g = _dg(x, w_ref[s, 0], NT)
            u = _dg(x, w_ref[s, 1], NT)
            sg = _sigmoid(g)
            silu = g * sg
            s_up_ref[s] = (MACARON * silu).astype(BF16)
            s_gate_ref[s] = (MACARON * u * (sg * (1.0 + g * (1.0 - sg)))).astype(BF16)
            a_ref[s] = (silu * u).astype(BF16)
        if nc:
            carry.ride(grid, riders, "finish")

    act = _spec((NSH, tm, FS), lambda i: (0, i, 0))
    shape = jax.ShapeDtypeStruct((NSH, t, FS), BF16)
    outs = pl.pallas_call(
        body, name=name, grid=grid,
        in_specs=[_spec((tm, D), lambda i: (i, 0)), _spec((NSH, 2, FS, D), lambda i: (0, 0, 0, 0))]
        + (carry.in_specs if nc else []),
        out_specs=[act, act, act] + (carry.out_specs if nc else []),
        out_shape=[shape, shape, shape] + (carry.out_shape if nc else []),
        scratch_shapes=carry.scratch if nc else [],
        compiler_params=_params(("arbitrary",) if nc else ("parallel",)),
    )(hn, wgu, *(carry.groups if nc else []))
    return (outs[0], outs[1], outs[2], carry.place(outs[3:]) if nc else [])


def _ffn_down(name, a, wd, h, carry=None, norm=None):
    t = h.shape[0]
    tm = min(HALF_TILE, t)
    row_spec = _spec((tm, D), lambda i: (i, 0))
    return _mm(name, [(a, _spec((None, tm, FS), lambda i, s=s: (s, i, 0)), wd, _spec((None, FS, D), lambda i, s=s: (s, 0, 0)))
                      for s in range(NSH)],
               grid=(t // tm,), o_shape=(t, D), o_dtype=F32, o_spec=row_spec, dims=NN, kaxis=0, nk=1,
               res=(h, row_spec), scale=MACARON, carry=carry, norm=norm)


def _in_proj(name, hn, w_in, two_branches, carry=None):
    t = hn.shape[0]
    tm = min(HALF_TILE, t)
    grid = (t // tm,)
    cols = w_in.shape[2]
    nc = carry.n if carry is not None else 0
    n_out = 2 if two_branches else 1

    def body(*refs):
        x_ref, w_ref = refs[:2]
        o_refs = refs[2 + nc:2 + nc + n_out]
        riders = (refs[2:2 + nc], refs[2 + nc + n_out:2 + 2 * nc + n_out], refs[-2], refs[-1]) if nc else None
        if nc:
            carry.ride(grid, riders, "start")
        x = x_ref[...]
        if two_branches:
            for s in range(2):
                o_refs[0][:, s * cols:(s + 1) * cols] = _dg(x, w_ref[s], NN)
                o_refs[1][:, s * cols:(s + 1) * cols] = _dg(x, w_ref[s + 2], NN).astype(BF16)
        else:
            for s in range(NSH):
                o_refs[0][s] = _dg(x, w_ref[s], NN)
        if nc:
            carry.ride(grid, riders, "finish")

    if two_branches:
        out_specs = [_spec((tm, 2 * cols), lambda i: (i, 0))] * 2
        out_shape = [jax.ShapeDtypeStruct((t, 2 * cols), F32), jax.ShapeDtypeStruct((t, 2 * cols), BF16)]
    else:
        out_specs = [_spec((NSH, tm, cols), lambda i: (0, i, 0))]
        out_shape = [jax.ShapeDtypeStruct((NSH, t, cols), F32)]
    outs = pl.pallas_call(
        body, name=name, grid=grid,
        in_specs=[_spec((tm, D), lambda i: (i, 0)), _spec((NSH, D, cols), lambda i: (0, 0, 0))]
        + (carry.in_specs if nc else []),
        out_specs=out_specs + (carry.out_specs if nc else []),
        out_shape=out_shape + (carry.out_shape if nc else []),
        scratch_shapes=carry.scratch if nc else [],
        compiler_params=_params(("arbitrary",) if nc else ("parallel",)),
    )(hn, w_in, *(carry.groups if nc else []))
    main = (outs[0], outs[1]) if two_branches else outs[0]
    return (main, carry.place(outs[n_out:])) if nc else main


def _ffn_bwd_up(name, dhb, wd, s_up, s_gate, rider=None):
    t = dhb.shape[0]
    tm = min(2 * MM_TILE, t)
    grid = (t // tm, NSH)
    nr = rider.n if rider is not None else 0

    def body(*refs):
        dh_ref, wd_ref, s_up_ref, s_gate_ref = refs[:4]
        dg_ref, du_ref = refs[4 + nr:6 + nr]
        riders = (refs[4:4 + nr], refs[6 + nr:6 + 2 * nr], refs[-2], refs[-1]) if nr else None
        if nr:
            rider.ride(grid, riders, "start")
        da = _dg(dh_ref[...], wd_ref[...], NT).astype(BF16)
        du_ref[...] = da * s_up_ref[...]
        dg_ref[...] = da * s_gate_ref[...]
        if nr:
            rider.ride(grid, riders, "finish")

    act = _spec((None, tm, FS), lambda i, s: (s, i, 0))
    shape = jax.ShapeDtypeStruct((NSH, t, FS), BF16)
    outs = pl.pallas_call(
        body, name=name, grid=grid,
        in_specs=[_spec((tm, D), lambda i, s: (i, 0)), _spec((None, FS, D), lambda i, s: (s, 0, 0)), act, act]
        + (rider.in_specs if nr else []),
        out_specs=[act, act] + (rider.out_specs if nr else []),
        out_shape=[shape, shape] + (rider.out_shape if nr else []),
        scratch_shapes=rider.scratch if nr else [],
        compiler_params=pltpu.CompilerParams(
            dimension_semantics=("arbitrary", "arbitrary") if nr else ("parallel", "parallel"),
            vmem_limit_bytes=VMEM_LIMIT_BIG),
    )(dhb, wd, s_up, s_gate, *(rider.groups if nr else []))
    return outs[0], outs[1], (rider.place(outs[2:]) if nr else [])


def _wgrad(name, a, a_spec, b, b_spec, out_rows, out_cols, t, tt, group, slot, scale=None, carry=None):
    first = isinstance(group, int)
    rows = group if first else group.shape[1]
    return _mm(name, [(a, a_spec, b, b_spec)], grid=(NSH, t // tt),
               o_shape=(NSH, rows, out_cols), o_dtype=BF16,
               o_spec=_spec((None, out_rows, out_cols), lambda s, k: (s, slot, 0)),
               dims=TN, kaxis=1, nk=t // tt, acc_shape=(out_rows, out_cols), scale=scale,
               into=None if first else group, carry=carry)


def _wgrad_gate_up(name, dg, du, hn, group, gate_idx):
    t = hn.shape[0]
    first = isinstance(group, int)
    rows = group if first else group.shape[1]

    def body(dg_ref, du_ref, hn_ref, *refs):
        o_ref = refs[-1]
        x = hn_ref[...]
        o_ref[...] = jnp.concatenate([_dg(dg_ref[...], x, TN), _dg(du_ref[...], x, TN)], axis=0).astype(BF16)

    hid = _spec((None, t, FS), lambda s: (s, 0, 0))
    return pl.pallas_call(
        body, name=name, grid=(NSH,),
        in_specs=[hid, hid, _spec((t, D), lambda s: (0, 0))] + ([] if first else [ANY]),
        out_specs=_spec((None, 2 * FS, D), lambda s: (s, gate_idx // 2, 0)),
        out_shape=jax.ShapeDtypeStruct((NSH, rows, D), BF16),
        input_output_aliases={} if first else {3: 0},
        compiler_params=pltpu.CompilerParams(dimension_semantics=("parallel",), vmem_limit_bytes=VMEM_LIMIT_BIG),
    )(dg, du, hn, *([] if first else [group]))


def _ffn_backward(tag, dh, dhb, h_in, hn, s_up, s_gate, a, wgu, wd, gate_idx, up_idx, down_idx, small, norm_row,
                  grad_a, grad_b, rider_up=None, rider_dhn=None):
    t = dh.shape[0]
    tt = min(WGRAD_TILE, t)
    tok = _spec((tt, D), lambda s, k: (k, 0))
    hid = _spec((None, tt, FS), lambda s, k: (s, k, 0))
    grad_b = _wgrad(tag + "_dwd", a, hid, dhb, tok, FS, D, t, tt, grad_b, down_idx, scale=MACARON)
    rider = rider_up(grad_a, grad_b) if rider_up is not None else None
    dg, du, landed_up = _ffn_bwd_up(tag + "_bwd_up", dhb, wd, s_up, s_gate, rider)
    assert up_idx == gate_idx + 1 and gate_idx % 2 == 0
    grad_a = _wgrad_gate_up(tag + "_dwgu", dg, du, hn, grad_a, gate_idx)
    rider = rider_dhn(grad_a, grad_b, landed_up) if rider_dhn is not None else None
    th = min(HALF_TILE, t)
    pairs = []
    for s in range(NSH):
        act = _spec((None, th, FS), lambda i, s=s: (s, i, 0))
        pairs += [(dg, act, wgu, _spec((None, None, FS, D), lambda i, s=s: (s, 0, 0, 0))),
                  (du, act, wgu, _spec((None, None, FS, D), lambda i, s=s: (s, 1, 0, 0)))]
    dh_in, dhb_in, d_gain, landed_dhn = _dhn_norm(tag + "_dhn", pairs, NN, h_in, small, norm_row, dh, rider)
    return dh_in, dhb_in, grad_a, grad_b, d_gain, landed_up, landed_dhn


def _conv_fwd(proj_a, conv_w):
    t = proj_a.shape[0]
    tm = min(ROW_TILE, t)
    hb = tm // CONV_HALO

    def body(ab_ref, ac_ref, ax_ref, acp_ref, axp_ref, w_ref, y_ref):
        i = pl.program_id(1)
        u = ac_ref[...] * ax_ref[...]
        up = jnp.where(i > 0, acp_ref[...] * axp_ref[...], 0.0)
        ext = jnp.concatenate([up, u], axis=0)
        u1 = pltpu.roll(ext, 1, 0)[CONV_HALO:]
        u2 = pltpu.roll(ext, 2, 0)[CONV_HALO:]
        w = w_ref[...]
        conv = w[0:1] * u2 + w[1:2] * u1 + w[2:3] * u
        y_ref[...] = (ab_ref[...] * conv).astype(BF16)

    def cur(off):
        return _spec((tm, LANE), lambda j, i: (i, off + j))

    def prev(off):
        return _spec((CONV_HALO, LANE), lambda j, i: (jnp.maximum(i * hb - 1, 0), off + j))

    return pl.pallas_call(
        body, name="conv_fwd", grid=(4, t // tm),
        in_specs=[cur(0), cur(4), cur(8), prev(4), prev(8),
                  _spec((None, None, 8, LANE), lambda j, i: (j, 0, 0, 0))],
        out_specs=_spec((tm, LANE), lambda j, i: (i, j)),
        out_shape=jax.ShapeDtypeStruct((t, 512), BF16),
        compiler_params=_params(("parallel", "parallel")),
    )(proj_a, proj_a, proj_a, proj_a, proj_a, conv_w)


def _conv_bwd(proj_a, conv_w, dy, rider=None):
    t = proj_a.shape[0]
    tm = min(ROW_TILE, t)
    hb = tm // CONV_HALO
    nt = t // tm
    grid = (4, nt)
    nr = rider.n if rider is not None else 0

    def body(*refs):
        ab_ref, ac_ref, ax_ref, dy_ref, acp_ref, axp_ref, abn_ref, dyn_ref, w_ref = refs[:9]
        dab_ref, dac_ref, dax_ref, dw_ref = refs[9 + nr:13 + nr]
        acc_ref = refs[13 + 2 * nr]
        riders = (refs[9:9 + nr], refs[13 + nr:13 + 2 * nr], refs[-2], refs[-1]) if nr else None
        if nr:
            rider.ride(grid, riders, "start")
        i = pl.program_id(1)
        ab, ac, ax = ab_ref[...], ac_ref[...], ax_ref[...]
        u = ac * ax
        up = jnp.where(i > 0, acp_ref[...] * axp_ref[...], 0.0)
        ext = jnp.concatenate([up, u], axis=0)
        u1 = pltpu.roll(ext, 1, 0)[CONV_HALO:]
        u2 = pltpu.roll(ext, 2, 0)[CONV_HALO:]
        w = w_ref[...]
        conv = w[0:1] * u2 + w[1:2] * u1 + w[2:3] * u
        dy_v = dy_ref[...]
        dab_ref[...] = (dy_v * conv).astype(BF16)
        dc = dy_v * ab
        dcn = jnp.where(i < nt - 1, dyn_ref[...] * abn_ref[...], 0.0)
        extn = jnp.concatenate([dc, dcn], axis=0)
        n = tm + CONV_HALO
        dc1 = pltpu.roll(extn, n - 1, 0)[:tm]
        dc2 = pltpu.roll(extn, n - 2, 0)[:tm]
        du = w[2:3] * dc + w[1:2] * dc1 + w[0:1] * dc2
        dac_ref[...] = (du * ax).astype(BF16)
        dax_ref[...] = (du * ac).astype(BF16)
        rid = lax.broadcasted_iota(jnp.int32, (8, LANE), 0)
        part = jnp.where(rid == 0, jnp.sum(dc * u2, axis=0, keepdims=True),
                         jnp.where(rid == 1, jnp.sum(dc * u1, axis=0, keepdims=True),
                                   jnp.where(rid == 2, jnp.sum(dc * u, axis=0, keepdims=True), 0.0)))

        @pl.when(i == 0)
        def _():
            acc_ref[...] = part

        @pl.when(i > 0)
        def _():
            acc_ref[...] += part

        @pl.when(i == nt - 1)
        def _():
            dw_ref[...] = acc_ref[...]

        if nr:
            rider.ride(grid, riders, "finish")

    def cur(off):
        return _spec((tm, LANE), lambda j, i: (i, off + j))

    def prev(off):
        return _spec((CONV_HALO, LANE), lambda j, i: (jnp.maximum(i * hb - 1, 0), off + j))

    def nxt(off):
        return _spec((CONV_HALO, LANE), lambda j, i: (jnp.minimum((i + 1) * hb, nt * hb - 1), off + j))

    outs = pl.pallas_call(
        body, name="conv_bwd", grid=grid,
        in_specs=[cur(0), cur(4), cur(8), cur(0), prev(4), prev(8), nxt(0), nxt(0),
                  _spec((None, None, 8, LANE), lambda j, i: (j, 0, 0, 0))] + (rider.in_specs if nr else []),
        out_specs=[_spec((tm, LANE), lambda j, i: (i, j)), _spec((tm, LANE), lambda j, i: (i, j)),
                   _spec((tm, LANE), lambda j, i: (i, j)), _spec((None, 8, LANE), lambda j, i: (j, 0, 0))]
        + (rider.out_specs if nr else []),
        out_shape=[jax.ShapeDtypeStruct((t, 512), BF16), jax.ShapeDtypeStruct((t, 512), BF16),
                   jax.ShapeDtypeStruct((t, 512), BF16), jax.ShapeDtypeStruct((4, 8, LANE), F32)]
        + (rider.out_shape if nr else []),
        scratch_shapes=[pltpu.VMEM((8, LANE), F32)] + (rider.scratch if nr else []),
        compiler_params=_params(("arbitrary", "arbitrary") if nr else ("parallel", "arbitrary")),
    )(proj_a, proj_a, proj_a, dy, proj_a, proj_a, proj_a, dy, conv_w, *(rider.groups if nr else []))
    return outs[0], outs[1], outs[2], outs[3], (rider.place(outs[4:]) if nr else [])


def _log_sigmoid(z):
    return jnp.minimum(z, 0.0) - jnp.log(1.0 + jnp.exp(-jnp.abs(z)))


def _sb_masks():
    row = lax.broadcasted_iota(jnp.int32, (SBQ, SBQ), 0)
    col = lax.broadcasted_iota(jnp.int32, (SBQ, SBQ), 1)
    return row, col


def _ones_where(mask):
    return jnp.where(mask, 1.0, 0.0).astype(BF16)


def _head_mask(head):
    lane = lax.broadcasted_iota(jnp.int32, (1, LANE), 1)
    return lane >= 64 if head else lane < 64


def _sb_fwd(proj_b, carry=None):
    t = proj_b.shape[0]
    nq = t // SBQ
    scale = 1.0 / math.sqrt(64.0)

    npair = SB_PAIRS_FWD
    wide = npair * LANE
    ngrp = 4 // npair
    chains = [(p, head) for p in range(npair) for head in range(2)]

    grid = (ngrp, nq)
    nc = carry.n if carry is not None else 0

    def body(*refs):
        q_ref, k_ref, v_ref = refs[:3]
        y_ref, l_ref, n_ref = refs[3 + nc:6 + nc]
        riders = (refs[3:3 + nc], refs[6 + nc:6 + 2 * nc], refs[-2], refs[-1]) if nc else None
        if nc:
            carry.ride(grid, riders, "start")
        grp = pl.program_id(0)
        qi = pl.program_id(1)
        row, col = _sb_masks()
        m_suffix = _ones_where(row > col)
        rows = len(chains) * SBQ
        strict = (lax.broadcasted_iota(jnp.int32, (rows, SBQ), 1)
                  < (lax.broadcasted_iota(jnp.int32, (rows, SBQ), 0) & (SBQ - 1)))
        q_pair = []
        for p in range(npair):
            q_all = q_ref[:, p * LANE:(p + 1) * LANE]
            q_pair.append(jnp.concatenate([jnp.where(_head_mask(head), q_all, jnp.zeros_like(q_all)) for head in range(2)],
                                          axis=0))

        def block(kb, state, diag):
            run, acc = state
            start = pl.multiple_of(kb * SBQ, SBQ)
            z = jnp.concatenate([_dg(q_pair[p], k_ref[pl.ds(start, SBQ), p * LANE:(p + 1) * LANE], NT)
                                 for p in range(npair)], axis=0) * scale
            lb = _log_sigmoid(z)
            lk = lb - z
            if diag:
                lk = jnp.where(strict, lk, 0.0)
            hi, lo = _split(lk)
            sums = _dg(jnp.concatenate([hi, lo], axis=0), m_suffix, NN)
            w = jnp.exp(lb + (run + sums[:rows] + sums[rows:]))
            if diag:
                w = jnp.where(strict, w, 0.0)
            wb = w.astype(BF16)
            acc = acc + jnp.concatenate(
                [_dg(wb[2 * p * SBQ:2 * (p + 1) * SBQ], v_ref[pl.ds(start, SBQ), p * LANE:(p + 1) * LANE], NN)
                 for p in range(npair)], axis=0)
            run = run + jnp.sum(hi.astype(F32) + lo.astype(F32), axis=1, keepdims=True)
            return run, acc

        state = block(qi, (jnp.zeros((rows, 1), F32), jnp.zeros((rows, LANE), F32)), True)

        def live(c):
            return jnp.logical_and(c[0] < qi, jnp.max(c[1][0]) > SB_DEAD)

        def step(c):
            return c[0] + 1, block(qi - 1 - c[0], c[1], False)

        count, (run, acc) = lax.while_loop(live, step, (jnp.int32(0), state))
        n_ref[grp * nq + qi] = count.astype(F32)
        hm = _head_mask(0)
        for p in range(npair):
            lo_rows, hi_rows = slice(2 * p * SBQ, (2 * p + 1) * SBQ), slice((2 * p + 1) * SBQ, (2 * p + 2) * SBQ)
            y_ref[:, p * LANE:(p + 1) * LANE] = jnp.where(hm, acc[lo_rows], acc[hi_rows]).astype(BF16)
            l_ref[p] = jnp.where(hm, run[lo_rows], run[hi_rows])
        if nc:
            carry.ride(grid, riders, "finish")

    outs = pl.pallas_call(
        body, name="sb_fwd", grid=grid,
        in_specs=[_spec((SBQ, wide), lambda g, i: (i, g)),
                  _spec((t, wide), lambda g, i: (0, ngrp + g)),
                  _spec((t, wide), lambda g, i: (0, 2 * ngrp + g))] + (carry.in_specs if nc else []),
        out_specs=[_spec((SBQ, wide), lambda g, i: (i, g)), _spec((npair, SBQ, LANE), lambda g, i: (g, i, 0)),
                   pl.BlockSpec(memory_space=pltpu.SMEM)] + (carry.out_specs if nc else []),
        out_shape=[jax.ShapeDtypeStruct((t, 512), BF16), jax.ShapeDtypeStruct((4, t, LANE), F32),
                   jax.ShapeDtypeStruct((ngrp * nq,), F32)] + (carry.out_shape if nc else []),
        scratch_shapes=carry.scratch if nc else [],
        compiler_params=_params(("arbitrary", "arbitrary")),
    )(proj_b, proj_b, proj_b, *(carry.groups if nc else []))
    return outs[0], outs[1], outs[2], (carry.place(outs[3:]) if nc else [])


def _sb_bwd(proj_b, dy, ltot, nblk, rider=None):
    t = proj_b.shape[0]
    nq = t // SBQ
    scale = 1.0 / math.sqrt(64.0)

    npair = SB_PAIRS_BWD
    wide = npair * LANE
    ngrp = 4 // npair
    chains = [(p, head) for p in range(npair) for head in range(2)]
    grid = (ngrp, nq)
    nr = rider.n if rider is not None else 0
    per_count = SB_PAIRS_FWD // SB_PAIRS_BWD

    def body(*refs):
        q_ref, k_ref, v_ref, dy_ref, l_ref, n_ref = refs[:6]
        dq_ref, dk_ref, dv_ref = refs[6 + nr:9 + nr]
        dk_acc, dv_acc = refs[9 + 2 * nr:11 + 2 * nr]
        riders = (refs[6:6 + nr], refs[9 + nr:9 + 2 * nr], refs[-2], refs[-1]) if nr else None
        if nr:
            rider.ride(grid, riders, "start")
        grp = pl.program_id(0)
        qi = pl.program_id(1)

        @pl.when(qi == 0)
        def _():
            dk_acc[...] = jnp.zeros_like(dk_acc)
            dv_acc[...] = jnp.zeros_like(dv_acc)

        row, col = _sb_masks()
        m_prefix = _ones_where(row <= col)
        m_before = _ones_where(row < col)
        rows = len(chains) * SBQ
        strict = (lax.broadcasted_iota(jnp.int32, (rows, SBQ), 1)
                  < (lax.broadcasted_iota(jnp.int32, (rows, SBQ), 0) & (SBQ - 1)))
        q_pair, do_pair, ltot = [], [], []
        for p in range(npair):
            pl_ = slice(p * LANE, (p + 1) * LANE)
            q_all = q_ref[:, pl_]
            do_all = dy_ref[:, pl_].astype(BF16)
            q_pair.append(jnp.concatenate([jnp.where(_head_mask(h), q_all, jnp.zeros_like(q_all)) for h in range(2)], axis=0))
            do_pair.append(jnp.concatenate([jnp.where(_head_mask(h), do_all, jnp.zeros_like(do_all)) for h in range(2)], axis=0))
            ltot += [l_ref[p][:, h * 64:h * 64 + 1] for h in range(2)]
        ltot = jnp.concatenate(ltot, axis=0)

        def pair_rows(a, p):
            return a[2 * p * SBQ:2 * (p + 1) * SBQ]

        def block(kb, state, diag):
            seen, dseen, dq = state
            start = pl.multiple_of(kb * SBQ, SBQ)
            kk = [k_ref[pl.ds(start, SBQ), p * LANE:(p + 1) * LANE] for p in range(npair)]
            vv = [v_ref[pl.ds(start, SBQ), p * LANE:(p + 1) * LANE] for p in range(npair)]
            z = jnp.concatenate([_dg(q_pair[p], kk[p], NT) for p in range(npair)], axis=0) * scale
            lb = _log_sigmoid(z)
            lk = lb - z
            if diag:
                lk = jnp.where(strict, lk, 0.0)
            hi, lo = _split(lk)
            sums = _dg(jnp.concatenate([hi, lo], axis=0), m_prefix, NN)
            w = jnp.exp(lb + ((ltot - seen) - (sums[:rows] + sums[rows:])))
            if diag:
                w = jnp.where(strict, w, 0.0)
            wb = w.astype(BF16)
            da = w * jnp.concatenate([_dg(do_pair[p], vv[p], NT) for p in range(npair)], axis=0)
            dah, dal = _split(da)
            dsums = _dg(jnp.concatenate([dah, dal], axis=0), m_before, NN)
            sig = jnp.exp(lb)
            dz = (da * (1.0 - sig) - (dseen + dsums[:rows] + dsums[rows:]) * sig) * scale
            if diag:
                dz = jnp.where(strict, dz, 0.0)
            dzb = dz.astype(BF16)
            for p in range(npair):
                pl_ = slice(p * LANE, (p + 1) * LANE)
                dv_acc[pl.ds(start, SBQ), pl_] += _dg(pair_rows(wb, p), do_pair[p], TN)
                dk_acc[pl.ds(start, SBQ), pl_] += _dg(pair_rows(dzb, p), q_pair[p], TN)
            dq = dq + jnp.concatenate([_dg(pair_rows(dzb, p), kk[p], NN) for p in range(npair)], axis=0)
            seen = seen + jnp.sum(hi.astype(F32) + lo.astype(F32), axis=1, keepdims=True)
            dseen = dseen + jnp.sum(da, axis=1, keepdims=True)
            return seen, dseen, dq

        zero = (jnp.zeros((rows, 1), F32), jnp.zeros((rows, 1), F32), jnp.zeros((rows, LANE), F32))
        first = qi - n_ref[(grp // per_count) * nq + qi].astype(jnp.int32)
        state = lax.fori_loop(first, qi, lambda kb, c: block(kb, c, False), zero)
        _, _, dq = block(qi, state, True)
        for p in range(npair):
            dq_ref[:, p * LANE:(p + 1) * LANE] = jnp.where(
                _head_mask(0), dq[2 * p * SBQ:(2 * p + 1) * SBQ], dq[(2 * p + 1) * SBQ:(2 * p + 2) * SBQ]).astype(BF16)

        @pl.when(qi == nq - 1)
        def _():
            dk_ref[...] = dk_acc[...].astype(BF16)
            dv_ref[...] = dv_acc[...].astype(BF16)

        if nr:
            rider.ride(grid, riders, "finish")

    full = jax.ShapeDtypeStruct((t, 512), BF16)
    outs = pl.pallas_call(
        body, name="sb_bwd", grid=grid,
        in_specs=[_spec((SBQ, wide), lambda g, i: (i, g)),
                  _spec((t, wide), lambda g, i: (0, ngrp + g)),
                  _spec((t, wide), lambda g, i: (0, 2 * ngrp + g)),
                  _spec((SBQ, wide), lambda g, i: (i, ngrp + g)),
                  _spec((npair, SBQ, LANE), lambda g, i: (g, i, 0)),
                  pl.BlockSpec(memory_space=pltpu.SMEM)] + (rider.in_specs if nr else []),
        out_specs=[_spec((SBQ, wide), lambda g, i: (i, g)),
                   _spec((t, wide), lambda g, i: (0, g)), _spec((t, wide), lambda g, i: (0, g))]
        + (rider.out_specs if nr else []),
        out_shape=[full, full, full] + (rider.out_shape if nr else []),
        scratch_shapes=[pltpu.VMEM((t, wide), F32), pltpu.VMEM((t, wide), F32)] + (rider.scratch if nr else []),
        compiler_params=pltpu.CompilerParams(dimension_semantics=("arbitrary", "arbitrary"),
                                             vmem_limit_bytes=VMEM_LIMIT_BIG),
    )(proj_b, proj_b, proj_b, dy, ltot, nblk, *(rider.groups if nr else []))
    return outs[0], outs[1], outs[2], (rider.place(outs[3:]) if nr else [])


def _hgrn_gates(qr, fr, c0, c1):
    mx = jnp.maximum(c0, c1)
    e0, e1 = jnp.exp(c0 - mx), jnp.exp(c1 - mx)
    lb = e1 / (e0 + e1)
    sx = _sigmoid(fr)
    f = lb + (1.0 - lb) * sx
    k = (1.0 - lb) * (1.0 - sx)
    sq = _sigmoid(qr)
    return lb, sx, f, k, sq, qr * sq


def _chunk_sums(mask, x):
    n = x.shape[1]
    hi, lo = _split(x)
    both = _dg(_ones_where(mask), jnp.concatenate([hi, lo], axis=1), NN)
    return both[:, :n] + both[:, n:]


def _chunk_masks():
    row = lax.broadcasted_iota(jnp.int32, (CHUNK, CHUNK), 0)
    col = lax.broadcasted_iota(jnp.int32, (CHUNK, CHUNK), 1)
    return col <= row, col >= row


def _hgrn_fwd(proj_c, small, carry=None):
    t = proj_c.shape[1]
    nc = t // CHUNK
    nh = D // HD

    hps = HGRN_HPS
    wide = hps * HD
    grid = (nh // hps, nc)
    nr = carry.n if carry is not None else 0

    def body(*refs):
        p_ref, c0_ref, c1_ref, gam_ref = refs[:4]
        o_ref, y_ref, sst_ref = refs[4 + nr:7 + nr]
        st_ref = refs[7 + 2 * nr]
        riders = (refs[4:4 + nr], refs[7 + nr:7 + 2 * nr], refs[-2], refs[-1]) if nr else None
        if nr:
            carry.ride(grid, riders, "start")
        c = pl.program_id(1)

        @pl.when(c == 0)
        def _():
            st_ref[...] = jnp.zeros_like(st_ref)

        _, _, f_all, k_all, _, q_all = _hgrn_gates(p_ref[0], p_ref[1], c0_ref[...], c1_ref[...])
        tril, _ = _chunk_masks()
        b_all = _chunk_sums(tril, jnp.log(f_all))
        for j in range(hps):
            ln = slice(j * HD, (j + 1) * HD)
            st0 = st_ref[j]
            sst_ref[j] = st0
            v, g = p_ref[2, :, ln], p_ref[3, :, ln]
            q, k, b = q_all[:, ln], k_all[:, ln], b_all[:, ln]
            bm = b[CHUNK // 2 - 1:CHUNK // 2]
            bl = b[CHUNK - 1:CHUNK]
            qd = q * jnp.exp(b)
            qt = q * jnp.exp(b - bm)
            kt = k * jnp.exp(bm - b)
            kl = k * jnp.exp(bl - b)
            vb = v.astype(BF16)
            att = jnp.where(tril, _dot3(qt, kt, NT), 0.0)
            o = _dg(qd.astype(BF16), st0.astype(BF16), NT) + _dg(att.astype(BF16), vb, NN)
            st_ref[j] = st0 * jnp.exp(bl) + _dg(vb, kl.astype(BF16), TN)
            o_ref[:, ln] = o
            r = lax.rsqrt(jnp.mean(o * o, axis=-1, keepdims=True) + RMS_EPS)
            y_ref[:, ln] = (o * r * gam_ref[...] * (g * _sigmoid(g))).astype(BF16)
        if nr:
            carry.ride(grid, riders, "finish")

    outs = pl.pallas_call(
        body, name="hgrn_fwd", grid=grid,
        in_specs=[_spec((4, CHUNK, wide), lambda h, c: (0, c, h)),
                  _spec((None, 1, wide), lambda h, c: (R_CLB, 0, h)),
                  _spec((None, 1, wide), lambda h, c: (R_CLB + 1, 0, h)),
                  _spec((None, 1, HD), lambda h, c: (R_GAM, 0, 0))] + (carry.in_specs if nr else []),
        out_specs=[_spec((CHUNK, wide), lambda h, c: (c, h)), _spec((CHUNK, wide), lambda h, c: (c, h)),
                   _spec((None, hps, HD, HD), lambda h, c: (c, h, 0, 0))] + (carry.out_specs if nr else []),
        out_shape=[jax.ShapeDtypeStruct((t, D), F32), jax.ShapeDtypeStruct((t, D), BF16),
                   jax.ShapeDtypeStruct((nc, nh, HD, HD), F32)] + (carry.out_shape if nr else []),
        scratch_shapes=[pltpu.VMEM((hps, HD, HD), F32)] + (carry.scratch if nr else []),
        compiler_params=_params(("arbitrary", "arbitrary")),
    )(proj_c, small, small, small, *(carry.groups if nr else []))
    return outs[0], outs[1], outs[2], (carry.place(outs[3:]) if nr else [])


def _hgrn_bwd(proj_c, small, o, sst, dyc, rider=None):
    t = proj_c.shape[1]
    nc = t // CHUNK
    nh = D // HD

    hps = HGRN_HPS
    wide = hps * HD
    ng = nh // hps
    grid = (ng, nc)
    nr = rider.n if rider is not None else 0

    def body(*refs):
        p_ref, c0_ref, c1_ref, gam_ref, o_ref, sst_ref, dy_ref = refs[:7]
        dp_ref, dclb_ref, dgam_ref = refs[7 + nr:10 + nr]
        dst_ref, dlb_acc, dgam_acc = refs[10 + 2 * nr:13 + 2 * nr]
        riders = (refs[7:7 + nr], refs[10 + nr:10 + 2 * nr], refs[-2], refs[-1]) if nr else None
        if nr:
            rider.ride(grid, riders, "start")
        group = pl.program_id(0)
        step = pl.program_id(1)

        @pl.when(step == 0)
        def _():
            dst_ref[...] = jnp.zeros_like(dst_ref)
            dlb_acc[...] = jnp.zeros_like(dlb_acc)

        @pl.when((step == 0) & (group == 0))
        def _():
            dgam_acc[...] = jnp.zeros_like(dgam_acc)

        gam = gam_ref[...]
        qr_all = p_ref[0]
        lb_all, sx_all, f_all, k_all, sq_all, q_all = _hgrn_gates(qr_all, p_ref[1], c0_ref[...], c1_ref[...])
        tril, triu = _chunk_masks()
        b_all = _chunk_sums(tril, jnp.log(f_all))
        dq_parts, dk_parts, db_parts, dgam_parts = [], [], [], []
        for j in range(hps):
            ln = slice(j * HD, (j + 1) * HD)
            st0 = sst_ref[j]
            dst1 = dst_ref[j]
            v, g = p_ref[2, :, ln], p_ref[3, :, ln]
            q, k, b = q_all[:, ln], k_all[:, ln], b_all[:, ln]
            bm = b[CHUNK // 2 - 1:CHUNK // 2]
            bl = b[CHUNK - 1:CHUNK]
            eb = jnp.exp(b)
            e_qt = jnp.exp(b - bm)
            e_kt = jnp.exp(bm - b)
            e_kl = jnp.exp(bl - b)
            e_bl = jnp.exp(bl)
            qd, qt, kt, kl = q * eb, q * e_qt, k * e_kt, k * e_kl
            ov = o_ref[:, ln]
            r = lax.rsqrt(jnp.mean(ov * ov, axis=-1, keepdims=True) + RMS_EPS)
            oh = ov * r
            sg = _sigmoid(g)
            dy = dy_ref[:, ln]
            dp_ref[3, :, ln] = (dy * oh * gam * (sg * (1.0 + g * (1.0 - sg)))).astype(BF16)
            dyv = dy * (g * sg)
            gdy = dyv * gam
            do = (gdy - oh * jnp.mean(gdy * oh, axis=-1, keepdims=True)) * r
            dob, vb = do.astype(BF16), v.astype(BF16)
            st0b, dst1b = st0.astype(BF16), dst1.astype(BF16)
            st1 = st0 * e_bl + _dg(vb, kl.astype(BF16), TN)
            att = jnp.where(tril, _dot3(qt, kt, NT), 0.0)
            datt = jnp.where(tril, _dg(dob, vb, NT), 0.0)
            dv = _dg(att.astype(BF16), dob, TN) + _dg(kl.astype(BF16), dst1b, NT)
            dq = _dot3(datt, kt, NN) * e_qt + _dg(dob, st0b, NN) * eb
            dk = _dot3(datt, qt, TN) * e_kt + _dg(vb, dst1b, NN) * e_kl
            db = q * dq - k * dk
            last = lax.broadcasted_iota(jnp.int32, (CHUNK, 1), 0) == CHUNK - 1
            db = db + jnp.where(last, jnp.sum(dst1 * st1, axis=0, keepdims=True), 0.0)
            dst_ref[j] = dst1 * e_bl + _dg(dob, qd.astype(BF16), TN)
            dp_ref[2, :, ln] = dv.astype(BF16)
            dq_parts.append(dq)
            dk_parts.append(dk)
            db_parts.append(db)
            dgam_parts.append(jnp.sum(dyv * oh, axis=0, keepdims=True))

        dq_all, dk_all = jnp.concatenate(dq_parts, axis=1), jnp.concatenate(dk_parts, axis=1)
        dlf = _chunk_sums(triu, jnp.concatenate(db_parts, axis=1))
        dp_ref[0] = (dq_all * (sq_all * (1.0 + qr_all * (1.0 - sq_all)))).astype(BF16)
        tmp = dlf / f_all - dk_all
        dp_ref[1] = (tmp * (1.0 - lb_all) * sx_all * (1.0 - sx_all)).astype(BF16)
        dlb_acc[...] += jnp.sum((1.0 - sx_all) * tmp, axis=0, keepdims=True)
        dgam_acc[...] += functools.reduce(lambda a, b: a + b, dgam_parts)

        @pl.when(step == nc - 1)
        def _():
            d1 = dlb_acc[...] * lb_all * (1.0 - lb_all)
            dclb_ref[...] = jnp.where(lax.broadcasted_iota(jnp.int32, (2, wide), 0) == 0, -d1, d1)

        @pl.when((step == nc - 1) & (group == ng - 1))
        def _():
            dgam_ref[...] = dgam_acc[...]

        if nr:
            rider.ride(grid, riders, "finish")

    rev = lambda h, s: (nc - 1 - s, h)
    outs = pl.pallas_call(
        body, name="hgrn_bwd", grid=grid,
        in_specs=[_spec((4, CHUNK, wide), lambda h, s: (0, nc - 1 - s, h)),
                  _spec((None, 1, wide), lambda h, s: (R_CLB, 0, h)),
                  _spec((None, 1, wide), lambda h, s: (R_CLB + 1, 0, h)),
                  _spec((None, 1, HD), lambda h, s: (R_GAM, 0, 0)),
                  _spec((CHUNK, wide), rev),
                  _spec((None, hps, HD, HD), lambda h, s: (nc - 1 - s, h, 0, 0)),
                  _spec((CHUNK, wide), rev)] + (rider.in_specs if nr else []),
        out_specs=[_spec((4, CHUNK, wide), lambda h, s: (0, nc - 1 - s, h)),
                   _spec((2, wide), lambda h, s: (0, h)),
                   _spec((1, HD), lambda h, s: (0, 0))] + (rider.out_specs if nr else []),
        out_shape=[jax.ShapeDtypeStruct((4, t, D), BF16), jax.ShapeDtypeStruct((2, D), F32),
                   jax.ShapeDtypeStruct((1, HD), F32)] + (rider.out_shape if nr else []),
        scratch_shapes=[pltpu.VMEM((hps, HD, HD), F32), pltpu.VMEM((1, wide), F32), pltpu.VMEM((1, HD), F32)]
        + (rider.scratch if nr else []),
        compiler_params=_params(("arbitrary", "arbitrary")),
    )(proj_c, small, small, small, o, sst, dyc, *(rider.groups if nr else []))
    return outs[0], outs[1], outs[2], (rider.place(outs[3:]) if nr else [])


def _adamw(name, w, grads, m, v):
    nl = len(grads)
    rows, cols = grads[0].shape
    br = rows
    for cand in (512, 352, 256):
        if rows % cand == 0:
            br = cand
            break
    nb = rows // br
    c1 = 1.0 - ADAM_B1 ** ADAM_STEP
    c2 = 1.0 - ADAM_B2 ** ADAM_STEP

    def body(w_ref, m_ref, v_ref, *refs):
        g_refs, (d_ref, mo_ref, vo_ref, go_ref) = refs[:nl], refs[nl:]
        layer = pl.program_id(0)
        gv = g_refs[0][...]
        for k in range(1, nl):
            gv = jnp.where(layer == k, g_refs[k][...], gv)
        mn = ADAM_B1 * m_ref[...] + (1.0 - ADAM_B1) * gv
        vn = ADAM_B2 * v_ref[...] + (1.0 - ADAM_B2) * (gv * gv)
        mo_ref[...] = mn
        vo_ref[...] = vn
        go_ref[...] = gv
        d_ref[...] = -ADAM_LR * ((mn / c1) / (jnp.sqrt(vn / c2) + ADAM_EPS) + ADAM_WD * w_ref[...])

    blk = _spec((br, cols), lambda l, i: (l * nb + i, 0))
    g_specs = [_spec((br, cols), lambda l, i, k=k: (jnp.where(l == k, i, 0), 0)) for k in range(nl)]
    shape = jax.ShapeDtypeStruct((nl * rows, cols), F32)
    return pl.pallas_call(
        body, name=name, grid=(nl, nb), in_specs=[blk] * 3 + g_specs, out_specs=[blk] * 4, out_shape=[shape] * 4,
        compiler_params=_params(("arbitrary", "arbitrary")),
    )(w, m, v, *grads)


def _place():
    x, y, c = lax.axis_index("x"), lax.axis_index("y"), lax.axis_index("c")
    chips = [(1 - x, y), (x, 1 - y), (1 - x, 1 - y)]
    return x, y, c, chips


class _Rider:
    n = 0
    relay = None

    def ride(self, grid, refs, when):
        if not self.n:
            return
        ids = [pl.program_id(a) for a in range(len(grid))]
        first = functools.reduce(jnp.logical_and, [i == 0 for i in ids])
        last = functools.reduce(jnp.logical_and, [i == g - 1 for i, g in zip(ids, grid)])
        phases = [(first, self.start), (last, self.relay)] if when == "start" else [(last, self.finish)]
        for cond, phase in phases:
            if phase is not None:
                pl.when(cond)(functools.partial(phase, *refs))


class _Gather(_Rider):
    PER_GROUP = 7

    def __init__(self, groups):
        self.groups = list(groups)
        self.n = len(self.groups)
        self.in_specs = [ANY] * self.n
        self.out_specs = [ANY] * self.n
        self.out_shape = [jax.ShapeDtypeStruct((NSH,) + g.shape, g.dtype) for g in self.groups]
        sems = pltpu.SemaphoreType.DMA((self.PER_GROUP * self.n,))
        self.scratch = [sems, sems] if self.n else []

    def _copies(self, ins, outs, send, recv):
        x, y, c, chips = _place()
        sibling = (x, y, 1 - c)

        def half(gi, chip, hc):
            return outs[gi].at[2 * chip[0] + chip[1], hc]

        def copy(gi, k, src, dst, to):
            sem = self.PER_GROUP * gi + k
            return pltpu.make_async_remote_copy(src_ref=src, dst_ref=dst, send_sem=send.at[sem], recv_sem=recv.at[sem],
                                                device_id=to, device_id_type=MESH)

        pairs = [(gi, j, chip) for gi in range(self.n) for j, chip in enumerate(chips)]
        first = [copy(gi, j, ins[gi].at[c], half(gi, (x, y), c), (*chip, c)) for gi, j, chip in pairs]
        first += [copy(gi, 6, ins[gi], outs[gi].at[2 * x + y], sibling) for gi in range(self.n)]
        landed = [copy(gi, j, half(gi, chip, c), half(gi, chip, c), sibling) for gi, j, chip in pairs]
        relay = [copy(gi, 3 + j, half(gi, chip, c), half(gi, chip, c), sibling) for gi, j, chip in pairs]
        relayed = [copy(gi, 3 + j, half(gi, chip, 1 - c), half(gi, chip, 1 - c), sibling) for gi, j, chip in pairs]
        relayed += [copy(gi, 6, ins[gi], outs[gi].at[2 * x + y], sibling) for gi in range(self.n)]
        return first, landed, relay, relayed

    def start(self, ins, outs, send, recv):
        for cp in self._copies(ins, outs, send, recv)[0]:
            cp.start()

    def relay(self, ins, outs, send, recv):
        _, landed, relay, _ = self._copies(ins, outs, send, recv)
        for arrived, onward in zip(landed, relay):
            arrived.wait_recv()
            onward.start()

    def finish(self, ins, outs, send, recv):
        first, _, relay, relayed = self._copies(ins, outs, send, recv)
        for cp in relayed:
            cp.wait_recv()
        for cp in first + relay:
            cp.wait_send()

    def place(self, outs):
        return list(outs)


def _alone(name, rider):
    n = rider.n

    def body(*refs):
        parts = (refs[:n], refs[n:2 * n], refs[2 * n], refs[2 * n + 1])
        rider.start(*parts)
        if rider.relay is not None:
            rider.relay(*parts)
        rider.finish(*parts)

    outs = pl.pallas_call(
        body, name=name, in_specs=rider.in_specs, out_specs=rider.out_specs, out_shape=rider.out_shape,
        scratch_shapes=rider.scratch, compiler_params=pltpu.CompilerParams(has_side_effects=True),
    )(*rider.groups)
    return rider.place(outs)


class _Swap(_Rider):
    def __init__(self, slots):
        self.slots = list(slots)
        self.groups = [buf for buf, _, _ in self.slots]
        self.n = len(self.slots)
        self.in_specs = [ANY] * self.n
        self.out_specs = [ANY] * self.n
        self.out_shape = [jax.ShapeDtypeStruct((NSH, rows // 2, buf.shape[2]), buf.dtype) for buf, _, rows in self.slots]
        self.scratch = [pltpu.SemaphoreType.DMA((self.n,)), pltpu.SemaphoreType.DMA((self.n,))]

    def _copies(self, ins, outs, send, recv):
        x, y, c, _ = _place()
        cps = []
        for i, (_, row0, rows) in enumerate(self.slots):
            half = rows // 2
            src = ins[i].at[:, pl.ds(pl.multiple_of(row0 + (1 - c) * half, 16), half)]
            cps.append(pltpu.make_async_remote_copy(src_ref=src, dst_ref=outs[i], send_sem=send.at[i],
                                                    recv_sem=recv.at[i], device_id=(x, y, 1 - c), device_id_type=MESH))
        return cps

    def start(self, ins, outs, send, recv):
        for cp in self._copies(ins, outs, send, recv):
            cp.start()

    def finish(self, ins, outs, send, recv):
        for cp in self._copies(ins, outs, send, recv):
            cp.wait()

    def place(self, outs):
        return list(outs)


class _Share(_Rider):
    def __init__(self, arrays):
        self.groups = list(arrays)
        self.n = len(self.groups)
        self.in_specs = [ANY] * self.n
        self.out_specs = [ANY] * self.n
        self.out_shape = [jax.ShapeDtypeStruct((2,) + g.shape, g.dtype) for g in self.groups]
        self.scratch = [pltpu.SemaphoreType.DMA((self.n,)), pltpu.SemaphoreType.DMA((self.n,))]

    def _copies(self, ins, outs, send, recv, half):
        x, y, c, _ = _place()
        return [pltpu.make_async_remote_copy(src_ref=ins[gi], dst_ref=outs[gi].at[c if half == "mine" else 1 - c],
                                             send_sem=send.at[gi], recv_sem=recv.at[gi], device_id=(x, y, 1 - c),
                                             device_id_type=MESH) for gi in range(self.n)]

    def start(self, ins, outs, send, recv):
        for cp in self._copies(ins, outs, send, recv, "mine"):
            cp.start()

    def finish(self, ins, outs, send, recv):
        for cp in self._copies(ins, outs, send, recv, "theirs"):
            cp.wait_recv()
        for cp in self._copies(ins, outs, send, recv, "mine"):
            cp.wait_send()

    def place(self, outs):
        c = lax.axis_index("c")
        return [lax.dynamic_update_index_in_dim(o, g, c, 0) for o, g in zip(outs, self.groups)]


class _Send(_Rider):
    def __init__(self, arrays):
        self.groups = list(arrays)
        self.n = len(self.groups)
        self.in_specs = [ANY] * self.n
        self.out_specs = [ANY] * self.n
        self.out_shape = [jax.ShapeDtypeStruct((3,) + g.shape[1:], g.dtype) for g in self.groups]
        self.scratch = [pltpu.SemaphoreType.DMA((3 * self.n,)), pltpu.SemaphoreType.DMA((3 * self.n,))]

    def _copies(self, ins, outs, send, recv):
        x, y, c, chips = _place()
        return [pltpu.make_async_remote_copy(src_ref=ins[gi].at[2 * chip[0] + chip[1]], dst_ref=outs[gi].at[j],
                                             send_sem=send.at[3 * gi + j], recv_sem=recv.at[3 * gi + j],
                                             device_id=(*chip, c), device_id_type=MESH)
                for gi in range(self.n) for j, chip in enumerate(chips)]

    def start(self, ins, outs, send, recv):
        for cp in self._copies(ins, outs, send, recv):
            cp.start()

    def finish(self, ins, outs, send, recv):
        for cp in self._copies(ins, outs, send, recv):
            cp.wait()

    def place(self, outs):
        return list(outs)


def _pair_sum(name, slots, got, c_idx):
    n = len(slots)
    in_specs, out_specs, out_shape, operands = [], [], [], []
    for (buf, row0, rows), g in zip(slots, got):
        hb, cols = rows // 4, buf.shape[2]
        in_specs += [pl.BlockSpec((None, hb, cols), lambda q, i, cr, r=row0 // hb: (q, r + 2 * cr[0] + i, 0)),
                     pl.BlockSpec((None, hb, cols), lambda q, i, cr: (q, i, 0))]
        out_specs.append(pl.BlockSpec((None, hb, cols), lambda q, i, cr: (q, i, 0)))
        out_shape.append(jax.ShapeDtypeStruct(g.shape, BF16))
        operands += [buf, g]

    def body(c_ref, *refs):
        for i in range(n):
            refs[2 * n + i][...] = (refs[2 * i][...].astype(F32) + refs[2 * i + 1][...].astype(F32)).astype(BF16)

    return pl.pallas_call(
        body, name=name,
        grid_spec=pltpu.PrefetchScalarGridSpec(num_scalar_prefetch=1, grid=(NSH, 2), in_specs=in_specs, out_specs=out_specs),
        out_shape=out_shape, compiler_params=_params(("parallel", "parallel")),
    )(c_idx, *operands)


def _owner_sum(name, pairs, got, p_idx):
    n = len(pairs)
    in_specs, out_specs, out_shape, operands = [], [], [], []
    for own, g in zip(pairs, got):
        _, rows, cols = own.shape
        hb = rows // 2
        in_specs += [pl.BlockSpec((None, hb, cols), lambda i, pr: (pr[0], i, 0)),
                     pl.BlockSpec((3, hb, cols), lambda i, pr: (0, i, 0))]
        out_specs.append(pl.BlockSpec((hb, cols), lambda i, pr: (i, 0)))
        out_shape.append(jax.ShapeDtypeStruct((rows, cols), F32))
        operands += [own, g]

    def body(p_ref, *refs):
        for i in range(n):
            a_ref, b_ref = refs[2 * i], refs[2 * i + 1]
            refs[2 * n + i][...] = ((a_ref[...].astype(F32) + b_ref[0].astype(F32)) + b_ref[1].astype(F32)) + b_ref[2].astype(F32)

    return pl.pallas_call(
        body, name=name,
        grid_spec=pltpu.PrefetchScalarGridSpec(num_scalar_prefetch=1, grid=(2,), in_specs=in_specs, out_specs=out_specs),
        out_shape=out_shape, compiler_params=_params(("parallel",)),
    )(p_idx, *operands)


def _sum_small(slab):
    def body(in_ref, out_ref, all_ref, send, recv):
        x, y, c, _ = _place()
        me = 4 * x + 2 * y + c
        all_ref[me] = in_ref[...]
        cps = []
        for k in range(1, 8):
            peer = (x ^ (k >> 2), y ^ ((k >> 1) & 1), c ^ (k & 1))
            cps.append(pltpu.make_async_remote_copy(src_ref=in_ref, dst_ref=all_ref.at[me], send_sem=send.at[k - 1],
                                                    recv_sem=recv.at[k - 1], device_id=peer, device_id_type=MESH))
        for cp in cps:
            cp.start()
        for cp in cps:
            cp.wait()
        total = all_ref[0]
        for d in range(1, 8):
            total = total + all_ref[d]
        out_ref[...] = total

    return pl.pallas_call(
        body, name="sum_small",
        in_specs=[pl.BlockSpec(memory_space=pltpu.VMEM)], out_specs=pl.BlockSpec(memory_space=pltpu.VMEM),
        out_shape=jax.ShapeDtypeStruct(slab.shape, F32),
        scratch_shapes=[pltpu.VMEM((8,) + slab.shape, F32), pltpu.SemaphoreType.DMA((7,)), pltpu.SemaphoreType.DMA((7,))],
        compiler_params=pltpu.CompilerParams(has_side_effects=True),
    )(slab)


FFNS = ("pre0", "post0", "pre1", "post1")
W_SHAPES = dict({f + "_gu": (NSH, 2, FS, D) for f in FFNS}, **{f + "_d": (NSH, FS, D) for f in FFNS},
                ab_in=(NSH, D, 768), ab_out=(NSH, 256, D), conv=(NSH, 2, 8, LANE), c_in=(NSH, D, D), c_out=(NSH, 256, D))
CARRIED = dict(pre0_norm=("pre0_gu",), pre0_up=("pre0_d",), pre0_down=("ab_in", "conv"), ab_proj=("ab_out",),
               c_proj=("post1_d",),
               sb_fwd=("post0_gu", "post0_d"), post0_up=("pre1_gu",), post0_down=("pre1_d",),
               pre1_up=("c_in", "c_out"), hgrn_fwd=("post1_gu",))


FFN_SLOTS = dict(pre0=(0, 1, 0), pre1=(2, 3, 1), post0=(4, 5, 2), post1=(6, 7, 3))
GRAD_SLOTS = dict(ab_out=("b", B_ABOUT, 256), c_in=("b", B_CIN, D), c_out=("b", B_COUT, 256), ab_in=("c", 0, D))
for _f, (_g, _u, _d) in FFN_SLOTS.items():
    GRAD_SLOTS.update({_f + "_g": ("a", _g * FS, FS), _f + "_u": ("a", _u * FS, FS), _f + "_d": ("b", _d * FS, FS)})
REDUCE_STAGES = dict(x=("post1_g", "post1_u", "post1_d", "c_out"), w=("c_in", "pre1_d"),
                     y=("pre1_g", "pre1_u", "post0_g", "post0_u", "post0_d", "ab_out"),
                     z1=("ab_in",), z2=("pre0_g", "pre0_u", "pre0_d"))


def _local_step(x, target, weights, small, shards=None, place=None):
    t = x.shape[0]
    th = min(HALF_TILE, t)
    tw = min(WGRAD_TILE, t)
    w = dict(weights)
    reduced = {}

    def stage_slots(stage, buffers):
        return [(buffers[GRAD_SLOTS[n][0]],) + GRAD_SLOTS[n][1:] for n in REDUCE_STAGES[stage]]

    def swap_rider(stage, buffers):
        return _Swap(stage_slots(stage, buffers)) if place is not None else None

    def reduce_start(stage, buffers, swapped=None):
        if place is None:
            return [], None
        slots = stage_slots(stage, buffers)
        if swapped is None:
            swapped = _alone("grad_swap_" + stage, _Swap(slots))
        pairs = _pair_sum("grad_pair_sum_" + stage, slots, swapped, place[0])
        return pairs, _Send(pairs)

    def reduce_end(stage, pairs, landed):
        if place is None:
            return None
        mine = _owner_sum("grad_owner_sum_" + stage, pairs, landed, place[1])
        return _Share(mine)

    def shared(stage, landed):
        if place is not None:
            reduced.update(zip(REDUCE_STAGES[stage], landed))

    def carried(kernel_name):
        names = [n for n in CARRIED[kernel_name] if n not in w]
        return names, (_Gather([shards[n] for n in names]) if names else None)

    def land(names, arrays):
        for n, a in zip(names, arrays):
            w[n] = a.reshape(W_SHAPES[n])

    def ffn_forward(tag, h, hn, next_row):
        names, gather = carried(tag + "_up") if tag + "_up" in CARRIED else ([], None)
        s_up, s_gate, a, got = _ffn_up(tag + "_up", hn, w[tag + "_gu"], gather)
        land(names, got)
        names, gather = carried(tag + "_down") if tag + "_down" in CARRIED else ([], None)
        out = _ffn_down(tag + "_down", a, w[tag + "_d"], h, gather, (small, next_row) if next_row is not None else None)
        if gather is not None:
            out, got = out
            land(names, got)
        out, hn_next = out if next_row is not None else (out, None)
        return out, hn_next, (h, hn, s_up, s_gate, a)

    def out_proj(name, y, w_out, h, next_row):
        row_spec = _spec((th, D), lambda i: (i, 0))
        return _mm(name, [(y, row_spec, w_out.reshape(D, D), _spec((D, D), lambda i: (0, 0)))],
                   grid=(t // th,), o_shape=(t, D), o_dtype=F32, o_spec=row_spec, dims=NN, kaxis=0, nk=1,
                   res=(h, row_spec), norm=(small, next_row))

    def out_proj_bwd(tag, dhb, y, w_out, blk, grad_b, make_rider=None):
        grad_b = _wgrad(tag + "_dwout", y, _spec((tw, 256), lambda s, k: (k, s)), dhb, _spec((tw, D), lambda s, k: (k, 0)),
                        256, D, t, tw, grad_b, blk)
        rider = make_rider(grad_b) if make_rider is not None else None
        row_spec = _spec((th, D), lambda i: (i, 0))
        dy = _mm(tag + "_dy", [(dhb, row_spec, w_out.reshape(D, D), _spec((D, D), lambda i: (0, 0)))],
                 grid=(t // th,), o_shape=(t, D), o_dtype=F32, o_spec=row_spec, dims=NT, kaxis=0, nk=1, carry=rider)
        dy, landed = dy if rider is not None else (dy, None)
        return dy, grad_b, landed

    h0 = x
    names, gather = carried("pre0_norm")
    hn, got = _norm_fwd("pre0_norm", h0, small, R_PRE, gather)
    land(names, got)
    h1, hn_ab, pre0 = ffn_forward("pre0", h0, hn, R_MIX)
    def carrying(kernel_name, make):
        names, gather = carried(kernel_name)
        out = make(gather)
        if gather is not None:
            out, got = out
            land(names, got)
        return out

    proj_a, proj_b = carrying("ab_proj", lambda gather: _in_proj("ab_proj", hn_ab, w["ab_in"], True, gather))
    y_a = _conv_fwd(proj_a, w["conv"])
    names, gather = carried("sb_fwd")
    y_b, ltot, nblk, got = _sb_fwd(proj_b, gather)
    land(names, got)
    y_ab = jnp.concatenate([y_a, y_b], axis=1)
    h2, hn = out_proj("ab_out", y_ab, w["ab_out"], h1, R_POST)
    h3, hn, post0 = ffn_forward("post0", h2, hn, R_PRE + 1)
    h4, hn_c, pre1 = ffn_forward("pre1", h3, hn, R_MIX + 1)
    proj_c = carrying("c_proj", lambda gather: _in_proj("c_proj", hn_c, w["c_in"], False, gather))
    names, gather = carried("hgrn_fwd")
    o_c, y_c, sst, got = _hgrn_fwd(proj_c, small, gather)
    land(names, got)
    h5, hn = out_proj("c_out", y_c, w["c_out"], h4, R_POST + 1)
    h6, _, post1 = ffn_forward("post1", h5, hn, None)
    dh, dhb, d_fin, loss = _final_loss(h6, small, target)

    dh, dhb, grad_a, grad_b, dn_post1, _, _ = _ffn_backward("post1", dh, dhb, *post1, w["post1_gu"], w["post1_d"],
                                                            *FFN_SLOTS["post1"], small, R_POST + 1, 8 * FS, B_ROWS)
    dy_c, grad_b, swapped = out_proj_bwd("c", dhb, y_c, w["c_out"], B_COUT // 256, grad_b,
                                         lambda buf_b: swap_rider("x", dict(a=grad_a, b=buf_b)))
    pairs, rider = reduce_start("x", dict(a=grad_a, b=grad_b), swapped)
    dproj_c, d_clb, d_gam, landed = _hgrn_bwd(proj_c, small, o_c, sst, dy_c, rider)
    share = reduce_end("x", pairs, landed)
    grad_b = _wgrad("c_dwin", hn_c, _spec((tw, D), lambda s, k: (k, 0)), dproj_c, _spec((None, tw, D), lambda s, k: (s, k, 0)),
                    D, D, t, tw, grad_b, B_CIN // D, carry=share)
    if share is not None:
        grad_b, landed = grad_b
        shared("x", landed)
    dh, dhb, dn_mix1, _ = _dhn_norm("c_dhn", [(dproj_c, _spec((None, th, D), lambda i, s=s: (s, i, 0)),
                                               w["c_in"], _spec((None, D, D), lambda i, s=s: (s, 0, 0)))
                                              for s in range(NSH)],
                                    NT, h4, small, R_MIX + 1, dh)
    stage_w = {}

    def w_swap(buf_a, buf_b):
        return swap_rider("w", dict(a=buf_a, b=buf_b))

    def w_send(buf_a, buf_b, swapped):
        stage_w["pairs"], rider = reduce_start("w", dict(a=buf_a, b=buf_b), swapped)
        return rider

    on = place is not None
    dh, dhb, grad_a, grad_b, dn_pre1, _, landed = _ffn_backward(
        "pre1", dh, dhb, *pre1, w["pre1_gu"], w["pre1_d"], *FFN_SLOTS["pre1"], small, R_PRE + 1, grad_a, grad_b,
        w_swap if on else None, w_send if on else None)
    share = reduce_end("w", stage_w.get("pairs"), landed)
    dh, dhb, grad_a, grad_b, dn_post0, landed, _ = _ffn_backward(
        "post0", dh, dhb, *post0, w["post0_gu"], w["post0_d"], *FFN_SLOTS["post0"], small, R_POST, grad_a, grad_b,
        (lambda buf_a, buf_b: share) if on else None)
    shared("w", landed)
    dy_ab, grad_b, _ = out_proj_bwd("ab", dhb, y_ab, w["ab_out"], B_ABOUT // 256, grad_b)
    dab, dac, dax, d_conv, swapped = _conv_bwd(proj_a, w["conv"], dy_ab, swap_rider("y", dict(a=grad_a, b=grad_b)))
    pairs, rider = reduce_start("y", dict(a=grad_a, b=grad_b), swapped)
    dq, dk, dv, landed = _sb_bwd(proj_b, dy_ab, ltot, nblk, rider)
    share = reduce_end("y", pairs, landed)
    dproj_ab = jnp.concatenate([dab, dac, dax, dq, dk, dv], axis=1)
    grad_c = _wgrad("ab_dwin", hn_ab, _spec((tw, D), lambda s, k: (k, 0)), dproj_ab, _spec((tw, 768), lambda s, k: (k, s)),
                    D, 768, t, tw, D, 0, carry=share)
    if share is not None:
        grad_c, landed = grad_c
        shared("y", landed)
    dh, dhb, dn_mix0, _ = _dhn_norm("ab_dhn", [(dproj_ab, _spec((th, 768), lambda i, s=s: (i, s)),
                                                w["ab_in"], _spec((None, D, 768), lambda i, s=s: (s, 0, 0)))
                                               for s in range(NSH)],
                                    NT, h1, small, R_MIX, dh)
    last = {}

    def z1_send(buf_a, buf_b):
        last["z1"], rider = reduce_start("z1", dict(b=buf_b, c=grad_c))
        return rider

    def z2_send(buf_a, buf_b, _):
        last["z2"], rider = reduce_start("z2", dict(a=buf_a, b=buf_b))
        return rider

    dh, dhb, grad_a, grad_b, dn_pre0, landed, landed_own = _ffn_backward(
        "pre0", dh, dhb, *pre0, w["pre0_gu"], w["pre0_d"], *FFN_SLOTS["pre0"], small, R_PRE, grad_a, grad_b,
        z1_send if on else None, z2_send if on else None)
    if place is not None:
        mine = (_owner_sum("grad_owner_sum_z1", last["z1"], landed, place[1])
                + _owner_sum("grad_owner_sum_z2", last["z2"], landed_own, place[1]))
        reduced.update(zip(REDUCE_STAGES["z1"] + REDUCE_STAGES["z2"], _alone("grad_share_z", _Share(mine))))
    zero = jnp.zeros((1, D), F32)
    conv_rows = jnp.pad(jnp.transpose(d_conv[:, :3, :], (1, 0, 2)).reshape(3, 512), ((0, 0), (0, D - 512)))
    small_grad = jnp.concatenate([
        dn_pre0, dn_pre1, dn_mix0, dn_mix1, dn_post0, dn_post1, d_clb, d_fin,
        jnp.pad(d_gam, ((0, 0), (0, D - HD))), conv_rows,
        jnp.pad(loss, ((0, 0), (0, D - 1))), zero, zero], axis=0)
    return dh, grad_a, grad_b, grad_c, small_grad, reduced


def _small_slab(rows):
    parts = [jnp.pad(r.astype(F32), ((0, 0), (0, D - r.shape[1]))) for r in rows]
    slab = jnp.concatenate(parts, axis=0)
    return jnp.pad(slab, ((0, SMALL_ROWS - slab.shape[0]), (0, 0)))


def kernel(x, ffn_pre_norm, ffn_pre_w_gate, ffn_pre_w_up, ffn_pre_w_down, mix_norm, ffn_post_norm, ffn_post_w_gate, ffn_post_w_up, ffn_post_w_down, ab_w_in, ab_conv_w, ab_w_out, c_w_in, c_lower_bounds, c_out_norm, c_w_out, final_norm, loss_target, m_ffn_pre_norm, m_ffn_pre_w_gate, m_ffn_pre_w_up, m_ffn_pre_w_down, m_mix_norm, m_ffn_post_norm, m_ffn_post_w_gate, m_ffn_post_w_up, m_ffn_post_w_down, m_ab_w_in, m_ab_conv_w, m_ab_w_out, m_c_w_in, m_c_lower_bounds, m_c_out_norm, m_c_w_out, m_final_norm, v_ffn_pre_norm, v_ffn_pre_w_gate, v_ffn_pre_w_up, v_ffn_pre_w_down, v_mix_norm, v_ffn_post_norm, v_ffn_post_w_gate, v_ffn_post_w_up, v_ffn_post_w_down, v_ab_w_in, v_ab_conv_w, v_ab_w_out, v_c_w_in, v_c_lower_bounds, v_c_out_norm, v_c_w_out, v_final_norm):
    t = x.shape[1]
    xi, yi, ci = lax.axis_index("x"), lax.axis_index("y"), lax.axis_index("c")
    p_idx = (2 * xi + yi).astype(jnp.int32).reshape(1)
    c_idx = ci.astype(jnp.int32).reshape(1)

    def halves(m):
        return m.astype(BF16).reshape(2, m.shape[0] // 2, m.shape[1])

    transposed = ("ffn_pre_w_gate", "ffn_pre_w_up", "ffn_post_w_gate", "ffn_post_w_up")

    def flip(a):
        return jnp.swapaxes(a, 1, 2)

    shards = {}
    for name, (w_gate, w_up, w_down, layer) in dict(
            pre0=(ffn_pre_w_gate, ffn_pre_w_up, ffn_pre_w_down, 0), post0=(ffn_post_w_gate, ffn_post_w_up, ffn_post_w_down, 0),
            pre1=(ffn_pre_w_gate, ffn_pre_w_up, ffn_pre_w_down, 1), post1=(ffn_post_w_gate, ffn_post_w_up, ffn_post_w_down, 1)).items():
        shards[name + "_gu"] = jnp.stack([flip(w_gate)[layer], flip(w_up)[layer]]).astype(BF16)
        shards[name + "_d"] = halves(w_down[layer])
    conv_pad = jnp.pad(ab_conv_w[0], ((0, 5), (0, 0)))
    shards.update(ab_in=halves(ab_w_in[0]), ab_out=halves(ab_w_out[0]), c_in=halves(c_w_in[0]), c_out=halves(c_w_out[0]),
                  conv=jnp.stack([conv_pad, jnp.zeros_like(conv_pad)]))
    small = _small_slab([ffn_pre_norm, mix_norm, ffn_post_norm, c_lower_bounds, final_norm.reshape(1, D), c_out_norm])
    small = small.reshape(SMALL_ROWS, 1, D)

    grad_x, _, _, _, small_grad, reduced = _local_step(x[0], loss_target[0], {}, small, shards,
                                                        (c_idx, p_idx))
    whole = {n: g.reshape(2 * g.shape[1], g.shape[2]) for n, g in reduced.items()}
    small_sum = _sum_small(small_grad)

    my_conv = lax.dynamic_slice(small_sum[R_CONV:R_CONV + 3], (0, (2 * xi + yi) * 128), (3, 128))
    grads = {
        "ffn_pre_norm": small_sum[R_PRE:R_PRE + 2], "mix_norm": small_sum[R_MIX:R_MIX + 2],
        "ffn_post_norm": small_sum[R_POST:R_POST + 2], "c_lower_bounds": small_sum[R_CLB:R_CLB + 2],
        "c_out_norm": small_sum[R_GAM:R_GAM + 1, :HD], "final_norm": small_sum[R_FIN],
        "ab_conv_w": my_conv.reshape(1, 3, 128),
    }
    layers = dict(ab_w_in=[whole["ab_in"]], ab_w_out=[whole["ab_out"]], c_w_in=[whole["c_in"]], c_w_out=[whole["c_out"]])
    for kind, key in (("gate", "_g"), ("up", "_u"), ("down", "_d")):
        layers["ffn_pre_w_" + kind] = [whole["pre0" + key], whole["pre1" + key]]
        layers["ffn_post_w_" + kind] = [whole["post0" + key], whole["post1" + key]]
    weights = dict(ffn_pre_norm=ffn_pre_norm, ffn_pre_w_gate=ffn_pre_w_gate, ffn_pre_w_up=ffn_pre_w_up, ffn_pre_w_down=ffn_pre_w_down, mix_norm=mix_norm, ffn_post_norm=ffn_post_norm, ffn_post_w_gate=ffn_post_w_gate, ffn_post_w_up=ffn_post_w_up, ffn_post_w_down=ffn_post_w_down, ab_w_in=ab_w_in, ab_conv_w=ab_conv_w, ab_w_out=ab_w_out, c_w_in=c_w_in, c_lower_bounds=c_lower_bounds, c_out_norm=c_out_norm, c_w_out=c_w_out, final_norm=final_norm)
    m_in = dict(ffn_pre_norm=m_ffn_pre_norm, ffn_pre_w_gate=m_ffn_pre_w_gate, ffn_pre_w_up=m_ffn_pre_w_up, ffn_pre_w_down=m_ffn_pre_w_down, mix_norm=m_mix_norm, ffn_post_norm=m_ffn_post_norm, ffn_post_w_gate=m_ffn_post_w_gate, ffn_post_w_up=m_ffn_post_w_up, ffn_post_w_down=m_ffn_post_w_down, ab_w_in=m_ab_w_in, ab_conv_w=m_ab_conv_w, ab_w_out=m_ab_w_out, c_w_in=m_c_w_in, c_lower_bounds=m_c_lower_bounds, c_out_norm=m_c_out_norm, c_w_out=m_c_w_out, final_norm=m_final_norm)
    v_in = dict(ffn_pre_norm=v_ffn_pre_norm, ffn_pre_w_gate=v_ffn_pre_w_gate, ffn_pre_w_up=v_ffn_pre_w_up, ffn_pre_w_down=v_ffn_pre_w_down, mix_norm=v_mix_norm, ffn_post_norm=v_ffn_post_norm, ffn_post_w_gate=v_ffn_post_w_gate, ffn_post_w_up=v_ffn_post_w_up, ffn_post_w_down=v_ffn_post_w_down, ab_w_in=v_ab_w_in, ab_conv_w=v_ab_conv_w, ab_w_out=v_ab_w_out, c_w_in=v_c_w_in, c_lower_bounds=v_c_lower_bounds, c_out_norm=v_c_out_norm, c_w_out=v_c_w_out, final_norm=v_final_norm)
    names = list(weights)
    big = [n for n in names if weights[n].size >= 65536]
    tiny = [n for n in names if n not in big]

    delta, new_m, new_v = {}, {}, {}
    for n in big:
        turn = flip if n in transposed else (lambda a: a)
        shape = turn(weights[n]).shape
        two_d = (shape[0] * shape[1], shape[2])
        d, m2, v2, g2 = _adamw("adamw_" + n, turn(weights[n]).reshape(two_d), layers[n],
                               turn(m_in[n]).reshape(two_d), turn(v_in[n]).reshape(two_d))
        delta[n], new_m[n], new_v[n] = turn(d.reshape(shape)), turn(m2.reshape(shape)), turn(v2.reshape(shape))
        grads[n] = turn(g2.reshape(shape))

    def tiny_slab(src):
        return _small_slab([src[n].reshape(-1, src[n].shape[-1]) for n in tiny])

    offs, row = {}, 0
    for n in tiny:
        nrows = weights[n].size // weights[n].shape[-1]
        offs[n] = (row, nrows)
        row += nrows
    d, m2, v2, _ = _adamw("adamw_small", tiny_slab(weights), [tiny_slab(grads)], tiny_slab(m_in), tiny_slab(v_in))
    for n in tiny:
        r0, nr = offs[n]
        shape = weights[n].shape
        for dst, src in ((delta, d), (new_m, m2), (new_v, v2)):
            dst[n] = src[r0:r0 + nr, :shape[-1]].reshape(shape)

    loss = small_sum[R_LOSS, 0]
    return (loss, grad_x.reshape(1, t, D), *[grads[n] for n in names], *[delta[n] for n in names],
            *[new_m[n] for n in names], *[new_v[n] for n in names])
```

```python
import functools
import math

import jax
import jax.numpy as jnp
from jax import lax
from jax.experimental import pallas as pl
from jax.experimental.pallas import tpu as pltpu

F32 = jnp.float32
BF16 = jnp.bfloat16
MESH = pl.DeviceIdType.MESH
ANY = pl.BlockSpec(memory_space=pl.ANY)

D = 1024
FS = 704
NSH = 4
RMS_EPS = 1e-6
MACARON = 0.5
CHUNK = 64
HD = 128
HGRN_HPS = 8
SBQ = 128
SB_PAIRS_FWD = 4
SB_PAIRS_BWD = 4
SB_DEAD = -105.0
CONV_HALO = 8
LANE = 128
ROW_TILE = 1024
MM_TILE = 1024
HALF_TILE = MM_TILE // 2
WGRAD_TILE = 4096
VMEM_LIMIT = 48 * 1024 * 1024
VMEM_LIMIT_BIG = 58 * 1024 * 1024

ADAM_LR, ADAM_B1, ADAM_B2, ADAM_EPS, ADAM_WD, ADAM_STEP = 0.001, 0.9, 0.999, 1e-08, 0.01, 10

NN = ((1,), (0,))
NT = ((1,), (1,))
TN = ((0,), (0,))

SMALL_ROWS = 16
R_PRE, R_MIX, R_POST, R_CLB, R_FIN, R_GAM, R_CONV, R_LOSS = 0, 2, 4, 6, 8, 9, 10, 13

B_ABOUT = 4 * FS
B_CIN = B_ABOUT + 256
B_COUT = B_CIN + 1024
B_ROWS = B_COUT + 256


def _dg(a, b, dims):
    return lax.dot_general(a, b, (dims, ((), ())), preferred_element_type=F32)


def _split(x):
    hi = x.astype(BF16)
    lo = (x - hi.astype(F32)).astype(BF16)
    return hi, lo


def _dot3(a, b, dims):
    ah, al = _split(a)
    bh, bl = _split(b)
    if dims == TN:
        n = b.shape[1]
        both = _dg(ah, jnp.concatenate([bh, bl], axis=1), dims)
        return both[:, :n] + both[:, n:] + _dg(al, bh, dims)
    m = a.shape[0]
    both = _dg(jnp.concatenate([ah, al], axis=0), bh, dims)
    return both[:m] + both[m:] + _dg(ah, bl, dims)


def _sigmoid(x):
    return 1.0 / (1.0 + jnp.exp(-x))


def _params(sem):
    return pltpu.CompilerParams(dimension_semantics=sem, vmem_limit_bytes=VMEM_LIMIT)


def _spec(shape, imap):
    return pl.BlockSpec(shape, imap)


def _accumulate(acc_ref, pairs, dims):
    part = None
    for a_ref, b_ref in pairs:
        d = _dg(a_ref[...], b_ref[...], dims)
        part = d if part is None else part + d
    acc_ref[...] += part


def _mm(name, pairs, *, grid, o_shape, o_dtype, o_spec, dims, kaxis, nk, acc_shape=None, res=None, scale=None,
        into=None, carry=None, norm=None):
    npairs = len(pairs)
    operands, specs = [], []
    for a, a_spec, b, b_spec in pairs:
        operands += [a, b]
        specs += [a_spec, b_spec]
    if res is not None:
        operands.append(res[0])
        specs.append(res[1])
    aliases = {}
    if into is not None:
        aliases = {len(operands): 0}
        operands.append(into)
        specs.append(ANY)
    if norm is not None:
        operands.append(norm[0])
        specs.append(_spec((None, 1, D), lambda *_: (norm[1], 0, 0)))
    n_own = len(operands)
    n_out = 2 if norm is not None else 1
    nc = carry.n if carry is not None else 0
    if nc:
        operands += carry.groups
        specs += carry.in_specs

    def body(*refs):
        o_ref = refs[n_own + nc]
        riders = ((refs[n_own:n_own + nc], refs[n_own + nc + n_out:n_own + 2 * nc + n_out], refs[-2], refs[-1])
                  if nc else None)
        if nc:
            carry.ride(grid, riders, "start")

        def finish(val):
            if scale is not None:
                val = val * scale
            if res is not None:
                val = val + refs[2 * npairs][...]
            o_ref[...] = val.astype(o_dtype)
            if norm is not None:
                r = lax.rsqrt(jnp.mean(val * val, axis=-1, keepdims=True) + RMS_EPS)
                refs[n_own + nc + 1][...] = (val * r * refs[n_own - 1][...]).astype(BF16)

        if nk == 1:
            part = None
            for n in range(npairs):
                d = _dg(refs[2 * n][...], refs[2 * n + 1][...], dims)
                part = d if part is None else part + d
            finish(part)
        else:
            acc_ref = refs[n_own + 2 * nc + n_out]
            k = pl.program_id(kaxis)

            @pl.when(k == 0)
            def _():
                acc_ref[...] = jnp.zeros_like(acc_ref)

            _accumulate(acc_ref, [(refs[2 * n], refs[2 * n + 1]) for n in range(npairs)], dims)

            @pl.when(k == nk - 1)
            def _():
                finish(acc_ref[...])
        if nc:
            carry.ride(grid, riders, "finish")

    sem = tuple("arbitrary" if (nc or (ax == kaxis and nk > 1)) else "parallel" for ax in range(len(grid)))
    outs = pl.pallas_call(
        body, name=name, grid=grid, in_specs=specs,
        out_specs=[o_spec] * n_out + (carry.out_specs if nc else []),
        out_shape=[jax.ShapeDtypeStruct(o_shape, o_dtype)] + ([jax.ShapeDtypeStruct(o_shape, BF16)] if norm is not None else [])
        + (carry.out_shape if nc else []),
        scratch_shapes=([pltpu.VMEM(acc_shape, F32)] if nk > 1 else []) + (carry.scratch if nc else []),
        input_output_aliases=aliases,
        compiler_params=_params(sem),
    )(*operands)
    main = (outs[0], outs[1]) if norm is not None else outs[0]
    return (main, carry.place(outs[n_out:])) if nc else main


def _norm_fwd(name, h, gain_slab, row, rider=None):
    t = h.shape[0]
    tm = min(ROW_TILE, t)
    grid = (t // tm,)
    nr = rider.n if rider is not None else 0

    def body(*refs):
        h_ref, g_ref, o_ref = refs[0], refs[1], refs[2 + nr]
        riders = (refs[2:2 + nr], refs[3 + nr:3 + 2 * nr], refs[-2], refs[-1]) if nr else None
        if nr:
            rider.ride(grid, riders, "start")
        x = h_ref[...]
        r = lax.rsqrt(jnp.mean(x * x, axis=-1, keepdims=True) + RMS_EPS)
        o_ref[...] = (x * r * g_ref[...]).astype(BF16)
        if nr:
            rider.ride(grid, riders, "finish")

    outs = pl.pallas_call(
        body, name=name, grid=grid,
        in_specs=[_spec((tm, D), lambda i: (i, 0)), _spec((None, 1, D), lambda i: (row, 0, 0))]
        + (rider.in_specs if nr else []),
        out_specs=[_spec((tm, D), lambda i: (i, 0))] + (rider.out_specs if nr else []),
        out_shape=[jax.ShapeDtypeStruct((t, D), BF16)] + (rider.out_shape if nr else []),
        scratch_shapes=rider.scratch if nr else [],
        compiler_params=_params(("arbitrary",) if nr else ("parallel",)),
    )(h, gain_slab, *(rider.groups if nr else []))
    return outs[0], (rider.place(outs[1:]) if nr else [])


def _dhn_norm(name, pairs, dims, h, gain_slab, row, dres, rider=None):
    t = h.shape[0]
    tm = min(HALF_TILE, t)
    nt = t // tm
    grid = (nt,)
    npairs = len(pairs)
    nr = rider.n if rider is not None else 0
    operands, specs = [], []
    for a, a_spec, b, b_spec in pairs:
        operands += [a, b]
        specs += [a_spec, b_spec]
    row_spec = _spec((tm, D), lambda i: (i, 0))
    operands += [h, gain_slab, dres]
    specs += [row_spec, _spec((None, 1, D), lambda i: (row, 0, 0)), row_spec]
    n_own = len(operands)

    def body(*refs):
        h_ref, g_ref, dres_ref = refs[2 * npairs:n_own]
        dh_ref, dhb_ref, dg_ref = refs[n_own + nr:n_own + nr + 3]
        gacc_ref = refs[n_own + 2 * nr + 3]
        riders = (refs[n_own:n_own + nr], refs[n_own + nr + 3:n_own + 2 * nr + 3], refs[-2], refs[-1]) if nr else None
        if nr:
            rider.ride(grid, riders, "start")
        i = pl.program_id(0)
        dy = None
        for n in range(npairs):
            d = _dg(refs[2 * n][...], refs[2 * n + 1][...], dims)
            dy = d if dy is None else dy + d
        x = h_ref[...]
        r = lax.rsqrt(jnp.mean(x * x, axis=-1, keepdims=True) + RMS_EPS)
        xh = x * r
        gdy = dy * g_ref[...]
        dh = dres_ref[...] + (gdy - xh * jnp.mean(gdy * xh, axis=-1, keepdims=True)) * r
        dh_ref[...] = dh
        dhb_ref[...] = dh.astype(BF16)
        gpart = jnp.sum((dy * xh).reshape(tm // 8, 8, D), axis=0)

        @pl.when(i == 0)
        def _():
            gacc_ref[...] = gpart

        @pl.when(i > 0)
        def _():
            gacc_ref[...] += gpart

        @pl.when(i == nt - 1)
        def _():
            dg_ref[...] = jnp.sum(gacc_ref[...], axis=0, keepdims=True)

        if nr:
            rider.ride(grid, riders, "finish")

    outs = pl.pallas_call(
        body, name=name, grid=grid, in_specs=specs + (rider.in_specs if nr else []),
        out_specs=[row_spec, row_spec, _spec((1, D), lambda i: (0, 0))] + (rider.out_specs if nr else []),
        out_shape=[jax.ShapeDtypeStruct((t, D), F32), jax.ShapeDtypeStruct((t, D), BF16),
                   jax.ShapeDtypeStruct((1, D), F32)] + (rider.out_shape if nr else []),
        scratch_shapes=[pltpu.VMEM((8, D), F32)] + (rider.scratch if nr else []),
        compiler_params=pltpu.CompilerParams(dimension_semantics=("arbitrary",),
                                             vmem_limit_bytes=VMEM_LIMIT_BIG),
    )(*operands, *(rider.groups if nr else []))
    return outs[0], outs[1], outs[2], (rider.place(outs[3:]) if nr else [])


def _final_loss(h, gain_slab, target):
    t = h.shape[0]
    tm = min(ROW_TILE, t)
    nt = t // tm

    def body(h_ref, g_ref, t_ref, dh_ref, dhb_ref, dg_ref, loss_ref, acc_ref, lacc_ref):
        i = pl.program_id(0)
        x = h_ref[...]
        g = g_ref[...]
        r = lax.rsqrt(jnp.mean(x * x, axis=-1, keepdims=True) + RMS_EPS)
        xh = x * r
        err = xh * g - t_ref[...]
        dy = err * (1.0 / D)
        gdy = dy * g
        dh = (gdy - xh * jnp.mean(gdy * xh, axis=-1, keepdims=True)) * r
        dh_ref[...] = dh
        dhb_ref[...] = dh.astype(BF16)
        part = jnp.sum((dy * xh).reshape(tm // 8, 8, D), axis=0)
        lpart = jnp.sum((err * err).reshape(tm // 8, 8, D), axis=0)

        @pl.when(i == 0)
        def _():
            acc_ref[...] = part
            lacc_ref[...] = lpart

        @pl.when(i > 0)
        def _():
            acc_ref[...] += part
            lacc_ref[...] += lpart

        @pl.when(i == nt - 1)
        def _():
            dg_ref[...] = jnp.sum(acc_ref[...], axis=0, keepdims=True)
            rows = jnp.sum(lacc_ref[...], axis=0, keepdims=True)
            loss_ref[...] = jnp.sum(rows, axis=1, keepdims=True) * (0.5 / D)

    row_spec = _spec((tm, D), lambda i: (i, 0))
    return pl.pallas_call(
        body, name="final_loss", grid=(nt,),
        in_specs=[row_spec, _spec((None, 1, D), lambda i: (R_FIN, 0, 0)), row_spec],
        out_specs=[row_spec, row_spec, _spec((1, D), lambda i: (0, 0)), _spec((1, 1), lambda i: (0, 0))],
        out_shape=[jax.ShapeDtypeStruct((t, D), F32), jax.ShapeDtypeStruct((t, D), BF16),
                   jax.ShapeDtypeStruct((1, D), F32), jax.ShapeDtypeStruct((1, 1), F32)],
        scratch_shapes=[pltpu.VMEM((8, D), F32), pltpu.VMEM((8, D), F32)],
        compiler_params=_params(("arbitrary",)),
    )(h, gain_slab, target)


def _ffn_up(name, hn, wgu, carry=None):
    t = hn.shape[0]
    tm = min(MM_TILE, t)
    grid = (NSH, t // tm)
    nc = carry.n if carry is not None else 0

    def body(*refs):
        x_ref, wg_ref, wu_ref = refs[:3]
        s_up_ref, s_gate_ref, a_ref = refs[3 + nc:6 + nc]
        riders = (refs[3:3 + nc], refs[6 + nc:6 + 2 * nc], refs[-2], refs[-1]) if nc else None
        if nc:
            carry.ride(grid, riders, "start")
        x = x_ref[...]
        g = _dg(x, wg_ref[...], NT)
        u = _dg(x, wu_ref[...], NT)
        sg = _sigmoid(g)
        silu = g * sg
        s_up_ref[...] = (MACARON * silu).astype(BF16)
        s_gate_ref[...] = (MACARON * u * (sg * (1.0 + g * (1.0 - sg)))).astype(BF16)
        a_ref[...] = (silu * u).astype(BF16)
        if nc:
            carry.ride(grid, riders, "finish")

    act = _spec((None, tm, FS), lambda s, i: (s, i, 0))
    shape = jax.ShapeDtypeStruct((NSH, t, FS), BF16)
    outs = pl.pallas_call(
        body, name=name, grid=grid,
        in_specs=[_spec((tm, D), lambda s, i: (i, 0)),
                  _spec((None, None, FS, D), lambda s, i: (s, 0, 0, 0)),
                  _spec((None, None, FS, D), lambda s, i: (s, 1, 0, 0))] + (carry.in_specs if nc else []),
        out_specs=[act, act, act] + (carry.out_specs if nc else []),
        out_shape=[shape, shape, shape] + (carry.out_shape if nc else []),
        scratch_shapes=carry.scratch if nc else [],
        compiler_params=_params(("arbitrary", "arbitrary") if nc else ("parallel", "parallel")),
    )(hn, wgu, wgu, *(carry.groups if nc else []))
    return (outs[0], outs[1], outs[2], carry.place(outs[3:]) if nc else [])


def _ffn_down(name, a, wd, h, carry=None, norm=None):
    t = h.shape[0]
    tm = min(MM_TILE, t)
    row_spec = _spec((tm, D), lambda i: (i, 0))
    return _mm(name, [(a, _spec((None, tm, FS), lambda i, s=s: (s, i, 0)),
                       wd, pl.BlockSpec((None, FS, D), lambda i, s=s: (s, 0, 0), pipeline_mode=pl.Buffered(1)))
                      for s in range(NSH)],
               grid=(t // tm,), o_shape=(t, D), o_dtype=F32, o_spec=row_spec, dims=NN, kaxis=0, nk=1,
               res=(h, row_spec), scale=MACARON, carry=carry, norm=norm)


def _in_proj(name, hn, w_in, two_branches, carry=None):
    t = hn.shape[0]
    tm = min(HALF_TILE, t)
    grid = (t // tm,)
    cols = w_in.shape[2]
    nc = carry.n if carry is not None else 0
    n_out = 2 if two_branches else 1

    def body(*refs):
        x_ref, w_ref = refs[:2]
        o_refs = refs[2 + nc:2 + nc + n_out]
        riders = (refs[2:2 + nc], refs[2 + nc + n_out:2 + 2 * nc + n_out], refs[-2], refs[-1]) if nc else None
        if nc:
            carry.ride(grid, riders, "start")
        x = x_ref[...]
        if two_branches:
            for s in range(2):
                o_refs[0][:, s * cols:(s + 1) * cols] = _dg(x, w_ref[s], NN)
                o_refs[1][:, s * cols:(s + 1) * cols] = _dg(x, w_ref[s + 2], NN).astype(BF16)
        else:
            for s in range(NSH):
                o_refs[0][s] = _dg(x, w_ref[s], NN)
        if nc:
            carry.ride(grid, riders, "finish")

    if two_branches:
        out_specs = [_spec((tm, 2 * cols), lambda i: (i, 0))] * 2
        out_shape = [jax.ShapeDtypeStruct((t, 2 * cols), F32), jax.ShapeDtypeStruct((t, 2 * cols), BF16)]
    else:
        out_specs = [_spec((NSH, tm, cols), lambda i: (0, i, 0))]
        out_shape = [jax.ShapeDtypeStruct((NSH, t, cols), F32)]
    outs = pl.pallas_call(
        body, name=name, grid=grid,
        in_specs=[_spec((tm, D), lambda i: (i, 0)), _spec((NSH, D, cols), lambda i: (0, 0, 0))]
        + (carry.in_specs if nc else []),
        out_specs=out_specs + (carry.out_specs if nc else []),
        out_shape=out_shape + (carry.out_shape if nc else []),
        scratch_shapes=carry.scratch if nc else [],
        compiler_params=_params(("arbitrary",) if nc else ("parallel",)),
    )(hn, w_in, *(carry.groups if nc else []))
    main = (outs[0], outs[1]) if two_branches else outs[0]
    return (main, carry.place(outs[n_out:])) if nc else main


def _ffn_bwd_up(name, dhb, wd, s_up, s_gate, rider=None):
    t = dhb.shape[0]
    tm = min(2 * MM_TILE, t)
    grid = (t // tm, NSH)
    nr = rider.n if rider is not None else 0

    def body(*refs):
        dh_ref, wd_ref, s_up_ref, s_gate_ref = refs[:4]
        dg_ref, du_ref = refs[4 + nr:6 + nr]
        riders = (refs[4:4 + nr], refs[6 + nr:6 + 2 * nr], refs[-2], refs[-1]) if nr else None
        if nr:
            rider.ride(grid, riders, "start")
        da = _dg(dh_ref[...], wd_ref[...], NT).astype(BF16)
        du_ref[...] = da * s_up_ref[...]
        dg_ref[...] = da * s_gate_ref[...]
        if nr:
            rider.ride(grid, riders, "finish")

    act = _spec((None, tm, FS), lambda i, s: (s, i, 0))
    shape = jax.ShapeDtypeStruct((NSH, t, FS), BF16)
    outs = pl.pallas_call(
        body, name=name, grid=grid,
        in_specs=[_spec((tm, D), lambda i, s: (i, 0)), _spec((None, FS, D), lambda i, s: (s, 0, 0)), act, act]
        + (rider.in_specs if nr else []),
        out_specs=[act, act] + (rider.out_specs if nr else []),
        out_shape=[shape, shape] + (rider.out_shape if nr else []),
        scratch_shapes=rider.scratch if nr else [],
        compiler_params=pltpu.CompilerParams(
            dimension_semantics=("arbitrary", "arbitrary") if nr else ("parallel", "parallel"),
            vmem_limit_bytes=VMEM_LIMIT_BIG),
    )(dhb, wd, s_up, s_gate, *(rider.groups if nr else []))
    return outs[0], outs[1], (rider.place(outs[2:]) if nr else [])


def _wgrad(name, a, a_spec, b, b_spec, out_rows, out_cols, t, tt, group, slot, scale=None, carry=None):
    first = isinstance(group, int)
    rows = group if first else group.shape[1]
    return _mm(name, [(a, a_spec, b, b_spec)], grid=(NSH, t // tt),
               o_shape=(NSH, rows, out_cols), o_dtype=BF16,
               o_spec=_spec((None, out_rows, out_cols), lambda s, k: (s, slot, 0)),
               dims=TN, kaxis=1, nk=t // tt, acc_shape=(out_rows, out_cols), scale=scale,
               into=None if first else group, carry=carry)


def _wgrad_gate_up(name, dg, du, hn, group, gate_idx):
    t = hn.shape[0]
    first = isinstance(group, int)
    rows = group if first else group.shape[1]

    def body(dg_ref, du_ref, hn_ref, *refs):
        o_ref = refs[-1]
        x = hn_ref[...]
        o_ref[...] = jnp.concatenate([_dg(dg_ref[...], x, TN), _dg(du_ref[...], x, TN)], axis=0).astype(BF16)

    hid = _spec((None, t, FS), lambda s: (s, 0, 0))
    return pl.pallas_call(
        body, name=name, grid=(NSH,),
        in_specs=[hid, hid, _spec((t, D), lambda s: (0, 0))] + ([] if first else [ANY]),
        out_specs=_spec((None, 2 * FS, D), lambda s: (s, gate_idx // 2, 0)),
        out_shape=jax.ShapeDtypeStruct((NSH, rows, D), BF16),
        input_output_aliases={} if first else {3: 0},
        compiler_params=pltpu.CompilerParams(dimension_semantics=("parallel",), vmem_limit_bytes=VMEM_LIMIT_BIG),
    )(dg, du, hn, *([] if first else [group]))


def _ffn_backward(tag, dh, dhb, h_in, hn, s_up, s_gate, a, wgu, wd, gate_idx, up_idx, down_idx, small, norm_row,
                  grad_a, grad_b, rider_up=None, rider_dhn=None):
    t = dh.shape[0]
    tt = min(WGRAD_TILE, t)
    tok = _spec((tt, D), lambda s, k: (k, 0))
    hid = _spec((None, tt, FS), lambda s, k: (s, k, 0))
    grad_b = _wgrad(tag + "_dwd", a, hid, dhb, tok, FS, D, t, tt, grad_b, down_idx, scale=MACARON)
    rider = rider_up(grad_a, grad_b) if rider_up is not None else None
    dg, du, landed_up = _ffn_bwd_up(tag + "_bwd_up", dhb, wd, s_up, s_gate, rider)
    assert up_idx == gate_idx + 1 and gate_idx % 2 == 0
    grad_a = _wgrad_gate_up(tag + "_dwgu", dg, du, hn, grad_a, gate_idx)
    rider = rider_dhn(grad_a, grad_b, landed_up) if rider_dhn is not None else None
    th = min(HALF_TILE, t)
    pairs = []
    for s in range(NSH):
        act = _spec((None, th, FS), lambda i, s=s: (s, i, 0))
        pairs += [(dg, act, wgu, _spec((None, None, FS, D), lambda i, s=s: (s, 0, 0, 0))),
                  (du, act, wgu, _spec((None, None, FS, D), lambda i, s=s: (s, 1, 0, 0)))]
    dh_in, dhb_in, d_gain, landed_dhn = _dhn_norm(tag + "_dhn", pairs, NN, h_in, small, norm_row, dh, rider)
    return dh_in, dhb_in, grad_a, grad_b, d_gain, landed_up, landed_dhn


def _conv_fwd(proj_a, conv_w):
    t = proj_a.shape[0]
    tm = min(ROW_TILE, t)
    hb = tm // CONV_HALO

    def body(ab_ref, ac_ref, ax_ref, acp_ref, axp_ref, w_ref, y_ref):
        i = pl.program_id(1)
        u = ac_ref[...] * ax_ref[...]
        up = jnp.where(i > 0, acp_ref[...] * axp_ref[...], 0.0)
        ext = jnp.concatenate([up, u], axis=0)
        u1 = pltpu.roll(ext, 1, 0)[CONV_HALO:]
        u2 = pltpu.roll(ext, 2, 0)[CONV_HALO:]
        w = w_ref[...]
        conv = w[0:1] * u2 + w[1:2] * u1 + w[2:3] * u
        y_ref[...] = (ab_ref[...] * conv).astype(BF16)

    def cur(off):
        return _spec((tm, LANE), lambda j, i: (i, off + j))

    def prev(off):
        return _spec((CONV_HALO, LANE), lambda j, i: (jnp.maximum(i * hb - 1, 0), off + j))

    return pl.pallas_call(
        body, name="conv_fwd", grid=(4, t // tm),
        in_specs=[cur(0), cur(4), cur(8), prev(4), prev(8),
                  _spec((None, None, 8, LANE), lambda j, i: (j, 0, 0, 0))],
        out_specs=_spec((tm, LANE), lambda j, i: (i, j)),
        out_shape=jax.ShapeDtypeStruct((t, 512), BF16),
        compiler_params=_params(("parallel", "parallel")),
    )(proj_a, proj_a, proj_a, proj_a, proj_a, conv_w)


def _conv_bwd(proj_a, conv_w, dy, rider=None):
    t = proj_a.shape[0]
    tm = min(ROW_TILE, t)
    hb = tm // CONV_HALO
    nt = t // tm
    grid = (4, nt)
    nr = rider.n if rider is not None else 0

    def body(*refs):
        ab_ref, ac_ref, ax_ref, dy_ref, acp_ref, axp_ref, abn_ref, dyn_ref, w_ref = refs[:9]
        dab_ref, dac_ref, dax_ref, dw_ref = refs[9 + nr:13 + nr]
        acc_ref = refs[13 + 2 * nr]
        riders = (refs[9:9 + nr], refs[13 + nr:13 + 2 * nr], refs[-2], refs[-1]) if nr else None
        if nr:
            rider.ride(grid, riders, "start")
        i = pl.program_id(1)
        ab, ac, ax = ab_ref[...], ac_ref[...], ax_ref[...]
        u = ac * ax
        up = jnp.where(i > 0, acp_ref[...] * axp_ref[...], 0.0)
        ext = jnp.concatenate([up, u], axis=0)
        u1 = pltpu.roll(ext, 1, 0)[CONV_HALO:]
        u2 = pltpu.roll(ext, 2, 0)[CONV_HALO:]
        w = w_ref[...]
        conv = w[0:1] * u2 + w[1:2] * u1 + w[2:3] * u
        dy_v = dy_ref[...]
        dab_ref[...] = (dy_v * conv).astype(BF16)
        dc = dy_v * ab
        dcn = jnp.where(i < nt - 1, dyn_ref[...] * abn_ref[...], 0.0)
        extn = jnp.concatenate([dc, dcn], axis=0)
        n = tm + CONV_HALO
        dc1 = pltpu.roll(extn, n - 1, 0)[:tm]
        dc2 = pltpu.roll(extn, n - 2, 0)[:tm]
        du = w[2:3] * dc + w[1:2] * dc1 + w[0:1] * dc2
        dac_ref[...] = (du * ax).astype(BF16)
        dax_ref[...] = (du * ac).astype(BF16)
        rid = lax.broadcasted_iota(jnp.int32, (8, LANE), 0)
        part = jnp.where(rid == 0, jnp.sum(dc * u2, axis=0, keepdims=True),
                         jnp.where(rid == 1, jnp.sum(dc * u1, axis=0, keepdims=True),
                                   jnp.where(rid == 2, jnp.sum(dc * u, axis=0, keepdims=True), 0.0)))

        @pl.when(i == 0)
        def _():
            acc_ref[...] = part

        @pl.when(i > 0)
        def _():
            acc_ref[...] += part

        @pl.when(i == nt - 1)
        def _():
            dw_ref[...] = acc_ref[...]

        if nr:
            rider.ride(grid, riders, "finish")

    def cur(off):
        return _spec((tm, LANE), lambda j, i: (i, off + j))

    def prev(off):
        return _spec((CONV_HALO, LANE), lambda j, i: (jnp.maximum(i * hb - 1, 0), off + j))

    def nxt(off):
        return _spec((CONV_HALO, LANE), lambda j, i: (jnp.minimum((i + 1) * hb, nt * hb - 1), off + j))

    outs = pl.pallas_call(
        body, name="conv_bwd", grid=grid,
        in_specs=[cur(0), cur(4), cur(8), cur(0), prev(4), prev(8), nxt(0), nxt(0),
                  _spec((None, None, 8, LANE), lambda j, i: (j, 0, 0, 0))] + (rider.in_specs if nr else []),
        out_specs=[_spec((tm, LANE), lambda j, i: (i, j)), _spec((tm, LANE), lambda j, i: (i, j)),
                   _spec((tm, LANE), lambda j, i: (i, j)), _spec((None, 8, LANE), lambda j, i: (j, 0, 0))]
        + (rider.out_specs if nr else []),
        out_shape=[jax.ShapeDtypeStruct((t, 512), BF16), jax.ShapeDtypeStruct((t, 512), BF16),
                   jax.ShapeDtypeStruct((t, 512), BF16), jax.ShapeDtypeStruct((4, 8, LANE), F32)]
        + (rider.out_shape if nr else []),
        scratch_shapes=[pltpu.VMEM((8, LANE), F32)] + (rider.scratch if nr else []),
        compiler_params=_params(("arbitrary", "arbitrary") if nr else ("parallel", "arbitrary")),
    )(proj_a, proj_a, proj_a, dy, proj_a, proj_a, proj_a, dy, conv_w, *(rider.groups if nr else []))
    return outs[0], outs[1], outs[2], outs[3], (rider.place(outs[4:]) if nr else [])


def _log_sigmoid(z):
    return jnp.minimum(z, 0.0) - jnp.log(1.0 + jnp.exp(-jnp.abs(z)))


def _sb_masks():
    row = lax.broadcasted_iota(jnp.int32, (SBQ, SBQ), 0)
    col = lax.broadcasted_iota(jnp.int32, (SBQ, SBQ), 1)
    return row, col


def _ones_where(mask):
    return jnp.where(mask, 1.0, 0.0).astype(BF16)


def _head_mask(head):
    lane = lax.broadcasted_iota(jnp.int32, (1, LANE), 1)
    return lane >= 64 if head else lane < 64


def _sb_fwd(proj_b, carry=None):
    t = proj_b.shape[0]
    nq = t // SBQ
    scale = 1.0 / math.sqrt(64.0)

    npair = SB_PAIRS_FWD
    wide = npair * LANE
    ngrp = 4 // npair
    chains = [(p, head) for p in range(npair) for head in range(2)]

    grid = (ngrp, nq)
    nc = carry.n if carry is not None else 0

    def body(*refs):
        q_ref, k_ref, v_ref = refs[:3]
        y_ref, l_ref, n_ref = refs[3 + nc:6 + nc]
        riders = (refs[3:3 + nc], refs[6 + nc:6 + 2 * nc], refs[-2], refs[-1]) if nc else None
        if nc:
            carry.ride(grid, riders, "start")
        grp = pl.program_id(0)
        qi = pl.program_id(1)
        row, col = _sb_masks()
        m_suffix = _ones_where(row > col)
        rows = len(chains) * SBQ
        strict = (lax.broadcasted_iota(jnp.int32, (rows, SBQ), 1)
                  < (lax.broadcasted_iota(jnp.int32, (rows, SBQ), 0) & (SBQ - 1)))
        q_pair = []
        for p in range(npair):
            q_all = q_ref[:, p * LANE:(p + 1) * LANE]
            q_pair.append(jnp.concatenate([jnp.where(_head_mask(head), q_all, jnp.zeros_like(q_all)) for head in range(2)],
                                          axis=0))

        def block(kb, state, diag):
            run, acc = state
            start = pl.multiple_of(kb * SBQ, SBQ)
            z = jnp.concatenate([_dg(q_pair[p], k_ref[pl.ds(start, SBQ), p * LANE:(p + 1) * LANE], NT)
                                 for p in range(npair)], axis=0) * scale
            lb = _log_sigmoid(z)
            lk = lb - z
            if diag:
                lk = jnp.where(strict, lk, 0.0)
            hi, lo = _split(lk)
            sums = _dg(jnp.concatenate([hi, lo], axis=0), m_suffix, NN)
            w = jnp.exp(lb + (run + sums[:rows] + sums[rows:]))
            if diag:
                w = jnp.where(strict, w, 0.0)
            wb = w.astype(BF16)
            acc = acc + jnp.concatenate(
                [_dg(wb[2 * p * SBQ:2 * (p + 1) * SBQ], v_ref[pl.ds(start, SBQ), p * LANE:(p + 1) * LANE], NN)
                 for p in range(npair)], axis=0)
            run = run + jnp.sum(hi.astype(F32) + lo.astype(F32), axis=1, keepdims=True)
            return run, acc

        state = block(qi, (jnp.zeros((rows, 1), F32), jnp.zeros((rows, LANE), F32)), True)

        def live(c):
            return jnp.logical_and(c[0] < qi, jnp.max(c[1][0]) > SB_DEAD)

        def step(c):
            return c[0] + 1, block(qi - 1 - c[0], c[1], False)

        count, (run, acc) = lax.while_loop(live, step, (jnp.int32(0), state))
        n_ref[grp * nq + qi] = count.astype(F32)
        hm = _head_mask(0)
        for p in range(npair):
            lo_rows, hi_rows = slice(2 * p * SBQ, (2 * p + 1) * SBQ), slice((2 * p + 1) * SBQ, (2 * p + 2) * SBQ)
            y_ref[:, p * LANE:(p + 1) * LANE] = jnp.where(hm, acc[lo_rows], acc[hi_rows]).astype(BF16)
            l_ref[p] = jnp.where(hm, run[lo_rows], run[hi_rows])
        if nc:
            carry.ride(grid, riders, "finish")

    outs = pl.pallas_call(
        body, name="sb_fwd", grid=grid,
        in_specs=[_spec((SBQ, wide), lambda g, i: (i, g)),
                  _spec((t, wide), lambda g, i: (0, ngrp + g)),
                  _spec((t, wide), lambda g, i: (0, 2 * ngrp + g))] + (carry.in_specs if nc else []),
        out_specs=[_spec((SBQ, wide), lambda g, i: (i, g)), _spec((npair, SBQ, LANE), lambda g, i: (g, i, 0)),
                   pl.BlockSpec(memory_space=pltpu.SMEM)] + (carry.out_specs if nc else []),
        out_shape=[jax.ShapeDtypeStruct((t, 512), BF16), jax.ShapeDtypeStruct((4, t, LANE), F32),
                   jax.ShapeDtypeStruct((ngrp * nq,), F32)] + (carry.out_shape if nc else []),
        scratch_shapes=carry.scratch if nc else [],
        compiler_params=_params(("arbitrary", "arbitrary")),
    )(proj_b, proj_b, proj_b, *(carry.groups if nc else []))
    return outs[0], outs[1], outs[2], (carry.place(outs[3:]) if nc else [])


def _sb_bwd(proj_b, dy, ltot, nblk, rider=None):
    t = proj_b.shape[0]
    nq = t // SBQ
    scale = 1.0 / math.sqrt(64.0)

    npair = SB_PAIRS_BWD
    wide = npair * LANE
    ngrp = 4 // npair
    chains = [(p, head) for p in range(npair) for head in range(2)]
    grid = (ngrp, nq)
    nr = rider.n if rider is not None else 0
    per_count = SB_PAIRS_FWD // SB_PAIRS_BWD

    def body(*refs):
        q_ref, k_ref, v_ref, dy_ref, l_ref, n_ref = refs[:6]
        dq_ref, dk_ref, dv_ref = refs[6 + nr:9 + nr]
        dk_acc, dv_acc = refs[9 + 2 * nr:11 + 2 * nr]
        riders = (refs[6:6 + nr], refs[9 + nr:9 + 2 * nr], refs[-2], refs[-1]) if nr else None
        if nr:
            rider.ride(grid, riders, "start")
        grp = pl.program_id(0)
        qi = pl.program_id(1)

        @pl.when(qi == 0)
        def _():
            dk_acc[...] = jnp.zeros_like(dk_acc)
            dv_acc[...] = jnp.zeros_like(dv_acc)

        row, col = _sb_masks()
        m_prefix = _ones_where(row <= col)
        m_before = _ones_where(row < col)
        rows = len(chains) * SBQ
        strict = (lax.broadcasted_iota(jnp.int32, (rows, SBQ), 1)
                  < (lax.broadcasted_iota(jnp.int32, (rows, SBQ), 0) & (SBQ - 1)))
        q_pair, do_pair, ltot = [], [], []
        for p in range(npair):
            pl_ = slice(p * LANE, (p + 1) * LANE)
            q_all = q_ref[:, pl_]
            do_all = dy_ref[:, pl_].astype(BF16)
            q_pair.append(jnp.concatenate([jnp.where(_head_mask(h), q_all, jnp.zeros_like(q_all)) for h in range(2)], axis=0))
            do_pair.append(jnp.concatenate([jnp.where(_head_mask(h), do_all, jnp.zeros_like(do_all)) for h in range(2)], axis=0))
            ltot += [l_ref[p][:, h * 64:h * 64 + 1] for h in range(2)]
        ltot = jnp.concatenate(ltot, axis=0)

        def pair_rows(a, p):
            return a[2 * p * SBQ:2 * (p + 1) * SBQ]

        def block(kb, state, diag):
            seen, dseen, dq = state
            start = pl.multiple_of(kb * SBQ, SBQ)
            kk = [k_ref[pl.ds(start, SBQ), p * LANE:(p + 1) * LANE] for p in range(npair)]
            vv = [v_ref[pl.ds(start, SBQ), p * LANE:(p + 1) * LANE] for p in range(npair)]
            z = jnp.concatenate([_dg(q_pair[p], kk[p], NT) for p in range(npair)], axis=0) * scale
            lb = _log_sigmoid(z)
            lk = lb - z
            if diag:
                lk = jnp.where(strict, lk, 0.0)
            hi, lo = _split(lk)
            sums = _dg(jnp.concatenate([hi, lo], axis=0), m_prefix, NN)
            w = jnp.exp(lb + ((ltot - seen) - (sums[:rows] + sums[rows:])))
            if diag:
                w = jnp.where(strict, w, 0.0)
            wb = w.astype(BF16)
            da = w * jnp.concatenate([_dg(do_pair[p], vv[p], NT) for p in range(npair)], axis=0)
            dah, dal = _split(da)
            dsums = _dg(jnp.concatenate([dah, dal], axis=0), m_before, NN)
            sig = jnp.exp(lb)
            dz = (da * (1.0 - sig) - (dseen + dsums[:rows] + dsums[rows:]) * sig) * scale
            if diag:
                dz = jnp.where(strict, dz, 0.0)
            dzb = dz.astype(BF16)
            for p in range(npair):
                pl_ = slice(p * LANE, (p + 1) * LANE)
                dv_acc[pl.ds(start, SBQ), pl_] += _dg(pair_rows(wb, p), do_pair[p], TN)
                dk_acc[pl.ds(start, SBQ), pl_] += _dg(pair_rows(dzb, p), q_pair[p], TN)
            dq = dq + jnp.concatenate([_dg(pair_rows(dzb, p), kk[p], NN) for p in range(npair)], axis=0)
            seen = seen + jnp.sum(hi.astype(F32) + lo.astype(F32), axis=1, keepdims=True)
            dseen = dseen + jnp.sum(da, axis=1, keepdims=True)
            return seen, dseen, dq

        zero = (jnp.zeros((rows, 1), F32), jnp.zeros((rows, 1), F32), jnp.zeros((rows, LANE), F32))
        first = qi - n_ref[(grp // per_count) * nq + qi].astype(jnp.int32)
        state = lax.fori_loop(first, qi, lambda kb, c: block(kb, c, False), zero)
        _, _, dq = block(qi, state, True)
        for p in range(npair):
            dq_ref[:, p * LANE:(p + 1) * LANE] = jnp.where(
                _head_mask(0), dq[2 * p * SBQ:(2 * p + 1) * SBQ], dq[(2 * p + 1) * SBQ:(2 * p + 2) * SBQ]).astype(BF16)

        @pl.when(qi == nq - 1)
        def _():
            dk_ref[...] = dk_acc[...].astype(BF16)
            dv_ref[...] = dv_acc[...].astype(BF16)

        if nr:
            rider.ride(grid, riders, "finish")

    full = jax.ShapeDtypeStruct((t, 512), BF16)
    outs = pl.pallas_call(
        body, name="sb_bwd", grid=grid,
        in_specs=[_spec((SBQ, wide), lambda g, i: (i, g)),
                  _spec((t, wide), lambda g, i: (0, ngrp + g)),
                  _spec((t, wide), lambda g, i: (0, 2 * ngrp + g)),
                  _spec((SBQ, wide), lambda g, i: (i, ngrp + g)),
                  _spec((npair, SBQ, LANE), lambda g, i: (g, i, 0)),
                  pl.BlockSpec(memory_space=pltpu.SMEM)] + (rider.in_specs if nr else []),
        out_specs=[_spec((SBQ, wide), lambda g, i: (i, g)),
                   _spec((t, wide), lambda g, i: (0, g)), _spec((t, wide), lambda g, i: (0, g))]
        + (rider.out_specs if nr else []),
        out_shape=[full, full, full] + (rider.out_shape if nr else []),
        scratch_shapes=[pltpu.VMEM((t, wide), F32), pltpu.VMEM((t, wide), F32)] + (rider.scratch if nr else []),
        compiler_params=pltpu.CompilerParams(dimension_semantics=("arbitrary", "arbitrary"),
                                             vmem_limit_bytes=VMEM_LIMIT_BIG),
    )(proj_b, proj_b, proj_b, dy, ltot, nblk, *(rider.groups if nr else []))
    return outs[0], outs[1], outs[2], (rider.place(outs[3:]) if nr else [])


def _hgrn_gates(qr, fr, c0, c1):
    mx = jnp.maximum(c0, c1)
    e0, e1 = jnp.exp(c0 - mx), jnp.exp(c1 - mx)
    lb = e1 / (e0 + e1)
    sx = _sigmoid(fr)
    f = lb + (1.0 - lb) * sx
    k = (1.0 - lb) * (1.0 - sx)
    sq = _sigmoid(qr)
    return lb, sx, f, k, sq, qr * sq


def _chunk_sums(mask, x):
    n = x.shape[1]
    hi, lo = _split(x)
    both = _dg(_ones_where(mask), jnp.concatenate([hi, lo], axis=1), NN)
    return both[:, :n] + both[:, n:]


def _chunk_masks():
    row = lax.broadcasted_iota(jnp.int32, (CHUNK, CHUNK), 0)
    col = lax.broadcasted_iota(jnp.int32, (CHUNK, CHUNK), 1)
    return col <= row, col >= row


def _hgrn_fwd(proj_c, small, carry=None):
    t = proj_c.shape[1]
    nc = t // CHUNK
    nh = D // HD

    hps = HGRN_HPS
    wide = hps * HD
    grid = (nh // hps, nc)
    nr = carry.n if carry is not None else 0

    def body(*refs):
        p_ref, c0_ref, c1_ref, gam_ref = refs[:4]
        o_ref, y_ref, sst_ref = refs[4 + nr:7 + nr]
        st_ref = refs[7 + 2 * nr]
        riders = (refs[4:4 + nr], refs[7 + nr:7 + 2 * nr], refs[-2], refs[-1]) if nr else None
        if nr:
            carry.ride(grid, riders, "start")
        c = pl.program_id(1)

        @pl.when(c == 0)
        def _():
            st_ref[...] = jnp.zeros_like(st_ref)

        _, _, f_all, k_all, _, q_all = _hgrn_gates(p_ref[0], p_ref[1], c0_ref[...], c1_ref[...])
        tril, _ = _chunk_masks()
        b_all = _chunk_sums(tril, jnp.log(f_all))
        for j in range(hps):
            ln = slice(j * HD, (j + 1) * HD)
            st0 = st_ref[j]
            sst_ref[j] = st0
            v, g = p_ref[2, :, ln], p_ref[3, :, ln]
            q, k, b = q_all[:, ln], k_all[:, ln], b_all[:, ln]
            bm = b[CHUNK // 2 - 1:CHUNK // 2]
            bl = b[CHUNK - 1:CHUNK]
            qd = q * jnp.exp(b)
            qt = q * jnp.exp(b - bm)
            kt = k * jnp.exp(bm - b)
            kl = k * jnp.exp(bl - b)
            vb = v.astype(BF16)
            att = jnp.where(tril, _dot3(qt, kt, NT), 0.0)
            o = _dg(qd.astype(BF16), st0.astype(BF16), NT) + _dg(att.astype(BF16), vb, NN)
            st_ref[j] = st0 * jnp.exp(bl) + _dg(vb, kl.astype(BF16), TN)
            o_ref[:, ln] = o
            r = lax.rsqrt(jnp.mean(o * o, axis=-1, keepdims=True) + RMS_EPS)
            y_ref[:, ln] = (o * r * gam_ref[...] * (g * _sigmoid(g))).astype(BF16)
        if nr:
            carry.ride(grid, riders, "finish")

    outs = pl.pallas_call(
        body, name="hgrn_fwd", grid=grid,
        in_specs=[_spec((4, CHUNK, wide), lambda h, c: (0, c, h)),
                  _spec((None, 1, wide), lambda h, c: (R_CLB, 0, h)),
                  _spec((None, 1, wide), lambda h, c: (R_CLB + 1, 0, h)),
                  _spec((None, 1, HD), lambda h, c: (R_GAM, 0, 0))] + (carry.in_specs if nr else []),
        out_specs=[_spec((CHUNK, wide), lambda h, c: (c, h)), _spec((CHUNK, wide), lambda h, c: (c, h)),
                   _spec((None, hps, HD, HD), lambda h, c: (c, h, 0, 0))] + (carry.out_specs if nr else []),
        out_shape=[jax.ShapeDtypeStruct((t, D), F32), jax.ShapeDtypeStruct((t, D), BF16),
                   jax.ShapeDtypeStruct((nc, nh, HD, HD), F32)] + (carry.out_shape if nr else []),
        scratch_shapes=[pltpu.VMEM((hps, HD, HD), F32)] + (carry.scratch if nr else []),
        compiler_params=_params(("arbitrary", "arbitrary")),
    )(proj_c, small, small, small, *(carry.groups if nr else []))
    return outs[0], outs[1], outs[2], (carry.place(outs[3:]) if nr else [])


def _hgrn_bwd(proj_c, small, o, sst, dyc, rider=None):
    t = proj_c.shape[1]
    nc = t // CHUNK
    nh = D // HD

    hps = HGRN_HPS
    wide = hps * HD
    ng = nh // hps
    grid = (ng, nc)
    nr = rider.n if rider is not None else 0

    def body(*refs):
        p_ref, c0_ref, c1_ref, gam_ref, o_ref, sst_ref, dy_ref = refs[:7]
        dp_ref, dclb_ref, dgam_ref = refs[7 + nr:10 + nr]
        dst_ref, dlb_acc, dgam_acc = refs[10 + 2 * nr:13 + 2 * nr]
        riders = (refs[7:7 + nr], refs[10 + nr:10 + 2 * nr], refs[-2], refs[-1]) if nr else None
        if nr:
            rider.ride(grid, riders, "start")
        group = pl.program_id(0)
        step = pl.program_id(1)

        @pl.when(step == 0)
        def _():
            dst_ref[...] = jnp.zeros_like(dst_ref)
            dlb_acc[...] = jnp.zeros_like(dlb_acc)

        @pl.when((step == 0) & (group == 0))
        def _():
            dgam_acc[...] = jnp.zeros_like(dgam_acc)

        gam = gam_ref[...]
        qr_all = p_ref[0]
        lb_all, sx_all, f_all, k_all, sq_all, q_all = _hgrn_gates(qr_all, p_ref[1], c0_ref[...], c1_ref[...])
        tril, triu = _chunk_masks()
        b_all = _chunk_sums(tril, jnp.log(f_all))
        dq_parts, dk_parts, db_parts, dgam_parts = [], [], [], []
        for j in range(hps):
            ln = slice(j * HD, (j + 1) * HD)
            st0 = sst_ref[j]
            dst1 = dst_ref[j]
            v, g = p_ref[2, :, ln], p_ref[3, :, ln]
            q, k, b = q_all[:, ln], k_all[:, ln], b_all[:, ln]
            bm = b[CHUNK // 2 - 1:CHUNK // 2]
            bl = b[CHUNK - 1:CHUNK]
            eb = jnp.exp(b)
            e_qt = jnp.exp(b - bm)
            e_kt = jnp.exp(bm - b)
            e_kl = jnp.exp(bl - b)
            e_bl = jnp.exp(bl)
            qd, qt, kt, kl = q * eb, q * e_qt, k * e_kt, k * e_kl
            ov = o_ref[:, ln]
            r = lax.rsqrt(jnp.mean(ov * ov, axis=-1, keepdims=True) + RMS_EPS)
            oh = ov * r
            sg = _sigmoid(g)
            dy = dy_ref[:, ln]
            dp_ref[3, :, ln] = (dy * oh * gam * (sg * (1.0 + g * (1.0 - sg)))).astype(BF16)
            dyv = dy * (g * sg)
            gdy = dyv * gam
            do = (gdy - oh * jnp.mean(gdy * oh, axis=-1, keepdims=True)) * r
            dob, vb = do.astype(BF16), v.astype(BF16)
            st0b, dst1b = st0.astype(BF16), dst1.astype(BF16)
            st1 = st0 * e_bl + _dg(vb, kl.astype(BF16), TN)
            att = jnp.where(tril, _dot3(qt, kt, NT), 0.0)
            datt = jnp.where(tril, _dg(dob, vb, NT), 0.0)
            dv = _dg(att.astype(BF16), dob, TN) + _dg(kl.astype(BF16), dst1b, NT)
            dq = _dot3(datt, kt, NN) * e_qt + _dg(dob, st0b, NN) * eb
            dk = _dot3(datt, qt, TN) * e_kt + _dg(vb, dst1b, NN) * e_kl
            db = q * dq - k * dk
            last = lax.broadcasted_iota(jnp.int32, (CHUNK, 1), 0) == CHUNK - 1
            db = db + jnp.where(last, jnp.sum(dst1 * st1, axis=0, keepdims=True), 0.0)
            dst_ref[j] = dst1 * e_bl + _dg(dob, qd.astype(BF16), TN)
            dp_ref[2, :, ln] = dv.astype(BF16)
            dq_parts.append(dq)
            dk_parts.append(dk)
            db_parts.append(db)
            dgam_parts.append(jnp.sum(dyv * oh, axis=0, keepdims=True))

        dq_all, dk_all = jnp.concatenate(dq_parts, axis=1), jnp.concatenate(dk_parts, axis=1)
        dlf = _chunk_sums(triu, jnp.concatenate(db_parts, axis=1))
        dp_ref[0] = (dq_all * (sq_all * (1.0 + qr_all * (1.0 - sq_all)))).astype(BF16)
        tmp = dlf / f_all - dk_all
        dp_ref[1] = (tmp * (1.0 - lb_all) * sx_all * (1.0 - sx_all)).astype(BF16)
        dlb_acc[...] += jnp.sum((1.0 - sx_all) * tmp, axis=0, keepdims=True)
        dgam_acc[...] += functools.reduce(lambda a, b: a + b, dgam_parts)

        @pl.when(step == nc - 1)
        def _():
            d1 = dlb_acc[...] * lb_all * (1.0 - lb_all)
            dclb_ref[...] = jnp.where(lax.broadcasted_iota(jnp.int32, (2, wide), 0) == 0, -d1, d1)

        @pl.when((step == nc - 1) & (group == ng - 1))
        def _():
            dgam_ref[...] = dgam_acc[...]

        if nr:
            rider.ride(grid, riders, "finish")

    rev = lambda h, s: (nc - 1 - s, h)
    outs = pl.pallas_call(
        body, name="hgrn_bwd", grid=grid,
        in_specs=[_spec((4, CHUNK, wide), lambda h, s: (0, nc - 1 - s, h)),
                  _spec((None, 1, wide), lambda h, s: (R_CLB, 0, h)),
                  _spec((None, 1, wide), lambda h, s: (R_CLB + 1, 0, h)),
                  _spec((None, 1, HD), lambda h, s: (R_GAM, 0, 0)),
                  _spec((CHUNK, wide), rev),
                  _spec((None, hps, HD, HD), lambda h, s: (nc - 1 - s, h, 0, 0)),
                  _spec((CHUNK, wide), rev)] + (rider.in_specs if nr else []),
        out_specs=[_spec((4, CHUNK, wide), lambda h, s: (0, nc - 1 - s, h)),
                   _spec((2, wide), lambda h, s: (0, h)),
                   _spec((1, HD), lambda h, s: (0, 0))] + (rider.out_specs if nr else []),
        out_shape=[jax.ShapeDtypeStruct((4, t, D), BF16), jax.ShapeDtypeStruct((2, D), F32),
                   jax.ShapeDtypeStruct((1, HD), F32)] + (rider.out_shape if nr else []),
        scratch_shapes=[pltpu.VMEM((hps, HD, HD), F32), pltpu.VMEM((1, wide), F32), pltpu.VMEM((1, HD), F32)]
        + (rider.scratch if nr else []),
        compiler_params=_params(("arbitrary", "arbitrary")),
    )(proj_c, small, small, small, o, sst, dyc, *(rider.groups if nr else []))
    return outs[0], outs[1], outs[2], (rider.place(outs[3:]) if nr else [])


def _adamw(name, w, grads, m, v):
    nl = len(grads)
    rows, cols = grads[0].shape
    br = rows
    for cand in (512, 352, 256):
        if rows % cand == 0:
            br = cand
            break
    nb = rows // br
    c1 = 1.0 - ADAM_B1 ** ADAM_STEP
    c2 = 1.0 - ADAM_B2 ** ADAM_STEP

    def body(w_ref, m_ref, v_ref, *refs):
        g_refs, (d_ref, mo_ref, vo_ref, go_ref) = refs[:nl], refs[nl:]
        layer = pl.program_id(0)
        gv = g_refs[0][...]
        for k in range(1, nl):
            gv = jnp.where(layer == k, g_refs[k][...], gv)
        mn = ADAM_B1 * m_ref[...] + (1.0 - ADAM_B1) * gv
        vn = ADAM_B2 * v_ref[...] + (1.0 - ADAM_B2) * (gv * gv)
        mo_ref[...] = mn
        vo_ref[...] = vn
        go_ref[...] = gv
        d_ref[...] = -ADAM_LR * ((mn / c1) / (jnp.sqrt(vn / c2) + ADAM_EPS) + ADAM_WD * w_ref[...])

    blk = _spec((br, cols), lambda l, i: (l * nb + i, 0))
    g_specs = [_spec((br, cols), lambda l, i, k=k: (jnp.where(l == k, i, 0), 0)) for k in range(nl)]
    shape = jax.ShapeDtypeStruct((nl * rows, cols), F32)
    return pl.pallas_call(
        body, name=name, grid=(nl, nb), in_specs=[blk] * 3 + g_specs, out_specs=[blk] * 4, out_shape=[shape] * 4,
        compiler_params=_params(("arbitrary", "arbitrary")),
    )(w, m, v, *grads)


def _place():
    x, y, c = lax.axis_index("x"), lax.axis_index("y"), lax.axis_index("c")
    chips = [(1 - x, y), (x, 1 - y), (1 - x, 1 - y)]
    return x, y, c, chips


class _Rider:
    n = 0
    relay = None

    def ride(self, grid, refs, when):
        if not self.n:
            return
        ids = [pl.program_id(a) for a in range(len(grid))]
        first = functools.reduce(jnp.logical_and, [i == 0 for i in ids])
        last = functools.reduce(jnp.logical_and, [i == g - 1 for i, g in zip(ids, grid)])
        phases = [(first, self.start), (last, self.relay)] if when == "start" else [(last, self.finish)]
        for cond, phase in phases:
            if phase is not None:
                pl.when(cond)(functools.partial(phase, *refs))


class _Gather(_Rider):
    PER_GROUP = 7

    def __init__(self, groups):
        self.groups = list(groups)
        self.n = len(self.groups)
        self.in_specs = [ANY] * self.n
        self.out_specs = [ANY] * self.n
        self.out_shape = [jax.ShapeDtypeStruct((NSH,) + g.shape, g.dtype) for g in self.groups]
        sems = pltpu.SemaphoreType.DMA((self.PER_GROUP * self.n,))
        self.scratch = [sems, sems] if self.n else []

    def _copies(self, ins, outs, send, recv):
        x, y, c, chips = _place()
        sibling = (x, y, 1 - c)

        def half(gi, chip, hc):
            return outs[gi].at[2 * chip[0] + chip[1], hc]

        def copy(gi, k, src, dst, to):
            sem = self.PER_GROUP * gi + k
            return pltpu.make_async_remote_copy(src_ref=src, dst_ref=dst, send_sem=send.at[sem], recv_sem=recv.at[sem],
                                                device_id=to, device_id_type=MESH)

        pairs = [(gi, j, chip) for gi in range(self.n) for j, chip in enumerate(chips)]
        first = [copy(gi, j, ins[gi].at[c], half(gi, (x, y), c), (*chip, c)) for gi, j, chip in pairs]
        first += [copy(gi, 6, ins[gi], outs[gi].at[2 * x + y], sibling) for gi in range(self.n)]
        landed = [copy(gi, j, half(gi, chip, c), half(gi, chip, c), sibling) for gi, j, chip in pairs]
        relay = [copy(gi, 3 + j, half(gi, chip, c), half(gi, chip, c), sibling) for gi, j, chip in pairs]
        relayed = [copy(gi, 3 + j, half(gi, chip, 1 - c), half(gi, chip, 1 - c), sibling) for gi, j, chip in pairs]
        relayed += [copy(gi, 6, ins[gi], outs[gi].at[2 * x + y], sibling) for gi in range(self.n)]
        return first, landed, relay, relayed

    def start(self, ins, outs, send, recv):
        for cp in self._copies(ins, outs, send, recv)[0]:
            cp.start()

    def relay(self, ins, outs, send, recv):
        _, landed, relay, _ = self._copies(ins, outs, send, recv)
        for arrived, onward in zip(landed, relay):
            arrived.wait_recv()
            onward.start()

    def finish(self, ins, outs, send, recv):
        first, _, relay, relayed = self._copies(ins, outs, send, recv)
        for cp in relayed:
            cp.wait_recv()
        for cp in first + relay:
            cp.wait_send()

    def place(self, outs):
        return list(outs)


def _alone(name, rider):
    n = rider.n

    def body(*refs):
        parts = (refs[:n], refs[n:2 * n], refs[2 * n], refs[2 * n + 1])
        rider.start(*parts)
        if rider.relay is not None:
            rider.relay(*parts)
        rider.finish(*parts)

    outs = pl.pallas_call(
        body, name=name, in_specs=rider.in_specs, out_specs=rider.out_specs, out_shape=rider.out_shape,
        scratch_shapes=rider.scratch, compiler_params=pltpu.CompilerParams(has_side_effects=True),
    )(*rider.groups)
    return rider.place(outs)


class _Swap(_Rider):
    def __init__(self, slots):
        self.slots = list(slots)
        self.groups = [buf for buf, _, _ in self.slots]
        self.n = len(self.slots)
        self.in_specs = [ANY] * self.n
        self.out_specs = [ANY] * self.n
        self.out_shape = [jax.ShapeDtypeStruct((NSH, rows // 2, buf.shape[2]), buf.dtype) for buf, _, rows in self.slots]
        self.scratch = [pltpu.SemaphoreType.DMA((self.n,)), pltpu.SemaphoreType.DMA((self.n,))]

    def _copies(self, ins, outs, send, recv):
        x, y, c, _ = _place()
        cps = []
        for i, (_, row0, rows) in enumerate(self.slots):
            half = rows // 2
            src = ins[i].at[:, pl.ds(pl.multiple_of(row0 + (1 - c) * half, 16), half)]
            cps.append(pltpu.make_async_remote_copy(src_ref=src, dst_ref=outs[i], send_sem=send.at[i],
                                                    recv_sem=recv.at[i], device_id=(x, y, 1 - c), device_id_type=MESH))
        return cps

    def start(self, ins, outs, send, recv):
        for cp in self._copies(ins, outs, send, recv):
            cp.start()

    def finish(self, ins, outs, send, recv):
        for cp in self._copies(ins, outs, send, recv):
            cp.wait()

    def place(self, outs):
        return list(outs)


class _Share(_Rider):
    def __init__(self, arrays):
        self.groups = list(arrays)
        self.n = len(self.groups)
        self.in_specs = [ANY] * self.n
        self.out_specs = [ANY] * self.n
        self.out_shape = [jax.ShapeDtypeStruct((2,) + g.shape, g.dtype) for g in self.groups]
        self.scratch = [pltpu.SemaphoreType.DMA((self.n,)), pltpu.SemaphoreType.DMA((self.n,))]

    def _copies(self, ins, outs, send, recv, half):
        x, y, c, _ = _place()
        return [pltpu.make_async_remote_copy(src_ref=ins[gi], dst_ref=outs[gi].at[c if half == "mine" else 1 - c],
                                             send_sem=send.at[gi], recv_sem=recv.at[gi], device_id=(x, y, 1 - c),
                                             device_id_type=MESH) for gi in range(self.n)]

    def start(self, ins, outs, send, recv):
        for cp in self._copies(ins, outs, send, recv, "mine"):
            cp.start()

    def finish(self, ins, outs, send, recv):
        for cp in self._copies(ins, outs, send, recv, "theirs"):
            cp.wait_recv()
        for cp in self._copies(ins, outs, send, recv, "mine"):
            cp.wait_send()

    def place(self, outs):
        c = lax.axis_index("c")
        return [lax.dynamic_update_index_in_dim(o, g, c, 0) for o, g in zip(outs, self.groups)]


class _Send(_Rider):
    def __init__(self, arrays):
        self.groups = list(arrays)
        self.n = len(self.groups)
        self.in_specs = [ANY] * self.n
        self.out_specs = [ANY] * self.n
        self.out_shape = [jax.ShapeDtypeStruct((3,) + g.shape[1:], g.dtype) for g in self.groups]
        self.scratch = [pltpu.SemaphoreType.DMA((3 * self.n,)), pltpu.SemaphoreType.DMA((3 * self.n,))]

    def _copies(self, ins, outs, send, recv):
        x, y, c, chips = _place()
        return [pltpu.make_async_remote_copy(src_ref=ins[gi].at[2 * chip[0] + chip[1]], dst_ref=outs[gi].at[j],
                                             send_sem=send.at[3 * gi + j], recv_sem=recv.at[3 * gi + j],
                                             device_id=(*chip, c), device_id_type=MESH)
                for gi in range(self.n) for j, chip in enumerate(chips)]

    def start(self, ins, outs, send, recv):
        for cp in self._copies(ins, outs, send, recv):
            cp.start()

    def finish(self, ins, outs, send, recv):
        for cp in self._copies(ins, outs, send, recv):
            cp.wait()

    def place(self, outs):
        return list(outs)


def _pair_sum(name, slots, got, c_idx):
    n = len(slots)
    in_specs, out_specs, out_shape, operands = [], [], [], []
    for (buf, row0, rows), g in zip(slots, got):
        hb, cols = rows // 4, buf.shape[2]
        in_specs += [pl.BlockSpec((None, hb, cols), lambda q, i, cr, r=row0 // hb: (q, r + 2 * cr[0] + i, 0)),
                     pl.BlockSpec((None, hb, cols), lambda q, i, cr: (q, i, 0))]
        out_specs.append(pl.BlockSpec((None, hb, cols), lambda q, i, cr: (q, i, 0)))
        out_shape.append(jax.ShapeDtypeStruct(g.shape, BF16))
        operands += [buf, g]

    def body(c_ref, *refs):
        for i in range(n):
            refs[2 * n + i][...] = (refs[2 * i][...].astype(F32) + refs[2 * i + 1][...].astype(F32)).astype(BF16)

    return pl.pallas_call(
        body, name=name,
        grid_spec=pltpu.PrefetchScalarGridSpec(num_scalar_prefetch=1, grid=(NSH, 2), in_specs=in_specs, out_specs=out_specs),
        out_shape=out_shape, compiler_params=_params(("parallel", "parallel")),
    )(c_idx, *operands)


def _owner_sum(name, pairs, got, p_idx):
    n = len(pairs)
    in_specs, out_specs, out_shape, operands = [], [], [], []
    for own, g in zip(pairs, got):
        _, rows, cols = own.shape
        hb = rows // 2
        in_specs += [pl.BlockSpec((None, hb, cols), lambda i, pr: (pr[0], i, 0)),
                     pl.BlockSpec((3, hb, cols), lambda i, pr: (0, i, 0))]
        out_specs.append(pl.BlockSpec((hb, cols), lambda i, pr: (i, 0)))
        out_shape.append(jax.ShapeDtypeStruct((rows, cols), F32))
        operands += [own, g]

    def body(p_ref, *refs):
        for i in range(n):
            a_ref, b_ref = refs[2 * i], refs[2 * i + 1]
            refs[2 * n + i][...] = ((a_ref[...].astype(F32) + b_ref[0].astype(F32)) + b_ref[1].astype(F32)) + b_ref[2].astype(F32)

    return pl.pallas_call(
        body, name=name,
        grid_spec=pltpu.PrefetchScalarGridSpec(num_scalar_prefetch=1, grid=(2,), in_specs=in_specs, out_specs=out_specs),
        out_shape=out_shape, compiler_params=_params(("parallel",)),
    )(p_idx, *operands)


def _sum_small(slab):
    def body(in_ref, out_ref, all_ref, send, recv):
        x, y, c, _ = _place()
        me = 4 * x + 2 * y + c
        all_ref[me] = in_ref[...]
        cps = []
        for k in range(1, 8):
            peer = (x ^ (k >> 2), y ^ ((k >> 1) & 1), c ^ (k & 1))
            cps.append(pltpu.make_async_remote_copy(src_ref=in_ref, dst_ref=all_ref.at[me], send_sem=send.at[k - 1],
                                                    recv_sem=recv.at[k - 1], device_id=peer, device_id_type=MESH))
        for cp in cps:
            cp.start()
        for cp in cps:
            cp.wait()
        total = all_ref[0]
        for d in range(1, 8):
            total = total + all_ref[d]
        out_ref[...] = total

    return pl.pallas_call(
        body, name="sum_small",
        in_specs=[pl.BlockSpec(memory_space=pltpu.VMEM)], out_specs=pl.BlockSpec(memory_space=pltpu.VMEM),
        out_shape=jax.ShapeDtypeStruct(slab.shape, F32),
        scratch_shapes=[pltpu.VMEM((8,) + slab.shape, F32), pltpu.SemaphoreType.DMA((7,)), pltpu.SemaphoreType.DMA((7,))],
        compiler_params=pltpu.CompilerParams(has_side_effects=True),
    )(slab)


FFNS = ("pre0", "post0", "pre1", "post1")
W_SHAPES = dict({f + "_gu": (NSH, 2, FS, D) for f in FFNS}, **{f + "_d": (NSH, FS, D) for f in FFNS},
                ab_in=(NSH, D, 768), ab_out=(NSH, 256, D), conv=(NSH, 2, 8, LANE), c_in=(NSH, D, D), c_out=(NSH, 256, D))
CARRIED = dict(pre0_norm=("pre0_gu",), pre0_up=("pre0_d",), pre0_down=("ab_in", "conv"), ab_proj=("ab_out",),
               c_proj=("post1_d",),
               sb_fwd=("post0_gu", "post0_d"), post0_up=("pre1_gu",), post0_down=("pre1_d",),
               pre1_up=("c_in", "c_out"), hgrn_fwd=("post1_gu",))


FFN_SLOTS = dict(pre0=(0, 1, 0), pre1=(2, 3, 1), post0=(4, 5, 2), post1=(6, 7, 3))
GRAD_SLOTS = dict(ab_out=("b", B_ABOUT, 256), c_in=("b", B_CIN, D), c_out=("b", B_COUT, 256), ab_in=("c", 0, D))
for _f, (_g, _u, _d) in FFN_SLOTS.items():
    GRAD_SLOTS.update({_f + "_g": ("a", _g * FS, FS), _f + "_u": ("a", _u * FS, FS), _f + "_d": ("b", _d * FS, FS)})
REDUCE_STAGES = dict(x=("post1_g", "post1_u", "post1_d", "c_out"), w=("c_in", "pre1_d"),
                     y=("pre1_g", "pre1_u", "post0_g", "post0_u", "post0_d", "ab_out"),
                     z1=("ab_in",), z2=("pre0_g", "pre0_u", "pre0_d"))


def _local_step(x, target, weights, small, shards=None, place=None):
    t = x.shape[0]
    th = min(HALF_TILE, t)
    tw = min(WGRAD_TILE, t)
    w = dict(weights)
    reduced = {}

    def stage_slots(stage, buffers):
        return [(buffers[GRAD_SLOTS[n][0]],) + GRAD_SLOTS[n][1:] for n in REDUCE_STAGES[stage]]

    def swap_rider(stage, buffers):
        return _Swap(stage_slots(stage, buffers)) if place is not None else None

    def reduce_start(stage, buffers, swapped=None):
        if place is None:
            return [], None
        slots = stage_slots(stage, buffers)
        if swapped is None:
            swapped = _alone("grad_swap_" + stage, _Swap(slots))
        pairs = _pair_sum("grad_pair_sum_" + stage, slots, swapped, place[0])
        return pairs, _Send(pairs)

    def reduce_end(stage, pairs, landed):
        if place is None:
            return None
        mine = _owner_sum("grad_owner_sum_" + stage, pairs, landed, place[1])
        return _Share(mine)

    def shared(stage, landed):
        if place is not None:
            reduced.update(zip(REDUCE_STAGES[stage], landed))

    def carried(kernel_name):
        names = [n for n in CARRIED[kernel_name] if n not in w]
        return names, (_Gather([shards[n] for n in names]) if names else None)

    def land(names, arrays):
        for n, a in zip(names, arrays):
            w[n] = a.reshape(W_SHAPES[n])

    def ffn_forward(tag, h, hn, next_row):
        names, gather = carried(tag + "_up") if tag + "_up" in CARRIED else ([], None)
        s_up, s_gate, a, got = _ffn_up(tag + "_up", hn, w[tag + "_gu"], gather)
        land(names, got)
        names, gather = carried(tag + "_down") if tag + "_down" in CARRIED else ([], None)
        out = _ffn_down(tag + "_down", a, w[tag + "_d"], h, gather, (small, next_row) if next_row is not None else None)
        if gather is not None:
            out, got = out
            land(names, got)
        out, hn_next = out if next_row is not None else (out, None)
        return out, hn_next, (h, hn, s_up, s_gate, a)

    def out_proj(name, y, w_out, h, next_row):
        row_spec = _spec((th, D), lambda i: (i, 0))
        return _mm(name, [(y, row_spec, w_out.reshape(D, D), _spec((D, D), lambda i: (0, 0)))],
                   grid=(t // th,), o_shape=(t, D), o_dtype=F32, o_spec=row_spec, dims=NN, kaxis=0, nk=1,
                   res=(h, row_spec), norm=(small, next_row))

    def out_proj_bwd(tag, dhb, y, w_out, blk, grad_b, make_rider=None):
        grad_b = _wgrad(tag + "_dwout", y, _spec((tw, 256), lambda s, k: (k, s)), dhb, _spec((tw, D), lambda s, k: (k, 0)),
                        256, D, t, tw, grad_b, blk)
        rider = make_rider(grad_b) if make_rider is not None else None
        row_spec = _spec((th, D), lambda i: (i, 0))
        dy = _mm(tag + "_dy", [(dhb, row_spec, w_out.reshape(D, D), _spec((D, D), lambda i: (0, 0)))],
                 grid=(t // th,), o_shape=(t, D), o_dtype=F32, o_spec=row_spec, dims=NT, kaxis=0, nk=1, carry=rider)
        dy, landed = dy if rider is not None else (dy, None)
        return dy, grad_b, landed

    h0 = x
    names, gather = carried("pre0_norm")
    hn, got = _norm_fwd("pre0_norm", h0, small, R_PRE, gather)
    land(names, got)
    h1, hn_ab, pre0 = ffn_forward("pre0", h0, hn, R_MIX)
    def carrying(kernel_name, make):
        names, gather = carried(kernel_name)
        out = make(gather)
        if gather is not None:
            out, got = out
            land(names, got)
        return out

    proj_a, proj_b = carrying("ab_proj", lambda gather: _in_proj("ab_proj", hn_ab, w["ab_in"], True, gather))
    y_a = _conv_fwd(proj_a, w["conv"])
    names, gather = carried("sb_fwd")
    y_b, ltot, nblk, got = _sb_fwd(proj_b, gather)
    land(names, got)
    y_ab = jnp.concatenate([y_a, y_b], axis=1)
    h2, hn = out_proj("ab_out", y_ab, w["ab_out"], h1, R_POST)
    h3, hn, post0 = ffn_forward("post0", h2, hn, R_PRE + 1)
    h4, hn_c, pre1 = ffn_forward("pre1", h3, hn, R_MIX + 1)
    proj_c = carrying("c_proj", lambda gather: _in_proj("c_proj", hn_c, w["c_in"], False, gather))
    names, gather = carried("hgrn_fwd")
    o_c, y_c, sst, got = _hgrn_fwd(proj_c, small, gather)
    land(names, got)
    h5, hn = out_proj("c_out", y_c, w["c_out"], h4, R_POST + 1)
    h6, _, post1 = ffn_forward("post1", h5, hn, None)
    dh, dhb, d_fin, loss = _final_loss(h6, small, target)

    dh, dhb, grad_a, grad_b, dn_post1, _, _ = _ffn_backward("post1", dh, dhb, *post1, w["post1_gu"], w["post1_d"],
                                                            *FFN_SLOTS["post1"], small, R_POST + 1, 8 * FS, B_ROWS)
    dy_c, grad_b, swapped = out_proj_bwd("c", dhb, y_c, w["c_out"], B_COUT // 256, grad_b,
                                         lambda buf_b: swap_rider("x", dict(a=grad_a, b=buf_b)))
    pairs, rider = reduce_start("x", dict(a=grad_a, b=grad_b), swapped)
    dproj_c, d_clb, d_gam, landed = _hgrn_bwd(proj_c, small, o_c, sst, dy_c, rider)
    share = reduce_end("x", pairs, landed)
    grad_b = _wgrad("c_dwin", hn_c, _spec((tw, D), lambda s, k: (k, 0)), dproj_c, _spec((None, tw, D), lambda s, k: (s, k, 0)),
                    D, D, t, tw, grad_b, B_CIN // D, carry=share)
    if share is not None:
        grad_b, landed = grad_b
        shared("x", landed)
    dh, dhb, dn_mix1, _ = _dhn_norm("c_dhn", [(dproj_c, _spec((None, th, D), lambda i, s=s: (s, i, 0)),
                                               w["c_in"], _spec((None, D, D), lambda i, s=s: (s, 0, 0)))
                                              for s in range(NSH)],
                                    NT, h4, small, R_MIX + 1, dh)
    stage_w = {}

    def w_swap(buf_a, buf_b):
        return swap_rider("w", dict(a=buf_a, b=buf_b))

    def w_send(buf_a, buf_b, swapped):
        stage_w["pairs"], rider = reduce_start("w", dict(a=buf_a, b=buf_b), swapped)
        return rider

    on = place is not None
    dh, dhb, grad_a, grad_b, dn_pre1, _, landed = _ffn_backward(
        "pre1", dh, dhb, *pre1, w["pre1_gu"], w["pre1_d"], *FFN_SLOTS["pre1"], small, R_PRE + 1, grad_a, grad_b,
        w_swap if on else None, w_send if on else None)
    share = reduce_end("w", stage_w.get("pairs"), landed)
    dh, dhb, grad_a, grad_b, dn_post0, landed, _ = _ffn_backward(
        "post0", dh, dhb, *post0, w["post0_gu"], w["post0_d"], *FFN_SLOTS["post0"], small, R_POST, grad_a, grad_b,
        (lambda buf_a, buf_b: share) if on else None)
    shared("w", landed)
    dy_ab, grad_b, _ = out_proj_bwd("ab", dhb, y_ab, w["ab_out"], B_ABOUT // 256, grad_b)
    dab, dac, dax, d_conv, swapped = _conv_bwd(proj_a, w["conv"], dy_ab, swap_rider("y", dict(a=grad_a, b=grad_b)))
    pairs, rider = reduce_start("y", dict(a=grad_a, b=grad_b), swapped)
    dq, dk, dv, landed = _sb_bwd(proj_b, dy_ab, ltot, nblk, rider)
    share = reduce_end("y", pairs, landed)
    dproj_ab = jnp.concatenate([dab, dac, dax, dq, dk, dv], axis=1)
    grad_c = _wgrad("ab_dwin", hn_ab, _spec((tw, D), lambda s, k: (k, 0)), dproj_ab, _spec((tw, 768), lambda s, k: (k, s)),
                    D, 768, t, tw, D, 0, carry=share)
    if share is not None:
        grad_c, landed = grad_c
        shared("y", landed)
    dh, dhb, dn_mix0, _ = _dhn_norm("ab_dhn", [(dproj_ab, _spec((th, 768), lambda i, s=s: (i, s)),
                                                w["ab_in"], _spec((None, D, 768), lambda i, s=s: (s, 0, 0)))
                                               for s in range(NSH)],
                                    NT, h1, small, R_MIX, dh)
    last = {}

    def z1_send(buf_a, buf_b):
        last["z1"], rider = reduce_start("z1", dict(b=buf_b, c=grad_c))
        return rider

    def z2_send(buf_a, buf_b, _):
        last["z2"], rider = reduce_start("z2", dict(a=buf_a, b=buf_b))
        return rider

    dh, dhb, grad_a, grad_b, dn_pre0, landed, landed_own = _ffn_backward(
        "pre0", dh, dhb, *pre0, w["pre0_gu"], w["pre0_d"], *FFN_SLOTS["pre0"], small, R_PRE, grad_a, grad_b,
        z1_send if on else None, z2_send if on else None)
    if place is not None:
        mine = (_owner_sum("grad_owner_sum_z1", last["z1"], landed, place[1])
                + _owner_sum("grad_owner_sum_z2", last["z2"], landed_own, place[1]))
        reduced.update(zip(REDUCE_STAGES["z1"] + REDUCE_STAGES["z2"], _alone("grad_share_z", _Share(mine))))
    zero = jnp.zeros((1, D), F32)
    conv_rows = jnp.pad(jnp.transpose(d_conv[:, :3, :], (1, 0, 2)).reshape(3, 512), ((0, 0), (0, D - 512)))
    small_grad = jnp.concatenate([
        dn_pre0, dn_pre1, dn_mix0, dn_mix1, dn_post0, dn_post1, d_clb, d_fin,
        jnp.pad(d_gam, ((0, 0), (0, D - HD))), conv_rows,
        jnp.pad(loss, ((0, 0), (0, D - 1))), zero, zero], axis=0)
    return dh, grad_a, grad_b, grad_c, small_grad, reduced


def _small_slab(rows):
    parts = [jnp.pad(r.astype(F32), ((0, 0), (0, D - r.shape[1]))) for r in rows]
    slab = jnp.concatenate(parts, axis=0)
    return jnp.pad(slab, ((0, SMALL_ROWS - slab.shape[0]), (0, 0)))


def kernel(x, ffn_pre_norm, ffn_pre_w_gate, ffn_pre_w_up, ffn_pre_w_down, mix_norm, ffn_post_norm, ffn_post_w_gate, ffn_post_w_up, ffn_post_w_down, ab_w_in, ab_conv_w, ab_w_out, c_w_in, c_lower_bounds, c_out_norm, c_w_out, final_norm, loss_target, m_ffn_pre_norm, m_ffn_pre_w_gate, m_ffn_pre_w_up, m_ffn_pre_w_down, m_mix_norm, m_ffn_post_norm, m_ffn_post_w_gate, m_ffn_post_w_up, m_ffn_post_w_down, m_ab_w_in, m_ab_conv_w, m_ab_w_out, m_c_w_in, m_c_lower_bounds, m_c_out_norm, m_c_w_out, m_final_norm, v_ffn_pre_norm, v_ffn_pre_w_gate, v_ffn_pre_w_up, v_ffn_pre_w_down, v_mix_norm, v_ffn_post_norm, v_ffn_post_w_gate, v_ffn_post_w_up, v_ffn_post_w_down, v_ab_w_in, v_ab_conv_w, v_ab_w_out, v_c_w_in, v_c_lower_bounds, v_c_out_norm, v_c_w_out, v_final_norm):
    t = x.shape[1]
    xi, yi, ci = lax.axis_index("x"), lax.axis_index("y"), lax.axis_index("c")
    p_idx = (2 * xi + yi).astype(jnp.int32).reshape(1)
    c_idx = ci.astype(jnp.int32).reshape(1)

    def halves(m):
        return m.astype(BF16).reshape(2, m.shape[0] // 2, m.shape[1])

    transposed = ("ffn_pre_w_gate", "ffn_pre_w_up", "ffn_post_w_gate", "ffn_post_w_up")

    def flip(a):
        return jnp.swapaxes(a, 1, 2)

    shards = {}
    for name, (w_gate, w_up, w_down, layer) in dict(
            pre0=(ffn_pre_w_gate, ffn_pre_w_up, ffn_pre_w_down, 0), post0=(ffn_post_w_gate, ffn_post_w_up, ffn_post_w_down, 0),
            pre1=(ffn_pre_w_gate, ffn_pre_w_up, ffn_pre_w_down, 1), post1=(ffn_post_w_gate, ffn_post_w_up, ffn_post_w_down, 1)).items():
        shards[name + "_gu"] = jnp.stack([flip(w_gate)[layer], flip(w_up)[layer]]).astype(BF16)
        shards[name + "_d"] = halves(w_down[layer])
    conv_pad = jnp.pad(ab_conv_w[0], ((0, 5), (0, 0)))
    shards.update(ab_in=halves(ab_w_in[0]), ab_out=halves(ab_w_out[0]), c_in=halves(c_w_in[0]), c_out=halves(c_w_out[0]),
                  conv=jnp.stack([conv_pad, jnp.zeros_like(conv_pad)]))
    small = _small_slab([ffn_pre_norm, mix_norm, ffn_post_norm, c_lower_bounds, final_norm.reshape(1, D), c_out_norm])
    small = small.reshape(SMALL_ROWS, 1, D)

    grad_x, _, _, _, small_grad, reduced = _local_step(x[0], loss_target[0], {}, small, shards,
                                                        (c_idx, p_idx))
    whole = {n: g.reshape(2 * g.shape[1], g.shape[2]) for n, g in reduced.items()}
    small_sum = _sum_small(small_grad)

    my_conv = lax.dynamic_slice(small_sum[R_CONV:R_CONV + 3], (0, (2 * xi + yi) * 128), (3, 128))
    grads = {
        "ffn_pre_norm": small_sum[R_PRE:R_PRE + 2], "mix_norm": small_sum[R_MIX:R_MIX + 2],
        "ffn_post_norm": small_sum[R_POST:R_POST + 2], "c_lower_bounds": small_sum[R_CLB:R_CLB + 2],
        "c_out_norm": small_sum[R_GAM:R_GAM + 1, :HD], "final_norm": small_sum[R_FIN],
        "ab_conv_w": my_conv.reshape(1, 3, 128),
    }
    layers = dict(ab_w_in=[whole["ab_in"]], ab_w_out=[whole["ab_out"]], c_w_in=[whole["c_in"]], c_w_out=[whole["c_out"]])
    for kind, key in (("gate", "_g"), ("up", "_u"), ("down", "_d")):
        layers["ffn_pre_w_" + kind] = [whole["pre0" + key], whole["pre1" + key]]
        layers["ffn_post_w_" + kind] = [whole["post0" + key], whole["post1" + key]]
    weights = dict(ffn_pre_norm=ffn_pre_norm, ffn_pre_w_gate=ffn_pre_w_gate, ffn_pre_w_up=ffn_pre_w_up, ffn_pre_w_down=ffn_pre_w_down, mix_norm=mix_norm, ffn_post_norm=ffn_post_norm, ffn_post_w_gate=ffn_post_w_gate, ffn_post_w_up=ffn_post_w_up, ffn_post_w_down=ffn_post_w_down, ab_w_in=ab_w_in, ab_conv_w=ab_conv_w, ab_w_out=ab_w_out, c_w_in=c_w_in, c_lower_bounds=c_lower_bounds, c_out_norm=c_out_norm, c_w_out=c_w_out, final_norm=final_norm)
    m_in = dict(ffn_pre_norm=m_ffn_pre_norm, ffn_pre_w_gate=m_ffn_pre_w_gate, ffn_pre_w_up=m_ffn_pre_w_up, ffn_pre_w_down=m_ffn_pre_w_down, mix_norm=m_mix_norm, ffn_post_norm=m_ffn_post_norm, ffn_post_w_gate=m_ffn_post_w_gate, ffn_post_w_up=m_ffn_post_w_up, ffn_post_w_down=m_ffn_post_w_down, ab_w_in=m_ab_w_in, ab_conv_w=m_ab_conv_w, ab_w_out=m_ab_w_out, c_w_in=m_c_w_in, c_lower_bounds=m_c_lower_bounds, c_out_norm=m_c_out_norm, c_w_out=m_c_w_out, final_norm=m_final_norm)
    v_in = dict(ffn_pre_norm=v_ffn_pre_norm, ffn_pre_w_gate=v_ffn_pre_w_gate, ffn_pre_w_up=v_ffn_pre_w_up, ffn_pre_w_down=v_ffn_pre_w_down, mix_norm=v_mix_norm, ffn_post_norm=v_ffn_post_norm, ffn_post_w_gate=v_ffn_post_w_gate, ffn_post_w_up=v_ffn_post_w_up, ffn_post_w_down=v_ffn_post_w_down, ab_w_in=v_ab_w_in, ab_conv_w=v_ab_conv_w, ab_w_out=v_ab_w_out, c_w_in=v_c_w_in, c_lower_bounds=v_c_lower_bounds, c_out_norm=v_c_out_norm, c_w_out=v_c_w_out, final_norm=v_final_norm)
    names = list(weights)
    big = [n for n in names if weights[n].size >= 65536]
    tiny = [n for n in names if n not in big]

    delta, new_m, new_v = {}, {}, {}
    for n in big:
        turn = flip if n in transposed else (lambda a: a)
        shape = turn(weights[n]).shape
        two_d = (shape[0] * shape[1], shape[2])
        d, m2, v2, g2 = _adamw("adamw_" + n, turn(weights[n]).reshape(two_d), layers[n],
                               turn(m_in[n]).reshape(two_d), turn(v_in[n]).reshape(two_d))
        delta[n], new_m[n], new_v[n] = turn(d.reshape(shape)), turn(m2.reshape(shape)), turn(v2.reshape(shape))
        grads[n] = turn(g2.reshape(shape))

    def tiny_slab(src):
        return _small_slab([src[n].reshape(-1, src[n].shape[-1]) for n in tiny])

    offs, row = {}, 0
    for n in tiny:
        nrows = weights[n].size // weights[n].shape[-1]
        offs[n] = (row, nrows)
        row += nrows
    d, m2, v2, _ = _adamw("adamw_small", tiny_slab(weights), [tiny_slab(grads)], tiny_slab(m_in), tiny_slab(v_in))
    for n in tiny:
        r0, nr = offs[n]
        shape = weights[n].shape
        for dst, src in ((delta, d), (new_m, m2), (new_v, v2)):
            dst[n] = src[r0:r0 + nr, :shape[-1]].reshape(shape)

    loss = small_sum[R_LOSS, 0]
    return (loss, grad_x.reshape(1, t, D), *[grads[n] for n in names], *[delta[n] for n in names],
            *[new_m[n] for n in names], *[new_v[n] for n in names])
```

```python
import functools
import math

import jax
import jax.numpy as jnp
from jax import lax
from jax.experimental import pallas as pl
from jax.experimental.pallas import tpu as pltpu

F32 = jnp.float32
BF16 = jnp.bfloat16
MESH = pl.DeviceIdType.MESH
ANY = pl.BlockSpec(memory_space=pl.ANY)

D = 1024
FS = 704
NSH = 4
RMS_EPS = 1e-6
MACARON = 0.5
CHUNK = 64
HD = 128
HGRN_HPS = 8
SBQ = 128
SB_PAIRS_FWD = 4
SB_PAIRS_BWD = 4
SB_DEAD = -105.0
CONV_HALO = 8
LANE = 128
ROW_TILE = 1024
MM_TILE = 1024
HALF_TILE = MM_TILE // 2
WGRAD_TILE = 4096
VMEM_LIMIT = 48 * 1024 * 1024
VMEM_LIMIT_BIG = 58 * 1024 * 1024

ADAM_LR, ADAM_B1, ADAM_B2, ADAM_EPS, ADAM_WD, ADAM_STEP = 0.001, 0.9, 0.999, 1e-08, 0.01, 10

NN = ((1,), (0,))
NT = ((1,), (1,))
TN = ((0,), (0,))

SMALL_ROWS = 16
R_PRE, R_MIX, R_POST, R_CLB, R_FIN, R_GAM, R_CONV, R_LOSS = 0, 2, 4, 6, 8, 9, 10, 13

B_ABOUT = 4 * FS
B_CIN = B_ABOUT + 256
B_COUT = B_CIN + 1024
B_ROWS = B_COUT + 256


def _dg(a, b, dims):
    return lax.dot_general(a, b, (dims, ((), ())), preferred_element_type=F32)


def _split(x):
    hi = x.astype(BF16)
    lo = (x - hi.astype(F32)).astype(BF16)
    return hi, lo


def _dot3(a, b, dims):
    ah, al = _split(a)
    bh, bl = _split(b)
    if dims == TN:
        n = b.shape[1]
        both = _dg(ah, jnp.concatenate([bh, bl], axis=1), dims)
        return both[:, :n] + both[:, n:] + _dg(al, bh, dims)
    m = a.shape[0]
    both = _dg(jnp.concatenate([ah, al], axis=0), bh, dims)
    return both[:m] + both[m:] + _dg(ah, bl, dims)


def _sigmoid(x):
    return 1.0 / (1.0 + jnp.exp(-x))


def _params(sem):
    return pltpu.CompilerParams(dimension_semantics=sem, vmem_limit_bytes=VMEM_LIMIT)


def _spec(shape, imap):
    return pl.BlockSpec(shape, imap)


def _accumulate(acc_ref, pairs, dims):
    part = None
    for a_ref, b_ref in pairs:
        d = _dg(a_ref[...], b_ref[...], dims)
        part = d if part is None else part + d
    acc_ref[...] += part


def _mm(name, pairs, *, grid, o_shape, o_dtype, o_spec, dims, kaxis, nk, acc_shape=None, res=None, scale=None,
        into=None, carry=None, norm=None):
    npairs = len(pairs)
    operands, specs = [], []
    for a, a_spec, b, b_spec in pairs:
        operands += [a, b]
        specs += [a_spec, b_spec]
    if res is not None:
        operands.append(res[0])
        specs.append(res[1])
    aliases = {}
    if into is not None:
        aliases = {len(operands): 0}
        operands.append(into)
        specs.append(ANY)
    if norm is not None:
        operands.append(norm[0])
        specs.append(_spec((None, 1, D), lambda *_: (norm[1], 0, 0)))
    n_own = len(operands)
    n_out = 2 if norm is not None else 1
    nc = carry.n if carry is not None else 0
    if nc:
        operands += carry.groups
        specs += carry.in_specs

    def body(*refs):
        o_ref = refs[n_own + nc]
        riders = ((refs[n_own:n_own + nc], refs[n_own + nc + n_out:n_own + 2 * nc + n_out], refs[-2], refs[-1])
                  if nc else None)
        if nc:
            carry.ride(grid, riders, "start")

        def finish(val):
            if scale is not None:
                val = val * scale
            if res is not None:
                val = val + refs[2 * npairs][...]
            o_ref[...] = val.astype(o_dtype)
            if norm is not None:
                r = lax.rsqrt(jnp.mean(val * val, axis=-1, keepdims=True) + RMS_EPS)
                refs[n_own + nc + 1][...] = (val * r * refs[n_own - 1][...]).astype(BF16)

        if nk == 1:
            part = None
            for n in range(npairs):
                d = _dg(refs[2 * n][...], refs[2 * n + 1][...], dims)
                part = d if part is None else part + d
            finish(part)
        else:
            acc_ref = refs[n_own + 2 * nc + n_out]
            k = pl.program_id(kaxis)

            @pl.when(k == 0)
            def _():
                acc_ref[...] = jnp.zeros_like(acc_ref)

            _accumulate(acc_ref, [(refs[2 * n], refs[2 * n + 1]) for n in range(npairs)], dims)

            @pl.when(k == nk - 1)
            def _():
                finish(acc_ref[...])
        if nc:
            carry.ride(grid, riders, "finish")

    sem = tuple("arbitrary" if (nc or (ax == kaxis and nk > 1)) else "parallel" for ax in range(len(grid)))
    outs = pl.pallas_call(
        body, name=name, grid=grid, in_specs=specs,
        out_specs=[o_spec] * n_out + (carry.out_specs if nc else []),
        out_shape=[jax.ShapeDtypeStruct(o_shape, o_dtype)] + ([jax.ShapeDtypeStruct(o_shape, BF16)] if norm is not None else [])
        + (carry.out_shape if nc else []),
        scratch_shapes=([pltpu.VMEM(acc_shape, F32)] if nk > 1 else []) + (carry.scratch if nc else []),
        input_output_aliases=aliases,
        compiler_params=_params(sem),
    )(*operands)
    main = (outs[0], outs[1]) if norm is not None else outs[0]
    return (main, carry.place(outs[n_out:])) if nc else main


def _norm_fwd(name, h, gain_slab, row, rider=None):
    t = h.shape[0]
    tm = min(ROW_TILE, t)
    grid = (t // tm,)
    nr = rider.n if rider is not None else 0

    def body(*refs):
        h_ref, g_ref, o_ref = refs[0], refs[1], refs[2 + nr]
        riders = (refs[2:2 + nr], refs[3 + nr:3 + 2 * nr], refs[-2], refs[-1]) if nr else None
        if nr:
            rider.ride(grid, riders, "start")
        x = h_ref[...]
        r = lax.rsqrt(jnp.mean(x * x, axis=-1, keepdims=True) + RMS_EPS)
        o_ref[...] = (x * r * g_ref[...]).astype(BF16)
        if nr:
            rider.ride(grid, riders, "finish")

    outs = pl.pallas_call(
        body, name=name, grid=grid,
        in_specs=[_spec((tm, D), lambda i: (i, 0)), _spec((None, 1, D), lambda i: (row, 0, 0))]
        + (rider.in_specs if nr else []),
        out_specs=[_spec((tm, D), lambda i: (i, 0))] + (rider.out_specs if nr else []),
        out_shape=[jax.ShapeDtypeStruct((t, D), BF16)] + (rider.out_shape if nr else []),
        scratch_shapes=rider.scratch if nr else [],
        compiler_params=_params(("arbitrary",) if nr else ("parallel",)),
    )(h, gain_slab, *(rider.groups if nr else []))
    return outs[0], (rider.place(outs[1:]) if nr else [])


def _dhn_norm(name, pairs, dims, h, gain_slab, row, dres, rider=None):
    t = h.shape[0]
    tm = min(HALF_TILE, t)
    nt = t // tm
    grid = (nt,)
    npairs = len(pairs)
    nr = rider.n if rider is not None else 0
    operands, specs = [], []
    for a, a_spec, b, b_spec in pairs:
        operands += [a, b]
        specs += [a_spec, b_spec]
    row_spec = _spec((tm, D), lambda i: (i, 0))
    operands += [h, gain_slab, dres]
    specs += [row_spec, _spec((None, 1, D), lambda i: (row, 0, 0)), row_spec]
    n_own = len(operands)

    def body(*refs):
        h_ref, g_ref, dres_ref = refs[2 * npairs:n_own]
        dh_ref, dhb_ref, dg_ref = refs[n_own + nr:n_own + nr + 3]
        gacc_ref = refs[n_own + 2 * nr + 3]
        riders = (refs[n_own:n_own + nr], refs[n_own + nr + 3:n_own + 2 * nr + 3], refs[-2], refs[-1]) if nr else None
        if nr:
            rider.ride(grid, riders, "start")
        i = pl.program_id(0)
        dy = None
        for n in range(npairs):
            d = _dg(refs[2 * n][...], refs[2 * n + 1][...], dims)
            dy = d if dy is None else dy + d
        x = h_ref[...]
        r = lax.rsqrt(jnp.mean(x * x, axis=-1, keepdims=True) + RMS_EPS)
        xh = x * r
        gdy = dy * g_ref[...]
        dh = dres_ref[...] + (gdy - xh * jnp.mean(gdy * xh, axis=-1, keepdims=True)) * r
        dh_ref[...] = dh
        dhb_ref[...] = dh.astype(BF16)
        gpart = jnp.sum((dy * xh).reshape(tm // 8, 8, D), axis=0)

        @pl.when(i == 0)
        def _():
            gacc_ref[...] = gpart

        @pl.when(i > 0)
        def _():
            gacc_ref[...] += gpart

        @pl.when(i == nt - 1)
        def _():
            dg_ref[...] = jnp.sum(gacc_ref[...], axis=0, keepdims=True)

        if nr:
            rider.ride(grid, riders, "finish")

    outs = pl.pallas_call(
        body, name=name, grid=grid, in_specs=specs + (rider.in_specs if nr else []),
        out_specs=[row_spec, row_spec, _spec((1, D), lambda i: (0, 0))] + (rider.out_specs if nr else []),
        out_shape=[jax.ShapeDtypeStruct((t, D), F32), jax.ShapeDtypeStruct((t, D), BF16),
                   jax.ShapeDtypeStruct((1, D), F32)] + (rider.out_shape if nr else []),
        scratch_shapes=[pltpu.VMEM((8, D), F32)] + (rider.scratch if nr else []),
        compiler_params=pltpu.CompilerParams(dimension_semantics=("arbitrary",),
                                             vmem_limit_bytes=VMEM_LIMIT_BIG),
    )(*operands, *(rider.groups if nr else []))
    return outs[0], outs[1], outs[2], (rider.place(outs[3:]) if nr else [])


def _final_loss(h, gain_slab, target):
    t = h.shape[0]
    tm = min(ROW_TILE, t)
    nt = t // tm

    def body(h_ref, g_ref, t_ref, dh_ref, dhb_ref, dg_ref, loss_ref, acc_ref, lacc_ref):
        i = pl.program_id(0)
        x = h_ref[...]
        g = g_ref[...]
        r = lax.rsqrt(jnp.mean(x * x, axis=-1, keepdims=True) + RMS_EPS)
        xh = x * r
        err = xh * g - t_ref[...]
        dy = err * (1.0 / D)
        gdy = dy * g
        dh = (gdy - xh * jnp.mean(gdy * xh, axis=-1, keepdims=True)) * r
        dh_ref[...] = dh
        dhb_ref[...] = dh.astype(BF16)
        part = jnp.sum((dy * xh).reshape(tm // 8, 8, D), axis=0)
        lpart = jnp.sum((err * err).reshape(tm // 8, 8, D), axis=0)

        @pl.when(i == 0)
        def _():
            acc_ref[...] = part
            lacc_ref[...] = lpart

        @pl.when(i > 0)
        def _():
            acc_ref[...] += part
            lacc_ref[...] += lpart

        @pl.when(i == nt - 1)
        def _():
            dg_ref[...] = jnp.sum(acc_ref[...], axis=0, keepdims=True)
            rows = jnp.sum(lacc_ref[...], axis=0, keepdims=True)
            loss_ref[...] = jnp.sum(rows, axis=1, keepdims=True) * (0.5 / D)

    row_spec = _spec((tm, D), lambda i: (i, 0))
    return pl.pallas_call(
        body, name="final_loss", grid=(nt,),
        in_specs=[row_spec, _spec((None, 1, D), lambda i: (R_FIN, 0, 0)), row_spec],
        out_specs=[row_spec, row_spec, _spec((1, D), lambda i: (0, 0)), _spec((1, 1), lambda i: (0, 0))],
        out_shape=[jax.ShapeDtypeStruct((t, D), F32), jax.ShapeDtypeStruct((t, D), BF16),
                   jax.ShapeDtypeStruct((1, D), F32), jax.ShapeDtypeStruct((1, 1), F32)],
        scratch_shapes=[pltpu.VMEM((8, D), F32), pltpu.VMEM((8, D), F32)],
        compiler_params=_params(("arbitrary",)),
    )(h, gain_slab, target)


def _ffn_up(name, hn, wgu, carry=None):
    t = hn.shape[0]
    tm = min(MM_TILE, t)
    grid = (NSH, t // tm)
    nc = carry.n if carry is not None else 0

    def body(*refs):
        x_ref, wg_ref, wu_ref = refs[:3]
        s_up_ref, s_gate_ref, a_ref = refs[3 + nc:6 + nc]
        riders = (refs[3:3 + nc], refs[6 + nc:6 + 2 * nc], refs[-2], refs[-1]) if nc else None
        if nc:
            carry.ride(grid, riders, "start")
        x = x_ref[...]
        g = _dg(x, wg_ref[...], NT)
        u = _dg(x, wu_ref[...], NT)
        sg = _sigmoid(g)
        silu = g * sg
        s_up_ref[...] = (MACARON * silu).astype(BF16)
        s_gate_ref[...] = (MACARON * u * (sg * (1.0 + g * (1.0 - sg)))).astype(BF16)
        a_ref[...] = (silu * u).astype(BF16)
        if nc:
            carry.ride(grid, riders, "finish")

    act = _spec((None, tm, FS), lambda s, i: (s, i, 0))
    shape = jax.ShapeDtypeStruct((NSH, t, FS), BF16)
    outs = pl.pallas_call(
        body, name=name, grid=grid,
        in_specs=[_spec((tm, D), lambda s, i: (i, 0)),
                  _spec((None, None, FS, D), lambda s, i: (s, 0, 0, 0)),
                  _spec((None, None, FS, D), lambda s, i: (s, 1, 0, 0))] + (carry.in_specs if nc else []),
        out_specs=[act, act, act] + (carry.out_specs if nc else []),
        out_shape=[shape, shape, shape] + (carry.out_shape if nc else []),
        scratch_shapes=carry.scratch if nc else [],
        compiler_params=_params(("arbitrary", "arbitrary") if nc else ("parallel", "parallel")),
    )(hn, wgu, wgu, *(carry.groups if nc else []))
    return (outs[0], outs[1], outs[2], carry.place(outs[3:]) if nc else [])


def _ffn_down(name, a, wd, h, carry=None, norm=None):
    t = h.shape[0]
    tm = min(HALF_TILE, t)
    row_spec = _spec((tm, D), lambda i: (i, 0))
    return _mm(name, [(a, _spec((None, tm, FS), lambda i, s=s: (s, i, 0)), wd, _spec((None, FS, D), lambda i, s=s: (s, 0, 0)))
                      for s in range(NSH)],
               grid=(t // tm,), o_shape=(t, D), o_dtype=F32, o_spec=row_spec, dims=NN, kaxis=0, nk=1,
               res=(h, row_spec), scale=MACARON, carry=carry, norm=norm)


def _in_proj(name, hn, w_in, two_branches, carry=None):
    t = hn.shape[0]
    tm = min(HALF_TILE, t)
    grid = (t // tm,)
    cols = w_in.shape[2]
    nc = carry.n if carry is not None else 0
    n_out = 2 if two_branches else 1

    def body(*refs):
        x_ref, w_ref = refs[:2]
        o_refs = refs[2 + nc:2 + nc + n_out]
        riders = (refs[2:2 + nc], refs[2 + nc + n_out:2 + 2 * nc + n_out], refs[-2], refs[-1]) if nc else None
        if nc:
            carry.ride(grid, riders, "start")
        x = x_ref[...]
        if two_branches:
            for s in range(2):
                o_refs[0][:, s * cols:(s + 1) * cols] = _dg(x, w_ref[s], NN)
                o_refs[1][:, s * cols:(s + 1) * cols] = _dg(x, w_ref[s + 2], NN).astype(BF16)
        else:
            for s in range(NSH):
                o_refs[0][s] = _dg(x, w_ref[s], NN)
        if nc:
            carry.ride(grid, riders, "finish")

    if two_branches:
        out_specs = [_spec((tm, 2 * cols), lambda i: (i, 0))] * 2
        out_shape = [jax.ShapeDtypeStruct((t, 2 * cols), F32), jax.ShapeDtypeStruct((t, 2 * cols), BF16)]
    else:
        out_specs = [_spec((NSH, tm, cols), lambda i: (0, i, 0))]
        out_shape = [jax.ShapeDtypeStruct((NSH, t, cols), F32)]
    outs = pl.pallas_call(
        body, name=name, grid=grid,
        in_specs=[_spec((tm, D), lambda i: (i, 0)), _spec((NSH, D, cols), lambda i: (0, 0, 0))]
        + (carry.in_specs if nc else []),
        out_specs=out_specs + (carry.out_specs if nc else []),
        out_shape=out_shape + (carry.out_shape if nc else []),
        scratch_shapes=carry.scratch if nc else [],
        compiler_params=_params(("arbitrary",) if nc else ("parallel",)),
    )(hn, w_in, *(carry.groups if nc else []))
    main = (outs[0], outs[1]) if two_branches else outs[0]
    return (main, carry.place(outs[n_out:])) if nc else main


def _ffn_bwd_up(name, dhb, wd, s_up, s_gate, rider=None):
    t = dhb.shape[0]
    tm = min(2 * MM_TILE, t)
    grid = (t // tm, NSH)
    nr = rider.n if rider is not None else 0

    def body(*refs):
        dh_ref, wd_ref, s_up_ref, s_gate_ref = refs[:4]
        dg_ref, du_ref = refs[4 + nr:6 + nr]
        riders = (refs[4:4 + nr], refs[6 + nr:6 + 2 * nr], refs[-2], refs[-1]) if nr else None
        if nr:
            rider.ride(grid, riders, "start")
        da = _dg(dh_ref[...], wd_ref[...], NT).astype(BF16)
        du_ref[...] = da * s_up_ref[...]
        dg_ref[...] = da * s_gate_ref[...]
        if nr:
            rider.ride(grid, riders, "finish")

    act = _spec((None, tm, FS), lambda i, s: (s, i, 0))
    shape = jax.ShapeDtypeStruct((NSH, t, FS), BF16)
    outs = pl.pallas_call(
        body, name=name, grid=grid,
        in_specs=[_spec((tm, D), lambda i, s: (i, 0)), _spec((None, FS, D), lambda i, s: (s, 0, 0)), act, act]
        + (rider.in_specs if nr else []),
        out_specs=[act, act] + (rider.out_specs if nr else []),
        out_shape=[shape, shape] + (rider.out_shape if nr else []),
        scratch_shapes=rider.scratch if nr else [],
        compiler_params=pltpu.CompilerParams(
            dimension_semantics=("arbitrary", "arbitrary") if nr else ("parallel", "parallel"),
            vmem_limit_bytes=VMEM_LIMIT_BIG),
    )(dhb, wd, s_up, s_gate, *(rider.groups if nr else []))
    return outs[0], outs[1], (rider.place(outs[2:]) if nr else [])


def _wgrad(name, a, a_spec, b, b_spec, out_rows, out_cols, t, tt, group, slot, scale=None, carry=None):
    first = isinstance(group, int)
    rows = group if first else group.shape[1]
    return _mm(name, [(a, a_spec, b, b_spec)], grid=(NSH, t // tt),
               o_shape=(NSH, rows, out_cols), o_dtype=BF16,
               o_spec=_spec((None, out_rows, out_cols), lambda s, k: (s, slot, 0)),
               dims=TN, kaxis=1, nk=t // tt, acc_shape=(out_rows, out_cols), scale=scale,
               into=None if first else group, carry=carry)


def _wgrad_gate_up(name, dg, du, hn, group, gate_idx):
    t = hn.shape[0]
    first = isinstance(group, int)
    rows = group if first else group.shape[1]

    def body(dg_ref, du_ref, hn_ref, *refs):
        o_ref = refs[-1]
        x = hn_ref[...]
        o_ref[...] = jnp.concatenate([_dg(dg_ref[...], x, TN), _dg(du_ref[...], x, TN)], axis=0).astype(BF16)

    hid = _spec((None, t, FS), lambda s: (s, 0, 0))
    return pl.pallas_call(
        body, name=name, grid=(NSH,),
        in_specs=[hid, hid, _spec((t, D), lambda s: (0, 0))] + ([] if first else [ANY]),
        out_specs=_spec((None, 2 * FS, D), lambda s: (s, gate_idx // 2, 0)),
        out_shape=jax.ShapeDtypeStruct((NSH, rows, D), BF16),
        input_output_aliases={} if first else {3: 0},
        compiler_params=pltpu.CompilerParams(dimension_semantics=("parallel",), vmem_limit_bytes=VMEM_LIMIT_BIG),
    )(dg, du, hn, *([] if first else [group]))


def _ffn_backward(tag, dh, dhb, h_in, hn, s_up, s_gate, a, wgu, wd, gate_idx, up_idx, down_idx, small, norm_row,
                  grad_a, grad_b, rider_up=None, rider_dhn=None):
    t = dh.shape[0]
    tt = min(WGRAD_TILE, t)
    tok = _spec((tt, D), lambda s, k: (k, 0))
    hid = _spec((None, tt, FS), lambda s, k: (s, k, 0))
    grad_b = _wgrad(tag + "_dwd", a, hid, dhb, tok, FS, D, t, tt, grad_b, down_idx, scale=MACARON)
    rider = rider_up(grad_a, grad_b) if rider_up is not None else None
    dg, du, landed_up = _ffn_bwd_up(tag + "_bwd_up", dhb, wd, s_up, s_gate, rider)
    assert up_idx == gate_idx + 1 and gate_idx % 2 == 0
    grad_a = _wgrad_gate_up(tag + "_dwgu", dg, du, hn, grad_a, gate_idx)
    rider = rider_dhn(grad_a, grad_b, landed_up) if rider_dhn is not None else None
    th = min(HALF_TILE, t)
    pairs = []
    for s in range(NSH):
        act = _spec((None, th, FS), lambda i, s=s: (s, i, 0))
        pairs += [(dg, act, wgu, _spec((None, None, FS, D), lambda i, s=s: (s, 0, 0, 0))),
                  (du, act, wgu, _spec((None, None, FS, D), lambda i, s=s: (s, 1, 0, 0)))]
    dh_in, dhb_in, d_gain, landed_dhn = _dhn_norm(tag + "_dhn", pairs, NN, h_in, small, norm_row, dh, rider)
    return dh_in, dhb_in, grad_a, grad_b, d_gain, landed_up, landed_dhn


def _conv_fwd(proj_a, conv_w):
    t = proj_a.shape[0]
    tm = min(ROW_TILE, t)
    hb = tm // CONV_HALO

    def body(ab_ref, ac_ref, ax_ref, acp_ref, axp_ref, w_ref, y_ref):
        i = pl.program_id(1)
        u = ac_ref[...] * ax_ref[...]
        up = jnp.where(i > 0, acp_ref[...] * axp_ref[...], 0.0)
        ext = jnp.concatenate([up, u], axis=0)
        u1 = pltpu.roll(ext, 1, 0)[CONV_HALO:]
        u2 = pltpu.roll(ext, 2, 0)[CONV_HALO:]
        w = w_ref[...]
        conv = w[0:1] * u2 + w[1:2] * u1 + w[2:3] * u
        y_ref[...] = (ab_ref[...] * conv).astype(BF16)

    def cur(off):
        return _spec((tm, LANE), lambda j, i: (i, off + j))

    def prev(off):
        return _spec((CONV_HALO, LANE), lambda j, i: (jnp.maximum(i * hb - 1, 0), off + j))

    return pl.pallas_call(
        body, name="conv_fwd", grid=(4, t // tm),
        in_specs=[cur(0), cur(4), cur(8), prev(4), prev(8),
                  _spec((None, None, 8, LANE), lambda j, i: (j, 0, 0, 0))],
        out_specs=_spec((tm, LANE), lambda j, i: (i, j)),
        out_shape=jax.ShapeDtypeStruct((t, 512), BF16),
        compiler_params=_params(("parallel", "parallel")),
    )(proj_a, proj_a, proj_a, proj_a, proj_a, conv_w)


def _conv_bwd(proj_a, conv_w, dy, rider=None):
    t = proj_a.shape[0]
    tm = min(ROW_TILE, t)
    hb = tm // CONV_HALO
    nt = t // tm
    grid = (4, nt)
    nr = rider.n if rider is not None else 0

    def body(*refs):
        ab_ref, ac_ref, ax_ref, dy_ref, acp_ref, axp_ref, abn_ref, dyn_ref, w_ref = refs[:9]
        dab_ref, dac_ref, dax_ref, dw_ref = refs[9 + nr:13 + nr]
        acc_ref = refs[13 + 2 * nr]
        riders = (refs[9:9 + nr], refs[13 + nr:13 + 2 * nr], refs[-2], refs[-1]) if nr else None
        if nr:
            rider.ride(grid, riders, "start")
        i = pl.program_id(1)
        ab, ac, ax = ab_ref[...], ac_ref[...], ax_ref[...]
        u = ac * ax
        up = jnp.where(i > 0, acp_ref[...] * axp_ref[...], 0.0)
        ext = jnp.concatenate([up, u], axis=0)
        u1 = pltpu.roll(ext, 1, 0)[CONV_HALO:]
        u2 = pltpu.roll(ext, 2, 0)[CONV_HALO:]
        w = w_ref[...]
        conv = w[0:1] * u2 + w[1:2] * u1 + w[2:3] * u
        dy_v = dy_ref[...]
        dab_ref[...] = (dy_v * conv).astype(BF16)
        dc = dy_v * ab
        dcn = jnp.where(i < nt - 1, dyn_ref[...] * abn_ref[...], 0.0)
        extn = jnp.concatenate([dc, dcn], axis=0)
        n = tm + CONV_HALO
        dc1 = pltpu.roll(extn, n - 1, 0)[:tm]
        dc2 = pltpu.roll(extn, n - 2, 0)[:tm]
        du = w[2:3] * dc + w[1:2] * dc1 + w[0:1] * dc2
        dac_ref[...] = (du * ax).astype(BF16)
        dax_ref[...] = (du * ac).astype(BF16)
        rid = lax.broadcasted_iota(jnp.int32, (8, LANE), 0)
        part = jnp.where(rid == 0, jnp.sum(dc * u2, axis=0, keepdims=True),
                         jnp.where(rid == 1, jnp.sum(dc * u1, axis=0, keepdims=True),
                                   jnp.where(rid == 2, jnp.sum(dc * u, axis=0, keepdims=True), 0.0)))

        @pl.when(i == 0)
        def _():
            acc_ref[...] = part

        @pl.when(i > 0)
        def _():
            acc_ref[...] += part

        @pl.when(i == nt - 1)
        def _():
            dw_ref[...] = acc_ref[...]

        if nr:
            rider.ride(grid, riders, "finish")

    def cur(off):
        return _spec((tm, LANE), lambda j, i: (i, off + j))

    def prev(off):
        return _spec((CONV_HALO, LANE), lambda j, i: (jnp.maximum(i * hb - 1, 0), off + j))

    def nxt(off):
        return _spec((CONV_HALO, LANE), lambda j, i: (jnp.minimum((i + 1) * hb, nt * hb - 1), off + j))

    outs = pl.pallas_call(
        body, name="conv_bwd", grid=grid,
        in_specs=[cur(0), cur(4), cur(8), cur(0), prev(4), prev(8), nxt(0), nxt(0),
                  _spec((None, None, 8, LANE), lambda j, i: (j, 0, 0, 0))] + (rider.in_specs if nr else []),
        out_specs=[_spec((tm, LANE), lambda j, i: (i, j)), _spec((tm, LANE), lambda j, i: (i, j)),
                   _spec((tm, LANE), lambda j, i: (i, j)), _spec((None, 8, LANE), lambda j, i: (j, 0, 0))]
        + (rider.out_specs if nr else []),
        out_shape=[jax.ShapeDtypeStruct((t, 512), BF16), jax.ShapeDtypeStruct((t, 512), BF16),
                   jax.ShapeDtypeStruct((t, 512), BF16), jax.ShapeDtypeStruct((4, 8, LANE), F32)]
        + (rider.out_shape if nr else []),
        scratch_shapes=[pltpu.VMEM((8, LANE), F32)] + (rider.scratch if nr else []),
        compiler_params=_params(("arbitrary", "arbitrary") if nr else ("parallel", "arbitrary")),
    )(proj_a, proj_a, proj_a, dy, proj_a, proj_a, proj_a, dy, conv_w, *(rider.groups if nr else []))
    return outs[0], outs[1], outs[2], outs[3], (rider.place(outs[4:]) if nr else [])


def _log_sigmoid(z):
    return jnp.minimum(z, 0.0) - jnp.log(1.0 + jnp.exp(-jnp.abs(z)))


def _sb_masks():
    row = lax.broadcasted_iota(jnp.int32, (SBQ, SBQ), 0)
    col = lax.broadcasted_iota(jnp.int32, (SBQ, SBQ), 1)
    return row, col


def _ones_where(mask):
    return jnp.where(mask, 1.0, 0.0).astype(BF16)


def _head_mask(head):
    lane = lax.broadcasted_iota(jnp.int32, (1, LANE), 1)
    return lane >= 64 if head else lane < 64


def _sb_fwd(proj_b, carry=None):
    t = proj_b.shape[0]
    nq = t // SBQ
    scale = 1.0 / math.sqrt(64.0)

    npair = SB_PAIRS_FWD
    wide = npair * LANE
    ngrp = 4 // npair
    chains = [(p, head) for p in range(npair) for head in range(2)]

    grid = (ngrp, nq)
    nc = carry.n if carry is not None else 0

    def body(*refs):
        q_ref, k_ref, v_ref = refs[:3]
        y_ref, l_ref, n_ref = refs[3 + nc:6 + nc]
        riders = (refs[3:3 + nc], refs[6 + nc:6 + 2 * nc], refs[-2], refs[-1]) if nc else None
        if nc:
            carry.ride(grid, riders, "start")
        grp = pl.program_id(0)
        qi = pl.program_id(1)
        row, col = _sb_masks()
        m_suffix = _ones_where(row > col)
        rows = len(chains) * SBQ
        strict = (lax.broadcasted_iota(jnp.int32, (rows, SBQ), 1)
                  < (lax.broadcasted_iota(jnp.int32, (rows, SBQ), 0) & (SBQ - 1)))
        q_pair = []
        for p in range(npair):
            q_all = q_ref[:, p * LANE:(p + 1) * LANE]
            q_pair.append(jnp.concatenate([jnp.where(_head_mask(head), q_all, jnp.zeros_like(q_all)) for head in range(2)],
                                          axis=0))

        def block(kb, state, diag):
            run, acc = state
            start = pl.multiple_of(kb * SBQ, SBQ)
            z = jnp.concatenate([_dg(q_pair[p], k_ref[pl.ds(start, SBQ), p * LANE:(p + 1) * LANE], NT)
                                 for p in range(npair)], axis=0) * scale
            lb = _log_sigmoid(z)
            lk = lb - z
            if diag:
                lk = jnp.where(strict, lk, 0.0)
            hi, lo = _split(lk)
            sums = _dg(jnp.concatenate([hi, lo], axis=0), m_suffix, NN)
            w = jnp.exp(lb + (run + sums[:rows] + sums[rows:]))
            if diag:
                w = jnp.where(strict, w, 0.0)
            wb = w.astype(BF16)
            acc = acc + jnp.concatenate(
                [_dg(wb[2 * p * SBQ:2 * (p + 1) * SBQ], v_ref[pl.ds(start, SBQ), p * LANE:(p + 1) * LANE], NN)
                 for p in range(npair)], axis=0)
            run = run + jnp.sum(hi.astype(F32) + lo.astype(F32), axis=1, keepdims=True)
            return run, acc

        state = block(qi, (jnp.zeros((rows, 1), F32), jnp.zeros((rows, LANE), F32)), True)

        def live(c):
            return jnp.logical_and(c[0] < qi, jnp.max(c[1][0]) > SB_DEAD)

        def step(c):
            return c[0] + 1, block(qi - 1 - c[0], c[1], False)

        count, (run, acc) = lax.while_loop(live, step, (jnp.int32(0), state))
        n_ref[grp * nq + qi] = count.astype(F32)
        hm = _head_mask(0)
        for p in range(npair):
            lo_rows, hi_rows = slice(2 * p * SBQ, (2 * p + 1) * SBQ), slice((2 * p + 1) * SBQ, (2 * p + 2) * SBQ)
            y_ref[:, p * LANE:(p + 1) * LANE] = jnp.where(hm, acc[lo_rows], acc[hi_rows]).astype(BF16)
            l_ref[p] = jnp.where(hm, run[lo_rows], run[hi_rows])
        if nc:
            carry.ride(grid, riders, "finish")

    outs = pl.pallas_call(
        body, name="sb_fwd", grid=grid,
        in_specs=[_spec((SBQ, wide), lambda g, i: (i, g)),
                  _spec((t, wide), lambda g, i: (0, ngrp + g)),
                  _spec((t, wide), lambda g, i: (0, 2 * ngrp + g))] + (carry.in_specs if nc else []),
        out_specs=[_spec((SBQ, wide), lambda g, i: (i, g)), _spec((npair, SBQ, LANE), lambda g, i: (g, i, 0)),
                   pl.BlockSpec(memory_space=pltpu.SMEM)] + (carry.out_specs if nc else []),
        out_shape=[jax.ShapeDtypeStruct((t, 512), BF16), jax.ShapeDtypeStruct((4, t, LANE), F32),
                   jax.ShapeDtypeStruct((ngrp * nq,), F32)] + (carry.out_shape if nc else []),
        scratch_shapes=carry.scratch if nc else [],
        compiler_params=_params(("arbitrary", "arbitrary")),
    )(proj_b, proj_b, proj_b, *(carry.groups if nc else []))
    return outs[0], outs[1], outs[2], (carry.place(outs[3:]) if nc else [])


def _sb_bwd(proj_b, dy, ltot, nblk, rider=None):
    t = proj_b.shape[0]
    nq = t // SBQ
    scale = 1.0 / math.sqrt(64.0)

    npair = SB_PAIRS_BWD
    wide = npair * LANE
    ngrp = 4 // npair
    chains = [(p, head) for p in range(npair) for head in range(2)]
    grid = (ngrp, nq)
    nr = rider.n if rider is not None else 0
    per_count = SB_PAIRS_FWD // SB_PAIRS_BWD

    def body(*refs):
        q_ref, k_ref, v_ref, dy_ref, l_ref, n_ref = refs[:6]
        dq_ref, dk_ref, dv_ref = refs[6 + nr:9 + nr]
        dk_acc, dv_acc = refs[9 + 2 * nr:11 + 2 * nr]
        riders = (refs[6:6 + nr], refs[9 + nr:9 + 2 * nr], refs[-2], refs[-1]) if nr else None
        if nr:
            rider.ride(grid, riders, "start")
        grp = pl.program_id(0)
        qi = pl.program_id(1)

        @pl.when(qi == 0)
        def _():
            dk_acc[...] = jnp.zeros_like(dk_acc)
            dv_acc[...] = jnp.zeros_like(dv_acc)

        row, col = _sb_masks()
        m_prefix = _ones_where(row <= col)
        m_before = _ones_where(row < col)
        rows = len(chains) * SBQ
        strict = (lax.broadcasted_iota(jnp.int32, (rows, SBQ), 1)
                  < (lax.broadcasted_iota(jnp.int32, (rows, SBQ), 0) & (SBQ - 1)))
        q_pair, do_pair, ltot = [], [], []
        for p in range(npair):
            pl_ = slice(p * LANE, (p + 1) * LANE)
            q_all = q_ref[:, pl_]
            do_all = dy_ref[:, pl_].astype(BF16)
            q_pair.append(jnp.concatenate([jnp.where(_head_mask(h), q_all, jnp.zeros_like(q_all)) for h in range(2)], axis=0))
            do_pair.append(jnp.concatenate([jnp.where(_head_mask(h), do_all, jnp.zeros_like(do_all)) for h in range(2)], axis=0))
            ltot += [l_ref[p][:, h * 64:h * 64 + 1] for h in range(2)]
        ltot = jnp.concatenate(ltot, axis=0)

        def pair_rows(a, p):
            return a[2 * p * SBQ:2 * (p + 1) * SBQ]

        def block(kb, state, diag):
            seen, dseen, dq = state
            start = pl.multiple_of(kb * SBQ, SBQ)
            kk = [k_ref[pl.ds(start, SBQ), p * LANE:(p + 1) * LANE] for p in range(npair)]
            vv = [v_ref[pl.ds(start, SBQ), p * LANE:(p + 1) * LANE] for p in range(npair)]
            z = jnp.concatenate([_dg(q_pair[p], kk[p], NT) for p in range(npair)], axis=0) * scale
            lb = _log_sigmoid(z)
            lk = lb - z
            if diag:
                lk = jnp.where(strict, lk, 0.0)
            hi, lo = _split(lk)
            sums = _dg(jnp.concatenate([hi, lo], axis=0), m_prefix, NN)
            w = jnp.exp(lb + ((ltot - seen) - (sums[:rows] + sums[rows:])))
            if diag:
                w = jnp.where(strict, w, 0.0)
            wb = w.astype(BF16)
            da = w * jnp.concatenate([_dg(do_pair[p], vv[p], NT) for p in range(npair)], axis=0)
            dah, dal = _split(da)
            dsums = _dg(jnp.concatenate([dah, dal], axis=0), m_before, NN)
            sig = jnp.exp(lb)
            dz = (da * (1.0 - sig) - (dseen + dsums[:rows] + dsums[rows:]) * sig) * scale
            if diag:
                dz = jnp.where(strict, dz, 0.0)
            dzb = dz.astype(BF16)
            for p in range(npair):
                pl_ = slice(p * LANE, (p + 1) * LANE)
                dv_acc[pl.ds(start, SBQ), pl_] += _dg(pair_rows(wb, p), do_pair[p], TN)
                dk_acc[pl.ds(start, SBQ), pl_] += _dg(pair_rows(dzb, p), q_pair[p], TN)
            dq = dq + jnp.concatenate([_dg(pair_rows(dzb, p), kk[p], NN) for p in range(npair)], axis=0)
            seen = seen + jnp.sum(hi.astype(F32) + lo.astype(F32), axis=1, keepdims=True)
            dseen = dseen + jnp.sum(da, axis=1, keepdims=True)
            return seen, dseen, dq

        zero = (jnp.zeros((rows, 1), F32), jnp.zeros((rows, 1), F32), jnp.zeros((rows, LANE), F32))
        first = qi - n_ref[(grp // per_count) * nq + qi].astype(jnp.int32)
        state = lax.fori_loop(first, qi, lambda kb, c: block(kb, c, False), zero)
        _, _, dq = block(qi, state, True)
        for p in range(npair):
            dq_ref[:, p * LANE:(p + 1) * LANE] = jnp.where(
                _head_mask(0), dq[2 * p * SBQ:(2 * p + 1) * SBQ], dq[(2 * p + 1) * SBQ:(2 * p + 2) * SBQ]).astype(BF16)

        @pl.when(qi == nq - 1)
        def _():
            dk_ref[...] = dk_acc[...].astype(BF16)
            dv_ref[...] = dv_acc[...].astype(BF16)

        if nr:
            rider.ride(grid, riders, "finish")

    full = jax.ShapeDtypeStruct((t, 512), BF16)
    outs = pl.pallas_call(
        body, name="sb_bwd", grid=grid,
        in_specs=[_spec((SBQ, wide), lambda g, i: (i, g)),
                  _spec((t, wide), lambda g, i: (0, ngrp + g)),
                  _spec((t, wide), lambda g, i: (0, 2 * ngrp + g)),
                  _spec((SBQ, wide), lambda g, i: (i, ngrp + g)),
                  _spec((npair, SBQ, LANE), lambda g, i: (g, i, 0)),
                  pl.BlockSpec(memory_space=pltpu.SMEM)] + (rider.in_specs if nr else []),
        out_specs=[_spec((SBQ, wide), lambda g, i: (i, g)),
                   _spec((t, wide), lambda g, i: (0, g)), _spec((t, wide), lambda g, i: (0, g))]
        + (rider.out_specs if nr else []),
        out_shape=[full, full, full] + (rider.out_shape if nr else []),
        scratch_shapes=[pltpu.VMEM((t, wide), F32), pltpu.VMEM((t, wide), F32)] + (rider.scratch if nr else []),
        compiler_params=pltpu.CompilerParams(dimension_semantics=("arbitrary", "arbitrary"),
                                             vmem_limit_bytes=VMEM_LIMIT_BIG),
    )(proj_b, proj_b, proj_b, dy, ltot, nblk, *(rider.groups if nr else []))
    return outs[0], outs[1], outs[2], (rider.place(outs[3:]) if nr else [])


def _hgrn_gates(qr, fr, c0, c1):
    mx = jnp.maximum(c0, c1)
    e0, e1 = jnp.exp(c0 - mx), jnp.exp(c1 - mx)
    lb = e1 / (e0 + e1)
    sx = _sigmoid(fr)
    f = lb + (1.0 - lb) * sx
    k = (1.0 - lb) * (1.0 - sx)
    sq = _sigmoid(qr)
    return lb, sx, f, k, sq, qr * sq


def _chunk_sums(mask, x):
    n = x.shape[1]
    hi, lo = _split(x)
    both = _dg(_ones_where(mask), jnp.concatenate([hi, lo], axis=1), NN)
    return both[:, :n] + both[:, n:]


def _chunk_masks():
    row = lax.broadcasted_iota(jnp.int32, (CHUNK, CHUNK), 0)
    col = lax.broadcasted_iota(jnp.int32, (CHUNK, CHUNK), 1)
    return col <= row, col >= row


def _hgrn_fwd(proj_c, small, carry=None):
    t = proj_c.shape[1]
    nc = t // CHUNK
    nh = D // HD

    hps = HGRN_HPS
    wide = hps * HD
    grid = (nh // hps, nc)
    nr = carry.n if carry is not None else 0

    def body(*refs):
        p_ref, c0_ref, c1_ref, gam_ref = refs[:4]
        o_ref, y_ref, sst_ref = refs[4 + nr:7 + nr]
        st_ref = refs[7 + 2 * nr]
        riders = (refs[4:4 + nr], refs[7 + nr:7 + 2 * nr], refs[-2], refs[-1]) if nr else None
        if nr:
            carry.ride(grid, riders, "start")
        c = pl.program_id(1)

        @pl.when(c == 0)
        def _():
            st_ref[...] = jnp.zeros_like(st_ref)

        _, _, f_all, k_all, _, q_all = _hgrn_gates(p_ref[0], p_ref[1], c0_ref[...], c1_ref[...])
        tril, _ = _chunk_masks()
        b_all = _chunk_sums(tril, jnp.log(f_all))
        for j in range(hps):
            ln = slice(j * HD, (j + 1) * HD)
            st0 = st_ref[j]
            sst_ref[j] = st0
            v, g = p_ref[2, :, ln], p_ref[3, :, ln]
            q, k, b = q_all[:, ln], k_all[:, ln], b_all[:, ln]
            bm = b[CHUNK // 2 - 1:CHUNK // 2]
            bl = b[CHUNK - 1:CHUNK]
            qd = q * jnp.exp(b)
            qt = q * jnp.exp(b - bm)
            kt = k * jnp.exp(bm - b)
            kl = k * jnp.exp(bl - b)
            vb = v.astype(BF16)
            att = jnp.where(tril, _dot3(qt, kt, NT), 0.0)
            o = _dg(qd.astype(BF16), st0.astype(BF16), NT) + _dg(att.astype(BF16), vb, NN)
            st_ref[j] = st0 * jnp.exp(bl) + _dg(vb, kl.astype(BF16), TN)
            o_ref[:, ln] = o
            r = lax.rsqrt(jnp.mean(o * o, axis=-1, keepdims=True) + RMS_EPS)
            y_ref[:, ln] = (o * r * gam_ref[...] * (g * _sigmoid(g))).astype(BF16)
        if nr:
            carry.ride(grid, riders, "finish")

    outs = pl.pallas_call(
        body, name="hgrn_fwd", grid=grid,
        in_specs=[_spec((4, CHUNK, wide), lambda h, c: (0, c, h)),
                  _spec((None, 1, wide), lambda h, c: (R_CLB, 0, h)),
                  _spec((None, 1, wide), lambda h, c: (R_CLB + 1, 0, h)),
                  _spec((None, 1, HD), lambda h, c: (R_GAM, 0, 0))] + (carry.in_specs if nr else []),
        out_specs=[_spec((CHUNK, wide), lambda h, c: (c, h)), _spec((CHUNK, wide), lambda h, c: (c, h)),
                   _spec((None, hps, HD, HD), lambda h, c: (c, h, 0, 0))] + (carry.out_specs if nr else []),
        out_shape=[jax.ShapeDtypeStruct((t, D), F32), jax.ShapeDtypeStruct((t, D), BF16),
                   jax.ShapeDtypeStruct((nc, nh, HD, HD), F32)] + (carry.out_shape if nr else []),
        scratch_shapes=[pltpu.VMEM((hps, HD, HD), F32)] + (carry.scratch if nr else []),
        compiler_params=_params(("arbitrary", "arbitrary")),
    )(proj_c, small, small, small, *(carry.groups if nr else []))
    return outs[0], outs[1], outs[2], (carry.place(outs[3:]) if nr else [])


def _hgrn_bwd(proj_c, small, o, sst, dyc, rider=None):
    t = proj_c.shape[1]
    nc = t // CHUNK
    nh = D // HD

    hps = HGRN_HPS
    wide = hps * HD
    ng = nh // hps
    grid = (ng, nc)
    nr = rider.n if rider is not None else 0

    def body(*refs):
        p_ref, c0_ref, c1_ref, gam_ref, o_ref, sst_ref, dy_ref = refs[:7]
        dp_ref, dclb_ref, dgam_ref = refs[7 + nr:10 + nr]
        dst_ref, dlb_acc, dgam_acc = refs[10 + 2 * nr:13 + 2 * nr]
        riders = (refs[7:7 + nr], refs[10 + nr:10 + 2 * nr], refs[-2], refs[-1]) if nr else None
        if nr:
            rider.ride(grid, riders, "start")
        group = pl.program_id(0)
        step = pl.program_id(1)

        @pl.when(step == 0)
        def _():
            dst_ref[...] = jnp.zeros_like(dst_ref)
            dlb_acc[...] = jnp.zeros_like(dlb_acc)

        @pl.when((step == 0) & (group == 0))
        def _():
            dgam_acc[...] = jnp.zeros_like(dgam_acc)

        gam = gam_ref[...]
        qr_all = p_ref[0]
        lb_all, sx_all, f_all, k_all, sq_all, q_all = _hgrn_gates(qr_all, p_ref[1], c0_ref[...], c1_ref[...])
        tril, triu = _chunk_masks()
        b_all = _chunk_sums(tril, jnp.log(f_all))
        dq_parts, dk_parts, db_parts, dgam_parts = [], [], [], []
        for j in range(hps):
            ln = slice(j * HD, (j + 1) * HD)
            st0 = sst_ref[j]
            dst1 = dst_ref[j]
            v, g = p_ref[2, :, ln], p_ref[3, :, ln]
            q, k, b = q_all[:, ln], k_all[:, ln], b_all[:, ln]
            bm = b[CHUNK // 2 - 1:CHUNK // 2]
            bl = b[CHUNK - 1:CHUNK]
            eb = jnp.exp(b)
            e_qt = jnp.exp(b - bm)
            e_kt = jnp.exp(bm - b)
            e_kl = jnp.exp(bl - b)
            e_bl = jnp.exp(bl)
            qd, qt, kt, kl = q * eb, q * e_qt, k * e_kt, k * e_kl
            ov = o_ref[:, ln]
            r = lax.rsqrt(jnp.mean(ov * ov, axis=-1, keepdims=True) + RMS_EPS)
            oh = ov * r
            sg = _sigmoid(g)
            dy = dy_ref[:, ln]
            dp_ref[3, :, ln] = (dy * oh * gam * (sg * (1.0 + g * (1.0 - sg)))).astype(BF16)
            dyv = dy * (g * sg)
            gdy = dyv * gam
            do = (gdy - oh * jnp.mean(gdy * oh, axis=-1, keepdims=True)) * r
            dob, vb = do.astype(BF16), v.astype(BF16)
            st0b, dst1b = st0.astype(BF16), dst1.astype(BF16)
            st1 = st0 * e_bl + _dg(vb, kl.astype(BF16), TN)
            att = jnp.where(tril, _dot3(qt, kt, NT), 0.0)
            datt = jnp.where(tril, _dg(dob, vb, NT), 0.0)
            dv = _dg(att.astype(BF16), dob, TN) + _dg(kl.astype(BF16), dst1b, NT)
            dq = _dot3(datt, kt, NN) * e_qt + _dg(dob, st0b, NN) * eb
            dk = _dot3(datt, qt, TN) * e_kt + _dg(vb, dst1b, NN) * e_kl
            db = q * dq - k * dk
            last = lax.broadcasted_iota(jnp.int32, (CHUNK, 1), 0) == CHUNK - 1
            db = db + jnp.where(last, jnp.sum(dst1 * st1, axis=0, keepdims=True), 0.0)
            dst_ref[j] = dst1 * e_bl + _dg(dob, qd.astype(BF16), TN)
            dp_ref[2, :, ln] = dv.astype(BF16)
            dq_parts.append(dq)
            dk_parts.append(dk)
            db_parts.append(db)
            dgam_parts.append(jnp.sum(dyv * oh, axis=0, keepdims=True))

        dq_all, dk_all = jnp.concatenate(dq_parts, axis=1), jnp.concatenate(dk_parts, axis=1)
        dlf = _chunk_sums(triu, jnp.concatenate(db_parts, axis=1))
        dp_ref[0] = (dq_all * (sq_all * (1.0 + qr_all * (1.0 - sq_all)))).astype(BF16)
        tmp = dlf / f_all - dk_all
        dp_ref[1] = (tmp * (1.0 - lb_all) * sx_all * (1.0 - sx_all)).astype(BF16)
        dlb_acc[...] += jnp.sum((1.0 - sx_all) * tmp, axis=0, keepdims=True)
        dgam_acc[...] += functools.reduce(lambda a, b: a + b, dgam_parts)

        @pl.when(step == nc - 1)
        def _():
            d1 = dlb_acc[...] * lb_all * (1.0 - lb_all)
            dclb_ref[...] = jnp.where(lax.broadcasted_iota(jnp.int32, (2, wide), 0) == 0, -d1, d1)

        @pl.when((step == nc - 1) & (group == ng - 1))
        def _():
            dgam_ref[...] = dgam_acc[...]

        if nr:
            rider.ride(grid, riders, "finish")

    rev = lambda h, s: (nc - 1 - s, h)
    outs = pl.pallas_call(
        body, name="hgrn_bwd", grid=grid,
        in_specs=[_spec((4, CHUNK, wide), lambda h, s: (0, nc - 1 - s, h)),
                  _spec((None, 1, wide), lambda h, s: (R_CLB, 0, h)),
                  _spec((None, 1, wide), lambda h, s: (R_CLB + 1, 0, h)),
                  _spec((None, 1, HD), lambda h, s: (R_GAM, 0, 0)),
                  _spec((CHUNK, wide), rev),
                  _spec((None, hps, HD, HD), lambda h, s: (nc - 1 - s, h, 0, 0)),
                  _spec((CHUNK, wide), rev)] + (rider.in_specs if nr else []),
        out_specs=[_spec((4, CHUNK, wide), lambda h, s: (0, nc - 1 - s, h)),
                   _spec((2, wide), lambda h, s: (0, h)),
                   _spec((1, HD), lambda h, s: (0, 0))] + (rider.out_specs if nr else []),
        out_shape=[jax.ShapeDtypeStruct((4, t, D), BF16), jax.ShapeDtypeStruct((2, D), F32),
                   jax.ShapeDtypeStruct((1, HD), F32)] + (rider.out_shape if nr else []),
        scratch_shapes=[pltpu.VMEM((hps, HD, HD), F32), pltpu.VMEM((1, wide), F32), pltpu.VMEM((1, HD), F32)]
        + (rider.scratch if nr else []),
        compiler_params=_params(("arbitrary", "arbitrary")),
    )(proj_c, small, small, small, o, sst, dyc, *(rider.groups if nr else []))
    return outs[0], outs[1], outs[2], (rider.place(outs[3:]) if nr else [])


def _adamw(name, w, grads, m, v):
    nl = len(grads)
    rows, cols = grads[0].shape
    br = rows
    for cand in (256, 176, 128):
        if rows % cand == 0:
            br = cand
            break
    nb = rows // br
    c1 = 1.0 - ADAM_B1 ** ADAM_STEP
    c2 = 1.0 - ADAM_B2 ** ADAM_STEP

    def body(w_ref, m_ref, v_ref, *refs):
        g_refs, (d_ref, mo_ref, vo_ref, go_ref) = refs[:nl], refs[nl:]
        layer = pl.program_id(0)
        gv = g_refs[0][...]
        for k in range(1, nl):
            gv = jnp.where(layer == k, g_refs[k][...], gv)
        mn = ADAM_B1 * m_ref[...] + (1.0 - ADAM_B1) * gv
        vn = ADAM_B2 * v_ref[...] + (1.0 - ADAM_B2) * (gv * gv)
        mo_ref[...] = mn
        vo_ref[...] = vn
        go_ref[...] = gv
        d_ref[...] = -ADAM_LR * ((mn / c1) / (jnp.sqrt(vn / c2) + ADAM_EPS) + ADAM_WD * w_ref[...])

    blk = _spec((br, cols), lambda l, i: (l * nb + i, 0))
    g_specs = [_spec((br, cols), lambda l, i, k=k: (jnp.where(l == k, i, 0), 0)) for k in range(nl)]
    shape = jax.ShapeDtypeStruct((nl * rows, cols), F32)
    return pl.pallas_call(
        body, name=name, grid=(nl, nb), in_specs=[blk] * 3 + g_specs, out_specs=[blk] * 4, out_shape=[shape] * 4,
        compiler_params=_params(("arbitrary", "arbitrary")),
    )(w, m, v, *grads)


def _place():
    x, y, c = lax.axis_index("x"), lax.axis_index("y"), lax.axis_index("c")
    chips = [(1 - x, y), (x, 1 - y), (1 - x, 1 - y)]
    return x, y, c, chips


class _Rider:
    n = 0
    relay = None

    def ride(self, grid, refs, when):
        if not self.n:
            return
        ids = [pl.program_id(a) for a in range(len(grid))]
        first = functools.reduce(jnp.logical_and, [i == 0 for i in ids])
        last = functools.reduce(jnp.logical_and, [i == g - 1 for i, g in zip(ids, grid)])
        phases = [(first, self.start), (last, self.relay)] if when == "start" else [(last, self.finish)]
        for cond, phase in phases:
            if phase is not None:
                pl.when(cond)(functools.partial(phase, *refs))


class _Gather(_Rider):
    PER_GROUP = 7

    def __init__(self, groups):
        self.groups = list(groups)
        self.n = len(self.groups)
        self.in_specs = [ANY] * self.n
        self.out_specs = [ANY] * self.n
        self.out_shape = [jax.ShapeDtypeStruct((NSH,) + g.shape, g.dtype) for g in self.groups]
        sems = pltpu.SemaphoreType.DMA((self.PER_GROUP * self.n,))
        self.scratch = [sems, sems] if self.n else []

    def _copies(self, ins, outs, send, recv):
        x, y, c, chips = _place()
        sibling = (x, y, 1 - c)

        def half(gi, chip, hc):
            return outs[gi].at[2 * chip[0] + chip[1], hc]

        def copy(gi, k, src, dst, to):
            sem = self.PER_GROUP * gi + k
            return pltpu.make_async_remote_copy(src_ref=src, dst_ref=dst, send_sem=send.at[sem], recv_sem=recv.at[sem],
                                                device_id=to, device_id_type=MESH)

        pairs = [(gi, j, chip) for gi in range(self.n) for j, chip in enumerate(chips)]
        first = [copy(gi, j, ins[gi].at[c], half(gi, (x, y), c), (*chip, c)) for gi, j, chip in pairs]
        first += [copy(gi, 6, ins[gi], outs[gi].at[2 * x + y], sibling) for gi in range(self.n)]
        landed = [copy(gi, j, half(gi, chip, c), half(gi, chip, c), sibling) for gi, j, chip in pairs]
        relay = [copy(gi, 3 + j, half(gi, chip, c), half(gi, chip, c), sibling) for gi, j, chip in pairs]
        relayed = [copy(gi, 3 + j, half(gi, chip, 1 - c), half(gi, chip, 1 - c), sibling) for gi, j, chip in pairs]
        relayed += [copy(gi, 6, ins[gi], outs[gi].at[2 * x + y], sibling) for gi in range(self.n)]
        return first, landed, relay, relayed

    def start(self, ins, outs, send, recv):
        for cp in self._copies(ins, outs, send, recv)[0]:
            cp.start()

    def relay(self, ins, outs, send, recv):
        _, landed, relay, _ = self._copies(ins, outs, send, recv)
        for arrived, onward in zip(landed, relay):
            arrived.wait_recv()
            onward.start()

    def finish(self, ins, outs, send, recv):
        first, _, relay, relayed = self._copies(ins, outs, send, recv)
        for cp in relayed:
            cp.wait_recv()
        for cp in first + relay:
            cp.wait_send()

    def place(self, outs):
        return list(outs)


def _alone(name, rider):
    n = rider.n

    def body(*refs):
        parts = (refs[:n], refs[n:2 * n], refs[2 * n], refs[2 * n + 1])
        rider.start(*parts)
        if rider.relay is not None:
            rider.relay(*parts)
        rider.finish(*parts)

    outs = pl.pallas_call(
        body, name=name, in_specs=rider.in_specs, out_specs=rider.out_specs, out_shape=rider.out_shape,
        scratch_shapes=rider.scratch, compiler_params=pltpu.CompilerParams(has_side_effects=True),
    )(*rider.groups)
    return rider.place(outs)


class _Swap(_Rider):
    def __init__(self, slots):
        self.slots = list(slots)
        self.groups = [buf for buf, _, _ in self.slots]
        self.n = len(self.slots)
        self.in_specs = [ANY] * self.n
        self.out_specs = [ANY] * self.n
        self.out_shape = [jax.ShapeDtypeStruct((NSH, rows // 2, buf.shape[2]), buf.dtype) for buf, _, rows in self.slots]
        self.scratch = [pltpu.SemaphoreType.DMA((self.n,)), pltpu.SemaphoreType.DMA((self.n,))]

    def _copies(self, ins, outs, send, recv):
        x, y, c, _ = _place()
        cps = []
        for i, (_, row0, rows) in enumerate(self.slots):
            half = rows // 2
            src = ins[i].at[:, pl.ds(pl.multiple_of(row0 + (1 - c) * half, 16), half)]
            cps.append(pltpu.make_async_remote_copy(src_ref=src, dst_ref=outs[i], send_sem=send.at[i],
                                                    recv_sem=recv.at[i], device_id=(x, y, 1 - c), device_id_type=MESH))
        return cps

    def start(self, ins, outs, send, recv):
        for cp in self._copies(ins, outs, send, recv):
            cp.start()

    def finish(self, ins, outs, send, recv):
        for cp in self._copies(ins, outs, send, recv):
            cp.wait()

    def place(self, outs):
        return list(outs)


class _Share(_Rider):
    def __init__(self, arrays):
        self.groups = list(arrays)
        self.n = len(self.groups)
        self.in_specs = [ANY] * self.n
        self.out_specs = [ANY] * self.n
        self.out_shape = [jax.ShapeDtypeStruct((2,) + g.shape, g.dtype) for g in self.groups]
        self.scratch = [pltpu.SemaphoreType.DMA((self.n,)), pltpu.SemaphoreType.DMA((self.n,))]

    def _copies(self, ins, outs, send, recv, half):
        x, y, c, _ = _place()
        return [pltpu.make_async_remote_copy(src_ref=ins[gi], dst_ref=outs[gi].at[c if half == "mine" else 1 - c],
                                             send_sem=send.at[gi], recv_sem=recv.at[gi], device_id=(x, y, 1 - c),
                                             device_id_type=MESH) for gi in range(self.n)]

    def start(self, ins, outs, send, recv):
        for cp in self._copies(ins, outs, send, recv, "mine"):
            cp.start()

    def finish(self, ins, outs, send, recv):
        for cp in self._copies(ins, outs, send, recv, "theirs"):
            cp.wait_recv()
        for cp in self._copies(ins, outs, send, recv, "mine"):
            cp.wait_send()

    def place(self, outs):
        c = lax.axis_index("c")
        return [lax.dynamic_update_index_in_dim(o, g, c, 0) for o, g in zip(outs, self.groups)]


class _Send(_Rider):
    def __init__(self, arrays):
        self.groups = list(arrays)
        self.n = len(self.groups)
        self.in_specs = [ANY] * self.n
        self.out_specs = [ANY] * self.n
        self.out_shape = [jax.ShapeDtypeStruct((3,) + g.shape[1:], g.dtype) for g in self.groups]
        self.scratch = [pltpu.SemaphoreType.DMA((3 * self.n,)), pltpu.SemaphoreType.DMA((3 * self.n,))]

    def _copies(self, ins, outs, send, recv):
        x, y, c, chips = _place()
        return [pltpu.make_async_remote_copy(src_ref=ins[gi].at[2 * chip[0] + chip[1]], dst_ref=outs[gi].at[j],
                                             send_sem=send.at[3 * gi + j], recv_sem=recv.at[3 * gi + j],
                                             device_id=(*chip, c), device_id_type=MESH)
                for gi in range(self.n) for j, chip in enumerate(chips)]

    def start(self, ins, outs, send, recv):
        for cp in self._copies(ins, outs, send, recv):
            cp.start()

    def finish(self, ins, outs, send, recv):
        for cp in self._copies(ins, outs, send, recv):
            cp.wait()

    def place(self, outs):
        return list(outs)


def _pair_sum(name, slots, got, c_idx):
    n = len(slots)
    in_specs, out_specs, out_shape, operands = [], [], [], []
    for (buf, row0, rows), g in zip(slots, got):
        hb, cols = rows // 4, buf.shape[2]
        in_specs += [pl.BlockSpec((None, hb, cols), lambda q, i, cr, r=row0 // hb: (q, r + 2 * cr[0] + i, 0)),
                     pl.BlockSpec((None, hb, cols), lambda q, i, cr: (q, i, 0))]
        out_specs.append(pl.BlockSpec((None, hb, cols), lambda q, i, cr: (q, i, 0)))
        out_shape.append(jax.ShapeDtypeStruct(g.shape, BF16))
        operands += [buf, g]

    def body(c_ref, *refs):
        for i in range(n):
            refs[2 * n + i][...] = (refs[2 * i][...].astype(F32) + refs[2 * i + 1][...].astype(F32)).astype(BF16)

    return pl.pallas_call(
        body, name=name,
        grid_spec=pltpu.PrefetchScalarGridSpec(num_scalar_prefetch=1, grid=(NSH, 2), in_specs=in_specs, out_specs=out_specs),
        out_shape=out_shape, compiler_params=_params(("parallel", "parallel")),
    )(c_idx, *operands)


def _owner_sum(name, pairs, got, p_idx):
    n = len(pairs)
    in_specs, out_specs, out_shape, operands = [], [], [], []
    for own, g in zip(pairs, got):
        _, rows, cols = own.shape
        hb = rows // 2
        in_specs += [pl.BlockSpec((None, hb, cols), lambda i, pr: (pr[0], i, 0)),
                     pl.BlockSpec((3, hb, cols), lambda i, pr: (0, i, 0))]
        out_specs.append(pl.BlockSpec((hb, cols), lambda i, pr: (i, 0)))
        out_shape.append(jax.ShapeDtypeStruct((rows, cols), F32))
        operands += [own, g]

    def body(p_ref, *refs):
        for i in range(n):
            a_ref, b_ref = refs[2 * i], refs[2 * i + 1]
            refs[2 * n + i][...] = ((a_ref[...].astype(F32) + b_ref[0].astype(F32)) + b_ref[1].astype(F32)) + b_ref[2].astype(F32)

    return pl.pallas_call(
        body, name=name,
        grid_spec=pltpu.PrefetchScalarGridSpec(num_scalar_prefetch=1, grid=(2,), in_specs=in_specs, out_specs=out_specs),
        out_shape=out_shape, compiler_params=_params(("parallel",)),
    )(p_idx, *operands)


def _sum_small(slab):
    def body(in_ref, out_ref, all_ref, send, recv):
        x, y, c, _ = _place()
        me = 4 * x + 2 * y + c
        all_ref[me] = in_ref[...]
        cps = []
        for k in range(1, 8):
            peer = (x ^ (k >> 2), y ^ ((k >> 1) & 1), c ^ (k & 1))
            cps.append(pltpu.make_async_remote_copy(src_ref=in_ref, dst_ref=all_ref.at[me], send_sem=send.at[k - 1],
                                                    recv_sem=recv.at[k - 1], device_id=peer, device_id_type=MESH))
        for cp in cps:
            cp.start()
        for cp in cps:
            cp.wait()
        total = all_ref[0]
        for d in range(1, 8):
            total = total + all_ref[d]
        out_ref[...] = total

    return pl.pallas_call(
        body, name="sum_small",
        in_specs=[pl.BlockSpec(memory_space=pltpu.VMEM)], out_specs=pl.BlockSpec(memory_space=pltpu.VMEM),
        out_shape=jax.ShapeDtypeStruct(slab.shape, F32),
        scratch_shapes=[pltpu.VMEM((8,) + slab.shape, F32), pltpu.SemaphoreType.DMA((7,)), pltpu.SemaphoreType.DMA((7,))],
        compiler_params=pltpu.CompilerParams(has_side_effects=True),
    )(slab)


FFNS = ("pre0", "post0", "pre1", "post1")
W_SHAPES = dict({f + "_gu": (NSH, 2, FS, D) for f in FFNS}, **{f + "_d": (NSH, FS, D) for f in FFNS},
                ab_in=(NSH, D, 768), ab_out=(NSH, 256, D), conv=(NSH, 2, 8, LANE), c_in=(NSH, D, D), c_out=(NSH, 256, D))
CARRIED = dict(pre0_norm=("pre0_gu",), pre0_up=("pre0_d",), pre0_down=("ab_in", "conv"), ab_proj=("ab_out",),
               c_proj=("post1_d",),
               sb_fwd=("post0_gu", "post0_d"), post0_up=("pre1_gu",), post0_down=("pre1_d",),
               pre1_up=("c_in", "c_out"), hgrn_fwd=("post1_gu",))


FFN_SLOTS = dict(pre0=(0, 1, 0), pre1=(2, 3, 1), post0=(4, 5, 2), post1=(6, 7, 3))
GRAD_SLOTS = dict(ab_out=("b", B_ABOUT, 256), c_in=("b", B_CIN, D), c_out=("b", B_COUT, 256), ab_in=("c", 0, D))
for _f, (_g, _u, _d) in FFN_SLOTS.items():
    GRAD_SLOTS.update({_f + "_g": ("a", _g * FS, FS), _f + "_u": ("a", _u * FS, FS), _f + "_d": ("b", _d * FS, FS)})
REDUCE_STAGES = dict(x=("post1_g", "post1_u", "post1_d", "c_out"), w=("c_in", "pre1_d"),
                     y=("pre1_g", "pre1_u", "post0_g", "post0_u", "post0_d", "ab_out"),
                     z1=("ab_in",), z2=("pre0_g", "pre0_u", "pre0_d"))


def _local_step(x, target, weights, small, shards=None, place=None):
    t = x.shape[0]
    th = min(HALF_TILE, t)
    tw = min(WGRAD_TILE, t)
    w = dict(weights)
    reduced = {}

    def stage_slots(stage, buffers):
        return [(buffers[GRAD_SLOTS[n][0]],) + GRAD_SLOTS[n][1:] for n in REDUCE_STAGES[stage]]

    def swap_rider(stage, buffers):
        return _Swap(stage_slots(stage, buffers)) if place is not None else None

    def reduce_start(stage, buffers, swapped=None):
        if place is None:
            return [], None
        slots = stage_slots(stage, buffers)
        if swapped is None:
            swapped = _alone("grad_swap_" + stage, _Swap(slots))
        pairs = _pair_sum("grad_pair_sum_" + stage, slots, swapped, place[0])
        return pairs, _Send(pairs)

    def reduce_end(stage, pairs, landed):
        if place is None:
            return None
        mine = _owner_sum("grad_owner_sum_" + stage, pairs, landed, place[1])
        return _Share(mine)

    def shared(stage, landed):
        if place is not None:
            reduced.update(zip(REDUCE_STAGES[stage], landed))

    def carried(kernel_name):
        names = [n for n in CARRIED[kernel_name] if n not in w]
        return names, (_Gather([shards[n] for n in names]) if names else None)

    def land(names, arrays):
        for n, a in zip(names, arrays):
            w[n] = a.reshape(W_SHAPES[n])

    def ffn_forward(tag, h, hn, next_row):
        names, gather = carried(tag + "_up") if tag + "_up" in CARRIED else ([], None)
        s_up, s_gate, a, got = _ffn_up(tag + "_up", hn, w[tag + "_gu"], gather)
        land(names, got)
        names, gather = carried(tag + "_down") if tag + "_down" in CARRIED else ([], None)
        out = _ffn_down(tag + "_down", a, w[tag + "_d"], h, gather, (small, next_row) if next_row is not None else None)
        if gather is not None:
            out, got = out
            land(names, got)
        out, hn_next = out if next_row is not None else (out, None)
        return out, hn_next, (h, hn, s_up, s_gate, a)

    def out_proj(name, y, w_out, h, next_row):
        row_spec = _spec((th, D), lambda i: (i, 0))
        return _mm(name, [(y, row_spec, w_out.reshape(D, D), _spec((D, D), lambda i: (0, 0)))],
                   grid=(t // th,), o_shape=(t, D), o_dtype=F32, o_spec=row_spec, dims=NN, kaxis=0, nk=1,
                   res=(h, row_spec), norm=(small, next_row))

    def out_proj_bwd(tag, dhb, y, w_out, blk, grad_b, make_rider=None):
        grad_b = _wgrad(tag + "_dwout", y, _spec((tw, 256), lambda s, k: (k, s)), dhb, _spec((tw, D), lambda s, k: (k, 0)),
                        256, D, t, tw, grad_b, blk)
        rider = make_rider(grad_b) if make_rider is not None else None
        row_spec = _spec((th, D), lambda i: (i, 0))
        dy = _mm(tag + "_dy", [(dhb, row_spec, w_out.reshape(D, D), _spec((D, D), lambda i: (0, 0)))],
                 grid=(t // th,), o_shape=(t, D), o_dtype=F32, o_spec=row_spec, dims=NT, kaxis=0, nk=1, carry=rider)
        dy, landed = dy if rider is not None else (dy, None)
        return dy, grad_b, landed

    h0 = x
    names, gather = carried("pre0_norm")
    hn, got = _norm_fwd("pre0_norm", h0, small, R_PRE, gather)
    land(names, got)
    h1, hn_ab, pre0 = ffn_forward("pre0", h0, hn, R_MIX)
    def carrying(kernel_name, make):
        names, gather = carried(kernel_name)
        out = make(gather)
        if gather is not None:
            out, got = out
            land(names, got)
        return out

    proj_a, proj_b = carrying("ab_proj", lambda gather: _in_proj("ab_proj", hn_ab, w["ab_in"], True, gather))
    y_a = _conv_fwd(proj_a, w["conv"])
    names, gather = carried("sb_fwd")
    y_b, ltot, nblk, got = _sb_fwd(proj_b, gather)
    land(names, got)
    y_ab = jnp.concatenate([y_a, y_b], axis=1)
    h2, hn = out_proj("ab_out", y_ab, w["ab_out"], h1, R_POST)
    h3, hn, post0 = ffn_forward("post0", h2, hn, R_PRE + 1)
    h4, hn_c, pre1 = ffn_forward("pre1", h3, hn, R_MIX + 1)
    proj_c = carrying("c_proj", lambda gather: _in_proj("c_proj", hn_c, w["c_in"], False, gather))
    names, gather = carried("hgrn_fwd")
    o_c, y_c, sst, got = _hgrn_fwd(proj_c, small, gather)
    land(names, got)
    h5, hn = out_proj("c_out", y_c, w["c_out"], h4, R_POST + 1)
    h6, _, post1 = ffn_forward("post1", h5, hn, None)
    dh, dhb, d_fin, loss = _final_loss(h6, small, target)

    dh, dhb, grad_a, grad_b, dn_post1, _, _ = _ffn_backward("post1", dh, dhb, *post1, w["post1_gu"], w["post1_d"],
                                                            *FFN_SLOTS["post1"], small, R_POST + 1, 8 * FS, B_ROWS)
    dy_c, grad_b, swapped = out_proj_bwd("c", dhb, y_c, w["c_out"], B_COUT // 256, grad_b,
                                         lambda buf_b: swap_rider("x", dict(a=grad_a, b=buf_b)))
    pairs, rider = reduce_start("x", dict(a=grad_a, b=grad_b), swapped)
    dproj_c, d_clb, d_gam, landed = _hgrn_bwd(proj_c, small, o_c, sst, dy_c, rider)
    share = reduce_end("x", pairs, landed)
    grad_b = _wgrad("c_dwin", hn_c, _spec((tw, D), lambda s, k: (k, 0)), dproj_c, _spec((None, tw, D), lambda s, k: (s, k, 0)),
                    D, D, t, tw, grad_b, B_CIN // D, carry=share)
    if share is not None:
        grad_b, landed = grad_b
        shared("x", landed)
    dh, dhb, dn_mix1, _ = _dhn_norm("c_dhn", [(dproj_c, _spec((None, th, D), lambda i, s=s: (s, i, 0)),
                                               w["c_in"], _spec((None, D, D), lambda i, s=s: (s, 0, 0)))
                                              for s in range(NSH)],
                                    NT, h4, small, R_MIX + 1, dh)
    stage_w = {}

    def w_swap(buf_a, buf_b):
        return swap_rider("w", dict(a=buf_a, b=buf_b))

    def w_send(buf_a, buf_b, swapped):
        stage_w["pairs"], rider = reduce_start("w", dict(a=buf_a, b=buf_b), swapped)
        return rider

    on = place is not None
    dh, dhb, grad_a, grad_b, dn_pre1, _, landed = _ffn_backward(
        "pre1", dh, dhb, *pre1, w["pre1_gu"], w["pre1_d"], *FFN_SLOTS["pre1"], small, R_PRE + 1, grad_a, grad_b,
        w_swap if on else None, w_send if on else None)
    share = reduce_end("w", stage_w.get("pairs"), landed)
    dh, dhb, grad_a, grad_b, dn_post0, landed, _ = _ffn_backward(
        "post0", dh, dhb, *post0, w["post0_gu"], w["post0_d"], *FFN_SLOTS["post0"], small, R_POST, grad_a, grad_b,
        (lambda buf_a, buf_b: share) if on else None)
    shared("w", landed)
    dy_ab, grad_b, _ = out_proj_bwd("ab", dhb, y_ab, w["ab_out"], B_ABOUT // 256, grad_b)
    dab, dac, dax, d_conv, swapped = _conv_bwd(proj_a, w["conv"], dy_ab, swap_rider("y", dict(a=grad_a, b=grad_b)))
    pairs, rider = reduce_start("y", dict(a=grad_a, b=grad_b), swapped)
    dq, dk, dv, landed = _sb_bwd(proj_b, dy_ab, ltot, nblk, rider)
    share = reduce_end("y", pairs, landed)
    dproj_ab = jnp.concatenate([dab, dac, dax, dq, dk, dv], axis=1)
    grad_c = _wgrad("ab_dwin", hn_ab, _spec((tw, D), lambda s, k: (k, 0)), dproj_ab, _spec((tw, 768), lambda s, k: (k, s)),
                    D, 768, t, tw, D, 0, carry=share)
    if share is not None:
        grad_c, landed = grad_c
        shared("y", landed)
    dh, dhb, dn_mix0, _ = _dhn_norm("ab_dhn", [(dproj_ab, _spec((th, 768), lambda i, s=s: (i, s)),
                                                w["ab_in"], _spec((None, D, 768), lambda i, s=s: (s, 0, 0)))
                                               for s in range(NSH)],
                                    NT, h1, small, R_MIX, dh)
    last = {}

    def z1_send(buf_a, buf_b):
        last["z1"], rider = reduce_start("z1", dict(b=buf_b, c=grad_c))
        return rider

    def z2_send(buf_a, buf_b, _):
        last["z2"], rider = reduce_start("z2", dict(a=buf_a, b=buf_b))
        return rider

    dh, dhb, grad_a, grad_b, dn_pre0, landed, landed_own = _ffn_backward(
        "pre0", dh, dhb, *pre0, w["pre0_gu"], w["pre0_d"], *FFN_SLOTS["pre0"], small, R_PRE, grad_a, grad_b,
        z1_send if on else None, z2_send if on else None)
    if place is not None:
        mine = (_owner_sum("grad_owner_sum_z1", last["z1"], landed, place[1])
                + _owner_sum("grad_owner_sum_z2", last["z2"], landed_own, place[1]))
        reduced.update(zip(REDUCE_STAGES["z1"] + REDUCE_STAGES["z2"], _alone("grad_share_z", _Share(mine))))
    zero = jnp.zeros((1, D), F32)
    conv_rows = jnp.pad(jnp.transpose(d_conv[:, :3, :], (1, 0, 2)).reshape(3, 512), ((0, 0), (0, D - 512)))
    small_grad = jnp.concatenate([
        dn_pre0, dn_pre1, dn_mix0, dn_mix1, dn_post0, dn_post1, d_clb, d_fin,
        jnp.pad(d_gam, ((0, 0), (0, D - HD))), conv_rows,
        jnp.pad(loss, ((0, 0), (0, D - 1))), zero, zero], axis=0)
    return dh, grad_a, grad_b, grad_c, small_grad, reduced


def _small_slab(rows):
    parts = [jnp.pad(r.astype(F32), ((0, 0), (0, D - r.shape[1]))) for r in rows]
    slab = jnp.concatenate(parts, axis=0)
    return jnp.pad(slab, ((0, SMALL_ROWS - slab.shape[0]), (0, 0)))


def kernel(x, ffn_pre_norm, ffn_pre_w_gate, ffn_pre_w_up, ffn_pre_w_down, mix_norm, ffn_post_norm, ffn_post_w_gate, ffn_post_w_up, ffn_post_w_down, ab_w_in, ab_conv_w, ab_w_out, c_w_in, c_lower_bounds, c_out_norm, c_w_out, final_norm, loss_target, m_ffn_pre_norm, m_ffn_pre_w_gate, m_ffn_pre_w_up, m_ffn_pre_w_down, m_mix_norm, m_ffn_post_norm, m_ffn_post_w_gate, m_ffn_post_w_up, m_ffn_post_w_down, m_ab_w_in, m_ab_conv_w, m_ab_w_out, m_c_w_in, m_c_lower_bounds, m_c_out_norm, m_c_w_out, m_final_norm, v_ffn_pre_norm, v_ffn_pre_w_gate, v_ffn_pre_w_up, v_ffn_pre_w_down, v_mix_norm, v_ffn_post_norm, v_ffn_post_w_gate, v_ffn_post_w_up, v_ffn_post_w_down, v_ab_w_in, v_ab_conv_w, v_ab_w_out, v_c_w_in, v_c_lower_bounds, v_c_out_norm, v_c_w_out, v_final_norm):
    t = x.shape[1]
    xi, yi, ci = lax.axis_index("x"), lax.axis_index("y"), lax.axis_index("c")
    p_idx = (2 * xi + yi).astype(jnp.int32).reshape(1)
    c_idx = ci.astype(jnp.int32).reshape(1)

    def halves(m):
        return m.astype(BF16).reshape(2, m.shape[0] // 2, m.shape[1])

    transposed = ("ffn_pre_w_gate", "ffn_pre_w_up", "ffn_post_w_gate", "ffn_post_w_up")

    def flip(a):
        return jnp.swapaxes(a, 1, 2)

    shards = {}
    for name, (w_gate, w_up, w_down, layer) in dict(
            pre0=(ffn_pre_w_gate, ffn_pre_w_up, ffn_pre_w_down, 0), post0=(ffn_post_w_gate, ffn_post_w_up, ffn_post_w_down, 0),
            pre1=(ffn_pre_w_gate, ffn_pre_w_up, ffn_pre_w_down, 1), post1=(ffn_post_w_gate, ffn_post_w_up, ffn_post_w_down, 1)).items():
        shards[name + "_gu"] = jnp.stack([flip(w_gate)[layer], flip(w_up)[layer]]).astype(BF16)
        shards[name + "_d"] = halves(w_down[layer])
    conv_pad = jnp.pad(ab_conv_w[0], ((0, 5), (0, 0)))
    shards.update(ab_in=halves(ab_w_in[0]), ab_out=halves(ab_w_out[0]), c_in=halves(c_w_in[0]), c_out=halves(c_w_out[0]),
                  conv=jnp.stack([conv_pad, jnp.zeros_like(conv_pad)]))
    small = _small_slab([ffn_pre_norm, mix_norm, ffn_post_norm, c_lower_bounds, final_norm.reshape(1, D), c_out_norm])
    small = small.reshape(SMALL_ROWS, 1, D)

    grad_x, _, _, _, small_grad, reduced = _local_step(x[0], loss_target[0], {}, small, shards,
                                                        (c_idx, p_idx))
    whole = {n: g.reshape(2 * g.shape[1], g.shape[2]) for n, g in reduced.items()}
    small_sum = _sum_small(small_grad)

    my_conv = lax.dynamic_slice(small_sum[R_CONV:R_CONV + 3], (0, (2 * xi + yi) * 128), (3, 128))
    grads = {
        "ffn_pre_norm": small_sum[R_PRE:R_PRE + 2], "mix_norm": small_sum[R_MIX:R_MIX + 2],
        "ffn_post_norm": small_sum[R_POST:R_POST + 2], "c_lower_bounds": small_sum[R_CLB:R_CLB + 2],
        "c_out_norm": small_sum[R_GAM:R_GAM + 1, :HD], "final_norm": small_sum[R_FIN],
        "ab_conv_w": my_conv.reshape(1, 3, 128),
    }
    layers = dict(ab_w_in=[whole["ab_in"]], ab_w_out=[whole["ab_out"]], c_w_in=[whole["c_in"]], c_w_out=[whole["c_out"]])
    for kind, key in (("gate", "_g"), ("up", "_u"), ("down", "_d")):
        layers["ffn_pre_w_" + kind] = [whole["pre0" + key], whole["pre1" + key]]
        layers["ffn_post_w_" + kind] = [whole["post0" + key], whole["post1" + key]]
    weights = dict(ffn_pre_norm=ffn_pre_norm, ffn_pre_w_gate=ffn_pre_w_gate, ffn_pre_w_up=ffn_pre_w_up, ffn_pre_w_down=ffn_pre_w_down, mix_norm=mix_norm, ffn_post_norm=ffn_post_norm, ffn_post_w_gate=ffn_post_w_gate, ffn_post_w_up=ffn_post_w_up, ffn_post_w_down=ffn_post_w_down, ab_w_in=ab_w_in, ab_conv_w=ab_conv_w, ab_w_out=ab_w_out, c_w_in=c_w_in, c_lower_bounds=c_lower_bounds, c_out_norm=c_out_norm, c_w_out=c_w_out, final_norm=final_norm)
    m_in = dict(ffn_pre_norm=m_ffn_pre_norm, ffn_pre_w_gate=m_ffn_pre_w_gate, ffn_pre_w_up=m_ffn_pre_w_up, ffn_pre_w_down=m_ffn_pre_w_down, mix_norm=m_mix_norm, ffn_post_norm=m_ffn_post_norm, ffn_post_w_gate=m_ffn_post_w_gate, ffn_post_w_up=m_ffn_post_w_up, ffn_post_w_down=m_ffn_post_w_down, ab_w_in=m_ab_w_in, ab_conv_w=m_ab_conv_w, ab_w_out=m_ab_w_out, c_w_in=m_c_w_in, c_lower_bounds=m_c_lower_bounds, c_out_norm=m_c_out_norm, c_w_out=m_c_w_out, final_norm=m_final_norm)
    v_in = dict(ffn_pre_norm=v_ffn_pre_norm, ffn_pre_w_gate=v_ffn_pre_w_gate, ffn_pre_w_up=v_ffn_pre_w_up, ffn_pre_w_down=v_ffn_pre_w_down, mix_norm=v_mix_norm, ffn_post_norm=v_ffn_post_norm, ffn_post_w_gate=v_ffn_post_w_gate, ffn_post_w_up=v_ffn_post_w_up, ffn_post_w_down=v_ffn_post_w_down, ab_w_in=v_ab_w_in, ab_conv_w=v_ab_conv_w, ab_w_out=v_ab_w_out, c_w_in=v_c_w_in, c_lower_bounds=v_c_lower_bounds, c_out_norm=v_c_out_norm, c_w_out=v_c_w_out, final_norm=v_final_norm)
    names = list(weights)
    big = [n for n in names if weights[n].size >= 65536]
    tiny = [n for n in names if n not in big]

    delta, new_m, new_v = {}, {}, {}
    for n in big:
        turn = flip if n in transposed else (lambda a: a)
        shape = turn(weights[n]).shape
        two_d = (shape[0] * shape[1], shape[2])
        d, m2, v2, g2 = _adamw("adamw_" + n, turn(weights[n]).reshape(two_d), layers[n],
                               turn(m_in[n]).reshape(two_d), turn(v_in[n]).reshape(two_d))
        delta[n], new_m[n], new_v[n] = turn(d.reshape(shape)), turn(m2.reshape(shape)), turn(v2.reshape(shape))
        grads[n] = turn(g2.reshape(shape))

    def tiny_slab(src):
        return _small_slab([src[n].reshape(-1, src[n].shape[-1]) for n in tiny])

    offs, row = {}, 0
    for n in tiny:
        nrows = weights[n].size // weights[n].shape[-1]
        offs[n] = (row, nrows)
        row += nrows
    d, m2, v2, _ = _adamw("adamw_small", tiny_slab(weights), [tiny_slab(grads)], tiny_slab(m_in), tiny_slab(v_in))
    for n in tiny:
        r0, nr = offs[n]
        shape = weights[n].shape
        for dst, src in ((delta, d), (new_m, m2), (new_v, v2)):
            dst[n] = src[r0:r0 + nr, :shape[-1]].reshape(shape)

    loss = small_sum[R_LOSS, 0]
    return (loss, grad_x.reshape(1, t, D), *[grads[n] for n in names], *[delta[n] for n in names],
            *[new_m[n] for n in names], *[new_v[n] for n in names])
```
